```python
import jax, jax.numpy as jnp
from jax import lax
import numpy as np

D_MODEL = 1024
BATCH = 8
SEQ = 2048
DEPTH = 2

CHUNK = 64
Q_BLOCK = 128
N_A = DEPTH // 2
N_B = DEPTH - N_A
CONV_W = 3
N_HEADS = 8
QK_NOPE = 128
QK_ROPE = 64
V_HEAD = 128
Q_LORA = 384
KV_LORA = 256
D_FF = 2816
ROPE_THETA = 10000.0
EPS = 1e-6
NEG_INF = -1e30
MAX_POS_OFFSET = 8192

kernel_name = 'yoco_shortconv_mla_convffn'


def rms_norm(x, g):
    xf = x.astype(jnp.float32)
    y = xf * lax.rsqrt(jnp.mean(xf * xf, axis=-1, keepdims=True) + EPS)
    return (y * g.astype(jnp.float32)).astype(x.dtype)


def causal_dwconv(x, w):
    s = x.shape[1]
    xp = jnp.pad(x, ((0, 0), (CONV_W - 1, 0), (0, 0)))
    y = xp[:, 0:s, :] * w[0]
    for j in range(1, CONV_W):
        y = y + xp[:, j:j + s, :] * w[j]
    return y


def rope(x, positions):
    half = QK_ROPE // 2
    inv_freq = 1.0 / (ROPE_THETA ** (jnp.arange(half, dtype=jnp.float32) / half))
    ang = positions.astype(jnp.float32)[..., None] * inv_freq
    cos, sin = jnp.cos(ang), jnp.sin(ang)
    if x.ndim == 4:
        cos, sin = cos[:, :, None, :], sin[:, :, None, :]
    x1 = x[..., :half].astype(jnp.float32)
    x2 = x[..., half:].astype(jnp.float32)
    out = jnp.concatenate([x1 * cos - x2 * sin, x2 * cos + x1 * sin], axis=-1)
    return out.astype(x.dtype)


def short_conv_mixer(h, w_in, conv_w, w_out):
    b_gate, c_gate, u = jnp.split(h @ w_in, 3, axis=-1)
    return (b_gate * causal_dwconv(c_gate * u, conv_w)) @ w_out


def conv_ffn(h, w_up, conv_w, conv_b, w_down):
    g, v = jnp.split(h @ w_up, 2, axis=-1)
    g = causal_dwconv(g, conv_w) + conv_b
    return (jax.nn.silu(g) * v) @ w_down


def shared_kv(h, kv_in_norm, w_dkv, kv_latent_norm, w_kr, w_uk, w_uv, positions):
    b, s, _ = h.shape
    hn = rms_norm(h, kv_in_norm)
    c_kv = rms_norm(hn @ w_dkv, kv_latent_norm)
    k_rope = rope(hn @ w_kr, positions)
    k_nope = (c_kv @ w_uk).reshape(b, s, N_HEADS, QK_NOPE)
    v = (c_kv @ w_uv).reshape(b, s, N_HEADS, V_HEAD)
    return k_nope, k_rope, v


def mla_attention(h, w_dq, q_latent_norm, w_uq, w_o, k_nope, k_rope, v, positions):
    b, s, _ = h.shape
    c_q = rms_norm(h @ w_dq, q_latent_norm)
    q = (c_q @ w_uq).reshape(b, s, N_HEADS, QK_NOPE + QK_ROPE)
    q_nope = q[..., :QK_NOPE]
    q_rope = rope(q[..., QK_NOPE:], positions)
    scale = (QK_NOPE + QK_ROPE) ** -0.5
    nb = s // Q_BLOCK
    qn_blocks = q_nope.reshape(b, nb, Q_BLOCK, N_HEADS, QK_NOPE).transpose(1, 0, 2, 3, 4)
    qr_blocks = q_rope.reshape(b, nb, Q_BLOCK, N_HEADS, QK_ROPE).transpose(1, 0, 2, 3, 4)
    key_chunk = jnp.arange(s) // CHUNK

    def attend_block(args):
        qn, qr, blk = args
        sc = (jnp.einsum('bqhd,bkhd->bhqk', qn, k_nope)
              + jnp.einsum('bqhr,bkr->bhqk', qr, k_rope)).astype(jnp.float32) * scale
        q_chunk = (blk * Q_BLOCK + jnp.arange(Q_BLOCK)) // CHUNK
        mask = key_chunk[None, :] <= q_chunk[:, None]
        sc = jnp.where(mask[None, None], sc, NEG_INF)
        p = jax.nn.softmax(sc, axis=-1).astype(v.dtype)
        return jnp.einsum('bhqk,bkhd->bqhd', p, v)

    o = lax.map(attend_block, (qn_blocks, qr_blocks, jnp.arange(nb)))
    o = o.transpose(1, 0, 2, 3, 4).reshape(b, s, N_HEADS * V_HEAD)
    return o @ w_o


def _fwd_setup_inputs(seed: int = 0) -> dict:
    key = jax.random.key(seed)
    ks = jax.random.split(key, 32)
    f32 = jnp.float32
    resid = (2 * DEPTH) ** -0.5

    def w(k, shape, fan_in, extra=1.0):
        return jax.random.normal(k, shape, f32) * (fan_in ** -0.5) * extra

    def gain(k, shape):
        return 1.0 + 0.02 * jax.random.normal(k, shape, f32)

    x = jax.random.normal(ks[0], (BATCH, SEQ, D_MODEL), f32)
    offsets = jax.random.randint(ks[1], (BATCH, 1), 0, MAX_POS_OFFSET, dtype=jnp.int32)
    positions = offsets + jnp.arange(SEQ, dtype=jnp.int32)[None, :]
    return {
        'x': x,
        'positions': positions,
        'attn_norm': gain(ks[2], (DEPTH, D_MODEL)),
        'ffn_norm': gain(ks[3], (DEPTH, D_MODEL)),
        'final_norm': gain(ks[4], (D_MODEL,)),
        'sc_w_in': w(ks[5], (N_A, D_MODEL, 3 * D_MODEL), D_MODEL),
        'sc_conv_w': w(ks[6], (N_A, CONV_W, D_MODEL), CONV_W),
        'sc_w_out': w(ks[7], (N_A, D_MODEL, D_MODEL), D_MODEL, resid),
        'kv_in_norm': gain(ks[8], (D_MODEL,)),
        'w_dkv': w(ks[9], (D_MODEL, KV_LORA), D_MODEL),
        'kv_latent_norm': gain(ks[10], (KV_LORA,)),
        'w_kr': w(ks[11], (D_MODEL, QK_ROPE), D_MODEL),
        'w_uk': w(ks[12], (KV_LORA, N_HEADS * QK_NOPE), KV_LORA),
        'w_uv': w(ks[13], (KV_LORA, N_HEADS * V_HEAD), KV_LORA),
        'w_dq': w(ks[14], (N_B, D_MODEL, Q_LORA), D_MODEL),
        'q_latent_norm': gain(ks[15], (N_B, Q_LORA)),
        'w_uq': w(ks[16], (N_B, Q_LORA, N_HEADS * (QK_NOPE + QK_ROPE)), Q_LORA),
        'w_o': w(ks[17], (N_B, N_HEADS * V_HEAD, D_MODEL), N_HEADS * V_HEAD, resid),
        'ffn_w_up': w(ks[18], (DEPTH, D_MODEL, 2 * D_FF), D_MODEL),
        'ffn_conv_w': w(ks[19], (DEPTH, CONV_W, D_FF), CONV_W),
        'ffn_conv_b': 0.02 * jax.random.normal(ks[20], (DEPTH, D_FF), f32),
        'ffn_w_down': w(ks[21], (DEPTH, D_FF, D_MODEL), D_FF, resid),
    }


def _fwd_reference(x, positions, attn_norm, ffn_norm, final_norm, sc_w_in, sc_conv_w, sc_w_out,
              kv_in_norm, w_dkv, kv_latent_norm, w_kr, w_uk, w_uv,
              w_dq, q_latent_norm, w_uq, w_o,
              ffn_w_up, ffn_conv_w, ffn_conv_b, ffn_w_down):
    h = x
    kv = None
    for layer in range(DEPTH):
        hn = rms_norm(h, attn_norm[layer])
        if layer < N_A:
            h = h + short_conv_mixer(hn, sc_w_in[layer], sc_conv_w[layer], sc_w_out[layer])
        else:
            i = layer - N_A
            k_nope, k_rope, v = kv
            h = h + mla_attention(hn, w_dq[i], q_latent_norm[i], w_uq[i], w_o[i],
                                  k_nope, k_rope, v, positions)
        h = h + conv_ffn(rms_norm(h, ffn_norm[layer]), ffn_w_up[layer], ffn_conv_w[layer],
                         ffn_conv_b[layer], ffn_w_down[layer])
        if layer == N_A - 1:
            kv = shared_kv(h, kv_in_norm, w_dkv, kv_latent_norm, w_kr, w_uk, w_uv, positions)
    return rms_norm(h, final_norm)


import jax as _jax
import jax.numpy as _jnp

TWIN_FORMAT = 'train_step'
FWD_PARAMS = ['x', 'positions', 'attn_norm', 'ffn_norm', 'final_norm', 'sc_w_in', 'sc_conv_w', 'sc_w_out', 'kv_in_norm', 'w_dkv', 'kv_latent_norm', 'w_kr', 'w_uk', 'w_uv', 'w_dq', 'q_latent_norm', 'w_uq', 'w_o', 'ffn_w_up', 'ffn_conv_w', 'ffn_conv_b', 'ffn_w_down']
TWIN_WEIGHTS = ['attn_norm', 'ffn_norm', 'final_norm', 'sc_w_in', 'sc_conv_w', 'sc_w_out', 'kv_in_norm', 'w_dkv', 'kv_latent_norm', 'w_kr', 'w_uk', 'w_uv', 'w_dq', 'q_latent_norm', 'w_uq', 'w_o', 'ffn_w_up', 'ffn_conv_w', 'ffn_conv_b', 'ffn_w_down']
TWIN_DIFF_INPUT = 'x'
TWIN_INPUTS = ['x', 'positions', 'attn_norm', 'ffn_norm', 'final_norm', 'sc_w_in', 'sc_conv_w', 'sc_w_out', 'kv_in_norm', 'w_dkv', 'kv_latent_norm', 'w_kr', 'w_uk', 'w_uv', 'w_dq', 'q_latent_norm', 'w_uq', 'w_o', 'ffn_w_up', 'ffn_conv_w', 'ffn_conv_b', 'ffn_w_down', 'loss_target', 'm_attn_norm', 'm_ffn_norm', 'm_final_norm', 'm_sc_w_in', 'm_sc_conv_w', 'm_sc_w_out', 'm_kv_in_norm', 'm_w_dkv', 'm_kv_latent_norm', 'm_w_kr', 'm_w_uk', 'm_w_uv', 'm_w_dq', 'm_q_latent_norm', 'm_w_uq', 'm_w_o', 'm_ffn_w_up', 'm_ffn_conv_w', 'm_ffn_conv_b', 'm_ffn_w_down', 'v_attn_norm', 'v_ffn_norm', 'v_final_norm', 'v_sc_w_in', 'v_sc_conv_w', 'v_sc_w_out', 'v_kv_in_norm', 'v_w_dkv', 'v_kv_latent_norm', 'v_w_kr', 'v_w_uk', 'v_w_uv', 'v_w_dq', 'v_q_latent_norm', 'v_w_uq', 'v_w_o', 'v_ffn_w_up', 'v_ffn_conv_w', 'v_ffn_conv_b', 'v_ffn_w_down']
TWIN_OUTPUTS = ['loss', 'grad_x', 'grad_attn_norm', 'grad_ffn_norm', 'grad_final_norm', 'grad_sc_w_in', 'grad_sc_conv_w', 'grad_sc_w_out', 'grad_kv_in_norm', 'grad_w_dkv', 'grad_kv_latent_norm', 'grad_w_kr', 'grad_w_uk', 'grad_w_uv', 'grad_w_dq', 'grad_q_latent_norm', 'grad_w_uq', 'grad_w_o', 'grad_ffn_w_up', 'grad_ffn_conv_w', 'grad_ffn_conv_b', 'grad_ffn_w_down', 'delta_attn_norm', 'delta_ffn_norm', 'delta_final_norm', 'delta_sc_w_in', 'delta_sc_conv_w', 'delta_sc_w_out', 'delta_kv_in_norm', 'delta_w_dkv', 'delta_kv_latent_norm', 'delta_w_kr', 'delta_w_uk', 'delta_w_uv', 'delta_w_dq', 'delta_q_latent_norm', 'delta_w_uq', 'delta_w_o', 'delta_ffn_w_up', 'delta_ffn_conv_w', 'delta_ffn_conv_b', 'delta_ffn_w_down', 'new_m_attn_norm', 'new_m_ffn_norm', 'new_m_final_norm', 'new_m_sc_w_in', 'new_m_sc_conv_w', 'new_m_sc_w_out', 'new_m_kv_in_norm', 'new_m_w_dkv', 'new_m_kv_latent_norm', 'new_m_w_kr', 'new_m_w_uk', 'new_m_w_uv', 'new_m_w_dq', 'new_m_q_latent_norm', 'new_m_w_uq', 'new_m_w_o', 'new_m_ffn_w_up', 'new_m_ffn_conv_w', 'new_m_ffn_conv_b', 'new_m_ffn_w_down', 'new_v_attn_norm', 'new_v_ffn_norm', 'new_v_final_norm', 'new_v_sc_w_in', 'new_v_sc_conv_w', 'new_v_sc_w_out', 'new_v_kv_in_norm', 'new_v_w_dkv', 'new_v_kv_latent_norm', 'new_v_w_kr', 'new_v_w_uk', 'new_v_w_uv', 'new_v_w_dq', 'new_v_q_latent_norm', 'new_v_w_uq', 'new_v_w_o', 'new_v_ffn_w_up', 'new_v_ffn_conv_w', 'new_v_ffn_conv_b', 'new_v_ffn_w_down']
TWIN_LEAF_KINDS = {'loss': 'loss', 'grad_x': 'grad_x', 'grad_attn_norm': 'grad_w', 'grad_ffn_norm': 'grad_w', 'grad_final_norm': 'grad_w', 'grad_sc_w_in': 'grad_w', 'grad_sc_conv_w': 'grad_w', 'grad_sc_w_out': 'grad_w', 'grad_kv_in_norm': 'grad_w', 'grad_w_dkv': 'grad_w', 'grad_kv_latent_norm': 'grad_w', 'grad_w_kr': 'grad_w', 'grad_w_uk': 'grad_w', 'grad_w_uv': 'grad_w', 'grad_w_dq': 'grad_w', 'grad_q_latent_norm': 'grad_w', 'grad_w_uq': 'grad_w', 'grad_w_o': 'grad_w', 'grad_ffn_w_up': 'grad_w', 'grad_ffn_conv_w': 'grad_w', 'grad_ffn_conv_b': 'grad_w', 'grad_ffn_w_down': 'grad_w', 'delta_attn_norm': 'delta_w', 'delta_ffn_norm': 'delta_w', 'delta_final_norm': 'delta_w', 'delta_sc_w_in': 'delta_w', 'delta_sc_conv_w': 'delta_w', 'delta_sc_w_out': 'delta_w', 'delta_kv_in_norm': 'delta_w', 'delta_w_dkv': 'delta_w', 'delta_kv_latent_norm': 'delta_w', 'delta_w_kr': 'delta_w', 'delta_w_uk': 'delta_w', 'delta_w_uv': 'delta_w', 'delta_w_dq': 'delta_w', 'delta_q_latent_norm': 'delta_w', 'delta_w_uq': 'delta_w', 'delta_w_o': 'delta_w', 'delta_ffn_w_up': 'delta_w', 'delta_ffn_conv_w': 'delta_w', 'delta_ffn_conv_b': 'delta_w', 'delta_ffn_w_down': 'delta_w', 'new_m_attn_norm': 'new_m', 'new_m_ffn_norm': 'new_m', 'new_m_final_norm': 'new_m', 'new_m_sc_w_in': 'new_m', 'new_m_sc_conv_w': 'new_m', 'new_m_sc_w_out': 'new_m', 'new_m_kv_in_norm': 'new_m', 'new_m_w_dkv': 'new_m', 'new_m_kv_latent_norm': 'new_m', 'new_m_w_kr': 'new_m', 'new_m_w_uk': 'new_m', 'new_m_w_uv': 'new_m', 'new_m_w_dq': 'new_m', 'new_m_q_latent_norm': 'new_m', 'new_m_w_uq': 'new_m', 'new_m_w_o': 'new_m', 'new_m_ffn_w_up': 'new_m', 'new_m_ffn_conv_w': 'new_m', 'new_m_ffn_conv_b': 'new_m', 'new_m_ffn_w_down': 'new_m', 'new_v_attn_norm': 'new_v', 'new_v_ffn_norm': 'new_v', 'new_v_final_norm': 'new_v', 'new_v_sc_w_in': 'new_v', 'new_v_sc_conv_w': 'new_v', 'new_v_sc_w_out': 'new_v', 'new_v_kv_in_norm': 'new_v', 'new_v_w_dkv': 'new_v', 'new_v_kv_latent_norm': 'new_v', 'new_v_w_kr': 'new_v', 'new_v_w_uk': 'new_v', 'new_v_w_uv': 'new_v', 'new_v_w_dq': 'new_v', 'new_v_q_latent_norm': 'new_v', 'new_v_w_uq': 'new_v', 'new_v_w_o': 'new_v', 'new_v_ffn_w_up': 'new_v', 'new_v_ffn_conv_w': 'new_v', 'new_v_ffn_conv_b': 'new_v', 'new_v_ffn_w_down': 'new_v'}


def _forward(args):
    return _fwd_reference(*[args[k] for k in FWD_PARAMS])


def _output_shape():
    out = _jax.eval_shape(lambda: _forward(_fwd_setup_inputs(0)))
    return out.shape, out.dtype

N_MICROBATCH = 1
ADAM_LR = 0.001
ADAM_B1 = 0.9
ADAM_B2 = 0.999
ADAM_EPS = 1e-08
ADAM_WD = 0.01
ADAM_STEP = 10
PER_EXAMPLE_BATCH_AXIS = {'x': 0, 'positions': 0, 'loss_target': 0}
SHARED_INPUTS = []
_WEIGHT_DTYPES = {'attn_norm': _jnp.float32, 'ffn_norm': _jnp.float32, 'final_norm': _jnp.float32, 'sc_w_in': _jnp.float32, 'sc_conv_w': _jnp.float32, 'sc_w_out': _jnp.float32, 'kv_in_norm': _jnp.float32, 'w_dkv': _jnp.float32, 'kv_latent_norm': _jnp.float32, 'w_kr': _jnp.float32, 'w_uk': _jnp.float32, 'w_uv': _jnp.float32, 'w_dq': _jnp.float32, 'q_latent_norm': _jnp.float32, 'w_uq': _jnp.float32, 'w_o': _jnp.float32, 'ffn_w_up': _jnp.float32, 'ffn_conv_w': _jnp.float32, 'ffn_conv_b': _jnp.float32, 'ffn_w_down': _jnp.float32}
MOMENT_SCALE = {'attn_norm': 7.461600e-02, 'ffn_norm': 4.770554e-02, 'final_norm': 1.598464e+01, 'sc_w_in': 5.966827e-02, 'sc_conv_w': 6.065669e-02, 'sc_w_out': 1.192020e-01, 'kv_in_norm': 1.147131e-02, 'w_dkv': 2.106404e-02, 'kv_latent_norm': 2.388956e-02, 'w_kr': 1.813419e-02, 'w_uk': 7.001924e-03, 'w_uv': 8.545273e-03, 'w_dq': 1.362261e-02, 'q_latent_norm': 1.286016e-02, 'w_uq': 6.818112e-03, 'w_o': 1.723829e-02, 'ffn_w_up': 2.011391e-02, 'ffn_conv_w': 2.070358e-02, 'ffn_conv_b': 1.984990e-02, 'ffn_w_down': 6.598664e-02}


def _to_microbatches(a, axis):
    t = _jnp.moveaxis(a, axis, 0)
    t = t.reshape((N_MICROBATCH, t.shape[0] // N_MICROBATCH) + t.shape[1:])
    return _jnp.moveaxis(t, 1, axis + 1)


def setup_inputs(seed: int = 0) -> dict:
    inp = _fwd_setup_inputs(seed)
    key = _jax.random.fold_in(_jax.random.key(seed), 7919)
    shape, _ = _output_shape()
    out = dict(inp)
    out["loss_target"] = _jax.random.normal(_jax.random.fold_in(key, 0), shape, _jnp.float32)
    for i, name in enumerate(TWIN_WEIGHTS):
        w = inp[name].astype(_jnp.float32)
        if MOMENT_SCALE is None:
            s = _jnp.sqrt(_jnp.mean(_jnp.square(w)) + 1e-30)
        else:
            s = MOMENT_SCALE[name]
        km, kv = _jax.random.split(_jax.random.fold_in(key, i + 1))
        out[name] = w
        out["m_" + name] = s * _jax.random.normal(km, w.shape, _jnp.float32)
        out["v_" + name] = (s * s) * _jax.random.uniform(kv, w.shape, _jnp.float32, 0.5, 1.5)
    if N_MICROBATCH > 1:
        for name, axis in PER_EXAMPLE_BATCH_AXIS.items():
            out[name] = _to_microbatches(out[name], axis)
    return {'x': out['x'], 'positions': out['positions'], 'attn_norm': out['attn_norm'], 'ffn_norm': out['ffn_norm'], 'final_norm': out['final_norm'], 'sc_w_in': out['sc_w_in'], 'sc_conv_w': out['sc_conv_w'], 'sc_w_out': out['sc_w_out'], 'kv_in_norm': out['kv_in_norm'], 'w_dkv': out['w_dkv'], 'kv_latent_norm': out['kv_latent_norm'], 'w_kr': out['w_kr'], 'w_uk': out['w_uk'], 'w_uv': out['w_uv'], 'w_dq': out['w_dq'], 'q_latent_norm': out['q_latent_norm'], 'w_uq': out['w_uq'], 'w_o': out['w_o'], 'ffn_w_up': out['ffn_w_up'], 'ffn_conv_w': out['ffn_conv_w'], 'ffn_conv_b': out['ffn_conv_b'], 'ffn_w_down': out['ffn_w_down'], 'loss_target': out['loss_target'], 'm_attn_norm': out['m_attn_norm'], 'm_ffn_norm': out['m_ffn_norm'], 'm_final_norm': out['m_final_norm'], 'm_sc_w_in': out['m_sc_w_in'], 'm_sc_conv_w': out['m_sc_conv_w'], 'm_sc_w_out': out['m_sc_w_out'], 'm_kv_in_norm': out['m_kv_in_norm'], 'm_w_dkv': out['m_w_dkv'], 'm_kv_latent_norm': out['m_kv_latent_norm'], 'm_w_kr': out['m_w_kr'], 'm_w_uk': out['m_w_uk'], 'm_w_uv': out['m_w_uv'], 'm_w_dq': out['m_w_dq'], 'm_q_latent_norm': out['m_q_latent_norm'], 'm_w_uq': out['m_w_uq'], 'm_w_o': out['m_w_o'], 'm_ffn_w_up': out['m_ffn_w_up'], 'm_ffn_conv_w': out['m_ffn_conv_w'], 'm_ffn_conv_b': out['m_ffn_conv_b'], 'm_ffn_w_down': out['m_ffn_w_down'], 'v_attn_norm': out['v_attn_norm'], 'v_ffn_norm': out['v_ffn_norm'], 'v_final_norm': out['v_final_norm'], 'v_sc_w_in': out['v_sc_w_in'], 'v_sc_conv_w': out['v_sc_conv_w'], 'v_sc_w_out': out['v_sc_w_out'], 'v_kv_in_norm': out['v_kv_in_norm'], 'v_w_dkv': out['v_w_dkv'], 'v_kv_latent_norm': out['v_kv_latent_norm'], 'v_w_kr': out['v_w_kr'], 'v_w_uk': out['v_w_uk'], 'v_w_uv': out['v_w_uv'], 'v_w_dq': out['v_w_dq'], 'v_q_latent_norm': out['v_q_latent_norm'], 'v_w_uq': out['v_w_uq'], 'v_w_o': out['v_w_o'], 'v_ffn_w_up': out['v_ffn_w_up'], 'v_ffn_conv_w': out['v_ffn_conv_w'], 'v_ffn_conv_b': out['v_ffn_conv_b'], 'v_ffn_w_down': out['v_ffn_w_down']}


def _loss(weights, diff, rest, loss_target):
    with _jax.named_scope("forward"):
        args = {**rest, TWIN_DIFF_INPUT: diff, **{k: w.astype(_WEIGHT_DTYPES[k]) for k, w in weights.items()}}
        y = _forward(args)
    with _jax.named_scope("loss_head"):
        err = _jnp.square(y.astype(_jnp.float32) - loss_target)
        return 0.5 * _jnp.sum(_jnp.mean(err, axis=-1)) if err.ndim else 0.5 * err


def _adamw(w, g, m, v):
    m = ADAM_B1 * m + (1.0 - ADAM_B1) * g
    v = ADAM_B2 * v + (1.0 - ADAM_B2) * _jnp.square(g)
    m_hat = m / (1.0 - ADAM_B1 ** ADAM_STEP)
    v_hat = v / (1.0 - ADAM_B2 ** ADAM_STEP)
    delta = -ADAM_LR * (m_hat / (_jnp.sqrt(v_hat) + ADAM_EPS) + ADAM_WD * w)
    return delta, m, v


def reference(x, positions, attn_norm, ffn_norm, final_norm, sc_w_in, sc_conv_w, sc_w_out, kv_in_norm, w_dkv, kv_latent_norm, w_kr, w_uk, w_uv, w_dq, q_latent_norm, w_uq, w_o, ffn_w_up, ffn_conv_w, ffn_conv_b, ffn_w_down, loss_target, m_attn_norm, m_ffn_norm, m_final_norm, m_sc_w_in, m_sc_conv_w, m_sc_w_out, m_kv_in_norm, m_w_dkv, m_kv_latent_norm, m_w_kr, m_w_uk, m_w_uv, m_w_dq, m_q_latent_norm, m_w_uq, m_w_o, m_ffn_w_up, m_ffn_conv_w, m_ffn_conv_b, m_ffn_w_down, v_attn_norm, v_ffn_norm, v_final_norm, v_sc_w_in, v_sc_conv_w, v_sc_w_out, v_kv_in_norm, v_w_dkv, v_kv_latent_norm, v_w_kr, v_w_uk, v_w_uv, v_w_dq, v_q_latent_norm, v_w_uq, v_w_o, v_ffn_w_up, v_ffn_conv_w, v_ffn_conv_b, v_ffn_w_down):
    given = dict(x=x, positions=positions, attn_norm=attn_norm, ffn_norm=ffn_norm, final_norm=final_norm, sc_w_in=sc_w_in, sc_conv_w=sc_conv_w, sc_w_out=sc_w_out, kv_in_norm=kv_in_norm, w_dkv=w_dkv, kv_latent_norm=kv_latent_norm, w_kr=w_kr, w_uk=w_uk, w_uv=w_uv, w_dq=w_dq, q_latent_norm=q_latent_norm, w_uq=w_uq, w_o=w_o, ffn_w_up=ffn_w_up, ffn_conv_w=ffn_conv_w, ffn_conv_b=ffn_conv_b, ffn_w_down=ffn_w_down, loss_target=loss_target, m_attn_norm=m_attn_norm, m_ffn_norm=m_ffn_norm, m_final_norm=m_final_norm, m_sc_w_in=m_sc_w_in, m_sc_conv_w=m_sc_conv_w, m_sc_w_out=m_sc_w_out, m_kv_in_norm=m_kv_in_norm, m_w_dkv=m_w_dkv, m_kv_latent_norm=m_kv_latent_norm, m_w_kr=m_w_kr, m_w_uk=m_w_uk, m_w_uv=m_w_uv, m_w_dq=m_w_dq, m_q_latent_norm=m_q_latent_norm, m_w_uq=m_w_uq, m_w_o=m_w_o, m_ffn_w_up=m_ffn_w_up, m_ffn_conv_w=m_ffn_conv_w, m_ffn_conv_b=m_ffn_conv_b, m_ffn_w_down=m_ffn_w_down, v_attn_norm=v_attn_norm, v_ffn_norm=v_ffn_norm, v_final_norm=v_final_norm, v_sc_w_in=v_sc_w_in, v_sc_conv_w=v_sc_conv_w, v_sc_w_out=v_sc_w_out, v_kv_in_norm=v_kv_in_norm, v_w_dkv=v_w_dkv, v_kv_latent_norm=v_kv_latent_norm, v_w_kr=v_w_kr, v_w_uk=v_w_uk, v_w_uv=v_w_uv, v_w_dq=v_w_dq, v_q_latent_norm=v_q_latent_norm, v_w_uq=v_w_uq, v_w_o=v_w_o, v_ffn_w_up=v_ffn_w_up, v_ffn_conv_w=v_ffn_conv_w, v_ffn_conv_b=v_ffn_conv_b, v_ffn_w_down=v_ffn_w_down)
    weights = {n: given[n] for n in TWIN_WEIGHTS}
    shared = {n: given[n] for n in SHARED_INPUTS}
    per_example = {n: given[n] for n in ['x', 'positions']}
    grad_fn = _jax.value_and_grad(_loss, argnums=(0, 1))

    def one_microbatch(ex, loss_target):
        ex = dict(ex)
        diff = ex.pop(TWIN_DIFF_INPUT)
        return grad_fn(weights, diff, {**shared, **ex}, loss_target)

    if N_MICROBATCH == 1:
        loss, (grad_w, grad_x) = one_microbatch(per_example, given["loss_target"])
    else:
        def body(carry, xs):
            loss_sum, grad_sum = carry
            l_k, (gw_k, gx_k) = one_microbatch(xs[0], xs[1])
            with _jax.named_scope("update"):
                return (loss_sum + l_k, _jax.tree.map(_jnp.add, grad_sum, gw_k)), gx_k

        init = (_jnp.zeros((), _jnp.float32), _jax.tree.map(_jnp.zeros_like, weights))
        (loss, grad_w), grad_x = _jax.lax.scan(body, init, (per_example, given["loss_target"]))
    with _jax.named_scope("update"):
        delta_w, new_m, new_v = {}, {}, {}
        for n in TWIN_WEIGHTS:
            delta_w[n], new_m[n], new_v[n] = _adamw(weights[n], grad_w[n], given["m_" + n], given["v_" + n])
    return (loss, grad_x, *[grad_w[n] for n in TWIN_WEIGHTS], *[delta_w[n] for n in TWIN_WEIGHTS],
            *[new_m[n] for n in TWIN_WEIGHTS], *[new_v[n] for n in TWIN_WEIGHTS])
```

```python
import functools

import jax
import jax.numpy as jnp
from jax import lax
from jax.experimental import pallas as pl
from jax.experimental.pallas import tpu as pltpu

F32 = jnp.float32
BF16 = jnp.bfloat16

T = 2048
D = 1024
F_FF = 2816
N_HEADS = 8
QK_NOPE = 128
QK_ROPE = 64
V_HEAD = 128
Q_LORA = 384
KV_LORA = 256
CHUNK = 64
CHUNK_SHIFT = 6
ROPE_THETA = 10000.0
EPS = 1e-6
NEG_INF = -1e30
HEAD_PAD = 256
KVP = KV_LORA + 128

ADAM_LR = 0.001
ADAM_B1 = 0.9
ADAM_B2 = 0.999
ADAM_EPS = 1e-08
ADAM_WD = 0.01
ADAM_STEP = 10

N_CHIPS = 4
N_DEV = 8
LANES = 128
TC = 256
V7X_VMEM_LIMIT = 56 * 1024 * 1024

MESH = pl.DeviceIdType.MESH

BIG = (("sc_w_in", "col"), ("sc_w_out", "row"), ("w_dkv", "row"), ("w_kr", "row"), ("w_uk", "col"),
       ("w_uv", "col"), ("w_dq", "row"), ("w_uq", "col"), ("w_o", "row"), ("ffn_w_up", "col"),
       ("ffn_w_down", "row"))
SLAB_COLS = 1024


def _cp(*sem):
    return pltpu.CompilerParams(dimension_semantics=sem, vmem_limit_bytes=V7X_VMEM_LIMIT)


def _tile(n, cands):
    for c in cands:
        if n % c == 0:
            return c
    raise ValueError(f"no tile for {n}")


def _matmul(a, b, form, out_dtype, name, add=None):
    if form == "nn":
        (m, k), (k2, n) = a.shape, b.shape
        dims = (((1,), (0,)), ((), ()))
    elif form == "nt":
        (m, k), (n, k2) = a.shape, b.shape
        dims = (((1,), (1,)), ((), ()))
    else:
        (k, m), (k2, n) = a.shape, b.shape
        dims = (((0,), (0,)), ((), ()))
    assert k == k2, (a.shape, b.shape, form)
    big_k = k > 4096
    tm = _tile(m, (512, 384, 256, 128) if big_k else (1024, 512, 384, 256, 128))
    tn = _tile(n, (512, 384, 256, 128))

    def body(*refs):
        if add is None:
            a_ref, b_ref, o_ref = refs
        else:
            a_ref, b_ref, add_ref, o_ref = refs
        acc = lax.dot_general(a_ref[...].astype(BF16), b_ref[...].astype(BF16), dims,
                              preferred_element_type=F32)
        if add is not None:
            acc = acc + add_ref[...]
        o_ref[...] = acc.astype(o_ref.dtype)

    if form == "tn":
        a_spec = pl.BlockSpec((k, tm), lambda i, j: (0, i))
    else:
        a_spec = pl.BlockSpec((tm, k), lambda i, j: (i, 0))
    if form == "nt":
        b_spec = pl.BlockSpec((tn, k), lambda i, j: (j, 0))
    else:
        b_spec = pl.BlockSpec((k, tn), lambda i, j: (0, j))
    o_spec = pl.BlockSpec((tm, tn), lambda i, j: (i, j))
    in_specs = [a_spec, b_spec] + ([o_spec] if add is not None else [])
    args = (a, b) + ((add,) if add is not None else ())
    return pl.pallas_call(
        body, name=name, grid=(m // tm, n // tn), in_specs=in_specs, out_specs=o_spec,
        out_shape=jax.ShapeDtypeStruct((m, n), out_dtype), compiler_params=_cp("parallel", "parallel"),
    )(*args)


def _rms_fwd(x, g, name):
    t, d = x.shape
    tr = 512

    def body(x_ref, g_ref, o_ref):
        xv = x_ref[...]
        r = lax.rsqrt(jnp.mean(xv * xv, axis=1, keepdims=True) + EPS)
        o_ref[...] = (xv * r * g_ref[...]).astype(o_ref.dtype)

    row = pl.BlockSpec((tr, d), lambda i: (i, 0))
    return pl.pallas_call(
        body, name=name, grid=(t // tr,), in_specs=[row, pl.BlockSpec((1, d), lambda i: (0, 0))],
        out_specs=row, out_shape=jax.ShapeDtypeStruct((t, d), BF16), compiler_params=_cp("parallel"),
    )(x, g)


def _rms_bwd_math(xv, g, dy):
    r = lax.rsqrt(jnp.mean(xv * xv, axis=1, keepdims=True) + EPS)
    xh = xv * r
    gy = dy * g
    dx = r * (gy - xh * jnp.mean(gy * xh, axis=1, keepdims=True))
    dg = jnp.sum(dy * xh, axis=0, keepdims=True)
    return dx, dg


def _rms_bwd(x, g, dy, add, name):
    t, d = x.shape
    tr = 512

    def body(*refs):
        if add is None:
            x_ref, g_ref, dy_ref, dx_ref, dg_ref = refs
        else:
            x_ref, g_ref, dy_ref, add_ref, dx_ref, dg_ref = refs
        dx, dg = _rms_bwd_math(x_ref[...], g_ref[...], dy_ref[...].astype(F32))
        if add is not None:
            dx = dx + add_ref[...]
        dx_ref[...] = dx

        @pl.when(pl.program_id(0) == 0)
        def _():
            dg_ref[...] = jnp.zeros_like(dg_ref)

        dg_ref[...] += dg

    row = pl.BlockSpec((tr, d), lambda i: (i, 0))
    vec = pl.BlockSpec((1, d), lambda i: (0, 0))
    in_specs = [row, vec, row] + ([row] if add is not None else [])
    args = (x, g, dy) + ((add,) if add is not None else ())
    return pl.pallas_call(
        body, name=name, grid=(t // tr,), in_specs=in_specs, out_specs=[row, vec],
        out_shape=[jax.ShapeDtypeStruct((t, d), F32), jax.ShapeDtypeStruct((1, d), F32)],
        compiler_params=_cp("arbitrary"),
    )(*args)


def _loss_head(h, g, tgt):
    t, d = h.shape
    tr = 512

    def body(h_ref, g_ref, t_ref, loss_ref, dh_ref, dg_ref):
        xv = h_ref[...]
        gv = g_ref[...]
        r = lax.rsqrt(jnp.mean(xv * xv, axis=1, keepdims=True) + EPS)
        err = xv * r * gv - t_ref[...]
        part = 0.5 * jnp.sum(jnp.mean(err * err, axis=1, keepdims=True), axis=0, keepdims=True)
        dx, dg = _rms_bwd_math(xv, gv, err * (1.0 / d))
        dh_ref[...] = dx

        @pl.when(pl.program_id(0) == 0)
        def _():
            dg_ref[...] = jnp.zeros_like(dg_ref)
            loss_ref[...] = jnp.zeros_like(loss_ref)

        dg_ref[...] += dg
        loss_ref[...] += jnp.broadcast_to(part, loss_ref.shape)

    row = pl.BlockSpec((tr, d), lambda i: (i, 0))
    vec = pl.BlockSpec((1, d), lambda i: (0, 0))
    lspec = pl.BlockSpec((1, LANES), lambda i: (0, 0))
    return pl.pallas_call(
        body, name="loss_head", grid=(t // tr,), in_specs=[row, vec, row], out_specs=[lspec, row, vec],
        out_shape=[jax.ShapeDtypeStruct((1, LANES), F32), jax.ShapeDtypeStruct((t, d), F32),
                   jax.ShapeDtypeStruct((1, d), F32)],
        compiler_params=_cp("arbitrary"),
    )(h, g, tgt)


def _rot_half(x):
    lane = lax.broadcasted_iota(jnp.int32, x.shape, 1)
    return jnp.where((lane % QK_ROPE) < QK_ROPE // 2, -pltpu.roll(x, LANES - 32, axis=1),
                     pltpu.roll(x, 32, axis=1))


def _rope_fwd_math(x, cos, sin):
    return x * cos + _rot_half(x) * sin


def _rope_bwd_math(dy, cos, sin):
    return dy * cos - _rot_half(dy * sin)


def _q_rope_fwd(qpre, cos, sin):
    t, w = qpre.shape
    tr = 256

    def body(q_ref, c_ref, s_ref, o_ref):
        cv, sv = c_ref[...], s_ref[...]
        for h in range(N_HEADS):
            lo = h * HEAD_PAD
            o_ref[:, lo:lo + QK_NOPE] = q_ref[:, lo:lo + QK_NOPE].astype(BF16)
            o_ref[:, lo + QK_NOPE:lo + HEAD_PAD] = _rope_fwd_math(
                q_ref[:, lo + QK_NOPE:lo + HEAD_PAD], cv, sv).astype(BF16)

    row = pl.BlockSpec((tr, w), lambda i: (i, 0))
    tab = pl.BlockSpec((tr, LANES), lambda i: (i, 0))
    return pl.pallas_call(
        body, name="q_rope_fwd", grid=(t // tr,), in_specs=[row, tab, tab], out_specs=row,
        out_shape=jax.ShapeDtypeStruct((t, w), BF16), compiler_params=_cp("parallel"),
    )(qpre, cos, sin)


def _kv_elem_fwd(kvpre, g, cos, sin):
    t = kvpre.shape[0]
    tr = 512

    def body(p_ref, g_ref, c_ref, s_ref, ckv_ref, kr_ref):
        lat = p_ref[:, :KV_LORA]
        r = lax.rsqrt(jnp.mean(lat * lat, axis=1, keepdims=True) + EPS)
        ckv_ref[...] = (lat * r * g_ref[...]).astype(BF16)
        kr_ref[...] = _rope_fwd_math(p_ref[:, KV_LORA:], c_ref[...], s_ref[...]).astype(BF16)

    tab = pl.BlockSpec((tr, LANES), lambda i: (i, 0))
    return pl.pallas_call(
        body, name="kv_elem_fwd", grid=(t // tr,),
        in_specs=[pl.BlockSpec((tr, KVP), lambda i: (i, 0)), pl.BlockSpec((1, KV_LORA), lambda i: (0, 0)), tab, tab],
        out_specs=[pl.BlockSpec((tr, KV_LORA), lambda i: (i, 0)), tab],
        out_shape=[jax.ShapeDtypeStruct((t, KV_LORA), BF16), jax.ShapeDtypeStruct((t, LANES), BF16)],
        compiler_params=_cp("parallel"),
    )(kvpre, g, cos, sin)


def _kv_elem_bwd(kvpre, g, dckv, dkr, cos, sin):
    t = kvpre.shape[0]
    tr = 512

    def body(p_ref, g_ref, dc_ref, dk_ref, c_ref, s_ref, dp_ref, dg_ref):
        dlat, dg = _rms_bwd_math(p_ref[:, :KV_LORA], g_ref[...], dc_ref[...])
        dp_ref[:, :KV_LORA] = dlat.astype(BF16)
        dp_ref[:, KV_LORA:] = _rope_bwd_math(dk_ref[...], c_ref[...], s_ref[...]).astype(BF16)

        @pl.when(pl.program_id(0) == 0)
        def _():
            dg_ref[...] = jnp.zeros_like(dg_ref)

        dg_ref[...] += dg

    tab = pl.BlockSpec((tr, LANES), lambda i: (i, 0))
    pre = pl.BlockSpec((tr, KVP), lambda i: (i, 0))
    vec = pl.BlockSpec((1, KV_LORA), lambda i: (0, 0))
    return pl.pallas_call(
        body, name="kv_elem_bwd", grid=(t // tr,),
        in_specs=[pre, vec, pl.BlockSpec((tr, KV_LORA), lambda i: (i, 0)), tab, tab, tab],
        out_specs=[pre, vec],
        out_shape=[jax.ShapeDtypeStruct((t, KVP), BF16), jax.ShapeDtypeStruct((1, KV_LORA), F32)],
        compiler_params=_cp("arbitrary"),
    )(kvpre, g, dckv, dkr, cos, sin)


def _shift_down(x, k):
    row = lax.broadcasted_iota(jnp.int32, x.shape, 0)
    return jnp.where(row >= k, pltpu.roll(x, k, axis=0), 0.0)


def _shift_up(x, k):
    n = x.shape[0]
    row = lax.broadcasted_iota(jnp.int32, x.shape, 0)
    return jnp.where(row < n - k, pltpu.roll(x, n - k, axis=0), 0.0)


def _conv3(x, w_ref):
    return _shift_down(x, 2) * w_ref[0:1, :] + _shift_down(x, 1) * w_ref[1:2, :] + x * w_ref[2:3, :]


def _conv3_t(dy, w_ref):
    return dy * w_ref[2:3, :] + _shift_up(dy, 1) * w_ref[1:2, :] + _shift_up(dy, 2) * w_ref[0:1, :]


def _conv3_dw(dy, x, dw_ref):
    dw_ref[0:1, :] = jnp.sum(dy * _shift_down(x, 2), axis=0, keepdims=True)
    dw_ref[1:2, :] = jnp.sum(dy * _shift_down(x, 1), axis=0, keepdims=True)
    dw_ref[2:3, :] = jnp.sum(dy * x, axis=0, keepdims=True)


def _scmix_fwd(z, w):
    t = z.shape[0]
    nb = D // TC

    def body(z_ref, w_ref, m_ref):
        cu = z_ref[:, TC:2 * TC] * z_ref[:, 2 * TC:]
        m_ref[...] = (z_ref[:, :TC] * _conv3(cu, w_ref)).astype(BF16)

    return pl.pallas_call(
        body, name="scmix_fwd", grid=(nb,),
        in_specs=[pl.BlockSpec((t, 3 * TC), lambda j: (0, j)), pl.BlockSpec((3, TC), lambda j: (0, j))],
        out_specs=pl.BlockSpec((t, TC), lambda j: (0, j)),
        out_shape=jax.ShapeDtypeStruct((t, D), BF16), compiler_params=_cp("parallel"),
    )(z, w)


def _scmix_bwd(z, w, dm):
    t = z.shape[0]
    nb = D // TC

    def body(z_ref, w_ref, dm_ref, dz_ref, dw_ref):
        b = z_ref[:, :TC]
        c = z_ref[:, TC:2 * TC]
        u = z_ref[:, 2 * TC:]
        cu = c * u
        dmv = dm_ref[...].astype(F32)
        dz_ref[:, :TC] = (dmv * _conv3(cu, w_ref)).astype(BF16)
        dcv = dmv * b
        _conv3_dw(dcv, cu, dw_ref)
        dcu = _conv3_t(dcv, w_ref)
        dz_ref[:, TC:2 * TC] = (dcu * u).astype(BF16)
        dz_ref[:, 2 * TC:] = (dcu * c).astype(BF16)

    zspec = pl.BlockSpec((t, 3 * TC), lambda j: (0, j))
    wspec = pl.BlockSpec((3, TC), lambda j: (0, j))
    return pl.pallas_call(
        body, name="scmix_bwd", grid=(nb,),
        in_specs=[zspec, wspec, pl.BlockSpec((t, TC), lambda j: (0, j))], out_specs=[zspec, wspec],
        out_shape=[jax.ShapeDtypeStruct((t, 3 * D), BF16), jax.ShapeDtypeStruct((3, D), F32)],
        compiler_params=_cp("parallel"),
    )(z, w, dm)


def _gate_fwd(up, w, bias, name):
    t = up.shape[0]
    nb = F_FF // TC

    def body(u_ref, w_ref, b_ref, a_ref):
        gc = _conv3(u_ref[:, :TC], w_ref) + b_ref[...]
        a_ref[...] = (gc * jax.nn.sigmoid(gc) * u_ref[:, TC:]).astype(BF16)

    return pl.pallas_call(
        body, name=name, grid=(nb,),
        in_specs=[pl.BlockSpec((t, 2 * TC), lambda j: (0, j)), pl.BlockSpec((3, TC), lambda j: (0, j)),
                  pl.BlockSpec((1, TC), lambda j: (0, j))],
        out_specs=pl.BlockSpec((t, TC), lambda j: (0, j)),
        out_shape=jax.ShapeDtypeStruct((t, F_FF), BF16), compiler_params=_cp("parallel"),
    )(up, w, bias)


def _gate_bwd(up, w, bias, da, name):
    t = up.shape[0]
    nb = F_FF // TC

    def body(u_ref, w_ref, b_ref, da_ref, du_ref, dw_ref, db_ref):
        g = u_ref[:, :TC]
        gc = _conv3(g, w_ref) + b_ref[...]
        sg = jax.nn.sigmoid(gc)
        dav = da_ref[...].astype(F32)
        du_ref[:, TC:] = (dav * (gc * sg)).astype(BF16)
        dgc = dav * u_ref[:, TC:] * (sg * (1.0 + gc * (1.0 - sg)))
        db_ref[...] = jnp.sum(dgc, axis=0, keepdims=True)
        _conv3_dw(dgc, g, dw_ref)
        du_ref[:, :TC] = _conv3_t(dgc, w_ref).astype(BF16)

    uspec = pl.BlockSpec((t, 2 * TC), lambda j: (0, j))
    wspec = pl.BlockSpec((3, TC), lambda j: (0, j))
    bspec = pl.BlockSpec((1, TC), lambda j: (0, j))
    return pl.pallas_call(
        body, name=name, grid=(nb,),
        in_specs=[uspec, wspec, bspec, pl.BlockSpec((t, TC), lambda j: (0, j))],
        out_specs=[uspec, wspec, bspec],
        out_shape=[jax.ShapeDtypeStruct((t, 2 * F_FF), BF16), jax.ShapeDtypeStruct((3, F_FF), F32),
                   jax.ShapeDtypeStruct((1, F_FF), F32)],
        compiler_params=_cp("parallel"),
    )(up, w, bias, da)


ATT_TQ = 256
ATT_SCALE = (QK_NOPE + QK_ROPE) ** -0.5
NT_DIMS = (((1,), (1,)), ((), ()))
TN_DIMS = (((0,), (0,)), ((), ()))


def _attn_probs(q, kn, kr, qi):
    s = lax.dot_general(q[:, :QK_NOPE], kn, NT_DIMS, preferred_element_type=F32)
    s = s + lax.dot_general(q[:, QK_NOPE:], kr, NT_DIMS, preferred_element_type=F32)
    s = s * ATT_SCALE
    row = qi * ATT_TQ + lax.broadcasted_iota(jnp.int32, s.shape, 0)
    col = lax.broadcasted_iota(jnp.int32, s.shape, 1)
    s = jnp.where(lax.shift_right_logical(col, CHUNK_SHIFT) <= lax.shift_right_logical(row, CHUNK_SHIFT), s, NEG_INF)
    p = jnp.exp(s - jnp.max(s, axis=1, keepdims=True))
    return p * (1.0 / jnp.sum(p, axis=1, keepdims=True))


def _attn_specs():
    q = pl.BlockSpec((ATT_TQ, HEAD_PAD), lambda h, i: (i, h))
    kn = pl.BlockSpec((T, QK_NOPE), lambda h, i: (0, h))
    kr = pl.BlockSpec((T, LANES), lambda h, i: (0, 0))
    v = pl.BlockSpec((T, V_HEAD), lambda h, i: (0, h))
    o = pl.BlockSpec((ATT_TQ, V_HEAD), lambda h, i: (i, h))
    return q, kn, kr, v, o


def _attn_fwd(q, kn, kr, v):
    def body(q_ref, kn_ref, kr_ref, v_ref, o_ref):
        p = _attn_probs(q_ref[...], kn_ref[...], kr_ref[...], pl.program_id(1))
        o_ref[...] = jnp.dot(p.astype(BF16), v_ref[...], preferred_element_type=F32).astype(BF16)

    qs, kns, krs, vs, os_ = _attn_specs()
    return pl.pallas_call(
        body, name="attn_fwd", grid=(N_HEADS, T // ATT_TQ), in_specs=[qs, kns, krs, vs], out_specs=os_,
        out_shape=jax.ShapeDtypeStruct((T, N_HEADS * V_HEAD), BF16), compiler_params=_cp("parallel", "parallel"),
    )(q, kn, kr, v)


def _attn_bwd(q, kn, kr, v, do, cos, sin):
    def body(q_ref, kn_ref, kr_ref, v_ref, do_ref, c_ref, s_ref, dq_ref, dkn_ref, dkr_ref, dv_ref):
        h, qi = pl.program_id(0), pl.program_id(1)
        qv, knv, krv, dov = q_ref[...], kn_ref[...], kr_ref[...], do_ref[...]
        p = _attn_probs(qv, knv, krv, qi)
        dp = lax.dot_general(dov, v_ref[...], NT_DIMS, preferred_element_type=F32)
        ds = (p * (dp - jnp.sum(p * dp, axis=1, keepdims=True)) * ATT_SCALE).astype(BF16)
        dq_ref[:, :QK_NOPE] = jnp.dot(ds, knv, preferred_element_type=F32).astype(BF16)
        dqr = jnp.dot(ds, krv, preferred_element_type=F32)
        dq_ref[:, QK_NOPE:] = _rope_bwd_math(dqr, c_ref[...], s_ref[...]).astype(BF16)
        dv = lax.dot_general(p.astype(BF16), dov, TN_DIMS, preferred_element_type=F32)
        dkn = lax.dot_general(ds, qv[:, :QK_NOPE], TN_DIMS, preferred_element_type=F32)
        dkr = lax.dot_general(ds, qv[:, QK_NOPE:], TN_DIMS, preferred_element_type=F32)

        @pl.when(qi == 0)
        def _():
            dkn_ref[...] = jnp.zeros_like(dkn_ref)
            dv_ref[...] = jnp.zeros_like(dv_ref)

        @pl.when((qi == 0) & (h == 0))
        def _():
            dkr_ref[...] = jnp.zeros_like(dkr_ref)

        dkn_ref[...] += dkn
        dv_ref[...] += dv
        dkr_ref[...] += dkr

    qs, kns, krs, vs, os_ = _attn_specs()
    tab = pl.BlockSpec((ATT_TQ, LANES), lambda h, i: (i, 0))
    return pl.pallas_call(
        body, name="attn_bwd", grid=(N_HEADS, T // ATT_TQ), in_specs=[qs, kns, krs, vs, os_, tab, tab],
        out_specs=[qs, kns, krs, vs],
        out_shape=[jax.ShapeDtypeStruct((T, N_HEADS * HEAD_PAD), BF16), jax.ShapeDtypeStruct((T, N_HEADS * QK_NOPE), F32),
                   jax.ShapeDtypeStruct((T, LANES), F32), jax.ShapeDtypeStruct((T, N_HEADS * V_HEAD), F32)],
        compiler_params=_cp("arbitrary", "arbitrary"),
    )(q, kn, kr, v, do, cos, sin)


def _adamw(w, g, m, v, name):
    r, c = w.shape
    tr = _tile(r, (496, 136, 8))

    def body(w_ref, g_ref, m_ref, v_ref, d_ref, nm_ref, nv_ref):
        gv = g_ref[...]
        nm = ADAM_B1 * m_ref[...] + (1.0 - ADAM_B1) * gv
        nv = ADAM_B2 * v_ref[...] + (1.0 - ADAM_B2) * (gv * gv)
        m_hat = nm / (1.0 - ADAM_B1 ** ADAM_STEP)
        v_hat = nv / (1.0 - ADAM_B2 ** ADAM_STEP)
        d_ref[...] = -ADAM_LR * (m_hat / (jnp.sqrt(v_hat) + ADAM_EPS) + ADAM_WD * w_ref[...])
        nm_ref[...] = nm
        nv_ref[...] = nv

    blk = pl.BlockSpec((tr, c), lambda i: (i, 0))
    shp = jax.ShapeDtypeStruct((r, c), F32)
    return pl.pallas_call(
        body, name=name, grid=(r // tr,), in_specs=[blk] * 4, out_specs=[blk] * 3, out_shape=[shp] * 3,
        compiler_params=_cp("parallel"),
    )(w, g, m, v)


SUM_TR = 496


def _peer_chip(k_me, j):
    return k_me ^ jnp.where(j == 0, 2, jnp.where(j == 1, 1, 3))


def _pair_sum(ids, g, ra):
    _, r, c = g.shape
    half = r // 2
    nblk = half // SUM_TR

    def body(ids_ref, g_ref, ra_ref, o_ref):
        o_ref[...] = (g_ref[...].astype(F32) + ra_ref[...].astype(F32)).astype(BF16)

    grid_spec = pltpu.PrefetchScalarGridSpec(
        num_scalar_prefetch=1, grid=(3, nblk),
        in_specs=[pl.BlockSpec((None, SUM_TR, c), lambda j, i, ids: (_peer_chip(ids[1], j), ids[0] * nblk + i, 0)),
                  pl.BlockSpec((None, SUM_TR, c), lambda j, i, ids: (_peer_chip(ids[1], j), i, 0))],
        out_specs=pl.BlockSpec((None, SUM_TR, c), lambda j, i, ids: (j, i, 0)))
    return pl.pallas_call(
        body, name="rs_pair_sum", grid_spec=grid_spec, out_shape=jax.ShapeDtypeStruct((3, half, c), BF16),
        compiler_params=_cp("parallel", "parallel"),
    )(ids, g, ra)


def _chip_sum(ids, g, ra, rb):
    _, r, c = g.shape
    half = r // 2
    nblk = half // SUM_TR

    def body(ids_ref, g_ref, ra_ref, rb_ref, o_ref):
        acc = g_ref[...].astype(F32) + ra_ref[...].astype(F32)
        for j in range(3):
            acc = acc + rb_ref[j].astype(F32)
        o_ref[...] = acc

    grid_spec = pltpu.PrefetchScalarGridSpec(
        num_scalar_prefetch=1, grid=(nblk,),
        in_specs=[pl.BlockSpec((None, SUM_TR, c), lambda i, ids: (ids[1], ids[0] * nblk + i, 0)),
                  pl.BlockSpec((None, SUM_TR, c), lambda i, ids: (ids[1], i, 0)),
                  pl.BlockSpec((3, SUM_TR, c), lambda i, ids: (0, i, 0))],
        out_specs=pl.BlockSpec((SUM_TR, c), lambda i, ids: (i, 0)))
    return pl.pallas_call(
        body, name="rs_chip_sum", grid_spec=grid_spec, out_shape=jax.ShapeDtypeStruct((half, c), F32),
        compiler_params=_cp("parallel"),
    )(ids, g, ra, rb)


ANY = pl.BlockSpec(memory_space=pl.ANY)


def _position():
    x, y, c = lax.axis_index("x"), lax.axis_index("y"), lax.axis_index("c")
    chips = [(1 - x, y), (x, 1 - y), (1 - x, 1 - y)]
    return x, y, c, chips


def _all_gather_slab(w):
    r, cols = w.shape
    half = r // 2

    def body(w_ref, out_ref, send_sems, recv_sems, pass_send, pass_recv, local_sem):
        x, y, c, chips = _position()
        sibling = (x, y, 1 - c)

        def blk(px, py, pc):
            return out_ref.at[2 * px + py, pl.ds(pc * half, half), :]

        def copy(sems, j, block, to, src=None):
            return pltpu.make_async_remote_copy(
                src_ref=blk(*block) if src is None else src, dst_ref=blk(*block),
                send_sem=sems[0].at[j], recv_sem=sems[1].at[j], device_id=to, device_id_type=MESH)

        ici, d2d = (send_sems, recv_sems), (pass_send, pass_recv)
        mine = pltpu.make_async_copy(w_ref, out_ref.at[2 * x + y], local_sem)
        mine.start()
        my_half = w_ref.at[pl.ds(c * half, half), :]
        first = [copy(ici, j, (x, y, c), (*chip, c), src=my_half) for j, chip in enumerate(chips)]
        for cp in first:
            cp.start()
        passed = [copy(d2d, j, (*chip, c), sibling) for j, chip in enumerate(chips)]
        for j, chip in enumerate(chips):
            copy(ici, j, (*chip, c), (x, y, c)).wait_recv()
            passed[j].start()
        for j, chip in enumerate(chips):
            copy(d2d, j, (*chip, 1 - c), (x, y, c)).wait_recv()
        for cp in first + passed:
            cp.wait_send()
        mine.wait()

    return pl.pallas_call(
        body, name="ag_weights", in_specs=[ANY], out_specs=ANY,
        out_shape=jax.ShapeDtypeStruct((N_CHIPS, r, cols), w.dtype),
        scratch_shapes=[pltpu.SemaphoreType.DMA((3,))] * 4 + [pltpu.SemaphoreType.DMA],
    )(w)


def _pair_exchange(g):
    n, r, cols = g.shape
    half = r // 2

    def body(g_ref, out_ref, send_sem, recv_sem):
        x, y, c, _ = _position()
        cp = pltpu.make_async_remote_copy(
            src_ref=g_ref.at[:, pl.ds((1 - c) * half, half), :], dst_ref=out_ref,
            send_sem=send_sem, recv_sem=recv_sem, device_id=(x, y, 1 - c), device_id_type=MESH)
        cp.start()
        cp.wait()

    return pl.pallas_call(
        body, name="rs_pair_exchange", in_specs=[ANY], out_specs=ANY,
        out_shape=jax.ShapeDtypeStruct((n, half, cols), g.dtype),
        scratch_shapes=[pltpu.SemaphoreType.DMA, pltpu.SemaphoreType.DMA],
    )(g)


def _chip_exchange(s):
    def body(s_ref, out_ref, send_sems, recv_sems):
        x, y, c, chips = _position()
        cps = [pltpu.make_async_remote_copy(
            src_ref=s_ref.at[j], dst_ref=out_ref.at[j], send_sem=send_sems.at[j], recv_sem=recv_sems.at[j],
            device_id=(*chip, c), device_id_type=MESH) for j, chip in enumerate(chips)]
        for cp in cps:
            cp.start()
        for cp in cps:
            cp.wait()

    return pl.pallas_call(
        body, name="rs_chip_exchange", in_specs=[ANY], out_specs=ANY,
        out_shape=jax.ShapeDtypeStruct(s.shape, s.dtype),
        scratch_shapes=[pltpu.SemaphoreType.DMA((3,)), pltpu.SemaphoreType.DMA((3,))],
    )(s)


def _pair_gather(g8):
    half, cols = g8.shape

    def body(g_ref, out_ref, send_sem, recv_sem, local_sem):
        x, y, c, _ = _position()
        mine = pltpu.make_async_copy(g_ref, out_ref.at[c], local_sem)
        mine.start()
        cp = pltpu.make_async_remote_copy(
            src_ref=g_ref, dst_ref=out_ref.at[c], send_sem=send_sem, recv_sem=recv_sem,
            device_id=(x, y, 1 - c), device_id_type=MESH)
        cp.start()
        cp.wait_send()
        pltpu.make_async_remote_copy(
            src_ref=g_ref, dst_ref=out_ref.at[1 - c], send_sem=send_sem, recv_sem=recv_sem,
            device_id=(x, y, 1 - c), device_id_type=MESH).wait_recv()
        mine.wait()

    return pl.pallas_call(
        body, name="rs_pair_gather", in_specs=[ANY], out_specs=ANY,
        out_shape=jax.ShapeDtypeStruct((2, half, cols), g8.dtype),
        scratch_shapes=[pltpu.SemaphoreType.DMA, pltpu.SemaphoreType.DMA, pltpu.SemaphoreType.DMA],
    )(g8)


def _all_reduce_small(vec, name):
    r, cols = vec.shape

    def body(v_ref, o_ref, gath, send_sems, recv_sems):
        x, y, c, _ = _position()
        me = 4 * x + 2 * y + c
        gath[me] = v_ref[...]
        cps = []
        for rel in range(1, N_DEV):
            peer = (x ^ (rel >> 2), y ^ ((rel >> 1) & 1), c ^ (rel & 1))
            cps.append(pltpu.make_async_remote_copy(
                src_ref=v_ref, dst_ref=gath.at[me], send_sem=send_sems.at[rel - 1], recv_sem=recv_sems.at[rel - 1],
                device_id=peer, device_id_type=MESH))
        for cp in cps:
            cp.start()
        for rel in range(1, N_DEV):
            pltpu.make_async_remote_copy(
                src_ref=v_ref, dst_ref=gath.at[me ^ rel], send_sem=send_sems.at[rel - 1],
                recv_sem=recv_sems.at[rel - 1], device_id=(x, y, c), device_id_type=MESH).wait_recv()
        for cp in cps:
            cp.wait_send()
        acc = gath[0]
        for d in range(1, N_DEV):
            acc = acc + gath[d]
        o_ref[...] = acc

    vm = pl.BlockSpec(memory_space=pltpu.VMEM)
    return pl.pallas_call(
        body, name=name, in_specs=[vm], out_specs=vm, out_shape=jax.ShapeDtypeStruct((r, cols), F32),
        scratch_shapes=[pltpu.VMEM((N_DEV, r, cols), F32), pltpu.SemaphoreType.DMA((N_DEV - 1,)),
                        pltpu.SemaphoreType.DMA((N_DEV - 1,))],
    )(vec)


def _as3(w):
    return w if w.ndim == 3 else w[None]


def _pack_local(ws, dtype):
    flat = jnp.concatenate([ws[n].astype(dtype).reshape(-1) for n, _ in BIG])
    return flat.reshape(-1, SLAB_COLS)


def _shard_sizes(ws):
    return [(n, kind, _as3(ws[n]).shape) for n, kind in BIG]


def _unpack_gathered(gath, sizes):
    flat = gath.reshape(N_CHIPS, -1)
    out, off = {}, 0
    for n, kind, (l, k, ns) in sizes:
        sz = l * k * ns
        blk = flat[:, off:off + sz].reshape(N_CHIPS, l, k, ns)
        off += sz
        if kind == "col":
            out[n] = blk.transpose(1, 2, 0, 3).reshape(l, k, N_CHIPS * ns)
        else:
            out[n] = blk.transpose(1, 0, 2, 3).reshape(l, N_CHIPS * k, ns)
    return out


def _pack_owner_major(full, sizes):
    parts = []
    for n, kind, (l, k, ns) in sizes:
        gfull = full[n]
        if kind == "col":
            blk = gfull.reshape(l, k, N_CHIPS, ns).transpose(2, 0, 1, 3)
        else:
            blk = gfull.reshape(l, N_CHIPS, k, ns).transpose(1, 0, 2, 3)
        parts.append(blk.reshape(N_CHIPS, -1))
    return jnp.concatenate(parts, axis=1).reshape(N_CHIPS, -1, SLAB_COLS)


def _unpack_local(slab, ws):
    flat = slab.reshape(-1)
    out, off = {}, 0
    for n, _ in BIG:
        sz = ws[n].size
        out[n] = flat[off:off + sz].reshape(ws[n].shape)
        off += sz
    return out


def _tiles_in(w, parts):
    k, n = w.shape
    c = n // parts
    return w.reshape(k, parts, c // TC, TC).transpose(0, 2, 1, 3).reshape(k, n)


def _tiles_out(w, parts):
    k, n = w.shape
    c = n // parts
    return w.reshape(k, c // TC, parts, TC).transpose(0, 2, 1, 3).reshape(k, n)


def _rope_tables(positions):
    half = QK_ROPE // 2
    inv_freq = 1.0 / (ROPE_THETA ** (jnp.arange(half, dtype=F32) / half))
    ang = positions.astype(F32)[:, None] * inv_freq
    zeros = jnp.zeros((positions.shape[0], LANES - QK_ROPE), F32)
    cos, sin = jnp.cos(ang), jnp.sin(ang)
    return jnp.concatenate([cos, cos, zeros], axis=1), jnp.concatenate([sin, sin, zeros], axis=1)


def _local_step(x, positions, tgt, wf, small):
    cos, sin = _rope_tables(positions)
    w_in = _tiles_in(wf["sc_w_in"][0], 3)
    w_out = wf["sc_w_out"][0]
    w_up = [_tiles_in(wf["ffn_w_up"][l], 2) for l in range(2)]
    w_down = [wf["ffn_w_down"][l] for l in range(2)]
    w_kv = jnp.concatenate([wf["w_dkv"][0], wf["w_kr"][0], jnp.zeros((D, LANES - QK_ROPE), BF16)], axis=1)
    w_uk, w_uv = wf["w_uk"][0], wf["w_uv"][0]
    w_dq = wf["w_dq"][0]
    w_uq = jnp.pad(wf["w_uq"][0].reshape(Q_LORA, N_HEADS, QK_NOPE + QK_ROPE),
                   ((0, 0), (0, 0), (0, HEAD_PAD - QK_NOPE - QK_ROPE))).reshape(Q_LORA, N_HEADS * HEAD_PAD)
    w_o = wf["w_o"][0]
    attn_norm, ffn_norm = small["attn_norm"], small["ffn_norm"]
    conv_b = small["ffn_conv_b"]

    def ffn_fwd(h, l):
        hf = _rms_fwd(h, ffn_norm[l:l + 1], f"ffn{l}_norm")
        up = _matmul(hf, w_up[l], "nn", F32, f"ffn{l}_up")
        a = _gate_fwd(up, small["ffn_conv_w"][l], conv_b[l:l + 1], f"ffn{l}_gate")
        return _matmul(a, w_down[l], "nn", F32, f"ffn{l}_down", add=h), (hf, up, a)

    def ffn_bwd(h, dh_out, l, saved):
        hf, up, a = saved
        da = _matmul(dh_out, w_down[l], "nt", BF16, f"ffn{l}_down_dx")
        d_down = _matmul(a, dh_out, "tn", BF16, f"ffn{l}_down_dw")
        dup, d_cw, d_cb = _gate_bwd(up, small["ffn_conv_w"][l], conv_b[l:l + 1], da, f"ffn{l}_gate_bwd")
        dhf = _matmul(dup, w_up[l], "nt", BF16, f"ffn{l}_up_dx")
        d_up = _tiles_out(_matmul(hf, dup, "tn", BF16, f"ffn{l}_up_dw"), 2)
        dh, d_norm = _rms_bwd(h, ffn_norm[l:l + 1], dhf, dh_out, f"ffn{l}_norm_bwd")
        return dh, d_down, d_up, d_cw, d_cb, d_norm

    hn0 = _rms_fwd(x, attn_norm[0:1], "attn0_norm")
    z = _matmul(hn0, w_in, "nn", F32, "sc_in")
    mix = _scmix_fwd(z, small["sc_conv_w"])
    h1 = _matmul(mix, w_out, "nn", F32, "sc_out", add=x)
    h2, ffn0_saved = ffn_fwd(h1, 0)

    hk = _rms_fwd(h2, small["kv_in_norm"], "kv_in_norm")
    kvpre = _matmul(hk, w_kv, "nn", F32, "kv_down")
    ckv, kr = _kv_elem_fwd(kvpre, small["kv_latent_norm"], cos, sin)
    kn = _matmul(ckv, w_uk, "nn", BF16, "kv_up_k")
    vv = _matmul(ckv, w_uv, "nn", BF16, "kv_up_v")

    hn1 = _rms_fwd(h2, attn_norm[1:2], "attn1_norm")
    cq_pre = _matmul(hn1, w_dq, "nn", F32, "q_down")
    cq = _rms_fwd(cq_pre, small["q_latent_norm"], "q_latent_norm")
    q = _q_rope_fwd(_matmul(cq, w_uq, "nn", F32, "q_up"), cos, sin)
    o = _attn_fwd(q, kn, kr, vv)
    h3 = _matmul(o, w_o, "nn", F32, "attn_out", add=h2)
    h4, ffn1_saved = ffn_fwd(h3, 1)

    loss, dh4, d_final = _loss_head(h4, small["final_norm"], tgt)

    dh3, d_down1, d_up1, d_cw1, d_cb1, d_fn1 = ffn_bwd(h3, dh4, 1, ffn1_saved)

    do = _matmul(dh3, w_o, "nt", BF16, "attn_out_dx")
    d_wo = _matmul(o, dh3, "tn", BF16, "attn_out_dw")
    dq, dkn, dkr, dvv = _attn_bwd(q, kn, kr, vv, do, cos, sin)
    dcq = _matmul(dq, w_uq, "nt", F32, "q_up_dx")
    d_wuq = _matmul(cq, dq, "tn", BF16, "q_up_dw")
    d_wuq = d_wuq.reshape(Q_LORA, N_HEADS, HEAD_PAD)[:, :, :QK_NOPE + QK_ROPE].reshape(Q_LORA, -1)
    dcq_pre, d_qln = _rms_bwd(cq_pre, small["q_latent_norm"], dcq, None, "q_latent_norm_bwd")
    dhn1 = _matmul(dcq_pre, w_dq, "nt", BF16, "q_down_dx")
    d_wdq = _matmul(hn1, dcq_pre, "tn", BF16, "q_down_dw")
    dh2, d_an1 = _rms_bwd(h2, attn_norm[1:2], dhn1, dh3, "attn1_norm_bwd")

    dckv = _matmul(dkn, w_uk, "nt", F32, "kv_up_k_dx")
    dckv = _matmul(dvv, w_uv, "nt", F32, "kv_up_v_dx", add=dckv)
    d_wuk = _matmul(ckv, dkn, "tn", BF16, "kv_up_k_dw")
    d_wuv = _matmul(ckv, dvv, "tn", BF16, "kv_up_v_dw")
    dkvpre, d_kvln = _kv_elem_bwd(kvpre, small["kv_latent_norm"], dckv, dkr, cos, sin)
    dhk = _matmul(dkvpre, w_kv, "nt", BF16, "kv_down_dx")
    d_wkv = _matmul(hk, dkvpre, "tn", BF16, "kv_down_dw")
    dh2, d_kvin = _rms_bwd(h2, small["kv_in_norm"], dhk, dh2, "kv_in_norm_bwd")

    dh1, d_down0, d_up0, d_cw0, d_cb0, d_fn0 = ffn_bwd(h1, dh2, 0, ffn0_saved)

    dmix = _matmul(dh1, w_out, "nt", BF16, "sc_out_dx")
    d_wout = _matmul(mix, dh1, "tn", BF16, "sc_out_dw")
    dz, d_scw = _scmix_bwd(z, small["sc_conv_w"], dmix)
    dhn0 = _matmul(dz, w_in, "nt", BF16, "sc_in_dx")
    d_win = _tiles_out(_matmul(hn0, dz, "tn", BF16, "sc_in_dw"), 3)
    dx, d_an0 = _rms_bwd(x, attn_norm[0:1], dhn0, dh1, "attn0_norm_bwd")

    big = {
        "sc_w_in": d_win[None], "sc_w_out": d_wout[None], "w_dkv": d_wkv[None, :, :KV_LORA],
        "w_kr": d_wkv[None, :, KV_LORA:KV_LORA + QK_ROPE], "w_uk": d_wuk[None], "w_uv": d_wuv[None],
        "w_dq": d_wdq[None], "w_uq": d_wuq[None], "w_o": d_wo[None],
        "ffn_w_up": jnp.stack([d_up0, d_up1]), "ffn_w_down": jnp.stack([d_down0, d_down1]),
    }
    small_g = {
        "attn_norm": jnp.concatenate([d_an0, d_an1]), "ffn_norm": jnp.concatenate([d_fn0, d_fn1]),
        "final_norm": d_final, "kv_in_norm": d_kvin, "kv_latent_norm": d_kvln, "q_latent_norm": d_qln,
        "ffn_conv_b": jnp.concatenate([d_cb0, d_cb1]), "sc_conv_w": d_scw, "ffn_conv_w": jnp.stack([d_cw0, d_cw1]),
    }
    return loss, dx, big, small_g


SMALL_REPL = ("attn_norm", "ffn_norm", "final_norm", "kv_in_norm", "kv_latent_norm", "q_latent_norm", "ffn_conv_b")
SMALL_SHARDED = ("sc_conv_w", "ffn_conv_w")
SMALL_ROWS = 256


def kernel(x, positions, attn_norm, ffn_norm, final_norm, sc_w_in, sc_conv_w, sc_w_out, kv_in_norm, w_dkv, kv_latent_norm, w_kr, w_uk, w_uv, w_dq, q_latent_norm, w_uq, w_o, ffn_w_up, ffn_conv_w, ffn_conv_b, ffn_w_down, loss_target, m_attn_norm, m_ffn_norm, m_final_norm, m_sc_w_in, m_sc_conv_w, m_sc_w_out, m_kv_in_norm, m_w_dkv, m_kv_latent_norm, m_w_kr, m_w_uk, m_w_uv, m_w_dq, m_q_latent_norm, m_w_uq, m_w_o, m_ffn_w_up, m_ffn_conv_w, m_ffn_conv_b, m_ffn_w_down, v_attn_norm, v_ffn_norm, v_final_norm, v_sc_w_in, v_sc_conv_w, v_sc_w_out, v_kv_in_norm, v_w_dkv, v_kv_latent_norm, v_w_kr, v_w_uk, v_w_uv, v_w_dq, v_q_latent_norm, v_w_uq, v_w_o, v_ffn_w_up, v_ffn_conv_w, v_ffn_conv_b, v_ffn_w_down):
    names = ("attn_norm", "ffn_norm", "final_norm", "sc_w_in", "sc_conv_w", "sc_w_out", "kv_in_norm", "w_dkv",
             "kv_latent_norm", "w_kr", "w_uk", "w_uv", "w_dq", "q_latent_norm", "w_uq", "w_o", "ffn_w_up",
             "ffn_conv_w", "ffn_conv_b", "ffn_w_down")
    w = dict(zip(names, (attn_norm, ffn_norm, final_norm, sc_w_in, sc_conv_w, sc_w_out, kv_in_norm, w_dkv,
                         kv_latent_norm, w_kr, w_uk, w_uv, w_dq, q_latent_norm, w_uq, w_o, ffn_w_up,
                         ffn_conv_w, ffn_conv_b, ffn_w_down)))
    m = dict(zip(names, (m_attn_norm, m_ffn_norm, m_final_norm, m_sc_w_in, m_sc_conv_w, m_sc_w_out, m_kv_in_norm,
                         m_w_dkv, m_kv_latent_norm, m_w_kr, m_w_uk, m_w_uv, m_w_dq, m_q_latent_norm, m_w_uq, m_w_o,
                         m_ffn_w_up, m_ffn_conv_w, m_ffn_conv_b, m_ffn_w_down)))
    v = dict(zip(names, (v_attn_norm, v_ffn_norm, v_final_norm, v_sc_w_in, v_sc_conv_w, v_sc_w_out, v_kv_in_norm,
                         v_w_dkv, v_kv_latent_norm, v_w_kr, v_w_uk, v_w_uv, v_w_dq, v_q_latent_norm, v_w_uq, v_w_o,
                         v_ffn_w_up, v_ffn_conv_w, v_ffn_conv_b, v_ffn_w_down)))

    ix, iy, ic = lax.axis_index("x"), lax.axis_index("y"), lax.axis_index("c")
    chip = 2 * ix + iy
    ids = jnp.stack([ic, chip]).astype(jnp.int32)

    sizes = _shard_sizes(w)
    gathered = _all_gather_slab(_pack_local(w, BF16))
    wf = _unpack_gathered(gathered, sizes)

    def place(shard, full_cols):
        ns = shard.shape[-1]
        full = jnp.zeros(shard.shape[:-1] + (full_cols,), F32)
        return lax.dynamic_update_slice_in_dim(full, shard, chip * ns, axis=shard.ndim - 1)

    taps = jnp.concatenate([place(sc_conv_w[0], D).reshape(-1), place(ffn_conv_w, F_FF).reshape(-1)])
    n_taps = taps.shape[0]
    tap_rows = -(-n_taps // (8 * LANES)) * 8
    taps = jnp.pad(taps, (0, tap_rows * LANES - n_taps)).reshape(tap_rows, LANES)
    taps = _all_reduce_small(jnp.where(ic == 0, taps, 0.0), "ag_taps").reshape(-1)
    small = {
        "attn_norm": attn_norm, "ffn_norm": ffn_norm, "final_norm": final_norm[None], "kv_in_norm": kv_in_norm[None],
        "kv_latent_norm": kv_latent_norm[None], "q_latent_norm": q_latent_norm, "ffn_conv_b": ffn_conv_b,
        "sc_conv_w": taps[:3 * D].reshape(3, D), "ffn_conv_w": taps[3 * D:3 * D + 2 * 3 * F_FF].reshape(2, 3, F_FF),
    }

    loss, dx, big_g, small_g = _local_step(x[0], positions[0], loss_target[0], wf, small)

    g_own = _pack_owner_major(big_g, sizes)
    ra = _pair_exchange(g_own)
    rb = _chip_exchange(_pair_sum(ids, g_own, ra))
    g_slab = _pair_gather(_chip_sum(ids, g_own, ra, rb))
    g_slab = g_slab.reshape(-1, SLAB_COLS)

    order = SMALL_REPL + SMALL_SHARDED
    flat = jnp.concatenate([small_g[n].reshape(-1) for n in order] + [loss.reshape(-1)])
    flat = jnp.pad(flat, (0, SMALL_ROWS * LANES - flat.shape[0])).reshape(SMALL_ROWS, LANES)
    red = _all_reduce_small(flat, "ar_small").reshape(-1)
    sg, off = {}, 0
    for n in order:
        sz = small_g[n].size
        sg[n] = red[off:off + sz].reshape(small_g[n].shape)
        off += sz
    loss_out = red[off]
    grads = {n: sg[n].reshape(w[n].shape) for n in SMALL_REPL}
    grads["sc_conv_w"] = lax.dynamic_slice_in_dim(sg["sc_conv_w"], chip * (D // N_CHIPS), D // N_CHIPS, axis=1)[None]
    grads["ffn_conv_w"] = lax.dynamic_slice_in_dim(sg["ffn_conv_w"], chip * (F_FF // N_CHIPS), F_FF // N_CHIPS, axis=2)

    d_slab, nm_slab, nv_slab = _adamw(_pack_local(w, F32), g_slab, _pack_local(m, F32), _pack_local(v, F32), "adamw_big")
    grads.update(_unpack_local(g_slab, w))
    delta, new_m, new_v = _unpack_local(d_slab, w), _unpack_local(nm_slab, w), _unpack_local(nv_slab, w)

    small_names = SMALL_REPL + SMALL_SHARDED

    def pack_small(tree):
        f = jnp.concatenate([tree[n].reshape(-1) for n in small_names])
        return f.reshape(-1, LANES)

    ds, nms, nvs = _adamw(pack_small(w), pack_small(grads), pack_small(m), pack_small(v), "adamw_small")
    for slab, dst in ((ds, delta), (nms, new_m), (nvs, new_v)):
        f, off = slab.reshape(-1), 0
        for n in small_names:
            dst[n] = f[off:off + w[n].size].reshape(w[n].shape)
            off += w[n].size

    return (loss_out, dx[None], *[grads[n] for n in names], *[delta[n] for n in names],
            *[new_m[n] for n in names], *[new_v[n] for n in names])
```

```python
from typing import NamedTuple

import jax
import jax.numpy as jnp
from jax import lax
from jax.experimental import pallas as pl
from jax.experimental.pallas import tpu as pltpu

F32 = jnp.float32
BF16 = jnp.bfloat16

T = 2048
D = 1024
F_FF = 2816
N_HEADS = 8
QK_NOPE = 128
QK_ROPE = 64
V_HEAD = 128
Q_LORA = 384
KV_LORA = 256
CHUNK_SHIFT = 6
ROPE_THETA = 10000.0
EPS = 1e-6
NEG_INF = -1e30
HEAD_PAD = 256
KVP = KV_LORA + 128

ADAM_LR = 0.001
ADAM_B1 = 0.9
ADAM_B2 = 0.999
ADAM_EPS = 1e-08
ADAM_WD = 0.01
ADAM_STEP = 10

N_CHIPS = 4
N_DEV = 8
LANES = 128
TC = 256
V7X_VMEM_LIMIT = 56 * 1024 * 1024

MESH = pl.DeviceIdType.MESH
ANY = pl.BlockSpec(memory_space=pl.ANY)


class _W(NamedTuple):
    name: str
    kind: str
    nl: int
    k: int
    n: int


BIGW = (
    _W("sc_w_in", "col", 1, D, 3 * D // N_CHIPS), _W("sc_w_out", "row", 1, D // N_CHIPS, D),
    _W("ffn_w_up", "col", 2, D, 2 * F_FF // N_CHIPS), _W("ffn_w_down", "row", 2, F_FF // N_CHIPS, D),
    _W("w_kv", "row", 1, D // N_CHIPS, KVP), _W("w_ukv", "col", 2, KV_LORA, N_HEADS * QK_NOPE // N_CHIPS),
    _W("w_dq", "row", 1, D // N_CHIPS, Q_LORA), _W("w_uq", "col", 1, Q_LORA, N_HEADS * (QK_NOPE + QK_ROPE) // N_CHIPS),
    _W("w_o", "row", 1, N_HEADS * V_HEAD // N_CHIPS, D),
)


def _cp(*sem):
    return pltpu.CompilerParams(dimension_semantics=sem, vmem_limit_bytes=V7X_VMEM_LIMIT)


def _tile(n, cands):
    for c in cands:
        if n % c == 0:
            return c
    raise ValueError(f"no tile for {n}")


NN_DIMS = (((1,), (0,)), ((), ()))
NT_DIMS = (((1,), (1,)), ((), ()))
TN_DIMS = (((0,), (0,)), ((), ()))
M_TILES = (1024, 512, 384, 256, 128)
N_TILES = (512, 384, 256, 128)


def _mm(name, a, b, dims, grid, a_spec, b_spec, o_spec, o_sds, add=None, red=None, acc_shape=None):
    n_red = None if red is None else grid[red]

    def body(*refs):
        a_ref, b_ref = refs[0], refs[1]
        add_ref = refs[2] if add is not None else None
        o_ref = refs[3] if add is not None else refs[2]
        part = lax.dot_general(a_ref[...].astype(BF16), b_ref[...].astype(BF16), dims, preferred_element_type=F32)
        if red is None:
            if add is not None:
                part = part + add_ref[...]
            o_ref[...] = part.astype(o_ref.dtype)
            return
        acc_ref = refs[-1]
        r = pl.program_id(red)

        @pl.when(r == 0)
        def _():
            acc_ref[...] = part

        @pl.when(r > 0)
        def _():
            acc_ref[...] += part

        @pl.when(r == n_red - 1)
        def _():
            o_ref[...] = acc_ref[...].astype(o_ref.dtype)

    sem = tuple("arbitrary" if ax == red else "parallel" for ax in range(len(grid)))
    in_specs = [a_spec, b_spec] + ([o_spec] if add is not None else [])
    args = (a, b) + ((add,) if add is not None else ())
    return pl.pallas_call(
        body, name=name, grid=grid, in_specs=in_specs, out_specs=o_spec, out_shape=o_sds,
        scratch_shapes=[] if red is None else [pltpu.VMEM(acc_shape, F32)], compiler_params=_cp(*sem),
    )(*args)


def _nn(name, a, b, out_dtype, add=None, lead=None):
    (m, k), n = a.shape, b.shape[-1]
    tm, tn = _tile(m, M_TILES), _tile(n, N_TILES)
    if lead is None:
        b_spec = pl.BlockSpec((k, tn), lambda i, j: (0, j))
    else:
        b_spec = pl.BlockSpec((None, k, tn), lambda i, j: (lead, 0, j))
    return _mm(name, a, b, NN_DIMS, (m // tm, n // tn), pl.BlockSpec((tm, k), lambda i, j: (i, 0)), b_spec,
               pl.BlockSpec((tm, tn), lambda i, j: (i, j)), jax.ShapeDtypeStruct((m, n), out_dtype), add=add)


def _nn_parts(name, a, b, parts, out_dtype, lead=None, stacked=False):
    m, k = a.shape
    c = b.shape[-1] if stacked else b.shape[-1] // parts
    tm, tn = _tile(m, M_TILES), _tile(c, N_TILES)
    nb = c // tn
    if stacked:
        b_spec = pl.BlockSpec((None, k, tn), lambda i, p, j: (p, 0, j))
    elif lead is None:
        b_spec = pl.BlockSpec((k, tn), lambda i, p, j: (0, p * nb + j))
    else:
        b_spec = pl.BlockSpec((None, k, tn), lambda i, p, j: (lead, 0, p * nb + j))
    return _mm(name, a, b, NN_DIMS, (m // tm, parts, nb), pl.BlockSpec((tm, k), lambda i, p, j: (i, 0)), b_spec,
               pl.BlockSpec((None, tm, tn), lambda i, p, j: (p, i, j)), jax.ShapeDtypeStruct((parts, m, c), out_dtype))


def _nt(name, a, b, out_dtype, lead=None):
    (m, k), n = a.shape, b.shape[-2]
    tm, tn = _tile(m, M_TILES), _tile(n, N_TILES)
    if lead is None:
        b_spec = pl.BlockSpec((tn, k), lambda i, j: (j, 0))
    else:
        b_spec = pl.BlockSpec((None, tn, k), lambda i, j: (lead, j, 0))
    return _mm(name, a, b, NT_DIMS, (m // tm, n // tn), pl.BlockSpec((tm, k), lambda i, j: (i, 0)), b_spec,
               pl.BlockSpec((tm, tn), lambda i, j: (i, j)), jax.ShapeDtypeStruct((m, n), out_dtype))


def _nt_parts(name, a, b, out_dtype, lead=None, stacked=False):
    parts, m, c = a.shape
    n = b.shape[-2]
    tm, tn = _tile(m, M_TILES), _tile(n, N_TILES)
    if stacked:
        b_spec = pl.BlockSpec((None, tn, c), lambda i, j, p: (p, j, 0))
    elif lead is None:
        b_spec = pl.BlockSpec((tn, c), lambda i, j, p: (j, p))
    else:
        b_spec = pl.BlockSpec((None, tn, c), lambda i, j, p: (lead, j, p))
    return _mm(name, a, b, NT_DIMS, (m // tm, n // tn, parts), pl.BlockSpec((None, tm, c), lambda i, j, p: (p, i, 0)),
               b_spec, pl.BlockSpec((tm, tn), lambda i, j, p: (i, j)), jax.ShapeDtypeStruct((m, n), out_dtype),
               red=2, acc_shape=(tm, tn))


def _tn(name, a, b, out_dtype):
    (k, m), n = a.shape, b.shape[1]
    tm, tn = _tile(m, M_TILES), _tile(n, N_TILES)
    return _mm(name, a, b, TN_DIMS, (m // tm, n // tn), pl.BlockSpec((k, tm), lambda i, j: (0, i)),
               pl.BlockSpec((k, tn), lambda i, j: (0, j)), pl.BlockSpec((tm, tn), lambda i, j: (i, j)),
               jax.ShapeDtypeStruct((m, n), out_dtype))


def _dw_sc_in(hn, dz):
    t, tn, tm = hn.shape[0], TC, 512
    per_part, per_chip = D // tn, 3 * D // N_CHIPS // tn
    return _mm("sc_in_dw", hn, dz, TN_DIMS, (D // tm, 3 * D // tn), pl.BlockSpec((t, tm), lambda i, j: (0, i)),
               pl.BlockSpec((None, t, tn), lambda i, j: (j // per_part, 0, j % per_part)),
               pl.BlockSpec((None, tm, tn), lambda i, j: (j // per_chip, i, j % per_chip)),
               jax.ShapeDtypeStruct((N_CHIPS, D, 3 * D // N_CHIPS), BF16))


def _dw_ffn_up(name, hf, dup):
    t, tm, ns = hf.shape[0], 512, 2 * F_FF // N_CHIPS
    return _mm(name, hf, dup, TN_DIMS, (N_CHIPS, D // tm), pl.BlockSpec((t, tm), lambda s, i: (0, i)),
               pl.BlockSpec((None, t, ns), lambda s, i: (s // 2, 0, s % 2)),
               pl.BlockSpec((None, tm, ns), lambda s, i: (s, i, 0)), jax.ShapeDtypeStruct((N_CHIPS, D, ns), BF16))


def _dw_ukv(ckv, dknv):
    t, ns = ckv.shape[0], N_HEADS * QK_NOPE // N_CHIPS
    return _mm("kv_up_dw", ckv, dknv, TN_DIMS, (2, N_CHIPS), pl.BlockSpec((t, KV_LORA), lambda p, s: (0, 0)),
               pl.BlockSpec((None, t, ns), lambda p, s: (p, 0, s)),
               pl.BlockSpec((None, None, KV_LORA, ns), lambda p, s: (s, p, 0, 0)),
               jax.ShapeDtypeStruct((N_CHIPS, 2, KV_LORA, ns), BF16))


def _rms_fwd(x, g, name):
    t, d = x.shape
    tr = 512

    def body(x_ref, g_ref, o_ref):
        xv = x_ref[...]
        r = lax.rsqrt(jnp.mean(xv * xv, axis=1, keepdims=True) + EPS)
        o_ref[...] = (xv * r * g_ref[...]).astype(o_ref.dtype)

    row = pl.BlockSpec((tr, d), lambda i: (i, 0))
    return pl.pallas_call(
        body, name=name, grid=(t // tr,), in_specs=[row, pl.BlockSpec((1, d), lambda i: (0, 0))],
        out_specs=row, out_shape=jax.ShapeDtypeStruct((t, d), BF16), compiler_params=_cp("parallel"),
    )(x, g)


def _rms_bwd_math(xv, g, dy):
    r = lax.rsqrt(jnp.mean(xv * xv, axis=1, keepdims=True) + EPS)
    xh = xv * r
    gy = dy * g
    dx = r * (gy - xh * jnp.mean(gy * xh, axis=1, keepdims=True))
    dg = jnp.sum(dy * xh, axis=0, keepdims=True)
    return dx, dg


def _rms_bwd(x, g, dy, add, name):
    t, d = x.shape
    tr = 512

    def body(*refs):
        if add is None:
            x_ref, g_ref, dy_ref, dx_ref, dg_ref = refs
        else:
            x_ref, g_ref, dy_ref, add_ref, dx_ref, dg_ref = refs
        dx, dg = _rms_bwd_math(x_ref[...], g_ref[...], dy_ref[...].astype(F32))
        if add is not None:
            dx = dx + add_ref[...]
        dx_ref[...] = dx

        @pl.when(pl.program_id(0) == 0)
        def _():
            dg_ref[...] = jnp.zeros_like(dg_ref)

        dg_ref[...] += dg

    row = pl.BlockSpec((tr, d), lambda i: (i, 0))
    vec = pl.BlockSpec((1, d), lambda i: (0, 0))
    in_specs = [row, vec, row] + ([row] if add is not None else [])
    args = (x, g, dy) + ((add,) if add is not None else ())
    return pl.pallas_call(
        body, name=name, grid=(t // tr,), in_specs=in_specs, out_specs=[row, vec],
        out_shape=[jax.ShapeDtypeStruct((t, d), F32), jax.ShapeDtypeStruct((1, d), F32)],
        compiler_params=_cp("arbitrary"),
    )(*args)


def _loss_head(h, g, tgt):
    t, d = h.shape
    tr = 512

    def body(h_ref, g_ref, t_ref, loss_ref, dh_ref, dg_ref):
        xv = h_ref[...]
        gv = g_ref[...]
        r = lax.rsqrt(jnp.mean(xv * xv, axis=1, keepdims=True) + EPS)
        err = xv * r * gv - t_ref[...]
        part = 0.5 * jnp.sum(jnp.mean(err * err, axis=1, keepdims=True), axis=0, keepdims=True)
        dx, dg = _rms_bwd_math(xv, gv, err * (1.0 / d))
        dh_ref[...] = dx

        @pl.when(pl.program_id(0) == 0)
        def _():
            dg_ref[...] = jnp.zeros_like(dg_ref)
            loss_ref[...] = jnp.zeros_like(loss_ref)

        dg_ref[...] += dg
        loss_ref[...] += jnp.broadcast_to(part, loss_ref.shape)

    row = pl.BlockSpec((tr, d), lambda i: (i, 0))
    vec = pl.BlockSpec((1, d), lambda i: (0, 0))
    lspec = pl.BlockSpec((1, LANES), lambda i: (0, 0))
    return pl.pallas_call(
        body, name="loss_head", grid=(t // tr,), in_specs=[row, vec, row], out_specs=[lspec, row, vec],
        out_shape=[jax.ShapeDtypeStruct((1, LANES), F32), jax.ShapeDtypeStruct((t, d), F32),
                   jax.ShapeDtypeStruct((1, d), F32)],
        compiler_params=_cp("arbitrary"),
    )(h, g, tgt)


def _rot_half(x):
    lane = lax.broadcasted_iota(jnp.int32, x.shape, 1)
    return jnp.where((lane % QK_ROPE) < QK_ROPE // 2, -pltpu.roll(x, LANES - 32, axis=1),
                     pltpu.roll(x, 32, axis=1))


def _rope_fwd_math(x, cos, sin):
    return x * cos + _rot_half(x) * sin


def _rope_bwd_math(dy, cos, sin):
    return dy * cos - _rot_half(dy * sin)


def _q_rope_fwd(qpre, cos, sin):
    t, w = qpre.shape
    tr = 256

    def body(q_ref, c_ref, s_ref, o_ref):
        cv, sv = c_ref[...], s_ref[...]
        for h in range(N_HEADS):
            lo = h * HEAD_PAD
            o_ref[:, lo:lo + QK_NOPE] = q_ref[:, lo:lo + QK_NOPE].astype(BF16)
            o_ref[:, lo + QK_NOPE:lo + HEAD_PAD] = _rope_fwd_math(
                q_ref[:, lo + QK_NOPE:lo + HEAD_PAD], cv, sv).astype(BF16)

    row = pl.BlockSpec((tr, w), lambda i: (i, 0))
    tab = pl.BlockSpec((tr, LANES), lambda i: (i, 0))
    return pl.pallas_call(
        body, name="q_rope_fwd", grid=(t // tr,), in_specs=[row, tab, tab], out_specs=row,
        out_shape=jax.ShapeDtypeStruct((t, w), BF16), compiler_params=_cp("parallel"),
    )(qpre, cos, sin)


def _kv_elem_fwd(kvpre, g, cos, sin):
    t = kvpre.shape[0]
    tr = 512

    def body(p_ref, g_ref, c_ref, s_ref, ckv_ref, kr_ref):
        lat = p_ref[:, :KV_LORA]
        r = lax.rsqrt(jnp.mean(lat * lat, axis=1, keepdims=True) + EPS)
        ckv_ref[...] = (lat * r * g_ref[...]).astype(BF16)
        kr_ref[...] = _rope_fwd_math(p_ref[:, KV_LORA:], c_ref[...], s_ref[...]).astype(BF16)

    tab = pl.BlockSpec((tr, LANES), lambda i: (i, 0))
    return pl.pallas_call(
        body, name="kv_elem_fwd", grid=(t // tr,),
        in_specs=[pl.BlockSpec((tr, KVP), lambda i: (i, 0)), pl.BlockSpec((1, KV_LORA), lambda i: (0, 0)), tab, tab],
        out_specs=[pl.BlockSpec((tr, KV_LORA), lambda i: (i, 0)), tab],
        out_shape=[jax.ShapeDtypeStruct((t, KV_LORA), BF16), jax.ShapeDtypeStruct((t, LANES), BF16)],
        compiler_params=_cp("parallel"),
    )(kvpre, g, cos, sin)


def _kv_elem_bwd(kvpre, g, dckv, dkr, cos, sin):
    t = kvpre.shape[0]
    tr = 512

    def body(p_ref, g_ref, dc_ref, dk_ref, c_ref, s_ref, dp_ref, dg_ref):
        dlat, dg = _rms_bwd_math(p_ref[:, :KV_LORA], g_ref[...], dc_ref[...])
        dp_ref[:, :KV_LORA] = dlat.astype(BF16)
        dp_ref[:, KV_LORA:] = _rope_bwd_math(dk_ref[...], c_ref[...], s_ref[...]).astype(BF16)

        @pl.when(pl.program_id(0) == 0)
        def _():
            dg_ref[...] = jnp.zeros_like(dg_ref)

        dg_ref[...] += dg

    tab = pl.BlockSpec((tr, LANES), lambda i: (i, 0))
    pre = pl.BlockSpec((tr, KVP), lambda i: (i, 0))
    vec = pl.BlockSpec((1, KV_LORA), lambda i: (0, 0))
    return pl.pallas_call(
        body, name="kv_elem_bwd", grid=(t // tr,),
        in_specs=[pre, vec, pl.BlockSpec((tr, KV_LORA), lambda i: (i, 0)), tab, tab, tab],
        out_specs=[pre, vec],
        out_shape=[jax.ShapeDtypeStruct((t, KVP), BF16), jax.ShapeDtypeStruct((1, KV_LORA), F32)],
        compiler_params=_cp("arbitrary"),
    )(kvpre, g, dckv, dkr, cos, sin)


def _shift_down(x, k):
    row = lax.broadcasted_iota(jnp.int32, x.shape, 0)
    return jnp.where(row >= k, pltpu.roll(x, k, axis=0), 0.0)


def _shift_up(x, k):
    n = x.shape[0]
    row = lax.broadcasted_iota(jnp.int32, x.shape, 0)
    return jnp.where(row < n - k, pltpu.roll(x, n - k, axis=0), 0.0)


def _conv3(x, w_ref):
    return _shift_down(x, 2) * w_ref[0:1, :] + _shift_down(x, 1) * w_ref[1:2, :] + x * w_ref[2:3, :]


def _conv3_t(dy, w_ref):
    return dy * w_ref[2:3, :] + _shift_up(dy, 1) * w_ref[1:2, :] + _shift_up(dy, 2) * w_ref[0:1, :]


def _conv3_dw(dy, x, dw_ref):
    dw_ref[0:1, :] = jnp.sum(dy * _shift_down(x, 2), axis=0, keepdims=True)
    dw_ref[1:2, :] = jnp.sum(dy * _shift_down(x, 1), axis=0, keepdims=True)
    dw_ref[2:3, :] = jnp.sum(dy * x, axis=0, keepdims=True)


def _col(parts, t):
    if parts is None:
        return pl.BlockSpec((t, TC), lambda j: (0, j))
    return pl.BlockSpec((parts, t, TC), lambda j: (0, 0, j))


def _scmix_fwd(z, w):
    t = z.shape[1]

    def body(z_ref, w_ref, m_ref):
        m_ref[...] = (z_ref[0] * _conv3(z_ref[1] * z_ref[2], w_ref)).astype(BF16)

    return pl.pallas_call(
        body, name="scmix_fwd", grid=(D // TC,), in_specs=[_col(3, t), pl.BlockSpec((3, TC), lambda j: (0, j))],
        out_specs=_col(None, t), out_shape=jax.ShapeDtypeStruct((t, D), BF16), compiler_params=_cp("parallel"),
    )(z, w)


def _scmix_bwd(z, w, dm):
    t = z.shape[1]

    def body(z_ref, w_ref, dm_ref, dz_ref, dw_ref):
        c, u = z_ref[1], z_ref[2]
        cu = c * u
        dmv = dm_ref[...].astype(F32)
        dz_ref[0] = (dmv * _conv3(cu, w_ref)).astype(BF16)
        dcv = dmv * z_ref[0]
        _conv3_dw(dcv, cu, dw_ref)
        dcu = _conv3_t(dcv, w_ref)
        dz_ref[1] = (dcu * u).astype(BF16)
        dz_ref[2] = (dcu * c).astype(BF16)

    wspec = pl.BlockSpec((3, TC), lambda j: (0, j))
    return pl.pallas_call(
        body, name="scmix_bwd", grid=(D // TC,), in_specs=[_col(3, t), wspec, _col(None, t)],
        out_specs=[_col(3, t), wspec],
        out_shape=[jax.ShapeDtypeStruct((3, t, D), BF16), jax.ShapeDtypeStruct((3, D), F32)],
        compiler_params=_cp("parallel"),
    )(z, w, dm)


def _gate_fwd(up, w, bias, name):
    t = up.shape[1]

    def body(u_ref, w_ref, b_ref, a_ref):
        gc = _conv3(u_ref[0], w_ref) + b_ref[...]
        a_ref[...] = (gc * jax.nn.sigmoid(gc) * u_ref[1]).astype(BF16)

    return pl.pallas_call(
        body, name=name, grid=(F_FF // TC,),
        in_specs=[_col(2, t), pl.BlockSpec((3, TC), lambda j: (0, j)), pl.BlockSpec((1, TC), lambda j: (0, j))],
        out_specs=_col(None, t), out_shape=jax.ShapeDtypeStruct((t, F_FF), BF16), compiler_params=_cp("parallel"),
    )(up, w, bias)


def _gate_bwd(up, w, bias, da, name):
    t = up.shape[1]

    def body(u_ref, w_ref, b_ref, da_ref, du_ref, dw_ref, db_ref):
        g = u_ref[0]
        gc = _conv3(g, w_ref) + b_ref[...]
        sg = jax.nn.sigmoid(gc)
        dav = da_ref[...].astype(F32)
        du_ref[1] = (dav * (gc * sg)).astype(BF16)
        dgc = dav * u_ref[1] * (sg * (1.0 + gc * (1.0 - sg)))
        db_ref[...] = jnp.sum(dgc, axis=0, keepdims=True)
        _conv3_dw(dgc, g, dw_ref)
        du_ref[0] = _conv3_t(dgc, w_ref).astype(BF16)

    wspec = pl.BlockSpec((3, TC), lambda j: (0, j))
    bspec = pl.BlockSpec((1, TC), lambda j: (0, j))
    return pl.pallas_call(
        body, name=name, grid=(F_FF // TC,), in_specs=[_col(2, t), wspec, bspec, _col(None, t)],
        out_specs=[_col(2, t), wspec, bspec],
        out_shape=[jax.ShapeDtypeStruct((2, t, F_FF), BF16), jax.ShapeDtypeStruct((3, F_FF), F32),
                   jax.ShapeDtypeStruct((1, F_FF), F32)],
        compiler_params=_cp("parallel"),
    )(up, w, bias, da)


ATT_TQ = 256
ATT_SCALE = (QK_NOPE + QK_ROPE) ** -0.5


def _attn_probs(q, kn, kr, qi):
    s = lax.dot_general(q[:, :QK_NOPE], kn, NT_DIMS, preferred_element_type=F32)
    s = s + lax.dot_general(q[:, QK_NOPE:], kr, NT_DIMS, preferred_element_type=F32)
    s = s * ATT_SCALE
    row = qi * ATT_TQ + lax.broadcasted_iota(jnp.int32, s.shape, 0)
    col = lax.broadcasted_iota(jnp.int32, s.shape, 1)
    s = jnp.where(lax.shift_right_logical(col, CHUNK_SHIFT) <= lax.shift_right_logical(row, CHUNK_SHIFT), s, NEG_INF)
    p = jnp.exp(s - jnp.max(s, axis=1, keepdims=True))
    return p * (1.0 / jnp.sum(p, axis=1, keepdims=True))


def _attn_specs(t):
    q = pl.BlockSpec((ATT_TQ, HEAD_PAD), lambda h, i: (i, h))
    kn = pl.BlockSpec((None, t, QK_NOPE), lambda h, i: (0, 0, h))
    kr = pl.BlockSpec((t, LANES), lambda h, i: (0, 0))
    v = pl.BlockSpec((None, t, V_HEAD), lambda h, i: (1, 0, h))
    o = pl.BlockSpec((ATT_TQ, V_HEAD), lambda h, i: (i, h))
    return q, kn, kr, v, o


def _attn_fwd(q, knv, kr):
    t = q.shape[0]

    def body(q_ref, kn_ref, kr_ref, v_ref, o_ref):
        p = _attn_probs(q_ref[...], kn_ref[...], kr_ref[...], pl.program_id(1))
        o_ref[...] = jnp.dot(p.astype(BF16), v_ref[...], preferred_element_type=F32).astype(BF16)

    qs, kns, krs, vs, os_ = _attn_specs(t)
    return pl.pallas_call(
        body, name="attn_fwd", grid=(N_HEADS, t // ATT_TQ), in_specs=[qs, kns, krs, vs], out_specs=os_,
        out_shape=jax.ShapeDtypeStruct((t, N_HEADS * V_HEAD), BF16), compiler_params=_cp("parallel", "parallel"),
    )(q, knv, kr, knv)


def _attn_bwd(q, knv, kr, do, cos, sin):
    t = q.shape[0]

    def body(q_ref, kn_ref, kr_ref, v_ref, do_ref, c_ref, s_ref, dq_ref, dknv_ref, dkr_ref):
        h, qi = pl.program_id(0), pl.program_id(1)
        qv, knv_, krv, dov = q_ref[...], kn_ref[...], kr_ref[...], do_ref[...]
        p = _attn_probs(qv, knv_, krv, qi)
        dp = lax.dot_general(dov, v_ref[...], NT_DIMS, preferred_element_type=F32)
        ds = (p * (dp - jnp.sum(p * dp, axis=1, keepdims=True)) * ATT_SCALE).astype(BF16)
        dq_ref[:, :QK_NOPE] = jnp.dot(ds, knv_, preferred_element_type=F32).astype(BF16)
        dqr = jnp.dot(ds, krv, preferred_element_type=F32)
        dq_ref[:, QK_NOPE:] = _rope_bwd_math(dqr, c_ref[...], s_ref[...]).astype(BF16)
        dv = lax.dot_general(p.astype(BF16), dov, TN_DIMS, preferred_element_type=F32)
        dkn = lax.dot_general(ds, qv[:, :QK_NOPE], TN_DIMS, preferred_element_type=F32)
        dkr = lax.dot_general(ds, qv[:, QK_NOPE:], TN_DIMS, preferred_element_type=F32)

        @pl.when(qi == 0)
        def _():
            dknv_ref[...] = jnp.zeros_like(dknv_ref)

        @pl.when((qi == 0) & (h == 0))
        def _():
            dkr_ref[...] = jnp.zeros_like(dkr_ref)

        dknv_ref[0] += dkn
        dknv_ref[1] += dv
        dkr_ref[...] += dkr

    qs, kns, krs, vs, os_ = _attn_specs(t)
    tab = pl.BlockSpec((ATT_TQ, LANES), lambda h, i: (i, 0))
    return pl.pallas_call(
        body, name="attn_bwd", grid=(N_HEADS, t // ATT_TQ), in_specs=[qs, kns, krs, vs, os_, tab, tab],
        out_specs=[qs, pl.BlockSpec((2, t, QK_NOPE), lambda h, i: (0, 0, h)), krs],
        out_shape=[jax.ShapeDtypeStruct((t, N_HEADS * HEAD_PAD), BF16),
                   jax.ShapeDtypeStruct((2, t, N_HEADS * QK_NOPE), F32), jax.ShapeDtypeStruct((t, LANES), F32)],
        compiler_params=_cp("arbitrary", "arbitrary"),
    )(q, knv, kr, knv, do, cos, sin)


def _adam_math(w, g, m, v):
    nm = ADAM_B1 * m + (1.0 - ADAM_B1) * g
    nv = ADAM_B2 * v + (1.0 - ADAM_B2) * (g * g)
    m_hat = nm / (1.0 - ADAM_B1 ** ADAM_STEP)
    v_hat = nv / (1.0 - ADAM_B2 ** ADAM_STEP)
    return -ADAM_LR * (m_hat / (jnp.sqrt(v_hat) + ADAM_EPS) + ADAM_WD * w), nm, nv


def _adamw_small(w, g, m, v):
    def body(w_ref, g_ref, m_ref, v_ref, d_ref, nm_ref, nv_ref):
        d_ref[...], nm_ref[...], nv_ref[...] = _adam_math(w_ref[...], g_ref[...], m_ref[...], v_ref[...])

    shp = jax.ShapeDtypeStruct(w.shape, F32)
    return pl.pallas_call(body, name="adamw_small", out_shape=[shp] * 3)(w, g, m, v)


ADAM_BLOCK_BYTES = 1 << 20


def _adamw_shard(ids, w, m, v, g_mine, g_sib, name, layer=None, prev=None):
    r, c = w.shape[-2:]
    half = r // 2
    tr = _tile(half, [d for d in range(half, 7, -8) if d * c * 4 <= ADAM_BLOCK_BYTES] or [8])
    nbh = half // tr

    def body(ids_ref, w_ref, m_ref, v_ref, gm_ref, gs_ref, *rest):
        g_ref, d_ref, nm_ref, nv_ref = rest[-4:]
        mine = (pl.program_id(0) // nbh) == ids_ref[0]

        @pl.when(mine)
        def _():
            g_ref[...] = gm_ref[...]

        @pl.when(jnp.logical_not(mine))
        def _():
            g_ref[...] = gs_ref[...]

        d_ref[...], nm_ref[...], nv_ref[...] = _adam_math(w_ref[...], g_ref[...], m_ref[...], v_ref[...])

    if layer is None:
        wspec = pl.BlockSpec((tr, c), lambda i, ids: (i, 0))
    else:
        wspec = pl.BlockSpec((None, tr, c), lambda i, ids: (layer, i, 0))
    gspec = pl.BlockSpec((tr, c), lambda i, ids: (i % nbh, 0))
    in_specs = [wspec] * 3 + [gspec] * 2
    args = [ids, w, m, v, g_mine, g_sib]
    aliases = {}
    if prev is not None:
        in_specs += [ANY] * 4
        args += list(prev)
        aliases = {6 + k: k for k in range(4)}
    grid_spec = pltpu.PrefetchScalarGridSpec(num_scalar_prefetch=1, grid=(r // tr,), in_specs=in_specs,
                                             out_specs=[wspec] * 4)
    return pl.pallas_call(
        body, name=name, grid_spec=grid_spec, out_shape=[jax.ShapeDtypeStruct(w.shape, F32)] * 4,
        input_output_aliases=aliases, compiler_params=_cp("parallel"),
    )(*args)


def _peer_chip(k_me, j):
    return k_me ^ jnp.where(j == 0, 2, jnp.where(j == 1, 1, 3))


def _pair_sum(ids, g, ra, name):
    _, r, c = g.shape
    half = r // 2

    def body(ids_ref, g_ref, ra_ref, o_ref):
        o_ref[...] = (g_ref[...].astype(F32) + ra_ref[...].astype(F32)).astype(BF16)

    grid_spec = pltpu.PrefetchScalarGridSpec(
        num_scalar_prefetch=1, grid=(3,),
        in_specs=[pl.BlockSpec((None, half, c), lambda j, ids: (_peer_chip(ids[1], j), ids[0], 0)),
                  pl.BlockSpec((None, half, c), lambda j, ids: (_peer_chip(ids[1], j), 0, 0))],
        out_specs=pl.BlockSpec((None, half, c), lambda j, ids: (j, 0, 0)))
    return pl.pallas_call(
        body, name=name, grid_spec=grid_spec, out_shape=jax.ShapeDtypeStruct((3, half, c), BF16),
        compiler_params=_cp("parallel"),
    )(ids, g, ra)


def _chip_sum(ids, g, ra, rb, name):
    _, r, c = g.shape
    half = r // 2

    def body(ids_ref, g_ref, ra_ref, rb_ref, o_ref):
        acc = g_ref[...].astype(F32) + ra_ref[...].astype(F32)
        for j in range(3):
            acc = acc + rb_ref[j].astype(F32)
        o_ref[...] = acc

    grid_spec = pltpu.PrefetchScalarGridSpec(
        num_scalar_prefetch=1, grid=(1,),
        in_specs=[pl.BlockSpec((None, half, c), lambda i, ids: (ids[1], ids[0], 0)),
                  pl.BlockSpec((None, half, c), lambda i, ids: (ids[1], 0, 0)),
                  pl.BlockSpec((3, half, c), lambda i, ids: (0, 0, 0))],
        out_specs=pl.BlockSpec((half, c), lambda i, ids: (0, 0)))
    return pl.pallas_call(
        body, name=name, grid_spec=grid_spec, out_shape=jax.ShapeDtypeStruct((half, c), F32),
        compiler_params=_cp("arbitrary"),
    )(ids, g, ra, rb)


def _position():
    x, y, c = lax.axis_index("x"), lax.axis_index("y"), lax.axis_index("c")
    chips = [(1 - x, y), (x, 1 - y), (1 - x, 1 - y)]
    return x, y, c, chips


def _shard_half(ref, wm, h):
    if wm.nl == 2:
        return ref.at[h]
    return ref.at[pl.ds(pl.multiple_of(h * (wm.k // 2), 16), wm.k // 2), :]


def _region(full, wm, s, h):
    cols = pl.ds(pl.multiple_of(s * wm.n, LANES), wm.n) if wm.kind == "col" else slice(None)
    if wm.nl == 2:
        rows = pl.ds(pl.multiple_of(s * wm.k, 16), wm.k) if wm.kind == "row" else slice(None)
        return full.at[slice(None) if h is None else h, rows, cols]
    if wm.kind == "col":
        rows = slice(None) if h is None else pl.ds(pl.multiple_of(h * (wm.k // 2), 16), wm.k // 2)
    elif h is None:
        rows = pl.ds(pl.multiple_of(s * wm.k, 16), wm.k)
    else:
        rows = pl.ds(pl.multiple_of(s * wm.k + h * (wm.k // 2), 16), wm.k // 2)
    return full.at[rows, cols]


def _full_shape(wm):
    shape = (wm.k, N_CHIPS * wm.n) if wm.kind == "col" else (N_CHIPS * wm.k, wm.n)
    return shape if wm.nl == 1 else (wm.nl,) + shape


def _all_gather_weights(shards):
    nw = len(BIGW)

    def body(*refs):
        sh, full = refs[:nw], refs[nw:2 * nw]
        ici_s, ici_r, pass_s, pass_r, own_s, own_r = refs[2 * nw:]
        x, y, c, chips = _position()
        me, sibling = 2 * x + y, (x, y, 1 - c)

        def rcopy(src, dst, s_sem, r_sem, to):
            return pltpu.make_async_remote_copy(src_ref=src, dst_ref=dst, send_sem=s_sem, recv_sem=r_sem,
                                                device_id=to, device_id_type=MESH)

        started = []
        for i, wm in enumerate(BIGW):
            for j, chip in enumerate(chips):
                started.append(rcopy(_shard_half(sh[i], wm, c), _region(full[i], wm, me, c),
                                     ici_s.at[i, j], ici_r.at[i, j], (*chip, c)))
                started[-1].start()
            started.append(rcopy(sh[i], _region(full[i], wm, me, None), own_s.at[i], own_r.at[i], sibling))
            started[-1].start()
        for i, wm in enumerate(BIGW):
            for j, chip in enumerate(chips):
                got = _region(full[i], wm, 2 * chip[0] + chip[1], c)
                rcopy(got, got, ici_s.at[i, j], ici_r.at[i, j], sibling).wait_recv()
                started.append(rcopy(got, got, pass_s.at[i, j], pass_r.at[i, j], sibling))
                started[-1].start()
        for i, wm in enumerate(BIGW):
            mine = _region(full[i], wm, me, None)
            rcopy(mine, mine, own_s.at[i], own_r.at[i], sibling).wait_recv()
            for j, chip in enumerate(chips):
                got = _region(full[i], wm, 2 * chip[0] + chip[1], 1 - c)
                rcopy(got, got, pass_s.at[i, j], pass_r.at[i, j], sibling).wait_recv()
        for cp in started:
            cp.wait_send()

    return pl.pallas_call(
        body, name="ag_weights", in_specs=[ANY] * nw, out_specs=[ANY] * nw,
        out_shape=[jax.ShapeDtypeStruct(_full_shape(wm), BF16) for wm in BIGW],
        scratch_shapes=[pltpu.SemaphoreType.DMA((nw, 3))] * 4 + [pltpu.SemaphoreType.DMA((nw,))] * 2,
    )(*shards)


def _pair_exchange(gs):
    n = len(gs)

    def body(*refs):
        g, out, send_sems, recv_sems = refs[:n], refs[n:2 * n], refs[2 * n], refs[2 * n + 1]
        x, y, c, _ = _position()
        cps = []
        for i in range(n):
            half = g[i].shape[1] // 2
            cps.append(pltpu.make_async_remote_copy(
                src_ref=g[i].at[:, pl.ds(pl.multiple_of((1 - c) * half, 16), half), :], dst_ref=out[i],
                send_sem=send_sems.at[i], recv_sem=recv_sems.at[i], device_id=(x, y, 1 - c), device_id_type=MESH))
            cps[-1].start()
        for cp in cps:
            cp.wait()

    return pl.pallas_call(
        body, name="rs_pair_exchange", in_specs=[ANY] * n, out_specs=[ANY] * n,
        out_shape=[jax.ShapeDtypeStruct((a.shape[0], a.shape[1] // 2, a.shape[2]), a.dtype) for a in gs],
        scratch_shapes=[pltpu.SemaphoreType.DMA((n,)), pltpu.SemaphoreType.DMA((n,))],
    )(*gs)


def _chip_exchange(ss):
    n = len(ss)

    def body(*refs):
        s, out, send_sems, recv_sems = refs[:n], refs[n:2 * n], refs[2 * n], refs[2 * n + 1]
        x, y, c, chips = _position()
        cps = []
        for i in range(n):
            for j, chip in enumerate(chips):
                cps.append(pltpu.make_async_remote_copy(
                    src_ref=s[i].at[j], dst_ref=out[i].at[j], send_sem=send_sems.at[i, j], recv_sem=recv_sems.at[i, j],
                    device_id=(*chip, c), device_id_type=MESH))
                cps[-1].start()
        for cp in cps:
            cp.wait()

    return pl.pallas_call(
        body, name="rs_chip_exchange", in_specs=[ANY] * n, out_specs=[ANY] * n,
        out_shape=[jax.ShapeDtypeStruct(a.shape, a.dtype) for a in ss],
        scratch_shapes=[pltpu.SemaphoreType.DMA((n, 3)), pltpu.SemaphoreType.DMA((n, 3))],
    )(*ss)


def _pair_swap(g8s):
    n = len(g8s)

    def body(*refs):
        g, out, send_sems, recv_sems = refs[:n], refs[n:2 * n], refs[2 * n], refs[2 * n + 1]
        x, y, c, _ = _position()
        cps = []
        for i in range(n):
            cps.append(pltpu.make_async_remote_copy(
                src_ref=g[i], dst_ref=out[i], send_sem=send_sems.at[i], recv_sem=recv_sems.at[i],
                device_id=(x, y, 1 - c), device_id_type=MESH))
            cps[-1].start()
        for cp in cps:
            cp.wait()

    return pl.pallas_call(
        body, name="rs_pair_swap", in_specs=[ANY] * n, out_specs=[ANY] * n,
        out_shape=[jax.ShapeDtypeStruct(a.shape, a.dtype) for a in g8s],
        scratch_shapes=[pltpu.SemaphoreType.DMA((n,)), pltpu.SemaphoreType.DMA((n,))],
    )(*g8s)


def _all_reduce_small(vec, name):
    r, cols = vec.shape

    def body(v_ref, o_ref, gath, send_sems, recv_sems):
        x, y, c, _ = _position()
        me = 4 * x + 2 * y + c
        gath[me] = v_ref[...]
        cps = []
        for rel in range(1, N_DEV):
            peer = (x ^ (rel >> 2), y ^ ((rel >> 1) & 1), c ^ (rel & 1))
            cps.append(pltpu.make_async_remote_copy(
                src_ref=v_ref, dst_ref=gath.at[me], send_sem=send_sems.at[rel - 1], recv_sem=recv_sems.at[rel - 1],
                device_id=peer, device_id_type=MESH))
        for cp in cps:
            cp.start()
        for rel in range(1, N_DEV):
            pltpu.make_async_remote_copy(
                src_ref=v_ref, dst_ref=gath.at[me ^ rel], send_sem=send_sems.at[rel - 1],
                recv_sem=recv_sems.at[rel - 1], device_id=(x, y, c), device_id_type=MESH).wait_recv()
        for cp in cps:
            cp.wait_send()
        acc = gath[0]
        for d in range(1, N_DEV):
            acc = acc + gath[d]
        o_ref[...] = acc

    vm = pl.BlockSpec(memory_space=pltpu.VMEM)
    return pl.pallas_call(
        body, name=name, in_specs=[vm], out_specs=vm, out_shape=jax.ShapeDtypeStruct((r, cols), F32),
        scratch_shapes=[pltpu.VMEM((N_DEV, r, cols), F32), pltpu.SemaphoreType.DMA((N_DEV - 1,)),
                        pltpu.SemaphoreType.DMA((N_DEV - 1,))],
    )(vec)


def _rope_tables(positions):
    half = QK_ROPE // 2
    inv_freq = 1.0 / (ROPE_THETA ** (jnp.arange(half, dtype=F32) / half))
    ang = positions.astype(F32)[:, None] * inv_freq
    zeros = jnp.zeros((positions.shape[0], LANES - QK_ROPE), F32)
    cos, sin = jnp.cos(ang), jnp.sin(ang)
    return jnp.concatenate([cos, cos, zeros], axis=1), jnp.concatenate([sin, sin, zeros], axis=1)


def _local_step(x, positions, tgt, wf, small):
    cos, sin = _rope_tables(positions)
    w_in, w_out, w_up, w_down = wf["sc_w_in"], wf["sc_w_out"], wf["ffn_w_up"], wf["ffn_w_down"]
    w_kv, w_ukv, w_dq, w_o = wf["w_kv"], wf["w_ukv"], wf["w_dq"], wf["w_o"]
    w_uq = jnp.pad(wf["w_uq"].reshape(Q_LORA, N_HEADS, QK_NOPE + QK_ROPE),
                   ((0, 0), (0, 0), (0, HEAD_PAD - QK_NOPE - QK_ROPE))).reshape(Q_LORA, N_HEADS * HEAD_PAD)
    attn_norm, ffn_norm = small["attn_norm"], small["ffn_norm"]
    conv_b = small["ffn_conv_b"]

    def ffn_fwd(h, l):
        hf = _rms_fwd(h, ffn_norm[l:l + 1], f"ffn{l}_norm")
        up = _nn_parts(f"ffn{l}_up", hf, w_up, 2, F32, lead=l)
        a = _gate_fwd(up, small["ffn_conv_w"][l], conv_b[l:l + 1], f"ffn{l}_gate")
        return _nn(f"ffn{l}_down", a, w_down, F32, add=h, lead=l), (hf, up, a)

    def ffn_bwd(h, dh_out, l, saved):
        hf, up, a = saved
        da = _nt(f"ffn{l}_down_dx", dh_out, w_down, BF16, lead=l)
        d_down = _tn(f"ffn{l}_down_dw", a, dh_out, BF16)
        dup, d_cw, d_cb = _gate_bwd(up, small["ffn_conv_w"][l], conv_b[l:l + 1], da, f"ffn{l}_gate_bwd")
        dhf = _nt_parts(f"ffn{l}_up_dx", dup, w_up, BF16, lead=l)
        d_up = _dw_ffn_up(f"ffn{l}_up_dw", hf, dup)
        dh, d_norm = _rms_bwd(h, ffn_norm[l:l + 1], dhf, dh_out, f"ffn{l}_norm_bwd")
        return dh, d_down.reshape(N_CHIPS, F_FF // N_CHIPS, D), d_up, d_cw, d_cb, d_norm

    hn0 = _rms_fwd(x, attn_norm[0:1], "attn0_norm")
    z = _nn_parts("sc_in", hn0, w_in, 3, F32)
    mix = _scmix_fwd(z, small["sc_conv_w"])
    h1 = _nn("sc_out", mix, w_out, F32, add=x)
    h2, ffn0_saved = ffn_fwd(h1, 0)

    hk = _rms_fwd(h2, small["kv_in_norm"], "kv_in_norm")
    kvpre = _nn("kv_down", hk, w_kv, F32)
    ckv, kr = _kv_elem_fwd(kvpre, small["kv_latent_norm"], cos, sin)
    knv = _nn_parts("kv_up", ckv, w_ukv, 2, BF16, stacked=True)

    hn1 = _rms_fwd(h2, attn_norm[1:2], "attn1_norm")
    cq_pre = _nn("q_down", hn1, w_dq, F32)
    cq = _rms_fwd(cq_pre, small["q_latent_norm"], "q_latent_norm")
    q = _q_rope_fwd(_nn("q_up", cq, w_uq, F32), cos, sin)
    o = _attn_fwd(q, knv, kr)
    h3 = _nn("attn_out", o, w_o, F32, add=h2)
    h4, ffn1_saved = ffn_fwd(h3, 1)

    loss, dh4, d_final = _loss_head(h4, small["final_norm"], tgt)

    dh3, d_down1, d_up1, d_cw1, d_cb1, d_fn1 = ffn_bwd(h3, dh4, 1, ffn1_saved)

    do = _nt("attn_out_dx", dh3, w_o, BF16)
    d_wo = _tn("attn_out_dw", o, dh3, BF16)
    dq, dknv, dkr = _attn_bwd(q, knv, kr, do, cos, sin)
    dcq = _nt("q_up_dx", dq, w_uq, F32)
    d_wuq = _tn("q_up_dw", cq, dq, BF16)
    d_wuq = d_wuq.reshape(Q_LORA, N_HEADS, HEAD_PAD)[:, :, :QK_NOPE + QK_ROPE]
    d_wuq = d_wuq.reshape(Q_LORA, N_CHIPS, -1).transpose(1, 0, 2)
    dcq_pre, d_qln = _rms_bwd(cq_pre, small["q_latent_norm"], dcq, None, "q_latent_norm_bwd")
    dhn1 = _nt("q_down_dx", dcq_pre, w_dq, BF16)
    d_wdq = _tn("q_down_dw", hn1, dcq_pre, BF16)
    dh2, d_an1 = _rms_bwd(h2, attn_norm[1:2], dhn1, dh3, "attn1_norm_bwd")

    dckv = _nt_parts("kv_up_dx", dknv, w_ukv, F32, stacked=True)
    d_wukv = _dw_ukv(ckv, dknv)
    dkvpre, d_kvln = _kv_elem_bwd(kvpre, small["kv_latent_norm"], dckv, dkr, cos, sin)
    dhk = _nt("kv_down_dx", dkvpre, w_kv, BF16)
    d_wkv = _tn("kv_down_dw", hk, dkvpre, BF16)
    dh2, d_kvin = _rms_bwd(h2, small["kv_in_norm"], dhk, dh2, "kv_in_norm_bwd")

    dh1, d_down0, d_up0, d_cw0, d_cb0, d_fn0 = ffn_bwd(h1, dh2, 0, ffn0_saved)

    dmix = _nt("sc_out_dx", dh1, w_out, BF16)
    d_wout = _tn("sc_out_dw", mix, dh1, BF16)
    dz, d_scw = _scmix_bwd(z, small["sc_conv_w"], dmix)
    dhn0 = _nt_parts("sc_in_dx", dz, w_in, BF16)
    d_win = _dw_sc_in(hn0, dz)
    dx, d_an0 = _rms_bwd(x, attn_norm[0:1], dhn0, dh1, "attn0_norm_bwd")

    rows = D // N_CHIPS
    big = {
        "sc_w_in": d_win, "sc_w_out": d_wout.reshape(N_CHIPS, rows, D),
        "ffn_w_up0": d_up0, "ffn_w_up1": d_up1, "ffn_w_down0": d_down0, "ffn_w_down1": d_down1,
        "w_kv": d_wkv.reshape(N_CHIPS, rows, KVP), "w_ukv": d_wukv.reshape(N_CHIPS, 2 * KV_LORA, -1),
        "w_dq": d_wdq.reshape(N_CHIPS, rows, Q_LORA), "w_uq": d_wuq, "w_o": d_wo.reshape(N_CHIPS, rows, D),
    }
    small_g = {
        "attn_norm": jnp.concatenate([d_an0, d_an1]), "ffn_norm": jnp.concatenate([d_fn0, d_fn1]),
        "final_norm": d_final, "kv_in_norm": d_kvin, "kv_latent_norm": d_kvln, "q_latent_norm": d_qln,
        "ffn_conv_b": jnp.concatenate([d_cb0, d_cb1]), "sc_conv_w": d_scw, "ffn_conv_w": jnp.stack([d_cw0, d_cw1]),
    }
    return loss, dx, big, small_g


SMALL_REPL = ("attn_norm", "ffn_norm", "final_norm", "kv_in_norm", "kv_latent_norm", "q_latent_norm", "ffn_conv_b")
SMALL_SHARDED = ("sc_conv_w", "ffn_conv_w")
SMALL_ROWS = 256


def _pack_kv(w_dkv, w_kr):
    return jnp.concatenate([w_dkv, w_kr, jnp.zeros((w_kr.shape[0], LANES - QK_ROPE), w_kr.dtype)], axis=1)


def kernel(x, positions, attn_norm, ffn_norm, final_norm, sc_w_in, sc_conv_w, sc_w_out, kv_in_norm, w_dkv, kv_latent_norm, w_kr, w_uk, w_uv, w_dq, q_latent_norm, w_uq, w_o, ffn_w_up, ffn_conv_w, ffn_conv_b, ffn_w_down, loss_target, m_attn_norm, m_ffn_norm, m_final_norm, m_sc_w_in, m_sc_conv_w, m_sc_w_out, m_kv_in_norm, m_w_dkv, m_kv_latent_norm, m_w_kr, m_w_uk, m_w_uv, m_w_dq, m_q_latent_norm, m_w_uq, m_w_o, m_ffn_w_up, m_ffn_conv_w, m_ffn_conv_b, m_ffn_w_down, v_attn_norm, v_ffn_norm, v_final_norm, v_sc_w_in, v_sc_conv_w, v_sc_w_out, v_kv_in_norm, v_w_dkv, v_kv_latent_norm, v_w_kr, v_w_uk, v_w_uv, v_w_dq, v_q_latent_norm, v_w_uq, v_w_o, v_ffn_w_up, v_ffn_conv_w, v_ffn_conv_b, v_ffn_w_down):
    names = ("attn_norm", "ffn_norm", "final_norm", "sc_w_in", "sc_conv_w", "sc_w_out", "kv_in_norm", "w_dkv",
             "kv_latent_norm", "w_kr", "w_uk", "w_uv", "w_dq", "q_latent_norm", "w_uq", "w_o", "ffn_w_up",
             "ffn_conv_w", "ffn_conv_b", "ffn_w_down")
    w = dict(zip(names, (attn_norm, ffn_norm, final_norm, sc_w_in, sc_conv_w, sc_w_out, kv_in_norm, w_dkv,
                         kv_latent_norm, w_kr, w_uk, w_uv, w_dq, q_latent_norm, w_uq, w_o, ffn_w_up,
                         ffn_conv_w, ffn_conv_b, ffn_w_down)))
    m = dict(zip(names, (m_attn_norm, m_ffn_norm, m_final_norm, m_sc_w_in, m_sc_conv_w, m_sc_w_out, m_kv_in_norm,
                         m_w_dkv, m_kv_latent_norm, m_w_kr, m_w_uk, m_w_uv, m_w_dq, m_q_latent_norm, m_w_uq, m_w_o,
                         m_ffn_w_up, m_ffn_conv_w, m_ffn_conv_b, m_ffn_w_down)))
    v = dict(zip(names, (v_attn_norm, v_ffn_norm, v_final_norm, v_sc_w_in, v_sc_conv_w, v_sc_w_out, v_kv_in_norm,
                         v_w_dkv, v_kv_latent_norm, v_w_kr, v_w_uk, v_w_uv, v_w_dq, v_q_latent_norm, v_w_uq, v_w_o,
                         v_ffn_w_up, v_ffn_conv_w, v_ffn_conv_b, v_ffn_w_down)))

    ix, iy, ic = lax.axis_index("x"), lax.axis_index("y"), lax.axis_index("c")
    chip = 2 * ix + iy
    ids = jnp.stack([ic, chip]).astype(jnp.int32)

    def shards_of(t):
        return {
            "sc_w_in": t["sc_w_in"][0], "sc_w_out": t["sc_w_out"][0], "ffn_w_up": t["ffn_w_up"],
            "ffn_w_down": t["ffn_w_down"], "w_kv": _pack_kv(t["w_dkv"], t["w_kr"]),
            "w_ukv": jnp.stack([t["w_uk"], t["w_uv"]]), "w_dq": t["w_dq"][0], "w_uq": t["w_uq"][0], "w_o": t["w_o"][0],
        }

    ws, ms, vs = shards_of(w), shards_of(m), shards_of(v)

    fulls = _all_gather_weights([ws[wm.name].astype(BF16) for wm in BIGW])
    wf = {wm.name: f for wm, f in zip(BIGW, fulls)}

    def place(shard, full_cols):
        ns = shard.shape[-1]
        full = jnp.zeros(shard.shape[:-1] + (full_cols,), F32)
        return lax.dynamic_update_slice_in_dim(full, shard, chip * ns, axis=shard.ndim - 1)

    taps = jnp.concatenate([place(sc_conv_w[0], D).reshape(-1), place(ffn_conv_w, F_FF).reshape(-1)])
    n_taps = taps.shape[0]
    tap_rows = -(-n_taps // (8 * LANES)) * 8
    taps = jnp.pad(taps, (0, tap_rows * LANES - n_taps)).reshape(tap_rows, LANES)
    taps = _all_reduce_small(jnp.where(ic == 0, taps, 0.0), "ag_taps").reshape(-1)
    small = {
        "attn_norm": attn_norm, "ffn_norm": ffn_norm, "final_norm": final_norm[None], "kv_in_norm": kv_in_norm[None],
        "kv_latent_norm": kv_latent_norm[None], "q_latent_norm": q_latent_norm, "ffn_conv_b": ffn_conv_b,
        "sc_conv_w": taps[:3 * D].reshape(3, D), "ffn_conv_w": taps[3 * D:3 * D + 2 * 3 * F_FF].reshape(2, 3, F_FF),
    }

    loss, dx, big_g, small_g = _local_step(x[0], positions[0], loss_target[0], wf, small)

    order = ("ffn_w_down1", "ffn_w_up1", "w_o", "w_uq", "w_dq", "w_ukv", "w_kv", "ffn_w_down0", "ffn_w_up0",
             "sc_w_out", "sc_w_in")
    g_own = [big_g[n] for n in order]
    ra = _pair_exchange(g_own)
    rb = _chip_exchange([_pair_sum(ids, g, a, f"rs_pair_sum_{n}") for n, g, a in zip(order, g_own, ra)])
    g_mine = [_chip_sum(ids, g, a, b, f"rs_chip_sum_{n}") for n, g, a, b in zip(order, g_own, ra, rb)]
    g_sib = _pair_swap(g_mine)
    g_mine, g_sib = dict(zip(order, g_mine)), dict(zip(order, g_sib))

    s_order = SMALL_REPL + SMALL_SHARDED
    flat = jnp.concatenate([small_g[n].reshape(-1) for n in s_order] + [loss.reshape(-1)])
    flat = jnp.pad(flat, (0, SMALL_ROWS * LANES - flat.shape[0])).reshape(SMALL_ROWS, LANES)
    red = _all_reduce_small(flat, "ar_small").reshape(-1)
    sg, off = {}, 0
    for n in s_order:
        sz = small_g[n].size
        sg[n] = red[off:off + sz].reshape(small_g[n].shape)
        off += sz
    loss_out = red[off]
    grads = {n: sg[n].reshape(w[n].shape) for n in SMALL_REPL}
    grads["sc_conv_w"] = lax.dynamic_slice_in_dim(sg["sc_conv_w"], chip * (D // N_CHIPS), D // N_CHIPS, axis=1)[None]
    grads["ffn_conv_w"] = lax.dynamic_slice_in_dim(sg["ffn_conv_w"], chip * (F_FF // N_CHIPS), F_FF // N_CHIPS, axis=2)

    res = {}
    for n in ("sc_w_in", "sc_w_out", "w_kv", "w_dq", "w_uq", "w_o"):
        res[n] = _adamw_shard(ids, ws[n], ms[n], vs[n], g_mine[n], g_sib[n], f"adamw_{n}")
    merged = lambda a: a.reshape(2 * KV_LORA, -1)
    res["w_ukv"] = _adamw_shard(ids, merged(ws["w_ukv"]), merged(ms["w_ukv"]), merged(vs["w_ukv"]),
                                g_mine["w_ukv"], g_sib["w_ukv"], "adamw_w_ukv")
    for n in ("ffn_w_up", "ffn_w_down"):
        first = _adamw_shard(ids, ws[n], ms[n], vs[n], g_mine[n + "0"], g_sib[n + "0"], f"adamw_{n}0", layer=0)
        res[n] = _adamw_shard(ids, ws[n], ms[n], vs[n], g_mine[n + "1"], g_sib[n + "1"], f"adamw_{n}1", layer=1,
                              prev=first)
    outs = [grads, {}, {}, {}]
    for k, dst in enumerate(outs):
        for n in ("sc_w_in", "sc_w_out", "w_dq", "w_uq", "w_o"):
            dst[n] = res[n][k][None]
        dst["ffn_w_up"], dst["ffn_w_down"] = res["ffn_w_up"][k], res["ffn_w_down"][k]
        dst["w_dkv"], dst["w_kr"] = res["w_kv"][k][:, :KV_LORA], res["w_kv"][k][:, KV_LORA:KV_LORA + QK_ROPE]
        dst["w_uk"], dst["w_uv"] = res["w_ukv"][k][:KV_LORA], res["w_ukv"][k][KV_LORA:]
    grads, delta, new_m, new_v = outs

    small_names = SMALL_REPL + SMALL_SHARDED

    def pack_small(tree):
        return jnp.concatenate([tree[n].reshape(-1) for n in small_names]).reshape(-1, LANES)

    small_res = _adamw_small(pack_small(w), pack_small(grads), pack_small(m), pack_small(v))
    for slab, dst in zip(small_res, (delta, new_m, new_v)):
        f, off = slab.reshape(-1), 0
        for n in small_names:
            dst[n] = f[off:off + w[n].size].reshape(w[n].shape)
            off += w[n].size

    return (loss_out, dx[None], *[grads[n] for n in names], *[delta[n] for n in names],
            *[new_m[n] for n in names], *[new_v[n] for n in names])
```

```python
from typing import NamedTuple

import jax
import jax.numpy as jnp
from jax import lax
from jax.experimental import pallas as pl
from jax.experimental.pallas import tpu as pltpu
from jax.experimental.pallas import tpu_sc as plsc

F32 = jnp.float32
BF16 = jnp.bfloat16

T = 2048
D = 1024
F_FF = 2816
N_HEADS = 8
QK_NOPE = 128
QK_ROPE = 64
V_HEAD = 128
Q_LORA = 384
KV_LORA = 256
CHUNK_SHIFT = 6
ROPE_THETA = 10000.0
EPS = 1e-6
NEG_INF = -1e30
HEAD_PAD = 256
KVP = KV_LORA + 128

ADAM_LR = 0.001
ADAM_B1 = 0.9
ADAM_B2 = 0.999
ADAM_EPS = 1e-08
ADAM_WD = 0.01
ADAM_STEP = 10

N_CHIPS = 4
N_DEV = 8
LANES = 128
TC = 256
V7X_VMEM_LIMIT = 56 * 1024 * 1024

MESH = pl.DeviceIdType.MESH
ANY = pl.BlockSpec(memory_space=pl.ANY)


class _W(NamedTuple):
    name: str
    kind: str
    nl: int
    k: int
    n: int


AG_GROUPS = (
    (_W("sc_w_in", "col", 1, D, 3 * D // N_CHIPS), _W("sc_w_out", "row", 1, D // N_CHIPS, D)),
    (_W("ffn_w_up0", "col", 1, D, 2 * F_FF // N_CHIPS), _W("ffn_w_down0", "row", 1, F_FF // N_CHIPS, D)),
    (_W("w_kv", "row", 1, D // N_CHIPS, KVP), _W("w_ukv", "col", 2, KV_LORA, N_HEADS * QK_NOPE // N_CHIPS),
     _W("w_dq", "row", 1, D // N_CHIPS, Q_LORA),
     _W("w_uq", "col", 1, Q_LORA, N_HEADS * (QK_NOPE + QK_ROPE) // N_CHIPS),
     _W("w_o", "row", 1, N_HEADS * V_HEAD // N_CHIPS, D)),
    (_W("ffn_w_up1", "col", 1, D, 2 * F_FF // N_CHIPS), _W("ffn_w_down1", "row", 1, F_FF // N_CHIPS, D)),
)


def _cp(*sem):
    return pltpu.CompilerParams(dimension_semantics=sem, vmem_limit_bytes=V7X_VMEM_LIMIT)


def _tile(n, cands):
    for c in cands:
        if n % c == 0:
            return c
    raise ValueError(f"no tile for {n}")


NN_DIMS = (((1,), (0,)), ((), ()))
NT_DIMS = (((1,), (1,)), ((), ()))
TN_DIMS = (((0,), (0,)), ((), ()))
M_TILES = (1024, 512, 384, 256, 128)
N_TILES = (512, 384, 256, 128)


def _mm(name, a, b, dims, grid, a_spec, b_spec, o_spec, o_sds, add=None, red=None, acc_shape=None):
    n_red = None if red is None else grid[red]

    def body(*refs):
        a_ref, b_ref = refs[0], refs[1]
        add_ref = refs[2] if add is not None else None
        o_ref = refs[3] if add is not None else refs[2]
        part = lax.dot_general(a_ref[...].astype(BF16), b_ref[...].astype(BF16), dims, preferred_element_type=F32)
        if red is None:
            if add is not None:
                part = part + add_ref[...]
            o_ref[...] = part.astype(o_ref.dtype)
            return
        acc_ref = refs[-1]
        r = pl.program_id(red)

        @pl.when(r == 0)
        def _():
            acc_ref[...] = part

        @pl.when(r > 0)
        def _():
            acc_ref[...] += part

        @pl.when(r == n_red - 1)
        def _():
            o_ref[...] = acc_ref[...].astype(o_ref.dtype)

    sem = tuple("arbitrary" if ax == red else "parallel" for ax in range(len(grid)))
    in_specs = [a_spec, b_spec] + ([o_spec] if add is not None else [])
    args = (a, b) + ((add,) if add is not None else ())
    return pl.pallas_call(
        body, name=name, grid=grid, in_specs=in_specs, out_specs=o_spec, out_shape=o_sds,
        scratch_shapes=[] if red is None else [pltpu.VMEM(acc_shape, F32)], compiler_params=_cp(*sem),
    )(*args)


def _nn(name, a, b, out_dtype, add=None, lead=None):
    (m, k), n = a.shape, b.shape[-1]
    tm, tn = _tile(m, M_TILES), _tile(n, N_TILES)
    if lead is None:
        b_spec = pl.BlockSpec((k, tn), lambda i, j: (0, j))
    else:
        b_spec = pl.BlockSpec((None, k, tn), lambda i, j: (lead, 0, j))
    return _mm(name, a, b, NN_DIMS, (m // tm, n // tn), pl.BlockSpec((tm, k), lambda i, j: (i, 0)), b_spec,
               pl.BlockSpec((tm, tn), lambda i, j: (i, j)), jax.ShapeDtypeStruct((m, n), out_dtype), add=add)


def _nn_parts(name, a, b, parts, out_dtype, lead=None, stacked=False):
    m, k = a.shape
    c = b.shape[-1] if stacked else b.shape[-1] // parts
    tm, tn = _tile(m, M_TILES), _tile(c, N_TILES)
    nb = c // tn
    if stacked:
        b_spec = pl.BlockSpec((None, k, tn), lambda i, p, j: (p, 0, j))
    elif lead is None:
        b_spec = pl.BlockSpec((k, tn), lambda i, p, j: (0, p * nb + j))
    else:
        b_spec = pl.BlockSpec((None, k, tn), lambda i, p, j: (lead, 0, p * nb + j))
    return _mm(name, a, b, NN_DIMS, (m // tm, parts, nb), pl.BlockSpec((tm, k), lambda i, p, j: (i, 0)), b_spec,
               pl.BlockSpec((None, tm, tn), lambda i, p, j: (p, i, j)), jax.ShapeDtypeStruct((parts, m, c), out_dtype))


def _nt(name, a, b, out_dtype, lead=None):
    (m, k), n = a.shape, b.shape[-2]
    tm, tn = _tile(m, M_TILES), _tile(n, N_TILES)
    if lead is None:
        b_spec = pl.BlockSpec((tn, k), lambda i, j: (j, 0))
    else:
        b_spec = pl.BlockSpec((None, tn, k), lambda i, j: (lead, j, 0))
    return _mm(name, a, b, NT_DIMS, (m // tm, n // tn), pl.BlockSpec((tm, k), lambda i, j: (i, 0)), b_spec,
               pl.BlockSpec((tm, tn), lambda i, j: (i, j)), jax.ShapeDtypeStruct((m, n), out_dtype))


def _nt_parts(name, a, b, out_dtype, lead=None, stacked=False):
    parts, m, c = a.shape
    n = b.shape[-2]
    tm, tn = _tile(m, M_TILES), _tile(n, N_TILES)
    if stacked:
        b_spec = pl.BlockSpec((None, tn, c), lambda i, j, p: (p, j, 0))
    elif lead is None:
        b_spec = pl.BlockSpec((tn, c), lambda i, j, p: (j, p))
    else:
        b_spec = pl.BlockSpec((None, tn, c), lambda i, j, p: (lead, j, p))
    return _mm(name, a, b, NT_DIMS, (m // tm, n // tn, parts), pl.BlockSpec((None, tm, c), lambda i, j, p: (p, i, 0)),
               b_spec, pl.BlockSpec((tm, tn), lambda i, j, p: (i, j)), jax.ShapeDtypeStruct((m, n), out_dtype),
               red=2, acc_shape=(tm, tn))


def _tn(name, a, b, out_dtype):
    (k, m), n = a.shape, b.shape[1]
    tm, tn = _tile(m, M_TILES), _tile(n, N_TILES)
    return _mm(name, a, b, TN_DIMS, (m // tm, n // tn), pl.BlockSpec((k, tm), lambda i, j: (0, i)),
               pl.BlockSpec((k, tn), lambda i, j: (0, j)), pl.BlockSpec((tm, tn), lambda i, j: (i, j)),
               jax.ShapeDtypeStruct((m, n), out_dtype))


def _dw_sc_in(hn, dz):
    t, tn, tm = hn.shape[0], TC, 512
    per_part, per_chip = D // tn, 3 * D // N_CHIPS // tn
    return _mm("sc_in_dw", hn, dz, TN_DIMS, (D // tm, 3 * D // tn), pl.BlockSpec((t, tm), lambda i, j: (0, i)),
               pl.BlockSpec((None, t, tn), lambda i, j: (j // per_part, 0, j % per_part)),
               pl.BlockSpec((None, tm, tn), lambda i, j: (j // per_chip, i, j % per_chip)),
               jax.ShapeDtypeStruct((N_CHIPS, D, 3 * D // N_CHIPS), BF16))


def _dw_ffn_up(name, hf, dup):
    t, tm, ns = hf.shape[0], 512, 2 * F_FF // N_CHIPS
    return _mm(name, hf, dup, TN_DIMS, (N_CHIPS, D // tm), pl.BlockSpec((t, tm), lambda s, i: (0, i)),
               pl.BlockSpec((None, t, ns), lambda s, i: (s // 2, 0, s % 2)),
               pl.BlockSpec((None, tm, ns), lambda s, i: (s, i, 0)), jax.ShapeDtypeStruct((N_CHIPS, D, ns), BF16))


def _dw_ukv(ckv, dknv):
    t, ns = ckv.shape[0], N_HEADS * QK_NOPE // N_CHIPS
    return _mm("kv_up_dw", ckv, dknv, TN_DIMS, (2, N_CHIPS), pl.BlockSpec((t, KV_LORA), lambda p, s: (0, 0)),
               pl.BlockSpec((None, t, ns), lambda p, s: (p, 0, s)),
               pl.BlockSpec((None, None, KV_LORA, ns), lambda p, s: (s, p, 0, 0)),
               jax.ShapeDtypeStruct((N_CHIPS, 2, KV_LORA, ns), BF16))


def _rms_fwd(x, g, name):
    t, d = x.shape
    tr = 512

    def body(x_ref, g_ref, o_ref):
        xv = x_ref[...]
        r = lax.rsqrt(jnp.mean(xv * xv, axis=1, keepdims=True) + EPS)
        o_ref[...] = (xv * r * g_ref[...]).astype(o_ref.dtype)

    row = pl.BlockSpec((tr, d), lambda i: (i, 0))
    return pl.pallas_call(
        body, name=name, grid=(t // tr,), in_specs=[row, pl.BlockSpec((1, d), lambda i: (0, 0))],
        out_specs=row, out_shape=jax.ShapeDtypeStruct((t, d), BF16), compiler_params=_cp("parallel"),
    )(x, g)


def _rms_bwd_math(xv, g, dy):
    r = lax.rsqrt(jnp.mean(xv * xv, axis=1, keepdims=True) + EPS)
    xh = xv * r
    gy = dy * g
    dx = r * (gy - xh * jnp.mean(gy * xh, axis=1, keepdims=True))
    dg = jnp.sum(dy * xh, axis=0, keepdims=True)
    return dx, dg


def _rms_bwd(x, g, dy, add, name):
    t, d = x.shape
    tr = 512

    def body(*refs):
        if add is None:
            x_ref, g_ref, dy_ref, dx_ref, dg_ref = refs
        else:
            x_ref, g_ref, dy_ref, add_ref, dx_ref, dg_ref = refs
        dx, dg = _rms_bwd_math(x_ref[...], g_ref[...], dy_ref[...].astype(F32))
        if add is not None:
            dx = dx + add_ref[...]
        dx_ref[...] = dx

        @pl.when(pl.program_id(0) == 0)
        def _():
            dg_ref[...] = jnp.zeros_like(dg_ref)

        dg_ref[...] += dg

    row = pl.BlockSpec((tr, d), lambda i: (i, 0))
    vec = pl.BlockSpec((1, d), lambda i: (0, 0))
    in_specs = [row, vec, row] + ([row] if add is not None else [])
    args = (x, g, dy) + ((add,) if add is not None else ())
    return pl.pallas_call(
        body, name=name, grid=(t // tr,), in_specs=in_specs, out_specs=[row, vec],
        out_shape=[jax.ShapeDtypeStruct((t, d), F32), jax.ShapeDtypeStruct((1, d), F32)],
        compiler_params=_cp("arbitrary"),
    )(*args)


def _loss_head(h, g, tgt):
    t, d = h.shape
    tr = 512

    def body(h_ref, g_ref, t_ref, loss_ref, dh_ref, dg_ref):
        xv = h_ref[...]
        gv = g_ref[...]
        r = lax.rsqrt(jnp.mean(xv * xv, axis=1, keepdims=True) + EPS)
        err = xv * r * gv - t_ref[...]
        part = 0.5 * jnp.sum(jnp.mean(err * err, axis=1, keepdims=True), axis=0, keepdims=True)
        dx, dg = _rms_bwd_math(xv, gv, err * (1.0 / d))
        dh_ref[...] = dx

        @pl.when(pl.program_id(0) == 0)
        def _():
            dg_ref[...] = jnp.zeros_like(dg_ref)
            loss_ref[...] = jnp.zeros_like(loss_ref)

        dg_ref[...] += dg
        loss_ref[...] += jnp.broadcast_to(part, loss_ref.shape)

    row = pl.BlockSpec((tr, d), lambda i: (i, 0))
    vec = pl.BlockSpec((1, d), lambda i: (0, 0))
    lspec = pl.BlockSpec((1, LANES), lambda i: (0, 0))
    return pl.pallas_call(
        body, name="loss_head", grid=(t // tr,), in_specs=[row, vec, row], out_specs=[lspec, row, vec],
        out_shape=[jax.ShapeDtypeStruct((1, LANES), F32), jax.ShapeDtypeStruct((t, d), F32),
                   jax.ShapeDtypeStruct((1, d), F32)],
        compiler_params=_cp("arbitrary"),
    )(h, g, tgt)


def _rot_half(x):
    lane = lax.broadcasted_iota(jnp.int32, x.shape, 1)
    return jnp.where((lane % QK_ROPE) < QK_ROPE // 2, -pltpu.roll(x, LANES - 32, axis=1),
                     pltpu.roll(x, 32, axis=1))


def _rope_fwd_math(x, cos, sin):
    return x * cos + _rot_half(x) * sin


def _rope_bwd_math(dy, cos, sin):
    return dy * cos - _rot_half(dy * sin)


def _q_rope_fwd(qpre, cos, sin):
    t, w = qpre.shape
    tr = 256

    def body(q_ref, c_ref, s_ref, o_ref):
        cv, sv = c_ref[...], s_ref[...]
        for h in range(N_HEADS):
            lo = h * HEAD_PAD
            o_ref[:, lo:lo + QK_NOPE] = q_ref[:, lo:lo + QK_NOPE].astype(BF16)
            o_ref[:, lo + QK_NOPE:lo + HEAD_PAD] = _rope_fwd_math(
                q_ref[:, lo + QK_NOPE:lo + HEAD_PAD], cv, sv).astype(BF16)

    row = pl.BlockSpec((tr, w), lambda i: (i, 0))
    tab = pl.BlockSpec((tr, LANES), lambda i: (i, 0))
    return pl.pallas_call(
        body, name="q_rope_fwd", grid=(t // tr,), in_specs=[row, tab, tab], out_specs=row,
        out_shape=jax.ShapeDtypeStruct((t, w), BF16), compiler_params=_cp("parallel"),
    )(qpre, cos, sin)


def _kv_elem_fwd(kvpre, g, cos, sin):
    t = kvpre.shape[0]
    tr = 512

    def body(p_ref, g_ref, c_ref, s_ref, ckv_ref, kr_ref):
        lat = p_ref[:, :KV_LORA]
        r = lax.rsqrt(jnp.mean(lat * lat, axis=1, keepdims=True) + EPS)
        ckv_ref[...] = (lat * r * g_ref[...]).astype(BF16)
        kr_ref[...] = _rope_fwd_math(p_ref[:, KV_LORA:], c_ref[...], s_ref[...]).astype(BF16)

    tab = pl.BlockSpec((tr, LANES), lambda i: (i, 0))
    return pl.pallas_call(
        body, name="kv_elem_fwd", grid=(t // tr,),
        in_specs=[pl.BlockSpec((tr, KVP), lambda i: (i, 0)), pl.BlockSpec((1, KV_LORA), lambda i: (0, 0)), tab, tab],
        out_specs=[pl.BlockSpec((tr, KV_LORA), lambda i: (i, 0)), tab],
        out_shape=[jax.ShapeDtypeStruct((t, KV_LORA), BF16), jax.ShapeDtypeStruct((t, LANES), BF16)],
        compiler_params=_cp("parallel"),
    )(kvpre, g, cos, sin)


def _kv_elem_bwd(kvpre, g, dckv, dkr, cos, sin):
    t = kvpre.shape[0]
    tr = 512

    def body(p_ref, g_ref, dc_ref, dk_ref, c_ref, s_ref, dp_ref, dg_ref):
        dlat, dg = _rms_bwd_math(p_ref[:, :KV_LORA], g_ref[...], dc_ref[...])
        dp_ref[:, :KV_LORA] = dlat.astype(BF16)
        dp_ref[:, KV_LORA:] = _rope_bwd_math(dk_ref[...], c_ref[...], s_ref[...]).astype(BF16)

        @pl.when(pl.program_id(0) == 0)
        def _():
            dg_ref[...] = jnp.zeros_like(dg_ref)

        dg_ref[...] += dg

    tab = pl.BlockSpec((tr, LANES), lambda i: (i, 0))
    pre = pl.BlockSpec((tr, KVP), lambda i: (i, 0))
    vec = pl.BlockSpec((1, KV_LORA), lambda i: (0, 0))
    return pl.pallas_call(
        body, name="kv_elem_bwd", grid=(t // tr,),
        in_specs=[pre, vec, pl.BlockSpec((tr, KV_LORA), lambda i: (i, 0)), tab, tab, tab],
        out_specs=[pre, vec],
        out_shape=[jax.ShapeDtypeStruct((t, KVP), BF16), jax.ShapeDtypeStruct((1, KV_LORA), F32)],
        compiler_params=_cp("arbitrary"),
    )(kvpre, g, dckv, dkr, cos, sin)


def _shift_down(x, k):
    row = lax.broadcasted_iota(jnp.int32, x.shape, 0)
    return jnp.where(row >= k, pltpu.roll(x, k, axis=0), 0.0)


def _shift_up(x, k):
    n = x.shape[0]
    row = lax.broadcasted_iota(jnp.int32, x.shape, 0)
    return jnp.where(row < n - k, pltpu.roll(x, n - k, axis=0), 0.0)


def _conv3(x, w_ref):
    return _shift_down(x, 2) * w_ref[0:1, :] + _shift_down(x, 1) * w_ref[1:2, :] + x * w_ref[2:3, :]


def _conv3_t(dy, w_ref):
    return dy * w_ref[2:3, :] + _shift_up(dy, 1) * w_ref[1:2, :] + _shift_up(dy, 2) * w_ref[0:1, :]


def _conv3_dw(dy, x, dw_ref):
    dw_ref[0:1, :] = jnp.sum(dy * _shift_down(x, 2), axis=0, keepdims=True)
    dw_ref[1:2, :] = jnp.sum(dy * _shift_down(x, 1), axis=0, keepdims=True)
    dw_ref[2:3, :] = jnp.sum(dy * x, axis=0, keepdims=True)


def _col(parts, t):
    if parts is None:
        return pl.BlockSpec((t, TC), lambda j: (0, j))
    return pl.BlockSpec((parts, t, TC), lambda j: (0, 0, j))


def _scmix_fwd(z, w):
    t = z.shape[1]

    def body(z_ref, w_ref, m_ref):
        m_ref[...] = (z_ref[0] * _conv3(z_ref[1] * z_ref[2], w_ref)).astype(BF16)

    return pl.pallas_call(
        body, name="scmix_fwd", grid=(D // TC,), in_specs=[_col(3, t), pl.BlockSpec((3, TC), lambda j: (0, j))],
        out_specs=_col(None, t), out_shape=jax.ShapeDtypeStruct((t, D), BF16), compiler_params=_cp("parallel"),
    )(z, w)


def _scmix_bwd(z, w, dm):
    t = z.shape[1]

    def body(z_ref, w_ref, dm_ref, dz_ref, dw_ref):
        c, u = z_ref[1], z_ref[2]
        cu = c * u
        dmv = dm_ref[...].astype(F32)
        dz_ref[0] = (dmv * _conv3(cu, w_ref)).astype(BF16)
        dcv = dmv * z_ref[0]
        _conv3_dw(dcv, cu, dw_ref)
        dcu = _conv3_t(dcv, w_ref)
        dz_ref[1] = (dcu * u).astype(BF16)
        dz_ref[2] = (dcu * c).astype(BF16)

    wspec = pl.BlockSpec((3, TC), lambda j: (0, j))
    return pl.pallas_call(
        body, name="scmix_bwd", grid=(D // TC,), in_specs=[_col(3, t), wspec, _col(None, t)],
        out_specs=[_col(3, t), wspec],
        out_shape=[jax.ShapeDtypeStruct((3, t, D), BF16), jax.ShapeDtypeStruct((3, D), F32)],
        compiler_params=_cp("parallel"),
    )(z, w, dm)


def _gate_fwd(up, w, bias, name):
    t = up.shape[1]

    def body(u_ref, w_ref, b_ref, a_ref):
        gc = _conv3(u_ref[0], w_ref) + b_ref[...]
        a_ref[...] = (gc * jax.nn.sigmoid(gc) * u_ref[1]).astype(BF16)

    return pl.pallas_call(
        body, name=name, grid=(F_FF // TC,),
        in_specs=[_col(2, t), pl.BlockSpec((3, TC), lambda j: (0, j)), pl.BlockSpec((1, TC), lambda j: (0, j))],
        out_specs=_col(None, t), out_shape=jax.ShapeDtypeStruct((t, F_FF), BF16), compiler_params=_cp("parallel"),
    )(up, w, bias)


def _gate_bwd(up, w, bias, da, name):
    t = up.shape[1]

    def body(u_ref, w_ref, b_ref, da_ref, du_ref, dw_ref, db_ref):
        g = u_ref[0]
        gc = _conv3(g, w_ref) + b_ref[...]
        sg = jax.nn.sigmoid(gc)
        dav = da_ref[...].astype(F32)
        du_ref[1] = (dav * (gc * sg)).astype(BF16)
        dgc = dav * u_ref[1] * (sg * (1.0 + gc * (1.0 - sg)))
        db_ref[...] = jnp.sum(dgc, axis=0, keepdims=True)
        _conv3_dw(dgc, g, dw_ref)
        du_ref[0] = _conv3_t(dgc, w_ref).astype(BF16)

    wspec = pl.BlockSpec((3, TC), lambda j: (0, j))
    bspec = pl.BlockSpec((1, TC), lambda j: (0, j))
    return pl.pallas_call(
        body, name=name, grid=(F_FF // TC,), in_specs=[_col(2, t), wspec, bspec, _col(None, t)],
        out_specs=[_col(2, t), wspec, bspec],
        out_shape=[jax.ShapeDtypeStruct((2, t, F_FF), BF16), jax.ShapeDtypeStruct((3, F_FF), F32),
                   jax.ShapeDtypeStruct((1, F_FF), F32)],
        compiler_params=_cp("parallel"),
    )(up, w, bias, da)


ATT_TQ = 256
ATT_SCALE = (QK_NOPE + QK_ROPE) ** -0.5


def _attn_probs(q, kn, kr, qi):
    s = lax.dot_general(q[:, :QK_NOPE], kn, NT_DIMS, preferred_element_type=F32)
    s = s + lax.dot_general(q[:, QK_NOPE:], kr, NT_DIMS, preferred_element_type=F32)
    s = s * ATT_SCALE
    row = qi * ATT_TQ + lax.broadcasted_iota(jnp.int32, s.shape, 0)
    col = lax.broadcasted_iota(jnp.int32, s.shape, 1)
    s = jnp.where(lax.shift_right_logical(col, CHUNK_SHIFT) <= lax.shift_right_logical(row, CHUNK_SHIFT), s, NEG_INF)
    p = jnp.exp(s - jnp.max(s, axis=1, keepdims=True))
    return p * (1.0 / jnp.sum(p, axis=1, keepdims=True))


def _attn_specs(t):
    q = pl.BlockSpec((ATT_TQ, HEAD_PAD), lambda h, i: (i, h))
    kn = pl.BlockSpec((None, t, QK_NOPE), lambda h, i: (0, 0, h))
    kr = pl.BlockSpec((t, LANES), lambda h, i: (0, 0))
    v = pl.BlockSpec((None, t, V_HEAD), lambda h, i: (1, 0, h))
    o = pl.BlockSpec((ATT_TQ, V_HEAD), lambda h, i: (i, h))
    return q, kn, kr, v, o


def _attn_fwd(q, knv, kr):
    t = q.shape[0]

    def body(q_ref, kn_ref, kr_ref, v_ref, o_ref):
        p = _attn_probs(q_ref[...], kn_ref[...], kr_ref[...], pl.program_id(1))
        o_ref[...] = jnp.dot(p.astype(BF16), v_ref[...], preferred_element_type=F32).astype(BF16)

    qs, kns, krs, vs, os_ = _attn_specs(t)
    return pl.pallas_call(
        body, name="attn_fwd", grid=(N_HEADS, t // ATT_TQ), in_specs=[qs, kns, krs, vs], out_specs=os_,
        out_shape=jax.ShapeDtypeStruct((t, N_HEADS * V_HEAD), BF16), compiler_params=_cp("parallel", "parallel"),
    )(q, knv, kr, knv)


def _attn_bwd(q, knv, kr, do, cos, sin):
    t = q.shape[0]

    def body(q_ref, kn_ref, kr_ref, v_ref, do_ref, c_ref, s_ref, dq_ref, dknv_ref, dkr_ref):
        h, qi = pl.program_id(0), pl.program_id(1)
        qv, knv_, krv, dov = q_ref[...], kn_ref[...], kr_ref[...], do_ref[...]
        p = _attn_probs(qv, knv_, krv, qi)
        dp = lax.dot_general(dov, v_ref[...], NT_DIMS, preferred_element_type=F32)
        ds = (p * (dp - jnp.sum(p * dp, axis=1, keepdims=True)) * ATT_SCALE).astype(BF16)
        dq_ref[:, :QK_NOPE] = jnp.dot(ds, knv_, preferred_element_type=F32).astype(BF16)
        dqr = jnp.dot(ds, krv, preferred_element_type=F32)
        dq_ref[:, QK_NOPE:] = _rope_bwd_math(dqr, c_ref[...], s_ref[...]).astype(BF16)
        dv = lax.dot_general(p.astype(BF16), dov, TN_DIMS, preferred_element_type=F32)
        dkn = lax.dot_general(ds, qv[:, :QK_NOPE], TN_DIMS, preferred_element_type=F32)
        dkr = lax.dot_general(ds, qv[:, QK_NOPE:], TN_DIMS, preferred_element_type=F32)

        @pl.when(qi == 0)
        def _():
            dknv_ref[...] = jnp.zeros_like(dknv_ref)

        @pl.when((qi == 0) & (h == 0))
        def _():
            dkr_ref[...] = jnp.zeros_like(dkr_ref)

        dknv_ref[0] += dkn
        dknv_ref[1] += dv
        dkr_ref[...] += dkr

    qs, kns, krs, vs, os_ = _attn_specs(t)
    tab = pl.BlockSpec((ATT_TQ, LANES), lambda h, i: (i, 0))
    return pl.pallas_call(
        body, name="attn_bwd", grid=(N_HEADS, t // ATT_TQ), in_specs=[qs, kns, krs, vs, os_, tab, tab],
        out_specs=[qs, pl.BlockSpec((2, t, QK_NOPE), lambda h, i: (0, 0, h)), krs],
        out_shape=[jax.ShapeDtypeStruct((t, N_HEADS * HEAD_PAD), BF16),
                   jax.ShapeDtypeStruct((2, t, N_HEADS * QK_NOPE), F32), jax.ShapeDtypeStruct((t, LANES), F32)],
        compiler_params=_cp("arbitrary", "arbitrary"),
    )(q, knv, kr, knv, do, cos, sin)


def _adam_math(w, g, m, v):
    nm = ADAM_B1 * m + (1.0 - ADAM_B1) * g
    nv = ADAM_B2 * v + (1.0 - ADAM_B2) * (g * g)
    m_hat = nm / (1.0 - ADAM_B1 ** ADAM_STEP)
    v_hat = nv / (1.0 - ADAM_B2 ** ADAM_STEP)
    return -ADAM_LR * (m_hat / (jnp.sqrt(v_hat) + ADAM_EPS) + ADAM_WD * w), nm, nv


def _adamw_small(w, g, m, v):
    def body(w_ref, g_ref, m_ref, v_ref, d_ref, nm_ref, nv_ref):
        d_ref[...], nm_ref[...], nv_ref[...] = _adam_math(w_ref[...], g_ref[...], m_ref[...], v_ref[...])

    shp = jax.ShapeDtypeStruct(w.shape, F32)
    return pl.pallas_call(body, name="adamw_small", out_shape=[shp] * 3)(w, g, m, v)


ADAM_BLOCK_BYTES = 1 << 20


def _adamw_shard(ids, w, m, v, g_mine, g_sib, name, layer=None, prev=None):
    r, c = w.shape[-2:]
    half = r // 2
    tr = _tile(half, [d for d in range(half, 7, -8) if d * c * 4 <= ADAM_BLOCK_BYTES] or [8])
    nbh = half // tr

    def body(ids_ref, w_ref, m_ref, v_ref, gm_ref, gs_ref, *rest):
        g_ref, d_ref, nm_ref, nv_ref = rest[-4:]
        mine = (pl.program_id(0) // nbh) == ids_ref[0]

        @pl.when(mine)
        def _():
            g_ref[...] = gm_ref[...]

        @pl.when(jnp.logical_not(mine))
        def _():
            g_ref[...] = gs_ref[...]

        d_ref[...], nm_ref[...], nv_ref[...] = _adam_math(w_ref[...], g_ref[...], m_ref[...], v_ref[...])

    if layer is None:
        wspec = pl.BlockSpec((tr, c), lambda i, ids: (i, 0))
    else:
        wspec = pl.BlockSpec((None, tr, c), lambda i, ids: (layer, i, 0))
    gspec = pl.BlockSpec((tr, c), lambda i, ids: (i % nbh, 0))
    in_specs = [wspec] * 3 + [gspec] * 2
    args = [ids, w, m, v, g_mine, g_sib]
    aliases = {}
    if prev is not None:
        in_specs += [ANY] * 4
        args += list(prev)
        aliases = {6 + k: k for k in range(4)}
    grid_spec = pltpu.PrefetchScalarGridSpec(num_scalar_prefetch=1, grid=(r // tr,), in_specs=in_specs,
                                             out_specs=[wspec] * 4)
    return pl.pallas_call(
        body, name=name, grid_spec=grid_spec, out_shape=[jax.ShapeDtypeStruct(w.shape, F32)] * 4,
        input_output_aliases=aliases, compiler_params=_cp("parallel"),
    )(*args)


def _peer_chip(k_me, j):
    return k_me ^ jnp.where(j == 0, 2, jnp.where(j == 1, 1, 3))


def _pair_sum(ids, g, ra, name):
    _, r, c = g.shape
    half = r // 2

    def body(ids_ref, g_ref, ra_ref, o_ref):
        o_ref[...] = (g_ref[...].astype(F32) + ra_ref[...].astype(F32)).astype(BF16)

    grid_spec = pltpu.PrefetchScalarGridSpec(
        num_scalar_prefetch=1, grid=(3,),
        in_specs=[pl.BlockSpec((None, half, c), lambda j, ids: (_peer_chip(ids[1], j), ids[0], 0)),
                  pl.BlockSpec((None, half, c), lambda j, ids: (_peer_chip(ids[1], j), 0, 0))],
        out_specs=pl.BlockSpec((None, half, c), lambda j, ids: (j, 0, 0)))
    return pl.pallas_call(
        body, name=name, grid_spec=grid_spec, out_shape=jax.ShapeDtypeStruct((3, half, c), BF16),
        compiler_params=_cp("parallel"),
    )(ids, g, ra)


def _chip_sum(ids, g, ra, rb, name):
    _, r, c = g.shape
    half = r // 2

    def body(ids_ref, g_ref, ra_ref, rb_ref, o_ref):
        acc = g_ref[...].astype(F32) + ra_ref[...].astype(F32)
        for j in range(3):
            acc = acc + rb_ref[j].astype(F32)
        o_ref[...] = acc

    grid_spec = pltpu.PrefetchScalarGridSpec(
        num_scalar_prefetch=1, grid=(1,),
        in_specs=[pl.BlockSpec((None, half, c), lambda i, ids: (ids[1], ids[0], 0)),
                  pl.BlockSpec((None, half, c), lambda i, ids: (ids[1], 0, 0)),
                  pl.BlockSpec((3, half, c), lambda i, ids: (0, 0, 0))],
        out_specs=pl.BlockSpec((half, c), lambda i, ids: (0, 0)))
    return pl.pallas_call(
        body, name=name, grid_spec=grid_spec, out_shape=jax.ShapeDtypeStruct((half, c), F32),
        compiler_params=_cp("arbitrary"),
    )(ids, g, ra, rb)


def _position():
    x, y, c = lax.axis_index("x"), lax.axis_index("y"), lax.axis_index("c")
    chips = [(1 - x, y), (x, 1 - y), (1 - x, 1 - y)]
    return x, y, c, chips


def _shard_half(ref, wm, h):
    if wm.nl == 2:
        return ref.at[h]
    return ref.at[pl.ds(pl.multiple_of(h * (wm.k // 2), 16), wm.k // 2), :]


def _region(full, wm, s, h):
    cols = pl.ds(pl.multiple_of(s * wm.n, LANES), wm.n) if wm.kind == "col" else slice(None)
    if wm.nl == 2:
        rows = pl.ds(pl.multiple_of(s * wm.k, 16), wm.k) if wm.kind == "row" else slice(None)
        return full.at[slice(None) if h is None else h, rows, cols]
    if wm.kind == "col":
        rows = slice(None) if h is None else pl.ds(pl.multiple_of(h * (wm.k // 2), 16), wm.k // 2)
    elif h is None:
        rows = pl.ds(pl.multiple_of(s * wm.k, 16), wm.k)
    else:
        rows = pl.ds(pl.multiple_of(s * wm.k + h * (wm.k // 2), 16), wm.k // 2)
    return full.at[rows, cols]


def _full_shape(wm):
    shape = (wm.k, N_CHIPS * wm.n) if wm.kind == "col" else (N_CHIPS * wm.k, wm.n)
    return shape if wm.nl == 1 else (wm.nl,) + shape


def _handshake(peers):
    barrier = pltpu.get_barrier_semaphore()
    for peer in peers:
        pl.semaphore_signal(barrier, inc=1, device_id=peer, device_id_type=MESH)
    pl.semaphore_wait(barrier, len(peers))


def _all_gather_group(gi, shards):
    wms = AG_GROUPS[gi]
    nw = len(wms)

    def body(*refs):
        sh, full = refs[:nw], refs[nw:2 * nw]
        ici_s, ici_r, pass_s, pass_r, own_s, own_r = refs[2 * nw:]
        x, y, c, chips = _position()
        me, sibling = 2 * x + y, (x, y, 1 - c)
        _handshake([(*chip, c) for chip in chips] + [sibling])

        def rcopy(src, dst, s_sem, r_sem, to):
            return pltpu.make_async_remote_copy(src_ref=src, dst_ref=dst, send_sem=s_sem, recv_sem=r_sem,
                                                device_id=to, device_id_type=MESH)

        started = []
        for i, wm in enumerate(wms):
            for j, chip in enumerate(chips):
                started.append(rcopy(_shard_half(sh[i], wm, c), _region(full[i], wm, me, c),
                                     ici_s.at[i, j], ici_r.at[i, j], (*chip, c)))
                started[-1].start()
            started.append(rcopy(sh[i], _region(full[i], wm, me, None), own_s.at[i], own_r.at[i], sibling))
            started[-1].start()
        for i, wm in enumerate(wms):
            for j, chip in enumerate(chips):
                got = _region(full[i], wm, 2 * chip[0] + chip[1], c)
                rcopy(got, got, ici_s.at[i, j], ici_r.at[i, j], sibling).wait_recv()
                started.append(rcopy(got, got, pass_s.at[i, j], pass_r.at[i, j], sibling))
                started[-1].start()
        for i, wm in enumerate(wms):
            mine = _region(full[i], wm, me, None)
            rcopy(mine, mine, own_s.at[i], own_r.at[i], sibling).wait_recv()
            for j, chip in enumerate(chips):
                got = _region(full[i], wm, 2 * chip[0] + chip[1], 1 - c)
                rcopy(got, got, pass_s.at[i, j], pass_r.at[i, j], sibling).wait_recv()
        for cp in started:
            cp.wait_send()

    return pl.kernel(
        body, out_type=[jax.ShapeDtypeStruct(_full_shape(wm), BF16) for wm in wms],
        mesh=plsc.ScalarSubcoreMesh(axis_name="sequencer", num_cores=1), name=f"ag_group{gi}",
        scratch_types=[pltpu.SemaphoreType.DMA((nw, 3))] * 4 + [pltpu.SemaphoreType.DMA((nw,))] * 2,
        compiler_params=pltpu.CompilerParams(collective_id=gi),
    )(*shards)


def _pair_exchange(gs):
    n = len(gs)

    def body(*refs):
        g, out, send_sems, recv_sems = refs[:n], refs[n:2 * n], refs[2 * n], refs[2 * n + 1]
        x, y, c, _ = _position()
        cps = []
        for i in range(n):
            half = g[i].shape[1] // 2
            cps.append(pltpu.make_async_remote_copy(
                src_ref=g[i].at[:, pl.ds(pl.multiple_of((1 - c) * half, 16), half), :], dst_ref=out[i],
                send_sem=send_sems.at[i], recv_sem=recv_sems.at[i], device_id=(x, y, 1 - c), device_id_type=MESH))
            cps[-1].start()
        for cp in cps:
            cp.wait()

    return pl.pallas_call(
        body, name="rs_pair_exchange", in_specs=[ANY] * n, out_specs=[ANY] * n,
        out_shape=[jax.ShapeDtypeStruct((a.shape[0], a.shape[1] // 2, a.shape[2]), a.dtype) for a in gs],
        scratch_shapes=[pltpu.SemaphoreType.DMA((n,)), pltpu.SemaphoreType.DMA((n,))],
    )(*gs)


def _chip_exchange(ss):
    n = len(ss)

    def body(*refs):
        s, out, send_sems, recv_sems = refs[:n], refs[n:2 * n], refs[2 * n], refs[2 * n + 1]
        x, y, c, chips = _position()
        cps = []
        for i in range(n):
            for j, chip in enumerate(chips):
                cps.append(pltpu.make_async_remote_copy(
                    src_ref=s[i].at[j], dst_ref=out[i].at[j], send_sem=send_sems.at[i, j], recv_sem=recv_sems.at[i, j],
                    device_id=(*chip, c), device_id_type=MESH))
                cps[-1].start()
        for cp in cps:
            cp.wait()

    return pl.pallas_call(
        body, name="rs_chip_exchange", in_specs=[ANY] * n, out_specs=[ANY] * n,
        out_shape=[jax.ShapeDtypeStruct(a.shape, a.dtype) for a in ss],
        scratch_shapes=[pltpu.SemaphoreType.DMA((n, 3)), pltpu.SemaphoreType.DMA((n, 3))],
    )(*ss)


def _pair_swap(g8s):
    n = len(g8s)

    def body(*refs):
        g, out, send_sems, recv_sems = refs[:n], refs[n:2 * n], refs[2 * n], refs[2 * n + 1]
        x, y, c, _ = _position()
        cps = []
        for i in range(n):
            cps.append(pltpu.make_async_remote_copy(
                src_ref=g[i], dst_ref=out[i], send_sem=send_sems.at[i], recv_sem=recv_sems.at[i],
                device_id=(x, y, 1 - c), device_id_type=MESH))
            cps[-1].start()
        for cp in cps:
            cp.wait()

    return pl.pallas_call(
        body, name="rs_pair_swap", in_specs=[ANY] * n, out_specs=[ANY] * n,
        out_shape=[jax.ShapeDtypeStruct(a.shape, a.dtype) for a in g8s],
        scratch_shapes=[pltpu.SemaphoreType.DMA((n,)), pltpu.SemaphoreType.DMA((n,))],
    )(*g8s)


def _all_reduce_small(vec, name):
    r, cols = vec.shape

    def body(v_ref, o_ref, gath, send_sems, recv_sems):
        x, y, c, _ = _position()
        me = 4 * x + 2 * y + c
        gath[me] = v_ref[...]
        cps = []
        for rel in range(1, N_DEV):
            peer = (x ^ (rel >> 2), y ^ ((rel >> 1) & 1), c ^ (rel & 1))
            cps.append(pltpu.make_async_remote_copy(
                src_ref=v_ref, dst_ref=gath.at[me], send_sem=send_sems.at[rel - 1], recv_sem=recv_sems.at[rel - 1],
                device_id=peer, device_id_type=MESH))
        for cp in cps:
            cp.start()
        for rel in range(1, N_DEV):
            pltpu.make_async_remote_copy(
                src_ref=v_ref, dst_ref=gath.at[me ^ rel], send_sem=send_sems.at[rel - 1],
                recv_sem=recv_sems.at[rel - 1], device_id=(x, y, c), device_id_type=MESH).wait_recv()
        for cp in cps:
            cp.wait_send()
        acc = gath[0]
        for d in range(1, N_DEV):
            acc = acc + gath[d]
        o_ref[...] = acc

    vm = pl.BlockSpec(memory_space=pltpu.VMEM)
    return pl.pallas_call(
        body, name=name, in_specs=[vm], out_specs=vm, out_shape=jax.ShapeDtypeStruct((r, cols), F32),
        scratch_shapes=[pltpu.VMEM((N_DEV, r, cols), F32), pltpu.SemaphoreType.DMA((N_DEV - 1,)),
                        pltpu.SemaphoreType.DMA((N_DEV - 1,))],
    )(vec)


def _rope_tables(positions):
    half = QK_ROPE // 2
    inv_freq = 1.0 / (ROPE_THETA ** (jnp.arange(half, dtype=F32) / half))
    ang = positions.astype(F32)[:, None] * inv_freq
    zeros = jnp.zeros((positions.shape[0], LANES - QK_ROPE), F32)
    cos, sin = jnp.cos(ang), jnp.sin(ang)
    return jnp.concatenate([cos, cos, zeros], axis=1), jnp.concatenate([sin, sin, zeros], axis=1)


def _local_step(x, positions, tgt, wf, small):
    cos, sin = _rope_tables(positions)
    w_in, w_out = wf["sc_w_in"], wf["sc_w_out"]
    w_ups, w_downs = (wf["ffn_w_up0"], wf["ffn_w_up1"]), (wf["ffn_w_down0"], wf["ffn_w_down1"])
    w_kv, w_ukv, w_dq, w_o = wf["w_kv"], wf["w_ukv"], wf["w_dq"], wf["w_o"]
    w_uq = jnp.pad(wf["w_uq"].reshape(Q_LORA, N_HEADS, QK_NOPE + QK_ROPE),
                   ((0, 0), (0, 0), (0, HEAD_PAD - QK_NOPE - QK_ROPE))).reshape(Q_LORA, N_HEADS * HEAD_PAD)
    attn_norm, ffn_norm = small["attn_norm"], small["ffn_norm"]
    conv_b = small["ffn_conv_b"]

    def ffn_fwd(h, l):
        hf = _rms_fwd(h, ffn_norm[l:l + 1], f"ffn{l}_norm")
        up = _nn_parts(f"ffn{l}_up", hf, w_ups[l], 2, F32)
        a = _gate_fwd(up, small["ffn_conv_w"][l], conv_b[l:l + 1], f"ffn{l}_gate")
        return _nn(f"ffn{l}_down", a, w_downs[l], F32, add=h), (hf, up, a)

    def ffn_bwd(h, dh_out, l, saved):
        hf, up, a = saved
        da = _nt(f"ffn{l}_down_dx", dh_out, w_downs[l], BF16)
        d_down = _tn(f"ffn{l}_down_dw", a, dh_out, BF16)
        dup, d_cw, d_cb = _gate_bwd(up, small["ffn_conv_w"][l], conv_b[l:l + 1], da, f"ffn{l}_gate_bwd")
        dhf = _nt_parts(f"ffn{l}_up_dx", dup, w_ups[l], BF16)
        d_up = _dw_ffn_up(f"ffn{l}_up_dw", hf, dup)
        dh, d_norm = _rms_bwd(h, ffn_norm[l:l + 1], dhf, dh_out, f"ffn{l}_norm_bwd")
        return dh, d_down.reshape(N_CHIPS, F_FF // N_CHIPS, D), d_up, d_cw, d_cb, d_norm

    hn0 = _rms_fwd(x, attn_norm[0:1], "attn0_norm")
    z = _nn_parts("sc_in", hn0, w_in, 3, F32)
    mix = _scmix_fwd(z, small["sc_conv_w"])
    h1 = _nn("sc_out", mix, w_out, F32, add=x)
    h2, ffn0_saved = ffn_fwd(h1, 0)

    hk = _rms_fwd(h2, small["kv_in_norm"], "kv_in_norm")
    kvpre = _nn("kv_down", hk, w_kv, F32)
    ckv, kr = _kv_elem_fwd(kvpre, small["kv_latent_norm"], cos, sin)
    knv = _nn_parts("kv_up", ckv, w_ukv, 2, BF16, stacked=True)

    hn1 = _rms_fwd(h2, attn_norm[1:2], "attn1_norm")
    cq_pre = _nn("q_down", hn1, w_dq, F32)
    cq = _rms_fwd(cq_pre, small["q_latent_norm"], "q_latent_norm")
    q = _q_rope_fwd(_nn("q_up", cq, w_uq, F32), cos, sin)
    o = _attn_fwd(q, knv, kr)
    h3 = _nn("attn_out", o, w_o, F32, add=h2)
    h4, ffn1_saved = ffn_fwd(h3, 1)

    loss, dh4, d_final = _loss_head(h4, small["final_norm"], tgt)

    dh3, d_down1, d_up1, d_cw1, d_cb1, d_fn1 = ffn_bwd(h3, dh4, 1, ffn1_saved)

    do = _nt("attn_out_dx", dh3, w_o, BF16)
    d_wo = _tn("attn_out_dw", o, dh3, BF16)
    dq, dknv, dkr = _attn_bwd(q, knv, kr, do, cos, sin)
    dcq = _nt("q_up_dx", dq, w_uq, F32)
    d_wuq = _tn("q_up_dw", cq, dq, BF16)
    d_wuq = d_wuq.reshape(Q_LORA, N_HEADS, HEAD_PAD)[:, :, :QK_NOPE + QK_ROPE]
    d_wuq = d_wuq.reshape(Q_LORA, N_CHIPS, -1).transpose(1, 0, 2)
    dcq_pre, d_qln = _rms_bwd(cq_pre, small["q_latent_norm"], dcq, None, "q_latent_norm_bwd")
    dhn1 = _nt("q_down_dx", dcq_pre, w_dq, BF16)
    d_wdq = _tn("q_down_dw", hn1, dcq_pre, BF16)
    dh2, d_an1 = _rms_bwd(h2, attn_norm[1:2], dhn1, dh3, "attn1_norm_bwd")

    dckv = _nt_parts("kv_up_dx", dknv, w_ukv, F32, stacked=True)
    d_wukv = _dw_ukv(ckv, dknv)
    dkvpre, d_kvln = _kv_elem_bwd(kvpre, small["kv_latent_norm"], dckv, dkr, cos, sin)
    dhk = _nt("kv_down_dx", dkvpre, w_kv, BF16)
    d_wkv = _tn("kv_down_dw", hk, dkvpre, BF16)
    dh2, d_kvin = _rms_bwd(h2, small["kv_in_norm"], dhk, dh2, "kv_in_norm_bwd")

    dh1, d_down0, d_up0, d_cw0, d_cb0, d_fn0 = ffn_bwd(h1, dh2, 0, ffn0_saved)

    dmix = _nt("sc_out_dx", dh1, w_out, BF16)
    d_wout = _tn("sc_out_dw", mix, dh1, BF16)
    dz, d_scw = _scmix_bwd(z, small["sc_conv_w"], dmix)
    dhn0 = _nt_parts("sc_in_dx", dz, w_in, BF16)
    d_win = _dw_sc_in(hn0, dz)
    dx, d_an0 = _rms_bwd(x, attn_norm[0:1], dhn0, dh1, "attn0_norm_bwd")

    rows = D // N_CHIPS
    big = {
        "sc_w_in": d_win, "sc_w_out": d_wout.reshape(N_CHIPS, rows, D),
        "ffn_w_up0": d_up0, "ffn_w_up1": d_up1, "ffn_w_down0": d_down0, "ffn_w_down1": d_down1,
        "w_kv": d_wkv.reshape(N_CHIPS, rows, KVP), "w_ukv": d_wukv.reshape(N_CHIPS, 2 * KV_LORA, -1),
        "w_dq": d_wdq.reshape(N_CHIPS, rows, Q_LORA), "w_uq": d_wuq, "w_o": d_wo.reshape(N_CHIPS, rows, D),
    }
    small_g = {
        "attn_norm": jnp.concatenate([d_an0, d_an1]), "ffn_norm": jnp.concatenate([d_fn0, d_fn1]),
        "final_norm": d_final, "kv_in_norm": d_kvin, "kv_latent_norm": d_kvln, "q_latent_norm": d_qln,
        "ffn_conv_b": jnp.concatenate([d_cb0, d_cb1]), "sc_conv_w": d_scw, "ffn_conv_w": jnp.stack([d_cw0, d_cw1]),
    }
    return loss, dx, big, small_g


SMALL_REPL = ("attn_norm", "ffn_norm", "final_norm", "kv_in_norm", "kv_latent_norm", "q_latent_norm", "ffn_conv_b")
SMALL_SHARDED = ("sc_conv_w", "ffn_conv_w")
SMALL_ROWS = 256


def _pack_kv(w_dkv, w_kr):
    return jnp.concatenate([w_dkv, w_kr, jnp.zeros((w_kr.shape[0], LANES - QK_ROPE), w_kr.dtype)], axis=1)


def kernel(x, positions, attn_norm, ffn_norm, final_norm, sc_w_in, sc_conv_w, sc_w_out, kv_in_norm, w_dkv, kv_latent_norm, w_kr, w_uk, w_uv, w_dq, q_latent_norm, w_uq, w_o, ffn_w_up, ffn_conv_w, ffn_conv_b, ffn_w_down, loss_target, m_attn_norm, m_ffn_norm, m_final_norm, m_sc_w_in, m_sc_conv_w, m_sc_w_out, m_kv_in_norm, m_w_dkv, m_kv_latent_norm, m_w_kr, m_w_uk, m_w_uv, m_w_dq, m_q_latent_norm, m_w_uq, m_w_o, m_ffn_w_up, m_ffn_conv_w, m_ffn_conv_b, m_ffn_w_down, v_attn_norm, v_ffn_norm, v_final_norm, v_sc_w_in, v_sc_conv_w, v_sc_w_out, v_kv_in_norm, v_w_dkv, v_kv_latent_norm, v_w_kr, v_w_uk, v_w_uv, v_w_dq, v_q_latent_norm, v_w_uq, v_w_o, v_ffn_w_up, v_ffn_conv_w, v_ffn_conv_b, v_ffn_w_down):
    names = ("attn_norm", "ffn_norm", "final_norm", "sc_w_in", "sc_conv_w", "sc_w_out", "kv_in_norm", "w_dkv",
             "kv_latent_norm", "w_kr", "w_uk", "w_uv", "w_dq", "q_latent_norm", "w_uq", "w_o", "ffn_w_up",
             "ffn_conv_w", "ffn_conv_b", "ffn_w_down")
    w = dict(zip(names, (attn_norm, ffn_norm, final_norm, sc_w_in, sc_conv_w, sc_w_out, kv_in_norm, w_dkv,
                         kv_latent_norm, w_kr, w_uk, w_uv, w_dq, q_latent_norm, w_uq, w_o, ffn_w_up,
                         ffn_conv_w, ffn_conv_b, ffn_w_down)))
    m = dict(zip(names, (m_attn_norm, m_ffn_norm, m_final_norm, m_sc_w_in, m_sc_conv_w, m_sc_w_out, m_kv_in_norm,
                         m_w_dkv, m_kv_latent_norm, m_w_kr, m_w_uk, m_w_uv, m_w_dq, m_q_latent_norm, m_w_uq, m_w_o,
                         m_ffn_w_up, m_ffn_conv_w, m_ffn_conv_b, m_ffn_w_down)))
    v = dict(zip(names, (v_attn_norm, v_ffn_norm, v_final_norm, v_sc_w_in, v_sc_conv_w, v_sc_w_out, v_kv_in_norm,
                         v_w_dkv, v_kv_latent_norm, v_w_kr, v_w_uk, v_w_uv, v_w_dq, v_q_latent_norm, v_w_uq, v_w_o,
                         v_ffn_w_up, v_ffn_conv_w, v_ffn_conv_b, v_ffn_w_down)))

    ix, iy, ic = lax.axis_index("x"), lax.axis_index("y"), lax.axis_index("c")
    chip = 2 * ix + iy
    ids = jnp.stack([ic, chip]).astype(jnp.int32)

    def shards_of(t):
        return {
            "sc_w_in": t["sc_w_in"][0], "sc_w_out": t["sc_w_out"][0], "ffn_w_up": t["ffn_w_up"],
            "ffn_w_down": t["ffn_w_down"], "w_kv": _pack_kv(t["w_dkv"], t["w_kr"]),
            "w_ukv": jnp.stack([t["w_uk"], t["w_uv"]]), "w_dq": t["w_dq"][0], "w_uq": t["w_uq"][0], "w_o": t["w_o"][0],
        }

    ws, ms, vs = shards_of(w), shards_of(m), shards_of(v)

    def bf16_shard(name):
        if name[:-1] in ("ffn_w_up", "ffn_w_down"):
            return ws[name[:-1]][int(name[-1])].astype(BF16)
        return ws[name].astype(BF16)

    wf = {}
    for gi, wms in enumerate(AG_GROUPS):
        fulls = _all_gather_group(gi, [bf16_shard(wm.name) for wm in wms])
        wf.update({wm.name: f for wm, f in zip(wms, fulls)})

    def place(shard, full_cols):
        ns = shard.shape[-1]
        full = jnp.zeros(shard.shape[:-1] + (full_cols,), F32)
        return lax.dynamic_update_slice_in_dim(full, shard, chip * ns, axis=shard.ndim - 1)

    taps = jnp.concatenate([place(sc_conv_w[0], D).reshape(-1), place(ffn_conv_w, F_FF).reshape(-1)])
    n_taps = taps.shape[0]
    tap_rows = -(-n_taps // (8 * LANES)) * 8
    taps = jnp.pad(taps, (0, tap_rows * LANES - n_taps)).reshape(tap_rows, LANES)
    taps = _all_reduce_small(jnp.where(ic == 0, taps, 0.0), "ag_taps").reshape(-1)
    small = {
        "attn_norm": attn_norm, "ffn_norm": ffn_norm, "final_norm": final_norm[None], "kv_in_norm": kv_in_norm[None],
        "kv_latent_norm": kv_latent_norm[None], "q_latent_norm": q_latent_norm, "ffn_conv_b": ffn_conv_b,
        "sc_conv_w": taps[:3 * D].reshape(3, D), "ffn_conv_w": taps[3 * D:3 * D + 2 * 3 * F_FF].reshape(2, 3, F_FF),
    }

    loss, dx, big_g, small_g = _local_step(x[0], positions[0], loss_target[0], wf, small)

    order = ("ffn_w_down1", "ffn_w_up1", "w_o", "w_uq", "w_dq", "w_ukv", "w_kv", "ffn_w_down0", "ffn_w_up0",
             "sc_w_out", "sc_w_in")
    g_own = [big_g[n] for n in order]
    ra = _pair_exchange(g_own)
    rb = _chip_exchange([_pair_sum(ids, g, a, f"rs_pair_sum_{n}") for n, g, a in zip(order, g_own, ra)])
    g_mine = [_chip_sum(ids, g, a, b, f"rs_chip_sum_{n}") for n, g, a, b in zip(order, g_own, ra, rb)]
    g_sib = _pair_swap(g_mine)
    g_mine, g_sib = dict(zip(order, g_mine)), dict(zip(order, g_sib))

    s_order = SMALL_REPL + SMALL_SHARDED
    flat = jnp.concatenate([small_g[n].reshape(-1) for n in s_order] + [loss.reshape(-1)])
    flat = jnp.pad(flat, (0, SMALL_ROWS * LANES - flat.shape[0])).reshape(SMALL_ROWS, LANES)
    red = _all_reduce_small(flat, "ar_small").reshape(-1)
    sg, off = {}, 0
    for n in s_order:
        sz = small_g[n].size
        sg[n] = red[off:off + sz].reshape(small_g[n].shape)
        off += sz
    loss_out = red[off]
    grads = {n: sg[n].reshape(w[n].shape) for n in SMALL_REPL}
    grads["sc_conv_w"] = lax.dynamic_slice_in_dim(sg["sc_conv_w"], chip * (D // N_CHIPS), D // N_CHIPS, axis=1)[None]
    grads["ffn_conv_w"] = lax.dynamic_slice_in_dim(sg["ffn_conv_w"], chip * (F_FF // N_CHIPS), F_FF // N_CHIPS, axis=2)

    res = {}
    for n in ("sc_w_in", "sc_w_out", "w_kv", "w_dq", "w_uq", "w_o"):
        res[n] = _adamw_shard(ids, ws[n], ms[n], vs[n], g_mine[n], g_sib[n], f"adamw_{n}")
    merged = lambda a: a.reshape(2 * KV_LORA, -1)
    res["w_ukv"] = _adamw_shard(ids, merged(ws["w_ukv"]), merged(ms["w_ukv"]), merged(vs["w_ukv"]),
                                g_mine["w_ukv"], g_sib["w_ukv"], "adamw_w_ukv")
    for n in ("ffn_w_up", "ffn_w_down"):
        first = _adamw_shard(ids, ws[n], ms[n], vs[n], g_mine[n + "0"], g_sib[n + "0"], f"adamw_{n}0", layer=0)
        res[n] = _adamw_shard(ids, ws[n], ms[n], vs[n], g_mine[n + "1"], g_sib[n + "1"], f"adamw_{n}1", layer=1,
                              prev=first)
    outs = [grads, {}, {}, {}]
    for k, dst in enumerate(outs):
        for n in ("sc_w_in", "sc_w_out", "w_dq", "w_uq", "w_o"):
            dst[n] = res[n][k][None]
        dst["ffn_w_up"], dst["ffn_w_down"] = res["ffn_w_up"][k], res["ffn_w_down"][k]
        dst["w_dkv"], dst["w_kr"] = res["w_kv"][k][:, :KV_LORA], res["w_kv"][k][:, KV_LORA:KV_LORA + QK_ROPE]
        dst["w_uk"], dst["w_uv"] = res["w_ukv"][k][:KV_LORA], res["w_ukv"][k][KV_LORA:]
    grads, delta, new_m, new_v = outs

    small_names = SMALL_REPL + SMALL_SHARDED

    def pack_small(tree):
        return jnp.concatenate([tree[n].reshape(-1) for n in small_names]).reshape(-1, LANES)

    small_res = _adamw_small(pack_small(w), pack_small(grads), pack_small(m), pack_small(v))
    for slab, dst in zip(small_res, (delta, new_m, new_v)):
        f, off = slab.reshape(-1), 0
        for n in small_names:
            dst[n] = f[off:off + w[n].size].reshape(w[n].shape)
            off += w[n].size

    return (loss_out, dx[None], *[grads[n] for n in names], *[delta[n] for n in names],
            *[new_m[n] for n in names], *[new_v[n] for n in names])
```

```python
from typing import NamedTuple

import jax
import jax.numpy as jnp
from jax import lax
from jax.experimental import pallas as pl
from jax.experimental.pallas import tpu as pltpu
from jax.experimental.pallas import tpu_sc as plsc

F32 = jnp.float32
BF16 = jnp.bfloat16

T = 2048
D = 1024
F_FF = 2816
N_HEADS = 8
QK_NOPE = 128
QK_ROPE = 64
V_HEAD = 128
Q_LORA = 384
KV_LORA = 256
CHUNK_SHIFT = 6
ROPE_THETA = 10000.0
EPS = 1e-6
NEG_INF = -1e30
HEAD_PAD = 256
KVP = KV_LORA + 128

ADAM_LR = 0.001
ADAM_B1 = 0.9
ADAM_B2 = 0.999
ADAM_EPS = 1e-08
ADAM_WD = 0.01
ADAM_STEP = 10

N_CHIPS = 4
N_DEV = 8
LANES = 128
TC = 256
V7X_VMEM_LIMIT = 56 * 1024 * 1024

MESH = pl.DeviceIdType.MESH
ANY = pl.BlockSpec(memory_space=pl.ANY)


class _W(NamedTuple):
    name: str
    kind: str
    nl: int
    k: int
    n: int


AG_GROUPS = (
    (_W("sc_w_in", "col", 1, D, 3 * D // N_CHIPS), _W("sc_w_out", "row", 1, D // N_CHIPS, D)),
    (_W("ffn_w_up0", "col", 1, D, 2 * F_FF // N_CHIPS), _W("ffn_w_down0", "row", 1, F_FF // N_CHIPS, D)),
    (_W("w_kv", "row", 1, D // N_CHIPS, KVP), _W("w_ukv", "col", 2, KV_LORA, N_HEADS * QK_NOPE // N_CHIPS),
     _W("w_dq", "row", 1, D // N_CHIPS, Q_LORA),
     _W("w_uq", "col", 1, Q_LORA, N_HEADS * (QK_NOPE + QK_ROPE) // N_CHIPS),
     _W("w_o", "row", 1, N_HEADS * V_HEAD // N_CHIPS, D)),
    (_W("ffn_w_up1", "col", 1, D, 2 * F_FF // N_CHIPS), _W("ffn_w_down1", "row", 1, F_FF // N_CHIPS, D)),
)


def _cp(*sem):
    return pltpu.CompilerParams(dimension_semantics=sem, vmem_limit_bytes=V7X_VMEM_LIMIT)


def _tile(n, cands):
    for c in cands:
        if n % c == 0:
            return c
    raise ValueError(f"no tile for {n}")


NN_DIMS = (((1,), (0,)), ((), ()))
NT_DIMS = (((1,), (1,)), ((), ()))
TN_DIMS = (((0,), (0,)), ((), ()))
M_TILES = (1024, 512, 384, 256, 128)
N_TILES = (512, 384, 256, 128)


def _mm(name, a, b, dims, grid, a_spec, b_spec, o_spec, o_sds, add=None, red=None, acc_shape=None):
    n_red = None if red is None else grid[red]

    def body(*refs):
        a_ref, b_ref = refs[0], refs[1]
        add_ref = refs[2] if add is not None else None
        o_ref = refs[3] if add is not None else refs[2]
        part = lax.dot_general(a_ref[...].astype(BF16), b_ref[...].astype(BF16), dims, preferred_element_type=F32)
        if red is None:
            if add is not None:
                part = part + add_ref[...]
            o_ref[...] = part.astype(o_ref.dtype)
            return
        acc_ref = refs[-1]
        r = pl.program_id(red)

        @pl.when(r == 0)
        def _():
            acc_ref[...] = part

        @pl.when(r > 0)
        def _():
            acc_ref[...] += part

        @pl.when(r == n_red - 1)
        def _():
            o_ref[...] = acc_ref[...].astype(o_ref.dtype)

    sem = tuple("arbitrary" if ax == red else "parallel" for ax in range(len(grid)))
    in_specs = [a_spec, b_spec] + ([o_spec] if add is not None else [])
    args = (a, b) + ((add,) if add is not None else ())
    return pl.pallas_call(
        body, name=name, grid=grid, in_specs=in_specs, out_specs=o_spec, out_shape=o_sds,
        scratch_shapes=[] if red is None else [pltpu.VMEM(acc_shape, F32)], compiler_params=_cp(*sem),
    )(*args)


def _nn(name, a, b, out_dtype, add=None, lead=None):
    (m, k), n = a.shape, b.shape[-1]
    tm, tn = _tile(m, M_TILES), _tile(n, N_TILES)
    if lead is None:
        b_spec = pl.BlockSpec((k, tn), lambda i, j: (0, j))
    else:
        b_spec = pl.BlockSpec((None, k, tn), lambda i, j: (lead, 0, j))
    return _mm(name, a, b, NN_DIMS, (m // tm, n // tn), pl.BlockSpec((tm, k), lambda i, j: (i, 0)), b_spec,
               pl.BlockSpec((tm, tn), lambda i, j: (i, j)), jax.ShapeDtypeStruct((m, n), out_dtype), add=add)


def _nn_parts(name, a, b, parts, out_dtype, lead=None, stacked=False):
    m, k = a.shape
    c = b.shape[-1] if stacked else b.shape[-1] // parts
    tm, tn = _tile(m, M_TILES), _tile(c, N_TILES)
    nb = c // tn
    if stacked:
        b_spec = pl.BlockSpec((None, k, tn), lambda i, p, j: (p, 0, j))
    elif lead is None:
        b_spec = pl.BlockSpec((k, tn), lambda i, p, j: (0, p * nb + j))
    else:
        b_spec = pl.BlockSpec((None, k, tn), lambda i, p, j: (lead, 0, p * nb + j))
    return _mm(name, a, b, NN_DIMS, (m // tm, parts, nb), pl.BlockSpec((tm, k), lambda i, p, j: (i, 0)), b_spec,
               pl.BlockSpec((None, tm, tn), lambda i, p, j: (p, i, j)), jax.ShapeDtypeStruct((parts, m, c), out_dtype))


def _nt(name, a, b, out_dtype, lead=None):
    (m, k), n = a.shape, b.shape[-2]
    tm, tn = _tile(m, M_TILES), _tile(n, N_TILES)
    if lead is None:
        b_spec = pl.BlockSpec((tn, k), lambda i, j: (j, 0))
    else:
        b_spec = pl.BlockSpec((None, tn, k), lambda i, j: (lead, j, 0))
    return _mm(name, a, b, NT_DIMS, (m // tm, n // tn), pl.BlockSpec((tm, k), lambda i, j: (i, 0)), b_spec,
               pl.BlockSpec((tm, tn), lambda i, j: (i, j)), jax.ShapeDtypeStruct((m, n), out_dtype))


def _nt_parts(name, a, b, out_dtype, lead=None, stacked=False):
    parts, m, c = a.shape
    n = b.shape[-2]
    tm, tn = _tile(m, M_TILES), _tile(n, N_TILES)
    if stacked:
        b_spec = pl.BlockSpec((None, tn, c), lambda i, j, p: (p, j, 0))
    elif lead is None:
        b_spec = pl.BlockSpec((tn, c), lambda i, j, p: (j, p))
    else:
        b_spec = pl.BlockSpec((None, tn, c), lambda i, j, p: (lead, j, p))
    return _mm(name, a, b, NT_DIMS, (m // tm, n // tn, parts), pl.BlockSpec((None, tm, c), lambda i, j, p: (p, i, 0)),
               b_spec, pl.BlockSpec((tm, tn), lambda i, j, p: (i, j)), jax.ShapeDtypeStruct((m, n), out_dtype),
               red=2, acc_shape=(tm, tn))


def _tn(name, a, b, out_dtype):
    (k, m), n = a.shape, b.shape[1]
    tm, tn = _tile(m, M_TILES), _tile(n, N_TILES)
    return _mm(name, a, b, TN_DIMS, (m // tm, n // tn), pl.BlockSpec((k, tm), lambda i, j: (0, i)),
               pl.BlockSpec((k, tn), lambda i, j: (0, j)), pl.BlockSpec((tm, tn), lambda i, j: (i, j)),
               jax.ShapeDtypeStruct((m, n), out_dtype))


def _dw_sc_in(hn, dz):
    t, tn, tm = hn.shape[0], TC, 512
    per_part, per_chip = D // tn, 3 * D // N_CHIPS // tn
    return _mm("sc_in_dw", hn, dz, TN_DIMS, (D // tm, 3 * D // tn), pl.BlockSpec((t, tm), lambda i, j: (0, i)),
               pl.BlockSpec((None, t, tn), lambda i, j: (j // per_part, 0, j % per_part)),
               pl.BlockSpec((None, tm, tn), lambda i, j: (j // per_chip, i, j % per_chip)),
               jax.ShapeDtypeStruct((N_CHIPS, D, 3 * D // N_CHIPS), BF16))


def _dw_ffn_up(name, hf, dup):
    t, tm, ns = hf.shape[0], 512, 2 * F_FF // N_CHIPS
    return _mm(name, hf, dup, TN_DIMS, (N_CHIPS, D // tm), pl.BlockSpec((t, tm), lambda s, i: (0, i)),
               pl.BlockSpec((None, t, ns), lambda s, i: (s // 2, 0, s % 2)),
               pl.BlockSpec((None, tm, ns), lambda s, i: (s, i, 0)), jax.ShapeDtypeStruct((N_CHIPS, D, ns), BF16))


def _dw_ukv(ckv, dknv):
    t, ns = ckv.shape[0], N_HEADS * QK_NOPE // N_CHIPS
    return _mm("kv_up_dw", ckv, dknv, TN_DIMS, (2, N_CHIPS), pl.BlockSpec((t, KV_LORA), lambda p, s: (0, 0)),
               pl.BlockSpec((None, t, ns), lambda p, s: (p, 0, s)),
               pl.BlockSpec((None, None, KV_LORA, ns), lambda p, s: (s, p, 0, 0)),
               jax.ShapeDtypeStruct((N_CHIPS, 2, KV_LORA, ns), BF16))


def _rms_fwd(x, g, name):
    t, d = x.shape
    tr = 512

    def body(x_ref, g_ref, o_ref):
        xv = x_ref[...]
        r = lax.rsqrt(jnp.mean(xv * xv, axis=1, keepdims=True) + EPS)
        o_ref[...] = (xv * r * g_ref[...]).astype(o_ref.dtype)

    row = pl.BlockSpec((tr, d), lambda i: (i, 0))
    return pl.pallas_call(
        body, name=name, grid=(t // tr,), in_specs=[row, pl.BlockSpec((1, d), lambda i: (0, 0))],
        out_specs=row, out_shape=jax.ShapeDtypeStruct((t, d), BF16), compiler_params=_cp("parallel"),
    )(x, g)


def _rms_bwd_math(xv, g, dy):
    r = lax.rsqrt(jnp.mean(xv * xv, axis=1, keepdims=True) + EPS)
    xh = xv * r
    gy = dy * g
    dx = r * (gy - xh * jnp.mean(gy * xh, axis=1, keepdims=True))
    dg = jnp.sum(dy * xh, axis=0, keepdims=True)
    return dx, dg


def _rms_bwd(x, g, dy, add, name):
    t, d = x.shape
    tr = 512

    def body(*refs):
        if add is None:
            x_ref, g_ref, dy_ref, dx_ref, dg_ref = refs
        else:
            x_ref, g_ref, dy_ref, add_ref, dx_ref, dg_ref = refs
        dx, dg = _rms_bwd_math(x_ref[...], g_ref[...], dy_ref[...].astype(F32))
        if add is not None:
            dx = dx + add_ref[...]
        dx_ref[...] = dx

        @pl.when(pl.program_id(0) == 0)
        def _():
            dg_ref[...] = jnp.zeros_like(dg_ref)

        dg_ref[...] += dg

    row = pl.BlockSpec((tr, d), lambda i: (i, 0))
    vec = pl.BlockSpec((1, d), lambda i: (0, 0))
    in_specs = [row, vec, row] + ([row] if add is not None else [])
    args = (x, g, dy) + ((add,) if add is not None else ())
    return pl.pallas_call(
        body, name=name, grid=(t // tr,), in_specs=in_specs, out_specs=[row, vec],
        out_shape=[jax.ShapeDtypeStruct((t, d), F32), jax.ShapeDtypeStruct((1, d), F32)],
        compiler_params=_cp("arbitrary"),
    )(*args)


def _loss_head(h, g, tgt):
    t, d = h.shape
    tr = 512

    def body(h_ref, g_ref, t_ref, loss_ref, dh_ref, dg_ref):
        xv = h_ref[...]
        gv = g_ref[...]
        r = lax.rsqrt(jnp.mean(xv * xv, axis=1, keepdims=True) + EPS)
        err = xv * r * gv - t_ref[...]
        part = 0.5 * jnp.sum(jnp.mean(err * err, axis=1, keepdims=True), axis=0, keepdims=True)
        dx, dg = _rms_bwd_math(xv, gv, err * (1.0 / d))
        dh_ref[...] = dx

        @pl.when(pl.program_id(0) == 0)
        def _():
            dg_ref[...] = jnp.zeros_like(dg_ref)
            loss_ref[...] = jnp.zeros_like(loss_ref)

        dg_ref[...] += dg
        loss_ref[...] += jnp.broadcast_to(part, loss_ref.shape)

    row = pl.BlockSpec((tr, d), lambda i: (i, 0))
    vec = pl.BlockSpec((1, d), lambda i: (0, 0))
    lspec = pl.BlockSpec((1, LANES), lambda i: (0, 0))
    return pl.pallas_call(
        body, name="loss_head", grid=(t // tr,), in_specs=[row, vec, row], out_specs=[lspec, row, vec],
        out_shape=[jax.ShapeDtypeStruct((1, LANES), F32), jax.ShapeDtypeStruct((t, d), F32),
                   jax.ShapeDtypeStruct((1, d), F32)],
        compiler_params=_cp("arbitrary"),
    )(h, g, tgt)


def _rot_half(x):
    lane = lax.broadcasted_iota(jnp.int32, x.shape, 1)
    return jnp.where((lane % QK_ROPE) < QK_ROPE // 2, -pltpu.roll(x, LANES - 32, axis=1),
                     pltpu.roll(x, 32, axis=1))


def _rope_fwd_math(x, cos, sin):
    return x * cos + _rot_half(x) * sin


def _rope_bwd_math(dy, cos, sin):
    return dy * cos - _rot_half(dy * sin)


def _q_rope_fwd(qpre, cos, sin):
    t, w = qpre.shape
    tr = 256

    def body(q_ref, c_ref, s_ref, o_ref):
        cv, sv = c_ref[...], s_ref[...]
        for h in range(N_HEADS):
            lo = h * HEAD_PAD
            o_ref[:, lo:lo + QK_NOPE] = q_ref[:, lo:lo + QK_NOPE].astype(BF16)
            o_ref[:, lo + QK_NOPE:lo + HEAD_PAD] = _rope_fwd_math(
                q_ref[:, lo + QK_NOPE:lo + HEAD_PAD], cv, sv).astype(BF16)

    row = pl.BlockSpec((tr, w), lambda i: (i, 0))
    tab = pl.BlockSpec((tr, LANES), lambda i: (i, 0))
    return pl.pallas_call(
        body, name="q_rope_fwd", grid=(t // tr,), in_specs=[row, tab, tab], out_specs=row,
        out_shape=jax.ShapeDtypeStruct((t, w), BF16), compiler_params=_cp("parallel"),
    )(qpre, cos, sin)


def _kv_elem_fwd(kvpre, g, cos, sin):
    t = kvpre.shape[0]
    tr = 512

    def body(p_ref, g_ref, c_ref, s_ref, ckv_ref, kr_ref):
        lat = p_ref[:, :KV_LORA]
        r = lax.rsqrt(jnp.mean(lat * lat, axis=1, keepdims=True) + EPS)
        ckv_ref[...] = (lat * r * g_ref[...]).astype(BF16)
        kr_ref[...] = _rope_fwd_math(p_ref[:, KV_LORA:], c_ref[...], s_ref[...]).astype(BF16)

    tab = pl.BlockSpec((tr, LANES), lambda i: (i, 0))
    return pl.pallas_call(
        body, name="kv_elem_fwd", grid=(t // tr,),
        in_specs=[pl.BlockSpec((tr, KVP), lambda i: (i, 0)), pl.BlockSpec((1, KV_LORA), lambda i: (0, 0)), tab, tab],
        out_specs=[pl.BlockSpec((tr, KV_LORA), lambda i: (i, 0)), tab],
        out_shape=[jax.ShapeDtypeStruct((t, KV_LORA), BF16), jax.ShapeDtypeStruct((t, LANES), BF16)],
        compiler_params=_cp("parallel"),
    )(kvpre, g, cos, sin)


def _kv_elem_bwd(kvpre, g, dckv, dkr, cos, sin):
    t = kvpre.shape[0]
    tr = 512

    def body(p_ref, g_ref, dc_ref, dk_ref, c_ref, s_ref, dp_ref, dg_ref):
        dlat, dg = _rms_bwd_math(p_ref[:, :KV_LORA], g_ref[...], dc_ref[...])
        dp_ref[:, :KV_LORA] = dlat.astype(BF16)
        dp_ref[:, KV_LORA:] = _rope_bwd_math(dk_ref[...], c_ref[...], s_ref[...]).astype(BF16)

        @pl.when(pl.program_id(0) == 0)
        def _():
            dg_ref[...] = jnp.zeros_like(dg_ref)

        dg_ref[...] += dg

    tab = pl.BlockSpec((tr, LANES), lambda i: (i, 0))
    pre = pl.BlockSpec((tr, KVP), lambda i: (i, 0))
    vec = pl.BlockSpec((1, KV_LORA), lambda i: (0, 0))
    return pl.pallas_call(
        body, name="kv_elem_bwd", grid=(t // tr,),
        in_specs=[pre, vec, pl.BlockSpec((tr, KV_LORA), lambda i: (i, 0)), tab, tab, tab],
        out_specs=[pre, vec],
        out_shape=[jax.ShapeDtypeStruct((t, KVP), BF16), jax.ShapeDtypeStruct((1, KV_LORA), F32)],
        compiler_params=_cp("arbitrary"),
    )(kvpre, g, dckv, dkr, cos, sin)


def _shift_down(x, k):
    row = lax.broadcasted_iota(jnp.int32, x.shape, 0)
    return jnp.where(row >= k, pltpu.roll(x, k, axis=0), 0.0)


def _shift_up(x, k):
    n = x.shape[0]
    row = lax.broadcasted_iota(jnp.int32, x.shape, 0)
    return jnp.where(row < n - k, pltpu.roll(x, n - k, axis=0), 0.0)


def _conv3(x, w_ref):
    return _shift_down(x, 2) * w_ref[0:1, :] + _shift_down(x, 1) * w_ref[1:2, :] + x * w_ref[2:3, :]


def _conv3_t(dy, w_ref):
    return dy * w_ref[2:3, :] + _shift_up(dy, 1) * w_ref[1:2, :] + _shift_up(dy, 2) * w_ref[0:1, :]


def _conv3_dw(dy, x, dw_ref):
    dw_ref[0:1, :] = jnp.sum(dy * _shift_down(x, 2), axis=0, keepdims=True)
    dw_ref[1:2, :] = jnp.sum(dy * _shift_down(x, 1), axis=0, keepdims=True)
    dw_ref[2:3, :] = jnp.sum(dy * x, axis=0, keepdims=True)


def _col(parts, t):
    if parts is None:
        return pl.BlockSpec((t, TC), lambda j: (0, j))
    return pl.BlockSpec((parts, t, TC), lambda j: (0, 0, j))


def _scmix_fwd(z, w):
    t = z.shape[1]

    def body(z_ref, w_ref, m_ref):
        m_ref[...] = (z_ref[0] * _conv3(z_ref[1] * z_ref[2], w_ref)).astype(BF16)

    return pl.pallas_call(
        body, name="scmix_fwd", grid=(D // TC,), in_specs=[_col(3, t), pl.BlockSpec((3, TC), lambda j: (0, j))],
        out_specs=_col(None, t), out_shape=jax.ShapeDtypeStruct((t, D), BF16), compiler_params=_cp("parallel"),
    )(z, w)


def _scmix_bwd(z, w, dm):
    t = z.shape[1]

    def body(z_ref, w_ref, dm_ref, dz_ref, dw_ref):
        c, u = z_ref[1], z_ref[2]
        cu = c * u
        dmv = dm_ref[...].astype(F32)
        dz_ref[0] = (dmv * _conv3(cu, w_ref)).astype(BF16)
        dcv = dmv * z_ref[0]
        _conv3_dw(dcv, cu, dw_ref)
        dcu = _conv3_t(dcv, w_ref)
        dz_ref[1] = (dcu * u).astype(BF16)
        dz_ref[2] = (dcu * c).astype(BF16)

    wspec = pl.BlockSpec((3, TC), lambda j: (0, j))
    return pl.pallas_call(
        body, name="scmix_bwd", grid=(D // TC,), in_specs=[_col(3, t), wspec, _col(None, t)],
        out_specs=[_col(3, t), wspec],
        out_shape=[jax.ShapeDtypeStruct((3, t, D), BF16), jax.ShapeDtypeStruct((3, D), F32)],
        compiler_params=_cp("parallel"),
    )(z, w, dm)


def _gate_fwd(up, w, bias, name):
    t = up.shape[1]

    def body(u_ref, w_ref, b_ref, a_ref):
        gc = _conv3(u_ref[0], w_ref) + b_ref[...]
        a_ref[...] = (gc * jax.nn.sigmoid(gc) * u_ref[1]).astype(BF16)

    return pl.pallas_call(
        body, name=name, grid=(F_FF // TC,),
        in_specs=[_col(2, t), pl.BlockSpec((3, TC), lambda j: (0, j)), pl.BlockSpec((1, TC), lambda j: (0, j))],
        out_specs=_col(None, t), out_shape=jax.ShapeDtypeStruct((t, F_FF), BF16), compiler_params=_cp("parallel"),
    )(up, w, bias)


def _gate_bwd(up, w, bias, da, name):
    t = up.shape[1]

    def body(u_ref, w_ref, b_ref, da_ref, du_ref, dw_ref, db_ref):
        g = u_ref[0]
        gc = _conv3(g, w_ref) + b_ref[...]
        sg = jax.nn.sigmoid(gc)
        dav = da_ref[...].astype(F32)
        du_ref[1] = (dav * (gc * sg)).astype(BF16)
        dgc = dav * u_ref[1] * (sg * (1.0 + gc * (1.0 - sg)))
        db_ref[...] = jnp.sum(dgc, axis=0, keepdims=True)
        _conv3_dw(dgc, g, dw_ref)
        du_ref[0] = _conv3_t(dgc, w_ref).astype(BF16)

    wspec = pl.BlockSpec((3, TC), lambda j: (0, j))
    bspec = pl.BlockSpec((1, TC), lambda j: (0, j))
    return pl.pallas_call(
        body, name=name, grid=(F_FF // TC,), in_specs=[_col(2, t), wspec, bspec, _col(None, t)],
        out_specs=[_col(2, t), wspec, bspec],
        out_shape=[jax.ShapeDtypeStruct((2, t, F_FF), BF16), jax.ShapeDtypeStruct((3, F_FF), F32),
                   jax.ShapeDtypeStruct((1, F_FF), F32)],
        compiler_params=_cp("parallel"),
    )(up, w, bias, da)


ATT_TQ = 256
ATT_SCALE = (QK_NOPE + QK_ROPE) ** -0.5


def _attn_probs(q, kn, kr, qi):
    s = lax.dot_general(q[:, :QK_NOPE], kn, NT_DIMS, preferred_element_type=F32)
    s = s + lax.dot_general(q[:, QK_NOPE:], kr, NT_DIMS, preferred_element_type=F32)
    s = s * ATT_SCALE
    row = qi * ATT_TQ + lax.broadcasted_iota(jnp.int32, s.shape, 0)
    col = lax.broadcasted_iota(jnp.int32, s.shape, 1)
    s = jnp.where(lax.shift_right_logical(col, CHUNK_SHIFT) <= lax.shift_right_logical(row, CHUNK_SHIFT), s, NEG_INF)
    p = jnp.exp(s - jnp.max(s, axis=1, keepdims=True))
    return p * (1.0 / jnp.sum(p, axis=1, keepdims=True))


def _attn_specs(t):
    q = pl.BlockSpec((ATT_TQ, HEAD_PAD), lambda h, i: (i, h))
    kn = pl.BlockSpec((None, t, QK_NOPE), lambda h, i: (0, 0, h))
    kr = pl.BlockSpec((t, LANES), lambda h, i: (0, 0))
    v = pl.BlockSpec((None, t, V_HEAD), lambda h, i: (1, 0, h))
    o = pl.BlockSpec((ATT_TQ, V_HEAD), lambda h, i: (i, h))
    return q, kn, kr, v, o


def _attn_fwd(q, knv, kr):
    t = q.shape[0]

    def body(q_ref, kn_ref, kr_ref, v_ref, o_ref):
        p = _attn_probs(q_ref[...], kn_ref[...], kr_ref[...], pl.program_id(1))
        o_ref[...] = jnp.dot(p.astype(BF16), v_ref[...], preferred_element_type=F32).astype(BF16)

    qs, kns, krs, vs, os_ = _attn_specs(t)
    return pl.pallas_call(
        body, name="attn_fwd", grid=(N_HEADS, t // ATT_TQ), in_specs=[qs, kns, krs, vs], out_specs=os_,
        out_shape=jax.ShapeDtypeStruct((t, N_HEADS * V_HEAD), BF16), compiler_params=_cp("parallel", "parallel"),
    )(q, knv, kr, knv)


def _attn_bwd(q, knv, kr, do, cos, sin):
    t = q.shape[0]

    def body(q_ref, kn_ref, kr_ref, v_ref, do_ref, c_ref, s_ref, dq_ref, dknv_ref, dkr_ref):
        h, qi = pl.program_id(0), pl.program_id(1)
        qv, knv_, krv, dov = q_ref[...], kn_ref[...], kr_ref[...], do_ref[...]
        p = _attn_probs(qv, knv_, krv, qi)
        dp = lax.dot_general(dov, v_ref[...], NT_DIMS, preferred_element_type=F32)
        ds = (p * (dp - jnp.sum(p * dp, axis=1, keepdims=True)) * ATT_SCALE).astype(BF16)
        dq_ref[:, :QK_NOPE] = jnp.dot(ds, knv_, preferred_element_type=F32).astype(BF16)
        dqr = jnp.dot(ds, krv, preferred_element_type=F32)
        dq_ref[:, QK_NOPE:] = _rope_bwd_math(dqr, c_ref[...], s_ref[...]).astype(BF16)
        dv = lax.dot_general(p.astype(BF16), dov, TN_DIMS, preferred_element_type=F32)
        dkn = lax.dot_general(ds, qv[:, :QK_NOPE], TN_DIMS, preferred_element_type=F32)
        dkr = lax.dot_general(ds, qv[:, QK_NOPE:], TN_DIMS, preferred_element_type=F32)

        @pl.when(qi == 0)
        def _():
            dknv_ref[...] = jnp.zeros_like(dknv_ref)

        @pl.when((qi == 0) & (h == 0))
        def _():
            dkr_ref[...] = jnp.zeros_like(dkr_ref)

        dknv_ref[0] += dkn
        dknv_ref[1] += dv
        dkr_ref[...] += dkr

    qs, kns, krs, vs, os_ = _attn_specs(t)
    tab = pl.BlockSpec((ATT_TQ, LANES), lambda h, i: (i, 0))
    return pl.pallas_call(
        body, name="attn_bwd", grid=(N_HEADS, t // ATT_TQ), in_specs=[qs, kns, krs, vs, os_, tab, tab],
        out_specs=[qs, pl.BlockSpec((2, t, QK_NOPE), lambda h, i: (0, 0, h)), krs],
        out_shape=[jax.ShapeDtypeStruct((t, N_HEADS * HEAD_PAD), BF16),
                   jax.ShapeDtypeStruct((2, t, N_HEADS * QK_NOPE), F32), jax.ShapeDtypeStruct((t, LANES), F32)],
        compiler_params=_cp("arbitrary", "arbitrary"),
    )(q, knv, kr, knv, do, cos, sin)


def _adam_math(w, g, m, v):
    nm = ADAM_B1 * m + (1.0 - ADAM_B1) * g
    nv = ADAM_B2 * v + (1.0 - ADAM_B2) * (g * g)
    m_hat = nm / (1.0 - ADAM_B1 ** ADAM_STEP)
    v_hat = nv / (1.0 - ADAM_B2 ** ADAM_STEP)
    return -ADAM_LR * (m_hat / (jnp.sqrt(v_hat) + ADAM_EPS) + ADAM_WD * w), nm, nv


def _adamw_small(w, g, m, v):
    def body(w_ref, g_ref, m_ref, v_ref, d_ref, nm_ref, nv_ref):
        d_ref[...], nm_ref[...], nv_ref[...] = _adam_math(w_ref[...], g_ref[...], m_ref[...], v_ref[...])

    shp = jax.ShapeDtypeStruct(w.shape, F32)
    return pl.pallas_call(body, name="adamw_small", out_shape=[shp] * 3)(w, g, m, v)


ADAM_BLOCK_BYTES = 1 << 20


def _adamw_shard(ids, w, m, v, g_mine, g_sib, name, layer=None, prev=None):
    r, c = w.shape[-2:]
    half = r // 2
    tr = _tile(half, [d for d in range(half, 7, -8) if d * c * 4 <= ADAM_BLOCK_BYTES] or [8])
    nbh = half // tr

    def body(ids_ref, w_ref, m_ref, v_ref, gm_ref, gs_ref, *rest):
        g_ref, d_ref, nm_ref, nv_ref = rest[-4:]
        mine = (pl.program_id(0) // nbh) == ids_ref[0]

        @pl.when(mine)
        def _():
            g_ref[...] = gm_ref[...]

        @pl.when(jnp.logical_not(mine))
        def _():
            g_ref[...] = gs_ref[...]

        d_ref[...], nm_ref[...], nv_ref[...] = _adam_math(w_ref[...], g_ref[...], m_ref[...], v_ref[...])

    if layer is None:
        wspec = pl.BlockSpec((tr, c), lambda i, ids: (i, 0))
    else:
        wspec = pl.BlockSpec((None, tr, c), lambda i, ids: (layer, i, 0))
    gspec = pl.BlockSpec((tr, c), lambda i, ids: (i % nbh, 0))
    in_specs = [wspec] * 3 + [gspec] * 2
    args = [ids, w, m, v, g_mine, g_sib]
    aliases = {}
    if prev is not None:
        in_specs += [ANY] * 4
        args += list(prev)
        aliases = {6 + k: k for k in range(4)}
    grid_spec = pltpu.PrefetchScalarGridSpec(num_scalar_prefetch=1, grid=(r // tr,), in_specs=in_specs,
                                             out_specs=[wspec] * 4)
    return pl.pallas_call(
        body, name=name, grid_spec=grid_spec, out_shape=[jax.ShapeDtypeStruct(w.shape, F32)] * 4,
        input_output_aliases=aliases, compiler_params=_cp("parallel"),
    )(*args)


def _peer_chip(k_me, j):
    return k_me ^ jnp.where(j == 0, 2, jnp.where(j == 1, 1, 3))


def _pair_sum(ids, g, ra, name):
    _, r, c = g.shape
    half = r // 2

    def body(ids_ref, g_ref, ra_ref, o_ref):
        o_ref[...] = (g_ref[...].astype(F32) + ra_ref[...].astype(F32)).astype(BF16)

    grid_spec = pltpu.PrefetchScalarGridSpec(
        num_scalar_prefetch=1, grid=(3,),
        in_specs=[pl.BlockSpec((None, half, c), lambda j, ids: (_peer_chip(ids[1], j), ids[0], 0)),
                  pl.BlockSpec((None, half, c), lambda j, ids: (_peer_chip(ids[1], j), 0, 0))],
        out_specs=pl.BlockSpec((None, half, c), lambda j, ids: (j, 0, 0)))
    return pl.pallas_call(
        body, name=name, grid_spec=grid_spec, out_shape=jax.ShapeDtypeStruct((3, half, c), BF16),
        compiler_params=_cp("parallel"),
    )(ids, g, ra)


def _chip_sum(ids, g, ra, rb, name):
    _, r, c = g.shape
    half = r // 2

    def body(ids_ref, g_ref, ra_ref, rb_ref, o_ref):
        acc = g_ref[...].astype(F32) + ra_ref[...].astype(F32)
        for j in range(3):
            acc = acc + rb_ref[j].astype(F32)
        o_ref[...] = acc

    grid_spec = pltpu.PrefetchScalarGridSpec(
        num_scalar_prefetch=1, grid=(1,),
        in_specs=[pl.BlockSpec((None, half, c), lambda i, ids: (ids[1], ids[0], 0)),
                  pl.BlockSpec((None, half, c), lambda i, ids: (ids[1], 0, 0)),
                  pl.BlockSpec((3, half, c), lambda i, ids: (0, 0, 0))],
        out_specs=pl.BlockSpec((half, c), lambda i, ids: (0, 0)))
    return pl.pallas_call(
        body, name=name, grid_spec=grid_spec, out_shape=jax.ShapeDtypeStruct((half, c), F32),
        compiler_params=_cp("arbitrary"),
    )(ids, g, ra, rb)


def _position():
    x, y, c = lax.axis_index("x"), lax.axis_index("y"), lax.axis_index("c")
    chips = [(1 - x, y), (x, 1 - y), (1 - x, 1 - y)]
    return x, y, c, chips


def _shard_half(ref, wm, h):
    if wm.nl == 2:
        return ref.at[h]
    return ref.at[pl.ds(pl.multiple_of(h * (wm.k // 2), 16), wm.k // 2), :]


def _region(full, wm, s, h):
    cols = pl.ds(pl.multiple_of(s * wm.n, LANES), wm.n) if wm.kind == "col" else slice(None)
    if wm.nl == 2:
        rows = pl.ds(pl.multiple_of(s * wm.k, 16), wm.k) if wm.kind == "row" else slice(None)
        return full.at[slice(None) if h is None else h, rows, cols]
    if wm.kind == "col":
        rows = slice(None) if h is None else pl.ds(pl.multiple_of(h * (wm.k // 2), 16), wm.k // 2)
    elif h is None:
        rows = pl.ds(pl.multiple_of(s * wm.k, 16), wm.k)
    else:
        rows = pl.ds(pl.multiple_of(s * wm.k + h * (wm.k // 2), 16), wm.k // 2)
    return full.at[rows, cols]


def _full_shape(wm):
    shape = (wm.k, N_CHIPS * wm.n) if wm.kind == "col" else (N_CHIPS * wm.k, wm.n)
    return shape if wm.nl == 1 else (wm.nl,) + shape


def _handshake(peers):
    barrier = pltpu.get_barrier_semaphore()
    for peer in peers:
        pl.semaphore_signal(barrier, inc=1, device_id=peer, device_id_type=MESH)
    pl.semaphore_wait(barrier, len(peers))


def _all_gather_group(gi, shards):
    wms = AG_GROUPS[gi]
    nw = len(wms)

    def body(*refs):
        sh, full = refs[:nw], refs[nw:2 * nw]
        ici_s, ici_r, pass_s, pass_r, own_s, own_r = refs[2 * nw:]
        x, y, c, chips = _position()
        me, sibling = 2 * x + y, (x, y, 1 - c)
        _handshake([(*chip, c) for chip in chips] + [sibling])

        def rcopy(src, dst, s_sem, r_sem, to):
            return pltpu.make_async_remote_copy(src_ref=src, dst_ref=dst, send_sem=s_sem, recv_sem=r_sem,
                                                device_id=to, device_id_type=MESH)

        started = []
        for i, wm in enumerate(wms):
            for j, chip in enumerate(chips):
                started.append(rcopy(_shard_half(sh[i], wm, c), _region(full[i], wm, me, c),
                                     ici_s.at[i, j], ici_r.at[i, j], (*chip, c)))
                started[-1].start()
            started.append(rcopy(sh[i], _region(full[i], wm, me, None), own_s.at[i], own_r.at[i], sibling))
            started[-1].start()
        for i, wm in enumerate(wms):
            for j, chip in enumerate(chips):
                got = _region(full[i], wm, 2 * chip[0] + chip[1], c)
                rcopy(got, got, ici_s.at[i, j], ici_r.at[i, j], sibling).wait_recv()
                started.append(rcopy(got, got, pass_s.at[i, j], pass_r.at[i, j], sibling))
                started[-1].start()
        for i, wm in enumerate(wms):
            mine = _region(full[i], wm, me, None)
            rcopy(mine, mine, own_s.at[i], own_r.at[i], sibling).wait_recv()
            for j, chip in enumerate(chips):
                got = _region(full[i], wm, 2 * chip[0] + chip[1], 1 - c)
                rcopy(got, got, pass_s.at[i, j], pass_r.at[i, j], sibling).wait_recv()
        for cp in started:
            cp.wait_send()

    return pl.kernel(
        body, out_type=[jax.ShapeDtypeStruct(_full_shape(wm), BF16) for wm in wms],
        mesh=plsc.ScalarSubcoreMesh(axis_name="sequencer", num_cores=1), name=f"ag_group{gi}",
        scratch_types=[pltpu.SemaphoreType.DMA((nw, 3))] * 4 + [pltpu.SemaphoreType.DMA((nw,))] * 2,
        compiler_params=pltpu.CompilerParams(collective_id=gi),
    )(*shards)


def _sequencer_call(body, name, cid, out_types, scratch, args):
    return pl.kernel(
        body, out_type=out_types, mesh=plsc.ScalarSubcoreMesh(axis_name="sequencer", num_cores=1), name=name,
        scratch_types=scratch, compiler_params=pltpu.CompilerParams(collective_id=cid),
    )(*args)


def _pair_exchange(gs, tag, cid):
    n = len(gs)

    def body(*refs):
        g, out, send_sems, recv_sems = refs[:n], refs[n:2 * n], refs[2 * n], refs[2 * n + 1]
        x, y, c, _ = _position()
        _handshake([(x, y, 1 - c)])
        cps = []
        for i in range(n):
            half = g[i].shape[1] // 2
            cps.append(pltpu.make_async_remote_copy(
                src_ref=g[i].at[:, pl.ds(pl.multiple_of((1 - c) * half, 16), half), :], dst_ref=out[i],
                send_sem=send_sems.at[i], recv_sem=recv_sems.at[i], device_id=(x, y, 1 - c), device_id_type=MESH))
            cps[-1].start()
        for cp in cps:
            cp.wait()

    return _sequencer_call(
        body, f"rs_pair_exchange{tag}", cid,
        [jax.ShapeDtypeStruct((a.shape[0], a.shape[1] // 2, a.shape[2]), a.dtype) for a in gs],
        [pltpu.SemaphoreType.DMA((n,)), pltpu.SemaphoreType.DMA((n,))], gs)


def _chip_exchange(ss, tag, cid):
    n = len(ss)

    def body(*refs):
        s, out, send_sems, recv_sems = refs[:n], refs[n:2 * n], refs[2 * n], refs[2 * n + 1]
        x, y, c, chips = _position()
        _handshake([(*chip, c) for chip in chips])
        cps = []
        for i in range(n):
            for j, chip in enumerate(chips):
                cps.append(pltpu.make_async_remote_copy(
                    src_ref=s[i].at[j], dst_ref=out[i].at[j], send_sem=send_sems.at[i, j], recv_sem=recv_sems.at[i, j],
                    device_id=(*chip, c), device_id_type=MESH))
                cps[-1].start()
        for cp in cps:
            cp.wait()

    return _sequencer_call(
        body, f"rs_chip_exchange{tag}", cid, [jax.ShapeDtypeStruct(a.shape, a.dtype) for a in ss],
        [pltpu.SemaphoreType.DMA((n, 3)), pltpu.SemaphoreType.DMA((n, 3))], ss)


def _pair_swap(g8s, tag, cid):
    n = len(g8s)

    def body(*refs):
        g, out, send_sems, recv_sems = refs[:n], refs[n:2 * n], refs[2 * n], refs[2 * n + 1]
        x, y, c, _ = _position()
        _handshake([(x, y, 1 - c)])
        cps = []
        for i in range(n):
            cps.append(pltpu.make_async_remote_copy(
                src_ref=g[i], dst_ref=out[i], send_sem=send_sems.at[i], recv_sem=recv_sems.at[i],
                device_id=(x, y, 1 - c), device_id_type=MESH))
            cps[-1].start()
        for cp in cps:
            cp.wait()

    return _sequencer_call(
        body, f"rs_pair_swap{tag}", cid, [jax.ShapeDtypeStruct(a.shape, a.dtype) for a in g8s],
        [pltpu.SemaphoreType.DMA((n,)), pltpu.SemaphoreType.DMA((n,))], g8s)


def _all_reduce_small(vec, name):
    r, cols = vec.shape

    def body(v_ref, o_ref, gath, send_sems, recv_sems):
        x, y, c, _ = _position()
        me = 4 * x + 2 * y + c
        gath[me] = v_ref[...]
        cps = []
        for rel in range(1, N_DEV):
            peer = (x ^ (rel >> 2), y ^ ((rel >> 1) & 1), c ^ (rel & 1))
            cps.append(pltpu.make_async_remote_copy(
                src_ref=v_ref, dst_ref=gath.at[me], send_sem=send_sems.at[rel - 1], recv_sem=recv_sems.at[rel - 1],
                device_id=peer, device_id_type=MESH))
        for cp in cps:
            cp.start()
        for rel in range(1, N_DEV):
            pltpu.make_async_remote_copy(
                src_ref=v_ref, dst_ref=gath.at[me ^ rel], send_sem=send_sems.at[rel - 1],
                recv_sem=recv_sems.at[rel - 1], device_id=(x, y, c), device_id_type=MESH).wait_recv()
        for cp in cps:
            cp.wait_send()
        acc = gath[0]
        for d in range(1, N_DEV):
            acc = acc + gath[d]
        o_ref[...] = acc

    vm = pl.BlockSpec(memory_space=pltpu.VMEM)
    return pl.pallas_call(
        body, name=name, in_specs=[vm], out_specs=vm, out_shape=jax.ShapeDtypeStruct((r, cols), F32),
        scratch_shapes=[pltpu.VMEM((N_DEV, r, cols), F32), pltpu.SemaphoreType.DMA((N_DEV - 1,)),
                        pltpu.SemaphoreType.DMA((N_DEV - 1,))],
    )(vec)


def _rope_tables(positions):
    half = QK_ROPE // 2
    inv_freq = 1.0 / (ROPE_THETA ** (jnp.arange(half, dtype=F32) / half))
    ang = positions.astype(F32)[:, None] * inv_freq
    zeros = jnp.zeros((positions.shape[0], LANES - QK_ROPE), F32)
    cos, sin = jnp.cos(ang), jnp.sin(ang)
    return jnp.concatenate([cos, cos, zeros], axis=1), jnp.concatenate([sin, sin, zeros], axis=1)


def _local_step(x, positions, tgt, wf, small):
    cos, sin = _rope_tables(positions)
    w_in, w_out = wf["sc_w_in"], wf["sc_w_out"]
    w_ups, w_downs = (wf["ffn_w_up0"], wf["ffn_w_up1"]), (wf["ffn_w_down0"], wf["ffn_w_down1"])
    w_kv, w_ukv, w_dq, w_o = wf["w_kv"], wf["w_ukv"], wf["w_dq"], wf["w_o"]
    w_uq = jnp.pad(wf["w_uq"].reshape(Q_LORA, N_HEADS, QK_NOPE + QK_ROPE),
                   ((0, 0), (0, 0), (0, HEAD_PAD - QK_NOPE - QK_ROPE))).reshape(Q_LORA, N_HEADS * HEAD_PAD)
    attn_norm, ffn_norm = small["attn_norm"], small["ffn_norm"]
    conv_b = small["ffn_conv_b"]

    def ffn_fwd(h, l):
        hf = _rms_fwd(h, ffn_norm[l:l + 1], f"ffn{l}_norm")
        up = _nn_parts(f"ffn{l}_up", hf, w_ups[l], 2, F32)
        a = _gate_fwd(up, small["ffn_conv_w"][l], conv_b[l:l + 1], f"ffn{l}_gate")
        return _nn(f"ffn{l}_down", a, w_downs[l], F32, add=h), (hf, up, a)

    def ffn_bwd(h, dh_out, l, saved):
        hf, up, a = saved
        da = _nt(f"ffn{l}_down_dx", dh_out, w_downs[l], BF16)
        d_down = _tn(f"ffn{l}_down_dw", a, dh_out, BF16)
        dup, d_cw, d_cb = _gate_bwd(up, small["ffn_conv_w"][l], conv_b[l:l + 1], da, f"ffn{l}_gate_bwd")
        dhf = _nt_parts(f"ffn{l}_up_dx", dup, w_ups[l], BF16)
        d_up = _dw_ffn_up(f"ffn{l}_up_dw", hf, dup)
        dh, d_norm = _rms_bwd(h, ffn_norm[l:l + 1], dhf, dh_out, f"ffn{l}_norm_bwd")
        return dh, d_down.reshape(N_CHIPS, F_FF // N_CHIPS, D), d_up, d_cw, d_cb, d_norm

    hn0 = _rms_fwd(x, attn_norm[0:1], "attn0_norm")
    z = _nn_parts("sc_in", hn0, w_in, 3, F32)
    mix = _scmix_fwd(z, small["sc_conv_w"])
    h1 = _nn("sc_out", mix, w_out, F32, add=x)
    h2, ffn0_saved = ffn_fwd(h1, 0)

    hk = _rms_fwd(h2, small["kv_in_norm"], "kv_in_norm")
    kvpre = _nn("kv_down", hk, w_kv, F32)
    ckv, kr = _kv_elem_fwd(kvpre, small["kv_latent_norm"], cos, sin)
    knv = _nn_parts("kv_up", ckv, w_ukv, 2, BF16, stacked=True)

    hn1 = _rms_fwd(h2, attn_norm[1:2], "attn1_norm")
    cq_pre = _nn("q_down", hn1, w_dq, F32)
    cq = _rms_fwd(cq_pre, small["q_latent_norm"], "q_latent_norm")
    q = _q_rope_fwd(_nn("q_up", cq, w_uq, F32), cos, sin)
    o = _attn_fwd(q, knv, kr)
    h3 = _nn("attn_out", o, w_o, F32, add=h2)
    h4, ffn1_saved = ffn_fwd(h3, 1)

    loss, dh4, d_final = _loss_head(h4, small["final_norm"], tgt)

    dh3, d_down1, d_up1, d_cw1, d_cb1, d_fn1 = ffn_bwd(h3, dh4, 1, ffn1_saved)

    do = _nt("attn_out_dx", dh3, w_o, BF16)
    d_wo = _tn("attn_out_dw", o, dh3, BF16)
    dq, dknv, dkr = _attn_bwd(q, knv, kr, do, cos, sin)
    dcq = _nt("q_up_dx", dq, w_uq, F32)
    d_wuq = _tn("q_up_dw", cq, dq, BF16)
    d_wuq = d_wuq.reshape(Q_LORA, N_HEADS, HEAD_PAD)[:, :, :QK_NOPE + QK_ROPE]
    d_wuq = d_wuq.reshape(Q_LORA, N_CHIPS, -1).transpose(1, 0, 2)
    dcq_pre, d_qln = _rms_bwd(cq_pre, small["q_latent_norm"], dcq, None, "q_latent_norm_bwd")
    dhn1 = _nt("q_down_dx", dcq_pre, w_dq, BF16)
    d_wdq = _tn("q_down_dw", hn1, dcq_pre, BF16)
    dh2, d_an1 = _rms_bwd(h2, attn_norm[1:2], dhn1, dh3, "attn1_norm_bwd")

    dckv = _nt_parts("kv_up_dx", dknv, w_ukv, F32, stacked=True)
    d_wukv = _dw_ukv(ckv, dknv)
    dkvpre, d_kvln = _kv_elem_bwd(kvpre, small["kv_latent_norm"], dckv, dkr, cos, sin)
    dhk = _nt("kv_down_dx", dkvpre, w_kv, BF16)
    d_wkv = _tn("kv_down_dw", hk, dkvpre, BF16)
    dh2, d_kvin = _rms_bwd(h2, small["kv_in_norm"], dhk, dh2, "kv_in_norm_bwd")

    dh1, d_down0, d_up0, d_cw0, d_cb0, d_fn0 = ffn_bwd(h1, dh2, 0, ffn0_saved)

    dmix = _nt("sc_out_dx", dh1, w_out, BF16)
    d_wout = _tn("sc_out_dw", mix, dh1, BF16)
    dz, d_scw = _scmix_bwd(z, small["sc_conv_w"], dmix)
    dhn0 = _nt_parts("sc_in_dx", dz, w_in, BF16)
    d_win = _dw_sc_in(hn0, dz)
    dx, d_an0 = _rms_bwd(x, attn_norm[0:1], dhn0, dh1, "attn0_norm_bwd")

    rows = D // N_CHIPS
    big = {
        "sc_w_in": d_win, "sc_w_out": d_wout.reshape(N_CHIPS, rows, D),
        "ffn_w_up0": d_up0, "ffn_w_up1": d_up1, "ffn_w_down0": d_down0, "ffn_w_down1": d_down1,
        "w_kv": d_wkv.reshape(N_CHIPS, rows, KVP), "w_ukv": d_wukv.reshape(N_CHIPS, 2 * KV_LORA, -1),
        "w_dq": d_wdq.reshape(N_CHIPS, rows, Q_LORA), "w_uq": d_wuq, "w_o": d_wo.reshape(N_CHIPS, rows, D),
    }
    small_g = {
        "attn_norm": jnp.concatenate([d_an0, d_an1]), "ffn_norm": jnp.concatenate([d_fn0, d_fn1]),
        "final_norm": d_final, "kv_in_norm": d_kvin, "kv_latent_norm": d_kvln, "q_latent_norm": d_qln,
        "ffn_conv_b": jnp.concatenate([d_cb0, d_cb1]), "sc_conv_w": d_scw, "ffn_conv_w": jnp.stack([d_cw0, d_cw1]),
    }
    return loss, dx, big, small_g


RS_GROUPS = (("ffn_w_down1", "ffn_w_up1"),
             ("w_o", "w_uq", "w_dq", "w_ukv", "w_kv", "ffn_w_down0", "ffn_w_up0"),
             ("sc_w_out", "sc_w_in"))

SMALL_REPL = ("attn_norm", "ffn_norm", "final_norm", "kv_in_norm", "kv_latent_norm", "q_latent_norm", "ffn_conv_b")
SMALL_SHARDED = ("sc_conv_w", "ffn_conv_w")
SMALL_ROWS = 256


def _pack_kv(w_dkv, w_kr):
    return jnp.concatenate([w_dkv, w_kr, jnp.zeros((w_kr.shape[0], LANES - QK_ROPE), w_kr.dtype)], axis=1)


def kernel(x, positions, attn_norm, ffn_norm, final_norm, sc_w_in, sc_conv_w, sc_w_out, kv_in_norm, w_dkv, kv_latent_norm, w_kr, w_uk, w_uv, w_dq, q_latent_norm, w_uq, w_o, ffn_w_up, ffn_conv_w, ffn_conv_b, ffn_w_down, loss_target, m_attn_norm, m_ffn_norm, m_final_norm, m_sc_w_in, m_sc_conv_w, m_sc_w_out, m_kv_in_norm, m_w_dkv, m_kv_latent_norm, m_w_kr, m_w_uk, m_w_uv, m_w_dq, m_q_latent_norm, m_w_uq, m_w_o, m_ffn_w_up, m_ffn_conv_w, m_ffn_conv_b, m_ffn_w_down, v_attn_norm, v_ffn_norm, v_final_norm, v_sc_w_in, v_sc_conv_w, v_sc_w_out, v_kv_in_norm, v_w_dkv, v_kv_latent_norm, v_w_kr, v_w_uk, v_w_uv, v_w_dq, v_q_latent_norm, v_w_uq, v_w_o, v_ffn_w_up, v_ffn_conv_w, v_ffn_conv_b, v_ffn_w_down):
    names = ("attn_norm", "ffn_norm", "final_norm", "sc_w_in", "sc_conv_w", "sc_w_out", "kv_in_norm", "w_dkv",
             "kv_latent_norm", "w_kr", "w_uk", "w_uv", "w_dq", "q_latent_norm", "w_uq", "w_o", "ffn_w_up",
             "ffn_conv_w", "ffn_conv_b", "ffn_w_down")
    w = dict(zip(names, (attn_norm, ffn_norm, final_norm, sc_w_in, sc_conv_w, sc_w_out, kv_in_norm, w_dkv,
                         kv_latent_norm, w_kr, w_uk, w_uv, w_dq, q_latent_norm, w_uq, w_o, ffn_w_up,
                         ffn_conv_w, ffn_conv_b, ffn_w_down)))
    m = dict(zip(names, (m_attn_norm, m_ffn_norm, m_final_norm, m_sc_w_in, m_sc_conv_w, m_sc_w_out, m_kv_in_norm,
                         m_w_dkv, m_kv_latent_norm, m_w_kr, m_w_uk, m_w_uv, m_w_dq, m_q_latent_norm, m_w_uq, m_w_o,
                         m_ffn_w_up, m_ffn_conv_w, m_ffn_conv_b, m_ffn_w_down)))
    v = dict(zip(names, (v_attn_norm, v_ffn_norm, v_final_norm, v_sc_w_in, v_sc_conv_w, v_sc_w_out, v_kv_in_norm,
                         v_w_dkv, v_kv_latent_norm, v_w_kr, v_w_uk, v_w_uv, v_w_dq, v_q_latent_norm, v_w_uq, v_w_o,
                         v_ffn_w_up, v_ffn_conv_w, v_ffn_conv_b, v_ffn_w_down)))

    ix, iy, ic = lax.axis_index("x"), lax.axis_index("y"), lax.axis_index("c")
    chip = 2 * ix + iy
    ids = jnp.stack([ic, chip]).astype(jnp.int32)

    def shards_of(t):
        return {
            "sc_w_in": t["sc_w_in"][0], "sc_w_out": t["sc_w_out"][0], "ffn_w_up": t["ffn_w_up"],
            "ffn_w_down": t["ffn_w_down"], "w_kv": _pack_kv(t["w_dkv"], t["w_kr"]),
            "w_ukv": jnp.stack([t["w_uk"], t["w_uv"]]), "w_dq": t["w_dq"][0], "w_uq": t["w_uq"][0], "w_o": t["w_o"][0],
        }

    ws, ms, vs = shards_of(w), shards_of(m), shards_of(v)

    def bf16_shard(name):
        if name[:-1] in ("ffn_w_up", "ffn_w_down"):
            return ws[name[:-1]][int(name[-1])].astype(BF16)
        return ws[name].astype(BF16)

    wf = {}
    for gi, wms in enumerate(AG_GROUPS):
        fulls = _all_gather_group(gi, [bf16_shard(wm.name) for wm in wms])
        wf.update({wm.name: f for wm, f in zip(wms, fulls)})

    def place(shard, full_cols):
        ns = shard.shape[-1]
        full = jnp.zeros(shard.shape[:-1] + (full_cols,), F32)
        return lax.dynamic_update_slice_in_dim(full, shard, chip * ns, axis=shard.ndim - 1)

    taps = jnp.concatenate([place(sc_conv_w[0], D).reshape(-1), place(ffn_conv_w, F_FF).reshape(-1)])
    n_taps = taps.shape[0]
    tap_rows = -(-n_taps // (8 * LANES)) * 8
    taps = jnp.pad(taps, (0, tap_rows * LANES - n_taps)).reshape(tap_rows, LANES)
    taps = _all_reduce_small(jnp.where(ic == 0, taps, 0.0), "ag_taps").reshape(-1)
    small = {
        "attn_norm": attn_norm, "ffn_norm": ffn_norm, "final_norm": final_norm[None], "kv_in_norm": kv_in_norm[None],
        "kv_latent_norm": kv_latent_norm[None], "q_latent_norm": q_latent_norm, "ffn_conv_b": ffn_conv_b,
        "sc_conv_w": taps[:3 * D].reshape(3, D), "ffn_conv_w": taps[3 * D:3 * D + 2 * 3 * F_FF].reshape(2, 3, F_FF),
    }

    loss, dx, big_g, small_g = _local_step(x[0], positions[0], loss_target[0], wf, small)

    g_mine, g_sib = {}, {}
    for gi, order in enumerate(RS_GROUPS):
        cid = len(AG_GROUPS) + 3 * gi
        g_own = [big_g[n] for n in order]
        ra = _pair_exchange(g_own, gi, cid)
        sums = [_pair_sum(ids, g, a, f"rs_pair_sum_{n}") for n, g, a in zip(order, g_own, ra)]
        rb = _chip_exchange(sums, gi, cid + 1)
        mine = [_chip_sum(ids, g, a, b, f"rs_chip_sum_{n}") for n, g, a, b in zip(order, g_own, ra, rb)]
        g_mine.update(zip(order, mine))
        g_sib.update(zip(order, _pair_swap(mine, gi, cid + 2)))

    s_order = SMALL_REPL + SMALL_SHARDED
    flat = jnp.concatenate([small_g[n].reshape(-1) for n in s_order] + [loss.reshape(-1)])
    flat = jnp.pad(flat, (0, SMALL_ROWS * LANES - flat.shape[0])).reshape(SMALL_ROWS, LANES)
    red = _all_reduce_small(flat, "ar_small").reshape(-1)
    sg, off = {}, 0
    for n in s_order:
        sz = small_g[n].size
        sg[n] = red[off:off + sz].reshape(small_g[n].shape)
        off += sz
    loss_out = red[off]
    grads = {n: sg[n].reshape(w[n].shape) for n in SMALL_REPL}
    grads["sc_conv_w"] = lax.dynamic_slice_in_dim(sg["sc_conv_w"], chip * (D // N_CHIPS), D // N_CHIPS, axis=1)[None]
    grads["ffn_conv_w"] = lax.dynamic_slice_in_dim(sg["ffn_conv_w"], chip * (F_FF // N_CHIPS), F_FF // N_CHIPS, axis=2)

    res = {}
    for n in ("sc_w_in", "sc_w_out", "w_kv", "w_dq", "w_uq", "w_o"):
        res[n] = _adamw_shard(ids, ws[n], ms[n], vs[n], g_mine[n], g_sib[n], f"adamw_{n}")
    merged = lambda a: a.reshape(2 * KV_LORA, -1)
    res["w_ukv"] = _adamw_shard(ids, merged(ws["w_ukv"]), merged(ms["w_ukv"]), merged(vs["w_ukv"]),
                                g_mine["w_ukv"], g_sib["w_ukv"], "adamw_w_ukv")
    for n in ("ffn_w_up", "ffn_w_down"):
        first = _adamw_shard(ids, ws[n], ms[n], vs[n], g_mine[n + "0"], g_sib[n + "0"], f"adamw_{n}0", layer=0)
        res[n] = _adamw_shard(ids, ws[n], ms[n], vs[n], g_mine[n + "1"], g_sib[n + "1"], f"adamw_{n}1", layer=1,
                              prev=first)
    outs = [grads, {}, {}, {}]
    for k, dst in enumerate(outs):
        for n in ("sc_w_in", "sc_w_out", "w_dq", "w_uq", "w_o"):
            dst[n] = res[n][k][None]
        dst["ffn_w_up"], dst["ffn_w_down"] = res["ffn_w_up"][k], res["ffn_w_down"][k]
        dst["w_dkv"], dst["w_kr"] = res["w_kv"][k][:, :KV_LORA], res["w_kv"][k][:, KV_LORA:KV_LORA + QK_ROPE]
        dst["w_uk"], dst["w_uv"] = res["w_ukv"][k][:KV_LORA], res["w_ukv"][k][KV_LORA:]
    grads, delta, new_m, new_v = outs

    small_names = SMALL_REPL + SMALL_SHARDED

    def pack_small(tree):
        return jnp.concatenate([tree[n].reshape(-1) for n in small_names]).reshape(-1, LANES)

    small_res = _adamw_small(pack_small(w), pack_small(grads), pack_small(m), pack_small(v))
    for slab, dst in zip(small_res, (delta, new_m, new_v)):
        f, off = slab.reshape(-1), 0
        for n in small_names:
            dst[n] = f[off:off + w[n].size].reshape(w[n].shape)
            off += w[n].size

    return (loss_out, dx[None], *[grads[n] for n in names], *[delta[n] for n in names],
            *[new_m[n] for n in names], *[new_v[n] for n in names])
```

```python
from typing import NamedTuple

import jax
import jax.numpy as jnp
from jax import lax
from jax.experimental import pallas as pl
from jax.experimental.pallas import tpu as pltpu
from jax.experimental.pallas import tpu_sc as plsc

F32 = jnp.float32
BF16 = jnp.bfloat16

T = 2048
D = 1024
F_FF = 2816
N_HEADS = 8
QK_NOPE = 128
QK_ROPE = 64
V_HEAD = 128
Q_LORA = 384
KV_LORA = 256
CHUNK_SHIFT = 6
ROPE_THETA = 10000.0
EPS = 1e-6
NEG_INF = -1e30
HEAD_PAD = 256
KVP = KV_LORA + 128

ADAM_LR = 0.001
ADAM_B1 = 0.9
ADAM_B2 = 0.999
ADAM_EPS = 1e-08
ADAM_WD = 0.01
ADAM_STEP = 10

N_CHIPS = 4
N_DEV = 8
LANES = 128
TC = 256
V7X_VMEM_LIMIT = 56 * 1024 * 1024

MESH = pl.DeviceIdType.MESH
ANY = pl.BlockSpec(memory_space=pl.ANY)


class _W(NamedTuple):
    name: str
    kind: str
    nl: int
    k: int
    n: int


AG_GROUPS = (
    (_W("sc_w_in", "col", 1, D, 3 * D // N_CHIPS), _W("sc_w_out", "row", 1, D // N_CHIPS, D)),
    (_W("ffn_w_up0", "col", 1, D, 2 * F_FF // N_CHIPS), _W("ffn_w_down0", "row", 1, F_FF // N_CHIPS, D)),
    (_W("w_kv", "row", 1, D // N_CHIPS, KVP), _W("w_ukv", "col", 2, KV_LORA, N_HEADS * QK_NOPE // N_CHIPS),
     _W("w_dq", "row", 1, D // N_CHIPS, Q_LORA),
     _W("w_uq", "col", 1, Q_LORA, N_HEADS * (QK_NOPE + QK_ROPE) // N_CHIPS),
     _W("w_o", "row", 1, N_HEADS * V_HEAD // N_CHIPS, D)),
    (_W("ffn_w_up1", "col", 1, D, 2 * F_FF // N_CHIPS), _W("ffn_w_down1", "row", 1, F_FF // N_CHIPS, D)),
)


def _cp(*sem):
    return pltpu.CompilerParams(dimension_semantics=sem, vmem_limit_bytes=V7X_VMEM_LIMIT)


_ORDER = [None]


def _tc_call(body, *, name, out_shape, in_specs=None, out_specs=None, grid=(), scratch_shapes=(), prefetch=0,
             input_output_aliases=None, compiler_params=None):
    def run(*args):
        specs = [pl.BlockSpec(memory_space=pltpu.VMEM)] * (len(args) - prefetch) if in_specs is None else list(in_specs)
        inner, dep = body, _ORDER[0]
        if dep is not None:
            unread = prefetch + len(specs)
            specs, args = specs + [ANY], (*args, dep)

            def inner(*refs):
                return body(*refs[:unread], *refs[unread + 1:])

        kwargs = dict(name=name, out_shape=out_shape, input_output_aliases=input_output_aliases or {},
                      compiler_params=compiler_params)
        if prefetch:
            kwargs["grid_spec"] = pltpu.PrefetchScalarGridSpec(
                num_scalar_prefetch=prefetch, grid=grid, in_specs=specs, out_specs=out_specs,
                scratch_shapes=scratch_shapes)
        else:
            kwargs.update(grid=grid, in_specs=specs, scratch_shapes=scratch_shapes)
            if out_specs is not None:
                kwargs["out_specs"] = out_specs
        out = pl.pallas_call(inner, **kwargs)(*args)
        _ORDER[0] = out[0] if isinstance(out, (list, tuple)) else out
        return out

    return run


def _tile(n, cands):
    for c in cands:
        if n % c == 0:
            return c
    raise ValueError(f"no tile for {n}")


NN_DIMS = (((1,), (0,)), ((), ()))
NT_DIMS = (((1,), (1,)), ((), ()))
TN_DIMS = (((0,), (0,)), ((), ()))
M_TILES = (1024, 512, 384, 256, 128)
N_TILES = (512, 384, 256, 128)


def _mm(name, a, b, dims, grid, a_spec, b_spec, o_spec, o_sds, add=None, red=None, acc_shape=None):
    n_red = None if red is None else grid[red]

    def body(*refs):
        a_ref, b_ref = refs[0], refs[1]
        add_ref = refs[2] if add is not None else None
        o_ref = refs[3] if add is not None else refs[2]
        part = lax.dot_general(a_ref[...].astype(BF16), b_ref[...].astype(BF16), dims, preferred_element_type=F32)
        if red is None:
            if add is not None:
                part = part + add_ref[...]
            o_ref[...] = part.astype(o_ref.dtype)
            return
        acc_ref = refs[-1]
        r = pl.program_id(red)

        @pl.when(r == 0)
        def _():
            acc_ref[...] = part

        @pl.when(r > 0)
        def _():
            acc_ref[...] += part

        @pl.when(r == n_red - 1)
        def _():
            o_ref[...] = acc_ref[...].astype(o_ref.dtype)

    sem = tuple("arbitrary" if ax == red else "parallel" for ax in range(len(grid)))
    in_specs = [a_spec, b_spec] + ([o_spec] if add is not None else [])
    args = (a, b) + ((add,) if add is not None else ())
    return _tc_call(
        body, name=name, grid=grid, in_specs=in_specs, out_specs=o_spec, out_shape=o_sds,
        scratch_shapes=[] if red is None else [pltpu.VMEM(acc_shape, F32)], compiler_params=_cp(*sem),
    )(*args)


def _nn(name, a, b, out_dtype, add=None, lead=None):
    (m, k), n = a.shape, b.shape[-1]
    tm, tn = _tile(m, M_TILES), _tile(n, N_TILES)
    if lead is None:
        b_spec = pl.BlockSpec((k, tn), lambda i, j: (0, j))
    else:
        b_spec = pl.BlockSpec((None, k, tn), lambda i, j: (lead, 0, j))
    return _mm(name, a, b, NN_DIMS, (m // tm, n // tn), pl.BlockSpec((tm, k), lambda i, j: (i, 0)), b_spec,
               pl.BlockSpec((tm, tn), lambda i, j: (i, j)), jax.ShapeDtypeStruct((m, n), out_dtype), add=add)


def _nn_parts(name, a, b, parts, out_dtype, lead=None, stacked=False):
    m, k = a.shape
    c = b.shape[-1] if stacked else b.shape[-1] // parts
    tm, tn = _tile(m, M_TILES), _tile(c, N_TILES)
    nb = c // tn
    if stacked:
        b_spec = pl.BlockSpec((None, k, tn), lambda i, p, j: (p, 0, j))
    elif lead is None:
        b_spec = pl.BlockSpec((k, tn), lambda i, p, j: (0, p * nb + j))
    else:
        b_spec = pl.BlockSpec((None, k, tn), lambda i, p, j: (lead, 0, p * nb + j))
    return _mm(name, a, b, NN_DIMS, (m // tm, parts, nb), pl.BlockSpec((tm, k), lambda i, p, j: (i, 0)), b_spec,
               pl.BlockSpec((None, tm, tn), lambda i, p, j: (p, i, j)), jax.ShapeDtypeStruct((parts, m, c), out_dtype))


def _nt(name, a, b, out_dtype, lead=None):
    (m, k), n = a.shape, b.shape[-2]
    tm, tn = _tile(m, M_TILES), _tile(n, N_TILES)
    if lead is None:
        b_spec = pl.BlockSpec((tn, k), lambda i, j: (j, 0))
    else:
        b_spec = pl.BlockSpec((None, tn, k), lambda i, j: (lead, j, 0))
    return _mm(name, a, b, NT_DIMS, (m // tm, n // tn), pl.BlockSpec((tm, k), lambda i, j: (i, 0)), b_spec,
               pl.BlockSpec((tm, tn), lambda i, j: (i, j)), jax.ShapeDtypeStruct((m, n), out_dtype))


def _nt_parts(name, a, b, out_dtype, lead=None, stacked=False):
    parts, m, c = a.shape
    n = b.shape[-2]
    tm, tn = _tile(m, M_TILES), _tile(n, N_TILES)
    if stacked:
        b_spec = pl.BlockSpec((None, tn, c), lambda i, j, p: (p, j, 0))
    elif lead is None:
        b_spec = pl.BlockSpec((tn, c), lambda i, j, p: (j, p))
    else:
        b_spec = pl.BlockSpec((None, tn, c), lambda i, j, p: (lead, j, p))
    return _mm(name, a, b, NT_DIMS, (m // tm, n // tn, parts), pl.BlockSpec((None, tm, c), lambda i, j, p: (p, i, 0)),
               b_spec, pl.BlockSpec((tm, tn), lambda i, j, p: (i, j)), jax.ShapeDtypeStruct((m, n), out_dtype),
               red=2, acc_shape=(tm, tn))


def _tn(name, a, b, out_dtype):
    (k, m), n = a.shape, b.shape[1]
    tm, tn = _tile(m, M_TILES), _tile(n, N_TILES)
    return _mm(name, a, b, TN_DIMS, (m // tm, n // tn), pl.BlockSpec((k, tm), lambda i, j: (0, i)),
               pl.BlockSpec((k, tn), lambda i, j: (0, j)), pl.BlockSpec((tm, tn), lambda i, j: (i, j)),
               jax.ShapeDtypeStruct((m, n), out_dtype))


def _dw_sc_in(hn, dz):
    t, tn, tm = hn.shape[0], TC, 512
    per_part, per_chip = D // tn, 3 * D // N_CHIPS // tn
    return _mm("sc_in_dw", hn, dz, TN_DIMS, (D // tm, 3 * D // tn), pl.BlockSpec((t, tm), lambda i, j: (0, i)),
               pl.BlockSpec((None, t, tn), lambda i, j: (j // per_part, 0, j % per_part)),
               pl.BlockSpec((None, tm, tn), lambda i, j: (j // per_chip, i, j % per_chip)),
               jax.ShapeDtypeStruct((N_CHIPS, D, 3 * D // N_CHIPS), BF16))


def _dw_ffn_up(name, hf, dup):
    t, tm, ns = hf.shape[0], 512, 2 * F_FF // N_CHIPS
    return _mm(name, hf, dup, TN_DIMS, (N_CHIPS, D // tm), pl.BlockSpec((t, tm), lambda s, i: (0, i)),
               pl.BlockSpec((None, t, ns), lambda s, i: (s // 2, 0, s % 2)),
               pl.BlockSpec((None, tm, ns), lambda s, i: (s, i, 0)), jax.ShapeDtypeStruct((N_CHIPS, D, ns), BF16))


def _dw_ukv(ckv, dknv):
    t, ns = ckv.shape[0], N_HEADS * QK_NOPE // N_CHIPS
    return _mm("kv_up_dw", ckv, dknv, TN_DIMS, (2, N_CHIPS), pl.BlockSpec((t, KV_LORA), lambda p, s: (0, 0)),
               pl.BlockSpec((None, t, ns), lambda p, s: (p, 0, s)),
               pl.BlockSpec((None, None, KV_LORA, ns), lambda p, s: (s, p, 0, 0)),
               jax.ShapeDtypeStruct((N_CHIPS, 2, KV_LORA, ns), BF16))


def _rms_fwd(x, g, name):
    t, d = x.shape
    tr = 512

    def body(x_ref, g_ref, o_ref):
        xv = x_ref[...]
        r = lax.rsqrt(jnp.mean(xv * xv, axis=1, keepdims=True) + EPS)
        o_ref[...] = (xv * r * g_ref[...]).astype(o_ref.dtype)

    row = pl.BlockSpec((tr, d), lambda i: (i, 0))
    return _tc_call(
        body, name=name, grid=(t // tr,), in_specs=[row, pl.BlockSpec((1, d), lambda i: (0, 0))],
        out_specs=row, out_shape=jax.ShapeDtypeStruct((t, d), BF16), compiler_params=_cp("parallel"),
    )(x, g)


def _rms_bwd_math(xv, g, dy):
    r = lax.rsqrt(jnp.mean(xv * xv, axis=1, keepdims=True) + EPS)
    xh = xv * r
    gy = dy * g
    dx = r * (gy - xh * jnp.mean(gy * xh, axis=1, keepdims=True))
    dg = jnp.sum(dy * xh, axis=0, keepdims=True)
    return dx, dg


def _rms_bwd(x, g, dy, add, name):
    t, d = x.shape
    tr = 512

    def body(*refs):
        if add is None:
            x_ref, g_ref, dy_ref, dx_ref, dg_ref = refs
        else:
            x_ref, g_ref, dy_ref, add_ref, dx_ref, dg_ref = refs
        dx, dg = _rms_bwd_math(x_ref[...], g_ref[...], dy_ref[...].astype(F32))
        if add is not None:
            dx = dx + add_ref[...]
        dx_ref[...] = dx

        @pl.when(pl.program_id(0) == 0)
        def _():
            dg_ref[...] = jnp.zeros_like(dg_ref)

        dg_ref[...] += dg

    row = pl.BlockSpec((tr, d), lambda i: (i, 0))
    vec = pl.BlockSpec((1, d), lambda i: (0, 0))
    in_specs = [row, vec, row] + ([row] if add is not None else [])
    args = (x, g, dy) + ((add,) if add is not None else ())
    return _tc_call(
        body, name=name, grid=(t // tr,), in_specs=in_specs, out_specs=[row, vec],
        out_shape=[jax.ShapeDtypeStruct((t, d), F32), jax.ShapeDtypeStruct((1, d), F32)],
        compiler_params=_cp("arbitrary"),
    )(*args)


def _loss_head(h, g, tgt):
    t, d = h.shape
    tr = 512

    def body(h_ref, g_ref, t_ref, loss_ref, dh_ref, dg_ref):
        xv = h_ref[...]
        gv = g_ref[...]
        r = lax.rsqrt(jnp.mean(xv * xv, axis=1, keepdims=True) + EPS)
        err = xv * r * gv - t_ref[...]
        part = 0.5 * jnp.sum(jnp.mean(err * err, axis=1, keepdims=True), axis=0, keepdims=True)
        dx, dg = _rms_bwd_math(xv, gv, err * (1.0 / d))
        dh_ref[...] = dx

        @pl.when(pl.program_id(0) == 0)
        def _():
            dg_ref[...] = jnp.zeros_like(dg_ref)
            loss_ref[...] = jnp.zeros_like(loss_ref)

        dg_ref[...] += dg
        loss_ref[...] += jnp.broadcast_to(part, loss_ref.shape)

    row = pl.BlockSpec((tr, d), lambda i: (i, 0))
    vec = pl.BlockSpec((1, d), lambda i: (0, 0))
    lspec = pl.BlockSpec((1, LANES), lambda i: (0, 0))
    return _tc_call(
        body, name="loss_head", grid=(t // tr,), in_specs=[row, vec, row], out_specs=[lspec, row, vec],
        out_shape=[jax.ShapeDtypeStruct((1, LANES), F32), jax.ShapeDtypeStruct((t, d), F32),
                   jax.ShapeDtypeStruct((1, d), F32)],
        compiler_params=_cp("arbitrary"),
    )(h, g, tgt)


def _rot_half(x):
    lane = lax.broadcasted_iota(jnp.int32, x.shape, 1)
    return jnp.where((lane % QK_ROPE) < QK_ROPE // 2, -pltpu.roll(x, LANES - 32, axis=1),
                     pltpu.roll(x, 32, axis=1))


def _rope_fwd_math(x, cos, sin):
    return x * cos + _rot_half(x) * sin


def _rope_bwd_math(dy, cos, sin):
    return dy * cos - _rot_half(dy * sin)


def _q_rope_fwd(qpre, cos, sin):
    t, w = qpre.shape
    tr = 256

    def body(q_ref, c_ref, s_ref, o_ref):
        cv, sv = c_ref[...], s_ref[...]
        for h in range(N_HEADS):
            lo = h * HEAD_PAD
            o_ref[:, lo:lo + QK_NOPE] = q_ref[:, lo:lo + QK_NOPE].astype(BF16)
            o_ref[:, lo + QK_NOPE:lo + HEAD_PAD] = _rope_fwd_math(
                q_ref[:, lo + QK_NOPE:lo + HEAD_PAD], cv, sv).astype(BF16)

    row = pl.BlockSpec((tr, w), lambda i: (i, 0))
    tab = pl.BlockSpec((tr, LANES), lambda i: (i, 0))
    return _tc_call(
        body, name="q_rope_fwd", grid=(t // tr,), in_specs=[row, tab, tab], out_specs=row,
        out_shape=jax.ShapeDtypeStruct((t, w), BF16), compiler_params=_cp("parallel"),
    )(qpre, cos, sin)


def _kv_elem_fwd(kvpre, g, cos, sin):
    t = kvpre.shape[0]
    tr = 512

    def body(p_ref, g_ref, c_ref, s_ref, ckv_ref, kr_ref):
        lat = p_ref[:, :KV_LORA]
        r = lax.rsqrt(jnp.mean(lat * lat, axis=1, keepdims=True) + EPS)
        ckv_ref[...] = (lat * r * g_ref[...]).astype(BF16)
        kr_ref[...] = _rope_fwd_math(p_ref[:, KV_LORA:], c_ref[...], s_ref[...]).astype(BF16)

    tab = pl.BlockSpec((tr, LANES), lambda i: (i, 0))
    return _tc_call(
        body, name="kv_elem_fwd", grid=(t // tr,),
        in_specs=[pl.BlockSpec((tr, KVP), lambda i: (i, 0)), pl.BlockSpec((1, KV_LORA), lambda i: (0, 0)), tab, tab],
        out_specs=[pl.BlockSpec((tr, KV_LORA), lambda i: (i, 0)), tab],
        out_shape=[jax.ShapeDtypeStruct((t, KV_LORA), BF16), jax.ShapeDtypeStruct((t, LANES), BF16)],
        compiler_params=_cp("parallel"),
    )(kvpre, g, cos, sin)


def _kv_elem_bwd(kvpre, g, dckv, dkr, cos, sin):
    t = kvpre.shape[0]
    tr = 512

    def body(p_ref, g_ref, dc_ref, dk_ref, c_ref, s_ref, dp_ref, dg_ref):
        dlat, dg = _rms_bwd_math(p_ref[:, :KV_LORA], g_ref[...], dc_ref[...])
        dp_ref[:, :KV_LORA] = dlat.astype(BF16)
        dp_ref[:, KV_LORA:] = _rope_bwd_math(dk_ref[...], c_ref[...], s_ref[...]).astype(BF16)

        @pl.when(pl.program_id(0) == 0)
        def _():
            dg_ref[...] = jnp.zeros_like(dg_ref)

        dg_ref[...] += dg

    tab = pl.BlockSpec((tr, LANES), lambda i: (i, 0))
    pre = pl.BlockSpec((tr, KVP), lambda i: (i, 0))
    vec = pl.BlockSpec((1, KV_LORA), lambda i: (0, 0))
    return _tc_call(
        body, name="kv_elem_bwd", grid=(t // tr,),
        in_specs=[pre, vec, pl.BlockSpec((tr, KV_LORA), lambda i: (i, 0)), tab, tab, tab],
        out_specs=[pre, vec],
        out_shape=[jax.ShapeDtypeStruct((t, KVP), BF16), jax.ShapeDtypeStruct((1, KV_LORA), F32)],
        compiler_params=_cp("arbitrary"),
    )(kvpre, g, dckv, dkr, cos, sin)


def _shift_down(x, k):
    row = lax.broadcasted_iota(jnp.int32, x.shape, 0)
    return jnp.where(row >= k, pltpu.roll(x, k, axis=0), 0.0)


def _shift_up(x, k):
    n = x.shape[0]
    row = lax.broadcasted_iota(jnp.int32, x.shape, 0)
    return jnp.where(row < n - k, pltpu.roll(x, n - k, axis=0), 0.0)


def _conv3(x, w_ref):
    return _shift_down(x, 2) * w_ref[0:1, :] + _shift_down(x, 1) * w_ref[1:2, :] + x * w_ref[2:3, :]


def _conv3_t(dy, w_ref):
    return dy * w_ref[2:3, :] + _shift_up(dy, 1) * w_ref[1:2, :] + _shift_up(dy, 2) * w_ref[0:1, :]


def _conv3_dw(dy, x, dw_ref):
    dw_ref[0:1, :] = jnp.sum(dy * _shift_down(x, 2), axis=0, keepdims=True)
    dw_ref[1:2, :] = jnp.sum(dy * _shift_down(x, 1), axis=0, keepdims=True)
    dw_ref[2:3, :] = jnp.sum(dy * x, axis=0, keepdims=True)


def _col(parts, t):
    if parts is None:
        return pl.BlockSpec((t, TC), lambda j: (0, j))
    return pl.BlockSpec((parts, t, TC), lambda j: (0, 0, j))


def _scmix_fwd(z, w):
    t = z.shape[1]

    def body(z_ref, w_ref, m_ref):
        m_ref[...] = (z_ref[0] * _conv3(z_ref[1] * z_ref[2], w_ref)).astype(BF16)

    return _tc_call(
        body, name="scmix_fwd", grid=(D // TC,), in_specs=[_col(3, t), pl.BlockSpec((3, TC), lambda j: (0, j))],
        out_specs=_col(None, t), out_shape=jax.ShapeDtypeStruct((t, D), BF16), compiler_params=_cp("parallel"),
    )(z, w)


def _scmix_bwd(z, w, dm):
    t = z.shape[1]

    def body(z_ref, w_ref, dm_ref, dz_ref, dw_ref):
        c, u = z_ref[1], z_ref[2]
        cu = c * u
        dmv = dm_ref[...].astype(F32)
        dz_ref[0] = (dmv * _conv3(cu, w_ref)).astype(BF16)
        dcv = dmv * z_ref[0]
        _conv3_dw(dcv, cu, dw_ref)
        dcu = _conv3_t(dcv, w_ref)
        dz_ref[1] = (dcu * u).astype(BF16)
        dz_ref[2] = (dcu * c).astype(BF16)

    wspec = pl.BlockSpec((3, TC), lambda j: (0, j))
    return _tc_call(
        body, name="scmix_bwd", grid=(D // TC,), in_specs=[_col(3, t), wspec, _col(None, t)],
        out_specs=[_col(3, t), wspec],
        out_shape=[jax.ShapeDtypeStruct((3, t, D), BF16), jax.ShapeDtypeStruct((3, D), F32)],
        compiler_params=_cp("parallel"),
    )(z, w, dm)


def _gate_fwd(up, w, bias, name):
    t = up.shape[1]

    def body(u_ref, w_ref, b_ref, a_ref):
        gc = _conv3(u_ref[0], w_ref) + b_ref[...]
        a_ref[...] = (gc * jax.nn.sigmoid(gc) * u_ref[1]).astype(BF16)

    return _tc_call(
        body, name=name, grid=(F_FF // TC,),
        in_specs=[_col(2, t), pl.BlockSpec((3, TC), lambda j: (0, j)), pl.BlockSpec((1, TC), lambda j: (0, j))],
        out_specs=_col(None, t), out_shape=jax.ShapeDtypeStruct((t, F_FF), BF16), compiler_params=_cp("parallel"),
    )(up, w, bias)


def _gate_bwd(up, w, bias, da, name):
    t = up.shape[1]

    def body(u_ref, w_ref, b_ref, da_ref, du_ref, dw_ref, db_ref):
        g = u_ref[0]
        gc = _conv3(g, w_ref) + b_ref[...]
        sg = jax.nn.sigmoid(gc)
        dav = da_ref[...].astype(F32)
        du_ref[1] = (dav * (gc * sg)).astype(BF16)
        dgc = dav * u_ref[1] * (sg * (1.0 + gc * (1.0 - sg)))
        db_ref[...] = jnp.sum(dgc, axis=0, keepdims=True)
        _conv3_dw(dgc, g, dw_ref)
        du_ref[0] = _conv3_t(dgc, w_ref).astype(BF16)

    wspec = pl.BlockSpec((3, TC), lambda j: (0, j))
    bspec = pl.BlockSpec((1, TC), lambda j: (0, j))
    return _tc_call(
        body, name=name, grid=(F_FF // TC,), in_specs=[_col(2, t), wspec, bspec, _col(None, t)],
        out_specs=[_col(2, t), wspec, bspec],
        out_shape=[jax.ShapeDtypeStruct((2, t, F_FF), BF16), jax.ShapeDtypeStruct((3, F_FF), F32),
                   jax.ShapeDtypeStruct((1, F_FF), F32)],
        compiler_params=_cp("parallel"),
    )(up, w, bias, da)


ATT_TQ = 256
ATT_SCALE = (QK_NOPE + QK_ROPE) ** -0.5


def _attn_probs(q, kn, kr, qi):
    s = lax.dot_general(q[:, :QK_NOPE], kn, NT_DIMS, preferred_element_type=F32)
    s = s + lax.dot_general(q[:, QK_NOPE:], kr, NT_DIMS, preferred_element_type=F32)
    s = s * ATT_SCALE
    row = qi * ATT_TQ + lax.broadcasted_iota(jnp.int32, s.shape, 0)
    col = lax.broadcasted_iota(jnp.int32, s.shape, 1)
    s = jnp.where(lax.shift_right_logical(col, CHUNK_SHIFT) <= lax.shift_right_logical(row, CHUNK_SHIFT), s, NEG_INF)
    p = jnp.exp(s - jnp.max(s, axis=1, keepdims=True))
    return p * (1.0 / jnp.sum(p, axis=1, keepdims=True))


def _attn_specs(t):
    q = pl.BlockSpec((ATT_TQ, HEAD_PAD), lambda h, i: (i, h))
    kn = pl.BlockSpec((None, t, QK_NOPE), lambda h, i: (0, 0, h))
    kr = pl.BlockSpec((t, LANES), lambda h, i: (0, 0))
    v = pl.BlockSpec((None, t, V_HEAD), lambda h, i: (1, 0, h))
    o = pl.BlockSpec((ATT_TQ, V_HEAD), lambda h, i: (i, h))
    return q, kn, kr, v, o


def _attn_fwd(q, knv, kr):
    t = q.shape[0]

    def body(q_ref, kn_ref, kr_ref, v_ref, o_ref):
        p = _attn_probs(q_ref[...], kn_ref[...], kr_ref[...], pl.program_id(1))
        o_ref[...] = jnp.dot(p.astype(BF16), v_ref[...], preferred_element_type=F32).astype(BF16)

    qs, kns, krs, vs, os_ = _attn_specs(t)
    return _tc_call(
        body, name="attn_fwd", grid=(N_HEADS, t // ATT_TQ), in_specs=[qs, kns, krs, vs], out_specs=os_,
        out_shape=jax.ShapeDtypeStruct((t, N_HEADS * V_HEAD), BF16), compiler_params=_cp("parallel", "parallel"),
    )(q, knv, kr, knv)


def _attn_bwd(q, knv, kr, do, cos, sin):
    t = q.shape[0]

    def body(q_ref, kn_ref, kr_ref, v_ref, do_ref, c_ref, s_ref, dq_ref, dknv_ref, dkr_ref):
        h, qi = pl.program_id(0), pl.program_id(1)
        qv, knv_, krv, dov = q_ref[...], kn_ref[...], kr_ref[...], do_ref[...]
        p = _attn_probs(qv, knv_, krv, qi)
        dp = lax.dot_general(dov, v_ref[...], NT_DIMS, preferred_element_type=F32)
        ds = (p * (dp - jnp.sum(p * dp, axis=1, keepdims=True)) * ATT_SCALE).astype(BF16)
        dq_ref[:, :QK_NOPE] = jnp.dot(ds, knv_, preferred_element_type=F32).astype(BF16)
        dqr = jnp.dot(ds, krv, preferred_element_type=F32)
        dq_ref[:, QK_NOPE:] = _rope_bwd_math(dqr, c_ref[...], s_ref[...]).astype(BF16)
        dv = lax.dot_general(p.astype(BF16), dov, TN_DIMS, preferred_element_type=F32)
        dkn = lax.dot_general(ds, qv[:, :QK_NOPE], TN_DIMS, preferred_element_type=F32)
        dkr = lax.dot_general(ds, qv[:, QK_NOPE:], TN_DIMS, preferred_element_type=F32)

        @pl.when(qi == 0)
        def _():
            dknv_ref[...] = jnp.zeros_like(dknv_ref)

        @pl.when((qi == 0) & (h == 0))
        def _():
            dkr_ref[...] = jnp.zeros_like(dkr_ref)

        dknv_ref[0] += dkn
        dknv_ref[1] += dv
        dkr_ref[...] += dkr

    qs, kns, krs, vs, os_ = _attn_specs(t)
    tab = pl.BlockSpec((ATT_TQ, LANES), lambda h, i: (i, 0))
    return _tc_call(
        body, name="attn_bwd", grid=(N_HEADS, t // ATT_TQ), in_specs=[qs, kns, krs, vs, os_, tab, tab],
        out_specs=[qs, pl.BlockSpec((2, t, QK_NOPE), lambda h, i: (0, 0, h)), krs],
        out_shape=[jax.ShapeDtypeStruct((t, N_HEADS * HEAD_PAD), BF16),
                   jax.ShapeDtypeStruct((2, t, N_HEADS * QK_NOPE), F32), jax.ShapeDtypeStruct((t, LANES), F32)],
        compiler_params=_cp("arbitrary", "arbitrary"),
    )(q, knv, kr, knv, do, cos, sin)


def _adam_math(w, g, m, v):
    nm = ADAM_B1 * m + (1.0 - ADAM_B1) * g
    nv = ADAM_B2 * v + (1.0 - ADAM_B2) * (g * g)
    m_hat = nm / (1.0 - ADAM_B1 ** ADAM_STEP)
    v_hat = nv / (1.0 - ADAM_B2 ** ADAM_STEP)
    return -ADAM_LR * (m_hat / (jnp.sqrt(v_hat) + ADAM_EPS) + ADAM_WD * w), nm, nv


def _adamw_small(w, g, m, v):
    def body(w_ref, g_ref, m_ref, v_ref, d_ref, nm_ref, nv_ref):
        d_ref[...], nm_ref[...], nv_ref[...] = _adam_math(w_ref[...], g_ref[...], m_ref[...], v_ref[...])

    shp = jax.ShapeDtypeStruct(w.shape, F32)
    return _tc_call(body, name="adamw_small", out_shape=[shp] * 3)(w, g, m, v)


ADAM_BLOCK_BYTES = 1 << 20


def _adamw_shard(ids, w, m, v, g_mine, g_sib, name, layer=None, prev=None):
    r, c = w.shape[-2:]
    half = r // 2
    tr = _tile(half, [d for d in range(half, 7, -8) if d * c * 4 <= ADAM_BLOCK_BYTES] or [8])
    nbh = half // tr

    def body(ids_ref, w_ref, m_ref, v_ref, gm_ref, gs_ref, *rest):
        g_ref, d_ref, nm_ref, nv_ref = rest[-4:]
        mine = (pl.program_id(0) // nbh) == ids_ref[0]

        @pl.when(mine)
        def _():
            g_ref[...] = gm_ref[...]

        @pl.when(jnp.logical_not(mine))
        def _():
            g_ref[...] = gs_ref[...]

        d_ref[...], nm_ref[...], nv_ref[...] = _adam_math(w_ref[...], g_ref[...], m_ref[...], v_ref[...])

    if layer is None:
        wspec = pl.BlockSpec((tr, c), lambda i, ids: (i, 0))
    else:
        wspec = pl.BlockSpec((None, tr, c), lambda i, ids: (layer, i, 0))
    gspec = pl.BlockSpec((tr, c), lambda i, ids: (i % nbh, 0))
    in_specs = [wspec] * 3 + [gspec] * 2
    args = [ids, w, m, v, g_mine, g_sib]
    aliases = {}
    if prev is not None:
        in_specs += [ANY] * 4
        args += list(prev)
        aliases = {6 + k: k for k in range(4)}
    return _tc_call(
        body, name=name, prefetch=1, grid=(r // tr,), in_specs=in_specs, out_specs=[wspec] * 4,
        out_shape=[jax.ShapeDtypeStruct(w.shape, F32)] * 4, input_output_aliases=aliases,
        compiler_params=_cp("parallel"),
    )(*args)


def _peer_chip(k_me, j):
    return k_me ^ jnp.where(j == 0, 2, jnp.where(j == 1, 1, 3))


def _pair_sum(ids, g, ra, name):
    _, r, c = g.shape
    half = r // 2

    def body(ids_ref, g_ref, ra_ref, o_ref):
        o_ref[...] = (g_ref[...].astype(F32) + ra_ref[...].astype(F32)).astype(BF16)

    return _tc_call(
        body, name=name, prefetch=1, grid=(3,),
        in_specs=[pl.BlockSpec((None, half, c), lambda j, ids: (_peer_chip(ids[1], j), ids[0], 0)),
                  pl.BlockSpec((None, half, c), lambda j, ids: (_peer_chip(ids[1], j), 0, 0))],
        out_specs=pl.BlockSpec((None, half, c), lambda j, ids: (j, 0, 0)),
        out_shape=jax.ShapeDtypeStruct((3, half, c), BF16), compiler_params=_cp("parallel"),
    )(ids, g, ra)


def _chip_sum(ids, g, ra, rb, name):
    _, r, c = g.shape
    half = r // 2

    def body(ids_ref, g_ref, ra_ref, rb_ref, o_ref):
        acc = g_ref[...].astype(F32) + ra_ref[...].astype(F32)
        for j in range(3):
            acc = acc + rb_ref[j].astype(F32)
        o_ref[...] = acc

    return _tc_call(
        body, name=name, prefetch=1, grid=(1,),
        in_specs=[pl.BlockSpec((None, half, c), lambda i, ids: (ids[1], ids[0], 0)),
                  pl.BlockSpec((None, half, c), lambda i, ids: (ids[1], 0, 0)),
                  pl.BlockSpec((3, half, c), lambda i, ids: (0, 0, 0))],
        out_specs=pl.BlockSpec((half, c), lambda i, ids: (0, 0)),
        out_shape=jax.ShapeDtypeStruct((half, c), F32), compiler_params=_cp("arbitrary"),
    )(ids, g, ra, rb)


def _position():
    x, y, c = lax.axis_index("x"), lax.axis_index("y"), lax.axis_index("c")
    chips = [(1 - x, y), (x, 1 - y), (1 - x, 1 - y)]
    return x, y, c, chips


def _shard_half(ref, wm, h):
    if wm.nl == 2:
        return ref.at[h]
    return ref.at[pl.ds(pl.multiple_of(h * (wm.k // 2), 16), wm.k // 2), :]


def _region(full, wm, s, h):
    cols = pl.ds(pl.multiple_of(s * wm.n, LANES), wm.n) if wm.kind == "col" else slice(None)
    if wm.nl == 2:
        rows = pl.ds(pl.multiple_of(s * wm.k, 16), wm.k) if wm.kind == "row" else slice(None)
        return full.at[slice(None) if h is None else h, rows, cols]
    if wm.kind == "col":
        rows = slice(None) if h is None else pl.ds(pl.multiple_of(h * (wm.k // 2), 16), wm.k // 2)
    elif h is None:
        rows = pl.ds(pl.multiple_of(s * wm.k, 16), wm.k)
    else:
        rows = pl.ds(pl.multiple_of(s * wm.k + h * (wm.k // 2), 16), wm.k // 2)
    return full.at[rows, cols]


def _full_shape(wm):
    shape = (wm.k, N_CHIPS * wm.n) if wm.kind == "col" else (N_CHIPS * wm.k, wm.n)
    return shape if wm.nl == 1 else (wm.nl,) + shape


def _handshake(peers):
    barrier = pltpu.get_barrier_semaphore()
    for peer in peers:
        pl.semaphore_signal(barrier, inc=1, device_id=peer, device_id_type=MESH)
    pl.semaphore_wait(barrier, len(peers))


def _all_gather_group(gi, shards):
    wms = AG_GROUPS[gi]
    nw = len(wms)

    def body(*refs):
        sh, full = refs[:nw], refs[nw:2 * nw]
        ici_s, ici_r, pass_s, pass_r, own_s, own_r = refs[2 * nw:]
        x, y, c, chips = _position()
        me, sibling = 2 * x + y, (x, y, 1 - c)
        _handshake([(*chip, c) for chip in chips] + [sibling])

        def rcopy(src, dst, s_sem, r_sem, to):
            return pltpu.make_async_remote_copy(src_ref=src, dst_ref=dst, send_sem=s_sem, recv_sem=r_sem,
                                                device_id=to, device_id_type=MESH)

        started = []
        for i, wm in enumerate(wms):
            for j, chip in enumerate(chips):
                started.append(rcopy(_shard_half(sh[i], wm, c), _region(full[i], wm, me, c),
                                     ici_s.at[i, j], ici_r.at[i, j], (*chip, c)))
                started[-1].start()
            started.append(rcopy(sh[i], _region(full[i], wm, me, None), own_s.at[i], own_r.at[i], sibling))
            started[-1].start()
        for i, wm in enumerate(wms):
            for j, chip in enumerate(chips):
                got = _region(full[i], wm, 2 * chip[0] + chip[1], c)
                rcopy(got, got, ici_s.at[i, j], ici_r.at[i, j], sibling).wait_recv()
                started.append(rcopy(got, got, pass_s.at[i, j], pass_r.at[i, j], sibling))
                started[-1].start()
        for i, wm in enumerate(wms):
            mine = _region(full[i], wm, me, None)
            rcopy(mine, mine, own_s.at[i], own_r.at[i], sibling).wait_recv()
            for j, chip in enumerate(chips):
                got = _region(full[i], wm, 2 * chip[0] + chip[1], 1 - c)
                rcopy(got, got, pass_s.at[i, j], pass_r.at[i, j], sibling).wait_recv()
        for cp in started:
            cp.wait_send()

    return pl.kernel(
        body, out_type=[jax.ShapeDtypeStruct(_full_shape(wm), BF16) for wm in wms],
        mesh=plsc.ScalarSubcoreMesh(axis_name="sequencer", num_cores=1), name=f"ag_group{gi}",
        scratch_types=[pltpu.SemaphoreType.DMA((nw, 3))] * 4 + [pltpu.SemaphoreType.DMA((nw,))] * 2,
        compiler_params=pltpu.CompilerParams(collective_id=gi),
    )(*shards)


def _sequencer_call(body, name, cid, out_types, scratch, args):
    return pl.kernel(
        body, out_type=out_types, mesh=plsc.ScalarSubcoreMesh(axis_name="sequencer", num_cores=1), name=name,
        scratch_types=scratch, compiler_params=pltpu.CompilerParams(collective_id=cid),
    )(*args)


def _pair_exchange(gs, tag, cid):
    n = len(gs)

    def body(*refs):
        g, out, send_sems, recv_sems = refs[:n], refs[n:2 * n], refs[2 * n], refs[2 * n + 1]
        x, y, c, _ = _position()
        _handshake([(x, y, 1 - c)])
        cps = []
        for i in range(n):
            half = g[i].shape[1] // 2
            cps.append(pltpu.make_async_remote_copy(
                src_ref=g[i].at[:, pl.ds(pl.multiple_of((1 - c) * half, 16), half), :], dst_ref=out[i],
                send_sem=send_sems.at[i], recv_sem=recv_sems.at[i], device_id=(x, y, 1 - c), device_id_type=MESH))
            cps[-1].start()
        for cp in cps:
            cp.wait()

    return _sequencer_call(
        body, f"rs_pair_exchange{tag}", cid,
        [jax.ShapeDtypeStruct((a.shape[0], a.shape[1] // 2, a.shape[2]), a.dtype) for a in gs],
        [pltpu.SemaphoreType.DMA((n,)), pltpu.SemaphoreType.DMA((n,))], gs)


def _chip_exchange(ss, tag, cid):
    n = len(ss)

    def body(*refs):
        s, out, send_sems, recv_sems = refs[:n], refs[n:2 * n], refs[2 * n], refs[2 * n + 1]
        x, y, c, chips = _position()
        _handshake([(*chip, c) for chip in chips])
        cps = []
        for i in range(n):
            for j, chip in enumerate(chips):
                cps.append(pltpu.make_async_remote_copy(
                    src_ref=s[i].at[j], dst_ref=out[i].at[j], send_sem=send_sems.at[i, j], recv_sem=recv_sems.at[i, j],
                    device_id=(*chip, c), device_id_type=MESH))
                cps[-1].start()
        for cp in cps:
            cp.wait()

    return _sequencer_call(
        body, f"rs_chip_exchange{tag}", cid, [jax.ShapeDtypeStruct(a.shape, a.dtype) for a in ss],
        [pltpu.SemaphoreType.DMA((n, 3)), pltpu.SemaphoreType.DMA((n, 3))], ss)


def _pair_swap(g8s, tag, cid):
    n = len(g8s)

    def body(*refs):
        g, out, send_sems, recv_sems = refs[:n], refs[n:2 * n], refs[2 * n], refs[2 * n + 1]
        x, y, c, _ = _position()
        _handshake([(x, y, 1 - c)])
        cps = []
        for i in range(n):
            cps.append(pltpu.make_async_remote_copy(
                src_ref=g[i], dst_ref=out[i], send_sem=send_sems.at[i], recv_sem=recv_sems.at[i],
                device_id=(x, y, 1 - c), device_id_type=MESH))
            cps[-1].start()
        for cp in cps:
            cp.wait()

    return _sequencer_call(
        body, f"rs_pair_swap{tag}", cid, [jax.ShapeDtypeStruct(a.shape, a.dtype) for a in g8s],
        [pltpu.SemaphoreType.DMA((n,)), pltpu.SemaphoreType.DMA((n,))], g8s)


def _all_reduce_small(vec, name):
    r, cols = vec.shape

    def body(v_ref, o_ref, gath, send_sems, recv_sems):
        x, y, c, _ = _position()
        me = 4 * x + 2 * y + c
        gath[me] = v_ref[...]
        cps = []
        for rel in range(1, N_DEV):
            peer = (x ^ (rel >> 2), y ^ ((rel >> 1) & 1), c ^ (rel & 1))
            cps.append(pltpu.make_async_remote_copy(
                src_ref=v_ref, dst_ref=gath.at[me], send_sem=send_sems.at[rel - 1], recv_sem=recv_sems.at[rel - 1],
                device_id=peer, device_id_type=MESH))
        for cp in cps:
            cp.start()
        for rel in range(1, N_DEV):
            pltpu.make_async_remote_copy(
                src_ref=v_ref, dst_ref=gath.at[me ^ rel], send_sem=send_sems.at[rel - 1],
                recv_sem=recv_sems.at[rel - 1], device_id=(x, y, c), device_id_type=MESH).wait_recv()
        for cp in cps:
            cp.wait_send()
        acc = gath[0]
        for d in range(1, N_DEV):
            acc = acc + gath[d]
        o_ref[...] = acc

    vm = pl.BlockSpec(memory_space=pltpu.VMEM)
    return _tc_call(
        body, name=name, in_specs=[vm], out_specs=vm, out_shape=jax.ShapeDtypeStruct((r, cols), F32),
        scratch_shapes=[pltpu.VMEM((N_DEV, r, cols), F32), pltpu.SemaphoreType.DMA((N_DEV - 1,)),
                        pltpu.SemaphoreType.DMA((N_DEV - 1,))],
    )(vec)


def _rope_tables(positions):
    half = QK_ROPE // 2
    inv_freq = 1.0 / (ROPE_THETA ** (jnp.arange(half, dtype=F32) / half))
    ang = positions.astype(F32)[:, None] * inv_freq
    zeros = jnp.zeros((positions.shape[0], LANES - QK_ROPE), F32)
    cos, sin = jnp.cos(ang), jnp.sin(ang)
    return jnp.concatenate([cos, cos, zeros], axis=1), jnp.concatenate([sin, sin, zeros], axis=1)


def _local_step(x, positions, tgt, wf, small, rs):
    cos, sin = _rope_tables(positions)
    w_in, w_out = wf["sc_w_in"], wf["sc_w_out"]
    w_ups, w_downs = (wf["ffn_w_up0"], wf["ffn_w_up1"]), (wf["ffn_w_down0"], wf["ffn_w_down1"])
    w_kv, w_ukv, w_dq, w_o = wf["w_kv"], wf["w_ukv"], wf["w_dq"], wf["w_o"]
    w_uq = jnp.pad(wf["w_uq"].reshape(Q_LORA, N_HEADS, QK_NOPE + QK_ROPE),
                   ((0, 0), (0, 0), (0, HEAD_PAD - QK_NOPE - QK_ROPE))).reshape(Q_LORA, N_HEADS * HEAD_PAD)
    attn_norm, ffn_norm = small["attn_norm"], small["ffn_norm"]
    conv_b = small["ffn_conv_b"]

    def ffn_fwd(h, l):
        hf = _rms_fwd(h, ffn_norm[l:l + 1], f"ffn{l}_norm")
        up = _nn_parts(f"ffn{l}_up", hf, w_ups[l], 2, F32)
        a = _gate_fwd(up, small["ffn_conv_w"][l], conv_b[l:l + 1], f"ffn{l}_gate")
        return _nn(f"ffn{l}_down", a, w_downs[l], F32, add=h), (hf, up, a)

    def ffn_bwd(h, dh_out, l, saved, gi, more):
        hf, up, a = saved
        da = _nt(f"ffn{l}_down_dx", dh_out, w_downs[l], BF16)
        d_down = _tn(f"ffn{l}_down_dw", a, dh_out, BF16)
        dup, d_cw, d_cb = _gate_bwd(up, small["ffn_conv_w"][l], conv_b[l:l + 1], da, f"ffn{l}_gate_bwd")
        d_up = _dw_ffn_up(f"ffn{l}_up_dw", hf, dup)
        rs.start(gi, {**more, f"ffn_w_down{l}": d_down.reshape(N_CHIPS, F_FF // N_CHIPS, D), f"ffn_w_up{l}": d_up})
        dhf = _nt_parts(f"ffn{l}_up_dx", dup, w_ups[l], BF16)
        dh, d_norm = _rms_bwd(h, ffn_norm[l:l + 1], dhf, dh_out, f"ffn{l}_norm_bwd")
        return dh, d_cw, d_cb, d_norm

    hn0 = _rms_fwd(x, attn_norm[0:1], "attn0_norm")
    z = _nn_parts("sc_in", hn0, w_in, 3, F32)
    mix = _scmix_fwd(z, small["sc_conv_w"])
    h1 = _nn("sc_out", mix, w_out, F32, add=x)
    h2, ffn0_saved = ffn_fwd(h1, 0)

    hk = _rms_fwd(h2, small["kv_in_norm"], "kv_in_norm")
    kvpre = _nn("kv_down", hk, w_kv, F32)
    ckv, kr = _kv_elem_fwd(kvpre, small["kv_latent_norm"], cos, sin)
    knv = _nn_parts("kv_up", ckv, w_ukv, 2, BF16, stacked=True)

    hn1 = _rms_fwd(h2, attn_norm[1:2], "attn1_norm")
    cq_pre = _nn("q_down", hn1, w_dq, F32)
    cq = _rms_fwd(cq_pre, small["q_latent_norm"], "q_latent_norm")
    q = _q_rope_fwd(_nn("q_up", cq, w_uq, F32), cos, sin)
    o = _attn_fwd(q, knv, kr)
    h3 = _nn("attn_out", o, w_o, F32, add=h2)
    h4, ffn1_saved = ffn_fwd(h3, 1)

    loss, dh4, d_final = _loss_head(h4, small["final_norm"], tgt)

    rows = D // N_CHIPS
    dh3, d_cw1, d_cb1, d_fn1 = ffn_bwd(h3, dh4, 1, ffn1_saved, 0, {})

    do = _nt("attn_out_dx", dh3, w_o, BF16)
    d_wo = _tn("attn_out_dw", o, dh3, BF16)
    rs.pair_sums(0)
    dq, dknv, dkr = _attn_bwd(q, knv, kr, do, cos, sin)
    rs.chip_sums(0)
    dcq = _nt("q_up_dx", dq, w_uq, F32)
    d_wuq = _tn("q_up_dw", cq, dq, BF16)
    d_wuq = d_wuq.reshape(Q_LORA, N_HEADS, HEAD_PAD)[:, :, :QK_NOPE + QK_ROPE]
    d_wuq = d_wuq.reshape(Q_LORA, N_CHIPS, -1).transpose(1, 0, 2)
    dcq_pre, d_qln = _rms_bwd(cq_pre, small["q_latent_norm"], dcq, None, "q_latent_norm_bwd")
    dhn1 = _nt("q_down_dx", dcq_pre, w_dq, BF16)
    d_wdq = _tn("q_down_dw", hn1, dcq_pre, BF16)
    dh2, d_an1 = _rms_bwd(h2, attn_norm[1:2], dhn1, dh3, "attn1_norm_bwd")

    dckv = _nt_parts("kv_up_dx", dknv, w_ukv, F32, stacked=True)
    d_wukv = _dw_ukv(ckv, dknv)
    dkvpre, d_kvln = _kv_elem_bwd(kvpre, small["kv_latent_norm"], dckv, dkr, cos, sin)
    dhk = _nt("kv_down_dx", dkvpre, w_kv, BF16)
    d_wkv = _tn("kv_down_dw", hk, dkvpre, BF16)
    dh2, d_kvin = _rms_bwd(h2, small["kv_in_norm"], dhk, dh2, "kv_in_norm_bwd")

    attn_grads = {
        "w_o": d_wo.reshape(N_CHIPS, rows, D), "w_uq": d_wuq, "w_dq": d_wdq.reshape(N_CHIPS, rows, Q_LORA),
        "w_ukv": d_wukv.reshape(N_CHIPS, 2 * KV_LORA, -1), "w_kv": d_wkv.reshape(N_CHIPS, rows, KVP),
    }
    dh1, d_cw0, d_cb0, d_fn0 = ffn_bwd(h1, dh2, 0, ffn0_saved, 1, attn_grads)

    d_wout = _tn("sc_out_dw", mix, dh1, BF16)
    dmix = _nt("sc_out_dx", dh1, w_out, BF16)
    rs.pair_sums(1)
    dz, d_scw = _scmix_bwd(z, small["sc_conv_w"], dmix)
    d_win = _dw_sc_in(hn0, dz)
    rs.start(2, {"sc_w_out": d_wout.reshape(N_CHIPS, rows, D), "sc_w_in": d_win})
    dhn0 = _nt_parts("sc_in_dx", dz, w_in, BF16)
    dx, d_an0 = _rms_bwd(x, attn_norm[0:1], dhn0, dh1, "attn0_norm_bwd")

    small_g = {
        "attn_norm": jnp.concatenate([d_an0, d_an1]), "ffn_norm": jnp.concatenate([d_fn0, d_fn1]),
        "final_norm": d_final, "kv_in_norm": d_kvin, "kv_latent_norm": d_kvln, "q_latent_norm": d_qln,
        "ffn_conv_b": jnp.concatenate([d_cb0, d_cb1]), "sc_conv_w": d_scw, "ffn_conv_w": jnp.stack([d_cw0, d_cw1]),
    }
    return loss, dx, small_g


RS_GROUPS = (("ffn_w_down1", "ffn_w_up1"),
             ("w_o", "w_uq", "w_dq", "w_ukv", "w_kv", "ffn_w_down0", "ffn_w_up0"),
             ("sc_w_out", "sc_w_in"))


class _ReduceScatter:
    def __init__(self, ids):
        self.ids, self.grads, self.step, self.mine, self.sib = ids, {}, {}, {}, {}

    def _cid(self, gi):
        return len(AG_GROUPS) + 3 * gi

    def start(self, gi, grads):
        self.grads.update(grads)
        own = [grads[n] for n in RS_GROUPS[gi]]
        self.step[gi] = (own, _pair_exchange(own, gi, self._cid(gi)))

    def pair_sums(self, gi):
        own, ra = self.step[gi]
        sums = [_pair_sum(self.ids, g, a, f"rs_pair_sum_{n}") for n, g, a in zip(RS_GROUPS[gi], own, ra)]
        self.step[gi] = (own, ra, _chip_exchange(sums, gi, self._cid(gi) + 1))

    def chip_sums(self, gi):
        own, ra, rb = self.step[gi]
        mine = [_chip_sum(self.ids, g, a, b, f"rs_chip_sum_{n}") for n, g, a, b in zip(RS_GROUPS[gi], own, ra, rb)]
        self.mine.update(zip(RS_GROUPS[gi], mine))
        self.sib.update(zip(RS_GROUPS[gi], _pair_swap(mine, gi, self._cid(gi) + 2)))

SMALL_REPL = ("attn_norm", "ffn_norm", "final_norm", "kv_in_norm", "kv_latent_norm", "q_latent_norm", "ffn_conv_b")
SMALL_SHARDED = ("sc_conv_w", "ffn_conv_w")
SMALL_ROWS = 256


def _pack_kv(w_dkv, w_kr):
    return jnp.concatenate([w_dkv, w_kr, jnp.zeros((w_kr.shape[0], LANES - QK_ROPE), w_kr.dtype)], axis=1)


def kernel(x, positions, attn_norm, ffn_norm, final_norm, sc_w_in, sc_conv_w, sc_w_out, kv_in_norm, w_dkv, kv_latent_norm, w_kr, w_uk, w_uv, w_dq, q_latent_norm, w_uq, w_o, ffn_w_up, ffn_conv_w, ffn_conv_b, ffn_w_down, loss_target, m_attn_norm, m_ffn_norm, m_final_norm, m_sc_w_in, m_sc_conv_w, m_sc_w_out, m_kv_in_norm, m_w_dkv, m_kv_latent_norm, m_w_kr, m_w_uk, m_w_uv, m_w_dq, m_q_latent_norm, m_w_uq, m_w_o, m_ffn_w_up, m_ffn_conv_w, m_ffn_conv_b, m_ffn_w_down, v_attn_norm, v_ffn_norm, v_final_norm, v_sc_w_in, v_sc_conv_w, v_sc_w_out, v_kv_in_norm, v_w_dkv, v_kv_latent_norm, v_w_kr, v_w_uk, v_w_uv, v_w_dq, v_q_latent_norm, v_w_uq, v_w_o, v_ffn_w_up, v_ffn_conv_w, v_ffn_conv_b, v_ffn_w_down):
    names = ("attn_norm", "ffn_norm", "final_norm", "sc_w_in", "sc_conv_w", "sc_w_out", "kv_in_norm", "w_dkv",
             "kv_latent_norm", "w_kr", "w_uk", "w_uv", "w_dq", "q_latent_norm", "w_uq", "w_o", "ffn_w_up",
             "ffn_conv_w", "ffn_conv_b", "ffn_w_down")
    w = dict(zip(names, (attn_norm, ffn_norm, final_norm, sc_w_in, sc_conv_w, sc_w_out, kv_in_norm, w_dkv,
                         kv_latent_norm, w_kr, w_uk, w_uv, w_dq, q_latent_norm, w_uq, w_o, ffn_w_up,
                         ffn_conv_w, ffn_conv_b, ffn_w_down)))
    m = dict(zip(names, (m_attn_norm, m_ffn_norm, m_final_norm, m_sc_w_in, m_sc_conv_w, m_sc_w_out, m_kv_in_norm,
                         m_w_dkv, m_kv_latent_norm, m_w_kr, m_w_uk, m_w_uv, m_w_dq, m_q_latent_norm, m_w_uq, m_w_o,
                         m_ffn_w_up, m_ffn_conv_w, m_ffn_conv_b, m_ffn_w_down)))
    v = dict(zip(names, (v_attn_norm, v_ffn_norm, v_final_norm, v_sc_w_in, v_sc_conv_w, v_sc_w_out, v_kv_in_norm,
                         v_w_dkv, v_kv_latent_norm, v_w_kr, v_w_uk, v_w_uv, v_w_dq, v_q_latent_norm, v_w_uq, v_w_o,
                         v_ffn_w_up, v_ffn_conv_w, v_ffn_conv_b, v_ffn_w_down)))

    _ORDER[0] = None
    ix, iy, ic = lax.axis_index("x"), lax.axis_index("y"), lax.axis_index("c")
    chip = 2 * ix + iy
    ids = jnp.stack([ic, chip]).astype(jnp.int32)

    def shards_of(t):
        return {
            "sc_w_in": t["sc_w_in"][0], "sc_w_out": t["sc_w_out"][0], "ffn_w_up": t["ffn_w_up"],
            "ffn_w_down": t["ffn_w_down"], "w_kv": _pack_kv(t["w_dkv"], t["w_kr"]),
            "w_ukv": jnp.stack([t["w_uk"], t["w_uv"]]), "w_dq": t["w_dq"][0], "w_uq": t["w_uq"][0], "w_o": t["w_o"][0],
        }

    ws, ms, vs = shards_of(w), shards_of(m), shards_of(v)

    def bf16_shard(name):
        if name[:-1] in ("ffn_w_up", "ffn_w_down"):
            return ws[name[:-1]][int(name[-1])].astype(BF16)
        return ws[name].astype(BF16)

    wf = {}
    for gi, wms in enumerate(AG_GROUPS):
        fulls = _all_gather_group(gi, [bf16_shard(wm.name) for wm in wms])
        wf.update({wm.name: f for wm, f in zip(wms, fulls)})

    def place(shard, full_cols):
        ns = shard.shape[-1]
        full = jnp.zeros(shard.shape[:-1] + (full_cols,), F32)
        return lax.dynamic_update_slice_in_dim(full, shard, chip * ns, axis=shard.ndim - 1)

    taps = jnp.concatenate([place(sc_conv_w[0], D).reshape(-1), place(ffn_conv_w, F_FF).reshape(-1)])
    n_taps = taps.shape[0]
    tap_rows = -(-n_taps // (8 * LANES)) * 8
    taps = jnp.pad(taps, (0, tap_rows * LANES - n_taps)).reshape(tap_rows, LANES)
    taps = _all_reduce_small(jnp.where(ic == 0, taps, 0.0), "ag_taps").reshape(-1)
    small = {
        "attn_norm": attn_norm, "ffn_norm": ffn_norm, "final_norm": final_norm[None], "kv_in_norm": kv_in_norm[None],
        "kv_latent_norm": kv_latent_norm[None], "q_latent_norm": q_latent_norm, "ffn_conv_b": ffn_conv_b,
        "sc_conv_w": taps[:3 * D].reshape(3, D), "ffn_conv_w": taps[3 * D:3 * D + 2 * 3 * F_FF].reshape(2, 3, F_FF),
    }

    rs = _ReduceScatter(ids)
    loss, dx, small_g = _local_step(x[0], positions[0], loss_target[0], wf, small, rs)
    rs.pair_sums(2)
    g_mine, g_sib = rs.mine, rs.sib

    def adamw_layer(n, layer, prev=None):
        key = f"{n}{layer}"
        return _adamw_shard(ids, ws[n], ms[n], vs[n], g_mine[key], g_sib[key], f"adamw_{key}", layer=layer, prev=prev)

    s_order = SMALL_REPL + SMALL_SHARDED
    flat = jnp.concatenate([small_g[n].reshape(-1) for n in s_order] + [loss.reshape(-1)])
    flat = jnp.pad(flat, (0, SMALL_ROWS * LANES - flat.shape[0])).reshape(SMALL_ROWS, LANES)
    red = _all_reduce_small(flat, "ar_small").reshape(-1)
    sg, off = {}, 0
    for n in s_order:
        sz = small_g[n].size
        sg[n] = red[off:off + sz].reshape(small_g[n].shape)
        off += sz
    loss_out = red[off]
    grads = {n: sg[n].reshape(w[n].shape) for n in SMALL_REPL}
    grads["sc_conv_w"] = lax.dynamic_slice_in_dim(sg["sc_conv_w"], chip * (D // N_CHIPS), D // N_CHIPS, axis=1)[None]
    grads["ffn_conv_w"] = lax.dynamic_slice_in_dim(sg["ffn_conv_w"], chip * (F_FF // N_CHIPS), F_FF // N_CHIPS, axis=2)

    res = {}
    second_layer = {n: adamw_layer(n, 1) for n in ("ffn_w_up", "ffn_w_down")}
    rs.chip_sums(1)
    rs.chip_sums(2)
    for n in ("ffn_w_up", "ffn_w_down"):
        res[n] = adamw_layer(n, 0, prev=second_layer[n])
    for n in ("w_kv", "w_dq", "w_uq", "w_o", "sc_w_out", "sc_w_in"):
        res[n] = _adamw_shard(ids, ws[n], ms[n], vs[n], g_mine[n], g_sib[n], f"adamw_{n}")
    merged = lambda a: a.reshape(2 * KV_LORA, -1)
    res["w_ukv"] = _adamw_shard(ids, merged(ws["w_ukv"]), merged(ms["w_ukv"]), merged(vs["w_ukv"]),
                                g_mine["w_ukv"], g_sib["w_ukv"], "adamw_w_ukv")
    outs = [grads, {}, {}, {}]
    for k, dst in enumerate(outs):
        for n in ("sc_w_in", "sc_w_out", "w_dq", "w_uq", "w_o"):
            dst[n] = res[n][k][None]
        dst["ffn_w_up"], dst["ffn_w_down"] = res["ffn_w_up"][k], res["ffn_w_down"][k]
        dst["w_dkv"], dst["w_kr"] = res["w_kv"][k][:, :KV_LORA], res["w_kv"][k][:, KV_LORA:KV_LORA + QK_ROPE]
        dst["w_uk"], dst["w_uv"] = res["w_ukv"][k][:KV_LORA], res["w_ukv"][k][KV_LORA:]
    grads, delta, new_m, new_v = outs

    small_names = SMALL_REPL + SMALL_SHARDED

    def pack_small(tree):
        return jnp.concatenate([tree[n].reshape(-1) for n in small_names]).reshape(-1, LANES)

    small_res = _adamw_small(pack_small(w), pack_small(grads), pack_small(m), pack_small(v))
    for slab, dst in zip(small_res, (delta, new_m, new_v)):
        f, off = slab.reshape(-1), 0
        for n in small_names:
            dst[n] = f[off:off + w[n].size].reshape(w[n].shape)
            off += w[n].size

    _ORDER[0] = None
    return (loss_out, dx[None], *[grads[n] for n in names], *[delta[n] for n in names],
            *[new_m[n] for n in names], *[new_v[n] for n in names])
```

```python
from typing import NamedTuple

import jax
import jax.numpy as jnp
from jax import lax
from jax.experimental import pallas as pl
from jax.experimental.pallas import tpu as pltpu
from jax.experimental.pallas import tpu_sc as plsc

F32 = jnp.float32
BF16 = jnp.bfloat16

T = 2048
D = 1024
F_FF = 2816
N_HEADS = 8
QK_NOPE = 128
QK_ROPE = 64
V_HEAD = 128
Q_LORA = 384
KV_LORA = 256
CHUNK_SHIFT = 6
ROPE_THETA = 10000.0
EPS = 1e-6
NEG_INF = -1e30
HEAD_PAD = 256
KVP = KV_LORA + 128

ADAM_LR = 0.001
ADAM_B1 = 0.9
ADAM_B2 = 0.999
ADAM_EPS = 1e-08
ADAM_WD = 0.01
ADAM_STEP = 10

N_CHIPS = 4
N_DEV = 8
LANES = 128
TC = 256
V7X_VMEM_LIMIT = 56 * 1024 * 1024

MESH = pl.DeviceIdType.MESH
ANY = pl.BlockSpec(memory_space=pl.ANY)


class _W(NamedTuple):
    name: str
    kind: str
    nl: int
    k: int
    n: int


AG_GROUPS = (
    (_W("sc_w_in", "col", 1, D, 3 * D // N_CHIPS), _W("sc_conv_w", "tiny", 1, 3, D // N_CHIPS),
     _W("ffn_conv_w", "tiny", 1, 6, F_FF // N_CHIPS)),
    (_W("sc_w_out", "row", 1, D // N_CHIPS, D),),
    (_W("ffn_w_up0", "col", 1, D, 2 * F_FF // N_CHIPS),),
    (_W("ffn_w_down0", "row", 1, F_FF // N_CHIPS, D),),
    (_W("w_kv", "row", 1, D // N_CHIPS, KVP), _W("w_ukv", "col", 2, KV_LORA, N_HEADS * QK_NOPE // N_CHIPS),
     _W("w_dq", "row", 1, D // N_CHIPS, Q_LORA),
     _W("w_uq", "col", 1, Q_LORA, N_HEADS * (QK_NOPE + QK_ROPE) // N_CHIPS),
     _W("w_o", "row", 1, N_HEADS * V_HEAD // N_CHIPS, D)),
    (_W("ffn_w_up1", "col", 1, D, 2 * F_FF // N_CHIPS), _W("ffn_w_down1", "row", 1, F_FF // N_CHIPS, D)),
)


def _cp(*sem):
    return pltpu.CompilerParams(dimension_semantics=sem, vmem_limit_bytes=V7X_VMEM_LIMIT)


_ORDER = [None]


def _tc_call(body, *, name, out_shape, in_specs=None, out_specs=None, grid=(), scratch_shapes=(), prefetch=0,
             input_output_aliases=None, compiler_params=None):
    def run(*args):
        specs = [pl.BlockSpec(memory_space=pltpu.VMEM)] * (len(args) - prefetch) if in_specs is None else list(in_specs)
        inner, dep = body, _ORDER[0]
        if dep is not None:
            unread = prefetch + len(specs)
            specs, args = specs + [ANY], (*args, dep)

            def inner(*refs):
                return body(*refs[:unread], *refs[unread + 1:])

        kwargs = dict(name=name, out_shape=out_shape, input_output_aliases=input_output_aliases or {},
                      compiler_params=compiler_params)
        if prefetch:
            kwargs["grid_spec"] = pltpu.PrefetchScalarGridSpec(
                num_scalar_prefetch=prefetch, grid=grid, in_specs=specs, out_specs=out_specs,
                scratch_shapes=scratch_shapes)
        else:
            kwargs.update(grid=grid, in_specs=specs, scratch_shapes=scratch_shapes)
            if out_specs is not None:
                kwargs["out_specs"] = out_specs
        out = pl.pallas_call(inner, **kwargs)(*args)
        _ORDER[0] = out[0] if isinstance(out, (list, tuple)) else out
        return out

    return run


def _tile(n, cands):
    for c in cands:
        if n % c == 0:
            return c
    raise ValueError(f"no tile for {n}")


NN_DIMS = (((1,), (0,)), ((), ()))
NT_DIMS = (((1,), (1,)), ((), ()))
TN_DIMS = (((0,), (0,)), ((), ()))
M_TILES = (1024, 512, 384, 256, 128)
N_TILES = (512, 384, 256, 128)


def _mm(name, a, b, dims, grid, a_spec, b_spec, o_spec, o_sds, add=None, red=None, acc_shape=None):
    n_red = None if red is None else grid[red]

    def body(*refs):
        a_ref, b_ref = refs[0], refs[1]
        add_ref = refs[2] if add is not None else None
        o_ref = refs[3] if add is not None else refs[2]
        part = lax.dot_general(a_ref[...].astype(BF16), b_ref[...].astype(BF16), dims, preferred_element_type=F32)
        if red is None:
            if add is not None:
                part = part + add_ref[...]
            o_ref[...] = part.astype(o_ref.dtype)
            return
        acc_ref = refs[-1]
        r = pl.program_id(red)

        @pl.when(r == 0)
        def _():
            acc_ref[...] = part

        @pl.when(r > 0)
        def _():
            acc_ref[...] += part

        @pl.when(r == n_red - 1)
        def _():
            o_ref[...] = acc_ref[...].astype(o_ref.dtype)

    sem = tuple("arbitrary" if ax == red else "parallel" for ax in range(len(grid)))
    in_specs = [a_spec, b_spec] + ([o_spec] if add is not None else [])
    args = (a, b) + ((add,) if add is not None else ())
    return _tc_call(
        body, name=name, grid=grid, in_specs=in_specs, out_specs=o_spec, out_shape=o_sds,
        scratch_shapes=[] if red is None else [pltpu.VMEM(acc_shape, F32)], compiler_params=_cp(*sem),
    )(*args)


def _nn(name, a, b, out_dtype, add=None, lead=None):
    (m, k), n = a.shape, b.shape[-1]
    tm, tn = _tile(m, M_TILES), _tile(n, N_TILES)
    if lead is None:
        b_spec = pl.BlockSpec((k, tn), lambda i, j: (0, j))
    else:
        b_spec = pl.BlockSpec((None, k, tn), lambda i, j: (lead, 0, j))
    return _mm(name, a, b, NN_DIMS, (m // tm, n // tn), pl.BlockSpec((tm, k), lambda i, j: (i, 0)), b_spec,
               pl.BlockSpec((tm, tn), lambda i, j: (i, j)), jax.ShapeDtypeStruct((m, n), out_dtype), add=add)


def _nn_parts(name, a, b, parts, out_dtype, lead=None, stacked=False):
    m, k = a.shape
    c = b.shape[-1] if stacked else b.shape[-1] // parts
    tm, tn = _tile(m, M_TILES), _tile(c, N_TILES)
    nb = c // tn
    if stacked:
        b_spec = pl.BlockSpec((None, k, tn), lambda i, p, j: (p, 0, j))
    elif lead is None:
        b_spec = pl.BlockSpec((k, tn), lambda i, p, j: (0, p * nb + j))
    else:
        b_spec = pl.BlockSpec((None, k, tn), lambda i, p, j: (lead, 0, p * nb + j))
    return _mm(name, a, b, NN_DIMS, (m // tm, parts, nb), pl.BlockSpec((tm, k), lambda i, p, j: (i, 0)), b_spec,
               pl.BlockSpec((None, tm, tn), lambda i, p, j: (p, i, j)), jax.ShapeDtypeStruct((parts, m, c), out_dtype))


def _nt(name, a, b, out_dtype, lead=None):
    (m, k), n = a.shape, b.shape[-2]
    tm, tn = _tile(m, M_TILES), _tile(n, N_TILES)
    if lead is None:
        b_spec = pl.BlockSpec((tn, k), lambda i, j: (j, 0))
    else:
        b_spec = pl.BlockSpec((None, tn, k), lambda i, j: (lead, j, 0))
    return _mm(name, a, b, NT_DIMS, (m // tm, n // tn), pl.BlockSpec((tm, k), lambda i, j: (i, 0)), b_spec,
               pl.BlockSpec((tm, tn), lambda i, j: (i, j)), jax.ShapeDtypeStruct((m, n), out_dtype))


def _nt_parts(name, a, b, out_dtype, lead=None, stacked=False):
    parts, m, c = a.shape
    n = b.shape[-2]
    tm, tn = _tile(m, M_TILES), _tile(n, N_TILES)
    if stacked:
        b_spec = pl.BlockSpec((None, tn, c), lambda i, j, p: (p, j, 0))
    elif lead is None:
        b_spec = pl.BlockSpec((tn, c), lambda i, j, p: (j, p))
    else:
        b_spec = pl.BlockSpec((None, tn, c), lambda i, j, p: (lead, j, p))
    return _mm(name, a, b, NT_DIMS, (m // tm, n // tn, parts), pl.BlockSpec((None, tm, c), lambda i, j, p: (p, i, 0)),
               b_spec, pl.BlockSpec((tm, tn), lambda i, j, p: (i, j)), jax.ShapeDtypeStruct((m, n), out_dtype),
               red=2, acc_shape=(tm, tn))


def _tn(name, a, b, out_dtype):
    (k, m), n = a.shape, b.shape[1]
    tm, tn = _tile(m, M_TILES), _tile(n, N_TILES)
    return _mm(name, a, b, TN_DIMS, (m // tm, n // tn), pl.BlockSpec((k, tm), lambda i, j: (0, i)),
               pl.BlockSpec((k, tn), lambda i, j: (0, j)), pl.BlockSpec((tm, tn), lambda i, j: (i, j)),
               jax.ShapeDtypeStruct((m, n), out_dtype))


def _dw_sc_in(hn, dz):
    t, tn, tm = hn.shape[0], TC, 512
    per_part, per_chip = D // tn, 3 * D // N_CHIPS // tn
    return _mm("sc_in_dw", hn, dz, TN_DIMS, (D // tm, 3 * D // tn), pl.BlockSpec((t, tm), lambda i, j: (0, i)),
               pl.BlockSpec((None, t, tn), lambda i, j: (j // per_part, 0, j % per_part)),
               pl.BlockSpec((None, tm, tn), lambda i, j: (j // per_chip, i, j % per_chip)),
               jax.ShapeDtypeStruct((N_CHIPS, D, 3 * D // N_CHIPS), BF16))


def _dw_ffn_up(name, hf, dup):
    t, tm, ns = hf.shape[0], 512, 2 * F_FF // N_CHIPS
    return _mm(name, hf, dup, TN_DIMS, (N_CHIPS, D // tm), pl.BlockSpec((t, tm), lambda s, i: (0, i)),
               pl.BlockSpec((None, t, ns), lambda s, i: (s // 2, 0, s % 2)),
               pl.BlockSpec((None, tm, ns), lambda s, i: (s, i, 0)), jax.ShapeDtypeStruct((N_CHIPS, D, ns), BF16))


def _dw_ukv(ckv, dknv):
    t, ns = ckv.shape[0], N_HEADS * QK_NOPE // N_CHIPS
    return _mm("kv_up_dw", ckv, dknv, TN_DIMS, (2, N_CHIPS), pl.BlockSpec((t, KV_LORA), lambda p, s: (0, 0)),
               pl.BlockSpec((None, t, ns), lambda p, s: (p, 0, s)),
               pl.BlockSpec((None, None, KV_LORA, ns), lambda p, s: (s, p, 0, 0)),
               jax.ShapeDtypeStruct((N_CHIPS, 2, KV_LORA, ns), BF16))


def _rms_fwd(x, g, name):
    t, d = x.shape
    tr = 512

    def body(x_ref, g_ref, o_ref):
        xv = x_ref[...]
        r = lax.rsqrt(jnp.mean(xv * xv, axis=1, keepdims=True) + EPS)
        o_ref[...] = (xv * r * g_ref[...]).astype(o_ref.dtype)

    row = pl.BlockSpec((tr, d), lambda i: (i, 0))
    return _tc_call(
        body, name=name, grid=(t // tr,), in_specs=[row, pl.BlockSpec((1, d), lambda i: (0, 0))],
        out_specs=row, out_shape=jax.ShapeDtypeStruct((t, d), BF16), compiler_params=_cp("parallel"),
    )(x, g)


def _rms_bwd_math(xv, g, dy):
    r = lax.rsqrt(jnp.mean(xv * xv, axis=1, keepdims=True) + EPS)
    xh = xv * r
    gy = dy * g
    dx = r * (gy - xh * jnp.mean(gy * xh, axis=1, keepdims=True))
    dg = jnp.sum(dy * xh, axis=0, keepdims=True)
    return dx, dg


def _rms_bwd(x, g, dy, add, name):
    t, d = x.shape
    tr = 512

    def body(*refs):
        if add is None:
            x_ref, g_ref, dy_ref, dx_ref, dg_ref = refs
        else:
            x_ref, g_ref, dy_ref, add_ref, dx_ref, dg_ref = refs
        dx, dg = _rms_bwd_math(x_ref[...], g_ref[...], dy_ref[...].astype(F32))
        if add is not None:
            dx = dx + add_ref[...]
        dx_ref[...] = dx

        @pl.when(pl.program_id(0) == 0)
        def _():
            dg_ref[...] = jnp.zeros_like(dg_ref)

        dg_ref[...] += dg

    row = pl.BlockSpec((tr, d), lambda i: (i, 0))
    vec = pl.BlockSpec((1, d), lambda i: (0, 0))
    in_specs = [row, vec, row] + ([row] if add is not None else [])
    args = (x, g, dy) + ((add,) if add is not None else ())
    return _tc_call(
        body, name=name, grid=(t // tr,), in_specs=in_specs, out_specs=[row, vec],
        out_shape=[jax.ShapeDtypeStruct((t, d), F32), jax.ShapeDtypeStruct((1, d), F32)],
        compiler_params=_cp("arbitrary"),
    )(*args)


def _loss_head(h, g, tgt):
    t, d = h.shape
    tr = 512

    def body(h_ref, g_ref, t_ref, loss_ref, dh_ref, dg_ref):
        xv = h_ref[...]
        gv = g_ref[...]
        r = lax.rsqrt(jnp.mean(xv * xv, axis=1, keepdims=True) + EPS)
        err = xv * r * gv - t_ref[...]
        part = 0.5 * jnp.sum(jnp.mean(err * err, axis=1, keepdims=True), axis=0, keepdims=True)
        dx, dg = _rms_bwd_math(xv, gv, err * (1.0 / d))
        dh_ref[...] = dx

        @pl.when(pl.program_id(0) == 0)
        def _():
            dg_ref[...] = jnp.zeros_like(dg_ref)
            loss_ref[...] = jnp.zeros_like(loss_ref)

        dg_ref[...] += dg
        loss_ref[...] += jnp.broadcast_to(part, loss_ref.shape)

    row = pl.BlockSpec((tr, d), lambda i: (i, 0))
    vec = pl.BlockSpec((1, d), lambda i: (0, 0))
    lspec = pl.BlockSpec((1, LANES), lambda i: (0, 0))
    return _tc_call(
        body, name="loss_head", grid=(t // tr,), in_specs=[row, vec, row], out_specs=[lspec, row, vec],
        out_shape=[jax.ShapeDtypeStruct((1, LANES), F32), jax.ShapeDtypeStruct((t, d), F32),
                   jax.ShapeDtypeStruct((1, d), F32)],
        compiler_params=_cp("arbitrary"),
    )(h, g, tgt)


def _rot_half(x):
    lane = lax.broadcasted_iota(jnp.int32, x.shape, 1)
    return jnp.where((lane % QK_ROPE) < QK_ROPE // 2, -pltpu.roll(x, LANES - 32, axis=1),
                     pltpu.roll(x, 32, axis=1))


def _rope_fwd_math(x, cos, sin):
    return x * cos + _rot_half(x) * sin


def _rope_bwd_math(dy, cos, sin):
    return dy * cos - _rot_half(dy * sin)


def _q_rope_fwd(qpre, cos, sin):
    t, w = qpre.shape
    tr = 256

    def body(q_ref, c_ref, s_ref, o_ref):
        cv, sv = c_ref[...], s_ref[...]
        for h in range(N_HEADS):
            lo = h * HEAD_PAD
            o_ref[:, lo:lo + QK_NOPE] = q_ref[:, lo:lo + QK_NOPE].astype(BF16)
            o_ref[:, lo + QK_NOPE:lo + HEAD_PAD] = _rope_fwd_math(
                q_ref[:, lo + QK_NOPE:lo + HEAD_PAD], cv, sv).astype(BF16)

    row = pl.BlockSpec((tr, w), lambda i: (i, 0))
    tab = pl.BlockSpec((tr, LANES), lambda i: (i, 0))
    return _tc_call(
        body, name="q_rope_fwd", grid=(t // tr,), in_specs=[row, tab, tab], out_specs=row,
        out_shape=jax.ShapeDtypeStruct((t, w), BF16), compiler_params=_cp("parallel"),
    )(qpre, cos, sin)


def _kv_elem_fwd(kvpre, g, cos, sin):
    t = kvpre.shape[0]
    tr = 512

    def body(p_ref, g_ref, c_ref, s_ref, ckv_ref, kr_ref):
        lat = p_ref[:, :KV_LORA]
        r = lax.rsqrt(jnp.mean(lat * lat, axis=1, keepdims=True) + EPS)
        ckv_ref[...] = (lat * r * g_ref[...]).astype(BF16)
        kr_ref[...] = _rope_fwd_math(p_ref[:, KV_LORA:], c_ref[...], s_ref[...]).astype(BF16)

    tab = pl.BlockSpec((tr, LANES), lambda i: (i, 0))
    return _tc_call(
        body, name="kv_elem_fwd", grid=(t // tr,),
        in_specs=[pl.BlockSpec((tr, KVP), lambda i: (i, 0)), pl.BlockSpec((1, KV_LORA), lambda i: (0, 0)), tab, tab],
        out_specs=[pl.BlockSpec((tr, KV_LORA), lambda i: (i, 0)), tab],
        out_shape=[jax.ShapeDtypeStruct((t, KV_LORA), BF16), jax.ShapeDtypeStruct((t, LANES), BF16)],
        compiler_params=_cp("parallel"),
    )(kvpre, g, cos, sin)


def _kv_elem_bwd(kvpre, g, dckv, dkr, cos, sin):
    t = kvpre.shape[0]
    tr = 512

    def body(p_ref, g_ref, dc_ref, dk_ref, c_ref, s_ref, dp_ref, dg_ref):
        dlat, dg = _rms_bwd_math(p_ref[:, :KV_LORA], g_ref[...], dc_ref[...])
        dp_ref[:, :KV_LORA] = dlat.astype(BF16)
        dp_ref[:, KV_LORA:] = _rope_bwd_math(dk_ref[...], c_ref[...], s_ref[...]).astype(BF16)

        @pl.when(pl.program_id(0) == 0)
        def _():
            dg_ref[...] = jnp.zeros_like(dg_ref)

        dg_ref[...] += dg

    tab = pl.BlockSpec((tr, LANES), lambda i: (i, 0))
    pre = pl.BlockSpec((tr, KVP), lambda i: (i, 0))
    vec = pl.BlockSpec((1, KV_LORA), lambda i: (0, 0))
    return _tc_call(
        body, name="kv_elem_bwd", grid=(t // tr,),
        in_specs=[pre, vec, pl.BlockSpec((tr, KV_LORA), lambda i: (i, 0)), tab, tab, tab],
        out_specs=[pre, vec],
        out_shape=[jax.ShapeDtypeStruct((t, KVP), BF16), jax.ShapeDtypeStruct((1, KV_LORA), F32)],
        compiler_params=_cp("arbitrary"),
    )(kvpre, g, dckv, dkr, cos, sin)


def _shift_down(x, k):
    row = lax.broadcasted_iota(jnp.int32, x.shape, 0)
    return jnp.where(row >= k, pltpu.roll(x, k, axis=0), 0.0)


def _shift_up(x, k):
    n = x.shape[0]
    row = lax.broadcasted_iota(jnp.int32, x.shape, 0)
    return jnp.where(row < n - k, pltpu.roll(x, n - k, axis=0), 0.0)


def _conv3(x, w_ref):
    return _shift_down(x, 2) * w_ref[0:1, :] + _shift_down(x, 1) * w_ref[1:2, :] + x * w_ref[2:3, :]


def _conv3_t(dy, w_ref):
    return dy * w_ref[2:3, :] + _shift_up(dy, 1) * w_ref[1:2, :] + _shift_up(dy, 2) * w_ref[0:1, :]


def _conv3_dw(dy, x, dw_ref):
    dw_ref[0:1, :] = jnp.sum(dy * _shift_down(x, 2), axis=0, keepdims=True)
    dw_ref[1:2, :] = jnp.sum(dy * _shift_down(x, 1), axis=0, keepdims=True)
    dw_ref[2:3, :] = jnp.sum(dy * x, axis=0, keepdims=True)


def _col(parts, t):
    if parts is None:
        return pl.BlockSpec((t, TC), lambda j: (0, j))
    return pl.BlockSpec((parts, t, TC), lambda j: (0, 0, j))


def _scmix_fwd(z, w):
    t = z.shape[1]

    def body(z_ref, w_ref, m_ref):
        b, c, u = (z_ref[p].astype(F32) for p in range(3))
        m_ref[...] = (b * _conv3(c * u, w_ref)).astype(BF16)

    return _tc_call(
        body, name="scmix_fwd", grid=(D // TC,), in_specs=[_col(3, t), pl.BlockSpec((3, TC), lambda j: (0, j))],
        out_specs=_col(None, t), out_shape=jax.ShapeDtypeStruct((t, D), BF16), compiler_params=_cp("parallel"),
    )(z, w)


def _scmix_bwd(z, w, dm):
    t = z.shape[1]

    def body(z_ref, w_ref, dm_ref, dz_ref, dw_ref):
        c, u = z_ref[1].astype(F32), z_ref[2].astype(F32)
        cu = c * u
        dmv = dm_ref[...].astype(F32)
        dz_ref[0] = (dmv * _conv3(cu, w_ref)).astype(BF16)
        dcv = dmv * z_ref[0].astype(F32)
        _conv3_dw(dcv, cu, dw_ref)
        dcu = _conv3_t(dcv, w_ref)
        dz_ref[1] = (dcu * u).astype(BF16)
        dz_ref[2] = (dcu * c).astype(BF16)

    wspec = pl.BlockSpec((3, TC), lambda j: (0, j))
    return _tc_call(
        body, name="scmix_bwd", grid=(D // TC,), in_specs=[_col(3, t), wspec, _col(None, t)],
        out_specs=[_col(3, t), wspec],
        out_shape=[jax.ShapeDtypeStruct((3, t, D), BF16), jax.ShapeDtypeStruct((3, D), F32)],
        compiler_params=_cp("parallel"),
    )(z, w, dm)


def _gate_fwd(up, w, bias, name):
    t = up.shape[1]

    def body(u_ref, w_ref, b_ref, a_ref):
        gc = _conv3(u_ref[0].astype(F32), w_ref) + b_ref[...]
        a_ref[...] = (gc * jax.nn.sigmoid(gc) * u_ref[1].astype(F32)).astype(BF16)

    return _tc_call(
        body, name=name, grid=(F_FF // TC,),
        in_specs=[_col(2, t), pl.BlockSpec((3, TC), lambda j: (0, j)), pl.BlockSpec((1, TC), lambda j: (0, j))],
        out_specs=_col(None, t), out_shape=jax.ShapeDtypeStruct((t, F_FF), BF16), compiler_params=_cp("parallel"),
    )(up, w, bias)


def _gate_bwd(up, w, bias, da, name):
    t = up.shape[1]

    def body(u_ref, w_ref, b_ref, da_ref, du_ref, dw_ref, db_ref):
        g = u_ref[0].astype(F32)
        gc = _conv3(g, w_ref) + b_ref[...]
        sg = jax.nn.sigmoid(gc)
        dav = da_ref[...].astype(F32)
        du_ref[1] = (dav * (gc * sg)).astype(BF16)
        dgc = dav * u_ref[1].astype(F32) * (sg * (1.0 + gc * (1.0 - sg)))
        db_ref[...] = jnp.sum(dgc, axis=0, keepdims=True)
        _conv3_dw(dgc, g, dw_ref)
        du_ref[0] = _conv3_t(dgc, w_ref).astype(BF16)

    wspec = pl.BlockSpec((3, TC), lambda j: (0, j))
    bspec = pl.BlockSpec((1, TC), lambda j: (0, j))
    return _tc_call(
        body, name=name, grid=(F_FF // TC,), in_specs=[_col(2, t), wspec, bspec, _col(None, t)],
        out_specs=[_col(2, t), wspec, bspec],
        out_shape=[jax.ShapeDtypeStruct((2, t, F_FF), BF16), jax.ShapeDtypeStruct((3, F_FF), F32),
                   jax.ShapeDtypeStruct((1, F_FF), F32)],
        compiler_params=_cp("parallel"),
    )(up, w, bias, da)


ATT_TQ = 256
ATT_SCALE = (QK_NOPE + QK_ROPE) ** -0.5


def _attn_probs(q, kn, kr, qi):
    s = lax.dot_general(q[:, :QK_NOPE], kn, NT_DIMS, preferred_element_type=F32)
    s = s + lax.dot_general(q[:, QK_NOPE:], kr, NT_DIMS, preferred_element_type=F32)
    s = s * ATT_SCALE
    row = qi * ATT_TQ + lax.broadcasted_iota(jnp.int32, s.shape, 0)
    col = lax.broadcasted_iota(jnp.int32, s.shape, 1)
    s = jnp.where(lax.shift_right_logical(col, CHUNK_SHIFT) <= lax.shift_right_logical(row, CHUNK_SHIFT), s, NEG_INF)
    p = jnp.exp(s - jnp.max(s, axis=1, keepdims=True))
    return p * (1.0 / jnp.sum(p, axis=1, keepdims=True))


def _attn_specs(t):
    q = pl.BlockSpec((ATT_TQ, HEAD_PAD), lambda h, i: (i, h))
    kn = pl.BlockSpec((None, t, QK_NOPE), lambda h, i: (0, 0, h))
    kr = pl.BlockSpec((t, LANES), lambda h, i: (0, 0))
    v = pl.BlockSpec((None, t, V_HEAD), lambda h, i: (1, 0, h))
    o = pl.BlockSpec((ATT_TQ, V_HEAD), lambda h, i: (i, h))
    return q, kn, kr, v, o


def _attn_fwd(q, knv, kr):
    t = q.shape[0]

    def body(q_ref, kn_ref, kr_ref, v_ref, o_ref):
        p = _attn_probs(q_ref[...], kn_ref[...], kr_ref[...], pl.program_id(1))
        o_ref[...] = jnp.dot(p.astype(BF16), v_ref[...], preferred_element_type=F32).astype(BF16)

    qs, kns, krs, vs, os_ = _attn_specs(t)
    return _tc_call(
        body, name="attn_fwd", grid=(N_HEADS, t // ATT_TQ), in_specs=[qs, kns, krs, vs], out_specs=os_,
        out_shape=jax.ShapeDtypeStruct((t, N_HEADS * V_HEAD), BF16), compiler_params=_cp("parallel", "parallel"),
    )(q, knv, kr, knv)


def _attn_bwd(q, knv, kr, do, cos, sin):
    t = q.shape[0]

    def body(q_ref, kn_ref, kr_ref, v_ref, do_ref, c_ref, s_ref, dq_ref, dknv_ref, dkr_ref):
        h, qi = pl.program_id(0), pl.program_id(1)
        qv, knv_, krv, dov = q_ref[...], kn_ref[...], kr_ref[...], do_ref[...]
        p = _attn_probs(qv, knv_, krv, qi)
        dp = lax.dot_general(dov, v_ref[...], NT_DIMS, preferred_element_type=F32)
        ds = (p * (dp - jnp.sum(p * dp, axis=1, keepdims=True)) * ATT_SCALE).astype(BF16)
        dq_ref[:, :QK_NOPE] = jnp.dot(ds, knv_, preferred_element_type=F32).astype(BF16)
        dqr = jnp.dot(ds, krv, preferred_element_type=F32)
        dq_ref[:, QK_NOPE:] = _rope_bwd_math(dqr, c_ref[...], s_ref[...]).astype(BF16)
        dv = lax.dot_general(p.astype(BF16), dov, TN_DIMS, preferred_element_type=F32)
        dkn = lax.dot_general(ds, qv[:, :QK_NOPE], TN_DIMS, preferred_element_type=F32)
        dkr = lax.dot_general(ds, qv[:, QK_NOPE:], TN_DIMS, preferred_element_type=F32)

        @pl.when(qi == 0)
        def _():
            dknv_ref[...] = jnp.zeros_like(dknv_ref)

        @pl.when((qi == 0) & (h == 0))
        def _():
            dkr_ref[...] = jnp.zeros_like(dkr_ref)

        dknv_ref[0] += dkn
        dknv_ref[1] += dv
        dkr_ref[...] += dkr

    qs, kns, krs, vs, os_ = _attn_specs(t)
    tab = pl.BlockSpec((ATT_TQ, LANES), lambda h, i: (i, 0))
    return _tc_call(
        body, name="attn_bwd", grid=(N_HEADS, t // ATT_TQ), in_specs=[qs, kns, krs, vs, os_, tab, tab],
        out_specs=[qs, pl.BlockSpec((2, t, QK_NOPE), lambda h, i: (0, 0, h)), krs],
        out_shape=[jax.ShapeDtypeStruct((t, N_HEADS * HEAD_PAD), BF16),
                   jax.ShapeDtypeStruct((2, t, N_HEADS * QK_NOPE), F32), jax.ShapeDtypeStruct((t, LANES), F32)],
        compiler_params=_cp("arbitrary", "arbitrary"),
    )(q, knv, kr, knv, do, cos, sin)


def _adam_math(w, g, m, v):
    nm = ADAM_B1 * m + (1.0 - ADAM_B1) * g
    nv = ADAM_B2 * v + (1.0 - ADAM_B2) * (g * g)
    m_hat = nm / (1.0 - ADAM_B1 ** ADAM_STEP)
    v_hat = nv / (1.0 - ADAM_B2 ** ADAM_STEP)
    return -ADAM_LR * (m_hat / (jnp.sqrt(v_hat) + ADAM_EPS) + ADAM_WD * w), nm, nv


def _adamw_small(w, g, m, v):
    def body(w_ref, g_ref, m_ref, v_ref, d_ref, nm_ref, nv_ref):
        d_ref[...], nm_ref[...], nv_ref[...] = _adam_math(w_ref[...], g_ref[...], m_ref[...], v_ref[...])

    shp = jax.ShapeDtypeStruct(w.shape, F32)
    return _tc_call(body, name="adamw_small", out_shape=[shp] * 3)(w, g, m, v)


ADAM_BLOCK_BYTES = 1 << 20


def _adamw_shard(ids, w, m, v, g_mine, g_sib, name, layer=None, prev=None):
    r, c = w.shape[-2:]
    half = r // 2
    tr = _tile(half, [d for d in range(half, 7, -8) if d * c * 4 <= ADAM_BLOCK_BYTES] or [8])
    nbh = half // tr

    def body(ids_ref, w_ref, m_ref, v_ref, gm_ref, gs_ref, *rest):
        g_ref, d_ref, nm_ref, nv_ref = rest[-4:]
        mine = (pl.program_id(0) // nbh) == ids_ref[0]

        @pl.when(mine)
        def _():
            g_ref[...] = gm_ref[...]

        @pl.when(jnp.logical_not(mine))
        def _():
            g_ref[...] = gs_ref[...]

        d_ref[...], nm_ref[...], nv_ref[...] = _adam_math(w_ref[...], g_ref[...], m_ref[...], v_ref[...])

    if layer is None:
        wspec = pl.BlockSpec((tr, c), lambda i, ids: (i, 0))
    else:
        wspec = pl.BlockSpec((None, tr, c), lambda i, ids: (layer, i, 0))
    gspec = pl.BlockSpec((tr, c), lambda i, ids: (i % nbh, 0))
    in_specs = [wspec] * 3 + [gspec] * 2
    args = [ids, w, m, v, g_mine, g_sib]
    aliases = {}
    if prev is not None:
        in_specs += [ANY] * 4
        args += list(prev)
        aliases = {6 + k: k for k in range(4)}
    return _tc_call(
        body, name=name, prefetch=1, grid=(r // tr,), in_specs=in_specs, out_specs=[wspec] * 4,
        out_shape=[jax.ShapeDtypeStruct(w.shape, F32)] * 4, input_output_aliases=aliases,
        compiler_params=_cp("parallel"),
    )(*args)


def _peer_chip(k_me, j):
    return k_me ^ jnp.where(j == 0, 2, jnp.where(j == 1, 1, 3))


def _pair_sum(ids, g, ra, name):
    _, r, c = g.shape
    half = r // 2

    def body(ids_ref, g_ref, ra_ref, o_ref):
        o_ref[...] = (g_ref[...].astype(F32) + ra_ref[...].astype(F32)).astype(BF16)

    return _tc_call(
        body, name=name, prefetch=1, grid=(3,),
        in_specs=[pl.BlockSpec((None, half, c), lambda j, ids: (_peer_chip(ids[1], j), ids[0], 0)),
                  pl.BlockSpec((None, half, c), lambda j, ids: (_peer_chip(ids[1], j), 0, 0))],
        out_specs=pl.BlockSpec((None, half, c), lambda j, ids: (j, 0, 0)),
        out_shape=jax.ShapeDtypeStruct((3, half, c), BF16), compiler_params=_cp("parallel"),
    )(ids, g, ra)


def _chip_sum(ids, g, ra, rb, name):
    _, r, c = g.shape
    half = r // 2

    def body(ids_ref, g_ref, ra_ref, rb_ref, o_ref):
        acc = g_ref[...].astype(F32) + ra_ref[...].astype(F32)
        for j in range(3):
            acc = acc + rb_ref[j].astype(F32)
        o_ref[...] = acc

    return _tc_call(
        body, name=name, prefetch=1, grid=(1,),
        in_specs=[pl.BlockSpec((None, half, c), lambda i, ids: (ids[1], ids[0], 0)),
                  pl.BlockSpec((None, half, c), lambda i, ids: (ids[1], 0, 0)),
                  pl.BlockSpec((3, half, c), lambda i, ids: (0, 0, 0))],
        out_specs=pl.BlockSpec((half, c), lambda i, ids: (0, 0)),
        out_shape=jax.ShapeDtypeStruct((half, c), F32), compiler_params=_cp("arbitrary"),
    )(ids, g, ra, rb)


def _position():
    x, y, c = lax.axis_index("x"), lax.axis_index("y"), lax.axis_index("c")
    chips = [(1 - x, y), (x, 1 - y), (1 - x, 1 - y)]
    return x, y, c, chips


def _shard_half(ref, wm, h):
    if wm.kind == "tiny":
        return ref
    if wm.nl == 2:
        return ref.at[h]
    return ref.at[pl.ds(pl.multiple_of(h * (wm.k // 2), 16), wm.k // 2), :]


def _region(full, wm, s, h):
    if wm.kind == "tiny":
        return full.at[s]
    cols = pl.ds(pl.multiple_of(s * wm.n, LANES), wm.n) if wm.kind == "col" else slice(None)
    if wm.nl == 2:
        rows = pl.ds(pl.multiple_of(s * wm.k, 16), wm.k) if wm.kind == "row" else slice(None)
        return full.at[slice(None) if h is None else h, rows, cols]
    if wm.kind == "col":
        rows = slice(None) if h is None else pl.ds(pl.multiple_of(h * (wm.k // 2), 16), wm.k // 2)
    elif h is None:
        rows = pl.ds(pl.multiple_of(s * wm.k, 16), wm.k)
    else:
        rows = pl.ds(pl.multiple_of(s * wm.k + h * (wm.k // 2), 16), wm.k // 2)
    return full.at[rows, cols]


def _full_shape(wm):
    if wm.kind == "tiny":
        return (N_CHIPS, wm.k, wm.n)
    shape = (wm.k, N_CHIPS * wm.n) if wm.kind == "col" else (N_CHIPS * wm.k, wm.n)
    return shape if wm.nl == 1 else (wm.nl,) + shape


def _handshake(peers):
    barrier = pltpu.get_barrier_semaphore()
    for peer in peers:
        pl.semaphore_signal(barrier, inc=1, device_id=peer, device_id_type=MESH)
    pl.semaphore_wait(barrier, len(peers))


def _all_gather_group(gi, shards):
    wms = AG_GROUPS[gi]
    nw = len(wms)

    def body(*refs):
        sh, full = refs[:nw], refs[nw:2 * nw]
        ici_s, ici_r, pass_s, pass_r, own_s, own_r = refs[2 * nw:]
        x, y, c, chips = _position()
        me, sibling = 2 * x + y, (x, y, 1 - c)
        _handshake([(*chip, c) for chip in chips] + [sibling])

        def rcopy(src, dst, s_sem, r_sem, to):
            return pltpu.make_async_remote_copy(src_ref=src, dst_ref=dst, send_sem=s_sem, recv_sem=r_sem,
                                                device_id=to, device_id_type=MESH)

        started = []
        for i, wm in enumerate(wms):
            for j, chip in enumerate(chips):
                started.append(rcopy(_shard_half(sh[i], wm, c), _region(full[i], wm, me, c),
                                     ici_s.at[i, j], ici_r.at[i, j], (*chip, c)))
                started[-1].start()
            started.append(rcopy(sh[i], _region(full[i], wm, me, None), own_s.at[i], own_r.at[i], sibling))
            started[-1].start()
        for i, wm in enumerate(wms):
            for j, chip in enumerate(chips):
                got = _region(full[i], wm, 2 * chip[0] + chip[1], c)
                rcopy(got, got, ici_s.at[i, j], ici_r.at[i, j], sibling).wait_recv()
                if wm.kind != "tiny":
                    started.append(rcopy(got, got, pass_s.at[i, j], pass_r.at[i, j], sibling))
                    started[-1].start()
        for i, wm in enumerate(wms):
            mine = _region(full[i], wm, me, None)
            rcopy(mine, mine, own_s.at[i], own_r.at[i], sibling).wait_recv()
            for j, chip in enumerate(chips):
                if wm.kind != "tiny":
                    got = _region(full[i], wm, 2 * chip[0] + chip[1], 1 - c)
                    rcopy(got, got, pass_s.at[i, j], pass_r.at[i, j], sibling).wait_recv()
        for cp in started:
            cp.wait_send()

    return pl.kernel(
        body, out_type=[jax.ShapeDtypeStruct(_full_shape(wm), s.dtype) for wm, s in zip(wms, shards)],
        mesh=plsc.ScalarSubcoreMesh(axis_name="sequencer", num_cores=1), name=f"ag_group{gi}",
        scratch_types=[pltpu.SemaphoreType.DMA((nw, 3))] * 4 + [pltpu.SemaphoreType.DMA((nw,))] * 2,
        compiler_params=pltpu.CompilerParams(collective_id=gi),
    )(*shards)


def _sequencer_call(body, name, cid, out_types, scratch, args):
    return pl.kernel(
        body, out_type=out_types, mesh=plsc.ScalarSubcoreMesh(axis_name="sequencer", num_cores=1), name=name,
        scratch_types=scratch, compiler_params=pltpu.CompilerParams(collective_id=cid),
    )(*args)


def _pair_exchange(gs, tag, cid):
    n = len(gs)

    def body(*refs):
        g, out, send_sems, recv_sems = refs[:n], refs[n:2 * n], refs[2 * n], refs[2 * n + 1]
        x, y, c, _ = _position()
        _handshake([(x, y, 1 - c)])
        cps = []
        for i in range(n):
            half = g[i].shape[1] // 2
            cps.append(pltpu.make_async_remote_copy(
                src_ref=g[i].at[:, pl.ds(pl.multiple_of((1 - c) * half, 16), half), :], dst_ref=out[i],
                send_sem=send_sems.at[i], recv_sem=recv_sems.at[i], device_id=(x, y, 1 - c), device_id_type=MESH))
            cps[-1].start()
        for cp in cps:
            cp.wait()

    return _sequencer_call(
        body, f"rs_pair_exchange{tag}", cid,
        [jax.ShapeDtypeStruct((a.shape[0], a.shape[1] // 2, a.shape[2]), a.dtype) for a in gs],
        [pltpu.SemaphoreType.DMA((n,)), pltpu.SemaphoreType.DMA((n,))], gs)


def _chip_exchange(ss, tag, cid):
    n = len(ss)

    def body(*refs):
        s, out, send_sems, recv_sems = refs[:n], refs[n:2 * n], refs[2 * n], refs[2 * n + 1]
        x, y, c, chips = _position()
        _handshake([(*chip, c) for chip in chips])
        cps = []
        for i in range(n):
            for j, chip in enumerate(chips):
                cps.append(pltpu.make_async_remote_copy(
                    src_ref=s[i].at[j], dst_ref=out[i].at[j], send_sem=send_sems.at[i, j], recv_sem=recv_sems.at[i, j],
                    device_id=(*chip, c), device_id_type=MESH))
                cps[-1].start()
        for cp in cps:
            cp.wait()

    return _sequencer_call(
        body, f"rs_chip_exchange{tag}", cid, [jax.ShapeDtypeStruct(a.shape, a.dtype) for a in ss],
        [pltpu.SemaphoreType.DMA((n, 3)), pltpu.SemaphoreType.DMA((n, 3))], ss)


def _pair_swap(g8s, tag, cid):
    n = len(g8s)

    def body(*refs):
        g, out, send_sems, recv_sems = refs[:n], refs[n:2 * n], refs[2 * n], refs[2 * n + 1]
        x, y, c, _ = _position()
        _handshake([(x, y, 1 - c)])
        cps = []
        for i in range(n):
            cps.append(pltpu.make_async_remote_copy(
                src_ref=g[i], dst_ref=out[i], send_sem=send_sems.at[i], recv_sem=recv_sems.at[i],
                device_id=(x, y, 1 - c), device_id_type=MESH))
            cps[-1].start()
        for cp in cps:
            cp.wait()

    return _sequencer_call(
        body, f"rs_pair_swap{tag}", cid, [jax.ShapeDtypeStruct(a.shape, a.dtype) for a in g8s],
        [pltpu.SemaphoreType.DMA((n,)), pltpu.SemaphoreType.DMA((n,))], g8s)


def _all_reduce_small(vec, name):
    r, cols = vec.shape

    def body(v_ref, o_ref, gath, send_sems, recv_sems):
        x, y, c, _ = _position()
        me = 4 * x + 2 * y + c
        gath[me] = v_ref[...]
        cps = []
        for rel in range(1, N_DEV):
            peer = (x ^ (rel >> 2), y ^ ((rel >> 1) & 1), c ^ (rel & 1))
            cps.append(pltpu.make_async_remote_copy(
                src_ref=v_ref, dst_ref=gath.at[me], send_sem=send_sems.at[rel - 1], recv_sem=recv_sems.at[rel - 1],
                device_id=peer, device_id_type=MESH))
        for cp in cps:
            cp.start()
        for rel in range(1, N_DEV):
            pltpu.make_async_remote_copy(
                src_ref=v_ref, dst_ref=gath.at[me ^ rel], send_sem=send_sems.at[rel - 1],
                recv_sem=recv_sems.at[rel - 1], device_id=(x, y, c), device_id_type=MESH).wait_recv()
        for cp in cps:
            cp.wait_send()
        acc = gath[0]
        for d in range(1, N_DEV):
            acc = acc + gath[d]
        o_ref[...] = acc

    vm = pl.BlockSpec(memory_space=pltpu.VMEM)
    return _tc_call(
        body, name=name, in_specs=[vm], out_specs=vm, out_shape=jax.ShapeDtypeStruct((r, cols), F32),
        scratch_shapes=[pltpu.VMEM((N_DEV, r, cols), F32), pltpu.SemaphoreType.DMA((N_DEV - 1,)),
                        pltpu.SemaphoreType.DMA((N_DEV - 1,))],
    )(vec)


def _rope_tables(positions):
    half = QK_ROPE // 2
    inv_freq = 1.0 / (ROPE_THETA ** (jnp.arange(half, dtype=F32) / half))
    ang = positions.astype(F32)[:, None] * inv_freq
    zeros = jnp.zeros((positions.shape[0], LANES - QK_ROPE), F32)
    cos, sin = jnp.cos(ang), jnp.sin(ang)
    return jnp.concatenate([cos, cos, zeros], axis=1), jnp.concatenate([sin, sin, zeros], axis=1)


def _local_step(x, positions, tgt, wf, small, rs):
    cos, sin = _rope_tables(positions)
    w_in, w_out = wf["sc_w_in"], wf["sc_w_out"]
    w_ups, w_downs = (wf["ffn_w_up0"], wf["ffn_w_up1"]), (wf["ffn_w_down0"], wf["ffn_w_down1"])
    w_kv, w_ukv, w_dq, w_o = wf["w_kv"], wf["w_ukv"], wf["w_dq"], wf["w_o"]
    w_uq = jnp.pad(wf["w_uq"].reshape(Q_LORA, N_HEADS, QK_NOPE + QK_ROPE),
                   ((0, 0), (0, 0), (0, HEAD_PAD - QK_NOPE - QK_ROPE))).reshape(Q_LORA, N_HEADS * HEAD_PAD)
    attn_norm, ffn_norm = small["attn_norm"], small["ffn_norm"]
    conv_b = small["ffn_conv_b"]

    def ffn_fwd(h, l):
        hf = _rms_fwd(h, ffn_norm[l:l + 1], f"ffn{l}_norm")
        up = _nn_parts(f"ffn{l}_up", hf, w_ups[l], 2, BF16)
        a = _gate_fwd(up, small["ffn_conv_w"][l], conv_b[l:l + 1], f"ffn{l}_gate")
        return _nn(f"ffn{l}_down", a, w_downs[l], F32, add=h), (hf, up, a)

    def ffn_bwd(h, dh_out, l, saved, gi, before_start=None, after_down_dx=None):
        hf, up, a = saved
        da = _nt(f"ffn{l}_down_dx", dh_out, w_downs[l], BF16)
        if after_down_dx is not None:
            after_down_dx()
        d_down = _tn(f"ffn{l}_down_dw", a, dh_out, BF16)
        dup, d_cw, d_cb = _gate_bwd(up, small["ffn_conv_w"][l], conv_b[l:l + 1], da, f"ffn{l}_gate_bwd")
        d_up = _dw_ffn_up(f"ffn{l}_up_dw", hf, dup)
        if before_start is not None:
            before_start()
        rs.start(gi, {f"ffn_w_down{l}": d_down.reshape(N_CHIPS, F_FF // N_CHIPS, D), f"ffn_w_up{l}": d_up})
        dhf = _nt_parts(f"ffn{l}_up_dx", dup, w_ups[l], BF16)
        dh, d_norm = _rms_bwd(h, ffn_norm[l:l + 1], dhf, dh_out, f"ffn{l}_norm_bwd")
        return dh, d_cw, d_cb, d_norm

    hn0 = _rms_fwd(x, attn_norm[0:1], "attn0_norm")
    z = _nn_parts("sc_in", hn0, w_in, 3, BF16)
    mix = _scmix_fwd(z, small["sc_conv_w"])
    h1 = _nn("sc_out", mix, w_out, F32, add=x)
    h2, ffn0_saved = ffn_fwd(h1, 0)

    hk = _rms_fwd(h2, small["kv_in_norm"], "kv_in_norm")
    kvpre = _nn("kv_down", hk, w_kv, F32)
    ckv, kr = _kv_elem_fwd(kvpre, small["kv_latent_norm"], cos, sin)
    knv = _nn_parts("kv_up", ckv, w_ukv, 2, BF16, stacked=True)

    hn1 = _rms_fwd(h2, attn_norm[1:2], "attn1_norm")
    cq_pre = _nn("q_down", hn1, w_dq, F32)
    cq = _rms_fwd(cq_pre, small["q_latent_norm"], "q_latent_norm")
    q = _q_rope_fwd(_nn("q_up", cq, w_uq, F32), cos, sin)
    o = _attn_fwd(q, knv, kr)
    h3 = _nn("attn_out", o, w_o, F32, add=h2)
    h4, ffn1_saved = ffn_fwd(h3, 1)

    loss, dh4, d_final = _loss_head(h4, small["final_norm"], tgt)

    rows = D // N_CHIPS
    dh3, d_cw1, d_cb1, d_fn1 = ffn_bwd(h3, dh4, 1, ffn1_saved, 0)

    do = _nt("attn_out_dx", dh3, w_o, BF16)
    d_wo = _tn("attn_out_dw", o, dh3, BF16)
    rs.pair_sums(0)
    dq, dknv, dkr = _attn_bwd(q, knv, kr, do, cos, sin)
    rs.chip_sums(0)
    dcq = _nt("q_up_dx", dq, w_uq, F32)
    d_wuq = _tn("q_up_dw", cq, dq, BF16)
    d_wuq = d_wuq.reshape(Q_LORA, N_HEADS, HEAD_PAD)[:, :, :QK_NOPE + QK_ROPE]
    d_wuq = d_wuq.reshape(Q_LORA, N_CHIPS, -1).transpose(1, 0, 2)
    dcq_pre, d_qln = _rms_bwd(cq_pre, small["q_latent_norm"], dcq, None, "q_latent_norm_bwd")
    dhn1 = _nt("q_down_dx", dcq_pre, w_dq, BF16)
    d_wdq = _tn("q_down_dw", hn1, dcq_pre, BF16)
    dh2, d_an1 = _rms_bwd(h2, attn_norm[1:2], dhn1, dh3, "attn1_norm_bwd")

    dckv = _nt_parts("kv_up_dx", dknv, w_ukv, F32, stacked=True)
    d_wukv = _dw_ukv(ckv, dknv)
    dkvpre, d_kvln = _kv_elem_bwd(kvpre, small["kv_latent_norm"], dckv, dkr, cos, sin)
    dhk = _nt("kv_down_dx", dkvpre, w_kv, BF16)
    d_wkv = _tn("kv_down_dw", hk, dkvpre, BF16)
    rs.start(1, {
        "w_o": d_wo.reshape(N_CHIPS, rows, D), "w_uq": d_wuq, "w_dq": d_wdq.reshape(N_CHIPS, rows, Q_LORA),
        "w_ukv": d_wukv.reshape(N_CHIPS, 2 * KV_LORA, -1), "w_kv": d_wkv.reshape(N_CHIPS, rows, KVP),
    })
    dh2, d_kvin = _rms_bwd(h2, small["kv_in_norm"], dhk, dh2, "kv_in_norm_bwd")

    dh1, d_cw0, d_cb0, d_fn0 = ffn_bwd(h1, dh2, 0, ffn0_saved, 2, before_start=lambda: rs.chip_sums(1),
                                       after_down_dx=lambda: rs.pair_sums(1))
    rs.pair_sums(2)

    d_wout = _tn("sc_out_dw", mix, dh1, BF16)
    dmix = _nt("sc_out_dx", dh1, w_out, BF16)
    dz, d_scw = _scmix_bwd(z, small["sc_conv_w"], dmix)
    d_win = _dw_sc_in(hn0, dz)
    rs.start(3, {"sc_w_out": d_wout.reshape(N_CHIPS, rows, D), "sc_w_in": d_win})
    dhn0 = _nt_parts("sc_in_dx", dz, w_in, BF16)
    dx, d_an0 = _rms_bwd(x, attn_norm[0:1], dhn0, dh1, "attn0_norm_bwd")
    rs.pair_sums(3)

    small_g = {
        "attn_norm": jnp.concatenate([d_an0, d_an1]), "ffn_norm": jnp.concatenate([d_fn0, d_fn1]),
        "final_norm": d_final, "kv_in_norm": d_kvin, "kv_latent_norm": d_kvln, "q_latent_norm": d_qln,
        "ffn_conv_b": jnp.concatenate([d_cb0, d_cb1]), "sc_conv_w": d_scw, "ffn_conv_w": jnp.stack([d_cw0, d_cw1]),
    }
    return loss, dx, small_g


RS_GROUPS = (("ffn_w_down1", "ffn_w_up1"), ("w_o", "w_uq", "w_dq", "w_ukv", "w_kv"),
             ("ffn_w_down0", "ffn_w_up0"), ("sc_w_out", "sc_w_in"))


class _ReduceScatter:
    def __init__(self, ids):
        self.ids, self.grads, self.step, self.mine, self.sib = ids, {}, {}, {}, {}

    def _cid(self, gi):
        return len(AG_GROUPS) + 3 * gi

    def start(self, gi, grads):
        self.grads.update(grads)
        own = [grads[n] for n in RS_GROUPS[gi]]
        self.step[gi] = (own, _pair_exchange(own, gi, self._cid(gi)))

    def pair_sums(self, gi):
        own, ra = self.step[gi]
        sums = [_pair_sum(self.ids, g, a, f"rs_pair_sum_{n}") for n, g, a in zip(RS_GROUPS[gi], own, ra)]
        self.step[gi] = (own, ra, _chip_exchange(sums, gi, self._cid(gi) + 1))

    def chip_sums(self, gi):
        own, ra, rb = self.step[gi]
        mine = [_chip_sum(self.ids, g, a, b, f"rs_chip_sum_{n}") for n, g, a, b in zip(RS_GROUPS[gi], own, ra, rb)]
        self.mine.update(zip(RS_GROUPS[gi], mine))
        self.sib.update(zip(RS_GROUPS[gi], _pair_swap(mine, gi, self._cid(gi) + 2)))

SMALL_REPL = ("attn_norm", "ffn_norm", "final_norm", "kv_in_norm", "kv_latent_norm", "q_latent_norm", "ffn_conv_b")
SMALL_SHARDED = ("sc_conv_w", "ffn_conv_w")
SMALL_ROWS = 256


def _pack_kv(w_dkv, w_kr):
    return jnp.concatenate([w_dkv, w_kr, jnp.zeros((w_kr.shape[0], LANES - QK_ROPE), w_kr.dtype)], axis=1)


def kernel(x, positions, attn_norm, ffn_norm, final_norm, sc_w_in, sc_conv_w, sc_w_out, kv_in_norm, w_dkv, kv_latent_norm, w_kr, w_uk, w_uv, w_dq, q_latent_norm, w_uq, w_o, ffn_w_up, ffn_conv_w, ffn_conv_b, ffn_w_down, loss_target, m_attn_norm, m_ffn_norm, m_final_norm, m_sc_w_in, m_sc_conv_w, m_sc_w_out, m_kv_in_norm, m_w_dkv, m_kv_latent_norm, m_w_kr, m_w_uk, m_w_uv, m_w_dq, m_q_latent_norm, m_w_uq, m_w_o, m_ffn_w_up, m_ffn_conv_w, m_ffn_conv_b, m_ffn_w_down, v_attn_norm, v_ffn_norm, v_final_norm, v_sc_w_in, v_sc_conv_w, v_sc_w_out, v_kv_in_norm, v_w_dkv, v_kv_latent_norm, v_w_kr, v_w_uk, v_w_uv, v_w_dq, v_q_latent_norm, v_w_uq, v_w_o, v_ffn_w_up, v_ffn_conv_w, v_ffn_conv_b, v_ffn_w_down):
    names = ("attn_norm", "ffn_norm", "final_norm", "sc_w_in", "sc_conv_w", "sc_w_out", "kv_in_norm", "w_dkv",
             "kv_latent_norm", "w_kr", "w_uk", "w_uv", "w_dq", "q_latent_norm", "w_uq", "w_o", "ffn_w_up",
             "ffn_conv_w", "ffn_conv_b", "ffn_w_down")
    w = dict(zip(names, (attn_norm, ffn_norm, final_norm, sc_w_in, sc_conv_w, sc_w_out, kv_in_norm, w_dkv,
                         kv_latent_norm, w_kr, w_uk, w_uv, w_dq, q_latent_norm, w_uq, w_o, ffn_w_up,
                         ffn_conv_w, ffn_conv_b, ffn_w_down)))
    m = dict(zip(names, (m_attn_norm, m_ffn_norm, m_final_norm, m_sc_w_in, m_sc_conv_w, m_sc_w_out, m_kv_in_norm,
                         m_w_dkv, m_kv_latent_norm, m_w_kr, m_w_uk, m_w_uv, m_w_dq, m_q_latent_norm, m_w_uq, m_w_o,
                         m_ffn_w_up, m_ffn_conv_w, m_ffn_conv_b, m_ffn_w_down)))
    v = dict(zip(names, (v_attn_norm, v_ffn_norm, v_final_norm, v_sc_w_in, v_sc_conv_w, v_sc_w_out, v_kv_in_norm,
                         v_w_dkv, v_kv_latent_norm, v_w_kr, v_w_uk, v_w_uv, v_w_dq, v_q_latent_norm, v_w_uq, v_w_o,
                         v_ffn_w_up, v_ffn_conv_w, v_ffn_conv_b, v_ffn_w_down)))

    _ORDER[0] = None
    ix, iy, ic = lax.axis_index("x"), lax.axis_index("y"), lax.axis_index("c")
    chip = 2 * ix + iy
    ids = jnp.stack([ic, chip]).astype(jnp.int32)

    def shards_of(t):
        return {
            "sc_w_in": t["sc_w_in"][0], "sc_w_out": t["sc_w_out"][0], "ffn_w_up": t["ffn_w_up"],
            "ffn_w_down": t["ffn_w_down"], "w_kv": _pack_kv(t["w_dkv"], t["w_kr"]),
            "w_ukv": jnp.stack([t["w_uk"], t["w_uv"]]), "w_dq": t["w_dq"][0], "w_uq": t["w_uq"][0], "w_o": t["w_o"][0],
        }

    ws, ms, vs = shards_of(w), shards_of(m), shards_of(v)

    def ag_shard(name):
        if name == "sc_conv_w":
            return sc_conv_w[0]
        if name == "ffn_conv_w":
            return ffn_conv_w.reshape(6, -1)
        if name[:-1] in ("ffn_w_up", "ffn_w_down"):
            return ws[name[:-1]][int(name[-1])].astype(BF16)
        return ws[name].astype(BF16)

    wf = {}
    for gi, wms in enumerate(AG_GROUPS):
        fulls = _all_gather_group(gi, [ag_shard(wm.name) for wm in wms])
        wf.update({wm.name: f for wm, f in zip(wms, fulls)})
    small = {
        "attn_norm": attn_norm, "ffn_norm": ffn_norm, "final_norm": final_norm[None], "kv_in_norm": kv_in_norm[None],
        "kv_latent_norm": kv_latent_norm[None], "q_latent_norm": q_latent_norm, "ffn_conv_b": ffn_conv_b,
        "sc_conv_w": wf["sc_conv_w"].transpose(1, 0, 2).reshape(3, D),
        "ffn_conv_w": wf["ffn_conv_w"].reshape(N_CHIPS, 2, 3, -1).transpose(1, 2, 0, 3).reshape(2, 3, F_FF),
    }

    rs = _ReduceScatter(ids)
    loss, dx, small_g = _local_step(x[0], positions[0], loss_target[0], wf, small, rs)
    g_mine, g_sib = rs.mine, rs.sib

    def adamw_layer(n, layer, prev=None):
        key = f"{n}{layer}"
        return _adamw_shard(ids, ws[n], ms[n], vs[n], g_mine[key], g_sib[key], f"adamw_{key}", layer=layer, prev=prev)

    s_order = SMALL_REPL + SMALL_SHARDED
    flat = jnp.concatenate([small_g[n].reshape(-1) for n in s_order] + [loss.reshape(-1)])
    flat = jnp.pad(flat, (0, SMALL_ROWS * LANES - flat.shape[0])).reshape(SMALL_ROWS, LANES)
    red = _all_reduce_small(flat, "ar_small").reshape(-1)
    sg, off = {}, 0
    for n in s_order:
        sz = small_g[n].size
        sg[n] = red[off:off + sz].reshape(small_g[n].shape)
        off += sz
    loss_out = red[off]
    grads = {n: sg[n].reshape(w[n].shape) for n in SMALL_REPL}
    grads["sc_conv_w"] = lax.dynamic_slice_in_dim(sg["sc_conv_w"], chip * (D // N_CHIPS), D // N_CHIPS, axis=1)[None]
    grads["ffn_conv_w"] = lax.dynamic_slice_in_dim(sg["ffn_conv_w"], chip * (F_FF // N_CHIPS), F_FF // N_CHIPS, axis=2)

    res = {}

    def adamw_plain(n):
        res[n] = _adamw_shard(ids, ws[n], ms[n], vs[n], g_mine[n], g_sib[n], f"adamw_{n}")

    second_layer = {n: adamw_layer(n, 1) for n in ("ffn_w_up", "ffn_w_down")}
    for n in ("w_kv", "w_dq", "w_uq", "w_o"):
        adamw_plain(n)
    merged = lambda a: a.reshape(2 * KV_LORA, -1)
    res["w_ukv"] = _adamw_shard(ids, merged(ws["w_ukv"]), merged(ms["w_ukv"]), merged(vs["w_ukv"]),
                                g_mine["w_ukv"], g_sib["w_ukv"], "adamw_w_ukv")
    rs.chip_sums(2)
    rs.chip_sums(3)
    for n in ("ffn_w_up", "ffn_w_down"):
        res[n] = adamw_layer(n, 0, prev=second_layer[n])
    for n in ("sc_w_out", "sc_w_in"):
        adamw_plain(n)
    outs = [grads, {}, {}, {}]
    for k, dst in enumerate(outs):
        for n in ("sc_w_in", "sc_w_out", "w_dq", "w_uq", "w_o"):
            dst[n] = res[n][k][None]
        dst["ffn_w_up"], dst["ffn_w_down"] = res["ffn_w_up"][k], res["ffn_w_down"][k]
        dst["w_dkv"], dst["w_kr"] = res["w_kv"][k][:, :KV_LORA], res["w_kv"][k][:, KV_LORA:KV_LORA + QK_ROPE]
        dst["w_uk"], dst["w_uv"] = res["w_ukv"][k][:KV_LORA], res["w_ukv"][k][KV_LORA:]
    grads, delta, new_m, new_v = outs

    small_names = SMALL_REPL + SMALL_SHARDED

    def pack_small(tree):
        return jnp.concatenate([tree[n].reshape(-1) for n in small_names]).reshape(-1, LANES)

    small_res = _adamw_small(pack_small(w), pack_small(grads), pack_small(m), pack_small(v))
    for slab, dst in zip(small_res, (delta, new_m, new_v)):
        f, off = slab.reshape(-1), 0
        for n in small_names:
            dst[n] = f[off:off + w[n].size].reshape(w[n].shape)
            off += w[n].size

    _ORDER[0] = None
    return (loss_out, dx[None], *[grads[n] for n in names], *[delta[n] for n in names],
            *[new_m[n] for n in names], *[new_v[n] for n in names])
```

```python
from typing import NamedTuple

import jax
import jax.numpy as jnp
from jax import lax
from jax.experimental import pallas as pl
from jax.experimental.pallas import tpu as pltpu
from jax.experimental.pallas import tpu_sc as plsc

F32 = jnp.float32
BF16 = jnp.bfloat16

T = 2048
D = 1024
F_FF = 2816
N_HEADS = 8
QK_NOPE = 128
QK_ROPE = 64
V_HEAD = 128
Q_LORA = 384
KV_LORA = 256
CHUNK_SHIFT = 6
ROPE_THETA = 10000.0
EPS = 1e-6
NEG_INF = -1e30
HEAD_PAD = 256
KVP = KV_LORA + 128

ADAM_LR = 0.001
ADAM_B1 = 0.9
ADAM_B2 = 0.999
ADAM_EPS = 1e-08
ADAM_WD = 0.01
ADAM_STEP = 10

N_CHIPS = 4
N_DEV = 8
LANES = 128
TC = 256
V7X_VMEM_LIMIT = 56 * 1024 * 1024

MESH = pl.DeviceIdType.MESH
ANY = pl.BlockSpec(memory_space=pl.ANY)


class _W(NamedTuple):
    name: str
    kind: str
    nl: int
    k: int
    n: int


AG_GROUPS = (
    (_W("sc_w_in", "col", 1, D, 3 * D // N_CHIPS), _W("sc_conv_w", "tiny", 1, 3, D // N_CHIPS),
     _W("ffn_conv_w", "tiny", 1, 6, F_FF // N_CHIPS)),
    (_W("sc_w_out", "row", 1, D // N_CHIPS, D),),
    (_W("ffn_w_up0", "col", 1, D, 2 * F_FF // N_CHIPS),),
    (_W("ffn_w_down0", "row", 1, F_FF // N_CHIPS, D),),
    (_W("w_kv", "row", 1, D // N_CHIPS, KVP), _W("w_ukv", "col", 2, KV_LORA, N_HEADS * QK_NOPE // N_CHIPS),
     _W("w_dq", "row", 1, D // N_CHIPS, Q_LORA),
     _W("w_uq", "col", 1, Q_LORA, N_HEADS * HEAD_PAD // N_CHIPS),
     _W("w_o", "row", 1, N_HEADS * V_HEAD // N_CHIPS, D)),
    (_W("ffn_w_up1", "col", 1, D, 2 * F_FF // N_CHIPS), _W("ffn_w_down1", "row", 1, F_FF // N_CHIPS, D)),
)


def _cp(*sem):
    return pltpu.CompilerParams(dimension_semantics=sem, vmem_limit_bytes=V7X_VMEM_LIMIT)


_ORDER = [None]


def _tc_call(body, *, name, out_shape, in_specs=None, out_specs=None, grid=(), scratch_shapes=(), prefetch=0,
             input_output_aliases=None, compiler_params=None):
    def run(*args):
        specs = [pl.BlockSpec(memory_space=pltpu.VMEM)] * (len(args) - prefetch) if in_specs is None else list(in_specs)
        inner, dep = body, _ORDER[0]
        if dep is not None:
            unread = prefetch + len(specs)
            specs, args = specs + [ANY], (*args, dep)

            def inner(*refs):
                return body(*refs[:unread], *refs[unread + 1:])

        kwargs = dict(name=name, out_shape=out_shape, input_output_aliases=input_output_aliases or {},
                      compiler_params=compiler_params)
        if prefetch:
            kwargs["grid_spec"] = pltpu.PrefetchScalarGridSpec(
                num_scalar_prefetch=prefetch, grid=grid, in_specs=specs, out_specs=out_specs,
                scratch_shapes=scratch_shapes)
        else:
            kwargs.update(grid=grid, in_specs=specs, scratch_shapes=scratch_shapes)
            if out_specs is not None:
                kwargs["out_specs"] = out_specs
        out = pl.pallas_call(inner, **kwargs)(*args)
        _ORDER[0] = out[0] if isinstance(out, (list, tuple)) else out
        return out

    return run


def _tile(n, cands):
    for c in cands:
        if n % c == 0:
            return c
    raise ValueError(f"no tile for {n}")


NN_DIMS = (((1,), (0,)), ((), ()))
NT_DIMS = (((1,), (1,)), ((), ()))
TN_DIMS = (((0,), (0,)), ((), ()))
M_TILES = (1024, 512, 384, 256, 128)
N_TILES = (512, 384, 256, 128)


def _mm(name, a, b, dims, grid, a_spec, b_spec, o_spec, o_sds, add=None, red=None, acc_shape=None):
    n_red = None if red is None else grid[red]

    def body(*refs):
        a_ref, b_ref = refs[0], refs[1]
        add_ref = refs[2] if add is not None else None
        o_ref = refs[3] if add is not None else refs[2]
        part = lax.dot_general(a_ref[...].astype(BF16), b_ref[...].astype(BF16), dims, preferred_element_type=F32)
        if red is None:
            if add is not None:
                part = part + add_ref[...]
            o_ref[...] = part.astype(o_ref.dtype)
            return
        acc_ref = refs[-1]
        r = pl.program_id(red)

        @pl.when(r == 0)
        def _():
            acc_ref[...] = part

        @pl.when(r > 0)
        def _():
            acc_ref[...] += part

        @pl.when(r == n_red - 1)
        def _():
            o_ref[...] = acc_ref[...].astype(o_ref.dtype)

    sem = tuple("arbitrary" if ax == red else "parallel" for ax in range(len(grid)))
    in_specs = [a_spec, b_spec] + ([o_spec] if add is not None else [])
    args = (a, b) + ((add,) if add is not None else ())
    return _tc_call(
        body, name=name, grid=grid, in_specs=in_specs, out_specs=o_spec, out_shape=o_sds,
        scratch_shapes=[] if red is None else [pltpu.VMEM(acc_shape, F32)], compiler_params=_cp(*sem),
    )(*args)


def _nn(name, a, b, out_dtype, add=None, lead=None):
    (m, k), n = a.shape, b.shape[-1]
    tm, tn = _tile(m, M_TILES), _tile(n, N_TILES)
    if lead is None:
        b_spec = pl.BlockSpec((k, tn), lambda i, j: (0, j))
    else:
        b_spec = pl.BlockSpec((None, k, tn), lambda i, j: (lead, 0, j))
    return _mm(name, a, b, NN_DIMS, (m // tm, n // tn), pl.BlockSpec((tm, k), lambda i, j: (i, 0)), b_spec,
               pl.BlockSpec((tm, tn), lambda i, j: (i, j)), jax.ShapeDtypeStruct((m, n), out_dtype), add=add)


def _nn_parts(name, a, b, parts, out_dtype, lead=None, stacked=False):
    m, k = a.shape
    c = b.shape[-1] if stacked else b.shape[-1] // parts
    tm, tn = _tile(m, M_TILES), _tile(c, N_TILES)
    nb = c // tn
    if stacked:
        b_spec = pl.BlockSpec((None, k, tn), lambda i, p, j: (p, 0, j))
    elif lead is None:
        b_spec = pl.BlockSpec((k, tn), lambda i, p, j: (0, p * nb + j))
    else:
        b_spec = pl.BlockSpec((None, k, tn), lambda i, p, j: (lead, 0, p * nb + j))
    return _mm(name, a, b, NN_DIMS, (m // tm, parts, nb), pl.BlockSpec((tm, k), lambda i, p, j: (i, 0)), b_spec,
               pl.BlockSpec((None, tm, tn), lambda i, p, j: (p, i, j)), jax.ShapeDtypeStruct((parts, m, c), out_dtype))


def _nt(name, a, b, out_dtype, lead=None):
    (m, k), n = a.shape, b.shape[-2]
    tm, tn = _tile(m, M_TILES), _tile(n, N_TILES)
    if lead is None:
        b_spec = pl.BlockSpec((tn, k), lambda i, j: (j, 0))
    else:
        b_spec = pl.BlockSpec((None, tn, k), lambda i, j: (lead, j, 0))
    return _mm(name, a, b, NT_DIMS, (m // tm, n // tn), pl.BlockSpec((tm, k), lambda i, j: (i, 0)), b_spec,
               pl.BlockSpec((tm, tn), lambda i, j: (i, j)), jax.ShapeDtypeStruct((m, n), out_dtype))


def _nt_parts(name, a, b, out_dtype, lead=None, stacked=False):
    parts, m, c = a.shape
    n = b.shape[-2]
    tm, tn = _tile(m, M_TILES), _tile(n, N_TILES)
    if stacked:
        b_spec = pl.BlockSpec((None, tn, c), lambda i, j, p: (p, j, 0))
    elif lead is None:
        b_spec = pl.BlockSpec((tn, c), lambda i, j, p: (j, p))
    else:
        b_spec = pl.BlockSpec((None, tn, c), lambda i, j, p: (lead, j, p))
    return _mm(name, a, b, NT_DIMS, (m // tm, n // tn, parts), pl.BlockSpec((None, tm, c), lambda i, j, p: (p, i, 0)),
               b_spec, pl.BlockSpec((tm, tn), lambda i, j, p: (i, j)), jax.ShapeDtypeStruct((m, n), out_dtype),
               red=2, acc_shape=(tm, tn))


def _tn(name, a, b, out_dtype):
    (k, m), n = a.shape, b.shape[1]
    tm, tn = _tile(m, M_TILES), _tile(n, N_TILES)
    return _mm(name, a, b, TN_DIMS, (m // tm, n // tn), pl.BlockSpec((k, tm), lambda i, j: (0, i)),
               pl.BlockSpec((k, tn), lambda i, j: (0, j)), pl.BlockSpec((tm, tn), lambda i, j: (i, j)),
               jax.ShapeDtypeStruct((m, n), out_dtype))


def _dw_sc_in(hn, dz):
    t, tn, tm = hn.shape[0], TC, 512
    per_part, per_chip = D // tn, 3 * D // N_CHIPS // tn
    return _mm("sc_in_dw", hn, dz, TN_DIMS, (D // tm, 3 * D // tn), pl.BlockSpec((t, tm), lambda i, j: (0, i)),
               pl.BlockSpec((None, t, tn), lambda i, j: (j // per_part, 0, j % per_part)),
               pl.BlockSpec((None, tm, tn), lambda i, j: (j // per_chip, i, j % per_chip)),
               jax.ShapeDtypeStruct((N_CHIPS, D, 3 * D // N_CHIPS), BF16))


def _dw_ffn_up(name, hf, dup):
    t, tm, ns = hf.shape[0], 512, 2 * F_FF // N_CHIPS
    return _mm(name, hf, dup, TN_DIMS, (N_CHIPS, D // tm), pl.BlockSpec((t, tm), lambda s, i: (0, i)),
               pl.BlockSpec((None, t, ns), lambda s, i: (s // 2, 0, s % 2)),
               pl.BlockSpec((None, tm, ns), lambda s, i: (s, i, 0)), jax.ShapeDtypeStruct((N_CHIPS, D, ns), BF16))


def _dw_ukv(ckv, dknv):
    t, ns = ckv.shape[0], N_HEADS * QK_NOPE // N_CHIPS
    return _mm("kv_up_dw", ckv, dknv, TN_DIMS, (2, N_CHIPS), pl.BlockSpec((t, KV_LORA), lambda p, s: (0, 0)),
               pl.BlockSpec((None, t, ns), lambda p, s: (p, 0, s)),
               pl.BlockSpec((None, None, KV_LORA, ns), lambda p, s: (s, p, 0, 0)),
               jax.ShapeDtypeStruct((N_CHIPS, 2, KV_LORA, ns), BF16))


def _rms_fwd(x, g, name):
    t, d = x.shape
    tr = 512

    def body(x_ref, g_ref, o_ref):
        xv = x_ref[...]
        r = lax.rsqrt(jnp.mean(xv * xv, axis=1, keepdims=True) + EPS)
        o_ref[...] = (xv * r * g_ref[...]).astype(o_ref.dtype)

    row = pl.BlockSpec((tr, d), lambda i: (i, 0))
    return _tc_call(
        body, name=name, grid=(t // tr,), in_specs=[row, pl.BlockSpec((1, d), lambda i: (0, 0))],
        out_specs=row, out_shape=jax.ShapeDtypeStruct((t, d), BF16), compiler_params=_cp("parallel"),
    )(x, g)


def _rms_bwd_math(xv, g, dy):
    r = lax.rsqrt(jnp.mean(xv * xv, axis=1, keepdims=True) + EPS)
    xh = xv * r
    gy = dy * g
    dx = r * (gy - xh * jnp.mean(gy * xh, axis=1, keepdims=True))
    dg = jnp.sum(dy * xh, axis=0, keepdims=True)
    return dx, dg


def _rms_bwd(x, g, dy, add, name):
    t, d = x.shape
    tr = 512

    def body(*refs):
        if add is None:
            x_ref, g_ref, dy_ref, dx_ref, dg_ref = refs
        else:
            x_ref, g_ref, dy_ref, add_ref, dx_ref, dg_ref = refs
        dx, dg = _rms_bwd_math(x_ref[...], g_ref[...], dy_ref[...].astype(F32))
        if add is not None:
            dx = dx + add_ref[...]
        dx_ref[...] = dx

        @pl.when(pl.program_id(0) == 0)
        def _():
            dg_ref[...] = jnp.zeros_like(dg_ref)

        dg_ref[...] += dg

    row = pl.BlockSpec((tr, d), lambda i: (i, 0))
    vec = pl.BlockSpec((1, d), lambda i: (0, 0))
    in_specs = [row, vec, row] + ([row] if add is not None else [])
    args = (x, g, dy) + ((add,) if add is not None else ())
    return _tc_call(
        body, name=name, grid=(t // tr,), in_specs=in_specs, out_specs=[row, vec],
        out_shape=[jax.ShapeDtypeStruct((t, d), F32), jax.ShapeDtypeStruct((1, d), F32)],
        compiler_params=_cp("arbitrary"),
    )(*args)


def _loss_head(h, g, tgt):
    t, d = h.shape
    tr = 512

    def body(h_ref, g_ref, t_ref, loss_ref, dh_ref, dg_ref):
        xv = h_ref[...]
        gv = g_ref[...]
        r = lax.rsqrt(jnp.mean(xv * xv, axis=1, keepdims=True) + EPS)
        err = xv * r * gv - t_ref[...]
        part = 0.5 * jnp.sum(jnp.mean(err * err, axis=1, keepdims=True), axis=0, keepdims=True)
        dx, dg = _rms_bwd_math(xv, gv, err * (1.0 / d))
        dh_ref[...] = dx

        @pl.when(pl.program_id(0) == 0)
        def _():
            dg_ref[...] = jnp.zeros_like(dg_ref)
            loss_ref[...] = jnp.zeros_like(loss_ref)

        dg_ref[...] += dg
        loss_ref[...] += jnp.broadcast_to(part, loss_ref.shape)

    row = pl.BlockSpec((tr, d), lambda i: (i, 0))
    vec = pl.BlockSpec((1, d), lambda i: (0, 0))
    lspec = pl.BlockSpec((1, LANES), lambda i: (0, 0))
    return _tc_call(
        body, name="loss_head", grid=(t // tr,), in_specs=[row, vec, row], out_specs=[lspec, row, vec],
        out_shape=[jax.ShapeDtypeStruct((1, LANES), F32), jax.ShapeDtypeStruct((t, d), F32),
                   jax.ShapeDtypeStruct((1, d), F32)],
        compiler_params=_cp("arbitrary"),
    )(h, g, tgt)


def _rot_half(x):
    lane = lax.broadcasted_iota(jnp.int32, x.shape, 1)
    return jnp.where((lane % QK_ROPE) < QK_ROPE // 2, -pltpu.roll(x, LANES - 32, axis=1),
                     pltpu.roll(x, 32, axis=1))


def _rope_fwd_math(x, cos, sin):
    return x * cos + _rot_half(x) * sin


def _rope_bwd_math(dy, cos, sin):
    return dy * cos - _rot_half(dy * sin)


def _q_rope_fwd(qpre, cos, sin):
    t, w = qpre.shape
    tr = 256

    def body(q_ref, c_ref, s_ref, o_ref):
        cv, sv = c_ref[...], s_ref[...]
        for h in range(N_HEADS):
            lo = h * HEAD_PAD
            o_ref[:, lo:lo + QK_NOPE] = q_ref[:, lo:lo + QK_NOPE].astype(BF16)
            o_ref[:, lo + QK_NOPE:lo + HEAD_PAD] = _rope_fwd_math(
                q_ref[:, lo + QK_NOPE:lo + HEAD_PAD], cv, sv).astype(BF16)

    row = pl.BlockSpec((tr, w), lambda i: (i, 0))
    tab = pl.BlockSpec((tr, LANES), lambda i: (i, 0))
    return _tc_call(
        body, name="q_rope_fwd", grid=(t // tr,), in_specs=[row, tab, tab], out_specs=row,
        out_shape=jax.ShapeDtypeStruct((t, w), BF16), compiler_params=_cp("parallel"),
    )(qpre, cos, sin)


def _kv_elem_fwd(kvpre, g, cos, sin):
    t = kvpre.shape[0]
    tr = 512

    def body(p_ref, g_ref, c_ref, s_ref, ckv_ref, kr_ref):
        lat = p_ref[:, :KV_LORA]
        r = lax.rsqrt(jnp.mean(lat * lat, axis=1, keepdims=True) + EPS)
        ckv_ref[...] = (lat * r * g_ref[...]).astype(BF16)
        kr_ref[...] = _rope_fwd_math(p_ref[:, KV_LORA:], c_ref[...], s_ref[...]).astype(BF16)

    tab = pl.BlockSpec((tr, LANES), lambda i: (i, 0))
    return _tc_call(
        body, name="kv_elem_fwd", grid=(t // tr,),
        in_specs=[pl.BlockSpec((tr, KVP), lambda i: (i, 0)), pl.BlockSpec((1, KV_LORA), lambda i: (0, 0)), tab, tab],
        out_specs=[pl.BlockSpec((tr, KV_LORA), lambda i: (i, 0)), tab],
        out_shape=[jax.ShapeDtypeStruct((t, KV_LORA), BF16), jax.ShapeDtypeStruct((t, LANES), BF16)],
        compiler_params=_cp("parallel"),
    )(kvpre, g, cos, sin)


def _kv_elem_bwd(kvpre, g, dckv, dkr, cos, sin):
    t = kvpre.shape[0]
    tr = 512

    def body(p_ref, g_ref, dc_ref, dk_ref, c_ref, s_ref, dp_ref, dg_ref):
        dlat, dg = _rms_bwd_math(p_ref[:, :KV_LORA], g_ref[...], dc_ref[...])
        dp_ref[:, :KV_LORA] = dlat.astype(BF16)
        dp_ref[:, KV_LORA:] = _rope_bwd_math(dk_ref[...], c_ref[...], s_ref[...]).astype(BF16)

        @pl.when(pl.program_id(0) == 0)
        def _():
            dg_ref[...] = jnp.zeros_like(dg_ref)

        dg_ref[...] += dg

    tab = pl.BlockSpec((tr, LANES), lambda i: (i, 0))
    pre = pl.BlockSpec((tr, KVP), lambda i: (i, 0))
    vec = pl.BlockSpec((1, KV_LORA), lambda i: (0, 0))
    return _tc_call(
        body, name="kv_elem_bwd", grid=(t // tr,),
        in_specs=[pre, vec, pl.BlockSpec((tr, KV_LORA), lambda i: (i, 0)), tab, tab, tab],
        out_specs=[pre, vec],
        out_shape=[jax.ShapeDtypeStruct((t, KVP), BF16), jax.ShapeDtypeStruct((1, KV_LORA), F32)],
        compiler_params=_cp("arbitrary"),
    )(kvpre, g, dckv, dkr, cos, sin)


def _shift_down(x, k):
    row = lax.broadcasted_iota(jnp.int32, x.shape, 0)
    return jnp.where(row >= k, pltpu.roll(x, k, axis=0), 0.0)


def _shift_up(x, k):
    n = x.shape[0]
    row = lax.broadcasted_iota(jnp.int32, x.shape, 0)
    return jnp.where(row < n - k, pltpu.roll(x, n - k, axis=0), 0.0)


def _conv3(x, w_ref):
    return _shift_down(x, 2) * w_ref[0:1, :] + _shift_down(x, 1) * w_ref[1:2, :] + x * w_ref[2:3, :]


def _conv3_t(dy, w_ref):
    return dy * w_ref[2:3, :] + _shift_up(dy, 1) * w_ref[1:2, :] + _shift_up(dy, 2) * w_ref[0:1, :]


def _conv3_dw(dy, x, dw_ref):
    dw_ref[0:1, :] = jnp.sum(dy * _shift_down(x, 2), axis=0, keepdims=True)
    dw_ref[1:2, :] = jnp.sum(dy * _shift_down(x, 1), axis=0, keepdims=True)
    dw_ref[2:3, :] = jnp.sum(dy * x, axis=0, keepdims=True)


def _col(parts, t):
    if parts is None:
        return pl.BlockSpec((t, TC), lambda j: (0, j))
    return pl.BlockSpec((parts, t, TC), lambda j: (0, 0, j))


def _scmix_fwd(z, w):
    t = z.shape[1]

    def body(z_ref, w_ref, m_ref):
        b, c, u = (z_ref[p].astype(F32) for p in range(3))
        m_ref[...] = (b * _conv3(c * u, w_ref)).astype(BF16)

    return _tc_call(
        body, name="scmix_fwd", grid=(D // TC,), in_specs=[_col(3, t), pl.BlockSpec((3, TC), lambda j: (0, j))],
        out_specs=_col(None, t), out_shape=jax.ShapeDtypeStruct((t, D), BF16), compiler_params=_cp("parallel"),
    )(z, w)


def _scmix_bwd(z, w, dm):
    t = z.shape[1]

    def body(z_ref, w_ref, dm_ref, dz_ref, dw_ref):
        c, u = z_ref[1].astype(F32), z_ref[2].astype(F32)
        cu = c * u
        dmv = dm_ref[...].astype(F32)
        dz_ref[0] = (dmv * _conv3(cu, w_ref)).astype(BF16)
        dcv = dmv * z_ref[0].astype(F32)
        _conv3_dw(dcv, cu, dw_ref)
        dcu = _conv3_t(dcv, w_ref)
        dz_ref[1] = (dcu * u).astype(BF16)
        dz_ref[2] = (dcu * c).astype(BF16)

    wspec = pl.BlockSpec((3, TC), lambda j: (0, j))
    return _tc_call(
        body, name="scmix_bwd", grid=(D // TC,), in_specs=[_col(3, t), wspec, _col(None, t)],
        out_specs=[_col(3, t), wspec],
        out_shape=[jax.ShapeDtypeStruct((3, t, D), BF16), jax.ShapeDtypeStruct((3, D), F32)],
        compiler_params=_cp("parallel"),
    )(z, w, dm)


def _gate_fwd(up, w, bias, name):
    t = up.shape[1]

    def body(u_ref, w_ref, b_ref, a_ref):
        gc = _conv3(u_ref[0].astype(F32), w_ref) + b_ref[...]
        a_ref[...] = (gc * jax.nn.sigmoid(gc) * u_ref[1].astype(F32)).astype(BF16)

    return _tc_call(
        body, name=name, grid=(F_FF // TC,),
        in_specs=[_col(2, t), pl.BlockSpec((3, TC), lambda j: (0, j)), pl.BlockSpec((1, TC), lambda j: (0, j))],
        out_specs=_col(None, t), out_shape=jax.ShapeDtypeStruct((t, F_FF), BF16), compiler_params=_cp("parallel"),
    )(up, w, bias)


def _gate_bwd(up, w, bias, da, name):
    t = up.shape[1]

    def body(u_ref, w_ref, b_ref, da_ref, du_ref, dw_ref, db_ref):
        g = u_ref[0].astype(F32)
        gc = _conv3(g, w_ref) + b_ref[...]
        sg = jax.nn.sigmoid(gc)
        dav = da_ref[...].astype(F32)
        du_ref[1] = (dav * (gc * sg)).astype(BF16)
        dgc = dav * u_ref[1].astype(F32) * (sg * (1.0 + gc * (1.0 - sg)))
        db_ref[...] = jnp.sum(dgc, axis=0, keepdims=True)
        _conv3_dw(dgc, g, dw_ref)
        du_ref[0] = _conv3_t(dgc, w_ref).astype(BF16)

    wspec = pl.BlockSpec((3, TC), lambda j: (0, j))
    bspec = pl.BlockSpec((1, TC), lambda j: (0, j))
    return _tc_call(
        body, name=name, grid=(F_FF // TC,), in_specs=[_col(2, t), wspec, bspec, _col(None, t)],
        out_specs=[_col(2, t), wspec, bspec],
        out_shape=[jax.ShapeDtypeStruct((2, t, F_FF), BF16), jax.ShapeDtypeStruct((3, F_FF), F32),
                   jax.ShapeDtypeStruct((1, F_FF), F32)],
        compiler_params=_cp("parallel"),
    )(up, w, bias, da)


ATT_TQ = 256
ATT_SCALE = (QK_NOPE + QK_ROPE) ** -0.5


def _attn_probs(q, kn, kr, qi):
    s = lax.dot_general(q[:, :QK_NOPE], kn, NT_DIMS, preferred_element_type=F32)
    s = s + lax.dot_general(q[:, QK_NOPE:], kr, NT_DIMS, preferred_element_type=F32)
    s = s * ATT_SCALE
    row = qi * ATT_TQ + lax.broadcasted_iota(jnp.int32, s.shape, 0)
    col = lax.broadcasted_iota(jnp.int32, s.shape, 1)
    s = jnp.where(lax.shift_right_logical(col, CHUNK_SHIFT) <= lax.shift_right_logical(row, CHUNK_SHIFT), s, NEG_INF)
    p = jnp.exp(s - jnp.max(s, axis=1, keepdims=True))
    return p * (1.0 / jnp.sum(p, axis=1, keepdims=True))


def _attn_specs(t):
    q = pl.BlockSpec((ATT_TQ, HEAD_PAD), lambda h, i: (i, h))
    kn = pl.BlockSpec((None, t, QK_NOPE), lambda h, i: (0, 0, h))
    kr = pl.BlockSpec((t, LANES), lambda h, i: (0, 0))
    v = pl.BlockSpec((None, t, V_HEAD), lambda h, i: (1, 0, h))
    o = pl.BlockSpec((ATT_TQ, V_HEAD), lambda h, i: (i, h))
    return q, kn, kr, v, o


def _attn_fwd(q, knv, kr):
    t = q.shape[0]

    def body(q_ref, kn_ref, kr_ref, v_ref, o_ref):
        p = _attn_probs(q_ref[...], kn_ref[...], kr_ref[...], pl.program_id(1))
        o_ref[...] = jnp.dot(p.astype(BF16), v_ref[...], preferred_element_type=F32).astype(BF16)

    qs, kns, krs, vs, os_ = _attn_specs(t)
    return _tc_call(
        body, name="attn_fwd", grid=(N_HEADS, t // ATT_TQ), in_specs=[qs, kns, krs, vs], out_specs=os_,
        out_shape=jax.ShapeDtypeStruct((t, N_HEADS * V_HEAD), BF16), compiler_params=_cp("parallel", "parallel"),
    )(q, knv, kr, knv)


def _attn_bwd(q, knv, kr, do, cos, sin):
    t = q.shape[0]

    def body(q_ref, kn_ref, kr_ref, v_ref, do_ref, c_ref, s_ref, dq_ref, dknv_ref, dkr_ref):
        h, qi = pl.program_id(0), pl.program_id(1)
        qv, knv_, krv, dov = q_ref[...], kn_ref[...], kr_ref[...], do_ref[...]
        p = _attn_probs(qv, knv_, krv, qi)
        dp = lax.dot_general(dov, v_ref[...], NT_DIMS, preferred_element_type=F32)
        ds = (p * (dp - jnp.sum(p * dp, axis=1, keepdims=True)) * ATT_SCALE).astype(BF16)
        dq_ref[:, :QK_NOPE] = jnp.dot(ds, knv_, preferred_element_type=F32).astype(BF16)
        dqr = jnp.dot(ds, krv, preferred_element_type=F32)
        dq_ref[:, QK_NOPE:] = _rope_bwd_math(dqr, c_ref[...], s_ref[...]).astype(BF16)
        dv = lax.dot_general(p.astype(BF16), dov, TN_DIMS, preferred_element_type=F32)
        dkn = lax.dot_general(ds, qv[:, :QK_NOPE], TN_DIMS, preferred_element_type=F32)
        dkr = lax.dot_general(ds, qv[:, QK_NOPE:], TN_DIMS, preferred_element_type=F32)

        @pl.when(qi == 0)
        def _():
            dknv_ref[...] = jnp.zeros_like(dknv_ref)

        @pl.when((qi == 0) & (h == 0))
        def _():
            dkr_ref[...] = jnp.zeros_like(dkr_ref)

        dknv_ref[0] += dkn
        dknv_ref[1] += dv
        dkr_ref[...] += dkr

    qs, kns, krs, vs, os_ = _attn_specs(t)
    tab = pl.BlockSpec((ATT_TQ, LANES), lambda h, i: (i, 0))
    return _tc_call(
        body, name="attn_bwd", grid=(N_HEADS, t // ATT_TQ), in_specs=[qs, kns, krs, vs, os_, tab, tab],
        out_specs=[qs, pl.BlockSpec((2, t, QK_NOPE), lambda h, i: (0, 0, h)), krs],
        out_shape=[jax.ShapeDtypeStruct((t, N_HEADS * HEAD_PAD), BF16),
                   jax.ShapeDtypeStruct((2, t, N_HEADS * QK_NOPE), F32), jax.ShapeDtypeStruct((t, LANES), F32)],
        compiler_params=_cp("arbitrary", "arbitrary"),
    )(q, knv, kr, knv, do, cos, sin)


def _adam_math(w, g, m, v):
    nm = ADAM_B1 * m + (1.0 - ADAM_B1) * g
    nv = ADAM_B2 * v + (1.0 - ADAM_B2) * (g * g)
    m_hat = nm / (1.0 - ADAM_B1 ** ADAM_STEP)
    v_hat = nv / (1.0 - ADAM_B2 ** ADAM_STEP)
    return -ADAM_LR * (m_hat / (jnp.sqrt(v_hat) + ADAM_EPS) + ADAM_WD * w), nm, nv


def _adamw_small(w, g, m, v):
    def body(w_ref, g_ref, m_ref, v_ref, d_ref, nm_ref, nv_ref):
        d_ref[...], nm_ref[...], nv_ref[...] = _adam_math(w_ref[...], g_ref[...], m_ref[...], v_ref[...])

    shp = jax.ShapeDtypeStruct(w.shape, F32)
    return _tc_call(body, name="adamw_small", out_shape=[shp] * 3)(w, g, m, v)


ADAM_BLOCK_BYTES = 1 << 20


def _adamw_shard(ids, w, m, v, g_mine, g_sib, name, layer=None, prev=None):
    r, c = w.shape[-2:]
    half = r // 2
    tr = _tile(half, [d for d in range(half, 7, -8) if d * c * 4 <= ADAM_BLOCK_BYTES] or [8])
    nbh = half // tr

    def body(ids_ref, w_ref, m_ref, v_ref, gm_ref, gs_ref, *rest):
        g_ref, d_ref, nm_ref, nv_ref = rest[-4:]
        mine = (pl.program_id(0) // nbh) == ids_ref[0]

        @pl.when(mine)
        def _():
            g_ref[...] = gm_ref[...]

        @pl.when(jnp.logical_not(mine))
        def _():
            g_ref[...] = gs_ref[...]

        d_ref[...], nm_ref[...], nv_ref[...] = _adam_math(w_ref[...], g_ref[...], m_ref[...], v_ref[...])

    if layer is None:
        wspec = pl.BlockSpec((tr, c), lambda i, ids: (i, 0))
    else:
        wspec = pl.BlockSpec((None, tr, c), lambda i, ids: (layer, i, 0))
    gspec = pl.BlockSpec((tr, c), lambda i, ids: (i % nbh, 0))
    in_specs = [wspec] * 3 + [gspec] * 2
    args = [ids, w, m, v, g_mine, g_sib]
    aliases = {}
    if prev is not None:
        in_specs += [ANY] * 4
        args += list(prev)
        aliases = {6 + k: k for k in range(4)}
    return _tc_call(
        body, name=name, prefetch=1, grid=(r // tr,), in_specs=in_specs, out_specs=[wspec] * 4,
        out_shape=[jax.ShapeDtypeStruct(w.shape, F32)] * 4, input_output_aliases=aliases,
        compiler_params=_cp("parallel"),
    )(*args)


def _peer_chip(k_me, j):
    return k_me ^ jnp.where(j == 0, 2, jnp.where(j == 1, 1, 3))


def _pair_sum(ids, g, ra, name):
    _, r, c = g.shape
    half = r // 2

    def body(ids_ref, g_ref, ra_ref, o_ref):
        o_ref[...] = (g_ref[...].astype(F32) + ra_ref[...].astype(F32)).astype(BF16)

    return _tc_call(
        body, name=name, prefetch=1, grid=(3,),
        in_specs=[pl.BlockSpec((None, half, c), lambda j, ids: (_peer_chip(ids[1], j), ids[0], 0)),
                  pl.BlockSpec((None, half, c), lambda j, ids: (_peer_chip(ids[1], j), 0, 0))],
        out_specs=pl.BlockSpec((None, half, c), lambda j, ids: (j, 0, 0)),
        out_shape=jax.ShapeDtypeStruct((3, half, c), BF16), compiler_params=_cp("parallel"),
    )(ids, g, ra)


def _chip_sum(ids, g, ra, rb, name):
    _, r, c = g.shape
    half = r // 2

    def body(ids_ref, g_ref, ra_ref, rb_ref, o_ref):
        acc = g_ref[...].astype(F32) + ra_ref[...].astype(F32)
        for j in range(3):
            acc = acc + rb_ref[j].astype(F32)
        o_ref[...] = acc

    return _tc_call(
        body, name=name, prefetch=1, grid=(1,),
        in_specs=[pl.BlockSpec((None, half, c), lambda i, ids: (ids[1], ids[0], 0)),
                  pl.BlockSpec((None, half, c), lambda i, ids: (ids[1], 0, 0)),
                  pl.BlockSpec((3, half, c), lambda i, ids: (0, 0, 0))],
        out_specs=pl.BlockSpec((half, c), lambda i, ids: (0, 0)),
        out_shape=jax.ShapeDtypeStruct((half, c), F32), compiler_params=_cp("arbitrary"),
    )(ids, g, ra, rb)


def _position():
    x, y, c = lax.axis_index("x"), lax.axis_index("y"), lax.axis_index("c")
    chips = [(1 - x, y), (x, 1 - y), (1 - x, 1 - y)]
    return x, y, c, chips


def _shard_half(ref, wm, h):
    if wm.kind == "tiny":
        return ref
    if wm.nl == 2:
        return ref.at[h]
    return ref.at[pl.ds(pl.multiple_of(h * (wm.k // 2), 16), wm.k // 2), :]


def _region(full, wm, s, h):
    if wm.kind == "tiny":
        return full.at[s]
    cols = pl.ds(pl.multiple_of(s * wm.n, LANES), wm.n) if wm.kind == "col" else slice(None)
    if wm.nl == 2:
        rows = pl.ds(pl.multiple_of(s * wm.k, 16), wm.k) if wm.kind == "row" else slice(None)
        return full.at[slice(None) if h is None else h, rows, cols]
    if wm.kind == "col":
        rows = slice(None) if h is None else pl.ds(pl.multiple_of(h * (wm.k // 2), 16), wm.k // 2)
    elif h is None:
        rows = pl.ds(pl.multiple_of(s * wm.k, 16), wm.k)
    else:
        rows = pl.ds(pl.multiple_of(s * wm.k + h * (wm.k // 2), 16), wm.k // 2)
    return full.at[rows, cols]


def _full_shape(wm):
    if wm.kind == "tiny":
        return (N_CHIPS, wm.k, wm.n)
    shape = (wm.k, N_CHIPS * wm.n) if wm.kind == "col" else (N_CHIPS * wm.k, wm.n)
    return shape if wm.nl == 1 else (wm.nl,) + shape


def _handshake(peers):
    barrier = pltpu.get_barrier_semaphore()
    for peer in peers:
        pl.semaphore_signal(barrier, inc=1, device_id=peer, device_id_type=MESH)
    pl.semaphore_wait(barrier, len(peers))


def _all_gather_group(gi, shards):
    wms = AG_GROUPS[gi]
    nw = len(wms)

    def body(*refs):
        sh, full = refs[:nw], refs[nw:2 * nw]
        ici_s, ici_r, pass_s, pass_r, own_s, own_r = refs[2 * nw:]
        x, y, c, chips = _position()
        me, sibling = 2 * x + y, (x, y, 1 - c)
        _handshake([(*chip, c) for chip in chips] + [sibling])

        def rcopy(src, dst, s_sem, r_sem, to):
            return pltpu.make_async_remote_copy(src_ref=src, dst_ref=dst, send_sem=s_sem, recv_sem=r_sem,
                                                device_id=to, device_id_type=MESH)

        started = []
        for i, wm in enumerate(wms):
            for j, chip in enumerate(chips):
                started.append(rcopy(_shard_half(sh[i], wm, c), _region(full[i], wm, me, c),
                                     ici_s.at[i, j], ici_r.at[i, j], (*chip, c)))
                started[-1].start()
            started.append(rcopy(sh[i], _region(full[i], wm, me, None), own_s.at[i], own_r.at[i], sibling))
            started[-1].start()
        for i, wm in enumerate(wms):
            for j, chip in enumerate(chips):
                got = _region(full[i], wm, 2 * chip[0] + chip[1], c)
                rcopy(got, got, ici_s.at[i, j], ici_r.at[i, j], sibling).wait_recv()
                if wm.kind != "tiny":
                    started.append(rcopy(got, got, pass_s.at[i, j], pass_r.at[i, j], sibling))
                    started[-1].start()
        for i, wm in enumerate(wms):
            mine = _region(full[i], wm, me, None)
            rcopy(mine, mine, own_s.at[i], own_r.at[i], sibling).wait_recv()
            for j, chip in enumerate(chips):
                if wm.kind != "tiny":
                    got = _region(full[i], wm, 2 * chip[0] + chip[1], 1 - c)
                    rcopy(got, got, pass_s.at[i, j], pass_r.at[i, j], sibling).wait_recv()
        for cp in started:
            cp.wait_send()

    return pl.kernel(
        body, out_type=[jax.ShapeDtypeStruct(_full_shape(wm), s.dtype) for wm, s in zip(wms, shards)],
        mesh=plsc.ScalarSubcoreMesh(axis_name="sequencer", num_cores=1), name=f"ag_group{gi}",
        scratch_types=[pltpu.SemaphoreType.DMA((nw, 3))] * 4 + [pltpu.SemaphoreType.DMA((nw,))] * 2,
        compiler_params=pltpu.CompilerParams(collective_id=gi),
    )(*shards)


def _sequencer_call(body, name, cid, out_types, scratch, args):
    return pl.kernel(
        body, out_type=out_types, mesh=plsc.ScalarSubcoreMesh(axis_name="sequencer", num_cores=1), name=name,
        scratch_types=scratch, compiler_params=pltpu.CompilerParams(collective_id=cid),
    )(*args)


def _pair_exchange(gs, tag, cid):
    n = len(gs)

    def body(*refs):
        g, out, send_sems, recv_sems = refs[:n], refs[n:2 * n], refs[2 * n], refs[2 * n + 1]
        x, y, c, _ = _position()
        _handshake([(x, y, 1 - c)])
        cps = []
        for i in range(n):
            half = g[i].shape[1] // 2
            cps.append(pltpu.make_async_remote_copy(
                src_ref=g[i].at[:, pl.ds(pl.multiple_of((1 - c) * half, 16), half), :], dst_ref=out[i],
                send_sem=send_sems.at[i], recv_sem=recv_sems.at[i], device_id=(x, y, 1 - c), device_id_type=MESH))
            cps[-1].start()
        for cp in cps:
            cp.wait()

    return _sequencer_call(
        body, f"rs_pair_exchange{tag}", cid,
        [jax.ShapeDtypeStruct((a.shape[0], a.shape[1] // 2, a.shape[2]), a.dtype) for a in gs],
        [pltpu.SemaphoreType.DMA((n,)), pltpu.SemaphoreType.DMA((n,))], gs)


def _chip_exchange(ss, tag, cid):
    n = len(ss)

    def body(*refs):
        s, out, send_sems, recv_sems = refs[:n], refs[n:2 * n], refs[2 * n], refs[2 * n + 1]
        x, y, c, chips = _position()
        _handshake([(*chip, c) for chip in chips])
        cps = []
        for i in range(n):
            for j, chip in enumerate(chips):
                cps.append(pltpu.make_async_remote_copy(
                    src_ref=s[i].at[j], dst_ref=out[i].at[j], send_sem=send_sems.at[i, j], recv_sem=recv_sems.at[i, j],
                    device_id=(*chip, c), device_id_type=MESH))
                cps[-1].start()
        for cp in cps:
            cp.wait()

    return _sequencer_call(
        body, f"rs_chip_exchange{tag}", cid, [jax.ShapeDtypeStruct(a.shape, a.dtype) for a in ss],
        [pltpu.SemaphoreType.DMA((n, 3)), pltpu.SemaphoreType.DMA((n, 3))], ss)


def _pair_swap(g8s, tag, cid):
    n = len(g8s)

    def body(*refs):
        g, out, send_sems, recv_sems = refs[:n], refs[n:2 * n], refs[2 * n], refs[2 * n + 1]
        x, y, c, _ = _position()
        _handshake([(x, y, 1 - c)])
        cps = []
        for i in range(n):
            cps.append(pltpu.make_async_remote_copy(
                src_ref=g[i], dst_ref=out[i], send_sem=send_sems.at[i], recv_sem=recv_sems.at[i],
                device_id=(x, y, 1 - c), device_id_type=MESH))
            cps[-1].start()
        for cp in cps:
            cp.wait()

    return _sequencer_call(
        body, f"rs_pair_swap{tag}", cid, [jax.ShapeDtypeStruct(a.shape, a.dtype) for a in g8s],
        [pltpu.SemaphoreType.DMA((n,)), pltpu.SemaphoreType.DMA((n,))], g8s)


def _all_reduce_small(vec, name):
    r, cols = vec.shape

    def body(v_ref, o_ref, gath, send_sems, recv_sems):
        x, y, c, _ = _position()
        me = 4 * x + 2 * y + c
        gath[me] = v_ref[...]
        cps = []
        for rel in range(1, N_DEV):
            peer = (x ^ (rel >> 2), y ^ ((rel >> 1) & 1), c ^ (rel & 1))
            cps.append(pltpu.make_async_remote_copy(
                src_ref=v_ref, dst_ref=gath.at[me], send_sem=send_sems.at[rel - 1], recv_sem=recv_sems.at[rel - 1],
                device_id=peer, device_id_type=MESH))
        for cp in cps:
            cp.start()
        for rel in range(1, N_DEV):
            pltpu.make_async_remote_copy(
                src_ref=v_ref, dst_ref=gath.at[me ^ rel], send_sem=send_sems.at[rel - 1],
                recv_sem=recv_sems.at[rel - 1], device_id=(x, y, c), device_id_type=MESH).wait_recv()
        for cp in cps:
            cp.wait_send()
        acc = gath[0]
        for d in range(1, N_DEV):
            acc = acc + gath[d]
        o_ref[...] = acc

    vm = pl.BlockSpec(memory_space=pltpu.VMEM)
    return _tc_call(
        body, name=name, in_specs=[vm], out_specs=vm, out_shape=jax.ShapeDtypeStruct((r, cols), F32),
        scratch_shapes=[pltpu.VMEM((N_DEV, r, cols), F32), pltpu.SemaphoreType.DMA((N_DEV - 1,)),
                        pltpu.SemaphoreType.DMA((N_DEV - 1,))],
    )(vec)


def _rope_tables(positions):
    half = QK_ROPE // 2
    inv_freq = 1.0 / (ROPE_THETA ** (jnp.arange(half, dtype=F32) / half))
    ang = positions.astype(F32)[:, None] * inv_freq
    zeros = jnp.zeros((positions.shape[0], LANES - QK_ROPE), F32)
    cos, sin = jnp.cos(ang), jnp.sin(ang)
    return jnp.concatenate([cos, cos, zeros], axis=1), jnp.concatenate([sin, sin, zeros], axis=1)


def _local_step(x, positions, tgt, wf, small, rs):
    cos, sin = _rope_tables(positions)
    w_in, w_out = wf["sc_w_in"], wf["sc_w_out"]
    w_ups, w_downs = (wf["ffn_w_up0"], wf["ffn_w_up1"]), (wf["ffn_w_down0"], wf["ffn_w_down1"])
    w_kv, w_ukv, w_dq, w_uq, w_o = wf["w_kv"], wf["w_ukv"], wf["w_dq"], wf["w_uq"], wf["w_o"]
    attn_norm, ffn_norm = small["attn_norm"], small["ffn_norm"]
    conv_b = small["ffn_conv_b"]

    def ffn_fwd(h, l):
        hf = _rms_fwd(h, ffn_norm[l:l + 1], f"ffn{l}_norm")
        up = _nn_parts(f"ffn{l}_up", hf, w_ups[l], 2, BF16)
        a = _gate_fwd(up, small["ffn_conv_w"][l], conv_b[l:l + 1], f"ffn{l}_gate")
        return _nn(f"ffn{l}_down", a, w_downs[l], F32, add=h), (hf, up, a)

    def ffn_bwd(h, dh_out, l, saved, gi, hooks):
        run = lambda stage: hooks.get(stage, lambda: None)()
        hf, up, a = saved
        da = _nt(f"ffn{l}_down_dx", dh_out, w_downs[l], BF16)
        run("down_dx")
        d_down = _tn(f"ffn{l}_down_dw", a, dh_out, BF16)
        dup, d_cw, d_cb = _gate_bwd(up, small["ffn_conv_w"][l], conv_b[l:l + 1], da, f"ffn{l}_gate_bwd")
        run("gate_bwd")
        d_up = _dw_ffn_up(f"ffn{l}_up_dw", hf, dup)
        rs.start(gi, {f"ffn_w_down{l}": d_down.reshape(N_CHIPS, F_FF // N_CHIPS, D), f"ffn_w_up{l}": d_up})
        dhf = _nt_parts(f"ffn{l}_up_dx", dup, w_ups[l], BF16)
        run("up_dx")
        dh, d_norm = _rms_bwd(h, ffn_norm[l:l + 1], dhf, dh_out, f"ffn{l}_norm_bwd")
        return dh, d_cw, d_cb, d_norm

    hn0 = _rms_fwd(x, attn_norm[0:1], "attn0_norm")
    z = _nn_parts("sc_in", hn0, w_in, 3, BF16)
    mix = _scmix_fwd(z, small["sc_conv_w"])
    h1 = _nn("sc_out", mix, w_out, F32, add=x)
    h2, ffn0_saved = ffn_fwd(h1, 0)

    hk = _rms_fwd(h2, small["kv_in_norm"], "kv_in_norm")
    kvpre = _nn("kv_down", hk, w_kv, F32)
    ckv, kr = _kv_elem_fwd(kvpre, small["kv_latent_norm"], cos, sin)
    knv = _nn_parts("kv_up", ckv, w_ukv, 2, BF16, stacked=True)

    hn1 = _rms_fwd(h2, attn_norm[1:2], "attn1_norm")
    cq_pre = _nn("q_down", hn1, w_dq, F32)
    cq = _rms_fwd(cq_pre, small["q_latent_norm"], "q_latent_norm")
    q = _q_rope_fwd(_nn("q_up", cq, w_uq, F32), cos, sin)
    o = _attn_fwd(q, knv, kr)
    h3 = _nn("attn_out", o, w_o, F32, add=h2)
    h4, ffn1_saved = ffn_fwd(h3, 1)

    loss, dh4, d_final = _loss_head(h4, small["final_norm"], tgt)

    rows = D // N_CHIPS
    dh3, d_cw1, d_cb1, d_fn1 = ffn_bwd(h3, dh4, 1, ffn1_saved, 0, {})

    do = _nt("attn_out_dx", dh3, w_o, BF16)
    d_wo = _tn("attn_out_dw", o, dh3, BF16)
    rs.pair_sums(0)
    dq, dknv, dkr = _attn_bwd(q, knv, kr, do, cos, sin)
    rs.chip_sums(0)
    dcq = _nt("q_up_dx", dq, w_uq, F32)
    d_wuq = _tn("q_up_dw", cq, dq, BF16).reshape(Q_LORA, N_CHIPS, -1).transpose(1, 0, 2)
    dcq_pre, d_qln = _rms_bwd(cq_pre, small["q_latent_norm"], dcq, None, "q_latent_norm_bwd")
    rs.finish(0)
    dhn1 = _nt("q_down_dx", dcq_pre, w_dq, BF16)
    d_wdq = _tn("q_down_dw", hn1, dcq_pre, BF16)
    dh2, d_an1 = _rms_bwd(h2, attn_norm[1:2], dhn1, dh3, "attn1_norm_bwd")

    dckv = _nt_parts("kv_up_dx", dknv, w_ukv, F32, stacked=True)
    d_wukv = _dw_ukv(ckv, dknv)
    dkvpre, d_kvln = _kv_elem_bwd(kvpre, small["kv_latent_norm"], dckv, dkr, cos, sin)
    dhk = _nt("kv_down_dx", dkvpre, w_kv, BF16)
    d_wkv = _tn("kv_down_dw", hk, dkvpre, BF16)
    rs.start(1, {
        "w_o": d_wo.reshape(N_CHIPS, rows, D), "w_uq": d_wuq, "w_dq": d_wdq.reshape(N_CHIPS, rows, Q_LORA),
        "w_ukv": d_wukv.reshape(N_CHIPS, 2 * KV_LORA, -1), "w_kv": d_wkv.reshape(N_CHIPS, rows, KVP),
    })
    dh2, d_kvin = _rms_bwd(h2, small["kv_in_norm"], dhk, dh2, "kv_in_norm_bwd")

    dh1, d_cw0, d_cb0, d_fn0 = ffn_bwd(h1, dh2, 0, ffn0_saved, 2, {
        "down_dx": lambda: rs.pair_sums(1), "gate_bwd": lambda: rs.chip_sums(1), "up_dx": lambda: rs.finish(1)})
    rs.pair_sums(2)

    d_wout = _tn("sc_out_dw", mix, dh1, BF16)
    dmix = _nt("sc_out_dx", dh1, w_out, BF16)
    dz, d_scw = _scmix_bwd(z, small["sc_conv_w"], dmix)
    d_win = _dw_sc_in(hn0, dz)
    rs.start(3, {"sc_w_out": d_wout.reshape(N_CHIPS, rows, D), "sc_w_in": d_win})
    dhn0 = _nt_parts("sc_in_dx", dz, w_in, BF16)
    dx, d_an0 = _rms_bwd(x, attn_norm[0:1], dhn0, dh1, "attn0_norm_bwd")

    small_g = {
        "attn_norm": jnp.concatenate([d_an0, d_an1]), "ffn_norm": jnp.concatenate([d_fn0, d_fn1]),
        "final_norm": d_final, "kv_in_norm": d_kvin, "kv_latent_norm": d_kvln, "q_latent_norm": d_qln,
        "ffn_conv_b": jnp.concatenate([d_cb0, d_cb1]), "sc_conv_w": d_scw, "ffn_conv_w": jnp.stack([d_cw0, d_cw1]),
    }
    return loss, dx, small_g


RS_GROUPS = (("ffn_w_down1", "ffn_w_up1"), ("w_o", "w_uq", "w_dq", "w_ukv", "w_kv"),
             ("ffn_w_down0", "ffn_w_up0"), ("sc_w_out", "sc_w_in"))


class _ReduceScatter:
    def __init__(self, ids, finish):
        self.ids, self.grads, self.step, self.mine, self.sib, self.finish = ids, {}, {}, {}, {}, finish

    def _cid(self, gi):
        return len(AG_GROUPS) + 3 * gi

    def start(self, gi, grads):
        self.grads.update(grads)
        own = [grads[n] for n in RS_GROUPS[gi]]
        self.step[gi] = (own, _pair_exchange(own, gi, self._cid(gi)))

    def pair_sums(self, gi):
        own, ra = self.step[gi]
        sums = [_pair_sum(self.ids, g, a, f"rs_pair_sum_{n}") for n, g, a in zip(RS_GROUPS[gi], own, ra)]
        self.step[gi] = (own, ra, _chip_exchange(sums, gi, self._cid(gi) + 1))

    def chip_sums(self, gi):
        own, ra, rb = self.step[gi]
        mine = [_chip_sum(self.ids, g, a, b, f"rs_chip_sum_{n}") for n, g, a, b in zip(RS_GROUPS[gi], own, ra, rb)]
        self.mine.update(zip(RS_GROUPS[gi], mine))
        self.sib.update(zip(RS_GROUPS[gi], _pair_swap(mine, gi, self._cid(gi) + 2)))

SMALL_REPL = ("attn_norm", "ffn_norm", "final_norm", "kv_in_norm", "kv_latent_norm", "q_latent_norm", "ffn_conv_b")
SMALL_SHARDED = ("sc_conv_w", "ffn_conv_w")
SMALL_ROWS = 256


def _pad_heads(w_uq):
    per_head = w_uq.reshape(Q_LORA, -1, QK_NOPE + QK_ROPE)
    return jnp.pad(per_head, ((0, 0), (0, 0), (0, HEAD_PAD - QK_NOPE - QK_ROPE))).reshape(Q_LORA, -1)


def _pack_kv(w_dkv, w_kr):
    return jnp.concatenate([w_dkv, w_kr, jnp.zeros((w_kr.shape[0], LANES - QK_ROPE), w_kr.dtype)], axis=1)


def kernel(x, positions, attn_norm, ffn_norm, final_norm, sc_w_in, sc_conv_w, sc_w_out, kv_in_norm, w_dkv, kv_latent_norm, w_kr, w_uk, w_uv, w_dq, q_latent_norm, w_uq, w_o, ffn_w_up, ffn_conv_w, ffn_conv_b, ffn_w_down, loss_target, m_attn_norm, m_ffn_norm, m_final_norm, m_sc_w_in, m_sc_conv_w, m_sc_w_out, m_kv_in_norm, m_w_dkv, m_kv_latent_norm, m_w_kr, m_w_uk, m_w_uv, m_w_dq, m_q_latent_norm, m_w_uq, m_w_o, m_ffn_w_up, m_ffn_conv_w, m_ffn_conv_b, m_ffn_w_down, v_attn_norm, v_ffn_norm, v_final_norm, v_sc_w_in, v_sc_conv_w, v_sc_w_out, v_kv_in_norm, v_w_dkv, v_kv_latent_norm, v_w_kr, v_w_uk, v_w_uv, v_w_dq, v_q_latent_norm, v_w_uq, v_w_o, v_ffn_w_up, v_ffn_conv_w, v_ffn_conv_b, v_ffn_w_down):
    names = ("attn_norm", "ffn_norm", "final_norm", "sc_w_in", "sc_conv_w", "sc_w_out", "kv_in_norm", "w_dkv",
             "kv_latent_norm", "w_kr", "w_uk", "w_uv", "w_dq", "q_latent_norm", "w_uq", "w_o", "ffn_w_up",
             "ffn_conv_w", "ffn_conv_b", "ffn_w_down")
    w = dict(zip(names, (attn_norm, ffn_norm, final_norm, sc_w_in, sc_conv_w, sc_w_out, kv_in_norm, w_dkv,
                         kv_latent_norm, w_kr, w_uk, w_uv, w_dq, q_latent_norm, w_uq, w_o, ffn_w_up,
                         ffn_conv_w, ffn_conv_b, ffn_w_down)))
    m = dict(zip(names, (m_attn_norm, m_ffn_norm, m_final_norm, m_sc_w_in, m_sc_conv_w, m_sc_w_out, m_kv_in_norm,
                         m_w_dkv, m_kv_latent_norm, m_w_kr, m_w_uk, m_w_uv, m_w_dq, m_q_latent_norm, m_w_uq, m_w_o,
                         m_ffn_w_up, m_ffn_conv_w, m_ffn_conv_b, m_ffn_w_down)))
    v = dict(zip(names, (v_attn_norm, v_ffn_norm, v_final_norm, v_sc_w_in, v_sc_conv_w, v_sc_w_out, v_kv_in_norm,
                         v_w_dkv, v_kv_latent_norm, v_w_kr, v_w_uk, v_w_uv, v_w_dq, v_q_latent_norm, v_w_uq, v_w_o,
                         v_ffn_w_up, v_ffn_conv_w, v_ffn_conv_b, v_ffn_w_down)))

    _ORDER[0] = None
    ix, iy, ic = lax.axis_index("x"), lax.axis_index("y"), lax.axis_index("c")
    chip = 2 * ix + iy
    ids = jnp.stack([ic, chip]).astype(jnp.int32)

    def shards_of(t):
        return {
            "sc_w_in": t["sc_w_in"][0], "sc_w_out": t["sc_w_out"][0], "ffn_w_up": t["ffn_w_up"],
            "ffn_w_down": t["ffn_w_down"], "w_kv": _pack_kv(t["w_dkv"], t["w_kr"]),
            "w_ukv": jnp.stack([t["w_uk"], t["w_uv"]]), "w_dq": t["w_dq"][0], "w_uq": _pad_heads(t["w_uq"][0]),
            "w_o": t["w_o"][0],
        }

    ws, ms, vs = shards_of(w), shards_of(m), shards_of(v)

    def ag_shard(name):
        if name == "sc_conv_w":
            return sc_conv_w[0]
        if name == "ffn_conv_w":
            return ffn_conv_w.reshape(6, -1)
        if name[:-1] in ("ffn_w_up", "ffn_w_down"):
            return ws[name[:-1]][int(name[-1])].astype(BF16)
        return ws[name].astype(BF16)

    wf = {}
    for gi, wms in enumerate(AG_GROUPS):
        fulls = _all_gather_group(gi, [ag_shard(wm.name) for wm in wms])
        wf.update({wm.name: f for wm, f in zip(wms, fulls)})
    small = {
        "attn_norm": attn_norm, "ffn_norm": ffn_norm, "final_norm": final_norm[None], "kv_in_norm": kv_in_norm[None],
        "kv_latent_norm": kv_latent_norm[None], "q_latent_norm": q_latent_norm, "ffn_conv_b": ffn_conv_b,
        "sc_conv_w": wf["sc_conv_w"].transpose(1, 0, 2).reshape(3, D),
        "ffn_conv_w": wf["ffn_conv_w"].reshape(N_CHIPS, 2, 3, -1).transpose(1, 2, 0, 3).reshape(2, 3, F_FF),
    }

    res = {}

    def adamw_plain(n):
        res[n] = _adamw_shard(ids, ws[n], ms[n], vs[n], rs.mine[n], rs.sib[n], f"adamw_{n}")

    def adamw_layer(n, layer):
        key = f"{n}{layer}"
        res[n] = _adamw_shard(ids, ws[n], ms[n], vs[n], rs.mine[key], rs.sib[key], f"adamw_{key}", layer=layer,
                              prev=res.get(n))

    def adamw_group(gi):
        if gi in (0, 2):
            for n in ("ffn_w_up", "ffn_w_down"):
                adamw_layer(n, 1 if gi == 0 else 0)
        elif gi == 1:
            for n in ("w_kv", "w_dq", "w_uq", "w_o"):
                adamw_plain(n)
            merged = lambda a: a.reshape(2 * KV_LORA, -1)
            res["w_ukv"] = _adamw_shard(ids, merged(ws["w_ukv"]), merged(ms["w_ukv"]), merged(vs["w_ukv"]),
                                        rs.mine["w_ukv"], rs.sib["w_ukv"], "adamw_w_ukv")
        else:
            for n in ("sc_w_out", "sc_w_in"):
                adamw_plain(n)

    rs = _ReduceScatter(ids, adamw_group)
    loss, dx, small_g = _local_step(x[0], positions[0], loss_target[0], wf, small, rs)

    s_order = SMALL_REPL + SMALL_SHARDED
    flat = jnp.concatenate([small_g[n].reshape(-1) for n in s_order] + [loss.reshape(-1)])
    flat = jnp.pad(flat, (0, SMALL_ROWS * LANES - flat.shape[0])).reshape(SMALL_ROWS, LANES)
    red = _all_reduce_small(flat, "ar_small").reshape(-1)
    sg, off = {}, 0
    for n in s_order:
        sz = small_g[n].size
        sg[n] = red[off:off + sz].reshape(small_g[n].shape)
        off += sz
    loss_out = red[off]
    grads = {n: sg[n].reshape(w[n].shape) for n in SMALL_REPL}
    grads["sc_conv_w"] = lax.dynamic_slice_in_dim(sg["sc_conv_w"], chip * (D // N_CHIPS), D // N_CHIPS, axis=1)[None]
    grads["ffn_conv_w"] = lax.dynamic_slice_in_dim(sg["ffn_conv_w"], chip * (F_FF // N_CHIPS), F_FF // N_CHIPS, axis=2)

    rs.chip_sums(2)
    rs.pair_sums(3)
    rs.finish(2)
    rs.chip_sums(3)
    rs.finish(3)
    outs = [grads, {}, {}, {}]
    for k, dst in enumerate(outs):
        for n in ("sc_w_in", "sc_w_out", "w_dq", "w_o"):
            dst[n] = res[n][k][None]
        unpadded = res["w_uq"][k].reshape(Q_LORA, -1, HEAD_PAD)[:, :, :QK_NOPE + QK_ROPE]
        dst["w_uq"] = unpadded.reshape(w_uq.shape)
        dst["ffn_w_up"], dst["ffn_w_down"] = res["ffn_w_up"][k], res["ffn_w_down"][k]
        dst["w_dkv"], dst["w_kr"] = res["w_kv"][k][:, :KV_LORA], res["w_kv"][k][:, KV_LORA:KV_LORA + QK_ROPE]
        dst["w_uk"], dst["w_uv"] = res["w_ukv"][k][:KV_LORA], res["w_ukv"][k][KV_LORA:]
    grads, delta, new_m, new_v = outs

    small_names = SMALL_REPL + SMALL_SHARDED

    def pack_small(tree):
        return jnp.concatenate([tree[n].reshape(-1) for n in small_names]).reshape(-1, LANES)

    small_res = _adamw_small(pack_small(w), pack_small(grads), pack_small(m), pack_small(v))
    for slab, dst in zip(small_res, (delta, new_m, new_v)):
        f, off = slab.reshape(-1), 0
        for n in small_names:
            dst[n] = f[off:off + w[n].size].reshape(w[n].shape)
            off += w[n].size

    _ORDER[0] = None
    return (loss_out, dx[None], *[grads[n] for n in names], *[delta[n] for n in names],
            *[new_m[n] for n in names], *[new_v[n] for n in names])
```

```python
from typing import NamedTuple

import jax
import jax.numpy as jnp
from jax import lax
from jax.experimental import pallas as pl
from jax.experimental.pallas import tpu as pltpu
from jax.experimental.pallas import tpu_sc as plsc

F32 = jnp.float32
BF16 = jnp.bfloat16

T = 2048
D = 1024
F_FF = 2816
N_HEADS = 8
QK_NOPE = 128
QK_ROPE = 64
V_HEAD = 128
Q_LORA = 384
KV_LORA = 256
CHUNK_SHIFT = 6
ROPE_THETA = 10000.0
EPS = 1e-6
NEG_INF = -1e30
HEAD_PAD = 256
KVP = KV_LORA + 128

ADAM_LR = 0.001
ADAM_B1 = 0.9
ADAM_B2 = 0.999
ADAM_EPS = 1e-08
ADAM_WD = 0.01
ADAM_STEP = 10

N_CHIPS = 4
N_DEV = 8
LANES = 128
TC = 256
V7X_VMEM_LIMIT = 56 * 1024 * 1024

MESH = pl.DeviceIdType.MESH
ANY = pl.BlockSpec(memory_space=pl.ANY)


class _W(NamedTuple):
    name: str
    kind: str
    nl: int
    k: int
    n: int


AG_GROUPS = (
    (_W("sc_w_in", "col", 1, D, 3 * D // N_CHIPS), _W("sc_conv_w", "tiny", 1, 3, D // N_CHIPS),
     _W("ffn_conv_w", "tiny", 1, 6, F_FF // N_CHIPS)),
    (_W("sc_w_out", "row", 1, D // N_CHIPS, D),),
    (_W("ffn_w_up0", "col", 1, D, 2 * F_FF // N_CHIPS),),
    (_W("ffn_w_down0", "row", 1, F_FF // N_CHIPS, D),),
    (_W("w_kv", "row", 1, D // N_CHIPS, KVP), _W("w_ukv", "col", 2, KV_LORA, N_HEADS * QK_NOPE // N_CHIPS),
     _W("w_dq", "row", 1, D // N_CHIPS, Q_LORA),
     _W("w_uq", "col", 1, Q_LORA, N_HEADS * HEAD_PAD // N_CHIPS),
     _W("w_o", "row", 1, N_HEADS * V_HEAD // N_CHIPS, D)),
    (_W("ffn_w_up1", "col", 1, D, 2 * F_FF // N_CHIPS), _W("ffn_w_down1", "row", 1, F_FF // N_CHIPS, D)),
)


def _cp(*sem):
    return pltpu.CompilerParams(dimension_semantics=sem, vmem_limit_bytes=V7X_VMEM_LIMIT)


_ORDER = [None]


def _tc_call(body, *, name, out_shape, in_specs=None, out_specs=None, grid=(), scratch_shapes=(), prefetch=0,
             input_output_aliases=None, compiler_params=None):
    def run(*args):
        specs = [pl.BlockSpec(memory_space=pltpu.VMEM)] * (len(args) - prefetch) if in_specs is None else list(in_specs)
        inner, dep = body, _ORDER[0]
        if dep is not None:
            unread = prefetch + len(specs)
            specs, args = specs + [ANY], (*args, dep)

            def inner(*refs):
                return body(*refs[:unread], *refs[unread + 1:])

        kwargs = dict(name=name, out_shape=out_shape, input_output_aliases=input_output_aliases or {},
                      compiler_params=compiler_params)
        if prefetch:
            kwargs["grid_spec"] = pltpu.PrefetchScalarGridSpec(
                num_scalar_prefetch=prefetch, grid=grid, in_specs=specs, out_specs=out_specs,
                scratch_shapes=scratch_shapes)
        else:
            kwargs.update(grid=grid, in_specs=specs, scratch_shapes=scratch_shapes)
            if out_specs is not None:
                kwargs["out_specs"] = out_specs
        out = pl.pallas_call(inner, **kwargs)(*args)
        _ORDER[0] = out[0] if isinstance(out, (list, tuple)) else out
        return out

    return run


def _tile(n, cands):
    for c in cands:
        if n % c == 0:
            return c
    raise ValueError(f"no tile for {n}")


NN_DIMS = (((1,), (0,)), ((), ()))
NT_DIMS = (((1,), (1,)), ((), ()))
TN_DIMS = (((0,), (0,)), ((), ()))
M_TILES = (1024, 512, 384, 256, 128)
N_TILES = (512, 384, 256, 128)


def _mm(name, a, b, dims, grid, a_spec, b_spec, o_spec, o_sds, add=None, red=None, acc_shape=None):
    n_red = None if red is None else grid[red]

    def body(*refs):
        a_ref, b_ref = refs[0], refs[1]
        add_ref = refs[2] if add is not None else None
        o_ref = refs[3] if add is not None else refs[2]
        part = lax.dot_general(a_ref[...].astype(BF16), b_ref[...].astype(BF16), dims, preferred_element_type=F32)
        if red is None:
            if add is not None:
                part = part + add_ref[...]
            o_ref[...] = part.astype(o_ref.dtype)
            return
        acc_ref = refs[-1]
        r = pl.program_id(red)

        @pl.when(r == 0)
        def _():
            acc_ref[...] = part

        @pl.when(r > 0)
        def _():
            acc_ref[...] += part

        @pl.when(r == n_red - 1)
        def _():
            o_ref[...] = acc_ref[...].astype(o_ref.dtype)

    sem = tuple("arbitrary" if ax == red else "parallel" for ax in range(len(grid)))
    in_specs = [a_spec, b_spec] + ([o_spec] if add is not None else [])
    args = (a, b) + ((add,) if add is not None else ())
    return _tc_call(
        body, name=name, grid=grid, in_specs=in_specs, out_specs=o_spec, out_shape=o_sds,
        scratch_shapes=[] if red is None else [pltpu.VMEM(acc_shape, F32)], compiler_params=_cp(*sem),
    )(*args)


def _nn(name, a, b, out_dtype, add=None, lead=None):
    (m, k), n = a.shape, b.shape[-1]
    tm, tn = _tile(m, M_TILES), _tile(n, N_TILES)
    if lead is None:
        b_spec = pl.BlockSpec((k, tn), lambda i, j: (0, j))
    else:
        b_spec = pl.BlockSpec((None, k, tn), lambda i, j: (lead, 0, j))
    return _mm(name, a, b, NN_DIMS, (m // tm, n // tn), pl.BlockSpec((tm, k), lambda i, j: (i, 0)), b_spec,
               pl.BlockSpec((tm, tn), lambda i, j: (i, j)), jax.ShapeDtypeStruct((m, n), out_dtype), add=add)


def _nn_parts(name, a, b, parts, out_dtype, lead=None, stacked=False):
    m, k = a.shape
    c = b.shape[-1] if stacked else b.shape[-1] // parts
    tm, tn = _tile(m, M_TILES), _tile(c, N_TILES)
    nb = c // tn
    if stacked:
        b_spec = pl.BlockSpec((None, k, tn), lambda i, p, j: (p, 0, j))
    elif lead is None:
        b_spec = pl.BlockSpec((k, tn), lambda i, p, j: (0, p * nb + j))
    else:
        b_spec = pl.BlockSpec((None, k, tn), lambda i, p, j: (lead, 0, p * nb + j))
    return _mm(name, a, b, NN_DIMS, (m // tm, parts, nb), pl.BlockSpec((tm, k), lambda i, p, j: (i, 0)), b_spec,
               pl.BlockSpec((None, tm, tn), lambda i, p, j: (p, i, j)), jax.ShapeDtypeStruct((parts, m, c), out_dtype))


def _nt(name, a, b, out_dtype, lead=None):
    (m, k), n = a.shape, b.shape[-2]
    tm, tn = _tile(m, M_TILES), _tile(n, N_TILES)
    if lead is None:
        b_spec = pl.BlockSpec((tn, k), lambda i, j: (j, 0))
    else:
        b_spec = pl.BlockSpec((None, tn, k), lambda i, j: (lead, j, 0))
    return _mm(name, a, b, NT_DIMS, (m // tm, n // tn), pl.BlockSpec((tm, k), lambda i, j: (i, 0)), b_spec,
               pl.BlockSpec((tm, tn), lambda i, j: (i, j)), jax.ShapeDtypeStruct((m, n), out_dtype))


def _nt_parts(name, a, b, out_dtype, lead=None, stacked=False):
    parts, m, c = a.shape
    n = b.shape[-2]
    tm, tn = _tile(m, M_TILES), _tile(n, N_TILES)
    if stacked:
        b_spec = pl.BlockSpec((None, tn, c), lambda i, j, p: (p, j, 0))
    elif lead is None:
        b_spec = pl.BlockSpec((tn, c), lambda i, j, p: (j, p))
    else:
        b_spec = pl.BlockSpec((None, tn, c), lambda i, j, p: (lead, j, p))
    return _mm(name, a, b, NT_DIMS, (m // tm, n // tn, parts), pl.BlockSpec((None, tm, c), lambda i, j, p: (p, i, 0)),
               b_spec, pl.BlockSpec((tm, tn), lambda i, j, p: (i, j)), jax.ShapeDtypeStruct((m, n), out_dtype),
               red=2, acc_shape=(tm, tn))


def _tn(name, a, b, out_dtype):
    (k, m), n = a.shape, b.shape[1]
    tm, tn = _tile(m, M_TILES), _tile(n, N_TILES)
    return _mm(name, a, b, TN_DIMS, (m // tm, n // tn), pl.BlockSpec((k, tm), lambda i, j: (0, i)),
               pl.BlockSpec((k, tn), lambda i, j: (0, j)), pl.BlockSpec((tm, tn), lambda i, j: (i, j)),
               jax.ShapeDtypeStruct((m, n), out_dtype))


def _dw_sc_in(hn, dz):
    t, tn, tm = hn.shape[0], TC, 512
    per_part, per_chip = D // tn, 3 * D // N_CHIPS // tn
    return _mm("sc_in_dw", hn, dz, TN_DIMS, (D // tm, 3 * D // tn), pl.BlockSpec((t, tm), lambda i, j: (0, i)),
               pl.BlockSpec((None, t, tn), lambda i, j: (j // per_part, 0, j % per_part)),
               pl.BlockSpec((None, tm, tn), lambda i, j: (j // per_chip, i, j % per_chip)),
               jax.ShapeDtypeStruct((N_CHIPS, D, 3 * D // N_CHIPS), BF16))


def _dw_ffn_up(name, hf, dup):
    t, tm, ns = hf.shape[0], 512, 2 * F_FF // N_CHIPS
    return _mm(name, hf, dup, TN_DIMS, (N_CHIPS, D // tm), pl.BlockSpec((t, tm), lambda s, i: (0, i)),
               pl.BlockSpec((None, t, ns), lambda s, i: (s // 2, 0, s % 2)),
               pl.BlockSpec((None, tm, ns), lambda s, i: (s, i, 0)), jax.ShapeDtypeStruct((N_CHIPS, D, ns), BF16))


def _dw_ukv(ckv, dknv):
    t, ns = ckv.shape[0], N_HEADS * QK_NOPE // N_CHIPS
    return _mm("kv_up_dw", ckv, dknv, TN_DIMS, (2, N_CHIPS), pl.BlockSpec((t, KV_LORA), lambda p, s: (0, 0)),
               pl.BlockSpec((None, t, ns), lambda p, s: (p, 0, s)),
               pl.BlockSpec((None, None, KV_LORA, ns), lambda p, s: (s, p, 0, 0)),
               jax.ShapeDtypeStruct((N_CHIPS, 2, KV_LORA, ns), BF16))


def _rms_fwd(x, g, name):
    t, d = x.shape
    tr = 512

    def body(x_ref, g_ref, o_ref):
        xv = x_ref[...]
        r = lax.rsqrt(jnp.mean(xv * xv, axis=1, keepdims=True) + EPS)
        o_ref[...] = (xv * r * g_ref[...]).astype(o_ref.dtype)

    row = pl.BlockSpec((tr, d), lambda i: (i, 0))
    return _tc_call(
        body, name=name, grid=(t // tr,), in_specs=[row, pl.BlockSpec((1, d), lambda i: (0, 0))],
        out_specs=row, out_shape=jax.ShapeDtypeStruct((t, d), BF16), compiler_params=_cp("parallel"),
    )(x, g)


def _rms_bwd_math(xv, g, dy):
    r = lax.rsqrt(jnp.mean(xv * xv, axis=1, keepdims=True) + EPS)
    xh = xv * r
    gy = dy * g
    dx = r * (gy - xh * jnp.mean(gy * xh, axis=1, keepdims=True))
    dg = jnp.sum(dy * xh, axis=0, keepdims=True)
    return dx, dg


def _rms_bwd(x, g, dy, add, name):
    t, d = x.shape
    tr = 512

    def body(*refs):
        if add is None:
            x_ref, g_ref, dy_ref, dx_ref, dg_ref = refs
        else:
            x_ref, g_ref, dy_ref, add_ref, dx_ref, dg_ref = refs
        dx, dg = _rms_bwd_math(x_ref[...], g_ref[...], dy_ref[...].astype(F32))
        if add is not None:
            dx = dx + add_ref[...]
        dx_ref[...] = dx

        @pl.when(pl.program_id(0) == 0)
        def _():
            dg_ref[...] = jnp.zeros_like(dg_ref)

        dg_ref[...] += dg

    row = pl.BlockSpec((tr, d), lambda i: (i, 0))
    vec = pl.BlockSpec((1, d), lambda i: (0, 0))
    in_specs = [row, vec, row] + ([row] if add is not None else [])
    args = (x, g, dy) + ((add,) if add is not None else ())
    return _tc_call(
        body, name=name, grid=(t // tr,), in_specs=in_specs, out_specs=[row, vec],
        out_shape=[jax.ShapeDtypeStruct((t, d), F32), jax.ShapeDtypeStruct((1, d), F32)],
        compiler_params=_cp("arbitrary"),
    )(*args)


def _loss_head(h, g, tgt):
    t, d = h.shape
    tr = 512

    def body(h_ref, g_ref, t_ref, loss_ref, dh_ref, dg_ref):
        xv = h_ref[...]
        gv = g_ref[...]
        r = lax.rsqrt(jnp.mean(xv * xv, axis=1, keepdims=True) + EPS)
        err = xv * r * gv - t_ref[...]
        part = 0.5 * jnp.sum(jnp.mean(err * err, axis=1, keepdims=True), axis=0, keepdims=True)
        dx, dg = _rms_bwd_math(xv, gv, err * (1.0 / d))
        dh_ref[...] = dx

        @pl.when(pl.program_id(0) == 0)
        def _():
            dg_ref[...] = jnp.zeros_like(dg_ref)
            loss_ref[...] = jnp.zeros_like(loss_ref)

        dg_ref[...] += dg
        loss_ref[...] += jnp.broadcast_to(part, loss_ref.shape)

    row = pl.BlockSpec((tr, d), lambda i: (i, 0))
    vec = pl.BlockSpec((1, d), lambda i: (0, 0))
    lspec = pl.BlockSpec((1, LANES), lambda i: (0, 0))
    return _tc_call(
        body, name="loss_head", grid=(t // tr,), in_specs=[row, vec, row], out_specs=[lspec, row, vec],
        out_shape=[jax.ShapeDtypeStruct((1, LANES), F32), jax.ShapeDtypeStruct((t, d), F32),
                   jax.ShapeDtypeStruct((1, d), F32)],
        compiler_params=_cp("arbitrary"),
    )(h, g, tgt)


def _rot_half(x):
    lane = lax.broadcasted_iota(jnp.int32, x.shape, 1)
    return jnp.where((lane % QK_ROPE) < QK_ROPE // 2, -pltpu.roll(x, LANES - 32, axis=1),
                     pltpu.roll(x, 32, axis=1))


def _rope_fwd_math(x, cos, sin):
    return x * cos + _rot_half(x) * sin


def _rope_bwd_math(dy, cos, sin):
    return dy * cos - _rot_half(dy * sin)


def _q_rope_fwd(qpre, cos, sin):
    t, w = qpre.shape
    tr = 256

    def body(q_ref, c_ref, s_ref, o_ref):
        cv, sv = c_ref[...], s_ref[...]
        for h in range(N_HEADS):
            lo = h * HEAD_PAD
            o_ref[:, lo:lo + QK_NOPE] = q_ref[:, lo:lo + QK_NOPE].astype(BF16)
            o_ref[:, lo + QK_NOPE:lo + HEAD_PAD] = _rope_fwd_math(
                q_ref[:, lo + QK_NOPE:lo + HEAD_PAD], cv, sv).astype(BF16)

    row = pl.BlockSpec((tr, w), lambda i: (i, 0))
    tab = pl.BlockSpec((tr, LANES), lambda i: (i, 0))
    return _tc_call(
        body, name="q_rope_fwd", grid=(t // tr,), in_specs=[row, tab, tab], out_specs=row,
        out_shape=jax.ShapeDtypeStruct((t, w), BF16), compiler_params=_cp("parallel"),
    )(qpre, cos, sin)


def _kv_elem_fwd(kvpre, g, cos, sin):
    t = kvpre.shape[0]
    tr = 512

    def body(p_ref, g_ref, c_ref, s_ref, ckv_ref, kr_ref):
        lat = p_ref[:, :KV_LORA]
        r = lax.rsqrt(jnp.mean(lat * lat, axis=1, keepdims=True) + EPS)
        ckv_ref[...] = (lat * r * g_ref[...]).astype(BF16)
        kr_ref[...] = _rope_fwd_math(p_ref[:, KV_LORA:], c_ref[...], s_ref[...]).astype(BF16)

    tab = pl.BlockSpec((tr, LANES), lambda i: (i, 0))
    return _tc_call(
        body, name="kv_elem_fwd", grid=(t // tr,),
        in_specs=[pl.BlockSpec((tr, KVP), lambda i: (i, 0)), pl.BlockSpec((1, KV_LORA), lambda i: (0, 0)), tab, tab],
        out_specs=[pl.BlockSpec((tr, KV_LORA), lambda i: (i, 0)), tab],
        out_shape=[jax.ShapeDtypeStruct((t, KV_LORA), BF16), jax.ShapeDtypeStruct((t, LANES), BF16)],
        compiler_params=_cp("parallel"),
    )(kvpre, g, cos, sin)


def _kv_elem_bwd(kvpre, g, dckv, dkr, cos, sin):
    t = kvpre.shape[0]
    tr = 512

    def body(p_ref, g_ref, dc_ref, dk_ref, c_ref, s_ref, dp_ref, dg_ref):
        dlat, dg = _rms_bwd_math(p_ref[:, :KV_LORA], g_ref[...], dc_ref[...])
        dp_ref[:, :KV_LORA] = dlat.astype(BF16)
        dp_ref[:, KV_LORA:] = _rope_bwd_math(dk_ref[...], c_ref[...], s_ref[...]).astype(BF16)

        @pl.when(pl.program_id(0) == 0)
        def _():
            dg_ref[...] = jnp.zeros_like(dg_ref)

        dg_ref[...] += dg

    tab = pl.BlockSpec((tr, LANES), lambda i: (i, 0))
    pre = pl.BlockSpec((tr, KVP), lambda i: (i, 0))
    vec = pl.BlockSpec((1, KV_LORA), lambda i: (0, 0))
    return _tc_call(
        body, name="kv_elem_bwd", grid=(t // tr,),
        in_specs=[pre, vec, pl.BlockSpec((tr, KV_LORA), lambda i: (i, 0)), tab, tab, tab],
        out_specs=[pre, vec],
        out_shape=[jax.ShapeDtypeStruct((t, KVP), BF16), jax.ShapeDtypeStruct((1, KV_LORA), F32)],
        compiler_params=_cp("arbitrary"),
    )(kvpre, g, dckv, dkr, cos, sin)


def _shift_down(x, k):
    row = lax.broadcasted_iota(jnp.int32, x.shape, 0)
    return jnp.where(row >= k, pltpu.roll(x, k, axis=0), 0.0)


def _shift_up(x, k):
    n = x.shape[0]
    row = lax.broadcasted_iota(jnp.int32, x.shape, 0)
    return jnp.where(row < n - k, pltpu.roll(x, n - k, axis=0), 0.0)


def _conv3(x, w_ref):
    return _shift_down(x, 2) * w_ref[0:1, :] + _shift_down(x, 1) * w_ref[1:2, :] + x * w_ref[2:3, :]


def _conv3_t(dy, w_ref):
    return dy * w_ref[2:3, :] + _shift_up(dy, 1) * w_ref[1:2, :] + _shift_up(dy, 2) * w_ref[0:1, :]


def _conv3_dw(dy, x, dw_ref):
    dw_ref[0:1, :] = jnp.sum(dy * _shift_down(x, 2), axis=0, keepdims=True)
    dw_ref[1:2, :] = jnp.sum(dy * _shift_down(x, 1), axis=0, keepdims=True)
    dw_ref[2:3, :] = jnp.sum(dy * x, axis=0, keepdims=True)


def _col(parts, t):
    if parts is None:
        return pl.BlockSpec((t, TC), lambda j: (0, j))
    return pl.BlockSpec((parts, t, TC), lambda j: (0, 0, j))


def _scmix_fwd(z, w):
    t = z.shape[1]

    def body(z_ref, w_ref, m_ref):
        b, c, u = (z_ref[p].astype(F32) for p in range(3))
        m_ref[...] = (b * _conv3(c * u, w_ref)).astype(BF16)

    return _tc_call(
        body, name="scmix_fwd", grid=(D // TC,), in_specs=[_col(3, t), pl.BlockSpec((3, TC), lambda j: (0, j))],
        out_specs=_col(None, t), out_shape=jax.ShapeDtypeStruct((t, D), BF16), compiler_params=_cp("parallel"),
    )(z, w)


def _scmix_bwd(z, w, dm):
    t = z.shape[1]

    def body(z_ref, w_ref, dm_ref, dz_ref, dw_ref):
        c, u = z_ref[1].astype(F32), z_ref[2].astype(F32)
        cu = c * u
        dmv = dm_ref[...].astype(F32)
        dz_ref[0] = (dmv * _conv3(cu, w_ref)).astype(BF16)
        dcv = dmv * z_ref[0].astype(F32)
        _conv3_dw(dcv, cu, dw_ref)
        dcu = _conv3_t(dcv, w_ref)
        dz_ref[1] = (dcu * u).astype(BF16)
        dz_ref[2] = (dcu * c).astype(BF16)

    wspec = pl.BlockSpec((3, TC), lambda j: (0, j))
    return _tc_call(
        body, name="scmix_bwd", grid=(D // TC,), in_specs=[_col(3, t), wspec, _col(None, t)],
        out_specs=[_col(3, t), wspec],
        out_shape=[jax.ShapeDtypeStruct((3, t, D), BF16), jax.ShapeDtypeStruct((3, D), F32)],
        compiler_params=_cp("parallel"),
    )(z, w, dm)


def _gate_fwd(up, w, bias, name):
    t = up.shape[1]

    def body(u_ref, w_ref, b_ref, a_ref):
        gc = _conv3(u_ref[0].astype(F32), w_ref) + b_ref[...]
        a_ref[...] = (gc * jax.nn.sigmoid(gc) * u_ref[1].astype(F32)).astype(BF16)

    return _tc_call(
        body, name=name, grid=(F_FF // TC,),
        in_specs=[_col(2, t), pl.BlockSpec((3, TC), lambda j: (0, j)), pl.BlockSpec((1, TC), lambda j: (0, j))],
        out_specs=_col(None, t), out_shape=jax.ShapeDtypeStruct((t, F_FF), BF16), compiler_params=_cp("parallel"),
    )(up, w, bias)


def _gate_bwd(up, w, bias, da, name):
    t = up.shape[1]

    def body(u_ref, w_ref, b_ref, da_ref, du_ref, dw_ref, db_ref):
        g = u_ref[0].astype(F32)
        gc = _conv3(g, w_ref) + b_ref[...]
        sg = jax.nn.sigmoid(gc)
        dav = da_ref[...].astype(F32)
        du_ref[1] = (dav * (gc * sg)).astype(BF16)
        dgc = dav * u_ref[1].astype(F32) * (sg * (1.0 + gc * (1.0 - sg)))
        db_ref[...] = jnp.sum(dgc, axis=0, keepdims=True)
        _conv3_dw(dgc, g, dw_ref)
        du_ref[0] = _conv3_t(dgc, w_ref).astype(BF16)

    wspec = pl.BlockSpec((3, TC), lambda j: (0, j))
    bspec = pl.BlockSpec((1, TC), lambda j: (0, j))
    return _tc_call(
        body, name=name, grid=(F_FF // TC,), in_specs=[_col(2, t), wspec, bspec, _col(None, t)],
        out_specs=[_col(2, t), wspec, bspec],
        out_shape=[jax.ShapeDtypeStruct((2, t, F_FF), BF16), jax.ShapeDtypeStruct((3, F_FF), F32),
                   jax.ShapeDtypeStruct((1, F_FF), F32)],
        compiler_params=_cp("parallel"),
    )(up, w, bias, da)


ATT_TQ = 256
ATT_SCALE = (QK_NOPE + QK_ROPE) ** -0.5


def _key_ranges(lvl):
    lo = lvl * ATT_TQ
    return ([(0, lo, False)] if lvl else []) + [(lo, lo + ATT_TQ, True)]


def _attn_probs(q, kn_ref, kr_ref, lvl):
    scores = []
    for lo, hi, diagonal in _key_ranges(lvl):
        s = lax.dot_general(q[:, :QK_NOPE], kn_ref[lo:hi, :], NT_DIMS, preferred_element_type=F32)
        s = s + lax.dot_general(q[:, QK_NOPE:], kr_ref[lo:hi, :], NT_DIMS, preferred_element_type=F32)
        s = s * ATT_SCALE
        if diagonal:
            row = lax.broadcasted_iota(jnp.int32, s.shape, 0)
            col = lax.broadcasted_iota(jnp.int32, s.shape, 1)
            seen = lax.shift_right_logical(col, CHUNK_SHIFT) <= lax.shift_right_logical(row, CHUNK_SHIFT)
            s = jnp.where(seen, s, NEG_INF)
        scores.append(s)
    m = jnp.max(scores[0], axis=1, keepdims=True)
    for s in scores[1:]:
        m = jnp.maximum(m, jnp.max(s, axis=1, keepdims=True))
    ps = [jnp.exp(s - m) for s in scores]
    total = jnp.sum(ps[0], axis=1, keepdims=True)
    for p in ps[1:]:
        total = total + jnp.sum(p, axis=1, keepdims=True)
    inv = 1.0 / total
    return [p * inv for p in ps]


def _per_query_block(qi, n_blocks, branch):
    for lvl in range(n_blocks):
        pl.when(qi == lvl)(lambda lvl=lvl: branch(lvl))


def _attn_specs(t):
    q = pl.BlockSpec((ATT_TQ, HEAD_PAD), lambda h, i: (i, h))
    kn = pl.BlockSpec((None, t, QK_NOPE), lambda h, i: (0, 0, h))
    kr = pl.BlockSpec((t, LANES), lambda h, i: (0, 0))
    v = pl.BlockSpec((None, t, V_HEAD), lambda h, i: (1, 0, h))
    o = pl.BlockSpec((ATT_TQ, V_HEAD), lambda h, i: (i, h))
    return q, kn, kr, v, o


def _attn_fwd(q, knv, kr):
    t = q.shape[0]

    def body(q_ref, kn_ref, kr_ref, v_ref, o_ref):
        def branch(lvl):
            ps = _attn_probs(q_ref[...], kn_ref, kr_ref, lvl)
            o = None
            for p, (lo, hi, _) in zip(ps, _key_ranges(lvl)):
                part = jnp.dot(p.astype(BF16), v_ref[lo:hi, :], preferred_element_type=F32)
                o = part if o is None else o + part
            o_ref[...] = o.astype(BF16)

        _per_query_block(pl.program_id(1), t // ATT_TQ, branch)

    qs, kns, krs, vs, os_ = _attn_specs(t)
    return _tc_call(
        body, name="attn_fwd", grid=(N_HEADS, t // ATT_TQ), in_specs=[qs, kns, krs, vs], out_specs=os_,
        out_shape=jax.ShapeDtypeStruct((t, N_HEADS * V_HEAD), BF16), compiler_params=_cp("parallel", "parallel"),
    )(q, knv, kr, knv)


def _attn_bwd(q, knv, kr, do, cos, sin):
    t = q.shape[0]

    def body(q_ref, kn_ref, kr_ref, v_ref, do_ref, c_ref, s_ref, dq_ref, dknv_ref, dkr_ref):
        h, qi = pl.program_id(0), pl.program_id(1)

        @pl.when(qi == 0)
        def _():
            dknv_ref[...] = jnp.zeros_like(dknv_ref)

        @pl.when((qi == 0) & (h == 0))
        def _():
            dkr_ref[...] = jnp.zeros_like(dkr_ref)

        def branch(lvl):
            qv, dov = q_ref[...], do_ref[...]
            ranges = _key_ranges(lvl)
            ps = _attn_probs(qv, kn_ref, kr_ref, lvl)
            dps = [lax.dot_general(dov, v_ref[lo:hi, :], NT_DIMS, preferred_element_type=F32) for lo, hi, _ in ranges]
            di = None
            for p, dp in zip(ps, dps):
                part = jnp.sum(p * dp, axis=1, keepdims=True)
                di = part if di is None else di + part
            dqn = dqr = None
            for p, dp, (lo, hi, _) in zip(ps, dps, ranges):
                ds = (p * (dp - di) * ATT_SCALE).astype(BF16)
                part_n = jnp.dot(ds, kn_ref[lo:hi, :], preferred_element_type=F32)
                part_r = jnp.dot(ds, kr_ref[lo:hi, :], preferred_element_type=F32)
                dqn, dqr = (part_n, part_r) if dqn is None else (dqn + part_n, dqr + part_r)
                dknv_ref[0, lo:hi, :] += lax.dot_general(ds, qv[:, :QK_NOPE], TN_DIMS, preferred_element_type=F32)
                dknv_ref[1, lo:hi, :] += lax.dot_general(p.astype(BF16), dov, TN_DIMS, preferred_element_type=F32)
                dkr_ref[lo:hi, :] += lax.dot_general(ds, qv[:, QK_NOPE:], TN_DIMS, preferred_element_type=F32)
            dq_ref[:, :QK_NOPE] = dqn.astype(BF16)
            dq_ref[:, QK_NOPE:] = _rope_bwd_math(dqr, c_ref[...], s_ref[...]).astype(BF16)

        _per_query_block(qi, t // ATT_TQ, branch)

    qs, kns, krs, vs, os_ = _attn_specs(t)
    tab = pl.BlockSpec((ATT_TQ, LANES), lambda h, i: (i, 0))
    return _tc_call(
        body, name="attn_bwd", grid=(N_HEADS, t // ATT_TQ), in_specs=[qs, kns, krs, vs, os_, tab, tab],
        out_specs=[qs, pl.BlockSpec((2, t, QK_NOPE), lambda h, i: (0, 0, h)), krs],
        out_shape=[jax.ShapeDtypeStruct((t, N_HEADS * HEAD_PAD), BF16),
                   jax.ShapeDtypeStruct((2, t, N_HEADS * QK_NOPE), F32), jax.ShapeDtypeStruct((t, LANES), F32)],
        compiler_params=_cp("arbitrary", "arbitrary"),
    )(q, knv, kr, knv, do, cos, sin)


def _adam_math(w, g, m, v):
    nm = ADAM_B1 * m + (1.0 - ADAM_B1) * g
    nv = ADAM_B2 * v + (1.0 - ADAM_B2) * (g * g)
    m_hat = nm / (1.0 - ADAM_B1 ** ADAM_STEP)
    v_hat = nv / (1.0 - ADAM_B2 ** ADAM_STEP)
    return -ADAM_LR * (m_hat / (jnp.sqrt(v_hat) + ADAM_EPS) + ADAM_WD * w), nm, nv


def _adamw_small(w, g, m, v):
    def body(w_ref, g_ref, m_ref, v_ref, d_ref, nm_ref, nv_ref):
        d_ref[...], nm_ref[...], nv_ref[...] = _adam_math(w_ref[...], g_ref[...], m_ref[...], v_ref[...])

    shp = jax.ShapeDtypeStruct(w.shape, F32)
    return _tc_call(body, name="adamw_small", out_shape=[shp] * 3)(w, g, m, v)


ADAM_BLOCK_BYTES = 1 << 20


def _adamw_shard(ids, w, m, v, g_mine, g_sib, name, layer=None, prev=None):
    r, c = w.shape[-2:]
    half = r // 2
    tr = _tile(half, [d for d in range(half, 7, -8) if d * c * 4 <= ADAM_BLOCK_BYTES] or [8])
    nbh = half // tr

    def body(ids_ref, w_ref, m_ref, v_ref, gm_ref, gs_ref, *rest):
        g_ref, d_ref, nm_ref, nv_ref = rest[-4:]
        mine = (pl.program_id(0) // nbh) == ids_ref[0]

        @pl.when(mine)
        def _():
            g_ref[...] = gm_ref[...]

        @pl.when(jnp.logical_not(mine))
        def _():
            g_ref[...] = gs_ref[...]

        d_ref[...], nm_ref[...], nv_ref[...] = _adam_math(w_ref[...], g_ref[...], m_ref[...], v_ref[...])

    if layer is None:
        wspec = pl.BlockSpec((tr, c), lambda i, ids: (i, 0))
    else:
        wspec = pl.BlockSpec((None, tr, c), lambda i, ids: (layer, i, 0))
    gspec = pl.BlockSpec((tr, c), lambda i, ids: (i % nbh, 0))
    in_specs = [wspec] * 3 + [gspec] * 2
    args = [ids, w, m, v, g_mine, g_sib]
    aliases = {}
    if prev is not None:
        in_specs += [ANY] * 4
        args += list(prev)
        aliases = {6 + k: k for k in range(4)}
    return _tc_call(
        body, name=name, prefetch=1, grid=(r // tr,), in_specs=in_specs, out_specs=[wspec] * 4,
        out_shape=[jax.ShapeDtypeStruct(w.shape, F32)] * 4, input_output_aliases=aliases,
        compiler_params=_cp("parallel"),
    )(*args)


def _peer_chip(k_me, j):
    return k_me ^ jnp.where(j == 0, 2, jnp.where(j == 1, 1, 3))


def _pair_sum(ids, g, ra, name):
    _, r, c = g.shape
    half = r // 2

    def body(ids_ref, g_ref, ra_ref, o_ref):
        o_ref[...] = (g_ref[...].astype(F32) + ra_ref[...].astype(F32)).astype(BF16)

    return _tc_call(
        body, name=name, prefetch=1, grid=(3,),
        in_specs=[pl.BlockSpec((None, half, c), lambda j, ids: (_peer_chip(ids[1], j), ids[0], 0)),
                  pl.BlockSpec((None, half, c), lambda j, ids: (_peer_chip(ids[1], j), 0, 0))],
        out_specs=pl.BlockSpec((None, half, c), lambda j, ids: (j, 0, 0)),
        out_shape=jax.ShapeDtypeStruct((3, half, c), BF16), compiler_params=_cp("parallel"),
    )(ids, g, ra)


def _chip_sum(ids, g, ra, rb, name):
    _, r, c = g.shape
    half = r // 2

    def body(ids_ref, g_ref, ra_ref, rb_ref, o_ref):
        acc = g_ref[...].astype(F32) + ra_ref[...].astype(F32)
        for j in range(3):
            acc = acc + rb_ref[j].astype(F32)
        o_ref[...] = acc

    return _tc_call(
        body, name=name, prefetch=1, grid=(1,),
        in_specs=[pl.BlockSpec((None, half, c), lambda i, ids: (ids[1], ids[0], 0)),
                  pl.BlockSpec((None, half, c), lambda i, ids: (ids[1], 0, 0)),
                  pl.BlockSpec((3, half, c), lambda i, ids: (0, 0, 0))],
        out_specs=pl.BlockSpec((half, c), lambda i, ids: (0, 0)),
        out_shape=jax.ShapeDtypeStruct((half, c), F32), compiler_params=_cp("arbitrary"),
    )(ids, g, ra, rb)


def _position():
    x, y, c = lax.axis_index("x"), lax.axis_index("y"), lax.axis_index("c")
    chips = [(1 - x, y), (x, 1 - y), (1 - x, 1 - y)]
    return x, y, c, chips


def _shard_half(ref, wm, h):
    if wm.kind == "tiny":
        return ref
    if wm.nl == 2:
        return ref.at[h]
    return ref.at[pl.ds(pl.multiple_of(h * (wm.k // 2), 16), wm.k // 2), :]


def _region(full, wm, s, h):
    if wm.kind == "tiny":
        return full.at[s]
    cols = pl.ds(pl.multiple_of(s * wm.n, LANES), wm.n) if wm.kind == "col" else slice(None)
    if wm.nl == 2:
        rows = pl.ds(pl.multiple_of(s * wm.k, 16), wm.k) if wm.kind == "row" else slice(None)
        return full.at[slice(None) if h is None else h, rows, cols]
    if wm.kind == "col":
        rows = slice(None) if h is None else pl.ds(pl.multiple_of(h * (wm.k // 2), 16), wm.k // 2)
    elif h is None:
        rows = pl.ds(pl.multiple_of(s * wm.k, 16), wm.k)
    else:
        rows = pl.ds(pl.multiple_of(s * wm.k + h * (wm.k // 2), 16), wm.k // 2)
    return full.at[rows, cols]


def _full_shape(wm):
    if wm.kind == "tiny":
        return (N_CHIPS, wm.k, wm.n)
    shape = (wm.k, N_CHIPS * wm.n) if wm.kind == "col" else (N_CHIPS * wm.k, wm.n)
    return shape if wm.nl == 1 else (wm.nl,) + shape


def _handshake(peers):
    barrier = pltpu.get_barrier_semaphore()
    for peer in peers:
        pl.semaphore_signal(barrier, inc=1, device_id=peer, device_id_type=MESH)
    pl.semaphore_wait(barrier, len(peers))


def _all_gather_group(gi, shards):
    wms = AG_GROUPS[gi]
    nw = len(wms)

    def body(*refs):
        sh, full = refs[:nw], refs[nw:2 * nw]
        ici_s, ici_r, pass_s, pass_r, own_s, own_r = refs[2 * nw:]
        x, y, c, chips = _position()
        me, sibling = 2 * x + y, (x, y, 1 - c)
        _handshake([(*chip, c) for chip in chips] + [sibling])

        def rcopy(src, dst, s_sem, r_sem, to):
            return pltpu.make_async_remote_copy(src_ref=src, dst_ref=dst, send_sem=s_sem, recv_sem=r_sem,
                                                device_id=to, device_id_type=MESH)

        started = []
        for i, wm in enumerate(wms):
            for j, chip in enumerate(chips):
                started.append(rcopy(_shard_half(sh[i], wm, c), _region(full[i], wm, me, c),
                                     ici_s.at[i, j], ici_r.at[i, j], (*chip, c)))
                started[-1].start()
            started.append(rcopy(sh[i], _region(full[i], wm, me, None), own_s.at[i], own_r.at[i], sibling))
            started[-1].start()
        for i, wm in enumerate(wms):
            for j, chip in enumerate(chips):
                got = _region(full[i], wm, 2 * chip[0] + chip[1], c)
                rcopy(got, got, ici_s.at[i, j], ici_r.at[i, j], sibling).wait_recv()
                if wm.kind != "tiny":
                    started.append(rcopy(got, got, pass_s.at[i, j], pass_r.at[i, j], sibling))
                    started[-1].start()
        for i, wm in enumerate(wms):
            mine = _region(full[i], wm, me, None)
            rcopy(mine, mine, own_s.at[i], own_r.at[i], sibling).wait_recv()
            for j, chip in enumerate(chips):
                if wm.kind != "tiny":
                    got = _region(full[i], wm, 2 * chip[0] + chip[1], 1 - c)
                    rcopy(got, got, pass_s.at[i, j], pass_r.at[i, j], sibling).wait_recv()
        for cp in started:
            cp.wait_send()

    return pl.kernel(
        body, out_type=[jax.ShapeDtypeStruct(_full_shape(wm), s.dtype) for wm, s in zip(wms, shards)],
        mesh=plsc.ScalarSubcoreMesh(axis_name="sequencer", num_cores=1), name=f"ag_group{gi}",
        scratch_types=[pltpu.SemaphoreType.DMA((nw, 3))] * 4 + [pltpu.SemaphoreType.DMA((nw,))] * 2,
        compiler_params=pltpu.CompilerParams(collective_id=gi),
    )(*shards)


def _sequencer_call(body, name, cid, out_types, scratch, args):
    return pl.kernel(
        body, out_type=out_types, mesh=plsc.ScalarSubcoreMesh(axis_name="sequencer", num_cores=1), name=name,
        scratch_types=scratch, compiler_params=pltpu.CompilerParams(collective_id=cid),
    )(*args)


def _pair_exchange(gs, tag, cid):
    n = len(gs)

    def body(*refs):
        g, out, send_sems, recv_sems = refs[:n], refs[n:2 * n], refs[2 * n], refs[2 * n + 1]
        x, y, c, _ = _position()
        _handshake([(x, y, 1 - c)])
        cps = []
        for i in range(n):
            half = g[i].shape[1] // 2
            cps.append(pltpu.make_async_remote_copy(
                src_ref=g[i].at[:, pl.ds(pl.multiple_of((1 - c) * half, 16), half), :], dst_ref=out[i],
                send_sem=send_sems.at[i], recv_sem=recv_sems.at[i], device_id=(x, y, 1 - c), device_id_type=MESH))
            cps[-1].start()
        for cp in cps:
            cp.wait()

    return _sequencer_call(
        body, f"rs_pair_exchange{tag}", cid,
        [jax.ShapeDtypeStruct((a.shape[0], a.shape[1] // 2, a.shape[2]), a.dtype) for a in gs],
        [pltpu.SemaphoreType.DMA((n,)), pltpu.SemaphoreType.DMA((n,))], gs)


def _chip_exchange(ss, tag, cid):
    n = len(ss)

    def body(*refs):
        s, out, send_sems, recv_sems = refs[:n], refs[n:2 * n], refs[2 * n], refs[2 * n + 1]
        x, y, c, chips = _position()
        _handshake([(*chip, c) for chip in chips])
        cps = []
        for i in range(n):
            for j, chip in enumerate(chips):
                cps.append(pltpu.make_async_remote_copy(
                    src_ref=s[i].at[j], dst_ref=out[i].at[j], send_sem=send_sems.at[i, j], recv_sem=recv_sems.at[i, j],
                    device_id=(*chip, c), device_id_type=MESH))
                cps[-1].start()
        for cp in cps:
            cp.wait()

    return _sequencer_call(
        body, f"rs_chip_exchange{tag}", cid, [jax.ShapeDtypeStruct(a.shape, a.dtype) for a in ss],
        [pltpu.SemaphoreType.DMA((n, 3)), pltpu.SemaphoreType.DMA((n, 3))], ss)


def _pair_swap(g8s, tag, cid):
    n = len(g8s)

    def body(*refs):
        g, out, send_sems, recv_sems = refs[:n], refs[n:2 * n], refs[2 * n], refs[2 * n + 1]
        x, y, c, _ = _position()
        _handshake([(x, y, 1 - c)])
        cps = []
        for i in range(n):
            cps.append(pltpu.make_async_remote_copy(
                src_ref=g[i], dst_ref=out[i], send_sem=send_sems.at[i], recv_sem=recv_sems.at[i],
                device_id=(x, y, 1 - c), device_id_type=MESH))
            cps[-1].start()
        for cp in cps:
            cp.wait()

    return _sequencer_call(
        body, f"rs_pair_swap{tag}", cid, [jax.ShapeDtypeStruct(a.shape, a.dtype) for a in g8s],
        [pltpu.SemaphoreType.DMA((n,)), pltpu.SemaphoreType.DMA((n,))], g8s)


def _all_reduce_small(vec, name):
    r, cols = vec.shape

    def body(v_ref, o_ref, gath, send_sems, recv_sems):
        x, y, c, _ = _position()
        me = 4 * x + 2 * y + c
        gath[me] = v_ref[...]
        cps = []
        for rel in range(1, N_DEV):
            peer = (x ^ (rel >> 2), y ^ ((rel >> 1) & 1), c ^ (rel & 1))
            cps.append(pltpu.make_async_remote_copy(
                src_ref=v_ref, dst_ref=gath.at[me], send_sem=send_sems.at[rel - 1], recv_sem=recv_sems.at[rel - 1],
                device_id=peer, device_id_type=MESH))
        for cp in cps:
            cp.start()
        for rel in range(1, N_DEV):
            pltpu.make_async_remote_copy(
                src_ref=v_ref, dst_ref=gath.at[me ^ rel], send_sem=send_sems.at[rel - 1],
                recv_sem=recv_sems.at[rel - 1], device_id=(x, y, c), device_id_type=MESH).wait_recv()
        for cp in cps:
            cp.wait_send()
        acc = gath[0]
        for d in range(1, N_DEV):
            acc = acc + gath[d]
        o_ref[...] = acc

    vm = pl.BlockSpec(memory_space=pltpu.VMEM)
    return _tc_call(
        body, name=name, in_specs=[vm], out_specs=vm, out_shape=jax.ShapeDtypeStruct((r, cols), F32),
        scratch_shapes=[pltpu.VMEM((N_DEV, r, cols), F32), pltpu.SemaphoreType.DMA((N_DEV - 1,)),
                        pltpu.SemaphoreType.DMA((N_DEV - 1,))],
    )(vec)


def _rope_tables(positions):
    half = QK_ROPE // 2
    inv_freq = 1.0 / (ROPE_THETA ** (jnp.arange(half, dtype=F32) / half))
    ang = positions.astype(F32)[:, None] * inv_freq
    zeros = jnp.zeros((positions.shape[0], LANES - QK_ROPE), F32)
    cos, sin = jnp.cos(ang), jnp.sin(ang)
    return jnp.concatenate([cos, cos, zeros], axis=1), jnp.concatenate([sin, sin, zeros], axis=1)


def _local_step(x, positions, tgt, wf, small, rs):
    cos, sin = _rope_tables(positions)
    w_in, w_out = wf["sc_w_in"], wf["sc_w_out"]
    w_ups, w_downs = (wf["ffn_w_up0"], wf["ffn_w_up1"]), (wf["ffn_w_down0"], wf["ffn_w_down1"])
    w_kv, w_ukv, w_dq, w_uq, w_o = wf["w_kv"], wf["w_ukv"], wf["w_dq"], wf["w_uq"], wf["w_o"]
    attn_norm, ffn_norm = small["attn_norm"], small["ffn_norm"]
    conv_b = small["ffn_conv_b"]

    def ffn_fwd(h, l):
        hf = _rms_fwd(h, ffn_norm[l:l + 1], f"ffn{l}_norm")
        up = _nn_parts(f"ffn{l}_up", hf, w_ups[l], 2, BF16)
        a = _gate_fwd(up, small["ffn_conv_w"][l], conv_b[l:l + 1], f"ffn{l}_gate")
        return _nn(f"ffn{l}_down", a, w_downs[l], F32, add=h), (hf, up, a)

    def ffn_bwd(h, dh_out, l, saved, gi, hooks):
        run = lambda stage: hooks.get(stage, lambda: None)()
        hf, up, a = saved
        da = _nt(f"ffn{l}_down_dx", dh_out, w_downs[l], BF16)
        run("down_dx")
        d_down = _tn(f"ffn{l}_down_dw", a, dh_out, BF16)
        dup, d_cw, d_cb = _gate_bwd(up, small["ffn_conv_w"][l], conv_b[l:l + 1], da, f"ffn{l}_gate_bwd")
        run("gate_bwd")
        d_up = _dw_ffn_up(f"ffn{l}_up_dw", hf, dup)
        rs.start(gi, {f"ffn_w_down{l}": d_down.reshape(N_CHIPS, F_FF // N_CHIPS, D), f"ffn_w_up{l}": d_up})
        dhf = _nt_parts(f"ffn{l}_up_dx", dup, w_ups[l], BF16)
        run("up_dx")
        dh, d_norm = _rms_bwd(h, ffn_norm[l:l + 1], dhf, dh_out, f"ffn{l}_norm_bwd")
        return dh, d_cw, d_cb, d_norm

    hn0 = _rms_fwd(x, attn_norm[0:1], "attn0_norm")
    z = _nn_parts("sc_in", hn0, w_in, 3, BF16)
    mix = _scmix_fwd(z, small["sc_conv_w"])
    h1 = _nn("sc_out", mix, w_out, F32, add=x)
    h2, ffn0_saved = ffn_fwd(h1, 0)

    hk = _rms_fwd(h2, small["kv_in_norm"], "kv_in_norm")
    kvpre = _nn("kv_down", hk, w_kv, F32)
    ckv, kr = _kv_elem_fwd(kvpre, small["kv_latent_norm"], cos, sin)
    knv = _nn_parts("kv_up", ckv, w_ukv, 2, BF16, stacked=True)

    hn1 = _rms_fwd(h2, attn_norm[1:2], "attn1_norm")
    cq_pre = _nn("q_down", hn1, w_dq, F32)
    cq = _rms_fwd(cq_pre, small["q_latent_norm"], "q_latent_norm")
    q = _q_rope_fwd(_nn("q_up", cq, w_uq, F32), cos, sin)
    o = _attn_fwd(q, knv, kr)
    h3 = _nn("attn_out", o, w_o, F32, add=h2)
    h4, ffn1_saved = ffn_fwd(h3, 1)

    loss, dh4, d_final = _loss_head(h4, small["final_norm"], tgt)

    rows = D // N_CHIPS
    dh3, d_cw1, d_cb1, d_fn1 = ffn_bwd(h3, dh4, 1, ffn1_saved, 0, {})

    do = _nt("attn_out_dx", dh3, w_o, BF16)
    d_wo = _tn("attn_out_dw", o, dh3, BF16)
    rs.pair_sums(0)
    dq, dknv, dkr = _attn_bwd(q, knv, kr, do, cos, sin)
    rs.chip_sums(0)
    dcq = _nt("q_up_dx", dq, w_uq, F32)
    d_wuq = _tn("q_up_dw", cq, dq, BF16).reshape(Q_LORA, N_CHIPS, -1).transpose(1, 0, 2)
    dcq_pre, d_qln = _rms_bwd(cq_pre, small["q_latent_norm"], dcq, None, "q_latent_norm_bwd")
    rs.finish(0)
    dhn1 = _nt("q_down_dx", dcq_pre, w_dq, BF16)
    d_wdq = _tn("q_down_dw", hn1, dcq_pre, BF16)
    dh2, d_an1 = _rms_bwd(h2, attn_norm[1:2], dhn1, dh3, "attn1_norm_bwd")

    dckv = _nt_parts("kv_up_dx", dknv, w_ukv, F32, stacked=True)
    d_wukv = _dw_ukv(ckv, dknv)
    dkvpre, d_kvln = _kv_elem_bwd(kvpre, small["kv_latent_norm"], dckv, dkr, cos, sin)
    dhk = _nt("kv_down_dx", dkvpre, w_kv, BF16)
    d_wkv = _tn("kv_down_dw", hk, dkvpre, BF16)
    rs.start(1, {
        "w_o": d_wo.reshape(N_CHIPS, rows, D), "w_uq": d_wuq, "w_dq": d_wdq.reshape(N_CHIPS, rows, Q_LORA),
        "w_ukv": d_wukv.reshape(N_CHIPS, 2 * KV_LORA, -1), "w_kv": d_wkv.reshape(N_CHIPS, rows, KVP),
    })
    dh2, d_kvin = _rms_bwd(h2, small["kv_in_norm"], dhk, dh2, "kv_in_norm_bwd")

    dh1, d_cw0, d_cb0, d_fn0 = ffn_bwd(h1, dh2, 0, ffn0_saved, 2, {
        "down_dx": lambda: rs.pair_sums(1), "gate_bwd": lambda: rs.chip_sums(1), "up_dx": lambda: rs.finish(1)})
    rs.pair_sums(2)

    d_wout = _tn("sc_out_dw", mix, dh1, BF16)
    dmix = _nt("sc_out_dx", dh1, w_out, BF16)
    dz, d_scw = _scmix_bwd(z, small["sc_conv_w"], dmix)
    d_win = _dw_sc_in(hn0, dz)
    rs.start(3, {"sc_w_out": d_wout.reshape(N_CHIPS, rows, D), "sc_w_in": d_win})
    dhn0 = _nt_parts("sc_in_dx", dz, w_in, BF16)
    dx, d_an0 = _rms_bwd(x, attn_norm[0:1], dhn0, dh1, "attn0_norm_bwd")

    small_g = {
        "attn_norm": jnp.concatenate([d_an0, d_an1]), "ffn_norm": jnp.concatenate([d_fn0, d_fn1]),
        "final_norm": d_final, "kv_in_norm": d_kvin, "kv_latent_norm": d_kvln, "q_latent_norm": d_qln,
        "ffn_conv_b": jnp.concatenate([d_cb0, d_cb1]), "sc_conv_w": d_scw, "ffn_conv_w": jnp.stack([d_cw0, d_cw1]),
    }
    return loss, dx, small_g


RS_GROUPS = (("ffn_w_down1", "ffn_w_up1"), ("w_o", "w_uq", "w_dq", "w_ukv", "w_kv"),
             ("ffn_w_down0", "ffn_w_up0"), ("sc_w_out", "sc_w_in"))


class _ReduceScatter:
    def __init__(self, ids, finish):
        self.ids, self.grads, self.step, self.mine, self.sib, self.finish = ids, {}, {}, {}, {}, finish

    def _cid(self, gi):
        return len(AG_GROUPS) + 3 * gi

    def start(self, gi, grads):
        self.grads.update(grads)
        own = [grads[n] for n in RS_GROUPS[gi]]
        self.step[gi] = (own, _pair_exchange(own, gi, self._cid(gi)))

    def pair_sums(self, gi):
        own, ra = self.step[gi]
        sums = [_pair_sum(self.ids, g, a, f"rs_pair_sum_{n}") for n, g, a in zip(RS_GROUPS[gi], own, ra)]
        self.step[gi] = (own, ra, _chip_exchange(sums, gi, self._cid(gi) + 1))

    def chip_sums(self, gi):
        own, ra, rb = self.step[gi]
        mine = [_chip_sum(self.ids, g, a, b, f"rs_chip_sum_{n}") for n, g, a, b in zip(RS_GROUPS[gi], own, ra, rb)]
        self.mine.update(zip(RS_GROUPS[gi], mine))
        self.sib.update(zip(RS_GROUPS[gi], _pair_swap(mine, gi, self._cid(gi) + 2)))

SMALL_REPL = ("attn_norm", "ffn_norm", "final_norm", "kv_in_norm", "kv_latent_norm", "q_latent_norm", "ffn_conv_b")
SMALL_SHARDED = ("sc_conv_w", "ffn_conv_w")
SMALL_ROWS = 256


def _pad_heads(w_uq):
    per_head = w_uq.reshape(Q_LORA, -1, QK_NOPE + QK_ROPE)
    return jnp.pad(per_head, ((0, 0), (0, 0), (0, HEAD_PAD - QK_NOPE - QK_ROPE))).reshape(Q_LORA, -1)


def _pack_kv(w_dkv, w_kr):
    return jnp.concatenate([w_dkv, w_kr, jnp.zeros((w_kr.shape[0], LANES - QK_ROPE), w_kr.dtype)], axis=1)


def kernel(x, positions, attn_norm, ffn_norm, final_norm, sc_w_in, sc_conv_w, sc_w_out, kv_in_norm, w_dkv, kv_latent_norm, w_kr, w_uk, w_uv, w_dq, q_latent_norm, w_uq, w_o, ffn_w_up, ffn_conv_w, ffn_conv_b, ffn_w_down, loss_target, m_attn_norm, m_ffn_norm, m_final_norm, m_sc_w_in, m_sc_conv_w, m_sc_w_out, m_kv_in_norm, m_w_dkv, m_kv_latent_norm, m_w_kr, m_w_uk, m_w_uv, m_w_dq, m_q_latent_norm, m_w_uq, m_w_o, m_ffn_w_up, m_ffn_conv_w, m_ffn_conv_b, m_ffn_w_down, v_attn_norm, v_ffn_norm, v_final_norm, v_sc_w_in, v_sc_conv_w, v_sc_w_out, v_kv_in_norm, v_w_dkv, v_kv_latent_norm, v_w_kr, v_w_uk, v_w_uv, v_w_dq, v_q_latent_norm, v_w_uq, v_w_o, v_ffn_w_up, v_ffn_conv_w, v_ffn_conv_b, v_ffn_w_down):
    names = ("attn_norm", "ffn_norm", "final_norm", "sc_w_in", "sc_conv_w", "sc_w_out", "kv_in_norm", "w_dkv",
             "kv_latent_norm", "w_kr", "w_uk", "w_uv", "w_dq", "q_latent_norm", "w_uq", "w_o", "ffn_w_up",
             "ffn_conv_w", "ffn_conv_b", "ffn_w_down")
    w = dict(zip(names, (attn_norm, ffn_norm, final_norm, sc_w_in, sc_conv_w, sc_w_out, kv_in_norm, w_dkv,
                         kv_latent_norm, w_kr, w_uk, w_uv, w_dq, q_latent_norm, w_uq, w_o, ffn_w_up,
                         ffn_conv_w, ffn_conv_b, ffn_w_down)))
    m = dict(zip(names, (m_attn_norm, m_ffn_norm, m_final_norm, m_sc_w_in, m_sc_conv_w, m_sc_w_out, m_kv_in_norm,
                         m_w_dkv, m_kv_latent_norm, m_w_kr, m_w_uk, m_w_uv, m_w_dq, m_q_latent_norm, m_w_uq, m_w_o,
                         m_ffn_w_up, m_ffn_conv_w, m_ffn_conv_b, m_ffn_w_down)))
    v = dict(zip(names, (v_attn_norm, v_ffn_norm, v_final_norm, v_sc_w_in, v_sc_conv_w, v_sc_w_out, v_kv_in_norm,
                         v_w_dkv, v_kv_latent_norm, v_w_kr, v_w_uk, v_w_uv, v_w_dq, v_q_latent_norm, v_w_uq, v_w_o,
                         v_ffn_w_up, v_ffn_conv_w, v_ffn_conv_b, v_ffn_w_down)))

    _ORDER[0] = None
    ix, iy, ic = lax.axis_index("x"), lax.axis_index("y"), lax.axis_index("c")
    chip = 2 * ix + iy
    ids = jnp.stack([ic, chip]).astype(jnp.int32)

    def shards_of(t):
        return {
            "sc_w_in": t["sc_w_in"][0], "sc_w_out": t["sc_w_out"][0], "ffn_w_up": t["ffn_w_up"],
            "ffn_w_down": t["ffn_w_down"], "w_kv": _pack_kv(t["w_dkv"], t["w_kr"]),
            "w_ukv": jnp.stack([t["w_uk"], t["w_uv"]]), "w_dq": t["w_dq"][0], "w_uq": _pad_heads(t["w_uq"][0]),
            "w_o": t["w_o"][0],
        }

    ws, ms, vs = shards_of(w), shards_of(m), shards_of(v)

    def ag_shard(name):
        if name == "sc_conv_w":
            return sc_conv_w[0]
        if name == "ffn_conv_w":
            return ffn_conv_w.reshape(6, -1)
        if name[:-1] in ("ffn_w_up", "ffn_w_down"):
            return ws[name[:-1]][int(name[-1])].astype(BF16)
        return ws[name].astype(BF16)

    wf = {}
    for gi, wms in enumerate(AG_GROUPS):
        fulls = _all_gather_group(gi, [ag_shard(wm.name) for wm in wms])
        wf.update({wm.name: f for wm, f in zip(wms, fulls)})
    small = {
        "attn_norm": attn_norm, "ffn_norm": ffn_norm, "final_norm": final_norm[None], "kv_in_norm": kv_in_norm[None],
        "kv_latent_norm": kv_latent_norm[None], "q_latent_norm": q_latent_norm, "ffn_conv_b": ffn_conv_b,
        "sc_conv_w": wf["sc_conv_w"].transpose(1, 0, 2).reshape(3, D),
        "ffn_conv_w": wf["ffn_conv_w"].reshape(N_CHIPS, 2, 3, -1).transpose(1, 2, 0, 3).reshape(2, 3, F_FF),
    }

    res = {}

    def adamw_plain(n):
        res[n] = _adamw_shard(ids, ws[n], ms[n], vs[n], rs.mine[n], rs.sib[n], f"adamw_{n}")

    def adamw_layer(n, layer):
        key = f"{n}{layer}"
        res[n] = _adamw_shard(ids, ws[n], ms[n], vs[n], rs.mine[key], rs.sib[key], f"adamw_{key}", layer=layer,
                              prev=res.get(n))

    def adamw_group(gi):
        if gi in (0, 2):
            for n in ("ffn_w_up", "ffn_w_down"):
                adamw_layer(n, 1 if gi == 0 else 0)
        elif gi == 1:
            for n in ("w_kv", "w_dq", "w_uq", "w_o"):
                adamw_plain(n)
            merged = lambda a: a.reshape(2 * KV_LORA, -1)
            res["w_ukv"] = _adamw_shard(ids, merged(ws["w_ukv"]), merged(ms["w_ukv"]), merged(vs["w_ukv"]),
                                        rs.mine["w_ukv"], rs.sib["w_ukv"], "adamw_w_ukv")
        else:
            for n in ("sc_w_out", "sc_w_in"):
                adamw_plain(n)

    rs = _ReduceScatter(ids, adamw_group)
    loss, dx, small_g = _local_step(x[0], positions[0], loss_target[0], wf, small, rs)

    s_order = SMALL_REPL + SMALL_SHARDED
    flat = jnp.concatenate([small_g[n].reshape(-1) for n in s_order] + [loss.reshape(-1)])
    flat = jnp.pad(flat, (0, SMALL_ROWS * LANES - flat.shape[0])).reshape(SMALL_ROWS, LANES)
    red = _all_reduce_small(flat, "ar_small").reshape(-1)
    sg, off = {}, 0
    for n in s_order:
        sz = small_g[n].size
        sg[n] = red[off:off + sz].reshape(small_g[n].shape)
        off += sz
    loss_out = red[off]
    grads = {n: sg[n].reshape(w[n].shape) for n in SMALL_REPL}
    grads["sc_conv_w"] = lax.dynamic_slice_in_dim(sg["sc_conv_w"], chip * (D // N_CHIPS), D // N_CHIPS, axis=1)[None]
    grads["ffn_conv_w"] = lax.dynamic_slice_in_dim(sg["ffn_conv_w"], chip * (F_FF // N_CHIPS), F_FF // N_CHIPS, axis=2)

    rs.chip_sums(2)
    rs.pair_sums(3)
    rs.finish(2)
    rs.chip_sums(3)
    rs.finish(3)
    outs = [grads, {}, {}, {}]
    for k, dst in enumerate(outs):
        for n in ("sc_w_in", "sc_w_out", "w_dq", "w_o"):
            dst[n] = res[n][k][None]
        unpadded = res["w_uq"][k].reshape(Q_LORA, -1, HEAD_PAD)[:, :, :QK_NOPE + QK_ROPE]
        dst["w_uq"] = unpadded.reshape(w_uq.shape)
        dst["ffn_w_up"], dst["ffn_w_down"] = res["ffn_w_up"][k], res["ffn_w_down"][k]
        dst["w_dkv"], dst["w_kr"] = res["w_kv"][k][:, :KV_LORA], res["w_kv"][k][:, KV_LORA:KV_LORA + QK_ROPE]
        dst["w_uk"], dst["w_uv"] = res["w_ukv"][k][:KV_LORA], res["w_ukv"][k][KV_LORA:]
    grads, delta, new_m, new_v = outs

    small_names = SMALL_REPL + SMALL_SHARDED

    def pack_small(tree):
        return jnp.concatenate([tree[n].reshape(-1) for n in small_names]).reshape(-1, LANES)

    small_res = _adamw_small(pack_small(w), pack_small(grads), pack_small(m), pack_small(v))
    for slab, dst in zip(small_res, (delta, new_m, new_v)):
        f, off = slab.reshape(-1), 0
        for n in small_names:
            dst[n] = f[off:off + w[n].size].reshape(w[n].shape)
            off += w[n].size

    _ORDER[0] = None
    return (loss_out, dx[None], *[grads[n] for n in names], *[delta[n] for n in names],
            *[new_m[n] for n in names], *[new_v[n] for n in names])
```

```python
from typing import NamedTuple

import jax
import jax.numpy as jnp
from jax import lax
from jax.experimental import pallas as pl
from jax.experimental.pallas import tpu as pltpu
from jax.experimental.pallas import tpu_sc as plsc

F32 = jnp.float32
BF16 = jnp.bfloat16

T = 2048
D = 1024
F_FF = 2816
N_HEADS = 8
QK_NOPE = 128
QK_ROPE = 64
V_HEAD = 128
Q_LORA = 384
KV_LORA = 256
CHUNK_SHIFT = 6
ROPE_THETA = 10000.0
EPS = 1e-6
NEG_INF = -1e30
HEAD_PAD = 256
KVP = KV_LORA + 128

ADAM_LR = 0.001
ADAM_B1 = 0.9
ADAM_B2 = 0.999
ADAM_EPS = 1e-08
ADAM_WD = 0.01
ADAM_STEP = 10

N_CHIPS = 4
N_DEV = 8
LANES = 128
TC = 256
V7X_VMEM_LIMIT = 56 * 1024 * 1024

MESH = pl.DeviceIdType.MESH
ANY = pl.BlockSpec(memory_space=pl.ANY)


class _W(NamedTuple):
    name: str
    kind: str
    nl: int
    k: int
    n: int


AG_GROUPS = (
    (_W("sc_w_in", "col", 1, D, 3 * D // N_CHIPS), _W("sc_conv_w", "tiny", 1, 3, D // N_CHIPS),
     _W("ffn_conv_w", "tiny", 1, 6, F_FF // N_CHIPS)),
    (_W("sc_w_out", "row", 1, D // N_CHIPS, D),),
    (_W("ffn_w_up0", "col", 1, D, 2 * F_FF // N_CHIPS),),
    (_W("ffn_w_down0", "row", 1, F_FF // N_CHIPS, D),),
    (_W("w_kv", "row", 1, D // N_CHIPS, KVP), _W("w_ukv", "col", 2, KV_LORA, N_HEADS * QK_NOPE // N_CHIPS),
     _W("w_dq", "row", 1, D // N_CHIPS, Q_LORA),
     _W("w_uq", "col", 1, Q_LORA, N_HEADS * HEAD_PAD // N_CHIPS),
     _W("w_o", "row", 1, N_HEADS * V_HEAD // N_CHIPS, D)),
    (_W("ffn_w_up1", "col", 1, D, 2 * F_FF // N_CHIPS), _W("ffn_w_down1", "row", 1, F_FF // N_CHIPS, D)),
)


def _cp(*sem):
    return pltpu.CompilerParams(dimension_semantics=sem, vmem_limit_bytes=V7X_VMEM_LIMIT)


_ORDER = [None]


def _tc_call(body, *, name, out_shape, in_specs=None, out_specs=None, grid=(), scratch_shapes=(), prefetch=0,
             input_output_aliases=None, compiler_params=None):
    def run(*args):
        specs = [pl.BlockSpec(memory_space=pltpu.VMEM)] * (len(args) - prefetch) if in_specs is None else list(in_specs)
        inner, dep = body, _ORDER[0]
        if dep is not None:
            unread = prefetch + len(specs)
            specs, args = specs + [ANY], (*args, dep)

            def inner(*refs):
                return body(*refs[:unread], *refs[unread + 1:])

        kwargs = dict(name=name, out_shape=out_shape, input_output_aliases=input_output_aliases or {},
                      compiler_params=compiler_params)
        if prefetch:
            kwargs["grid_spec"] = pltpu.PrefetchScalarGridSpec(
                num_scalar_prefetch=prefetch, grid=grid, in_specs=specs, out_specs=out_specs,
                scratch_shapes=scratch_shapes)
        else:
            kwargs.update(grid=grid, in_specs=specs, scratch_shapes=scratch_shapes)
            if out_specs is not None:
                kwargs["out_specs"] = out_specs
        out = pl.pallas_call(inner, **kwargs)(*args)
        _ORDER[0] = out[0] if isinstance(out, (list, tuple)) else out
        return out

    return run


def _tile(n, cands):
    for c in cands:
        if n % c == 0:
            return c
    raise ValueError(f"no tile for {n}")


NN_DIMS = (((1,), (0,)), ((), ()))
NT_DIMS = (((1,), (1,)), ((), ()))
TN_DIMS = (((0,), (0,)), ((), ()))
M_TILES = (1024, 512, 384, 256, 128)
N_TILES = (1408, 1024, 768, 512, 384, 256, 128)
MM_BLOCK_BYTES = 36 * 1024 * 1024


def _fit(m, n, block_bytes, m_tiles=M_TILES, n_tiles=N_TILES):
    for tm in [c for c in m_tiles if m % c == 0]:
        for tn in [c for c in n_tiles if n % c == 0]:
            if 2 * block_bytes(tm, tn) + 4 * tm * tn <= MM_BLOCK_BYTES:
                return tm, tn
    raise ValueError(f"no tiles for {m} x {n}")


def _size(x):
    return x.dtype.itemsize


def _mm(name, a, b, dims, grid, a_spec, b_spec, o_spec, o_sds, add=None, red=None, acc_shape=None):
    n_red = None if red is None else grid[red]

    def body(*refs):
        a_ref, b_ref = refs[0], refs[1]
        add_ref = refs[2] if add is not None else None
        o_ref = refs[3] if add is not None else refs[2]
        part = lax.dot_general(a_ref[...].astype(BF16), b_ref[...].astype(BF16), dims, preferred_element_type=F32)
        if red is None:
            if add is not None:
                part = part + add_ref[...]
            o_ref[...] = part.astype(o_ref.dtype)
            return
        acc_ref = refs[-1]
        r = pl.program_id(red)

        @pl.when(r == 0)
        def _():
            acc_ref[...] = part

        @pl.when(r > 0)
        def _():
            acc_ref[...] += part

        @pl.when(r == n_red - 1)
        def _():
            o_ref[...] = acc_ref[...].astype(o_ref.dtype)

    sem = tuple("arbitrary" if ax == red else "parallel" for ax in range(len(grid)))
    in_specs = [a_spec, b_spec] + ([o_spec] if add is not None else [])
    args = (a, b) + ((add,) if add is not None else ())
    return _tc_call(
        body, name=name, grid=grid, in_specs=in_specs, out_specs=o_spec, out_shape=o_sds,
        scratch_shapes=[] if red is None else [pltpu.VMEM(acc_shape, F32)], compiler_params=_cp(*sem),
    )(*args)


def _nn(name, a, b, out_dtype, add=None, lead=None):
    (m, k), n = a.shape, b.shape[-1]
    osz = jnp.dtype(out_dtype).itemsize + (4 if add is not None else 0)
    tm, tn = _fit(m, n, lambda tm, tn: tm * k * _size(a) + k * tn * _size(b) + tm * tn * osz)
    if lead is None:
        b_spec = pl.BlockSpec((k, tn), lambda i, j: (0, j))
    else:
        b_spec = pl.BlockSpec((None, k, tn), lambda i, j: (lead, 0, j))
    return _mm(name, a, b, NN_DIMS, (m // tm, n // tn), pl.BlockSpec((tm, k), lambda i, j: (i, 0)), b_spec,
               pl.BlockSpec((tm, tn), lambda i, j: (i, j)), jax.ShapeDtypeStruct((m, n), out_dtype), add=add)


def _nn_parts(name, a, b, parts, out_dtype, lead=None, stacked=False):
    m, k = a.shape
    c = b.shape[-1] if stacked else b.shape[-1] // parts
    osz = jnp.dtype(out_dtype).itemsize
    tm, tn = _fit(m, c, lambda tm, tn: tm * k * _size(a) + k * tn * _size(b) + tm * tn * osz)
    nb = c // tn
    if stacked:
        b_spec = pl.BlockSpec((None, k, tn), lambda i, p, j: (p, 0, j))
    elif lead is None:
        b_spec = pl.BlockSpec((k, tn), lambda i, p, j: (0, p * nb + j))
    else:
        b_spec = pl.BlockSpec((None, k, tn), lambda i, p, j: (lead, 0, p * nb + j))
    return _mm(name, a, b, NN_DIMS, (m // tm, parts, nb), pl.BlockSpec((tm, k), lambda i, p, j: (i, 0)), b_spec,
               pl.BlockSpec((None, tm, tn), lambda i, p, j: (p, i, j)), jax.ShapeDtypeStruct((parts, m, c), out_dtype))


def _nt(name, a, b, out_dtype, lead=None):
    (m, k), n = a.shape, b.shape[-2]
    osz = jnp.dtype(out_dtype).itemsize
    tm, tn = _fit(m, n, lambda tm, tn: tm * k * _size(a) + tn * k * _size(b) + tm * tn * osz)
    if lead is None:
        b_spec = pl.BlockSpec((tn, k), lambda i, j: (j, 0))
    else:
        b_spec = pl.BlockSpec((None, tn, k), lambda i, j: (lead, j, 0))
    return _mm(name, a, b, NT_DIMS, (m // tm, n // tn), pl.BlockSpec((tm, k), lambda i, j: (i, 0)), b_spec,
               pl.BlockSpec((tm, tn), lambda i, j: (i, j)), jax.ShapeDtypeStruct((m, n), out_dtype))


def _nt_parts(name, a, b, out_dtype, lead=None, stacked=False):
    parts, m, c = a.shape
    n = b.shape[-2]
    osz = jnp.dtype(out_dtype).itemsize + 2
    tm, tn = _fit(m, n, lambda tm, tn: tm * c * _size(a) + tn * c * _size(b) + tm * tn * osz)
    if stacked:
        b_spec = pl.BlockSpec((None, tn, c), lambda i, j, p: (p, j, 0))
    elif lead is None:
        b_spec = pl.BlockSpec((tn, c), lambda i, j, p: (j, p))
    else:
        b_spec = pl.BlockSpec((None, tn, c), lambda i, j, p: (lead, j, p))
    return _mm(name, a, b, NT_DIMS, (m // tm, n // tn, parts), pl.BlockSpec((None, tm, c), lambda i, j, p: (p, i, 0)),
               b_spec, pl.BlockSpec((tm, tn), lambda i, j, p: (i, j)), jax.ShapeDtypeStruct((m, n), out_dtype),
               red=2, acc_shape=(tm, tn))


def _tn(name, a, b, out_dtype):
    (k, m), n = a.shape, b.shape[1]
    osz = jnp.dtype(out_dtype).itemsize
    tm, tn = _fit(m, n, lambda tm, tn: k * tm * _size(a) + k * tn * _size(b) + tm * tn * osz,
                  m_tiles=(512, 384, 256, 128), n_tiles=(n,) + N_TILES)
    return _mm(name, a, b, TN_DIMS, (m // tm, n // tn), pl.BlockSpec((k, tm), lambda i, j: (0, i)),
               pl.BlockSpec((k, tn), lambda i, j: (0, j)), pl.BlockSpec((tm, tn), lambda i, j: (i, j)),
               jax.ShapeDtypeStruct((m, n), out_dtype))


def _dw_sc_in(hn, dz):
    t, tn, tm = hn.shape[0], TC, 512
    per_part, per_chip = D // tn, 3 * D // N_CHIPS // tn
    return _mm("sc_in_dw", hn, dz, TN_DIMS, (D // tm, 3 * D // tn), pl.BlockSpec((t, tm), lambda i, j: (0, i)),
               pl.BlockSpec((None, t, tn), lambda i, j: (j // per_part, 0, j % per_part)),
               pl.BlockSpec((None, tm, tn), lambda i, j: (j // per_chip, i, j % per_chip)),
               jax.ShapeDtypeStruct((N_CHIPS, D, 3 * D // N_CHIPS), BF16))


def _dw_ffn_up(name, hf, dup):
    t, tm, ns = hf.shape[0], 512, 2 * F_FF // N_CHIPS
    return _mm(name, hf, dup, TN_DIMS, (N_CHIPS, D // tm), pl.BlockSpec((t, tm), lambda s, i: (0, i)),
               pl.BlockSpec((None, t, ns), lambda s, i: (s // 2, 0, s % 2)),
               pl.BlockSpec((None, tm, ns), lambda s, i: (s, i, 0)), jax.ShapeDtypeStruct((N_CHIPS, D, ns), BF16))


def _dw_ukv(ckv, dknv):
    t, ns = ckv.shape[0], N_HEADS * QK_NOPE // N_CHIPS
    return _mm("kv_up_dw", ckv, dknv, TN_DIMS, (2, N_CHIPS), pl.BlockSpec((t, KV_LORA), lambda p, s: (0, 0)),
               pl.BlockSpec((None, t, ns), lambda p, s: (p, 0, s)),
               pl.BlockSpec((None, None, KV_LORA, ns), lambda p, s: (s, p, 0, 0)),
               jax.ShapeDtypeStruct((N_CHIPS, 2, KV_LORA, ns), BF16))


def _rms_fwd(x, g, name):
    t, d = x.shape
    tr = 512

    def body(x_ref, g_ref, o_ref):
        xv = x_ref[...]
        r = lax.rsqrt(jnp.mean(xv * xv, axis=1, keepdims=True) + EPS)
        o_ref[...] = (xv * r * g_ref[...]).astype(o_ref.dtype)

    row = pl.BlockSpec((tr, d), lambda i: (i, 0))
    return _tc_call(
        body, name=name, grid=(t // tr,), in_specs=[row, pl.BlockSpec((1, d), lambda i: (0, 0))],
        out_specs=row, out_shape=jax.ShapeDtypeStruct((t, d), BF16), compiler_params=_cp("parallel"),
    )(x, g)


def _rms_bwd_math(xv, g, dy):
    r = lax.rsqrt(jnp.mean(xv * xv, axis=1, keepdims=True) + EPS)
    xh = xv * r
    gy = dy * g
    dx = r * (gy - xh * jnp.mean(gy * xh, axis=1, keepdims=True))
    dg = jnp.sum(dy * xh, axis=0, keepdims=True)
    return dx, dg


def _rms_bwd(x, g, dy, add, name, matmul_copy=False):
    t, d = x.shape
    tr = 512
    n_in = 3 + (add is not None)

    def body(*refs):
        x_ref, g_ref, dy_ref = refs[:3]
        dx_ref, dg_ref = refs[n_in], refs[-1]
        dx, dg = _rms_bwd_math(x_ref[...], g_ref[...], dy_ref[...].astype(F32))
        if add is not None:
            dx = dx + refs[3][...]
        dx_ref[...] = dx
        if matmul_copy:
            refs[n_in + 1][...] = dx.astype(BF16)

        @pl.when(pl.program_id(0) == 0)
        def _():
            dg_ref[...] = jnp.zeros_like(dg_ref)

        dg_ref[...] += dg

    row = pl.BlockSpec((tr, d), lambda i: (i, 0))
    vec = pl.BlockSpec((1, d), lambda i: (0, 0))
    in_specs = [row, vec, row] + ([row] if add is not None else [])
    args = (x, g, dy) + ((add,) if add is not None else ())
    copies = [jax.ShapeDtypeStruct((t, d), BF16)] if matmul_copy else []
    return _tc_call(
        body, name=name, grid=(t // tr,), in_specs=in_specs, out_specs=[row] * (1 + len(copies)) + [vec],
        out_shape=[jax.ShapeDtypeStruct((t, d), F32)] + copies + [jax.ShapeDtypeStruct((1, d), F32)],
        compiler_params=_cp("arbitrary"),
    )(*args)


def _loss_head(h, g, tgt):
    t, d = h.shape
    tr = 512

    def body(h_ref, g_ref, t_ref, loss_ref, dh_ref, dhb_ref, dg_ref):
        xv = h_ref[...]
        gv = g_ref[...]
        r = lax.rsqrt(jnp.mean(xv * xv, axis=1, keepdims=True) + EPS)
        err = xv * r * gv - t_ref[...]
        part = 0.5 * jnp.sum(jnp.mean(err * err, axis=1, keepdims=True), axis=0, keepdims=True)
        dx, dg = _rms_bwd_math(xv, gv, err * (1.0 / d))
        dh_ref[...] = dx
        dhb_ref[...] = dx.astype(BF16)

        @pl.when(pl.program_id(0) == 0)
        def _():
            dg_ref[...] = jnp.zeros_like(dg_ref)
            loss_ref[...] = jnp.zeros_like(loss_ref)

        dg_ref[...] += dg
        loss_ref[...] += jnp.broadcast_to(part, loss_ref.shape)

    row = pl.BlockSpec((tr, d), lambda i: (i, 0))
    vec = pl.BlockSpec((1, d), lambda i: (0, 0))
    lspec = pl.BlockSpec((1, LANES), lambda i: (0, 0))
    return _tc_call(
        body, name="loss_head", grid=(t // tr,), in_specs=[row, vec, row], out_specs=[lspec, row, row, vec],
        out_shape=[jax.ShapeDtypeStruct((1, LANES), F32), jax.ShapeDtypeStruct((t, d), F32),
                   jax.ShapeDtypeStruct((t, d), BF16), jax.ShapeDtypeStruct((1, d), F32)],
        compiler_params=_cp("arbitrary"),
    )(h, g, tgt)


def _rot_half(x):
    lane = lax.broadcasted_iota(jnp.int32, x.shape, 1)
    return jnp.where((lane % QK_ROPE) < QK_ROPE // 2, -pltpu.roll(x, LANES - 32, axis=1),
                     pltpu.roll(x, 32, axis=1))


def _rope_fwd_math(x, cos, sin):
    return x * cos + _rot_half(x) * sin


def _rope_bwd_math(dy, cos, sin):
    return dy * cos - _rot_half(dy * sin)


def _q_rope_fwd(qpre, cos, sin):
    t, w = qpre.shape
    tr = 256

    def body(q_ref, c_ref, s_ref, o_ref):
        cv, sv = c_ref[...], s_ref[...]
        for h in range(N_HEADS):
            lo = h * HEAD_PAD
            o_ref[:, lo:lo + QK_NOPE] = q_ref[:, lo:lo + QK_NOPE].astype(BF16)
            o_ref[:, lo + QK_NOPE:lo + HEAD_PAD] = _rope_fwd_math(
                q_ref[:, lo + QK_NOPE:lo + HEAD_PAD], cv, sv).astype(BF16)

    row = pl.BlockSpec((tr, w), lambda i: (i, 0))
    tab = pl.BlockSpec((tr, LANES), lambda i: (i, 0))
    return _tc_call(
        body, name="q_rope_fwd", grid=(t // tr,), in_specs=[row, tab, tab], out_specs=row,
        out_shape=jax.ShapeDtypeStruct((t, w), BF16), compiler_params=_cp("parallel"),
    )(qpre, cos, sin)


def _kv_elem_fwd(kvpre, g, cos, sin):
    t = kvpre.shape[0]
    tr = 512

    def body(p_ref, g_ref, c_ref, s_ref, ckv_ref, kr_ref):
        lat = p_ref[:, :KV_LORA]
        r = lax.rsqrt(jnp.mean(lat * lat, axis=1, keepdims=True) + EPS)
        ckv_ref[...] = (lat * r * g_ref[...]).astype(BF16)
        kr_ref[...] = _rope_fwd_math(p_ref[:, KV_LORA:], c_ref[...], s_ref[...]).astype(BF16)

    tab = pl.BlockSpec((tr, LANES), lambda i: (i, 0))
    return _tc_call(
        body, name="kv_elem_fwd", grid=(t // tr,),
        in_specs=[pl.BlockSpec((tr, KVP), lambda i: (i, 0)), pl.BlockSpec((1, KV_LORA), lambda i: (0, 0)), tab, tab],
        out_specs=[pl.BlockSpec((tr, KV_LORA), lambda i: (i, 0)), tab],
        out_shape=[jax.ShapeDtypeStruct((t, KV_LORA), BF16), jax.ShapeDtypeStruct((t, LANES), BF16)],
        compiler_params=_cp("parallel"),
    )(kvpre, g, cos, sin)


def _kv_elem_bwd(kvpre, g, dckv, dkr, cos, sin):
    t = kvpre.shape[0]
    tr = 512

    def body(p_ref, g_ref, dc_ref, dk_ref, c_ref, s_ref, dp_ref, dg_ref):
        dlat, dg = _rms_bwd_math(p_ref[:, :KV_LORA], g_ref[...], dc_ref[...])
        dp_ref[:, :KV_LORA] = dlat.astype(BF16)
        dp_ref[:, KV_LORA:] = _rope_bwd_math(dk_ref[...], c_ref[...], s_ref[...]).astype(BF16)

        @pl.when(pl.program_id(0) == 0)
        def _():
            dg_ref[...] = jnp.zeros_like(dg_ref)

        dg_ref[...] += dg

    tab = pl.BlockSpec((tr, LANES), lambda i: (i, 0))
    pre = pl.BlockSpec((tr, KVP), lambda i: (i, 0))
    vec = pl.BlockSpec((1, KV_LORA), lambda i: (0, 0))
    return _tc_call(
        body, name="kv_elem_bwd", grid=(t // tr,),
        in_specs=[pre, vec, pl.BlockSpec((tr, KV_LORA), lambda i: (i, 0)), tab, tab, tab],
        out_specs=[pre, vec],
        out_shape=[jax.ShapeDtypeStruct((t, KVP), BF16), jax.ShapeDtypeStruct((1, KV_LORA), F32)],
        compiler_params=_cp("arbitrary"),
    )(kvpre, g, dckv, dkr, cos, sin)


def _shift_down(x, k):
    row = lax.broadcasted_iota(jnp.int32, x.shape, 0)
    return jnp.where(row >= k, pltpu.roll(x, k, axis=0), 0.0)


def _shift_up(x, k):
    n = x.shape[0]
    row = lax.broadcasted_iota(jnp.int32, x.shape, 0)
    return jnp.where(row < n - k, pltpu.roll(x, n - k, axis=0), 0.0)


def _conv3(x, w_ref):
    return _shift_down(x, 2) * w_ref[0:1, :] + _shift_down(x, 1) * w_ref[1:2, :] + x * w_ref[2:3, :]


def _conv3_t(dy, w_ref):
    return dy * w_ref[2:3, :] + _shift_up(dy, 1) * w_ref[1:2, :] + _shift_up(dy, 2) * w_ref[0:1, :]


def _conv3_dw(dy, x, dw_ref):
    dw_ref[0:1, :] = jnp.sum(dy * _shift_down(x, 2), axis=0, keepdims=True)
    dw_ref[1:2, :] = jnp.sum(dy * _shift_down(x, 1), axis=0, keepdims=True)
    dw_ref[2:3, :] = jnp.sum(dy * x, axis=0, keepdims=True)


def _col(parts, t):
    if parts is None:
        return pl.BlockSpec((t, TC), lambda j: (0, j))
    return pl.BlockSpec((parts, t, TC), lambda j: (0, 0, j))


def _scmix_fwd(z, w):
    t = z.shape[1]

    def body(z_ref, w_ref, m_ref):
        b, c, u = (z_ref[p].astype(F32) for p in range(3))
        m_ref[...] = (b * _conv3(c * u, w_ref)).astype(BF16)

    return _tc_call(
        body, name="scmix_fwd", grid=(D // TC,), in_specs=[_col(3, t), pl.BlockSpec((3, TC), lambda j: (0, j))],
        out_specs=_col(None, t), out_shape=jax.ShapeDtypeStruct((t, D), BF16), compiler_params=_cp("parallel"),
    )(z, w)


def _scmix_bwd(z, w, dm):
    t = z.shape[1]

    def body(z_ref, w_ref, dm_ref, dz_ref, dw_ref):
        c, u = z_ref[1].astype(F32), z_ref[2].astype(F32)
        cu = c * u
        dmv = dm_ref[...].astype(F32)
        dz_ref[0] = (dmv * _conv3(cu, w_ref)).astype(BF16)
        dcv = dmv * z_ref[0].astype(F32)
        _conv3_dw(dcv, cu, dw_ref)
        dcu = _conv3_t(dcv, w_ref)
        dz_ref[1] = (dcu * u).astype(BF16)
        dz_ref[2] = (dcu * c).astype(BF16)

    wspec = pl.BlockSpec((3, TC), lambda j: (0, j))
    return _tc_call(
        body, name="scmix_bwd", grid=(D // TC,), in_specs=[_col(3, t), wspec, _col(None, t)],
        out_specs=[_col(3, t), wspec],
        out_shape=[jax.ShapeDtypeStruct((3, t, D), BF16), jax.ShapeDtypeStruct((3, D), F32)],
        compiler_params=_cp("parallel"),
    )(z, w, dm)


def _gate_fwd(up, w, bias, name):
    t = up.shape[1]

    def body(u_ref, w_ref, b_ref, a_ref):
        gc = _conv3(u_ref[0].astype(F32), w_ref) + b_ref[...]
        a_ref[...] = (gc * jax.nn.sigmoid(gc) * u_ref[1].astype(F32)).astype(BF16)

    return _tc_call(
        body, name=name, grid=(F_FF // TC,),
        in_specs=[_col(2, t), pl.BlockSpec((3, TC), lambda j: (0, j)), pl.BlockSpec((1, TC), lambda j: (0, j))],
        out_specs=_col(None, t), out_shape=jax.ShapeDtypeStruct((t, F_FF), BF16), compiler_params=_cp("parallel"),
    )(up, w, bias)


def _gate_bwd(up, w, bias, da, name):
    t = up.shape[1]

    def body(u_ref, w_ref, b_ref, da_ref, du_ref, dw_ref, db_ref):
        g = u_ref[0].astype(F32)
        gc = _conv3(g, w_ref) + b_ref[...]
        sg = jax.nn.sigmoid(gc)
        dav = da_ref[...].astype(F32)
        du_ref[1] = (dav * (gc * sg)).astype(BF16)
        dgc = dav * u_ref[1].astype(F32) * (sg * (1.0 + gc * (1.0 - sg)))
        db_ref[...] = jnp.sum(dgc, axis=0, keepdims=True)
        _conv3_dw(dgc, g, dw_ref)
        du_ref[0] = _conv3_t(dgc, w_ref).astype(BF16)

    wspec = pl.BlockSpec((3, TC), lambda j: (0, j))
    bspec = pl.BlockSpec((1, TC), lambda j: (0, j))
    return _tc_call(
        body, name=name, grid=(F_FF // TC,), in_specs=[_col(2, t), wspec, bspec, _col(None, t)],
        out_specs=[_col(2, t), wspec, bspec],
        out_shape=[jax.ShapeDtypeStruct((2, t, F_FF), BF16), jax.ShapeDtypeStruct((3, F_FF), F32),
                   jax.ShapeDtypeStruct((1, F_FF), F32)],
        compiler_params=_cp("parallel"),
    )(up, w, bias, da)


ATT_TQ = 256
ATT_SCALE = (QK_NOPE + QK_ROPE) ** -0.5


def _key_ranges(lvl):
    lo = lvl * ATT_TQ
    return ([(0, lo, False)] if lvl else []) + [(lo, lo + ATT_TQ, True)]


def _attn_probs(q, kn_ref, kr_ref, lvl):
    scores = []
    for lo, hi, diagonal in _key_ranges(lvl):
        s = lax.dot_general(q[:, :QK_NOPE], kn_ref[lo:hi, :], NT_DIMS, preferred_element_type=F32)
        s = s + lax.dot_general(q[:, QK_NOPE:], kr_ref[lo:hi, :], NT_DIMS, preferred_element_type=F32)
        s = s * ATT_SCALE
        if diagonal:
            row = lax.broadcasted_iota(jnp.int32, s.shape, 0)
            col = lax.broadcasted_iota(jnp.int32, s.shape, 1)
            seen = lax.shift_right_logical(col, CHUNK_SHIFT) <= lax.shift_right_logical(row, CHUNK_SHIFT)
            s = jnp.where(seen, s, NEG_INF)
        scores.append(s)
    m = jnp.max(scores[0], axis=1, keepdims=True)
    for s in scores[1:]:
        m = jnp.maximum(m, jnp.max(s, axis=1, keepdims=True))
    ps = [jnp.exp(s - m) for s in scores]
    total = jnp.sum(ps[0], axis=1, keepdims=True)
    for p in ps[1:]:
        total = total + jnp.sum(p, axis=1, keepdims=True)
    inv = 1.0 / total
    return [p * inv for p in ps]


def _per_query_block(qi, n_blocks, branch):
    for lvl in range(n_blocks):
        pl.when(qi == lvl)(lambda lvl=lvl: branch(lvl))


def _attn_specs(t):
    q = pl.BlockSpec((ATT_TQ, HEAD_PAD), lambda h, i: (i, h))
    kn = pl.BlockSpec((None, t, QK_NOPE), lambda h, i: (0, 0, h))
    kr = pl.BlockSpec((t, LANES), lambda h, i: (0, 0))
    v = pl.BlockSpec((None, t, V_HEAD), lambda h, i: (1, 0, h))
    o = pl.BlockSpec((ATT_TQ, V_HEAD), lambda h, i: (i, h))
    return q, kn, kr, v, o


def _attn_fwd(q, knv, kr):
    t = q.shape[0]

    def body(q_ref, kn_ref, kr_ref, v_ref, o_ref):
        def branch(lvl):
            ps = _attn_probs(q_ref[...], kn_ref, kr_ref, lvl)
            o = None
            for p, (lo, hi, _) in zip(ps, _key_ranges(lvl)):
                part = jnp.dot(p.astype(BF16), v_ref[lo:hi, :], preferred_element_type=F32)
                o = part if o is None else o + part
            o_ref[...] = o.astype(BF16)

        _per_query_block(pl.program_id(1), t // ATT_TQ, branch)

    qs, kns, krs, vs, os_ = _attn_specs(t)
    return _tc_call(
        body, name="attn_fwd", grid=(N_HEADS, t // ATT_TQ), in_specs=[qs, kns, krs, vs], out_specs=os_,
        out_shape=jax.ShapeDtypeStruct((t, N_HEADS * V_HEAD), BF16), compiler_params=_cp("parallel", "parallel"),
    )(q, knv, kr, knv)


def _attn_bwd(q, knv, kr, do, cos, sin):
    t = q.shape[0]

    def body(q_ref, kn_ref, kr_ref, v_ref, do_ref, c_ref, s_ref, dq_ref, dknv_ref, dkr_ref):
        h, qi = pl.program_id(0), pl.program_id(1)

        @pl.when(qi == 0)
        def _():
            dknv_ref[...] = jnp.zeros_like(dknv_ref)

        @pl.when((qi == 0) & (h == 0))
        def _():
            dkr_ref[...] = jnp.zeros_like(dkr_ref)

        def branch(lvl):
            qv, dov = q_ref[...], do_ref[...]
            ranges = _key_ranges(lvl)
            ps = _attn_probs(qv, kn_ref, kr_ref, lvl)
            dps = [lax.dot_general(dov, v_ref[lo:hi, :], NT_DIMS, preferred_element_type=F32) for lo, hi, _ in ranges]
            di = None
            for p, dp in zip(ps, dps):
                part = jnp.sum(p * dp, axis=1, keepdims=True)
                di = part if di is None else di + part
            dqn = dqr = None
            for p, dp, (lo, hi, _) in zip(ps, dps, ranges):
                ds = (p * (dp - di) * ATT_SCALE).astype(BF16)
                part_n = jnp.dot(ds, kn_ref[lo:hi, :], preferred_element_type=F32)
                part_r = jnp.dot(ds, kr_ref[lo:hi, :], preferred_element_type=F32)
                dqn, dqr = (part_n, part_r) if dqn is None else (dqn + part_n, dqr + part_r)
                dknv_ref[0, lo:hi, :] += lax.dot_general(ds, qv[:, :QK_NOPE], TN_DIMS, preferred_element_type=F32)
                dknv_ref[1, lo:hi, :] += lax.dot_general(p.astype(BF16), dov, TN_DIMS, preferred_element_type=F32)
                dkr_ref[lo:hi, :] += lax.dot_general(ds, qv[:, QK_NOPE:], TN_DIMS, preferred_element_type=F32)
            dq_ref[:, :QK_NOPE] = dqn.astype(BF16)
            dq_ref[:, QK_NOPE:] = _rope_bwd_math(dqr, c_ref[...], s_ref[...]).astype(BF16)

        _per_query_block(qi, t // ATT_TQ, branch)

    qs, kns, krs, vs, os_ = _attn_specs(t)
    tab = pl.BlockSpec((ATT_TQ, LANES), lambda h, i: (i, 0))
    return _tc_call(
        body, name="attn_bwd", grid=(N_HEADS, t // ATT_TQ), in_specs=[qs, kns, krs, vs, os_, tab, tab],
        out_specs=[qs, pl.BlockSpec((2, t, QK_NOPE), lambda h, i: (0, 0, h)), krs],
        out_shape=[jax.ShapeDtypeStruct((t, N_HEADS * HEAD_PAD), BF16),
                   jax.ShapeDtypeStruct((2, t, N_HEADS * QK_NOPE), F32), jax.ShapeDtypeStruct((t, LANES), F32)],
        compiler_params=_cp("arbitrary", "arbitrary"),
    )(q, knv, kr, knv, do, cos, sin)


def _adam_math(w, g, m, v):
    nm = ADAM_B1 * m + (1.0 - ADAM_B1) * g
    nv = ADAM_B2 * v + (1.0 - ADAM_B2) * (g * g)
    m_hat = nm / (1.0 - ADAM_B1 ** ADAM_STEP)
    v_hat = nv / (1.0 - ADAM_B2 ** ADAM_STEP)
    return -ADAM_LR * (m_hat / (jnp.sqrt(v_hat) + ADAM_EPS) + ADAM_WD * w), nm, nv


def _adamw_small(w, g, m, v):
    def body(w_ref, g_ref, m_ref, v_ref, d_ref, nm_ref, nv_ref):
        d_ref[...], nm_ref[...], nv_ref[...] = _adam_math(w_ref[...], g_ref[...], m_ref[...], v_ref[...])

    shp = jax.ShapeDtypeStruct(w.shape, F32)
    return _tc_call(body, name="adamw_small", out_shape=[shp] * 3)(w, g, m, v)


ADAM_BLOCK_BYTES = 1 << 20


def _adamw_shard(ids, w, m, v, g_mine, g_sib, name, layer=None, prev=None):
    r, c = w.shape[-2:]
    half = r // 2
    tr = _tile(half, [d for d in range(half, 7, -8) if d * c * 4 <= ADAM_BLOCK_BYTES] or [8])
    nbh = half // tr

    def body(ids_ref, w_ref, m_ref, v_ref, gm_ref, gs_ref, *rest):
        g_ref, d_ref, nm_ref, nv_ref = rest[-4:]
        mine = (pl.program_id(0) // nbh) == ids_ref[0]

        @pl.when(mine)
        def _():
            g_ref[...] = gm_ref[...]

        @pl.when(jnp.logical_not(mine))
        def _():
            g_ref[...] = gs_ref[...]

        d_ref[...], nm_ref[...], nv_ref[...] = _adam_math(w_ref[...], g_ref[...], m_ref[...], v_ref[...])

    if layer is None:
        wspec = pl.BlockSpec((tr, c), lambda i, ids: (i, 0))
    else:
        wspec = pl.BlockSpec((None, tr, c), lambda i, ids: (layer, i, 0))
    gspec = pl.BlockSpec((tr, c), lambda i, ids: (i % nbh, 0))
    in_specs = [wspec] * 3 + [gspec] * 2
    args = [ids, w, m, v, g_mine, g_sib]
    aliases = {}
    if prev is not None:
        in_specs += [ANY] * 4
        args += list(prev)
        aliases = {6 + k: k for k in range(4)}
    return _tc_call(
        body, name=name, prefetch=1, grid=(r // tr,), in_specs=in_specs, out_specs=[wspec] * 4,
        out_shape=[jax.ShapeDtypeStruct(w.shape, F32)] * 4, input_output_aliases=aliases,
        compiler_params=_cp("parallel"),
    )(*args)


def _peer_chip(k_me, j):
    return k_me ^ jnp.where(j == 0, 2, jnp.where(j == 1, 1, 3))


def _pair_sum(ids, g, ra, name):
    _, r, c = g.shape
    half = r // 2

    def body(ids_ref, g_ref, ra_ref, o_ref):
        o_ref[...] = (g_ref[...].astype(F32) + ra_ref[...].astype(F32)).astype(BF16)

    return _tc_call(
        body, name=name, prefetch=1, grid=(3,),
        in_specs=[pl.BlockSpec((None, half, c), lambda j, ids: (_peer_chip(ids[1], j), ids[0], 0)),
                  pl.BlockSpec((None, half, c), lambda j, ids: (_peer_chip(ids[1], j), 0, 0))],
        out_specs=pl.BlockSpec((None, half, c), lambda j, ids: (j, 0, 0)),
        out_shape=jax.ShapeDtypeStruct((3, half, c), BF16), compiler_params=_cp("parallel"),
    )(ids, g, ra)


def _chip_sum(ids, g, ra, rb, name):
    _, r, c = g.shape
    half = r // 2

    def body(ids_ref, g_ref, ra_ref, rb_ref, o_ref):
        acc = g_ref[...].astype(F32) + ra_ref[...].astype(F32)
        for j in range(3):
            acc = acc + rb_ref[j].astype(F32)
        o_ref[...] = acc

    return _tc_call(
        body, name=name, prefetch=1, grid=(1,),
        in_specs=[pl.BlockSpec((None, half, c), lambda i, ids: (ids[1], ids[0], 0)),
                  pl.BlockSpec((None, half, c), lambda i, ids: (ids[1], 0, 0)),
                  pl.BlockSpec((3, half, c), lambda i, ids: (0, 0, 0))],
        out_specs=pl.BlockSpec((half, c), lambda i, ids: (0, 0)),
        out_shape=jax.ShapeDtypeStruct((half, c), F32), compiler_params=_cp("arbitrary"),
    )(ids, g, ra, rb)


def _position():
    x, y, c = lax.axis_index("x"), lax.axis_index("y"), lax.axis_index("c")
    chips = [(1 - x, y), (x, 1 - y), (1 - x, 1 - y)]
    return x, y, c, chips


def _shard_half(ref, wm, h):
    if wm.kind == "tiny":
        return ref
    if wm.nl == 2:
        return ref.at[h]
    return ref.at[pl.ds(pl.multiple_of(h * (wm.k // 2), 16), wm.k // 2), :]


def _region(full, wm, s, h):
    if wm.kind == "tiny":
        return full.at[s]
    cols = pl.ds(pl.multiple_of(s * wm.n, LANES), wm.n) if wm.kind == "col" else slice(None)
    if wm.nl == 2:
        rows = pl.ds(pl.multiple_of(s * wm.k, 16), wm.k) if wm.kind == "row" else slice(None)
        return full.at[slice(None) if h is None else h, rows, cols]
    if wm.kind == "col":
        rows = slice(None) if h is None else pl.ds(pl.multiple_of(h * (wm.k // 2), 16), wm.k // 2)
    elif h is None:
        rows = pl.ds(pl.multiple_of(s * wm.k, 16), wm.k)
    else:
        rows = pl.ds(pl.multiple_of(s * wm.k + h * (wm.k // 2), 16), wm.k // 2)
    return full.at[rows, cols]


def _full_shape(wm):
    if wm.kind == "tiny":
        return (N_CHIPS, wm.k, wm.n)
    shape = (wm.k, N_CHIPS * wm.n) if wm.kind == "col" else (N_CHIPS * wm.k, wm.n)
    return shape if wm.nl == 1 else (wm.nl,) + shape


def _handshake(peers):
    barrier = pltpu.get_barrier_semaphore()
    for peer in peers:
        pl.semaphore_signal(barrier, inc=1, device_id=peer, device_id_type=MESH)
    pl.semaphore_wait(barrier, len(peers))


def _all_gather_group(gi, shards):
    wms = AG_GROUPS[gi]
    nw = len(wms)

    def body(*refs):
        sh, full = refs[:nw], refs[nw:2 * nw]
        ici_s, ici_r, pass_s, pass_r, own_s, own_r = refs[2 * nw:]
        x, y, c, chips = _position()
        me, sibling = 2 * x + y, (x, y, 1 - c)
        _handshake([(*chip, c) for chip in chips] + [sibling])

        def rcopy(src, dst, s_sem, r_sem, to):
            return pltpu.make_async_remote_copy(src_ref=src, dst_ref=dst, send_sem=s_sem, recv_sem=r_sem,
                                                device_id=to, device_id_type=MESH)

        started = []
        for i, wm in enumerate(wms):
            for j, chip in enumerate(chips):
                started.append(rcopy(_shard_half(sh[i], wm, c), _region(full[i], wm, me, c),
                                     ici_s.at[i, j], ici_r.at[i, j], (*chip, c)))
                started[-1].start()
            started.append(rcopy(sh[i], _region(full[i], wm, me, None), own_s.at[i], own_r.at[i], sibling))
            started[-1].start()
        for i, wm in enumerate(wms):
            for j, chip in enumerate(chips):
                got = _region(full[i], wm, 2 * chip[0] + chip[1], c)
                rcopy(got, got, ici_s.at[i, j], ici_r.at[i, j], sibling).wait_recv()
                if wm.kind != "tiny":
                    started.append(rcopy(got, got, pass_s.at[i, j], pass_r.at[i, j], sibling))
                    started[-1].start()
        for i, wm in enumerate(wms):
            mine = _region(full[i], wm, me, None)
            rcopy(mine, mine, own_s.at[i], own_r.at[i], sibling).wait_recv()
            for j, chip in enumerate(chips):
                if wm.kind != "tiny":
                    got = _region(full[i], wm, 2 * chip[0] + chip[1], 1 - c)
                    rcopy(got, got, pass_s.at[i, j], pass_r.at[i, j], sibling).wait_recv()
        for cp in started:
            cp.wait_send()

    return pl.kernel(
        body, out_type=[jax.ShapeDtypeStruct(_full_shape(wm), s.dtype) for wm, s in zip(wms, shards)],
        mesh=plsc.ScalarSubcoreMesh(axis_name="sequencer", num_cores=1), name=f"ag_group{gi}",
        scratch_types=[pltpu.SemaphoreType.DMA((nw, 3))] * 4 + [pltpu.SemaphoreType.DMA((nw,))] * 2,
        compiler_params=pltpu.CompilerParams(collective_id=gi),
    )(*shards)


def _sequencer_call(body, name, cid, out_types, scratch, args):
    return pl.kernel(
        body, out_type=out_types, mesh=plsc.ScalarSubcoreMesh(axis_name="sequencer", num_cores=1), name=name,
        scratch_types=scratch, compiler_params=pltpu.CompilerParams(collective_id=cid),
    )(*args)


def _pair_exchange(gs, tag, cid):
    n = len(gs)

    def body(*refs):
        g, out, send_sems, recv_sems = refs[:n], refs[n:2 * n], refs[2 * n], refs[2 * n + 1]
        x, y, c, _ = _position()
        _handshake([(x, y, 1 - c)])
        cps = []
        for i in range(n):
            half = g[i].shape[1] // 2
            cps.append(pltpu.make_async_remote_copy(
                src_ref=g[i].at[:, pl.ds(pl.multiple_of((1 - c) * half, 16), half), :], dst_ref=out[i],
                send_sem=send_sems.at[i], recv_sem=recv_sems.at[i], device_id=(x, y, 1 - c), device_id_type=MESH))
            cps[-1].start()
        for cp in cps:
            cp.wait()

    return _sequencer_call(
        body, f"rs_pair_exchange{tag}", cid,
        [jax.ShapeDtypeStruct((a.shape[0], a.shape[1] // 2, a.shape[2]), a.dtype) for a in gs],
        [pltpu.SemaphoreType.DMA((n,)), pltpu.SemaphoreType.DMA((n,))], gs)


def _chip_exchange(ss, tag, cid):
    n = len(ss)

    def body(*refs):
        s, out, send_sems, recv_sems = refs[:n], refs[n:2 * n], refs[2 * n], refs[2 * n + 1]
        x, y, c, chips = _position()
        _handshake([(*chip, c) for chip in chips])
        cps = []
        for i in range(n):
            for j, chip in enumerate(chips):
                cps.append(pltpu.make_async_remote_copy(
                    src_ref=s[i].at[j], dst_ref=out[i].at[j], send_sem=send_sems.at[i, j], recv_sem=recv_sems.at[i, j],
                    device_id=(*chip, c), device_id_type=MESH))
                cps[-1].start()
        for cp in cps:
            cp.wait()

    return _sequencer_call(
        body, f"rs_chip_exchange{tag}", cid, [jax.ShapeDtypeStruct(a.shape, a.dtype) for a in ss],
        [pltpu.SemaphoreType.DMA((n, 3)), pltpu.SemaphoreType.DMA((n, 3))], ss)


def _pair_swap(g8s, tag, cid):
    n = len(g8s)

    def body(*refs):
        g, out, send_sems, recv_sems = refs[:n], refs[n:2 * n], refs[2 * n], refs[2 * n + 1]
        x, y, c, _ = _position()
        _handshake([(x, y, 1 - c)])
        cps = []
        for i in range(n):
            cps.append(pltpu.make_async_remote_copy(
                src_ref=g[i], dst_ref=out[i], send_sem=send_sems.at[i], recv_sem=recv_sems.at[i],
                device_id=(x, y, 1 - c), device_id_type=MESH))
            cps[-1].start()
        for cp in cps:
            cp.wait()

    return _sequencer_call(
        body, f"rs_pair_swap{tag}", cid, [jax.ShapeDtypeStruct(a.shape, a.dtype) for a in g8s],
        [pltpu.SemaphoreType.DMA((n,)), pltpu.SemaphoreType.DMA((n,))], g8s)


def _all_reduce_small(vec, name):
    r, cols = vec.shape

    def body(v_ref, o_ref, gath, send_sems, recv_sems):
        x, y, c, _ = _position()
        me = 4 * x + 2 * y + c
        gath[me] = v_ref[...]
        cps = []
        for rel in range(1, N_DEV):
            peer = (x ^ (rel >> 2), y ^ ((rel >> 1) & 1), c ^ (rel & 1))
            cps.append(pltpu.make_async_remote_copy(
                src_ref=v_ref, dst_ref=gath.at[me], send_sem=send_sems.at[rel - 1], recv_sem=recv_sems.at[rel - 1],
                device_id=peer, device_id_type=MESH))
        for cp in cps:
            cp.start()
        for rel in range(1, N_DEV):
            pltpu.make_async_remote_copy(
                src_ref=v_ref, dst_ref=gath.at[me ^ rel], send_sem=send_sems.at[rel - 1],
                recv_sem=recv_sems.at[rel - 1], device_id=(x, y, c), device_id_type=MESH).wait_recv()
        for cp in cps:
            cp.wait_send()
        acc = gath[0]
        for d in range(1, N_DEV):
            acc = acc + gath[d]
        o_ref[...] = acc

    vm = pl.BlockSpec(memory_space=pltpu.VMEM)
    return _tc_call(
        body, name=name, in_specs=[vm], out_specs=vm, out_shape=jax.ShapeDtypeStruct((r, cols), F32),
        scratch_shapes=[pltpu.VMEM((N_DEV, r, cols), F32), pltpu.SemaphoreType.DMA((N_DEV - 1,)),
                        pltpu.SemaphoreType.DMA((N_DEV - 1,))],
    )(vec)


def _rope_tables(positions):
    half = QK_ROPE // 2
    inv_freq = 1.0 / (ROPE_THETA ** (jnp.arange(half, dtype=F32) / half))
    ang = positions.astype(F32)[:, None] * inv_freq
    zeros = jnp.zeros((positions.shape[0], LANES - QK_ROPE), F32)
    cos, sin = jnp.cos(ang), jnp.sin(ang)
    return jnp.concatenate([cos, cos, zeros], axis=1), jnp.concatenate([sin, sin, zeros], axis=1)


def _local_step(x, positions, tgt, wf, small, rs):
    cos, sin = _rope_tables(positions)
    w_in, w_out = wf["sc_w_in"], wf["sc_w_out"]
    w_ups, w_downs = (wf["ffn_w_up0"], wf["ffn_w_up1"]), (wf["ffn_w_down0"], wf["ffn_w_down1"])
    w_kv, w_ukv, w_dq, w_uq, w_o = wf["w_kv"], wf["w_ukv"], wf["w_dq"], wf["w_uq"], wf["w_o"]
    attn_norm, ffn_norm = small["attn_norm"], small["ffn_norm"]
    conv_b = small["ffn_conv_b"]

    def ffn_fwd(h, l):
        hf = _rms_fwd(h, ffn_norm[l:l + 1], f"ffn{l}_norm")
        up = _nn_parts(f"ffn{l}_up", hf, w_ups[l], 2, BF16)
        a = _gate_fwd(up, small["ffn_conv_w"][l], conv_b[l:l + 1], f"ffn{l}_gate")
        return _nn(f"ffn{l}_down", a, w_downs[l], F32, add=h), (hf, up, a)

    def ffn_bwd(h, dh_out, dh_out_b, l, saved, gi, hooks):
        run = lambda stage: hooks.get(stage, lambda: None)()
        hf, up, a = saved
        da = _nt(f"ffn{l}_down_dx", dh_out_b, w_downs[l], BF16)
        run("down_dx")
        d_down = _tn(f"ffn{l}_down_dw", a, dh_out_b, BF16)
        dup, d_cw, d_cb = _gate_bwd(up, small["ffn_conv_w"][l], conv_b[l:l + 1], da, f"ffn{l}_gate_bwd")
        run("gate_bwd")
        d_up = _dw_ffn_up(f"ffn{l}_up_dw", hf, dup)
        rs.start(gi, {f"ffn_w_down{l}": d_down.reshape(N_CHIPS, F_FF // N_CHIPS, D), f"ffn_w_up{l}": d_up})
        dhf = _nt_parts(f"ffn{l}_up_dx", dup, w_ups[l], BF16)
        run("up_dx")
        dh, dh_b, d_norm = _rms_bwd(h, ffn_norm[l:l + 1], dhf, dh_out, f"ffn{l}_norm_bwd", matmul_copy=True)
        return dh, dh_b, d_cw, d_cb, d_norm

    hn0 = _rms_fwd(x, attn_norm[0:1], "attn0_norm")
    z = _nn_parts("sc_in", hn0, w_in, 3, BF16)
    mix = _scmix_fwd(z, small["sc_conv_w"])
    h1 = _nn("sc_out", mix, w_out, F32, add=x)
    h2, ffn0_saved = ffn_fwd(h1, 0)

    hk = _rms_fwd(h2, small["kv_in_norm"], "kv_in_norm")
    kvpre = _nn("kv_down", hk, w_kv, F32)
    ckv, kr = _kv_elem_fwd(kvpre, small["kv_latent_norm"], cos, sin)
    knv = _nn_parts("kv_up", ckv, w_ukv, 2, BF16, stacked=True)

    hn1 = _rms_fwd(h2, attn_norm[1:2], "attn1_norm")
    cq_pre = _nn("q_down", hn1, w_dq, F32)
    cq = _rms_fwd(cq_pre, small["q_latent_norm"], "q_latent_norm")
    q = _q_rope_fwd(_nn("q_up", cq, w_uq, F32), cos, sin)
    o = _attn_fwd(q, knv, kr)
    h3 = _nn("attn_out", o, w_o, F32, add=h2)
    h4, ffn1_saved = ffn_fwd(h3, 1)

    loss, dh4, dh4_b, d_final = _loss_head(h4, small["final_norm"], tgt)

    rows = D // N_CHIPS
    dh3, dh3_b, d_cw1, d_cb1, d_fn1 = ffn_bwd(h3, dh4, dh4_b, 1, ffn1_saved, 0, {})

    do = _nt("attn_out_dx", dh3_b, w_o, BF16)
    d_wo = _tn("attn_out_dw", o, dh3_b, BF16)
    rs.pair_sums(0)
    dq, dknv, dkr = _attn_bwd(q, knv, kr, do, cos, sin)
    rs.chip_sums(0)
    dcq = _nt("q_up_dx", dq, w_uq, F32)
    d_wuq = _tn("q_up_dw", cq, dq, BF16).reshape(Q_LORA, N_CHIPS, -1).transpose(1, 0, 2)
    dcq_pre, d_qln = _rms_bwd(cq_pre, small["q_latent_norm"], dcq, None, "q_latent_norm_bwd")
    rs.finish(0)
    dhn1 = _nt("q_down_dx", dcq_pre, w_dq, BF16)
    d_wdq = _tn("q_down_dw", hn1, dcq_pre, BF16)
    dh2, d_an1 = _rms_bwd(h2, attn_norm[1:2], dhn1, dh3, "attn1_norm_bwd")

    dckv = _nt_parts("kv_up_dx", dknv, w_ukv, F32, stacked=True)
    d_wukv = _dw_ukv(ckv, dknv)
    dkvpre, d_kvln = _kv_elem_bwd(kvpre, small["kv_latent_norm"], dckv, dkr, cos, sin)
    dhk = _nt("kv_down_dx", dkvpre, w_kv, BF16)
    d_wkv = _tn("kv_down_dw", hk, dkvpre, BF16)
    rs.start(1, {
        "w_o": d_wo.reshape(N_CHIPS, rows, D), "w_uq": d_wuq, "w_dq": d_wdq.reshape(N_CHIPS, rows, Q_LORA),
        "w_ukv": d_wukv.reshape(N_CHIPS, 2 * KV_LORA, -1), "w_kv": d_wkv.reshape(N_CHIPS, rows, KVP),
    })
    dh2, dh2_b, d_kvin = _rms_bwd(h2, small["kv_in_norm"], dhk, dh2, "kv_in_norm_bwd", matmul_copy=True)

    dh1, dh1_b, d_cw0, d_cb0, d_fn0 = ffn_bwd(h1, dh2, dh2_b, 0, ffn0_saved, 2, {
        "down_dx": lambda: rs.pair_sums(1), "gate_bwd": lambda: rs.chip_sums(1), "up_dx": lambda: rs.finish(1)})
    rs.pair_sums(2)

    d_wout = _tn("sc_out_dw", mix, dh1_b, BF16)
    dmix = _nt("sc_out_dx", dh1_b, w_out, BF16)
    dz, d_scw = _scmix_bwd(z, small["sc_conv_w"], dmix)
    d_win = _dw_sc_in(hn0, dz)
    rs.start(3, {"sc_w_out": d_wout.reshape(N_CHIPS, rows, D), "sc_w_in": d_win})
    dhn0 = _nt_parts("sc_in_dx", dz, w_in, BF16)
    dx, d_an0 = _rms_bwd(x, attn_norm[0:1], dhn0, dh1, "attn0_norm_bwd")

    small_g = {
        "attn_norm": jnp.concatenate([d_an0, d_an1]), "ffn_norm": jnp.concatenate([d_fn0, d_fn1]),
        "final_norm": d_final, "kv_in_norm": d_kvin, "kv_latent_norm": d_kvln, "q_latent_norm": d_qln,
        "ffn_conv_b": jnp.concatenate([d_cb0, d_cb1]), "sc_conv_w": d_scw, "ffn_conv_w": jnp.stack([d_cw0, d_cw1]),
    }
    return loss, dx, small_g


RS_GROUPS = (("ffn_w_down1", "ffn_w_up1"), ("w_o", "w_uq", "w_dq", "w_ukv", "w_kv"),
             ("ffn_w_down0", "ffn_w_up0"), ("sc_w_out", "sc_w_in"))


class _ReduceScatter:
    def __init__(self, ids, finish):
        self.ids, self.grads, self.step, self.mine, self.sib, self.finish = ids, {}, {}, {}, {}, finish

    def _cid(self, gi):
        return len(AG_GROUPS) + 3 * gi

    def start(self, gi, grads):
        self.grads.update(grads)
        own = [grads[n] for n in RS_GROUPS[gi]]
        self.step[gi] = (own, _pair_exchange(own, gi, self._cid(gi)))

    def pair_sums(self, gi):
        own, ra = self.step[gi]
        sums = [_pair_sum(self.ids, g, a, f"rs_pair_sum_{n}") for n, g, a in zip(RS_GROUPS[gi], own, ra)]
        self.step[gi] = (own, ra, _chip_exchange(sums, gi, self._cid(gi) + 1))

    def chip_sums(self, gi):
        own, ra, rb = self.step[gi]
        mine = [_chip_sum(self.ids, g, a, b, f"rs_chip_sum_{n}") for n, g, a, b in zip(RS_GROUPS[gi], own, ra, rb)]
        self.mine.update(zip(RS_GROUPS[gi], mine))
        self.sib.update(zip(RS_GROUPS[gi], _pair_swap(mine, gi, self._cid(gi) + 2)))

SMALL_REPL = ("attn_norm", "ffn_norm", "final_norm", "kv_in_norm", "kv_latent_norm", "q_latent_norm", "ffn_conv_b")
SMALL_SHARDED = ("sc_conv_w", "ffn_conv_w")
SMALL_ROWS = 256


def _pad_heads(w_uq):
    per_head = w_uq.reshape(Q_LORA, -1, QK_NOPE + QK_ROPE)
    return jnp.pad(per_head, ((0, 0), (0, 0), (0, HEAD_PAD - QK_NOPE - QK_ROPE))).reshape(Q_LORA, -1)


def _pack_kv(w_dkv, w_kr):
    return jnp.concatenate([w_dkv, w_kr, jnp.zeros((w_kr.shape[0], LANES - QK_ROPE), w_kr.dtype)], axis=1)


def kernel(x, positions, attn_norm, ffn_norm, final_norm, sc_w_in, sc_conv_w, sc_w_out, kv_in_norm, w_dkv, kv_latent_norm, w_kr, w_uk, w_uv, w_dq, q_latent_norm, w_uq, w_o, ffn_w_up, ffn_conv_w, ffn_conv_b, ffn_w_down, loss_target, m_attn_norm, m_ffn_norm, m_final_norm, m_sc_w_in, m_sc_conv_w, m_sc_w_out, m_kv_in_norm, m_w_dkv, m_kv_latent_norm, m_w_kr, m_w_uk, m_w_uv, m_w_dq, m_q_latent_norm, m_w_uq, m_w_o, m_ffn_w_up, m_ffn_conv_w, m_ffn_conv_b, m_ffn_w_down, v_attn_norm, v_ffn_norm, v_final_norm, v_sc_w_in, v_sc_conv_w, v_sc_w_out, v_kv_in_norm, v_w_dkv, v_kv_latent_norm, v_w_kr, v_w_uk, v_w_uv, v_w_dq, v_q_latent_norm, v_w_uq, v_w_o, v_ffn_w_up, v_ffn_conv_w, v_ffn_conv_b, v_ffn_w_down):
    names = ("attn_norm", "ffn_norm", "final_norm", "sc_w_in", "sc_conv_w", "sc_w_out", "kv_in_norm", "w_dkv",
             "kv_latent_norm", "w_kr", "w_uk", "w_uv", "w_dq", "q_latent_norm", "w_uq", "w_o", "ffn_w_up",
             "ffn_conv_w", "ffn_conv_b", "ffn_w_down")
    w = dict(zip(names, (attn_norm, ffn_norm, final_norm, sc_w_in, sc_conv_w, sc_w_out, kv_in_norm, w_dkv,
                         kv_latent_norm, w_kr, w_uk, w_uv, w_dq, q_latent_norm, w_uq, w_o, ffn_w_up,
                         ffn_conv_w, ffn_conv_b, ffn_w_down)))
    m = dict(zip(names, (m_attn_norm, m_ffn_norm, m_final_norm, m_sc_w_in, m_sc_conv_w, m_sc_w_out, m_kv_in_norm,
                         m_w_dkv, m_kv_latent_norm, m_w_kr, m_w_uk, m_w_uv, m_w_dq, m_q_latent_norm, m_w_uq, m_w_o,
                         m_ffn_w_up, m_ffn_conv_w, m_ffn_conv_b, m_ffn_w_down)))
    v = dict(zip(names, (v_attn_norm, v_ffn_norm, v_final_norm, v_sc_w_in, v_sc_conv_w, v_sc_w_out, v_kv_in_norm,
                         v_w_dkv, v_kv_latent_norm, v_w_kr, v_w_uk, v_w_uv, v_w_dq, v_q_latent_norm, v_w_uq, v_w_o,
                         v_ffn_w_up, v_ffn_conv_w, v_ffn_conv_b, v_ffn_w_down)))

    _ORDER[0] = None
    ix, iy, ic = lax.axis_index("x"), lax.axis_index("y"), lax.axis_index("c")
    chip = 2 * ix + iy
    ids = jnp.stack([ic, chip]).astype(jnp.int32)

    def shards_of(t):
        return {
            "sc_w_in": t["sc_w_in"][0], "sc_w_out": t["sc_w_out"][0], "ffn_w_up": t["ffn_w_up"],
            "ffn_w_down": t["ffn_w_down"], "w_kv": _pack_kv(t["w_dkv"], t["w_kr"]),
            "w_ukv": jnp.stack([t["w_uk"], t["w_uv"]]), "w_dq": t["w_dq"][0], "w_uq": _pad_heads(t["w_uq"][0]),
            "w_o": t["w_o"][0],
        }

    ws, ms, vs = shards_of(w), shards_of(m), shards_of(v)

    def ag_shard(name):
        if name == "sc_conv_w":
            return sc_conv_w[0]
        if name == "ffn_conv_w":
            return ffn_conv_w.reshape(6, -1)
        if name[:-1] in ("ffn_w_up", "ffn_w_down"):
            return ws[name[:-1]][int(name[-1])].astype(BF16)
        return ws[name].astype(BF16)

    wf = {}
    for gi, wms in enumerate(AG_GROUPS):
        fulls = _all_gather_group(gi, [ag_shard(wm.name) for wm in wms])
        wf.update({wm.name: f for wm, f in zip(wms, fulls)})
    small = {
        "attn_norm": attn_norm, "ffn_norm": ffn_norm, "final_norm": final_norm[None], "kv_in_norm": kv_in_norm[None],
        "kv_latent_norm": kv_latent_norm[None], "q_latent_norm": q_latent_norm, "ffn_conv_b": ffn_conv_b,
        "sc_conv_w": wf["sc_conv_w"].transpose(1, 0, 2).reshape(3, D),
        "ffn_conv_w": wf["ffn_conv_w"].reshape(N_CHIPS, 2, 3, -1).transpose(1, 2, 0, 3).reshape(2, 3, F_FF),
    }

    res = {}

    def adamw_plain(n):
        res[n] = _adamw_shard(ids, ws[n], ms[n], vs[n], rs.mine[n], rs.sib[n], f"adamw_{n}")

    def adamw_layer(n, layer):
        key = f"{n}{layer}"
        res[n] = _adamw_shard(ids, ws[n], ms[n], vs[n], rs.mine[key], rs.sib[key], f"adamw_{key}", layer=layer,
                              prev=res.get(n))

    def adamw_group(gi):
        if gi in (0, 2):
            for n in ("ffn_w_up", "ffn_w_down"):
                adamw_layer(n, 1 if gi == 0 else 0)
        elif gi == 1:
            for n in ("w_kv", "w_dq", "w_uq", "w_o"):
                adamw_plain(n)
            merged = lambda a: a.reshape(2 * KV_LORA, -1)
            res["w_ukv"] = _adamw_shard(ids, merged(ws["w_ukv"]), merged(ms["w_ukv"]), merged(vs["w_ukv"]),
                                        rs.mine["w_ukv"], rs.sib["w_ukv"], "adamw_w_ukv")
        else:
            for n in ("sc_w_out", "sc_w_in"):
                adamw_plain(n)

    rs = _ReduceScatter(ids, adamw_group)
    loss, dx, small_g = _local_step(x[0], positions[0], loss_target[0], wf, small, rs)

    s_order = SMALL_REPL + SMALL_SHARDED
    flat = jnp.concatenate([small_g[n].reshape(-1) for n in s_order] + [loss.reshape(-1)])
    flat = jnp.pad(flat, (0, SMALL_ROWS * LANES - flat.shape[0])).reshape(SMALL_ROWS, LANES)
    red = _all_reduce_small(flat, "ar_small").reshape(-1)
    sg, off = {}, 0
    for n in s_order:
        sz = small_g[n].size
        sg[n] = red[off:off + sz].reshape(small_g[n].shape)
        off += sz
    loss_out = red[off]
    grads = {n: sg[n].reshape(w[n].shape) for n in SMALL_REPL}
    grads["sc_conv_w"] = lax.dynamic_slice_in_dim(sg["sc_conv_w"], chip * (D // N_CHIPS), D // N_CHIPS, axis=1)[None]
    grads["ffn_conv_w"] = lax.dynamic_slice_in_dim(sg["ffn_conv_w"], chip * (F_FF // N_CHIPS), F_FF // N_CHIPS, axis=2)

    rs.chip_sums(2)
    rs.pair_sums(3)
    rs.finish(2)
    rs.chip_sums(3)
    rs.finish(3)
    outs = [grads, {}, {}, {}]
    for k, dst in enumerate(outs):
        for n in ("sc_w_in", "sc_w_out", "w_dq", "w_o"):
            dst[n] = res[n][k][None]
        unpadded = res["w_uq"][k].reshape(Q_LORA, -1, HEAD_PAD)[:, :, :QK_NOPE + QK_ROPE]
        dst["w_uq"] = unpadded.reshape(w_uq.shape)
        dst["ffn_w_up"], dst["ffn_w_down"] = res["ffn_w_up"][k], res["ffn_w_down"][k]
        dst["w_dkv"], dst["w_kr"] = res["w_kv"][k][:, :KV_LORA], res["w_kv"][k][:, KV_LORA:KV_LORA + QK_ROPE]
        dst["w_uk"], dst["w_uv"] = res["w_ukv"][k][:KV_LORA], res["w_ukv"][k][KV_LORA:]
    grads, delta, new_m, new_v = outs

    small_names = SMALL_REPL + SMALL_SHARDED

    def pack_small(tree):
        return jnp.concatenate([tree[n].reshape(-1) for n in small_names]).reshape(-1, LANES)

    small_res = _adamw_small(pack_small(w), pack_small(grads), pack_small(m), pack_small(v))
    for slab, dst in zip(small_res, (delta, new_m, new_v)):
        f, off = slab.reshape(-1), 0
        for n in small_names:
            dst[n] = f[off:off + w[n].size].reshape(w[n].shape)
            off += w[n].size

    _ORDER[0] = None
    return (loss_out, dx[None], *[grads[n] for n in names], *[delta[n] for n in names],
            *[new_m[n] for n in names], *[new_v[n] for n in names])
```

```python
from typing import NamedTuple

import jax
import jax.numpy as jnp
from jax import lax
from jax.experimental import pallas as pl
from jax.experimental.pallas import tpu as pltpu
from jax.experimental.pallas import tpu_sc as plsc

F32 = jnp.float32
BF16 = jnp.bfloat16

T = 2048
D = 1024
F_FF = 2816
N_HEADS = 8
QK_NOPE = 128
QK_ROPE = 64
V_HEAD = 128
Q_LORA = 384
KV_LORA = 256
CHUNK_SHIFT = 6
ROPE_THETA = 10000.0
EPS = 1e-6
NEG_INF = -1e30
HEAD_PAD = 256
KVP = KV_LORA + 128

ADAM_LR = 0.001
ADAM_B1 = 0.9
ADAM_B2 = 0.999
ADAM_EPS = 1e-08
ADAM_WD = 0.01
ADAM_STEP = 10

N_CHIPS = 4
N_DEV = 8
LANES = 128
TC = 256
V7X_VMEM_LIMIT = 56 * 1024 * 1024

MESH = pl.DeviceIdType.MESH
ANY = pl.BlockSpec(memory_space=pl.ANY)


class _W(NamedTuple):
    name: str
    kind: str
    nl: int
    k: int
    n: int


AG_GROUPS = (
    (_W("sc_w_in", "col", 1, D, 3 * D // N_CHIPS), _W("sc_conv_w", "tiny", 1, 3, D // N_CHIPS),
     _W("ffn_conv_w", "tiny", 1, 6, F_FF // N_CHIPS)),
    (_W("sc_w_out", "row", 1, D // N_CHIPS, D),),
    (_W("ffn_w_up0", "col", 1, D, 2 * F_FF // N_CHIPS),),
    (_W("ffn_w_down0", "row", 1, F_FF // N_CHIPS, D),),
    (_W("w_kv", "row", 1, D // N_CHIPS, KVP), _W("w_ukv", "col", 2, KV_LORA, N_HEADS * QK_NOPE // N_CHIPS),
     _W("w_dq", "row", 1, D // N_CHIPS, Q_LORA),
     _W("w_uq", "col", 1, Q_LORA, N_HEADS * HEAD_PAD // N_CHIPS),
     _W("w_o", "row", 1, N_HEADS * V_HEAD // N_CHIPS, D)),
    (_W("ffn_w_up1", "col", 1, D, 2 * F_FF // N_CHIPS), _W("ffn_w_down1", "row", 1, F_FF // N_CHIPS, D)),
)


def _cp(*sem):
    return pltpu.CompilerParams(dimension_semantics=sem, vmem_limit_bytes=V7X_VMEM_LIMIT)


_ORDER = [None]


def _tc_call(body, *, name, out_shape, in_specs=None, out_specs=None, grid=(), scratch_shapes=(), prefetch=0,
             input_output_aliases=None, compiler_params=None):
    def run(*args):
        specs = [pl.BlockSpec(memory_space=pltpu.VMEM)] * (len(args) - prefetch) if in_specs is None else list(in_specs)
        inner, dep = body, _ORDER[0]
        if dep is not None:
            unread = prefetch + len(specs)
            specs, args = specs + [ANY], (*args, dep)

            def inner(*refs):
                return body(*refs[:unread], *refs[unread + 1:])

        kwargs = dict(name=name, out_shape=out_shape, input_output_aliases=input_output_aliases or {},
                      compiler_params=compiler_params)
        if prefetch:
            kwargs["grid_spec"] = pltpu.PrefetchScalarGridSpec(
                num_scalar_prefetch=prefetch, grid=grid, in_specs=specs, out_specs=out_specs,
                scratch_shapes=scratch_shapes)
        else:
            kwargs.update(grid=grid, in_specs=specs, scratch_shapes=scratch_shapes)
            if out_specs is not None:
                kwargs["out_specs"] = out_specs
        out = pl.pallas_call(inner, **kwargs)(*args)
        _ORDER[0] = out[0] if isinstance(out, (list, tuple)) else out
        return out

    return run


def _tile(n, cands):
    for c in cands:
        if n % c == 0:
            return c
    raise ValueError(f"no tile for {n}")


NN_DIMS = (((1,), (0,)), ((), ()))
NT_DIMS = (((1,), (1,)), ((), ()))
TN_DIMS = (((0,), (0,)), ((), ()))
M_TILES = (1024, 512, 384, 256, 128)
N_TILES = (1408, 1024, 768, 512, 384, 256, 128)
MM_BLOCK_BYTES = 36 * 1024 * 1024


def _fit(m, n, block_bytes, m_tiles=M_TILES, n_tiles=N_TILES):
    for tm in [c for c in m_tiles if m % c == 0]:
        for tn in [c for c in n_tiles if n % c == 0]:
            if 2 * block_bytes(tm, tn) + 4 * tm * tn <= MM_BLOCK_BYTES:
                return tm, tn
    raise ValueError(f"no tiles for {m} x {n}")


def _size(x):
    return x.dtype.itemsize


def _mm(name, a, b, dims, grid, a_spec, b_spec, o_spec, o_sds, add=None, red=None, acc_shape=None):
    n_red = None if red is None else grid[red]

    def body(*refs):
        a_ref, b_ref = refs[0], refs[1]
        add_ref = refs[2] if add is not None else None
        o_ref = refs[3] if add is not None else refs[2]
        part = lax.dot_general(a_ref[...].astype(BF16), b_ref[...].astype(BF16), dims, preferred_element_type=F32)
        if red is None:
            if add is not None:
                part = part + add_ref[...]
            o_ref[...] = part.astype(o_ref.dtype)
            return
        acc_ref = refs[-1]
        r = pl.program_id(red)

        @pl.when(r == 0)
        def _():
            acc_ref[...] = part

        @pl.when(r > 0)
        def _():
            acc_ref[...] += part

        @pl.when(r == n_red - 1)
        def _():
            o_ref[...] = acc_ref[...].astype(o_ref.dtype)

    sem = tuple("arbitrary" if ax == red else "parallel" for ax in range(len(grid)))
    in_specs = [a_spec, b_spec] + ([o_spec] if add is not None else [])
    args = (a, b) + ((add,) if add is not None else ())
    return _tc_call(
        body, name=name, grid=grid, in_specs=in_specs, out_specs=o_spec, out_shape=o_sds,
        scratch_shapes=[] if red is None else [pltpu.VMEM(acc_shape, F32)], compiler_params=_cp(*sem),
    )(*args)


def _nn(name, a, b, out_dtype, add=None, lead=None):
    (m, k), n = a.shape, b.shape[-1]
    osz = jnp.dtype(out_dtype).itemsize + (4 if add is not None else 0)
    tm, tn = _fit(m, n, lambda tm, tn: tm * k * _size(a) + k * tn * _size(b) + tm * tn * osz)
    if lead is None:
        b_spec = pl.BlockSpec((k, tn), lambda i, j: (0, j))
    else:
        b_spec = pl.BlockSpec((None, k, tn), lambda i, j: (lead, 0, j))
    return _mm(name, a, b, NN_DIMS, (m // tm, n // tn), pl.BlockSpec((tm, k), lambda i, j: (i, 0)), b_spec,
               pl.BlockSpec((tm, tn), lambda i, j: (i, j)), jax.ShapeDtypeStruct((m, n), out_dtype), add=add)


def _nn_parts(name, a, b, parts, out_dtype, lead=None, stacked=False):
    m, k = a.shape
    c = b.shape[-1] if stacked else b.shape[-1] // parts
    osz = jnp.dtype(out_dtype).itemsize
    tm, tn = _fit(m, c, lambda tm, tn: tm * k * _size(a) + k * tn * _size(b) + tm * tn * osz)
    nb = c // tn
    if stacked:
        b_spec = pl.BlockSpec((None, k, tn), lambda i, p, j: (p, 0, j))
    elif lead is None:
        b_spec = pl.BlockSpec((k, tn), lambda i, p, j: (0, p * nb + j))
    else:
        b_spec = pl.BlockSpec((None, k, tn), lambda i, p, j: (lead, 0, p * nb + j))
    return _mm(name, a, b, NN_DIMS, (m // tm, parts, nb), pl.BlockSpec((tm, k), lambda i, p, j: (i, 0)), b_spec,
               pl.BlockSpec((None, tm, tn), lambda i, p, j: (p, i, j)), jax.ShapeDtypeStruct((parts, m, c), out_dtype))


def _nt(name, a, b, out_dtype, lead=None):
    (m, k), n = a.shape, b.shape[-2]
    osz = jnp.dtype(out_dtype).itemsize
    tm, tn = _fit(m, n, lambda tm, tn: tm * k * _size(a) + tn * k * _size(b) + tm * tn * osz)
    if lead is None:
        b_spec = pl.BlockSpec((tn, k), lambda i, j: (j, 0))
    else:
        b_spec = pl.BlockSpec((None, tn, k), lambda i, j: (lead, j, 0))
    return _mm(name, a, b, NT_DIMS, (m // tm, n // tn), pl.BlockSpec((tm, k), lambda i, j: (i, 0)), b_spec,
               pl.BlockSpec((tm, tn), lambda i, j: (i, j)), jax.ShapeDtypeStruct((m, n), out_dtype))


def _nt_parts(name, a, b, out_dtype, lead=None, stacked=False):
    parts, m, c = a.shape
    n = b.shape[-2]
    osz = jnp.dtype(out_dtype).itemsize + 2
    tm, tn = _fit(m, n, lambda tm, tn: tm * c * _size(a) + tn * c * _size(b) + tm * tn * osz)
    if stacked:
        b_spec = pl.BlockSpec((None, tn, c), lambda i, j, p: (p, j, 0))
    elif lead is None:
        b_spec = pl.BlockSpec((tn, c), lambda i, j, p: (j, p))
    else:
        b_spec = pl.BlockSpec((None, tn, c), lambda i, j, p: (lead, j, p))
    return _mm(name, a, b, NT_DIMS, (m // tm, n // tn, parts), pl.BlockSpec((None, tm, c), lambda i, j, p: (p, i, 0)),
               b_spec, pl.BlockSpec((tm, tn), lambda i, j, p: (i, j)), jax.ShapeDtypeStruct((m, n), out_dtype),
               red=2, acc_shape=(tm, tn))


def _tn(name, a, b, out_dtype):
    (k, m), n = a.shape, b.shape[1]
    osz = jnp.dtype(out_dtype).itemsize
    tm, tn = _fit(m, n, lambda tm, tn: k * tm * _size(a) + k * tn * _size(b) + tm * tn * osz,
                  m_tiles=(512, 384, 256, 128), n_tiles=(n,) + N_TILES)
    return _mm(name, a, b, TN_DIMS, (m // tm, n // tn), pl.BlockSpec((k, tm), lambda i, j: (0, i)),
               pl.BlockSpec((k, tn), lambda i, j: (0, j)), pl.BlockSpec((tm, tn), lambda i, j: (i, j)),
               jax.ShapeDtypeStruct((m, n), out_dtype))


def _dw_sc_in(hn, dz):
    t, tn, tm = hn.shape[0], TC, 512
    per_part, per_chip = D // tn, 3 * D // N_CHIPS // tn
    return _mm("sc_in_dw", hn, dz, TN_DIMS, (D // tm, 3 * D // tn), pl.BlockSpec((t, tm), lambda i, j: (0, i)),
               pl.BlockSpec((None, t, tn), lambda i, j: (j // per_part, 0, j % per_part)),
               pl.BlockSpec((None, tm, tn), lambda i, j: (j // per_chip, i, j % per_chip)),
               jax.ShapeDtypeStruct((N_CHIPS, D, 3 * D // N_CHIPS), BF16))


def _dw_ffn_up(name, hf, dup):
    t, tm, ns = hf.shape[0], 512, 2 * F_FF // N_CHIPS
    return _mm(name, hf, dup, TN_DIMS, (N_CHIPS, D // tm), pl.BlockSpec((t, tm), lambda s, i: (0, i)),
               pl.BlockSpec((None, t, ns), lambda s, i: (s // 2, 0, s % 2)),
               pl.BlockSpec((None, tm, ns), lambda s, i: (s, i, 0)), jax.ShapeDtypeStruct((N_CHIPS, D, ns), BF16))


def _dw_ukv(ckv, dknv):
    t, ns = ckv.shape[0], N_HEADS * QK_NOPE // N_CHIPS
    return _mm("kv_up_dw", ckv, dknv, TN_DIMS, (2, N_CHIPS), pl.BlockSpec((t, KV_LORA), lambda p, s: (0, 0)),
               pl.BlockSpec((None, t, ns), lambda p, s: (p, 0, s)),
               pl.BlockSpec((None, None, KV_LORA, ns), lambda p, s: (s, p, 0, 0)),
               jax.ShapeDtypeStruct((N_CHIPS, 2, KV_LORA, ns), BF16))


def _rms_fwd(x, g, name):
    t, d = x.shape
    tr = 512

    def body(x_ref, g_ref, o_ref):
        xv = x_ref[...]
        r = lax.rsqrt(jnp.mean(xv * xv, axis=1, keepdims=True) + EPS)
        o_ref[...] = (xv * r * g_ref[...]).astype(o_ref.dtype)

    row = pl.BlockSpec((tr, d), lambda i: (i, 0))
    return _tc_call(
        body, name=name, grid=(t // tr,), in_specs=[row, pl.BlockSpec((1, d), lambda i: (0, 0))],
        out_specs=row, out_shape=jax.ShapeDtypeStruct((t, d), BF16), compiler_params=_cp("parallel"),
    )(x, g)


def _rms_bwd_math(xv, g, dy):
    r = lax.rsqrt(jnp.mean(xv * xv, axis=1, keepdims=True) + EPS)
    xh = xv * r
    gy = dy * g
    dx = r * (gy - xh * jnp.mean(gy * xh, axis=1, keepdims=True))
    dg = jnp.sum(dy * xh, axis=0, keepdims=True)
    return dx, dg


def _rms_bwd(x, g, dy, add, name, matmul_copy=False):
    t, d = x.shape
    tr = 512
    n_in = 3 + (add is not None)

    def body(*refs):
        x_ref, g_ref, dy_ref = refs[:3]
        dx_ref, dg_ref = refs[n_in], refs[-1]
        dx, dg = _rms_bwd_math(x_ref[...], g_ref[...], dy_ref[...].astype(F32))
        if add is not None:
            dx = dx + refs[3][...]
        dx_ref[...] = dx
        if matmul_copy:
            refs[n_in + 1][...] = dx.astype(BF16)

        @pl.when(pl.program_id(0) == 0)
        def _():
            dg_ref[...] = jnp.zeros_like(dg_ref)

        dg_ref[...] += dg

    row = pl.BlockSpec((tr, d), lambda i: (i, 0))
    vec = pl.BlockSpec((1, d), lambda i: (0, 0))
    in_specs = [row, vec, row] + ([row] if add is not None else [])
    args = (x, g, dy) + ((add,) if add is not None else ())
    copies = [jax.ShapeDtypeStruct((t, d), BF16)] if matmul_copy else []
    return _tc_call(
        body, name=name, grid=(t // tr,), in_specs=in_specs, out_specs=[row] * (1 + len(copies)) + [vec],
        out_shape=[jax.ShapeDtypeStruct((t, d), F32)] + copies + [jax.ShapeDtypeStruct((1, d), F32)],
        compiler_params=_cp("arbitrary"),
    )(*args)


def _loss_head(h, g, tgt):
    t, d = h.shape
    tr = 512

    def body(h_ref, g_ref, t_ref, loss_ref, dh_ref, dhb_ref, dg_ref):
        xv = h_ref[...]
        gv = g_ref[...]
        r = lax.rsqrt(jnp.mean(xv * xv, axis=1, keepdims=True) + EPS)
        err = xv * r * gv - t_ref[...]
        part = 0.5 * jnp.sum(jnp.mean(err * err, axis=1, keepdims=True), axis=0, keepdims=True)
        dx, dg = _rms_bwd_math(xv, gv, err * (1.0 / d))
        dh_ref[...] = dx
        dhb_ref[...] = dx.astype(BF16)

        @pl.when(pl.program_id(0) == 0)
        def _():
            dg_ref[...] = jnp.zeros_like(dg_ref)
            loss_ref[...] = jnp.zeros_like(loss_ref)

        dg_ref[...] += dg
        loss_ref[...] += jnp.broadcast_to(part, loss_ref.shape)

    row = pl.BlockSpec((tr, d), lambda i: (i, 0))
    vec = pl.BlockSpec((1, d), lambda i: (0, 0))
    lspec = pl.BlockSpec((1, LANES), lambda i: (0, 0))
    return _tc_call(
        body, name="loss_head", grid=(t // tr,), in_specs=[row, vec, row], out_specs=[lspec, row, row, vec],
        out_shape=[jax.ShapeDtypeStruct((1, LANES), F32), jax.ShapeDtypeStruct((t, d), F32),
                   jax.ShapeDtypeStruct((t, d), BF16), jax.ShapeDtypeStruct((1, d), F32)],
        compiler_params=_cp("arbitrary"),
    )(h, g, tgt)


def _rot_half(x):
    lane = lax.broadcasted_iota(jnp.int32, x.shape, 1)
    return jnp.where((lane % QK_ROPE) < QK_ROPE // 2, -pltpu.roll(x, LANES - 32, axis=1),
                     pltpu.roll(x, 32, axis=1))


def _rope_fwd_math(x, cos, sin):
    return x * cos + _rot_half(x) * sin


def _rope_bwd_math(dy, cos, sin):
    return dy * cos - _rot_half(dy * sin)


def _q_rope_fwd(qpre, cos, sin):
    t, w = qpre.shape
    tr = 256

    def body(q_ref, c_ref, s_ref, o_ref):
        cv, sv = c_ref[...], s_ref[...]
        for h in range(N_HEADS):
            lo = h * HEAD_PAD
            o_ref[:, lo:lo + QK_NOPE] = q_ref[:, lo:lo + QK_NOPE].astype(BF16)
            o_ref[:, lo + QK_NOPE:lo + HEAD_PAD] = _rope_fwd_math(
                q_ref[:, lo + QK_NOPE:lo + HEAD_PAD], cv, sv).astype(BF16)

    row = pl.BlockSpec((tr, w), lambda i: (i, 0))
    tab = pl.BlockSpec((tr, LANES), lambda i: (i, 0))
    return _tc_call(
        body, name="q_rope_fwd", grid=(t // tr,), in_specs=[row, tab, tab], out_specs=row,
        out_shape=jax.ShapeDtypeStruct((t, w), BF16), compiler_params=_cp("parallel"),
    )(qpre, cos, sin)


def _kv_elem_fwd(kvpre, g, cos, sin):
    t = kvpre.shape[0]
    tr = 512

    def body(p_ref, g_ref, c_ref, s_ref, ckv_ref, kr_ref):
        lat = p_ref[:, :KV_LORA]
        r = lax.rsqrt(jnp.mean(lat * lat, axis=1, keepdims=True) + EPS)
        ckv_ref[...] = (lat * r * g_ref[...]).astype(BF16)
        kr_ref[...] = _rope_fwd_math(p_ref[:, KV_LORA:], c_ref[...], s_ref[...]).astype(BF16)

    tab = pl.BlockSpec((tr, LANES), lambda i: (i, 0))
    return _tc_call(
        body, name="kv_elem_fwd", grid=(t // tr,),
        in_specs=[pl.BlockSpec((tr, KVP), lambda i: (i, 0)), pl.BlockSpec((1, KV_LORA), lambda i: (0, 0)), tab, tab],
        out_specs=[pl.BlockSpec((tr, KV_LORA), lambda i: (i, 0)), tab],
        out_shape=[jax.ShapeDtypeStruct((t, KV_LORA), BF16), jax.ShapeDtypeStruct((t, LANES), BF16)],
        compiler_params=_cp("parallel"),
    )(kvpre, g, cos, sin)


def _kv_elem_bwd(kvpre, g, dckv, dkr, cos, sin):
    t = kvpre.shape[0]
    tr = 512

    def body(p_ref, g_ref, dc_ref, dk_ref, c_ref, s_ref, dp_ref, dg_ref):
        dlat, dg = _rms_bwd_math(p_ref[:, :KV_LORA], g_ref[...], dc_ref[...])
        dp_ref[:, :KV_LORA] = dlat.astype(BF16)
        dp_ref[:, KV_LORA:] = _rope_bwd_math(dk_ref[...], c_ref[...], s_ref[...]).astype(BF16)

        @pl.when(pl.program_id(0) == 0)
        def _():
            dg_ref[...] = jnp.zeros_like(dg_ref)

        dg_ref[...] += dg

    tab = pl.BlockSpec((tr, LANES), lambda i: (i, 0))
    pre = pl.BlockSpec((tr, KVP), lambda i: (i, 0))
    vec = pl.BlockSpec((1, KV_LORA), lambda i: (0, 0))
    return _tc_call(
        body, name="kv_elem_bwd", grid=(t // tr,),
        in_specs=[pre, vec, pl.BlockSpec((tr, KV_LORA), lambda i: (i, 0)), tab, tab, tab],
        out_specs=[pre, vec],
        out_shape=[jax.ShapeDtypeStruct((t, KVP), BF16), jax.ShapeDtypeStruct((1, KV_LORA), F32)],
        compiler_params=_cp("arbitrary"),
    )(kvpre, g, dckv, dkr, cos, sin)


def _shift_down(x, k):
    row = lax.broadcasted_iota(jnp.int32, x.shape, 0)
    return jnp.where(row >= k, pltpu.roll(x, k, axis=0), 0.0)


def _shift_up(x, k):
    n = x.shape[0]
    row = lax.broadcasted_iota(jnp.int32, x.shape, 0)
    return jnp.where(row < n - k, pltpu.roll(x, n - k, axis=0), 0.0)


def _conv3(x, w_ref):
    return _shift_down(x, 2) * w_ref[0:1, :] + _shift_down(x, 1) * w_ref[1:2, :] + x * w_ref[2:3, :]


def _conv3_t(dy, w_ref):
    return dy * w_ref[2:3, :] + _shift_up(dy, 1) * w_ref[1:2, :] + _shift_up(dy, 2) * w_ref[0:1, :]


def _conv3_dw(dy, x, dw_ref):
    dw_ref[0:1, :] = jnp.sum(dy * _shift_down(x, 2), axis=0, keepdims=True)
    dw_ref[1:2, :] = jnp.sum(dy * _shift_down(x, 1), axis=0, keepdims=True)
    dw_ref[2:3, :] = jnp.sum(dy * x, axis=0, keepdims=True)


def _col(parts, t):
    if parts is None:
        return pl.BlockSpec((t, TC), lambda j: (0, j))
    return pl.BlockSpec((parts, t, TC), lambda j: (0, 0, j))


def _scmix_fwd(z, w):
    t = z.shape[1]

    def body(z_ref, w_ref, m_ref):
        b, c, u = (z_ref[p].astype(F32) for p in range(3))
        m_ref[...] = (b * _conv3(c * u, w_ref)).astype(BF16)

    return _tc_call(
        body, name="scmix_fwd", grid=(D // TC,), in_specs=[_col(3, t), pl.BlockSpec((3, TC), lambda j: (0, j))],
        out_specs=_col(None, t), out_shape=jax.ShapeDtypeStruct((t, D), BF16), compiler_params=_cp("parallel"),
    )(z, w)


def _scmix_bwd(z, w, dm):
    t = z.shape[1]

    def body(z_ref, w_ref, dm_ref, dz_ref, dw_ref):
        c, u = z_ref[1].astype(F32), z_ref[2].astype(F32)
        cu = c * u
        dmv = dm_ref[...].astype(F32)
        dz_ref[0] = (dmv * _conv3(cu, w_ref)).astype(BF16)
        dcv = dmv * z_ref[0].astype(F32)
        _conv3_dw(dcv, cu, dw_ref)
        dcu = _conv3_t(dcv, w_ref)
        dz_ref[1] = (dcu * u).astype(BF16)
        dz_ref[2] = (dcu * c).astype(BF16)

    wspec = pl.BlockSpec((3, TC), lambda j: (0, j))
    return _tc_call(
        body, name="scmix_bwd", grid=(D // TC,), in_specs=[_col(3, t), wspec, _col(None, t)],
        out_specs=[_col(3, t), wspec],
        out_shape=[jax.ShapeDtypeStruct((3, t, D), BF16), jax.ShapeDtypeStruct((3, D), F32)],
        compiler_params=_cp("parallel"),
    )(z, w, dm)


def _gate_fwd(up, w, bias, name):
    t = up.shape[1]

    def body(u_ref, w_ref, b_ref, a_ref):
        gc = _conv3(u_ref[0].astype(F32), w_ref) + b_ref[...]
        a_ref[...] = (gc * jax.nn.sigmoid(gc) * u_ref[1].astype(F32)).astype(BF16)

    return _tc_call(
        body, name=name, grid=(F_FF // TC,),
        in_specs=[_col(2, t), pl.BlockSpec((3, TC), lambda j: (0, j)), pl.BlockSpec((1, TC), lambda j: (0, j))],
        out_specs=_col(None, t), out_shape=jax.ShapeDtypeStruct((t, F_FF), BF16), compiler_params=_cp("parallel"),
    )(up, w, bias)


def _gate_bwd(up, w, bias, da, name):
    t = up.shape[1]

    def body(u_ref, w_ref, b_ref, da_ref, du_ref, dw_ref, db_ref):
        g = u_ref[0].astype(F32)
        gc = _conv3(g, w_ref) + b_ref[...]
        sg = jax.nn.sigmoid(gc)
        dav = da_ref[...].astype(F32)
        du_ref[1] = (dav * (gc * sg)).astype(BF16)
        dgc = dav * u_ref[1].astype(F32) * (sg * (1.0 + gc * (1.0 - sg)))
        db_ref[...] = jnp.sum(dgc, axis=0, keepdims=True)
        _conv3_dw(dgc, g, dw_ref)
        du_ref[0] = _conv3_t(dgc, w_ref).astype(BF16)

    wspec = pl.BlockSpec((3, TC), lambda j: (0, j))
    bspec = pl.BlockSpec((1, TC), lambda j: (0, j))
    return _tc_call(
        body, name=name, grid=(F_FF // TC,), in_specs=[_col(2, t), wspec, bspec, _col(None, t)],
        out_specs=[_col(2, t), wspec, bspec],
        out_shape=[jax.ShapeDtypeStruct((2, t, F_FF), BF16), jax.ShapeDtypeStruct((3, F_FF), F32),
                   jax.ShapeDtypeStruct((1, F_FF), F32)],
        compiler_params=_cp("parallel"),
    )(up, w, bias, da)


ATT_TQ = 256
ATT_SCALE = (QK_NOPE + QK_ROPE) ** -0.5


def _key_ranges(lvl):
    lo = lvl * ATT_TQ
    return ([(0, lo, False)] if lvl else []) + [(lo, lo + ATT_TQ, True)]


def _attn_probs(q, kn_ref, kr_ref, lvl):
    scores = []
    for lo, hi, diagonal in _key_ranges(lvl):
        s = lax.dot_general(q[:, :QK_NOPE], kn_ref[lo:hi, :], NT_DIMS, preferred_element_type=F32)
        s = s + lax.dot_general(q[:, QK_NOPE:], kr_ref[lo:hi, :], NT_DIMS, preferred_element_type=F32)
        s = s * ATT_SCALE
        if diagonal:
            row = lax.broadcasted_iota(jnp.int32, s.shape, 0)
            col = lax.broadcasted_iota(jnp.int32, s.shape, 1)
            seen = lax.shift_right_logical(col, CHUNK_SHIFT) <= lax.shift_right_logical(row, CHUNK_SHIFT)
            s = jnp.where(seen, s, NEG_INF)
        scores.append(s)
    m = jnp.max(scores[0], axis=1, keepdims=True)
    for s in scores[1:]:
        m = jnp.maximum(m, jnp.max(s, axis=1, keepdims=True))
    ps = [jnp.exp(s - m) for s in scores]
    total = jnp.sum(ps[0], axis=1, keepdims=True)
    for p in ps[1:]:
        total = total + jnp.sum(p, axis=1, keepdims=True)
    inv = 1.0 / total
    return [p * inv for p in ps]


def _per_query_block(qi, n_blocks, branch):
    for lvl in range(n_blocks):
        pl.when(qi == lvl)(lambda lvl=lvl: branch(lvl))


def _attn_specs(t):
    q = pl.BlockSpec((ATT_TQ, HEAD_PAD), lambda h, i: (i, h))
    kn = pl.BlockSpec((None, t, QK_NOPE), lambda h, i: (0, 0, h))
    kr = pl.BlockSpec((t, LANES), lambda h, i: (0, 0))
    v = pl.BlockSpec((None, t, V_HEAD), lambda h, i: (1, 0, h))
    o = pl.BlockSpec((ATT_TQ, V_HEAD), lambda h, i: (i, h))
    return q, kn, kr, v, o


def _attn_fwd(q, knv, kr):
    t = q.shape[0]

    def body(q_ref, kn_ref, kr_ref, v_ref, o_ref):
        def branch(lvl):
            ps = _attn_probs(q_ref[...], kn_ref, kr_ref, lvl)
            o = None
            for p, (lo, hi, _) in zip(ps, _key_ranges(lvl)):
                part = jnp.dot(p.astype(BF16), v_ref[lo:hi, :], preferred_element_type=F32)
                o = part if o is None else o + part
            o_ref[...] = o.astype(BF16)

        _per_query_block(pl.program_id(1), t // ATT_TQ, branch)

    qs, kns, krs, vs, os_ = _attn_specs(t)
    return _tc_call(
        body, name="attn_fwd", grid=(N_HEADS, t // ATT_TQ), in_specs=[qs, kns, krs, vs], out_specs=os_,
        out_shape=jax.ShapeDtypeStruct((t, N_HEADS * V_HEAD), BF16), compiler_params=_cp("parallel", "parallel"),
    )(q, knv, kr, knv)


def _attn_bwd(q, knv, kr, do, cos, sin):
    t = q.shape[0]

    def body(q_ref, kn_ref, kr_ref, v_ref, do_ref, c_ref, s_ref, dq_ref, dknv_ref, dkr_ref):
        h, qi = pl.program_id(0), pl.program_id(1)

        @pl.when(qi == 0)
        def _():
            dknv_ref[...] = jnp.zeros_like(dknv_ref)

        @pl.when((qi == 0) & (h == 0))
        def _():
            dkr_ref[...] = jnp.zeros_like(dkr_ref)

        def branch(lvl):
            qv, dov = q_ref[...], do_ref[...]
            ranges = _key_ranges(lvl)
            ps = _attn_probs(qv, kn_ref, kr_ref, lvl)
            dps = [lax.dot_general(dov, v_ref[lo:hi, :], NT_DIMS, preferred_element_type=F32) for lo, hi, _ in ranges]
            di = None
            for p, dp in zip(ps, dps):
                part = jnp.sum(p * dp, axis=1, keepdims=True)
                di = part if di is None else di + part
            dqn = dqr = None
            for p, dp, (lo, hi, _) in zip(ps, dps, ranges):
                ds = (p * (dp - di) * ATT_SCALE).astype(BF16)
                part_n = jnp.dot(ds, kn_ref[lo:hi, :], preferred_element_type=F32)
                part_r = jnp.dot(ds, kr_ref[lo:hi, :], preferred_element_type=F32)
                dqn, dqr = (part_n, part_r) if dqn is None else (dqn + part_n, dqr + part_r)
                dknv_ref[0, lo:hi, :] += lax.dot_general(ds, qv[:, :QK_NOPE], TN_DIMS, preferred_element_type=F32)
                dknv_ref[1, lo:hi, :] += lax.dot_general(p.astype(BF16), dov, TN_DIMS, preferred_element_type=F32)
                dkr_ref[lo:hi, :] += lax.dot_general(ds, qv[:, QK_NOPE:], TN_DIMS, preferred_element_type=F32)
            dq_ref[:, :QK_NOPE] = dqn.astype(BF16)
            dq_ref[:, QK_NOPE:] = _rope_bwd_math(dqr, c_ref[...], s_ref[...]).astype(BF16)

        _per_query_block(qi, t // ATT_TQ, branch)

    qs, kns, krs, vs, os_ = _attn_specs(t)
    tab = pl.BlockSpec((ATT_TQ, LANES), lambda h, i: (i, 0))
    return _tc_call(
        body, name="attn_bwd", grid=(N_HEADS, t // ATT_TQ), in_specs=[qs, kns, krs, vs, os_, tab, tab],
        out_specs=[qs, pl.BlockSpec((2, t, QK_NOPE), lambda h, i: (0, 0, h)), krs],
        out_shape=[jax.ShapeDtypeStruct((t, N_HEADS * HEAD_PAD), BF16),
                   jax.ShapeDtypeStruct((2, t, N_HEADS * QK_NOPE), F32), jax.ShapeDtypeStruct((t, LANES), F32)],
        compiler_params=_cp("arbitrary", "arbitrary"),
    )(q, knv, kr, knv, do, cos, sin)


def _adam_math(w, g, m, v):
    nm = ADAM_B1 * m + (1.0 - ADAM_B1) * g
    nv = ADAM_B2 * v + (1.0 - ADAM_B2) * (g * g)
    m_hat = nm / (1.0 - ADAM_B1 ** ADAM_STEP)
    v_hat = nv / (1.0 - ADAM_B2 ** ADAM_STEP)
    return -ADAM_LR * (m_hat / (jnp.sqrt(v_hat) + ADAM_EPS) + ADAM_WD * w), nm, nv


def _adamw_small(w, g, m, v):
    def body(w_ref, g_ref, m_ref, v_ref, d_ref, nm_ref, nv_ref):
        d_ref[...], nm_ref[...], nv_ref[...] = _adam_math(w_ref[...], g_ref[...], m_ref[...], v_ref[...])

    shp = jax.ShapeDtypeStruct(w.shape, F32)
    return _tc_call(body, name="adamw_small", out_shape=[shp] * 3)(w, g, m, v)


ADAM_SPLIT = 4


def _adamw_shards(ids, items, name):
    n = len(items)

    def body(ids_ref, *refs):
        outs = refs[len(refs) - 4 * n:]
        mine = pl.program_id(0) == ids_ref[0]
        for i in range(n):
            w_ref, m_ref, v_ref, gm_ref, gs_ref = refs[5 * i:5 * i + 5]
            g_ref, d_ref, nm_ref, nv_ref = outs[4 * i:4 * i + 4]

            @pl.when(mine)
            def _(g_ref=g_ref, gm_ref=gm_ref):
                g_ref[...] = gm_ref[...]

            @pl.when(jnp.logical_not(mine))
            def _(g_ref=g_ref, gs_ref=gs_ref):
                g_ref[...] = gs_ref[...]

            d_ref[...], nm_ref[...], nv_ref[...] = _adam_math(w_ref[...], g_ref[...], m_ref[...], v_ref[...])

    in_specs, out_specs, out_shape, args, carried, aliases = [], [], [], [ids], [], {}
    for i, it in enumerate(items):
        w = it["w"]
        r, c = w.shape[-2:]
        tr = r // 2 // ADAM_SPLIT
        assert tr % 8 == 0, (name, w.shape)
        layer = it.get("layer")
        if layer is None:
            wspec = pl.BlockSpec((tr, c), lambda h, k, ids: (h * ADAM_SPLIT + k, 0))
        else:
            wspec = pl.BlockSpec((None, tr, c), lambda h, k, ids, layer=layer: (layer, h * ADAM_SPLIT + k, 0))
        gspec = pl.BlockSpec((tr, c), lambda h, k, ids: (k, 0))
        in_specs += [wspec] * 3 + [gspec] * 2
        args += [w, it["m"], it["v"], it["g_mine"], it["g_sib"]]
        out_specs += [wspec] * 4
        out_shape += [jax.ShapeDtypeStruct(w.shape, F32)] * 4
        if it.get("prev") is not None:
            for k, p in enumerate(it["prev"]):
                aliases[1 + 5 * n + len(carried)] = 4 * i + k
                carried.append(p)
    res = _tc_call(
        body, name=name, prefetch=1, grid=(2, ADAM_SPLIT), in_specs=in_specs + [ANY] * len(carried),
        out_specs=out_specs, out_shape=out_shape, input_output_aliases=aliases,
        compiler_params=_cp("parallel", "parallel"),
    )(*args, *carried)
    return [res[4 * i:4 * i + 4] for i in range(n)]


def _peer_chip(k_me, j):
    return k_me ^ jnp.where(j == 0, 2, jnp.where(j == 1, 1, 3))


def _pair_sums(ids, gs, ras, name):
    n = len(gs)

    def body(ids_ref, *refs):
        for i in range(n):
            g_ref, ra_ref, o_ref = refs[2 * i], refs[2 * i + 1], refs[2 * n + i]
            o_ref[...] = (g_ref[...].astype(F32) + ra_ref[...].astype(F32)).astype(BF16)

    in_specs, out_specs, out_shape = [], [], []
    for g in gs:
        half, c = g.shape[1] // 2, g.shape[2]
        in_specs += [pl.BlockSpec((None, half, c), lambda j, ids: (_peer_chip(ids[1], j), ids[0], 0)),
                     pl.BlockSpec((None, half, c), lambda j, ids: (_peer_chip(ids[1], j), 0, 0))]
        out_specs.append(pl.BlockSpec((None, half, c), lambda j, ids: (j, 0, 0)))
        out_shape.append(jax.ShapeDtypeStruct((3, half, c), BF16))
    return _tc_call(
        body, name=name, prefetch=1, grid=(3,), in_specs=in_specs, out_specs=out_specs, out_shape=out_shape,
        compiler_params=_cp("parallel"),
    )(ids, *[a for pair in zip(gs, ras) for a in pair])


def _chip_sums(ids, gs, ras, rbs, name):
    n = len(gs)

    def body(ids_ref, *refs):
        for i in range(n):
            g_ref, ra_ref, rb_ref, o_ref = refs[3 * i], refs[3 * i + 1], refs[3 * i + 2], refs[3 * n + i]
            acc = g_ref[...].astype(F32) + ra_ref[...].astype(F32)
            for j in range(3):
                acc = acc + rb_ref[j].astype(F32)
            o_ref[...] = acc

    in_specs, out_specs, out_shape = [], [], []
    for g in gs:
        half, c = g.shape[1] // 2, g.shape[2]
        in_specs += [pl.BlockSpec((None, half, c), lambda i, ids: (ids[1], ids[0], 0)),
                     pl.BlockSpec((None, half, c), lambda i, ids: (ids[1], 0, 0)),
                     pl.BlockSpec((3, half, c), lambda i, ids: (0, 0, 0))]
        out_specs.append(pl.BlockSpec((half, c), lambda i, ids: (0, 0)))
        out_shape.append(jax.ShapeDtypeStruct((half, c), F32))
    return _tc_call(
        body, name=name, prefetch=1, grid=(1,), in_specs=in_specs, out_specs=out_specs, out_shape=out_shape,
        compiler_params=_cp("arbitrary"),
    )(ids, *[a for trio in zip(gs, ras, rbs) for a in trio])


def _position():
    x, y, c = lax.axis_index("x"), lax.axis_index("y"), lax.axis_index("c")
    chips = [(1 - x, y), (x, 1 - y), (1 - x, 1 - y)]
    return x, y, c, chips


def _shard_half(ref, wm, h):
    if wm.kind == "tiny":
        return ref
    if wm.nl == 2:
        return ref.at[h]
    return ref.at[pl.ds(pl.multiple_of(h * (wm.k // 2), 16), wm.k // 2), :]


def _region(full, wm, s, h):
    if wm.kind == "tiny":
        return full.at[s]
    cols = pl.ds(pl.multiple_of(s * wm.n, LANES), wm.n) if wm.kind == "col" else slice(None)
    if wm.nl == 2:
        rows = pl.ds(pl.multiple_of(s * wm.k, 16), wm.k) if wm.kind == "row" else slice(None)
        return full.at[slice(None) if h is None else h, rows, cols]
    if wm.kind == "col":
        rows = slice(None) if h is None else pl.ds(pl.multiple_of(h * (wm.k // 2), 16), wm.k // 2)
    elif h is None:
        rows = pl.ds(pl.multiple_of(s * wm.k, 16), wm.k)
    else:
        rows = pl.ds(pl.multiple_of(s * wm.k + h * (wm.k // 2), 16), wm.k // 2)
    return full.at[rows, cols]


def _full_shape(wm):
    if wm.kind == "tiny":
        return (N_CHIPS, wm.k, wm.n)
    shape = (wm.k, N_CHIPS * wm.n) if wm.kind == "col" else (N_CHIPS * wm.k, wm.n)
    return shape if wm.nl == 1 else (wm.nl,) + shape


def _handshake(peers):
    barrier = pltpu.get_barrier_semaphore()
    for peer in peers:
        pl.semaphore_signal(barrier, inc=1, device_id=peer, device_id_type=MESH)
    pl.semaphore_wait(barrier, len(peers))


def _all_gather_group(gi, shards):
    wms = AG_GROUPS[gi]
    nw = len(wms)

    def body(*refs):
        sh, full = refs[:nw], refs[nw:2 * nw]
        ici_s, ici_r, pass_s, pass_r, own_s, own_r = refs[2 * nw:]
        x, y, c, chips = _position()
        me, sibling = 2 * x + y, (x, y, 1 - c)
        _handshake([(*chip, c) for chip in chips] + [sibling])

        def rcopy(src, dst, s_sem, r_sem, to):
            return pltpu.make_async_remote_copy(src_ref=src, dst_ref=dst, send_sem=s_sem, recv_sem=r_sem,
                                                device_id=to, device_id_type=MESH)

        started = []
        for i, wm in enumerate(wms):
            for j, chip in enumerate(chips):
                started.append(rcopy(_shard_half(sh[i], wm, c), _region(full[i], wm, me, c),
                                     ici_s.at[i, j], ici_r.at[i, j], (*chip, c)))
                started[-1].start()
            started.append(rcopy(sh[i], _region(full[i], wm, me, None), own_s.at[i], own_r.at[i], sibling))
            started[-1].start()
        for i, wm in enumerate(wms):
            for j, chip in enumerate(chips):
                got = _region(full[i], wm, 2 * chip[0] + chip[1], c)
                rcopy(got, got, ici_s.at[i, j], ici_r.at[i, j], sibling).wait_recv()
                if wm.kind != "tiny":
                    started.append(rcopy(got, got, pass_s.at[i, j], pass_r.at[i, j], sibling))
                    started[-1].start()
        for i, wm in enumerate(wms):
            mine = _region(full[i], wm, me, None)
            rcopy(mine, mine, own_s.at[i], own_r.at[i], sibling).wait_recv()
            for j, chip in enumerate(chips):
                if wm.kind != "tiny":
                    got = _region(full[i], wm, 2 * chip[0] + chip[1], 1 - c)
                    rcopy(got, got, pass_s.at[i, j], pass_r.at[i, j], sibling).wait_recv()
        for cp in started:
            cp.wait_send()

    return pl.kernel(
        body, out_type=[jax.ShapeDtypeStruct(_full_shape(wm), s.dtype) for wm, s in zip(wms, shards)],
        mesh=plsc.ScalarSubcoreMesh(axis_name="sequencer", num_cores=1), name=f"ag_group{gi}",
        scratch_types=[pltpu.SemaphoreType.DMA((nw, 3))] * 4 + [pltpu.SemaphoreType.DMA((nw,))] * 2,
        compiler_params=pltpu.CompilerParams(collective_id=gi),
    )(*shards)


def _sequencer_call(body, name, cid, out_types, scratch, args):
    return pl.kernel(
        body, out_type=out_types, mesh=plsc.ScalarSubcoreMesh(axis_name="sequencer", num_cores=1), name=name,
        scratch_types=scratch, compiler_params=pltpu.CompilerParams(collective_id=cid),
    )(*args)


def _pair_exchange(gs, tag, cid):
    n = len(gs)

    def body(*refs):
        g, out, send_sems, recv_sems = refs[:n], refs[n:2 * n], refs[2 * n], refs[2 * n + 1]
        x, y, c, _ = _position()
        _handshake([(x, y, 1 - c)])
        cps = []
        for i in range(n):
            half = g[i].shape[1] // 2
            cps.append(pltpu.make_async_remote_copy(
                src_ref=g[i].at[:, pl.ds(pl.multiple_of((1 - c) * half, 16), half), :], dst_ref=out[i],
                send_sem=send_sems.at[i], recv_sem=recv_sems.at[i], device_id=(x, y, 1 - c), device_id_type=MESH))
            cps[-1].start()
        for cp in cps:
            cp.wait()

    return _sequencer_call(
        body, f"rs_pair_exchange{tag}", cid,
        [jax.ShapeDtypeStruct((a.shape[0], a.shape[1] // 2, a.shape[2]), a.dtype) for a in gs],
        [pltpu.SemaphoreType.DMA((n,)), pltpu.SemaphoreType.DMA((n,))], gs)


def _chip_exchange(ss, tag, cid):
    n = len(ss)

    def body(*refs):
        s, out, send_sems, recv_sems = refs[:n], refs[n:2 * n], refs[2 * n], refs[2 * n + 1]
        x, y, c, chips = _position()
        _handshake([(*chip, c) for chip in chips])
        cps = []
        for i in range(n):
            for j, chip in enumerate(chips):
                cps.append(pltpu.make_async_remote_copy(
                    src_ref=s[i].at[j], dst_ref=out[i].at[j], send_sem=send_sems.at[i, j], recv_sem=recv_sems.at[i, j],
                    device_id=(*chip, c), device_id_type=MESH))
                cps[-1].start()
        for cp in cps:
            cp.wait()

    return _sequencer_call(
        body, f"rs_chip_exchange{tag}", cid, [jax.ShapeDtypeStruct(a.shape, a.dtype) for a in ss],
        [pltpu.SemaphoreType.DMA((n, 3)), pltpu.SemaphoreType.DMA((n, 3))], ss)


def _pair_swap(g8s, tag, cid):
    n = len(g8s)

    def body(*refs):
        g, out, send_sems, recv_sems = refs[:n], refs[n:2 * n], refs[2 * n], refs[2 * n + 1]
        x, y, c, _ = _position()
        _handshake([(x, y, 1 - c)])
        cps = []
        for i in range(n):
            cps.append(pltpu.make_async_remote_copy(
                src_ref=g[i], dst_ref=out[i], send_sem=send_sems.at[i], recv_sem=recv_sems.at[i],
                device_id=(x, y, 1 - c), device_id_type=MESH))
            cps[-1].start()
        for cp in cps:
            cp.wait()

    return _sequencer_call(
        body, f"rs_pair_swap{tag}", cid, [jax.ShapeDtypeStruct(a.shape, a.dtype) for a in g8s],
        [pltpu.SemaphoreType.DMA((n,)), pltpu.SemaphoreType.DMA((n,))], g8s)


def _all_reduce_small(vec, name):
    r, cols = vec.shape

    def body(v_ref, o_ref, gath, send_sems, recv_sems):
        x, y, c, _ = _position()
        me = 4 * x + 2 * y + c
        gath[me] = v_ref[...]
        cps = []
        for rel in range(1, N_DEV):
            peer = (x ^ (rel >> 2), y ^ ((rel >> 1) & 1), c ^ (rel & 1))
            cps.append(pltpu.make_async_remote_copy(
                src_ref=v_ref, dst_ref=gath.at[me], send_sem=send_sems.at[rel - 1], recv_sem=recv_sems.at[rel - 1],
                device_id=peer, device_id_type=MESH))
        for cp in cps:
            cp.start()
        for rel in range(1, N_DEV):
            pltpu.make_async_remote_copy(
                src_ref=v_ref, dst_ref=gath.at[me ^ rel], send_sem=send_sems.at[rel - 1],
                recv_sem=recv_sems.at[rel - 1], device_id=(x, y, c), device_id_type=MESH).wait_recv()
        for cp in cps:
            cp.wait_send()
        acc = gath[0]
        for d in range(1, N_DEV):
            acc = acc + gath[d]
        o_ref[...] = acc

    vm = pl.BlockSpec(memory_space=pltpu.VMEM)
    return _tc_call(
        body, name=name, in_specs=[vm], out_specs=vm, out_shape=jax.ShapeDtypeStruct((r, cols), F32),
        scratch_shapes=[pltpu.VMEM((N_DEV, r, cols), F32), pltpu.SemaphoreType.DMA((N_DEV - 1,)),
                        pltpu.SemaphoreType.DMA((N_DEV - 1,))],
    )(vec)


def _rope_tables(positions):
    half = QK_ROPE // 2
    inv_freq = 1.0 / (ROPE_THETA ** (jnp.arange(half, dtype=F32) / half))
    ang = positions.astype(F32)[:, None] * inv_freq
    zeros = jnp.zeros((positions.shape[0], LANES - QK_ROPE), F32)
    cos, sin = jnp.cos(ang), jnp.sin(ang)
    return jnp.concatenate([cos, cos, zeros], axis=1), jnp.concatenate([sin, sin, zeros], axis=1)


def _local_step(x, positions, tgt, wf, small, rs):
    cos, sin = _rope_tables(positions)
    w_in, w_out = wf["sc_w_in"], wf["sc_w_out"]
    w_ups, w_downs = (wf["ffn_w_up0"], wf["ffn_w_up1"]), (wf["ffn_w_down0"], wf["ffn_w_down1"])
    w_kv, w_ukv, w_dq, w_uq, w_o = wf["w_kv"], wf["w_ukv"], wf["w_dq"], wf["w_uq"], wf["w_o"]
    attn_norm, ffn_norm = small["attn_norm"], small["ffn_norm"]
    conv_b = small["ffn_conv_b"]

    def ffn_fwd(h, l):
        hf = _rms_fwd(h, ffn_norm[l:l + 1], f"ffn{l}_norm")
        up = _nn_parts(f"ffn{l}_up", hf, w_ups[l], 2, BF16)
        a = _gate_fwd(up, small["ffn_conv_w"][l], conv_b[l:l + 1], f"ffn{l}_gate")
        return _nn(f"ffn{l}_down", a, w_downs[l], F32, add=h), (hf, up, a)

    def ffn_bwd(h, dh_out, dh_out_b, l, saved, gi, hooks):
        run = lambda stage: hooks.get(stage, lambda: None)()
        hf, up, a = saved
        da = _nt(f"ffn{l}_down_dx", dh_out_b, w_downs[l], BF16)
        run("down_dx")
        d_down = _tn(f"ffn{l}_down_dw", a, dh_out_b, BF16)
        dup, d_cw, d_cb = _gate_bwd(up, small["ffn_conv_w"][l], conv_b[l:l + 1], da, f"ffn{l}_gate_bwd")
        run("gate_bwd")
        d_up = _dw_ffn_up(f"ffn{l}_up_dw", hf, dup)
        rs.start(gi, {f"ffn_w_down{l}": d_down.reshape(N_CHIPS, F_FF // N_CHIPS, D), f"ffn_w_up{l}": d_up})
        dhf = _nt_parts(f"ffn{l}_up_dx", dup, w_ups[l], BF16)
        run("up_dx")
        dh, dh_b, d_norm = _rms_bwd(h, ffn_norm[l:l + 1], dhf, dh_out, f"ffn{l}_norm_bwd", matmul_copy=True)
        return dh, dh_b, d_cw, d_cb, d_norm

    hn0 = _rms_fwd(x, attn_norm[0:1], "attn0_norm")
    z = _nn_parts("sc_in", hn0, w_in, 3, BF16)
    mix = _scmix_fwd(z, small["sc_conv_w"])
    h1 = _nn("sc_out", mix, w_out, F32, add=x)
    h2, ffn0_saved = ffn_fwd(h1, 0)

    hk = _rms_fwd(h2, small["kv_in_norm"], "kv_in_norm")
    kvpre = _nn("kv_down", hk, w_kv, F32)
    ckv, kr = _kv_elem_fwd(kvpre, small["kv_latent_norm"], cos, sin)
    knv = _nn_parts("kv_up", ckv, w_ukv, 2, BF16, stacked=True)

    hn1 = _rms_fwd(h2, attn_norm[1:2], "attn1_norm")
    cq_pre = _nn("q_down", hn1, w_dq, F32)
    cq = _rms_fwd(cq_pre, small["q_latent_norm"], "q_latent_norm")
    q = _q_rope_fwd(_nn("q_up", cq, w_uq, F32), cos, sin)
    o = _attn_fwd(q, knv, kr)
    h3 = _nn("attn_out", o, w_o, F32, add=h2)
    h4, ffn1_saved = ffn_fwd(h3, 1)

    loss, dh4, dh4_b, d_final = _loss_head(h4, small["final_norm"], tgt)

    rows = D // N_CHIPS
    dh3, dh3_b, d_cw1, d_cb1, d_fn1 = ffn_bwd(h3, dh4, dh4_b, 1, ffn1_saved, 0, {})

    do = _nt("attn_out_dx", dh3_b, w_o, BF16)
    d_wo = _tn("attn_out_dw", o, dh3_b, BF16)
    rs.pair_sums(0)
    dq, dknv, dkr = _attn_bwd(q, knv, kr, do, cos, sin)
    rs.chip_sums(0)
    dcq = _nt("q_up_dx", dq, w_uq, F32)
    d_wuq = _tn("q_up_dw", cq, dq, BF16).reshape(Q_LORA, N_CHIPS, -1).transpose(1, 0, 2)
    dcq_pre, d_qln = _rms_bwd(cq_pre, small["q_latent_norm"], dcq, None, "q_latent_norm_bwd")
    rs.finish(0)
    dhn1 = _nt("q_down_dx", dcq_pre, w_dq, BF16)
    d_wdq = _tn("q_down_dw", hn1, dcq_pre, BF16)
    dh2, d_an1 = _rms_bwd(h2, attn_norm[1:2], dhn1, dh3, "attn1_norm_bwd")

    dckv = _nt_parts("kv_up_dx", dknv, w_ukv, F32, stacked=True)
    d_wukv = _dw_ukv(ckv, dknv)
    dkvpre, d_kvln = _kv_elem_bwd(kvpre, small["kv_latent_norm"], dckv, dkr, cos, sin)
    dhk = _nt("kv_down_dx", dkvpre, w_kv, BF16)
    d_wkv = _tn("kv_down_dw", hk, dkvpre, BF16)
    rs.start(1, {
        "w_o": d_wo.reshape(N_CHIPS, rows, D), "w_uq": d_wuq, "w_dq": d_wdq.reshape(N_CHIPS, rows, Q_LORA),
        "w_ukv": d_wukv.reshape(N_CHIPS, 2 * KV_LORA, -1), "w_kv": d_wkv.reshape(N_CHIPS, rows, KVP),
    })
    dh2, dh2_b, d_kvin = _rms_bwd(h2, small["kv_in_norm"], dhk, dh2, "kv_in_norm_bwd", matmul_copy=True)

    dh1, dh1_b, d_cw0, d_cb0, d_fn0 = ffn_bwd(h1, dh2, dh2_b, 0, ffn0_saved, 2, {
        "down_dx": lambda: rs.pair_sums(1), "gate_bwd": lambda: rs.chip_sums(1), "up_dx": lambda: rs.finish(1)})
    rs.pair_sums(2)

    d_wout = _tn("sc_out_dw", mix, dh1_b, BF16)
    dmix = _nt("sc_out_dx", dh1_b, w_out, BF16)
    dz, d_scw = _scmix_bwd(z, small["sc_conv_w"], dmix)
    d_win = _dw_sc_in(hn0, dz)
    rs.start(3, {"sc_w_out": d_wout.reshape(N_CHIPS, rows, D), "sc_w_in": d_win})
    dhn0 = _nt_parts("sc_in_dx", dz, w_in, BF16)
    dx, d_an0 = _rms_bwd(x, attn_norm[0:1], dhn0, dh1, "attn0_norm_bwd")

    small_g = {
        "attn_norm": jnp.concatenate([d_an0, d_an1]), "ffn_norm": jnp.concatenate([d_fn0, d_fn1]),
        "final_norm": d_final, "kv_in_norm": d_kvin, "kv_latent_norm": d_kvln, "q_latent_norm": d_qln,
        "ffn_conv_b": jnp.concatenate([d_cb0, d_cb1]), "sc_conv_w": d_scw, "ffn_conv_w": jnp.stack([d_cw0, d_cw1]),
    }
    return loss, dx, small_g


RS_GROUPS = (("ffn_w_down1", "ffn_w_up1"), ("w_o", "w_uq", "w_dq", "w_ukv", "w_kv"),
             ("ffn_w_down0", "ffn_w_up0"), ("sc_w_out", "sc_w_in"))


class _ReduceScatter:
    def __init__(self, ids, finish):
        self.ids, self.grads, self.step, self.mine, self.sib, self.finish = ids, {}, {}, {}, {}, finish

    def _cid(self, gi):
        return len(AG_GROUPS) + 3 * gi

    def start(self, gi, grads):
        self.grads.update(grads)
        own = [grads[n] for n in RS_GROUPS[gi]]
        self.step[gi] = (own, _pair_exchange(own, gi, self._cid(gi)))

    def pair_sums(self, gi):
        own, ra = self.step[gi]
        sums = _pair_sums(self.ids, own, ra, f"rs_pair_sums{gi}")
        self.step[gi] = (own, ra, _chip_exchange(sums, gi, self._cid(gi) + 1))

    def chip_sums(self, gi):
        own, ra, rb = self.step[gi]
        mine = _chip_sums(self.ids, own, ra, rb, f"rs_chip_sums{gi}")
        self.mine.update(zip(RS_GROUPS[gi], mine))
        self.sib.update(zip(RS_GROUPS[gi], _pair_swap(mine, gi, self._cid(gi) + 2)))

SMALL_REPL = ("attn_norm", "ffn_norm", "final_norm", "kv_in_norm", "kv_latent_norm", "q_latent_norm", "ffn_conv_b")
SMALL_SHARDED = ("sc_conv_w", "ffn_conv_w")
SMALL_ROWS = 256


def _pad_heads(w_uq):
    per_head = w_uq.reshape(Q_LORA, -1, QK_NOPE + QK_ROPE)
    return jnp.pad(per_head, ((0, 0), (0, 0), (0, HEAD_PAD - QK_NOPE - QK_ROPE))).reshape(Q_LORA, -1)


def _pack_kv(w_dkv, w_kr):
    return jnp.concatenate([w_dkv, w_kr, jnp.zeros((w_kr.shape[0], LANES - QK_ROPE), w_kr.dtype)], axis=1)


def kernel(x, positions, attn_norm, ffn_norm, final_norm, sc_w_in, sc_conv_w, sc_w_out, kv_in_norm, w_dkv, kv_latent_norm, w_kr, w_uk, w_uv, w_dq, q_latent_norm, w_uq, w_o, ffn_w_up, ffn_conv_w, ffn_conv_b, ffn_w_down, loss_target, m_attn_norm, m_ffn_norm, m_final_norm, m_sc_w_in, m_sc_conv_w, m_sc_w_out, m_kv_in_norm, m_w_dkv, m_kv_latent_norm, m_w_kr, m_w_uk, m_w_uv, m_w_dq, m_q_latent_norm, m_w_uq, m_w_o, m_ffn_w_up, m_ffn_conv_w, m_ffn_conv_b, m_ffn_w_down, v_attn_norm, v_ffn_norm, v_final_norm, v_sc_w_in, v_sc_conv_w, v_sc_w_out, v_kv_in_norm, v_w_dkv, v_kv_latent_norm, v_w_kr, v_w_uk, v_w_uv, v_w_dq, v_q_latent_norm, v_w_uq, v_w_o, v_ffn_w_up, v_ffn_conv_w, v_ffn_conv_b, v_ffn_w_down):
    names = ("attn_norm", "ffn_norm", "final_norm", "sc_w_in", "sc_conv_w", "sc_w_out", "kv_in_norm", "w_dkv",
             "kv_latent_norm", "w_kr", "w_uk", "w_uv", "w_dq", "q_latent_norm", "w_uq", "w_o", "ffn_w_up",
             "ffn_conv_w", "ffn_conv_b", "ffn_w_down")
    w = dict(zip(names, (attn_norm, ffn_norm, final_norm, sc_w_in, sc_conv_w, sc_w_out, kv_in_norm, w_dkv,
                         kv_latent_norm, w_kr, w_uk, w_uv, w_dq, q_latent_norm, w_uq, w_o, ffn_w_up,
                         ffn_conv_w, ffn_conv_b, ffn_w_down)))
    m = dict(zip(names, (m_attn_norm, m_ffn_norm, m_final_norm, m_sc_w_in, m_sc_conv_w, m_sc_w_out, m_kv_in_norm,
                         m_w_dkv, m_kv_latent_norm, m_w_kr, m_w_uk, m_w_uv, m_w_dq, m_q_latent_norm, m_w_uq, m_w_o,
                         m_ffn_w_up, m_ffn_conv_w, m_ffn_conv_b, m_ffn_w_down)))
    v = dict(zip(names, (v_attn_norm, v_ffn_norm, v_final_norm, v_sc_w_in, v_sc_conv_w, v_sc_w_out, v_kv_in_norm,
                         v_w_dkv, v_kv_latent_norm, v_w_kr, v_w_uk, v_w_uv, v_w_dq, v_q_latent_norm, v_w_uq, v_w_o,
                         v_ffn_w_up, v_ffn_conv_w, v_ffn_conv_b, v_ffn_w_down)))

    _ORDER[0] = None
    ix, iy, ic = lax.axis_index("x"), lax.axis_index("y"), lax.axis_index("c")
    chip = 2 * ix + iy
    ids = jnp.stack([ic, chip]).astype(jnp.int32)

    def shards_of(t):
        return {
            "sc_w_in": t["sc_w_in"][0], "sc_w_out": t["sc_w_out"][0], "ffn_w_up": t["ffn_w_up"],
            "ffn_w_down": t["ffn_w_down"], "w_kv": _pack_kv(t["w_dkv"], t["w_kr"]),
            "w_ukv": jnp.stack([t["w_uk"], t["w_uv"]]), "w_dq": t["w_dq"][0], "w_uq": _pad_heads(t["w_uq"][0]),
            "w_o": t["w_o"][0],
        }

    ws, ms, vs = shards_of(w), shards_of(m), shards_of(v)

    def ag_shard(name):
        if name == "sc_conv_w":
            return sc_conv_w[0]
        if name == "ffn_conv_w":
            return ffn_conv_w.reshape(6, -1)
        if name[:-1] in ("ffn_w_up", "ffn_w_down"):
            return ws[name[:-1]][int(name[-1])].astype(BF16)
        return ws[name].astype(BF16)

    wf = {}
    for gi, wms in enumerate(AG_GROUPS):
        fulls = _all_gather_group(gi, [ag_shard(wm.name) for wm in wms])
        wf.update({wm.name: f for wm, f in zip(wms, fulls)})
    small = {
        "attn_norm": attn_norm, "ffn_norm": ffn_norm, "final_norm": final_norm[None], "kv_in_norm": kv_in_norm[None],
        "kv_latent_norm": kv_latent_norm[None], "q_latent_norm": q_latent_norm, "ffn_conv_b": ffn_conv_b,
        "sc_conv_w": wf["sc_conv_w"].transpose(1, 0, 2).reshape(3, D),
        "ffn_conv_w": wf["ffn_conv_w"].reshape(N_CHIPS, 2, 3, -1).transpose(1, 2, 0, 3).reshape(2, 3, F_FF),
    }

    res = {}

    merged = lambda a: a.reshape(2 * KV_LORA, -1)

    def adamw_group(gi):
        items = []
        for key in RS_GROUPS[gi]:
            n, layer = (key[:-1], int(key[-1])) if key[:-1] in ("ffn_w_up", "ffn_w_down") else (key, None)
            w_, m_, v_ = (merged(t[n]) for t in (ws, ms, vs)) if n == "w_ukv" else (ws[n], ms[n], vs[n])
            items.append(dict(name=n, w=w_, m=m_, v=v_, g_mine=rs.mine[key], g_sib=rs.sib[key], layer=layer,
                              prev=res.get(n)))
        for it, out in zip(items, _adamw_shards(ids, items, f"adamw_group{gi}")):
            res[it["name"]] = out

    rs = _ReduceScatter(ids, adamw_group)
    loss, dx, small_g = _local_step(x[0], positions[0], loss_target[0], wf, small, rs)

    s_order = SMALL_REPL + SMALL_SHARDED
    flat = jnp.concatenate([small_g[n].reshape(-1) for n in s_order] + [loss.reshape(-1)])
    flat = jnp.pad(flat, (0, SMALL_ROWS * LANES - flat.shape[0])).reshape(SMALL_ROWS, LANES)
    red = _all_reduce_small(flat, "ar_small").reshape(-1)
    sg, off = {}, 0
    for n in s_order:
        sz = small_g[n].size
        sg[n] = red[off:off + sz].reshape(small_g[n].shape)
        off += sz
    loss_out = red[off]
    grads = {n: sg[n].reshape(w[n].shape) for n in SMALL_REPL}
    grads["sc_conv_w"] = lax.dynamic_slice_in_dim(sg["sc_conv_w"], chip * (D // N_CHIPS), D // N_CHIPS, axis=1)[None]
    grads["ffn_conv_w"] = lax.dynamic_slice_in_dim(sg["ffn_conv_w"], chip * (F_FF // N_CHIPS), F_FF // N_CHIPS, axis=2)

    rs.chip_sums(2)
    rs.pair_sums(3)
    rs.finish(2)
    rs.chip_sums(3)
    rs.finish(3)
    outs = [grads, {}, {}, {}]
    for k, dst in enumerate(outs):
        for n in ("sc_w_in", "sc_w_out", "w_dq", "w_o"):
            dst[n] = res[n][k][None]
        unpadded = res["w_uq"][k].reshape(Q_LORA, -1, HEAD_PAD)[:, :, :QK_NOPE + QK_ROPE]
        dst["w_uq"] = unpadded.reshape(w_uq.shape)
        dst["ffn_w_up"], dst["ffn_w_down"] = res["ffn_w_up"][k], res["ffn_w_down"][k]
        dst["w_dkv"], dst["w_kr"] = res["w_kv"][k][:, :KV_LORA], res["w_kv"][k][:, KV_LORA:KV_LORA + QK_ROPE]
        dst["w_uk"], dst["w_uv"] = res["w_ukv"][k][:KV_LORA], res["w_ukv"][k][KV_LORA:]
    grads, delta, new_m, new_v = outs

    small_names = SMALL_REPL + SMALL_SHARDED

    def pack_small(tree):
        return jnp.concatenate([tree[n].reshape(-1) for n in small_names]).reshape(-1, LANES)

    small_res = _adamw_small(pack_small(w), pack_small(grads), pack_small(m), pack_small(v))
    for slab, dst in zip(small_res, (delta, new_m, new_v)):
        f, off = slab.reshape(-1), 0
        for n in small_names:
            dst[n] = f[off:off + w[n].size].reshape(w[n].shape)
            off += w[n].size

    _ORDER[0] = None
    return (loss_out, dx[None], *[grads[n] for n in names], *[delta[n] for n in names],
            *[new_m[n] for n in names], *[new_v[n] for n in names])
```

```python
from typing import NamedTuple

import jax
import jax.numpy as jnp
from jax import lax
from jax.experimental import pallas as pl
from jax.experimental.pallas import tpu as pltpu
from jax.experimental.pallas import tpu_sc as plsc

F32 = jnp.float32
BF16 = jnp.bfloat16

T = 2048
D = 1024
F_FF = 2816
N_HEADS = 8
QK_NOPE = 128
QK_ROPE = 64
V_HEAD = 128
Q_LORA = 384
KV_LORA = 256
CHUNK_SHIFT = 6
ROPE_THETA = 10000.0
EPS = 1e-6
NEG_INF = -1e30
HEAD_PAD = 256
KVP = KV_LORA + 128

ADAM_LR = 0.001
ADAM_B1 = 0.9
ADAM_B2 = 0.999
ADAM_EPS = 1e-08
ADAM_WD = 0.01
ADAM_STEP = 10

N_CHIPS = 4
N_DEV = 8
LANES = 128
TC = 256
V7X_VMEM_LIMIT = 56 * 1024 * 1024

MESH = pl.DeviceIdType.MESH
ANY = pl.BlockSpec(memory_space=pl.ANY)


class _W(NamedTuple):
    name: str
    kind: str
    nl: int
    k: int
    n: int


AG_GROUPS = (
    (_W("sc_w_in", "col", 1, D, 3 * D // N_CHIPS), _W("sc_conv_w", "tiny", 1, 3, D // N_CHIPS),
     _W("ffn_conv_w", "tiny", 1, 6, F_FF // N_CHIPS)),
    (_W("sc_w_out", "row", 1, D // N_CHIPS, D),),
    (_W("ffn_w_up0", "col", 1, D, 2 * F_FF // N_CHIPS),),
    (_W("ffn_w_down0", "row", 1, F_FF // N_CHIPS, D),),
    (_W("w_kv", "row", 1, D // N_CHIPS, KVP), _W("w_ukv", "col", 2, KV_LORA, N_HEADS * QK_NOPE // N_CHIPS),
     _W("w_dq", "row", 1, D // N_CHIPS, Q_LORA),
     _W("w_uq", "col", 1, Q_LORA, N_HEADS * HEAD_PAD // N_CHIPS),
     _W("w_o", "row", 1, N_HEADS * V_HEAD // N_CHIPS, D)),
    (_W("ffn_w_up1", "col", 1, D, 2 * F_FF // N_CHIPS), _W("ffn_w_down1", "row", 1, F_FF // N_CHIPS, D)),
)


def _cp(*sem):
    return pltpu.CompilerParams(dimension_semantics=sem, vmem_limit_bytes=V7X_VMEM_LIMIT)


_ORDER = [None]


def _tc_call(body, *, name, out_shape, in_specs=None, out_specs=None, grid=(), scratch_shapes=(), prefetch=0,
             input_output_aliases=None, compiler_params=None):
    def run(*args):
        specs = [pl.BlockSpec(memory_space=pltpu.VMEM)] * (len(args) - prefetch) if in_specs is None else list(in_specs)
        inner, dep = body, _ORDER[0]
        if dep is not None:
            unread = prefetch + len(specs)
            specs, args = specs + [ANY], (*args, dep)

            def inner(*refs):
                return body(*refs[:unread], *refs[unread + 1:])

        kwargs = dict(name=name, out_shape=out_shape, input_output_aliases=input_output_aliases or {},
                      compiler_params=compiler_params)
        if prefetch:
            kwargs["grid_spec"] = pltpu.PrefetchScalarGridSpec(
                num_scalar_prefetch=prefetch, grid=grid, in_specs=specs, out_specs=out_specs,
                scratch_shapes=scratch_shapes)
        else:
            kwargs.update(grid=grid, in_specs=specs, scratch_shapes=scratch_shapes)
            if out_specs is not None:
                kwargs["out_specs"] = out_specs
        out = pl.pallas_call(inner, **kwargs)(*args)
        _ORDER[0] = out[0] if isinstance(out, (list, tuple)) else out
        return out

    return run


def _tile(n, cands):
    for c in cands:
        if n % c == 0:
            return c
    raise ValueError(f"no tile for {n}")


NN_DIMS = (((1,), (0,)), ((), ()))
NT_DIMS = (((1,), (1,)), ((), ()))
TN_DIMS = (((0,), (0,)), ((), ()))
M_TILES = (1024, 512, 384, 256, 128)
N_TILES = (1408, 1024, 768, 512, 384, 256, 128)
MM_BLOCK_BYTES = 36 * 1024 * 1024


def _fit(m, n, block_bytes, m_tiles=M_TILES, n_tiles=N_TILES):
    for tm in [c for c in m_tiles if m % c == 0]:
        for tn in [c for c in n_tiles if n % c == 0]:
            if 2 * block_bytes(tm, tn) + 4 * tm * tn <= MM_BLOCK_BYTES:
                return tm, tn
    raise ValueError(f"no tiles for {m} x {n}")


def _size(x):
    return x.dtype.itemsize


def _mm(name, a, b, dims, grid, a_spec, b_spec, o_spec, o_sds, add=None, red=None, acc_shape=None):
    n_red = None if red is None else grid[red]

    def body(*refs):
        a_ref, b_ref = refs[0], refs[1]
        add_ref = refs[2] if add is not None else None
        o_ref = refs[3] if add is not None else refs[2]
        part = lax.dot_general(a_ref[...].astype(BF16), b_ref[...].astype(BF16), dims, preferred_element_type=F32)
        if red is None:
            if add is not None:
                part = part + add_ref[...]
            o_ref[...] = part.astype(o_ref.dtype)
            return
        acc_ref = refs[-1]
        r = pl.program_id(red)

        @pl.when(r == 0)
        def _():
            acc_ref[...] = part

        @pl.when(r > 0)
        def _():
            acc_ref[...] += part

        @pl.when(r == n_red - 1)
        def _():
            o_ref[...] = acc_ref[...].astype(o_ref.dtype)

    sem = tuple("arbitrary" if ax == red else "parallel" for ax in range(len(grid)))
    in_specs = [a_spec, b_spec] + ([o_spec] if add is not None else [])
    args = (a, b) + ((add,) if add is not None else ())
    return _tc_call(
        body, name=name, grid=grid, in_specs=in_specs, out_specs=o_spec, out_shape=o_sds,
        scratch_shapes=[] if red is None else [pltpu.VMEM(acc_shape, F32)], compiler_params=_cp(*sem),
    )(*args)


def _nn(name, a, b, out_dtype, add=None, lead=None):
    (m, k), n = a.shape, b.shape[-1]
    osz = jnp.dtype(out_dtype).itemsize + (4 if add is not None else 0)
    tm, tn = _fit(m, n, lambda tm, tn: tm * k * _size(a) + k * tn * _size(b) + tm * tn * osz)
    if lead is None:
        b_spec = pl.BlockSpec((k, tn), lambda i, j: (0, j))
    else:
        b_spec = pl.BlockSpec((None, k, tn), lambda i, j: (lead, 0, j))
    return _mm(name, a, b, NN_DIMS, (m // tm, n // tn), pl.BlockSpec((tm, k), lambda i, j: (i, 0)), b_spec,
               pl.BlockSpec((tm, tn), lambda i, j: (i, j)), jax.ShapeDtypeStruct((m, n), out_dtype), add=add)


def _nn_parts(name, a, b, parts, out_dtype, lead=None, stacked=False):
    m, k = a.shape
    c = b.shape[-1] if stacked else b.shape[-1] // parts
    osz = jnp.dtype(out_dtype).itemsize
    tm, tn = _fit(m, c, lambda tm, tn: tm * k * _size(a) + k * tn * _size(b) + tm * tn * osz)
    nb = c // tn
    if stacked:
        b_spec = pl.BlockSpec((None, k, tn), lambda i, p, j: (p, 0, j))
    elif lead is None:
        b_spec = pl.BlockSpec((k, tn), lambda i, p, j: (0, p * nb + j))
    else:
        b_spec = pl.BlockSpec((None, k, tn), lambda i, p, j: (lead, 0, p * nb + j))
    return _mm(name, a, b, NN_DIMS, (m // tm, parts, nb), pl.BlockSpec((tm, k), lambda i, p, j: (i, 0)), b_spec,
               pl.BlockSpec((None, tm, tn), lambda i, p, j: (p, i, j)), jax.ShapeDtypeStruct((parts, m, c), out_dtype))


def _nt(name, a, b, out_dtype, lead=None):
    (m, k), n = a.shape, b.shape[-2]
    osz = jnp.dtype(out_dtype).itemsize
    tm, tn = _fit(m, n, lambda tm, tn: tm * k * _size(a) + tn * k * _size(b) + tm * tn * osz)
    if lead is None:
        b_spec = pl.BlockSpec((tn, k), lambda i, j: (j, 0))
    else:
        b_spec = pl.BlockSpec((None, tn, k), lambda i, j: (lead, j, 0))
    return _mm(name, a, b, NT_DIMS, (m // tm, n // tn), pl.BlockSpec((tm, k), lambda i, j: (i, 0)), b_spec,
               pl.BlockSpec((tm, tn), lambda i, j: (i, j)), jax.ShapeDtypeStruct((m, n), out_dtype))


def _nt_parts(name, a, b, out_dtype, lead=None, stacked=False):
    parts, m, c = a.shape
    n = b.shape[-2]
    osz = jnp.dtype(out_dtype).itemsize + 2
    tm, tn = _fit(m, n, lambda tm, tn: tm * c * _size(a) + tn * c * _size(b) + tm * tn * osz)
    if stacked:
        b_spec = pl.BlockSpec((None, tn, c), lambda i, j, p: (p, j, 0))
    elif lead is None:
        b_spec = pl.BlockSpec((tn, c), lambda i, j, p: (j, p))
    else:
        b_spec = pl.BlockSpec((None, tn, c), lambda i, j, p: (lead, j, p))
    return _mm(name, a, b, NT_DIMS, (m // tm, n // tn, parts), pl.BlockSpec((None, tm, c), lambda i, j, p: (p, i, 0)),
               b_spec, pl.BlockSpec((tm, tn), lambda i, j, p: (i, j)), jax.ShapeDtypeStruct((m, n), out_dtype),
               red=2, acc_shape=(tm, tn))


def _tn(name, a, b, out_dtype):
    (k, m), n = a.shape, b.shape[1]
    osz = jnp.dtype(out_dtype).itemsize
    tm, tn = _fit(m, n, lambda tm, tn: k * tm * _size(a) + k * tn * _size(b) + tm * tn * osz,
                  m_tiles=(512, 384, 256, 128), n_tiles=(n,) + N_TILES)
    return _mm(name, a, b, TN_DIMS, (m // tm, n // tn), pl.BlockSpec((k, tm), lambda i, j: (0, i)),
               pl.BlockSpec((k, tn), lambda i, j: (0, j)), pl.BlockSpec((tm, tn), lambda i, j: (i, j)),
               jax.ShapeDtypeStruct((m, n), out_dtype))


def _dw_sc_in(hn, dz):
    t, tn, tm = hn.shape[0], TC, 512
    per_part, per_chip = D // tn, 3 * D // N_CHIPS // tn
    return _mm("sc_in_dw", hn, dz, TN_DIMS, (D // tm, 3 * D // tn), pl.BlockSpec((t, tm), lambda i, j: (0, i)),
               pl.BlockSpec((None, t, tn), lambda i, j: (j // per_part, 0, j % per_part)),
               pl.BlockSpec((None, tm, tn), lambda i, j: (j // per_chip, i, j % per_chip)),
               jax.ShapeDtypeStruct((N_CHIPS, D, 3 * D // N_CHIPS), BF16))


def _dw_ffn_up(name, hf, dup):
    t, tm, ns = hf.shape[0], 512, 2 * F_FF // N_CHIPS
    return _mm(name, hf, dup, TN_DIMS, (N_CHIPS, D // tm), pl.BlockSpec((t, tm), lambda s, i: (0, i)),
               pl.BlockSpec((None, t, ns), lambda s, i: (s // 2, 0, s % 2)),
               pl.BlockSpec((None, tm, ns), lambda s, i: (s, i, 0)), jax.ShapeDtypeStruct((N_CHIPS, D, ns), BF16))


def _dw_ukv(ckv, dknv):
    t, ns = ckv.shape[0], N_HEADS * QK_NOPE // N_CHIPS
    return _mm("kv_up_dw", ckv, dknv, TN_DIMS, (2, N_CHIPS), pl.BlockSpec((t, KV_LORA), lambda p, s: (0, 0)),
               pl.BlockSpec((None, t, ns), lambda p, s: (p, 0, s)),
               pl.BlockSpec((None, None, KV_LORA, ns), lambda p, s: (s, p, 0, 0)),
               jax.ShapeDtypeStruct((N_CHIPS, 2, KV_LORA, ns), BF16))


def _rms_fwd(x, g, name):
    t, d = x.shape
    tr = 512

    def body(x_ref, g_ref, o_ref):
        xv = x_ref[...]
        r = lax.rsqrt(jnp.mean(xv * xv, axis=1, keepdims=True) + EPS)
        o_ref[...] = (xv * r * g_ref[...]).astype(o_ref.dtype)

    row = pl.BlockSpec((tr, d), lambda i: (i, 0))
    return _tc_call(
        body, name=name, grid=(t // tr,), in_specs=[row, pl.BlockSpec((1, d), lambda i: (0, 0))],
        out_specs=row, out_shape=jax.ShapeDtypeStruct((t, d), BF16), compiler_params=_cp("parallel"),
    )(x, g)


def _rms_bwd_math(xv, g, dy):
    r = lax.rsqrt(jnp.mean(xv * xv, axis=1, keepdims=True) + EPS)
    xh = xv * r
    gy = dy * g
    dx = r * (gy - xh * jnp.mean(gy * xh, axis=1, keepdims=True))
    dg = jnp.sum(dy * xh, axis=0, keepdims=True)
    return dx, dg


def _rms_bwd(x, g, dy, add, name, matmul_copy=False):
    t, d = x.shape
    tr = 512
    n_in = 3 + (add is not None)

    def body(*refs):
        x_ref, g_ref, dy_ref = refs[:3]
        dx_ref, dg_ref = refs[n_in], refs[-1]
        dx, dg = _rms_bwd_math(x_ref[...], g_ref[...], dy_ref[...].astype(F32))
        if add is not None:
            dx = dx + refs[3][...]
        dx_ref[...] = dx
        if matmul_copy:
            refs[n_in + 1][...] = dx.astype(BF16)

        @pl.when(pl.program_id(0) == 0)
        def _():
            dg_ref[...] = jnp.zeros_like(dg_ref)

        dg_ref[...] += dg

    row = pl.BlockSpec((tr, d), lambda i: (i, 0))
    vec = pl.BlockSpec((1, d), lambda i: (0, 0))
    in_specs = [row, vec, row] + ([row] if add is not None else [])
    args = (x, g, dy) + ((add,) if add is not None else ())
    copies = [jax.ShapeDtypeStruct((t, d), BF16)] if matmul_copy else []
    return _tc_call(
        body, name=name, grid=(t // tr,), in_specs=in_specs, out_specs=[row] * (1 + len(copies)) + [vec],
        out_shape=[jax.ShapeDtypeStruct((t, d), F32)] + copies + [jax.ShapeDtypeStruct((1, d), F32)],
        compiler_params=_cp("arbitrary"),
    )(*args)


def _loss_head(h, g, tgt):
    t, d = h.shape
    tr = 512

    def body(h_ref, g_ref, t_ref, loss_ref, dh_ref, dhb_ref, dg_ref):
        xv = h_ref[...]
        gv = g_ref[...]
        r = lax.rsqrt(jnp.mean(xv * xv, axis=1, keepdims=True) + EPS)
        err = xv * r * gv - t_ref[...]
        part = 0.5 * jnp.sum(jnp.mean(err * err, axis=1, keepdims=True), axis=0, keepdims=True)
        dx, dg = _rms_bwd_math(xv, gv, err * (1.0 / d))
        dh_ref[...] = dx
        dhb_ref[...] = dx.astype(BF16)

        @pl.when(pl.program_id(0) == 0)
        def _():
            dg_ref[...] = jnp.zeros_like(dg_ref)
            loss_ref[...] = jnp.zeros_like(loss_ref)

        dg_ref[...] += dg
        loss_ref[...] += jnp.broadcast_to(part, loss_ref.shape)

    row = pl.BlockSpec((tr, d), lambda i: (i, 0))
    vec = pl.BlockSpec((1, d), lambda i: (0, 0))
    lspec = pl.BlockSpec((1, LANES), lambda i: (0, 0))
    return _tc_call(
        body, name="loss_head", grid=(t // tr,), in_specs=[row, vec, row], out_specs=[lspec, row, row, vec],
        out_shape=[jax.ShapeDtypeStruct((1, LANES), F32), jax.ShapeDtypeStruct((t, d), F32),
                   jax.ShapeDtypeStruct((t, d), BF16), jax.ShapeDtypeStruct((1, d), F32)],
        compiler_params=_cp("arbitrary"),
    )(h, g, tgt)


def _rot_half(x):
    lane = lax.broadcasted_iota(jnp.int32, x.shape, 1)
    return jnp.where((lane % QK_ROPE) < QK_ROPE // 2, -pltpu.roll(x, LANES - 32, axis=1),
                     pltpu.roll(x, 32, axis=1))


def _rope_fwd_math(x, cos, sin):
    return x * cos + _rot_half(x) * sin


def _rope_bwd_math(dy, cos, sin):
    return dy * cos - _rot_half(dy * sin)


def _q_rope_fwd(qpre, cos, sin):
    t, w = qpre.shape
    tr = 256

    def body(q_ref, c_ref, s_ref, o_ref):
        cv, sv = c_ref[...], s_ref[...]
        for h in range(N_HEADS):
            lo = h * HEAD_PAD
            o_ref[:, lo:lo + QK_NOPE] = q_ref[:, lo:lo + QK_NOPE].astype(BF16)
            o_ref[:, lo + QK_NOPE:lo + HEAD_PAD] = _rope_fwd_math(
                q_ref[:, lo + QK_NOPE:lo + HEAD_PAD], cv, sv).astype(BF16)

    row = pl.BlockSpec((tr, w), lambda i: (i, 0))
    tab = pl.BlockSpec((tr, LANES), lambda i: (i, 0))
    return _tc_call(
        body, name="q_rope_fwd", grid=(t // tr,), in_specs=[row, tab, tab], out_specs=row,
        out_shape=jax.ShapeDtypeStruct((t, w), BF16), compiler_params=_cp("parallel"),
    )(qpre, cos, sin)


def _kv_elem_fwd(kvpre, g, cos, sin):
    t = kvpre.shape[0]
    tr = 512

    def body(p_ref, g_ref, c_ref, s_ref, ckv_ref, kr_ref):
        lat = p_ref[:, :KV_LORA]
        r = lax.rsqrt(jnp.mean(lat * lat, axis=1, keepdims=True) + EPS)
        ckv_ref[...] = (lat * r * g_ref[...]).astype(BF16)
        kr_ref[...] = _rope_fwd_math(p_ref[:, KV_LORA:], c_ref[...], s_ref[...]).astype(BF16)

    tab = pl.BlockSpec((tr, LANES), lambda i: (i, 0))
    return _tc_call(
        body, name="kv_elem_fwd", grid=(t // tr,),
        in_specs=[pl.BlockSpec((tr, KVP), lambda i: (i, 0)), pl.BlockSpec((1, KV_LORA), lambda i: (0, 0)), tab, tab],
        out_specs=[pl.BlockSpec((tr, KV_LORA), lambda i: (i, 0)), tab],
        out_shape=[jax.ShapeDtypeStruct((t, KV_LORA), BF16), jax.ShapeDtypeStruct((t, LANES), BF16)],
        compiler_params=_cp("parallel"),
    )(kvpre, g, cos, sin)


def _kv_elem_bwd(kvpre, g, dckv, dkr, cos, sin):
    t = kvpre.shape[0]
    tr = 512

    def body(p_ref, g_ref, dc_ref, dk_ref, c_ref, s_ref, dp_ref, dg_ref):
        dlat, dg = _rms_bwd_math(p_ref[:, :KV_LORA], g_ref[...], dc_ref[...])
        dp_ref[:, :KV_LORA] = dlat.astype(BF16)
        dp_ref[:, KV_LORA:] = _rope_bwd_math(dk_ref[...], c_ref[...], s_ref[...]).astype(BF16)

        @pl.when(pl.program_id(0) == 0)
        def _():
            dg_ref[...] = jnp.zeros_like(dg_ref)

        dg_ref[...] += dg

    tab = pl.BlockSpec((tr, LANES), lambda i: (i, 0))
    pre = pl.BlockSpec((tr, KVP), lambda i: (i, 0))
    vec = pl.BlockSpec((1, KV_LORA), lambda i: (0, 0))
    return _tc_call(
        body, name="kv_elem_bwd", grid=(t // tr,),
        in_specs=[pre, vec, pl.BlockSpec((tr, KV_LORA), lambda i: (i, 0)), tab, tab, tab],
        out_specs=[pre, vec],
        out_shape=[jax.ShapeDtypeStruct((t, KVP), BF16), jax.ShapeDtypeStruct((1, KV_LORA), F32)],
        compiler_params=_cp("arbitrary"),
    )(kvpre, g, dckv, dkr, cos, sin)


ROW_CHUNK = 64
HALO = 16
WIN = ROW_CHUNK + 16
LANE_HALVES = (slice(0, LANES), slice(LANES, TC))


def _stage(s_ref, p, src):
    t = src.shape[0]
    s_ref[p, :HALO] = jnp.zeros((HALO, TC), BF16)
    s_ref[p, HALO:HALO + t] = src
    s_ref[p, HALO + t:] = jnp.zeros((HALO, TC), BF16)


def _window(s_ref, p, i, lanes):
    base = pl.multiple_of(i * ROW_CHUNK, ROW_CHUNK)
    return s_ref[p, pl.ds(base, ROW_CHUNK + 2 * HALO), lanes].astype(F32)[8:8 + WIN]


def _valid(x):
    return x[8:8 + ROW_CHUNK]


def _prev(x, k):
    return pltpu.roll(x, k, axis=0)


def _next(x, k):
    return pltpu.roll(x, WIN - k, axis=0)


def _taps(w_ref, lanes):
    return w_ref[0:1, lanes], w_ref[1:2, lanes], w_ref[2:3, lanes]


def _fold8(x):
    return jnp.sum(x.reshape(ROW_CHUNK // 8, 8, x.shape[-1]), axis=0)


def _store_rows(ref, idx, i, lanes, x):
    rows = pl.ds(pl.multiple_of(i * ROW_CHUNK, ROW_CHUNK), ROW_CHUNK)
    ref[(*idx, rows, lanes)] = x.astype(ref.dtype)


def _for_chunks(t, chunk):
    def step(i, carry):
        for lanes in LANE_HALVES:
            chunk(i, lanes)
        return carry

    lax.fori_loop(0, t // ROW_CHUNK, step, 0)


def _write_col_sums(acc_ref, outs):
    for k, (ref, row) in enumerate(outs):
        ref[row:row + 1, :] = jnp.sum(acc_ref[k], axis=0, keepdims=True)


def _col(parts, t):
    if parts is None:
        return pl.BlockSpec((t, TC), lambda j: (0, j))
    return pl.BlockSpec((parts, t, TC), lambda j: (0, 0, j))


def _staging(parts, t):
    return pltpu.VMEM((parts, t + 2 * HALO, TC), BF16)


def _scmix_fwd(z, w):
    t = z.shape[1]

    def body(z_ref, w_ref, m_ref, s_ref):
        for p in range(3):
            _stage(s_ref, p, z_ref[p])

        def chunk(i, lanes):
            w0, w1, w2 = _taps(w_ref, lanes)
            b, c, u = (_window(s_ref, p, i, lanes) for p in range(3))
            cu = c * u
            cv = _prev(cu, 2) * w0 + _prev(cu, 1) * w1 + cu * w2
            _store_rows(m_ref, (), i, lanes, _valid(b * cv))

        _for_chunks(t, chunk)

    return _tc_call(
        body, name="scmix_fwd", grid=(D // TC,), in_specs=[_col(3, t), pl.BlockSpec((3, TC), lambda j: (0, j))],
        out_specs=_col(None, t), out_shape=jax.ShapeDtypeStruct((t, D), BF16), scratch_shapes=[_staging(3, t)],
        compiler_params=_cp("parallel"),
    )(z, w)


def _scmix_bwd(z, w, dm):
    t = z.shape[1]

    def body(z_ref, w_ref, dm_ref, dz_ref, dw_ref, s_ref, acc_ref):
        for p in range(3):
            _stage(s_ref, p, z_ref[p])
        _stage(s_ref, 3, dm_ref[...])
        acc_ref[...] = jnp.zeros_like(acc_ref)

        def chunk(i, lanes):
            w0, w1, w2 = _taps(w_ref, lanes)
            b, c, u, dm = (_window(s_ref, p, i, lanes) for p in range(4))
            cu = c * u
            cu1, cu2 = _prev(cu, 1), _prev(cu, 2)
            _store_rows(dz_ref, (0,), i, lanes, _valid(dm * (cu2 * w0 + cu1 * w1 + cu * w2)))
            dcv = dm * b
            dcu = dcv * w2 + _next(dcv, 1) * w1 + _next(dcv, 2) * w0
            _store_rows(dz_ref, (1,), i, lanes, _valid(dcu * u))
            _store_rows(dz_ref, (2,), i, lanes, _valid(dcu * c))
            for k, shifted in enumerate((cu2, cu1, cu)):
                acc_ref[k, :, lanes] += _fold8(_valid(dcv * shifted))

        _for_chunks(t, chunk)
        _write_col_sums(acc_ref, [(dw_ref, 0), (dw_ref, 1), (dw_ref, 2)])

    wspec = pl.BlockSpec((3, TC), lambda j: (0, j))
    return _tc_call(
        body, name="scmix_bwd", grid=(D // TC,), in_specs=[_col(3, t), wspec, _col(None, t)],
        out_specs=[_col(3, t), wspec],
        out_shape=[jax.ShapeDtypeStruct((3, t, D), BF16), jax.ShapeDtypeStruct((3, D), F32)],
        scratch_shapes=[_staging(4, t), pltpu.VMEM((3, 8, TC), F32)], compiler_params=_cp("parallel"),
    )(z, w, dm)


def _gate_fwd(up, w, bias, name):
    t = up.shape[1]

    def body(u_ref, w_ref, b_ref, a_ref, s_ref):
        for p in range(2):
            _stage(s_ref, p, u_ref[p])

        def chunk(i, lanes):
            w0, w1, w2 = _taps(w_ref, lanes)
            g, v = (_window(s_ref, p, i, lanes) for p in range(2))
            gc = _prev(g, 2) * w0 + _prev(g, 1) * w1 + g * w2 + b_ref[:, lanes]
            _store_rows(a_ref, (), i, lanes, _valid(gc * jax.nn.sigmoid(gc) * v))

        _for_chunks(t, chunk)

    return _tc_call(
        body, name=name, grid=(F_FF // TC,),
        in_specs=[_col(2, t), pl.BlockSpec((3, TC), lambda j: (0, j)), pl.BlockSpec((1, TC), lambda j: (0, j))],
        out_specs=_col(None, t), out_shape=jax.ShapeDtypeStruct((t, F_FF), BF16), scratch_shapes=[_staging(2, t)],
        compiler_params=_cp("parallel"),
    )(up, w, bias)


def _gate_bwd(up, w, bias, da, name):
    t = up.shape[1]

    def body(u_ref, w_ref, b_ref, da_ref, du_ref, dw_ref, db_ref, s_ref, acc_ref):
        for p in range(2):
            _stage(s_ref, p, u_ref[p])
        _stage(s_ref, 2, da_ref[...])
        acc_ref[...] = jnp.zeros_like(acc_ref)

        def chunk(i, lanes):
            w0, w1, w2 = _taps(w_ref, lanes)
            g, v, da = (_window(s_ref, p, i, lanes) for p in range(3))
            g1, g2 = _prev(g, 1), _prev(g, 2)
            gc = g2 * w0 + g1 * w1 + g * w2 + b_ref[:, lanes]
            sg = jax.nn.sigmoid(gc)
            _store_rows(du_ref, (1,), i, lanes, _valid(da * (gc * sg)))
            dgc = da * v * (sg * (1.0 + gc * (1.0 - sg)))
            _store_rows(du_ref, (0,), i, lanes, _valid(dgc * w2 + _next(dgc, 1) * w1 + _next(dgc, 2) * w0))
            for k, shifted in enumerate((g2, g1, g)):
                acc_ref[k, :, lanes] += _fold8(_valid(dgc * shifted))
            acc_ref[3, :, lanes] += _fold8(_valid(dgc))

        _for_chunks(t, chunk)
        _write_col_sums(acc_ref, [(dw_ref, 0), (dw_ref, 1), (dw_ref, 2), (db_ref, 0)])

    wspec = pl.BlockSpec((3, TC), lambda j: (0, j))
    bspec = pl.BlockSpec((1, TC), lambda j: (0, j))
    return _tc_call(
        body, name=name, grid=(F_FF // TC,), in_specs=[_col(2, t), wspec, bspec, _col(None, t)],
        out_specs=[_col(2, t), wspec, bspec],
        out_shape=[jax.ShapeDtypeStruct((2, t, F_FF), BF16), jax.ShapeDtypeStruct((3, F_FF), F32),
                   jax.ShapeDtypeStruct((1, F_FF), F32)],
        scratch_shapes=[_staging(3, t), pltpu.VMEM((4, 8, TC), F32)], compiler_params=_cp("parallel"),
    )(up, w, bias, da)


ATT_TQ = 256
ATT_SCALE = (QK_NOPE + QK_ROPE) ** -0.5


def _key_ranges(lvl):
    lo = lvl * ATT_TQ
    return ([(0, lo, False)] if lvl else []) + [(lo, lo + ATT_TQ, True)]


def _fill_keys(k_ref, kn_ref, kr_ref):
    @pl.when(pl.program_id(1) == 0)
    def _():
        k_ref[:, :QK_NOPE] = kn_ref[...]
        k_ref[:, QK_NOPE:] = kr_ref[...]


def _attn_probs(q, k_ref, lvl):
    scores = []
    for lo, hi, diagonal in _key_ranges(lvl):
        s = lax.dot_general(q, k_ref[lo:hi, :], NT_DIMS, preferred_element_type=F32) * ATT_SCALE
        if diagonal:
            row = lax.broadcasted_iota(jnp.int32, s.shape, 0)
            col = lax.broadcasted_iota(jnp.int32, s.shape, 1)
            seen = lax.shift_right_logical(col, CHUNK_SHIFT) <= lax.shift_right_logical(row, CHUNK_SHIFT)
            s = jnp.where(seen, s, NEG_INF)
        scores.append(s)
    m = jnp.max(scores[0], axis=1, keepdims=True)
    for s in scores[1:]:
        m = jnp.maximum(m, jnp.max(s, axis=1, keepdims=True))
    ps = [jnp.exp(s - m) for s in scores]
    total = jnp.sum(ps[0], axis=1, keepdims=True)
    for p in ps[1:]:
        total = total + jnp.sum(p, axis=1, keepdims=True)
    inv = 1.0 / total
    return [p * inv for p in ps]


def _per_query_block(qi, n_blocks, branch):
    for lvl in range(n_blocks):
        pl.when(qi == lvl)(lambda lvl=lvl: branch(lvl))


def _attn_specs(t):
    q = pl.BlockSpec((ATT_TQ, HEAD_PAD), lambda h, i: (i, h))
    kn = pl.BlockSpec((None, t, QK_NOPE), lambda h, i: (0, 0, h))
    kr = pl.BlockSpec((t, LANES), lambda h, i: (0, 0))
    v = pl.BlockSpec((None, t, V_HEAD), lambda h, i: (1, 0, h))
    o = pl.BlockSpec((ATT_TQ, V_HEAD), lambda h, i: (i, h))
    return q, kn, kr, v, o


def _attn_fwd(q, knv, kr):
    t = q.shape[0]

    def body(q_ref, kn_ref, kr_ref, v_ref, o_ref, k_ref):
        _fill_keys(k_ref, kn_ref, kr_ref)

        def branch(lvl):
            ps = _attn_probs(q_ref[...], k_ref, lvl)
            o = None
            for p, (lo, hi, _) in zip(ps, _key_ranges(lvl)):
                part = jnp.dot(p.astype(BF16), v_ref[lo:hi, :], preferred_element_type=F32)
                o = part if o is None else o + part
            o_ref[...] = o.astype(BF16)

        _per_query_block(pl.program_id(1), t // ATT_TQ, branch)

    qs, kns, krs, vs, os_ = _attn_specs(t)
    return _tc_call(
        body, name="attn_fwd", grid=(N_HEADS, t // ATT_TQ), in_specs=[qs, kns, krs, vs], out_specs=os_,
        out_shape=jax.ShapeDtypeStruct((t, N_HEADS * V_HEAD), BF16), scratch_shapes=[pltpu.VMEM((t, HEAD_PAD), BF16)],
        compiler_params=_cp("parallel", "arbitrary"),
    )(q, knv, kr, knv)


def _attn_bwd(q, knv, kr, do, cos, sin):
    t = q.shape[0]

    def body(q_ref, kn_ref, kr_ref, v_ref, do_ref, c_ref, s_ref, dq_ref, dknv_ref, dkr_ref, k_ref, dk_ref):
        h, qi = pl.program_id(0), pl.program_id(1)
        _fill_keys(k_ref, kn_ref, kr_ref)

        @pl.when(qi == 0)
        def _():
            dknv_ref[1] = jnp.zeros((t, V_HEAD), F32)
            dk_ref[...] = jnp.zeros_like(dk_ref)

        @pl.when((qi == 0) & (h == 0))
        def _():
            dkr_ref[...] = jnp.zeros_like(dkr_ref)

        def branch(lvl):
            qv, dov = q_ref[...], do_ref[...]
            ranges = _key_ranges(lvl)
            ps = _attn_probs(qv, k_ref, lvl)
            dps = [lax.dot_general(dov, v_ref[lo:hi, :], NT_DIMS, preferred_element_type=F32) for lo, hi, _ in ranges]
            di = None
            for p, dp in zip(ps, dps):
                part = jnp.sum(p * dp, axis=1, keepdims=True)
                di = part if di is None else di + part
            dq = None
            for p, dp, (lo, hi, _) in zip(ps, dps, ranges):
                ds = (p * (dp - di) * ATT_SCALE).astype(BF16)
                part = jnp.dot(ds, k_ref[lo:hi, :], preferred_element_type=F32)
                dq = part if dq is None else dq + part
                dk_ref[lo:hi, :] += lax.dot_general(ds, qv, TN_DIMS, preferred_element_type=F32)
                dknv_ref[1, lo:hi, :] += lax.dot_general(p.astype(BF16), dov, TN_DIMS, preferred_element_type=F32)
            dq_ref[:, :QK_NOPE] = dq[:, :QK_NOPE].astype(BF16)
            dq_ref[:, QK_NOPE:] = _rope_bwd_math(dq[:, QK_NOPE:], c_ref[...], s_ref[...]).astype(BF16)

        _per_query_block(qi, t // ATT_TQ, branch)

        @pl.when(qi == t // ATT_TQ - 1)
        def _():
            dknv_ref[0] = dk_ref[:, :QK_NOPE]
            dkr_ref[...] += dk_ref[:, QK_NOPE:]

    qs, kns, krs, vs, os_ = _attn_specs(t)
    tab = pl.BlockSpec((ATT_TQ, LANES), lambda h, i: (i, 0))
    return _tc_call(
        body, name="attn_bwd", grid=(N_HEADS, t // ATT_TQ), in_specs=[qs, kns, krs, vs, os_, tab, tab],
        out_specs=[qs, pl.BlockSpec((2, t, QK_NOPE), lambda h, i: (0, 0, h)), krs],
        out_shape=[jax.ShapeDtypeStruct((t, N_HEADS * HEAD_PAD), BF16),
                   jax.ShapeDtypeStruct((2, t, N_HEADS * QK_NOPE), F32), jax.ShapeDtypeStruct((t, LANES), F32)],
        scratch_shapes=[pltpu.VMEM((t, HEAD_PAD), BF16), pltpu.VMEM((t, HEAD_PAD), F32)],
        compiler_params=_cp("arbitrary", "arbitrary"),
    )(q, knv, kr, knv, do, cos, sin)


def _adam_math(w, g, m, v):
    nm = ADAM_B1 * m + (1.0 - ADAM_B1) * g
    nv = ADAM_B2 * v + (1.0 - ADAM_B2) * (g * g)
    m_hat = nm / (1.0 - ADAM_B1 ** ADAM_STEP)
    v_hat = nv / (1.0 - ADAM_B2 ** ADAM_STEP)
    return -ADAM_LR * (m_hat / (jnp.sqrt(v_hat) + ADAM_EPS) + ADAM_WD * w), nm, nv


def _adamw_small(w, g, m, v):
    def body(w_ref, g_ref, m_ref, v_ref, d_ref, nm_ref, nv_ref):
        d_ref[...], nm_ref[...], nv_ref[...] = _adam_math(w_ref[...], g_ref[...], m_ref[...], v_ref[...])

    shp = jax.ShapeDtypeStruct(w.shape, F32)
    return _tc_call(body, name="adamw_small", out_shape=[shp] * 3)(w, g, m, v)


ADAM_SPLIT = 4


def _adamw_shards(ids, items, name):
    n = len(items)

    def body(ids_ref, *refs):
        outs = refs[len(refs) - 4 * n:]
        mine = pl.program_id(0) == ids_ref[0]
        for i in range(n):
            w_ref, m_ref, v_ref, gm_ref, gs_ref = refs[5 * i:5 * i + 5]
            g_ref, d_ref, nm_ref, nv_ref = outs[4 * i:4 * i + 4]

            @pl.when(mine)
            def _(g_ref=g_ref, gm_ref=gm_ref):
                g_ref[...] = gm_ref[...]

            @pl.when(jnp.logical_not(mine))
            def _(g_ref=g_ref, gs_ref=gs_ref):
                g_ref[...] = gs_ref[...]

            d_ref[...], nm_ref[...], nv_ref[...] = _adam_math(w_ref[...], g_ref[...], m_ref[...], v_ref[...])

    in_specs, out_specs, out_shape, args, carried, aliases = [], [], [], [ids], [], {}
    for i, it in enumerate(items):
        w = it["w"]
        r, c = w.shape[-2:]
        tr = r // 2 // ADAM_SPLIT
        assert tr % 8 == 0, (name, w.shape)
        layer = it.get("layer")
        if layer is None:
            wspec = pl.BlockSpec((tr, c), lambda h, k, ids: (h * ADAM_SPLIT + k, 0))
        else:
            wspec = pl.BlockSpec((None, tr, c), lambda h, k, ids, layer=layer: (layer, h * ADAM_SPLIT + k, 0))
        gspec = pl.BlockSpec((tr, c), lambda h, k, ids: (k, 0))
        in_specs += [wspec] * 3 + [gspec] * 2
        args += [w, it["m"], it["v"], it["g_mine"], it["g_sib"]]
        out_specs += [wspec] * 4
        out_shape += [jax.ShapeDtypeStruct(w.shape, F32)] * 4
        if it.get("prev") is not None:
            for k, p in enumerate(it["prev"]):
                aliases[1 + 5 * n + len(carried)] = 4 * i + k
                carried.append(p)
    res = _tc_call(
        body, name=name, prefetch=1, grid=(2, ADAM_SPLIT), in_specs=in_specs + [ANY] * len(carried),
        out_specs=out_specs, out_shape=out_shape, input_output_aliases=aliases,
        compiler_params=_cp("parallel", "parallel"),
    )(*args, *carried)
    return [res[4 * i:4 * i + 4] for i in range(n)]


def _peer_chip(k_me, j):
    return k_me ^ jnp.where(j == 0, 2, jnp.where(j == 1, 1, 3))


def _pair_sums(ids, gs, ras, name):
    n = len(gs)

    def body(ids_ref, *refs):
        for i in range(n):
            g_ref, ra_ref, o_ref = refs[2 * i], refs[2 * i + 1], refs[2 * n + i]
            o_ref[...] = (g_ref[...].astype(F32) + ra_ref[...].astype(F32)).astype(BF16)

    in_specs, out_specs, out_shape = [], [], []
    for g in gs:
        half, c = g.shape[1] // 2, g.shape[2]
        in_specs += [pl.BlockSpec((None, half, c), lambda j, ids: (_peer_chip(ids[1], j), ids[0], 0)),
                     pl.BlockSpec((None, half, c), lambda j, ids: (_peer_chip(ids[1], j), 0, 0))]
        out_specs.append(pl.BlockSpec((None, half, c), lambda j, ids: (j, 0, 0)))
        out_shape.append(jax.ShapeDtypeStruct((3, half, c), BF16))
    return _tc_call(
        body, name=name, prefetch=1, grid=(3,), in_specs=in_specs, out_specs=out_specs, out_shape=out_shape,
        compiler_params=_cp("parallel"),
    )(ids, *[a for pair in zip(gs, ras) for a in pair])


def _chip_sums(ids, gs, ras, rbs, name):
    n = len(gs)

    def body(ids_ref, *refs):
        for i in range(n):
            g_ref, ra_ref, rb_ref, o_ref = refs[3 * i], refs[3 * i + 1], refs[3 * i + 2], refs[3 * n + i]
            acc = g_ref[...].astype(F32) + ra_ref[...].astype(F32)
            for j in range(3):
                acc = acc + rb_ref[j].astype(F32)
            o_ref[...] = acc

    in_specs, out_specs, out_shape = [], [], []
    for g in gs:
        half, c = g.shape[1] // 2, g.shape[2]
        in_specs += [pl.BlockSpec((None, half, c), lambda i, ids: (ids[1], ids[0], 0)),
                     pl.BlockSpec((None, half, c), lambda i, ids: (ids[1], 0, 0)),
                     pl.BlockSpec((3, half, c), lambda i, ids: (0, 0, 0))]
        out_specs.append(pl.BlockSpec((half, c), lambda i, ids: (0, 0)))
        out_shape.append(jax.ShapeDtypeStruct((half, c), F32))
    return _tc_call(
        body, name=name, prefetch=1, grid=(1,), in_specs=in_specs, out_specs=out_specs, out_shape=out_shape,
        compiler_params=_cp("arbitrary"),
    )(ids, *[a for trio in zip(gs, ras, rbs) for a in trio])


def _position():
    x, y, c = lax.axis_index("x"), lax.axis_index("y"), lax.axis_index("c")
    chips = [(1 - x, y), (x, 1 - y), (1 - x, 1 - y)]
    return x, y, c, chips


def _shard_half(ref, wm, h):
    if wm.kind == "tiny":
        return ref
    if wm.nl == 2:
        return ref.at[h]
    return ref.at[pl.ds(pl.multiple_of(h * (wm.k // 2), 16), wm.k // 2), :]


def _region(full, wm, s, h):
    if wm.kind == "tiny":
        return full.at[s]
    cols = pl.ds(pl.multiple_of(s * wm.n, LANES), wm.n) if wm.kind == "col" else slice(None)
    if wm.nl == 2:
        rows = pl.ds(pl.multiple_of(s * wm.k, 16), wm.k) if wm.kind == "row" else slice(None)
        return full.at[slice(None) if h is None else h, rows, cols]
    if wm.kind == "col":
        rows = slice(None) if h is None else pl.ds(pl.multiple_of(h * (wm.k // 2), 16), wm.k // 2)
    elif h is None:
        rows = pl.ds(pl.multiple_of(s * wm.k, 16), wm.k)
    else:
        rows = pl.ds(pl.multiple_of(s * wm.k + h * (wm.k // 2), 16), wm.k // 2)
    return full.at[rows, cols]


def _full_shape(wm):
    if wm.kind == "tiny":
        return (N_CHIPS, wm.k, wm.n)
    shape = (wm.k, N_CHIPS * wm.n) if wm.kind == "col" else (N_CHIPS * wm.k, wm.n)
    return shape if wm.nl == 1 else (wm.nl,) + shape


def _handshake(peers):
    barrier = pltpu.get_barrier_semaphore()
    for peer in peers:
        pl.semaphore_signal(barrier, inc=1, device_id=peer, device_id_type=MESH)
    pl.semaphore_wait(barrier, len(peers))


def _all_gather_group(gi, shards):
    wms = AG_GROUPS[gi]
    nw = len(wms)

    def body(*refs):
        sh, full = refs[:nw], refs[nw:2 * nw]
        ici_s, ici_r, pass_s, pass_r, own_s, own_r = refs[2 * nw:]
        x, y, c, chips = _position()
        me, sibling = 2 * x + y, (x, y, 1 - c)
        _handshake([(*chip, c) for chip in chips] + [sibling])

        def rcopy(src, dst, s_sem, r_sem, to):
            return pltpu.make_async_remote_copy(src_ref=src, dst_ref=dst, send_sem=s_sem, recv_sem=r_sem,
                                                device_id=to, device_id_type=MESH)

        started = []
        for i, wm in enumerate(wms):
            for j, chip in enumerate(chips):
                started.append(rcopy(_shard_half(sh[i], wm, c), _region(full[i], wm, me, c),
                                     ici_s.at[i, j], ici_r.at[i, j], (*chip, c)))
                started[-1].start()
            started.append(rcopy(sh[i], _region(full[i], wm, me, None), own_s.at[i], own_r.at[i], sibling))
            started[-1].start()
        for i, wm in enumerate(wms):
            for j, chip in enumerate(chips):
                got = _region(full[i], wm, 2 * chip[0] + chip[1], c)
                rcopy(got, got, ici_s.at[i, j], ici_r.at[i, j], sibling).wait_recv()
                if wm.kind != "tiny":
                    started.append(rcopy(got, got, pass_s.at[i, j], pass_r.at[i, j], sibling))
                    started[-1].start()
        for i, wm in enumerate(wms):
            mine = _region(full[i], wm, me, None)
            rcopy(mine, mine, own_s.at[i], own_r.at[i], sibling).wait_recv()
            for j, chip in enumerate(chips):
                if wm.kind != "tiny":
                    got = _region(full[i], wm, 2 * chip[0] + chip[1], 1 - c)
                    rcopy(got, got, pass_s.at[i, j], pass_r.at[i, j], sibling).wait_recv()
        for cp in started:
            cp.wait_send()

    return pl.kernel(
        body, out_type=[jax.ShapeDtypeStruct(_full_shape(wm), s.dtype) for wm, s in zip(wms, shards)],
        mesh=plsc.ScalarSubcoreMesh(axis_name="sequencer", num_cores=1), name=f"ag_group{gi}",
        scratch_types=[pltpu.SemaphoreType.DMA((nw, 3))] * 4 + [pltpu.SemaphoreType.DMA((nw,))] * 2,
        compiler_params=pltpu.CompilerParams(collective_id=gi),
    )(*shards)


def _sequencer_call(body, name, cid, out_types, scratch, args):
    return pl.kernel(
        body, out_type=out_types, mesh=plsc.ScalarSubcoreMesh(axis_name="sequencer", num_cores=1), name=name,
        scratch_types=scratch, compiler_params=pltpu.CompilerParams(collective_id=cid),
    )(*args)


def _pair_exchange(gs, tag, cid):
    n = len(gs)

    def body(*refs):
        g, out, send_sems, recv_sems = refs[:n], refs[n:2 * n], refs[2 * n], refs[2 * n + 1]
        x, y, c, _ = _position()
        _handshake([(x, y, 1 - c)])
        cps = []
        for i in range(n):
            half = g[i].shape[1] // 2
            cps.append(pltpu.make_async_remote_copy(
                src_ref=g[i].at[:, pl.ds(pl.multiple_of((1 - c) * half, 16), half), :], dst_ref=out[i],
                send_sem=send_sems.at[i], recv_sem=recv_sems.at[i], device_id=(x, y, 1 - c), device_id_type=MESH))
            cps[-1].start()
        for cp in cps:
            cp.wait()

    return _sequencer_call(
        body, f"rs_pair_exchange{tag}", cid,
        [jax.ShapeDtypeStruct((a.shape[0], a.shape[1] // 2, a.shape[2]), a.dtype) for a in gs],
        [pltpu.SemaphoreType.DMA((n,)), pltpu.SemaphoreType.DMA((n,))], gs)


def _chip_exchange(ss, tag, cid):
    n = len(ss)

    def body(*refs):
        s, out, send_sems, recv_sems = refs[:n], refs[n:2 * n], refs[2 * n], refs[2 * n + 1]
        x, y, c, chips = _position()
        _handshake([(*chip, c) for chip in chips])
        cps = []
        for i in range(n):
            for j, chip in enumerate(chips):
                cps.append(pltpu.make_async_remote_copy(
                    src_ref=s[i].at[j], dst_ref=out[i].at[j], send_sem=send_sems.at[i, j], recv_sem=recv_sems.at[i, j],
                    device_id=(*chip, c), device_id_type=MESH))
                cps[-1].start()
        for cp in cps:
            cp.wait()

    return _sequencer_call(
        body, f"rs_chip_exchange{tag}", cid, [jax.ShapeDtypeStruct(a.shape, a.dtype) for a in ss],
        [pltpu.SemaphoreType.DMA((n, 3)), pltpu.SemaphoreType.DMA((n, 3))], ss)


def _pair_swap(g8s, tag, cid):
    n = len(g8s)

    def body(*refs):
        g, out, send_sems, recv_sems = refs[:n], refs[n:2 * n], refs[2 * n], refs[2 * n + 1]
        x, y, c, _ = _position()
        _handshake([(x, y, 1 - c)])
        cps = []
        for i in range(n):
            cps.append(pltpu.make_async_remote_copy(
                src_ref=g[i], dst_ref=out[i], send_sem=send_sems.at[i], recv_sem=recv_sems.at[i],
                device_id=(x, y, 1 - c), device_id_type=MESH))
            cps[-1].start()
        for cp in cps:
            cp.wait()

    return _sequencer_call(
        body, f"rs_pair_swap{tag}", cid, [jax.ShapeDtypeStruct(a.shape, a.dtype) for a in g8s],
        [pltpu.SemaphoreType.DMA((n,)), pltpu.SemaphoreType.DMA((n,))], g8s)


def _all_reduce_small(vec, name):
    r, cols = vec.shape

    def body(v_ref, o_ref, gath, send_sems, recv_sems):
        x, y, c, _ = _position()
        me = 4 * x + 2 * y + c
        gath[me] = v_ref[...]
        cps = []
        for rel in range(1, N_DEV):
            peer = (x ^ (rel >> 2), y ^ ((rel >> 1) & 1), c ^ (rel & 1))
            cps.append(pltpu.make_async_remote_copy(
                src_ref=v_ref, dst_ref=gath.at[me], send_sem=send_sems.at[rel - 1], recv_sem=recv_sems.at[rel - 1],
                device_id=peer, device_id_type=MESH))
        for cp in cps:
            cp.start()
        for rel in range(1, N_DEV):
            pltpu.make_async_remote_copy(
                src_ref=v_ref, dst_ref=gath.at[me ^ rel], send_sem=send_sems.at[rel - 1],
                recv_sem=recv_sems.at[rel - 1], device_id=(x, y, c), device_id_type=MESH).wait_recv()
        for cp in cps:
            cp.wait_send()
        acc = gath[0]
        for d in range(1, N_DEV):
            acc = acc + gath[d]
        o_ref[...] = acc

    vm = pl.BlockSpec(memory_space=pltpu.VMEM)
    return _tc_call(
        body, name=name, in_specs=[vm], out_specs=vm, out_shape=jax.ShapeDtypeStruct((r, cols), F32),
        scratch_shapes=[pltpu.VMEM((N_DEV, r, cols), F32), pltpu.SemaphoreType.DMA((N_DEV - 1,)),
                        pltpu.SemaphoreType.DMA((N_DEV - 1,))],
    )(vec)


def _rope_tables(positions):
    half = QK_ROPE // 2
    inv_freq = 1.0 / (ROPE_THETA ** (jnp.arange(half, dtype=F32) / half))
    ang = positions.astype(F32)[:, None] * inv_freq
    zeros = jnp.zeros((positions.shape[0], LANES - QK_ROPE), F32)
    cos, sin = jnp.cos(ang), jnp.sin(ang)
    return jnp.concatenate([cos, cos, zeros], axis=1), jnp.concatenate([sin, sin, zeros], axis=1)


def _local_step(x, positions, tgt, wf, small, rs):
    cos, sin = _rope_tables(positions)
    w_in, w_out = wf["sc_w_in"], wf["sc_w_out"]
    w_ups, w_downs = (wf["ffn_w_up0"], wf["ffn_w_up1"]), (wf["ffn_w_down0"], wf["ffn_w_down1"])
    w_kv, w_ukv, w_dq, w_uq, w_o = wf["w_kv"], wf["w_ukv"], wf["w_dq"], wf["w_uq"], wf["w_o"]
    attn_norm, ffn_norm = small["attn_norm"], small["ffn_norm"]
    conv_b = small["ffn_conv_b"]

    def ffn_fwd(h, l):
        hf = _rms_fwd(h, ffn_norm[l:l + 1], f"ffn{l}_norm")
        up = _nn_parts(f"ffn{l}_up", hf, w_ups[l], 2, BF16)
        a = _gate_fwd(up, small["ffn_conv_w"][l], conv_b[l:l + 1], f"ffn{l}_gate")
        return _nn(f"ffn{l}_down", a, w_downs[l], F32, add=h), (hf, up, a)

    def ffn_bwd(h, dh_out, dh_out_b, l, saved, gi, hooks):
        run = lambda stage: hooks.get(stage, lambda: None)()
        hf, up, a = saved
        da = _nt(f"ffn{l}_down_dx", dh_out_b, w_downs[l], BF16)
        run("down_dx")
        d_down = _tn(f"ffn{l}_down_dw", a, dh_out_b, BF16)
        dup, d_cw, d_cb = _gate_bwd(up, small["ffn_conv_w"][l], conv_b[l:l + 1], da, f"ffn{l}_gate_bwd")
        run("gate_bwd")
        d_up = _dw_ffn_up(f"ffn{l}_up_dw", hf, dup)
        rs.start(gi, {f"ffn_w_down{l}": d_down.reshape(N_CHIPS, F_FF // N_CHIPS, D), f"ffn_w_up{l}": d_up})
        dhf = _nt_parts(f"ffn{l}_up_dx", dup, w_ups[l], BF16)
        run("up_dx")
        dh, dh_b, d_norm = _rms_bwd(h, ffn_norm[l:l + 1], dhf, dh_out, f"ffn{l}_norm_bwd", matmul_copy=True)
        return dh, dh_b, d_cw, d_cb, d_norm

    hn0 = _rms_fwd(x, attn_norm[0:1], "attn0_norm")
    z = _nn_parts("sc_in", hn0, w_in, 3, BF16)
    mix = _scmix_fwd(z, small["sc_conv_w"])
    h1 = _nn("sc_out", mix, w_out, F32, add=x)
    h2, ffn0_saved = ffn_fwd(h1, 0)

    hk = _rms_fwd(h2, small["kv_in_norm"], "kv_in_norm")
    kvpre = _nn("kv_down", hk, w_kv, F32)
    ckv, kr = _kv_elem_fwd(kvpre, small["kv_latent_norm"], cos, sin)
    knv = _nn_parts("kv_up", ckv, w_ukv, 2, BF16, stacked=True)

    hn1 = _rms_fwd(h2, attn_norm[1:2], "attn1_norm")
    cq_pre = _nn("q_down", hn1, w_dq, F32)
    cq = _rms_fwd(cq_pre, small["q_latent_norm"], "q_latent_norm")
    q = _q_rope_fwd(_nn("q_up", cq, w_uq, F32), cos, sin)
    o = _attn_fwd(q, knv, kr)
    h3 = _nn("attn_out", o, w_o, F32, add=h2)
    h4, ffn1_saved = ffn_fwd(h3, 1)

    loss, dh4, dh4_b, d_final = _loss_head(h4, small["final_norm"], tgt)

    rows = D // N_CHIPS
    dh3, dh3_b, d_cw1, d_cb1, d_fn1 = ffn_bwd(h3, dh4, dh4_b, 1, ffn1_saved, 0, {})

    do = _nt("attn_out_dx", dh3_b, w_o, BF16)
    d_wo = _tn("attn_out_dw", o, dh3_b, BF16)
    rs.pair_sums(0)
    dq, dknv, dkr = _attn_bwd(q, knv, kr, do, cos, sin)
    rs.chip_sums(0)
    dcq = _nt("q_up_dx", dq, w_uq, F32)
    d_wuq = _tn("q_up_dw", cq, dq, BF16).reshape(Q_LORA, N_CHIPS, -1).transpose(1, 0, 2)
    dcq_pre, d_qln = _rms_bwd(cq_pre, small["q_latent_norm"], dcq, None, "q_latent_norm_bwd")
    rs.finish(0)
    dhn1 = _nt("q_down_dx", dcq_pre, w_dq, BF16)
    d_wdq = _tn("q_down_dw", hn1, dcq_pre, BF16)
    dh2, d_an1 = _rms_bwd(h2, attn_norm[1:2], dhn1, dh3, "attn1_norm_bwd")

    dckv = _nt_parts("kv_up_dx", dknv, w_ukv, F32, stacked=True)
    d_wukv = _dw_ukv(ckv, dknv)
    dkvpre, d_kvln = _kv_elem_bwd(kvpre, small["kv_latent_norm"], dckv, dkr, cos, sin)
    dhk = _nt("kv_down_dx", dkvpre, w_kv, BF16)
    d_wkv = _tn("kv_down_dw", hk, dkvpre, BF16)
    rs.start(1, {
        "w_o": d_wo.reshape(N_CHIPS, rows, D), "w_uq": d_wuq, "w_dq": d_wdq.reshape(N_CHIPS, rows, Q_LORA),
        "w_ukv": d_wukv.reshape(N_CHIPS, 2 * KV_LORA, -1), "w_kv": d_wkv.reshape(N_CHIPS, rows, KVP),
    })
    dh2, dh2_b, d_kvin = _rms_bwd(h2, small["kv_in_norm"], dhk, dh2, "kv_in_norm_bwd", matmul_copy=True)

    dh1, dh1_b, d_cw0, d_cb0, d_fn0 = ffn_bwd(h1, dh2, dh2_b, 0, ffn0_saved, 2, {
        "down_dx": lambda: rs.pair_sums(1), "gate_bwd": lambda: rs.chip_sums(1), "up_dx": lambda: rs.finish(1)})
    rs.pair_sums(2)

    d_wout = _tn("sc_out_dw", mix, dh1_b, BF16)
    dmix = _nt("sc_out_dx", dh1_b, w_out, BF16)
    dz, d_scw = _scmix_bwd(z, small["sc_conv_w"], dmix)
    d_win = _dw_sc_in(hn0, dz)
    rs.start(3, {"sc_w_out": d_wout.reshape(N_CHIPS, rows, D), "sc_w_in": d_win})
    dhn0 = _nt_parts("sc_in_dx", dz, w_in, BF16)
    dx, d_an0 = _rms_bwd(x, attn_norm[0:1], dhn0, dh1, "attn0_norm_bwd")

    small_g = {
        "attn_norm": jnp.concatenate([d_an0, d_an1]), "ffn_norm": jnp.concatenate([d_fn0, d_fn1]),
        "final_norm": d_final, "kv_in_norm": d_kvin, "kv_latent_norm": d_kvln, "q_latent_norm": d_qln,
        "ffn_conv_b": jnp.concatenate([d_cb0, d_cb1]), "sc_conv_w": d_scw, "ffn_conv_w": jnp.stack([d_cw0, d_cw1]),
    }
    return loss, dx, small_g


RS_GROUPS = (("ffn_w_down1", "ffn_w_up1"), ("w_o", "w_uq", "w_dq", "w_ukv", "w_kv"),
             ("ffn_w_down0", "ffn_w_up0"), ("sc_w_out", "sc_w_in"))


class _ReduceScatter:
    def __init__(self, ids, finish):
        self.ids, self.grads, self.step, self.mine, self.sib, self.finish = ids, {}, {}, {}, {}, finish

    def _cid(self, gi):
        return len(AG_GROUPS) + 3 * gi

    def start(self, gi, grads):
        self.grads.update(grads)
        own = [grads[n] for n in RS_GROUPS[gi]]
        self.step[gi] = (own, _pair_exchange(own, gi, self._cid(gi)))

    def pair_sums(self, gi):
        own, ra = self.step[gi]
        sums = _pair_sums(self.ids, own, ra, f"rs_pair_sums{gi}")
        self.step[gi] = (own, ra, _chip_exchange(sums, gi, self._cid(gi) + 1))

    def chip_sums(self, gi):
        own, ra, rb = self.step[gi]
        mine = _chip_sums(self.ids, own, ra, rb, f"rs_chip_sums{gi}")
        self.mine.update(zip(RS_GROUPS[gi], mine))
        self.sib.update(zip(RS_GROUPS[gi], _pair_swap(mine, gi, self._cid(gi) + 2)))

SMALL_REPL = ("attn_norm", "ffn_norm", "final_norm", "kv_in_norm", "kv_latent_norm", "q_latent_norm", "ffn_conv_b")
SMALL_SHARDED = ("sc_conv_w", "ffn_conv_w")
SMALL_ROWS = 256


def _pad_heads(w_uq):
    per_head = w_uq.reshape(Q_LORA, -1, QK_NOPE + QK_ROPE)
    return jnp.pad(per_head, ((0, 0), (0, 0), (0, HEAD_PAD - QK_NOPE - QK_ROPE))).reshape(Q_LORA, -1)


def _pack_kv(w_dkv, w_kr):
    return jnp.concatenate([w_dkv, w_kr, jnp.zeros((w_kr.shape[0], LANES - QK_ROPE), w_kr.dtype)], axis=1)


def kernel(x, positions, attn_norm, ffn_norm, final_norm, sc_w_in, sc_conv_w, sc_w_out, kv_in_norm, w_dkv, kv_latent_norm, w_kr, w_uk, w_uv, w_dq, q_latent_norm, w_uq, w_o, ffn_w_up, ffn_conv_w, ffn_conv_b, ffn_w_down, loss_target, m_attn_norm, m_ffn_norm, m_final_norm, m_sc_w_in, m_sc_conv_w, m_sc_w_out, m_kv_in_norm, m_w_dkv, m_kv_latent_norm, m_w_kr, m_w_uk, m_w_uv, m_w_dq, m_q_latent_norm, m_w_uq, m_w_o, m_ffn_w_up, m_ffn_conv_w, m_ffn_conv_b, m_ffn_w_down, v_attn_norm, v_ffn_norm, v_final_norm, v_sc_w_in, v_sc_conv_w, v_sc_w_out, v_kv_in_norm, v_w_dkv, v_kv_latent_norm, v_w_kr, v_w_uk, v_w_uv, v_w_dq, v_q_latent_norm, v_w_uq, v_w_o, v_ffn_w_up, v_ffn_conv_w, v_ffn_conv_b, v_ffn_w_down):
    names = ("attn_norm", "ffn_norm", "final_norm", "sc_w_in", "sc_conv_w", "sc_w_out", "kv_in_norm", "w_dkv",
             "kv_latent_norm", "w_kr", "w_uk", "w_uv", "w_dq", "q_latent_norm", "w_uq", "w_o", "ffn_w_up",
             "ffn_conv_w", "ffn_conv_b", "ffn_w_down")
    w = dict(zip(names, (attn_norm, ffn_norm, final_norm, sc_w_in, sc_conv_w, sc_w_out, kv_in_norm, w_dkv,
                         kv_latent_norm, w_kr, w_uk, w_uv, w_dq, q_latent_norm, w_uq, w_o, ffn_w_up,
                         ffn_conv_w, ffn_conv_b, ffn_w_down)))
    m = dict(zip(names, (m_attn_norm, m_ffn_norm, m_final_norm, m_sc_w_in, m_sc_conv_w, m_sc_w_out, m_kv_in_norm,
                         m_w_dkv, m_kv_latent_norm, m_w_kr, m_w_uk, m_w_uv, m_w_dq, m_q_latent_norm, m_w_uq, m_w_o,
                         m_ffn_w_up, m_ffn_conv_w, m_ffn_conv_b, m_ffn_w_down)))
    v = dict(zip(names, (v_attn_norm, v_ffn_norm, v_final_norm, v_sc_w_in, v_sc_conv_w, v_sc_w_out, v_kv_in_norm,
                         v_w_dkv, v_kv_latent_norm, v_w_kr, v_w_uk, v_w_uv, v_w_dq, v_q_latent_norm, v_w_uq, v_w_o,
                         v_ffn_w_up, v_ffn_conv_w, v_ffn_conv_b, v_ffn_w_down)))

    _ORDER[0] = None
    ix, iy, ic = lax.axis_index("x"), lax.axis_index("y"), lax.axis_index("c")
    chip = 2 * ix + iy
    ids = jnp.stack([ic, chip]).astype(jnp.int32)

    def shards_of(t):
        return {
            "sc_w_in": t["sc_w_in"][0], "sc_w_out": t["sc_w_out"][0], "ffn_w_up": t["ffn_w_up"],
            "ffn_w_down": t["ffn_w_down"], "w_kv": _pack_kv(t["w_dkv"], t["w_kr"]),
            "w_ukv": jnp.stack([t["w_uk"], t["w_uv"]]), "w_dq": t["w_dq"][0], "w_uq": _pad_heads(t["w_uq"][0]),
            "w_o": t["w_o"][0],
        }

    ws, ms, vs = shards_of(w), shards_of(m), shards_of(v)

    def ag_shard(name):
        if name == "sc_conv_w":
            return sc_conv_w[0]
        if name == "ffn_conv_w":
            return ffn_conv_w.reshape(6, -1)
        if name[:-1] in ("ffn_w_up", "ffn_w_down"):
            return ws[name[:-1]][int(name[-1])].astype(BF16)
        return ws[name].astype(BF16)

    wf = {}
    for gi, wms in enumerate(AG_GROUPS):
        fulls = _all_gather_group(gi, [ag_shard(wm.name) for wm in wms])
        wf.update({wm.name: f for wm, f in zip(wms, fulls)})
    small = {
        "attn_norm": attn_norm, "ffn_norm": ffn_norm, "final_norm": final_norm[None], "kv_in_norm": kv_in_norm[None],
        "kv_latent_norm": kv_latent_norm[None], "q_latent_norm": q_latent_norm, "ffn_conv_b": ffn_conv_b,
        "sc_conv_w": wf["sc_conv_w"].transpose(1, 0, 2).reshape(3, D),
        "ffn_conv_w": wf["ffn_conv_w"].reshape(N_CHIPS, 2, 3, -1).transpose(1, 2, 0, 3).reshape(2, 3, F_FF),
    }

    res = {}

    merged = lambda a: a.reshape(2 * KV_LORA, -1)

    def adamw_group(gi):
        items = []
        for key in RS_GROUPS[gi]:
            n, layer = (key[:-1], int(key[-1])) if key[:-1] in ("ffn_w_up", "ffn_w_down") else (key, None)
            w_, m_, v_ = (merged(t[n]) for t in (ws, ms, vs)) if n == "w_ukv" else (ws[n], ms[n], vs[n])
            items.append(dict(name=n, w=w_, m=m_, v=v_, g_mine=rs.mine[key], g_sib=rs.sib[key], layer=layer,
                              prev=res.get(n)))
        for it, out in zip(items, _adamw_shards(ids, items, f"adamw_group{gi}")):
            res[it["name"]] = out

    rs = _ReduceScatter(ids, adamw_group)
    loss, dx, small_g = _local_step(x[0], positions[0], loss_target[0], wf, small, rs)

    s_order = SMALL_REPL + SMALL_SHARDED
    flat = jnp.concatenate([small_g[n].reshape(-1) for n in s_order] + [loss.reshape(-1)])
    flat = jnp.pad(flat, (0, SMALL_ROWS * LANES - flat.shape[0])).reshape(SMALL_ROWS, LANES)
    red = _all_reduce_small(flat, "ar_small").reshape(-1)
    sg, off = {}, 0
    for n in s_order:
        sz = small_g[n].size
        sg[n] = red[off:off + sz].reshape(small_g[n].shape)
        off += sz
    loss_out = red[off]
    grads = {n: sg[n].reshape(w[n].shape) for n in SMALL_REPL}
    grads["sc_conv_w"] = lax.dynamic_slice_in_dim(sg["sc_conv_w"], chip * (D // N_CHIPS), D // N_CHIPS, axis=1)[None]
    grads["ffn_conv_w"] = lax.dynamic_slice_in_dim(sg["ffn_conv_w"], chip * (F_FF // N_CHIPS), F_FF // N_CHIPS, axis=2)

    rs.chip_sums(2)
    rs.pair_sums(3)
    rs.finish(2)
    rs.chip_sums(3)
    rs.finish(3)
    outs = [grads, {}, {}, {}]
    for k, dst in enumerate(outs):
        for n in ("sc_w_in", "sc_w_out", "w_dq", "w_o"):
            dst[n] = res[n][k][None]
        unpadded = res["w_uq"][k].reshape(Q_LORA, -1, HEAD_PAD)[:, :, :QK_NOPE + QK_ROPE]
        dst["w_uq"] = unpadded.reshape(w_uq.shape)
        dst["ffn_w_up"], dst["ffn_w_down"] = res["ffn_w_up"][k], res["ffn_w_down"][k]
        dst["w_dkv"], dst["w_kr"] = res["w_kv"][k][:, :KV_LORA], res["w_kv"][k][:, KV_LORA:KV_LORA + QK_ROPE]
        dst["w_uk"], dst["w_uv"] = res["w_ukv"][k][:KV_LORA], res["w_ukv"][k][KV_LORA:]
    grads, delta, new_m, new_v = outs

    small_names = SMALL_REPL + SMALL_SHARDED

    def pack_small(tree):
        return jnp.concatenate([tree[n].reshape(-1) for n in small_names]).reshape(-1, LANES)

    small_res = _adamw_small(pack_small(w), pack_small(grads), pack_small(m), pack_small(v))
    for slab, dst in zip(small_res, (delta, new_m, new_v)):
        f, off = slab.reshape(-1), 0
        for n in small_names:
            dst[n] = f[off:off + w[n].size].reshape(w[n].shape)
            off += w[n].size

    _ORDER[0] = None
    return (loss_out, dx[None], *[grads[n] for n in names], *[delta[n] for n in names],
            *[new_m[n] for n in names], *[new_v[n] for n in names])
```

```python
from typing import NamedTuple

import jax
import jax.numpy as jnp
from jax import lax
from jax.experimental import pallas as pl
from jax.experimental.pallas import tpu as pltpu
from jax.experimental.pallas import tpu_sc as plsc

F32 = jnp.float32
BF16 = jnp.bfloat16

T = 2048
D = 1024
F_FF = 2816
N_HEADS = 8
QK_NOPE = 128
QK_ROPE = 64
V_HEAD = 128
Q_LORA = 384
KV_LORA = 256
CHUNK_SHIFT = 6
ROPE_THETA = 10000.0
EPS = 1e-6
NEG_INF = -1e30
HEAD_PAD = 256
KVP = KV_LORA + 128

ADAM_LR = 0.001
ADAM_B1 = 0.9
ADAM_B2 = 0.999
ADAM_EPS = 1e-08
ADAM_WD = 0.01
ADAM_STEP = 10

N_CHIPS = 4
N_DEV = 8
LANES = 128
TC = 256
V7X_VMEM_LIMIT = 56 * 1024 * 1024

MESH = pl.DeviceIdType.MESH
ANY = pl.BlockSpec(memory_space=pl.ANY)


class _W(NamedTuple):
    name: str
    kind: str
    nl: int
    k: int
    n: int


AG_GROUPS = (
    (_W("sc_w_in", "col", 1, D, 3 * D // N_CHIPS), _W("sc_conv_w", "tiny", 1, 3, D // N_CHIPS),
     _W("ffn_conv_w", "tiny", 1, 6, F_FF // N_CHIPS)),
    (_W("sc_w_out", "row", 1, D // N_CHIPS, D),),
    (_W("ffn_w_up0", "col", 1, D, 2 * F_FF // N_CHIPS),),
    (_W("ffn_w_down0", "row", 1, F_FF // N_CHIPS, D),),
    (_W("w_kv", "row", 1, D // N_CHIPS, KVP), _W("w_ukv", "col", 2, KV_LORA, N_HEADS * QK_NOPE // N_CHIPS),
     _W("w_dq", "row", 1, D // N_CHIPS, Q_LORA),
     _W("w_uq", "col", 1, Q_LORA, N_HEADS * HEAD_PAD // N_CHIPS),
     _W("w_o", "row", 1, N_HEADS * V_HEAD // N_CHIPS, D)),
    (_W("ffn_w_up1", "col", 1, D, 2 * F_FF // N_CHIPS), _W("ffn_w_down1", "row", 1, F_FF // N_CHIPS, D)),
)


def _cp(*sem):
    return pltpu.CompilerParams(dimension_semantics=sem, vmem_limit_bytes=V7X_VMEM_LIMIT)


_ORDER = [None]


def _tc_call(body, *, name, out_shape, in_specs=None, out_specs=None, grid=(), scratch_shapes=(), prefetch=0,
             input_output_aliases=None, compiler_params=None):
    def run(*args):
        specs = [pl.BlockSpec(memory_space=pltpu.VMEM)] * (len(args) - prefetch) if in_specs is None else list(in_specs)
        inner, dep = body, _ORDER[0]
        if dep is not None:
            unread = prefetch + len(specs)
            specs, args = specs + [ANY], (*args, dep)

            def inner(*refs):
                return body(*refs[:unread], *refs[unread + 1:])

        kwargs = dict(name=name, out_shape=out_shape, input_output_aliases=input_output_aliases or {},
                      compiler_params=compiler_params)
        if prefetch:
            kwargs["grid_spec"] = pltpu.PrefetchScalarGridSpec(
                num_scalar_prefetch=prefetch, grid=grid, in_specs=specs, out_specs=out_specs,
                scratch_shapes=scratch_shapes)
        else:
            kwargs.update(grid=grid, in_specs=specs, scratch_shapes=scratch_shapes)
            if out_specs is not None:
                kwargs["out_specs"] = out_specs
        out = pl.pallas_call(inner, **kwargs)(*args)
        _ORDER[0] = out[0] if isinstance(out, (list, tuple)) else out
        return out

    return run


def _tile(n, cands):
    for c in cands:
        if n % c == 0:
            return c
    raise ValueError(f"no tile for {n}")


NN_DIMS = (((1,), (0,)), ((), ()))
NT_DIMS = (((1,), (1,)), ((), ()))
TN_DIMS = (((0,), (0,)), ((), ()))
M_TILES = (1024, 512, 384, 256, 128)
N_TILES = (1408, 1024, 768, 512, 384, 256, 128)
MM_BLOCK_BYTES = 36 * 1024 * 1024


def _fit(m, n, block_bytes, m_tiles=M_TILES, n_tiles=N_TILES):
    for tm in [c for c in m_tiles if m % c == 0]:
        for tn in [c for c in n_tiles if n % c == 0]:
            if 2 * block_bytes(tm, tn) + 4 * tm * tn <= MM_BLOCK_BYTES:
                return tm, tn
    raise ValueError(f"no tiles for {m} x {n}")


def _size(x):
    return x.dtype.itemsize


def _mm(name, a, b, dims, grid, a_spec, b_spec, o_spec, o_sds, add=None, red=None, acc_shape=None):
    n_red = None if red is None else grid[red]

    def body(*refs):
        a_ref, b_ref = refs[0], refs[1]
        add_ref = refs[2] if add is not None else None
        o_ref = refs[3] if add is not None else refs[2]
        part = lax.dot_general(a_ref[...].astype(BF16), b_ref[...].astype(BF16), dims, preferred_element_type=F32)
        if red is None:
            if add is not None:
                part = part + add_ref[...]
            o_ref[...] = part.astype(o_ref.dtype)
            return
        acc_ref = refs[-1]
        r = pl.program_id(red)

        @pl.when(r == 0)
        def _():
            acc_ref[...] = part

        @pl.when(r > 0)
        def _():
            acc_ref[...] += part

        @pl.when(r == n_red - 1)
        def _():
            o_ref[...] = acc_ref[...].astype(o_ref.dtype)

    sem = tuple("arbitrary" if ax == red else "parallel" for ax in range(len(grid)))
    in_specs = [a_spec, b_spec] + ([o_spec] if add is not None else [])
    args = (a, b) + ((add,) if add is not None else ())
    return _tc_call(
        body, name=name, grid=grid, in_specs=in_specs, out_specs=o_spec, out_shape=o_sds,
        scratch_shapes=[] if red is None else [pltpu.VMEM(acc_shape, F32)], compiler_params=_cp(*sem),
    )(*args)


def _nn(name, a, b, out_dtype, add=None, lead=None):
    (m, k), n = a.shape, b.shape[-1]
    osz = jnp.dtype(out_dtype).itemsize + (4 if add is not None else 0)
    tm, tn = _fit(m, n, lambda tm, tn: tm * k * _size(a) + k * tn * _size(b) + tm * tn * osz)
    if lead is None:
        b_spec = pl.BlockSpec((k, tn), lambda i, j: (0, j))
    else:
        b_spec = pl.BlockSpec((None, k, tn), lambda i, j: (lead, 0, j))
    return _mm(name, a, b, NN_DIMS, (m // tm, n // tn), pl.BlockSpec((tm, k), lambda i, j: (i, 0)), b_spec,
               pl.BlockSpec((tm, tn), lambda i, j: (i, j)), jax.ShapeDtypeStruct((m, n), out_dtype), add=add)


def _nn_parts(name, a, b, parts, out_dtype, lead=None, stacked=False):
    m, k = a.shape
    c = b.shape[-1] if stacked else b.shape[-1] // parts
    osz = jnp.dtype(out_dtype).itemsize
    tm, tn = _fit(m, c, lambda tm, tn: tm * k * _size(a) + k * tn * _size(b) + tm * tn * osz)
    nb = c // tn
    if stacked:
        b_spec = pl.BlockSpec((None, k, tn), lambda i, p, j: (p, 0, j))
    elif lead is None:
        b_spec = pl.BlockSpec((k, tn), lambda i, p, j: (0, p * nb + j))
    else:
        b_spec = pl.BlockSpec((None, k, tn), lambda i, p, j: (lead, 0, p * nb + j))
    return _mm(name, a, b, NN_DIMS, (m // tm, parts, nb), pl.BlockSpec((tm, k), lambda i, p, j: (i, 0)), b_spec,
               pl.BlockSpec((None, tm, tn), lambda i, p, j: (p, i, j)), jax.ShapeDtypeStruct((parts, m, c), out_dtype))


def _nt(name, a, b, out_dtype, lead=None):
    (m, k), n = a.shape, b.shape[-2]
    osz = jnp.dtype(out_dtype).itemsize
    tm, tn = _fit(m, n, lambda tm, tn: tm * k * _size(a) + tn * k * _size(b) + tm * tn * osz)
    if lead is None:
        b_spec = pl.BlockSpec((tn, k), lambda i, j: (j, 0))
    else:
        b_spec = pl.BlockSpec((None, tn, k), lambda i, j: (lead, j, 0))
    return _mm(name, a, b, NT_DIMS, (m // tm, n // tn), pl.BlockSpec((tm, k), lambda i, j: (i, 0)), b_spec,
               pl.BlockSpec((tm, tn), lambda i, j: (i, j)), jax.ShapeDtypeStruct((m, n), out_dtype))


def _nt_parts(name, a, b, out_dtype, lead=None, stacked=False):
    parts, m, c = a.shape
    n = b.shape[-2]
    osz = jnp.dtype(out_dtype).itemsize + 2
    tm, tn = _fit(m, n, lambda tm, tn: tm * c * _size(a) + tn * c * _size(b) + tm * tn * osz)
    if stacked:
        b_spec = pl.BlockSpec((None, tn, c), lambda i, j, p: (p, j, 0))
    elif lead is None:
        b_spec = pl.BlockSpec((tn, c), lambda i, j, p: (j, p))
    else:
        b_spec = pl.BlockSpec((None, tn, c), lambda i, j, p: (lead, j, p))
    return _mm(name, a, b, NT_DIMS, (m // tm, n // tn, parts), pl.BlockSpec((None, tm, c), lambda i, j, p: (p, i, 0)),
               b_spec, pl.BlockSpec((tm, tn), lambda i, j, p: (i, j)), jax.ShapeDtypeStruct((m, n), out_dtype),
               red=2, acc_shape=(tm, tn))


def _tn(name, a, b, out_dtype):
    (k, m), n = a.shape, b.shape[1]
    osz = jnp.dtype(out_dtype).itemsize
    tm, tn = _fit(m, n, lambda tm, tn: k * tm * _size(a) + k * tn * _size(b) + tm * tn * osz,
                  m_tiles=(512, 384, 256, 128), n_tiles=(n,) + N_TILES)
    return _mm(name, a, b, TN_DIMS, (m // tm, n // tn), pl.BlockSpec((k, tm), lambda i, j: (0, i)),
               pl.BlockSpec((k, tn), lambda i, j: (0, j)), pl.BlockSpec((tm, tn), lambda i, j: (i, j)),
               jax.ShapeDtypeStruct((m, n), out_dtype))


def _dw_sc_in(hn, dz):
    t, tn, tm = hn.shape[0], TC, 512
    per_part, per_chip = D // tn, 3 * D // N_CHIPS // tn
    return _mm("sc_in_dw", hn, dz, TN_DIMS, (D // tm, 3 * D // tn), pl.BlockSpec((t, tm), lambda i, j: (0, i)),
               pl.BlockSpec((None, t, tn), lambda i, j: (j // per_part, 0, j % per_part)),
               pl.BlockSpec((None, tm, tn), lambda i, j: (j // per_chip, i, j % per_chip)),
               jax.ShapeDtypeStruct((N_CHIPS, D, 3 * D // N_CHIPS), BF16))


def _dw_ffn_up(name, hf, dup):
    t, tm, ns = hf.shape[0], 512, 2 * F_FF // N_CHIPS
    return _mm(name, hf, dup, TN_DIMS, (N_CHIPS, D // tm), pl.BlockSpec((t, tm), lambda s, i: (0, i)),
               pl.BlockSpec((None, t, ns), lambda s, i: (s // 2, 0, s % 2)),
               pl.BlockSpec((None, tm, ns), lambda s, i: (s, i, 0)), jax.ShapeDtypeStruct((N_CHIPS, D, ns), BF16))


def _dw_ukv(ckv, dknv):
    t, ns = ckv.shape[0], N_HEADS * QK_NOPE // N_CHIPS
    return _mm("kv_up_dw", ckv, dknv, TN_DIMS, (2, N_CHIPS), pl.BlockSpec((t, KV_LORA), lambda p, s: (0, 0)),
               pl.BlockSpec((None, t, ns), lambda p, s: (p, 0, s)),
               pl.BlockSpec((None, None, KV_LORA, ns), lambda p, s: (s, p, 0, 0)),
               jax.ShapeDtypeStruct((N_CHIPS, 2, KV_LORA, ns), BF16))


def _rms_fwd(x, g, name):
    t, d = x.shape
    tr = 512

    def body(x_ref, g_ref, o_ref):
        def rows(sl):
            xv = x_ref[sl, :]
            r = lax.rsqrt(jnp.mean(xv * xv, axis=1, keepdims=True) + EPS)
            o_ref[sl, :] = (xv * r * g_ref[...]).astype(o_ref.dtype)

        _for_row_chunks(tr, rows)

    row = pl.BlockSpec((tr, d), lambda i: (i, 0))
    return _tc_call(
        body, name=name, grid=(t // tr,), in_specs=[row, pl.BlockSpec((1, d), lambda i: (0, 0))],
        out_specs=row, out_shape=jax.ShapeDtypeStruct((t, d), BF16), compiler_params=_cp("parallel"),
    )(x, g)


NORM_ROWS = 16


def _for_row_chunks(n_rows, rows):
    def step(i, carry):
        rows(pl.ds(pl.multiple_of(i * NORM_ROWS, NORM_ROWS), NORM_ROWS))
        return carry

    lax.fori_loop(0, n_rows // NORM_ROWS, step, 0)


def _fold_rows(x):
    return jnp.sum(x.reshape(x.shape[0] // 8, 8, x.shape[1]), axis=0)


def _rms_bwd_rows(xv, g, dy):
    r = lax.rsqrt(jnp.mean(xv * xv, axis=1, keepdims=True) + EPS)
    xh = xv * r
    gy = dy * g
    return r * (gy - xh * jnp.mean(gy * xh, axis=1, keepdims=True)), dy * xh


def _rms_bwd_math(xv, g, dy):
    dx, prod = _rms_bwd_rows(xv, g, dy)
    return dx, jnp.sum(prod, axis=0, keepdims=True)


def _rms_bwd(x, g, dy, add, name, matmul_copy=False):
    t, d = x.shape
    tr = 512
    n_in = 3 + (add is not None)

    def body(*refs):
        x_ref, g_ref, dy_ref = refs[:3]
        dx_ref, dg_ref, acc_ref = refs[n_in], refs[-2], refs[-1]
        acc_ref[...] = jnp.zeros_like(acc_ref)

        def rows(sl):
            dx, prod = _rms_bwd_rows(x_ref[sl, :], g_ref[...], dy_ref[sl, :].astype(F32))
            if add is not None:
                dx = dx + refs[3][sl, :]
            dx_ref[sl, :] = dx
            if matmul_copy:
                refs[n_in + 1][sl, :] = dx.astype(BF16)
            acc_ref[...] += _fold_rows(prod)

        _for_row_chunks(tr, rows)

        @pl.when(pl.program_id(0) == 0)
        def _():
            dg_ref[...] = jnp.zeros_like(dg_ref)

        dg_ref[...] += jnp.sum(acc_ref[...], axis=0, keepdims=True)

    row = pl.BlockSpec((tr, d), lambda i: (i, 0))
    vec = pl.BlockSpec((1, d), lambda i: (0, 0))
    in_specs = [row, vec, row] + ([row] if add is not None else [])
    args = (x, g, dy) + ((add,) if add is not None else ())
    copies = [jax.ShapeDtypeStruct((t, d), BF16)] if matmul_copy else []
    return _tc_call(
        body, name=name, grid=(t // tr,), in_specs=in_specs, out_specs=[row] * (1 + len(copies)) + [vec],
        out_shape=[jax.ShapeDtypeStruct((t, d), F32)] + copies + [jax.ShapeDtypeStruct((1, d), F32)],
        scratch_shapes=[pltpu.VMEM((8, d), F32)], compiler_params=_cp("arbitrary"),
    )(*args)


def _loss_head(h, g, tgt):
    t, d = h.shape
    tr = 512

    def body(h_ref, g_ref, t_ref, loss_ref, dh_ref, dhb_ref, dg_ref, acc_ref, lacc_ref):
        acc_ref[...] = jnp.zeros_like(acc_ref)
        lacc_ref[...] = jnp.zeros_like(lacc_ref)

        def rows(sl):
            xv, gv = h_ref[sl, :], g_ref[...]
            r = lax.rsqrt(jnp.mean(xv * xv, axis=1, keepdims=True) + EPS)
            err = xv * r * gv - t_ref[sl, :]
            lacc_ref[...] += jnp.mean(err * err, axis=1, keepdims=True)
            dx, prod = _rms_bwd_rows(xv, gv, err * (1.0 / d))
            dh_ref[sl, :] = dx
            dhb_ref[sl, :] = dx.astype(BF16)
            acc_ref[...] += _fold_rows(prod)

        _for_row_chunks(tr, rows)

        @pl.when(pl.program_id(0) == 0)
        def _():
            dg_ref[...] = jnp.zeros_like(dg_ref)
            loss_ref[...] = jnp.zeros_like(loss_ref)

        dg_ref[...] += jnp.sum(acc_ref[...], axis=0, keepdims=True)
        loss_ref[...] += jnp.broadcast_to(0.5 * jnp.sum(lacc_ref[...], axis=0, keepdims=True), loss_ref.shape)

    row = pl.BlockSpec((tr, d), lambda i: (i, 0))
    vec = pl.BlockSpec((1, d), lambda i: (0, 0))
    lspec = pl.BlockSpec((1, LANES), lambda i: (0, 0))
    return _tc_call(
        body, name="loss_head", grid=(t // tr,), in_specs=[row, vec, row], out_specs=[lspec, row, row, vec],
        out_shape=[jax.ShapeDtypeStruct((1, LANES), F32), jax.ShapeDtypeStruct((t, d), F32),
                   jax.ShapeDtypeStruct((t, d), BF16), jax.ShapeDtypeStruct((1, d), F32)],
        scratch_shapes=[pltpu.VMEM((8, d), F32), pltpu.VMEM((NORM_ROWS, 1), F32)], compiler_params=_cp("arbitrary"),
    )(h, g, tgt)


def _rot_half(x):
    lane = lax.broadcasted_iota(jnp.int32, x.shape, 1)
    return jnp.where((lane % QK_ROPE) < QK_ROPE // 2, -pltpu.roll(x, LANES - 32, axis=1),
                     pltpu.roll(x, 32, axis=1))


def _rope_fwd_math(x, cos, sin):
    return x * cos + _rot_half(x) * sin


def _rope_bwd_math(dy, cos, sin):
    return dy * cos - _rot_half(dy * sin)


def _q_up_rope(cq, w_uq, cos, sin):
    t, k = cq.shape
    w = w_uq.shape[1]
    tr = 256

    def body(a_ref, w_ref, c_ref, s_ref, o_ref):
        av, cv, sv = a_ref[...], c_ref[...], s_ref[...]
        for h in range(N_HEADS):
            lo = h * HEAD_PAD
            qh = jnp.dot(av, w_ref[:, lo:lo + HEAD_PAD], preferred_element_type=F32)
            o_ref[:, lo:lo + QK_NOPE] = qh[:, :QK_NOPE].astype(BF16)
            o_ref[:, lo + QK_NOPE:lo + HEAD_PAD] = _rope_fwd_math(qh[:, QK_NOPE:], cv, sv).astype(BF16)

    tab = pl.BlockSpec((tr, LANES), lambda i: (i, 0))
    return _tc_call(
        body, name="q_up_rope", grid=(t // tr,),
        in_specs=[pl.BlockSpec((tr, k), lambda i: (i, 0)), pl.BlockSpec((k, w), lambda i: (0, 0)), tab, tab],
        out_specs=pl.BlockSpec((tr, w), lambda i: (i, 0)), out_shape=jax.ShapeDtypeStruct((t, w), BF16),
        compiler_params=_cp("parallel"),
    )(cq, w_uq, cos, sin)


def _kv_elem_fwd(kvpre, g, cos, sin):
    t = kvpre.shape[0]
    tr = 512

    def body(p_ref, g_ref, c_ref, s_ref, ckv_ref, kr_ref):
        lat = p_ref[:, :KV_LORA]
        r = lax.rsqrt(jnp.mean(lat * lat, axis=1, keepdims=True) + EPS)
        ckv_ref[...] = (lat * r * g_ref[...]).astype(BF16)
        kr_ref[...] = _rope_fwd_math(p_ref[:, KV_LORA:], c_ref[...], s_ref[...]).astype(BF16)

    tab = pl.BlockSpec((tr, LANES), lambda i: (i, 0))
    return _tc_call(
        body, name="kv_elem_fwd", grid=(t // tr,),
        in_specs=[pl.BlockSpec((tr, KVP), lambda i: (i, 0)), pl.BlockSpec((1, KV_LORA), lambda i: (0, 0)), tab, tab],
        out_specs=[pl.BlockSpec((tr, KV_LORA), lambda i: (i, 0)), tab],
        out_shape=[jax.ShapeDtypeStruct((t, KV_LORA), BF16), jax.ShapeDtypeStruct((t, LANES), BF16)],
        compiler_params=_cp("parallel"),
    )(kvpre, g, cos, sin)


def _kv_elem_bwd(kvpre, g, dckv, dkr, cos, sin):
    t = kvpre.shape[0]
    tr = 512

    def body(p_ref, g_ref, dc_ref, dk_ref, c_ref, s_ref, dp_ref, dg_ref):
        dlat, dg = _rms_bwd_math(p_ref[:, :KV_LORA], g_ref[...], dc_ref[...])
        dp_ref[:, :KV_LORA] = dlat.astype(BF16)
        dp_ref[:, KV_LORA:] = _rope_bwd_math(dk_ref[...], c_ref[...], s_ref[...]).astype(BF16)

        @pl.when(pl.program_id(0) == 0)
        def _():
            dg_ref[...] = jnp.zeros_like(dg_ref)

        dg_ref[...] += dg

    tab = pl.BlockSpec((tr, LANES), lambda i: (i, 0))
    pre = pl.BlockSpec((tr, KVP), lambda i: (i, 0))
    vec = pl.BlockSpec((1, KV_LORA), lambda i: (0, 0))
    return _tc_call(
        body, name="kv_elem_bwd", grid=(t // tr,),
        in_specs=[pre, vec, pl.BlockSpec((tr, KV_LORA), lambda i: (i, 0)), tab, tab, tab],
        out_specs=[pre, vec],
        out_shape=[jax.ShapeDtypeStruct((t, KVP), BF16), jax.ShapeDtypeStruct((1, KV_LORA), F32)],
        compiler_params=_cp("arbitrary"),
    )(kvpre, g, dckv, dkr, cos, sin)


ROW_CHUNK = 64
HALO = 16
WIN = ROW_CHUNK + 16
LANE_HALVES = (slice(0, LANES), slice(LANES, TC))


def _stage(s_ref, p, src):
    t = src.shape[0]
    s_ref[p, :HALO] = jnp.zeros((HALO, TC), BF16)
    s_ref[p, HALO:HALO + t] = src
    s_ref[p, HALO + t:] = jnp.zeros((HALO, TC), BF16)


def _window(s_ref, p, i, lanes):
    base = pl.multiple_of(i * ROW_CHUNK, ROW_CHUNK)
    return s_ref[p, pl.ds(base, ROW_CHUNK + 2 * HALO), lanes].astype(F32)[8:8 + WIN]


def _valid(x):
    return x[8:8 + ROW_CHUNK]


def _prev(x, k):
    return pltpu.roll(x, k, axis=0)


def _next(x, k):
    return pltpu.roll(x, WIN - k, axis=0)


def _taps(w_ref, lanes):
    return w_ref[0:1, lanes], w_ref[1:2, lanes], w_ref[2:3, lanes]


def _fold8(x):
    return jnp.sum(x.reshape(ROW_CHUNK // 8, 8, x.shape[-1]), axis=0)


def _store_rows(ref, idx, i, lanes, x):
    rows = pl.ds(pl.multiple_of(i * ROW_CHUNK, ROW_CHUNK), ROW_CHUNK)
    ref[(*idx, rows, lanes)] = x.astype(ref.dtype)


def _for_chunks(t, chunk):
    def step(i, carry):
        for lanes in LANE_HALVES:
            chunk(i, lanes)
        return carry

    lax.fori_loop(0, t // ROW_CHUNK, step, 0)


def _write_col_sums(acc_ref, outs):
    for k, (ref, row) in enumerate(outs):
        ref[row:row + 1, :] = jnp.sum(acc_ref[k], axis=0, keepdims=True)


def _shift_down(x, k):
    row = lax.broadcasted_iota(jnp.int32, x.shape, 0)
    return jnp.where(row >= k, pltpu.roll(x, k, axis=0), 0.0)


def _shift_up(x, k):
    n = x.shape[0]
    row = lax.broadcasted_iota(jnp.int32, x.shape, 0)
    return jnp.where(row < n - k, pltpu.roll(x, n - k, axis=0), 0.0)


def _conv3(x, w_ref):
    return _shift_down(x, 2) * w_ref[0:1, :] + _shift_down(x, 1) * w_ref[1:2, :] + x * w_ref[2:3, :]


def _col(parts, t):
    if parts is None:
        return pl.BlockSpec((t, TC), lambda j: (0, j))
    return pl.BlockSpec((parts, t, TC), lambda j: (0, 0, j))


def _staging(parts, t):
    return pltpu.VMEM((parts, t + 2 * HALO, TC), BF16)


def _scmix_fwd(z, w):
    t = z.shape[1]

    def body(z_ref, w_ref, m_ref):
        b, c, u = (z_ref[p].astype(F32) for p in range(3))
        m_ref[...] = (b * _conv3(c * u, w_ref)).astype(BF16)

    return _tc_call(
        body, name="scmix_fwd", grid=(D // TC,), in_specs=[_col(3, t), pl.BlockSpec((3, TC), lambda j: (0, j))],
        out_specs=_col(None, t), out_shape=jax.ShapeDtypeStruct((t, D), BF16), compiler_params=_cp("parallel"),
    )(z, w)


def _scmix_bwd(z, w, dm):
    t = z.shape[1]

    def body(z_ref, w_ref, dm_ref, dz_ref, dw_ref, s_ref, acc_ref):
        for p in range(3):
            _stage(s_ref, p, z_ref[p])
        _stage(s_ref, 3, dm_ref[...])
        acc_ref[...] = jnp.zeros_like(acc_ref)

        def chunk(i, lanes):
            w0, w1, w2 = _taps(w_ref, lanes)
            b, c, u, dm = (_window(s_ref, p, i, lanes) for p in range(4))
            cu = c * u
            cu1, cu2 = _prev(cu, 1), _prev(cu, 2)
            _store_rows(dz_ref, (0,), i, lanes, _valid(dm * (cu2 * w0 + cu1 * w1 + cu * w2)))
            dcv = dm * b
            dcu = dcv * w2 + _next(dcv, 1) * w1 + _next(dcv, 2) * w0
            _store_rows(dz_ref, (1,), i, lanes, _valid(dcu * u))
            _store_rows(dz_ref, (2,), i, lanes, _valid(dcu * c))
            for k, shifted in enumerate((cu2, cu1, cu)):
                acc_ref[k, :, lanes] += _fold8(_valid(dcv * shifted))

        _for_chunks(t, chunk)
        _write_col_sums(acc_ref, [(dw_ref, 0), (dw_ref, 1), (dw_ref, 2)])

    wspec = pl.BlockSpec((3, TC), lambda j: (0, j))
    return _tc_call(
        body, name="scmix_bwd", grid=(D // TC,), in_specs=[_col(3, t), wspec, _col(None, t)],
        out_specs=[_col(3, t), wspec],
        out_shape=[jax.ShapeDtypeStruct((3, t, D), BF16), jax.ShapeDtypeStruct((3, D), F32)],
        scratch_shapes=[_staging(4, t), pltpu.VMEM((3, 8, TC), F32)], compiler_params=_cp("parallel"),
    )(z, w, dm)


def _gate_fwd(up, w, bias, name):
    t = up.shape[1]

    def body(u_ref, w_ref, b_ref, a_ref):
        gc = _conv3(u_ref[0].astype(F32), w_ref) + b_ref[...]
        a_ref[...] = (gc * jax.nn.sigmoid(gc) * u_ref[1].astype(F32)).astype(BF16)

    return _tc_call(
        body, name=name, grid=(F_FF // TC,),
        in_specs=[_col(2, t), pl.BlockSpec((3, TC), lambda j: (0, j)), pl.BlockSpec((1, TC), lambda j: (0, j))],
        out_specs=_col(None, t), out_shape=jax.ShapeDtypeStruct((t, F_FF), BF16), compiler_params=_cp("parallel"),
    )(up, w, bias)


def _gate_bwd(up, w, bias, da, name):
    t = up.shape[1]

    def body(u_ref, w_ref, b_ref, da_ref, du_ref, dw_ref, db_ref, s_ref, acc_ref):
        for p in range(2):
            _stage(s_ref, p, u_ref[p])
        _stage(s_ref, 2, da_ref[...])
        acc_ref[...] = jnp.zeros_like(acc_ref)

        def chunk(i, lanes):
            w0, w1, w2 = _taps(w_ref, lanes)
            g, v, da = (_window(s_ref, p, i, lanes) for p in range(3))
            g1, g2 = _prev(g, 1), _prev(g, 2)
            gc = g2 * w0 + g1 * w1 + g * w2 + b_ref[:, lanes]
            sg = jax.nn.sigmoid(gc)
            _store_rows(du_ref, (1,), i, lanes, _valid(da * (gc * sg)))
            dgc = da * v * (sg * (1.0 + gc * (1.0 - sg)))
            _store_rows(du_ref, (0,), i, lanes, _valid(dgc * w2 + _next(dgc, 1) * w1 + _next(dgc, 2) * w0))
            for k, shifted in enumerate((g2, g1, g)):
                acc_ref[k, :, lanes] += _fold8(_valid(dgc * shifted))
            acc_ref[3, :, lanes] += _fold8(_valid(dgc))

        _for_chunks(t, chunk)
        _write_col_sums(acc_ref, [(dw_ref, 0), (dw_ref, 1), (dw_ref, 2), (db_ref, 0)])

    wspec = pl.BlockSpec((3, TC), lambda j: (0, j))
    bspec = pl.BlockSpec((1, TC), lambda j: (0, j))
    return _tc_call(
        body, name=name, grid=(F_FF // TC,), in_specs=[_col(2, t), wspec, bspec, _col(None, t)],
        out_specs=[_col(2, t), wspec, bspec],
        out_shape=[jax.ShapeDtypeStruct((2, t, F_FF), BF16), jax.ShapeDtypeStruct((3, F_FF), F32),
                   jax.ShapeDtypeStruct((1, F_FF), F32)],
        scratch_shapes=[_staging(3, t), pltpu.VMEM((4, 8, TC), F32)], compiler_params=_cp("parallel"),
    )(up, w, bias, da)


ATT_TQ = 256
ATT_SCALE = (QK_NOPE + QK_ROPE) ** -0.5


def _key_ranges(lvl):
    lo = lvl * ATT_TQ
    return ([(0, lo, False)] if lvl else []) + [(lo, lo + ATT_TQ, True)]


def _fill_keys(k_ref, kn_ref, kr_ref):
    @pl.when(pl.program_id(1) == 0)
    def _():
        k_ref[:, :QK_NOPE] = kn_ref[...]
        k_ref[:, QK_NOPE:] = kr_ref[...]


def _attn_probs(q, k_ref, lvl):
    scores = []
    for lo, hi, diagonal in _key_ranges(lvl):
        s = lax.dot_general(q, k_ref[lo:hi, :], NT_DIMS, preferred_element_type=F32) * ATT_SCALE
        if diagonal:
            row = lax.broadcasted_iota(jnp.int32, s.shape, 0)
            col = lax.broadcasted_iota(jnp.int32, s.shape, 1)
            seen = lax.shift_right_logical(col, CHUNK_SHIFT) <= lax.shift_right_logical(row, CHUNK_SHIFT)
            s = jnp.where(seen, s, NEG_INF)
        scores.append(s)
    m = jnp.max(scores[0], axis=1, keepdims=True)
    for s in scores[1:]:
        m = jnp.maximum(m, jnp.max(s, axis=1, keepdims=True))
    ps = [jnp.exp(s - m) for s in scores]
    total = jnp.sum(ps[0], axis=1, keepdims=True)
    for p in ps[1:]:
        total = total + jnp.sum(p, axis=1, keepdims=True)
    inv = 1.0 / total
    return [p * inv for p in ps]


def _per_query_block(qi, n_blocks, branch):
    for lvl in range(n_blocks):
        pl.when(qi == lvl)(lambda lvl=lvl: branch(lvl))


def _attn_specs(t):
    q = pl.BlockSpec((ATT_TQ, HEAD_PAD), lambda h, i: (i, h))
    kn = pl.BlockSpec((None, t, QK_NOPE), lambda h, i: (0, 0, h))
    kr = pl.BlockSpec((t, LANES), lambda h, i: (0, 0))
    v = pl.BlockSpec((None, t, V_HEAD), lambda h, i: (1, 0, h))
    o = pl.BlockSpec((ATT_TQ, V_HEAD), lambda h, i: (i, h))
    return q, kn, kr, v, o


def _attn_fwd(q, knv, kr):
    t = q.shape[0]

    def body(q_ref, kn_ref, kr_ref, v_ref, o_ref, k_ref):
        _fill_keys(k_ref, kn_ref, kr_ref)

        def branch(lvl):
            ps = _attn_probs(q_ref[...], k_ref, lvl)
            o = None
            for p, (lo, hi, _) in zip(ps, _key_ranges(lvl)):
                part = jnp.dot(p.astype(BF16), v_ref[lo:hi, :], preferred_element_type=F32)
                o = part if o is None else o + part
            o_ref[...] = o.astype(BF16)

        _per_query_block(pl.program_id(1), t // ATT_TQ, branch)

    qs, kns, krs, vs, os_ = _attn_specs(t)
    return _tc_call(
        body, name="attn_fwd", grid=(N_HEADS, t // ATT_TQ), in_specs=[qs, kns, krs, vs], out_specs=os_,
        out_shape=jax.ShapeDtypeStruct((t, N_HEADS * V_HEAD), BF16), scratch_shapes=[pltpu.VMEM((t, HEAD_PAD), BF16)],
        compiler_params=_cp("parallel", "arbitrary"),
    )(q, knv, kr, knv)


def _attn_bwd(q, knv, kr, do, cos, sin):
    t = q.shape[0]

    def body(q_ref, kn_ref, kr_ref, v_ref, do_ref, c_ref, s_ref, dq_ref, dknv_ref, dkr_ref, k_ref, dk_ref):
        h, qi = pl.program_id(0), pl.program_id(1)
        _fill_keys(k_ref, kn_ref, kr_ref)

        @pl.when(qi == 0)
        def _():
            dknv_ref[1] = jnp.zeros((t, V_HEAD), F32)
            dk_ref[...] = jnp.zeros_like(dk_ref)

        @pl.when((qi == 0) & (h == 0))
        def _():
            dkr_ref[...] = jnp.zeros_like(dkr_ref)

        def branch(lvl):
            qv, dov = q_ref[...], do_ref[...]
            ranges = _key_ranges(lvl)
            ps = _attn_probs(qv, k_ref, lvl)
            dps = [lax.dot_general(dov, v_ref[lo:hi, :], NT_DIMS, preferred_element_type=F32) for lo, hi, _ in ranges]
            di = None
            for p, dp in zip(ps, dps):
                part = jnp.sum(p * dp, axis=1, keepdims=True)
                di = part if di is None else di + part
            dq = None
            for p, dp, (lo, hi, _) in zip(ps, dps, ranges):
                ds = (p * (dp - di) * ATT_SCALE).astype(BF16)
                part = jnp.dot(ds, k_ref[lo:hi, :], preferred_element_type=F32)
                dq = part if dq is None else dq + part
                dk_ref[lo:hi, :] += lax.dot_general(ds, qv, TN_DIMS, preferred_element_type=F32)
                dknv_ref[1, lo:hi, :] += lax.dot_general(p.astype(BF16), dov, TN_DIMS, preferred_element_type=F32)
            dq_ref[:, :QK_NOPE] = dq[:, :QK_NOPE].astype(BF16)
            dq_ref[:, QK_NOPE:] = _rope_bwd_math(dq[:, QK_NOPE:], c_ref[...], s_ref[...]).astype(BF16)

        _per_query_block(qi, t // ATT_TQ, branch)

        @pl.when(qi == t // ATT_TQ - 1)
        def _():
            dknv_ref[0] = dk_ref[:, :QK_NOPE]
            dkr_ref[...] += dk_ref[:, QK_NOPE:]

    qs, kns, krs, vs, os_ = _attn_specs(t)
    tab = pl.BlockSpec((ATT_TQ, LANES), lambda h, i: (i, 0))
    return _tc_call(
        body, name="attn_bwd", grid=(N_HEADS, t // ATT_TQ), in_specs=[qs, kns, krs, vs, os_, tab, tab],
        out_specs=[qs, pl.BlockSpec((2, t, QK_NOPE), lambda h, i: (0, 0, h)), krs],
        out_shape=[jax.ShapeDtypeStruct((t, N_HEADS * HEAD_PAD), BF16),
                   jax.ShapeDtypeStruct((2, t, N_HEADS * QK_NOPE), F32), jax.ShapeDtypeStruct((t, LANES), F32)],
        scratch_shapes=[pltpu.VMEM((t, HEAD_PAD), BF16), pltpu.VMEM((t, HEAD_PAD), F32)],
        compiler_params=_cp("arbitrary", "arbitrary"),
    )(q, knv, kr, knv, do, cos, sin)


def _adam_math(w, g, m, v):
    nm = ADAM_B1 * m + (1.0 - ADAM_B1) * g
    nv = ADAM_B2 * v + (1.0 - ADAM_B2) * (g * g)
    m_hat = nm / (1.0 - ADAM_B1 ** ADAM_STEP)
    v_hat = nv / (1.0 - ADAM_B2 ** ADAM_STEP)
    return -ADAM_LR * (m_hat / (jnp.sqrt(v_hat) + ADAM_EPS) + ADAM_WD * w), nm, nv


def _adamw_small(w, g, m, v):
    def body(w_ref, g_ref, m_ref, v_ref, d_ref, nm_ref, nv_ref):
        d_ref[...], nm_ref[...], nv_ref[...] = _adam_math(w_ref[...], g_ref[...], m_ref[...], v_ref[...])

    shp = jax.ShapeDtypeStruct(w.shape, F32)
    return _tc_call(body, name="adamw_small", out_shape=[shp] * 3)(w, g, m, v)


ADAM_SPLIT = 4


def _adamw_shards(ids, items, name):
    n = len(items)

    def body(ids_ref, *refs):
        outs = refs[len(refs) - 4 * n:]
        mine = pl.program_id(0) == ids_ref[0]
        for i in range(n):
            w_ref, m_ref, v_ref, gm_ref, gs_ref = refs[5 * i:5 * i + 5]
            g_ref, d_ref, nm_ref, nv_ref = outs[4 * i:4 * i + 4]

            @pl.when(mine)
            def _(g_ref=g_ref, gm_ref=gm_ref):
                g_ref[...] = gm_ref[...]

            @pl.when(jnp.logical_not(mine))
            def _(g_ref=g_ref, gs_ref=gs_ref):
                g_ref[...] = gs_ref[...]

            d_ref[...], nm_ref[...], nv_ref[...] = _adam_math(w_ref[...], g_ref[...], m_ref[...], v_ref[...])

    in_specs, out_specs, out_shape, args, carried, aliases = [], [], [], [ids], [], {}
    for i, it in enumerate(items):
        w = it["w"]
        r, c = w.shape[-2:]
        tr = r // 2 // ADAM_SPLIT
        assert tr % 8 == 0, (name, w.shape)
        layer = it.get("layer")
        if layer is None:
            wspec = pl.BlockSpec((tr, c), lambda h, k, ids: (h * ADAM_SPLIT + k, 0))
        else:
            wspec = pl.BlockSpec((None, tr, c), lambda h, k, ids, layer=layer: (layer, h * ADAM_SPLIT + k, 0))
        gspec = pl.BlockSpec((tr, c), lambda h, k, ids: (k, 0))
        in_specs += [wspec] * 3 + [gspec] * 2
        args += [w, it["m"], it["v"], it["g_mine"], it["g_sib"]]
        out_specs += [wspec] * 4
        out_shape += [jax.ShapeDtypeStruct(w.shape, F32)] * 4
        if it.get("prev") is not None:
            for k, p in enumerate(it["prev"]):
                aliases[1 + 5 * n + len(carried)] = 4 * i + k
                carried.append(p)
    res = _tc_call(
        body, name=name, prefetch=1, grid=(2, ADAM_SPLIT), in_specs=in_specs + [ANY] * len(carried),
        out_specs=out_specs, out_shape=out_shape, input_output_aliases=aliases,
        compiler_params=_cp("parallel", "parallel"),
    )(*args, *carried)
    return [res[4 * i:4 * i + 4] for i in range(n)]


def _peer_chip(k_me, j):
    return k_me ^ jnp.where(j == 0, 2, jnp.where(j == 1, 1, 3))


def _pair_sums(ids, gs, ras, name):
    n = len(gs)

    def body(ids_ref, *refs):
        for i in range(n):
            g_ref, ra_ref, o_ref = refs[2 * i], refs[2 * i + 1], refs[2 * n + i]
            o_ref[...] = (g_ref[...].astype(F32) + ra_ref[...].astype(F32)).astype(BF16)

    in_specs, out_specs, out_shape = [], [], []
    for g in gs:
        half, c = g.shape[1] // 2, g.shape[2]
        in_specs += [pl.BlockSpec((None, half, c), lambda j, ids: (_peer_chip(ids[1], j), ids[0], 0)),
                     pl.BlockSpec((None, half, c), lambda j, ids: (_peer_chip(ids[1], j), 0, 0))]
        out_specs.append(pl.BlockSpec((None, half, c), lambda j, ids: (j, 0, 0)))
        out_shape.append(jax.ShapeDtypeStruct((3, half, c), BF16))
    return _tc_call(
        body, name=name, prefetch=1, grid=(3,), in_specs=in_specs, out_specs=out_specs, out_shape=out_shape,
        compiler_params=_cp("parallel"),
    )(ids, *[a for pair in zip(gs, ras) for a in pair])


def _chip_sums(ids, gs, ras, rbs, name):
    n = len(gs)

    def body(ids_ref, *refs):
        for i in range(n):
            g_ref, ra_ref, rb_ref, o_ref = refs[3 * i], refs[3 * i + 1], refs[3 * i + 2], refs[3 * n + i]
            acc = g_ref[...].astype(F32) + ra_ref[...].astype(F32)
            for j in range(3):
                acc = acc + rb_ref[j].astype(F32)
            o_ref[...] = acc

    in_specs, out_specs, out_shape = [], [], []
    for g in gs:
        half, c = g.shape[1] // 2, g.shape[2]
        in_specs += [pl.BlockSpec((None, half, c), lambda i, ids: (ids[1], ids[0], 0)),
                     pl.BlockSpec((None, half, c), lambda i, ids: (ids[1], 0, 0)),
                     pl.BlockSpec((3, half, c), lambda i, ids: (0, 0, 0))]
        out_specs.append(pl.BlockSpec((half, c), lambda i, ids: (0, 0)))
        out_shape.append(jax.ShapeDtypeStruct((half, c), F32))
    return _tc_call(
        body, name=name, prefetch=1, grid=(1,), in_specs=in_specs, out_specs=out_specs, out_shape=out_shape,
        compiler_params=_cp("arbitrary"),
    )(ids, *[a for trio in zip(gs, ras, rbs) for a in trio])


def _position():
    x, y, c = lax.axis_index("x"), lax.axis_index("y"), lax.axis_index("c")
    chips = [(1 - x, y), (x, 1 - y), (1 - x, 1 - y)]
    return x, y, c, chips


def _shard_half(ref, wm, h):
    if wm.kind == "tiny":
        return ref
    if wm.nl == 2:
        return ref.at[h]
    return ref.at[pl.ds(pl.multiple_of(h * (wm.k // 2), 16), wm.k // 2), :]


def _region(full, wm, s, h):
    if wm.kind == "tiny":
        return full.at[s]
    cols = pl.ds(pl.multiple_of(s * wm.n, LANES), wm.n) if wm.kind == "col" else slice(None)
    if wm.nl == 2:
        rows = pl.ds(pl.multiple_of(s * wm.k, 16), wm.k) if wm.kind == "row" else slice(None)
        return full.at[slice(None) if h is None else h, rows, cols]
    if wm.kind == "col":
        rows = slice(None) if h is None else pl.ds(pl.multiple_of(h * (wm.k // 2), 16), wm.k // 2)
    elif h is None:
        rows = pl.ds(pl.multiple_of(s * wm.k, 16), wm.k)
    else:
        rows = pl.ds(pl.multiple_of(s * wm.k + h * (wm.k // 2), 16), wm.k // 2)
    return full.at[rows, cols]


def _full_shape(wm):
    if wm.kind == "tiny":
        return (N_CHIPS, wm.k, wm.n)
    shape = (wm.k, N_CHIPS * wm.n) if wm.kind == "col" else (N_CHIPS * wm.k, wm.n)
    return shape if wm.nl == 1 else (wm.nl,) + shape


def _handshake(peers):
    barrier = pltpu.get_barrier_semaphore()
    for peer in peers:
        pl.semaphore_signal(barrier, inc=1, device_id=peer, device_id_type=MESH)
    pl.semaphore_wait(barrier, len(peers))


def _all_gather_group(gi, shards):
    wms = AG_GROUPS[gi]
    nw = len(wms)

    def body(*refs):
        sh, full = refs[:nw], refs[nw:2 * nw]
        ici_s, ici_r, pass_s, pass_r, own_s, own_r = refs[2 * nw:]
        x, y, c, chips = _position()
        me, sibling = 2 * x + y, (x, y, 1 - c)
        _handshake([(*chip, c) for chip in chips] + [sibling])

        def rcopy(src, dst, s_sem, r_sem, to):
            return pltpu.make_async_remote_copy(src_ref=src, dst_ref=dst, send_sem=s_sem, recv_sem=r_sem,
                                                device_id=to, device_id_type=MESH)

        started = []
        for i, wm in enumerate(wms):
            for j, chip in enumerate(chips):
                started.append(rcopy(_shard_half(sh[i], wm, c), _region(full[i], wm, me, c),
                                     ici_s.at[i, j], ici_r.at[i, j], (*chip, c)))
                started[-1].start()
            started.append(rcopy(sh[i], _region(full[i], wm, me, None), own_s.at[i], own_r.at[i], sibling))
            started[-1].start()
        for i, wm in enumerate(wms):
            for j, chip in enumerate(chips):
                got = _region(full[i], wm, 2 * chip[0] + chip[1], c)
                rcopy(got, got, ici_s.at[i, j], ici_r.at[i, j], sibling).wait_recv()
                if wm.kind != "tiny":
                    started.append(rcopy(got, got, pass_s.at[i, j], pass_r.at[i, j], sibling))
                    started[-1].start()
        for i, wm in enumerate(wms):
            mine = _region(full[i], wm, me, None)
            rcopy(mine, mine, own_s.at[i], own_r.at[i], sibling).wait_recv()
            for j, chip in enumerate(chips):
                if wm.kind != "tiny":
                    got = _region(full[i], wm, 2 * chip[0] + chip[1], 1 - c)
                    rcopy(got, got, pass_s.at[i, j], pass_r.at[i, j], sibling).wait_recv()
        for cp in started:
            cp.wait_send()

    return pl.kernel(
        body, out_type=[jax.ShapeDtypeStruct(_full_shape(wm), s.dtype) for wm, s in zip(wms, shards)],
        mesh=plsc.ScalarSubcoreMesh(axis_name="sequencer", num_cores=1), name=f"ag_group{gi}",
        scratch_types=[pltpu.SemaphoreType.DMA((nw, 3))] * 4 + [pltpu.SemaphoreType.DMA((nw,))] * 2,
        compiler_params=pltpu.CompilerParams(collective_id=gi),
    )(*shards)


def _sequencer_call(body, name, cid, out_types, scratch, args):
    return pl.kernel(
        body, out_type=out_types, mesh=plsc.ScalarSubcoreMesh(axis_name="sequencer", num_cores=1), name=name,
        scratch_types=scratch, compiler_params=pltpu.CompilerParams(collective_id=cid),
    )(*args)


def _pair_exchange(gs, tag, cid):
    n = len(gs)

    def body(*refs):
        g, out, send_sems, recv_sems = refs[:n], refs[n:2 * n], refs[2 * n], refs[2 * n + 1]
        x, y, c, _ = _position()
        _handshake([(x, y, 1 - c)])
        cps = []
        for i in range(n):
            half = g[i].shape[1] // 2
            cps.append(pltpu.make_async_remote_copy(
                src_ref=g[i].at[:, pl.ds(pl.multiple_of((1 - c) * half, 16), half), :], dst_ref=out[i],
                send_sem=send_sems.at[i], recv_sem=recv_sems.at[i], device_id=(x, y, 1 - c), device_id_type=MESH))
            cps[-1].start()
        for cp in cps:
            cp.wait()

    return _sequencer_call(
        body, f"rs_pair_exchange{tag}", cid,
        [jax.ShapeDtypeStruct((a.shape[0], a.shape[1] // 2, a.shape[2]), a.dtype) for a in gs],
        [pltpu.SemaphoreType.DMA((n,)), pltpu.SemaphoreType.DMA((n,))], gs)


def _chip_exchange(ss, tag, cid):
    n = len(ss)

    def body(*refs):
        s, out, send_sems, recv_sems = refs[:n], refs[n:2 * n], refs[2 * n], refs[2 * n + 1]
        x, y, c, chips = _position()
        _handshake([(*chip, c) for chip in chips])
        cps = []
        for i in range(n):
            for j, chip in enumerate(chips):
                cps.append(pltpu.make_async_remote_copy(
                    src_ref=s[i].at[j], dst_ref=out[i].at[j], send_sem=send_sems.at[i, j], recv_sem=recv_sems.at[i, j],
                    device_id=(*chip, c), device_id_type=MESH))
                cps[-1].start()
        for cp in cps:
            cp.wait()

    return _sequencer_call(
        body, f"rs_chip_exchange{tag}", cid, [jax.ShapeDtypeStruct(a.shape, a.dtype) for a in ss],
        [pltpu.SemaphoreType.DMA((n, 3)), pltpu.SemaphoreType.DMA((n, 3))], ss)


def _pair_swap(g8s, tag, cid):
    n = len(g8s)

    def body(*refs):
        g, out, send_sems, recv_sems = refs[:n], refs[n:2 * n], refs[2 * n], refs[2 * n + 1]
        x, y, c, _ = _position()
        _handshake([(x, y, 1 - c)])
        cps = []
        for i in range(n):
            cps.append(pltpu.make_async_remote_copy(
                src_ref=g[i], dst_ref=out[i], send_sem=send_sems.at[i], recv_sem=recv_sems.at[i],
                device_id=(x, y, 1 - c), device_id_type=MESH))
            cps[-1].start()
        for cp in cps:
            cp.wait()

    return _sequencer_call(
        body, f"rs_pair_swap{tag}", cid, [jax.ShapeDtypeStruct(a.shape, a.dtype) for a in g8s],
        [pltpu.SemaphoreType.DMA((n,)), pltpu.SemaphoreType.DMA((n,))], g8s)


def _all_reduce_small(vec, name):
    r, cols = vec.shape

    def body(v_ref, o_ref, gath, send_sems, recv_sems):
        x, y, c, _ = _position()
        me = 4 * x + 2 * y + c
        gath[me] = v_ref[...]
        cps = []
        for rel in range(1, N_DEV):
            peer = (x ^ (rel >> 2), y ^ ((rel >> 1) & 1), c ^ (rel & 1))
            cps.append(pltpu.make_async_remote_copy(
                src_ref=v_ref, dst_ref=gath.at[me], send_sem=send_sems.at[rel - 1], recv_sem=recv_sems.at[rel - 1],
                device_id=peer, device_id_type=MESH))
        for cp in cps:
            cp.start()
        for rel in range(1, N_DEV):
            pltpu.make_async_remote_copy(
                src_ref=v_ref, dst_ref=gath.at[me ^ rel], send_sem=send_sems.at[rel - 1],
                recv_sem=recv_sems.at[rel - 1], device_id=(x, y, c), device_id_type=MESH).wait_recv()
        for cp in cps:
            cp.wait_send()
        acc = gath[0]
        for d in range(1, N_DEV):
            acc = acc + gath[d]
        o_ref[...] = acc

    vm = pl.BlockSpec(memory_space=pltpu.VMEM)
    return _tc_call(
        body, name=name, in_specs=[vm], out_specs=vm, out_shape=jax.ShapeDtypeStruct((r, cols), F32),
        scratch_shapes=[pltpu.VMEM((N_DEV, r, cols), F32), pltpu.SemaphoreType.DMA((N_DEV - 1,)),
                        pltpu.SemaphoreType.DMA((N_DEV - 1,))],
    )(vec)


def _rope_tables(positions):
    half = QK_ROPE // 2
    inv_freq = 1.0 / (ROPE_THETA ** (jnp.arange(half, dtype=F32) / half))
    ang = positions.astype(F32)[:, None] * inv_freq
    zeros = jnp.zeros((positions.shape[0], LANES - QK_ROPE), F32)
    cos, sin = jnp.cos(ang), jnp.sin(ang)
    return jnp.concatenate([cos, cos, zeros], axis=1), jnp.concatenate([sin, sin, zeros], axis=1)


def _local_step(x, positions, tgt, wf, small, rs):
    cos, sin = _rope_tables(positions)
    w_in, w_out = wf["sc_w_in"], wf["sc_w_out"]
    w_ups, w_downs = (wf["ffn_w_up0"], wf["ffn_w_up1"]), (wf["ffn_w_down0"], wf["ffn_w_down1"])
    w_kv, w_ukv, w_dq, w_uq, w_o = wf["w_kv"], wf["w_ukv"], wf["w_dq"], wf["w_uq"], wf["w_o"]
    attn_norm, ffn_norm = small["attn_norm"], small["ffn_norm"]
    conv_b = small["ffn_conv_b"]

    def ffn_fwd(h, l):
        hf = _rms_fwd(h, ffn_norm[l:l + 1], f"ffn{l}_norm")
        up = _nn_parts(f"ffn{l}_up", hf, w_ups[l], 2, BF16)
        a = _gate_fwd(up, small["ffn_conv_w"][l], conv_b[l:l + 1], f"ffn{l}_gate")
        return _nn(f"ffn{l}_down", a, w_downs[l], F32, add=h), (hf, up, a)

    def ffn_bwd(h, dh_out, dh_out_b, l, saved, gi, hooks):
        run = lambda stage: hooks.get(stage, lambda: None)()
        hf, up, a = saved
        da = _nt(f"ffn{l}_down_dx", dh_out_b, w_downs[l], BF16)
        run("down_dx")
        d_down = _tn(f"ffn{l}_down_dw", a, dh_out_b, BF16)
        dup, d_cw, d_cb = _gate_bwd(up, small["ffn_conv_w"][l], conv_b[l:l + 1], da, f"ffn{l}_gate_bwd")
        run("gate_bwd")
        d_up = _dw_ffn_up(f"ffn{l}_up_dw", hf, dup)
        rs.start(gi, {f"ffn_w_down{l}": d_down.reshape(N_CHIPS, F_FF // N_CHIPS, D), f"ffn_w_up{l}": d_up})
        dhf = _nt_parts(f"ffn{l}_up_dx", dup, w_ups[l], BF16)
        run("up_dx")
        dh, dh_b, d_norm = _rms_bwd(h, ffn_norm[l:l + 1], dhf, dh_out, f"ffn{l}_norm_bwd", matmul_copy=True)
        return dh, dh_b, d_cw, d_cb, d_norm

    hn0 = _rms_fwd(x, attn_norm[0:1], "attn0_norm")
    z = _nn_parts("sc_in", hn0, w_in, 3, BF16)
    mix = _scmix_fwd(z, small["sc_conv_w"])
    h1 = _nn("sc_out", mix, w_out, F32, add=x)
    h2, ffn0_saved = ffn_fwd(h1, 0)

    hk = _rms_fwd(h2, small["kv_in_norm"], "kv_in_norm")
    kvpre = _nn("kv_down", hk, w_kv, F32)
    ckv, kr = _kv_elem_fwd(kvpre, small["kv_latent_norm"], cos, sin)
    knv = _nn_parts("kv_up", ckv, w_ukv, 2, BF16, stacked=True)

    hn1 = _rms_fwd(h2, attn_norm[1:2], "attn1_norm")
    cq_pre = _nn("q_down", hn1, w_dq, F32)
    cq = _rms_fwd(cq_pre, small["q_latent_norm"], "q_latent_norm")
    q = _q_up_rope(cq, w_uq, cos, sin)
    o = _attn_fwd(q, knv, kr)
    h3 = _nn("attn_out", o, w_o, F32, add=h2)
    h4, ffn1_saved = ffn_fwd(h3, 1)

    loss, dh4, dh4_b, d_final = _loss_head(h4, small["final_norm"], tgt)

    rows = D // N_CHIPS
    dh3, dh3_b, d_cw1, d_cb1, d_fn1 = ffn_bwd(h3, dh4, dh4_b, 1, ffn1_saved, 0, {})

    do = _nt("attn_out_dx", dh3_b, w_o, BF16)
    d_wo = _tn("attn_out_dw", o, dh3_b, BF16)
    rs.pair_sums(0)
    dq, dknv, dkr = _attn_bwd(q, knv, kr, do, cos, sin)
    rs.chip_sums(0)
    dcq = _nt("q_up_dx", dq, w_uq, F32)
    d_wuq = _tn("q_up_dw", cq, dq, BF16).reshape(Q_LORA, N_CHIPS, -1).transpose(1, 0, 2)
    dcq_pre, d_qln = _rms_bwd(cq_pre, small["q_latent_norm"], dcq, None, "q_latent_norm_bwd")
    rs.finish(0)
    dhn1 = _nt("q_down_dx", dcq_pre, w_dq, BF16)
    d_wdq = _tn("q_down_dw", hn1, dcq_pre, BF16)
    dh2, d_an1 = _rms_bwd(h2, attn_norm[1:2], dhn1, dh3, "attn1_norm_bwd")

    dckv = _nt_parts("kv_up_dx", dknv, w_ukv, F32, stacked=True)
    d_wukv = _dw_ukv(ckv, dknv)
    dkvpre, d_kvln = _kv_elem_bwd(kvpre, small["kv_latent_norm"], dckv, dkr, cos, sin)
    dhk = _nt("kv_down_dx", dkvpre, w_kv, BF16)
    d_wkv = _tn("kv_down_dw", hk, dkvpre, BF16)
    rs.start(1, {
        "w_o": d_wo.reshape(N_CHIPS, rows, D), "w_uq": d_wuq, "w_dq": d_wdq.reshape(N_CHIPS, rows, Q_LORA),
        "w_ukv": d_wukv.reshape(N_CHIPS, 2 * KV_LORA, -1), "w_kv": d_wkv.reshape(N_CHIPS, rows, KVP),
    })
    dh2, dh2_b, d_kvin = _rms_bwd(h2, small["kv_in_norm"], dhk, dh2, "kv_in_norm_bwd", matmul_copy=True)

    dh1, dh1_b, d_cw0, d_cb0, d_fn0 = ffn_bwd(h1, dh2, dh2_b, 0, ffn0_saved, 2, {
        "down_dx": lambda: rs.pair_sums(1), "gate_bwd": lambda: rs.chip_sums(1), "up_dx": lambda: rs.finish(1)})
    rs.pair_sums(2)

    d_wout = _tn("sc_out_dw", mix, dh1_b, BF16)
    dmix = _nt("sc_out_dx", dh1_b, w_out, BF16)
    dz, d_scw = _scmix_bwd(z, small["sc_conv_w"], dmix)
    d_win = _dw_sc_in(hn0, dz)
    rs.start(3, {"sc_w_out": d_wout.reshape(N_CHIPS, rows, D), "sc_w_in": d_win})
    dhn0 = _nt_parts("sc_in_dx", dz, w_in, BF16)
    dx, d_an0 = _rms_bwd(x, attn_norm[0:1], dhn0, dh1, "attn0_norm_bwd")

    small_g = {
        "attn_norm": jnp.concatenate([d_an0, d_an1]), "ffn_norm": jnp.concatenate([d_fn0, d_fn1]),
        "final_norm": d_final, "kv_in_norm": d_kvin, "kv_latent_norm": d_kvln, "q_latent_norm": d_qln,
        "ffn_conv_b": jnp.concatenate([d_cb0, d_cb1]), "sc_conv_w": d_scw, "ffn_conv_w": jnp.stack([d_cw0, d_cw1]),
    }
    return loss, dx, small_g


RS_GROUPS = (("ffn_w_down1", "ffn_w_up1"), ("w_o", "w_uq", "w_dq", "w_ukv", "w_kv"),
             ("ffn_w_down0", "ffn_w_up0"), ("sc_w_out", "sc_w_in"))


class _ReduceScatter:
    def __init__(self, ids, finish):
        self.ids, self.grads, self.step, self.mine, self.sib, self.finish = ids, {}, {}, {}, {}, finish

    def _cid(self, gi):
        return len(AG_GROUPS) + 3 * gi

    def start(self, gi, grads):
        self.grads.update(grads)
        own = [grads[n] for n in RS_GROUPS[gi]]
        self.step[gi] = (own, _pair_exchange(own, gi, self._cid(gi)))

    def pair_sums(self, gi):
        own, ra = self.step[gi]
        sums = _pair_sums(self.ids, own, ra, f"rs_pair_sums{gi}")
        self.step[gi] = (own, ra, _chip_exchange(sums, gi, self._cid(gi) + 1))

    def chip_sums(self, gi):
        own, ra, rb = self.step[gi]
        mine = _chip_sums(self.ids, own, ra, rb, f"rs_chip_sums{gi}")
        self.mine.update(zip(RS_GROUPS[gi], mine))
        self.sib.update(zip(RS_GROUPS[gi], _pair_swap(mine, gi, self._cid(gi) + 2)))

SMALL_REPL = ("attn_norm", "ffn_norm", "final_norm", "kv_in_norm", "kv_latent_norm", "q_latent_norm", "ffn_conv_b")
SMALL_SHARDED = ("sc_conv_w", "ffn_conv_w")
SMALL_ROWS = 256


def _pad_heads(w_uq):
    per_head = w_uq.reshape(Q_LORA, -1, QK_NOPE + QK_ROPE)
    return jnp.pad(per_head, ((0, 0), (0, 0), (0, HEAD_PAD - QK_NOPE - QK_ROPE))).reshape(Q_LORA, -1)


def _pack_kv(w_dkv, w_kr):
    return jnp.concatenate([w_dkv, w_kr, jnp.zeros((w_kr.shape[0], LANES - QK_ROPE), w_kr.dtype)], axis=1)


def kernel(x, positions, attn_norm, ffn_norm, final_norm, sc_w_in, sc_conv_w, sc_w_out, kv_in_norm, w_dkv, kv_latent_norm, w_kr, w_uk, w_uv, w_dq, q_latent_norm, w_uq, w_o, ffn_w_up, ffn_conv_w, ffn_conv_b, ffn_w_down, loss_target, m_attn_norm, m_ffn_norm, m_final_norm, m_sc_w_in, m_sc_conv_w, m_sc_w_out, m_kv_in_norm, m_w_dkv, m_kv_latent_norm, m_w_kr, m_w_uk, m_w_uv, m_w_dq, m_q_latent_norm, m_w_uq, m_w_o, m_ffn_w_up, m_ffn_conv_w, m_ffn_conv_b, m_ffn_w_down, v_attn_norm, v_ffn_norm, v_final_norm, v_sc_w_in, v_sc_conv_w, v_sc_w_out, v_kv_in_norm, v_w_dkv, v_kv_latent_norm, v_w_kr, v_w_uk, v_w_uv, v_w_dq, v_q_latent_norm, v_w_uq, v_w_o, v_ffn_w_up, v_ffn_conv_w, v_ffn_conv_b, v_ffn_w_down):
    names = ("attn_norm", "ffn_norm", "final_norm", "sc_w_in", "sc_conv_w", "sc_w_out", "kv_in_norm", "w_dkv",
             "kv_latent_norm", "w_kr", "w_uk", "w_uv", "w_dq", "q_latent_norm", "w_uq", "w_o", "ffn_w_up",
             "ffn_conv_w", "ffn_conv_b", "ffn_w_down")
    w = dict(zip(names, (attn_norm, ffn_norm, final_norm, sc_w_in, sc_conv_w, sc_w_out, kv_in_norm, w_dkv,
                         kv_latent_norm, w_kr, w_uk, w_uv, w_dq, q_latent_norm, w_uq, w_o, ffn_w_up,
                         ffn_conv_w, ffn_conv_b, ffn_w_down)))
    m = dict(zip(names, (m_attn_norm, m_ffn_norm, m_final_norm, m_sc_w_in, m_sc_conv_w, m_sc_w_out, m_kv_in_norm,
                         m_w_dkv, m_kv_latent_norm, m_w_kr, m_w_uk, m_w_uv, m_w_dq, m_q_latent_norm, m_w_uq, m_w_o,
                         m_ffn_w_up, m_ffn_conv_w, m_ffn_conv_b, m_ffn_w_down)))
    v = dict(zip(names, (v_attn_norm, v_ffn_norm, v_final_norm, v_sc_w_in, v_sc_conv_w, v_sc_w_out, v_kv_in_norm,
                         v_w_dkv, v_kv_latent_norm, v_w_kr, v_w_uk, v_w_uv, v_w_dq, v_q_latent_norm, v_w_uq, v_w_o,
                         v_ffn_w_up, v_ffn_conv_w, v_ffn_conv_b, v_ffn_w_down)))

    _ORDER[0] = None
    ix, iy, ic = lax.axis_index("x"), lax.axis_index("y"), lax.axis_index("c")
    chip = 2 * ix + iy
    ids = jnp.stack([ic, chip]).astype(jnp.int32)

    def shards_of(t):
        return {
            "sc_w_in": t["sc_w_in"][0], "sc_w_out": t["sc_w_out"][0], "ffn_w_up": t["ffn_w_up"],
            "ffn_w_down": t["ffn_w_down"], "w_kv": _pack_kv(t["w_dkv"], t["w_kr"]),
            "w_ukv": jnp.stack([t["w_uk"], t["w_uv"]]), "w_dq": t["w_dq"][0], "w_uq": _pad_heads(t["w_uq"][0]),
            "w_o": t["w_o"][0],
        }

    ws, ms, vs = shards_of(w), shards_of(m), shards_of(v)

    def ag_shard(name):
        if name == "sc_conv_w":
            return sc_conv_w[0]
        if name == "ffn_conv_w":
            return ffn_conv_w.reshape(6, -1)
        if name[:-1] in ("ffn_w_up", "ffn_w_down"):
            return ws[name[:-1]][int(name[-1])].astype(BF16)
        return ws[name].astype(BF16)

    wf = {}
    for gi, wms in enumerate(AG_GROUPS):
        fulls = _all_gather_group(gi, [ag_shard(wm.name) for wm in wms])
        wf.update({wm.name: f for wm, f in zip(wms, fulls)})
    small = {
        "attn_norm": attn_norm, "ffn_norm": ffn_norm, "final_norm": final_norm[None], "kv_in_norm": kv_in_norm[None],
        "kv_latent_norm": kv_latent_norm[None], "q_latent_norm": q_latent_norm, "ffn_conv_b": ffn_conv_b,
        "sc_conv_w": wf["sc_conv_w"].transpose(1, 0, 2).reshape(3, D),
        "ffn_conv_w": wf["ffn_conv_w"].reshape(N_CHIPS, 2, 3, -1).transpose(1, 2, 0, 3).reshape(2, 3, F_FF),
    }

    res = {}

    merged = lambda a: a.reshape(2 * KV_LORA, -1)

    def adamw_group(gi):
        items = []
        for key in RS_GROUPS[gi]:
            n, layer = (key[:-1], int(key[-1])) if key[:-1] in ("ffn_w_up", "ffn_w_down") else (key, None)
            w_, m_, v_ = (merged(t[n]) for t in (ws, ms, vs)) if n == "w_ukv" else (ws[n], ms[n], vs[n])
            items.append(dict(name=n, w=w_, m=m_, v=v_, g_mine=rs.mine[key], g_sib=rs.sib[key], layer=layer,
                              prev=res.get(n)))
        for it, out in zip(items, _adamw_shards(ids, items, f"adamw_group{gi}")):
            res[it["name"]] = out

    rs = _ReduceScatter(ids, adamw_group)
    loss, dx, small_g = _local_step(x[0], positions[0], loss_target[0], wf, small, rs)

    s_order = SMALL_REPL + SMALL_SHARDED
    flat = jnp.concatenate([small_g[n].reshape(-1) for n in s_order] + [loss.reshape(-1)])
    flat = jnp.pad(flat, (0, SMALL_ROWS * LANES - flat.shape[0])).reshape(SMALL_ROWS, LANES)
    red = _all_reduce_small(flat, "ar_small").reshape(-1)
    sg, off = {}, 0
    for n in s_order:
        sz = small_g[n].size
        sg[n] = red[off:off + sz].reshape(small_g[n].shape)
        off += sz
    loss_out = red[off]
    grads = {n: sg[n].reshape(w[n].shape) for n in SMALL_REPL}
    grads["sc_conv_w"] = lax.dynamic_slice_in_dim(sg["sc_conv_w"], chip * (D // N_CHIPS), D // N_CHIPS, axis=1)[None]
    grads["ffn_conv_w"] = lax.dynamic_slice_in_dim(sg["ffn_conv_w"], chip * (F_FF // N_CHIPS), F_FF // N_CHIPS, axis=2)

    rs.chip_sums(2)
    rs.pair_sums(3)
    rs.finish(2)
    rs.chip_sums(3)
    rs.finish(3)
    outs = [grads, {}, {}, {}]
    for k, dst in enumerate(outs):
        for n in ("sc_w_in", "sc_w_out", "w_dq", "w_o"):
            dst[n] = res[n][k][None]
        unpadded = res["w_uq"][k].reshape(Q_LORA, -1, HEAD_PAD)[:, :, :QK_NOPE + QK_ROPE]
        dst["w_uq"] = unpadded.reshape(w_uq.shape)
        dst["ffn_w_up"], dst["ffn_w_down"] = res["ffn_w_up"][k], res["ffn_w_down"][k]
        dst["w_dkv"], dst["w_kr"] = res["w_kv"][k][:, :KV_LORA], res["w_kv"][k][:, KV_LORA:KV_LORA + QK_ROPE]
        dst["w_uk"], dst["w_uv"] = res["w_ukv"][k][:KV_LORA], res["w_ukv"][k][KV_LORA:]
    grads, delta, new_m, new_v = outs

    small_names = SMALL_REPL + SMALL_SHARDED

    def pack_small(tree):
        return jnp.concatenate([tree[n].reshape(-1) for n in small_names]).reshape(-1, LANES)

    small_res = _adamw_small(pack_small(w), pack_small(grads), pack_small(m), pack_small(v))
    for slab, dst in zip(small_res, (delta, new_m, new_v)):
        f, off = slab.reshape(-1), 0
        for n in small_names:
            dst[n] = f[off:off + w[n].size].reshape(w[n].shape)
            off += w[n].size

    _ORDER[0] = None
    return (loss_out, dx[None], *[grads[n] for n in names], *[delta[n] for n in names],
            *[new_m[n] for n in names], *[new_v[n] for n in names])
```

```python
from typing import NamedTuple

import jax
import jax.numpy as jnp
from jax import lax
from jax.experimental import pallas as pl
from jax.experimental.pallas import tpu as pltpu
from jax.experimental.pallas import tpu_sc as plsc

F32 = jnp.float32
BF16 = jnp.bfloat16

T = 2048
D = 1024
F_FF = 2816
N_HEADS = 8
QK_NOPE = 128
QK_ROPE = 64
V_HEAD = 128
Q_LORA = 384
KV_LORA = 256
CHUNK_SHIFT = 6
ROPE_THETA = 10000.0
EPS = 1e-6
NEG_INF = -1e30
HEAD_PAD = 256
KVP = KV_LORA + 128

ADAM_LR = 0.001
ADAM_B1 = 0.9
ADAM_B2 = 0.999
ADAM_EPS = 1e-08
ADAM_WD = 0.01
ADAM_STEP = 10

N_CHIPS = 4
N_DEV = 8
LANES = 128
TC = 256
V7X_VMEM_LIMIT = 56 * 1024 * 1024

MESH = pl.DeviceIdType.MESH
ANY = pl.BlockSpec(memory_space=pl.ANY)


class _W(NamedTuple):
    name: str
    kind: str
    nl: int
    k: int
    n: int


AG_GROUPS = (
    (_W("sc_w_in", "col", 1, D, 3 * D // N_CHIPS), _W("sc_conv_w", "tiny", 1, 3, D // N_CHIPS),
     _W("ffn_conv_w", "tiny", 1, 6, F_FF // N_CHIPS)),
    (_W("sc_w_out", "row", 1, D // N_CHIPS, D),),
    (_W("ffn_w_up0", "col", 1, D, 2 * F_FF // N_CHIPS),),
    (_W("ffn_w_down0", "row", 1, F_FF // N_CHIPS, D),),
    (_W("w_kv", "row", 1, D // N_CHIPS, KVP), _W("w_ukv", "col", 2, KV_LORA, N_HEADS * QK_NOPE // N_CHIPS),
     _W("w_dq", "row", 1, D // N_CHIPS, Q_LORA),
     _W("w_uq", "col", 1, Q_LORA, N_HEADS * HEAD_PAD // N_CHIPS),
     _W("w_o", "row", 1, N_HEADS * V_HEAD // N_CHIPS, D)),
    (_W("ffn_w_up1", "col", 1, D, 2 * F_FF // N_CHIPS), _W("ffn_w_down1", "row", 1, F_FF // N_CHIPS, D)),
)


def _cp(*sem):
    return pltpu.CompilerParams(dimension_semantics=sem, vmem_limit_bytes=V7X_VMEM_LIMIT)


_ORDER = [None]


def _tc_call(body, *, name, out_shape, in_specs=None, out_specs=None, grid=(), scratch_shapes=(), prefetch=0,
             input_output_aliases=None, compiler_params=None):
    def run(*args):
        specs = [pl.BlockSpec(memory_space=pltpu.VMEM)] * (len(args) - prefetch) if in_specs is None else list(in_specs)
        inner, dep = body, _ORDER[0]
        if dep is not None:
            unread = prefetch + len(specs)
            specs, args = specs + [ANY], (*args, dep)

            def inner(*refs):
                return body(*refs[:unread], *refs[unread + 1:])

        kwargs = dict(name=name, out_shape=out_shape, input_output_aliases=input_output_aliases or {},
                      compiler_params=compiler_params)
        if prefetch:
            kwargs["grid_spec"] = pltpu.PrefetchScalarGridSpec(
                num_scalar_prefetch=prefetch, grid=grid, in_specs=specs, out_specs=out_specs,
                scratch_shapes=scratch_shapes)
        else:
            kwargs.update(grid=grid, in_specs=specs, scratch_shapes=scratch_shapes)
            if out_specs is not None:
                kwargs["out_specs"] = out_specs
        out = pl.pallas_call(inner, **kwargs)(*args)
        _ORDER[0] = out[0] if isinstance(out, (list, tuple)) else out
        return out

    return run


def _tile(n, cands):
    for c in cands:
        if n % c == 0:
            return c
    raise ValueError(f"no tile for {n}")


NN_DIMS = (((1,), (0,)), ((), ()))
NT_DIMS = (((1,), (1,)), ((), ()))
TN_DIMS = (((0,), (0,)), ((), ()))
M_TILES = (1024, 512, 384, 256, 128)
N_TILES = (1408, 1024, 768, 512, 384, 256, 128)
MM_BLOCK_BYTES = 36 * 1024 * 1024


def _fit(m, n, block_bytes, m_tiles=M_TILES, n_tiles=N_TILES):
    for tm in [c for c in m_tiles if m % c == 0]:
        for tn in [c for c in n_tiles if n % c == 0]:
            if 2 * block_bytes(tm, tn) + 4 * tm * tn <= MM_BLOCK_BYTES:
                return tm, tn
    raise ValueError(f"no tiles for {m} x {n}")


def _size(x):
    return x.dtype.itemsize


def _mm(name, a, b, dims, grid, a_spec, b_spec, o_spec, o_sds, add=None, red=None, acc_shape=None):
    n_red = None if red is None else grid[red]

    def body(*refs):
        a_ref, b_ref = refs[0], refs[1]
        add_ref = refs[2] if add is not None else None
        o_ref = refs[3] if add is not None else refs[2]
        part = lax.dot_general(a_ref[...].astype(BF16), b_ref[...].astype(BF16), dims, preferred_element_type=F32)
        if red is None:
            if add is not None:
                part = part + add_ref[...]
            o_ref[...] = part.astype(o_ref.dtype)
            return
        acc_ref = refs[-1]
        r = pl.program_id(red)

        @pl.when(r == 0)
        def _():
            acc_ref[...] = part

        @pl.when(r > 0)
        def _():
            acc_ref[...] += part

        @pl.when(r == n_red - 1)
        def _():
            o_ref[...] = acc_ref[...].astype(o_ref.dtype)

    sem = tuple("arbitrary" if ax == red else "parallel" for ax in range(len(grid)))
    in_specs = [a_spec, b_spec] + ([o_spec] if add is not None else [])
    args = (a, b) + ((add,) if add is not None else ())
    return _tc_call(
        body, name=name, grid=grid, in_specs=in_specs, out_specs=o_spec, out_shape=o_sds,
        scratch_shapes=[] if red is None else [pltpu.VMEM(acc_shape, F32)], compiler_params=_cp(*sem),
    )(*args)


def _nn(name, a, b, out_dtype, add=None, lead=None):
    (m, k), n = a.shape, b.shape[-1]
    osz = jnp.dtype(out_dtype).itemsize + (4 if add is not None else 0)
    tm, tn = _fit(m, n, lambda tm, tn: tm * k * _size(a) + k * tn * _size(b) + tm * tn * osz)
    if lead is None:
        b_spec = pl.BlockSpec((k, tn), lambda i, j: (0, j))
    else:
        b_spec = pl.BlockSpec((None, k, tn), lambda i, j: (lead, 0, j))
    return _mm(name, a, b, NN_DIMS, (m // tm, n // tn), pl.BlockSpec((tm, k), lambda i, j: (i, 0)), b_spec,
               pl.BlockSpec((tm, tn), lambda i, j: (i, j)), jax.ShapeDtypeStruct((m, n), out_dtype), add=add)


def _nn_parts(name, a, b, parts, out_dtype, lead=None, stacked=False):
    m, k = a.shape
    c = b.shape[-1] if stacked else b.shape[-1] // parts
    osz = jnp.dtype(out_dtype).itemsize
    tm, tn = _fit(m, c, lambda tm, tn: tm * k * _size(a) + k * tn * _size(b) + tm * tn * osz)
    nb = c // tn
    if stacked:
        b_spec = pl.BlockSpec((None, k, tn), lambda i, p, j: (p, 0, j))
    elif lead is None:
        b_spec = pl.BlockSpec((k, tn), lambda i, p, j: (0, p * nb + j))
    else:
        b_spec = pl.BlockSpec((None, k, tn), lambda i, p, j: (lead, 0, p * nb + j))
    return _mm(name, a, b, NN_DIMS, (m // tm, parts, nb), pl.BlockSpec((tm, k), lambda i, p, j: (i, 0)), b_spec,
               pl.BlockSpec((None, tm, tn), lambda i, p, j: (p, i, j)), jax.ShapeDtypeStruct((parts, m, c), out_dtype))


def _nt(name, a, b, out_dtype, lead=None):
    (m, k), n = a.shape, b.shape[-2]
    osz = jnp.dtype(out_dtype).itemsize
    tm, tn = _fit(m, n, lambda tm, tn: tm * k * _size(a) + tn * k * _size(b) + tm * tn * osz)
    if lead is None:
        b_spec = pl.BlockSpec((tn, k), lambda i, j: (j, 0))
    else:
        b_spec = pl.BlockSpec((None, tn, k), lambda i, j: (lead, j, 0))
    return _mm(name, a, b, NT_DIMS, (m // tm, n // tn), pl.BlockSpec((tm, k), lambda i, j: (i, 0)), b_spec,
               pl.BlockSpec((tm, tn), lambda i, j: (i, j)), jax.ShapeDtypeStruct((m, n), out_dtype))


def _nt_parts(name, a, b, out_dtype, lead=None, stacked=False):
    parts, m, c = a.shape
    n = b.shape[-2]
    osz = jnp.dtype(out_dtype).itemsize + 2
    tm, tn = _fit(m, n, lambda tm, tn: tm * c * _size(a) + tn * c * _size(b) + tm * tn * osz)
    if stacked:
        b_spec = pl.BlockSpec((None, tn, c), lambda i, j, p: (p, j, 0))
    elif lead is None:
        b_spec = pl.BlockSpec((tn, c), lambda i, j, p: (j, p))
    else:
        b_spec = pl.BlockSpec((None, tn, c), lambda i, j, p: (lead, j, p))
    return _mm(name, a, b, NT_DIMS, (m // tm, n // tn, parts), pl.BlockSpec((None, tm, c), lambda i, j, p: (p, i, 0)),
               b_spec, pl.BlockSpec((tm, tn), lambda i, j, p: (i, j)), jax.ShapeDtypeStruct((m, n), out_dtype),
               red=2, acc_shape=(tm, tn))


def _tn(name, a, b, out_dtype):
    (k, m), n = a.shape, b.shape[1]
    osz = jnp.dtype(out_dtype).itemsize
    tm, tn = _fit(m, n, lambda tm, tn: k * tm * _size(a) + k * tn * _size(b) + tm * tn * osz,
                  m_tiles=(512, 384, 256, 128), n_tiles=(n,) + N_TILES)
    return _mm(name, a, b, TN_DIMS, (m // tm, n // tn), pl.BlockSpec((k, tm), lambda i, j: (0, i)),
               pl.BlockSpec((k, tn), lambda i, j: (0, j)), pl.BlockSpec((tm, tn), lambda i, j: (i, j)),
               jax.ShapeDtypeStruct((m, n), out_dtype))


def _dw_sc_in(hn, dz):
    t, tn, tm = hn.shape[0], TC, 512
    per_part, per_chip = D // tn, 3 * D // N_CHIPS // tn
    return _mm("sc_in_dw", hn, dz, TN_DIMS, (D // tm, 3 * D // tn), pl.BlockSpec((t, tm), lambda i, j: (0, i)),
               pl.BlockSpec((None, t, tn), lambda i, j: (j // per_part, 0, j % per_part)),
               pl.BlockSpec((None, tm, tn), lambda i, j: (j // per_chip, i, j % per_chip)),
               jax.ShapeDtypeStruct((N_CHIPS, D, 3 * D // N_CHIPS), BF16))


def _dw_ffn_up(name, hf, dup):
    t, tm, ns = hf.shape[0], 512, 2 * F_FF // N_CHIPS
    return _mm(name, hf, dup, TN_DIMS, (N_CHIPS, D // tm), pl.BlockSpec((t, tm), lambda s, i: (0, i)),
               pl.BlockSpec((None, t, ns), lambda s, i: (s // 2, 0, s % 2)),
               pl.BlockSpec((None, tm, ns), lambda s, i: (s, i, 0)), jax.ShapeDtypeStruct((N_CHIPS, D, ns), BF16))


def _dw_ukv(ckv, dknv):
    t, ns = ckv.shape[0], N_HEADS * QK_NOPE // N_CHIPS
    return _mm("kv_up_dw", ckv, dknv, TN_DIMS, (2, N_CHIPS), pl.BlockSpec((t, KV_LORA), lambda p, s: (0, 0)),
               pl.BlockSpec((None, t, ns), lambda p, s: (p, 0, s)),
               pl.BlockSpec((None, None, KV_LORA, ns), lambda p, s: (s, p, 0, 0)),
               jax.ShapeDtypeStruct((N_CHIPS, 2, KV_LORA, ns), BF16))


def _rms_fwd(x, g, name):
    t, d = x.shape
    tr = 512

    def body(x_ref, g_ref, o_ref):
        xv = x_ref[...]
        r = lax.rsqrt(jnp.mean(xv * xv, axis=1, keepdims=True) + EPS)
        o_ref[...] = (xv * r * g_ref[...]).astype(o_ref.dtype)

    row = pl.BlockSpec((tr, d), lambda i: (i, 0))
    return _tc_call(
        body, name=name, grid=(t // tr,), in_specs=[row, pl.BlockSpec((1, d), lambda i: (0, 0))],
        out_specs=row, out_shape=jax.ShapeDtypeStruct((t, d), BF16), compiler_params=_cp("parallel"),
    )(x, g)


def _rms_bwd_math(xv, g, dy):
    r = lax.rsqrt(jnp.mean(xv * xv, axis=1, keepdims=True) + EPS)
    xh = xv * r
    gy = dy * g
    dx = r * (gy - xh * jnp.mean(gy * xh, axis=1, keepdims=True))
    dg = jnp.sum(dy * xh, axis=0, keepdims=True)
    return dx, dg


def _rms_bwd(x, g, dy, add, name, matmul_copy=False):
    t, d = x.shape
    tr = 512
    n_in = 3 + (add is not None)

    def body(*refs):
        x_ref, g_ref, dy_ref = refs[:3]
        dx_ref, dg_ref = refs[n_in], refs[-1]
        dx, dg = _rms_bwd_math(x_ref[...], g_ref[...], dy_ref[...].astype(F32))
        if add is not None:
            dx = dx + refs[3][...]
        dx_ref[...] = dx
        if matmul_copy:
            refs[n_in + 1][...] = dx.astype(BF16)

        @pl.when(pl.program_id(0) == 0)
        def _():
            dg_ref[...] = jnp.zeros_like(dg_ref)

        dg_ref[...] += dg

    row = pl.BlockSpec((tr, d), lambda i: (i, 0))
    vec = pl.BlockSpec((1, d), lambda i: (0, 0))
    in_specs = [row, vec, row] + ([row] if add is not None else [])
    args = (x, g, dy) + ((add,) if add is not None else ())
    copies = [jax.ShapeDtypeStruct((t, d), BF16)] if matmul_copy else []
    return _tc_call(
        body, name=name, grid=(t // tr,), in_specs=in_specs, out_specs=[row] * (1 + len(copies)) + [vec],
        out_shape=[jax.ShapeDtypeStruct((t, d), F32)] + copies + [jax.ShapeDtypeStruct((1, d), F32)],
        compiler_params=_cp("arbitrary"),
    )(*args)


def _loss_head(h, g, tgt):
    t, d = h.shape
    tr = 512

    def body(h_ref, g_ref, t_ref, loss_ref, dh_ref, dhb_ref, dg_ref):
        xv = h_ref[...]
        gv = g_ref[...]
        r = lax.rsqrt(jnp.mean(xv * xv, axis=1, keepdims=True) + EPS)
        err = xv * r * gv - t_ref[...]
        part = 0.5 * jnp.sum(jnp.mean(err * err, axis=1, keepdims=True), axis=0, keepdims=True)
        dx, dg = _rms_bwd_math(xv, gv, err * (1.0 / d))
        dh_ref[...] = dx
        dhb_ref[...] = dx.astype(BF16)

        @pl.when(pl.program_id(0) == 0)
        def _():
            dg_ref[...] = jnp.zeros_like(dg_ref)
            loss_ref[...] = jnp.zeros_like(loss_ref)

        dg_ref[...] += dg
        loss_ref[...] += jnp.broadcast_to(part, loss_ref.shape)

    row = pl.BlockSpec((tr, d), lambda i: (i, 0))
    vec = pl.BlockSpec((1, d), lambda i: (0, 0))
    lspec = pl.BlockSpec((1, LANES), lambda i: (0, 0))
    return _tc_call(
        body, name="loss_head", grid=(t // tr,), in_specs=[row, vec, row], out_specs=[lspec, row, row, vec],
        out_shape=[jax.ShapeDtypeStruct((1, LANES), F32), jax.ShapeDtypeStruct((t, d), F32),
                   jax.ShapeDtypeStruct((t, d), BF16), jax.ShapeDtypeStruct((1, d), F32)],
        compiler_params=_cp("arbitrary"),
    )(h, g, tgt)


def _rot_half(x):
    lane = lax.broadcasted_iota(jnp.int32, x.shape, 1)
    return jnp.where((lane % QK_ROPE) < QK_ROPE // 2, -pltpu.roll(x, LANES - 32, axis=1),
                     pltpu.roll(x, 32, axis=1))


def _rope_fwd_math(x, cos, sin):
    return x * cos + _rot_half(x) * sin


def _rope_bwd_math(dy, cos, sin):
    return dy * cos - _rot_half(dy * sin)


def _q_up_rope(cq, w_uq, cos, sin):
    t, k = cq.shape
    w = w_uq.shape[1]
    tr = 256

    def body(a_ref, w_ref, c_ref, s_ref, o_ref):
        av, cv, sv = a_ref[...], c_ref[...], s_ref[...]
        for h in range(N_HEADS):
            lo = h * HEAD_PAD
            qh = jnp.dot(av, w_ref[:, lo:lo + HEAD_PAD], preferred_element_type=F32)
            o_ref[:, lo:lo + QK_NOPE] = qh[:, :QK_NOPE].astype(BF16)
            o_ref[:, lo + QK_NOPE:lo + HEAD_PAD] = _rope_fwd_math(qh[:, QK_NOPE:], cv, sv).astype(BF16)

    tab = pl.BlockSpec((tr, LANES), lambda i: (i, 0))
    return _tc_call(
        body, name="q_up_rope", grid=(t // tr,),
        in_specs=[pl.BlockSpec((tr, k), lambda i: (i, 0)), pl.BlockSpec((k, w), lambda i: (0, 0)), tab, tab],
        out_specs=pl.BlockSpec((tr, w), lambda i: (i, 0)), out_shape=jax.ShapeDtypeStruct((t, w), BF16),
        compiler_params=_cp("parallel"),
    )(cq, w_uq, cos, sin)


def _kv_elem_fwd(kvpre, g, cos, sin):
    t = kvpre.shape[0]
    tr = 512

    def body(p_ref, g_ref, c_ref, s_ref, ckv_ref, kr_ref):
        lat = p_ref[:, :KV_LORA]
        r = lax.rsqrt(jnp.mean(lat * lat, axis=1, keepdims=True) + EPS)
        ckv_ref[...] = (lat * r * g_ref[...]).astype(BF16)
        kr_ref[...] = _rope_fwd_math(p_ref[:, KV_LORA:], c_ref[...], s_ref[...]).astype(BF16)

    tab = pl.BlockSpec((tr, LANES), lambda i: (i, 0))
    return _tc_call(
        body, name="kv_elem_fwd", grid=(t // tr,),
        in_specs=[pl.BlockSpec((tr, KVP), lambda i: (i, 0)), pl.BlockSpec((1, KV_LORA), lambda i: (0, 0)), tab, tab],
        out_specs=[pl.BlockSpec((tr, KV_LORA), lambda i: (i, 0)), tab],
        out_shape=[jax.ShapeDtypeStruct((t, KV_LORA), BF16), jax.ShapeDtypeStruct((t, LANES), BF16)],
        compiler_params=_cp("parallel"),
    )(kvpre, g, cos, sin)


def _kv_elem_bwd(kvpre, g, dckv, dkr, cos, sin):
    t = kvpre.shape[0]
    tr = 512

    def body(p_ref, g_ref, dc_ref, dk_ref, c_ref, s_ref, dp_ref, dg_ref):
        dlat, dg = _rms_bwd_math(p_ref[:, :KV_LORA], g_ref[...], dc_ref[...])
        dp_ref[:, :KV_LORA] = dlat.astype(BF16)
        dp_ref[:, KV_LORA:] = _rope_bwd_math(dk_ref[...], c_ref[...], s_ref[...]).astype(BF16)

        @pl.when(pl.program_id(0) == 0)
        def _():
            dg_ref[...] = jnp.zeros_like(dg_ref)

        dg_ref[...] += dg

    tab = pl.BlockSpec((tr, LANES), lambda i: (i, 0))
    pre = pl.BlockSpec((tr, KVP), lambda i: (i, 0))
    vec = pl.BlockSpec((1, KV_LORA), lambda i: (0, 0))
    return _tc_call(
        body, name="kv_elem_bwd", grid=(t // tr,),
        in_specs=[pre, vec, pl.BlockSpec((tr, KV_LORA), lambda i: (i, 0)), tab, tab, tab],
        out_specs=[pre, vec],
        out_shape=[jax.ShapeDtypeStruct((t, KVP), BF16), jax.ShapeDtypeStruct((1, KV_LORA), F32)],
        compiler_params=_cp("arbitrary"),
    )(kvpre, g, dckv, dkr, cos, sin)


ROW_CHUNK = 64
HALO = 16
WIN = ROW_CHUNK + 16
LANE_HALVES = (slice(0, LANES), slice(LANES, TC))


def _stage(s_ref, p, src):
    t = src.shape[0]
    s_ref[p, :HALO] = jnp.zeros((HALO, TC), BF16)
    s_ref[p, HALO:HALO + t] = src
    s_ref[p, HALO + t:] = jnp.zeros((HALO, TC), BF16)


def _window(s_ref, p, i, lanes):
    base = pl.multiple_of(i * ROW_CHUNK, ROW_CHUNK)
    return s_ref[p, pl.ds(base, ROW_CHUNK + 2 * HALO), lanes].astype(F32)[8:8 + WIN]


def _valid(x):
    return x[8:8 + ROW_CHUNK]


def _prev(x, k):
    return pltpu.roll(x, k, axis=0)


def _next(x, k):
    return pltpu.roll(x, WIN - k, axis=0)


def _taps(w_ref, lanes):
    return w_ref[0:1, lanes], w_ref[1:2, lanes], w_ref[2:3, lanes]


def _fold8(x):
    return jnp.sum(x.reshape(ROW_CHUNK // 8, 8, x.shape[-1]), axis=0)


def _store_rows(ref, idx, i, lanes, x):
    rows = pl.ds(pl.multiple_of(i * ROW_CHUNK, ROW_CHUNK), ROW_CHUNK)
    ref[(*idx, rows, lanes)] = x.astype(ref.dtype)


def _for_chunks(t, chunk):
    def step(i, carry):
        for lanes in LANE_HALVES:
            chunk(i, lanes)
        return carry

    lax.fori_loop(0, t // ROW_CHUNK, step, 0)


def _write_col_sums(acc_ref, outs):
    for k, (ref, row) in enumerate(outs):
        ref[row:row + 1, :] = jnp.sum(acc_ref[k], axis=0, keepdims=True)


def _shift_down(x, k):
    row = lax.broadcasted_iota(jnp.int32, x.shape, 0)
    return jnp.where(row >= k, pltpu.roll(x, k, axis=0), 0.0)


def _shift_up(x, k):
    n = x.shape[0]
    row = lax.broadcasted_iota(jnp.int32, x.shape, 0)
    return jnp.where(row < n - k, pltpu.roll(x, n - k, axis=0), 0.0)


def _conv3(x, w_ref):
    return _shift_down(x, 2) * w_ref[0:1, :] + _shift_down(x, 1) * w_ref[1:2, :] + x * w_ref[2:3, :]


def _col(parts, t):
    if parts is None:
        return pl.BlockSpec((t, TC), lambda j: (0, j))
    return pl.BlockSpec((parts, t, TC), lambda j: (0, 0, j))


def _staging(parts, t):
    return pltpu.VMEM((parts, t + 2 * HALO, TC), BF16)


def _scmix_fwd(z, w):
    t = z.shape[1]

    def body(z_ref, w_ref, m_ref):
        b, c, u = (z_ref[p].astype(F32) for p in range(3))
        m_ref[...] = (b * _conv3(c * u, w_ref)).astype(BF16)

    return _tc_call(
        body, name="scmix_fwd", grid=(D // TC,), in_specs=[_col(3, t), pl.BlockSpec((3, TC), lambda j: (0, j))],
        out_specs=_col(None, t), out_shape=jax.ShapeDtypeStruct((t, D), BF16), compiler_params=_cp("parallel"),
    )(z, w)


def _scmix_bwd(z, w, dm):
    t = z.shape[1]

    def body(z_ref, w_ref, dm_ref, dz_ref, dw_ref, s_ref, acc_ref):
        for p in range(3):
            _stage(s_ref, p, z_ref[p])
        _stage(s_ref, 3, dm_ref[...])
        acc_ref[...] = jnp.zeros_like(acc_ref)

        def chunk(i, lanes):
            w0, w1, w2 = _taps(w_ref, lanes)
            b, c, u, dm = (_window(s_ref, p, i, lanes) for p in range(4))
            cu = c * u
            cu1, cu2 = _prev(cu, 1), _prev(cu, 2)
            _store_rows(dz_ref, (0,), i, lanes, _valid(dm * (cu2 * w0 + cu1 * w1 + cu * w2)))
            dcv = dm * b
            dcu = dcv * w2 + _next(dcv, 1) * w1 + _next(dcv, 2) * w0
            _store_rows(dz_ref, (1,), i, lanes, _valid(dcu * u))
            _store_rows(dz_ref, (2,), i, lanes, _valid(dcu * c))
            for k, shifted in enumerate((cu2, cu1, cu)):
                acc_ref[k, :, lanes] += _fold8(_valid(dcv * shifted))

        _for_chunks(t, chunk)
        _write_col_sums(acc_ref, [(dw_ref, 0), (dw_ref, 1), (dw_ref, 2)])

    wspec = pl.BlockSpec((3, TC), lambda j: (0, j))
    return _tc_call(
        body, name="scmix_bwd", grid=(D // TC,), in_specs=[_col(3, t), wspec, _col(None, t)],
        out_specs=[_col(3, t), wspec],
        out_shape=[jax.ShapeDtypeStruct((3, t, D), BF16), jax.ShapeDtypeStruct((3, D), F32)],
        scratch_shapes=[_staging(4, t), pltpu.VMEM((3, 8, TC), F32)], compiler_params=_cp("parallel"),
    )(z, w, dm)


def _gate_fwd(up, w, bias, name):
    t = up.shape[1]

    def body(u_ref, w_ref, b_ref, a_ref):
        gc = _conv3(u_ref[0].astype(F32), w_ref) + b_ref[...]
        a_ref[...] = (gc * jax.nn.sigmoid(gc) * u_ref[1].astype(F32)).astype(BF16)

    return _tc_call(
        body, name=name, grid=(F_FF // TC,),
        in_specs=[_col(2, t), pl.BlockSpec((3, TC), lambda j: (0, j)), pl.BlockSpec((1, TC), lambda j: (0, j))],
        out_specs=_col(None, t), out_shape=jax.ShapeDtypeStruct((t, F_FF), BF16), compiler_params=_cp("parallel"),
    )(up, w, bias)


def _gate_bwd(up, w, bias, da, name):
    t = up.shape[1]

    def body(u_ref, w_ref, b_ref, da_ref, du_ref, dw_ref, db_ref, s_ref, acc_ref):
        for p in range(2):
            _stage(s_ref, p, u_ref[p])
        _stage(s_ref, 2, da_ref[...])
        acc_ref[...] = jnp.zeros_like(acc_ref)

        def chunk(i, lanes):
            w0, w1, w2 = _taps(w_ref, lanes)
            g, v, da = (_window(s_ref, p, i, lanes) for p in range(3))
            g1, g2 = _prev(g, 1), _prev(g, 2)
            gc = g2 * w0 + g1 * w1 + g * w2 + b_ref[:, lanes]
            sg = jax.nn.sigmoid(gc)
            _store_rows(du_ref, (1,), i, lanes, _valid(da * (gc * sg)))
            dgc = da * v * (sg * (1.0 + gc * (1.0 - sg)))
            _store_rows(du_ref, (0,), i, lanes, _valid(dgc * w2 + _next(dgc, 1) * w1 + _next(dgc, 2) * w0))
            for k, shifted in enumerate((g2, g1, g)):
                acc_ref[k, :, lanes] += _fold8(_valid(dgc * shifted))
            acc_ref[3, :, lanes] += _fold8(_valid(dgc))

        _for_chunks(t, chunk)
        _write_col_sums(acc_ref, [(dw_ref, 0), (dw_ref, 1), (dw_ref, 2), (db_ref, 0)])

    wspec = pl.BlockSpec((3, TC), lambda j: (0, j))
    bspec = pl.BlockSpec((1, TC), lambda j: (0, j))
    return _tc_call(
        body, name=name, grid=(F_FF // TC,), in_specs=[_col(2, t), wspec, bspec, _col(None, t)],
        out_specs=[_col(2, t), wspec, bspec],
        out_shape=[jax.ShapeDtypeStruct((2, t, F_FF), BF16), jax.ShapeDtypeStruct((3, F_FF), F32),
                   jax.ShapeDtypeStruct((1, F_FF), F32)],
        scratch_shapes=[_staging(3, t), pltpu.VMEM((4, 8, TC), F32)], compiler_params=_cp("parallel"),
    )(up, w, bias, da)


ATT_TQ = 256
ATT_SCALE = (QK_NOPE + QK_ROPE) ** -0.5


def _key_ranges(lvl):
    lo = lvl * ATT_TQ
    return ([(0, lo, False)] if lvl else []) + [(lo, lo + ATT_TQ, True)]


def _fill_keys(k_ref, kn_ref, kr_ref):
    @pl.when(pl.program_id(1) == 0)
    def _():
        k_ref[:, :QK_NOPE] = kn_ref[...]
        k_ref[:, QK_NOPE:] = kr_ref[...]


def _attn_probs(q, k_ref, lvl):
    scores = []
    for lo, hi, diagonal in _key_ranges(lvl):
        s = lax.dot_general(q, k_ref[lo:hi, :], NT_DIMS, preferred_element_type=F32) * ATT_SCALE
        if diagonal:
            row = lax.broadcasted_iota(jnp.int32, s.shape, 0)
            col = lax.broadcasted_iota(jnp.int32, s.shape, 1)
            seen = lax.shift_right_logical(col, CHUNK_SHIFT) <= lax.shift_right_logical(row, CHUNK_SHIFT)
            s = jnp.where(seen, s, NEG_INF)
        scores.append(s)
    m = jnp.max(scores[0], axis=1, keepdims=True)
    for s in scores[1:]:
        m = jnp.maximum(m, jnp.max(s, axis=1, keepdims=True))
    ps = [jnp.exp(s - m) for s in scores]
    total = jnp.sum(ps[0], axis=1, keepdims=True)
    for p in ps[1:]:
        total = total + jnp.sum(p, axis=1, keepdims=True)
    inv = 1.0 / total
    return [p * inv for p in ps]


def _per_query_block(qi, n_blocks, branch):
    for lvl in range(n_blocks):
        pl.when(qi == lvl)(lambda lvl=lvl: branch(lvl))


def _attn_specs(t):
    q = pl.BlockSpec((ATT_TQ, HEAD_PAD), lambda h, i: (i, h))
    kn = pl.BlockSpec((None, t, QK_NOPE), lambda h, i: (0, 0, h))
    kr = pl.BlockSpec((t, LANES), lambda h, i: (0, 0))
    v = pl.BlockSpec((None, t, V_HEAD), lambda h, i: (1, 0, h))
    o = pl.BlockSpec((ATT_TQ, V_HEAD), lambda h, i: (i, h))
    return q, kn, kr, v, o


def _attn_fwd(q, knv, kr):
    t = q.shape[0]

    def body(q_ref, kn_ref, kr_ref, v_ref, o_ref, k_ref):
        _fill_keys(k_ref, kn_ref, kr_ref)

        def branch(lvl):
            ps = _attn_probs(q_ref[...], k_ref, lvl)
            o = None
            for p, (lo, hi, _) in zip(ps, _key_ranges(lvl)):
                part = jnp.dot(p.astype(BF16), v_ref[lo:hi, :], preferred_element_type=F32)
                o = part if o is None else o + part
            o_ref[...] = o.astype(BF16)

        _per_query_block(pl.program_id(1), t // ATT_TQ, branch)

    qs, kns, krs, vs, os_ = _attn_specs(t)
    return _tc_call(
        body, name="attn_fwd", grid=(N_HEADS, t // ATT_TQ), in_specs=[qs, kns, krs, vs], out_specs=os_,
        out_shape=jax.ShapeDtypeStruct((t, N_HEADS * V_HEAD), BF16), scratch_shapes=[pltpu.VMEM((t, HEAD_PAD), BF16)],
        compiler_params=_cp("parallel", "arbitrary"),
    )(q, knv, kr, knv)


def _attn_bwd(q, knv, kr, do, cos, sin):
    t = q.shape[0]

    def body(q_ref, kn_ref, kr_ref, v_ref, do_ref, c_ref, s_ref, dq_ref, dknv_ref, dkr_ref, k_ref, dk_ref):
        h, qi = pl.program_id(0), pl.program_id(1)
        _fill_keys(k_ref, kn_ref, kr_ref)

        @pl.when(qi == 0)
        def _():
            dknv_ref[1] = jnp.zeros((t, V_HEAD), F32)
            dk_ref[...] = jnp.zeros_like(dk_ref)

        @pl.when((qi == 0) & (h == 0))
        def _():
            dkr_ref[...] = jnp.zeros_like(dkr_ref)

        def branch(lvl):
            qv, dov = q_ref[...], do_ref[...]
            ranges = _key_ranges(lvl)
            ps = _attn_probs(qv, k_ref, lvl)
            dps = [lax.dot_general(dov, v_ref[lo:hi, :], NT_DIMS, preferred_element_type=F32) for lo, hi, _ in ranges]
            di = None
            for p, dp in zip(ps, dps):
                part = jnp.sum(p * dp, axis=1, keepdims=True)
                di = part if di is None else di + part
            dq = None
            for p, dp, (lo, hi, _) in zip(ps, dps, ranges):
                ds = (p * (dp - di) * ATT_SCALE).astype(BF16)
                part = jnp.dot(ds, k_ref[lo:hi, :], preferred_element_type=F32)
                dq = part if dq is None else dq + part
                dk_ref[lo:hi, :] += lax.dot_general(ds, qv, TN_DIMS, preferred_element_type=F32)
                dknv_ref[1, lo:hi, :] += lax.dot_general(p.astype(BF16), dov, TN_DIMS, preferred_element_type=F32)
            dq_ref[:, :QK_NOPE] = dq[:, :QK_NOPE].astype(BF16)
            dq_ref[:, QK_NOPE:] = _rope_bwd_math(dq[:, QK_NOPE:], c_ref[...], s_ref[...]).astype(BF16)

        _per_query_block(qi, t // ATT_TQ, branch)

        @pl.when(qi == t // ATT_TQ - 1)
        def _():
            dknv_ref[0] = dk_ref[:, :QK_NOPE]
            dkr_ref[...] += dk_ref[:, QK_NOPE:]

    qs, kns, krs, vs, os_ = _attn_specs(t)
    tab = pl.BlockSpec((ATT_TQ, LANES), lambda h, i: (i, 0))
    return _tc_call(
        body, name="attn_bwd", grid=(N_HEADS, t // ATT_TQ), in_specs=[qs, kns, krs, vs, os_, tab, tab],
        out_specs=[qs, pl.BlockSpec((2, t, QK_NOPE), lambda h, i: (0, 0, h)), krs],
        out_shape=[jax.ShapeDtypeStruct((t, N_HEADS * HEAD_PAD), BF16),
                   jax.ShapeDtypeStruct((2, t, N_HEADS * QK_NOPE), F32), jax.ShapeDtypeStruct((t, LANES), F32)],
        scratch_shapes=[pltpu.VMEM((t, HEAD_PAD), BF16), pltpu.VMEM((t, HEAD_PAD), F32)],
        compiler_params=_cp("arbitrary", "arbitrary"),
    )(q, knv, kr, knv, do, cos, sin)


def _adam_math(w, g, m, v):
    nm = ADAM_B1 * m + (1.0 - ADAM_B1) * g
    nv = ADAM_B2 * v + (1.0 - ADAM_B2) * (g * g)
    m_hat = nm / (1.0 - ADAM_B1 ** ADAM_STEP)
    v_hat = nv / (1.0 - ADAM_B2 ** ADAM_STEP)
    return -ADAM_LR * (m_hat / (jnp.sqrt(v_hat) + ADAM_EPS) + ADAM_WD * w), nm, nv


def _adamw_small(w, g, m, v):
    def body(w_ref, g_ref, m_ref, v_ref, d_ref, nm_ref, nv_ref):
        d_ref[...], nm_ref[...], nv_ref[...] = _adam_math(w_ref[...], g_ref[...], m_ref[...], v_ref[...])

    shp = jax.ShapeDtypeStruct(w.shape, F32)
    return _tc_call(body, name="adamw_small", out_shape=[shp] * 3)(w, g, m, v)


ADAM_SPLIT = 4


def _adamw_shards(ids, items, name):
    n = len(items)

    def body(ids_ref, *refs):
        outs = refs[len(refs) - 4 * n:]
        mine = pl.program_id(0) == ids_ref[0]
        for i in range(n):
            w_ref, m_ref, v_ref, gm_ref, gs_ref = refs[5 * i:5 * i + 5]
            g_ref, d_ref, nm_ref, nv_ref = outs[4 * i:4 * i + 4]

            @pl.when(mine)
            def _(g_ref=g_ref, gm_ref=gm_ref):
                g_ref[...] = gm_ref[...]

            @pl.when(jnp.logical_not(mine))
            def _(g_ref=g_ref, gs_ref=gs_ref):
                g_ref[...] = gs_ref[...]

            d_ref[...], nm_ref[...], nv_ref[...] = _adam_math(w_ref[...], g_ref[...], m_ref[...], v_ref[...])

    in_specs, out_specs, out_shape, args, carried, aliases = [], [], [], [ids], [], {}
    for i, it in enumerate(items):
        w = it["w"]
        r, c = w.shape[-2:]
        tr = r // 2 // ADAM_SPLIT
        assert tr % 8 == 0, (name, w.shape)
        layer = it.get("layer")
        if layer is None:
            wspec = pl.BlockSpec((tr, c), lambda h, k, ids: (h * ADAM_SPLIT + k, 0))
        else:
            wspec = pl.BlockSpec((None, tr, c), lambda h, k, ids, layer=layer: (layer, h * ADAM_SPLIT + k, 0))
        gspec = pl.BlockSpec((tr, c), lambda h, k, ids: (k, 0))
        in_specs += [wspec] * 3 + [gspec] * 2
        args += [w, it["m"], it["v"], it["g_mine"], it["g_sib"]]
        out_specs += [wspec] * 4
        out_shape += [jax.ShapeDtypeStruct(w.shape, F32)] * 4
        if it.get("prev") is not None:
            for k, p in enumerate(it["prev"]):
                aliases[1 + 5 * n + len(carried)] = 4 * i + k
                carried.append(p)
    res = _tc_call(
        body, name=name, prefetch=1, grid=(2, ADAM_SPLIT), in_specs=in_specs + [ANY] * len(carried),
        out_specs=out_specs, out_shape=out_shape, input_output_aliases=aliases,
        compiler_params=_cp("parallel", "parallel"),
    )(*args, *carried)
    return [res[4 * i:4 * i + 4] for i in range(n)]


def _peer_chip(k_me, j):
    return k_me ^ jnp.where(j == 0, 2, jnp.where(j == 1, 1, 3))


def _pair_sums(ids, gs, ras, name):
    n = len(gs)

    def body(ids_ref, *refs):
        for i in range(n):
            g_ref, ra_ref, o_ref = refs[2 * i], refs[2 * i + 1], refs[2 * n + i]
            o_ref[...] = (g_ref[...].astype(F32) + ra_ref[...].astype(F32)).astype(BF16)

    in_specs, out_specs, out_shape = [], [], []
    for g in gs:
        half, c = g.shape[1] // 2, g.shape[2]
        in_specs += [pl.BlockSpec((None, half, c), lambda j, ids: (_peer_chip(ids[1], j), ids[0], 0)),
                     pl.BlockSpec((None, half, c), lambda j, ids: (_peer_chip(ids[1], j), 0, 0))]
        out_specs.append(pl.BlockSpec((None, half, c), lambda j, ids: (j, 0, 0)))
        out_shape.append(jax.ShapeDtypeStruct((3, half, c), BF16))
    return _tc_call(
        body, name=name, prefetch=1, grid=(3,), in_specs=in_specs, out_specs=out_specs, out_shape=out_shape,
        compiler_params=_cp("parallel"),
    )(ids, *[a for pair in zip(gs, ras) for a in pair])


def _chip_sums(ids, gs, ras, rbs, name):
    n = len(gs)

    def body(ids_ref, *refs):
        for i in range(n):
            g_ref, ra_ref, rb_ref, o_ref = refs[3 * i], refs[3 * i + 1], refs[3 * i + 2], refs[3 * n + i]
            acc = g_ref[...].astype(F32) + ra_ref[...].astype(F32)
            for j in range(3):
                acc = acc + rb_ref[j].astype(F32)
            o_ref[...] = acc

    in_specs, out_specs, out_shape = [], [], []
    for g in gs:
        half, c = g.shape[1] // 2, g.shape[2]
        in_specs += [pl.BlockSpec((None, half, c), lambda i, ids: (ids[1], ids[0], 0)),
                     pl.BlockSpec((None, half, c), lambda i, ids: (ids[1], 0, 0)),
                     pl.BlockSpec((3, half, c), lambda i, ids: (0, 0, 0))]
        out_specs.append(pl.BlockSpec((half, c), lambda i, ids: (0, 0)))
        out_shape.append(jax.ShapeDtypeStruct((half, c), F32))
    return _tc_call(
        body, name=name, prefetch=1, grid=(1,), in_specs=in_specs, out_specs=out_specs, out_shape=out_shape,
        compiler_params=_cp("arbitrary"),
    )(ids, *[a for trio in zip(gs, ras, rbs) for a in trio])


def _position():
    x, y, c = lax.axis_index("x"), lax.axis_index("y"), lax.axis_index("c")
    chips = [(1 - x, y), (x, 1 - y), (1 - x, 1 - y)]
    return x, y, c, chips


def _shard_half(ref, wm, h):
    if wm.kind == "tiny":
        return ref
    if wm.nl == 2:
        return ref.at[h]
    return ref.at[pl.ds(pl.multiple_of(h * (wm.k // 2), 16), wm.k // 2), :]


def _region(full, wm, s, h):
    if wm.kind == "tiny":
        return full.at[s]
    cols = pl.ds(pl.multiple_of(s * wm.n, LANES), wm.n) if wm.kind == "col" else slice(None)
    if wm.nl == 2:
        rows = pl.ds(pl.multiple_of(s * wm.k, 16), wm.k) if wm.kind == "row" else slice(None)
        return full.at[slice(None) if h is None else h, rows, cols]
    if wm.kind == "col":
        rows = slice(None) if h is None else pl.ds(pl.multiple_of(h * (wm.k // 2), 16), wm.k // 2)
    elif h is None:
        rows = pl.ds(pl.multiple_of(s * wm.k, 16), wm.k)
    else:
        rows = pl.ds(pl.multiple_of(s * wm.k + h * (wm.k // 2), 16), wm.k // 2)
    return full.at[rows, cols]


def _full_shape(wm):
    if wm.kind == "tiny":
        return (N_CHIPS, wm.k, wm.n)
    shape = (wm.k, N_CHIPS * wm.n) if wm.kind == "col" else (N_CHIPS * wm.k, wm.n)
    return shape if wm.nl == 1 else (wm.nl,) + shape


def _handshake(peers):
    barrier = pltpu.get_barrier_semaphore()
    for peer in peers:
        pl.semaphore_signal(barrier, inc=1, device_id=peer, device_id_type=MESH)
    pl.semaphore_wait(barrier, len(peers))


def _all_gather_group(gi, shards):
    wms = AG_GROUPS[gi]
    nw = len(wms)

    def body(*refs):
        sh, full = refs[:nw], refs[nw:2 * nw]
        ici_s, ici_r, pass_s, pass_r, own_s, own_r = refs[2 * nw:]
        x, y, c, chips = _position()
        me, sibling = 2 * x + y, (x, y, 1 - c)
        _handshake([(*chip, c) for chip in chips] + [sibling])

        def rcopy(src, dst, s_sem, r_sem, to):
            return pltpu.make_async_remote_copy(src_ref=src, dst_ref=dst, send_sem=s_sem, recv_sem=r_sem,
                                                device_id=to, device_id_type=MESH)

        started = []
        for i, wm in enumerate(wms):
            for j, chip in enumerate(chips):
                started.append(rcopy(_shard_half(sh[i], wm, c), _region(full[i], wm, me, c),
                                     ici_s.at[i, j], ici_r.at[i, j], (*chip, c)))
                started[-1].start()
            started.append(rcopy(sh[i], _region(full[i], wm, me, None), own_s.at[i], own_r.at[i], sibling))
            started[-1].start()
        for i, wm in enumerate(wms):
            for j, chip in enumerate(chips):
                got = _region(full[i], wm, 2 * chip[0] + chip[1], c)
                rcopy(got, got, ici_s.at[i, j], ici_r.at[i, j], sibling).wait_recv()
                if wm.kind != "tiny":
                    started.append(rcopy(got, got, pass_s.at[i, j], pass_r.at[i, j], sibling))
                    started[-1].start()
        for i, wm in enumerate(wms):
            mine = _region(full[i], wm, me, None)
            rcopy(mine, mine, own_s.at[i], own_r.at[i], sibling).wait_recv()
            for j, chip in enumerate(chips):
                if wm.kind != "tiny":
                    got = _region(full[i], wm, 2 * chip[0] + chip[1], 1 - c)
                    rcopy(got, got, pass_s.at[i, j], pass_r.at[i, j], sibling).wait_recv()
        for cp in started:
            cp.wait_send()

    return pl.kernel(
        body, out_type=[jax.ShapeDtypeStruct(_full_shape(wm), s.dtype) for wm, s in zip(wms, shards)],
        mesh=plsc.ScalarSubcoreMesh(axis_name="sequencer", num_cores=1), name=f"ag_group{gi}",
        scratch_types=[pltpu.SemaphoreType.DMA((nw, 3))] * 4 + [pltpu.SemaphoreType.DMA((nw,))] * 2,
        compiler_params=pltpu.CompilerParams(collective_id=gi),
    )(*shards)


def _sequencer_call(body, name, cid, out_types, scratch, args):
    return pl.kernel(
        body, out_type=out_types, mesh=plsc.ScalarSubcoreMesh(axis_name="sequencer", num_cores=1), name=name,
        scratch_types=scratch, compiler_params=pltpu.CompilerParams(collective_id=cid),
    )(*args)


def _pair_exchange(gs, tag, cid):
    n = len(gs)

    def body(*refs):
        g, out, send_sems, recv_sems = refs[:n], refs[n:2 * n], refs[2 * n], refs[2 * n + 1]
        x, y, c, _ = _position()
        _handshake([(x, y, 1 - c)])
        cps = []
        for i in range(n):
            half = g[i].shape[1] // 2
            cps.append(pltpu.make_async_remote_copy(
                src_ref=g[i].at[:, pl.ds(pl.multiple_of((1 - c) * half, 16), half), :], dst_ref=out[i],
                send_sem=send_sems.at[i], recv_sem=recv_sems.at[i], device_id=(x, y, 1 - c), device_id_type=MESH))
            cps[-1].start()
        for cp in cps:
            cp.wait()

    return _sequencer_call(
        body, f"rs_pair_exchange{tag}", cid,
        [jax.ShapeDtypeStruct((a.shape[0], a.shape[1] // 2, a.shape[2]), a.dtype) for a in gs],
        [pltpu.SemaphoreType.DMA((n,)), pltpu.SemaphoreType.DMA((n,))], gs)


def _chip_exchange(ss, tag, cid):
    n = len(ss)

    def body(*refs):
        s, out, send_sems, recv_sems = refs[:n], refs[n:2 * n], refs[2 * n], refs[2 * n + 1]
        x, y, c, chips = _position()
        _handshake([(*chip, c) for chip in chips])
        cps = []
        for i in range(n):
            for j, chip in enumerate(chips):
                cps.append(pltpu.make_async_remote_copy(
                    src_ref=s[i].at[j], dst_ref=out[i].at[j], send_sem=send_sems.at[i, j], recv_sem=recv_sems.at[i, j],
                    device_id=(*chip, c), device_id_type=MESH))
                cps[-1].start()
        for cp in cps:
            cp.wait()

    return _sequencer_call(
        body, f"rs_chip_exchange{tag}", cid, [jax.ShapeDtypeStruct(a.shape, a.dtype) for a in ss],
        [pltpu.SemaphoreType.DMA((n, 3)), pltpu.SemaphoreType.DMA((n, 3))], ss)


def _pair_swap(g8s, tag, cid):
    n = len(g8s)

    def body(*refs):
        g, out, send_sems, recv_sems = refs[:n], refs[n:2 * n], refs[2 * n], refs[2 * n + 1]
        x, y, c, _ = _position()
        _handshake([(x, y, 1 - c)])
        cps = []
        for i in range(n):
            cps.append(pltpu.make_async_remote_copy(
                src_ref=g[i], dst_ref=out[i], send_sem=send_sems.at[i], recv_sem=recv_sems.at[i],
                device_id=(x, y, 1 - c), device_id_type=MESH))
            cps[-1].start()
        for cp in cps:
            cp.wait()

    return _sequencer_call(
        body, f"rs_pair_swap{tag}", cid, [jax.ShapeDtypeStruct(a.shape, a.dtype) for a in g8s],
        [pltpu.SemaphoreType.DMA((n,)), pltpu.SemaphoreType.DMA((n,))], g8s)


def _all_reduce_small(vec, name):
    r, cols = vec.shape

    def body(v_ref, o_ref, gath, send_sems, recv_sems):
        x, y, c, _ = _position()
        me = 4 * x + 2 * y + c
        gath[me] = v_ref[...]
        cps = []
        for rel in range(1, N_DEV):
            peer = (x ^ (rel >> 2), y ^ ((rel >> 1) & 1), c ^ (rel & 1))
            cps.append(pltpu.make_async_remote_copy(
                src_ref=v_ref, dst_ref=gath.at[me], send_sem=send_sems.at[rel - 1], recv_sem=recv_sems.at[rel - 1],
                device_id=peer, device_id_type=MESH))
        for cp in cps:
            cp.start()
        for rel in range(1, N_DEV):
            pltpu.make_async_remote_copy(
                src_ref=v_ref, dst_ref=gath.at[me ^ rel], send_sem=send_sems.at[rel - 1],
                recv_sem=recv_sems.at[rel - 1], device_id=(x, y, c), device_id_type=MESH).wait_recv()
        for cp in cps:
            cp.wait_send()
        acc = gath[0]
        for d in range(1, N_DEV):
            acc = acc + gath[d]
        o_ref[...] = acc

    vm = pl.BlockSpec(memory_space=pltpu.VMEM)
    return _tc_call(
        body, name=name, in_specs=[vm], out_specs=vm, out_shape=jax.ShapeDtypeStruct((r, cols), F32),
        scratch_shapes=[pltpu.VMEM((N_DEV, r, cols), F32), pltpu.SemaphoreType.DMA((N_DEV - 1,)),
                        pltpu.SemaphoreType.DMA((N_DEV - 1,))],
    )(vec)


def _rope_tables(positions):
    half = QK_ROPE // 2
    inv_freq = 1.0 / (ROPE_THETA ** (jnp.arange(half, dtype=F32) / half))
    ang = positions.astype(F32)[:, None] * inv_freq
    zeros = jnp.zeros((positions.shape[0], LANES - QK_ROPE), F32)
    cos, sin = jnp.cos(ang), jnp.sin(ang)
    return jnp.concatenate([cos, cos, zeros], axis=1), jnp.concatenate([sin, sin, zeros], axis=1)


def _local_step(x, positions, tgt, wf, small, rs):
    cos, sin = _rope_tables(positions)
    w_in, w_out = wf["sc_w_in"], wf["sc_w_out"]
    w_ups, w_downs = (wf["ffn_w_up0"], wf["ffn_w_up1"]), (wf["ffn_w_down0"], wf["ffn_w_down1"])
    w_kv, w_ukv, w_dq, w_uq, w_o = wf["w_kv"], wf["w_ukv"], wf["w_dq"], wf["w_uq"], wf["w_o"]
    attn_norm, ffn_norm = small["attn_norm"], small["ffn_norm"]
    conv_b = small["ffn_conv_b"]

    def ffn_fwd(h, l):
        hf = _rms_fwd(h, ffn_norm[l:l + 1], f"ffn{l}_norm")
        up = _nn_parts(f"ffn{l}_up", hf, w_ups[l], 2, BF16)
        a = _gate_fwd(up, small["ffn_conv_w"][l], conv_b[l:l + 1], f"ffn{l}_gate")
        return _nn(f"ffn{l}_down", a, w_downs[l], F32, add=h), (hf, up, a)

    def ffn_bwd(h, dh_out, dh_out_b, l, saved, gi, hooks):
        run = lambda stage: hooks.get(stage, lambda: None)()
        hf, up, a = saved
        da = _nt(f"ffn{l}_down_dx", dh_out_b, w_downs[l], BF16)
        run("down_dx")
        d_down = _tn(f"ffn{l}_down_dw", a, dh_out_b, BF16)
        dup, d_cw, d_cb = _gate_bwd(up, small["ffn_conv_w"][l], conv_b[l:l + 1], da, f"ffn{l}_gate_bwd")
        run("gate_bwd")
        d_up = _dw_ffn_up(f"ffn{l}_up_dw", hf, dup)
        rs.start(gi, {f"ffn_w_down{l}": d_down.reshape(N_CHIPS, F_FF // N_CHIPS, D), f"ffn_w_up{l}": d_up})
        dhf = _nt_parts(f"ffn{l}_up_dx", dup, w_ups[l], BF16)
        run("up_dx")
        dh, dh_b, d_norm = _rms_bwd(h, ffn_norm[l:l + 1], dhf, dh_out, f"ffn{l}_norm_bwd", matmul_copy=True)
        return dh, dh_b, d_cw, d_cb, d_norm

    hn0 = _rms_fwd(x, attn_norm[0:1], "attn0_norm")
    z = _nn_parts("sc_in", hn0, w_in, 3, BF16)
    mix = _scmix_fwd(z, small["sc_conv_w"])
    h1 = _nn("sc_out", mix, w_out, F32, add=x)
    h2, ffn0_saved = ffn_fwd(h1, 0)

    hk = _rms_fwd(h2, small["kv_in_norm"], "kv_in_norm")
    kvpre = _nn("kv_down", hk, w_kv, F32)
    ckv, kr = _kv_elem_fwd(kvpre, small["kv_latent_norm"], cos, sin)
    knv = _nn_parts("kv_up", ckv, w_ukv, 2, BF16, stacked=True)

    hn1 = _rms_fwd(h2, attn_norm[1:2], "attn1_norm")
    cq_pre = _nn("q_down", hn1, w_dq, F32)
    cq = _rms_fwd(cq_pre, small["q_latent_norm"], "q_latent_norm")
    q = _q_up_rope(cq, w_uq, cos, sin)
    o = _attn_fwd(q, knv, kr)
    h3 = _nn("attn_out", o, w_o, F32, add=h2)
    h4, ffn1_saved = ffn_fwd(h3, 1)

    loss, dh4, dh4_b, d_final = _loss_head(h4, small["final_norm"], tgt)

    rows = D // N_CHIPS
    dh3, dh3_b, d_cw1, d_cb1, d_fn1 = ffn_bwd(h3, dh4, dh4_b, 1, ffn1_saved, 0, {})

    do = _nt("attn_out_dx", dh3_b, w_o, BF16)
    d_wo = _tn("attn_out_dw", o, dh3_b, BF16)
    rs.pair_sums(0)
    dq, dknv, dkr = _attn_bwd(q, knv, kr, do, cos, sin)
    rs.chip_sums(0)
    dcq = _nt("q_up_dx", dq, w_uq, F32)
    d_wuq = _tn("q_up_dw", cq, dq, BF16).reshape(Q_LORA, N_CHIPS, -1).transpose(1, 0, 2)
    dcq_pre, d_qln = _rms_bwd(cq_pre, small["q_latent_norm"], dcq, None, "q_latent_norm_bwd")
    rs.finish(0)
    dhn1 = _nt("q_down_dx", dcq_pre, w_dq, BF16)
    d_wdq = _tn("q_down_dw", hn1, dcq_pre, BF16)
    dh2, d_an1 = _rms_bwd(h2, attn_norm[1:2], dhn1, dh3, "attn1_norm_bwd")

    dckv = _nt_parts("kv_up_dx", dknv, w_ukv, F32, stacked=True)
    d_wukv = _dw_ukv(ckv, dknv)
    dkvpre, d_kvln = _kv_elem_bwd(kvpre, small["kv_latent_norm"], dckv, dkr, cos, sin)
    dhk = _nt("kv_down_dx", dkvpre, w_kv, BF16)
    d_wkv = _tn("kv_down_dw", hk, dkvpre, BF16)
    rs.start(1, {
        "w_o": d_wo.reshape(N_CHIPS, rows, D), "w_uq": d_wuq, "w_dq": d_wdq.reshape(N_CHIPS, rows, Q_LORA),
        "w_ukv": d_wukv.reshape(N_CHIPS, 2 * KV_LORA, -1), "w_kv": d_wkv.reshape(N_CHIPS, rows, KVP),
    })
    dh2, dh2_b, d_kvin = _rms_bwd(h2, small["kv_in_norm"], dhk, dh2, "kv_in_norm_bwd", matmul_copy=True)

    dh1, dh1_b, d_cw0, d_cb0, d_fn0 = ffn_bwd(h1, dh2, dh2_b, 0, ffn0_saved, 2, {
        "down_dx": lambda: rs.pair_sums(1), "gate_bwd": lambda: rs.chip_sums(1), "up_dx": lambda: rs.finish(1)})
    rs.pair_sums(2)

    d_wout = _tn("sc_out_dw", mix, dh1_b, BF16)
    dmix = _nt("sc_out_dx", dh1_b, w_out, BF16)
    dz, d_scw = _scmix_bwd(z, small["sc_conv_w"], dmix)
    d_win = _dw_sc_in(hn0, dz)
    rs.start(3, {"sc_w_out": d_wout.reshape(N_CHIPS, rows, D), "sc_w_in": d_win})
    dhn0 = _nt_parts("sc_in_dx", dz, w_in, BF16)
    dx, d_an0 = _rms_bwd(x, attn_norm[0:1], dhn0, dh1, "attn0_norm_bwd")

    small_g = {
        "attn_norm": jnp.concatenate([d_an0, d_an1]), "ffn_norm": jnp.concatenate([d_fn0, d_fn1]),
        "final_norm": d_final, "kv_in_norm": d_kvin, "kv_latent_norm": d_kvln, "q_latent_norm": d_qln,
        "ffn_conv_b": jnp.concatenate([d_cb0, d_cb1]), "sc_conv_w": d_scw, "ffn_conv_w": jnp.stack([d_cw0, d_cw1]),
    }
    return loss, dx, small_g


RS_GROUPS = (("ffn_w_down1", "ffn_w_up1"), ("w_o", "w_uq", "w_dq", "w_ukv", "w_kv"),
             ("ffn_w_down0", "ffn_w_up0"), ("sc_w_out", "sc_w_in"))


class _ReduceScatter:
    def __init__(self, ids, finish):
        self.ids, self.grads, self.step, self.mine, self.sib, self.finish = ids, {}, {}, {}, {}, finish

    def _cid(self, gi):
        return len(AG_GROUPS) + 3 * gi

    def start(self, gi, grads):
        self.grads.update(grads)
        own = [grads[n] for n in RS_GROUPS[gi]]
        self.step[gi] = (own, _pair_exchange(own, gi, self._cid(gi)))

    def pair_sums(self, gi):
        own, ra = self.step[gi]
        sums = _pair_sums(self.ids, own, ra, f"rs_pair_sums{gi}")
        self.step[gi] = (own, ra, _chip_exchange(sums, gi, self._cid(gi) + 1))

    def chip_sums(self, gi):
        own, ra, rb = self.step[gi]
        mine = _chip_sums(self.ids, own, ra, rb, f"rs_chip_sums{gi}")
        self.mine.update(zip(RS_GROUPS[gi], mine))
        self.sib.update(zip(RS_GROUPS[gi], _pair_swap(mine, gi, self._cid(gi) + 2)))

SMALL_REPL = ("attn_norm", "ffn_norm", "final_norm", "kv_in_norm", "kv_latent_norm", "q_latent_norm", "ffn_conv_b")
SMALL_SHARDED = ("sc_conv_w", "ffn_conv_w")
SMALL_ROWS = 256


def _pad_heads(w_uq):
    per_head = w_uq.reshape(Q_LORA, -1, QK_NOPE + QK_ROPE)
    return jnp.pad(per_head, ((0, 0), (0, 0), (0, HEAD_PAD - QK_NOPE - QK_ROPE))).reshape(Q_LORA, -1)


def _pack_kv(w_dkv, w_kr):
    return jnp.concatenate([w_dkv, w_kr, jnp.zeros((w_kr.shape[0], LANES - QK_ROPE), w_kr.dtype)], axis=1)


def kernel(x, positions, attn_norm, ffn_norm, final_norm, sc_w_in, sc_conv_w, sc_w_out, kv_in_norm, w_dkv, kv_latent_norm, w_kr, w_uk, w_uv, w_dq, q_latent_norm, w_uq, w_o, ffn_w_up, ffn_conv_w, ffn_conv_b, ffn_w_down, loss_target, m_attn_norm, m_ffn_norm, m_final_norm, m_sc_w_in, m_sc_conv_w, m_sc_w_out, m_kv_in_norm, m_w_dkv, m_kv_latent_norm, m_w_kr, m_w_uk, m_w_uv, m_w_dq, m_q_latent_norm, m_w_uq, m_w_o, m_ffn_w_up, m_ffn_conv_w, m_ffn_conv_b, m_ffn_w_down, v_attn_norm, v_ffn_norm, v_final_norm, v_sc_w_in, v_sc_conv_w, v_sc_w_out, v_kv_in_norm, v_w_dkv, v_kv_latent_norm, v_w_kr, v_w_uk, v_w_uv, v_w_dq, v_q_latent_norm, v_w_uq, v_w_o, v_ffn_w_up, v_ffn_conv_w, v_ffn_conv_b, v_ffn_w_down):
    names = ("attn_norm", "ffn_norm", "final_norm", "sc_w_in", "sc_conv_w", "sc_w_out", "kv_in_norm", "w_dkv",
             "kv_latent_norm", "w_kr", "w_uk", "w_uv", "w_dq", "q_latent_norm", "w_uq", "w_o", "ffn_w_up",
             "ffn_conv_w", "ffn_conv_b", "ffn_w_down")
    w = dict(zip(names, (attn_norm, ffn_norm, final_norm, sc_w_in, sc_conv_w, sc_w_out, kv_in_norm, w_dkv,
                         kv_latent_norm, w_kr, w_uk, w_uv, w_dq, q_latent_norm, w_uq, w_o, ffn_w_up,
                         ffn_conv_w, ffn_conv_b, ffn_w_down)))
    m = dict(zip(names, (m_attn_norm, m_ffn_norm, m_final_norm, m_sc_w_in, m_sc_conv_w, m_sc_w_out, m_kv_in_norm,
                         m_w_dkv, m_kv_latent_norm, m_w_kr, m_w_uk, m_w_uv, m_w_dq, m_q_latent_norm, m_w_uq, m_w_o,
                         m_ffn_w_up, m_ffn_conv_w, m_ffn_conv_b, m_ffn_w_down)))
    v = dict(zip(names, (v_attn_norm, v_ffn_norm, v_final_norm, v_sc_w_in, v_sc_conv_w, v_sc_w_out, v_kv_in_norm,
                         v_w_dkv, v_kv_latent_norm, v_w_kr, v_w_uk, v_w_uv, v_w_dq, v_q_latent_norm, v_w_uq, v_w_o,
                         v_ffn_w_up, v_ffn_conv_w, v_ffn_conv_b, v_ffn_w_down)))

    _ORDER[0] = None
    ix, iy, ic = lax.axis_index("x"), lax.axis_index("y"), lax.axis_index("c")
    chip = 2 * ix + iy
    ids = jnp.stack([ic, chip]).astype(jnp.int32)

    def shards_of(t):
        return {
            "sc_w_in": t["sc_w_in"][0], "sc_w_out": t["sc_w_out"][0], "ffn_w_up": t["ffn_w_up"],
            "ffn_w_down": t["ffn_w_down"], "w_kv": _pack_kv(t["w_dkv"], t["w_kr"]),
            "w_ukv": jnp.stack([t["w_uk"], t["w_uv"]]), "w_dq": t["w_dq"][0], "w_uq": _pad_heads(t["w_uq"][0]),
            "w_o": t["w_o"][0],
        }

    ws, ms, vs = shards_of(w), shards_of(m), shards_of(v)

    def ag_shard(name):
        if name == "sc_conv_w":
            return sc_conv_w[0]
        if name == "ffn_conv_w":
            return ffn_conv_w.reshape(6, -1)
        if name[:-1] in ("ffn_w_up", "ffn_w_down"):
            return ws[name[:-1]][int(name[-1])].astype(BF16)
        return ws[name].astype(BF16)

    wf = {}
    for gi, wms in enumerate(AG_GROUPS):
        fulls = _all_gather_group(gi, [ag_shard(wm.name) for wm in wms])
        wf.update({wm.name: f for wm, f in zip(wms, fulls)})
    small = {
        "attn_norm": attn_norm, "ffn_norm": ffn_norm, "final_norm": final_norm[None], "kv_in_norm": kv_in_norm[None],
        "kv_latent_norm": kv_latent_norm[None], "q_latent_norm": q_latent_norm, "ffn_conv_b": ffn_conv_b,
        "sc_conv_w": wf["sc_conv_w"].transpose(1, 0, 2).reshape(3, D),
        "ffn_conv_w": wf["ffn_conv_w"].reshape(N_CHIPS, 2, 3, -1).transpose(1, 2, 0, 3).reshape(2, 3, F_FF),
    }

    res = {}

    merged = lambda a: a.reshape(2 * KV_LORA, -1)

    def adamw_group(gi):
        items = []
        for key in RS_GROUPS[gi]:
            n, layer = (key[:-1], int(key[-1])) if key[:-1] in ("ffn_w_up", "ffn_w_down") else (key, None)
            w_, m_, v_ = (merged(t[n]) for t in (ws, ms, vs)) if n == "w_ukv" else (ws[n], ms[n], vs[n])
            items.append(dict(name=n, w=w_, m=m_, v=v_, g_mine=rs.mine[key], g_sib=rs.sib[key], layer=layer,
                              prev=res.get(n)))
        for it, out in zip(items, _adamw_shards(ids, items, f"adamw_group{gi}")):
            res[it["name"]] = out

    rs = _ReduceScatter(ids, adamw_group)
    loss, dx, small_g = _local_step(x[0], positions[0], loss_target[0], wf, small, rs)

    s_order = SMALL_REPL + SMALL_SHARDED
    flat = jnp.concatenate([small_g[n].reshape(-1) for n in s_order] + [loss.reshape(-1)])
    flat = jnp.pad(flat, (0, SMALL_ROWS * LANES - flat.shape[0])).reshape(SMALL_ROWS, LANES)
    red = _all_reduce_small(flat, "ar_small").reshape(-1)
    sg, off = {}, 0
    for n in s_order:
        sz = small_g[n].size
        sg[n] = red[off:off + sz].reshape(small_g[n].shape)
        off += sz
    loss_out = red[off]
    grads = {n: sg[n].reshape(w[n].shape) for n in SMALL_REPL}
    grads["sc_conv_w"] = lax.dynamic_slice_in_dim(sg["sc_conv_w"], chip * (D // N_CHIPS), D // N_CHIPS, axis=1)[None]
    grads["ffn_conv_w"] = lax.dynamic_slice_in_dim(sg["ffn_conv_w"], chip * (F_FF // N_CHIPS), F_FF // N_CHIPS, axis=2)

    rs.chip_sums(2)
    rs.pair_sums(3)
    rs.finish(2)
    rs.chip_sums(3)
    rs.finish(3)
    outs = [grads, {}, {}, {}]
    for k, dst in enumerate(outs):
        for n in ("sc_w_in", "sc_w_out", "w_dq", "w_o"):
            dst[n] = res[n][k][None]
        unpadded = res["w_uq"][k].reshape(Q_LORA, -1, HEAD_PAD)[:, :, :QK_NOPE + QK_ROPE]
        dst["w_uq"] = unpadded.reshape(w_uq.shape)
        dst["ffn_w_up"], dst["ffn_w_down"] = res["ffn_w_up"][k], res["ffn_w_down"][k]
        dst["w_dkv"], dst["w_kr"] = res["w_kv"][k][:, :KV_LORA], res["w_kv"][k][:, KV_LORA:KV_LORA + QK_ROPE]
        dst["w_uk"], dst["w_uv"] = res["w_ukv"][k][:KV_LORA], res["w_ukv"][k][KV_LORA:]
    grads, delta, new_m, new_v = outs

    small_names = SMALL_REPL + SMALL_SHARDED

    def pack_small(tree):
        return jnp.concatenate([tree[n].reshape(-1) for n in small_names]).reshape(-1, LANES)

    small_res = _adamw_small(pack_small(w), pack_small(grads), pack_small(m), pack_small(v))
    for slab, dst in zip(small_res, (delta, new_m, new_v)):
        f, off = slab.reshape(-1), 0
        for n in small_names:
            dst[n] = f[off:off + w[n].size].reshape(w[n].shape)
            off += w[n].size

    _ORDER[0] = None
    return (loss_out, dx[None], *[grads[n] for n in names], *[delta[n] for n in names],
            *[new_m[n] for n in names], *[new_v[n] for n in names])
```

```python
from typing import NamedTuple

import jax
import jax.numpy as jnp
from jax import lax
from jax.experimental import pallas as pl
from jax.experimental.pallas import tpu as pltpu
from jax.experimental.pallas import tpu_sc as plsc

F32 = jnp.float32
BF16 = jnp.bfloat16

T = 2048
D = 1024
F_FF = 2816
N_HEADS = 8
QK_NOPE = 128
QK_ROPE = 64
V_HEAD = 128
Q_LORA = 384
KV_LORA = 256
CHUNK_SHIFT = 6
ROPE_THETA = 10000.0
EPS = 1e-6
NEG_INF = -1e30
HEAD_PAD = 256
KVP = KV_LORA + 128

ADAM_LR = 0.001
ADAM_B1 = 0.9
ADAM_B2 = 0.999
ADAM_EPS = 1e-08
ADAM_WD = 0.01
ADAM_STEP = 10

N_CHIPS = 4
N_DEV = 8
LANES = 128
TC = 256
V7X_VMEM_LIMIT = 56 * 1024 * 1024

MESH = pl.DeviceIdType.MESH
ANY = pl.BlockSpec(memory_space=pl.ANY)


class _W(NamedTuple):
    name: str
    kind: str
    nl: int
    k: int
    n: int


AG_GROUPS = (
    (_W("sc_w_in", "col", 1, D, 3 * D // N_CHIPS), _W("sc_conv_w", "tiny", 1, 3, D // N_CHIPS),
     _W("ffn_conv_w", "tiny", 1, 6, F_FF // N_CHIPS)),
    (_W("sc_w_out", "row", 1, D // N_CHIPS, D),),
    (_W("ffn_w_up0", "col", 1, D, 2 * F_FF // N_CHIPS),),
    (_W("ffn_w_down0", "row", 1, F_FF // N_CHIPS, D),),
    (_W("w_kv", "row", 1, D // N_CHIPS, KVP), _W("w_ukv", "col", 2, KV_LORA, N_HEADS * QK_NOPE // N_CHIPS),
     _W("w_dq", "row", 1, D // N_CHIPS, Q_LORA),
     _W("w_uq", "col", 1, Q_LORA, N_HEADS * HEAD_PAD // N_CHIPS),
     _W("w_o", "row", 1, N_HEADS * V_HEAD // N_CHIPS, D)),
    (_W("ffn_w_up1", "col", 1, D, 2 * F_FF // N_CHIPS), _W("ffn_w_down1", "row", 1, F_FF // N_CHIPS, D)),
)


def _cp(*sem):
    return pltpu.CompilerParams(dimension_semantics=sem, vmem_limit_bytes=V7X_VMEM_LIMIT)


_ORDER = [None]


def _tc_call(body, *, name, out_shape, in_specs=None, out_specs=None, grid=(), scratch_shapes=(), prefetch=0,
             input_output_aliases=None, compiler_params=None):
    def run(*args):
        specs = [pl.BlockSpec(memory_space=pltpu.VMEM)] * (len(args) - prefetch) if in_specs is None else list(in_specs)
        inner, dep = body, _ORDER[0]
        if dep is not None:
            unread = prefetch + len(specs)
            specs, args = specs + [ANY], (*args, dep)

            def inner(*refs):
                return body(*refs[:unread], *refs[unread + 1:])

        kwargs = dict(name=name, out_shape=out_shape, input_output_aliases=input_output_aliases or {},
                      compiler_params=compiler_params)
        if prefetch:
            kwargs["grid_spec"] = pltpu.PrefetchScalarGridSpec(
                num_scalar_prefetch=prefetch, grid=grid, in_specs=specs, out_specs=out_specs,
                scratch_shapes=scratch_shapes)
        else:
            kwargs.update(grid=grid, in_specs=specs, scratch_shapes=scratch_shapes)
            if out_specs is not None:
                kwargs["out_specs"] = out_specs
        out = pl.pallas_call(inner, **kwargs)(*args)
        _ORDER[0] = out[0] if isinstance(out, (list, tuple)) else out
        return out

    return run


def _tile(n, cands):
    for c in cands:
        if n % c == 0:
            return c
    raise ValueError(f"no tile for {n}")


NN_DIMS = (((1,), (0,)), ((), ()))
NT_DIMS = (((1,), (1,)), ((), ()))
TN_DIMS = (((0,), (0,)), ((), ()))
M_TILES = (1024, 512, 384, 256, 128)
N_TILES = (1408, 1024, 768, 512, 384, 256, 128)
MM_BLOCK_BYTES = 36 * 1024 * 1024


def _fit(m, n, block_bytes, m_tiles=M_TILES, n_tiles=N_TILES):
    for tm in [c for c in m_tiles if m % c == 0]:
        for tn in [c for c in n_tiles if n % c == 0]:
            if 2 * block_bytes(tm, tn) + 4 * tm * tn <= MM_BLOCK_BYTES:
                return tm, tn
    raise ValueError(f"no tiles for {m} x {n}")


def _size(x):
    return x.dtype.itemsize


def _mm(name, a, b, dims, grid, a_spec, b_spec, o_spec, o_sds, add=None, red=None, acc_shape=None):
    n_red = None if red is None else grid[red]

    def body(*refs):
        a_ref, b_ref = refs[0], refs[1]
        add_ref = refs[2] if add is not None else None
        o_ref = refs[3] if add is not None else refs[2]
        part = lax.dot_general(a_ref[...].astype(BF16), b_ref[...].astype(BF16), dims, preferred_element_type=F32)
        if red is None:
            if add is not None:
                part = part + add_ref[...]
            o_ref[...] = part.astype(o_ref.dtype)
            return
        acc_ref = refs[-1]
        r = pl.program_id(red)

        @pl.when(r == 0)
        def _():
            acc_ref[...] = part

        @pl.when(r > 0)
        def _():
            acc_ref[...] += part

        @pl.when(r == n_red - 1)
        def _():
            o_ref[...] = acc_ref[...].astype(o_ref.dtype)

    sem = tuple("arbitrary" if ax == red else "parallel" for ax in range(len(grid)))
    in_specs = [a_spec, b_spec] + ([o_spec] if add is not None else [])
    args = (a, b) + ((add,) if add is not None else ())
    return _tc_call(
        body, name=name, grid=grid, in_specs=in_specs, out_specs=o_spec, out_shape=o_sds,
        scratch_shapes=[] if red is None else [pltpu.VMEM(acc_shape, F32)], compiler_params=_cp(*sem),
    )(*args)


def _nn(name, a, b, out_dtype, add=None, lead=None):
    (m, k), n = a.shape, b.shape[-1]
    osz = jnp.dtype(out_dtype).itemsize + (4 if add is not None else 0)
    tm, tn = _fit(m, n, lambda tm, tn: tm * k * _size(a) + k * tn * _size(b) + tm * tn * osz)
    if lead is None:
        b_spec = pl.BlockSpec((k, tn), lambda i, j: (0, j))
    else:
        b_spec = pl.BlockSpec((None, k, tn), lambda i, j: (lead, 0, j))
    return _mm(name, a, b, NN_DIMS, (m // tm, n // tn), pl.BlockSpec((tm, k), lambda i, j: (i, 0)), b_spec,
               pl.BlockSpec((tm, tn), lambda i, j: (i, j)), jax.ShapeDtypeStruct((m, n), out_dtype), add=add)


def _nn_parts(name, a, b, parts, out_dtype, lead=None, stacked=False):
    m, k = a.shape
    c = b.shape[-1] if stacked else b.shape[-1] // parts
    osz = jnp.dtype(out_dtype).itemsize
    tm, tn = _fit(m, c, lambda tm, tn: tm * k * _size(a) + k * tn * _size(b) + tm * tn * osz)
    nb = c // tn
    if stacked:
        b_spec = pl.BlockSpec((None, k, tn), lambda i, p, j: (p, 0, j))
    elif lead is None:
        b_spec = pl.BlockSpec((k, tn), lambda i, p, j: (0, p * nb + j))
    else:
        b_spec = pl.BlockSpec((None, k, tn), lambda i, p, j: (lead, 0, p * nb + j))
    return _mm(name, a, b, NN_DIMS, (m // tm, parts, nb), pl.BlockSpec((tm, k), lambda i, p, j: (i, 0)), b_spec,
               pl.BlockSpec((None, tm, tn), lambda i, p, j: (p, i, j)), jax.ShapeDtypeStruct((parts, m, c), out_dtype))


def _nt(name, a, b, out_dtype, lead=None):
    (m, k), n = a.shape, b.shape[-2]
    osz = jnp.dtype(out_dtype).itemsize
    tm, tn = _fit(m, n, lambda tm, tn: tm * k * _size(a) + tn * k * _size(b) + tm * tn * osz)
    if lead is None:
        b_spec = pl.BlockSpec((tn, k), lambda i, j: (j, 0))
    else:
        b_spec = pl.BlockSpec((None, tn, k), lambda i, j: (lead, j, 0))
    return _mm(name, a, b, NT_DIMS, (m // tm, n // tn), pl.BlockSpec((tm, k), lambda i, j: (i, 0)), b_spec,
               pl.BlockSpec((tm, tn), lambda i, j: (i, j)), jax.ShapeDtypeStruct((m, n), out_dtype))


def _nt_parts(name, a, b, out_dtype, lead=None, stacked=False):
    parts, m, c = a.shape
    n = b.shape[-2]
    osz = jnp.dtype(out_dtype).itemsize + 2
    tm, tn = _fit(m, n, lambda tm, tn: tm * c * _size(a) + tn * c * _size(b) + tm * tn * osz)
    if stacked:
        b_spec = pl.BlockSpec((None, tn, c), lambda i, j, p: (p, j, 0))
    elif lead is None:
        b_spec = pl.BlockSpec((tn, c), lambda i, j, p: (j, p))
    else:
        b_spec = pl.BlockSpec((None, tn, c), lambda i, j, p: (lead, j, p))
    return _mm(name, a, b, NT_DIMS, (m // tm, n // tn, parts), pl.BlockSpec((None, tm, c), lambda i, j, p: (p, i, 0)),
               b_spec, pl.BlockSpec((tm, tn), lambda i, j, p: (i, j)), jax.ShapeDtypeStruct((m, n), out_dtype),
               red=2, acc_shape=(tm, tn))


def _tn(name, a, b, out_dtype):
    (k, m), n = a.shape, b.shape[1]
    osz = jnp.dtype(out_dtype).itemsize
    tm, tn = _fit(m, n, lambda tm, tn: k * tm * _size(a) + k * tn * _size(b) + tm * tn * osz,
                  m_tiles=(512, 384, 256, 128), n_tiles=(n,) + N_TILES)
    return _mm(name, a, b, TN_DIMS, (m // tm, n // tn), pl.BlockSpec((k, tm), lambda i, j: (0, i)),
               pl.BlockSpec((k, tn), lambda i, j: (0, j)), pl.BlockSpec((tm, tn), lambda i, j: (i, j)),
               jax.ShapeDtypeStruct((m, n), out_dtype))


def _dw_sc_in(hn, dz):
    t, tn, tm = hn.shape[0], TC, 512
    per_part, per_chip = D // tn, 3 * D // N_CHIPS // tn
    return _mm("sc_in_dw", hn, dz, TN_DIMS, (D // tm, 3 * D // tn), pl.BlockSpec((t, tm), lambda i, j: (0, i)),
               pl.BlockSpec((None, t, tn), lambda i, j: (j // per_part, 0, j % per_part)),
               pl.BlockSpec((None, tm, tn), lambda i, j: (j // per_chip, i, j % per_chip)),
               jax.ShapeDtypeStruct((N_CHIPS, D, 3 * D // N_CHIPS), BF16))


def _dw_ffn_up(name, hf, dup):
    t, tm, ns = hf.shape[0], 512, 2 * F_FF // N_CHIPS
    return _mm(name, hf, dup, TN_DIMS, (N_CHIPS, D // tm), pl.BlockSpec((t, tm), lambda s, i: (0, i)),
               pl.BlockSpec((None, t, ns), lambda s, i: (s // 2, 0, s % 2)),
               pl.BlockSpec((None, tm, ns), lambda s, i: (s, i, 0)), jax.ShapeDtypeStruct((N_CHIPS, D, ns), BF16))


def _dw_ukv(ckv, dknv):
    t, ns = ckv.shape[0], N_HEADS * QK_NOPE // N_CHIPS
    return _mm("kv_up_dw", ckv, dknv, TN_DIMS, (2, N_CHIPS), pl.BlockSpec((t, KV_LORA), lambda p, s: (0, 0)),
               pl.BlockSpec((None, t, ns), lambda p, s: (p, 0, s)),
               pl.BlockSpec((None, None, KV_LORA, ns), lambda p, s: (s, p, 0, 0)),
               jax.ShapeDtypeStruct((N_CHIPS, 2, KV_LORA, ns), BF16))


def _rms_fwd(x, g, name):
    t, d = x.shape
    tr = 512

    def body(x_ref, g_ref, o_ref):
        xv = x_ref[...]
        r = lax.rsqrt(jnp.mean(xv * xv, axis=1, keepdims=True) + EPS)
        o_ref[...] = (xv * r * g_ref[...]).astype(o_ref.dtype)

    row = pl.BlockSpec((tr, d), lambda i: (i, 0))
    return _tc_call(
        body, name=name, grid=(t // tr,), in_specs=[row, pl.BlockSpec((1, d), lambda i: (0, 0))],
        out_specs=row, out_shape=jax.ShapeDtypeStruct((t, d), BF16), compiler_params=_cp("parallel"),
    )(x, g)


def _rms_bwd_math(xv, g, dy):
    r = lax.rsqrt(jnp.mean(xv * xv, axis=1, keepdims=True) + EPS)
    xh = xv * r
    gy = dy * g
    dx = r * (gy - xh * jnp.mean(gy * xh, axis=1, keepdims=True))
    dg = jnp.sum(dy * xh, axis=0, keepdims=True)
    return dx, dg


def _rms_bwd(x, g, dy, add, name, matmul_copy=False):
    t, d = x.shape
    tr = 512
    n_in = 3 + (add is not None)

    def body(*refs):
        x_ref, g_ref, dy_ref = refs[:3]
        dx_ref, dg_ref = refs[n_in], refs[-1]
        dx, dg = _rms_bwd_math(x_ref[...], g_ref[...], dy_ref[...].astype(F32))
        if add is not None:
            dx = dx + refs[3][...]
        dx_ref[...] = dx
        if matmul_copy:
            refs[n_in + 1][...] = dx.astype(BF16)

        @pl.when(pl.program_id(0) == 0)
        def _():
            dg_ref[...] = jnp.zeros_like(dg_ref)

        dg_ref[...] += dg

    row = pl.BlockSpec((tr, d), lambda i: (i, 0))
    vec = pl.BlockSpec((1, d), lambda i: (0, 0))
    in_specs = [row, vec, row] + ([row] if add is not None else [])
    args = (x, g, dy) + ((add,) if add is not None else ())
    copies = [jax.ShapeDtypeStruct((t, d), BF16)] if matmul_copy else []
    return _tc_call(
        body, name=name, grid=(t // tr,), in_specs=in_specs, out_specs=[row] * (1 + len(copies)) + [vec],
        out_shape=[jax.ShapeDtypeStruct((t, d), F32)] + copies + [jax.ShapeDtypeStruct((1, d), F32)],
        compiler_params=_cp("arbitrary"),
    )(*args)


def _loss_head(h, g, tgt):
    t, d = h.shape
    tr = 512

    def body(h_ref, g_ref, t_ref, loss_ref, dh_ref, dhb_ref, dg_ref):
        xv = h_ref[...]
        gv = g_ref[...]
        r = lax.rsqrt(jnp.mean(xv * xv, axis=1, keepdims=True) + EPS)
        err = xv * r * gv - t_ref[...]
        part = 0.5 * jnp.sum(jnp.mean(err * err, axis=1, keepdims=True), axis=0, keepdims=True)
        dx, dg = _rms_bwd_math(xv, gv, err * (1.0 / d))
        dh_ref[...] = dx
        dhb_ref[...] = dx.astype(BF16)

        @pl.when(pl.program_id(0) == 0)
        def _():
            dg_ref[...] = jnp.zeros_like(dg_ref)
            loss_ref[...] = jnp.zeros_like(loss_ref)

        dg_ref[...] += dg
        loss_ref[...] += jnp.broadcast_to(part, loss_ref.shape)

    row = pl.BlockSpec((tr, d), lambda i: (i, 0))
    vec = pl.BlockSpec((1, d), lambda i: (0, 0))
    lspec = pl.BlockSpec((1, LANES), lambda i: (0, 0))
    return _tc_call(
        body, name="loss_head", grid=(t // tr,), in_specs=[row, vec, row], out_specs=[lspec, row, row, vec],
        out_shape=[jax.ShapeDtypeStruct((1, LANES), F32), jax.ShapeDtypeStruct((t, d), F32),
                   jax.ShapeDtypeStruct((t, d), BF16), jax.ShapeDtypeStruct((1, d), F32)],
        compiler_params=_cp("arbitrary"),
    )(h, g, tgt)


def _rot_half(x):
    lane = lax.broadcasted_iota(jnp.int32, x.shape, 1)
    return jnp.where((lane % QK_ROPE) < QK_ROPE // 2, -pltpu.roll(x, LANES - 32, axis=1),
                     pltpu.roll(x, 32, axis=1))


def _rope_fwd_math(x, cos, sin):
    return x * cos + _rot_half(x) * sin


def _rope_bwd_math(dy, cos, sin):
    return dy * cos - _rot_half(dy * sin)


def _q_up_rope(cq, w_uq, cos, sin):
    t, k = cq.shape
    w = w_uq.shape[1]
    tr = 256

    def body(a_ref, w_ref, c_ref, s_ref, o_ref):
        av, cv, sv = a_ref[...], c_ref[...], s_ref[...]
        for h in range(N_HEADS):
            lo = h * HEAD_PAD
            qh = jnp.dot(av, w_ref[:, lo:lo + HEAD_PAD], preferred_element_type=F32)
            o_ref[:, lo:lo + QK_NOPE] = qh[:, :QK_NOPE].astype(BF16)
            o_ref[:, lo + QK_NOPE:lo + HEAD_PAD] = _rope_fwd_math(qh[:, QK_NOPE:], cv, sv).astype(BF16)

    tab = pl.BlockSpec((tr, LANES), lambda i: (i, 0))
    return _tc_call(
        body, name="q_up_rope", grid=(t // tr,),
        in_specs=[pl.BlockSpec((tr, k), lambda i: (i, 0)), pl.BlockSpec((k, w), lambda i: (0, 0)), tab, tab],
        out_specs=pl.BlockSpec((tr, w), lambda i: (i, 0)), out_shape=jax.ShapeDtypeStruct((t, w), BF16),
        compiler_params=_cp("parallel"),
    )(cq, w_uq, cos, sin)


def _kv_elem_fwd(kvpre, g, cos, sin):
    t = kvpre.shape[0]
    tr = 512

    def body(p_ref, g_ref, c_ref, s_ref, ckv_ref, kr_ref):
        lat = p_ref[:, :KV_LORA]
        r = lax.rsqrt(jnp.mean(lat * lat, axis=1, keepdims=True) + EPS)
        ckv_ref[...] = (lat * r * g_ref[...]).astype(BF16)
        kr_ref[...] = _rope_fwd_math(p_ref[:, KV_LORA:], c_ref[...], s_ref[...]).astype(BF16)

    tab = pl.BlockSpec((tr, LANES), lambda i: (i, 0))
    return _tc_call(
        body, name="kv_elem_fwd", grid=(t // tr,),
        in_specs=[pl.BlockSpec((tr, KVP), lambda i: (i, 0)), pl.BlockSpec((1, KV_LORA), lambda i: (0, 0)), tab, tab],
        out_specs=[pl.BlockSpec((tr, KV_LORA), lambda i: (i, 0)), tab],
        out_shape=[jax.ShapeDtypeStruct((t, KV_LORA), BF16), jax.ShapeDtypeStruct((t, LANES), BF16)],
        compiler_params=_cp("parallel"),
    )(kvpre, g, cos, sin)


def _kv_elem_bwd(kvpre, g, dckv, dkr, cos, sin):
    t = kvpre.shape[0]
    tr = 512

    def body(p_ref, g_ref, dc_ref, dk_ref, c_ref, s_ref, dp_ref, dg_ref):
        dlat, dg = _rms_bwd_math(p_ref[:, :KV_LORA], g_ref[...], dc_ref[...])
        dp_ref[:, :KV_LORA] = dlat.astype(BF16)
        dp_ref[:, KV_LORA:] = _rope_bwd_math(dk_ref[...], c_ref[...], s_ref[...]).astype(BF16)

        @pl.when(pl.program_id(0) == 0)
        def _():
            dg_ref[...] = jnp.zeros_like(dg_ref)

        dg_ref[...] += dg

    tab = pl.BlockSpec((tr, LANES), lambda i: (i, 0))
    pre = pl.BlockSpec((tr, KVP), lambda i: (i, 0))
    vec = pl.BlockSpec((1, KV_LORA), lambda i: (0, 0))
    return _tc_call(
        body, name="kv_elem_bwd", grid=(t // tr,),
        in_specs=[pre, vec, pl.BlockSpec((tr, KV_LORA), lambda i: (i, 0)), tab, tab, tab],
        out_specs=[pre, vec],
        out_shape=[jax.ShapeDtypeStruct((t, KVP), BF16), jax.ShapeDtypeStruct((1, KV_LORA), F32)],
        compiler_params=_cp("arbitrary"),
    )(kvpre, g, dckv, dkr, cos, sin)


ROW_CHUNK = 64
HALO = 16
WIN = ROW_CHUNK + 16
LANE_HALVES = (slice(0, LANES), slice(LANES, TC))


def _stage(s_ref, p, src):
    t = src.shape[0]
    s_ref[p, :HALO] = jnp.zeros((HALO, TC), BF16)
    s_ref[p, HALO:HALO + t] = src
    s_ref[p, HALO + t:] = jnp.zeros((HALO, TC), BF16)


def _window(s_ref, p, i, lanes):
    base = pl.multiple_of(i * ROW_CHUNK, ROW_CHUNK)
    return s_ref[p, pl.ds(base, ROW_CHUNK + 2 * HALO), lanes].astype(F32)[8:8 + WIN]


def _valid(x):
    return x[8:8 + ROW_CHUNK]


def _prev(x, k):
    return pltpu.roll(x, k, axis=0)


def _next(x, k):
    return pltpu.roll(x, WIN - k, axis=0)


def _taps(w_ref, lanes):
    return w_ref[0:1, lanes], w_ref[1:2, lanes], w_ref[2:3, lanes]


def _fold8(x):
    return jnp.sum(x.reshape(ROW_CHUNK // 8, 8, x.shape[-1]), axis=0)


def _store_rows(ref, idx, i, lanes, x):
    rows = pl.ds(pl.multiple_of(i * ROW_CHUNK, ROW_CHUNK), ROW_CHUNK)
    ref[(*idx, rows, lanes)] = x.astype(ref.dtype)


def _for_chunks(t, chunk):
    def step(i, carry):
        for lanes in LANE_HALVES:
            chunk(i, lanes)
        return carry

    lax.fori_loop(0, t // ROW_CHUNK, step, 0)


def _write_col_sums(acc_ref, outs):
    for k, (ref, row) in enumerate(outs):
        ref[row:row + 1, :] = jnp.sum(acc_ref[k], axis=0, keepdims=True)


def _shift_down(x, k):
    row = lax.broadcasted_iota(jnp.int32, x.shape, 0)
    return jnp.where(row >= k, pltpu.roll(x, k, axis=0), 0.0)


def _shift_up(x, k):
    n = x.shape[0]
    row = lax.broadcasted_iota(jnp.int32, x.shape, 0)
    return jnp.where(row < n - k, pltpu.roll(x, n - k, axis=0), 0.0)


def _conv3(x, w_ref):
    return _shift_down(x, 2) * w_ref[0:1, :] + _shift_down(x, 1) * w_ref[1:2, :] + x * w_ref[2:3, :]


def _col(parts, t):
    if parts is None:
        return pl.BlockSpec((t, TC), lambda j: (0, j))
    return pl.BlockSpec((parts, t, TC), lambda j: (0, 0, j))


def _staging(parts, t):
    return pltpu.VMEM((parts, t + 2 * HALO, TC), BF16)


def _scmix_fwd(z, w):
    t = z.shape[1]

    def body(z_ref, w_ref, m_ref):
        b, c, u = (z_ref[p].astype(F32) for p in range(3))
        m_ref[...] = (b * _conv3(c * u, w_ref)).astype(BF16)

    return _tc_call(
        body, name="scmix_fwd", grid=(D // TC,), in_specs=[_col(3, t), pl.BlockSpec((3, TC), lambda j: (0, j))],
        out_specs=_col(None, t), out_shape=jax.ShapeDtypeStruct((t, D), BF16), compiler_params=_cp("parallel"),
    )(z, w)


def _scmix_bwd(z, w, dm):
    t = z.shape[1]

    def body(z_ref, w_ref, dm_ref, dz_ref, dw_ref, s_ref, acc_ref):
        for p in range(3):
            _stage(s_ref, p, z_ref[p])
        _stage(s_ref, 3, dm_ref[...])
        acc_ref[...] = jnp.zeros_like(acc_ref)

        def chunk(i, lanes):
            w0, w1, w2 = _taps(w_ref, lanes)
            b, c, u, dm = (_window(s_ref, p, i, lanes) for p in range(4))
            cu = c * u
            cu1, cu2 = _prev(cu, 1), _prev(cu, 2)
            _store_rows(dz_ref, (0,), i, lanes, _valid(dm * (cu2 * w0 + cu1 * w1 + cu * w2)))
            dcv = dm * b
            dcu = dcv * w2 + _next(dcv, 1) * w1 + _next(dcv, 2) * w0
            _store_rows(dz_ref, (1,), i, lanes, _valid(dcu * u))
            _store_rows(dz_ref, (2,), i, lanes, _valid(dcu * c))
            for k, shifted in enumerate((cu2, cu1, cu)):
                acc_ref[k, :, lanes] += _fold8(_valid(dcv * shifted))

        _for_chunks(t, chunk)
        _write_col_sums(acc_ref, [(dw_ref, 0), (dw_ref, 1), (dw_ref, 2)])

    wspec = pl.BlockSpec((3, TC), lambda j: (0, j))
    return _tc_call(
        body, name="scmix_bwd", grid=(D // TC,), in_specs=[_col(3, t), wspec, _col(None, t)],
        out_specs=[_col(3, t), wspec],
        out_shape=[jax.ShapeDtypeStruct((3, t, D), BF16), jax.ShapeDtypeStruct((3, D), F32)],
        scratch_shapes=[_staging(4, t), pltpu.VMEM((3, 8, TC), F32)], compiler_params=_cp("parallel"),
    )(z, w, dm)


def _ffn_up_gate(hf, w_up, w, bias, name):
    t, d = hf.shape
    nb = F_FF // TC

    def body(hf_ref, wg_ref, wv_ref, w_ref, b_ref, up_ref, a_ref, prev_ref):
        @pl.when(pl.program_id(0) == 0)
        def _():
            prev_ref[...] = jnp.zeros_like(prev_ref)

        gc = _conv3(prev_ref[0].astype(F32), w_ref) + b_ref[...]
        a_ref[...] = (gc * jax.nn.sigmoid(gc) * prev_ref[1].astype(F32)).astype(BF16)
        hv = hf_ref[...]
        up_ref[0] = jnp.dot(hv, wg_ref[...], preferred_element_type=F32).astype(BF16)
        up_ref[1] = jnp.dot(hv, wv_ref[...], preferred_element_type=F32).astype(BF16)
        prev_ref[...] = up_ref[...]

    tile = lambda j: jnp.minimum(j, nb - 1)
    gated = lambda j: jnp.maximum(j - 1, 0)
    return _tc_call(
        body, name=name, grid=(nb + 1,),
        in_specs=[pl.BlockSpec((t, d), lambda j: (0, 0)), pl.BlockSpec((d, TC), lambda j: (0, tile(j))),
                  pl.BlockSpec((d, TC), lambda j: (0, nb + tile(j))), pl.BlockSpec((3, TC), lambda j: (0, gated(j))),
                  pl.BlockSpec((1, TC), lambda j: (0, gated(j)))],
        out_specs=[pl.BlockSpec((2, t, TC), lambda j: (0, 0, tile(j))), pl.BlockSpec((t, TC), lambda j: (0, gated(j)))],
        out_shape=[jax.ShapeDtypeStruct((2, t, F_FF), BF16), jax.ShapeDtypeStruct((t, F_FF), BF16)],
        scratch_shapes=[pltpu.VMEM((2, t, TC), BF16)], compiler_params=_cp("arbitrary"),
    )(hf, w_up, w_up, w, bias)


def _gate_bwd(up, w, bias, da, name):
    t = up.shape[1]

    def body(u_ref, w_ref, b_ref, da_ref, du_ref, dw_ref, db_ref, s_ref, acc_ref):
        for p in range(2):
            _stage(s_ref, p, u_ref[p])
        _stage(s_ref, 2, da_ref[...])
        acc_ref[...] = jnp.zeros_like(acc_ref)

        def chunk(i, lanes):
            w0, w1, w2 = _taps(w_ref, lanes)
            g, v, da = (_window(s_ref, p, i, lanes) for p in range(3))
            g1, g2 = _prev(g, 1), _prev(g, 2)
            gc = g2 * w0 + g1 * w1 + g * w2 + b_ref[:, lanes]
            sg = jax.nn.sigmoid(gc)
            _store_rows(du_ref, (1,), i, lanes, _valid(da * (gc * sg)))
            dgc = da * v * (sg * (1.0 + gc * (1.0 - sg)))
            _store_rows(du_ref, (0,), i, lanes, _valid(dgc * w2 + _next(dgc, 1) * w1 + _next(dgc, 2) * w0))
            for k, shifted in enumerate((g2, g1, g)):
                acc_ref[k, :, lanes] += _fold8(_valid(dgc * shifted))
            acc_ref[3, :, lanes] += _fold8(_valid(dgc))

        _for_chunks(t, chunk)
        _write_col_sums(acc_ref, [(dw_ref, 0), (dw_ref, 1), (dw_ref, 2), (db_ref, 0)])

    wspec = pl.BlockSpec((3, TC), lambda j: (0, j))
    bspec = pl.BlockSpec((1, TC), lambda j: (0, j))
    return _tc_call(
        body, name=name, grid=(F_FF // TC,), in_specs=[_col(2, t), wspec, bspec, _col(None, t)],
        out_specs=[_col(2, t), wspec, bspec],
        out_shape=[jax.ShapeDtypeStruct((2, t, F_FF), BF16), jax.ShapeDtypeStruct((3, F_FF), F32),
                   jax.ShapeDtypeStruct((1, F_FF), F32)],
        scratch_shapes=[_staging(3, t), pltpu.VMEM((4, 8, TC), F32)], compiler_params=_cp("parallel"),
    )(up, w, bias, da)


ATT_TQ = 256
ATT_SCALE = (QK_NOPE + QK_ROPE) ** -0.5


def _key_ranges(lvl):
    lo = lvl * ATT_TQ
    return ([(0, lo, False)] if lvl else []) + [(lo, lo + ATT_TQ, True)]


def _fill_keys(k_ref, kn_ref, kr_ref):
    @pl.when(pl.program_id(1) == 0)
    def _():
        k_ref[:, :QK_NOPE] = kn_ref[...]
        k_ref[:, QK_NOPE:] = kr_ref[...]


def _attn_probs(q, k_ref, lvl):
    scores = []
    for lo, hi, diagonal in _key_ranges(lvl):
        s = lax.dot_general(q, k_ref[lo:hi, :], NT_DIMS, preferred_element_type=F32) * ATT_SCALE
        if diagonal:
            row = lax.broadcasted_iota(jnp.int32, s.shape, 0)
            col = lax.broadcasted_iota(jnp.int32, s.shape, 1)
            seen = lax.shift_right_logical(col, CHUNK_SHIFT) <= lax.shift_right_logical(row, CHUNK_SHIFT)
            s = jnp.where(seen, s, NEG_INF)
        scores.append(s)
    m = jnp.max(scores[0], axis=1, keepdims=True)
    for s in scores[1:]:
        m = jnp.maximum(m, jnp.max(s, axis=1, keepdims=True))
    ps = [jnp.exp(s - m) for s in scores]
    total = jnp.sum(ps[0], axis=1, keepdims=True)
    for p in ps[1:]:
        total = total + jnp.sum(p, axis=1, keepdims=True)
    inv = 1.0 / total
    return [p * inv for p in ps]


def _per_query_block(qi, n_blocks, branch):
    for lvl in range(n_blocks):
        pl.when(qi == lvl)(lambda lvl=lvl: branch(lvl))


def _attn_specs(t):
    q = pl.BlockSpec((ATT_TQ, HEAD_PAD), lambda h, i: (i, h))
    kn = pl.BlockSpec((None, t, QK_NOPE), lambda h, i: (0, 0, h))
    kr = pl.BlockSpec((t, LANES), lambda h, i: (0, 0))
    v = pl.BlockSpec((None, t, V_HEAD), lambda h, i: (1, 0, h))
    o = pl.BlockSpec((ATT_TQ, V_HEAD), lambda h, i: (i, h))
    return q, kn, kr, v, o


def _attn_fwd(q, knv, kr):
    t = q.shape[0]

    def body(q_ref, kn_ref, kr_ref, v_ref, o_ref, k_ref):
        _fill_keys(k_ref, kn_ref, kr_ref)

        def branch(lvl):
            ps = _attn_probs(q_ref[...], k_ref, lvl)
            o = None
            for p, (lo, hi, _) in zip(ps, _key_ranges(lvl)):
                part = jnp.dot(p.astype(BF16), v_ref[lo:hi, :], preferred_element_type=F32)
                o = part if o is None else o + part
            o_ref[...] = o.astype(BF16)

        _per_query_block(pl.program_id(1), t // ATT_TQ, branch)

    qs, kns, krs, vs, os_ = _attn_specs(t)
    return _tc_call(
        body, name="attn_fwd", grid=(N_HEADS, t // ATT_TQ), in_specs=[qs, kns, krs, vs], out_specs=os_,
        out_shape=jax.ShapeDtypeStruct((t, N_HEADS * V_HEAD), BF16), scratch_shapes=[pltpu.VMEM((t, HEAD_PAD), BF16)],
        compiler_params=_cp("parallel", "arbitrary"),
    )(q, knv, kr, knv)


def _attn_bwd(q, knv, kr, do, cos, sin):
    t = q.shape[0]

    def body(q_ref, kn_ref, kr_ref, v_ref, do_ref, c_ref, s_ref, dq_ref, dknv_ref, dkr_ref, k_ref, dk_ref):
        h, qi = pl.program_id(0), pl.program_id(1)
        _fill_keys(k_ref, kn_ref, kr_ref)

        @pl.when(qi == 0)
        def _():
            dknv_ref[1] = jnp.zeros((t, V_HEAD), F32)
            dk_ref[...] = jnp.zeros_like(dk_ref)

        @pl.when((qi == 0) & (h == 0))
        def _():
            dkr_ref[...] = jnp.zeros_like(dkr_ref)

        def branch(lvl):
            qv, dov = q_ref[...], do_ref[...]
            ranges = _key_ranges(lvl)
            ps = _attn_probs(qv, k_ref, lvl)
            dps = [lax.dot_general(dov, v_ref[lo:hi, :], NT_DIMS, preferred_element_type=F32) for lo, hi, _ in ranges]
            di = None
            for p, dp in zip(ps, dps):
                part = jnp.sum(p * dp, axis=1, keepdims=True)
                di = part if di is None else di + part
            dq = None
            for p, dp, (lo, hi, _) in zip(ps, dps, ranges):
                ds = (p * (dp - di) * ATT_SCALE).astype(BF16)
                part = jnp.dot(ds, k_ref[lo:hi, :], preferred_element_type=F32)
                dq = part if dq is None else dq + part
                dk_ref[lo:hi, :] += lax.dot_general(ds, qv, TN_DIMS, preferred_element_type=F32)
                dknv_ref[1, lo:hi, :] += lax.dot_general(p.astype(BF16), dov, TN_DIMS, preferred_element_type=F32)
            dq_ref[:, :QK_NOPE] = dq[:, :QK_NOPE].astype(BF16)
            dq_ref[:, QK_NOPE:] = _rope_bwd_math(dq[:, QK_NOPE:], c_ref[...], s_ref[...]).astype(BF16)

        _per_query_block(qi, t // ATT_TQ, branch)

        @pl.when(qi == t // ATT_TQ - 1)
        def _():
            dknv_ref[0] = dk_ref[:, :QK_NOPE]
            dkr_ref[...] += dk_ref[:, QK_NOPE:]

    qs, kns, krs, vs, os_ = _attn_specs(t)
    tab = pl.BlockSpec((ATT_TQ, LANES), lambda h, i: (i, 0))
    return _tc_call(
        body, name="attn_bwd", grid=(N_HEADS, t // ATT_TQ), in_specs=[qs, kns, krs, vs, os_, tab, tab],
        out_specs=[qs, pl.BlockSpec((2, t, QK_NOPE), lambda h, i: (0, 0, h)), krs],
        out_shape=[jax.ShapeDtypeStruct((t, N_HEADS * HEAD_PAD), BF16),
                   jax.ShapeDtypeStruct((2, t, N_HEADS * QK_NOPE), F32), jax.ShapeDtypeStruct((t, LANES), F32)],
        scratch_shapes=[pltpu.VMEM((t, HEAD_PAD), BF16), pltpu.VMEM((t, HEAD_PAD), F32)],
        compiler_params=_cp("arbitrary", "arbitrary"),
    )(q, knv, kr, knv, do, cos, sin)


def _adam_math(w, g, m, v):
    nm = ADAM_B1 * m + (1.0 - ADAM_B1) * g
    nv = ADAM_B2 * v + (1.0 - ADAM_B2) * (g * g)
    m_hat = nm / (1.0 - ADAM_B1 ** ADAM_STEP)
    v_hat = nv / (1.0 - ADAM_B2 ** ADAM_STEP)
    return -ADAM_LR * (m_hat / (jnp.sqrt(v_hat) + ADAM_EPS) + ADAM_WD * w), nm, nv


def _adamw_small(w, g, m, v):
    def body(w_ref, g_ref, m_ref, v_ref, d_ref, nm_ref, nv_ref):
        d_ref[...], nm_ref[...], nv_ref[...] = _adam_math(w_ref[...], g_ref[...], m_ref[...], v_ref[...])

    shp = jax.ShapeDtypeStruct(w.shape, F32)
    return _tc_call(body, name="adamw_small", out_shape=[shp] * 3)(w, g, m, v)


ADAM_SPLIT = 4


def _adamw_shards(ids, items, name):
    n = len(items)

    def body(ids_ref, *refs):
        outs = refs[len(refs) - 4 * n:]
        mine = pl.program_id(0) == ids_ref[0]
        for i in range(n):
            w_ref, m_ref, v_ref, gm_ref, gs_ref = refs[5 * i:5 * i + 5]
            g_ref, d_ref, nm_ref, nv_ref = outs[4 * i:4 * i + 4]

            @pl.when(mine)
            def _(g_ref=g_ref, gm_ref=gm_ref):
                g_ref[...] = gm_ref[...]

            @pl.when(jnp.logical_not(mine))
            def _(g_ref=g_ref, gs_ref=gs_ref):
                g_ref[...] = gs_ref[...]

            d_ref[...], nm_ref[...], nv_ref[...] = _adam_math(w_ref[...], g_ref[...], m_ref[...], v_ref[...])

    in_specs, out_specs, out_shape, args, carried, aliases = [], [], [], [ids], [], {}
    for i, it in enumerate(items):
        w = it["w"]
        r, c = w.shape[-2:]
        tr = r // 2 // ADAM_SPLIT
        assert tr % 8 == 0, (name, w.shape)
        layer = it.get("layer")
        if layer is None:
            wspec = pl.BlockSpec((tr, c), lambda h, k, ids: (h * ADAM_SPLIT + k, 0))
        else:
            wspec = pl.BlockSpec((None, tr, c), lambda h, k, ids, layer=layer: (layer, h * ADAM_SPLIT + k, 0))
        gspec = pl.BlockSpec((tr, c), lambda h, k, ids: (k, 0))
        in_specs += [wspec] * 3 + [gspec] * 2
        args += [w, it["m"], it["v"], it["g_mine"], it["g_sib"]]
        out_specs += [wspec] * 4
        out_shape += [jax.ShapeDtypeStruct(w.shape, F32)] * 4
        if it.get("prev") is not None:
            for k, p in enumerate(it["prev"]):
                aliases[1 + 5 * n + len(carried)] = 4 * i + k
                carried.append(p)
    res = _tc_call(
        body, name=name, prefetch=1, grid=(2, ADAM_SPLIT), in_specs=in_specs + [ANY] * len(carried),
        out_specs=out_specs, out_shape=out_shape, input_output_aliases=aliases,
        compiler_params=_cp("parallel", "parallel"),
    )(*args, *carried)
    return [res[4 * i:4 * i + 4] for i in range(n)]


def _peer_chip(k_me, j):
    return k_me ^ jnp.where(j == 0, 2, jnp.where(j == 1, 1, 3))


def _pair_sums(ids, gs, ras, name):
    n = len(gs)

    def body(ids_ref, *refs):
        for i in range(n):
            g_ref, ra_ref, o_ref = refs[2 * i], refs[2 * i + 1], refs[2 * n + i]
            o_ref[...] = (g_ref[...].astype(F32) + ra_ref[...].astype(F32)).astype(BF16)

    in_specs, out_specs, out_shape = [], [], []
    for g in gs:
        half, c = g.shape[1] // 2, g.shape[2]
        in_specs += [pl.BlockSpec((None, half, c), lambda j, ids: (_peer_chip(ids[1], j), ids[0], 0)),
                     pl.BlockSpec((None, half, c), lambda j, ids: (_peer_chip(ids[1], j), 0, 0))]
        out_specs.append(pl.BlockSpec((None, half, c), lambda j, ids: (j, 0, 0)))
        out_shape.append(jax.ShapeDtypeStruct((3, half, c), BF16))
    return _tc_call(
        body, name=name, prefetch=1, grid=(3,), in_specs=in_specs, out_specs=out_specs, out_shape=out_shape,
        compiler_params=_cp("parallel"),
    )(ids, *[a for pair in zip(gs, ras) for a in pair])


def _chip_sums(ids, gs, ras, rbs, name):
    n = len(gs)

    def body(ids_ref, *refs):
        for i in range(n):
            g_ref, ra_ref, rb_ref, o_ref = refs[3 * i], refs[3 * i + 1], refs[3 * i + 2], refs[3 * n + i]
            acc = g_ref[...].astype(F32) + ra_ref[...].astype(F32)
            for j in range(3):
                acc = acc + rb_ref[j].astype(F32)
            o_ref[...] = acc

    in_specs, out_specs, out_shape = [], [], []
    for g in gs:
        half, c = g.shape[1] // 2, g.shape[2]
        in_specs += [pl.BlockSpec((None, half, c), lambda i, ids: (ids[1], ids[0], 0)),
                     pl.BlockSpec((None, half, c), lambda i, ids: (ids[1], 0, 0)),
                     pl.BlockSpec((3, half, c), lambda i, ids: (0, 0, 0))]
        out_specs.append(pl.BlockSpec((half, c), lambda i, ids: (0, 0)))
        out_shape.append(jax.ShapeDtypeStruct((half, c), F32))
    return _tc_call(
        body, name=name, prefetch=1, grid=(1,), in_specs=in_specs, out_specs=out_specs, out_shape=out_shape,
        compiler_params=_cp("arbitrary"),
    )(ids, *[a for trio in zip(gs, ras, rbs) for a in trio])


def _position():
    x, y, c = lax.axis_index("x"), lax.axis_index("y"), lax.axis_index("c")
    chips = [(1 - x, y), (x, 1 - y), (1 - x, 1 - y)]
    return x, y, c, chips


def _shard_half(ref, wm, h):
    if wm.kind == "tiny":
        return ref
    if wm.nl == 2:
        return ref.at[h]
    return ref.at[pl.ds(pl.multiple_of(h * (wm.k // 2), 16), wm.k // 2), :]


def _region(full, wm, s, h):
    if wm.kind == "tiny":
        return full.at[s]
    cols = pl.ds(pl.multiple_of(s * wm.n, LANES), wm.n) if wm.kind == "col" else slice(None)
    if wm.nl == 2:
        rows = pl.ds(pl.multiple_of(s * wm.k, 16), wm.k) if wm.kind == "row" else slice(None)
        return full.at[slice(None) if h is None else h, rows, cols]
    if wm.kind == "col":
        rows = slice(None) if h is None else pl.ds(pl.multiple_of(h * (wm.k // 2), 16), wm.k // 2)
    elif h is None:
        rows = pl.ds(pl.multiple_of(s * wm.k, 16), wm.k)
    else:
        rows = pl.ds(pl.multiple_of(s * wm.k + h * (wm.k // 2), 16), wm.k // 2)
    return full.at[rows, cols]


def _full_shape(wm):
    if wm.kind == "tiny":
        return (N_CHIPS, wm.k, wm.n)
    shape = (wm.k, N_CHIPS * wm.n) if wm.kind == "col" else (N_CHIPS * wm.k, wm.n)
    return shape if wm.nl == 1 else (wm.nl,) + shape


def _handshake(peers):
    barrier = pltpu.get_barrier_semaphore()
    for peer in peers:
        pl.semaphore_signal(barrier, inc=1, device_id=peer, device_id_type=MESH)
    pl.semaphore_wait(barrier, len(peers))


def _all_gather_group(gi, shards):
    wms = AG_GROUPS[gi]
    nw = len(wms)

    def body(*refs):
        sh, full = refs[:nw], refs[nw:2 * nw]
        ici_s, ici_r, pass_s, pass_r, own_s, own_r = refs[2 * nw:]
        x, y, c, chips = _position()
        me, sibling = 2 * x + y, (x, y, 1 - c)
        _handshake([(*chip, c) for chip in chips] + [sibling])

        def rcopy(src, dst, s_sem, r_sem, to):
            return pltpu.make_async_remote_copy(src_ref=src, dst_ref=dst, send_sem=s_sem, recv_sem=r_sem,
                                                device_id=to, device_id_type=MESH)

        started = []
        for i, wm in enumerate(wms):
            for j, chip in enumerate(chips):
                started.append(rcopy(_shard_half(sh[i], wm, c), _region(full[i], wm, me, c),
                                     ici_s.at[i, j], ici_r.at[i, j], (*chip, c)))
                started[-1].start()
            started.append(rcopy(sh[i], _region(full[i], wm, me, None), own_s.at[i], own_r.at[i], sibling))
            started[-1].start()
        for i, wm in enumerate(wms):
            for j, chip in enumerate(chips):
                got = _region(full[i], wm, 2 * chip[0] + chip[1], c)
                rcopy(got, got, ici_s.at[i, j], ici_r.at[i, j], sibling).wait_recv()
                if wm.kind != "tiny":
                    started.append(rcopy(got, got, pass_s.at[i, j], pass_r.at[i, j], sibling))
                    started[-1].start()
        for i, wm in enumerate(wms):
            mine = _region(full[i], wm, me, None)
            rcopy(mine, mine, own_s.at[i], own_r.at[i], sibling).wait_recv()
            for j, chip in enumerate(chips):
                if wm.kind != "tiny":
                    got = _region(full[i], wm, 2 * chip[0] + chip[1], 1 - c)
                    rcopy(got, got, pass_s.at[i, j], pass_r.at[i, j], sibling).wait_recv()
        for cp in started:
            cp.wait_send()

    return pl.kernel(
        body, out_type=[jax.ShapeDtypeStruct(_full_shape(wm), s.dtype) for wm, s in zip(wms, shards)],
        mesh=plsc.ScalarSubcoreMesh(axis_name="sequencer", num_cores=1), name=f"ag_group{gi}",
        scratch_types=[pltpu.SemaphoreType.DMA((nw, 3))] * 4 + [pltpu.SemaphoreType.DMA((nw,))] * 2,
        compiler_params=pltpu.CompilerParams(collective_id=gi),
    )(*shards)


def _sequencer_call(body, name, cid, out_types, scratch, args):
    return pl.kernel(
        body, out_type=out_types, mesh=plsc.ScalarSubcoreMesh(axis_name="sequencer", num_cores=1), name=name,
        scratch_types=scratch, compiler_params=pltpu.CompilerParams(collective_id=cid),
    )(*args)


def _pair_exchange(gs, tag, cid):
    n = len(gs)

    def body(*refs):
        g, out, send_sems, recv_sems = refs[:n], refs[n:2 * n], refs[2 * n], refs[2 * n + 1]
        x, y, c, _ = _position()
        _handshake([(x, y, 1 - c)])
        cps = []
        for i in range(n):
            half = g[i].shape[1] // 2
            cps.append(pltpu.make_async_remote_copy(
                src_ref=g[i].at[:, pl.ds(pl.multiple_of((1 - c) * half, 16), half), :], dst_ref=out[i],
                send_sem=send_sems.at[i], recv_sem=recv_sems.at[i], device_id=(x, y, 1 - c), device_id_type=MESH))
            cps[-1].start()
        for cp in cps:
            cp.wait()

    return _sequencer_call(
        body, f"rs_pair_exchange{tag}", cid,
        [jax.ShapeDtypeStruct((a.shape[0], a.shape[1] // 2, a.shape[2]), a.dtype) for a in gs],
        [pltpu.SemaphoreType.DMA((n,)), pltpu.SemaphoreType.DMA((n,))], gs)


def _chip_exchange(ss, tag, cid):
    n = len(ss)

    def body(*refs):
        s, out, send_sems, recv_sems = refs[:n], refs[n:2 * n], refs[2 * n], refs[2 * n + 1]
        x, y, c, chips = _position()
        _handshake([(*chip, c) for chip in chips])
        cps = []
        for i in range(n):
            for j, chip in enumerate(chips):
                cps.append(pltpu.make_async_remote_copy(
                    src_ref=s[i].at[j], dst_ref=out[i].at[j], send_sem=send_sems.at[i, j], recv_sem=recv_sems.at[i, j],
                    device_id=(*chip, c), device_id_type=MESH))
                cps[-1].start()
        for cp in cps:
            cp.wait()

    return _sequencer_call(
        body, f"rs_chip_exchange{tag}", cid, [jax.ShapeDtypeStruct(a.shape, a.dtype) for a in ss],
        [pltpu.SemaphoreType.DMA((n, 3)), pltpu.SemaphoreType.DMA((n, 3))], ss)


def _pair_swap(g8s, tag, cid):
    n = len(g8s)

    def body(*refs):
        g, out, send_sems, recv_sems = refs[:n], refs[n:2 * n], refs[2 * n], refs[2 * n + 1]
        x, y, c, _ = _position()
        _handshake([(x, y, 1 - c)])
        cps = []
        for i in range(n):
            cps.append(pltpu.make_async_remote_copy(
                src_ref=g[i], dst_ref=out[i], send_sem=send_sems.at[i], recv_sem=recv_sems.at[i],
                device_id=(x, y, 1 - c), device_id_type=MESH))
            cps[-1].start()
        for cp in cps:
            cp.wait()

    return _sequencer_call(
        body, f"rs_pair_swap{tag}", cid, [jax.ShapeDtypeStruct(a.shape, a.dtype) for a in g8s],
        [pltpu.SemaphoreType.DMA((n,)), pltpu.SemaphoreType.DMA((n,))], g8s)


def _all_reduce_small(vec, name):
    r, cols = vec.shape

    def body(v_ref, o_ref, gath, send_sems, recv_sems):
        x, y, c, _ = _position()
        me = 4 * x + 2 * y + c
        gath[me] = v_ref[...]
        cps = []
        for rel in range(1, N_DEV):
            peer = (x ^ (rel >> 2), y ^ ((rel >> 1) & 1), c ^ (rel & 1))
            cps.append(pltpu.make_async_remote_copy(
                src_ref=v_ref, dst_ref=gath.at[me], send_sem=send_sems.at[rel - 1], recv_sem=recv_sems.at[rel - 1],
                device_id=peer, device_id_type=MESH))
        for cp in cps:
            cp.start()
        for rel in range(1, N_DEV):
            pltpu.make_async_remote_copy(
                src_ref=v_ref, dst_ref=gath.at[me ^ rel], send_sem=send_sems.at[rel - 1],
                recv_sem=recv_sems.at[rel - 1], device_id=(x, y, c), device_id_type=MESH).wait_recv()
        for cp in cps:
            cp.wait_send()
        acc = gath[0]
        for d in range(1, N_DEV):
            acc = acc + gath[d]
        o_ref[...] = acc

    vm = pl.BlockSpec(memory_space=pltpu.VMEM)
    return _tc_call(
        body, name=name, in_specs=[vm], out_specs=vm, out_shape=jax.ShapeDtypeStruct((r, cols), F32),
        scratch_shapes=[pltpu.VMEM((N_DEV, r, cols), F32), pltpu.SemaphoreType.DMA((N_DEV - 1,)),
                        pltpu.SemaphoreType.DMA((N_DEV - 1,))],
    )(vec)


def _rope_tables(positions):
    half = QK_ROPE // 2
    inv_freq = 1.0 / (ROPE_THETA ** (jnp.arange(half, dtype=F32) / half))
    ang = positions.astype(F32)[:, None] * inv_freq
    zeros = jnp.zeros((positions.shape[0], LANES - QK_ROPE), F32)
    cos, sin = jnp.cos(ang), jnp.sin(ang)
    return jnp.concatenate([cos, cos, zeros], axis=1), jnp.concatenate([sin, sin, zeros], axis=1)


def _local_step(x, positions, tgt, wf, small, rs):
    cos, sin = _rope_tables(positions)
    w_in, w_out = wf["sc_w_in"], wf["sc_w_out"]
    w_ups, w_downs = (wf["ffn_w_up0"], wf["ffn_w_up1"]), (wf["ffn_w_down0"], wf["ffn_w_down1"])
    w_kv, w_ukv, w_dq, w_uq, w_o = wf["w_kv"], wf["w_ukv"], wf["w_dq"], wf["w_uq"], wf["w_o"]
    attn_norm, ffn_norm = small["attn_norm"], small["ffn_norm"]
    conv_b = small["ffn_conv_b"]

    def ffn_fwd(h, l):
        hf = _rms_fwd(h, ffn_norm[l:l + 1], f"ffn{l}_norm")
        up, a = _ffn_up_gate(hf, w_ups[l], small["ffn_conv_w"][l], conv_b[l:l + 1], f"ffn{l}_up_gate")
        return _nn(f"ffn{l}_down", a, w_downs[l], F32, add=h), (hf, up, a)

    def ffn_bwd(h, dh_out, dh_out_b, l, saved, gi, hooks):
        run = lambda stage: hooks.get(stage, lambda: None)()
        hf, up, a = saved
        da = _nt(f"ffn{l}_down_dx", dh_out_b, w_downs[l], BF16)
        run("down_dx")
        d_down = _tn(f"ffn{l}_down_dw", a, dh_out_b, BF16)
        dup, d_cw, d_cb = _gate_bwd(up, small["ffn_conv_w"][l], conv_b[l:l + 1], da, f"ffn{l}_gate_bwd")
        run("gate_bwd")
        d_up = _dw_ffn_up(f"ffn{l}_up_dw", hf, dup)
        rs.start(gi, {f"ffn_w_down{l}": d_down.reshape(N_CHIPS, F_FF // N_CHIPS, D), f"ffn_w_up{l}": d_up})
        dhf = _nt_parts(f"ffn{l}_up_dx", dup, w_ups[l], BF16)
        run("up_dx")
        dh, dh_b, d_norm = _rms_bwd(h, ffn_norm[l:l + 1], dhf, dh_out, f"ffn{l}_norm_bwd", matmul_copy=True)
        return dh, dh_b, d_cw, d_cb, d_norm

    hn0 = _rms_fwd(x, attn_norm[0:1], "attn0_norm")
    z = _nn_parts("sc_in", hn0, w_in, 3, BF16)
    mix = _scmix_fwd(z, small["sc_conv_w"])
    h1 = _nn("sc_out", mix, w_out, F32, add=x)
    h2, ffn0_saved = ffn_fwd(h1, 0)

    hk = _rms_fwd(h2, small["kv_in_norm"], "kv_in_norm")
    kvpre = _nn("kv_down", hk, w_kv, F32)
    ckv, kr = _kv_elem_fwd(kvpre, small["kv_latent_norm"], cos, sin)
    knv = _nn_parts("kv_up", ckv, w_ukv, 2, BF16, stacked=True)

    hn1 = _rms_fwd(h2, attn_norm[1:2], "attn1_norm")
    cq_pre = _nn("q_down", hn1, w_dq, F32)
    cq = _rms_fwd(cq_pre, small["q_latent_norm"], "q_latent_norm")
    q = _q_up_rope(cq, w_uq, cos, sin)
    o = _attn_fwd(q, knv, kr)
    h3 = _nn("attn_out", o, w_o, F32, add=h2)
    h4, ffn1_saved = ffn_fwd(h3, 1)

    loss, dh4, dh4_b, d_final = _loss_head(h4, small["final_norm"], tgt)

    rows = D // N_CHIPS
    dh3, dh3_b, d_cw1, d_cb1, d_fn1 = ffn_bwd(h3, dh4, dh4_b, 1, ffn1_saved, 0, {})

    do = _nt("attn_out_dx", dh3_b, w_o, BF16)
    d_wo = _tn("attn_out_dw", o, dh3_b, BF16)
    rs.pair_sums(0)
    dq, dknv, dkr = _attn_bwd(q, knv, kr, do, cos, sin)
    rs.chip_sums(0)
    dcq = _nt("q_up_dx", dq, w_uq, F32)
    d_wuq = _tn("q_up_dw", cq, dq, BF16).reshape(Q_LORA, N_CHIPS, -1).transpose(1, 0, 2)
    dcq_pre, d_qln = _rms_bwd(cq_pre, small["q_latent_norm"], dcq, None, "q_latent_norm_bwd")
    rs.finish(0)
    dhn1 = _nt("q_down_dx", dcq_pre, w_dq, BF16)
    d_wdq = _tn("q_down_dw", hn1, dcq_pre, BF16)
    dh2, d_an1 = _rms_bwd(h2, attn_norm[1:2], dhn1, dh3, "attn1_norm_bwd")

    dckv = _nt_parts("kv_up_dx", dknv, w_ukv, F32, stacked=True)
    d_wukv = _dw_ukv(ckv, dknv)
    dkvpre, d_kvln = _kv_elem_bwd(kvpre, small["kv_latent_norm"], dckv, dkr, cos, sin)
    dhk = _nt("kv_down_dx", dkvpre, w_kv, BF16)
    d_wkv = _tn("kv_down_dw", hk, dkvpre, BF16)
    rs.start(1, {
        "w_o": d_wo.reshape(N_CHIPS, rows, D), "w_uq": d_wuq, "w_dq": d_wdq.reshape(N_CHIPS, rows, Q_LORA),
        "w_ukv": d_wukv.reshape(N_CHIPS, 2 * KV_LORA, -1), "w_kv": d_wkv.reshape(N_CHIPS, rows, KVP),
    })
    dh2, dh2_b, d_kvin = _rms_bwd(h2, small["kv_in_norm"], dhk, dh2, "kv_in_norm_bwd", matmul_copy=True)

    dh1, dh1_b, d_cw0, d_cb0, d_fn0 = ffn_bwd(h1, dh2, dh2_b, 0, ffn0_saved, 2, {
        "down_dx": lambda: rs.pair_sums(1), "gate_bwd": lambda: rs.chip_sums(1), "up_dx": lambda: rs.finish(1)})
    rs.pair_sums(2)

    d_wout = _tn("sc_out_dw", mix, dh1_b, BF16)
    dmix = _nt("sc_out_dx", dh1_b, w_out, BF16)
    dz, d_scw = _scmix_bwd(z, small["sc_conv_w"], dmix)
    d_win = _dw_sc_in(hn0, dz)
    rs.start(3, {"sc_w_out": d_wout.reshape(N_CHIPS, rows, D), "sc_w_in": d_win})
    dhn0 = _nt_parts("sc_in_dx", dz, w_in, BF16)
    dx, d_an0 = _rms_bwd(x, attn_norm[0:1], dhn0, dh1, "attn0_norm_bwd")

    small_g = {
        "attn_norm": jnp.concatenate([d_an0, d_an1]), "ffn_norm": jnp.concatenate([d_fn0, d_fn1]),
        "final_norm": d_final, "kv_in_norm": d_kvin, "kv_latent_norm": d_kvln, "q_latent_norm": d_qln,
        "ffn_conv_b": jnp.concatenate([d_cb0, d_cb1]), "sc_conv_w": d_scw, "ffn_conv_w": jnp.stack([d_cw0, d_cw1]),
    }
    return loss, dx, small_g


RS_GROUPS = (("ffn_w_down1", "ffn_w_up1"), ("w_o", "w_uq", "w_dq", "w_ukv", "w_kv"),
             ("ffn_w_down0", "ffn_w_up0"), ("sc_w_out", "sc_w_in"))


class _ReduceScatter:
    def __init__(self, ids, finish):
        self.ids, self.grads, self.step, self.mine, self.sib, self.finish = ids, {}, {}, {}, {}, finish

    def _cid(self, gi):
        return len(AG_GROUPS) + 3 * gi

    def start(self, gi, grads):
        self.grads.update(grads)
        own = [grads[n] for n in RS_GROUPS[gi]]
        self.step[gi] = (own, _pair_exchange(own, gi, self._cid(gi)))

    def pair_sums(self, gi):
        own, ra = self.step[gi]
        sums = _pair_sums(self.ids, own, ra, f"rs_pair_sums{gi}")
        self.step[gi] = (own, ra, _chip_exchange(sums, gi, self._cid(gi) + 1))

    def chip_sums(self, gi):
        own, ra, rb = self.step[gi]
        mine = _chip_sums(self.ids, own, ra, rb, f"rs_chip_sums{gi}")
        self.mine.update(zip(RS_GROUPS[gi], mine))
        self.sib.update(zip(RS_GROUPS[gi], _pair_swap(mine, gi, self._cid(gi) + 2)))

SMALL_REPL = ("attn_norm", "ffn_norm", "final_norm", "kv_in_norm", "kv_latent_norm", "q_latent_norm", "ffn_conv_b")
SMALL_SHARDED = ("sc_conv_w", "ffn_conv_w")
SMALL_ROWS = 256


def _pad_heads(w_uq):
    per_head = w_uq.reshape(Q_LORA, -1, QK_NOPE + QK_ROPE)
    return jnp.pad(per_head, ((0, 0), (0, 0), (0, HEAD_PAD - QK_NOPE - QK_ROPE))).reshape(Q_LORA, -1)


def _pack_kv(w_dkv, w_kr):
    return jnp.concatenate([w_dkv, w_kr, jnp.zeros((w_kr.shape[0], LANES - QK_ROPE), w_kr.dtype)], axis=1)


def kernel(x, positions, attn_norm, ffn_norm, final_norm, sc_w_in, sc_conv_w, sc_w_out, kv_in_norm, w_dkv, kv_latent_norm, w_kr, w_uk, w_uv, w_dq, q_latent_norm, w_uq, w_o, ffn_w_up, ffn_conv_w, ffn_conv_b, ffn_w_down, loss_target, m_attn_norm, m_ffn_norm, m_final_norm, m_sc_w_in, m_sc_conv_w, m_sc_w_out, m_kv_in_norm, m_w_dkv, m_kv_latent_norm, m_w_kr, m_w_uk, m_w_uv, m_w_dq, m_q_latent_norm, m_w_uq, m_w_o, m_ffn_w_up, m_ffn_conv_w, m_ffn_conv_b, m_ffn_w_down, v_attn_norm, v_ffn_norm, v_final_norm, v_sc_w_in, v_sc_conv_w, v_sc_w_out, v_kv_in_norm, v_w_dkv, v_kv_latent_norm, v_w_kr, v_w_uk, v_w_uv, v_w_dq, v_q_latent_norm, v_w_uq, v_w_o, v_ffn_w_up, v_ffn_conv_w, v_ffn_conv_b, v_ffn_w_down):
    names = ("attn_norm", "ffn_norm", "final_norm", "sc_w_in", "sc_conv_w", "sc_w_out", "kv_in_norm", "w_dkv",
             "kv_latent_norm", "w_kr", "w_uk", "w_uv", "w_dq", "q_latent_norm", "w_uq", "w_o", "ffn_w_up",
             "ffn_conv_w", "ffn_conv_b", "ffn_w_down")
    w = dict(zip(names, (attn_norm, ffn_norm, final_norm, sc_w_in, sc_conv_w, sc_w_out, kv_in_norm, w_dkv,
                         kv_latent_norm, w_kr, w_uk, w_uv, w_dq, q_latent_norm, w_uq, w_o, ffn_w_up,
                         ffn_conv_w, ffn_conv_b, ffn_w_down)))
    m = dict(zip(names, (m_attn_norm, m_ffn_norm, m_final_norm, m_sc_w_in, m_sc_conv_w, m_sc_w_out, m_kv_in_norm,
                         m_w_dkv, m_kv_latent_norm, m_w_kr, m_w_uk, m_w_uv, m_w_dq, m_q_latent_norm, m_w_uq, m_w_o,
                         m_ffn_w_up, m_ffn_conv_w, m_ffn_conv_b, m_ffn_w_down)))
    v = dict(zip(names, (v_attn_norm, v_ffn_norm, v_final_norm, v_sc_w_in, v_sc_conv_w, v_sc_w_out, v_kv_in_norm,
                         v_w_dkv, v_kv_latent_norm, v_w_kr, v_w_uk, v_w_uv, v_w_dq, v_q_latent_norm, v_w_uq, v_w_o,
                         v_ffn_w_up, v_ffn_conv_w, v_ffn_conv_b, v_ffn_w_down)))

    _ORDER[0] = None
    ix, iy, ic = lax.axis_index("x"), lax.axis_index("y"), lax.axis_index("c")
    chip = 2 * ix + iy
    ids = jnp.stack([ic, chip]).astype(jnp.int32)

    def shards_of(t):
        return {
            "sc_w_in": t["sc_w_in"][0], "sc_w_out": t["sc_w_out"][0], "ffn_w_up": t["ffn_w_up"],
            "ffn_w_down": t["ffn_w_down"], "w_kv": _pack_kv(t["w_dkv"], t["w_kr"]),
            "w_ukv": jnp.stack([t["w_uk"], t["w_uv"]]), "w_dq": t["w_dq"][0], "w_uq": _pad_heads(t["w_uq"][0]),
            "w_o": t["w_o"][0],
        }

    ws, ms, vs = shards_of(w), shards_of(m), shards_of(v)

    def ag_shard(name):
        if name == "sc_conv_w":
            return sc_conv_w[0]
        if name == "ffn_conv_w":
            return ffn_conv_w.reshape(6, -1)
        if name[:-1] in ("ffn_w_up", "ffn_w_down"):
            return ws[name[:-1]][int(name[-1])].astype(BF16)
        return ws[name].astype(BF16)

    wf = {}
    for gi, wms in enumerate(AG_GROUPS):
        fulls = _all_gather_group(gi, [ag_shard(wm.name) for wm in wms])
        wf.update({wm.name: f for wm, f in zip(wms, fulls)})
    small = {
        "attn_norm": attn_norm, "ffn_norm": ffn_norm, "final_norm": final_norm[None], "kv_in_norm": kv_in_norm[None],
        "kv_latent_norm": kv_latent_norm[None], "q_latent_norm": q_latent_norm, "ffn_conv_b": ffn_conv_b,
        "sc_conv_w": wf["sc_conv_w"].transpose(1, 0, 2).reshape(3, D),
        "ffn_conv_w": wf["ffn_conv_w"].reshape(N_CHIPS, 2, 3, -1).transpose(1, 2, 0, 3).reshape(2, 3, F_FF),
    }

    res = {}

    merged = lambda a: a.reshape(2 * KV_LORA, -1)

    def adamw_group(gi):
        items = []
        for key in RS_GROUPS[gi]:
            n, layer = (key[:-1], int(key[-1])) if key[:-1] in ("ffn_w_up", "ffn_w_down") else (key, None)
            w_, m_, v_ = (merged(t[n]) for t in (ws, ms, vs)) if n == "w_ukv" else (ws[n], ms[n], vs[n])
            items.append(dict(name=n, w=w_, m=m_, v=v_, g_mine=rs.mine[key], g_sib=rs.sib[key], layer=layer,
                              prev=res.get(n)))
        for it, out in zip(items, _adamw_shards(ids, items, f"adamw_group{gi}")):
            res[it["name"]] = out

    rs = _ReduceScatter(ids, adamw_group)
    loss, dx, small_g = _local_step(x[0], positions[0], loss_target[0], wf, small, rs)

    s_order = SMALL_REPL + SMALL_SHARDED
    flat = jnp.concatenate([small_g[n].reshape(-1) for n in s_order] + [loss.reshape(-1)])
    flat = jnp.pad(flat, (0, SMALL_ROWS * LANES - flat.shape[0])).reshape(SMALL_ROWS, LANES)
    red = _all_reduce_small(flat, "ar_small").reshape(-1)
    sg, off = {}, 0
    for n in s_order:
        sz = small_g[n].size
        sg[n] = red[off:off + sz].reshape(small_g[n].shape)
        off += sz
    loss_out = red[off]
    grads = {n: sg[n].reshape(w[n].shape) for n in SMALL_REPL}
    grads["sc_conv_w"] = lax.dynamic_slice_in_dim(sg["sc_conv_w"], chip * (D // N_CHIPS), D // N_CHIPS, axis=1)[None]
    grads["ffn_conv_w"] = lax.dynamic_slice_in_dim(sg["ffn_conv_w"], chip * (F_FF // N_CHIPS), F_FF // N_CHIPS, axis=2)

    rs.chip_sums(2)
    rs.pair_sums(3)
    rs.finish(2)
    rs.chip_sums(3)
    rs.finish(3)
    outs = [grads, {}, {}, {}]
    for k, dst in enumerate(outs):
        for n in ("sc_w_in", "sc_w_out", "w_dq", "w_o"):
            dst[n] = res[n][k][None]
        unpadded = res["w_uq"][k].reshape(Q_LORA, -1, HEAD_PAD)[:, :, :QK_NOPE + QK_ROPE]
        dst["w_uq"] = unpadded.reshape(w_uq.shape)
        dst["ffn_w_up"], dst["ffn_w_down"] = res["ffn_w_up"][k], res["ffn_w_down"][k]
        dst["w_dkv"], dst["w_kr"] = res["w_kv"][k][:, :KV_LORA], res["w_kv"][k][:, KV_LORA:KV_LORA + QK_ROPE]
        dst["w_uk"], dst["w_uv"] = res["w_ukv"][k][:KV_LORA], res["w_ukv"][k][KV_LORA:]
    grads, delta, new_m, new_v = outs

    small_names = SMALL_REPL + SMALL_SHARDED

    def pack_small(tree):
        return jnp.concatenate([tree[n].reshape(-1) for n in small_names]).reshape(-1, LANES)

    small_res = _adamw_small(pack_small(w), pack_small(grads), pack_small(m), pack_small(v))
    for slab, dst in zip(small_res, (delta, new_m, new_v)):
        f, off = slab.reshape(-1), 0
        for n in small_names:
            dst[n] = f[off:off + w[n].size].reshape(w[n].shape)
            off += w[n].size

    _ORDER[0] = None
    return (loss_out, dx[None], *[grads[n] for n in names], *[delta[n] for n in names],
            *[new_m[n] for n in names], *[new_v[n] for n in names])
```

```python
from typing import NamedTuple

import jax
import jax.numpy as jnp
from jax import lax
from jax.experimental import pallas as pl
from jax.experimental.pallas import tpu as pltpu
from jax.experimental.pallas import tpu_sc as plsc

F32 = jnp.float32
BF16 = jnp.bfloat16

T = 2048
D = 1024
F_FF = 2816
N_HEADS = 8
QK_NOPE = 128
QK_ROPE = 64
V_HEAD = 128
Q_LORA = 384
KV_LORA = 256
CHUNK_SHIFT = 6
ROPE_THETA = 10000.0
EPS = 1e-6
NEG_INF = -1e30
HEAD_PAD = 256
KVP = KV_LORA + 128

ADAM_LR = 0.001
ADAM_B1 = 0.9
ADAM_B2 = 0.999
ADAM_EPS = 1e-08
ADAM_WD = 0.01
ADAM_STEP = 10

N_CHIPS = 4
N_DEV = 8
LANES = 128
TC = 256
V7X_VMEM_LIMIT = 56 * 1024 * 1024

MESH = pl.DeviceIdType.MESH
ANY = pl.BlockSpec(memory_space=pl.ANY)


class _W(NamedTuple):
    name: str
    kind: str
    nl: int
    k: int
    n: int


AG_GROUPS = (
    (_W("sc_w_in", "col", 1, D, 3 * D // N_CHIPS), _W("sc_conv_w", "tiny", 1, 3, D // N_CHIPS),
     _W("ffn_conv_w", "tiny", 1, 6, F_FF // N_CHIPS)),
    (_W("sc_w_out", "row", 1, D // N_CHIPS, D),),
    (_W("ffn_w_up0", "col", 1, D, 2 * F_FF // N_CHIPS),),
    (_W("ffn_w_down0", "row", 1, F_FF // N_CHIPS, D),),
    (_W("w_kv", "row", 1, D // N_CHIPS, KVP), _W("w_ukv", "col", 2, KV_LORA, N_HEADS * QK_NOPE // N_CHIPS),
     _W("w_dq", "row", 1, D // N_CHIPS, Q_LORA),
     _W("w_uq", "col", 1, Q_LORA, N_HEADS * HEAD_PAD // N_CHIPS),
     _W("w_o", "row", 1, N_HEADS * V_HEAD // N_CHIPS, D)),
    (_W("ffn_w_up1", "col", 1, D, 2 * F_FF // N_CHIPS), _W("ffn_w_down1", "row", 1, F_FF // N_CHIPS, D)),
)


def _cp(*sem):
    return pltpu.CompilerParams(dimension_semantics=sem, vmem_limit_bytes=V7X_VMEM_LIMIT)


_ORDER = [None]


def _tc_call(body, *, name, out_shape, in_specs=None, out_specs=None, grid=(), scratch_shapes=(), prefetch=0,
             input_output_aliases=None, compiler_params=None):
    def run(*args):
        specs = [pl.BlockSpec(memory_space=pltpu.VMEM)] * (len(args) - prefetch) if in_specs is None else list(in_specs)
        inner, dep = body, _ORDER[0]
        if dep is not None:
            unread = prefetch + len(specs)
            specs, args = specs + [ANY], (*args, dep)

            def inner(*refs):
                return body(*refs[:unread], *refs[unread + 1:])

        kwargs = dict(name=name, out_shape=out_shape, input_output_aliases=input_output_aliases or {},
                      compiler_params=compiler_params)
        if prefetch:
            kwargs["grid_spec"] = pltpu.PrefetchScalarGridSpec(
                num_scalar_prefetch=prefetch, grid=grid, in_specs=specs, out_specs=out_specs,
                scratch_shapes=scratch_shapes)
        else:
            kwargs.update(grid=grid, in_specs=specs, scratch_shapes=scratch_shapes)
            if out_specs is not None:
                kwargs["out_specs"] = out_specs
        out = pl.pallas_call(inner, **kwargs)(*args)
        _ORDER[0] = out[0] if isinstance(out, (list, tuple)) else out
        return out

    return run


def _tile(n, cands):
    for c in cands:
        if n % c == 0:
            return c
    raise ValueError(f"no tile for {n}")


NN_DIMS = (((1,), (0,)), ((), ()))
NT_DIMS = (((1,), (1,)), ((), ()))
TN_DIMS = (((0,), (0,)), ((), ()))
M_TILES = (1024, 512, 384, 256, 128)
N_TILES = (1408, 1024, 768, 512, 384, 256, 128)
MM_BLOCK_BYTES = 36 * 1024 * 1024


def _fit(m, n, block_bytes, m_tiles=M_TILES, n_tiles=N_TILES):
    for tm in [c for c in m_tiles if m % c == 0]:
        for tn in [c for c in n_tiles if n % c == 0]:
            if 2 * block_bytes(tm, tn) + 4 * tm * tn <= MM_BLOCK_BYTES:
                return tm, tn
    raise ValueError(f"no tiles for {m} x {n}")


def _size(x):
    return x.dtype.itemsize


def _mm(name, a, b, dims, grid, a_spec, b_spec, o_spec, o_sds, add=None, red=None, acc_shape=None):
    n_red = None if red is None else grid[red]

    def body(*refs):
        a_ref, b_ref = refs[0], refs[1]
        add_ref = refs[2] if add is not None else None
        o_ref = refs[3] if add is not None else refs[2]
        part = lax.dot_general(a_ref[...].astype(BF16), b_ref[...].astype(BF16), dims, preferred_element_type=F32)
        if red is None:
            if add is not None:
                part = part + add_ref[...]
            o_ref[...] = part.astype(o_ref.dtype)
            return
        acc_ref = refs[-1]
        r = pl.program_id(red)

        @pl.when(r == 0)
        def _():
            acc_ref[...] = part

        @pl.when(r > 0)
        def _():
            acc_ref[...] += part

        @pl.when(r == n_red - 1)
        def _():
            o_ref[...] = acc_ref[...].astype(o_ref.dtype)

    sem = tuple("arbitrary" if ax == red else "parallel" for ax in range(len(grid)))
    in_specs = [a_spec, b_spec] + ([o_spec] if add is not None else [])
    args = (a, b) + ((add,) if add is not None else ())
    return _tc_call(
        body, name=name, grid=grid, in_specs=in_specs, out_specs=o_spec, out_shape=o_sds,
        scratch_shapes=[] if red is None else [pltpu.VMEM(acc_shape, F32)], compiler_params=_cp(*sem),
    )(*args)


def _nn(name, a, b, out_dtype, add=None, lead=None):
    (m, k), n = a.shape, b.shape[-1]
    osz = jnp.dtype(out_dtype).itemsize + (4 if add is not None else 0)
    tm, tn = _fit(m, n, lambda tm, tn: tm * k * _size(a) + k * tn * _size(b) + tm * tn * osz)
    if lead is None:
        b_spec = pl.BlockSpec((k, tn), lambda i, j: (0, j))
    else:
        b_spec = pl.BlockSpec((None, k, tn), lambda i, j: (lead, 0, j))
    return _mm(name, a, b, NN_DIMS, (m // tm, n // tn), pl.BlockSpec((tm, k), lambda i, j: (i, 0)), b_spec,
               pl.BlockSpec((tm, tn), lambda i, j: (i, j)), jax.ShapeDtypeStruct((m, n), out_dtype), add=add)


def _nn_parts(name, a, b, parts, out_dtype, lead=None, stacked=False):
    m, k = a.shape
    c = b.shape[-1] if stacked else b.shape[-1] // parts
    osz = jnp.dtype(out_dtype).itemsize
    tm, tn = _fit(m, c, lambda tm, tn: tm * k * _size(a) + k * tn * _size(b) + tm * tn * osz)
    nb = c // tn
    if stacked:
        b_spec = pl.BlockSpec((None, k, tn), lambda i, p, j: (p, 0, j))
    elif lead is None:
        b_spec = pl.BlockSpec((k, tn), lambda i, p, j: (0, p * nb + j))
    else:
        b_spec = pl.BlockSpec((None, k, tn), lambda i, p, j: (lead, 0, p * nb + j))
    return _mm(name, a, b, NN_DIMS, (m // tm, parts, nb), pl.BlockSpec((tm, k), lambda i, p, j: (i, 0)), b_spec,
               pl.BlockSpec((None, tm, tn), lambda i, p, j: (p, i, j)), jax.ShapeDtypeStruct((parts, m, c), out_dtype))


def _nt(name, a, b, out_dtype, lead=None):
    (m, k), n = a.shape, b.shape[-2]
    osz = jnp.dtype(out_dtype).itemsize
    tm, tn = _fit(m, n, lambda tm, tn: tm * k * _size(a) + tn * k * _size(b) + tm * tn * osz)
    if lead is None:
        b_spec = pl.BlockSpec((tn, k), lambda i, j: (j, 0))
    else:
        b_spec = pl.BlockSpec((None, tn, k), lambda i, j: (lead, j, 0))
    return _mm(name, a, b, NT_DIMS, (m // tm, n // tn), pl.BlockSpec((tm, k), lambda i, j: (i, 0)), b_spec,
               pl.BlockSpec((tm, tn), lambda i, j: (i, j)), jax.ShapeDtypeStruct((m, n), out_dtype))


def _nt_parts(name, a, b, out_dtype, lead=None, stacked=False):
    parts, m, c = a.shape
    n = b.shape[-2]
    osz = jnp.dtype(out_dtype).itemsize + 2
    tm, tn = _fit(m, n, lambda tm, tn: tm * c * _size(a) + tn * c * _size(b) + tm * tn * osz)
    if stacked:
        b_spec = pl.BlockSpec((None, tn, c), lambda i, j, p: (p, j, 0))
    elif lead is None:
        b_spec = pl.BlockSpec((tn, c), lambda i, j, p: (j, p))
    else:
        b_spec = pl.BlockSpec((None, tn, c), lambda i, j, p: (lead, j, p))
    return _mm(name, a, b, NT_DIMS, (m // tm, n // tn, parts), pl.BlockSpec((None, tm, c), lambda i, j, p: (p, i, 0)),
               b_spec, pl.BlockSpec((tm, tn), lambda i, j, p: (i, j)), jax.ShapeDtypeStruct((m, n), out_dtype),
               red=2, acc_shape=(tm, tn))


def _tn(name, a, b, out_dtype):
    (k, m), n = a.shape, b.shape[1]
    osz = jnp.dtype(out_dtype).itemsize
    tm, tn = _fit(m, n, lambda tm, tn: k * tm * _size(a) + k * tn * _size(b) + tm * tn * osz,
                  m_tiles=(512, 384, 256, 128), n_tiles=(n,) + N_TILES)
    return _mm(name, a, b, TN_DIMS, (m // tm, n // tn), pl.BlockSpec((k, tm), lambda i, j: (0, i)),
               pl.BlockSpec((k, tn), lambda i, j: (0, j)), pl.BlockSpec((tm, tn), lambda i, j: (i, j)),
               jax.ShapeDtypeStruct((m, n), out_dtype))


def _dw_sc_in(hn, dz):
    t, tn, tm = hn.shape[0], TC, 512
    per_part, per_chip = D // tn, 3 * D // N_CHIPS // tn
    return _mm("sc_in_dw", hn, dz, TN_DIMS, (D // tm, 3 * D // tn), pl.BlockSpec((t, tm), lambda i, j: (0, i)),
               pl.BlockSpec((None, t, tn), lambda i, j: (j // per_part, 0, j % per_part)),
               pl.BlockSpec((None, tm, tn), lambda i, j: (j // per_chip, i, j % per_chip)),
               jax.ShapeDtypeStruct((N_CHIPS, D, 3 * D // N_CHIPS), BF16))


def _dw_ffn_up(name, hf, dup):
    t, tm, ns = hf.shape[0], 512, 2 * F_FF // N_CHIPS
    return _mm(name, hf, dup, TN_DIMS, (N_CHIPS, D // tm), pl.BlockSpec((t, tm), lambda s, i: (0, i)),
               pl.BlockSpec((None, t, ns), lambda s, i: (s // 2, 0, s % 2)),
               pl.BlockSpec((None, tm, ns), lambda s, i: (s, i, 0)), jax.ShapeDtypeStruct((N_CHIPS, D, ns), BF16))


def _rms_fwd(x, g, name):
    t, d = x.shape
    tr = 512

    def body(x_ref, g_ref, o_ref):
        xv = x_ref[...]
        r = lax.rsqrt(jnp.mean(xv * xv, axis=1, keepdims=True) + EPS)
        o_ref[...] = (xv * r * g_ref[...]).astype(o_ref.dtype)

    row = pl.BlockSpec((tr, d), lambda i: (i, 0))
    return _tc_call(
        body, name=name, grid=(t // tr,), in_specs=[row, pl.BlockSpec((1, d), lambda i: (0, 0))],
        out_specs=row, out_shape=jax.ShapeDtypeStruct((t, d), BF16), compiler_params=_cp("parallel"),
    )(x, g)


def _rms_bwd_math(xv, g, dy):
    r = lax.rsqrt(jnp.mean(xv * xv, axis=1, keepdims=True) + EPS)
    xh = xv * r
    gy = dy * g
    dx = r * (gy - xh * jnp.mean(gy * xh, axis=1, keepdims=True))
    dg = jnp.sum(dy * xh, axis=0, keepdims=True)
    return dx, dg


def _rms_bwd(x, g, dy, add, name, matmul_copy=False):
    t, d = x.shape
    tr = 512
    n_in = 3 + (add is not None)

    def body(*refs):
        x_ref, g_ref, dy_ref = refs[:3]
        dx_ref, dg_ref = refs[n_in], refs[-1]
        dx, dg = _rms_bwd_math(x_ref[...], g_ref[...], dy_ref[...].astype(F32))
        if add is not None:
            dx = dx + refs[3][...]
        dx_ref[...] = dx
        if matmul_copy:
            refs[n_in + 1][...] = dx.astype(BF16)

        @pl.when(pl.program_id(0) == 0)
        def _():
            dg_ref[...] = jnp.zeros_like(dg_ref)

        dg_ref[...] += dg

    row = pl.BlockSpec((tr, d), lambda i: (i, 0))
    vec = pl.BlockSpec((1, d), lambda i: (0, 0))
    in_specs = [row, vec, row] + ([row] if add is not None else [])
    args = (x, g, dy) + ((add,) if add is not None else ())
    copies = [jax.ShapeDtypeStruct((t, d), BF16)] if matmul_copy else []
    return _tc_call(
        body, name=name, grid=(t // tr,), in_specs=in_specs, out_specs=[row] * (1 + len(copies)) + [vec],
        out_shape=[jax.ShapeDtypeStruct((t, d), F32)] + copies + [jax.ShapeDtypeStruct((1, d), F32)],
        compiler_params=_cp("arbitrary"),
    )(*args)


def _loss_head(h, g, tgt):
    t, d = h.shape
    tr = 512

    def body(h_ref, g_ref, t_ref, loss_ref, dh_ref, dhb_ref, dg_ref):
        xv = h_ref[...]
        gv = g_ref[...]
        r = lax.rsqrt(jnp.mean(xv * xv, axis=1, keepdims=True) + EPS)
        err = xv * r * gv - t_ref[...]
        part = 0.5 * jnp.sum(jnp.mean(err * err, axis=1, keepdims=True), axis=0, keepdims=True)
        dx, dg = _rms_bwd_math(xv, gv, err * (1.0 / d))
        dh_ref[...] = dx
        dhb_ref[...] = dx.astype(BF16)

        @pl.when(pl.program_id(0) == 0)
        def _():
            dg_ref[...] = jnp.zeros_like(dg_ref)
            loss_ref[...] = jnp.zeros_like(loss_ref)

        dg_ref[...] += dg
        loss_ref[...] += jnp.broadcast_to(part, loss_ref.shape)

    row = pl.BlockSpec((tr, d), lambda i: (i, 0))
    vec = pl.BlockSpec((1, d), lambda i: (0, 0))
    lspec = pl.BlockSpec((1, LANES), lambda i: (0, 0))
    return _tc_call(
        body, name="loss_head", grid=(t // tr,), in_specs=[row, vec, row], out_specs=[lspec, row, row, vec],
        out_shape=[jax.ShapeDtypeStruct((1, LANES), F32), jax.ShapeDtypeStruct((t, d), F32),
                   jax.ShapeDtypeStruct((t, d), BF16), jax.ShapeDtypeStruct((1, d), F32)],
        compiler_params=_cp("arbitrary"),
    )(h, g, tgt)


def _rot_half(x):
    lane = lax.broadcasted_iota(jnp.int32, x.shape, 1)
    return jnp.where((lane % QK_ROPE) < QK_ROPE // 2, -pltpu.roll(x, LANES - 32, axis=1),
                     pltpu.roll(x, 32, axis=1))


def _rope_fwd_math(x, cos, sin):
    return x * cos + _rot_half(x) * sin


def _rope_bwd_math(dy, cos, sin):
    return dy * cos - _rot_half(dy * sin)


def _rms_rows(x, g):
    return x * lax.rsqrt(jnp.mean(x * x, axis=1, keepdims=True) + EPS) * g


def _attn_prep(h, g_attn, g_kvin, w_dq, g_ql, w_uq, w_kv, g_kvl, w_ukv, cos, sin):
    t, d = h.shape
    tr = 256
    wq = N_HEADS * HEAD_PAD

    def body(h_ref, ga_ref, gk_ref, wdq_ref, gq_ref, wuq_ref, wkv_ref, gl_ref, wukv_ref, c_ref, s_ref,
             hn_ref, hk_ref, cqp_ref, cq_ref, q_ref, kvp_ref, ckv_ref, kr_ref, knv_ref):
        xv, cv, sv = h_ref[...], c_ref[...], s_ref[...]
        xh = xv * lax.rsqrt(jnp.mean(xv * xv, axis=1, keepdims=True) + EPS)
        hn = (xh * ga_ref[...]).astype(BF16)
        hk = (xh * gk_ref[...]).astype(BF16)
        hn_ref[...], hk_ref[...] = hn, hk
        cq_pre = jnp.dot(hn, wdq_ref[...], preferred_element_type=F32)
        cqp_ref[...] = cq_pre
        cq = _rms_rows(cq_pre, gq_ref[...]).astype(BF16)
        cq_ref[...] = cq
        for hd in range(N_HEADS):
            lo = hd * HEAD_PAD
            qh = jnp.dot(cq, wuq_ref[:, lo:lo + HEAD_PAD], preferred_element_type=F32)
            q_ref[:, lo:lo + QK_NOPE] = qh[:, :QK_NOPE].astype(BF16)
            q_ref[:, lo + QK_NOPE:lo + HEAD_PAD] = _rope_fwd_math(qh[:, QK_NOPE:], cv, sv).astype(BF16)
        kvpre = jnp.dot(hk, wkv_ref[...], preferred_element_type=F32)
        kvp_ref[...] = kvpre
        ckv = _rms_rows(kvpre[:, :KV_LORA], gl_ref[...]).astype(BF16)
        ckv_ref[...] = ckv
        kr_ref[...] = _rope_fwd_math(kvpre[:, KV_LORA:], cv, sv).astype(BF16)
        for p in range(2):
            knv_ref[p] = jnp.dot(ckv, wukv_ref[p], preferred_element_type=F32).astype(BF16)

    rows = lambda w: pl.BlockSpec((tr, w), lambda i: (i, 0))
    whole = lambda a: pl.BlockSpec(a.shape, lambda i: (0,) * a.ndim)
    sds = lambda w, dt: jax.ShapeDtypeStruct((t, w), dt)
    args = (h, g_attn, g_kvin, w_dq, g_ql, w_uq, w_kv, g_kvl, w_ukv, cos, sin)
    return _tc_call(
        body, name="attn_prep", grid=(t // tr,),
        in_specs=[rows(d)] + [whole(a) for a in args[1:9]] + [rows(LANES), rows(LANES)],
        out_specs=[rows(d), rows(d), rows(Q_LORA), rows(Q_LORA), rows(wq), rows(KVP), rows(KV_LORA), rows(LANES),
                   pl.BlockSpec((2, tr, N_HEADS * QK_NOPE), lambda i: (0, i, 0))],
        out_shape=[sds(d, BF16), sds(d, BF16), sds(Q_LORA, F32), sds(Q_LORA, BF16), sds(wq, BF16), sds(KVP, F32),
                   sds(KV_LORA, BF16), sds(LANES, BF16), jax.ShapeDtypeStruct((2, t, N_HEADS * QK_NOPE), BF16)],
        compiler_params=_cp("parallel"),
    )(*args)


def _attn_prep_bwd(dq, dknv, dkr, dh, h, hn, hk, cq_pre, cq, kvpre, ckv, g_attn, g_kvin, w_dq, g_ql, w_uq, w_kv, g_kvl,
                   w_ukv, cos, sin):
    t, d = h.shape
    tr = 256
    n_steps = t // tr
    wq = N_HEADS * HEAD_PAD
    wk = N_HEADS * QK_NOPE

    def body(dq_ref, dknv_ref, dkr_ref, dh_ref, h_ref, hn_ref, hk_ref, cqp_ref, cq_ref, kvp_ref, ckv_ref,
             ga_ref, gk_ref, wdq_ref, gq_ref, wuq_ref, wkv_ref, gl_ref, wukv_ref, c_ref, s_ref,
             dho_ref, dhb_ref, dwuq_ref, dwdq_ref, dwukv_ref, dwkv_ref, dga_ref, dgk_ref, dgq_ref, dgl_ref,
             a_uq, a_dq, a_ukv, a_kv):
        i = pl.program_id(0)

        @pl.when(i == 0)
        def _():
            for ref in (a_uq, a_dq, a_ukv, a_kv, dga_ref, dgk_ref, dgq_ref, dgl_ref):
                ref[...] = jnp.zeros_like(ref)

        dqv = dq_ref[...]
        dcq = lax.dot_general(dqv, wuq_ref[...], NT_DIMS, preferred_element_type=F32)
        a_uq[...] += lax.dot_general(cq_ref[...], dqv, TN_DIMS, preferred_element_type=F32)
        dcq_pre, dg = _rms_bwd_math(cqp_ref[...], gq_ref[...], dcq)
        dgq_ref[...] += dg
        dcq_pre = dcq_pre.astype(BF16)
        dhn = lax.dot_general(dcq_pre, wdq_ref[...], NT_DIMS, preferred_element_type=F32)
        a_dq[...] += lax.dot_general(hn_ref[...], dcq_pre, TN_DIMS, preferred_element_type=F32)
        dckv = None
        for p in range(2):
            dk = dknv_ref[p].astype(BF16)
            part = lax.dot_general(dk, wukv_ref[p], NT_DIMS, preferred_element_type=F32)
            dckv = part if dckv is None else dckv + part
            a_ukv[p] += lax.dot_general(ckv_ref[...], dk, TN_DIMS, preferred_element_type=F32)
        dlat, dg = _rms_bwd_math(kvp_ref[:, :KV_LORA], gl_ref[...], dckv)
        dgl_ref[...] += dg
        dkr_pre = _rope_bwd_math(dkr_ref[...], c_ref[...], s_ref[...])
        dkvpre = jnp.concatenate([dlat, dkr_pre], axis=1).astype(BF16)
        dhk = lax.dot_general(dkvpre, wkv_ref[...], NT_DIMS, preferred_element_type=F32)
        a_kv[...] += lax.dot_general(hk_ref[...], dkvpre, TN_DIMS, preferred_element_type=F32)
        xv = h_ref[...]
        dx1, dg = _rms_bwd_math(xv, ga_ref[...], dhn)
        dga_ref[...] += dg
        dx2, dg = _rms_bwd_math(xv, gk_ref[...], dhk)
        dgk_ref[...] += dg
        dh_new = dh_ref[...] + dx1 + dx2
        dho_ref[...] = dh_new
        dhb_ref[...] = dh_new.astype(BF16)

        @pl.when(i == n_steps - 1)
        def _():
            dwuq_ref[...] = a_uq[...].astype(BF16)
            dwdq_ref[...] = a_dq[...].astype(BF16)
            dwukv_ref[...] = a_ukv[...].astype(BF16)
            dwkv_ref[...] = a_kv[...].astype(BF16)

    rows = lambda w: pl.BlockSpec((tr, w), lambda i: (i, 0))
    whole = lambda shape: pl.BlockSpec(shape, lambda i: (0,) * len(shape))
    weights = (g_attn, g_kvin, w_dq, g_ql, w_uq, w_kv, g_kvl, w_ukv)
    dw_shapes = [(Q_LORA, wq), (d, Q_LORA), (2, KV_LORA, wk), (d, KVP)]
    dg_shapes = [(1, d), (1, d), (1, Q_LORA), (1, KV_LORA)]
    return _tc_call(
        body, name="attn_prep_bwd", grid=(n_steps,),
        in_specs=[rows(wq), pl.BlockSpec((2, tr, wk), lambda i: (0, i, 0)), rows(LANES), rows(d), rows(d), rows(d),
                  rows(d), rows(Q_LORA), rows(Q_LORA), rows(KVP), rows(KV_LORA)]
        + [whole(a.shape) for a in weights] + [rows(LANES), rows(LANES)],
        out_specs=[rows(d), rows(d)] + [whole(s) for s in dw_shapes + dg_shapes],
        out_shape=[jax.ShapeDtypeStruct((t, d), F32), jax.ShapeDtypeStruct((t, d), BF16)]
        + [jax.ShapeDtypeStruct(s, BF16) for s in dw_shapes] + [jax.ShapeDtypeStruct(s, F32) for s in dg_shapes],
        scratch_shapes=[pltpu.VMEM(s, F32) for s in dw_shapes], compiler_params=_cp("arbitrary"),
    )(dq, dknv, dkr, dh, h, hn, hk, cq_pre, cq, kvpre, ckv, *weights, cos, sin)


ROW_CHUNK = 64
HALO = 16
WIN = ROW_CHUNK + 16
LANE_HALVES = (slice(0, LANES), slice(LANES, TC))


def _stage(s_ref, p, src):
    t = src.shape[0]
    s_ref[p, :HALO] = jnp.zeros((HALO, TC), BF16)
    s_ref[p, HALO:HALO + t] = src
    s_ref[p, HALO + t:] = jnp.zeros((HALO, TC), BF16)


def _window(s_ref, p, i, lanes):
    base = pl.multiple_of(i * ROW_CHUNK, ROW_CHUNK)
    return s_ref[p, pl.ds(base, ROW_CHUNK + 2 * HALO), lanes].astype(F32)[8:8 + WIN]


def _valid(x):
    return x[8:8 + ROW_CHUNK]


def _prev(x, k):
    return pltpu.roll(x, k, axis=0)


def _next(x, k):
    return pltpu.roll(x, WIN - k, axis=0)


def _taps(w_ref, lanes):
    return w_ref[0:1, lanes], w_ref[1:2, lanes], w_ref[2:3, lanes]


def _fold8(x):
    return jnp.sum(x.reshape(ROW_CHUNK // 8, 8, x.shape[-1]), axis=0)


def _store_rows(ref, idx, i, lanes, x):
    rows = pl.ds(pl.multiple_of(i * ROW_CHUNK, ROW_CHUNK), ROW_CHUNK)
    ref[(*idx, rows, lanes)] = x.astype(ref.dtype)


def _for_chunks(t, chunk):
    def step(i, carry):
        for lanes in LANE_HALVES:
            chunk(i, lanes)
        return carry

    lax.fori_loop(0, t // ROW_CHUNK, step, 0)


def _write_col_sums(acc_ref, outs):
    for k, (ref, row) in enumerate(outs):
        ref[row:row + 1, :] = jnp.sum(acc_ref[k], axis=0, keepdims=True)


def _shift_down(x, k):
    row = lax.broadcasted_iota(jnp.int32, x.shape, 0)
    return jnp.where(row >= k, pltpu.roll(x, k, axis=0), 0.0)


def _shift_up(x, k):
    n = x.shape[0]
    row = lax.broadcasted_iota(jnp.int32, x.shape, 0)
    return jnp.where(row < n - k, pltpu.roll(x, n - k, axis=0), 0.0)


def _conv3(x, w_ref):
    return _shift_down(x, 2) * w_ref[0:1, :] + _shift_down(x, 1) * w_ref[1:2, :] + x * w_ref[2:3, :]


def _col(parts, t):
    if parts is None:
        return pl.BlockSpec((t, TC), lambda j: (0, j))
    return pl.BlockSpec((parts, t, TC), lambda j: (0, 0, j))


def _staging(parts, t):
    return pltpu.VMEM((parts, t + 2 * HALO, TC), BF16)


def _scmix_fwd(z, w):
    t = z.shape[1]

    def body(z_ref, w_ref, m_ref):
        b, c, u = (z_ref[p].astype(F32) for p in range(3))
        m_ref[...] = (b * _conv3(c * u, w_ref)).astype(BF16)

    return _tc_call(
        body, name="scmix_fwd", grid=(D // TC,), in_specs=[_col(3, t), pl.BlockSpec((3, TC), lambda j: (0, j))],
        out_specs=_col(None, t), out_shape=jax.ShapeDtypeStruct((t, D), BF16), compiler_params=_cp("parallel"),
    )(z, w)


def _scmix_bwd(z, w, dm):
    t = z.shape[1]

    def body(z_ref, w_ref, dm_ref, dz_ref, dw_ref, s_ref, acc_ref):
        for p in range(3):
            _stage(s_ref, p, z_ref[p])
        _stage(s_ref, 3, dm_ref[...])
        acc_ref[...] = jnp.zeros_like(acc_ref)

        def chunk(i, lanes):
            w0, w1, w2 = _taps(w_ref, lanes)
            b, c, u, dm = (_window(s_ref, p, i, lanes) for p in range(4))
            cu = c * u
            cu1, cu2 = _prev(cu, 1), _prev(cu, 2)
            _store_rows(dz_ref, (0,), i, lanes, _valid(dm * (cu2 * w0 + cu1 * w1 + cu * w2)))
            dcv = dm * b
            dcu = dcv * w2 + _next(dcv, 1) * w1 + _next(dcv, 2) * w0
            _store_rows(dz_ref, (1,), i, lanes, _valid(dcu * u))
            _store_rows(dz_ref, (2,), i, lanes, _valid(dcu * c))
            for k, shifted in enumerate((cu2, cu1, cu)):
                acc_ref[k, :, lanes] += _fold8(_valid(dcv * shifted))

        _for_chunks(t, chunk)
        _write_col_sums(acc_ref, [(dw_ref, 0), (dw_ref, 1), (dw_ref, 2)])

    wspec = pl.BlockSpec((3, TC), lambda j: (0, j))
    return _tc_call(
        body, name="scmix_bwd", grid=(D // TC,), in_specs=[_col(3, t), wspec, _col(None, t)],
        out_specs=[_col(3, t), wspec],
        out_shape=[jax.ShapeDtypeStruct((3, t, D), BF16), jax.ShapeDtypeStruct((3, D), F32)],
        scratch_shapes=[_staging(4, t), pltpu.VMEM((3, 8, TC), F32)], compiler_params=_cp("parallel"),
    )(z, w, dm)


def _ffn_up_gate(hf, w_up, w, bias, name):
    t, d = hf.shape
    nb = F_FF // TC

    def body(hf_ref, wg_ref, wv_ref, w_ref, b_ref, up_ref, a_ref, prev_ref):
        @pl.when(pl.program_id(0) == 0)
        def _():
            prev_ref[...] = jnp.zeros_like(prev_ref)

        gc = _conv3(prev_ref[0].astype(F32), w_ref) + b_ref[...]
        a_ref[...] = (gc * jax.nn.sigmoid(gc) * prev_ref[1].astype(F32)).astype(BF16)
        hv = hf_ref[...]
        up_ref[0] = jnp.dot(hv, wg_ref[...], preferred_element_type=F32).astype(BF16)
        up_ref[1] = jnp.dot(hv, wv_ref[...], preferred_element_type=F32).astype(BF16)
        prev_ref[...] = up_ref[...]

    tile = lambda j: jnp.minimum(j, nb - 1)
    gated = lambda j: jnp.maximum(j - 1, 0)
    return _tc_call(
        body, name=name, grid=(nb + 1,),
        in_specs=[pl.BlockSpec((t, d), lambda j: (0, 0)), pl.BlockSpec((d, TC), lambda j: (0, tile(j))),
                  pl.BlockSpec((d, TC), lambda j: (0, nb + tile(j))), pl.BlockSpec((3, TC), lambda j: (0, gated(j))),
                  pl.BlockSpec((1, TC), lambda j: (0, gated(j)))],
        out_specs=[pl.BlockSpec((2, t, TC), lambda j: (0, 0, tile(j))), pl.BlockSpec((t, TC), lambda j: (0, gated(j)))],
        out_shape=[jax.ShapeDtypeStruct((2, t, F_FF), BF16), jax.ShapeDtypeStruct((t, F_FF), BF16)],
        scratch_shapes=[pltpu.VMEM((2, t, TC), BF16)], compiler_params=_cp("arbitrary"),
    )(hf, w_up, w_up, w, bias)


def _gate_bwd(up, w, bias, da, name):
    t = up.shape[1]

    def body(u_ref, w_ref, b_ref, da_ref, du_ref, dw_ref, db_ref, s_ref, acc_ref):
        for p in range(2):
            _stage(s_ref, p, u_ref[p])
        _stage(s_ref, 2, da_ref[...])
        acc_ref[...] = jnp.zeros_like(acc_ref)

        def chunk(i, lanes):
            w0, w1, w2 = _taps(w_ref, lanes)
            g, v, da = (_window(s_ref, p, i, lanes) for p in range(3))
            g1, g2 = _prev(g, 1), _prev(g, 2)
            gc = g2 * w0 + g1 * w1 + g * w2 + b_ref[:, lanes]
            sg = jax.nn.sigmoid(gc)
            _store_rows(du_ref, (1,), i, lanes, _valid(da * (gc * sg)))
            dgc = da * v * (sg * (1.0 + gc * (1.0 - sg)))
            _store_rows(du_ref, (0,), i, lanes, _valid(dgc * w2 + _next(dgc, 1) * w1 + _next(dgc, 2) * w0))
            for k, shifted in enumerate((g2, g1, g)):
                acc_ref[k, :, lanes] += _fold8(_valid(dgc * shifted))
            acc_ref[3, :, lanes] += _fold8(_valid(dgc))

        _for_chunks(t, chunk)
        _write_col_sums(acc_ref, [(dw_ref, 0), (dw_ref, 1), (dw_ref, 2), (db_ref, 0)])

    wspec = pl.BlockSpec((3, TC), lambda j: (0, j))
    bspec = pl.BlockSpec((1, TC), lambda j: (0, j))
    return _tc_call(
        body, name=name, grid=(F_FF // TC,), in_specs=[_col(2, t), wspec, bspec, _col(None, t)],
        out_specs=[_col(2, t), wspec, bspec],
        out_shape=[jax.ShapeDtypeStruct((2, t, F_FF), BF16), jax.ShapeDtypeStruct((3, F_FF), F32),
                   jax.ShapeDtypeStruct((1, F_FF), F32)],
        scratch_shapes=[_staging(3, t), pltpu.VMEM((4, 8, TC), F32)], compiler_params=_cp("parallel"),
    )(up, w, bias, da)


ATT_TQ = 256
ATT_SCALE = (QK_NOPE + QK_ROPE) ** -0.5


def _key_ranges(lvl):
    lo = lvl * ATT_TQ
    return ([(0, lo, False)] if lvl else []) + [(lo, lo + ATT_TQ, True)]


def _fill_keys(k_ref, kn_ref, kr_ref):
    @pl.when(pl.program_id(1) == 0)
    def _():
        k_ref[:, :QK_NOPE] = kn_ref[...]
        k_ref[:, QK_NOPE:] = kr_ref[...]


def _attn_probs(q, k_ref, lvl):
    scores = []
    for lo, hi, diagonal in _key_ranges(lvl):
        s = lax.dot_general(q, k_ref[lo:hi, :], NT_DIMS, preferred_element_type=F32) * ATT_SCALE
        if diagonal:
            row = lax.broadcasted_iota(jnp.int32, s.shape, 0)
            col = lax.broadcasted_iota(jnp.int32, s.shape, 1)
            seen = lax.shift_right_logical(col, CHUNK_SHIFT) <= lax.shift_right_logical(row, CHUNK_SHIFT)
            s = jnp.where(seen, s, NEG_INF)
        scores.append(s)
    m = jnp.max(scores[0], axis=1, keepdims=True)
    for s in scores[1:]:
        m = jnp.maximum(m, jnp.max(s, axis=1, keepdims=True))
    ps = [jnp.exp(s - m) for s in scores]
    total = jnp.sum(ps[0], axis=1, keepdims=True)
    for p in ps[1:]:
        total = total + jnp.sum(p, axis=1, keepdims=True)
    inv = 1.0 / total
    return [p * inv for p in ps]


def _per_query_block(qi, n_blocks, branch):
    for lvl in range(n_blocks):
        pl.when(qi == lvl)(lambda lvl=lvl: branch(lvl))


def _attn_specs(t):
    q = pl.BlockSpec((ATT_TQ, HEAD_PAD), lambda h, i: (i, h))
    kn = pl.BlockSpec((None, t, QK_NOPE), lambda h, i: (0, 0, h))
    kr = pl.BlockSpec((t, LANES), lambda h, i: (0, 0))
    v = pl.BlockSpec((None, t, V_HEAD), lambda h, i: (1, 0, h))
    o = pl.BlockSpec((ATT_TQ, V_HEAD), lambda h, i: (i, h))
    return q, kn, kr, v, o


def _attn_fwd(q, knv, kr):
    t = q.shape[0]

    def body(q_ref, kn_ref, kr_ref, v_ref, o_ref, k_ref):
        _fill_keys(k_ref, kn_ref, kr_ref)

        def branch(lvl):
            ps = _attn_probs(q_ref[...], k_ref, lvl)
            o = None
            for p, (lo, hi, _) in zip(ps, _key_ranges(lvl)):
                part = jnp.dot(p.astype(BF16), v_ref[lo:hi, :], preferred_element_type=F32)
                o = part if o is None else o + part
            o_ref[...] = o.astype(BF16)

        _per_query_block(pl.program_id(1), t // ATT_TQ, branch)

    qs, kns, krs, vs, os_ = _attn_specs(t)
    return _tc_call(
        body, name="attn_fwd", grid=(N_HEADS, t // ATT_TQ), in_specs=[qs, kns, krs, vs], out_specs=os_,
        out_shape=jax.ShapeDtypeStruct((t, N_HEADS * V_HEAD), BF16), scratch_shapes=[pltpu.VMEM((t, HEAD_PAD), BF16)],
        compiler_params=_cp("parallel", "arbitrary"),
    )(q, knv, kr, knv)


def _attn_bwd(q, knv, kr, do, cos, sin):
    t = q.shape[0]

    def body(q_ref, kn_ref, kr_ref, v_ref, do_ref, c_ref, s_ref, dq_ref, dknv_ref, dkr_ref, k_ref, dk_ref):
        h, qi = pl.program_id(0), pl.program_id(1)
        _fill_keys(k_ref, kn_ref, kr_ref)

        @pl.when(qi == 0)
        def _():
            dknv_ref[1] = jnp.zeros((t, V_HEAD), F32)
            dk_ref[...] = jnp.zeros_like(dk_ref)

        @pl.when((qi == 0) & (h == 0))
        def _():
            dkr_ref[...] = jnp.zeros_like(dkr_ref)

        def branch(lvl):
            qv, dov = q_ref[...], do_ref[...]
            ranges = _key_ranges(lvl)
            ps = _attn_probs(qv, k_ref, lvl)
            dps = [lax.dot_general(dov, v_ref[lo:hi, :], NT_DIMS, preferred_element_type=F32) for lo, hi, _ in ranges]
            di = None
            for p, dp in zip(ps, dps):
                part = jnp.sum(p * dp, axis=1, keepdims=True)
                di = part if di is None else di + part
            dq = None
            for p, dp, (lo, hi, _) in zip(ps, dps, ranges):
                ds = (p * (dp - di) * ATT_SCALE).astype(BF16)
                part = jnp.dot(ds, k_ref[lo:hi, :], preferred_element_type=F32)
                dq = part if dq is None else dq + part
                dk_ref[lo:hi, :] += lax.dot_general(ds, qv, TN_DIMS, preferred_element_type=F32)
                dknv_ref[1, lo:hi, :] += lax.dot_general(p.astype(BF16), dov, TN_DIMS, preferred_element_type=F32)
            dq_ref[:, :QK_NOPE] = dq[:, :QK_NOPE].astype(BF16)
            dq_ref[:, QK_NOPE:] = _rope_bwd_math(dq[:, QK_NOPE:], c_ref[...], s_ref[...]).astype(BF16)

        _per_query_block(qi, t // ATT_TQ, branch)

        @pl.when(qi == t // ATT_TQ - 1)
        def _():
            dknv_ref[0] = dk_ref[:, :QK_NOPE]
            dkr_ref[...] += dk_ref[:, QK_NOPE:]

    qs, kns, krs, vs, os_ = _attn_specs(t)
    tab = pl.BlockSpec((ATT_TQ, LANES), lambda h, i: (i, 0))
    return _tc_call(
        body, name="attn_bwd", grid=(N_HEADS, t // ATT_TQ), in_specs=[qs, kns, krs, vs, os_, tab, tab],
        out_specs=[qs, pl.BlockSpec((2, t, QK_NOPE), lambda h, i: (0, 0, h)), krs],
        out_shape=[jax.ShapeDtypeStruct((t, N_HEADS * HEAD_PAD), BF16),
                   jax.ShapeDtypeStruct((2, t, N_HEADS * QK_NOPE), F32), jax.ShapeDtypeStruct((t, LANES), F32)],
        scratch_shapes=[pltpu.VMEM((t, HEAD_PAD), BF16), pltpu.VMEM((t, HEAD_PAD), F32)],
        compiler_params=_cp("arbitrary", "arbitrary"),
    )(q, knv, kr, knv, do, cos, sin)


def _adam_math(w, g, m, v):
    nm = ADAM_B1 * m + (1.0 - ADAM_B1) * g
    nv = ADAM_B2 * v + (1.0 - ADAM_B2) * (g * g)
    m_hat = nm / (1.0 - ADAM_B1 ** ADAM_STEP)
    v_hat = nv / (1.0 - ADAM_B2 ** ADAM_STEP)
    return -ADAM_LR * (m_hat / (jnp.sqrt(v_hat) + ADAM_EPS) + ADAM_WD * w), nm, nv


def _adamw_small(w, g, m, v):
    def body(w_ref, g_ref, m_ref, v_ref, d_ref, nm_ref, nv_ref):
        d_ref[...], nm_ref[...], nv_ref[...] = _adam_math(w_ref[...], g_ref[...], m_ref[...], v_ref[...])

    shp = jax.ShapeDtypeStruct(w.shape, F32)
    return _tc_call(body, name="adamw_small", out_shape=[shp] * 3)(w, g, m, v)


ADAM_SPLIT = 4


def _adamw_shards(ids, items, name):
    n = len(items)

    def body(ids_ref, *refs):
        outs = refs[len(refs) - 4 * n:]
        mine = pl.program_id(0) == ids_ref[0]
        for i in range(n):
            w_ref, m_ref, v_ref, gm_ref, gs_ref = refs[5 * i:5 * i + 5]
            g_ref, d_ref, nm_ref, nv_ref = outs[4 * i:4 * i + 4]

            @pl.when(mine)
            def _(g_ref=g_ref, gm_ref=gm_ref):
                g_ref[...] = gm_ref[...]

            @pl.when(jnp.logical_not(mine))
            def _(g_ref=g_ref, gs_ref=gs_ref):
                g_ref[...] = gs_ref[...]

            d_ref[...], nm_ref[...], nv_ref[...] = _adam_math(w_ref[...], g_ref[...], m_ref[...], v_ref[...])

    in_specs, out_specs, out_shape, args, carried, aliases = [], [], [], [ids], [], {}
    for i, it in enumerate(items):
        w = it["w"]
        r, c = w.shape[-2:]
        tr = r // 2 // ADAM_SPLIT
        assert tr % 8 == 0, (name, w.shape)
        layer = it.get("layer")
        if layer is None:
            wspec = pl.BlockSpec((tr, c), lambda h, k, ids: (h * ADAM_SPLIT + k, 0))
        else:
            wspec = pl.BlockSpec((None, tr, c), lambda h, k, ids, layer=layer: (layer, h * ADAM_SPLIT + k, 0))
        gspec = pl.BlockSpec((tr, c), lambda h, k, ids: (k, 0))
        in_specs += [wspec] * 3 + [gspec] * 2
        args += [w, it["m"], it["v"], it["g_mine"], it["g_sib"]]
        out_specs += [wspec] * 4
        out_shape += [jax.ShapeDtypeStruct(w.shape, F32)] * 4
        if it.get("prev") is not None:
            for k, p in enumerate(it["prev"]):
                aliases[1 + 5 * n + len(carried)] = 4 * i + k
                carried.append(p)
    res = _tc_call(
        body, name=name, prefetch=1, grid=(2, ADAM_SPLIT), in_specs=in_specs + [ANY] * len(carried),
        out_specs=out_specs, out_shape=out_shape, input_output_aliases=aliases,
        compiler_params=_cp("parallel", "parallel"),
    )(*args, *carried)
    return [res[4 * i:4 * i + 4] for i in range(n)]


def _peer_chip(k_me, j):
    return k_me ^ jnp.where(j == 0, 2, jnp.where(j == 1, 1, 3))


def _pair_sums(ids, gs, ras, name):
    n = len(gs)

    def body(ids_ref, *refs):
        for i in range(n):
            g_ref, ra_ref, o_ref = refs[2 * i], refs[2 * i + 1], refs[2 * n + i]
            o_ref[...] = (g_ref[...].astype(F32) + ra_ref[...].astype(F32)).astype(BF16)

    in_specs, out_specs, out_shape = [], [], []
    for g in gs:
        half, c = g.shape[1] // 2, g.shape[2]
        in_specs += [pl.BlockSpec((None, half, c), lambda j, ids: (_peer_chip(ids[1], j), ids[0], 0)),
                     pl.BlockSpec((None, half, c), lambda j, ids: (_peer_chip(ids[1], j), 0, 0))]
        out_specs.append(pl.BlockSpec((None, half, c), lambda j, ids: (j, 0, 0)))
        out_shape.append(jax.ShapeDtypeStruct((3, half, c), BF16))
    return _tc_call(
        body, name=name, prefetch=1, grid=(3,), in_specs=in_specs, out_specs=out_specs, out_shape=out_shape,
        compiler_params=_cp("parallel"),
    )(ids, *[a for pair in zip(gs, ras) for a in pair])


def _chip_sums(ids, gs, ras, rbs, name):
    n = len(gs)

    def body(ids_ref, *refs):
        for i in range(n):
            g_ref, ra_ref, rb_ref, o_ref = refs[3 * i], refs[3 * i + 1], refs[3 * i + 2], refs[3 * n + i]
            acc = g_ref[...].astype(F32) + ra_ref[...].astype(F32)
            for j in range(3):
                acc = acc + rb_ref[j].astype(F32)
            o_ref[...] = acc

    in_specs, out_specs, out_shape = [], [], []
    for g in gs:
        half, c = g.shape[1] // 2, g.shape[2]
        in_specs += [pl.BlockSpec((None, half, c), lambda i, ids: (ids[1], ids[0], 0)),
                     pl.BlockSpec((None, half, c), lambda i, ids: (ids[1], 0, 0)),
                     pl.BlockSpec((3, half, c), lambda i, ids: (0, 0, 0))]
        out_specs.append(pl.BlockSpec((half, c), lambda i, ids: (0, 0)))
        out_shape.append(jax.ShapeDtypeStruct((half, c), F32))
    return _tc_call(
        body, name=name, prefetch=1, grid=(1,), in_specs=in_specs, out_specs=out_specs, out_shape=out_shape,
        compiler_params=_cp("arbitrary"),
    )(ids, *[a for trio in zip(gs, ras, rbs) for a in trio])


def _position():
    x, y, c = lax.axis_index("x"), lax.axis_index("y"), lax.axis_index("c")
    chips = [(1 - x, y), (x, 1 - y), (1 - x, 1 - y)]
    return x, y, c, chips


def _shard_half(ref, wm, h):
    if wm.kind == "tiny":
        return ref
    if wm.nl == 2:
        return ref.at[h]
    return ref.at[pl.ds(pl.multiple_of(h * (wm.k // 2), 16), wm.k // 2), :]


def _region(full, wm, s, h):
    if wm.kind == "tiny":
        return full.at[s]
    cols = pl.ds(pl.multiple_of(s * wm.n, LANES), wm.n) if wm.kind == "col" else slice(None)
    if wm.nl == 2:
        rows = pl.ds(pl.multiple_of(s * wm.k, 16), wm.k) if wm.kind == "row" else slice(None)
        return full.at[slice(None) if h is None else h, rows, cols]
    if wm.kind == "col":
        rows = slice(None) if h is None else pl.ds(pl.multiple_of(h * (wm.k // 2), 16), wm.k // 2)
    elif h is None:
        rows = pl.ds(pl.multiple_of(s * wm.k, 16), wm.k)
    else:
        rows = pl.ds(pl.multiple_of(s * wm.k + h * (wm.k // 2), 16), wm.k // 2)
    return full.at[rows, cols]


def _full_shape(wm):
    if wm.kind == "tiny":
        return (N_CHIPS, wm.k, wm.n)
    shape = (wm.k, N_CHIPS * wm.n) if wm.kind == "col" else (N_CHIPS * wm.k, wm.n)
    return shape if wm.nl == 1 else (wm.nl,) + shape


def _handshake(peers):
    barrier = pltpu.get_barrier_semaphore()
    for peer in peers:
        pl.semaphore_signal(barrier, inc=1, device_id=peer, device_id_type=MESH)
    pl.semaphore_wait(barrier, len(peers))


def _all_gather_group(gi, shards):
    wms = AG_GROUPS[gi]
    nw = len(wms)

    def body(*refs):
        sh, full = refs[:nw], refs[nw:2 * nw]
        ici_s, ici_r, pass_s, pass_r, own_s, own_r = refs[2 * nw:]
        x, y, c, chips = _position()
        me, sibling = 2 * x + y, (x, y, 1 - c)
        _handshake([(*chip, c) for chip in chips] + [sibling])

        def rcopy(src, dst, s_sem, r_sem, to):
            return pltpu.make_async_remote_copy(src_ref=src, dst_ref=dst, send_sem=s_sem, recv_sem=r_sem,
                                                device_id=to, device_id_type=MESH)

        started = []
        for i, wm in enumerate(wms):
            for j, chip in enumerate(chips):
                started.append(rcopy(_shard_half(sh[i], wm, c), _region(full[i], wm, me, c),
                                     ici_s.at[i, j], ici_r.at[i, j], (*chip, c)))
                started[-1].start()
            started.append(rcopy(sh[i], _region(full[i], wm, me, None), own_s.at[i], own_r.at[i], sibling))
            started[-1].start()
        for i, wm in enumerate(wms):
            for j, chip in enumerate(chips):
                got = _region(full[i], wm, 2 * chip[0] + chip[1], c)
                rcopy(got, got, ici_s.at[i, j], ici_r.at[i, j], sibling).wait_recv()
                if wm.kind != "tiny":
                    started.append(rcopy(got, got, pass_s.at[i, j], pass_r.at[i, j], sibling))
                    started[-1].start()
        for i, wm in enumerate(wms):
            mine = _region(full[i], wm, me, None)
            rcopy(mine, mine, own_s.at[i], own_r.at[i], sibling).wait_recv()
            for j, chip in enumerate(chips):
                if wm.kind != "tiny":
                    got = _region(full[i], wm, 2 * chip[0] + chip[1], 1 - c)
                    rcopy(got, got, pass_s.at[i, j], pass_r.at[i, j], sibling).wait_recv()
        for cp in started:
            cp.wait_send()

    return pl.kernel(
        body, out_type=[jax.ShapeDtypeStruct(_full_shape(wm), s.dtype) for wm, s in zip(wms, shards)],
        mesh=plsc.ScalarSubcoreMesh(axis_name="sequencer", num_cores=1), name=f"ag_group{gi}",
        scratch_types=[pltpu.SemaphoreType.DMA((nw, 3))] * 4 + [pltpu.SemaphoreType.DMA((nw,))] * 2,
        compiler_params=pltpu.CompilerParams(collective_id=gi),
    )(*shards)


def _sequencer_call(body, name, cid, out_types, scratch, args):
    return pl.kernel(
        body, out_type=out_types, mesh=plsc.ScalarSubcoreMesh(axis_name="sequencer", num_cores=1), name=name,
        scratch_types=scratch, compiler_params=pltpu.CompilerParams(collective_id=cid),
    )(*args)


def _pair_exchange(gs, tag, cid):
    n = len(gs)

    def body(*refs):
        g, out, send_sems, recv_sems = refs[:n], refs[n:2 * n], refs[2 * n], refs[2 * n + 1]
        x, y, c, _ = _position()
        _handshake([(x, y, 1 - c)])
        cps = []
        for i in range(n):
            half = g[i].shape[1] // 2
            cps.append(pltpu.make_async_remote_copy(
                src_ref=g[i].at[:, pl.ds(pl.multiple_of((1 - c) * half, 16), half), :], dst_ref=out[i],
                send_sem=send_sems.at[i], recv_sem=recv_sems.at[i], device_id=(x, y, 1 - c), device_id_type=MESH))
            cps[-1].start()
        for cp in cps:
            cp.wait()

    return _sequencer_call(
        body, f"rs_pair_exchange{tag}", cid,
        [jax.ShapeDtypeStruct((a.shape[0], a.shape[1] // 2, a.shape[2]), a.dtype) for a in gs],
        [pltpu.SemaphoreType.DMA((n,)), pltpu.SemaphoreType.DMA((n,))], gs)


def _chip_exchange(ss, tag, cid):
    n = len(ss)

    def body(*refs):
        s, out, send_sems, recv_sems = refs[:n], refs[n:2 * n], refs[2 * n], refs[2 * n + 1]
        x, y, c, chips = _position()
        _handshake([(*chip, c) for chip in chips])
        cps = []
        for i in range(n):
            for j, chip in enumerate(chips):
                cps.append(pltpu.make_async_remote_copy(
                    src_ref=s[i].at[j], dst_ref=out[i].at[j], send_sem=send_sems.at[i, j], recv_sem=recv_sems.at[i, j],
                    device_id=(*chip, c), device_id_type=MESH))
                cps[-1].start()
        for cp in cps:
            cp.wait()

    return _sequencer_call(
        body, f"rs_chip_exchange{tag}", cid, [jax.ShapeDtypeStruct(a.shape, a.dtype) for a in ss],
        [pltpu.SemaphoreType.DMA((n, 3)), pltpu.SemaphoreType.DMA((n, 3))], ss)


def _pair_swap(g8s, tag, cid):
    n = len(g8s)

    def body(*refs):
        g, out, send_sems, recv_sems = refs[:n], refs[n:2 * n], refs[2 * n], refs[2 * n + 1]
        x, y, c, _ = _position()
        _handshake([(x, y, 1 - c)])
        cps = []
        for i in range(n):
            cps.append(pltpu.make_async_remote_copy(
                src_ref=g[i], dst_ref=out[i], send_sem=send_sems.at[i], recv_sem=recv_sems.at[i],
                device_id=(x, y, 1 - c), device_id_type=MESH))
            cps[-1].start()
        for cp in cps:
            cp.wait()

    return _sequencer_call(
        body, f"rs_pair_swap{tag}", cid, [jax.ShapeDtypeStruct(a.shape, a.dtype) for a in g8s],
        [pltpu.SemaphoreType.DMA((n,)), pltpu.SemaphoreType.DMA((n,))], g8s)


def _all_reduce_small(vec, name):
    r, cols = vec.shape

    def body(v_ref, o_ref, gath, send_sems, recv_sems):
        x, y, c, _ = _position()
        me = 4 * x + 2 * y + c
        gath[me] = v_ref[...]
        cps = []
        for rel in range(1, N_DEV):
            peer = (x ^ (rel >> 2), y ^ ((rel >> 1) & 1), c ^ (rel & 1))
            cps.append(pltpu.make_async_remote_copy(
                src_ref=v_ref, dst_ref=gath.at[me], send_sem=send_sems.at[rel - 1], recv_sem=recv_sems.at[rel - 1],
                device_id=peer, device_id_type=MESH))
        for cp in cps:
            cp.start()
        for rel in range(1, N_DEV):
            pltpu.make_async_remote_copy(
                src_ref=v_ref, dst_ref=gath.at[me ^ rel], send_sem=send_sems.at[rel - 1],
                recv_sem=recv_sems.at[rel - 1], device_id=(x, y, c), device_id_type=MESH).wait_recv()
        for cp in cps:
            cp.wait_send()
        acc = gath[0]
        for d in range(1, N_DEV):
            acc = acc + gath[d]
        o_ref[...] = acc

    vm = pl.BlockSpec(memory_space=pltpu.VMEM)
    return _tc_call(
        body, name=name, in_specs=[vm], out_specs=vm, out_shape=jax.ShapeDtypeStruct((r, cols), F32),
        scratch_shapes=[pltpu.VMEM((N_DEV, r, cols), F32), pltpu.SemaphoreType.DMA((N_DEV - 1,)),
                        pltpu.SemaphoreType.DMA((N_DEV - 1,))],
    )(vec)


def _rope_tables(positions):
    half = QK_ROPE // 2
    inv_freq = 1.0 / (ROPE_THETA ** (jnp.arange(half, dtype=F32) / half))
    ang = positions.astype(F32)[:, None] * inv_freq
    zeros = jnp.zeros((positions.shape[0], LANES - QK_ROPE), F32)
    cos, sin = jnp.cos(ang), jnp.sin(ang)
    return jnp.concatenate([cos, cos, zeros], axis=1), jnp.concatenate([sin, sin, zeros], axis=1)


def _local_step(x, positions, tgt, wf, small, rs):
    cos, sin = _rope_tables(positions)
    w_in, w_out = wf["sc_w_in"], wf["sc_w_out"]
    w_ups, w_downs = (wf["ffn_w_up0"], wf["ffn_w_up1"]), (wf["ffn_w_down0"], wf["ffn_w_down1"])
    w_kv, w_ukv, w_dq, w_uq, w_o = wf["w_kv"], wf["w_ukv"], wf["w_dq"], wf["w_uq"], wf["w_o"]
    attn_norm, ffn_norm = small["attn_norm"], small["ffn_norm"]
    conv_b = small["ffn_conv_b"]

    def ffn_fwd(h, l):
        hf = _rms_fwd(h, ffn_norm[l:l + 1], f"ffn{l}_norm")
        up, a = _ffn_up_gate(hf, w_ups[l], small["ffn_conv_w"][l], conv_b[l:l + 1], f"ffn{l}_up_gate")
        return _nn(f"ffn{l}_down", a, w_downs[l], F32, add=h), (hf, up, a)

    def ffn_bwd(h, dh_out, dh_out_b, l, saved, gi, hooks):
        run = lambda stage: hooks.get(stage, lambda: None)()
        hf, up, a = saved
        da = _nt(f"ffn{l}_down_dx", dh_out_b, w_downs[l], BF16)
        run("down_dx")
        d_down = _tn(f"ffn{l}_down_dw", a, dh_out_b, BF16)
        dup, d_cw, d_cb = _gate_bwd(up, small["ffn_conv_w"][l], conv_b[l:l + 1], da, f"ffn{l}_gate_bwd")
        run("gate_bwd")
        d_up = _dw_ffn_up(f"ffn{l}_up_dw", hf, dup)
        rs.start(gi, {f"ffn_w_down{l}": d_down.reshape(N_CHIPS, F_FF // N_CHIPS, D), f"ffn_w_up{l}": d_up})
        dhf = _nt_parts(f"ffn{l}_up_dx", dup, w_ups[l], BF16)
        run("up_dx")
        dh, dh_b, d_norm = _rms_bwd(h, ffn_norm[l:l + 1], dhf, dh_out, f"ffn{l}_norm_bwd", matmul_copy=True)
        return dh, dh_b, d_cw, d_cb, d_norm

    hn0 = _rms_fwd(x, attn_norm[0:1], "attn0_norm")
    z = _nn_parts("sc_in", hn0, w_in, 3, BF16)
    mix = _scmix_fwd(z, small["sc_conv_w"])
    h1 = _nn("sc_out", mix, w_out, F32, add=x)
    h2, ffn0_saved = ffn_fwd(h1, 0)

    hn1, hk, cq_pre, cq, q, kvpre, ckv, kr, knv = _attn_prep(
        h2, attn_norm[1:2], small["kv_in_norm"], w_dq, small["q_latent_norm"], w_uq, w_kv, small["kv_latent_norm"],
        w_ukv, cos, sin)
    o = _attn_fwd(q, knv, kr)
    h3 = _nn("attn_out", o, w_o, F32, add=h2)
    h4, ffn1_saved = ffn_fwd(h3, 1)

    loss, dh4, dh4_b, d_final = _loss_head(h4, small["final_norm"], tgt)

    rows = D // N_CHIPS
    dh3, dh3_b, d_cw1, d_cb1, d_fn1 = ffn_bwd(h3, dh4, dh4_b, 1, ffn1_saved, 0, {})

    do = _nt("attn_out_dx", dh3_b, w_o, BF16)
    d_wo = _tn("attn_out_dw", o, dh3_b, BF16)
    rs.pair_sums(0)
    dq, dknv, dkr = _attn_bwd(q, knv, kr, do, cos, sin)
    rs.chip_sums(0)
    dh2, dh2_b, d_wuq, d_wdq, d_wukv, d_wkv, d_an1, d_kvin, d_qln, d_kvln = _attn_prep_bwd(
        dq, dknv, dkr, dh3, h2, hn1, hk, cq_pre, cq, kvpre, ckv, attn_norm[1:2], small["kv_in_norm"], w_dq,
        small["q_latent_norm"], w_uq, w_kv, small["kv_latent_norm"], w_ukv, cos, sin)
    rs.finish(0)
    by_owner = lambda dw: dw.reshape(dw.shape[0], N_CHIPS, -1).transpose(1, 0, 2)
    rs.start(1, {
        "w_o": d_wo.reshape(N_CHIPS, rows, D), "w_uq": by_owner(d_wuq), "w_dq": d_wdq.reshape(N_CHIPS, rows, Q_LORA),
        "w_ukv": by_owner(d_wukv.reshape(2 * KV_LORA, -1)).reshape(N_CHIPS, 2 * KV_LORA, -1),
        "w_kv": d_wkv.reshape(N_CHIPS, rows, KVP),
    })

    dh1, dh1_b, d_cw0, d_cb0, d_fn0 = ffn_bwd(h1, dh2, dh2_b, 0, ffn0_saved, 2, {
        "down_dx": lambda: rs.pair_sums(1), "gate_bwd": lambda: rs.chip_sums(1), "up_dx": lambda: rs.finish(1)})
    rs.pair_sums(2)

    d_wout = _tn("sc_out_dw", mix, dh1_b, BF16)
    dmix = _nt("sc_out_dx", dh1_b, w_out, BF16)
    dz, d_scw = _scmix_bwd(z, small["sc_conv_w"], dmix)
    d_win = _dw_sc_in(hn0, dz)
    rs.start(3, {"sc_w_out": d_wout.reshape(N_CHIPS, rows, D), "sc_w_in": d_win})
    dhn0 = _nt_parts("sc_in_dx", dz, w_in, BF16)
    dx, d_an0 = _rms_bwd(x, attn_norm[0:1], dhn0, dh1, "attn0_norm_bwd")

    small_g = {
        "attn_norm": jnp.concatenate([d_an0, d_an1]), "ffn_norm": jnp.concatenate([d_fn0, d_fn1]),
        "final_norm": d_final, "kv_in_norm": d_kvin, "kv_latent_norm": d_kvln, "q_latent_norm": d_qln,
        "ffn_conv_b": jnp.concatenate([d_cb0, d_cb1]), "sc_conv_w": d_scw, "ffn_conv_w": jnp.stack([d_cw0, d_cw1]),
    }
    return loss, dx, small_g


RS_GROUPS = (("ffn_w_down1", "ffn_w_up1"), ("w_o", "w_uq", "w_dq", "w_ukv", "w_kv"),
             ("ffn_w_down0", "ffn_w_up0"), ("sc_w_out", "sc_w_in"))


class _ReduceScatter:
    def __init__(self, ids, finish):
        self.ids, self.grads, self.step, self.mine, self.sib, self.finish = ids, {}, {}, {}, {}, finish

    def _cid(self, gi):
        return len(AG_GROUPS) + 3 * gi

    def start(self, gi, grads):
        self.grads.update(grads)
        own = [grads[n] for n in RS_GROUPS[gi]]
        self.step[gi] = (own, _pair_exchange(own, gi, self._cid(gi)))

    def pair_sums(self, gi):
        own, ra = self.step[gi]
        sums = _pair_sums(self.ids, own, ra, f"rs_pair_sums{gi}")
        self.step[gi] = (own, ra, _chip_exchange(sums, gi, self._cid(gi) + 1))

    def chip_sums(self, gi):
        own, ra, rb = self.step[gi]
        mine = _chip_sums(self.ids, own, ra, rb, f"rs_chip_sums{gi}")
        self.mine.update(zip(RS_GROUPS[gi], mine))
        self.sib.update(zip(RS_GROUPS[gi], _pair_swap(mine, gi, self._cid(gi) + 2)))

SMALL_REPL = ("attn_norm", "ffn_norm", "final_norm", "kv_in_norm", "kv_latent_norm", "q_latent_norm", "ffn_conv_b")
SMALL_SHARDED = ("sc_conv_w", "ffn_conv_w")
SMALL_ROWS = 256


def _pad_heads(w_uq):
    per_head = w_uq.reshape(Q_LORA, -1, QK_NOPE + QK_ROPE)
    return jnp.pad(per_head, ((0, 0), (0, 0), (0, HEAD_PAD - QK_NOPE - QK_ROPE))).reshape(Q_LORA, -1)


def _pack_kv(w_dkv, w_kr):
    return jnp.concatenate([w_dkv, w_kr, jnp.zeros((w_kr.shape[0], LANES - QK_ROPE), w_kr.dtype)], axis=1)


def kernel(x, positions, attn_norm, ffn_norm, final_norm, sc_w_in, sc_conv_w, sc_w_out, kv_in_norm, w_dkv, kv_latent_norm, w_kr, w_uk, w_uv, w_dq, q_latent_norm, w_uq, w_o, ffn_w_up, ffn_conv_w, ffn_conv_b, ffn_w_down, loss_target, m_attn_norm, m_ffn_norm, m_final_norm, m_sc_w_in, m_sc_conv_w, m_sc_w_out, m_kv_in_norm, m_w_dkv, m_kv_latent_norm, m_w_kr, m_w_uk, m_w_uv, m_w_dq, m_q_latent_norm, m_w_uq, m_w_o, m_ffn_w_up, m_ffn_conv_w, m_ffn_conv_b, m_ffn_w_down, v_attn_norm, v_ffn_norm, v_final_norm, v_sc_w_in, v_sc_conv_w, v_sc_w_out, v_kv_in_norm, v_w_dkv, v_kv_latent_norm, v_w_kr, v_w_uk, v_w_uv, v_w_dq, v_q_latent_norm, v_w_uq, v_w_o, v_ffn_w_up, v_ffn_conv_w, v_ffn_conv_b, v_ffn_w_down):
    names = ("attn_norm", "ffn_norm", "final_norm", "sc_w_in", "sc_conv_w", "sc_w_out", "kv_in_norm", "w_dkv",
             "kv_latent_norm", "w_kr", "w_uk", "w_uv", "w_dq", "q_latent_norm", "w_uq", "w_o", "ffn_w_up",
             "ffn_conv_w", "ffn_conv_b", "ffn_w_down")
    w = dict(zip(names, (attn_norm, ffn_norm, final_norm, sc_w_in, sc_conv_w, sc_w_out, kv_in_norm, w_dkv,
                         kv_latent_norm, w_kr, w_uk, w_uv, w_dq, q_latent_norm, w_uq, w_o, ffn_w_up,
                         ffn_conv_w, ffn_conv_b, ffn_w_down)))
    m = dict(zip(names, (m_attn_norm, m_ffn_norm, m_final_norm, m_sc_w_in, m_sc_conv_w, m_sc_w_out, m_kv_in_norm,
                         m_w_dkv, m_kv_latent_norm, m_w_kr, m_w_uk, m_w_uv, m_w_dq, m_q_latent_norm, m_w_uq, m_w_o,
                         m_ffn_w_up, m_ffn_conv_w, m_ffn_conv_b, m_ffn_w_down)))
    v = dict(zip(names, (v_attn_norm, v_ffn_norm, v_final_norm, v_sc_w_in, v_sc_conv_w, v_sc_w_out, v_kv_in_norm,
                         v_w_dkv, v_kv_latent_norm, v_w_kr, v_w_uk, v_w_uv, v_w_dq, v_q_latent_norm, v_w_uq, v_w_o,
                         v_ffn_w_up, v_ffn_conv_w, v_ffn_conv_b, v_ffn_w_down)))

    _ORDER[0] = None
    ix, iy, ic = lax.axis_index("x"), lax.axis_index("y"), lax.axis_index("c")
    chip = 2 * ix + iy
    ids = jnp.stack([ic, chip]).astype(jnp.int32)

    def shards_of(t):
        return {
            "sc_w_in": t["sc_w_in"][0], "sc_w_out": t["sc_w_out"][0], "ffn_w_up": t["ffn_w_up"],
            "ffn_w_down": t["ffn_w_down"], "w_kv": _pack_kv(t["w_dkv"], t["w_kr"]),
            "w_ukv": jnp.stack([t["w_uk"], t["w_uv"]]), "w_dq": t["w_dq"][0], "w_uq": _pad_heads(t["w_uq"][0]),
            "w_o": t["w_o"][0],
        }

    ws, ms, vs = shards_of(w), shards_of(m), shards_of(v)

    def ag_shard(name):
        if name == "sc_conv_w":
            return sc_conv_w[0]
        if name == "ffn_conv_w":
            return ffn_conv_w.reshape(6, -1)
        if name[:-1] in ("ffn_w_up", "ffn_w_down"):
            return ws[name[:-1]][int(name[-1])].astype(BF16)
        return ws[name].astype(BF16)

    wf = {}
    for gi, wms in enumerate(AG_GROUPS):
        fulls = _all_gather_group(gi, [ag_shard(wm.name) for wm in wms])
        wf.update({wm.name: f for wm, f in zip(wms, fulls)})
    small = {
        "attn_norm": attn_norm, "ffn_norm": ffn_norm, "final_norm": final_norm[None], "kv_in_norm": kv_in_norm[None],
        "kv_latent_norm": kv_latent_norm[None], "q_latent_norm": q_latent_norm, "ffn_conv_b": ffn_conv_b,
        "sc_conv_w": wf["sc_conv_w"].transpose(1, 0, 2).reshape(3, D),
        "ffn_conv_w": wf["ffn_conv_w"].reshape(N_CHIPS, 2, 3, -1).transpose(1, 2, 0, 3).reshape(2, 3, F_FF),
    }

    res = {}

    merged = lambda a: a.reshape(2 * KV_LORA, -1)

    def adamw_group(gi):
        items = []
        for key in RS_GROUPS[gi]:
            n, layer = (key[:-1], int(key[-1])) if key[:-1] in ("ffn_w_up", "ffn_w_down") else (key, None)
            w_, m_, v_ = (merged(t[n]) for t in (ws, ms, vs)) if n == "w_ukv" else (ws[n], ms[n], vs[n])
            items.append(dict(name=n, w=w_, m=m_, v=v_, g_mine=rs.mine[key], g_sib=rs.sib[key], layer=layer,
                              prev=res.get(n)))
        for it, out in zip(items, _adamw_shards(ids, items, f"adamw_group{gi}")):
            res[it["name"]] = out

    rs = _ReduceScatter(ids, adamw_group)
    loss, dx, small_g = _local_step(x[0], positions[0], loss_target[0], wf, small, rs)

    s_order = SMALL_REPL + SMALL_SHARDED
    flat = jnp.concatenate([small_g[n].reshape(-1) for n in s_order] + [loss.reshape(-1)])
    flat = jnp.pad(flat, (0, SMALL_ROWS * LANES - flat.shape[0])).reshape(SMALL_ROWS, LANES)
    red = _all_reduce_small(flat, "ar_small").reshape(-1)
    sg, off = {}, 0
    for n in s_order:
        sz = small_g[n].size
        sg[n] = red[off:off + sz].reshape(small_g[n].shape)
        off += sz
    loss_out = red[off]
    grads = {n: sg[n].reshape(w[n].shape) for n in SMALL_REPL}
    grads["sc_conv_w"] = lax.dynamic_slice_in_dim(sg["sc_conv_w"], chip * (D // N_CHIPS), D // N_CHIPS, axis=1)[None]
    grads["ffn_conv_w"] = lax.dynamic_slice_in_dim(sg["ffn_conv_w"], chip * (F_FF // N_CHIPS), F_FF // N_CHIPS, axis=2)

    rs.chip_sums(2)
    rs.pair_sums(3)
    rs.finish(2)
    rs.chip_sums(3)
    rs.finish(3)
    outs = [grads, {}, {}, {}]
    for k, dst in enumerate(outs):
        for n in ("sc_w_in", "sc_w_out", "w_dq", "w_o"):
            dst[n] = res[n][k][None]
        unpadded = res["w_uq"][k].reshape(Q_LORA, -1, HEAD_PAD)[:, :, :QK_NOPE + QK_ROPE]
        dst["w_uq"] = unpadded.reshape(w_uq.shape)
        dst["ffn_w_up"], dst["ffn_w_down"] = res["ffn_w_up"][k], res["ffn_w_down"][k]
        dst["w_dkv"], dst["w_kr"] = res["w_kv"][k][:, :KV_LORA], res["w_kv"][k][:, KV_LORA:KV_LORA + QK_ROPE]
        dst["w_uk"], dst["w_uv"] = res["w_ukv"][k][:KV_LORA], res["w_ukv"][k][KV_LORA:]
    grads, delta, new_m, new_v = outs

    small_names = SMALL_REPL + SMALL_SHARDED

    def pack_small(tree):
        return jnp.concatenate([tree[n].reshape(-1) for n in small_names]).reshape(-1, LANES)

    small_res = _adamw_small(pack_small(w), pack_small(grads), pack_small(m), pack_small(v))
    for slab, dst in zip(small_res, (delta, new_m, new_v)):
        f, off = slab.reshape(-1), 0
        for n in small_names:
            dst[n] = f[off:off + w[n].size].reshape(w[n].shape)
            off += w[n].size

    _ORDER[0] = None
    return (loss_out, dx[None], *[grads[n] for n in names], *[delta[n] for n in names],
            *[new_m[n] for n in names], *[new_v[n] for n in names])
```

```python
from typing import NamedTuple

import jax
import jax.numpy as jnp
from jax import lax
from jax.experimental import pallas as pl
from jax.experimental.pallas import tpu as pltpu
from jax.experimental.pallas import tpu_sc as plsc

F32 = jnp.float32
BF16 = jnp.bfloat16

T = 2048
D = 1024
F_FF = 2816
N_HEADS = 8
QK_NOPE = 128
QK_ROPE = 64
V_HEAD = 128
Q_LORA = 384
KV_LORA = 256
CHUNK_SHIFT = 6
ROPE_THETA = 10000.0
EPS = 1e-6
NEG_INF = -1e30
HEAD_PAD = 256
KVP = KV_LORA + 128

ADAM_LR = 0.001
ADAM_B1 = 0.9
ADAM_B2 = 0.999
ADAM_EPS = 1e-08
ADAM_WD = 0.01
ADAM_STEP = 10

N_CHIPS = 4
N_DEV = 8
LANES = 128
TC = 256
V7X_VMEM_LIMIT = 56 * 1024 * 1024

MESH = pl.DeviceIdType.MESH
ANY = pl.BlockSpec(memory_space=pl.ANY)


class _W(NamedTuple):
    name: str
    kind: str
    nl: int
    k: int
    n: int


AG_GROUPS = (
    (_W("sc_w_in", "col", 1, D, 3 * D // N_CHIPS), _W("sc_conv_w", "tiny", 1, 3, D // N_CHIPS),
     _W("ffn_conv_w", "tiny", 1, 6, F_FF // N_CHIPS)),
    (_W("sc_w_out", "row", 1, D // N_CHIPS, D),),
    (_W("ffn_w_up0", "col", 1, D, 2 * F_FF // N_CHIPS),),
    (_W("ffn_w_down0", "row", 1, F_FF // N_CHIPS, D),),
    (_W("w_kv", "row", 1, D // N_CHIPS, KVP), _W("w_ukv", "col", 2, KV_LORA, N_HEADS * QK_NOPE // N_CHIPS),
     _W("w_dq", "row", 1, D // N_CHIPS, Q_LORA),
     _W("w_uq", "col", 1, Q_LORA, N_HEADS * HEAD_PAD // N_CHIPS),
     _W("w_o", "row", 1, N_HEADS * V_HEAD // N_CHIPS, D)),
    (_W("ffn_w_up1", "col", 1, D, 2 * F_FF // N_CHIPS), _W("ffn_w_down1", "row", 1, F_FF // N_CHIPS, D)),
)


def _cp(*sem):
    return pltpu.CompilerParams(dimension_semantics=sem, vmem_limit_bytes=V7X_VMEM_LIMIT)


_ORDER = [None]


def _tc_call(body, *, name, out_shape, in_specs=None, out_specs=None, grid=(), scratch_shapes=(), prefetch=0,
             input_output_aliases=None, compiler_params=None):
    def run(*args):
        specs = [pl.BlockSpec(memory_space=pltpu.VMEM)] * (len(args) - prefetch) if in_specs is None else list(in_specs)
        inner, dep = body, _ORDER[0]
        if dep is not None:
            unread = prefetch + len(specs)
            specs, args = specs + [ANY], (*args, dep)

            def inner(*refs):
                return body(*refs[:unread], *refs[unread + 1:])

        kwargs = dict(name=name, out_shape=out_shape, input_output_aliases=input_output_aliases or {},
                      compiler_params=compiler_params)
        if prefetch:
            kwargs["grid_spec"] = pltpu.PrefetchScalarGridSpec(
                num_scalar_prefetch=prefetch, grid=grid, in_specs=specs, out_specs=out_specs,
                scratch_shapes=scratch_shapes)
        else:
            kwargs.update(grid=grid, in_specs=specs, scratch_shapes=scratch_shapes)
            if out_specs is not None:
                kwargs["out_specs"] = out_specs
        out = pl.pallas_call(inner, **kwargs)(*args)
        _ORDER[0] = out[0] if isinstance(out, (list, tuple)) else out
        return out

    return run


def _tile(n, cands):
    for c in cands:
        if n % c == 0:
            return c
    raise ValueError(f"no tile for {n}")


NN_DIMS = (((1,), (0,)), ((), ()))
NT_DIMS = (((1,), (1,)), ((), ()))
TN_DIMS = (((0,), (0,)), ((), ()))
M_TILES = (1024, 512, 384, 256, 128)
N_TILES = (1408, 1024, 768, 512, 384, 256, 128)
MM_BLOCK_BYTES = 36 * 1024 * 1024


def _fit(m, n, block_bytes, m_tiles=M_TILES, n_tiles=N_TILES):
    for tm in [c for c in m_tiles if m % c == 0]:
        for tn in [c for c in n_tiles if n % c == 0]:
            if 2 * block_bytes(tm, tn) + 4 * tm * tn <= MM_BLOCK_BYTES:
                return tm, tn
    raise ValueError(f"no tiles for {m} x {n}")


def _size(x):
    return x.dtype.itemsize


def _mm(name, a, b, dims, grid, a_spec, b_spec, o_spec, o_sds, add=None, red=None, acc_shape=None):
    n_red = None if red is None else grid[red]

    def body(*refs):
        a_ref, b_ref = refs[0], refs[1]
        add_ref = refs[2] if add is not None else None
        o_ref = refs[3] if add is not None else refs[2]
        part = lax.dot_general(a_ref[...].astype(BF16), b_ref[...].astype(BF16), dims, preferred_element_type=F32)
        if red is None:
            if add is not None:
                part = part + add_ref[...]
            o_ref[...] = part.astype(o_ref.dtype)
            return
        acc_ref = refs[-1]
        r = pl.program_id(red)

        @pl.when(r == 0)
        def _():
            acc_ref[...] = part

        @pl.when(r > 0)
        def _():
            acc_ref[...] += part

        @pl.when(r == n_red - 1)
        def _():
            o_ref[...] = acc_ref[...].astype(o_ref.dtype)

    sem = tuple("arbitrary" if ax == red else "parallel" for ax in range(len(grid)))
    in_specs = [a_spec, b_spec] + ([o_spec] if add is not None else [])
    args = (a, b) + ((add,) if add is not None else ())
    return _tc_call(
        body, name=name, grid=grid, in_specs=in_specs, out_specs=o_spec, out_shape=o_sds,
        scratch_shapes=[] if red is None else [pltpu.VMEM(acc_shape, F32)], compiler_params=_cp(*sem),
    )(*args)


def _nn(name, a, b, out_dtype, add=None, lead=None):
    (m, k), n = a.shape, b.shape[-1]
    osz = jnp.dtype(out_dtype).itemsize + (4 if add is not None else 0)
    tm, tn = _fit(m, n, lambda tm, tn: tm * k * _size(a) + k * tn * _size(b) + tm * tn * osz)
    if lead is None:
        b_spec = pl.BlockSpec((k, tn), lambda i, j: (0, j))
    else:
        b_spec = pl.BlockSpec((None, k, tn), lambda i, j: (lead, 0, j))
    return _mm(name, a, b, NN_DIMS, (m // tm, n // tn), pl.BlockSpec((tm, k), lambda i, j: (i, 0)), b_spec,
               pl.BlockSpec((tm, tn), lambda i, j: (i, j)), jax.ShapeDtypeStruct((m, n), out_dtype), add=add)


def _nn_parts(name, a, b, parts, out_dtype, lead=None, stacked=False):
    m, k = a.shape
    c = b.shape[-1] if stacked else b.shape[-1] // parts
    osz = jnp.dtype(out_dtype).itemsize
    tm, tn = _fit(m, c, lambda tm, tn: tm * k * _size(a) + k * tn * _size(b) + tm * tn * osz)
    nb = c // tn
    if stacked:
        b_spec = pl.BlockSpec((None, k, tn), lambda i, p, j: (p, 0, j))
    elif lead is None:
        b_spec = pl.BlockSpec((k, tn), lambda i, p, j: (0, p * nb + j))
    else:
        b_spec = pl.BlockSpec((None, k, tn), lambda i, p, j: (lead, 0, p * nb + j))
    return _mm(name, a, b, NN_DIMS, (m // tm, parts, nb), pl.BlockSpec((tm, k), lambda i, p, j: (i, 0)), b_spec,
               pl.BlockSpec((None, tm, tn), lambda i, p, j: (p, i, j)), jax.ShapeDtypeStruct((parts, m, c), out_dtype))


def _nt(name, a, b, out_dtype, lead=None):
    (m, k), n = a.shape, b.shape[-2]
    osz = jnp.dtype(out_dtype).itemsize
    tm, tn = _fit(m, n, lambda tm, tn: tm * k * _size(a) + tn * k * _size(b) + tm * tn * osz)
    if lead is None:
        b_spec = pl.BlockSpec((tn, k), lambda i, j: (j, 0))
    else:
        b_spec = pl.BlockSpec((None, tn, k), lambda i, j: (lead, j, 0))
    return _mm(name, a, b, NT_DIMS, (m // tm, n // tn), pl.BlockSpec((tm, k), lambda i, j: (i, 0)), b_spec,
               pl.BlockSpec((tm, tn), lambda i, j: (i, j)), jax.ShapeDtypeStruct((m, n), out_dtype))


def _tn(name, a, b, out_dtype):
    (k, m), n = a.shape, b.shape[1]
    osz = jnp.dtype(out_dtype).itemsize
    tm, tn = _fit(m, n, lambda tm, tn: k * tm * _size(a) + k * tn * _size(b) + tm * tn * osz,
                  m_tiles=(512, 384, 256, 128), n_tiles=(n,) + N_TILES)
    return _mm(name, a, b, TN_DIMS, (m // tm, n // tn), pl.BlockSpec((k, tm), lambda i, j: (0, i)),
               pl.BlockSpec((k, tn), lambda i, j: (0, j)), pl.BlockSpec((tm, tn), lambda i, j: (i, j)),
               jax.ShapeDtypeStruct((m, n), out_dtype))


def _nn_add_norm(name, a, b, add, g):
    (m, k), n = a.shape, b.shape[1]
    tm = 512

    def body(a_ref, b_ref, add_ref, g_ref, h_ref, hn_ref):
        h = jnp.dot(a_ref[...], b_ref[...], preferred_element_type=F32) + add_ref[...]
        h_ref[...] = h
        hn_ref[...] = _rms_rows(h, g_ref[...]).astype(BF16)

    rows = lambda w: pl.BlockSpec((tm, w), lambda i: (i, 0))
    return _tc_call(
        body, name=name, grid=(m // tm,),
        in_specs=[rows(k), pl.BlockSpec((k, n), lambda i: (0, 0)), rows(n), pl.BlockSpec((1, n), lambda i: (0, 0))],
        out_specs=[rows(n), rows(n)],
        out_shape=[jax.ShapeDtypeStruct((m, n), F32), jax.ShapeDtypeStruct((m, n), BF16)], compiler_params=_cp("parallel"),
    )(a, b, add, g)


def _nn_add_loss(name, a, b, add, g, tgt):
    (m, k), n = a.shape, b.shape[1]
    tm = 512

    def body(a_ref, b_ref, add_ref, g_ref, t_ref, loss_ref, dh_ref, dhb_ref, dg_ref):
        xv = jnp.dot(a_ref[...], b_ref[...], preferred_element_type=F32) + add_ref[...]
        gv = g_ref[...]
        r = lax.rsqrt(jnp.mean(xv * xv, axis=1, keepdims=True) + EPS)
        err = xv * r * gv - t_ref[...]
        part = 0.5 * jnp.sum(jnp.mean(err * err, axis=1, keepdims=True), axis=0, keepdims=True)
        dx, dg = _rms_bwd_math(xv, gv, err * (1.0 / n))
        dh_ref[...] = dx
        dhb_ref[...] = dx.astype(BF16)

        @pl.when(pl.program_id(0) == 0)
        def _():
            dg_ref[...] = jnp.zeros_like(dg_ref)
            loss_ref[...] = jnp.zeros_like(loss_ref)

        dg_ref[...] += dg
        loss_ref[...] += jnp.broadcast_to(part, loss_ref.shape)

    rows = lambda w: pl.BlockSpec((tm, w), lambda i: (i, 0))
    vec = pl.BlockSpec((1, n), lambda i: (0, 0))
    return _tc_call(
        body, name=name, grid=(m // tm,),
        in_specs=[rows(k), pl.BlockSpec((k, n), lambda i: (0, 0)), rows(n), vec, rows(n)],
        out_specs=[pl.BlockSpec((1, LANES), lambda i: (0, 0)), rows(n), rows(n), vec],
        out_shape=[jax.ShapeDtypeStruct((1, LANES), F32), jax.ShapeDtypeStruct((m, n), F32),
                   jax.ShapeDtypeStruct((m, n), BF16), jax.ShapeDtypeStruct((1, n), F32)],
        compiler_params=_cp("arbitrary"),
    )(a, b, add, g, tgt)


def _dx_norm_bwd(name, a, b, x, g, add):
    parts, t, c = a.shape
    d = b.shape[0]
    tm = 256

    def body(a_ref, b_ref, x_ref, g_ref, add_ref, dx_ref, dxb_ref, dg_ref):
        dy = None
        for p in range(parts):
            part = lax.dot_general(a_ref[p], b_ref[:, p * c:(p + 1) * c], NT_DIMS, preferred_element_type=F32)
            dy = part if dy is None else dy + part
        dx, dg = _rms_bwd_math(x_ref[...], g_ref[...], dy)
        dx = dx + add_ref[...]
        dx_ref[...] = dx
        dxb_ref[...] = dx.astype(BF16)

        @pl.when(pl.program_id(0) == 0)
        def _():
            dg_ref[...] = jnp.zeros_like(dg_ref)

        dg_ref[...] += dg

    rows = pl.BlockSpec((tm, d), lambda i: (i, 0))
    vec = pl.BlockSpec((1, d), lambda i: (0, 0))
    return _tc_call(
        body, name=name, grid=(t // tm,),
        in_specs=[pl.BlockSpec((parts, tm, c), lambda i: (0, i, 0)), pl.BlockSpec(b.shape, lambda i: (0, 0)), rows, vec,
                  rows],
        out_specs=[rows, rows, vec],
        out_shape=[jax.ShapeDtypeStruct((t, d), F32), jax.ShapeDtypeStruct((t, d), BF16),
                   jax.ShapeDtypeStruct((1, d), F32)],
        compiler_params=_cp("arbitrary"),
    )(a, b, x, g, add)


def _dw_sc_in(hn, dz):
    t, tn, tm = hn.shape[0], TC, 512
    per_part, per_chip = D // tn, 3 * D // N_CHIPS // tn
    return _mm("sc_in_dw", hn, dz, TN_DIMS, (D // tm, 3 * D // tn), pl.BlockSpec((t, tm), lambda i, j: (0, i)),
               pl.BlockSpec((None, t, tn), lambda i, j: (j // per_part, 0, j % per_part)),
               pl.BlockSpec((None, tm, tn), lambda i, j: (j // per_chip, i, j % per_chip)),
               jax.ShapeDtypeStruct((N_CHIPS, D, 3 * D // N_CHIPS), BF16))


def _dw_ffn_up(name, hf, dup):
    t, tm, ns = hf.shape[0], 512, 2 * F_FF // N_CHIPS
    return _mm(name, hf, dup, TN_DIMS, (N_CHIPS, D // tm), pl.BlockSpec((t, tm), lambda s, i: (0, i)),
               pl.BlockSpec((None, t, ns), lambda s, i: (s // 2, 0, s % 2)),
               pl.BlockSpec((None, tm, ns), lambda s, i: (s, i, 0)), jax.ShapeDtypeStruct((N_CHIPS, D, ns), BF16))


def _rms_fwd(x, g, name):
    t, d = x.shape
    tr = 512

    def body(x_ref, g_ref, o_ref):
        xv = x_ref[...]
        r = lax.rsqrt(jnp.mean(xv * xv, axis=1, keepdims=True) + EPS)
        o_ref[...] = (xv * r * g_ref[...]).astype(o_ref.dtype)

    row = pl.BlockSpec((tr, d), lambda i: (i, 0))
    return _tc_call(
        body, name=name, grid=(t // tr,), in_specs=[row, pl.BlockSpec((1, d), lambda i: (0, 0))],
        out_specs=row, out_shape=jax.ShapeDtypeStruct((t, d), BF16), compiler_params=_cp("parallel"),
    )(x, g)


def _rms_bwd_math(xv, g, dy):
    r = lax.rsqrt(jnp.mean(xv * xv, axis=1, keepdims=True) + EPS)
    xh = xv * r
    gy = dy * g
    dx = r * (gy - xh * jnp.mean(gy * xh, axis=1, keepdims=True))
    dg = jnp.sum(dy * xh, axis=0, keepdims=True)
    return dx, dg


def _rot_half(x):
    lane = lax.broadcasted_iota(jnp.int32, x.shape, 1)
    return jnp.where((lane % QK_ROPE) < QK_ROPE // 2, -pltpu.roll(x, LANES - 32, axis=1),
                     pltpu.roll(x, 32, axis=1))


def _rope_fwd_math(x, cos, sin):
    return x * cos + _rot_half(x) * sin


def _rope_bwd_math(dy, cos, sin):
    return dy * cos - _rot_half(dy * sin)


def _rms_rows(x, g):
    return x * lax.rsqrt(jnp.mean(x * x, axis=1, keepdims=True) + EPS) * g


def _attn_prep(h, g_attn, g_kvin, w_dq, g_ql, w_uq, w_kv, g_kvl, w_ukv, cos, sin):
    t, d = h.shape
    tr = 256
    wq = N_HEADS * HEAD_PAD

    def body(h_ref, ga_ref, gk_ref, wdq_ref, gq_ref, wuq_ref, wkv_ref, gl_ref, wukv_ref, c_ref, s_ref,
             hn_ref, hk_ref, cqp_ref, cq_ref, q_ref, kvp_ref, ckv_ref, kr_ref, knv_ref):
        xv, cv, sv = h_ref[...], c_ref[...], s_ref[...]
        xh = xv * lax.rsqrt(jnp.mean(xv * xv, axis=1, keepdims=True) + EPS)
        hn = (xh * ga_ref[...]).astype(BF16)
        hk = (xh * gk_ref[...]).astype(BF16)
        hn_ref[...], hk_ref[...] = hn, hk
        cq_pre = jnp.dot(hn, wdq_ref[...], preferred_element_type=F32)
        cqp_ref[...] = cq_pre
        cq = _rms_rows(cq_pre, gq_ref[...]).astype(BF16)
        cq_ref[...] = cq
        for hd in range(N_HEADS):
            lo = hd * HEAD_PAD
            qh = jnp.dot(cq, wuq_ref[:, lo:lo + HEAD_PAD], preferred_element_type=F32)
            q_ref[:, lo:lo + QK_NOPE] = qh[:, :QK_NOPE].astype(BF16)
            q_ref[:, lo + QK_NOPE:lo + HEAD_PAD] = _rope_fwd_math(qh[:, QK_NOPE:], cv, sv).astype(BF16)
        kvpre = jnp.dot(hk, wkv_ref[...], preferred_element_type=F32)
        kvp_ref[...] = kvpre
        ckv = _rms_rows(kvpre[:, :KV_LORA], gl_ref[...]).astype(BF16)
        ckv_ref[...] = ckv
        kr_ref[...] = _rope_fwd_math(kvpre[:, KV_LORA:], cv, sv).astype(BF16)
        for p in range(2):
            knv_ref[p] = jnp.dot(ckv, wukv_ref[p], preferred_element_type=F32).astype(BF16)

    rows = lambda w: pl.BlockSpec((tr, w), lambda i: (i, 0))
    whole = lambda a: pl.BlockSpec(a.shape, lambda i: (0,) * a.ndim)
    sds = lambda w, dt: jax.ShapeDtypeStruct((t, w), dt)
    args = (h, g_attn, g_kvin, w_dq, g_ql, w_uq, w_kv, g_kvl, w_ukv, cos, sin)
    return _tc_call(
        body, name="attn_prep", grid=(t // tr,),
        in_specs=[rows(d)] + [whole(a) for a in args[1:9]] + [rows(LANES), rows(LANES)],
        out_specs=[rows(d), rows(d), rows(Q_LORA), rows(Q_LORA), rows(wq), rows(KVP), rows(KV_LORA), rows(LANES),
                   pl.BlockSpec((2, tr, N_HEADS * QK_NOPE), lambda i: (0, i, 0))],
        out_shape=[sds(d, BF16), sds(d, BF16), sds(Q_LORA, F32), sds(Q_LORA, BF16), sds(wq, BF16), sds(KVP, F32),
                   sds(KV_LORA, BF16), sds(LANES, BF16), jax.ShapeDtypeStruct((2, t, N_HEADS * QK_NOPE), BF16)],
        compiler_params=_cp("parallel"),
    )(*args)


def _attn_prep_bwd(dq, dknv, dkr, dh, h, hn, hk, cq_pre, cq, kvpre, ckv, g_attn, g_kvin, w_dq, g_ql, w_uq, w_kv, g_kvl,
                   w_ukv, cos, sin):
    t, d = h.shape
    tr = 256
    n_steps = t // tr
    wq = N_HEADS * HEAD_PAD
    wk = N_HEADS * QK_NOPE

    def body(dq_ref, dknv_ref, dkr_ref, dh_ref, h_ref, hn_ref, hk_ref, cqp_ref, cq_ref, kvp_ref, ckv_ref,
             ga_ref, gk_ref, wdq_ref, gq_ref, wuq_ref, wkv_ref, gl_ref, wukv_ref, c_ref, s_ref,
             dho_ref, dhb_ref, dwuq_ref, dwdq_ref, dwukv_ref, dwkv_ref, dga_ref, dgk_ref, dgq_ref, dgl_ref,
             a_uq, a_dq, a_ukv, a_kv):
        i = pl.program_id(0)

        @pl.when(i == 0)
        def _():
            for ref in (a_uq, a_dq, a_ukv, a_kv, dga_ref, dgk_ref, dgq_ref, dgl_ref):
                ref[...] = jnp.zeros_like(ref)

        dqv = dq_ref[...]
        dcq = lax.dot_general(dqv, wuq_ref[...], NT_DIMS, preferred_element_type=F32)
        a_uq[...] += lax.dot_general(cq_ref[...], dqv, TN_DIMS, preferred_element_type=F32)
        dcq_pre, dg = _rms_bwd_math(cqp_ref[...], gq_ref[...], dcq)
        dgq_ref[...] += dg
        dcq_pre = dcq_pre.astype(BF16)
        dhn = lax.dot_general(dcq_pre, wdq_ref[...], NT_DIMS, preferred_element_type=F32)
        a_dq[...] += lax.dot_general(hn_ref[...], dcq_pre, TN_DIMS, preferred_element_type=F32)
        dckv = None
        for p in range(2):
            dk = dknv_ref[p].astype(BF16)
            part = lax.dot_general(dk, wukv_ref[p], NT_DIMS, preferred_element_type=F32)
            dckv = part if dckv is None else dckv + part
            a_ukv[p] += lax.dot_general(ckv_ref[...], dk, TN_DIMS, preferred_element_type=F32)
        dlat, dg = _rms_bwd_math(kvp_ref[:, :KV_LORA], gl_ref[...], dckv)
        dgl_ref[...] += dg
        dkr_pre = _rope_bwd_math(dkr_ref[...], c_ref[...], s_ref[...])
        dkvpre = jnp.concatenate([dlat, dkr_pre], axis=1).astype(BF16)
        dhk = lax.dot_general(dkvpre, wkv_ref[...], NT_DIMS, preferred_element_type=F32)
        a_kv[...] += lax.dot_general(hk_ref[...], dkvpre, TN_DIMS, preferred_element_type=F32)
        xv = h_ref[...]
        dx1, dg = _rms_bwd_math(xv, ga_ref[...], dhn)
        dga_ref[...] += dg
        dx2, dg = _rms_bwd_math(xv, gk_ref[...], dhk)
        dgk_ref[...] += dg
        dh_new = dh_ref[...] + dx1 + dx2
        dho_ref[...] = dh_new
        dhb_ref[...] = dh_new.astype(BF16)

        @pl.when(i == n_steps - 1)
        def _():
            dwuq_ref[...] = a_uq[...].astype(BF16)
            dwdq_ref[...] = a_dq[...].astype(BF16)
            dwukv_ref[...] = a_ukv[...].astype(BF16)
            dwkv_ref[...] = a_kv[...].astype(BF16)

    rows = lambda w: pl.BlockSpec((tr, w), lambda i: (i, 0))
    whole = lambda shape: pl.BlockSpec(shape, lambda i: (0,) * len(shape))
    weights = (g_attn, g_kvin, w_dq, g_ql, w_uq, w_kv, g_kvl, w_ukv)
    dw_shapes = [(Q_LORA, wq), (d, Q_LORA), (2, KV_LORA, wk), (d, KVP)]
    dg_shapes = [(1, d), (1, d), (1, Q_LORA), (1, KV_LORA)]
    return _tc_call(
        body, name="attn_prep_bwd", grid=(n_steps,),
        in_specs=[rows(wq), pl.BlockSpec((2, tr, wk), lambda i: (0, i, 0)), rows(LANES), rows(d), rows(d), rows(d),
                  rows(d), rows(Q_LORA), rows(Q_LORA), rows(KVP), rows(KV_LORA)]
        + [whole(a.shape) for a in weights] + [rows(LANES), rows(LANES)],
        out_specs=[rows(d), rows(d)] + [whole(s) for s in dw_shapes + dg_shapes],
        out_shape=[jax.ShapeDtypeStruct((t, d), F32), jax.ShapeDtypeStruct((t, d), BF16)]
        + [jax.ShapeDtypeStruct(s, BF16) for s in dw_shapes] + [jax.ShapeDtypeStruct(s, F32) for s in dg_shapes],
        scratch_shapes=[pltpu.VMEM(s, F32) for s in dw_shapes], compiler_params=_cp("arbitrary"),
    )(dq, dknv, dkr, dh, h, hn, hk, cq_pre, cq, kvpre, ckv, *weights, cos, sin)


ROW_CHUNK = 64
HALO = 16
WIN = ROW_CHUNK + 16
LANE_HALVES = (slice(0, LANES), slice(LANES, TC))


def _stage(s_ref, p, src):
    t = src.shape[0]
    s_ref[p, :HALO] = jnp.zeros((HALO, TC), BF16)
    s_ref[p, HALO:HALO + t] = src
    s_ref[p, HALO + t:] = jnp.zeros((HALO, TC), BF16)


def _window(s_ref, p, i, lanes):
    base = pl.multiple_of(i * ROW_CHUNK, ROW_CHUNK)
    return s_ref[p, pl.ds(base, ROW_CHUNK + 2 * HALO), lanes].astype(F32)[8:8 + WIN]


def _valid(x):
    return x[8:8 + ROW_CHUNK]


def _prev(x, k):
    return pltpu.roll(x, k, axis=0)


def _next(x, k):
    return pltpu.roll(x, WIN - k, axis=0)


def _taps(w_ref, lanes):
    return w_ref[0:1, lanes], w_ref[1:2, lanes], w_ref[2:3, lanes]


def _fold8(x):
    return jnp.sum(x.reshape(ROW_CHUNK // 8, 8, x.shape[-1]), axis=0)


def _store_rows(ref, idx, i, lanes, x):
    rows = pl.ds(pl.multiple_of(i * ROW_CHUNK, ROW_CHUNK), ROW_CHUNK)
    ref[(*idx, rows, lanes)] = x.astype(ref.dtype)


def _for_chunks(t, chunk):
    def step(i, carry):
        for lanes in LANE_HALVES:
            chunk(i, lanes)
        return carry

    lax.fori_loop(0, t // ROW_CHUNK, step, 0)


def _write_col_sums(acc_ref, outs):
    for k, (ref, row) in enumerate(outs):
        ref[row:row + 1, :] = jnp.sum(acc_ref[k], axis=0, keepdims=True)


def _shift_down(x, k):
    row = lax.broadcasted_iota(jnp.int32, x.shape, 0)
    return jnp.where(row >= k, pltpu.roll(x, k, axis=0), 0.0)


def _shift_up(x, k):
    n = x.shape[0]
    row = lax.broadcasted_iota(jnp.int32, x.shape, 0)
    return jnp.where(row < n - k, pltpu.roll(x, n - k, axis=0), 0.0)


def _conv3(x, w_ref):
    return _shift_down(x, 2) * w_ref[0:1, :] + _shift_down(x, 1) * w_ref[1:2, :] + x * w_ref[2:3, :]


def _col(parts, t):
    if parts is None:
        return pl.BlockSpec((t, TC), lambda j: (0, j))
    return pl.BlockSpec((parts, t, TC), lambda j: (0, 0, j))


def _staging(parts, t):
    return pltpu.VMEM((parts, t + 2 * HALO, TC), BF16)


def _scmix_fwd(z, w):
    t = z.shape[1]

    def body(z_ref, w_ref, m_ref):
        b, c, u = (z_ref[p].astype(F32) for p in range(3))
        m_ref[...] = (b * _conv3(c * u, w_ref)).astype(BF16)

    return _tc_call(
        body, name="scmix_fwd", grid=(D // TC,), in_specs=[_col(3, t), pl.BlockSpec((3, TC), lambda j: (0, j))],
        out_specs=_col(None, t), out_shape=jax.ShapeDtypeStruct((t, D), BF16), compiler_params=_cp("parallel"),
    )(z, w)


def _scmix_bwd(z, w, dm):
    t = z.shape[1]

    def body(z_ref, w_ref, dm_ref, dz_ref, dw_ref, s_ref, acc_ref):
        for p in range(3):
            _stage(s_ref, p, z_ref[p])
        _stage(s_ref, 3, dm_ref[...])
        acc_ref[...] = jnp.zeros_like(acc_ref)

        def chunk(i, lanes):
            w0, w1, w2 = _taps(w_ref, lanes)
            b, c, u, dm = (_window(s_ref, p, i, lanes) for p in range(4))
            cu = c * u
            cu1, cu2 = _prev(cu, 1), _prev(cu, 2)
            _store_rows(dz_ref, (0,), i, lanes, _valid(dm * (cu2 * w0 + cu1 * w1 + cu * w2)))
            dcv = dm * b
            dcu = dcv * w2 + _next(dcv, 1) * w1 + _next(dcv, 2) * w0
            _store_rows(dz_ref, (1,), i, lanes, _valid(dcu * u))
            _store_rows(dz_ref, (2,), i, lanes, _valid(dcu * c))
            for k, shifted in enumerate((cu2, cu1, cu)):
                acc_ref[k, :, lanes] += _fold8(_valid(dcv * shifted))

        _for_chunks(t, chunk)
        _write_col_sums(acc_ref, [(dw_ref, 0), (dw_ref, 1), (dw_ref, 2)])

    wspec = pl.BlockSpec((3, TC), lambda j: (0, j))
    return _tc_call(
        body, name="scmix_bwd", grid=(D // TC,), in_specs=[_col(3, t), wspec, _col(None, t)],
        out_specs=[_col(3, t), wspec],
        out_shape=[jax.ShapeDtypeStruct((3, t, D), BF16), jax.ShapeDtypeStruct((3, D), F32)],
        scratch_shapes=[_staging(4, t), pltpu.VMEM((3, 8, TC), F32)], compiler_params=_cp("parallel"),
    )(z, w, dm)


def _ffn_up_gate(hf, w_up, w, bias, name):
    t, d = hf.shape
    nb = F_FF // TC

    def body(hf_ref, wg_ref, wv_ref, w_ref, b_ref, up_ref, a_ref, prev_ref):
        @pl.when(pl.program_id(0) == 0)
        def _():
            prev_ref[...] = jnp.zeros_like(prev_ref)

        gc = _conv3(prev_ref[0].astype(F32), w_ref) + b_ref[...]
        a_ref[...] = (gc * jax.nn.sigmoid(gc) * prev_ref[1].astype(F32)).astype(BF16)
        hv = hf_ref[...]
        up_ref[0] = jnp.dot(hv, wg_ref[...], preferred_element_type=F32).astype(BF16)
        up_ref[1] = jnp.dot(hv, wv_ref[...], preferred_element_type=F32).astype(BF16)
        prev_ref[...] = up_ref[...]

    tile = lambda j: jnp.minimum(j, nb - 1)
    gated = lambda j: jnp.maximum(j - 1, 0)
    return _tc_call(
        body, name=name, grid=(nb + 1,),
        in_specs=[pl.BlockSpec((t, d), lambda j: (0, 0)), pl.BlockSpec((d, TC), lambda j: (0, tile(j))),
                  pl.BlockSpec((d, TC), lambda j: (0, nb + tile(j))), pl.BlockSpec((3, TC), lambda j: (0, gated(j))),
                  pl.BlockSpec((1, TC), lambda j: (0, gated(j)))],
        out_specs=[pl.BlockSpec((2, t, TC), lambda j: (0, 0, tile(j))), pl.BlockSpec((t, TC), lambda j: (0, gated(j)))],
        out_shape=[jax.ShapeDtypeStruct((2, t, F_FF), BF16), jax.ShapeDtypeStruct((t, F_FF), BF16)],
        scratch_shapes=[pltpu.VMEM((2, t, TC), BF16)], compiler_params=_cp("arbitrary"),
    )(hf, w_up, w_up, w, bias)


def _gate_bwd(up, w, bias, da, name):
    t = up.shape[1]

    def body(u_ref, w_ref, b_ref, da_ref, du_ref, dw_ref, db_ref, s_ref, acc_ref):
        for p in range(2):
            _stage(s_ref, p, u_ref[p])
        _stage(s_ref, 2, da_ref[...])
        acc_ref[...] = jnp.zeros_like(acc_ref)

        def chunk(i, lanes):
            w0, w1, w2 = _taps(w_ref, lanes)
            g, v, da = (_window(s_ref, p, i, lanes) for p in range(3))
            g1, g2 = _prev(g, 1), _prev(g, 2)
            gc = g2 * w0 + g1 * w1 + g * w2 + b_ref[:, lanes]
            sg = jax.nn.sigmoid(gc)
            _store_rows(du_ref, (1,), i, lanes, _valid(da * (gc * sg)))
            dgc = da * v * (sg * (1.0 + gc * (1.0 - sg)))
            _store_rows(du_ref, (0,), i, lanes, _valid(dgc * w2 + _next(dgc, 1) * w1 + _next(dgc, 2) * w0))
            for k, shifted in enumerate((g2, g1, g)):
                acc_ref[k, :, lanes] += _fold8(_valid(dgc * shifted))
            acc_ref[3, :, lanes] += _fold8(_valid(dgc))

        _for_chunks(t, chunk)
        _write_col_sums(acc_ref, [(dw_ref, 0), (dw_ref, 1), (dw_ref, 2), (db_ref, 0)])

    wspec = pl.BlockSpec((3, TC), lambda j: (0, j))
    bspec = pl.BlockSpec((1, TC), lambda j: (0, j))
    return _tc_call(
        body, name=name, grid=(F_FF // TC,), in_specs=[_col(2, t), wspec, bspec, _col(None, t)],
        out_specs=[_col(2, t), wspec, bspec],
        out_shape=[jax.ShapeDtypeStruct((2, t, F_FF), BF16), jax.ShapeDtypeStruct((3, F_FF), F32),
                   jax.ShapeDtypeStruct((1, F_FF), F32)],
        scratch_shapes=[_staging(3, t), pltpu.VMEM((4, 8, TC), F32)], compiler_params=_cp("parallel"),
    )(up, w, bias, da)


ATT_TQ = 256
ATT_SCALE = (QK_NOPE + QK_ROPE) ** -0.5


def _key_ranges(lvl):
    lo = lvl * ATT_TQ
    return ([(0, lo, False)] if lvl else []) + [(lo, lo + ATT_TQ, True)]


def _fill_keys(k_ref, kn_ref, kr_ref):
    @pl.when(pl.program_id(1) == 0)
    def _():
        k_ref[:, :QK_NOPE] = kn_ref[...]
        k_ref[:, QK_NOPE:] = kr_ref[...]


def _attn_probs(q, k_ref, lvl):
    scores = []
    for lo, hi, diagonal in _key_ranges(lvl):
        s = lax.dot_general(q, k_ref[lo:hi, :], NT_DIMS, preferred_element_type=F32) * ATT_SCALE
        if diagonal:
            row = lax.broadcasted_iota(jnp.int32, s.shape, 0)
            col = lax.broadcasted_iota(jnp.int32, s.shape, 1)
            seen = lax.shift_right_logical(col, CHUNK_SHIFT) <= lax.shift_right_logical(row, CHUNK_SHIFT)
            s = jnp.where(seen, s, NEG_INF)
        scores.append(s)
    m = jnp.max(scores[0], axis=1, keepdims=True)
    for s in scores[1:]:
        m = jnp.maximum(m, jnp.max(s, axis=1, keepdims=True))
    ps = [jnp.exp(s - m) for s in scores]
    total = jnp.sum(ps[0], axis=1, keepdims=True)
    for p in ps[1:]:
        total = total + jnp.sum(p, axis=1, keepdims=True)
    inv = 1.0 / total
    return [p * inv for p in ps]


def _per_query_block(qi, n_blocks, branch):
    for lvl in range(n_blocks):
        pl.when(qi == lvl)(lambda lvl=lvl: branch(lvl))


def _attn_specs(t):
    q = pl.BlockSpec((ATT_TQ, HEAD_PAD), lambda h, i: (i, h))
    kn = pl.BlockSpec((None, t, QK_NOPE), lambda h, i: (0, 0, h))
    kr = pl.BlockSpec((t, LANES), lambda h, i: (0, 0))
    v = pl.BlockSpec((None, t, V_HEAD), lambda h, i: (1, 0, h))
    o = pl.BlockSpec((ATT_TQ, V_HEAD), lambda h, i: (i, h))
    return q, kn, kr, v, o


def _attn_fwd(q, knv, kr):
    t = q.shape[0]

    def body(q_ref, kn_ref, kr_ref, v_ref, o_ref, k_ref):
        _fill_keys(k_ref, kn_ref, kr_ref)

        def branch(lvl):
            ps = _attn_probs(q_ref[...], k_ref, lvl)
            o = None
            for p, (lo, hi, _) in zip(ps, _key_ranges(lvl)):
                part = jnp.dot(p.astype(BF16), v_ref[lo:hi, :], preferred_element_type=F32)
                o = part if o is None else o + part
            o_ref[...] = o.astype(BF16)

        _per_query_block(pl.program_id(1), t // ATT_TQ, branch)

    qs, kns, krs, vs, os_ = _attn_specs(t)
    return _tc_call(
        body, name="attn_fwd", grid=(N_HEADS, t // ATT_TQ), in_specs=[qs, kns, krs, vs], out_specs=os_,
        out_shape=jax.ShapeDtypeStruct((t, N_HEADS * V_HEAD), BF16), scratch_shapes=[pltpu.VMEM((t, HEAD_PAD), BF16)],
        compiler_params=_cp("parallel", "arbitrary"),
    )(q, knv, kr, knv)


def _attn_bwd(q, knv, kr, do, cos, sin):
    t = q.shape[0]

    def body(q_ref, kn_ref, kr_ref, v_ref, do_ref, c_ref, s_ref, dq_ref, dknv_ref, dkr_ref, k_ref, dk_ref):
        h, qi = pl.program_id(0), pl.program_id(1)
        _fill_keys(k_ref, kn_ref, kr_ref)

        @pl.when(qi == 0)
        def _():
            dknv_ref[1] = jnp.zeros((t, V_HEAD), F32)
            dk_ref[...] = jnp.zeros_like(dk_ref)

        @pl.when((qi == 0) & (h == 0))
        def _():
            dkr_ref[...] = jnp.zeros_like(dkr_ref)

        def branch(lvl):
            qv, dov = q_ref[...], do_ref[...]
            ranges = _key_ranges(lvl)
            ps = _attn_probs(qv, k_ref, lvl)
            dps = [lax.dot_general(dov, v_ref[lo:hi, :], NT_DIMS, preferred_element_type=F32) for lo, hi, _ in ranges]
            di = None
            for p, dp in zip(ps, dps):
                part = jnp.sum(p * dp, axis=1, keepdims=True)
                di = part if di is None else di + part
            dq = None
            for p, dp, (lo, hi, _) in zip(ps, dps, ranges):
                ds = (p * (dp - di) * ATT_SCALE).astype(BF16)
                part = jnp.dot(ds, k_ref[lo:hi, :], preferred_element_type=F32)
                dq = part if dq is None else dq + part
                dk_ref[lo:hi, :] += lax.dot_general(ds, qv, TN_DIMS, preferred_element_type=F32)
                dknv_ref[1, lo:hi, :] += lax.dot_general(p.astype(BF16), dov, TN_DIMS, preferred_element_type=F32)
            dq_ref[:, :QK_NOPE] = dq[:, :QK_NOPE].astype(BF16)
            dq_ref[:, QK_NOPE:] = _rope_bwd_math(dq[:, QK_NOPE:], c_ref[...], s_ref[...]).astype(BF16)

        _per_query_block(qi, t // ATT_TQ, branch)

        @pl.when(qi == t // ATT_TQ - 1)
        def _():
            dknv_ref[0] = dk_ref[:, :QK_NOPE]
            dkr_ref[...] += dk_ref[:, QK_NOPE:]

    qs, kns, krs, vs, os_ = _attn_specs(t)
    tab = pl.BlockSpec((ATT_TQ, LANES), lambda h, i: (i, 0))
    return _tc_call(
        body, name="attn_bwd", grid=(N_HEADS, t // ATT_TQ), in_specs=[qs, kns, krs, vs, os_, tab, tab],
        out_specs=[qs, pl.BlockSpec((2, t, QK_NOPE), lambda h, i: (0, 0, h)), krs],
        out_shape=[jax.ShapeDtypeStruct((t, N_HEADS * HEAD_PAD), BF16),
                   jax.ShapeDtypeStruct((2, t, N_HEADS * QK_NOPE), F32), jax.ShapeDtypeStruct((t, LANES), F32)],
        scratch_shapes=[pltpu.VMEM((t, HEAD_PAD), BF16), pltpu.VMEM((t, HEAD_PAD), F32)],
        compiler_params=_cp("arbitrary", "arbitrary"),
    )(q, knv, kr, knv, do, cos, sin)


def _adam_math(w, g, m, v):
    nm = ADAM_B1 * m + (1.0 - ADAM_B1) * g
    nv = ADAM_B2 * v + (1.0 - ADAM_B2) * (g * g)
    m_hat = nm / (1.0 - ADAM_B1 ** ADAM_STEP)
    v_hat = nv / (1.0 - ADAM_B2 ** ADAM_STEP)
    return -ADAM_LR * (m_hat / (jnp.sqrt(v_hat) + ADAM_EPS) + ADAM_WD * w), nm, nv


def _adamw_small(w, g, m, v):
    def body(w_ref, g_ref, m_ref, v_ref, d_ref, nm_ref, nv_ref):
        d_ref[...], nm_ref[...], nv_ref[...] = _adam_math(w_ref[...], g_ref[...], m_ref[...], v_ref[...])

    shp = jax.ShapeDtypeStruct(w.shape, F32)
    return _tc_call(body, name="adamw_small", out_shape=[shp] * 3)(w, g, m, v)


ADAM_SPLIT = 4


def _adamw_shards(ids, items, name):
    n = len(items)

    def body(ids_ref, *refs):
        outs = refs[len(refs) - 4 * n:]
        mine = pl.program_id(0) == ids_ref[0]
        for i in range(n):
            w_ref, m_ref, v_ref, gm_ref, gs_ref = refs[5 * i:5 * i + 5]
            g_ref, d_ref, nm_ref, nv_ref = outs[4 * i:4 * i + 4]

            @pl.when(mine)
            def _(g_ref=g_ref, gm_ref=gm_ref):
                g_ref[...] = gm_ref[...]

            @pl.when(jnp.logical_not(mine))
            def _(g_ref=g_ref, gs_ref=gs_ref):
                g_ref[...] = gs_ref[...]

            d_ref[...], nm_ref[...], nv_ref[...] = _adam_math(w_ref[...], g_ref[...], m_ref[...], v_ref[...])

    in_specs, out_specs, out_shape, args, carried, aliases = [], [], [], [ids], [], {}
    for i, it in enumerate(items):
        w = it["w"]
        r, c = w.shape[-2:]
        tr = r // 2 // ADAM_SPLIT
        assert tr % 8 == 0, (name, w.shape)
        layer = it.get("layer")
        if layer is None:
            wspec = pl.BlockSpec((tr, c), lambda h, k, ids: (h * ADAM_SPLIT + k, 0))
        else:
            wspec = pl.BlockSpec((None, tr, c), lambda h, k, ids, layer=layer: (layer, h * ADAM_SPLIT + k, 0))
        gspec = pl.BlockSpec((tr, c), lambda h, k, ids: (k, 0))
        in_specs += [wspec] * 3 + [gspec] * 2
        args += [w, it["m"], it["v"], it["g_mine"], it["g_sib"]]
        out_specs += [wspec] * 4
        out_shape += [jax.ShapeDtypeStruct(w.shape, F32)] * 4
        if it.get("prev") is not None:
            for k, p in enumerate(it["prev"]):
                aliases[1 + 5 * n + len(carried)] = 4 * i + k
                carried.append(p)
    res = _tc_call(
        body, name=name, prefetch=1, grid=(2, ADAM_SPLIT), in_specs=in_specs + [ANY] * len(carried),
        out_specs=out_specs, out_shape=out_shape, input_output_aliases=aliases,
        compiler_params=_cp("parallel", "parallel"),
    )(*args, *carried)
    return [res[4 * i:4 * i + 4] for i in range(n)]


def _peer_chip(k_me, j):
    return k_me ^ jnp.where(j == 0, 2, jnp.where(j == 1, 1, 3))


def _pair_sums(ids, gs, ras, name):
    n = len(gs)

    def body(ids_ref, *refs):
        for i in range(n):
            g_ref, ra_ref, o_ref = refs[2 * i], refs[2 * i + 1], refs[2 * n + i]
            o_ref[...] = (g_ref[...].astype(F32) + ra_ref[...].astype(F32)).astype(BF16)

    in_specs, out_specs, out_shape = [], [], []
    for g in gs:
        half, c = g.shape[1] // 2, g.shape[2]
        in_specs += [pl.BlockSpec((None, half, c), lambda j, ids: (_peer_chip(ids[1], j), ids[0], 0)),
                     pl.BlockSpec((None, half, c), lambda j, ids: (_peer_chip(ids[1], j), 0, 0))]
        out_specs.append(pl.BlockSpec((None, half, c), lambda j, ids: (j, 0, 0)))
        out_shape.append(jax.ShapeDtypeStruct((3, half, c), BF16))
    return _tc_call(
        body, name=name, prefetch=1, grid=(3,), in_specs=in_specs, out_specs=out_specs, out_shape=out_shape,
        compiler_params=_cp("parallel"),
    )(ids, *[a for pair in zip(gs, ras) for a in pair])


def _chip_sums(ids, gs, ras, rbs, name):
    n = len(gs)

    def body(ids_ref, *refs):
        for i in range(n):
            g_ref, ra_ref, rb_ref, o_ref = refs[3 * i], refs[3 * i + 1], refs[3 * i + 2], refs[3 * n + i]
            acc = g_ref[...].astype(F32) + ra_ref[...].astype(F32)
            for j in range(3):
                acc = acc + rb_ref[j].astype(F32)
            o_ref[...] = acc

    in_specs, out_specs, out_shape = [], [], []
    for g in gs:
        half, c = g.shape[1] // 2, g.shape[2]
        in_specs += [pl.BlockSpec((None, half, c), lambda i, ids: (ids[1], ids[0], 0)),
                     pl.BlockSpec((None, half, c), lambda i, ids: (ids[1], 0, 0)),
                     pl.BlockSpec((3, half, c), lambda i, ids: (0, 0, 0))]
        out_specs.append(pl.BlockSpec((half, c), lambda i, ids: (0, 0)))
        out_shape.append(jax.ShapeDtypeStruct((half, c), F32))
    return _tc_call(
        body, name=name, prefetch=1, grid=(1,), in_specs=in_specs, out_specs=out_specs, out_shape=out_shape,
        compiler_params=_cp("arbitrary"),
    )(ids, *[a for trio in zip(gs, ras, rbs) for a in trio])


def _position():
    x, y, c = lax.axis_index("x"), lax.axis_index("y"), lax.axis_index("c")
    chips = [(1 - x, y), (x, 1 - y), (1 - x, 1 - y)]
    return x, y, c, chips


def _shard_half(ref, wm, h):
    if wm.kind == "tiny":
        return ref
    if wm.nl == 2:
        return ref.at[h]
    return ref.at[pl.ds(pl.multiple_of(h * (wm.k // 2), 16), wm.k // 2), :]


def _region(full, wm, s, h):
    if wm.kind == "tiny":
        return full.at[s]
    cols = pl.ds(pl.multiple_of(s * wm.n, LANES), wm.n) if wm.kind == "col" else slice(None)
    if wm.nl == 2:
        rows = pl.ds(pl.multiple_of(s * wm.k, 16), wm.k) if wm.kind == "row" else slice(None)
        return full.at[slice(None) if h is None else h, rows, cols]
    if wm.kind == "col":
        rows = slice(None) if h is None else pl.ds(pl.multiple_of(h * (wm.k // 2), 16), wm.k // 2)
    elif h is None:
        rows = pl.ds(pl.multiple_of(s * wm.k, 16), wm.k)
    else:
        rows = pl.ds(pl.multiple_of(s * wm.k + h * (wm.k // 2), 16), wm.k // 2)
    return full.at[rows, cols]


def _full_shape(wm):
    if wm.kind == "tiny":
        return (N_CHIPS, wm.k, wm.n)
    shape = (wm.k, N_CHIPS * wm.n) if wm.kind == "col" else (N_CHIPS * wm.k, wm.n)
    return shape if wm.nl == 1 else (wm.nl,) + shape


def _handshake(peers):
    barrier = pltpu.get_barrier_semaphore()
    for peer in peers:
        pl.semaphore_signal(barrier, inc=1, device_id=peer, device_id_type=MESH)
    pl.semaphore_wait(barrier, len(peers))


def _all_gather_group(gi, shards):
    wms = AG_GROUPS[gi]
    nw = len(wms)

    def body(*refs):
        sh, full = refs[:nw], refs[nw:2 * nw]
        ici_s, ici_r, pass_s, pass_r, own_s, own_r = refs[2 * nw:]
        x, y, c, chips = _position()
        me, sibling = 2 * x + y, (x, y, 1 - c)
        _handshake([(*chip, c) for chip in chips] + [sibling])

        def rcopy(src, dst, s_sem, r_sem, to):
            return pltpu.make_async_remote_copy(src_ref=src, dst_ref=dst, send_sem=s_sem, recv_sem=r_sem,
                                                device_id=to, device_id_type=MESH)

        started = []
        for i, wm in enumerate(wms):
            for j, chip in enumerate(chips):
                started.append(rcopy(_shard_half(sh[i], wm, c), _region(full[i], wm, me, c),
                                     ici_s.at[i, j], ici_r.at[i, j], (*chip, c)))
                started[-1].start()
            started.append(rcopy(sh[i], _region(full[i], wm, me, None), own_s.at[i], own_r.at[i], sibling))
            started[-1].start()
        for i, wm in enumerate(wms):
            for j, chip in enumerate(chips):
                got = _region(full[i], wm, 2 * chip[0] + chip[1], c)
                rcopy(got, got, ici_s.at[i, j], ici_r.at[i, j], sibling).wait_recv()
                if wm.kind != "tiny":
                    started.append(rcopy(got, got, pass_s.at[i, j], pass_r.at[i, j], sibling))
                    started[-1].start()
        for i, wm in enumerate(wms):
            mine = _region(full[i], wm, me, None)
            rcopy(mine, mine, own_s.at[i], own_r.at[i], sibling).wait_recv()
            for j, chip in enumerate(chips):
                if wm.kind != "tiny":
                    got = _region(full[i], wm, 2 * chip[0] + chip[1], 1 - c)
                    rcopy(got, got, pass_s.at[i, j], pass_r.at[i, j], sibling).wait_recv()
        for cp in started:
            cp.wait_send()

    return pl.kernel(
        body, out_type=[jax.ShapeDtypeStruct(_full_shape(wm), s.dtype) for wm, s in zip(wms, shards)],
        mesh=plsc.ScalarSubcoreMesh(axis_name="sequencer", num_cores=1), name=f"ag_group{gi}",
        scratch_types=[pltpu.SemaphoreType.DMA((nw, 3))] * 4 + [pltpu.SemaphoreType.DMA((nw,))] * 2,
        compiler_params=pltpu.CompilerParams(collective_id=gi),
    )(*shards)


def _sequencer_call(body, name, cid, out_types, scratch, args):
    return pl.kernel(
        body, out_type=out_types, mesh=plsc.ScalarSubcoreMesh(axis_name="sequencer", num_cores=1), name=name,
        scratch_types=scratch, compiler_params=pltpu.CompilerParams(collective_id=cid),
    )(*args)


def _pair_exchange(gs, tag, cid):
    n = len(gs)

    def body(*refs):
        g, out, send_sems, recv_sems = refs[:n], refs[n:2 * n], refs[2 * n], refs[2 * n + 1]
        x, y, c, _ = _position()
        _handshake([(x, y, 1 - c)])
        cps = []
        for i in range(n):
            half = g[i].shape[1] // 2
            cps.append(pltpu.make_async_remote_copy(
                src_ref=g[i].at[:, pl.ds(pl.multiple_of((1 - c) * half, 16), half), :], dst_ref=out[i],
                send_sem=send_sems.at[i], recv_sem=recv_sems.at[i], device_id=(x, y, 1 - c), device_id_type=MESH))
            cps[-1].start()
        for cp in cps:
            cp.wait()

    return _sequencer_call(
        body, f"rs_pair_exchange{tag}", cid,
        [jax.ShapeDtypeStruct((a.shape[0], a.shape[1] // 2, a.shape[2]), a.dtype) for a in gs],
        [pltpu.SemaphoreType.DMA((n,)), pltpu.SemaphoreType.DMA((n,))], gs)


def _chip_exchange(ss, tag, cid):
    n = len(ss)

    def body(*refs):
        s, out, send_sems, recv_sems = refs[:n], refs[n:2 * n], refs[2 * n], refs[2 * n + 1]
        x, y, c, chips = _position()
        _handshake([(*chip, c) for chip in chips])
        cps = []
        for i in range(n):
            for j, chip in enumerate(chips):
                cps.append(pltpu.make_async_remote_copy(
                    src_ref=s[i].at[j], dst_ref=out[i].at[j], send_sem=send_sems.at[i, j], recv_sem=recv_sems.at[i, j],
                    device_id=(*chip, c), device_id_type=MESH))
                cps[-1].start()
        for cp in cps:
            cp.wait()

    return _sequencer_call(
        body, f"rs_chip_exchange{tag}", cid, [jax.ShapeDtypeStruct(a.shape, a.dtype) for a in ss],
        [pltpu.SemaphoreType.DMA((n, 3)), pltpu.SemaphoreType.DMA((n, 3))], ss)


def _pair_swap(g8s, tag, cid):
    n = len(g8s)

    def body(*refs):
        g, out, send_sems, recv_sems = refs[:n], refs[n:2 * n], refs[2 * n], refs[2 * n + 1]
        x, y, c, _ = _position()
        _handshake([(x, y, 1 - c)])
        cps = []
        for i in range(n):
            cps.append(pltpu.make_async_remote_copy(
                src_ref=g[i], dst_ref=out[i], send_sem=send_sems.at[i], recv_sem=recv_sems.at[i],
                device_id=(x, y, 1 - c), device_id_type=MESH))
            cps[-1].start()
        for cp in cps:
            cp.wait()

    return _sequencer_call(
        body, f"rs_pair_swap{tag}", cid, [jax.ShapeDtypeStruct(a.shape, a.dtype) for a in g8s],
        [pltpu.SemaphoreType.DMA((n,)), pltpu.SemaphoreType.DMA((n,))], g8s)


def _all_reduce_small(vec, name):
    r, cols = vec.shape

    def body(v_ref, o_ref, gath, send_sems, recv_sems):
        x, y, c, _ = _position()
        me = 4 * x + 2 * y + c
        gath[me] = v_ref[...]
        cps = []
        for rel in range(1, N_DEV):
            peer = (x ^ (rel >> 2), y ^ ((rel >> 1) & 1), c ^ (rel & 1))
            cps.append(pltpu.make_async_remote_copy(
                src_ref=v_ref, dst_ref=gath.at[me], send_sem=send_sems.at[rel - 1], recv_sem=recv_sems.at[rel - 1],
                device_id=peer, device_id_type=MESH))
        for cp in cps:
            cp.start()
        for rel in range(1, N_DEV):
            pltpu.make_async_remote_copy(
                src_ref=v_ref, dst_ref=gath.at[me ^ rel], send_sem=send_sems.at[rel - 1],
                recv_sem=recv_sems.at[rel - 1], device_id=(x, y, c), device_id_type=MESH).wait_recv()
        for cp in cps:
            cp.wait_send()
        acc = gath[0]
        for d in range(1, N_DEV):
            acc = acc + gath[d]
        o_ref[...] = acc

    vm = pl.BlockSpec(memory_space=pltpu.VMEM)
    return _tc_call(
        body, name=name, in_specs=[vm], out_specs=vm, out_shape=jax.ShapeDtypeStruct((r, cols), F32),
        scratch_shapes=[pltpu.VMEM((N_DEV, r, cols), F32), pltpu.SemaphoreType.DMA((N_DEV - 1,)),
                        pltpu.SemaphoreType.DMA((N_DEV - 1,))],
    )(vec)


def _rope_tables(positions):
    half = QK_ROPE // 2
    inv_freq = 1.0 / (ROPE_THETA ** (jnp.arange(half, dtype=F32) / half))
    ang = positions.astype(F32)[:, None] * inv_freq
    zeros = jnp.zeros((positions.shape[0], LANES - QK_ROPE), F32)
    cos, sin = jnp.cos(ang), jnp.sin(ang)
    return jnp.concatenate([cos, cos, zeros], axis=1), jnp.concatenate([sin, sin, zeros], axis=1)


def _local_step(x, positions, tgt, wf, small, rs):
    cos, sin = _rope_tables(positions)
    w_in, w_out = wf["sc_w_in"], wf["sc_w_out"]
    w_ups, w_downs = (wf["ffn_w_up0"], wf["ffn_w_up1"]), (wf["ffn_w_down0"], wf["ffn_w_down1"])
    w_kv, w_ukv, w_dq, w_uq, w_o = wf["w_kv"], wf["w_ukv"], wf["w_dq"], wf["w_uq"], wf["w_o"]
    attn_norm, ffn_norm = small["attn_norm"], small["ffn_norm"]
    conv_b = small["ffn_conv_b"]

    def ffn_fwd(h, hf, l, then):
        up, a = _ffn_up_gate(hf, w_ups[l], small["ffn_conv_w"][l], conv_b[l:l + 1], f"ffn{l}_up_gate")
        return then(a, w_downs[l], h), (hf, up, a)

    def ffn_bwd(h, dh_out, dh_out_b, l, saved, gi, hooks):
        run = lambda stage: hooks.get(stage, lambda: None)()
        hf, up, a = saved
        da = _nt(f"ffn{l}_down_dx", dh_out_b, w_downs[l], BF16)
        run("down_dx")
        d_down = _tn(f"ffn{l}_down_dw", a, dh_out_b, BF16)
        dup, d_cw, d_cb = _gate_bwd(up, small["ffn_conv_w"][l], conv_b[l:l + 1], da, f"ffn{l}_gate_bwd")
        run("gate_bwd")
        d_up = _dw_ffn_up(f"ffn{l}_up_dw", hf, dup)
        rs.start(gi, {f"ffn_w_down{l}": d_down.reshape(N_CHIPS, F_FF // N_CHIPS, D), f"ffn_w_up{l}": d_up})
        dh, dh_b, d_norm = _dx_norm_bwd(f"ffn{l}_up_dx", dup, w_ups[l], h, ffn_norm[l:l + 1], dh_out)
        run("up_dx")
        return dh, dh_b, d_cw, d_cb, d_norm

    hn0 = _rms_fwd(x, attn_norm[0:1], "attn0_norm")
    z = _nn_parts("sc_in", hn0, w_in, 3, BF16)
    mix = _scmix_fwd(z, small["sc_conv_w"])
    h1, hf0 = _nn_add_norm("sc_out", mix, w_out, x, ffn_norm[0:1])
    h2, ffn0_saved = ffn_fwd(h1, hf0, 0, lambda a, w, h: _nn("ffn0_down", a, w, F32, add=h))

    hn1, hk, cq_pre, cq, q, kvpre, ckv, kr, knv = _attn_prep(
        h2, attn_norm[1:2], small["kv_in_norm"], w_dq, small["q_latent_norm"], w_uq, w_kv, small["kv_latent_norm"],
        w_ukv, cos, sin)
    o = _attn_fwd(q, knv, kr)
    h3, hf1 = _nn_add_norm("attn_out", o, w_o, h2, ffn_norm[1:2])
    (loss, dh4, dh4_b, d_final), ffn1_saved = ffn_fwd(
        h3, hf1, 1, lambda a, w, h: _nn_add_loss("ffn1_down_loss", a, w, h, small["final_norm"], tgt))

    rows = D // N_CHIPS
    dh3, dh3_b, d_cw1, d_cb1, d_fn1 = ffn_bwd(h3, dh4, dh4_b, 1, ffn1_saved, 0, {})

    do = _nt("attn_out_dx", dh3_b, w_o, BF16)
    d_wo = _tn("attn_out_dw", o, dh3_b, BF16)
    rs.pair_sums(0)
    dq, dknv, dkr = _attn_bwd(q, knv, kr, do, cos, sin)
    rs.chip_sums(0)
    dh2, dh2_b, d_wuq, d_wdq, d_wukv, d_wkv, d_an1, d_kvin, d_qln, d_kvln = _attn_prep_bwd(
        dq, dknv, dkr, dh3, h2, hn1, hk, cq_pre, cq, kvpre, ckv, attn_norm[1:2], small["kv_in_norm"], w_dq,
        small["q_latent_norm"], w_uq, w_kv, small["kv_latent_norm"], w_ukv, cos, sin)
    rs.finish(0)
    by_owner = lambda dw: dw.reshape(dw.shape[0], N_CHIPS, -1).transpose(1, 0, 2)
    rs.start(1, {
        "w_o": d_wo.reshape(N_CHIPS, rows, D), "w_uq": by_owner(d_wuq), "w_dq": d_wdq.reshape(N_CHIPS, rows, Q_LORA),
        "w_ukv": by_owner(d_wukv.reshape(2 * KV_LORA, -1)).reshape(N_CHIPS, 2 * KV_LORA, -1),
        "w_kv": d_wkv.reshape(N_CHIPS, rows, KVP),
    })

    dh1, dh1_b, d_cw0, d_cb0, d_fn0 = ffn_bwd(h1, dh2, dh2_b, 0, ffn0_saved, 2, {
        "down_dx": lambda: rs.pair_sums(1), "gate_bwd": lambda: rs.chip_sums(1), "up_dx": lambda: rs.finish(1)})
    rs.pair_sums(2)

    d_wout = _tn("sc_out_dw", mix, dh1_b, BF16)
    dmix = _nt("sc_out_dx", dh1_b, w_out, BF16)
    dz, d_scw = _scmix_bwd(z, small["sc_conv_w"], dmix)
    d_win = _dw_sc_in(hn0, dz)
    rs.start(3, {"sc_w_out": d_wout.reshape(N_CHIPS, rows, D), "sc_w_in": d_win})
    dx, _, d_an0 = _dx_norm_bwd("sc_in_dx", dz, w_in, x, attn_norm[0:1], dh1)

    small_g = {
        "attn_norm": jnp.concatenate([d_an0, d_an1]), "ffn_norm": jnp.concatenate([d_fn0, d_fn1]),
        "final_norm": d_final, "kv_in_norm": d_kvin, "kv_latent_norm": d_kvln, "q_latent_norm": d_qln,
        "ffn_conv_b": jnp.concatenate([d_cb0, d_cb1]), "sc_conv_w": d_scw, "ffn_conv_w": jnp.stack([d_cw0, d_cw1]),
    }
    return loss, dx, small_g


RS_GROUPS = (("ffn_w_down1", "ffn_w_up1"), ("w_o", "w_uq", "w_dq", "w_ukv", "w_kv"),
             ("ffn_w_down0", "ffn_w_up0"), ("sc_w_out", "sc_w_in"))


class _ReduceScatter:
    def __init__(self, ids, finish):
        self.ids, self.grads, self.step, self.mine, self.sib, self.finish = ids, {}, {}, {}, {}, finish

    def _cid(self, gi):
        return len(AG_GROUPS) + 3 * gi

    def start(self, gi, grads):
        self.grads.update(grads)
        own = [grads[n] for n in RS_GROUPS[gi]]
        self.step[gi] = (own, _pair_exchange(own, gi, self._cid(gi)))

    def pair_sums(self, gi):
        own, ra = self.step[gi]
        sums = _pair_sums(self.ids, own, ra, f"rs_pair_sums{gi}")
        self.step[gi] = (own, ra, _chip_exchange(sums, gi, self._cid(gi) + 1))

    def chip_sums(self, gi):
        own, ra, rb = self.step[gi]
        mine = _chip_sums(self.ids, own, ra, rb, f"rs_chip_sums{gi}")
        self.mine.update(zip(RS_GROUPS[gi], mine))
        self.sib.update(zip(RS_GROUPS[gi], _pair_swap(mine, gi, self._cid(gi) + 2)))

SMALL_REPL = ("attn_norm", "ffn_norm", "final_norm", "kv_in_norm", "kv_latent_norm", "q_latent_norm", "ffn_conv_b")
SMALL_SHARDED = ("sc_conv_w", "ffn_conv_w")
SMALL_ROWS = 256


def _pad_heads(w_uq):
    per_head = w_uq.reshape(Q_LORA, -1, QK_NOPE + QK_ROPE)
    return jnp.pad(per_head, ((0, 0), (0, 0), (0, HEAD_PAD - QK_NOPE - QK_ROPE))).reshape(Q_LORA, -1)


def _pack_kv(w_dkv, w_kr):
    return jnp.concatenate([w_dkv, w_kr, jnp.zeros((w_kr.shape[0], LANES - QK_ROPE), w_kr.dtype)], axis=1)


def kernel(x, positions, attn_norm, ffn_norm, final_norm, sc_w_in, sc_conv_w, sc_w_out, kv_in_norm, w_dkv, kv_latent_norm, w_kr, w_uk, w_uv, w_dq, q_latent_norm, w_uq, w_o, ffn_w_up, ffn_conv_w, ffn_conv_b, ffn_w_down, loss_target, m_attn_norm, m_ffn_norm, m_final_norm, m_sc_w_in, m_sc_conv_w, m_sc_w_out, m_kv_in_norm, m_w_dkv, m_kv_latent_norm, m_w_kr, m_w_uk, m_w_uv, m_w_dq, m_q_latent_norm, m_w_uq, m_w_o, m_ffn_w_up, m_ffn_conv_w, m_ffn_conv_b, m_ffn_w_down, v_attn_norm, v_ffn_norm, v_final_norm, v_sc_w_in, v_sc_conv_w, v_sc_w_out, v_kv_in_norm, v_w_dkv, v_kv_latent_norm, v_w_kr, v_w_uk, v_w_uv, v_w_dq, v_q_latent_norm, v_w_uq, v_w_o, v_ffn_w_up, v_ffn_conv_w, v_ffn_conv_b, v_ffn_w_down):
    names = ("attn_norm", "ffn_norm", "final_norm", "sc_w_in", "sc_conv_w", "sc_w_out", "kv_in_norm", "w_dkv",
             "kv_latent_norm", "w_kr", "w_uk", "w_uv", "w_dq", "q_latent_norm", "w_uq", "w_o", "ffn_w_up",
             "ffn_conv_w", "ffn_conv_b", "ffn_w_down")
    w = dict(zip(names, (attn_norm, ffn_norm, final_norm, sc_w_in, sc_conv_w, sc_w_out, kv_in_norm, w_dkv,
                         kv_latent_norm, w_kr, w_uk, w_uv, w_dq, q_latent_norm, w_uq, w_o, ffn_w_up,
                         ffn_conv_w, ffn_conv_b, ffn_w_down)))
    m = dict(zip(names, (m_attn_norm, m_ffn_norm, m_final_norm, m_sc_w_in, m_sc_conv_w, m_sc_w_out, m_kv_in_norm,
                         m_w_dkv, m_kv_latent_norm, m_w_kr, m_w_uk, m_w_uv, m_w_dq, m_q_latent_norm, m_w_uq, m_w_o,
                         m_ffn_w_up, m_ffn_conv_w, m_ffn_conv_b, m_ffn_w_down)))
    v = dict(zip(names, (v_attn_norm, v_ffn_norm, v_final_norm, v_sc_w_in, v_sc_conv_w, v_sc_w_out, v_kv_in_norm,
                         v_w_dkv, v_kv_latent_norm, v_w_kr, v_w_uk, v_w_uv, v_w_dq, v_q_latent_norm, v_w_uq, v_w_o,
                         v_ffn_w_up, v_ffn_conv_w, v_ffn_conv_b, v_ffn_w_down)))

    _ORDER[0] = None
    ix, iy, ic = lax.axis_index("x"), lax.axis_index("y"), lax.axis_index("c")
    chip = 2 * ix + iy
    ids = jnp.stack([ic, chip]).astype(jnp.int32)

    def shards_of(t):
        return {
            "sc_w_in": t["sc_w_in"][0], "sc_w_out": t["sc_w_out"][0], "ffn_w_up": t["ffn_w_up"],
            "ffn_w_down": t["ffn_w_down"], "w_kv": _pack_kv(t["w_dkv"], t["w_kr"]),
            "w_ukv": jnp.stack([t["w_uk"], t["w_uv"]]), "w_dq": t["w_dq"][0], "w_uq": _pad_heads(t["w_uq"][0]),
            "w_o": t["w_o"][0],
        }

    ws, ms, vs = shards_of(w), shards_of(m), shards_of(v)

    def ag_shard(name):
        if name == "sc_conv_w":
            return sc_conv_w[0]
        if name == "ffn_conv_w":
            return ffn_conv_w.reshape(6, -1)
        if name[:-1] in ("ffn_w_up", "ffn_w_down"):
            return ws[name[:-1]][int(name[-1])].astype(BF16)
        return ws[name].astype(BF16)

    wf = {}
    for gi, wms in enumerate(AG_GROUPS):
        fulls = _all_gather_group(gi, [ag_shard(wm.name) for wm in wms])
        wf.update({wm.name: f for wm, f in zip(wms, fulls)})
    small = {
        "attn_norm": attn_norm, "ffn_norm": ffn_norm, "final_norm": final_norm[None], "kv_in_norm": kv_in_norm[None],
        "kv_latent_norm": kv_latent_norm[None], "q_latent_norm": q_latent_norm, "ffn_conv_b": ffn_conv_b,
        "sc_conv_w": wf["sc_conv_w"].transpose(1, 0, 2).reshape(3, D),
        "ffn_conv_w": wf["ffn_conv_w"].reshape(N_CHIPS, 2, 3, -1).transpose(1, 2, 0, 3).reshape(2, 3, F_FF),
    }

    res = {}

    merged = lambda a: a.reshape(2 * KV_LORA, -1)

    def adamw_group(gi):
        items = []
        for key in RS_GROUPS[gi]:
            n, layer = (key[:-1], int(key[-1])) if key[:-1] in ("ffn_w_up", "ffn_w_down") else (key, None)
            w_, m_, v_ = (merged(t[n]) for t in (ws, ms, vs)) if n == "w_ukv" else (ws[n], ms[n], vs[n])
            items.append(dict(name=n, w=w_, m=m_, v=v_, g_mine=rs.mine[key], g_sib=rs.sib[key], layer=layer,
                              prev=res.get(n)))
        for it, out in zip(items, _adamw_shards(ids, items, f"adamw_group{gi}")):
            res[it["name"]] = out

    rs = _ReduceScatter(ids, adamw_group)
    loss, dx, small_g = _local_step(x[0], positions[0], loss_target[0], wf, small, rs)

    s_order = SMALL_REPL + SMALL_SHARDED
    flat = jnp.concatenate([small_g[n].reshape(-1) for n in s_order] + [loss.reshape(-1)])
    flat = jnp.pad(flat, (0, SMALL_ROWS * LANES - flat.shape[0])).reshape(SMALL_ROWS, LANES)
    red = _all_reduce_small(flat, "ar_small").reshape(-1)
    sg, off = {}, 0
    for n in s_order:
        sz = small_g[n].size
        sg[n] = red[off:off + sz].reshape(small_g[n].shape)
        off += sz
    loss_out = red[off]
    grads = {n: sg[n].reshape(w[n].shape) for n in SMALL_REPL}
    grads["sc_conv_w"] = lax.dynamic_slice_in_dim(sg["sc_conv_w"], chip * (D // N_CHIPS), D // N_CHIPS, axis=1)[None]
    grads["ffn_conv_w"] = lax.dynamic_slice_in_dim(sg["ffn_conv_w"], chip * (F_FF // N_CHIPS), F_FF // N_CHIPS, axis=2)

    rs.chip_sums(2)
    rs.pair_sums(3)
    rs.finish(2)
    rs.chip_sums(3)
    rs.finish(3)
    outs = [grads, {}, {}, {}]
    for k, dst in enumerate(outs):
        for n in ("sc_w_in", "sc_w_out", "w_dq", "w_o"):
            dst[n] = res[n][k][None]
        unpadded = res["w_uq"][k].reshape(Q_LORA, -1, HEAD_PAD)[:, :, :QK_NOPE + QK_ROPE]
        dst["w_uq"] = unpadded.reshape(w_uq.shape)
        dst["ffn_w_up"], dst["ffn_w_down"] = res["ffn_w_up"][k], res["ffn_w_down"][k]
        dst["w_dkv"], dst["w_kr"] = res["w_kv"][k][:, :KV_LORA], res["w_kv"][k][:, KV_LORA:KV_LORA + QK_ROPE]
        dst["w_uk"], dst["w_uv"] = res["w_ukv"][k][:KV_LORA], res["w_ukv"][k][KV_LORA:]
    grads, delta, new_m, new_v = outs

    small_names = SMALL_REPL + SMALL_SHARDED

    def pack_small(tree):
        return jnp.concatenate([tree[n].reshape(-1) for n in small_names]).reshape(-1, LANES)

    small_res = _adamw_small(pack_small(w), pack_small(grads), pack_small(m), pack_small(v))
    for slab, dst in zip(small_res, (delta, new_m, new_v)):
        f, off = slab.reshape(-1), 0
        for n in small_names:
            dst[n] = f[off:off + w[n].size].reshape(w[n].shape)
            off += w[n].size

    _ORDER[0] = None
    return (loss_out, dx[None], *[grads[n] for n in names], *[delta[n] for n in names],
            *[new_m[n] for n in names], *[new_v[n] for n in names])
```

```python
from typing import NamedTuple

import jax
import jax.numpy as jnp
from jax import lax
from jax.experimental import pallas as pl
from jax.experimental.pallas import tpu as pltpu
from jax.experimental.pallas import tpu_sc as plsc

F32 = jnp.float32
BF16 = jnp.bfloat16

T = 2048
D = 1024
F_FF = 2816
N_HEADS = 8
QK_NOPE = 128
QK_ROPE = 64
V_HEAD = 128
Q_LORA = 384
KV_LORA = 256
CHUNK_SHIFT = 6
ROPE_THETA = 10000.0
EPS = 1e-6
NEG_INF = -1e30
HEAD_PAD = 256
KVP = KV_LORA + 128

ADAM_LR = 0.001
ADAM_B1 = 0.9
ADAM_B2 = 0.999
ADAM_EPS = 1e-08
ADAM_WD = 0.01
ADAM_STEP = 10

N_CHIPS = 4
N_DEV = 8
LANES = 128
TC = 256
V7X_VMEM_LIMIT = 56 * 1024 * 1024

MESH = pl.DeviceIdType.MESH
ANY = pl.BlockSpec(memory_space=pl.ANY)


class _W(NamedTuple):
    name: str
    kind: str
    nl: int
    k: int
    n: int


AG_GROUPS = (
    (_W("sc_w_in", "col", 1, D, 3 * D // N_CHIPS), _W("sc_conv_w", "tiny", 1, 3, D // N_CHIPS),
     _W("ffn_conv_w", "tiny", 1, 6, F_FF // N_CHIPS)),
    (_W("sc_w_out", "row", 1, D // N_CHIPS, D),),
    (_W("ffn_w_up0", "col", 1, D, 2 * F_FF // N_CHIPS),),
    (_W("ffn_w_down0", "row", 1, F_FF // N_CHIPS, D),),
    (_W("w_kv", "row", 1, D // N_CHIPS, KVP), _W("w_ukv", "col", 2, KV_LORA, N_HEADS * QK_NOPE // N_CHIPS),
     _W("w_dq", "row", 1, D // N_CHIPS, Q_LORA),
     _W("w_uq", "col", 1, Q_LORA, N_HEADS * HEAD_PAD // N_CHIPS),
     _W("w_o", "row", 1, N_HEADS * V_HEAD // N_CHIPS, D)),
    (_W("ffn_w_up1", "col", 1, D, 2 * F_FF // N_CHIPS), _W("ffn_w_down1", "row", 1, F_FF // N_CHIPS, D)),
)


def _cp(*sem):
    return pltpu.CompilerParams(dimension_semantics=sem, vmem_limit_bytes=V7X_VMEM_LIMIT)


_ORDER = [None]


def _tc_call(body, *, name, out_shape, in_specs=None, out_specs=None, grid=(), scratch_shapes=(), prefetch=0,
             input_output_aliases=None, compiler_params=None):
    def run(*args):
        specs = [pl.BlockSpec(memory_space=pltpu.VMEM)] * (len(args) - prefetch) if in_specs is None else list(in_specs)
        inner, dep = body, _ORDER[0]
        if dep is not None:
            unread = prefetch + len(specs)
            specs, args = specs + [ANY], (*args, dep)

            def inner(*refs):
                return body(*refs[:unread], *refs[unread + 1:])

        kwargs = dict(name=name, out_shape=out_shape, input_output_aliases=input_output_aliases or {},
                      compiler_params=compiler_params)
        if prefetch:
            kwargs["grid_spec"] = pltpu.PrefetchScalarGridSpec(
                num_scalar_prefetch=prefetch, grid=grid, in_specs=specs, out_specs=out_specs,
                scratch_shapes=scratch_shapes)
        else:
            kwargs.update(grid=grid, in_specs=specs, scratch_shapes=scratch_shapes)
            if out_specs is not None:
                kwargs["out_specs"] = out_specs
        out = pl.pallas_call(inner, **kwargs)(*args)
        _ORDER[0] = out[0] if isinstance(out, (list, tuple)) else out
        return out

    return run


def _tile(n, cands):
    for c in cands:
        if n % c == 0:
            return c
    raise ValueError(f"no tile for {n}")


NN_DIMS = (((1,), (0,)), ((), ()))
NT_DIMS = (((1,), (1,)), ((), ()))
TN_DIMS = (((0,), (0,)), ((), ()))
M_TILES = (1024, 512, 384, 256, 128)
N_TILES = (1408, 1024, 768, 512, 384, 256, 128)
MM_BLOCK_BYTES = 36 * 1024 * 1024


def _fit(m, n, block_bytes, m_tiles=M_TILES, n_tiles=N_TILES):
    for tm in [c for c in m_tiles if m % c == 0]:
        for tn in [c for c in n_tiles if n % c == 0]:
            if 2 * block_bytes(tm, tn) + 4 * tm * tn <= MM_BLOCK_BYTES:
                return tm, tn
    raise ValueError(f"no tiles for {m} x {n}")


def _size(x):
    return x.dtype.itemsize


def _mm(name, a, b, dims, grid, a_spec, b_spec, o_spec, o_sds, add=None, red=None, acc_shape=None):
    n_red = None if red is None else grid[red]

    def body(*refs):
        a_ref, b_ref = refs[0], refs[1]
        add_ref = refs[2] if add is not None else None
        o_ref = refs[3] if add is not None else refs[2]
        part = lax.dot_general(a_ref[...].astype(BF16), b_ref[...].astype(BF16), dims, preferred_element_type=F32)
        if red is None:
            if add is not None:
                part = part + add_ref[...]
            o_ref[...] = part.astype(o_ref.dtype)
            return
        acc_ref = refs[-1]
        r = pl.program_id(red)

        @pl.when(r == 0)
        def _():
            acc_ref[...] = part

        @pl.when(r > 0)
        def _():
            acc_ref[...] += part

        @pl.when(r == n_red - 1)
        def _():
            o_ref[...] = acc_ref[...].astype(o_ref.dtype)

    sem = tuple("arbitrary" if ax == red else "parallel" for ax in range(len(grid)))
    in_specs = [a_spec, b_spec] + ([o_spec] if add is not None else [])
    args = (a, b) + ((add,) if add is not None else ())
    return _tc_call(
        body, name=name, grid=grid, in_specs=in_specs, out_specs=o_spec, out_shape=o_sds,
        scratch_shapes=[] if red is None else [pltpu.VMEM(acc_shape, F32)], compiler_params=_cp(*sem),
    )(*args)


def _nn(name, a, b, out_dtype, add=None, lead=None):
    (m, k), n = a.shape, b.shape[-1]
    osz = jnp.dtype(out_dtype).itemsize + (4 if add is not None else 0)
    tm, tn = _fit(m, n, lambda tm, tn: tm * k * _size(a) + k * tn * _size(b) + tm * tn * osz)
    if lead is None:
        b_spec = pl.BlockSpec((k, tn), lambda i, j: (0, j))
    else:
        b_spec = pl.BlockSpec((None, k, tn), lambda i, j: (lead, 0, j))
    return _mm(name, a, b, NN_DIMS, (m // tm, n // tn), pl.BlockSpec((tm, k), lambda i, j: (i, 0)), b_spec,
               pl.BlockSpec((tm, tn), lambda i, j: (i, j)), jax.ShapeDtypeStruct((m, n), out_dtype), add=add)


def _nn_parts(name, a, b, parts, out_dtype, lead=None, stacked=False):
    m, k = a.shape
    c = b.shape[-1] if stacked else b.shape[-1] // parts
    osz = jnp.dtype(out_dtype).itemsize
    tm, tn = _fit(m, c, lambda tm, tn: tm * k * _size(a) + k * tn * _size(b) + tm * tn * osz)
    nb = c // tn
    if stacked:
        b_spec = pl.BlockSpec((None, k, tn), lambda i, p, j: (p, 0, j))
    elif lead is None:
        b_spec = pl.BlockSpec((k, tn), lambda i, p, j: (0, p * nb + j))
    else:
        b_spec = pl.BlockSpec((None, k, tn), lambda i, p, j: (lead, 0, p * nb + j))
    return _mm(name, a, b, NN_DIMS, (m // tm, parts, nb), pl.BlockSpec((tm, k), lambda i, p, j: (i, 0)), b_spec,
               pl.BlockSpec((None, tm, tn), lambda i, p, j: (p, i, j)), jax.ShapeDtypeStruct((parts, m, c), out_dtype))


def _nt(name, a, b, out_dtype, lead=None):
    (m, k), n = a.shape, b.shape[-2]
    osz = jnp.dtype(out_dtype).itemsize
    tm, tn = _fit(m, n, lambda tm, tn: tm * k * _size(a) + tn * k * _size(b) + tm * tn * osz)
    if lead is None:
        b_spec = pl.BlockSpec((tn, k), lambda i, j: (j, 0))
    else:
        b_spec = pl.BlockSpec((None, tn, k), lambda i, j: (lead, j, 0))
    return _mm(name, a, b, NT_DIMS, (m // tm, n // tn), pl.BlockSpec((tm, k), lambda i, j: (i, 0)), b_spec,
               pl.BlockSpec((tm, tn), lambda i, j: (i, j)), jax.ShapeDtypeStruct((m, n), out_dtype))


def _tn(name, a, b, out_dtype):
    (k, m), n = a.shape, b.shape[1]
    osz = jnp.dtype(out_dtype).itemsize
    tm, tn = _fit(m, n, lambda tm, tn: k * tm * _size(a) + k * tn * _size(b) + tm * tn * osz,
                  m_tiles=(512, 384, 256, 128), n_tiles=(n,) + N_TILES)
    return _mm(name, a, b, TN_DIMS, (m // tm, n // tn), pl.BlockSpec((k, tm), lambda i, j: (0, i)),
               pl.BlockSpec((k, tn), lambda i, j: (0, j)), pl.BlockSpec((tm, tn), lambda i, j: (i, j)),
               jax.ShapeDtypeStruct((m, n), out_dtype))


def _nn_add_norm(name, a, b, add, g):
    (m, k), n = a.shape, b.shape[1]
    tm = 512

    def body(a_ref, b_ref, add_ref, g_ref, h_ref, hn_ref):
        h = jnp.dot(a_ref[...], b_ref[...], preferred_element_type=F32) + add_ref[...]
        h_ref[...] = h
        hn_ref[...] = _rms_rows(h, g_ref[...]).astype(BF16)

    rows = lambda w: pl.BlockSpec((tm, w), lambda i: (i, 0))
    return _tc_call(
        body, name=name, grid=(m // tm,),
        in_specs=[rows(k), pl.BlockSpec((k, n), lambda i: (0, 0)), rows(n), pl.BlockSpec((1, n), lambda i: (0, 0))],
        out_specs=[rows(n), rows(n)],
        out_shape=[jax.ShapeDtypeStruct((m, n), F32), jax.ShapeDtypeStruct((m, n), BF16)], compiler_params=_cp("parallel"),
    )(a, b, add, g)


def _nn_add_loss(name, a, b, add, g, tgt):
    (m, k), n = a.shape, b.shape[1]
    tm = 512

    def body(a_ref, b_ref, add_ref, g_ref, t_ref, loss_ref, dh_ref, dhb_ref, dg_ref):
        xv = jnp.dot(a_ref[...], b_ref[...], preferred_element_type=F32) + add_ref[...]
        gv = g_ref[...]
        r = lax.rsqrt(jnp.mean(xv * xv, axis=1, keepdims=True) + EPS)
        err = xv * r * gv - t_ref[...]
        part = 0.5 * jnp.sum(jnp.mean(err * err, axis=1, keepdims=True), axis=0, keepdims=True)
        dx, dg = _rms_bwd_math(xv, gv, err * (1.0 / n))
        dh_ref[...] = dx
        dhb_ref[...] = dx.astype(BF16)

        @pl.when(pl.program_id(0) == 0)
        def _():
            dg_ref[...] = jnp.zeros_like(dg_ref)
            loss_ref[...] = jnp.zeros_like(loss_ref)

        dg_ref[...] += dg
        loss_ref[...] += jnp.broadcast_to(part, loss_ref.shape)

    rows = lambda w: pl.BlockSpec((tm, w), lambda i: (i, 0))
    vec = pl.BlockSpec((1, n), lambda i: (0, 0))
    return _tc_call(
        body, name=name, grid=(m // tm,),
        in_specs=[rows(k), pl.BlockSpec((k, n), lambda i: (0, 0)), rows(n), vec, rows(n)],
        out_specs=[pl.BlockSpec((1, LANES), lambda i: (0, 0)), rows(n), rows(n), vec],
        out_shape=[jax.ShapeDtypeStruct((1, LANES), F32), jax.ShapeDtypeStruct((m, n), F32),
                   jax.ShapeDtypeStruct((m, n), BF16), jax.ShapeDtypeStruct((1, n), F32)],
        compiler_params=_cp("arbitrary"),
    )(a, b, add, g, tgt)


def _dx_norm_bwd(name, a, b, x, g, add):
    parts, t, c = a.shape
    d = b.shape[0]
    tm = 256

    def body(a_ref, b_ref, x_ref, g_ref, add_ref, dx_ref, dxb_ref, dg_ref):
        dy = None
        for p in range(parts):
            part = lax.dot_general(a_ref[p], b_ref[:, p * c:(p + 1) * c], NT_DIMS, preferred_element_type=F32)
            dy = part if dy is None else dy + part
        dx, dg = _rms_bwd_math(x_ref[...], g_ref[...], dy)
        dx = dx + add_ref[...]
        dx_ref[...] = dx
        dxb_ref[...] = dx.astype(BF16)

        @pl.when(pl.program_id(0) == 0)
        def _():
            dg_ref[...] = jnp.zeros_like(dg_ref)

        dg_ref[...] += dg

    rows = pl.BlockSpec((tm, d), lambda i: (i, 0))
    vec = pl.BlockSpec((1, d), lambda i: (0, 0))
    return _tc_call(
        body, name=name, grid=(t // tm,),
        in_specs=[pl.BlockSpec((parts, tm, c), lambda i: (0, i, 0)), pl.BlockSpec(b.shape, lambda i: (0, 0)), rows, vec,
                  rows],
        out_specs=[rows, rows, vec],
        out_shape=[jax.ShapeDtypeStruct((t, d), F32), jax.ShapeDtypeStruct((t, d), BF16),
                   jax.ShapeDtypeStruct((1, d), F32)],
        compiler_params=_cp("arbitrary"),
    )(a, b, x, g, add)


def _dw_sc_in(hn, dz):
    t, tn, tm = hn.shape[0], TC, 512
    per_part, per_chip = D // tn, 3 * D // N_CHIPS // tn
    return _mm("sc_in_dw", hn, dz, TN_DIMS, (D // tm, 3 * D // tn), pl.BlockSpec((t, tm), lambda i, j: (0, i)),
               pl.BlockSpec((None, t, tn), lambda i, j: (j // per_part, 0, j % per_part)),
               pl.BlockSpec((None, tm, tn), lambda i, j: (j // per_chip, i, j % per_chip)),
               jax.ShapeDtypeStruct((N_CHIPS, D, 3 * D // N_CHIPS), BF16))


def _dw_ffn_up(name, hf, dup):
    t, tm, ns = hf.shape[0], 512, 2 * F_FF // N_CHIPS
    return _mm(name, hf, dup, TN_DIMS, (N_CHIPS, D // tm), pl.BlockSpec((t, tm), lambda s, i: (0, i)),
               pl.BlockSpec((None, t, ns), lambda s, i: (s // 2, 0, s % 2)),
               pl.BlockSpec((None, tm, ns), lambda s, i: (s, i, 0)), jax.ShapeDtypeStruct((N_CHIPS, D, ns), BF16))


def _rms_fwd(x, g, name):
    t, d = x.shape
    tr = 512

    def body(x_ref, g_ref, o_ref):
        xv = x_ref[...]
        r = lax.rsqrt(jnp.mean(xv * xv, axis=1, keepdims=True) + EPS)
        o_ref[...] = (xv * r * g_ref[...]).astype(o_ref.dtype)

    row = pl.BlockSpec((tr, d), lambda i: (i, 0))
    return _tc_call(
        body, name=name, grid=(t // tr,), in_specs=[row, pl.BlockSpec((1, d), lambda i: (0, 0))],
        out_specs=row, out_shape=jax.ShapeDtypeStruct((t, d), BF16), compiler_params=_cp("parallel"),
    )(x, g)


def _rms_bwd_math(xv, g, dy):
    r = lax.rsqrt(jnp.mean(xv * xv, axis=1, keepdims=True) + EPS)
    xh = xv * r
    gy = dy * g
    dx = r * (gy - xh * jnp.mean(gy * xh, axis=1, keepdims=True))
    dg = jnp.sum(dy * xh, axis=0, keepdims=True)
    return dx, dg


def _rot_half(x):
    lane = lax.broadcasted_iota(jnp.int32, x.shape, 1)
    return jnp.where((lane % QK_ROPE) < QK_ROPE // 2, -pltpu.roll(x, LANES - 32, axis=1),
                     pltpu.roll(x, 32, axis=1))


def _rope_fwd_math(x, cos, sin):
    return x * cos + _rot_half(x) * sin


def _rope_bwd_math(dy, cos, sin):
    return dy * cos - _rot_half(dy * sin)


def _rms_rows(x, g):
    return x * lax.rsqrt(jnp.mean(x * x, axis=1, keepdims=True) + EPS) * g


def _attn_prep(h, g_attn, g_kvin, w_dq, g_ql, w_uq, w_kv, g_kvl, w_ukv, cos, sin):
    t, d = h.shape
    tr = 256
    wq = N_HEADS * HEAD_PAD

    def body(h_ref, ga_ref, gk_ref, wdq_ref, gq_ref, wuq_ref, wkv_ref, gl_ref, wukv_ref, c_ref, s_ref,
             hn_ref, hk_ref, cqp_ref, cq_ref, q_ref, kvp_ref, ckv_ref, kr_ref, knv_ref):
        xv, cv, sv = h_ref[...], c_ref[...], s_ref[...]
        xh = xv * lax.rsqrt(jnp.mean(xv * xv, axis=1, keepdims=True) + EPS)
        hn = (xh * ga_ref[...]).astype(BF16)
        hk = (xh * gk_ref[...]).astype(BF16)
        hn_ref[...], hk_ref[...] = hn, hk
        cq_pre = jnp.dot(hn, wdq_ref[...], preferred_element_type=F32)
        cqp_ref[...] = cq_pre
        cq = _rms_rows(cq_pre, gq_ref[...]).astype(BF16)
        cq_ref[...] = cq
        for hd in range(N_HEADS):
            lo = hd * HEAD_PAD
            qh = jnp.dot(cq, wuq_ref[:, lo:lo + HEAD_PAD], preferred_element_type=F32)
            q_ref[:, lo:lo + QK_NOPE] = qh[:, :QK_NOPE].astype(BF16)
            q_ref[:, lo + QK_NOPE:lo + HEAD_PAD] = _rope_fwd_math(qh[:, QK_NOPE:], cv, sv).astype(BF16)
        kvpre = jnp.dot(hk, wkv_ref[...], preferred_element_type=F32)
        kvp_ref[...] = kvpre
        ckv = _rms_rows(kvpre[:, :KV_LORA], gl_ref[...]).astype(BF16)
        ckv_ref[...] = ckv
        kr_ref[...] = _rope_fwd_math(kvpre[:, KV_LORA:], cv, sv).astype(BF16)
        for p in range(2):
            knv_ref[p] = jnp.dot(ckv, wukv_ref[p], preferred_element_type=F32).astype(BF16)

    rows = lambda w: pl.BlockSpec((tr, w), lambda i: (i, 0))
    whole = lambda a: pl.BlockSpec(a.shape, lambda i: (0,) * a.ndim)
    sds = lambda w, dt: jax.ShapeDtypeStruct((t, w), dt)
    args = (h, g_attn, g_kvin, w_dq, g_ql, w_uq, w_kv, g_kvl, w_ukv, cos, sin)
    return _tc_call(
        body, name="attn_prep", grid=(t // tr,),
        in_specs=[rows(d)] + [whole(a) for a in args[1:9]] + [rows(LANES), rows(LANES)],
        out_specs=[rows(d), rows(d), rows(Q_LORA), rows(Q_LORA), rows(wq), rows(KVP), rows(KV_LORA), rows(LANES),
                   pl.BlockSpec((2, tr, N_HEADS * QK_NOPE), lambda i: (0, i, 0))],
        out_shape=[sds(d, BF16), sds(d, BF16), sds(Q_LORA, F32), sds(Q_LORA, BF16), sds(wq, BF16), sds(KVP, F32),
                   sds(KV_LORA, BF16), sds(LANES, BF16), jax.ShapeDtypeStruct((2, t, N_HEADS * QK_NOPE), BF16)],
        compiler_params=_cp("parallel"),
    )(*args)


def _attn_prep_bwd(dq, dknv, dkr, dh, h, hn, hk, cq_pre, cq, kvpre, ckv, g_attn, g_kvin, w_dq, g_ql, w_uq, w_kv, g_kvl,
                   w_ukv, cos, sin):
    t, d = h.shape
    tr = 256
    n_steps = t // tr
    wq = N_HEADS * HEAD_PAD
    wk = N_HEADS * QK_NOPE

    def body(dq_ref, dknv_ref, dkr_ref, dh_ref, h_ref, hn_ref, hk_ref, cqp_ref, cq_ref, kvp_ref, ckv_ref,
             ga_ref, gk_ref, wdq_ref, gq_ref, wuq_ref, wkv_ref, gl_ref, wukv_ref, c_ref, s_ref,
             dho_ref, dhb_ref, dwuq_ref, dwdq_ref, dwukv_ref, dwkv_ref, dga_ref, dgk_ref, dgq_ref, dgl_ref,
             a_uq, a_dq, a_ukv, a_kv):
        i = pl.program_id(0)

        @pl.when(i == 0)
        def _():
            for ref in (a_uq, a_dq, a_ukv, a_kv, dga_ref, dgk_ref, dgq_ref, dgl_ref):
                ref[...] = jnp.zeros_like(ref)

        dqv = dq_ref[...]
        dcq = lax.dot_general(dqv, wuq_ref[...], NT_DIMS, preferred_element_type=F32)
        a_uq[...] += lax.dot_general(cq_ref[...], dqv, TN_DIMS, preferred_element_type=F32)
        dcq_pre, dg = _rms_bwd_math(cqp_ref[...], gq_ref[...], dcq)
        dgq_ref[...] += dg
        dcq_pre = dcq_pre.astype(BF16)
        dhn = lax.dot_general(dcq_pre, wdq_ref[...], NT_DIMS, preferred_element_type=F32)
        a_dq[...] += lax.dot_general(hn_ref[...], dcq_pre, TN_DIMS, preferred_element_type=F32)
        dckv = None
        for p in range(2):
            dk = dknv_ref[p].astype(BF16)
            part = lax.dot_general(dk, wukv_ref[p], NT_DIMS, preferred_element_type=F32)
            dckv = part if dckv is None else dckv + part
            a_ukv[p] += lax.dot_general(ckv_ref[...], dk, TN_DIMS, preferred_element_type=F32)
        dlat, dg = _rms_bwd_math(kvp_ref[:, :KV_LORA], gl_ref[...], dckv)
        dgl_ref[...] += dg
        dkr_pre = _rope_bwd_math(dkr_ref[...], c_ref[...], s_ref[...])
        dkvpre = jnp.concatenate([dlat, dkr_pre], axis=1).astype(BF16)
        dhk = lax.dot_general(dkvpre, wkv_ref[...], NT_DIMS, preferred_element_type=F32)
        a_kv[...] += lax.dot_general(hk_ref[...], dkvpre, TN_DIMS, preferred_element_type=F32)
        xv = h_ref[...]
        dx1, dg = _rms_bwd_math(xv, ga_ref[...], dhn)
        dga_ref[...] += dg
        dx2, dg = _rms_bwd_math(xv, gk_ref[...], dhk)
        dgk_ref[...] += dg
        dh_new = dh_ref[...] + dx1 + dx2
        dho_ref[...] = dh_new
        dhb_ref[...] = dh_new.astype(BF16)

        @pl.when(i == n_steps - 1)
        def _():
            dwuq_ref[...] = a_uq[...].astype(BF16)
            dwdq_ref[...] = a_dq[...].astype(BF16)
            dwukv_ref[...] = a_ukv[...].astype(BF16)
            dwkv_ref[...] = a_kv[...].astype(BF16)

    rows = lambda w: pl.BlockSpec((tr, w), lambda i: (i, 0))
    whole = lambda shape: pl.BlockSpec(shape, lambda i: (0,) * len(shape))
    weights = (g_attn, g_kvin, w_dq, g_ql, w_uq, w_kv, g_kvl, w_ukv)
    dw_shapes = [(Q_LORA, wq), (d, Q_LORA), (2, KV_LORA, wk), (d, KVP)]
    dg_shapes = [(1, d), (1, d), (1, Q_LORA), (1, KV_LORA)]
    return _tc_call(
        body, name="attn_prep_bwd", grid=(n_steps,),
        in_specs=[rows(wq), pl.BlockSpec((2, tr, wk), lambda i: (0, i, 0)), rows(LANES), rows(d), rows(d), rows(d),
                  rows(d), rows(Q_LORA), rows(Q_LORA), rows(KVP), rows(KV_LORA)]
        + [whole(a.shape) for a in weights] + [rows(LANES), rows(LANES)],
        out_specs=[rows(d), rows(d)] + [whole(s) for s in dw_shapes + dg_shapes],
        out_shape=[jax.ShapeDtypeStruct((t, d), F32), jax.ShapeDtypeStruct((t, d), BF16)]
        + [jax.ShapeDtypeStruct(s, BF16) for s in dw_shapes] + [jax.ShapeDtypeStruct(s, F32) for s in dg_shapes],
        scratch_shapes=[pltpu.VMEM(s, F32) for s in dw_shapes], compiler_params=_cp("arbitrary"),
    )(dq, dknv, dkr, dh, h, hn, hk, cq_pre, cq, kvpre, ckv, *weights, cos, sin)


ROW_CHUNK = 64
HALO = 16
WIN = ROW_CHUNK + 16
LANE_HALVES = (slice(0, LANES), slice(LANES, TC))


def _stage(s_ref, p, src):
    t = src.shape[0]
    s_ref[p, :HALO] = jnp.zeros((HALO, TC), BF16)
    s_ref[p, HALO:HALO + t] = src
    s_ref[p, HALO + t:] = jnp.zeros((HALO, TC), BF16)


def _window(s_ref, p, i, lanes):
    base = pl.multiple_of(i * ROW_CHUNK, ROW_CHUNK)
    return s_ref[p, pl.ds(base, ROW_CHUNK + 2 * HALO), lanes].astype(F32)[8:8 + WIN]


def _valid(x):
    return x[8:8 + ROW_CHUNK]


def _prev(x, k):
    return pltpu.roll(x, k, axis=0)


def _next(x, k):
    return pltpu.roll(x, WIN - k, axis=0)


def _taps(w_ref, lanes):
    return w_ref[0:1, lanes], w_ref[1:2, lanes], w_ref[2:3, lanes]


def _fold8(x):
    return jnp.sum(x.reshape(ROW_CHUNK // 8, 8, x.shape[-1]), axis=0)


def _store_rows(ref, idx, i, lanes, x):
    rows = pl.ds(pl.multiple_of(i * ROW_CHUNK, ROW_CHUNK), ROW_CHUNK)
    ref[(*idx, rows, lanes)] = x.astype(ref.dtype)


def _for_chunks(t, chunk):
    def step(i, carry):
        for lanes in LANE_HALVES:
            chunk(i, lanes)
        return carry

    lax.fori_loop(0, t // ROW_CHUNK, step, 0)


def _write_col_sums(acc_ref, outs):
    for k, (ref, row) in enumerate(outs):
        ref[row:row + 1, :] = jnp.sum(acc_ref[k], axis=0, keepdims=True)


def _shift_down(x, k):
    row = lax.broadcasted_iota(jnp.int32, x.shape, 0)
    return jnp.where(row >= k, pltpu.roll(x, k, axis=0), 0.0)


def _shift_up(x, k):
    n = x.shape[0]
    row = lax.broadcasted_iota(jnp.int32, x.shape, 0)
    return jnp.where(row < n - k, pltpu.roll(x, n - k, axis=0), 0.0)


def _conv3(x, w_ref):
    return _shift_down(x, 2) * w_ref[0:1, :] + _shift_down(x, 1) * w_ref[1:2, :] + x * w_ref[2:3, :]


def _col(parts, t):
    if parts is None:
        return pl.BlockSpec((t, TC), lambda j: (0, j))
    return pl.BlockSpec((parts, t, TC), lambda j: (0, 0, j))


def _staging(parts, t):
    return pltpu.VMEM((parts, t + 2 * HALO, TC), BF16)


def _scmix_fwd(z, w):
    t = z.shape[1]

    def body(z_ref, w_ref, m_ref):
        b, c, u = (z_ref[p].astype(F32) for p in range(3))
        m_ref[...] = (b * _conv3(c * u, w_ref)).astype(BF16)

    return _tc_call(
        body, name="scmix_fwd", grid=(D // TC,), in_specs=[_col(3, t), pl.BlockSpec((3, TC), lambda j: (0, j))],
        out_specs=_col(None, t), out_shape=jax.ShapeDtypeStruct((t, D), BF16), compiler_params=_cp("parallel"),
    )(z, w)


def _scmix_bwd(z, w, dm):
    t = z.shape[1]

    def body(z_ref, w_ref, dm_ref, dz_ref, dw_ref, s_ref, acc_ref):
        for p in range(3):
            _stage(s_ref, p, z_ref[p])
        _stage(s_ref, 3, dm_ref[...])
        acc_ref[...] = jnp.zeros_like(acc_ref)

        def chunk(i, lanes):
            w0, w1, w2 = _taps(w_ref, lanes)
            b, c, u, dm = (_window(s_ref, p, i, lanes) for p in range(4))
            cu = c * u
            cu1, cu2 = _prev(cu, 1), _prev(cu, 2)
            _store_rows(dz_ref, (0,), i, lanes, _valid(dm * (cu2 * w0 + cu1 * w1 + cu * w2)))
            dcv = dm * b
            dcu = dcv * w2 + _next(dcv, 1) * w1 + _next(dcv, 2) * w0
            _store_rows(dz_ref, (1,), i, lanes, _valid(dcu * u))
            _store_rows(dz_ref, (2,), i, lanes, _valid(dcu * c))
            for k, shifted in enumerate((cu2, cu1, cu)):
                acc_ref[k, :, lanes] += _fold8(_valid(dcv * shifted))

        _for_chunks(t, chunk)
        _write_col_sums(acc_ref, [(dw_ref, 0), (dw_ref, 1), (dw_ref, 2)])

    wspec = pl.BlockSpec((3, TC), lambda j: (0, j))
    return _tc_call(
        body, name="scmix_bwd", grid=(D // TC,), in_specs=[_col(3, t), wspec, _col(None, t)],
        out_specs=[_col(3, t), wspec],
        out_shape=[jax.ShapeDtypeStruct((3, t, D), BF16), jax.ShapeDtypeStruct((3, D), F32)],
        scratch_shapes=[_staging(4, t), pltpu.VMEM((3, 8, TC), F32)], compiler_params=_cp("parallel"),
    )(z, w, dm)


def _ffn_up_gate(hf, w_up, w, bias, name):
    t, d = hf.shape
    nb = F_FF // TC

    def body(hf_ref, wg_ref, wv_ref, w_ref, b_ref, up_ref, a_ref, prev_ref):
        @pl.when(pl.program_id(0) == 0)
        def _():
            prev_ref[...] = jnp.zeros_like(prev_ref)

        gc = _conv3(prev_ref[0].astype(F32), w_ref) + b_ref[...]
        a_ref[...] = (gc * jax.nn.sigmoid(gc) * prev_ref[1].astype(F32)).astype(BF16)
        hv = hf_ref[...]
        up_ref[0] = jnp.dot(hv, wg_ref[...], preferred_element_type=F32).astype(BF16)
        up_ref[1] = jnp.dot(hv, wv_ref[...], preferred_element_type=F32).astype(BF16)
        prev_ref[...] = up_ref[...]

    tile = lambda j: jnp.minimum(j, nb - 1)
    gated = lambda j: jnp.maximum(j - 1, 0)
    return _tc_call(
        body, name=name, grid=(nb + 1,),
        in_specs=[pl.BlockSpec((t, d), lambda j: (0, 0)), pl.BlockSpec((d, TC), lambda j: (0, tile(j))),
                  pl.BlockSpec((d, TC), lambda j: (0, nb + tile(j))), pl.BlockSpec((3, TC), lambda j: (0, gated(j))),
                  pl.BlockSpec((1, TC), lambda j: (0, gated(j)))],
        out_specs=[pl.BlockSpec((2, t, TC), lambda j: (0, 0, tile(j))), pl.BlockSpec((t, TC), lambda j: (0, gated(j)))],
        out_shape=[jax.ShapeDtypeStruct((2, t, F_FF), BF16), jax.ShapeDtypeStruct((t, F_FF), BF16)],
        scratch_shapes=[pltpu.VMEM((2, t, TC), BF16)], compiler_params=_cp("arbitrary"),
    )(hf, w_up, w_up, w, bias)


def _gate_bwd(up, w, bias, da, name):
    t = up.shape[1]

    def body(u_ref, w_ref, b_ref, da_ref, du_ref, dw_ref, db_ref, s_ref, acc_ref):
        for p in range(2):
            _stage(s_ref, p, u_ref[p])
        _stage(s_ref, 2, da_ref[...])
        acc_ref[...] = jnp.zeros_like(acc_ref)

        def chunk(i, lanes):
            w0, w1, w2 = _taps(w_ref, lanes)
            g, v, da = (_window(s_ref, p, i, lanes) for p in range(3))
            g1, g2 = _prev(g, 1), _prev(g, 2)
            gc = g2 * w0 + g1 * w1 + g * w2 + b_ref[:, lanes]
            sg = jax.nn.sigmoid(gc)
            _store_rows(du_ref, (1,), i, lanes, _valid(da * (gc * sg)))
            dgc = da * v * (sg * (1.0 + gc * (1.0 - sg)))
            _store_rows(du_ref, (0,), i, lanes, _valid(dgc * w2 + _next(dgc, 1) * w1 + _next(dgc, 2) * w0))
            for k, shifted in enumerate((g2, g1, g)):
                acc_ref[k, :, lanes] += _fold8(_valid(dgc * shifted))
            acc_ref[3, :, lanes] += _fold8(_valid(dgc))

        _for_chunks(t, chunk)
        _write_col_sums(acc_ref, [(dw_ref, 0), (dw_ref, 1), (dw_ref, 2), (db_ref, 0)])

    wspec = pl.BlockSpec((3, TC), lambda j: (0, j))
    bspec = pl.BlockSpec((1, TC), lambda j: (0, j))
    return _tc_call(
        body, name=name, grid=(F_FF // TC,), in_specs=[_col(2, t), wspec, bspec, _col(None, t)],
        out_specs=[_col(2, t), wspec, bspec],
        out_shape=[jax.ShapeDtypeStruct((2, t, F_FF), BF16), jax.ShapeDtypeStruct((3, F_FF), F32),
                   jax.ShapeDtypeStruct((1, F_FF), F32)],
        scratch_shapes=[_staging(3, t), pltpu.VMEM((4, 8, TC), F32)], compiler_params=_cp("parallel"),
    )(up, w, bias, da)


ATT_TQ = 256
ATT_SCALE = (QK_NOPE + QK_ROPE) ** -0.5


def _key_ranges(lvl):
    lo = lvl * ATT_TQ
    return ([(0, lo, False)] if lvl else []) + [(lo, lo + ATT_TQ, True)]


HEADS_PER_STEP = 2


def _fill_keys(k_ref, kn_ref, kr_ref):
    @pl.when(pl.program_id(1) == 0)
    def _():
        for hh in range(HEADS_PER_STEP):
            k_ref[hh, :, :QK_NOPE] = kn_ref[:, hh * QK_NOPE:(hh + 1) * QK_NOPE]
            k_ref[hh, :, QK_NOPE:] = kr_ref[...]


def _attn_probs(q, k_ref, lvl):
    scores = []
    for lo, hi, diagonal in _key_ranges(lvl):
        s = lax.dot_general(q, k_ref[lo:hi, :], NT_DIMS, preferred_element_type=F32) * ATT_SCALE
        if diagonal:
            row = lax.broadcasted_iota(jnp.int32, s.shape, 0)
            col = lax.broadcasted_iota(jnp.int32, s.shape, 1)
            seen = lax.shift_right_logical(col, CHUNK_SHIFT) <= lax.shift_right_logical(row, CHUNK_SHIFT)
            s = jnp.where(seen, s, NEG_INF)
        scores.append(s)
    m = jnp.max(scores[0], axis=1, keepdims=True)
    for s in scores[1:]:
        m = jnp.maximum(m, jnp.max(s, axis=1, keepdims=True))
    ps = [jnp.exp(s - m) for s in scores]
    total = jnp.sum(ps[0], axis=1, keepdims=True)
    for p in ps[1:]:
        total = total + jnp.sum(p, axis=1, keepdims=True)
    inv = 1.0 / total
    return [p * inv for p in ps]


def _per_query_block(qi, n_blocks, branch):
    for lvl in range(n_blocks):
        pl.when(qi == lvl)(lambda lvl=lvl: branch(lvl))


def _attn_specs(t):
    g = HEADS_PER_STEP
    q = pl.BlockSpec((ATT_TQ, g * HEAD_PAD), lambda h, i: (i, h))
    kn = pl.BlockSpec((None, t, g * QK_NOPE), lambda h, i: (0, 0, h))
    kr = pl.BlockSpec((t, LANES), lambda h, i: (0, 0))
    v = pl.BlockSpec((None, t, g * V_HEAD), lambda h, i: (1, 0, h))
    o = pl.BlockSpec((ATT_TQ, g * V_HEAD), lambda h, i: (i, h))
    return q, kn, kr, v, o


def _attn_fwd(q, knv, kr):
    t = q.shape[0]

    def body(q_ref, kn_ref, kr_ref, v_ref, o_ref, k_ref):
        _fill_keys(k_ref, kn_ref, kr_ref)

        def branch(lvl):
            for hh in range(HEADS_PER_STEP):
                vcols = slice(hh * V_HEAD, (hh + 1) * V_HEAD)
                ps = _attn_probs(q_ref[:, hh * HEAD_PAD:(hh + 1) * HEAD_PAD], k_ref.at[hh], lvl)
                o = None
                for p, (lo, hi, _) in zip(ps, _key_ranges(lvl)):
                    part = jnp.dot(p.astype(BF16), v_ref[lo:hi, vcols], preferred_element_type=F32)
                    o = part if o is None else o + part
                o_ref[:, vcols] = o.astype(BF16)

        _per_query_block(pl.program_id(1), t // ATT_TQ, branch)

    qs, kns, krs, vs, os_ = _attn_specs(t)
    return _tc_call(
        body, name="attn_fwd", grid=(N_HEADS // HEADS_PER_STEP, t // ATT_TQ), in_specs=[qs, kns, krs, vs],
        out_specs=os_, out_shape=jax.ShapeDtypeStruct((t, N_HEADS * V_HEAD), BF16),
        scratch_shapes=[pltpu.VMEM((HEADS_PER_STEP, t, HEAD_PAD), BF16)], compiler_params=_cp("parallel", "arbitrary"),
    )(q, knv, kr, knv)


def _attn_bwd(q, knv, kr, do, cos, sin):
    t = q.shape[0]

    def body(q_ref, kn_ref, kr_ref, v_ref, do_ref, c_ref, s_ref, dq_ref, dknv_ref, dkr_ref, k_ref, dk_ref):
        h, qi = pl.program_id(0), pl.program_id(1)
        _fill_keys(k_ref, kn_ref, kr_ref)

        @pl.when(qi == 0)
        def _():
            dknv_ref[1] = jnp.zeros(dknv_ref.shape[1:], F32)
            dk_ref[...] = jnp.zeros_like(dk_ref)

        @pl.when((qi == 0) & (h == 0))
        def _():
            dkr_ref[...] = jnp.zeros_like(dkr_ref)

        def branch(lvl):
            ranges = _key_ranges(lvl)
            for hh in range(HEADS_PER_STEP):
                qcols = slice(hh * HEAD_PAD, (hh + 1) * HEAD_PAD)
                vcols = slice(hh * V_HEAD, (hh + 1) * V_HEAD)
                qv, dov = q_ref[:, qcols], do_ref[:, vcols]
                ps = _attn_probs(qv, k_ref.at[hh], lvl)
                dps = [lax.dot_general(dov, v_ref[lo:hi, vcols], NT_DIMS, preferred_element_type=F32)
                       for lo, hi, _ in ranges]
                di = None
                for p, dp in zip(ps, dps):
                    part = jnp.sum(p * dp, axis=1, keepdims=True)
                    di = part if di is None else di + part
                dq = None
                for p, dp, (lo, hi, _) in zip(ps, dps, ranges):
                    ds = (p * (dp - di) * ATT_SCALE).astype(BF16)
                    part = jnp.dot(ds, k_ref[hh, lo:hi, :], preferred_element_type=F32)
                    dq = part if dq is None else dq + part
                    dk_ref[hh, lo:hi, :] += lax.dot_general(ds, qv, TN_DIMS, preferred_element_type=F32)
                    dknv_ref[1, lo:hi, vcols] += lax.dot_general(p.astype(BF16), dov, TN_DIMS,
                                                                 preferred_element_type=F32)
                dq_ref[:, hh * HEAD_PAD:hh * HEAD_PAD + QK_NOPE] = dq[:, :QK_NOPE].astype(BF16)
                dq_ref[:, hh * HEAD_PAD + QK_NOPE:(hh + 1) * HEAD_PAD] = _rope_bwd_math(
                    dq[:, QK_NOPE:], c_ref[...], s_ref[...]).astype(BF16)

        _per_query_block(qi, t // ATT_TQ, branch)

        @pl.when(qi == t // ATT_TQ - 1)
        def _():
            for hh in range(HEADS_PER_STEP):
                dknv_ref[0, :, hh * QK_NOPE:(hh + 1) * QK_NOPE] = dk_ref[hh, :, :QK_NOPE]
                dkr_ref[...] += dk_ref[hh, :, QK_NOPE:]

    qs, kns, krs, vs, os_ = _attn_specs(t)
    tab = pl.BlockSpec((ATT_TQ, LANES), lambda h, i: (i, 0))
    return _tc_call(
        body, name="attn_bwd", grid=(N_HEADS // HEADS_PER_STEP, t // ATT_TQ), in_specs=[qs, kns, krs, vs, os_, tab, tab],
        out_specs=[qs, pl.BlockSpec((2, t, HEADS_PER_STEP * QK_NOPE), lambda h, i: (0, 0, h)), krs],
        out_shape=[jax.ShapeDtypeStruct((t, N_HEADS * HEAD_PAD), BF16),
                   jax.ShapeDtypeStruct((2, t, N_HEADS * QK_NOPE), F32), jax.ShapeDtypeStruct((t, LANES), F32)],
        scratch_shapes=[pltpu.VMEM((HEADS_PER_STEP, t, HEAD_PAD), BF16), pltpu.VMEM((HEADS_PER_STEP, t, HEAD_PAD), F32)],
        compiler_params=_cp("arbitrary", "arbitrary"),
    )(q, knv, kr, knv, do, cos, sin)


def _adam_math(w, g, m, v):
    nm = ADAM_B1 * m + (1.0 - ADAM_B1) * g
    nv = ADAM_B2 * v + (1.0 - ADAM_B2) * (g * g)
    m_hat = nm / (1.0 - ADAM_B1 ** ADAM_STEP)
    v_hat = nv / (1.0 - ADAM_B2 ** ADAM_STEP)
    return -ADAM_LR * (m_hat / (jnp.sqrt(v_hat) + ADAM_EPS) + ADAM_WD * w), nm, nv


def _adamw_small(w, g, m, v):
    def body(w_ref, g_ref, m_ref, v_ref, d_ref, nm_ref, nv_ref):
        d_ref[...], nm_ref[...], nv_ref[...] = _adam_math(w_ref[...], g_ref[...], m_ref[...], v_ref[...])

    shp = jax.ShapeDtypeStruct(w.shape, F32)
    return _tc_call(body, name="adamw_small", out_shape=[shp] * 3)(w, g, m, v)


ADAM_SPLIT = 4


def _adamw_shards(ids, items, name):
    n = len(items)

    def body(ids_ref, *refs):
        outs = refs[len(refs) - 4 * n:]
        mine = pl.program_id(0) == ids_ref[0]
        for i in range(n):
            w_ref, m_ref, v_ref, gm_ref, gs_ref = refs[5 * i:5 * i + 5]
            g_ref, d_ref, nm_ref, nv_ref = outs[4 * i:4 * i + 4]

            @pl.when(mine)
            def _(g_ref=g_ref, gm_ref=gm_ref):
                g_ref[...] = gm_ref[...]

            @pl.when(jnp.logical_not(mine))
            def _(g_ref=g_ref, gs_ref=gs_ref):
                g_ref[...] = gs_ref[...]

            d_ref[...], nm_ref[...], nv_ref[...] = _adam_math(w_ref[...], g_ref[...], m_ref[...], v_ref[...])

    in_specs, out_specs, out_shape, args, carried, aliases = [], [], [], [ids], [], {}
    for i, it in enumerate(items):
        w = it["w"]
        r, c = w.shape[-2:]
        tr = r // 2 // ADAM_SPLIT
        assert tr % 8 == 0, (name, w.shape)
        layer = it.get("layer")
        if layer is None:
            wspec = pl.BlockSpec((tr, c), lambda h, k, ids: (h * ADAM_SPLIT + k, 0))
        else:
            wspec = pl.BlockSpec((None, tr, c), lambda h, k, ids, layer=layer: (layer, h * ADAM_SPLIT + k, 0))
        gspec = pl.BlockSpec((tr, c), lambda h, k, ids: (k, 0))
        in_specs += [wspec] * 3 + [gspec] * 2
        args += [w, it["m"], it["v"], it["g_mine"], it["g_sib"]]
        out_specs += [wspec] * 4
        out_shape += [jax.ShapeDtypeStruct(w.shape, F32)] * 4
        if it.get("prev") is not None:
            for k, p in enumerate(it["prev"]):
                aliases[1 + 5 * n + len(carried)] = 4 * i + k
                carried.append(p)
    res = _tc_call(
        body, name=name, prefetch=1, grid=(2, ADAM_SPLIT), in_specs=in_specs + [ANY] * len(carried),
        out_specs=out_specs, out_shape=out_shape, input_output_aliases=aliases,
        compiler_params=_cp("parallel", "parallel"),
    )(*args, *carried)
    return [res[4 * i:4 * i + 4] for i in range(n)]


def _peer_chip(k_me, j):
    return k_me ^ jnp.where(j == 0, 2, jnp.where(j == 1, 1, 3))


def _pair_sums(ids, gs, ras, name):
    n = len(gs)

    def body(ids_ref, *refs):
        for i in range(n):
            g_ref, ra_ref, o_ref = refs[2 * i], refs[2 * i + 1], refs[2 * n + i]
            o_ref[...] = (g_ref[...].astype(F32) + ra_ref[...].astype(F32)).astype(BF16)

    in_specs, out_specs, out_shape = [], [], []
    for g in gs:
        half, c = g.shape[1] // 2, g.shape[2]
        in_specs += [pl.BlockSpec((None, half, c), lambda j, ids: (_peer_chip(ids[1], j), ids[0], 0)),
                     pl.BlockSpec((None, half, c), lambda j, ids: (_peer_chip(ids[1], j), 0, 0))]
        out_specs.append(pl.BlockSpec((None, half, c), lambda j, ids: (j, 0, 0)))
        out_shape.append(jax.ShapeDtypeStruct((3, half, c), BF16))
    return _tc_call(
        body, name=name, prefetch=1, grid=(3,), in_specs=in_specs, out_specs=out_specs, out_shape=out_shape,
        compiler_params=_cp("parallel"),
    )(ids, *[a for pair in zip(gs, ras) for a in pair])


def _chip_sums(ids, gs, ras, rbs, name):
    n = len(gs)

    def body(ids_ref, *refs):
        for i in range(n):
            g_ref, ra_ref, rb_ref, o_ref = refs[3 * i], refs[3 * i + 1], refs[3 * i + 2], refs[3 * n + i]
            acc = g_ref[...].astype(F32) + ra_ref[...].astype(F32)
            for j in range(3):
                acc = acc + rb_ref[j].astype(F32)
            o_ref[...] = acc

    in_specs, out_specs, out_shape = [], [], []
    for g in gs:
        half, c = g.shape[1] // 2, g.shape[2]
        in_specs += [pl.BlockSpec((None, half, c), lambda i, ids: (ids[1], ids[0], 0)),
                     pl.BlockSpec((None, half, c), lambda i, ids: (ids[1], 0, 0)),
                     pl.BlockSpec((3, half, c), lambda i, ids: (0, 0, 0))]
        out_specs.append(pl.BlockSpec((half, c), lambda i, ids: (0, 0)))
        out_shape.append(jax.ShapeDtypeStruct((half, c), F32))
    return _tc_call(
        body, name=name, prefetch=1, grid=(1,), in_specs=in_specs, out_specs=out_specs, out_shape=out_shape,
        compiler_params=_cp("arbitrary"),
    )(ids, *[a for trio in zip(gs, ras, rbs) for a in trio])


def _position():
    x, y, c = lax.axis_index("x"), lax.axis_index("y"), lax.axis_index("c")
    chips = [(1 - x, y), (x, 1 - y), (1 - x, 1 - y)]
    return x, y, c, chips


def _shard_half(ref, wm, h):
    if wm.kind == "tiny":
        return ref
    if wm.nl == 2:
        return ref.at[h]
    return ref.at[pl.ds(pl.multiple_of(h * (wm.k // 2), 16), wm.k // 2), :]


def _region(full, wm, s, h):
    if wm.kind == "tiny":
        return full.at[s]
    cols = pl.ds(pl.multiple_of(s * wm.n, LANES), wm.n) if wm.kind == "col" else slice(None)
    if wm.nl == 2:
        rows = pl.ds(pl.multiple_of(s * wm.k, 16), wm.k) if wm.kind == "row" else slice(None)
        return full.at[slice(None) if h is None else h, rows, cols]
    if wm.kind == "col":
        rows = slice(None) if h is None else pl.ds(pl.multiple_of(h * (wm.k // 2), 16), wm.k // 2)
    elif h is None:
        rows = pl.ds(pl.multiple_of(s * wm.k, 16), wm.k)
    else:
        rows = pl.ds(pl.multiple_of(s * wm.k + h * (wm.k // 2), 16), wm.k // 2)
    return full.at[rows, cols]


def _full_shape(wm):
    if wm.kind == "tiny":
        return (N_CHIPS, wm.k, wm.n)
    shape = (wm.k, N_CHIPS * wm.n) if wm.kind == "col" else (N_CHIPS * wm.k, wm.n)
    return shape if wm.nl == 1 else (wm.nl,) + shape


def _handshake(peers):
    barrier = pltpu.get_barrier_semaphore()
    for peer in peers:
        pl.semaphore_signal(barrier, inc=1, device_id=peer, device_id_type=MESH)
    pl.semaphore_wait(barrier, len(peers))


def _all_gather_group(gi, shards):
    wms = AG_GROUPS[gi]
    nw = len(wms)

    def body(*refs):
        sh, full = refs[:nw], refs[nw:2 * nw]
        ici_s, ici_r, pass_s, pass_r, own_s, own_r = refs[2 * nw:]
        x, y, c, chips = _position()
        me, sibling = 2 * x + y, (x, y, 1 - c)
        _handshake([(*chip, c) for chip in chips] + [sibling])

        def rcopy(src, dst, s_sem, r_sem, to):
            return pltpu.make_async_remote_copy(src_ref=src, dst_ref=dst, send_sem=s_sem, recv_sem=r_sem,
                                                device_id=to, device_id_type=MESH)

        started = []
        for i, wm in enumerate(wms):
            for j, chip in enumerate(chips):
                started.append(rcopy(_shard_half(sh[i], wm, c), _region(full[i], wm, me, c),
                                     ici_s.at[i, j], ici_r.at[i, j], (*chip, c)))
                started[-1].start()
            started.append(rcopy(sh[i], _region(full[i], wm, me, None), own_s.at[i], own_r.at[i], sibling))
            started[-1].start()
        for i, wm in enumerate(wms):
            for j, chip in enumerate(chips):
                got = _region(full[i], wm, 2 * chip[0] + chip[1], c)
                rcopy(got, got, ici_s.at[i, j], ici_r.at[i, j], sibling).wait_recv()
                if wm.kind != "tiny":
                    started.append(rcopy(got, got, pass_s.at[i, j], pass_r.at[i, j], sibling))
                    started[-1].start()
        for i, wm in enumerate(wms):
            mine = _region(full[i], wm, me, None)
            rcopy(mine, mine, own_s.at[i], own_r.at[i], sibling).wait_recv()
            for j, chip in enumerate(chips):
                if wm.kind != "tiny":
                    got = _region(full[i], wm, 2 * chip[0] + chip[1], 1 - c)
                    rcopy(got, got, pass_s.at[i, j], pass_r.at[i, j], sibling).wait_recv()
        for cp in started:
            cp.wait_send()

    return pl.kernel(
        body, out_type=[jax.ShapeDtypeStruct(_full_shape(wm), s.dtype) for wm, s in zip(wms, shards)],
        mesh=plsc.ScalarSubcoreMesh(axis_name="sequencer", num_cores=1), name=f"ag_group{gi}",
        scratch_types=[pltpu.SemaphoreType.DMA((nw, 3))] * 4 + [pltpu.SemaphoreType.DMA((nw,))] * 2,
        compiler_params=pltpu.CompilerParams(collective_id=gi),
    )(*shards)


def _sequencer_call(body, name, cid, out_types, scratch, args):
    return pl.kernel(
        body, out_type=out_types, mesh=plsc.ScalarSubcoreMesh(axis_name="sequencer", num_cores=1), name=name,
        scratch_types=scratch, compiler_params=pltpu.CompilerParams(collective_id=cid),
    )(*args)


def _pair_exchange(gs, tag, cid):
    n = len(gs)

    def body(*refs):
        g, out, send_sems, recv_sems = refs[:n], refs[n:2 * n], refs[2 * n], refs[2 * n + 1]
        x, y, c, _ = _position()
        _handshake([(x, y, 1 - c)])
        cps = []
        for i in range(n):
            half = g[i].shape[1] // 2
            cps.append(pltpu.make_async_remote_copy(
                src_ref=g[i].at[:, pl.ds(pl.multiple_of((1 - c) * half, 16), half), :], dst_ref=out[i],
                send_sem=send_sems.at[i], recv_sem=recv_sems.at[i], device_id=(x, y, 1 - c), device_id_type=MESH))
            cps[-1].start()
        for cp in cps:
            cp.wait()

    return _sequencer_call(
        body, f"rs_pair_exchange{tag}", cid,
        [jax.ShapeDtypeStruct((a.shape[0], a.shape[1] // 2, a.shape[2]), a.dtype) for a in gs],
        [pltpu.SemaphoreType.DMA((n,)), pltpu.SemaphoreType.DMA((n,))], gs)


def _chip_exchange(ss, tag, cid):
    n = len(ss)

    def body(*refs):
        s, out, send_sems, recv_sems = refs[:n], refs[n:2 * n], refs[2 * n], refs[2 * n + 1]
        x, y, c, chips = _position()
        _handshake([(*chip, c) for chip in chips])
        cps = []
        for i in range(n):
            for j, chip in enumerate(chips):
                cps.append(pltpu.make_async_remote_copy(
                    src_ref=s[i].at[j], dst_ref=out[i].at[j], send_sem=send_sems.at[i, j], recv_sem=recv_sems.at[i, j],
                    device_id=(*chip, c), device_id_type=MESH))
                cps[-1].start()
        for cp in cps:
            cp.wait()

    return _sequencer_call(
        body, f"rs_chip_exchange{tag}", cid, [jax.ShapeDtypeStruct(a.shape, a.dtype) for a in ss],
        [pltpu.SemaphoreType.DMA((n, 3)), pltpu.SemaphoreType.DMA((n, 3))], ss)


def _pair_swap(g8s, tag, cid):
    n = len(g8s)

    def body(*refs):
        g, out, send_sems, recv_sems = refs[:n], refs[n:2 * n], refs[2 * n], refs[2 * n + 1]
        x, y, c, _ = _position()
        _handshake([(x, y, 1 - c)])
        cps = []
        for i in range(n):
            cps.append(pltpu.make_async_remote_copy(
                src_ref=g[i], dst_ref=out[i], send_sem=send_sems.at[i], recv_sem=recv_sems.at[i],
                device_id=(x, y, 1 - c), device_id_type=MESH))
            cps[-1].start()
        for cp in cps:
            cp.wait()

    return _sequencer_call(
        body, f"rs_pair_swap{tag}", cid, [jax.ShapeDtypeStruct(a.shape, a.dtype) for a in g8s],
        [pltpu.SemaphoreType.DMA((n,)), pltpu.SemaphoreType.DMA((n,))], g8s)


def _all_reduce_small(vec, name):
    r, cols = vec.shape

    def body(v_ref, o_ref, gath, send_sems, recv_sems):
        x, y, c, _ = _position()
        me = 4 * x + 2 * y + c
        gath[me] = v_ref[...]
        cps = []
        for rel in range(1, N_DEV):
            peer = (x ^ (rel >> 2), y ^ ((rel >> 1) & 1), c ^ (rel & 1))
            cps.append(pltpu.make_async_remote_copy(
                src_ref=v_ref, dst_ref=gath.at[me], send_sem=send_sems.at[rel - 1], recv_sem=recv_sems.at[rel - 1],
                device_id=peer, device_id_type=MESH))
        for cp in cps:
            cp.start()
        for rel in range(1, N_DEV):
            pltpu.make_async_remote_copy(
                src_ref=v_ref, dst_ref=gath.at[me ^ rel], send_sem=send_sems.at[rel - 1],
                recv_sem=recv_sems.at[rel - 1], device_id=(x, y, c), device_id_type=MESH).wait_recv()
        for cp in cps:
            cp.wait_send()
        acc = gath[0]
        for d in range(1, N_DEV):
            acc = acc + gath[d]
        o_ref[...] = acc

    vm = pl.BlockSpec(memory_space=pltpu.VMEM)
    return _tc_call(
        body, name=name, in_specs=[vm], out_specs=vm, out_shape=jax.ShapeDtypeStruct((r, cols), F32),
        scratch_shapes=[pltpu.VMEM((N_DEV, r, cols), F32), pltpu.SemaphoreType.DMA((N_DEV - 1,)),
                        pltpu.SemaphoreType.DMA((N_DEV - 1,))],
    )(vec)


def _rope_tables(positions):
    half = QK_ROPE // 2
    inv_freq = 1.0 / (ROPE_THETA ** (jnp.arange(half, dtype=F32) / half))
    ang = positions.astype(F32)[:, None] * inv_freq
    zeros = jnp.zeros((positions.shape[0], LANES - QK_ROPE), F32)
    cos, sin = jnp.cos(ang), jnp.sin(ang)
    return jnp.concatenate([cos, cos, zeros], axis=1), jnp.concatenate([sin, sin, zeros], axis=1)


def _local_step(x, positions, tgt, wf, small, rs):
    cos, sin = _rope_tables(positions)
    w_in, w_out = wf["sc_w_in"], wf["sc_w_out"]
    w_ups, w_downs = (wf["ffn_w_up0"], wf["ffn_w_up1"]), (wf["ffn_w_down0"], wf["ffn_w_down1"])
    w_kv, w_ukv, w_dq, w_uq, w_o = wf["w_kv"], wf["w_ukv"], wf["w_dq"], wf["w_uq"], wf["w_o"]
    attn_norm, ffn_norm = small["attn_norm"], small["ffn_norm"]
    conv_b = small["ffn_conv_b"]

    def ffn_fwd(h, hf, l, then):
        up, a = _ffn_up_gate(hf, w_ups[l], small["ffn_conv_w"][l], conv_b[l:l + 1], f"ffn{l}_up_gate")
        return then(a, w_downs[l], h), (hf, up, a)

    def ffn_bwd(h, dh_out, dh_out_b, l, saved, gi, hooks):
        run = lambda stage: hooks.get(stage, lambda: None)()
        hf, up, a = saved
        da = _nt(f"ffn{l}_down_dx", dh_out_b, w_downs[l], BF16)
        run("down_dx")
        d_down = _tn(f"ffn{l}_down_dw", a, dh_out_b, BF16)
        dup, d_cw, d_cb = _gate_bwd(up, small["ffn_conv_w"][l], conv_b[l:l + 1], da, f"ffn{l}_gate_bwd")
        run("gate_bwd")
        d_up = _dw_ffn_up(f"ffn{l}_up_dw", hf, dup)
        rs.start(gi, {f"ffn_w_down{l}": d_down.reshape(N_CHIPS, F_FF // N_CHIPS, D), f"ffn_w_up{l}": d_up})
        dh, dh_b, d_norm = _dx_norm_bwd(f"ffn{l}_up_dx", dup, w_ups[l], h, ffn_norm[l:l + 1], dh_out)
        run("up_dx")
        return dh, dh_b, d_cw, d_cb, d_norm

    hn0 = _rms_fwd(x, attn_norm[0:1], "attn0_norm")
    z = _nn_parts("sc_in", hn0, w_in, 3, BF16)
    mix = _scmix_fwd(z, small["sc_conv_w"])
    h1, hf0 = _nn_add_norm("sc_out", mix, w_out, x, ffn_norm[0:1])
    h2, ffn0_saved = ffn_fwd(h1, hf0, 0, lambda a, w, h: _nn("ffn0_down", a, w, F32, add=h))

    hn1, hk, cq_pre, cq, q, kvpre, ckv, kr, knv = _attn_prep(
        h2, attn_norm[1:2], small["kv_in_norm"], w_dq, small["q_latent_norm"], w_uq, w_kv, small["kv_latent_norm"],
        w_ukv, cos, sin)
    o = _attn_fwd(q, knv, kr)
    h3, hf1 = _nn_add_norm("attn_out", o, w_o, h2, ffn_norm[1:2])
    (loss, dh4, dh4_b, d_final), ffn1_saved = ffn_fwd(
        h3, hf1, 1, lambda a, w, h: _nn_add_loss("ffn1_down_loss", a, w, h, small["final_norm"], tgt))

    rows = D // N_CHIPS
    dh3, dh3_b, d_cw1, d_cb1, d_fn1 = ffn_bwd(h3, dh4, dh4_b, 1, ffn1_saved, 0, {})

    do = _nt("attn_out_dx", dh3_b, w_o, BF16)
    d_wo = _tn("attn_out_dw", o, dh3_b, BF16)
    rs.pair_sums(0)
    dq, dknv, dkr = _attn_bwd(q, knv, kr, do, cos, sin)
    rs.chip_sums(0)
    dh2, dh2_b, d_wuq, d_wdq, d_wukv, d_wkv, d_an1, d_kvin, d_qln, d_kvln = _attn_prep_bwd(
        dq, dknv, dkr, dh3, h2, hn1, hk, cq_pre, cq, kvpre, ckv, attn_norm[1:2], small["kv_in_norm"], w_dq,
        small["q_latent_norm"], w_uq, w_kv, small["kv_latent_norm"], w_ukv, cos, sin)
    rs.finish(0)
    by_owner = lambda dw: dw.reshape(dw.shape[0], N_CHIPS, -1).transpose(1, 0, 2)
    rs.start(1, {
        "w_o": d_wo.reshape(N_CHIPS, rows, D), "w_uq": by_owner(d_wuq), "w_dq": d_wdq.reshape(N_CHIPS, rows, Q_LORA),
        "w_ukv": by_owner(d_wukv.reshape(2 * KV_LORA, -1)).reshape(N_CHIPS, 2 * KV_LORA, -1),
        "w_kv": d_wkv.reshape(N_CHIPS, rows, KVP),
    })

    dh1, dh1_b, d_cw0, d_cb0, d_fn0 = ffn_bwd(h1, dh2, dh2_b, 0, ffn0_saved, 2, {
        "down_dx": lambda: rs.pair_sums(1), "gate_bwd": lambda: rs.chip_sums(1), "up_dx": lambda: rs.finish(1)})
    rs.pair_sums(2)

    d_wout = _tn("sc_out_dw", mix, dh1_b, BF16)
    dmix = _nt("sc_out_dx", dh1_b, w_out, BF16)
    dz, d_scw = _scmix_bwd(z, small["sc_conv_w"], dmix)
    d_win = _dw_sc_in(hn0, dz)
    rs.start(3, {"sc_w_out": d_wout.reshape(N_CHIPS, rows, D), "sc_w_in": d_win})
    dx, _, d_an0 = _dx_norm_bwd("sc_in_dx", dz, w_in, x, attn_norm[0:1], dh1)

    small_g = {
        "attn_norm": jnp.concatenate([d_an0, d_an1]), "ffn_norm": jnp.concatenate([d_fn0, d_fn1]),
        "final_norm": d_final, "kv_in_norm": d_kvin, "kv_latent_norm": d_kvln, "q_latent_norm": d_qln,
        "ffn_conv_b": jnp.concatenate([d_cb0, d_cb1]), "sc_conv_w": d_scw, "ffn_conv_w": jnp.stack([d_cw0, d_cw1]),
    }
    return loss, dx, small_g


RS_GROUPS = (("ffn_w_down1", "ffn_w_up1"), ("w_o", "w_uq", "w_dq", "w_ukv", "w_kv"),
             ("ffn_w_down0", "ffn_w_up0"), ("sc_w_out", "sc_w_in"))


class _ReduceScatter:
    def __init__(self, ids, finish):
        self.ids, self.grads, self.step, self.mine, self.sib, self.finish = ids, {}, {}, {}, {}, finish

    def _cid(self, gi):
        return len(AG_GROUPS) + 3 * gi

    def start(self, gi, grads):
        self.grads.update(grads)
        own = [grads[n] for n in RS_GROUPS[gi]]
        self.step[gi] = (own, _pair_exchange(own, gi, self._cid(gi)))

    def pair_sums(self, gi):
        own, ra = self.step[gi]
        sums = _pair_sums(self.ids, own, ra, f"rs_pair_sums{gi}")
        self.step[gi] = (own, ra, _chip_exchange(sums, gi, self._cid(gi) + 1))

    def chip_sums(self, gi):
        own, ra, rb = self.step[gi]
        mine = _chip_sums(self.ids, own, ra, rb, f"rs_chip_sums{gi}")
        self.mine.update(zip(RS_GROUPS[gi], mine))
        self.sib.update(zip(RS_GROUPS[gi], _pair_swap(mine, gi, self._cid(gi) + 2)))

SMALL_REPL = ("attn_norm", "ffn_norm", "final_norm", "kv_in_norm", "kv_latent_norm", "q_latent_norm", "ffn_conv_b")
SMALL_SHARDED = ("sc_conv_w", "ffn_conv_w")
SMALL_ROWS = 256


def _pad_heads(w_uq):
    per_head = w_uq.reshape(Q_LORA, -1, QK_NOPE + QK_ROPE)
    return jnp.pad(per_head, ((0, 0), (0, 0), (0, HEAD_PAD - QK_NOPE - QK_ROPE))).reshape(Q_LORA, -1)


def _pack_kv(w_dkv, w_kr):
    return jnp.concatenate([w_dkv, w_kr, jnp.zeros((w_kr.shape[0], LANES - QK_ROPE), w_kr.dtype)], axis=1)


def kernel(x, positions, attn_norm, ffn_norm, final_norm, sc_w_in, sc_conv_w, sc_w_out, kv_in_norm, w_dkv, kv_latent_norm, w_kr, w_uk, w_uv, w_dq, q_latent_norm, w_uq, w_o, ffn_w_up, ffn_conv_w, ffn_conv_b, ffn_w_down, loss_target, m_attn_norm, m_ffn_norm, m_final_norm, m_sc_w_in, m_sc_conv_w, m_sc_w_out, m_kv_in_norm, m_w_dkv, m_kv_latent_norm, m_w_kr, m_w_uk, m_w_uv, m_w_dq, m_q_latent_norm, m_w_uq, m_w_o, m_ffn_w_up, m_ffn_conv_w, m_ffn_conv_b, m_ffn_w_down, v_attn_norm, v_ffn_norm, v_final_norm, v_sc_w_in, v_sc_conv_w, v_sc_w_out, v_kv_in_norm, v_w_dkv, v_kv_latent_norm, v_w_kr, v_w_uk, v_w_uv, v_w_dq, v_q_latent_norm, v_w_uq, v_w_o, v_ffn_w_up, v_ffn_conv_w, v_ffn_conv_b, v_ffn_w_down):
    names = ("attn_norm", "ffn_norm", "final_norm", "sc_w_in", "sc_conv_w", "sc_w_out", "kv_in_norm", "w_dkv",
             "kv_latent_norm", "w_kr", "w_uk", "w_uv", "w_dq", "q_latent_norm", "w_uq", "w_o", "ffn_w_up",
             "ffn_conv_w", "ffn_conv_b", "ffn_w_down")
    w = dict(zip(names, (attn_norm, ffn_norm, final_norm, sc_w_in, sc_conv_w, sc_w_out, kv_in_norm, w_dkv,
                         kv_latent_norm, w_kr, w_uk, w_uv, w_dq, q_latent_norm, w_uq, w_o, ffn_w_up,
                         ffn_conv_w, ffn_conv_b, ffn_w_down)))
    m = dict(zip(names, (m_attn_norm, m_ffn_norm, m_final_norm, m_sc_w_in, m_sc_conv_w, m_sc_w_out, m_kv_in_norm,
                         m_w_dkv, m_kv_latent_norm, m_w_kr, m_w_uk, m_w_uv, m_w_dq, m_q_latent_norm, m_w_uq, m_w_o,
                         m_ffn_w_up, m_ffn_conv_w, m_ffn_conv_b, m_ffn_w_down)))
    v = dict(zip(names, (v_attn_norm, v_ffn_norm, v_final_norm, v_sc_w_in, v_sc_conv_w, v_sc_w_out, v_kv_in_norm,
                         v_w_dkv, v_kv_latent_norm, v_w_kr, v_w_uk, v_w_uv, v_w_dq, v_q_latent_norm, v_w_uq, v_w_o,
                         v_ffn_w_up, v_ffn_conv_w, v_ffn_conv_b, v_ffn_w_down)))

    _ORDER[0] = None
    ix, iy, ic = lax.axis_index("x"), lax.axis_index("y"), lax.axis_index("c")
    chip = 2 * ix + iy
    ids = jnp.stack([ic, chip]).astype(jnp.int32)

    def shards_of(t):
        return {
            "sc_w_in": t["sc_w_in"][0], "sc_w_out": t["sc_w_out"][0], "ffn_w_up": t["ffn_w_up"],
            "ffn_w_down": t["ffn_w_down"], "w_kv": _pack_kv(t["w_dkv"], t["w_kr"]),
            "w_ukv": jnp.stack([t["w_uk"], t["w_uv"]]), "w_dq": t["w_dq"][0], "w_uq": _pad_heads(t["w_uq"][0]),
            "w_o": t["w_o"][0],
        }

    ws, ms, vs = shards_of(w), shards_of(m), shards_of(v)

    def ag_shard(name):
        if name == "sc_conv_w":
            return sc_conv_w[0]
        if name == "ffn_conv_w":
            return ffn_conv_w.reshape(6, -1)
        if name[:-1] in ("ffn_w_up", "ffn_w_down"):
            return ws[name[:-1]][int(name[-1])].astype(BF16)
        return ws[name].astype(BF16)

    wf = {}
    for gi, wms in enumerate(AG_GROUPS):
        fulls = _all_gather_group(gi, [ag_shard(wm.name) for wm in wms])
        wf.update({wm.name: f for wm, f in zip(wms, fulls)})
    small = {
        "attn_norm": attn_norm, "ffn_norm": ffn_norm, "final_norm": final_norm[None], "kv_in_norm": kv_in_norm[None],
        "kv_latent_norm": kv_latent_norm[None], "q_latent_norm": q_latent_norm, "ffn_conv_b": ffn_conv_b,
        "sc_conv_w": wf["sc_conv_w"].transpose(1, 0, 2).reshape(3, D),
        "ffn_conv_w": wf["ffn_conv_w"].reshape(N_CHIPS, 2, 3, -1).transpose(1, 2, 0, 3).reshape(2, 3, F_FF),
    }

    res = {}

    merged = lambda a: a.reshape(2 * KV_LORA, -1)

    def adamw_group(gi):
        items = []
        for key in RS_GROUPS[gi]:
            n, layer = (key[:-1], int(key[-1])) if key[:-1] in ("ffn_w_up", "ffn_w_down") else (key, None)
            w_, m_, v_ = (merged(t[n]) for t in (ws, ms, vs)) if n == "w_ukv" else (ws[n], ms[n], vs[n])
            items.append(dict(name=n, w=w_, m=m_, v=v_, g_mine=rs.mine[key], g_sib=rs.sib[key], layer=layer,
                              prev=res.get(n)))
        for it, out in zip(items, _adamw_shards(ids, items, f"adamw_group{gi}")):
            res[it["name"]] = out

    rs = _ReduceScatter(ids, adamw_group)
    loss, dx, small_g = _local_step(x[0], positions[0], loss_target[0], wf, small, rs)

    s_order = SMALL_REPL + SMALL_SHARDED
    flat = jnp.concatenate([small_g[n].reshape(-1) for n in s_order] + [loss.reshape(-1)])
    flat = jnp.pad(flat, (0, SMALL_ROWS * LANES - flat.shape[0])).reshape(SMALL_ROWS, LANES)
    red = _all_reduce_small(flat, "ar_small").reshape(-1)
    sg, off = {}, 0
    for n in s_order:
        sz = small_g[n].size
        sg[n] = red[off:off + sz].reshape(small_g[n].shape)
        off += sz
    loss_out = red[off]
    grads = {n: sg[n].reshape(w[n].shape) for n in SMALL_REPL}
    grads["sc_conv_w"] = lax.dynamic_slice_in_dim(sg["sc_conv_w"], chip * (D // N_CHIPS), D // N_CHIPS, axis=1)[None]
    grads["ffn_conv_w"] = lax.dynamic_slice_in_dim(sg["ffn_conv_w"], chip * (F_FF // N_CHIPS), F_FF // N_CHIPS, axis=2)

    rs.chip_sums(2)
    rs.pair_sums(3)
    rs.finish(2)
    rs.chip_sums(3)
    rs.finish(3)
    outs = [grads, {}, {}, {}]
    for k, dst in enumerate(outs):
        for n in ("sc_w_in", "sc_w_out", "w_dq", "w_o"):
            dst[n] = res[n][k][None]
        unpadded = res["w_uq"][k].reshape(Q_LORA, -1, HEAD_PAD)[:, :, :QK_NOPE + QK_ROPE]
        dst["w_uq"] = unpadded.reshape(w_uq.shape)
        dst["ffn_w_up"], dst["ffn_w_down"] = res["ffn_w_up"][k], res["ffn_w_down"][k]
        dst["w_dkv"], dst["w_kr"] = res["w_kv"][k][:, :KV_LORA], res["w_kv"][k][:, KV_LORA:KV_LORA + QK_ROPE]
        dst["w_uk"], dst["w_uv"] = res["w_ukv"][k][:KV_LORA], res["w_ukv"][k][KV_LORA:]
    grads, delta, new_m, new_v = outs

    small_names = SMALL_REPL + SMALL_SHARDED

    def pack_small(tree):
        return jnp.concatenate([tree[n].reshape(-1) for n in small_names]).reshape(-1, LANES)

    small_res = _adamw_small(pack_small(w), pack_small(grads), pack_small(m), pack_small(v))
    for slab, dst in zip(small_res, (delta, new_m, new_v)):
        f, off = slab.reshape(-1), 0
        for n in small_names:
            dst[n] = f[off:off + w[n].size].reshape(w[n].shape)
            off += w[n].size

    _ORDER[0] = None
    return (loss_out, dx[None], *[grads[n] for n in names], *[delta[n] for n in names],
            *[new_m[n] for n in names], *[new_v[n] for n in names])
```

```python
from typing import NamedTuple

import jax
import jax.numpy as jnp
from jax import lax
from jax.experimental import pallas as pl
from jax.experimental.pallas import tpu as pltpu
from jax.experimental.pallas import tpu_sc as plsc

F32 = jnp.float32
BF16 = jnp.bfloat16

T = 2048
D = 1024
F_FF = 2816
N_HEADS = 8
QK_NOPE = 128
QK_ROPE = 64
V_HEAD = 128
Q_LORA = 384
KV_LORA = 256
CHUNK_SHIFT = 6
ROPE_THETA = 10000.0
EPS = 1e-6
NEG_INF = -1e30
HEAD_PAD = 256
KVP = KV_LORA + 128

ADAM_LR = 0.001
ADAM_B1 = 0.9
ADAM_B2 = 0.999
ADAM_EPS = 1e-08
ADAM_WD = 0.01
ADAM_STEP = 10

N_CHIPS = 4
N_DEV = 8
LANES = 128
TC = 256
V7X_VMEM_LIMIT = 56 * 1024 * 1024

MESH = pl.DeviceIdType.MESH
ANY = pl.BlockSpec(memory_space=pl.ANY)


class _W(NamedTuple):
    name: str
    kind: str
    nl: int
    k: int
    n: int


AG_GROUPS = (
    (_W("sc_w_in", "col", 1, D, 3 * D // N_CHIPS), _W("sc_conv_w", "tiny", 1, 3, D // N_CHIPS),
     _W("ffn_conv_w", "tiny", 1, 6, F_FF // N_CHIPS), _W("sc_w_out", "row", 1, D // N_CHIPS, D)),
    (_W("ffn_w_up0", "col", 1, D, 2 * F_FF // N_CHIPS),),
    (_W("ffn_w_down0", "row", 1, F_FF // N_CHIPS, D),),
    (_W("w_kv", "row", 1, D // N_CHIPS, KVP), _W("w_ukv", "col", 2, KV_LORA, N_HEADS * QK_NOPE // N_CHIPS),
     _W("w_dq", "row", 1, D // N_CHIPS, Q_LORA),
     _W("w_uq", "col", 1, Q_LORA, N_HEADS * HEAD_PAD // N_CHIPS),
     _W("w_o", "row", 1, N_HEADS * V_HEAD // N_CHIPS, D)),
    (_W("ffn_w_up1", "col", 1, D, 2 * F_FF // N_CHIPS), _W("ffn_w_down1", "row", 1, F_FF // N_CHIPS, D)),
)


def _cp(*sem):
    return pltpu.CompilerParams(dimension_semantics=sem, vmem_limit_bytes=V7X_VMEM_LIMIT)


_ORDER = [None]


def _tc_call(body, *, name, out_shape, in_specs=None, out_specs=None, grid=(), scratch_shapes=(), prefetch=0,
             input_output_aliases=None, compiler_params=None):
    def run(*args):
        specs = [pl.BlockSpec(memory_space=pltpu.VMEM)] * (len(args) - prefetch) if in_specs is None else list(in_specs)
        inner, dep = body, _ORDER[0]
        if dep is not None:
            unread = prefetch + len(specs)
            specs, args = specs + [ANY], (*args, dep)

            def inner(*refs):
                return body(*refs[:unread], *refs[unread + 1:])

        kwargs = dict(name=name, out_shape=out_shape, input_output_aliases=input_output_aliases or {},
                      compiler_params=compiler_params)
        if prefetch:
            kwargs["grid_spec"] = pltpu.PrefetchScalarGridSpec(
                num_scalar_prefetch=prefetch, grid=grid, in_specs=specs, out_specs=out_specs,
                scratch_shapes=scratch_shapes)
        else:
            kwargs.update(grid=grid, in_specs=specs, scratch_shapes=scratch_shapes)
            if out_specs is not None:
                kwargs["out_specs"] = out_specs
        out = pl.pallas_call(inner, **kwargs)(*args)
        _ORDER[0] = out[0] if isinstance(out, (list, tuple)) else out
        return out

    return run


def _tile(n, cands):
    for c in cands:
        if n % c == 0:
            return c
    raise ValueError(f"no tile for {n}")


NN_DIMS = (((1,), (0,)), ((), ()))
NT_DIMS = (((1,), (1,)), ((), ()))
TN_DIMS = (((0,), (0,)), ((), ()))
M_TILES = (1024, 512, 384, 256, 128)
N_TILES = (1408, 1024, 768, 512, 384, 256, 128)
MM_BLOCK_BYTES = 36 * 1024 * 1024


def _fit(m, n, block_bytes, m_tiles=M_TILES, n_tiles=N_TILES):
    for tm in [c for c in m_tiles if m % c == 0]:
        for tn in [c for c in n_tiles if n % c == 0]:
            if 2 * block_bytes(tm, tn) + 4 * tm * tn <= MM_BLOCK_BYTES:
                return tm, tn
    raise ValueError(f"no tiles for {m} x {n}")


def _size(x):
    return x.dtype.itemsize


def _mm(name, a, b, dims, grid, a_spec, b_spec, o_spec, o_sds, add=None, red=None, acc_shape=None):
    n_red = None if red is None else grid[red]

    def body(*refs):
        a_ref, b_ref = refs[0], refs[1]
        add_ref = refs[2] if add is not None else None
        o_ref = refs[3] if add is not None else refs[2]
        part = lax.dot_general(a_ref[...].astype(BF16), b_ref[...].astype(BF16), dims, preferred_element_type=F32)
        if red is None:
            if add is not None:
                part = part + add_ref[...]
            o_ref[...] = part.astype(o_ref.dtype)
            return
        acc_ref = refs[-1]
        r = pl.program_id(red)

        @pl.when(r == 0)
        def _():
            acc_ref[...] = part

        @pl.when(r > 0)
        def _():
            acc_ref[...] += part

        @pl.when(r == n_red - 1)
        def _():
            o_ref[...] = acc_ref[...].astype(o_ref.dtype)

    sem = tuple("arbitrary" if ax == red else "parallel" for ax in range(len(grid)))
    in_specs = [a_spec, b_spec] + ([o_spec] if add is not None else [])
    args = (a, b) + ((add,) if add is not None else ())
    return _tc_call(
        body, name=name, grid=grid, in_specs=in_specs, out_specs=o_spec, out_shape=o_sds,
        scratch_shapes=[] if red is None else [pltpu.VMEM(acc_shape, F32)], compiler_params=_cp(*sem),
    )(*args)


def _nn(name, a, b, out_dtype, add=None, lead=None):
    (m, k), n = a.shape, b.shape[-1]
    osz = jnp.dtype(out_dtype).itemsize + (4 if add is not None else 0)
    tm, tn = _fit(m, n, lambda tm, tn: tm * k * _size(a) + k * tn * _size(b) + tm * tn * osz)
    if lead is None:
        b_spec = pl.BlockSpec((k, tn), lambda i, j: (0, j))
    else:
        b_spec = pl.BlockSpec((None, k, tn), lambda i, j: (lead, 0, j))
    return _mm(name, a, b, NN_DIMS, (m // tm, n // tn), pl.BlockSpec((tm, k), lambda i, j: (i, 0)), b_spec,
               pl.BlockSpec((tm, tn), lambda i, j: (i, j)), jax.ShapeDtypeStruct((m, n), out_dtype), add=add)


def _nn_parts(name, a, b, parts, out_dtype, lead=None, stacked=False):
    m, k = a.shape
    c = b.shape[-1] if stacked else b.shape[-1] // parts
    osz = jnp.dtype(out_dtype).itemsize
    tm, tn = _fit(m, c, lambda tm, tn: tm * k * _size(a) + k * tn * _size(b) + tm * tn * osz)
    nb = c // tn
    if stacked:
        b_spec = pl.BlockSpec((None, k, tn), lambda i, p, j: (p, 0, j))
    elif lead is None:
        b_spec = pl.BlockSpec((k, tn), lambda i, p, j: (0, p * nb + j))
    else:
        b_spec = pl.BlockSpec((None, k, tn), lambda i, p, j: (lead, 0, p * nb + j))
    return _mm(name, a, b, NN_DIMS, (m // tm, parts, nb), pl.BlockSpec((tm, k), lambda i, p, j: (i, 0)), b_spec,
               pl.BlockSpec((None, tm, tn), lambda i, p, j: (p, i, j)), jax.ShapeDtypeStruct((parts, m, c), out_dtype))


def _nt(name, a, b, out_dtype, lead=None):
    (m, k), n = a.shape, b.shape[-2]
    osz = jnp.dtype(out_dtype).itemsize
    tm, tn = _fit(m, n, lambda tm, tn: tm * k * _size(a) + tn * k * _size(b) + tm * tn * osz)
    if lead is None:
        b_spec = pl.BlockSpec((tn, k), lambda i, j: (j, 0))
    else:
        b_spec = pl.BlockSpec((None, tn, k), lambda i, j: (lead, j, 0))
    return _mm(name, a, b, NT_DIMS, (m // tm, n // tn), pl.BlockSpec((tm, k), lambda i, j: (i, 0)), b_spec,
               pl.BlockSpec((tm, tn), lambda i, j: (i, j)), jax.ShapeDtypeStruct((m, n), out_dtype))


def _tn(name, a, b, out_dtype):
    (k, m), n = a.shape, b.shape[1]
    osz = jnp.dtype(out_dtype).itemsize
    tm, tn = _fit(m, n, lambda tm, tn: k * tm * _size(a) + k * tn * _size(b) + tm * tn * osz,
                  m_tiles=(512, 384, 256, 128), n_tiles=(n,) + N_TILES)
    return _mm(name, a, b, TN_DIMS, (m // tm, n // tn), pl.BlockSpec((k, tm), lambda i, j: (0, i)),
               pl.BlockSpec((k, tn), lambda i, j: (0, j)), pl.BlockSpec((tm, tn), lambda i, j: (i, j)),
               jax.ShapeDtypeStruct((m, n), out_dtype))


def _nn_add_norm(name, a, b, add, g):
    (m, k), n = a.shape, b.shape[1]
    tm = 512

    def body(a_ref, b_ref, add_ref, g_ref, h_ref, hn_ref):
        h = jnp.dot(a_ref[...], b_ref[...], preferred_element_type=F32) + add_ref[...]
        h_ref[...] = h
        hn_ref[...] = _rms_rows(h, g_ref[...]).astype(BF16)

    rows = lambda w: pl.BlockSpec((tm, w), lambda i: (i, 0))
    return _tc_call(
        body, name=name, grid=(m // tm,),
        in_specs=[rows(k), pl.BlockSpec((k, n), lambda i: (0, 0)), rows(n), pl.BlockSpec((1, n), lambda i: (0, 0))],
        out_specs=[rows(n), rows(n)],
        out_shape=[jax.ShapeDtypeStruct((m, n), F32), jax.ShapeDtypeStruct((m, n), BF16)], compiler_params=_cp("parallel"),
    )(a, b, add, g)


def _nn_add_loss(name, a, b, add, g, tgt):
    (m, k), n = a.shape, b.shape[1]
    tm = 512

    def body(a_ref, b_ref, add_ref, g_ref, t_ref, loss_ref, dh_ref, dhb_ref, dg_ref):
        xv = jnp.dot(a_ref[...], b_ref[...], preferred_element_type=F32) + add_ref[...]
        gv = g_ref[...]
        r = lax.rsqrt(jnp.mean(xv * xv, axis=1, keepdims=True) + EPS)
        err = xv * r * gv - t_ref[...]
        part = 0.5 * jnp.sum(jnp.mean(err * err, axis=1, keepdims=True), axis=0, keepdims=True)
        dx, dg = _rms_bwd_math(xv, gv, err * (1.0 / n))
        dh_ref[...] = dx
        dhb_ref[...] = dx.astype(BF16)

        @pl.when(pl.program_id(0) == 0)
        def _():
            dg_ref[...] = jnp.zeros_like(dg_ref)
            loss_ref[...] = jnp.zeros_like(loss_ref)

        dg_ref[...] += dg
        loss_ref[...] += jnp.broadcast_to(part, loss_ref.shape)

    rows = lambda w: pl.BlockSpec((tm, w), lambda i: (i, 0))
    vec = pl.BlockSpec((1, n), lambda i: (0, 0))
    return _tc_call(
        body, name=name, grid=(m // tm,),
        in_specs=[rows(k), pl.BlockSpec((k, n), lambda i: (0, 0)), rows(n), vec, rows(n)],
        out_specs=[pl.BlockSpec((1, LANES), lambda i: (0, 0)), rows(n), rows(n), vec],
        out_shape=[jax.ShapeDtypeStruct((1, LANES), F32), jax.ShapeDtypeStruct((m, n), F32),
                   jax.ShapeDtypeStruct((m, n), BF16), jax.ShapeDtypeStruct((1, n), F32)],
        compiler_params=_cp("arbitrary"),
    )(a, b, add, g, tgt)


def _dx_norm_bwd(name, a, b, x, g, add):
    parts, t, c = a.shape
    d = b.shape[0]
    tm = 256

    def body(a_ref, b_ref, x_ref, g_ref, add_ref, dx_ref, dxb_ref, dg_ref):
        dy = None
        for p in range(parts):
            part = lax.dot_general(a_ref[p], b_ref[:, p * c:(p + 1) * c], NT_DIMS, preferred_element_type=F32)
            dy = part if dy is None else dy + part
        dx, dg = _rms_bwd_math(x_ref[...], g_ref[...], dy)
        dx = dx + add_ref[...]
        dx_ref[...] = dx
        dxb_ref[...] = dx.astype(BF16)

        @pl.when(pl.program_id(0) == 0)
        def _():
            dg_ref[...] = jnp.zeros_like(dg_ref)

        dg_ref[...] += dg

    rows = pl.BlockSpec((tm, d), lambda i: (i, 0))
    vec = pl.BlockSpec((1, d), lambda i: (0, 0))
    return _tc_call(
        body, name=name, grid=(t // tm,),
        in_specs=[pl.BlockSpec((parts, tm, c), lambda i: (0, i, 0)), pl.BlockSpec(b.shape, lambda i: (0, 0)), rows, vec,
                  rows],
        out_specs=[rows, rows, vec],
        out_shape=[jax.ShapeDtypeStruct((t, d), F32), jax.ShapeDtypeStruct((t, d), BF16),
                   jax.ShapeDtypeStruct((1, d), F32)],
        compiler_params=_cp("arbitrary"),
    )(a, b, x, g, add)


def _dw_sc_in(hn, dz):
    t, tn, tm = hn.shape[0], TC, D
    per_part, per_chip = D // tn, 3 * D // N_CHIPS // tn
    return _mm("sc_in_dw", hn, dz, TN_DIMS, (D // tm, 3 * D // tn), pl.BlockSpec((t, tm), lambda i, j: (0, i)),
               pl.BlockSpec((None, t, tn), lambda i, j: (j // per_part, 0, j % per_part)),
               pl.BlockSpec((None, tm, tn), lambda i, j: (j // per_chip, i, j % per_chip)),
               jax.ShapeDtypeStruct((N_CHIPS, D, 3 * D // N_CHIPS), BF16))


def _dw_ffn_up(name, hf, dup):
    t, tm, ns = hf.shape[0], D, 2 * F_FF // N_CHIPS
    return _mm(name, hf, dup, TN_DIMS, (N_CHIPS, D // tm), pl.BlockSpec((t, tm), lambda s, i: (0, i)),
               pl.BlockSpec((None, t, ns), lambda s, i: (s // 2, 0, s % 2)),
               pl.BlockSpec((None, tm, ns), lambda s, i: (s, i, 0)), jax.ShapeDtypeStruct((N_CHIPS, D, ns), BF16))


def _rms_fwd(x, g, name):
    t, d = x.shape
    tr = 512

    def body(x_ref, g_ref, o_ref):
        xv = x_ref[...]
        r = lax.rsqrt(jnp.mean(xv * xv, axis=1, keepdims=True) + EPS)
        o_ref[...] = (xv * r * g_ref[...]).astype(o_ref.dtype)

    row = pl.BlockSpec((tr, d), lambda i: (i, 0))
    return _tc_call(
        body, name=name, grid=(t // tr,), in_specs=[row, pl.BlockSpec((1, d), lambda i: (0, 0))],
        out_specs=row, out_shape=jax.ShapeDtypeStruct((t, d), BF16), compiler_params=_cp("parallel"),
    )(x, g)


def _rms_bwd_math(xv, g, dy):
    r = lax.rsqrt(jnp.mean(xv * xv, axis=1, keepdims=True) + EPS)
    xh = xv * r
    gy = dy * g
    dx = r * (gy - xh * jnp.mean(gy * xh, axis=1, keepdims=True))
    dg = jnp.sum(dy * xh, axis=0, keepdims=True)
    return dx, dg


def _rot_half(x):
    lane = lax.broadcasted_iota(jnp.int32, x.shape, 1)
    return jnp.where((lane % QK_ROPE) < QK_ROPE // 2, -pltpu.roll(x, LANES - 32, axis=1),
                     pltpu.roll(x, 32, axis=1))


def _rope_fwd_math(x, cos, sin):
    return x * cos + _rot_half(x) * sin


def _rope_bwd_math(dy, cos, sin):
    return dy * cos - _rot_half(dy * sin)


def _rms_rows(x, g):
    return x * lax.rsqrt(jnp.mean(x * x, axis=1, keepdims=True) + EPS) * g


def _attn_prep(h, g_attn, g_kvin, w_dq, g_ql, w_uq, w_kv, g_kvl, w_ukv, cos, sin):
    t, d = h.shape
    tr = 256
    wq = N_HEADS * HEAD_PAD

    def body(h_ref, ga_ref, gk_ref, wdq_ref, gq_ref, wuq_ref, wkv_ref, gl_ref, wukv_ref, c_ref, s_ref,
             hn_ref, hk_ref, cqp_ref, cq_ref, q_ref, kvp_ref, ckv_ref, kr_ref, knv_ref):
        xv, cv, sv = h_ref[...], c_ref[...], s_ref[...]
        xh = xv * lax.rsqrt(jnp.mean(xv * xv, axis=1, keepdims=True) + EPS)
        hn = (xh * ga_ref[...]).astype(BF16)
        hk = (xh * gk_ref[...]).astype(BF16)
        hn_ref[...], hk_ref[...] = hn, hk
        cq_pre = jnp.dot(hn, wdq_ref[...], preferred_element_type=F32)
        cqp_ref[...] = cq_pre
        cq = _rms_rows(cq_pre, gq_ref[...]).astype(BF16)
        cq_ref[...] = cq
        for hd in range(N_HEADS):
            lo = hd * HEAD_PAD
            qh = jnp.dot(cq, wuq_ref[:, lo:lo + HEAD_PAD], preferred_element_type=F32)
            q_ref[:, lo:lo + QK_NOPE] = qh[:, :QK_NOPE].astype(BF16)
            q_ref[:, lo + QK_NOPE:lo + HEAD_PAD] = _rope_fwd_math(qh[:, QK_NOPE:], cv, sv).astype(BF16)
        kvpre = jnp.dot(hk, wkv_ref[...], preferred_element_type=F32)
        kvp_ref[...] = kvpre
        ckv = _rms_rows(kvpre[:, :KV_LORA], gl_ref[...]).astype(BF16)
        ckv_ref[...] = ckv
        kr_ref[...] = _rope_fwd_math(kvpre[:, KV_LORA:], cv, sv).astype(BF16)
        for p in range(2):
            knv_ref[p] = jnp.dot(ckv, wukv_ref[p], preferred_element_type=F32).astype(BF16)

    rows = lambda w: pl.BlockSpec((tr, w), lambda i: (i, 0))
    whole = lambda a: pl.BlockSpec(a.shape, lambda i: (0,) * a.ndim)
    sds = lambda w, dt: jax.ShapeDtypeStruct((t, w), dt)
    args = (h, g_attn, g_kvin, w_dq, g_ql, w_uq, w_kv, g_kvl, w_ukv, cos, sin)
    return _tc_call(
        body, name="attn_prep", grid=(t // tr,),
        in_specs=[rows(d)] + [whole(a) for a in args[1:9]] + [rows(LANES), rows(LANES)],
        out_specs=[rows(d), rows(d), rows(Q_LORA), rows(Q_LORA), rows(wq), rows(KVP), rows(KV_LORA), rows(LANES),
                   pl.BlockSpec((2, tr, N_HEADS * QK_NOPE), lambda i: (0, i, 0))],
        out_shape=[sds(d, BF16), sds(d, BF16), sds(Q_LORA, F32), sds(Q_LORA, BF16), sds(wq, BF16), sds(KVP, F32),
                   sds(KV_LORA, BF16), sds(LANES, BF16), jax.ShapeDtypeStruct((2, t, N_HEADS * QK_NOPE), BF16)],
        compiler_params=_cp("parallel"),
    )(*args)


def _attn_prep_bwd(dq, dknv, dkr, dh, h, hn, hk, cq_pre, cq, kvpre, ckv, g_attn, g_kvin, w_dq, g_ql, w_uq, w_kv, g_kvl,
                   w_ukv, cos, sin):
    t, d = h.shape
    tr = 256
    n_steps = t // tr
    wq = N_HEADS * HEAD_PAD
    wk = N_HEADS * QK_NOPE

    def body(dq_ref, dknv_ref, dkr_ref, dh_ref, h_ref, hn_ref, hk_ref, cqp_ref, cq_ref, kvp_ref, ckv_ref,
             ga_ref, gk_ref, wdq_ref, gq_ref, wuq_ref, wkv_ref, gl_ref, wukv_ref, c_ref, s_ref,
             dho_ref, dhb_ref, dwuq_ref, dwdq_ref, dwukv_ref, dwkv_ref, dga_ref, dgk_ref, dgq_ref, dgl_ref,
             a_uq, a_dq, a_ukv, a_kv):
        i = pl.program_id(0)

        @pl.when(i == 0)
        def _():
            for ref in (a_uq, a_dq, a_ukv, a_kv, dga_ref, dgk_ref, dgq_ref, dgl_ref):
                ref[...] = jnp.zeros_like(ref)

        dqv = dq_ref[...]
        dcq = lax.dot_general(dqv, wuq_ref[...], NT_DIMS, preferred_element_type=F32)
        a_uq[...] += lax.dot_general(cq_ref[...], dqv, TN_DIMS, preferred_element_type=F32)
        dcq_pre, dg = _rms_bwd_math(cqp_ref[...], gq_ref[...], dcq)
        dgq_ref[...] += dg
        dcq_pre = dcq_pre.astype(BF16)
        dhn = lax.dot_general(dcq_pre, wdq_ref[...], NT_DIMS, preferred_element_type=F32)
        a_dq[...] += lax.dot_general(hn_ref[...], dcq_pre, TN_DIMS, preferred_element_type=F32)
        dckv = None
        for p in range(2):
            dk = dknv_ref[p].astype(BF16)
            part = lax.dot_general(dk, wukv_ref[p], NT_DIMS, preferred_element_type=F32)
            dckv = part if dckv is None else dckv + part
            a_ukv[p] += lax.dot_general(ckv_ref[...], dk, TN_DIMS, preferred_element_type=F32)
        dlat, dg = _rms_bwd_math(kvp_ref[:, :KV_LORA], gl_ref[...], dckv)
        dgl_ref[...] += dg
        dkr_pre = _rope_bwd_math(dkr_ref[...], c_ref[...], s_ref[...])
        dkvpre = jnp.concatenate([dlat, dkr_pre], axis=1).astype(BF16)
        dhk = lax.dot_general(dkvpre, wkv_ref[...], NT_DIMS, preferred_element_type=F32)
        a_kv[...] += lax.dot_general(hk_ref[...], dkvpre, TN_DIMS, preferred_element_type=F32)
        xv = h_ref[...]
        dx1, dg = _rms_bwd_math(xv, ga_ref[...], dhn)
        dga_ref[...] += dg
        dx2, dg = _rms_bwd_math(xv, gk_ref[...], dhk)
        dgk_ref[...] += dg
        dh_new = dh_ref[...] + dx1 + dx2
        dho_ref[...] = dh_new
        dhb_ref[...] = dh_new.astype(BF16)

        @pl.when(i == n_steps - 1)
        def _():
            dwuq_ref[...] = a_uq[...].astype(BF16)
            dwdq_ref[...] = a_dq[...].astype(BF16)
            dwukv_ref[...] = a_ukv[...].astype(BF16)
            dwkv_ref[...] = a_kv[...].astype(BF16)

    rows = lambda w: pl.BlockSpec((tr, w), lambda i: (i, 0))
    whole = lambda shape: pl.BlockSpec(shape, lambda i: (0,) * len(shape))
    weights = (g_attn, g_kvin, w_dq, g_ql, w_uq, w_kv, g_kvl, w_ukv)
    dw_shapes = [(Q_LORA, wq), (d, Q_LORA), (2, KV_LORA, wk), (d, KVP)]
    dg_shapes = [(1, d), (1, d), (1, Q_LORA), (1, KV_LORA)]
    return _tc_call(
        body, name="attn_prep_bwd", grid=(n_steps,),
        in_specs=[rows(wq), pl.BlockSpec((2, tr, wk), lambda i: (0, i, 0)), rows(LANES), rows(d), rows(d), rows(d),
                  rows(d), rows(Q_LORA), rows(Q_LORA), rows(KVP), rows(KV_LORA)]
        + [whole(a.shape) for a in weights] + [rows(LANES), rows(LANES)],
        out_specs=[rows(d), rows(d)] + [whole(s) for s in dw_shapes + dg_shapes],
        out_shape=[jax.ShapeDtypeStruct((t, d), F32), jax.ShapeDtypeStruct((t, d), BF16)]
        + [jax.ShapeDtypeStruct(s, BF16) for s in dw_shapes] + [jax.ShapeDtypeStruct(s, F32) for s in dg_shapes],
        scratch_shapes=[pltpu.VMEM(s, F32) for s in dw_shapes], compiler_params=_cp("arbitrary"),
    )(dq, dknv, dkr, dh, h, hn, hk, cq_pre, cq, kvpre, ckv, *weights, cos, sin)


ROW_CHUNK = 64
HALO = 16
WIN = ROW_CHUNK + 16
LANE_HALVES = (slice(0, LANES), slice(LANES, TC))


def _stage(s_ref, p, src):
    t = src.shape[0]
    s_ref[p, :HALO] = jnp.zeros((HALO, TC), BF16)
    s_ref[p, HALO:HALO + t] = src
    s_ref[p, HALO + t:] = jnp.zeros((HALO, TC), BF16)


def _window(s_ref, p, i, lanes):
    base = pl.multiple_of(i * ROW_CHUNK, ROW_CHUNK)
    return s_ref[p, pl.ds(base, ROW_CHUNK + 2 * HALO), lanes].astype(F32)[8:8 + WIN]


def _valid(x):
    return x[8:8 + ROW_CHUNK]


def _prev(x, k):
    return pltpu.roll(x, k, axis=0)


def _next(x, k):
    return pltpu.roll(x, WIN - k, axis=0)


def _taps(w_ref, lanes):
    return w_ref[0:1, lanes], w_ref[1:2, lanes], w_ref[2:3, lanes]


def _fold8(x):
    return jnp.sum(x.reshape(ROW_CHUNK // 8, 8, x.shape[-1]), axis=0)


def _store_rows(ref, idx, i, lanes, x):
    rows = pl.ds(pl.multiple_of(i * ROW_CHUNK, ROW_CHUNK), ROW_CHUNK)
    ref[(*idx, rows, lanes)] = x.astype(ref.dtype)


def _for_chunks(t, chunk):
    def step(i, carry):
        for lanes in LANE_HALVES:
            chunk(i, lanes)
        return carry

    lax.fori_loop(0, t // ROW_CHUNK, step, 0)


def _write_col_sums(acc_ref, outs):
    for k, (ref, row) in enumerate(outs):
        ref[row:row + 1, :] = jnp.sum(acc_ref[k], axis=0, keepdims=True)


def _shift_down(x, k):
    row = lax.broadcasted_iota(jnp.int32, x.shape, 0)
    return jnp.where(row >= k, pltpu.roll(x, k, axis=0), 0.0)


def _shift_up(x, k):
    n = x.shape[0]
    row = lax.broadcasted_iota(jnp.int32, x.shape, 0)
    return jnp.where(row < n - k, pltpu.roll(x, n - k, axis=0), 0.0)


def _conv3(x, w_ref):
    return _shift_down(x, 2) * w_ref[0:1, :] + _shift_down(x, 1) * w_ref[1:2, :] + x * w_ref[2:3, :]


def _col(parts, t):
    if parts is None:
        return pl.BlockSpec((t, TC), lambda j: (0, j))
    return pl.BlockSpec((parts, t, TC), lambda j: (0, 0, j))


def _staging(parts, t):
    return pltpu.VMEM((parts, t + 2 * HALO, TC), BF16)


def _scmix_fwd(z, w):
    t = z.shape[1]

    def body(z_ref, w_ref, m_ref):
        b, c, u = (z_ref[p].astype(F32) for p in range(3))
        m_ref[...] = (b * _conv3(c * u, w_ref)).astype(BF16)

    return _tc_call(
        body, name="scmix_fwd", grid=(D // TC,), in_specs=[_col(3, t), pl.BlockSpec((3, TC), lambda j: (0, j))],
        out_specs=_col(None, t), out_shape=jax.ShapeDtypeStruct((t, D), BF16), compiler_params=_cp("parallel"),
    )(z, w)


def _scmix_bwd(z, w, dm):
    t = z.shape[1]

    def body(z_ref, w_ref, dm_ref, dz_ref, dw_ref, s_ref, acc_ref):
        for p in range(3):
            _stage(s_ref, p, z_ref[p])
        _stage(s_ref, 3, dm_ref[...])
        acc_ref[...] = jnp.zeros_like(acc_ref)

        def chunk(i, lanes):
            w0, w1, w2 = _taps(w_ref, lanes)
            b, c, u, dm = (_window(s_ref, p, i, lanes) for p in range(4))
            cu = c * u
            cu1, cu2 = _prev(cu, 1), _prev(cu, 2)
            _store_rows(dz_ref, (0,), i, lanes, _valid(dm * (cu2 * w0 + cu1 * w1 + cu * w2)))
            dcv = dm * b
            dcu = dcv * w2 + _next(dcv, 1) * w1 + _next(dcv, 2) * w0
            _store_rows(dz_ref, (1,), i, lanes, _valid(dcu * u))
            _store_rows(dz_ref, (2,), i, lanes, _valid(dcu * c))
            for k, shifted in enumerate((cu2, cu1, cu)):
                acc_ref[k, :, lanes] += _fold8(_valid(dcv * shifted))

        _for_chunks(t, chunk)
        _write_col_sums(acc_ref, [(dw_ref, 0), (dw_ref, 1), (dw_ref, 2)])

    wspec = pl.BlockSpec((3, TC), lambda j: (0, j))
    return _tc_call(
        body, name="scmix_bwd", grid=(D // TC,), in_specs=[_col(3, t), wspec, _col(None, t)],
        out_specs=[_col(3, t), wspec],
        out_shape=[jax.ShapeDtypeStruct((3, t, D), BF16), jax.ShapeDtypeStruct((3, D), F32)],
        scratch_shapes=[_staging(4, t), pltpu.VMEM((3, 8, TC), F32)], compiler_params=_cp("parallel"),
    )(z, w, dm)


def _ffn_up_gate(hf, w_up, w, bias, name):
    t, d = hf.shape
    nb = F_FF // TC

    def body(hf_ref, wg_ref, wv_ref, w_ref, b_ref, up_ref, a_ref, prev_ref):
        @pl.when(pl.program_id(0) == 0)
        def _():
            prev_ref[...] = jnp.zeros_like(prev_ref)

        gc = _conv3(prev_ref[0].astype(F32), w_ref) + b_ref[...]
        a_ref[...] = (gc * jax.nn.sigmoid(gc) * prev_ref[1].astype(F32)).astype(BF16)
        hv = hf_ref[...]
        up_ref[0] = jnp.dot(hv, wg_ref[...], preferred_element_type=F32).astype(BF16)
        up_ref[1] = jnp.dot(hv, wv_ref[...], preferred_element_type=F32).astype(BF16)
        prev_ref[...] = up_ref[...]

    tile = lambda j: jnp.minimum(j, nb - 1)
    gated = lambda j: jnp.maximum(j - 1, 0)
    return _tc_call(
        body, name=name, grid=(nb + 1,),
        in_specs=[pl.BlockSpec((t, d), lambda j: (0, 0)), pl.BlockSpec((d, TC), lambda j: (0, tile(j))),
                  pl.BlockSpec((d, TC), lambda j: (0, nb + tile(j))), pl.BlockSpec((3, TC), lambda j: (0, gated(j))),
                  pl.BlockSpec((1, TC), lambda j: (0, gated(j)))],
        out_specs=[pl.BlockSpec((2, t, TC), lambda j: (0, 0, tile(j))), pl.BlockSpec((t, TC), lambda j: (0, gated(j)))],
        out_shape=[jax.ShapeDtypeStruct((2, t, F_FF), BF16), jax.ShapeDtypeStruct((t, F_FF), BF16)],
        scratch_shapes=[pltpu.VMEM((2, t, TC), BF16)], compiler_params=_cp("arbitrary"),
    )(hf, w_up, w_up, w, bias)


def _gate_bwd(up, w, bias, da, name):
    t = up.shape[1]

    def body(u_ref, w_ref, b_ref, da_ref, du_ref, dw_ref, db_ref, s_ref, acc_ref):
        for p in range(2):
            _stage(s_ref, p, u_ref[p])
        _stage(s_ref, 2, da_ref[...])
        acc_ref[...] = jnp.zeros_like(acc_ref)

        def chunk(i, lanes):
            w0, w1, w2 = _taps(w_ref, lanes)
            g, v, da = (_window(s_ref, p, i, lanes) for p in range(3))
            g1, g2 = _prev(g, 1), _prev(g, 2)
            gc = g2 * w0 + g1 * w1 + g * w2 + b_ref[:, lanes]
            sg = jax.nn.sigmoid(gc)
            _store_rows(du_ref, (1,), i, lanes, _valid(da * (gc * sg)))
            dgc = da * v * (sg * (1.0 + gc * (1.0 - sg)))
            _store_rows(du_ref, (0,), i, lanes, _valid(dgc * w2 + _next(dgc, 1) * w1 + _next(dgc, 2) * w0))
            for k, shifted in enumerate((g2, g1, g)):
                acc_ref[k, :, lanes] += _fold8(_valid(dgc * shifted))
            acc_ref[3, :, lanes] += _fold8(_valid(dgc))

        _for_chunks(t, chunk)
        _write_col_sums(acc_ref, [(dw_ref, 0), (dw_ref, 1), (dw_ref, 2), (db_ref, 0)])

    wspec = pl.BlockSpec((3, TC), lambda j: (0, j))
    bspec = pl.BlockSpec((1, TC), lambda j: (0, j))
    return _tc_call(
        body, name=name, grid=(F_FF // TC,), in_specs=[_col(2, t), wspec, bspec, _col(None, t)],
        out_specs=[_col(2, t), wspec, bspec],
        out_shape=[jax.ShapeDtypeStruct((2, t, F_FF), BF16), jax.ShapeDtypeStruct((3, F_FF), F32),
                   jax.ShapeDtypeStruct((1, F_FF), F32)],
        scratch_shapes=[_staging(3, t), pltpu.VMEM((4, 8, TC), F32)], compiler_params=_cp("parallel"),
    )(up, w, bias, da)


ATT_TQ = 256
ATT_SCALE = (QK_NOPE + QK_ROPE) ** -0.5


def _key_ranges(lvl):
    lo = lvl * ATT_TQ
    return ([(0, lo, False)] if lvl else []) + [(lo, lo + ATT_TQ, True)]


HEADS_PER_STEP = 4


def _fill_keys(k_ref, kn_ref, kr_ref):
    @pl.when(pl.program_id(1) == 0)
    def _():
        for hh in range(HEADS_PER_STEP):
            k_ref[hh, :, :QK_NOPE] = kn_ref[:, hh * QK_NOPE:(hh + 1) * QK_NOPE]
            k_ref[hh, :, QK_NOPE:] = kr_ref[...]


def _attn_probs(q, k_ref, lvl):
    scores = []
    for lo, hi, diagonal in _key_ranges(lvl):
        s = lax.dot_general(q, k_ref[lo:hi, :], NT_DIMS, preferred_element_type=F32) * ATT_SCALE
        if diagonal:
            row = lax.broadcasted_iota(jnp.int32, s.shape, 0)
            col = lax.broadcasted_iota(jnp.int32, s.shape, 1)
            seen = lax.shift_right_logical(col, CHUNK_SHIFT) <= lax.shift_right_logical(row, CHUNK_SHIFT)
            s = jnp.where(seen, s, NEG_INF)
        scores.append(s)
    m = jnp.max(scores[0], axis=1, keepdims=True)
    for s in scores[1:]:
        m = jnp.maximum(m, jnp.max(s, axis=1, keepdims=True))
    ps = [jnp.exp(s - m) for s in scores]
    total = jnp.sum(ps[0], axis=1, keepdims=True)
    for p in ps[1:]:
        total = total + jnp.sum(p, axis=1, keepdims=True)
    inv = 1.0 / total
    return [p * inv for p in ps]


def _per_query_block(qi, n_blocks, branch):
    for lvl in range(n_blocks):
        pl.when(qi == lvl)(lambda lvl=lvl: branch(lvl))


def _attn_specs(t):
    g = HEADS_PER_STEP
    q = pl.BlockSpec((ATT_TQ, g * HEAD_PAD), lambda h, i: (i, h))
    kn = pl.BlockSpec((None, t, g * QK_NOPE), lambda h, i: (0, 0, h))
    kr = pl.BlockSpec((t, LANES), lambda h, i: (0, 0))
    v = pl.BlockSpec((None, t, g * V_HEAD), lambda h, i: (1, 0, h))
    o = pl.BlockSpec((ATT_TQ, g * V_HEAD), lambda h, i: (i, h))
    return q, kn, kr, v, o


def _attn_fwd(q, knv, kr):
    t = q.shape[0]

    def body(q_ref, kn_ref, kr_ref, v_ref, o_ref, k_ref):
        _fill_keys(k_ref, kn_ref, kr_ref)

        def branch(lvl):
            for hh in range(HEADS_PER_STEP):
                vcols = slice(hh * V_HEAD, (hh + 1) * V_HEAD)
                ps = _attn_probs(q_ref[:, hh * HEAD_PAD:(hh + 1) * HEAD_PAD], k_ref.at[hh], lvl)
                o = None
                for p, (lo, hi, _) in zip(ps, _key_ranges(lvl)):
                    part = jnp.dot(p.astype(BF16), v_ref[lo:hi, vcols], preferred_element_type=F32)
                    o = part if o is None else o + part
                o_ref[:, vcols] = o.astype(BF16)

        _per_query_block(pl.program_id(1), t // ATT_TQ, branch)

    qs, kns, krs, vs, os_ = _attn_specs(t)
    return _tc_call(
        body, name="attn_fwd", grid=(N_HEADS // HEADS_PER_STEP, t // ATT_TQ), in_specs=[qs, kns, krs, vs],
        out_specs=os_, out_shape=jax.ShapeDtypeStruct((t, N_HEADS * V_HEAD), BF16),
        scratch_shapes=[pltpu.VMEM((HEADS_PER_STEP, t, HEAD_PAD), BF16)], compiler_params=_cp("parallel", "arbitrary"),
    )(q, knv, kr, knv)


def _attn_bwd(q, knv, kr, do, cos, sin):
    t = q.shape[0]

    def body(q_ref, kn_ref, kr_ref, v_ref, do_ref, c_ref, s_ref, dq_ref, dknv_ref, dkr_ref, k_ref, dk_ref):
        h, qi = pl.program_id(0), pl.program_id(1)
        _fill_keys(k_ref, kn_ref, kr_ref)

        @pl.when(qi == 0)
        def _():
            dknv_ref[1] = jnp.zeros(dknv_ref.shape[1:], F32)
            dk_ref[...] = jnp.zeros_like(dk_ref)

        @pl.when((qi == 0) & (h == 0))
        def _():
            dkr_ref[...] = jnp.zeros_like(dkr_ref)

        def branch(lvl):
            ranges = _key_ranges(lvl)
            for hh in range(HEADS_PER_STEP):
                qcols = slice(hh * HEAD_PAD, (hh + 1) * HEAD_PAD)
                vcols = slice(hh * V_HEAD, (hh + 1) * V_HEAD)
                qv, dov = q_ref[:, qcols], do_ref[:, vcols]
                ps = _attn_probs(qv, k_ref.at[hh], lvl)
                dps = [lax.dot_general(dov, v_ref[lo:hi, vcols], NT_DIMS, preferred_element_type=F32)
                       for lo, hi, _ in ranges]
                di = None
                for p, dp in zip(ps, dps):
                    part = jnp.sum(p * dp, axis=1, keepdims=True)
                    di = part if di is None else di + part
                dq = None
                for p, dp, (lo, hi, _) in zip(ps, dps, ranges):
                    ds = (p * (dp - di) * ATT_SCALE).astype(BF16)
                    part = jnp.dot(ds, k_ref[hh, lo:hi, :], preferred_element_type=F32)
                    dq = part if dq is None else dq + part
                    dk_ref[hh, lo:hi, :] += lax.dot_general(ds, qv, TN_DIMS, preferred_element_type=F32)
                    dknv_ref[1, lo:hi, vcols] += lax.dot_general(p.astype(BF16), dov, TN_DIMS,
                                                                 preferred_element_type=F32)
                dq_ref[:, hh * HEAD_PAD:hh * HEAD_PAD + QK_NOPE] = dq[:, :QK_NOPE].astype(BF16)
                dq_ref[:, hh * HEAD_PAD + QK_NOPE:(hh + 1) * HEAD_PAD] = _rope_bwd_math(
                    dq[:, QK_NOPE:], c_ref[...], s_ref[...]).astype(BF16)

        _per_query_block(qi, t // ATT_TQ, branch)

        @pl.when(qi == t // ATT_TQ - 1)
        def _():
            for hh in range(HEADS_PER_STEP):
                dknv_ref[0, :, hh * QK_NOPE:(hh + 1) * QK_NOPE] = dk_ref[hh, :, :QK_NOPE]
                dkr_ref[...] += dk_ref[hh, :, QK_NOPE:]

    qs, kns, krs, vs, os_ = _attn_specs(t)
    tab = pl.BlockSpec((ATT_TQ, LANES), lambda h, i: (i, 0))
    return _tc_call(
        body, name="attn_bwd", grid=(N_HEADS // HEADS_PER_STEP, t // ATT_TQ), in_specs=[qs, kns, krs, vs, os_, tab, tab],
        out_specs=[qs, pl.BlockSpec((2, t, HEADS_PER_STEP * QK_NOPE), lambda h, i: (0, 0, h)), krs],
        out_shape=[jax.ShapeDtypeStruct((t, N_HEADS * HEAD_PAD), BF16),
                   jax.ShapeDtypeStruct((2, t, N_HEADS * QK_NOPE), F32), jax.ShapeDtypeStruct((t, LANES), F32)],
        scratch_shapes=[pltpu.VMEM((HEADS_PER_STEP, t, HEAD_PAD), BF16), pltpu.VMEM((HEADS_PER_STEP, t, HEAD_PAD), F32)],
        compiler_params=_cp("arbitrary", "arbitrary"),
    )(q, knv, kr, knv, do, cos, sin)


def _adam_math(w, g, m, v):
    nm = ADAM_B1 * m + (1.0 - ADAM_B1) * g
    nv = ADAM_B2 * v + (1.0 - ADAM_B2) * (g * g)
    m_hat = nm / (1.0 - ADAM_B1 ** ADAM_STEP)
    v_hat = nv / (1.0 - ADAM_B2 ** ADAM_STEP)
    return -ADAM_LR * (m_hat / (jnp.sqrt(v_hat) + ADAM_EPS) + ADAM_WD * w), nm, nv


def _adamw_small(w, g, m, v):
    def body(w_ref, g_ref, m_ref, v_ref, d_ref, nm_ref, nv_ref):
        d_ref[...], nm_ref[...], nv_ref[...] = _adam_math(w_ref[...], g_ref[...], m_ref[...], v_ref[...])

    shp = jax.ShapeDtypeStruct(w.shape, F32)
    return _tc_call(body, name="adamw_small", out_shape=[shp] * 3)(w, g, m, v)


ADAM_SPLIT = 4


def _adamw_shards(ids, items, name):
    n = len(items)

    def body(ids_ref, *refs):
        outs = refs[len(refs) - 4 * n:]
        mine = pl.program_id(0) == ids_ref[0]
        for i in range(n):
            w_ref, m_ref, v_ref, gm_ref, gs_ref = refs[5 * i:5 * i + 5]
            g_ref, d_ref, nm_ref, nv_ref = outs[4 * i:4 * i + 4]

            @pl.when(mine)
            def _(g_ref=g_ref, gm_ref=gm_ref):
                g_ref[...] = gm_ref[...]

            @pl.when(jnp.logical_not(mine))
            def _(g_ref=g_ref, gs_ref=gs_ref):
                g_ref[...] = gs_ref[...]

            d_ref[...], nm_ref[...], nv_ref[...] = _adam_math(w_ref[...], g_ref[...], m_ref[...], v_ref[...])

    in_specs, out_specs, out_shape, args, carried, aliases = [], [], [], [ids], [], {}
    for i, it in enumerate(items):
        w = it["w"]
        r, c = w.shape[-2:]
        tr = r // 2 // ADAM_SPLIT
        assert tr % 8 == 0, (name, w.shape)
        layer = it.get("layer")
        if layer is None:
            wspec = pl.BlockSpec((tr, c), lambda h, k, ids: (h * ADAM_SPLIT + k, 0))
        else:
            wspec = pl.BlockSpec((None, tr, c), lambda h, k, ids, layer=layer: (layer, h * ADAM_SPLIT + k, 0))
        gspec = pl.BlockSpec((tr, c), lambda h, k, ids: (k, 0))
        in_specs += [wspec] * 3 + [gspec] * 2
        args += [w, it["m"], it["v"], it["g_mine"], it["g_sib"]]
        out_specs += [wspec] * 4
        out_shape += [jax.ShapeDtypeStruct(w.shape, F32)] * 4
        if it.get("prev") is not None:
            for k, p in enumerate(it["prev"]):
                aliases[1 + 5 * n + len(carried)] = 4 * i + k
                carried.append(p)
    res = _tc_call(
        body, name=name, prefetch=1, grid=(2, ADAM_SPLIT), in_specs=in_specs + [ANY] * len(carried),
        out_specs=out_specs, out_shape=out_shape, input_output_aliases=aliases,
        compiler_params=_cp("parallel", "parallel"),
    )(*args, *carried)
    return [res[4 * i:4 * i + 4] for i in range(n)]


def _peer_chip(k_me, j):
    return k_me ^ jnp.where(j == 0, 2, jnp.where(j == 1, 1, 3))


def _pair_sums(ids, gs, ras, name):
    n = len(gs)

    def body(ids_ref, *refs):
        for i in range(n):
            g_ref, ra_ref, o_ref = refs[2 * i], refs[2 * i + 1], refs[2 * n + i]
            o_ref[...] = (g_ref[...].astype(F32) + ra_ref[...].astype(F32)).astype(BF16)

    in_specs, out_specs, out_shape = [], [], []
    for g in gs:
        half, c = g.shape[1] // 2, g.shape[2]
        in_specs += [pl.BlockSpec((None, half, c), lambda j, ids: (_peer_chip(ids[1], j), ids[0], 0)),
                     pl.BlockSpec((None, half, c), lambda j, ids: (_peer_chip(ids[1], j), 0, 0))]
        out_specs.append(pl.BlockSpec((None, half, c), lambda j, ids: (j, 0, 0)))
        out_shape.append(jax.ShapeDtypeStruct((3, half, c), BF16))
    return _tc_call(
        body, name=name, prefetch=1, grid=(3,), in_specs=in_specs, out_specs=out_specs, out_shape=out_shape,
        compiler_params=_cp("parallel"),
    )(ids, *[a for pair in zip(gs, ras) for a in pair])


def _chip_sums(ids, gs, ras, rbs, name):
    n = len(gs)

    def body(ids_ref, *refs):
        for i in range(n):
            g_ref, ra_ref, rb_ref, o_ref = refs[3 * i], refs[3 * i + 1], refs[3 * i + 2], refs[3 * n + i]
            acc = g_ref[...].astype(F32) + ra_ref[...].astype(F32)
            for j in range(3):
                acc = acc + rb_ref[j].astype(F32)
            o_ref[...] = acc

    in_specs, out_specs, out_shape = [], [], []
    for g in gs:
        half, c = g.shape[1] // 2, g.shape[2]
        in_specs += [pl.BlockSpec((None, half, c), lambda i, ids: (ids[1], ids[0], 0)),
                     pl.BlockSpec((None, half, c), lambda i, ids: (ids[1], 0, 0)),
                     pl.BlockSpec((3, half, c), lambda i, ids: (0, 0, 0))]
        out_specs.append(pl.BlockSpec((half, c), lambda i, ids: (0, 0)))
        out_shape.append(jax.ShapeDtypeStruct((half, c), F32))
    return _tc_call(
        body, name=name, prefetch=1, grid=(1,), in_specs=in_specs, out_specs=out_specs, out_shape=out_shape,
        compiler_params=_cp("arbitrary"),
    )(ids, *[a for trio in zip(gs, ras, rbs) for a in trio])


def _position():
    x, y, c = lax.axis_index("x"), lax.axis_index("y"), lax.axis_index("c")
    chips = [(1 - x, y), (x, 1 - y), (1 - x, 1 - y)]
    return x, y, c, chips


def _shard_half(ref, wm, h):
    if wm.kind == "tiny":
        return ref
    if wm.nl == 2:
        return ref.at[h]
    return ref.at[pl.ds(pl.multiple_of(h * (wm.k // 2), 16), wm.k // 2), :]


def _region(full, wm, s, h):
    if wm.kind == "tiny":
        return full.at[s]
    cols = pl.ds(pl.multiple_of(s * wm.n, LANES), wm.n) if wm.kind == "col" else slice(None)
    if wm.nl == 2:
        rows = pl.ds(pl.multiple_of(s * wm.k, 16), wm.k) if wm.kind == "row" else slice(None)
        return full.at[slice(None) if h is None else h, rows, cols]
    if wm.kind == "col":
        rows = slice(None) if h is None else pl.ds(pl.multiple_of(h * (wm.k // 2), 16), wm.k // 2)
    elif h is None:
        rows = pl.ds(pl.multiple_of(s * wm.k, 16), wm.k)
    else:
        rows = pl.ds(pl.multiple_of(s * wm.k + h * (wm.k // 2), 16), wm.k // 2)
    return full.at[rows, cols]


def _full_shape(wm):
    if wm.kind == "tiny":
        return (N_CHIPS, wm.k, wm.n)
    shape = (wm.k, N_CHIPS * wm.n) if wm.kind == "col" else (N_CHIPS * wm.k, wm.n)
    return shape if wm.nl == 1 else (wm.nl,) + shape


def _handshake(peers):
    barrier = pltpu.get_barrier_semaphore()
    for peer in peers:
        pl.semaphore_signal(barrier, inc=1, device_id=peer, device_id_type=MESH)
    pl.semaphore_wait(barrier, len(peers))


def _all_gather_group(gi, shards):
    wms = AG_GROUPS[gi]
    nw = len(wms)

    def body(*refs):
        sh, full = refs[:nw], refs[nw:2 * nw]
        ici_s, ici_r, pass_s, pass_r, own_s, own_r = refs[2 * nw:]
        x, y, c, chips = _position()
        me, sibling = 2 * x + y, (x, y, 1 - c)
        _handshake([(*chip, c) for chip in chips] + [sibling])

        def rcopy(src, dst, s_sem, r_sem, to):
            return pltpu.make_async_remote_copy(src_ref=src, dst_ref=dst, send_sem=s_sem, recv_sem=r_sem,
                                                device_id=to, device_id_type=MESH)

        started = []
        for i, wm in enumerate(wms):
            for j, chip in enumerate(chips):
                started.append(rcopy(_shard_half(sh[i], wm, c), _region(full[i], wm, me, c),
                                     ici_s.at[i, j], ici_r.at[i, j], (*chip, c)))
                started[-1].start()
            started.append(rcopy(sh[i], _region(full[i], wm, me, None), own_s.at[i], own_r.at[i], sibling))
            started[-1].start()
        for i, wm in enumerate(wms):
            for j, chip in enumerate(chips):
                got = _region(full[i], wm, 2 * chip[0] + chip[1], c)
                rcopy(got, got, ici_s.at[i, j], ici_r.at[i, j], sibling).wait_recv()
                if wm.kind != "tiny":
                    started.append(rcopy(got, got, pass_s.at[i, j], pass_r.at[i, j], sibling))
                    started[-1].start()
        for i, wm in enumerate(wms):
            mine = _region(full[i], wm, me, None)
            rcopy(mine, mine, own_s.at[i], own_r.at[i], sibling).wait_recv()
            for j, chip in enumerate(chips):
                if wm.kind != "tiny":
                    got = _region(full[i], wm, 2 * chip[0] + chip[1], 1 - c)
                    rcopy(got, got, pass_s.at[i, j], pass_r.at[i, j], sibling).wait_recv()
        for cp in started:
            cp.wait_send()

    return pl.kernel(
        body, out_type=[jax.ShapeDtypeStruct(_full_shape(wm), s.dtype) for wm, s in zip(wms, shards)],
        mesh=plsc.ScalarSubcoreMesh(axis_name="sequencer", num_cores=1), name=f"ag_group{gi}",
        scratch_types=[pltpu.SemaphoreType.DMA((nw, 3))] * 4 + [pltpu.SemaphoreType.DMA((nw,))] * 2,
        compiler_params=pltpu.CompilerParams(collective_id=gi),
    )(*shards)


def _sequencer_call(body, name, cid, out_types, scratch, args):
    return pl.kernel(
        body, out_type=out_types, mesh=plsc.ScalarSubcoreMesh(axis_name="sequencer", num_cores=1), name=name,
        scratch_types=scratch, compiler_params=pltpu.CompilerParams(collective_id=cid),
    )(*args)


def _pair_exchange(gs, tag, cid):
    n = len(gs)

    def body(*refs):
        g, out, send_sems, recv_sems = refs[:n], refs[n:2 * n], refs[2 * n], refs[2 * n + 1]
        x, y, c, _ = _position()
        _handshake([(x, y, 1 - c)])
        cps = []
        for i in range(n):
            half = g[i].shape[1] // 2
            cps.append(pltpu.make_async_remote_copy(
                src_ref=g[i].at[:, pl.ds(pl.multiple_of((1 - c) * half, 16), half), :], dst_ref=out[i],
                send_sem=send_sems.at[i], recv_sem=recv_sems.at[i], device_id=(x, y, 1 - c), device_id_type=MESH))
            cps[-1].start()
        for cp in cps:
            cp.wait()

    return _sequencer_call(
        body, f"rs_pair_exchange{tag}", cid,
        [jax.ShapeDtypeStruct((a.shape[0], a.shape[1] // 2, a.shape[2]), a.dtype) for a in gs],
        [pltpu.SemaphoreType.DMA((n,)), pltpu.SemaphoreType.DMA((n,))], gs)


def _chip_exchange(ss, tag, cid):
    n = len(ss)

    def body(*refs):
        s, out, send_sems, recv_sems = refs[:n], refs[n:2 * n], refs[2 * n], refs[2 * n + 1]
        x, y, c, chips = _position()
        _handshake([(*chip, c) for chip in chips])
        cps = []
        for i in range(n):
            for j, chip in enumerate(chips):
                cps.append(pltpu.make_async_remote_copy(
                    src_ref=s[i].at[j], dst_ref=out[i].at[j], send_sem=send_sems.at[i, j], recv_sem=recv_sems.at[i, j],
                    device_id=(*chip, c), device_id_type=MESH))
                cps[-1].start()
        for cp in cps:
            cp.wait()

    return _sequencer_call(
        body, f"rs_chip_exchange{tag}", cid, [jax.ShapeDtypeStruct(a.shape, a.dtype) for a in ss],
        [pltpu.SemaphoreType.DMA((n, 3)), pltpu.SemaphoreType.DMA((n, 3))], ss)


def _pair_swap(g8s, tag, cid):
    n = len(g8s)

    def body(*refs):
        g, out, send_sems, recv_sems = refs[:n], refs[n:2 * n], refs[2 * n], refs[2 * n + 1]
        x, y, c, _ = _position()
        _handshake([(x, y, 1 - c)])
        cps = []
        for i in range(n):
            cps.append(pltpu.make_async_remote_copy(
                src_ref=g[i], dst_ref=out[i], send_sem=send_sems.at[i], recv_sem=recv_sems.at[i],
                device_id=(x, y, 1 - c), device_id_type=MESH))
            cps[-1].start()
        for cp in cps:
            cp.wait()

    return _sequencer_call(
        body, f"rs_pair_swap{tag}", cid, [jax.ShapeDtypeStruct(a.shape, a.dtype) for a in g8s],
        [pltpu.SemaphoreType.DMA((n,)), pltpu.SemaphoreType.DMA((n,))], g8s)


def _all_reduce_small(vec, name):
    r, cols = vec.shape

    def body(v_ref, o_ref, gath, send_sems, recv_sems):
        x, y, c, _ = _position()
        me = 4 * x + 2 * y + c
        gath[me] = v_ref[...]
        cps = []
        for rel in range(1, N_DEV):
            peer = (x ^ (rel >> 2), y ^ ((rel >> 1) & 1), c ^ (rel & 1))
            cps.append(pltpu.make_async_remote_copy(
                src_ref=v_ref, dst_ref=gath.at[me], send_sem=send_sems.at[rel - 1], recv_sem=recv_sems.at[rel - 1],
                device_id=peer, device_id_type=MESH))
        for cp in cps:
            cp.start()
        for rel in range(1, N_DEV):
            pltpu.make_async_remote_copy(
                src_ref=v_ref, dst_ref=gath.at[me ^ rel], send_sem=send_sems.at[rel - 1],
                recv_sem=recv_sems.at[rel - 1], device_id=(x, y, c), device_id_type=MESH).wait_recv()
        for cp in cps:
            cp.wait_send()
        acc = gath[0]
        for d in range(1, N_DEV):
            acc = acc + gath[d]
        o_ref[...] = acc

    vm = pl.BlockSpec(memory_space=pltpu.VMEM)
    return _tc_call(
        body, name=name, in_specs=[vm], out_specs=vm, out_shape=jax.ShapeDtypeStruct((r, cols), F32),
        scratch_shapes=[pltpu.VMEM((N_DEV, r, cols), F32), pltpu.SemaphoreType.DMA((N_DEV - 1,)),
                        pltpu.SemaphoreType.DMA((N_DEV - 1,))],
    )(vec)


def _rope_tables(positions):
    half = QK_ROPE // 2
    inv_freq = 1.0 / (ROPE_THETA ** (jnp.arange(half, dtype=F32) / half))
    ang = positions.astype(F32)[:, None] * inv_freq
    zeros = jnp.zeros((positions.shape[0], LANES - QK_ROPE), F32)
    cos, sin = jnp.cos(ang), jnp.sin(ang)
    return jnp.concatenate([cos, cos, zeros], axis=1), jnp.concatenate([sin, sin, zeros], axis=1)


def _local_step(x, positions, tgt, wf, small, rs):
    cos, sin = _rope_tables(positions)
    w_in, w_out = wf["sc_w_in"], wf["sc_w_out"]
    w_ups, w_downs = (wf["ffn_w_up0"], wf["ffn_w_up1"]), (wf["ffn_w_down0"], wf["ffn_w_down1"])
    w_kv, w_ukv, w_dq, w_uq, w_o = wf["w_kv"], wf["w_ukv"], wf["w_dq"], wf["w_uq"], wf["w_o"]
    attn_norm, ffn_norm = small["attn_norm"], small["ffn_norm"]
    conv_b = small["ffn_conv_b"]

    def ffn_fwd(h, hf, l, then):
        up, a = _ffn_up_gate(hf, w_ups[l], small["ffn_conv_w"][l], conv_b[l:l + 1], f"ffn{l}_up_gate")
        return then(a, w_downs[l], h), (hf, up, a)

    def ffn_bwd(h, dh_out, dh_out_b, l, saved, gi, hooks):
        run = lambda stage: hooks.get(stage, lambda: None)()
        hf, up, a = saved
        da = _nt(f"ffn{l}_down_dx", dh_out_b, w_downs[l], BF16)
        run("down_dx")
        d_down = _tn(f"ffn{l}_down_dw", a, dh_out_b, BF16)
        dup, d_cw, d_cb = _gate_bwd(up, small["ffn_conv_w"][l], conv_b[l:l + 1], da, f"ffn{l}_gate_bwd")
        run("gate_bwd")
        d_up = _dw_ffn_up(f"ffn{l}_up_dw", hf, dup)
        rs.start(gi, {f"ffn_w_down{l}": d_down.reshape(N_CHIPS, F_FF // N_CHIPS, D), f"ffn_w_up{l}": d_up})
        dh, dh_b, d_norm = _dx_norm_bwd(f"ffn{l}_up_dx", dup, w_ups[l], h, ffn_norm[l:l + 1], dh_out)
        run("up_dx")
        return dh, dh_b, d_cw, d_cb, d_norm

    hn0 = _rms_fwd(x, attn_norm[0:1], "attn0_norm")
    z = _nn_parts("sc_in", hn0, w_in, 3, BF16)
    mix = _scmix_fwd(z, small["sc_conv_w"])
    h1, hf0 = _nn_add_norm("sc_out", mix, w_out, x, ffn_norm[0:1])
    h2, ffn0_saved = ffn_fwd(h1, hf0, 0, lambda a, w, h: _nn("ffn0_down", a, w, F32, add=h))

    hn1, hk, cq_pre, cq, q, kvpre, ckv, kr, knv = _attn_prep(
        h2, attn_norm[1:2], small["kv_in_norm"], w_dq, small["q_latent_norm"], w_uq, w_kv, small["kv_latent_norm"],
        w_ukv, cos, sin)
    o = _attn_fwd(q, knv, kr)
    h3, hf1 = _nn_add_norm("attn_out", o, w_o, h2, ffn_norm[1:2])
    (loss, dh4, dh4_b, d_final), ffn1_saved = ffn_fwd(
        h3, hf1, 1, lambda a, w, h: _nn_add_loss("ffn1_down_loss", a, w, h, small["final_norm"], tgt))

    rows = D // N_CHIPS
    dh3, dh3_b, d_cw1, d_cb1, d_fn1 = ffn_bwd(h3, dh4, dh4_b, 1, ffn1_saved, 0, {})

    do = _nt("attn_out_dx", dh3_b, w_o, BF16)
    d_wo = _tn("attn_out_dw", o, dh3_b, BF16)
    rs.pair_sums(0)
    dq, dknv, dkr = _attn_bwd(q, knv, kr, do, cos, sin)
    rs.chip_sums(0)
    dh2, dh2_b, d_wuq, d_wdq, d_wukv, d_wkv, d_an1, d_kvin, d_qln, d_kvln = _attn_prep_bwd(
        dq, dknv, dkr, dh3, h2, hn1, hk, cq_pre, cq, kvpre, ckv, attn_norm[1:2], small["kv_in_norm"], w_dq,
        small["q_latent_norm"], w_uq, w_kv, small["kv_latent_norm"], w_ukv, cos, sin)
    rs.finish(0)
    by_owner = lambda dw: dw.reshape(dw.shape[0], N_CHIPS, -1).transpose(1, 0, 2)
    rs.start(1, {
        "w_o": d_wo.reshape(N_CHIPS, rows, D), "w_uq": by_owner(d_wuq), "w_dq": d_wdq.reshape(N_CHIPS, rows, Q_LORA),
        "w_ukv": by_owner(d_wukv.reshape(2 * KV_LORA, -1)).reshape(N_CHIPS, 2 * KV_LORA, -1),
        "w_kv": d_wkv.reshape(N_CHIPS, rows, KVP),
    })

    dh1, dh1_b, d_cw0, d_cb0, d_fn0 = ffn_bwd(h1, dh2, dh2_b, 0, ffn0_saved, 2, {
        "down_dx": lambda: rs.pair_sums(1), "gate_bwd": lambda: rs.chip_sums(1), "up_dx": lambda: rs.finish(1)})
    rs.pair_sums(2)

    d_wout = _tn("sc_out_dw", mix, dh1_b, BF16)
    dmix = _nt("sc_out_dx", dh1_b, w_out, BF16)
    dz, d_scw = _scmix_bwd(z, small["sc_conv_w"], dmix)
    d_win = _dw_sc_in(hn0, dz)
    rs.start(3, {"sc_w_out": d_wout.reshape(N_CHIPS, rows, D), "sc_w_in": d_win})
    dx, _, d_an0 = _dx_norm_bwd("sc_in_dx", dz, w_in, x, attn_norm[0:1], dh1)

    small_g = {
        "attn_norm": jnp.concatenate([d_an0, d_an1]), "ffn_norm": jnp.concatenate([d_fn0, d_fn1]),
        "final_norm": d_final, "kv_in_norm": d_kvin, "kv_latent_norm": d_kvln, "q_latent_norm": d_qln,
        "ffn_conv_b": jnp.concatenate([d_cb0, d_cb1]), "sc_conv_w": d_scw, "ffn_conv_w": jnp.stack([d_cw0, d_cw1]),
    }
    return loss, dx, small_g


RS_GROUPS = (("ffn_w_down1", "ffn_w_up1"), ("w_o", "w_uq", "w_dq", "w_ukv", "w_kv"),
             ("ffn_w_down0", "ffn_w_up0"), ("sc_w_out", "sc_w_in"))


class _ReduceScatter:
    def __init__(self, ids, finish):
        self.ids, self.grads, self.step, self.mine, self.sib, self.finish = ids, {}, {}, {}, {}, finish

    def _cid(self, gi):
        return len(AG_GROUPS) + 3 * gi

    def start(self, gi, grads):
        self.grads.update(grads)
        own = [grads[n] for n in RS_GROUPS[gi]]
        self.step[gi] = (own, _pair_exchange(own, gi, self._cid(gi)))

    def pair_sums(self, gi):
        own, ra = self.step[gi]
        sums = _pair_sums(self.ids, own, ra, f"rs_pair_sums{gi}")
        self.step[gi] = (own, ra, _chip_exchange(sums, gi, self._cid(gi) + 1))

    def chip_sums(self, gi):
        own, ra, rb = self.step[gi]
        mine = _chip_sums(self.ids, own, ra, rb, f"rs_chip_sums{gi}")
        self.mine.update(zip(RS_GROUPS[gi], mine))
        self.sib.update(zip(RS_GROUPS[gi], _pair_swap(mine, gi, self._cid(gi) + 2)))

SMALL_REPL = ("attn_norm", "ffn_norm", "final_norm", "kv_in_norm", "kv_latent_norm", "q_latent_norm", "ffn_conv_b")
SMALL_SHARDED = ("sc_conv_w", "ffn_conv_w")
SMALL_ROWS = 256


def _pad_heads(w_uq):
    per_head = w_uq.reshape(Q_LORA, -1, QK_NOPE + QK_ROPE)
    return jnp.pad(per_head, ((0, 0), (0, 0), (0, HEAD_PAD - QK_NOPE - QK_ROPE))).reshape(Q_LORA, -1)


def _pack_kv(w_dkv, w_kr):
    return jnp.concatenate([w_dkv, w_kr, jnp.zeros((w_kr.shape[0], LANES - QK_ROPE), w_kr.dtype)], axis=1)


def kernel(x, positions, attn_norm, ffn_norm, final_norm, sc_w_in, sc_conv_w, sc_w_out, kv_in_norm, w_dkv, kv_latent_norm, w_kr, w_uk, w_uv, w_dq, q_latent_norm, w_uq, w_o, ffn_w_up, ffn_conv_w, ffn_conv_b, ffn_w_down, loss_target, m_attn_norm, m_ffn_norm, m_final_norm, m_sc_w_in, m_sc_conv_w, m_sc_w_out, m_kv_in_norm, m_w_dkv, m_kv_latent_norm, m_w_kr, m_w_uk, m_w_uv, m_w_dq, m_q_latent_norm, m_w_uq, m_w_o, m_ffn_w_up, m_ffn_conv_w, m_ffn_conv_b, m_ffn_w_down, v_attn_norm, v_ffn_norm, v_final_norm, v_sc_w_in, v_sc_conv_w, v_sc_w_out, v_kv_in_norm, v_w_dkv, v_kv_latent_norm, v_w_kr, v_w_uk, v_w_uv, v_w_dq, v_q_latent_norm, v_w_uq, v_w_o, v_ffn_w_up, v_ffn_conv_w, v_ffn_conv_b, v_ffn_w_down):
    names = ("attn_norm", "ffn_norm", "final_norm", "sc_w_in", "sc_conv_w", "sc_w_out", "kv_in_norm", "w_dkv",
             "kv_latent_norm", "w_kr", "w_uk", "w_uv", "w_dq", "q_latent_norm", "w_uq", "w_o", "ffn_w_up",
             "ffn_conv_w", "ffn_conv_b", "ffn_w_down")
    w = dict(zip(names, (attn_norm, ffn_norm, final_norm, sc_w_in, sc_conv_w, sc_w_out, kv_in_norm, w_dkv,
                         kv_latent_norm, w_kr, w_uk, w_uv, w_dq, q_latent_norm, w_uq, w_o, ffn_w_up,
                         ffn_conv_w, ffn_conv_b, ffn_w_down)))
    m = dict(zip(names, (m_attn_norm, m_ffn_norm, m_final_norm, m_sc_w_in, m_sc_conv_w, m_sc_w_out, m_kv_in_norm,
                         m_w_dkv, m_kv_latent_norm, m_w_kr, m_w_uk, m_w_uv, m_w_dq, m_q_latent_norm, m_w_uq, m_w_o,
                         m_ffn_w_up, m_ffn_conv_w, m_ffn_conv_b, m_ffn_w_down)))
    v = dict(zip(names, (v_attn_norm, v_ffn_norm, v_final_norm, v_sc_w_in, v_sc_conv_w, v_sc_w_out, v_kv_in_norm,
                         v_w_dkv, v_kv_latent_norm, v_w_kr, v_w_uk, v_w_uv, v_w_dq, v_q_latent_norm, v_w_uq, v_w_o,
                         v_ffn_w_up, v_ffn_conv_w, v_ffn_conv_b, v_ffn_w_down)))

    _ORDER[0] = None
    ix, iy, ic = lax.axis_index("x"), lax.axis_index("y"), lax.axis_index("c")
    chip = 2 * ix + iy
    ids = jnp.stack([ic, chip]).astype(jnp.int32)

    def shards_of(t):
        return {
            "sc_w_in": t["sc_w_in"][0], "sc_w_out": t["sc_w_out"][0], "ffn_w_up": t["ffn_w_up"],
            "ffn_w_down": t["ffn_w_down"], "w_kv": _pack_kv(t["w_dkv"], t["w_kr"]),
            "w_ukv": jnp.stack([t["w_uk"], t["w_uv"]]), "w_dq": t["w_dq"][0], "w_uq": _pad_heads(t["w_uq"][0]),
            "w_o": t["w_o"][0],
        }

    ws, ms, vs = shards_of(w), shards_of(m), shards_of(v)

    def ag_shard(name):
        if name == "sc_conv_w":
            return sc_conv_w[0]
        if name == "ffn_conv_w":
            return ffn_conv_w.reshape(6, -1)
        if name[:-1] in ("ffn_w_up", "ffn_w_down"):
            return ws[name[:-1]][int(name[-1])].astype(BF16)
        return ws[name].astype(BF16)

    wf = {}
    for gi, wms in enumerate(AG_GROUPS):
        fulls = _all_gather_group(gi, [ag_shard(wm.name) for wm in wms])
        wf.update({wm.name: f for wm, f in zip(wms, fulls)})
    small = {
        "attn_norm": attn_norm, "ffn_norm": ffn_norm, "final_norm": final_norm[None], "kv_in_norm": kv_in_norm[None],
        "kv_latent_norm": kv_latent_norm[None], "q_latent_norm": q_latent_norm, "ffn_conv_b": ffn_conv_b,
        "sc_conv_w": wf["sc_conv_w"].transpose(1, 0, 2).reshape(3, D),
        "ffn_conv_w": wf["ffn_conv_w"].reshape(N_CHIPS, 2, 3, -1).transpose(1, 2, 0, 3).reshape(2, 3, F_FF),
    }

    res = {}

    merged = lambda a: a.reshape(2 * KV_LORA, -1)

    def adamw_group(gi):
        items = []
        for key in RS_GROUPS[gi]:
            n, layer = (key[:-1], int(key[-1])) if key[:-1] in ("ffn_w_up", "ffn_w_down") else (key, None)
            w_, m_, v_ = (merged(t[n]) for t in (ws, ms, vs)) if n == "w_ukv" else (ws[n], ms[n], vs[n])
            items.append(dict(name=n, w=w_, m=m_, v=v_, g_mine=rs.mine[key], g_sib=rs.sib[key], layer=layer,
                              prev=res.get(n)))
        for it, out in zip(items, _adamw_shards(ids, items, f"adamw_group{gi}")):
            res[it["name"]] = out

    rs = _ReduceScatter(ids, adamw_group)
    loss, dx, small_g = _local_step(x[0], positions[0], loss_target[0], wf, small, rs)

    s_order = SMALL_REPL + SMALL_SHARDED
    flat = jnp.concatenate([small_g[n].reshape(-1) for n in s_order] + [loss.reshape(-1)])
    flat = jnp.pad(flat, (0, SMALL_ROWS * LANES - flat.shape[0])).reshape(SMALL_ROWS, LANES)
    red = _all_reduce_small(flat, "ar_small").reshape(-1)
    sg, off = {}, 0
    for n in s_order:
        sz = small_g[n].size
        sg[n] = red[off:off + sz].reshape(small_g[n].shape)
        off += sz
    loss_out = red[off]
    grads = {n: sg[n].reshape(w[n].shape) for n in SMALL_REPL}
    grads["sc_conv_w"] = lax.dynamic_slice_in_dim(sg["sc_conv_w"], chip * (D // N_CHIPS), D // N_CHIPS, axis=1)[None]
    grads["ffn_conv_w"] = lax.dynamic_slice_in_dim(sg["ffn_conv_w"], chip * (F_FF // N_CHIPS), F_FF // N_CHIPS, axis=2)

    rs.chip_sums(2)
    rs.pair_sums(3)
    rs.finish(2)
    rs.chip_sums(3)
    rs.finish(3)
    outs = [grads, {}, {}, {}]
    for k, dst in enumerate(outs):
        for n in ("sc_w_in", "sc_w_out", "w_dq", "w_o"):
            dst[n] = res[n][k][None]
        unpadded = res["w_uq"][k].reshape(Q_LORA, -1, HEAD_PAD)[:, :, :QK_NOPE + QK_ROPE]
        dst["w_uq"] = unpadded.reshape(w_uq.shape)
        dst["ffn_w_up"], dst["ffn_w_down"] = res["ffn_w_up"][k], res["ffn_w_down"][k]
        dst["w_dkv"], dst["w_kr"] = res["w_kv"][k][:, :KV_LORA], res["w_kv"][k][:, KV_LORA:KV_LORA + QK_ROPE]
        dst["w_uk"], dst["w_uv"] = res["w_ukv"][k][:KV_LORA], res["w_ukv"][k][KV_LORA:]
    grads, delta, new_m, new_v = outs

    small_names = SMALL_REPL + SMALL_SHARDED

    def pack_small(tree):
        return jnp.concatenate([tree[n].reshape(-1) for n in small_names]).reshape(-1, LANES)

    small_res = _adamw_small(pack_small(w), pack_small(grads), pack_small(m), pack_small(v))
    for slab, dst in zip(small_res, (delta, new_m, new_v)):
        f, off = slab.reshape(-1), 0
        for n in small_names:
            dst[n] = f[off:off + w[n].size].reshape(w[n].shape)
            off += w[n].size

    _ORDER[0] = None
    return (loss_out, dx[None], *[grads[n] for n in names], *[delta[n] for n in names],
            *[new_m[n] for n in names], *[new_v[n] for n in names])
```

```python
from typing import NamedTuple

import jax
import jax.numpy as jnp
from jax import lax
from jax.experimental import pallas as pl
from jax.experimental.pallas import tpu as pltpu
from jax.experimental.pallas import tpu_sc as plsc

F32 = jnp.float32
BF16 = jnp.bfloat16

T = 2048
D = 1024
F_FF = 2816
N_HEADS = 8
QK_NOPE = 128
QK_ROPE = 64
V_HEAD = 128
Q_LORA = 384
KV_LORA = 256
CHUNK_SHIFT = 6
ROPE_THETA = 10000.0
EPS = 1e-6
NEG_INF = -1e30
HEAD_PAD = 256
KVP = KV_LORA + 128

ADAM_LR = 0.001
ADAM_B1 = 0.9
ADAM_B2 = 0.999
ADAM_EPS = 1e-08
ADAM_WD = 0.01
ADAM_STEP = 10

N_CHIPS = 4
N_DEV = 8
LANES = 128
TC = 256
V7X_VMEM_LIMIT = 56 * 1024 * 1024

MESH = pl.DeviceIdType.MESH
ANY = pl.BlockSpec(memory_space=pl.ANY)


class _W(NamedTuple):
    name: str
    kind: str
    nl: int
    k: int
    n: int


AG_GROUPS = (
    (_W("sc_w_in", "col", 1, D, 3 * D // N_CHIPS), _W("sc_conv_w", "tiny", 1, 3, D // N_CHIPS),
     _W("ffn_conv_w", "tiny", 1, 6, F_FF // N_CHIPS), _W("sc_w_out", "row", 1, D // N_CHIPS, D)),
    (_W("ffn_w_up0", "col", 1, D, 2 * F_FF // N_CHIPS),),
    (_W("ffn_w_down0", "row", 1, F_FF // N_CHIPS, D),),
    (_W("w_kv", "row", 1, D // N_CHIPS, KVP), _W("w_ukv", "col", 2, KV_LORA, N_HEADS * QK_NOPE // N_CHIPS),
     _W("w_dq", "row", 1, D // N_CHIPS, Q_LORA),
     _W("w_uq", "col", 1, Q_LORA, N_HEADS * HEAD_PAD // N_CHIPS),
     _W("w_o", "row", 1, N_HEADS * V_HEAD // N_CHIPS, D)),
    (_W("ffn_w_up1", "col", 1, D, 2 * F_FF // N_CHIPS), _W("ffn_w_down1", "row", 1, F_FF // N_CHIPS, D)),
)


def _cp(*sem):
    return pltpu.CompilerParams(dimension_semantics=sem, vmem_limit_bytes=V7X_VMEM_LIMIT)


_ORDER = [None]


def _tc_call(body, *, name, out_shape, in_specs=None, out_specs=None, grid=(), scratch_shapes=(), prefetch=0,
             input_output_aliases=None, compiler_params=None):
    def run(*args):
        specs = [pl.BlockSpec(memory_space=pltpu.VMEM)] * (len(args) - prefetch) if in_specs is None else list(in_specs)
        inner, dep = body, _ORDER[0]
        if dep is not None:
            unread = prefetch + len(specs)
            specs, args = specs + [ANY], (*args, dep)

            def inner(*refs):
                return body(*refs[:unread], *refs[unread + 1:])

        kwargs = dict(name=name, out_shape=out_shape, input_output_aliases=input_output_aliases or {},
                      compiler_params=compiler_params)
        if prefetch:
            kwargs["grid_spec"] = pltpu.PrefetchScalarGridSpec(
                num_scalar_prefetch=prefetch, grid=grid, in_specs=specs, out_specs=out_specs,
                scratch_shapes=scratch_shapes)
        else:
            kwargs.update(grid=grid, in_specs=specs, scratch_shapes=scratch_shapes)
            if out_specs is not None:
                kwargs["out_specs"] = out_specs
        out = pl.pallas_call(inner, **kwargs)(*args)
        _ORDER[0] = out[0] if isinstance(out, (list, tuple)) else out
        return out

    return run


def _tile(n, cands):
    for c in cands:
        if n % c == 0:
            return c
    raise ValueError(f"no tile for {n}")


NN_DIMS = (((1,), (0,)), ((), ()))
NT_DIMS = (((1,), (1,)), ((), ()))
TN_DIMS = (((0,), (0,)), ((), ()))
M_TILES = (1024, 512, 384, 256, 128)
N_TILES = (1408, 1024, 768, 512, 384, 256, 128)
MM_BLOCK_BYTES = 36 * 1024 * 1024


def _fit(m, n, block_bytes, m_tiles=M_TILES, n_tiles=N_TILES):
    for tm in [c for c in m_tiles if m % c == 0]:
        for tn in [c for c in n_tiles if n % c == 0]:
            if 2 * block_bytes(tm, tn) + 4 * tm * tn <= MM_BLOCK_BYTES:
                return tm, tn
    raise ValueError(f"no tiles for {m} x {n}")


def _size(x):
    return x.dtype.itemsize


def _mm(name, a, b, dims, grid, a_spec, b_spec, o_spec, o_sds, add=None, red=None, acc_shape=None):
    n_red = None if red is None else grid[red]

    def body(*refs):
        a_ref, b_ref = refs[0], refs[1]
        add_ref = refs[2] if add is not None else None
        o_ref = refs[3] if add is not None else refs[2]
        part = lax.dot_general(a_ref[...].astype(BF16), b_ref[...].astype(BF16), dims, preferred_element_type=F32)
        if red is None:
            if add is not None:
                part = part + add_ref[...]
            o_ref[...] = part.astype(o_ref.dtype)
            return
        acc_ref = refs[-1]
        r = pl.program_id(red)

        @pl.when(r == 0)
        def _():
            acc_ref[...] = part

        @pl.when(r > 0)
        def _():
            acc_ref[...] += part

        @pl.when(r == n_red - 1)
        def _():
            o_ref[...] = acc_ref[...].astype(o_ref.dtype)

    sem = tuple("arbitrary" if ax == red else "parallel" for ax in range(len(grid)))
    in_specs = [a_spec, b_spec] + ([o_spec] if add is not None else [])
    args = (a, b) + ((add,) if add is not None else ())
    return _tc_call(
        body, name=name, grid=grid, in_specs=in_specs, out_specs=o_spec, out_shape=o_sds,
        scratch_shapes=[] if red is None else [pltpu.VMEM(acc_shape, F32)], compiler_params=_cp(*sem),
    )(*args)


def _nn(name, a, b, out_dtype, add=None, lead=None):
    (m, k), n = a.shape, b.shape[-1]
    osz = jnp.dtype(out_dtype).itemsize + (4 if add is not None else 0)
    tm, tn = _fit(m, n, lambda tm, tn: tm * k * _size(a) + k * tn * _size(b) + tm * tn * osz)
    if lead is None:
        b_spec = pl.BlockSpec((k, tn), lambda i, j: (0, j))
    else:
        b_spec = pl.BlockSpec((None, k, tn), lambda i, j: (lead, 0, j))
    return _mm(name, a, b, NN_DIMS, (m // tm, n // tn), pl.BlockSpec((tm, k), lambda i, j: (i, 0)), b_spec,
               pl.BlockSpec((tm, tn), lambda i, j: (i, j)), jax.ShapeDtypeStruct((m, n), out_dtype), add=add)


def _nn_parts(name, a, b, parts, out_dtype, lead=None, stacked=False):
    m, k = a.shape
    c = b.shape[-1] if stacked else b.shape[-1] // parts
    osz = jnp.dtype(out_dtype).itemsize
    tm, tn = _fit(m, c, lambda tm, tn: tm * k * _size(a) + k * tn * _size(b) + tm * tn * osz)
    nb = c // tn
    if stacked:
        b_spec = pl.BlockSpec((None, k, tn), lambda i, p, j: (p, 0, j))
    elif lead is None:
        b_spec = pl.BlockSpec((k, tn), lambda i, p, j: (0, p * nb + j))
    else:
        b_spec = pl.BlockSpec((None, k, tn), lambda i, p, j: (lead, 0, p * nb + j))
    return _mm(name, a, b, NN_DIMS, (m // tm, parts, nb), pl.BlockSpec((tm, k), lambda i, p, j: (i, 0)), b_spec,
               pl.BlockSpec((None, tm, tn), lambda i, p, j: (p, i, j)), jax.ShapeDtypeStruct((parts, m, c), out_dtype))


def _nt(name, a, b, out_dtype, lead=None):
    (m, k), n = a.shape, b.shape[-2]
    osz = jnp.dtype(out_dtype).itemsize
    tm, tn = _fit(m, n, lambda tm, tn: tm * k * _size(a) + tn * k * _size(b) + tm * tn * osz)
    if lead is None:
        b_spec = pl.BlockSpec((tn, k), lambda i, j: (j, 0))
    else:
        b_spec = pl.BlockSpec((None, tn, k), lambda i, j: (lead, j, 0))
    return _mm(name, a, b, NT_DIMS, (m // tm, n // tn), pl.BlockSpec((tm, k), lambda i, j: (i, 0)), b_spec,
               pl.BlockSpec((tm, tn), lambda i, j: (i, j)), jax.ShapeDtypeStruct((m, n), out_dtype))


def _tn(name, a, b, out_dtype):
    (k, m), n = a.shape, b.shape[1]
    osz = jnp.dtype(out_dtype).itemsize
    tm, tn = _fit(m, n, lambda tm, tn: k * tm * _size(a) + k * tn * _size(b) + tm * tn * osz,
                  m_tiles=(512, 384, 256, 128), n_tiles=(n,) + N_TILES)
    return _mm(name, a, b, TN_DIMS, (m // tm, n // tn), pl.BlockSpec((k, tm), lambda i, j: (0, i)),
               pl.BlockSpec((k, tn), lambda i, j: (0, j)), pl.BlockSpec((tm, tn), lambda i, j: (i, j)),
               jax.ShapeDtypeStruct((m, n), out_dtype))


def _nn_add_norm(name, a, b, add, g):
    (m, k), n = a.shape, b.shape[1]
    tm = 512

    def body(a_ref, b_ref, add_ref, g_ref, h_ref, hn_ref):
        h = jnp.dot(a_ref[...], b_ref[...], preferred_element_type=F32) + add_ref[...]
        h_ref[...] = h
        hn_ref[...] = _rms_rows(h, g_ref[...]).astype(BF16)

    rows = lambda w: pl.BlockSpec((tm, w), lambda i: (i, 0))
    return _tc_call(
        body, name=name, grid=(m // tm,),
        in_specs=[rows(k), pl.BlockSpec((k, n), lambda i: (0, 0)), rows(n), pl.BlockSpec((1, n), lambda i: (0, 0))],
        out_specs=[rows(n), rows(n)],
        out_shape=[jax.ShapeDtypeStruct((m, n), F32), jax.ShapeDtypeStruct((m, n), BF16)], compiler_params=_cp("parallel"),
    )(a, b, add, g)


def _nn_add_loss(name, a, b, add, g, tgt):
    (m, k), n = a.shape, b.shape[1]
    tm = 512

    def body(a_ref, b_ref, add_ref, g_ref, t_ref, loss_ref, dh_ref, dhb_ref, dg_ref):
        xv = jnp.dot(a_ref[...], b_ref[...], preferred_element_type=F32) + add_ref[...]
        gv = g_ref[...]
        r = lax.rsqrt(jnp.mean(xv * xv, axis=1, keepdims=True) + EPS)
        err = xv * r * gv - t_ref[...]
        part = 0.5 * jnp.sum(jnp.mean(err * err, axis=1, keepdims=True), axis=0, keepdims=True)
        dx, dg = _rms_bwd_math(xv, gv, err * (1.0 / n))
        dh_ref[...] = dx
        dhb_ref[...] = dx.astype(BF16)

        @pl.when(pl.program_id(0) == 0)
        def _():
            dg_ref[...] = jnp.zeros_like(dg_ref)
            loss_ref[...] = jnp.zeros_like(loss_ref)

        dg_ref[...] += dg
        loss_ref[...] += jnp.broadcast_to(part, loss_ref.shape)

    rows = lambda w: pl.BlockSpec((tm, w), lambda i: (i, 0))
    vec = pl.BlockSpec((1, n), lambda i: (0, 0))
    return _tc_call(
        body, name=name, grid=(m // tm,),
        in_specs=[rows(k), pl.BlockSpec((k, n), lambda i: (0, 0)), rows(n), vec, rows(n)],
        out_specs=[pl.BlockSpec((1, LANES), lambda i: (0, 0)), rows(n), rows(n), vec],
        out_shape=[jax.ShapeDtypeStruct((1, LANES), F32), jax.ShapeDtypeStruct((m, n), F32),
                   jax.ShapeDtypeStruct((m, n), BF16), jax.ShapeDtypeStruct((1, n), F32)],
        compiler_params=_cp("arbitrary"),
    )(a, b, add, g, tgt)


def _dx_norm_bwd(name, a, b, x, g, add):
    parts, t, c = a.shape
    d = b.shape[0]
    tm = 256

    def body(a_ref, b_ref, x_ref, g_ref, add_ref, dx_ref, dxb_ref, dg_ref):
        dy = None
        for p in range(parts):
            part = lax.dot_general(a_ref[p], b_ref[:, p * c:(p + 1) * c], NT_DIMS, preferred_element_type=F32)
            dy = part if dy is None else dy + part
        dx, dg = _rms_bwd_math(x_ref[...], g_ref[...], dy)
        dx = dx + add_ref[...]
        dx_ref[...] = dx
        dxb_ref[...] = dx.astype(BF16)

        @pl.when(pl.program_id(0) == 0)
        def _():
            dg_ref[...] = jnp.zeros_like(dg_ref)

        dg_ref[...] += dg

    rows = pl.BlockSpec((tm, d), lambda i: (i, 0))
    vec = pl.BlockSpec((1, d), lambda i: (0, 0))
    return _tc_call(
        body, name=name, grid=(t // tm,),
        in_specs=[pl.BlockSpec((parts, tm, c), lambda i: (0, i, 0)), pl.BlockSpec(b.shape, lambda i: (0, 0)), rows, vec,
                  rows],
        out_specs=[rows, rows, vec],
        out_shape=[jax.ShapeDtypeStruct((t, d), F32), jax.ShapeDtypeStruct((t, d), BF16),
                   jax.ShapeDtypeStruct((1, d), F32)],
        compiler_params=_cp("arbitrary"),
    )(a, b, x, g, add)


def _dw_sc_in(hn, dz):
    t, tn, tm = hn.shape[0], TC, D
    per_part, per_chip = D // tn, 3 * D // N_CHIPS // tn
    return _mm("sc_in_dw", hn, dz, TN_DIMS, (D // tm, 3 * D // tn), pl.BlockSpec((t, tm), lambda i, j: (0, i)),
               pl.BlockSpec((None, t, tn), lambda i, j: (j // per_part, 0, j % per_part)),
               pl.BlockSpec((None, tm, tn), lambda i, j: (j // per_chip, i, j % per_chip)),
               jax.ShapeDtypeStruct((N_CHIPS, D, 3 * D // N_CHIPS), BF16))


def _dw_ffn_up(name, hf, dup):
    t, tm, ns = hf.shape[0], D, 2 * F_FF // N_CHIPS
    return _mm(name, hf, dup, TN_DIMS, (N_CHIPS, D // tm), pl.BlockSpec((t, tm), lambda s, i: (0, i)),
               pl.BlockSpec((None, t, ns), lambda s, i: (s // 2, 0, s % 2)),
               pl.BlockSpec((None, tm, ns), lambda s, i: (s, i, 0)), jax.ShapeDtypeStruct((N_CHIPS, D, ns), BF16))


def _rms_fwd(x, g, name):
    t, d = x.shape
    tr = 512

    def body(x_ref, g_ref, o_ref):
        xv = x_ref[...]
        r = lax.rsqrt(jnp.mean(xv * xv, axis=1, keepdims=True) + EPS)
        o_ref[...] = (xv * r * g_ref[...]).astype(o_ref.dtype)

    row = pl.BlockSpec((tr, d), lambda i: (i, 0))
    return _tc_call(
        body, name=name, grid=(t // tr,), in_specs=[row, pl.BlockSpec((1, d), lambda i: (0, 0))],
        out_specs=row, out_shape=jax.ShapeDtypeStruct((t, d), BF16), compiler_params=_cp("parallel"),
    )(x, g)


def _rms_bwd_math(xv, g, dy):
    r = lax.rsqrt(jnp.mean(xv * xv, axis=1, keepdims=True) + EPS)
    xh = xv * r
    gy = dy * g
    dx = r * (gy - xh * jnp.mean(gy * xh, axis=1, keepdims=True))
    dg = jnp.sum(dy * xh, axis=0, keepdims=True)
    return dx, dg


def _rot_half(x):
    lane = lax.broadcasted_iota(jnp.int32, x.shape, 1)
    return jnp.where((lane % QK_ROPE) < QK_ROPE // 2, -pltpu.roll(x, LANES - 32, axis=1),
                     pltpu.roll(x, 32, axis=1))


def _rope_fwd_math(x, cos, sin):
    return x * cos + _rot_half(x) * sin


def _rope_bwd_math(dy, cos, sin):
    return dy * cos - _rot_half(dy * sin)


def _rms_rows(x, g):
    return x * lax.rsqrt(jnp.mean(x * x, axis=1, keepdims=True) + EPS) * g


def _attn_prep(h, g_attn, g_kvin, w_dq, g_ql, w_uq, w_kv, g_kvl, w_ukv, cos, sin):
    t, d = h.shape
    tr = 256
    wq = N_HEADS * HEAD_PAD

    def body(h_ref, ga_ref, gk_ref, wdq_ref, gq_ref, wuq_ref, wkv_ref, gl_ref, wukv_ref, c_ref, s_ref,
             hn_ref, hk_ref, cqp_ref, cq_ref, q_ref, kvp_ref, ckv_ref, kr_ref, knv_ref):
        xv, cv, sv = h_ref[...], c_ref[...], s_ref[...]
        xh = xv * lax.rsqrt(jnp.mean(xv * xv, axis=1, keepdims=True) + EPS)
        hn = (xh * ga_ref[...]).astype(BF16)
        hk = (xh * gk_ref[...]).astype(BF16)
        hn_ref[...], hk_ref[...] = hn, hk
        cq_pre = jnp.dot(hn, wdq_ref[...], preferred_element_type=F32)
        cqp_ref[...] = cq_pre
        cq = _rms_rows(cq_pre, gq_ref[...]).astype(BF16)
        cq_ref[...] = cq
        for hd in range(N_HEADS):
            lo = hd * HEAD_PAD
            qh = jnp.dot(cq, wuq_ref[:, lo:lo + HEAD_PAD], preferred_element_type=F32)
            q_ref[:, lo:lo + QK_NOPE] = qh[:, :QK_NOPE].astype(BF16)
            q_ref[:, lo + QK_NOPE:lo + HEAD_PAD] = _rope_fwd_math(qh[:, QK_NOPE:], cv, sv).astype(BF16)
        kvpre = jnp.dot(hk, wkv_ref[...], preferred_element_type=F32)
        kvp_ref[...] = kvpre
        ckv = _rms_rows(kvpre[:, :KV_LORA], gl_ref[...]).astype(BF16)
        ckv_ref[...] = ckv
        kr_ref[...] = _rope_fwd_math(kvpre[:, KV_LORA:], cv, sv).astype(BF16)
        for p in range(2):
            knv_ref[p] = jnp.dot(ckv, wukv_ref[p], preferred_element_type=F32).astype(BF16)

    rows = lambda w: pl.BlockSpec((tr, w), lambda i: (i, 0))
    whole = lambda a: pl.BlockSpec(a.shape, lambda i: (0,) * a.ndim)
    sds = lambda w, dt: jax.ShapeDtypeStruct((t, w), dt)
    args = (h, g_attn, g_kvin, w_dq, g_ql, w_uq, w_kv, g_kvl, w_ukv, cos, sin)
    return _tc_call(
        body, name="attn_prep", grid=(t // tr,),
        in_specs=[rows(d)] + [whole(a) for a in args[1:9]] + [rows(LANES), rows(LANES)],
        out_specs=[rows(d), rows(d), rows(Q_LORA), rows(Q_LORA), rows(wq), rows(KVP), rows(KV_LORA), rows(LANES),
                   pl.BlockSpec((2, tr, N_HEADS * QK_NOPE), lambda i: (0, i, 0))],
        out_shape=[sds(d, BF16), sds(d, BF16), sds(Q_LORA, F32), sds(Q_LORA, BF16), sds(wq, BF16), sds(KVP, F32),
                   sds(KV_LORA, BF16), sds(LANES, BF16), jax.ShapeDtypeStruct((2, t, N_HEADS * QK_NOPE), BF16)],
        compiler_params=_cp("parallel"),
    )(*args)


def _attn_prep_bwd(dq, dknv, dkr, dh, h, hn, hk, cq_pre, cq, kvpre, ckv, g_attn, g_kvin, w_dq, g_ql, w_uq, w_kv, g_kvl,
                   w_ukv, cos, sin):
    t, d = h.shape
    tr = 256
    n_steps = t // tr
    wq = N_HEADS * HEAD_PAD
    wk = N_HEADS * QK_NOPE

    def body(dq_ref, dknv_ref, dkr_ref, dh_ref, h_ref, hn_ref, hk_ref, cqp_ref, cq_ref, kvp_ref, ckv_ref,
             ga_ref, gk_ref, wdq_ref, gq_ref, wuq_ref, wkv_ref, gl_ref, wukv_ref, c_ref, s_ref,
             dho_ref, dhb_ref, dwuq_ref, dwdq_ref, dwukv_ref, dwkv_ref, dga_ref, dgk_ref, dgq_ref, dgl_ref,
             a_uq, a_dq, a_ukv, a_kv):
        i = pl.program_id(0)

        @pl.when(i == 0)
        def _():
            for ref in (a_uq, a_dq, a_ukv, a_kv, dga_ref, dgk_ref, dgq_ref, dgl_ref):
                ref[...] = jnp.zeros_like(ref)

        dqv = dq_ref[...]
        dcq = lax.dot_general(dqv, wuq_ref[...], NT_DIMS, preferred_element_type=F32)
        a_uq[...] += lax.dot_general(cq_ref[...], dqv, TN_DIMS, preferred_element_type=F32)
        dcq_pre, dg = _rms_bwd_math(cqp_ref[...], gq_ref[...], dcq)
        dgq_ref[...] += dg
        dcq_pre = dcq_pre.astype(BF16)
        dhn = lax.dot_general(dcq_pre, wdq_ref[...], NT_DIMS, preferred_element_type=F32)
        a_dq[...] += lax.dot_general(hn_ref[...], dcq_pre, TN_DIMS, preferred_element_type=F32)
        dckv = None
        for p in range(2):
            dk = dknv_ref[p].astype(BF16)
            part = lax.dot_general(dk, wukv_ref[p], NT_DIMS, preferred_element_type=F32)
            dckv = part if dckv is None else dckv + part
            a_ukv[p] += lax.dot_general(ckv_ref[...], dk, TN_DIMS, preferred_element_type=F32)
        dlat, dg = _rms_bwd_math(kvp_ref[:, :KV_LORA], gl_ref[...], dckv)
        dgl_ref[...] += dg
        dkr_pre = _rope_bwd_math(dkr_ref[...], c_ref[...], s_ref[...])
        dkvpre = jnp.concatenate([dlat, dkr_pre], axis=1).astype(BF16)
        dhk = lax.dot_general(dkvpre, wkv_ref[...], NT_DIMS, preferred_element_type=F32)
        a_kv[...] += lax.dot_general(hk_ref[...], dkvpre, TN_DIMS, preferred_element_type=F32)
        xv = h_ref[...]
        dx1, dg = _rms_bwd_math(xv, ga_ref[...], dhn)
        dga_ref[...] += dg
        dx2, dg = _rms_bwd_math(xv, gk_ref[...], dhk)
        dgk_ref[...] += dg
        dh_new = dh_ref[...] + dx1 + dx2
        dho_ref[...] = dh_new
        dhb_ref[...] = dh_new.astype(BF16)

        @pl.when(i == n_steps - 1)
        def _():
            dwuq_ref[...] = a_uq[...].astype(BF16)
            dwdq_ref[...] = a_dq[...].astype(BF16)
            dwukv_ref[...] = a_ukv[...].astype(BF16)
            dwkv_ref[...] = a_kv[...].astype(BF16)

    rows = lambda w: pl.BlockSpec((tr, w), lambda i: (i, 0))
    whole = lambda shape: pl.BlockSpec(shape, lambda i: (0,) * len(shape))
    weights = (g_attn, g_kvin, w_dq, g_ql, w_uq, w_kv, g_kvl, w_ukv)
    dw_shapes = [(Q_LORA, wq), (d, Q_LORA), (2, KV_LORA, wk), (d, KVP)]
    dg_shapes = [(1, d), (1, d), (1, Q_LORA), (1, KV_LORA)]
    return _tc_call(
        body, name="attn_prep_bwd", grid=(n_steps,),
        in_specs=[rows(wq), pl.BlockSpec((2, tr, wk), lambda i: (0, i, 0)), rows(LANES), rows(d), rows(d), rows(d),
                  rows(d), rows(Q_LORA), rows(Q_LORA), rows(KVP), rows(KV_LORA)]
        + [whole(a.shape) for a in weights] + [rows(LANES), rows(LANES)],
        out_specs=[rows(d), rows(d)] + [whole(s) for s in dw_shapes + dg_shapes],
        out_shape=[jax.ShapeDtypeStruct((t, d), F32), jax.ShapeDtypeStruct((t, d), BF16)]
        + [jax.ShapeDtypeStruct(s, BF16) for s in dw_shapes] + [jax.ShapeDtypeStruct(s, F32) for s in dg_shapes],
        scratch_shapes=[pltpu.VMEM(s, F32) for s in dw_shapes], compiler_params=_cp("arbitrary"),
    )(dq, dknv, dkr, dh, h, hn, hk, cq_pre, cq, kvpre, ckv, *weights, cos, sin)


ROW_CHUNK = 64
HALO = 16
WIN = ROW_CHUNK + 16
LANE_HALVES = (slice(0, LANES), slice(LANES, TC))


def _stage(s_ref, p, src):
    t = src.shape[0]
    s_ref[p, :HALO] = jnp.zeros((HALO, TC), BF16)
    s_ref[p, HALO:HALO + t] = src
    s_ref[p, HALO + t:] = jnp.zeros((HALO, TC), BF16)


def _window(s_ref, p, i, lanes):
    base = pl.multiple_of(i * ROW_CHUNK, ROW_CHUNK)
    return s_ref[p, pl.ds(base, ROW_CHUNK + 2 * HALO), lanes].astype(F32)[8:8 + WIN]


def _valid(x):
    return x[8:8 + ROW_CHUNK]


def _prev(x, k):
    return pltpu.roll(x, k, axis=0)


def _next(x, k):
    return pltpu.roll(x, WIN - k, axis=0)


def _taps(w_ref, lanes):
    return w_ref[0:1, lanes], w_ref[1:2, lanes], w_ref[2:3, lanes]


def _fold8(x):
    return jnp.sum(x.reshape(ROW_CHUNK // 8, 8, x.shape[-1]), axis=0)


def _store_rows(ref, idx, i, lanes, x):
    rows = pl.ds(pl.multiple_of(i * ROW_CHUNK, ROW_CHUNK), ROW_CHUNK)
    ref[(*idx, rows, lanes)] = x.astype(ref.dtype)


def _for_chunks(t, chunk):
    def step(i, carry):
        for lanes in LANE_HALVES:
            chunk(i, lanes)
        return carry

    lax.fori_loop(0, t // ROW_CHUNK, step, 0)


def _write_col_sums(acc_ref, outs):
    for k, (ref, row) in enumerate(outs):
        ref[row:row + 1, :] = jnp.sum(acc_ref[k], axis=0, keepdims=True)


def _shift_down(x, k):
    row = lax.broadcasted_iota(jnp.int32, x.shape, 0)
    return jnp.where(row >= k, pltpu.roll(x, k, axis=0), 0.0)


def _shift_up(x, k):
    n = x.shape[0]
    row = lax.broadcasted_iota(jnp.int32, x.shape, 0)
    return jnp.where(row < n - k, pltpu.roll(x, n - k, axis=0), 0.0)


def _conv3(x, w_ref):
    return _shift_down(x, 2) * w_ref[0:1, :] + _shift_down(x, 1) * w_ref[1:2, :] + x * w_ref[2:3, :]


def _col(parts, t):
    if parts is None:
        return pl.BlockSpec((t, TC), lambda j: (0, j))
    return pl.BlockSpec((parts, t, TC), lambda j: (0, 0, j))


def _staging(parts, t):
    return pltpu.VMEM((parts, t + 2 * HALO, TC), BF16)


def _scmix_fwd(z, w):
    t = z.shape[1]

    def body(z_ref, w_ref, m_ref):
        b, c, u = (z_ref[p].astype(F32) for p in range(3))
        m_ref[...] = (b * _conv3(c * u, w_ref)).astype(BF16)

    return _tc_call(
        body, name="scmix_fwd", grid=(D // TC,), in_specs=[_col(3, t), pl.BlockSpec((3, TC), lambda j: (0, j))],
        out_specs=_col(None, t), out_shape=jax.ShapeDtypeStruct((t, D), BF16), compiler_params=_cp("parallel"),
    )(z, w)


def _scmix_bwd(z, w, dm):
    t = z.shape[1]

    def body(z_ref, w_ref, dm_ref, dz_ref, dw_ref, s_ref, acc_ref):
        for p in range(3):
            _stage(s_ref, p, z_ref[p])
        _stage(s_ref, 3, dm_ref[...])
        acc_ref[...] = jnp.zeros_like(acc_ref)

        def chunk(i, lanes):
            w0, w1, w2 = _taps(w_ref, lanes)
            b, c, u, dm = (_window(s_ref, p, i, lanes) for p in range(4))
            cu = c * u
            cu1, cu2 = _prev(cu, 1), _prev(cu, 2)
            _store_rows(dz_ref, (0,), i, lanes, _valid(dm * (cu2 * w0 + cu1 * w1 + cu * w2)))
            dcv = dm * b
            dcu = dcv * w2 + _next(dcv, 1) * w1 + _next(dcv, 2) * w0
            _store_rows(dz_ref, (1,), i, lanes, _valid(dcu * u))
            _store_rows(dz_ref, (2,), i, lanes, _valid(dcu * c))
            for k, shifted in enumerate((cu2, cu1, cu)):
                acc_ref[k, :, lanes] += _fold8(_valid(dcv * shifted))

        _for_chunks(t, chunk)
        _write_col_sums(acc_ref, [(dw_ref, 0), (dw_ref, 1), (dw_ref, 2)])

    wspec = pl.BlockSpec((3, TC), lambda j: (0, j))
    return _tc_call(
        body, name="scmix_bwd", grid=(D // TC,), in_specs=[_col(3, t), wspec, _col(None, t)],
        out_specs=[_col(3, t), wspec],
        out_shape=[jax.ShapeDtypeStruct((3, t, D), BF16), jax.ShapeDtypeStruct((3, D), F32)],
        scratch_shapes=[_staging(4, t), pltpu.VMEM((3, 8, TC), F32)], compiler_params=_cp("parallel"),
    )(z, w, dm)


def _ffn_up_gate(hf, w_up, w, bias, name):
    t, d = hf.shape
    nb = F_FF // TC

    def body(hf_ref, wg_ref, wv_ref, w_ref, b_ref, up_ref, a_ref, prev_ref):
        @pl.when(pl.program_id(0) == 0)
        def _():
            prev_ref[...] = jnp.zeros_like(prev_ref)

        gc = _conv3(prev_ref[0].astype(F32), w_ref) + b_ref[...]
        a_ref[...] = (gc * jax.nn.sigmoid(gc) * prev_ref[1].astype(F32)).astype(BF16)
        hv = hf_ref[...]
        up_ref[0] = jnp.dot(hv, wg_ref[...], preferred_element_type=F32).astype(BF16)
        up_ref[1] = jnp.dot(hv, wv_ref[...], preferred_element_type=F32).astype(BF16)
        prev_ref[...] = up_ref[...]

    tile = lambda j: jnp.minimum(j, nb - 1)
    gated = lambda j: jnp.maximum(j - 1, 0)
    return _tc_call(
        body, name=name, grid=(nb + 1,),
        in_specs=[pl.BlockSpec((t, d), lambda j: (0, 0)), pl.BlockSpec((d, TC), lambda j: (0, tile(j))),
                  pl.BlockSpec((d, TC), lambda j: (0, nb + tile(j))), pl.BlockSpec((3, TC), lambda j: (0, gated(j))),
                  pl.BlockSpec((1, TC), lambda j: (0, gated(j)))],
        out_specs=[pl.BlockSpec((2, t, TC), lambda j: (0, 0, tile(j))), pl.BlockSpec((t, TC), lambda j: (0, gated(j)))],
        out_shape=[jax.ShapeDtypeStruct((2, t, F_FF), BF16), jax.ShapeDtypeStruct((t, F_FF), BF16)],
        scratch_shapes=[pltpu.VMEM((2, t, TC), BF16)], compiler_params=_cp("arbitrary"),
    )(hf, w_up, w_up, w, bias)


def _gate_bwd(up, w, bias, da, name):
    t = up.shape[1]

    def body(u_ref, w_ref, b_ref, da_ref, du_ref, dw_ref, db_ref, s_ref, acc_ref):
        for p in range(2):
            _stage(s_ref, p, u_ref[p])
        _stage(s_ref, 2, da_ref[...])
        acc_ref[...] = jnp.zeros_like(acc_ref)

        def chunk(i, lanes):
            w0, w1, w2 = _taps(w_ref, lanes)
            g, v, da = (_window(s_ref, p, i, lanes) for p in range(3))
            g1, g2 = _prev(g, 1), _prev(g, 2)
            gc = g2 * w0 + g1 * w1 + g * w2 + b_ref[:, lanes]
            sg = jax.nn.sigmoid(gc)
            _store_rows(du_ref, (1,), i, lanes, _valid(da * (gc * sg)))
            dgc = da * v * (sg * (1.0 + gc * (1.0 - sg)))
            _store_rows(du_ref, (0,), i, lanes, _valid(dgc * w2 + _next(dgc, 1) * w1 + _next(dgc, 2) * w0))
            for k, shifted in enumerate((g2, g1, g)):
                acc_ref[k, :, lanes] += _fold8(_valid(dgc * shifted))
            acc_ref[3, :, lanes] += _fold8(_valid(dgc))

        _for_chunks(t, chunk)
        _write_col_sums(acc_ref, [(dw_ref, 0), (dw_ref, 1), (dw_ref, 2), (db_ref, 0)])

    wspec = pl.BlockSpec((3, TC), lambda j: (0, j))
    bspec = pl.BlockSpec((1, TC), lambda j: (0, j))
    return _tc_call(
        body, name=name, grid=(F_FF // TC,), in_specs=[_col(2, t), wspec, bspec, _col(None, t)],
        out_specs=[_col(2, t), wspec, bspec],
        out_shape=[jax.ShapeDtypeStruct((2, t, F_FF), BF16), jax.ShapeDtypeStruct((3, F_FF), F32),
                   jax.ShapeDtypeStruct((1, F_FF), F32)],
        scratch_shapes=[_staging(3, t), pltpu.VMEM((4, 8, TC), F32)], compiler_params=_cp("parallel"),
    )(up, w, bias, da)


ATT_TQ = 256
ATT_SCALE = (QK_NOPE + QK_ROPE) ** -0.5


def _key_ranges(lvl):
    lo = lvl * ATT_TQ
    return ([(0, lo, False)] if lvl else []) + [(lo, lo + ATT_TQ, True)]


FWD_HEADS = 4
BWD_HEADS = 2


def _fill_keys(k_ref, kn_ref, kr_ref):
    @pl.when(pl.program_id(1) == 0)
    def _():
        for hh in range(k_ref.shape[0]):
            k_ref[hh, :, :QK_NOPE] = kn_ref[:, hh * QK_NOPE:(hh + 1) * QK_NOPE]
            k_ref[hh, :, QK_NOPE:] = kr_ref[...]


def _attn_probs(q, k_ref, lvl):
    scores = []
    for lo, hi, diagonal in _key_ranges(lvl):
        s = lax.dot_general(q, k_ref[lo:hi, :], NT_DIMS, preferred_element_type=F32) * ATT_SCALE
        if diagonal:
            row = lax.broadcasted_iota(jnp.int32, s.shape, 0)
            col = lax.broadcasted_iota(jnp.int32, s.shape, 1)
            seen = lax.shift_right_logical(col, CHUNK_SHIFT) <= lax.shift_right_logical(row, CHUNK_SHIFT)
            s = jnp.where(seen, s, NEG_INF)
        scores.append(s)
    m = jnp.max(scores[0], axis=1, keepdims=True)
    for s in scores[1:]:
        m = jnp.maximum(m, jnp.max(s, axis=1, keepdims=True))
    ps = [jnp.exp(s - m) for s in scores]
    total = jnp.sum(ps[0], axis=1, keepdims=True)
    for p in ps[1:]:
        total = total + jnp.sum(p, axis=1, keepdims=True)
    inv = 1.0 / total
    return [p * inv for p in ps]


def _per_query_block(qi, n_blocks, branch):
    for lvl in range(n_blocks):
        pl.when(qi == lvl)(lambda lvl=lvl: branch(lvl))


def _attn_specs(t, g):
    q = pl.BlockSpec((ATT_TQ, g * HEAD_PAD), lambda h, i: (i, h))
    kn = pl.BlockSpec((None, t, g * QK_NOPE), lambda h, i: (0, 0, h))
    kr = pl.BlockSpec((t, LANES), lambda h, i: (0, 0))
    v = pl.BlockSpec((None, t, g * V_HEAD), lambda h, i: (1, 0, h))
    o = pl.BlockSpec((ATT_TQ, g * V_HEAD), lambda h, i: (i, h))
    return q, kn, kr, v, o


def _attn_fwd(q, knv, kr):
    t = q.shape[0]

    def body(q_ref, kn_ref, kr_ref, v_ref, o_ref, k_ref):
        _fill_keys(k_ref, kn_ref, kr_ref)

        def branch(lvl):
            for hh in range(FWD_HEADS):
                vcols = slice(hh * V_HEAD, (hh + 1) * V_HEAD)
                ps = _attn_probs(q_ref[:, hh * HEAD_PAD:(hh + 1) * HEAD_PAD], k_ref.at[hh], lvl)
                o = None
                for p, (lo, hi, _) in zip(ps, _key_ranges(lvl)):
                    part = jnp.dot(p.astype(BF16), v_ref[lo:hi, vcols], preferred_element_type=F32)
                    o = part if o is None else o + part
                o_ref[:, vcols] = o.astype(BF16)

        _per_query_block(pl.program_id(1), t // ATT_TQ, branch)

    qs, kns, krs, vs, os_ = _attn_specs(t, FWD_HEADS)
    return _tc_call(
        body, name="attn_fwd", grid=(N_HEADS // FWD_HEADS, t // ATT_TQ), in_specs=[qs, kns, krs, vs],
        out_specs=os_, out_shape=jax.ShapeDtypeStruct((t, N_HEADS * V_HEAD), BF16),
        scratch_shapes=[pltpu.VMEM((FWD_HEADS, t, HEAD_PAD), BF16)], compiler_params=_cp("parallel", "arbitrary"),
    )(q, knv, kr, knv)


def _attn_bwd(q, knv, kr, do, cos, sin):
    t = q.shape[0]

    def body(q_ref, kn_ref, kr_ref, v_ref, do_ref, c_ref, s_ref, dq_ref, dknv_ref, dkr_ref, k_ref, dk_ref):
        h, qi = pl.program_id(0), pl.program_id(1)
        _fill_keys(k_ref, kn_ref, kr_ref)

        @pl.when(qi == 0)
        def _():
            dknv_ref[1] = jnp.zeros(dknv_ref.shape[1:], F32)
            dk_ref[...] = jnp.zeros_like(dk_ref)

        @pl.when((qi == 0) & (h == 0))
        def _():
            dkr_ref[...] = jnp.zeros_like(dkr_ref)

        def branch(lvl):
            ranges = _key_ranges(lvl)
            for hh in range(BWD_HEADS):
                qcols = slice(hh * HEAD_PAD, (hh + 1) * HEAD_PAD)
                vcols = slice(hh * V_HEAD, (hh + 1) * V_HEAD)
                qv, dov = q_ref[:, qcols], do_ref[:, vcols]
                ps = _attn_probs(qv, k_ref.at[hh], lvl)
                dps = [lax.dot_general(dov, v_ref[lo:hi, vcols], NT_DIMS, preferred_element_type=F32)
                       for lo, hi, _ in ranges]
                di = None
                for p, dp in zip(ps, dps):
                    part = jnp.sum(p * dp, axis=1, keepdims=True)
                    di = part if di is None else di + part
                dq = None
                for p, dp, (lo, hi, _) in zip(ps, dps, ranges):
                    ds = (p * (dp - di) * ATT_SCALE).astype(BF16)
                    part = jnp.dot(ds, k_ref[hh, lo:hi, :], preferred_element_type=F32)
                    dq = part if dq is None else dq + part
                    dk_ref[hh, lo:hi, :] += lax.dot_general(ds, qv, TN_DIMS, preferred_element_type=F32)
                    dknv_ref[1, lo:hi, vcols] += lax.dot_general(p.astype(BF16), dov, TN_DIMS,
                                                                 preferred_element_type=F32)
                dq_ref[:, hh * HEAD_PAD:hh * HEAD_PAD + QK_NOPE] = dq[:, :QK_NOPE].astype(BF16)
                dq_ref[:, hh * HEAD_PAD + QK_NOPE:(hh + 1) * HEAD_PAD] = _rope_bwd_math(
                    dq[:, QK_NOPE:], c_ref[...], s_ref[...]).astype(BF16)

        _per_query_block(qi, t // ATT_TQ, branch)

        @pl.when(qi == t // ATT_TQ - 1)
        def _():
            for hh in range(BWD_HEADS):
                dknv_ref[0, :, hh * QK_NOPE:(hh + 1) * QK_NOPE] = dk_ref[hh, :, :QK_NOPE]
                dkr_ref[...] += dk_ref[hh, :, QK_NOPE:]

    qs, kns, krs, vs, os_ = _attn_specs(t, BWD_HEADS)
    tab = pl.BlockSpec((ATT_TQ, LANES), lambda h, i: (i, 0))
    return _tc_call(
        body, name="attn_bwd", grid=(N_HEADS // BWD_HEADS, t // ATT_TQ), in_specs=[qs, kns, krs, vs, os_, tab, tab],
        out_specs=[qs, pl.BlockSpec((2, t, BWD_HEADS * QK_NOPE), lambda h, i: (0, 0, h)), krs],
        out_shape=[jax.ShapeDtypeStruct((t, N_HEADS * HEAD_PAD), BF16),
                   jax.ShapeDtypeStruct((2, t, N_HEADS * QK_NOPE), F32), jax.ShapeDtypeStruct((t, LANES), F32)],
        scratch_shapes=[pltpu.VMEM((BWD_HEADS, t, HEAD_PAD), BF16), pltpu.VMEM((BWD_HEADS, t, HEAD_PAD), F32)],
        compiler_params=_cp("arbitrary", "arbitrary"),
    )(q, knv, kr, knv, do, cos, sin)


def _adam_math(w, g, m, v):
    nm = ADAM_B1 * m + (1.0 - ADAM_B1) * g
    nv = ADAM_B2 * v + (1.0 - ADAM_B2) * (g * g)
    m_hat = nm / (1.0 - ADAM_B1 ** ADAM_STEP)
    v_hat = nv / (1.0 - ADAM_B2 ** ADAM_STEP)
    return -ADAM_LR * (m_hat / (jnp.sqrt(v_hat) + ADAM_EPS) + ADAM_WD * w), nm, nv


def _adamw_small(w, g, m, v):
    def body(w_ref, g_ref, m_ref, v_ref, d_ref, nm_ref, nv_ref):
        d_ref[...], nm_ref[...], nv_ref[...] = _adam_math(w_ref[...], g_ref[...], m_ref[...], v_ref[...])

    shp = jax.ShapeDtypeStruct(w.shape, F32)
    return _tc_call(body, name="adamw_small", out_shape=[shp] * 3)(w, g, m, v)


ADAM_SPLIT = 4


def _adamw_shards(ids, items, name):
    n = len(items)

    def body(ids_ref, *refs):
        outs = refs[len(refs) - 4 * n:]
        mine = pl.program_id(0) == ids_ref[0]
        for i in range(n):
            w_ref, m_ref, v_ref, gm_ref, gs_ref = refs[5 * i:5 * i + 5]
            g_ref, d_ref, nm_ref, nv_ref = outs[4 * i:4 * i + 4]

            @pl.when(mine)
            def _(g_ref=g_ref, gm_ref=gm_ref):
                g_ref[...] = gm_ref[...]

            @pl.when(jnp.logical_not(mine))
            def _(g_ref=g_ref, gs_ref=gs_ref):
                g_ref[...] = gs_ref[...]

            d_ref[...], nm_ref[...], nv_ref[...] = _adam_math(w_ref[...], g_ref[...], m_ref[...], v_ref[...])

    in_specs, out_specs, out_shape, args, carried, aliases = [], [], [], [ids], [], {}
    for i, it in enumerate(items):
        w = it["w"]
        r, c = w.shape[-2:]
        tr = r // 2 // ADAM_SPLIT
        assert tr % 8 == 0, (name, w.shape)
        layer = it.get("layer")
        if layer is None:
            wspec = pl.BlockSpec((tr, c), lambda h, k, ids: (h * ADAM_SPLIT + k, 0))
        else:
            wspec = pl.BlockSpec((None, tr, c), lambda h, k, ids, layer=layer: (layer, h * ADAM_SPLIT + k, 0))
        gspec = pl.BlockSpec((tr, c), lambda h, k, ids: (k, 0))
        in_specs += [wspec] * 3 + [gspec] * 2
        args += [w, it["m"], it["v"], it["g_mine"], it["g_sib"]]
        out_specs += [wspec] * 4
        out_shape += [jax.ShapeDtypeStruct(w.shape, F32)] * 4
        if it.get("prev") is not None:
            for k, p in enumerate(it["prev"]):
                aliases[1 + 5 * n + len(carried)] = 4 * i + k
                carried.append(p)
    res = _tc_call(
        body, name=name, prefetch=1, grid=(2, ADAM_SPLIT), in_specs=in_specs + [ANY] * len(carried),
        out_specs=out_specs, out_shape=out_shape, input_output_aliases=aliases,
        compiler_params=_cp("parallel", "parallel"),
    )(*args, *carried)
    return [res[4 * i:4 * i + 4] for i in range(n)]


def _peer_chip(k_me, j):
    return k_me ^ jnp.where(j == 0, 2, jnp.where(j == 1, 1, 3))


def _pair_sums(ids, gs, ras, name):
    n = len(gs)

    def body(ids_ref, *refs):
        for i in range(n):
            g_ref, ra_ref, o_ref = refs[2 * i], refs[2 * i + 1], refs[2 * n + i]
            o_ref[...] = (g_ref[...].astype(F32) + ra_ref[...].astype(F32)).astype(BF16)

    in_specs, out_specs, out_shape = [], [], []
    for g in gs:
        half, c = g.shape[1] // 2, g.shape[2]
        in_specs += [pl.BlockSpec((None, half, c), lambda j, ids: (_peer_chip(ids[1], j), ids[0], 0)),
                     pl.BlockSpec((None, half, c), lambda j, ids: (_peer_chip(ids[1], j), 0, 0))]
        out_specs.append(pl.BlockSpec((None, half, c), lambda j, ids: (j, 0, 0)))
        out_shape.append(jax.ShapeDtypeStruct((3, half, c), BF16))
    return _tc_call(
        body, name=name, prefetch=1, grid=(3,), in_specs=in_specs, out_specs=out_specs, out_shape=out_shape,
        compiler_params=_cp("parallel"),
    )(ids, *[a for pair in zip(gs, ras) for a in pair])


def _chip_sums(ids, gs, ras, rbs, name):
    n = len(gs)

    def body(ids_ref, *refs):
        for i in range(n):
            g_ref, ra_ref, rb_ref, o_ref = refs[3 * i], refs[3 * i + 1], refs[3 * i + 2], refs[3 * n + i]
            acc = g_ref[...].astype(F32) + ra_ref[...].astype(F32)
            for j in range(3):
                acc = acc + rb_ref[j].astype(F32)
            o_ref[...] = acc

    in_specs, out_specs, out_shape = [], [], []
    for g in gs:
        half, c = g.shape[1] // 2, g.shape[2]
        in_specs += [pl.BlockSpec((None, half, c), lambda i, ids: (ids[1], ids[0], 0)),
                     pl.BlockSpec((None, half, c), lambda i, ids: (ids[1], 0, 0)),
                     pl.BlockSpec((3, half, c), lambda i, ids: (0, 0, 0))]
        out_specs.append(pl.BlockSpec((half, c), lambda i, ids: (0, 0)))
        out_shape.append(jax.ShapeDtypeStruct((half, c), F32))
    return _tc_call(
        body, name=name, prefetch=1, grid=(1,), in_specs=in_specs, out_specs=out_specs, out_shape=out_shape,
        compiler_params=_cp("arbitrary"),
    )(ids, *[a for trio in zip(gs, ras, rbs) for a in trio])


def _position():
    x, y, c = lax.axis_index("x"), lax.axis_index("y"), lax.axis_index("c")
    chips = [(1 - x, y), (x, 1 - y), (1 - x, 1 - y)]
    return x, y, c, chips


def _shard_half(ref, wm, h):
    if wm.kind == "tiny":
        return ref
    if wm.nl == 2:
        return ref.at[h]
    return ref.at[pl.ds(pl.multiple_of(h * (wm.k // 2), 16), wm.k // 2), :]


def _region(full, wm, s, h):
    if wm.kind == "tiny":
        return full.at[s]
    cols = pl.ds(pl.multiple_of(s * wm.n, LANES), wm.n) if wm.kind == "col" else slice(None)
    if wm.nl == 2:
        rows = pl.ds(pl.multiple_of(s * wm.k, 16), wm.k) if wm.kind == "row" else slice(None)
        return full.at[slice(None) if h is None else h, rows, cols]
    if wm.kind == "col":
        rows = slice(None) if h is None else pl.ds(pl.multiple_of(h * (wm.k // 2), 16), wm.k // 2)
    elif h is None:
        rows = pl.ds(pl.multiple_of(s * wm.k, 16), wm.k)
    else:
        rows = pl.ds(pl.multiple_of(s * wm.k + h * (wm.k // 2), 16), wm.k // 2)
    return full.at[rows, cols]


def _full_shape(wm):
    if wm.kind == "tiny":
        return (N_CHIPS, wm.k, wm.n)
    shape = (wm.k, N_CHIPS * wm.n) if wm.kind == "col" else (N_CHIPS * wm.k, wm.n)
    return shape if wm.nl == 1 else (wm.nl,) + shape


def _handshake(peers):
    barrier = pltpu.get_barrier_semaphore()
    for peer in peers:
        pl.semaphore_signal(barrier, inc=1, device_id=peer, device_id_type=MESH)
    pl.semaphore_wait(barrier, len(peers))


def _all_gather_group(gi, shards):
    wms = AG_GROUPS[gi]
    nw = len(wms)

    def body(*refs):
        sh, full = refs[:nw], refs[nw:2 * nw]
        ici_s, ici_r, pass_s, pass_r, own_s, own_r = refs[2 * nw:]
        x, y, c, chips = _position()
        me, sibling = 2 * x + y, (x, y, 1 - c)
        _handshake([(*chip, c) for chip in chips] + [sibling])

        def rcopy(src, dst, s_sem, r_sem, to):
            return pltpu.make_async_remote_copy(src_ref=src, dst_ref=dst, send_sem=s_sem, recv_sem=r_sem,
                                                device_id=to, device_id_type=MESH)

        started = []
        for i, wm in enumerate(wms):
            for j, chip in enumerate(chips):
                started.append(rcopy(_shard_half(sh[i], wm, c), _region(full[i], wm, me, c),
                                     ici_s.at[i, j], ici_r.at[i, j], (*chip, c)))
                started[-1].start()
            started.append(rcopy(sh[i], _region(full[i], wm, me, None), own_s.at[i], own_r.at[i], sibling))
            started[-1].start()
        for i, wm in enumerate(wms):
            for j, chip in enumerate(chips):
                got = _region(full[i], wm, 2 * chip[0] + chip[1], c)
                rcopy(got, got, ici_s.at[i, j], ici_r.at[i, j], sibling).wait_recv()
                if wm.kind != "tiny":
                    started.append(rcopy(got, got, pass_s.at[i, j], pass_r.at[i, j], sibling))
                    started[-1].start()
        for i, wm in enumerate(wms):
            mine = _region(full[i], wm, me, None)
            rcopy(mine, mine, own_s.at[i], own_r.at[i], sibling).wait_recv()
            for j, chip in enumerate(chips):
                if wm.kind != "tiny":
                    got = _region(full[i], wm, 2 * chip[0] + chip[1], 1 - c)
                    rcopy(got, got, pass_s.at[i, j], pass_r.at[i, j], sibling).wait_recv()
        for cp in started:
            cp.wait_send()

    return pl.kernel(
        body, out_type=[jax.ShapeDtypeStruct(_full_shape(wm), s.dtype) for wm, s in zip(wms, shards)],
        mesh=plsc.ScalarSubcoreMesh(axis_name="sequencer", num_cores=1), name=f"ag_group{gi}",
        scratch_types=[pltpu.SemaphoreType.DMA((nw, 3))] * 4 + [pltpu.SemaphoreType.DMA((nw,))] * 2,
        compiler_params=pltpu.CompilerParams(collective_id=gi),
    )(*shards)


def _sequencer_call(body, name, cid, out_types, scratch, args):
    return pl.kernel(
        body, out_type=out_types, mesh=plsc.ScalarSubcoreMesh(axis_name="sequencer", num_cores=1), name=name,
        scratch_types=scratch, compiler_params=pltpu.CompilerParams(collective_id=cid),
    )(*args)


def _pair_exchange(gs, tag, cid):
    n = len(gs)

    def body(*refs):
        g, out, send_sems, recv_sems = refs[:n], refs[n:2 * n], refs[2 * n], refs[2 * n + 1]
        x, y, c, _ = _position()
        _handshake([(x, y, 1 - c)])
        cps = []
        for i in range(n):
            half = g[i].shape[1] // 2
            cps.append(pltpu.make_async_remote_copy(
                src_ref=g[i].at[:, pl.ds(pl.multiple_of((1 - c) * half, 16), half), :], dst_ref=out[i],
                send_sem=send_sems.at[i], recv_sem=recv_sems.at[i], device_id=(x, y, 1 - c), device_id_type=MESH))
            cps[-1].start()
        for cp in cps:
            cp.wait()

    return _sequencer_call(
        body, f"rs_pair_exchange{tag}", cid,
        [jax.ShapeDtypeStruct((a.shape[0], a.shape[1] // 2, a.shape[2]), a.dtype) for a in gs],
        [pltpu.SemaphoreType.DMA((n,)), pltpu.SemaphoreType.DMA((n,))], gs)


def _chip_exchange(ss, tag, cid):
    n = len(ss)

    def body(*refs):
        s, out, send_sems, recv_sems = refs[:n], refs[n:2 * n], refs[2 * n], refs[2 * n + 1]
        x, y, c, chips = _position()
        _handshake([(*chip, c) for chip in chips])
        cps = []
        for i in range(n):
            for j, chip in enumerate(chips):
                cps.append(pltpu.make_async_remote_copy(
                    src_ref=s[i].at[j], dst_ref=out[i].at[j], send_sem=send_sems.at[i, j], recv_sem=recv_sems.at[i, j],
                    device_id=(*chip, c), device_id_type=MESH))
                cps[-1].start()
        for cp in cps:
            cp.wait()

    return _sequencer_call(
        body, f"rs_chip_exchange{tag}", cid, [jax.ShapeDtypeStruct(a.shape, a.dtype) for a in ss],
        [pltpu.SemaphoreType.DMA((n, 3)), pltpu.SemaphoreType.DMA((n, 3))], ss)


def _pair_swap(g8s, tag, cid):
    n = len(g8s)

    def body(*refs):
        g, out, send_sems, recv_sems = refs[:n], refs[n:2 * n], refs[2 * n], refs[2 * n + 1]
        x, y, c, _ = _position()
        _handshake([(x, y, 1 - c)])
        cps = []
        for i in range(n):
            cps.append(pltpu.make_async_remote_copy(
                src_ref=g[i], dst_ref=out[i], send_sem=send_sems.at[i], recv_sem=recv_sems.at[i],
                device_id=(x, y, 1 - c), device_id_type=MESH))
            cps[-1].start()
        for cp in cps:
            cp.wait()

    return _sequencer_call(
        body, f"rs_pair_swap{tag}", cid, [jax.ShapeDtypeStruct(a.shape, a.dtype) for a in g8s],
        [pltpu.SemaphoreType.DMA((n,)), pltpu.SemaphoreType.DMA((n,))], g8s)


def _all_reduce_small(vec, name):
    r, cols = vec.shape

    def body(v_ref, o_ref, gath, send_sems, recv_sems):
        x, y, c, _ = _position()
        me = 4 * x + 2 * y + c
        gath[me] = v_ref[...]
        cps = []
        for rel in range(1, N_DEV):
            peer = (x ^ (rel >> 2), y ^ ((rel >> 1) & 1), c ^ (rel & 1))
            cps.append(pltpu.make_async_remote_copy(
                src_ref=v_ref, dst_ref=gath.at[me], send_sem=send_sems.at[rel - 1], recv_sem=recv_sems.at[rel - 1],
                device_id=peer, device_id_type=MESH))
        for cp in cps:
            cp.start()
        for rel in range(1, N_DEV):
            pltpu.make_async_remote_copy(
                src_ref=v_ref, dst_ref=gath.at[me ^ rel], send_sem=send_sems.at[rel - 1],
                recv_sem=recv_sems.at[rel - 1], device_id=(x, y, c), device_id_type=MESH).wait_recv()
        for cp in cps:
            cp.wait_send()
        acc = gath[0]
        for d in range(1, N_DEV):
            acc = acc + gath[d]
        o_ref[...] = acc

    vm = pl.BlockSpec(memory_space=pltpu.VMEM)
    return _tc_call(
        body, name=name, in_specs=[vm], out_specs=vm, out_shape=jax.ShapeDtypeStruct((r, cols), F32),
        scratch_shapes=[pltpu.VMEM((N_DEV, r, cols), F32), pltpu.SemaphoreType.DMA((N_DEV - 1,)),
                        pltpu.SemaphoreType.DMA((N_DEV - 1,))],
    )(vec)


def _rope_tables(positions):
    half = QK_ROPE // 2
    inv_freq = 1.0 / (ROPE_THETA ** (jnp.arange(half, dtype=F32) / half))
    ang = positions.astype(F32)[:, None] * inv_freq
    zeros = jnp.zeros((positions.shape[0], LANES - QK_ROPE), F32)
    cos, sin = jnp.cos(ang), jnp.sin(ang)
    return jnp.concatenate([cos, cos, zeros], axis=1), jnp.concatenate([sin, sin, zeros], axis=1)


def _local_step(x, positions, tgt, wf, small, rs):
    cos, sin = _rope_tables(positions)
    w_in, w_out = wf["sc_w_in"], wf["sc_w_out"]
    w_ups, w_downs = (wf["ffn_w_up0"], wf["ffn_w_up1"]), (wf["ffn_w_down0"], wf["ffn_w_down1"])
    w_kv, w_ukv, w_dq, w_uq, w_o = wf["w_kv"], wf["w_ukv"], wf["w_dq"], wf["w_uq"], wf["w_o"]
    attn_norm, ffn_norm = small["attn_norm"], small["ffn_norm"]
    conv_b = small["ffn_conv_b"]

    def ffn_fwd(h, hf, l, then):
        up, a = _ffn_up_gate(hf, w_ups[l], small["ffn_conv_w"][l], conv_b[l:l + 1], f"ffn{l}_up_gate")
        return then(a, w_downs[l], h), (hf, up, a)

    def ffn_bwd(h, dh_out, dh_out_b, l, saved, gi, hooks):
        run = lambda stage: hooks.get(stage, lambda: None)()
        hf, up, a = saved
        da = _nt(f"ffn{l}_down_dx", dh_out_b, w_downs[l], BF16)
        run("down_dx")
        d_down = _tn(f"ffn{l}_down_dw", a, dh_out_b, BF16)
        dup, d_cw, d_cb = _gate_bwd(up, small["ffn_conv_w"][l], conv_b[l:l + 1], da, f"ffn{l}_gate_bwd")
        run("gate_bwd")
        d_up = _dw_ffn_up(f"ffn{l}_up_dw", hf, dup)
        rs.start(gi, {f"ffn_w_down{l}": d_down.reshape(N_CHIPS, F_FF // N_CHIPS, D), f"ffn_w_up{l}": d_up})
        dh, dh_b, d_norm = _dx_norm_bwd(f"ffn{l}_up_dx", dup, w_ups[l], h, ffn_norm[l:l + 1], dh_out)
        run("up_dx")
        return dh, dh_b, d_cw, d_cb, d_norm

    hn0 = _rms_fwd(x, attn_norm[0:1], "attn0_norm")
    z = _nn_parts("sc_in", hn0, w_in, 3, BF16)
    mix = _scmix_fwd(z, small["sc_conv_w"])
    h1, hf0 = _nn_add_norm("sc_out", mix, w_out, x, ffn_norm[0:1])
    h2, ffn0_saved = ffn_fwd(h1, hf0, 0, lambda a, w, h: _nn("ffn0_down", a, w, F32, add=h))

    hn1, hk, cq_pre, cq, q, kvpre, ckv, kr, knv = _attn_prep(
        h2, attn_norm[1:2], small["kv_in_norm"], w_dq, small["q_latent_norm"], w_uq, w_kv, small["kv_latent_norm"],
        w_ukv, cos, sin)
    o = _attn_fwd(q, knv, kr)
    h3, hf1 = _nn_add_norm("attn_out", o, w_o, h2, ffn_norm[1:2])
    (loss, dh4, dh4_b, d_final), ffn1_saved = ffn_fwd(
        h3, hf1, 1, lambda a, w, h: _nn_add_loss("ffn1_down_loss", a, w, h, small["final_norm"], tgt))

    rows = D // N_CHIPS
    dh3, dh3_b, d_cw1, d_cb1, d_fn1 = ffn_bwd(h3, dh4, dh4_b, 1, ffn1_saved, 0, {})

    do = _nt("attn_out_dx", dh3_b, w_o, BF16)
    d_wo = _tn("attn_out_dw", o, dh3_b, BF16)
    rs.pair_sums(0)
    dq, dknv, dkr = _attn_bwd(q, knv, kr, do, cos, sin)
    rs.chip_sums(0)
    dh2, dh2_b, d_wuq, d_wdq, d_wukv, d_wkv, d_an1, d_kvin, d_qln, d_kvln = _attn_prep_bwd(
        dq, dknv, dkr, dh3, h2, hn1, hk, cq_pre, cq, kvpre, ckv, attn_norm[1:2], small["kv_in_norm"], w_dq,
        small["q_latent_norm"], w_uq, w_kv, small["kv_latent_norm"], w_ukv, cos, sin)
    rs.finish(0)
    by_owner = lambda dw: dw.reshape(dw.shape[0], N_CHIPS, -1).transpose(1, 0, 2)
    rs.start(1, {
        "w_o": d_wo.reshape(N_CHIPS, rows, D), "w_uq": by_owner(d_wuq), "w_dq": d_wdq.reshape(N_CHIPS, rows, Q_LORA),
        "w_ukv": by_owner(d_wukv.reshape(2 * KV_LORA, -1)).reshape(N_CHIPS, 2 * KV_LORA, -1),
        "w_kv": d_wkv.reshape(N_CHIPS, rows, KVP),
    })

    dh1, dh1_b, d_cw0, d_cb0, d_fn0 = ffn_bwd(h1, dh2, dh2_b, 0, ffn0_saved, 2, {
        "down_dx": lambda: rs.pair_sums(1), "gate_bwd": lambda: rs.chip_sums(1), "up_dx": lambda: rs.finish(1)})
    rs.pair_sums(2)

    d_wout = _tn("sc_out_dw", mix, dh1_b, BF16)
    dmix = _nt("sc_out_dx", dh1_b, w_out, BF16)
    dz, d_scw = _scmix_bwd(z, small["sc_conv_w"], dmix)
    d_win = _dw_sc_in(hn0, dz)
    rs.start(3, {"sc_w_out": d_wout.reshape(N_CHIPS, rows, D), "sc_w_in": d_win})
    dx, _, d_an0 = _dx_norm_bwd("sc_in_dx", dz, w_in, x, attn_norm[0:1], dh1)

    small_g = {
        "attn_norm": jnp.concatenate([d_an0, d_an1]), "ffn_norm": jnp.concatenate([d_fn0, d_fn1]),
        "final_norm": d_final, "kv_in_norm": d_kvin, "kv_latent_norm": d_kvln, "q_latent_norm": d_qln,
        "ffn_conv_b": jnp.concatenate([d_cb0, d_cb1]), "sc_conv_w": d_scw, "ffn_conv_w": jnp.stack([d_cw0, d_cw1]),
    }
    return loss, dx, small_g


RS_GROUPS = (("ffn_w_down1", "ffn_w_up1"), ("w_o", "w_uq", "w_dq", "w_ukv", "w_kv"),
             ("ffn_w_down0", "ffn_w_up0"), ("sc_w_out", "sc_w_in"))


class _ReduceScatter:
    def __init__(self, ids, finish):
        self.ids, self.grads, self.step, self.mine, self.sib, self.finish = ids, {}, {}, {}, {}, finish

    def _cid(self, gi):
        return len(AG_GROUPS) + 3 * gi

    def start(self, gi, grads):
        self.grads.update(grads)
        own = [grads[n] for n in RS_GROUPS[gi]]
        self.step[gi] = (own, _pair_exchange(own, gi, self._cid(gi)))

    def pair_sums(self, gi):
        own, ra = self.step[gi]
        sums = _pair_sums(self.ids, own, ra, f"rs_pair_sums{gi}")
        self.step[gi] = (own, ra, _chip_exchange(sums, gi, self._cid(gi) + 1))

    def chip_sums(self, gi):
        own, ra, rb = self.step[gi]
        mine = _chip_sums(self.ids, own, ra, rb, f"rs_chip_sums{gi}")
        self.mine.update(zip(RS_GROUPS[gi], mine))
        self.sib.update(zip(RS_GROUPS[gi], _pair_swap(mine, gi, self._cid(gi) + 2)))

SMALL_REPL = ("attn_norm", "ffn_norm", "final_norm", "kv_in_norm", "kv_latent_norm", "q_latent_norm", "ffn_conv_b")
SMALL_SHARDED = ("sc_conv_w", "ffn_conv_w")
SMALL_ROWS = 256


def _pad_heads(w_uq):
    per_head = w_uq.reshape(Q_LORA, -1, QK_NOPE + QK_ROPE)
    return jnp.pad(per_head, ((0, 0), (0, 0), (0, HEAD_PAD - QK_NOPE - QK_ROPE))).reshape(Q_LORA, -1)


def _pack_kv(w_dkv, w_kr):
    return jnp.concatenate([w_dkv, w_kr, jnp.zeros((w_kr.shape[0], LANES - QK_ROPE), w_kr.dtype)], axis=1)


def kernel(x, positions, attn_norm, ffn_norm, final_norm, sc_w_in, sc_conv_w, sc_w_out, kv_in_norm, w_dkv, kv_latent_norm, w_kr, w_uk, w_uv, w_dq, q_latent_norm, w_uq, w_o, ffn_w_up, ffn_conv_w, ffn_conv_b, ffn_w_down, loss_target, m_attn_norm, m_ffn_norm, m_final_norm, m_sc_w_in, m_sc_conv_w, m_sc_w_out, m_kv_in_norm, m_w_dkv, m_kv_latent_norm, m_w_kr, m_w_uk, m_w_uv, m_w_dq, m_q_latent_norm, m_w_uq, m_w_o, m_ffn_w_up, m_ffn_conv_w, m_ffn_conv_b, m_ffn_w_down, v_attn_norm, v_ffn_norm, v_final_norm, v_sc_w_in, v_sc_conv_w, v_sc_w_out, v_kv_in_norm, v_w_dkv, v_kv_latent_norm, v_w_kr, v_w_uk, v_w_uv, v_w_dq, v_q_latent_norm, v_w_uq, v_w_o, v_ffn_w_up, v_ffn_conv_w, v_ffn_conv_b, v_ffn_w_down):
    names = ("attn_norm", "ffn_norm", "final_norm", "sc_w_in", "sc_conv_w", "sc_w_out", "kv_in_norm", "w_dkv",
             "kv_latent_norm", "w_kr", "w_uk", "w_uv", "w_dq", "q_latent_norm", "w_uq", "w_o", "ffn_w_up",
             "ffn_conv_w", "ffn_conv_b", "ffn_w_down")
    w = dict(zip(names, (attn_norm, ffn_norm, final_norm, sc_w_in, sc_conv_w, sc_w_out, kv_in_norm, w_dkv,
                         kv_latent_norm, w_kr, w_uk, w_uv, w_dq, q_latent_norm, w_uq, w_o, ffn_w_up,
                         ffn_conv_w, ffn_conv_b, ffn_w_down)))
    m = dict(zip(names, (m_attn_norm, m_ffn_norm, m_final_norm, m_sc_w_in, m_sc_conv_w, m_sc_w_out, m_kv_in_norm,
                         m_w_dkv, m_kv_latent_norm, m_w_kr, m_w_uk, m_w_uv, m_w_dq, m_q_latent_norm, m_w_uq, m_w_o,
                         m_ffn_w_up, m_ffn_conv_w, m_ffn_conv_b, m_ffn_w_down)))
    v = dict(zip(names, (v_attn_norm, v_ffn_norm, v_final_norm, v_sc_w_in, v_sc_conv_w, v_sc_w_out, v_kv_in_norm,
                         v_w_dkv, v_kv_latent_norm, v_w_kr, v_w_uk, v_w_uv, v_w_dq, v_q_latent_norm, v_w_uq, v_w_o,
                         v_ffn_w_up, v_ffn_conv_w, v_ffn_conv_b, v_ffn_w_down)))

    _ORDER[0] = None
    ix, iy, ic = lax.axis_index("x"), lax.axis_index("y"), lax.axis_index("c")
    chip = 2 * ix + iy
    ids = jnp.stack([ic, chip]).astype(jnp.int32)

    def shards_of(t):
        return {
            "sc_w_in": t["sc_w_in"][0], "sc_w_out": t["sc_w_out"][0], "ffn_w_up": t["ffn_w_up"],
            "ffn_w_down": t["ffn_w_down"], "w_kv": _pack_kv(t["w_dkv"], t["w_kr"]),
            "w_ukv": jnp.stack([t["w_uk"], t["w_uv"]]), "w_dq": t["w_dq"][0], "w_uq": _pad_heads(t["w_uq"][0]),
            "w_o": t["w_o"][0],
        }

    ws, ms, vs = shards_of(w), shards_of(m), shards_of(v)

    def ag_shard(name):
        if name == "sc_conv_w":
            return sc_conv_w[0]
        if name == "ffn_conv_w":
            return ffn_conv_w.reshape(6, -1)
        if name[:-1] in ("ffn_w_up", "ffn_w_down"):
            return ws[name[:-1]][int(name[-1])].astype(BF16)
        return ws[name].astype(BF16)

    wf = {}
    for gi, wms in enumerate(AG_GROUPS):
        fulls = _all_gather_group(gi, [ag_shard(wm.name) for wm in wms])
        wf.update({wm.name: f for wm, f in zip(wms, fulls)})
    small = {
        "attn_norm": attn_norm, "ffn_norm": ffn_norm, "final_norm": final_norm[None], "kv_in_norm": kv_in_norm[None],
        "kv_latent_norm": kv_latent_norm[None], "q_latent_norm": q_latent_norm, "ffn_conv_b": ffn_conv_b,
        "sc_conv_w": wf["sc_conv_w"].transpose(1, 0, 2).reshape(3, D),
        "ffn_conv_w": wf["ffn_conv_w"].reshape(N_CHIPS, 2, 3, -1).transpose(1, 2, 0, 3).reshape(2, 3, F_FF),
    }

    res = {}

    merged = lambda a: a.reshape(2 * KV_LORA, -1)

    def adamw_group(gi):
        items = []
        for key in RS_GROUPS[gi]:
            n, layer = (key[:-1], int(key[-1])) if key[:-1] in ("ffn_w_up", "ffn_w_down") else (key, None)
            w_, m_, v_ = (merged(t[n]) for t in (ws, ms, vs)) if n == "w_ukv" else (ws[n], ms[n], vs[n])
            items.append(dict(name=n, w=w_, m=m_, v=v_, g_mine=rs.mine[key], g_sib=rs.sib[key], layer=layer,
                              prev=res.get(n)))
        for it, out in zip(items, _adamw_shards(ids, items, f"adamw_group{gi}")):
            res[it["name"]] = out

    rs = _ReduceScatter(ids, adamw_group)
    loss, dx, small_g = _local_step(x[0], positions[0], loss_target[0], wf, small, rs)

    s_order = SMALL_REPL + SMALL_SHARDED
    flat = jnp.concatenate([small_g[n].reshape(-1) for n in s_order] + [loss.reshape(-1)])
    flat = jnp.pad(flat, (0, SMALL_ROWS * LANES - flat.shape[0])).reshape(SMALL_ROWS, LANES)
    red = _all_reduce_small(flat, "ar_small").reshape(-1)
    sg, off = {}, 0
    for n in s_order:
        sz = small_g[n].size
        sg[n] = red[off:off + sz].reshape(small_g[n].shape)
        off += sz
    loss_out = red[off]
    grads = {n: sg[n].reshape(w[n].shape) for n in SMALL_REPL}
    grads["sc_conv_w"] = lax.dynamic_slice_in_dim(sg["sc_conv_w"], chip * (D // N_CHIPS), D // N_CHIPS, axis=1)[None]
    grads["ffn_conv_w"] = lax.dynamic_slice_in_dim(sg["ffn_conv_w"], chip * (F_FF // N_CHIPS), F_FF // N_CHIPS, axis=2)

    small_names = SMALL_REPL + SMALL_SHARDED

    def pack_small(tree):
        return jnp.concatenate([tree[n].reshape(-1) for n in small_names]).reshape(-1, LANES)

    small_res = _adamw_small(pack_small(w), pack_small(grads), pack_small(m), pack_small(v))
    rs.chip_sums(2)
    rs.pair_sums(3)
    rs.finish(2)
    rs.chip_sums(3)
    rs.finish(3)
    outs = [grads, {}, {}, {}]
    for k, dst in enumerate(outs):
        for n in ("sc_w_in", "sc_w_out", "w_dq", "w_o"):
            dst[n] = res[n][k][None]
        unpadded = res["w_uq"][k].reshape(Q_LORA, -1, HEAD_PAD)[:, :, :QK_NOPE + QK_ROPE]
        dst["w_uq"] = unpadded.reshape(w_uq.shape)
        dst["ffn_w_up"], dst["ffn_w_down"] = res["ffn_w_up"][k], res["ffn_w_down"][k]
        dst["w_dkv"], dst["w_kr"] = res["w_kv"][k][:, :KV_LORA], res["w_kv"][k][:, KV_LORA:KV_LORA + QK_ROPE]
        dst["w_uk"], dst["w_uv"] = res["w_ukv"][k][:KV_LORA], res["w_ukv"][k][KV_LORA:]
    grads, delta, new_m, new_v = outs
    for slab, dst in zip(small_res, (delta, new_m, new_v)):
        f, off = slab.reshape(-1), 0
        for n in small_names:
            dst[n] = f[off:off + w[n].size].reshape(w[n].shape)
            off += w[n].size

    _ORDER[0] = None
    return (loss_out, dx[None], *[grads[n] for n in names], *[delta[n] for n in names],
            *[new_m[n] for n in names], *[new_v[n] for n in names])
```

```python
from typing import NamedTuple

import jax
import jax.numpy as jnp
from jax import lax
from jax.experimental import pallas as pl
from jax.experimental.pallas import tpu as pltpu
from jax.experimental.pallas import tpu_sc as plsc

F32 = jnp.float32
BF16 = jnp.bfloat16

T = 2048
D = 1024
F_FF = 2816
N_HEADS = 8
QK_NOPE = 128
QK_ROPE = 64
V_HEAD = 128
Q_LORA = 384
KV_LORA = 256
CHUNK_SHIFT = 6
ROPE_THETA = 10000.0
EPS = 1e-6
NEG_INF = -1e30
HEAD_PAD = 256
KVP = KV_LORA + 128

ADAM_LR = 0.001
ADAM_B1 = 0.9
ADAM_B2 = 0.999
ADAM_EPS = 1e-08
ADAM_WD = 0.01
ADAM_STEP = 10

N_CHIPS = 4
N_DEV = 8
LANES = 128
TC = 256
V7X_VMEM_LIMIT = 56 * 1024 * 1024

MESH = pl.DeviceIdType.MESH
ANY = pl.BlockSpec(memory_space=pl.ANY)


class _W(NamedTuple):
    name: str
    kind: str
    nl: int
    k: int
    n: int


AG_GROUPS = (
    (_W("sc_w_in", "col", 1, D, 3 * D // N_CHIPS), _W("sc_conv_w", "tiny", 1, 3, D // N_CHIPS),
     _W("ffn_conv_w", "tiny", 1, 6, F_FF // N_CHIPS), _W("sc_w_out", "row", 1, D // N_CHIPS, D)),
    (_W("ffn_w_up0", "col", 1, D, 2 * F_FF // N_CHIPS),),
    (_W("ffn_w_down0", "row", 1, F_FF // N_CHIPS, D),),
    (_W("w_kv", "row", 1, D // N_CHIPS, KVP), _W("w_ukv", "col", 2, KV_LORA, N_HEADS * QK_NOPE // N_CHIPS),
     _W("w_dq", "row", 1, D // N_CHIPS, Q_LORA),
     _W("w_uq", "col", 1, Q_LORA, N_HEADS * HEAD_PAD // N_CHIPS),
     _W("w_o", "row", 1, N_HEADS * V_HEAD // N_CHIPS, D)),
    (_W("ffn_w_up1", "col", 1, D, 2 * F_FF // N_CHIPS), _W("ffn_w_down1", "row", 1, F_FF // N_CHIPS, D)),
)


def _cp(*sem):
    return pltpu.CompilerParams(dimension_semantics=sem, vmem_limit_bytes=V7X_VMEM_LIMIT)


_ORDER = [None]


def _tc_call(body, *, name, out_shape, in_specs=None, out_specs=None, grid=(), scratch_shapes=(), prefetch=0,
             input_output_aliases=None, compiler_params=None):
    def run(*args):
        specs = [pl.BlockSpec(memory_space=pltpu.VMEM)] * (len(args) - prefetch) if in_specs is None else list(in_specs)
        inner, dep = body, _ORDER[0]
        if dep is not None:
            unread = prefetch + len(specs)
            specs, args = specs + [ANY], (*args, dep)

            def inner(*refs):
                return body(*refs[:unread], *refs[unread + 1:])

        kwargs = dict(name=name, out_shape=out_shape, input_output_aliases=input_output_aliases or {},
                      compiler_params=compiler_params)
        if prefetch:
            kwargs["grid_spec"] = pltpu.PrefetchScalarGridSpec(
                num_scalar_prefetch=prefetch, grid=grid, in_specs=specs, out_specs=out_specs,
                scratch_shapes=scratch_shapes)
        else:
            kwargs.update(grid=grid, in_specs=specs, scratch_shapes=scratch_shapes)
            if out_specs is not None:
                kwargs["out_specs"] = out_specs
        out = pl.pallas_call(inner, **kwargs)(*args)
        _ORDER[0] = out[0] if isinstance(out, (list, tuple)) else out
        return out

    return run


def _tile(n, cands):
    for c in cands:
        if n % c == 0:
            return c
    raise ValueError(f"no tile for {n}")


NN_DIMS = (((1,), (0,)), ((), ()))
NT_DIMS = (((1,), (1,)), ((), ()))
TN_DIMS = (((0,), (0,)), ((), ()))
M_TILES = (1024, 512, 384, 256, 128)
N_TILES = (1408, 1024, 768, 512, 384, 256, 128)
MM_BLOCK_BYTES = 36 * 1024 * 1024


def _fit(m, n, block_bytes, m_tiles=M_TILES, n_tiles=N_TILES):
    for tm in [c for c in m_tiles if m % c == 0]:
        for tn in [c for c in n_tiles if n % c == 0]:
            if 2 * block_bytes(tm, tn) + 4 * tm * tn <= MM_BLOCK_BYTES:
                return tm, tn
    raise ValueError(f"no tiles for {m} x {n}")


def _size(x):
    return x.dtype.itemsize


def _mm(name, a, b, dims, grid, a_spec, b_spec, o_spec, o_sds, add=None, red=None, acc_shape=None):
    n_red = None if red is None else grid[red]

    def body(*refs):
        a_ref, b_ref = refs[0], refs[1]
        add_ref = refs[2] if add is not None else None
        o_ref = refs[3] if add is not None else refs[2]
        part = lax.dot_general(a_ref[...].astype(BF16), b_ref[...].astype(BF16), dims, preferred_element_type=F32)
        if red is None:
            if add is not None:
                part = part + add_ref[...]
            o_ref[...] = part.astype(o_ref.dtype)
            return
        acc_ref = refs[-1]
        r = pl.program_id(red)

        @pl.when(r == 0)
        def _():
            acc_ref[...] = part

        @pl.when(r > 0)
        def _():
            acc_ref[...] += part

        @pl.when(r == n_red - 1)
        def _():
            o_ref[...] = acc_ref[...].astype(o_ref.dtype)

    sem = tuple("arbitrary" if ax == red else "parallel" for ax in range(len(grid)))
    in_specs = [a_spec, b_spec] + ([o_spec] if add is not None else [])
    args = (a, b) + ((add,) if add is not None else ())
    return _tc_call(
        body, name=name, grid=grid, in_specs=in_specs, out_specs=o_spec, out_shape=o_sds,
        scratch_shapes=[] if red is None else [pltpu.VMEM(acc_shape, F32)], compiler_params=_cp(*sem),
    )(*args)


def _nn(name, a, b, out_dtype, add=None, lead=None):
    (m, k), n = a.shape, b.shape[-1]
    osz = jnp.dtype(out_dtype).itemsize + (4 if add is not None else 0)
    tm, tn = _fit(m, n, lambda tm, tn: tm * k * _size(a) + k * tn * _size(b) + tm * tn * osz)
    if lead is None:
        b_spec = pl.BlockSpec((k, tn), lambda i, j: (0, j))
    else:
        b_spec = pl.BlockSpec((None, k, tn), lambda i, j: (lead, 0, j))
    return _mm(name, a, b, NN_DIMS, (m // tm, n // tn), pl.BlockSpec((tm, k), lambda i, j: (i, 0)), b_spec,
               pl.BlockSpec((tm, tn), lambda i, j: (i, j)), jax.ShapeDtypeStruct((m, n), out_dtype), add=add)


def _nn_parts(name, a, b, parts, out_dtype, lead=None, stacked=False):
    m, k = a.shape
    c = b.shape[-1] if stacked else b.shape[-1] // parts
    osz = jnp.dtype(out_dtype).itemsize
    tm, tn = _fit(m, c, lambda tm, tn: tm * k * _size(a) + k * tn * _size(b) + tm * tn * osz)
    nb = c // tn
    if stacked:
        b_spec = pl.BlockSpec((None, k, tn), lambda i, p, j: (p, 0, j))
    elif lead is None:
        b_spec = pl.BlockSpec((k, tn), lambda i, p, j: (0, p * nb + j))
    else:
        b_spec = pl.BlockSpec((None, k, tn), lambda i, p, j: (lead, 0, p * nb + j))
    return _mm(name, a, b, NN_DIMS, (m // tm, parts, nb), pl.BlockSpec((tm, k), lambda i, p, j: (i, 0)), b_spec,
               pl.BlockSpec((None, tm, tn), lambda i, p, j: (p, i, j)), jax.ShapeDtypeStruct((parts, m, c), out_dtype))


def _nt(name, a, b, out_dtype, lead=None):
    (m, k), n = a.shape, b.shape[-2]
    osz = jnp.dtype(out_dtype).itemsize
    tm, tn = _fit(m, n, lambda tm, tn: tm * k * _size(a) + tn * k * _size(b) + tm * tn * osz)
    if lead is None:
        b_spec = pl.BlockSpec((tn, k), lambda i, j: (j, 0))
    else:
        b_spec = pl.BlockSpec((None, tn, k), lambda i, j: (lead, j, 0))
    return _mm(name, a, b, NT_DIMS, (m // tm, n // tn), pl.BlockSpec((tm, k), lambda i, j: (i, 0)), b_spec,
               pl.BlockSpec((tm, tn), lambda i, j: (i, j)), jax.ShapeDtypeStruct((m, n), out_dtype))


def _tn(name, a, b, out_dtype):
    (k, m), n = a.shape, b.shape[1]
    osz = jnp.dtype(out_dtype).itemsize
    tm, tn = _fit(m, n, lambda tm, tn: k * tm * _size(a) + k * tn * _size(b) + tm * tn * osz,
                  m_tiles=(512, 384, 256, 128), n_tiles=(n,) + N_TILES)
    return _mm(name, a, b, TN_DIMS, (m // tm, n // tn), pl.BlockSpec((k, tm), lambda i, j: (0, i)),
               pl.BlockSpec((k, tn), lambda i, j: (0, j)), pl.BlockSpec((tm, tn), lambda i, j: (i, j)),
               jax.ShapeDtypeStruct((m, n), out_dtype))


def _nn_add_norm(name, a, b, add, g):
    (m, k), n = a.shape, b.shape[1]
    tm = 512

    def body(a_ref, b_ref, add_ref, g_ref, h_ref, hn_ref):
        h = jnp.dot(a_ref[...], b_ref[...], preferred_element_type=F32) + add_ref[...]
        h_ref[...] = h
        hn_ref[...] = _rms_rows(h, g_ref[...]).astype(BF16)

    rows = lambda w: pl.BlockSpec((tm, w), lambda i: (i, 0))
    return _tc_call(
        body, name=name, grid=(m // tm,),
        in_specs=[rows(k), pl.BlockSpec((k, n), lambda i: (0, 0)), rows(n), pl.BlockSpec((1, n), lambda i: (0, 0))],
        out_specs=[rows(n), rows(n)],
        out_shape=[jax.ShapeDtypeStruct((m, n), F32), jax.ShapeDtypeStruct((m, n), BF16)], compiler_params=_cp("parallel"),
    )(a, b, add, g)


def _nn_add_loss(name, a, b, add, g, tgt):
    (m, k), n = a.shape, b.shape[1]
    tm = 512

    def body(a_ref, b_ref, add_ref, g_ref, t_ref, loss_ref, dh_ref, dhb_ref, dg_ref):
        xv = jnp.dot(a_ref[...], b_ref[...], preferred_element_type=F32) + add_ref[...]
        gv = g_ref[...]
        r = lax.rsqrt(jnp.mean(xv * xv, axis=1, keepdims=True) + EPS)
        err = xv * r * gv - t_ref[...]
        part = 0.5 * jnp.sum(jnp.mean(err * err, axis=1, keepdims=True), axis=0, keepdims=True)
        dx, dg = _rms_bwd_math(xv, gv, err * (1.0 / n))
        dh_ref[...] = dx
        dhb_ref[...] = dx.astype(BF16)

        @pl.when(pl.program_id(0) == 0)
        def _():
            dg_ref[...] = jnp.zeros_like(dg_ref)
            loss_ref[...] = jnp.zeros_like(loss_ref)

        dg_ref[...] += dg
        loss_ref[...] += jnp.broadcast_to(part, loss_ref.shape)

    rows = lambda w: pl.BlockSpec((tm, w), lambda i: (i, 0))
    vec = pl.BlockSpec((1, n), lambda i: (0, 0))
    return _tc_call(
        body, name=name, grid=(m // tm,),
        in_specs=[rows(k), pl.BlockSpec((k, n), lambda i: (0, 0)), rows(n), vec, rows(n)],
        out_specs=[pl.BlockSpec((1, LANES), lambda i: (0, 0)), rows(n), rows(n), vec],
        out_shape=[jax.ShapeDtypeStruct((1, LANES), F32), jax.ShapeDtypeStruct((m, n), F32),
                   jax.ShapeDtypeStruct((m, n), BF16), jax.ShapeDtypeStruct((1, n), F32)],
        compiler_params=_cp("arbitrary"),
    )(a, b, add, g, tgt)


def _dx_norm_bwd(name, a, b, x, g, add):
    parts, t, c = a.shape
    d = b.shape[0]
    tm = 256

    def body(a_ref, b_ref, x_ref, g_ref, add_ref, dx_ref, dxb_ref, dg_ref):
        dy = None
        for p in range(parts):
            part = lax.dot_general(a_ref[p], b_ref[:, p * c:(p + 1) * c], NT_DIMS, preferred_element_type=F32)
            dy = part if dy is None else dy + part
        dx, dg = _rms_bwd_math(x_ref[...], g_ref[...], dy)
        dx = dx + add_ref[...]
        dx_ref[...] = dx
        dxb_ref[...] = dx.astype(BF16)

        @pl.when(pl.program_id(0) == 0)
        def _():
            dg_ref[...] = jnp.zeros_like(dg_ref)

        dg_ref[...] += dg

    rows = pl.BlockSpec((tm, d), lambda i: (i, 0))
    vec = pl.BlockSpec((1, d), lambda i: (0, 0))
    return _tc_call(
        body, name=name, grid=(t // tm,),
        in_specs=[pl.BlockSpec((parts, tm, c), lambda i: (0, i, 0)), pl.BlockSpec(b.shape, lambda i: (0, 0)), rows, vec,
                  rows],
        out_specs=[rows, rows, vec],
        out_shape=[jax.ShapeDtypeStruct((t, d), F32), jax.ShapeDtypeStruct((t, d), BF16),
                   jax.ShapeDtypeStruct((1, d), F32)],
        compiler_params=_cp("arbitrary"),
    )(a, b, x, g, add)


def _dw_sc_in(hn, dz):
    t, tn, tm = hn.shape[0], TC, D
    per_part, per_chip = D // tn, 3 * D // N_CHIPS // tn
    return _mm("sc_in_dw", hn, dz, TN_DIMS, (D // tm, 3 * D // tn), pl.BlockSpec((t, tm), lambda i, j: (0, i)),
               pl.BlockSpec((None, t, tn), lambda i, j: (j // per_part, 0, j % per_part)),
               pl.BlockSpec((None, tm, tn), lambda i, j: (j // per_chip, i, j % per_chip)),
               jax.ShapeDtypeStruct((N_CHIPS, D, 3 * D // N_CHIPS), BF16))


def _dw_ffn_up(name, hf, dup):
    t, tm, ns = hf.shape[0], D, 2 * F_FF // N_CHIPS
    return _mm(name, hf, dup, TN_DIMS, (N_CHIPS, D // tm), pl.BlockSpec((t, tm), lambda s, i: (0, i)),
               pl.BlockSpec((None, t, ns), lambda s, i: (s // 2, 0, s % 2)),
               pl.BlockSpec((None, tm, ns), lambda s, i: (s, i, 0)), jax.ShapeDtypeStruct((N_CHIPS, D, ns), BF16))


def _rms_fwd(x, g, name):
    t, d = x.shape
    tr = 512

    def body(x_ref, g_ref, o_ref):
        xv = x_ref[...]
        r = lax.rsqrt(jnp.mean(xv * xv, axis=1, keepdims=True) + EPS)
        o_ref[...] = (xv * r * g_ref[...]).astype(o_ref.dtype)

    row = pl.BlockSpec((tr, d), lambda i: (i, 0))
    return _tc_call(
        body, name=name, grid=(t // tr,), in_specs=[row, pl.BlockSpec((1, d), lambda i: (0, 0))],
        out_specs=row, out_shape=jax.ShapeDtypeStruct((t, d), BF16), compiler_params=_cp("parallel"),
    )(x, g)


def _rms_bwd_math(xv, g, dy):
    r = lax.rsqrt(jnp.mean(xv * xv, axis=1, keepdims=True) + EPS)
    xh = xv * r
    gy = dy * g
    dx = r * (gy - xh * jnp.mean(gy * xh, axis=1, keepdims=True))
    dg = jnp.sum(dy * xh, axis=0, keepdims=True)
    return dx, dg


def _rot_half(x):
    lane = lax.broadcasted_iota(jnp.int32, x.shape, 1)
    return jnp.where((lane % QK_ROPE) < QK_ROPE // 2, -pltpu.roll(x, LANES - 32, axis=1),
                     pltpu.roll(x, 32, axis=1))


def _rope_fwd_math(x, cos, sin):
    return x * cos + _rot_half(x) * sin


def _rope_bwd_math(dy, cos, sin):
    return dy * cos - _rot_half(dy * sin)


def _rms_rows(x, g):
    return x * lax.rsqrt(jnp.mean(x * x, axis=1, keepdims=True) + EPS) * g


def _attn_prep(h, g_attn, g_kvin, w_dq, g_ql, w_uq, w_kv, g_kvl, w_ukv, cos, sin):
    t, d = h.shape
    tr = 256
    wq = N_HEADS * HEAD_PAD

    def body(h_ref, ga_ref, gk_ref, wdq_ref, gq_ref, wuq_ref, wkv_ref, gl_ref, wukv_ref, c_ref, s_ref,
             hn_ref, hk_ref, cqp_ref, cq_ref, q_ref, kvp_ref, ckv_ref, kr_ref, knv_ref):
        xv, cv, sv = h_ref[...], c_ref[...], s_ref[...]
        xh = xv * lax.rsqrt(jnp.mean(xv * xv, axis=1, keepdims=True) + EPS)
        hn = (xh * ga_ref[...]).astype(BF16)
        hk = (xh * gk_ref[...]).astype(BF16)
        hn_ref[...], hk_ref[...] = hn, hk
        cq_pre = jnp.dot(hn, wdq_ref[...], preferred_element_type=F32)
        cqp_ref[...] = cq_pre
        cq = _rms_rows(cq_pre, gq_ref[...]).astype(BF16)
        cq_ref[...] = cq
        for hd in range(N_HEADS):
            lo = hd * HEAD_PAD
            qh = jnp.dot(cq, wuq_ref[:, lo:lo + HEAD_PAD], preferred_element_type=F32)
            q_ref[:, lo:lo + QK_NOPE] = qh[:, :QK_NOPE].astype(BF16)
            q_ref[:, lo + QK_NOPE:lo + HEAD_PAD] = _rope_fwd_math(qh[:, QK_NOPE:], cv, sv).astype(BF16)
        kvpre = jnp.dot(hk, wkv_ref[...], preferred_element_type=F32)
        kvp_ref[...] = kvpre
        ckv = _rms_rows(kvpre[:, :KV_LORA], gl_ref[...]).astype(BF16)
        ckv_ref[...] = ckv
        kr_ref[...] = _rope_fwd_math(kvpre[:, KV_LORA:], cv, sv).astype(BF16)
        for p in range(2):
            knv_ref[p] = jnp.dot(ckv, wukv_ref[p], preferred_element_type=F32).astype(BF16)

    rows = lambda w: pl.BlockSpec((tr, w), lambda i: (i, 0))
    whole = lambda a: pl.BlockSpec(a.shape, lambda i: (0,) * a.ndim)
    sds = lambda w, dt: jax.ShapeDtypeStruct((t, w), dt)
    args = (h, g_attn, g_kvin, w_dq, g_ql, w_uq, w_kv, g_kvl, w_ukv, cos, sin)
    return _tc_call(
        body, name="attn_prep", grid=(t // tr,),
        in_specs=[rows(d)] + [whole(a) for a in args[1:9]] + [rows(LANES), rows(LANES)],
        out_specs=[rows(d), rows(d), rows(Q_LORA), rows(Q_LORA), rows(wq), rows(KVP), rows(KV_LORA), rows(LANES),
                   pl.BlockSpec((2, tr, N_HEADS * QK_NOPE), lambda i: (0, i, 0))],
        out_shape=[sds(d, BF16), sds(d, BF16), sds(Q_LORA, F32), sds(Q_LORA, BF16), sds(wq, BF16), sds(KVP, F32),
                   sds(KV_LORA, BF16), sds(LANES, BF16), jax.ShapeDtypeStruct((2, t, N_HEADS * QK_NOPE), BF16)],
        compiler_params=_cp("parallel"),
    )(*args)


def _attn_prep_bwd(dq, dknv, dkr, dh, h, hn, hk, cq_pre, cq, kvpre, ckv, g_attn, g_kvin, w_dq, g_ql, w_uq, w_kv, g_kvl,
                   w_ukv, cos, sin):
    t, d = h.shape
    tr = 256
    n_steps = t // tr
    wq = N_HEADS * HEAD_PAD
    wk = N_HEADS * QK_NOPE

    def body(dq_ref, dknv_ref, dkr_ref, dh_ref, h_ref, hn_ref, hk_ref, cqp_ref, cq_ref, kvp_ref, ckv_ref,
             ga_ref, gk_ref, wdq_ref, gq_ref, wuq_ref, wkv_ref, gl_ref, wukv_ref, c_ref, s_ref,
             dho_ref, dhb_ref, dwuq_ref, dwdq_ref, dwukv_ref, dwkv_ref, dga_ref, dgk_ref, dgq_ref, dgl_ref,
             a_uq, a_dq, a_ukv, a_kv):
        i = pl.program_id(0)

        @pl.when(i == 0)
        def _():
            for ref in (a_uq, a_dq, a_ukv, a_kv, dga_ref, dgk_ref, dgq_ref, dgl_ref):
                ref[...] = jnp.zeros_like(ref)

        dqv = dq_ref[...]
        dcq = lax.dot_general(dqv, wuq_ref[...], NT_DIMS, preferred_element_type=F32)
        a_uq[...] += lax.dot_general(cq_ref[...], dqv, TN_DIMS, preferred_element_type=F32)
        dcq_pre, dg = _rms_bwd_math(cqp_ref[...], gq_ref[...], dcq)
        dgq_ref[...] += dg
        dcq_pre = dcq_pre.astype(BF16)
        dhn = lax.dot_general(dcq_pre, wdq_ref[...], NT_DIMS, preferred_element_type=F32)
        a_dq[...] += lax.dot_general(hn_ref[...], dcq_pre, TN_DIMS, preferred_element_type=F32)
        dckv = None
        for p in range(2):
            dk = dknv_ref[p].astype(BF16)
            part = lax.dot_general(dk, wukv_ref[p], NT_DIMS, preferred_element_type=F32)
            dckv = part if dckv is None else dckv + part
            a_ukv[p] += lax.dot_general(ckv_ref[...], dk, TN_DIMS, preferred_element_type=F32)
        dlat, dg = _rms_bwd_math(kvp_ref[:, :KV_LORA], gl_ref[...], dckv)
        dgl_ref[...] += dg
        dkr_pre = _rope_bwd_math(dkr_ref[...], c_ref[...], s_ref[...])
        dkvpre = jnp.concatenate([dlat, dkr_pre], axis=1).astype(BF16)
        dhk = lax.dot_general(dkvpre, wkv_ref[...], NT_DIMS, preferred_element_type=F32)
        a_kv[...] += lax.dot_general(hk_ref[...], dkvpre, TN_DIMS, preferred_element_type=F32)
        xv = h_ref[...]
        dx1, dg = _rms_bwd_math(xv, ga_ref[...], dhn)
        dga_ref[...] += dg
        dx2, dg = _rms_bwd_math(xv, gk_ref[...], dhk)
        dgk_ref[...] += dg
        dh_new = dh_ref[...] + dx1 + dx2
        dho_ref[...] = dh_new
        dhb_ref[...] = dh_new.astype(BF16)

        @pl.when(i == n_steps - 1)
        def _():
            dwuq_ref[...] = a_uq[...].astype(BF16)
            dwdq_ref[...] = a_dq[...].astype(BF16)
            dwukv_ref[...] = a_ukv[...].astype(BF16)
            dwkv_ref[...] = a_kv[...].astype(BF16)

    rows = lambda w: pl.BlockSpec((tr, w), lambda i: (i, 0))
    whole = lambda shape: pl.BlockSpec(shape, lambda i: (0,) * len(shape))
    weights = (g_attn, g_kvin, w_dq, g_ql, w_uq, w_kv, g_kvl, w_ukv)
    dw_shapes = [(Q_LORA, wq), (d, Q_LORA), (2, KV_LORA, wk), (d, KVP)]
    dg_shapes = [(1, d), (1, d), (1, Q_LORA), (1, KV_LORA)]
    return _tc_call(
        body, name="attn_prep_bwd", grid=(n_steps,),
        in_specs=[rows(wq), pl.BlockSpec((2, tr, wk), lambda i: (0, i, 0)), rows(LANES), rows(d), rows(d), rows(d),
                  rows(d), rows(Q_LORA), rows(Q_LORA), rows(KVP), rows(KV_LORA)]
        + [whole(a.shape) for a in weights] + [rows(LANES), rows(LANES)],
        out_specs=[rows(d), rows(d)] + [whole(s) for s in dw_shapes + dg_shapes],
        out_shape=[jax.ShapeDtypeStruct((t, d), F32), jax.ShapeDtypeStruct((t, d), BF16)]
        + [jax.ShapeDtypeStruct(s, BF16) for s in dw_shapes] + [jax.ShapeDtypeStruct(s, F32) for s in dg_shapes],
        scratch_shapes=[pltpu.VMEM(s, F32) for s in dw_shapes], compiler_params=_cp("arbitrary"),
    )(dq, dknv, dkr, dh, h, hn, hk, cq_pre, cq, kvpre, ckv, *weights, cos, sin)


ROW_CHUNK = 64
HALO = 16
WIN = ROW_CHUNK + 16
LANE_HALVES = (slice(0, LANES), slice(LANES, TC))


def _stage(s_ref, p, src):
    t = src.shape[0]
    s_ref[p, :HALO] = jnp.zeros((HALO, TC), BF16)
    s_ref[p, HALO:HALO + t] = src
    s_ref[p, HALO + t:] = jnp.zeros((HALO, TC), BF16)


def _window(s_ref, p, i, lanes):
    base = pl.multiple_of(i * ROW_CHUNK, ROW_CHUNK)
    return s_ref[p, pl.ds(base, ROW_CHUNK + 2 * HALO), lanes].astype(F32)[8:8 + WIN]


def _valid(x):
    return x[8:8 + ROW_CHUNK]


def _prev(x, k):
    return pltpu.roll(x, k, axis=0)


def _next(x, k):
    return pltpu.roll(x, WIN - k, axis=0)


def _taps(w_ref, lanes):
    return w_ref[0:1, lanes], w_ref[1:2, lanes], w_ref[2:3, lanes]


def _fold8(x):
    return jnp.sum(x.reshape(ROW_CHUNK // 8, 8, x.shape[-1]), axis=0)


def _store_rows(ref, idx, i, lanes, x):
    rows = pl.ds(pl.multiple_of(i * ROW_CHUNK, ROW_CHUNK), ROW_CHUNK)
    ref[(*idx, rows, lanes)] = x.astype(ref.dtype)


def _for_chunks(t, chunk):
    def step(i, carry):
        for lanes in LANE_HALVES:
            chunk(i, lanes)
        return carry

    lax.fori_loop(0, t // ROW_CHUNK, step, 0)


def _write_col_sums(acc_ref, outs):
    for k, (ref, row) in enumerate(outs):
        ref[row:row + 1, :] = jnp.sum(acc_ref[k], axis=0, keepdims=True)


def _shift_down(x, k):
    row = lax.broadcasted_iota(jnp.int32, x.shape, 0)
    return jnp.where(row >= k, pltpu.roll(x, k, axis=0), 0.0)


def _shift_up(x, k):
    n = x.shape[0]
    row = lax.broadcasted_iota(jnp.int32, x.shape, 0)
    return jnp.where(row < n - k, pltpu.roll(x, n - k, axis=0), 0.0)


def _conv3(x, w_ref):
    return _shift_down(x, 2) * w_ref[0:1, :] + _shift_down(x, 1) * w_ref[1:2, :] + x * w_ref[2:3, :]


def _col(parts, t):
    if parts is None:
        return pl.BlockSpec((t, TC), lambda j: (0, j))
    return pl.BlockSpec((parts, t, TC), lambda j: (0, 0, j))


def _staging(parts, t):
    return pltpu.VMEM((parts, t + 2 * HALO, TC), BF16)


def _scmix_fwd(z, w):
    t = z.shape[1]

    def body(z_ref, w_ref, m_ref):
        b, c, u = (z_ref[p].astype(F32) for p in range(3))
        m_ref[...] = (b * _conv3(c * u, w_ref)).astype(BF16)

    return _tc_call(
        body, name="scmix_fwd", grid=(D // TC,), in_specs=[_col(3, t), pl.BlockSpec((3, TC), lambda j: (0, j))],
        out_specs=_col(None, t), out_shape=jax.ShapeDtypeStruct((t, D), BF16), compiler_params=_cp("parallel"),
    )(z, w)


def _scmix_bwd(z, w, dm):
    t = z.shape[1]

    def body(z_ref, w_ref, dm_ref, dz_ref, dw_ref, s_ref, acc_ref):
        for p in range(3):
            _stage(s_ref, p, z_ref[p])
        _stage(s_ref, 3, dm_ref[...])
        acc_ref[...] = jnp.zeros_like(acc_ref)

        def chunk(i, lanes):
            w0, w1, w2 = _taps(w_ref, lanes)
            b, c, u, dm = (_window(s_ref, p, i, lanes) for p in range(4))
            cu = c * u
            cu1, cu2 = _prev(cu, 1), _prev(cu, 2)
            _store_rows(dz_ref, (0,), i, lanes, _valid(dm * (cu2 * w0 + cu1 * w1 + cu * w2)))
            dcv = dm * b
            dcu = dcv * w2 + _next(dcv, 1) * w1 + _next(dcv, 2) * w0
            _store_rows(dz_ref, (1,), i, lanes, _valid(dcu * u))
            _store_rows(dz_ref, (2,), i, lanes, _valid(dcu * c))
            for k, shifted in enumerate((cu2, cu1, cu)):
                acc_ref[k, :, lanes] += _fold8(_valid(dcv * shifted))

        _for_chunks(t, chunk)
        _write_col_sums(acc_ref, [(dw_ref, 0), (dw_ref, 1), (dw_ref, 2)])

    wspec = pl.BlockSpec((3, TC), lambda j: (0, j))
    return _tc_call(
        body, name="scmix_bwd", grid=(D // TC,), in_specs=[_col(3, t), wspec, _col(None, t)],
        out_specs=[_col(3, t), wspec],
        out_shape=[jax.ShapeDtypeStruct((3, t, D), BF16), jax.ShapeDtypeStruct((3, D), F32)],
        scratch_shapes=[_staging(4, t), pltpu.VMEM((3, 8, TC), F32)], compiler_params=_cp("parallel"),
    )(z, w, dm)


def _ffn_up_gate(hf, w_up, w, bias, name):
    t, d = hf.shape
    nb = F_FF // TC

    def body(hf_ref, wg_ref, wv_ref, w_ref, b_ref, up_ref, a_ref, prev_ref):
        @pl.when(pl.program_id(0) == 0)
        def _():
            prev_ref[...] = jnp.zeros_like(prev_ref)

        gc = _conv3(prev_ref[0].astype(F32), w_ref) + b_ref[...]
        a_ref[...] = (gc * jax.nn.sigmoid(gc) * prev_ref[1].astype(F32)).astype(BF16)
        hv = hf_ref[...]
        up_ref[0] = jnp.dot(hv, wg_ref[...], preferred_element_type=F32).astype(BF16)
        up_ref[1] = jnp.dot(hv, wv_ref[...], preferred_element_type=F32).astype(BF16)
        prev_ref[...] = up_ref[...]

    tile = lambda j: jnp.minimum(j, nb - 1)
    gated = lambda j: jnp.maximum(j - 1, 0)
    return _tc_call(
        body, name=name, grid=(nb + 1,),
        in_specs=[pl.BlockSpec((t, d), lambda j: (0, 0)), pl.BlockSpec((d, TC), lambda j: (0, tile(j))),
                  pl.BlockSpec((d, TC), lambda j: (0, nb + tile(j))), pl.BlockSpec((3, TC), lambda j: (0, gated(j))),
                  pl.BlockSpec((1, TC), lambda j: (0, gated(j)))],
        out_specs=[pl.BlockSpec((2, t, TC), lambda j: (0, 0, tile(j))), pl.BlockSpec((t, TC), lambda j: (0, gated(j)))],
        out_shape=[jax.ShapeDtypeStruct((2, t, F_FF), BF16), jax.ShapeDtypeStruct((t, F_FF), BF16)],
        scratch_shapes=[pltpu.VMEM((2, t, TC), BF16)], compiler_params=_cp("arbitrary"),
    )(hf, w_up, w_up, w, bias)


def _gate_bwd(up, w, bias, da, name):
    t = up.shape[1]

    def body(u_ref, w_ref, b_ref, da_ref, du_ref, dw_ref, db_ref, s_ref, acc_ref):
        for p in range(2):
            _stage(s_ref, p, u_ref[p])
        _stage(s_ref, 2, da_ref[...])
        acc_ref[...] = jnp.zeros_like(acc_ref)

        def chunk(i, lanes):
            w0, w1, w2 = _taps(w_ref, lanes)
            g, v, da = (_window(s_ref, p, i, lanes) for p in range(3))
            g1, g2 = _prev(g, 1), _prev(g, 2)
            gc = g2 * w0 + g1 * w1 + g * w2 + b_ref[:, lanes]
            sg = jax.nn.sigmoid(gc)
            _store_rows(du_ref, (1,), i, lanes, _valid(da * (gc * sg)))
            dgc = da * v * (sg * (1.0 + gc * (1.0 - sg)))
            _store_rows(du_ref, (0,), i, lanes, _valid(dgc * w2 + _next(dgc, 1) * w1 + _next(dgc, 2) * w0))
            for k, shifted in enumerate((g2, g1, g)):
                acc_ref[k, :, lanes] += _fold8(_valid(dgc * shifted))
            acc_ref[3, :, lanes] += _fold8(_valid(dgc))

        _for_chunks(t, chunk)
        _write_col_sums(acc_ref, [(dw_ref, 0), (dw_ref, 1), (dw_ref, 2), (db_ref, 0)])

    wspec = pl.BlockSpec((3, TC), lambda j: (0, j))
    bspec = pl.BlockSpec((1, TC), lambda j: (0, j))
    return _tc_call(
        body, name=name, grid=(F_FF // TC,), in_specs=[_col(2, t), wspec, bspec, _col(None, t)],
        out_specs=[_col(2, t), wspec, bspec],
        out_shape=[jax.ShapeDtypeStruct((2, t, F_FF), BF16), jax.ShapeDtypeStruct((3, F_FF), F32),
                   jax.ShapeDtypeStruct((1, F_FF), F32)],
        scratch_shapes=[_staging(3, t), pltpu.VMEM((4, 8, TC), F32)], compiler_params=_cp("parallel"),
    )(up, w, bias, da)


ATT_TQ = 256
ATT_SCALE = (QK_NOPE + QK_ROPE) ** -0.5


def _key_ranges(lvl):
    lo = lvl * ATT_TQ
    return ([(0, lo, False)] if lvl else []) + [(lo, lo + ATT_TQ, True)]


FWD_HEADS = 4
BWD_HEADS = 2


def _fill_keys(k_ref, kn_ref, kr_ref):
    @pl.when(pl.program_id(1) == 0)
    def _():
        for hh in range(k_ref.shape[0]):
            k_ref[hh, :, :QK_NOPE] = kn_ref[:, hh * QK_NOPE:(hh + 1) * QK_NOPE]
            k_ref[hh, :, QK_NOPE:] = kr_ref[...]


def _attn_probs(q, k_ref, lvl):
    scores = []
    for lo, hi, diagonal in _key_ranges(lvl):
        s = lax.dot_general(q, k_ref[lo:hi, :], NT_DIMS, preferred_element_type=F32) * ATT_SCALE
        if diagonal:
            row = lax.broadcasted_iota(jnp.int32, s.shape, 0)
            col = lax.broadcasted_iota(jnp.int32, s.shape, 1)
            seen = lax.shift_right_logical(col, CHUNK_SHIFT) <= lax.shift_right_logical(row, CHUNK_SHIFT)
            s = jnp.where(seen, s, NEG_INF)
        scores.append(s)
    m = jnp.max(scores[0], axis=1, keepdims=True)
    for s in scores[1:]:
        m = jnp.maximum(m, jnp.max(s, axis=1, keepdims=True))
    ps = [jnp.exp(s - m) for s in scores]
    total = jnp.sum(ps[0], axis=1, keepdims=True)
    for p in ps[1:]:
        total = total + jnp.sum(p, axis=1, keepdims=True)
    inv = 1.0 / total
    return [p * inv for p in ps]


def _per_query_block(qi, n_blocks, branch):
    for lvl in range(n_blocks):
        pl.when(qi == lvl)(lambda lvl=lvl: branch(lvl))


def _attn_specs(t, g):
    q = pl.BlockSpec((ATT_TQ, g * HEAD_PAD), lambda h, i: (i, h))
    kn = pl.BlockSpec((None, t, g * QK_NOPE), lambda h, i: (0, 0, h))
    kr = pl.BlockSpec((t, LANES), lambda h, i: (0, 0))
    v = pl.BlockSpec((None, t, g * V_HEAD), lambda h, i: (1, 0, h))
    o = pl.BlockSpec((ATT_TQ, g * V_HEAD), lambda h, i: (i, h))
    return q, kn, kr, v, o


def _attn_fwd(q, knv, kr):
    t = q.shape[0]

    def body(q_ref, kn_ref, kr_ref, v_ref, o_ref, k_ref):
        _fill_keys(k_ref, kn_ref, kr_ref)

        def branch(lvl):
            for hh in range(FWD_HEADS):
                vcols = slice(hh * V_HEAD, (hh + 1) * V_HEAD)
                ps = _attn_probs(q_ref[:, hh * HEAD_PAD:(hh + 1) * HEAD_PAD], k_ref.at[hh], lvl)
                o = None
                for p, (lo, hi, _) in zip(ps, _key_ranges(lvl)):
                    part = jnp.dot(p.astype(BF16), v_ref[lo:hi, vcols], preferred_element_type=F32)
                    o = part if o is None else o + part
                o_ref[:, vcols] = o.astype(BF16)

        _per_query_block(pl.program_id(1), t // ATT_TQ, branch)

    qs, kns, krs, vs, os_ = _attn_specs(t, FWD_HEADS)
    return _tc_call(
        body, name="attn_fwd", grid=(N_HEADS // FWD_HEADS, t // ATT_TQ), in_specs=[qs, kns, krs, vs],
        out_specs=os_, out_shape=jax.ShapeDtypeStruct((t, N_HEADS * V_HEAD), BF16),
        scratch_shapes=[pltpu.VMEM((FWD_HEADS, t, HEAD_PAD), BF16)], compiler_params=_cp("parallel", "arbitrary"),
    )(q, knv, kr, knv)


def _attn_bwd(q, knv, kr, do, cos, sin):
    t = q.shape[0]

    def body(q_ref, kn_ref, kr_ref, v_ref, do_ref, c_ref, s_ref, dq_ref, dknv_ref, dkr_ref, k_ref, dk_ref):
        h, qi = pl.program_id(0), pl.program_id(1)
        _fill_keys(k_ref, kn_ref, kr_ref)

        @pl.when(qi == 0)
        def _():
            dknv_ref[1] = jnp.zeros(dknv_ref.shape[1:], F32)
            dk_ref[...] = jnp.zeros_like(dk_ref)

        @pl.when((qi == 0) & (h == 0))
        def _():
            dkr_ref[...] = jnp.zeros_like(dkr_ref)

        def branch(lvl):
            ranges = _key_ranges(lvl)
            for hh in range(BWD_HEADS):
                qcols = slice(hh * HEAD_PAD, (hh + 1) * HEAD_PAD)
                vcols = slice(hh * V_HEAD, (hh + 1) * V_HEAD)
                qv, dov = q_ref[:, qcols], do_ref[:, vcols]
                ps = _attn_probs(qv, k_ref.at[hh], lvl)
                dps = [lax.dot_general(dov, v_ref[lo:hi, vcols], NT_DIMS, preferred_element_type=F32)
                       for lo, hi, _ in ranges]
                di = None
                for p, dp in zip(ps, dps):
                    part = jnp.sum(p * dp, axis=1, keepdims=True)
                    di = part if di is None else di + part
                dq = None
                for p, dp, (lo, hi, _) in zip(ps, dps, ranges):
                    ds = (p * (dp - di) * ATT_SCALE).astype(BF16)
                    part = jnp.dot(ds, k_ref[hh, lo:hi, :], preferred_element_type=F32)
                    dq = part if dq is None else dq + part
                    dk_ref[hh, lo:hi, :] += lax.dot_general(ds, qv, TN_DIMS, preferred_element_type=F32)
                    dknv_ref[1, lo:hi, vcols] += lax.dot_general(p.astype(BF16), dov, TN_DIMS,
                                                                 preferred_element_type=F32)
                dq_ref[:, hh * HEAD_PAD:hh * HEAD_PAD + QK_NOPE] = dq[:, :QK_NOPE].astype(BF16)
                dq_ref[:, hh * HEAD_PAD + QK_NOPE:(hh + 1) * HEAD_PAD] = _rope_bwd_math(
                    dq[:, QK_NOPE:], c_ref[...], s_ref[...]).astype(BF16)

        _per_query_block(qi, t // ATT_TQ, branch)

        @pl.when(qi == t // ATT_TQ - 1)
        def _():
            for hh in range(BWD_HEADS):
                dknv_ref[0, :, hh * QK_NOPE:(hh + 1) * QK_NOPE] = dk_ref[hh, :, :QK_NOPE]
                dkr_ref[...] += dk_ref[hh, :, QK_NOPE:]

    qs, kns, krs, vs, os_ = _attn_specs(t, BWD_HEADS)
    tab = pl.BlockSpec((ATT_TQ, LANES), lambda h, i: (i, 0))
    return _tc_call(
        body, name="attn_bwd", grid=(N_HEADS // BWD_HEADS, t // ATT_TQ), in_specs=[qs, kns, krs, vs, os_, tab, tab],
        out_specs=[qs, pl.BlockSpec((2, t, BWD_HEADS * QK_NOPE), lambda h, i: (0, 0, h)), krs],
        out_shape=[jax.ShapeDtypeStruct((t, N_HEADS * HEAD_PAD), BF16),
                   jax.ShapeDtypeStruct((2, t, N_HEADS * QK_NOPE), F32), jax.ShapeDtypeStruct((t, LANES), F32)],
        scratch_shapes=[pltpu.VMEM((BWD_HEADS, t, HEAD_PAD), BF16), pltpu.VMEM((BWD_HEADS, t, HEAD_PAD), F32)],
        compiler_params=_cp("arbitrary", "arbitrary"),
    )(q, knv, kr, knv, do, cos, sin)


def _adam_math(w, g, m, v):
    nm = ADAM_B1 * m + (1.0 - ADAM_B1) * g
    nv = ADAM_B2 * v + (1.0 - ADAM_B2) * (g * g)
    m_hat = nm / (1.0 - ADAM_B1 ** ADAM_STEP)
    v_hat = nv / (1.0 - ADAM_B2 ** ADAM_STEP)
    return -ADAM_LR * (m_hat / (jnp.sqrt(v_hat) + ADAM_EPS) + ADAM_WD * w), nm, nv


def _adamw_small(w, g, m, v):
    def body(w_ref, g_ref, m_ref, v_ref, d_ref, nm_ref, nv_ref):
        d_ref[...], nm_ref[...], nv_ref[...] = _adam_math(w_ref[...], g_ref[...], m_ref[...], v_ref[...])

    shp = jax.ShapeDtypeStruct(w.shape, F32)
    return _tc_call(body, name="adamw_small", out_shape=[shp] * 3)(w, g, m, v)


ADAM_SPLIT = 4


def _adamw_shards(ids, items, name):
    n = len(items)

    def body(ids_ref, *refs):
        outs = refs[len(refs) - 4 * n:]
        mine = pl.program_id(0) == ids_ref[0]
        for i in range(n):
            w_ref, m_ref, v_ref, gm_ref, gs_ref = refs[5 * i:5 * i + 5]
            g_ref, d_ref, nm_ref, nv_ref = outs[4 * i:4 * i + 4]

            @pl.when(mine)
            def _(g_ref=g_ref, gm_ref=gm_ref):
                g_ref[...] = gm_ref[...]

            @pl.when(jnp.logical_not(mine))
            def _(g_ref=g_ref, gs_ref=gs_ref):
                g_ref[...] = gs_ref[...]

            d_ref[...], nm_ref[...], nv_ref[...] = _adam_math(w_ref[...], g_ref[...], m_ref[...], v_ref[...])

    in_specs, out_specs, out_shape, args, carried, aliases = [], [], [], [ids], [], {}
    for i, it in enumerate(items):
        w = it["w"]
        r, c = w.shape[-2:]
        tr = r // 2 // ADAM_SPLIT
        assert tr % 8 == 0, (name, w.shape)
        layer = it.get("layer")
        if layer is None:
            wspec = pl.BlockSpec((tr, c), lambda h, k, ids: (h * ADAM_SPLIT + k, 0))
        else:
            wspec = pl.BlockSpec((None, tr, c), lambda h, k, ids, layer=layer: (layer, h * ADAM_SPLIT + k, 0))
        gspec = pl.BlockSpec((tr, c), lambda h, k, ids: (k, 0))
        in_specs += [wspec] * 3 + [gspec] * 2
        args += [w, it["m"], it["v"], it["g_mine"], it["g_sib"]]
        out_specs += [wspec] * 4
        out_shape += [jax.ShapeDtypeStruct(w.shape, F32)] * 4
        if it.get("prev") is not None:
            for k, p in enumerate(it["prev"]):
                aliases[1 + 5 * n + len(carried)] = 4 * i + k
                carried.append(p)
    res = _tc_call(
        body, name=name, prefetch=1, grid=(2, ADAM_SPLIT), in_specs=in_specs + [ANY] * len(carried),
        out_specs=out_specs, out_shape=out_shape, input_output_aliases=aliases,
        compiler_params=_cp("parallel", "parallel"),
    )(*args, *carried)
    return [res[4 * i:4 * i + 4] for i in range(n)]


def _peer_chip(k_me, j):
    return k_me ^ jnp.where(j == 0, 2, jnp.where(j == 1, 1, 3))


def _pair_sums(ids, gs, ras, name):
    n = len(gs)

    def body(ids_ref, *refs):
        for i in range(n):
            g_ref, ra_ref, o_ref = refs[2 * i], refs[2 * i + 1], refs[2 * n + i]
            o_ref[...] = (g_ref[...].astype(F32) + ra_ref[...].astype(F32)).astype(BF16)

    in_specs, out_specs, out_shape = [], [], []
    for g in gs:
        half, c = g.shape[1] // 2, g.shape[2]
        in_specs += [pl.BlockSpec((None, half, c), lambda j, ids: (_peer_chip(ids[1], j), ids[0], 0)),
                     pl.BlockSpec((None, half, c), lambda j, ids: (_peer_chip(ids[1], j), 0, 0))]
        out_specs.append(pl.BlockSpec((None, half, c), lambda j, ids: (j, 0, 0)))
        out_shape.append(jax.ShapeDtypeStruct((3, half, c), BF16))
    return _tc_call(
        body, name=name, prefetch=1, grid=(3,), in_specs=in_specs, out_specs=out_specs, out_shape=out_shape,
        compiler_params=_cp("parallel"),
    )(ids, *[a for pair in zip(gs, ras) for a in pair])


def _chip_sums(ids, gs, ras, rbs, name):
    n = len(gs)

    def body(ids_ref, *refs):
        for i in range(n):
            g_ref, ra_ref, rb_ref, o_ref = refs[3 * i], refs[3 * i + 1], refs[3 * i + 2], refs[3 * n + i]
            acc = g_ref[...].astype(F32) + ra_ref[...].astype(F32)
            for j in range(3):
                acc = acc + rb_ref[j].astype(F32)
            o_ref[...] = acc

    in_specs, out_specs, out_shape = [], [], []
    for g in gs:
        half, c = g.shape[1] // 2, g.shape[2]
        in_specs += [pl.BlockSpec((None, half, c), lambda i, ids: (ids[1], ids[0], 0)),
                     pl.BlockSpec((None, half, c), lambda i, ids: (ids[1], 0, 0)),
                     pl.BlockSpec((3, half, c), lambda i, ids: (0, 0, 0))]
        out_specs.append(pl.BlockSpec((half, c), lambda i, ids: (0, 0)))
        out_shape.append(jax.ShapeDtypeStruct((half, c), F32))
    return _tc_call(
        body, name=name, prefetch=1, grid=(1,), in_specs=in_specs, out_specs=out_specs, out_shape=out_shape,
        compiler_params=_cp("arbitrary"),
    )(ids, *[a for trio in zip(gs, ras, rbs) for a in trio])


def _position():
    x, y, c = lax.axis_index("x"), lax.axis_index("y"), lax.axis_index("c")
    chips = [(1 - x, y), (x, 1 - y), (1 - x, 1 - y)]
    return x, y, c, chips


def _shard_half(ref, wm, h):
    if wm.kind == "tiny":
        return ref
    if wm.nl == 2:
        return ref.at[h]
    return ref.at[pl.ds(pl.multiple_of(h * (wm.k // 2), 16), wm.k // 2), :]


def _region(full, wm, s, h):
    if wm.kind == "tiny":
        return full.at[s]
    cols = pl.ds(pl.multiple_of(s * wm.n, LANES), wm.n) if wm.kind == "col" else slice(None)
    if wm.nl == 2:
        rows = pl.ds(pl.multiple_of(s * wm.k, 16), wm.k) if wm.kind == "row" else slice(None)
        return full.at[slice(None) if h is None else h, rows, cols]
    if wm.kind == "col":
        rows = slice(None) if h is None else pl.ds(pl.multiple_of(h * (wm.k // 2), 16), wm.k // 2)
    elif h is None:
        rows = pl.ds(pl.multiple_of(s * wm.k, 16), wm.k)
    else:
        rows = pl.ds(pl.multiple_of(s * wm.k + h * (wm.k // 2), 16), wm.k // 2)
    return full.at[rows, cols]


def _full_shape(wm):
    if wm.kind == "tiny":
        return (N_CHIPS, wm.k, wm.n)
    shape = (wm.k, N_CHIPS * wm.n) if wm.kind == "col" else (N_CHIPS * wm.k, wm.n)
    return shape if wm.nl == 1 else (wm.nl,) + shape


def _handshake(peers):
    barrier = pltpu.get_barrier_semaphore()
    for peer in peers:
        pl.semaphore_signal(barrier, inc=1, device_id=peer, device_id_type=MESH)
    pl.semaphore_wait(barrier, len(peers))


def _all_gather_group(gi, shards):
    wms = AG_GROUPS[gi]
    nw = len(wms)

    def body(*refs):
        sh, full = refs[:nw], refs[nw:2 * nw]
        ici_s, ici_r, pass_s, pass_r, own_s, own_r = refs[2 * nw:]
        x, y, c, _ = _position()
        me, sibling = 2 * x + y, (x, y, 1 - c)
        first, second, diagonal = (x ^ (1 - c), y ^ c), (x ^ c, y ^ (1 - c)), (1 - x, 1 - y)
        chip_id = lambda chip: 2 * chip[0] + chip[1]
        _handshake([(*first, c), (*second, c), sibling])

        def rcopy(src, dst, s_sem, r_sem, to):
            return pltpu.make_async_remote_copy(src_ref=src, dst_ref=dst, send_sem=s_sem, recv_sem=r_sem,
                                                device_id=to, device_id_type=MESH)

        started = []

        def go(cp):
            cp.start()
            started.append(cp)

        for i, wm in enumerate(wms):
            half, dst = _shard_half(sh[i], wm, c), _region(full[i], wm, me, c)
            go(rcopy(half, dst, ici_s.at[i, 0], ici_r.at[i, 0], (*first, c)))
            go(rcopy(half, dst, ici_s.at[i, 1], ici_r.at[i, 1], (*second, c)))
            go(rcopy(sh[i], _region(full[i], wm, me, None), own_s.at[i], own_r.at[i], sibling))
        for i, wm in enumerate(wms):
            got = _region(full[i], wm, chip_id(first), c)
            rcopy(got, got, ici_s.at[i, 0], ici_r.at[i, 0], sibling).wait_recv()
            go(rcopy(got, got, ici_s.at[i, 2], ici_r.at[i, 2], (*second, c)))
            if wm.kind != "tiny":
                go(rcopy(got, got, pass_s.at[i, 0], pass_r.at[i, 0], sibling))
        for i, wm in enumerate(wms):
            for j, chip in ((1, second), (2, diagonal)):
                got = _region(full[i], wm, chip_id(chip), c)
                rcopy(got, got, ici_s.at[i, j], ici_r.at[i, j], sibling).wait_recv()
                if wm.kind != "tiny":
                    go(rcopy(got, got, pass_s.at[i, j], pass_r.at[i, j], sibling))
        for i, wm in enumerate(wms):
            mine = _region(full[i], wm, me, None)
            rcopy(mine, mine, own_s.at[i], own_r.at[i], sibling).wait_recv()
            if wm.kind != "tiny":
                for j, chip in ((0, second), (1, first), (2, diagonal)):
                    got = _region(full[i], wm, chip_id(chip), 1 - c)
                    rcopy(got, got, pass_s.at[i, j], pass_r.at[i, j], sibling).wait_recv()
        for cp in started:
            cp.wait_send()

    return pl.kernel(
        body, out_type=[jax.ShapeDtypeStruct(_full_shape(wm), s.dtype) for wm, s in zip(wms, shards)],
        mesh=plsc.ScalarSubcoreMesh(axis_name="sequencer", num_cores=1), name=f"ag_group{gi}",
        scratch_types=[pltpu.SemaphoreType.DMA((nw, 3))] * 4 + [pltpu.SemaphoreType.DMA((nw,))] * 2,
        compiler_params=pltpu.CompilerParams(collective_id=gi),
    )(*shards)


def _sequencer_call(body, name, cid, out_types, scratch, args):
    return pl.kernel(
        body, out_type=out_types, mesh=plsc.ScalarSubcoreMesh(axis_name="sequencer", num_cores=1), name=name,
        scratch_types=scratch, compiler_params=pltpu.CompilerParams(collective_id=cid),
    )(*args)


def _pair_exchange(gs, tag, cid):
    n = len(gs)

    def body(*refs):
        g, out, send_sems, recv_sems = refs[:n], refs[n:2 * n], refs[2 * n], refs[2 * n + 1]
        x, y, c, _ = _position()
        _handshake([(x, y, 1 - c)])
        cps = []
        for i in range(n):
            half = g[i].shape[1] // 2
            cps.append(pltpu.make_async_remote_copy(
                src_ref=g[i].at[:, pl.ds(pl.multiple_of((1 - c) * half, 16), half), :], dst_ref=out[i],
                send_sem=send_sems.at[i], recv_sem=recv_sems.at[i], device_id=(x, y, 1 - c), device_id_type=MESH))
            cps[-1].start()
        for cp in cps:
            cp.wait()

    return _sequencer_call(
        body, f"rs_pair_exchange{tag}", cid,
        [jax.ShapeDtypeStruct((a.shape[0], a.shape[1] // 2, a.shape[2]), a.dtype) for a in gs],
        [pltpu.SemaphoreType.DMA((n,)), pltpu.SemaphoreType.DMA((n,))], gs)


def _chip_exchange(ss, tag, cid):
    n = len(ss)

    def body(*refs):
        s, out, send_sems, recv_sems = refs[:n], refs[n:2 * n], refs[2 * n], refs[2 * n + 1]
        x, y, c, chips = _position()
        _handshake([(*chip, c) for chip in chips])
        cps = []
        for i in range(n):
            for j, chip in enumerate(chips):
                cps.append(pltpu.make_async_remote_copy(
                    src_ref=s[i].at[j], dst_ref=out[i].at[j], send_sem=send_sems.at[i, j], recv_sem=recv_sems.at[i, j],
                    device_id=(*chip, c), device_id_type=MESH))
                cps[-1].start()
        for cp in cps:
            cp.wait()

    return _sequencer_call(
        body, f"rs_chip_exchange{tag}", cid, [jax.ShapeDtypeStruct(a.shape, a.dtype) for a in ss],
        [pltpu.SemaphoreType.DMA((n, 3)), pltpu.SemaphoreType.DMA((n, 3))], ss)


def _pair_swap(g8s, tag, cid):
    n = len(g8s)

    def body(*refs):
        g, out, send_sems, recv_sems = refs[:n], refs[n:2 * n], refs[2 * n], refs[2 * n + 1]
        x, y, c, _ = _position()
        _handshake([(x, y, 1 - c)])
        cps = []
        for i in range(n):
            cps.append(pltpu.make_async_remote_copy(
                src_ref=g[i], dst_ref=out[i], send_sem=send_sems.at[i], recv_sem=recv_sems.at[i],
                device_id=(x, y, 1 - c), device_id_type=MESH))
            cps[-1].start()
        for cp in cps:
            cp.wait()

    return _sequencer_call(
        body, f"rs_pair_swap{tag}", cid, [jax.ShapeDtypeStruct(a.shape, a.dtype) for a in g8s],
        [pltpu.SemaphoreType.DMA((n,)), pltpu.SemaphoreType.DMA((n,))], g8s)


def _all_reduce_small(vec, name):
    r, cols = vec.shape

    def body(v_ref, o_ref, gath, send_sems, recv_sems):
        x, y, c, _ = _position()
        me = 4 * x + 2 * y + c
        gath[me] = v_ref[...]
        cps = []
        for rel in range(1, N_DEV):
            peer = (x ^ (rel >> 2), y ^ ((rel >> 1) & 1), c ^ (rel & 1))
            cps.append(pltpu.make_async_remote_copy(
                src_ref=v_ref, dst_ref=gath.at[me], send_sem=send_sems.at[rel - 1], recv_sem=recv_sems.at[rel - 1],
                device_id=peer, device_id_type=MESH))
        for cp in cps:
            cp.start()
        for rel in range(1, N_DEV):
            pltpu.make_async_remote_copy(
                src_ref=v_ref, dst_ref=gath.at[me ^ rel], send_sem=send_sems.at[rel - 1],
                recv_sem=recv_sems.at[rel - 1], device_id=(x, y, c), device_id_type=MESH).wait_recv()
        for cp in cps:
            cp.wait_send()
        acc = gath[0]
        for d in range(1, N_DEV):
            acc = acc + gath[d]
        o_ref[...] = acc

    vm = pl.BlockSpec(memory_space=pltpu.VMEM)
    return _tc_call(
        body, name=name, in_specs=[vm], out_specs=vm, out_shape=jax.ShapeDtypeStruct((r, cols), F32),
        scratch_shapes=[pltpu.VMEM((N_DEV, r, cols), F32), pltpu.SemaphoreType.DMA((N_DEV - 1,)),
                        pltpu.SemaphoreType.DMA((N_DEV - 1,))],
    )(vec)


def _rope_tables(positions):
    half = QK_ROPE // 2
    inv_freq = 1.0 / (ROPE_THETA ** (jnp.arange(half, dtype=F32) / half))
    ang = positions.astype(F32)[:, None] * inv_freq
    zeros = jnp.zeros((positions.shape[0], LANES - QK_ROPE), F32)
    cos, sin = jnp.cos(ang), jnp.sin(ang)
    return jnp.concatenate([cos, cos, zeros], axis=1), jnp.concatenate([sin, sin, zeros], axis=1)


def _local_step(x, positions, tgt, wf, small, rs):
    cos, sin = _rope_tables(positions)
    w_in, w_out = wf["sc_w_in"], wf["sc_w_out"]
    w_ups, w_downs = (wf["ffn_w_up0"], wf["ffn_w_up1"]), (wf["ffn_w_down0"], wf["ffn_w_down1"])
    w_kv, w_ukv, w_dq, w_uq, w_o = wf["w_kv"], wf["w_ukv"], wf["w_dq"], wf["w_uq"], wf["w_o"]
    attn_norm, ffn_norm = small["attn_norm"], small["ffn_norm"]
    conv_b = small["ffn_conv_b"]

    def ffn_fwd(h, hf, l, then):
        up, a = _ffn_up_gate(hf, w_ups[l], small["ffn_conv_w"][l], conv_b[l:l + 1], f"ffn{l}_up_gate")
        return then(a, w_downs[l], h), (hf, up, a)

    def ffn_bwd(h, dh_out, dh_out_b, l, saved, gi, hooks):
        run = lambda stage: hooks.get(stage, lambda: None)()
        hf, up, a = saved
        da = _nt(f"ffn{l}_down_dx", dh_out_b, w_downs[l], BF16)
        run("down_dx")
        d_down = _tn(f"ffn{l}_down_dw", a, dh_out_b, BF16)
        dup, d_cw, d_cb = _gate_bwd(up, small["ffn_conv_w"][l], conv_b[l:l + 1], da, f"ffn{l}_gate_bwd")
        run("gate_bwd")
        d_up = _dw_ffn_up(f"ffn{l}_up_dw", hf, dup)
        rs.start(gi, {f"ffn_w_down{l}": d_down.reshape(N_CHIPS, F_FF // N_CHIPS, D), f"ffn_w_up{l}": d_up})
        dh, dh_b, d_norm = _dx_norm_bwd(f"ffn{l}_up_dx", dup, w_ups[l], h, ffn_norm[l:l + 1], dh_out)
        run("up_dx")
        return dh, dh_b, d_cw, d_cb, d_norm

    hn0 = _rms_fwd(x, attn_norm[0:1], "attn0_norm")
    z = _nn_parts("sc_in", hn0, w_in, 3, BF16)
    mix = _scmix_fwd(z, small["sc_conv_w"])
    h1, hf0 = _nn_add_norm("sc_out", mix, w_out, x, ffn_norm[0:1])
    h2, ffn0_saved = ffn_fwd(h1, hf0, 0, lambda a, w, h: _nn("ffn0_down", a, w, F32, add=h))

    hn1, hk, cq_pre, cq, q, kvpre, ckv, kr, knv = _attn_prep(
        h2, attn_norm[1:2], small["kv_in_norm"], w_dq, small["q_latent_norm"], w_uq, w_kv, small["kv_latent_norm"],
        w_ukv, cos, sin)
    o = _attn_fwd(q, knv, kr)
    h3, hf1 = _nn_add_norm("attn_out", o, w_o, h2, ffn_norm[1:2])
    (loss, dh4, dh4_b, d_final), ffn1_saved = ffn_fwd(
        h3, hf1, 1, lambda a, w, h: _nn_add_loss("ffn1_down_loss", a, w, h, small["final_norm"], tgt))

    rows = D // N_CHIPS
    dh3, dh3_b, d_cw1, d_cb1, d_fn1 = ffn_bwd(h3, dh4, dh4_b, 1, ffn1_saved, 0, {})

    do = _nt("attn_out_dx", dh3_b, w_o, BF16)
    d_wo = _tn("attn_out_dw", o, dh3_b, BF16)
    rs.pair_sums(0)
    dq, dknv, dkr = _attn_bwd(q, knv, kr, do, cos, sin)
    rs.chip_sums(0)
    dh2, dh2_b, d_wuq, d_wdq, d_wukv, d_wkv, d_an1, d_kvin, d_qln, d_kvln = _attn_prep_bwd(
        dq, dknv, dkr, dh3, h2, hn1, hk, cq_pre, cq, kvpre, ckv, attn_norm[1:2], small["kv_in_norm"], w_dq,
        small["q_latent_norm"], w_uq, w_kv, small["kv_latent_norm"], w_ukv, cos, sin)
    rs.finish(0)
    by_owner = lambda dw: dw.reshape(dw.shape[0], N_CHIPS, -1).transpose(1, 0, 2)
    rs.start(1, {
        "w_o": d_wo.reshape(N_CHIPS, rows, D), "w_uq": by_owner(d_wuq), "w_dq": d_wdq.reshape(N_CHIPS, rows, Q_LORA),
        "w_ukv": by_owner(d_wukv.reshape(2 * KV_LORA, -1)).reshape(N_CHIPS, 2 * KV_LORA, -1),
        "w_kv": d_wkv.reshape(N_CHIPS, rows, KVP),
    })

    dh1, dh1_b, d_cw0, d_cb0, d_fn0 = ffn_bwd(h1, dh2, dh2_b, 0, ffn0_saved, 2, {
        "down_dx": lambda: rs.pair_sums(1), "gate_bwd": lambda: rs.chip_sums(1), "up_dx": lambda: rs.finish(1)})
    rs.pair_sums(2)

    d_wout = _tn("sc_out_dw", mix, dh1_b, BF16)
    dmix = _nt("sc_out_dx", dh1_b, w_out, BF16)
    dz, d_scw = _scmix_bwd(z, small["sc_conv_w"], dmix)
    d_win = _dw_sc_in(hn0, dz)
    rs.start(3, {"sc_w_out": d_wout.reshape(N_CHIPS, rows, D), "sc_w_in": d_win})
    dx, _, d_an0 = _dx_norm_bwd("sc_in_dx", dz, w_in, x, attn_norm[0:1], dh1)

    small_g = {
        "attn_norm": jnp.concatenate([d_an0, d_an1]), "ffn_norm": jnp.concatenate([d_fn0, d_fn1]),
        "final_norm": d_final, "kv_in_norm": d_kvin, "kv_latent_norm": d_kvln, "q_latent_norm": d_qln,
        "ffn_conv_b": jnp.concatenate([d_cb0, d_cb1]), "sc_conv_w": d_scw, "ffn_conv_w": jnp.stack([d_cw0, d_cw1]),
    }
    return loss, dx, small_g


RS_GROUPS = (("ffn_w_down1", "ffn_w_up1"), ("w_o", "w_uq", "w_dq", "w_ukv", "w_kv"),
             ("ffn_w_down0", "ffn_w_up0"), ("sc_w_out", "sc_w_in"))


class _ReduceScatter:
    def __init__(self, ids, finish):
        self.ids, self.grads, self.step, self.mine, self.sib, self.finish = ids, {}, {}, {}, {}, finish

    def _cid(self, gi):
        return len(AG_GROUPS) + 3 * gi

    def start(self, gi, grads):
        self.grads.update(grads)
        own = [grads[n] for n in RS_GROUPS[gi]]
        self.step[gi] = (own, _pair_exchange(own, gi, self._cid(gi)))

    def pair_sums(self, gi):
        own, ra = self.step[gi]
        sums = _pair_sums(self.ids, own, ra, f"rs_pair_sums{gi}")
        self.step[gi] = (own, ra, _chip_exchange(sums, gi, self._cid(gi) + 1))

    def chip_sums(self, gi):
        own, ra, rb = self.step[gi]
        mine = _chip_sums(self.ids, own, ra, rb, f"rs_chip_sums{gi}")
        self.mine.update(zip(RS_GROUPS[gi], mine))
        self.sib.update(zip(RS_GROUPS[gi], _pair_swap(mine, gi, self._cid(gi) + 2)))

SMALL_REPL = ("attn_norm", "ffn_norm", "final_norm", "kv_in_norm", "kv_latent_norm", "q_latent_norm", "ffn_conv_b")
SMALL_SHARDED = ("sc_conv_w", "ffn_conv_w")
SMALL_ROWS = 256


def _pad_heads(w_uq):
    per_head = w_uq.reshape(Q_LORA, -1, QK_NOPE + QK_ROPE)
    return jnp.pad(per_head, ((0, 0), (0, 0), (0, HEAD_PAD - QK_NOPE - QK_ROPE))).reshape(Q_LORA, -1)


def _pack_kv(w_dkv, w_kr):
    return jnp.concatenate([w_dkv, w_kr, jnp.zeros((w_kr.shape[0], LANES - QK_ROPE), w_kr.dtype)], axis=1)


def kernel(x, positions, attn_norm, ffn_norm, final_norm, sc_w_in, sc_conv_w, sc_w_out, kv_in_norm, w_dkv, kv_latent_norm, w_kr, w_uk, w_uv, w_dq, q_latent_norm, w_uq, w_o, ffn_w_up, ffn_conv_w, ffn_conv_b, ffn_w_down, loss_target, m_attn_norm, m_ffn_norm, m_final_norm, m_sc_w_in, m_sc_conv_w, m_sc_w_out, m_kv_in_norm, m_w_dkv, m_kv_latent_norm, m_w_kr, m_w_uk, m_w_uv, m_w_dq, m_q_latent_norm, m_w_uq, m_w_o, m_ffn_w_up, m_ffn_conv_w, m_ffn_conv_b, m_ffn_w_down, v_attn_norm, v_ffn_norm, v_final_norm, v_sc_w_in, v_sc_conv_w, v_sc_w_out, v_kv_in_norm, v_w_dkv, v_kv_latent_norm, v_w_kr, v_w_uk, v_w_uv, v_w_dq, v_q_latent_norm, v_w_uq, v_w_o, v_ffn_w_up, v_ffn_conv_w, v_ffn_conv_b, v_ffn_w_down):
    names = ("attn_norm", "ffn_norm", "final_norm", "sc_w_in", "sc_conv_w", "sc_w_out", "kv_in_norm", "w_dkv",
             "kv_latent_norm", "w_kr", "w_uk", "w_uv", "w_dq", "q_latent_norm", "w_uq", "w_o", "ffn_w_up",
             "ffn_conv_w", "ffn_conv_b", "ffn_w_down")
    w = dict(zip(names, (attn_norm, ffn_norm, final_norm, sc_w_in, sc_conv_w, sc_w_out, kv_in_norm, w_dkv,
                         kv_latent_norm, w_kr, w_uk, w_uv, w_dq, q_latent_norm, w_uq, w_o, ffn_w_up,
                         ffn_conv_w, ffn_conv_b, ffn_w_down)))
    m = dict(zip(names, (m_attn_norm, m_ffn_norm, m_final_norm, m_sc_w_in, m_sc_conv_w, m_sc_w_out, m_kv_in_norm,
                         m_w_dkv, m_kv_latent_norm, m_w_kr, m_w_uk, m_w_uv, m_w_dq, m_q_latent_norm, m_w_uq, m_w_o,
                         m_ffn_w_up, m_ffn_conv_w, m_ffn_conv_b, m_ffn_w_down)))
    v = dict(zip(names, (v_attn_norm, v_ffn_norm, v_final_norm, v_sc_w_in, v_sc_conv_w, v_sc_w_out, v_kv_in_norm,
                         v_w_dkv, v_kv_latent_norm, v_w_kr, v_w_uk, v_w_uv, v_w_dq, v_q_latent_norm, v_w_uq, v_w_o,
                         v_ffn_w_up, v_ffn_conv_w, v_ffn_conv_b, v_ffn_w_down)))

    _ORDER[0] = None
    ix, iy, ic = lax.axis_index("x"), lax.axis_index("y"), lax.axis_index("c")
    chip = 2 * ix + iy
    ids = jnp.stack([ic, chip]).astype(jnp.int32)

    def shards_of(t):
        return {
            "sc_w_in": t["sc_w_in"][0], "sc_w_out": t["sc_w_out"][0], "ffn_w_up": t["ffn_w_up"],
            "ffn_w_down": t["ffn_w_down"], "w_kv": _pack_kv(t["w_dkv"], t["w_kr"]),
            "w_ukv": jnp.stack([t["w_uk"], t["w_uv"]]), "w_dq": t["w_dq"][0], "w_uq": _pad_heads(t["w_uq"][0]),
            "w_o": t["w_o"][0],
        }

    ws, ms, vs = shards_of(w), shards_of(m), shards_of(v)

    def ag_shard(name):
        if name == "sc_conv_w":
            return sc_conv_w[0]
        if name == "ffn_conv_w":
            return ffn_conv_w.reshape(6, -1)
        if name[:-1] in ("ffn_w_up", "ffn_w_down"):
            return ws[name[:-1]][int(name[-1])].astype(BF16)
        return ws[name].astype(BF16)

    wf = {}
    for gi, wms in enumerate(AG_GROUPS):
        fulls = _all_gather_group(gi, [ag_shard(wm.name) for wm in wms])
        wf.update({wm.name: f for wm, f in zip(wms, fulls)})
    small = {
        "attn_norm": attn_norm, "ffn_norm": ffn_norm, "final_norm": final_norm[None], "kv_in_norm": kv_in_norm[None],
        "kv_latent_norm": kv_latent_norm[None], "q_latent_norm": q_latent_norm, "ffn_conv_b": ffn_conv_b,
        "sc_conv_w": wf["sc_conv_w"].transpose(1, 0, 2).reshape(3, D),
        "ffn_conv_w": wf["ffn_conv_w"].reshape(N_CHIPS, 2, 3, -1).transpose(1, 2, 0, 3).reshape(2, 3, F_FF),
    }

    res = {}

    merged = lambda a: a.reshape(2 * KV_LORA, -1)

    def adamw_group(gi):
        items = []
        for key in RS_GROUPS[gi]:
            n, layer = (key[:-1], int(key[-1])) if key[:-1] in ("ffn_w_up", "ffn_w_down") else (key, None)
            w_, m_, v_ = (merged(t[n]) for t in (ws, ms, vs)) if n == "w_ukv" else (ws[n], ms[n], vs[n])
            items.append(dict(name=n, w=w_, m=m_, v=v_, g_mine=rs.mine[key], g_sib=rs.sib[key], layer=layer,
                              prev=res.get(n)))
        for it, out in zip(items, _adamw_shards(ids, items, f"adamw_group{gi}")):
            res[it["name"]] = out

    rs = _ReduceScatter(ids, adamw_group)
    loss, dx, small_g = _local_step(x[0], positions[0], loss_target[0], wf, small, rs)

    rs.chip_sums(2)
    rs.pair_sums(3)

    s_order = SMALL_REPL + SMALL_SHARDED
    flat = jnp.concatenate([small_g[n].reshape(-1) for n in s_order] + [loss.reshape(-1)])
    flat = jnp.pad(flat, (0, SMALL_ROWS * LANES - flat.shape[0])).reshape(SMALL_ROWS, LANES)
    red = _all_reduce_small(flat, "ar_small").reshape(-1)
    sg, off = {}, 0
    for n in s_order:
        sz = small_g[n].size
        sg[n] = red[off:off + sz].reshape(small_g[n].shape)
        off += sz
    loss_out = red[off]
    grads = {n: sg[n].reshape(w[n].shape) for n in SMALL_REPL}
    grads["sc_conv_w"] = lax.dynamic_slice_in_dim(sg["sc_conv_w"], chip * (D // N_CHIPS), D // N_CHIPS, axis=1)[None]
    grads["ffn_conv_w"] = lax.dynamic_slice_in_dim(sg["ffn_conv_w"], chip * (F_FF // N_CHIPS), F_FF // N_CHIPS, axis=2)

    small_names = SMALL_REPL + SMALL_SHARDED

    def pack_small(tree):
        return jnp.concatenate([tree[n].reshape(-1) for n in small_names]).reshape(-1, LANES)

    small_res = _adamw_small(pack_small(w), pack_small(grads), pack_small(m), pack_small(v))
    rs.finish(2)
    rs.chip_sums(3)
    rs.finish(3)
    outs = [grads, {}, {}, {}]
    for k, dst in enumerate(outs):
        for n in ("sc_w_in", "sc_w_out", "w_dq", "w_o"):
            dst[n] = res[n][k][None]
        unpadded = res["w_uq"][k].reshape(Q_LORA, -1, HEAD_PAD)[:, :, :QK_NOPE + QK_ROPE]
        dst["w_uq"] = unpadded.reshape(w_uq.shape)
        dst["ffn_w_up"], dst["ffn_w_down"] = res["ffn_w_up"][k], res["ffn_w_down"][k]
        dst["w_dkv"], dst["w_kr"] = res["w_kv"][k][:, :KV_LORA], res["w_kv"][k][:, KV_LORA:KV_LORA + QK_ROPE]
        dst["w_uk"], dst["w_uv"] = res["w_ukv"][k][:KV_LORA], res["w_ukv"][k][KV_LORA:]
    grads, delta, new_m, new_v = outs
    for slab, dst in zip(small_res, (delta, new_m, new_v)):
        f, off = slab.reshape(-1), 0
        for n in small_names:
            dst[n] = f[off:off + w[n].size].reshape(w[n].shape)
            off += w[n].size

    _ORDER[0] = None
    return (loss_out, dx[None], *[grads[n] for n in names], *[delta[n] for n in names],
            *[new_m[n] for n in names], *[new_v[n] for n in names])
```

```python
from typing import NamedTuple

import jax
import jax.numpy as jnp
from jax import lax
from jax.experimental import pallas as pl
from jax.experimental.pallas import tpu as pltpu
from jax.experimental.pallas import tpu_sc as plsc

F32 = jnp.float32
BF16 = jnp.bfloat16

T = 2048
D = 1024
F_FF = 2816
N_HEADS = 8
QK_NOPE = 128
QK_ROPE = 64
V_HEAD = 128
Q_LORA = 384
KV_LORA = 256
CHUNK_SHIFT = 6
ROPE_THETA = 10000.0
EPS = 1e-6
NEG_INF = -1e30
HEAD_PAD = 256
KVP = KV_LORA + 128

ADAM_LR = 0.001
ADAM_B1 = 0.9
ADAM_B2 = 0.999
ADAM_EPS = 1e-08
ADAM_WD = 0.01
ADAM_STEP = 10

N_CHIPS = 4
N_DEV = 8
LANES = 128
TC = 256
V7X_VMEM_LIMIT = 56 * 1024 * 1024

MESH = pl.DeviceIdType.MESH
ANY = pl.BlockSpec(memory_space=pl.ANY)


class _W(NamedTuple):
    name: str
    kind: str
    nl: int
    k: int
    n: int


AG_GROUPS = (
    (_W("sc_w_in", "col", 1, D, 3 * D // N_CHIPS), _W("sc_conv_w", "tiny", 1, 3, D // N_CHIPS),
     _W("ffn_conv_w", "tiny", 1, 6, F_FF // N_CHIPS), _W("sc_w_out", "row", 1, D // N_CHIPS, D)),
    (_W("ffn_w_up0", "col", 1, D, 2 * F_FF // N_CHIPS),),
    (_W("ffn_w_down0", "row", 1, F_FF // N_CHIPS, D),),
    (_W("w_kv", "row", 1, D // N_CHIPS, KVP), _W("w_ukv", "col", 2, KV_LORA, N_HEADS * QK_NOPE // N_CHIPS),
     _W("w_dq", "row", 1, D // N_CHIPS, Q_LORA),
     _W("w_uq", "col", 1, Q_LORA, N_HEADS * HEAD_PAD // N_CHIPS),
     _W("w_o", "row", 1, N_HEADS * V_HEAD // N_CHIPS, D)),
    (_W("ffn_w_up1", "col", 1, D, 2 * F_FF // N_CHIPS), _W("ffn_w_down1", "row", 1, F_FF // N_CHIPS, D)),
)


def _cp(*sem):
    return pltpu.CompilerParams(dimension_semantics=sem, vmem_limit_bytes=V7X_VMEM_LIMIT)


_ORDER = [None]


def _tc_call(body, *, name, out_shape, in_specs=None, out_specs=None, grid=(), scratch_shapes=(), prefetch=0,
             input_output_aliases=None, compiler_params=None):
    def run(*args):
        specs = [pl.BlockSpec(memory_space=pltpu.VMEM)] * (len(args) - prefetch) if in_specs is None else list(in_specs)
        inner, dep = body, _ORDER[0]
        if dep is not None:
            unread = prefetch + len(specs)
            specs, args = specs + [ANY], (*args, dep)

            def inner(*refs):
                return body(*refs[:unread], *refs[unread + 1:])

        kwargs = dict(name=name, out_shape=out_shape, input_output_aliases=input_output_aliases or {},
                      compiler_params=compiler_params)
        if prefetch:
            kwargs["grid_spec"] = pltpu.PrefetchScalarGridSpec(
                num_scalar_prefetch=prefetch, grid=grid, in_specs=specs, out_specs=out_specs,
                scratch_shapes=scratch_shapes)
        else:
            kwargs.update(grid=grid, in_specs=specs, scratch_shapes=scratch_shapes)
            if out_specs is not None:
                kwargs["out_specs"] = out_specs
        out = pl.pallas_call(inner, **kwargs)(*args)
        _ORDER[0] = out[0] if isinstance(out, (list, tuple)) else out
        return out

    return run


def _tile(n, cands):
    for c in cands:
        if n % c == 0:
            return c
    raise ValueError(f"no tile for {n}")


NN_DIMS = (((1,), (0,)), ((), ()))
NT_DIMS = (((1,), (1,)), ((), ()))
TN_DIMS = (((0,), (0,)), ((), ()))
M_TILES = (1024, 512, 384, 256, 128)
N_TILES = (1408, 1024, 768, 512, 384, 256, 128)
MM_BLOCK_BYTES = 36 * 1024 * 1024


def _fit(m, n, block_bytes, m_tiles=M_TILES, n_tiles=N_TILES):
    for tm in [c for c in m_tiles if m % c == 0]:
        for tn in [c for c in n_tiles if n % c == 0]:
            if 2 * block_bytes(tm, tn) + 4 * tm * tn <= MM_BLOCK_BYTES:
                return tm, tn
    raise ValueError(f"no tiles for {m} x {n}")


def _size(x):
    return x.dtype.itemsize


def _mm(name, a, b, dims, grid, a_spec, b_spec, o_spec, o_sds, add=None, red=None, acc_shape=None):
    n_red = None if red is None else grid[red]

    def body(*refs):
        a_ref, b_ref = refs[0], refs[1]
        add_ref = refs[2] if add is not None else None
        o_ref = refs[3] if add is not None else refs[2]
        part = lax.dot_general(a_ref[...].astype(BF16), b_ref[...].astype(BF16), dims, preferred_element_type=F32)
        if red is None:
            if add is not None:
                part = part + add_ref[...]
            o_ref[...] = part.astype(o_ref.dtype)
            return
        acc_ref = refs[-1]
        r = pl.program_id(red)

        @pl.when(r == 0)
        def _():
            acc_ref[...] = part

        @pl.when(r > 0)
        def _():
            acc_ref[...] += part

        @pl.when(r == n_red - 1)
        def _():
            o_ref[...] = acc_ref[...].astype(o_ref.dtype)

    sem = tuple("arbitrary" if ax == red else "parallel" for ax in range(len(grid)))
    in_specs = [a_spec, b_spec] + ([o_spec] if add is not None else [])
    args = (a, b) + ((add,) if add is not None else ())
    return _tc_call(
        body, name=name, grid=grid, in_specs=in_specs, out_specs=o_spec, out_shape=o_sds,
        scratch_shapes=[] if red is None else [pltpu.VMEM(acc_shape, F32)], compiler_params=_cp(*sem),
    )(*args)


def _nn(name, a, b, out_dtype, add=None, lead=None):
    (m, k), n = a.shape, b.shape[-1]
    osz = jnp.dtype(out_dtype).itemsize + (4 if add is not None else 0)
    tm, tn = _fit(m, n, lambda tm, tn: tm * k * _size(a) + k * tn * _size(b) + tm * tn * osz)
    if lead is None:
        b_spec = pl.BlockSpec((k, tn), lambda i, j: (0, j))
    else:
        b_spec = pl.BlockSpec((None, k, tn), lambda i, j: (lead, 0, j))
    return _mm(name, a, b, NN_DIMS, (m // tm, n // tn), pl.BlockSpec((tm, k), lambda i, j: (i, 0)), b_spec,
               pl.BlockSpec((tm, tn), lambda i, j: (i, j)), jax.ShapeDtypeStruct((m, n), out_dtype), add=add)


def _nn_parts(name, a, b, parts, out_dtype, lead=None, stacked=False):
    m, k = a.shape
    c = b.shape[-1] if stacked else b.shape[-1] // parts
    osz = jnp.dtype(out_dtype).itemsize
    tm, tn = _fit(m, c, lambda tm, tn: tm * k * _size(a) + k * tn * _size(b) + tm * tn * osz)
    nb = c // tn
    if stacked:
        b_spec = pl.BlockSpec((None, k, tn), lambda i, p, j: (p, 0, j))
    elif lead is None:
        b_spec = pl.BlockSpec((k, tn), lambda i, p, j: (0, p * nb + j))
    else:
        b_spec = pl.BlockSpec((None, k, tn), lambda i, p, j: (lead, 0, p * nb + j))
    return _mm(name, a, b, NN_DIMS, (m // tm, parts, nb), pl.BlockSpec((tm, k), lambda i, p, j: (i, 0)), b_spec,
               pl.BlockSpec((None, tm, tn), lambda i, p, j: (p, i, j)), jax.ShapeDtypeStruct((parts, m, c), out_dtype))


def _nt(name, a, b, out_dtype, lead=None):
    (m, k), n = a.shape, b.shape[-2]
    osz = jnp.dtype(out_dtype).itemsize
    tm, tn = _fit(m, n, lambda tm, tn: tm * k * _size(a) + tn * k * _size(b) + tm * tn * osz)
    if lead is None:
        b_spec = pl.BlockSpec((tn, k), lambda i, j: (j, 0))
    else:
        b_spec = pl.BlockSpec((None, tn, k), lambda i, j: (lead, j, 0))
    return _mm(name, a, b, NT_DIMS, (m // tm, n // tn), pl.BlockSpec((tm, k), lambda i, j: (i, 0)), b_spec,
               pl.BlockSpec((tm, tn), lambda i, j: (i, j)), jax.ShapeDtypeStruct((m, n), out_dtype))


def _tn(name, a, b, out_dtype):
    (k, m), n = a.shape, b.shape[1]
    osz = jnp.dtype(out_dtype).itemsize
    tm, tn = _fit(m, n, lambda tm, tn: k * tm * _size(a) + k * tn * _size(b) + tm * tn * osz,
                  m_tiles=(512, 384, 256, 128), n_tiles=(n,) + N_TILES)
    return _mm(name, a, b, TN_DIMS, (m // tm, n // tn), pl.BlockSpec((k, tm), lambda i, j: (0, i)),
               pl.BlockSpec((k, tn), lambda i, j: (0, j)), pl.BlockSpec((tm, tn), lambda i, j: (i, j)),
               jax.ShapeDtypeStruct((m, n), out_dtype))


def _nn_add_norm(name, a, b, add, g):
    (m, k), n = a.shape, b.shape[1]
    tm = 512

    def body(a_ref, b_ref, add_ref, g_ref, h_ref, hn_ref):
        h = jnp.dot(a_ref[...], b_ref[...], preferred_element_type=F32) + add_ref[...]
        h_ref[...] = h
        hn_ref[...] = _rms_rows(h, g_ref[...]).astype(BF16)

    rows = lambda w: pl.BlockSpec((tm, w), lambda i: (i, 0))
    return _tc_call(
        body, name=name, grid=(m // tm,),
        in_specs=[rows(k), pl.BlockSpec((k, n), lambda i: (0, 0)), rows(n), pl.BlockSpec((1, n), lambda i: (0, 0))],
        out_specs=[rows(n), rows(n)],
        out_shape=[jax.ShapeDtypeStruct((m, n), F32), jax.ShapeDtypeStruct((m, n), BF16)], compiler_params=_cp("parallel"),
    )(a, b, add, g)


def _nn_add_loss(name, a, b, add, g, tgt):
    (m, k), n = a.shape, b.shape[1]
    tm = 512

    def body(a_ref, b_ref, add_ref, g_ref, t_ref, loss_ref, dh_ref, dhb_ref, dg_ref):
        xv = jnp.dot(a_ref[...], b_ref[...], preferred_element_type=F32) + add_ref[...]
        gv = g_ref[...]
        r = lax.rsqrt(jnp.mean(xv * xv, axis=1, keepdims=True) + EPS)
        err = xv * r * gv - t_ref[...]
        part = 0.5 * jnp.sum(jnp.mean(err * err, axis=1, keepdims=True), axis=0, keepdims=True)
        dx, dg = _rms_bwd_math(xv, gv, err * (1.0 / n))
        dh_ref[...] = dx
        dhb_ref[...] = dx.astype(BF16)

        @pl.when(pl.program_id(0) == 0)
        def _():
            dg_ref[...] = jnp.zeros_like(dg_ref)
            loss_ref[...] = jnp.zeros_like(loss_ref)

        dg_ref[...] += dg
        loss_ref[...] += jnp.broadcast_to(part, loss_ref.shape)

    rows = lambda w: pl.BlockSpec((tm, w), lambda i: (i, 0))
    vec = pl.BlockSpec((1, n), lambda i: (0, 0))
    return _tc_call(
        body, name=name, grid=(m // tm,),
        in_specs=[rows(k), pl.BlockSpec((k, n), lambda i: (0, 0)), rows(n), vec, rows(n)],
        out_specs=[pl.BlockSpec((1, LANES), lambda i: (0, 0)), rows(n), rows(n), vec],
        out_shape=[jax.ShapeDtypeStruct((1, LANES), F32), jax.ShapeDtypeStruct((m, n), F32),
                   jax.ShapeDtypeStruct((m, n), BF16), jax.ShapeDtypeStruct((1, n), F32)],
        compiler_params=_cp("arbitrary"),
    )(a, b, add, g, tgt)


def _dx_norm_bwd(name, a, b, x, g, add):
    parts, t, c = a.shape
    d = b.shape[0]
    tm = 256

    def body(a_ref, b_ref, x_ref, g_ref, add_ref, dx_ref, dxb_ref, dg_ref):
        dy = None
        for p in range(parts):
            part = lax.dot_general(a_ref[p], b_ref[:, p * c:(p + 1) * c], NT_DIMS, preferred_element_type=F32)
            dy = part if dy is None else dy + part
        dx, dg = _rms_bwd_math(x_ref[...], g_ref[...], dy)
        dx = dx + add_ref[...]
        dx_ref[...] = dx
        dxb_ref[...] = dx.astype(BF16)

        @pl.when(pl.program_id(0) == 0)
        def _():
            dg_ref[...] = jnp.zeros_like(dg_ref)

        dg_ref[...] += dg

    rows = pl.BlockSpec((tm, d), lambda i: (i, 0))
    vec = pl.BlockSpec((1, d), lambda i: (0, 0))
    return _tc_call(
        body, name=name, grid=(t // tm,),
        in_specs=[pl.BlockSpec((parts, tm, c), lambda i: (0, i, 0)), pl.BlockSpec(b.shape, lambda i: (0, 0)), rows, vec,
                  rows],
        out_specs=[rows, rows, vec],
        out_shape=[jax.ShapeDtypeStruct((t, d), F32), jax.ShapeDtypeStruct((t, d), BF16),
                   jax.ShapeDtypeStruct((1, d), F32)],
        compiler_params=_cp("arbitrary"),
    )(a, b, x, g, add)


def _dw_sc_in(hn, dz):
    t, tn, tm = hn.shape[0], TC, D
    per_part, per_chip = D // tn, 3 * D // N_CHIPS // tn
    return _mm("sc_in_dw", hn, dz, TN_DIMS, (D // tm, 3 * D // tn), pl.BlockSpec((t, tm), lambda i, j: (0, i)),
               pl.BlockSpec((None, t, tn), lambda i, j: (j // per_part, 0, j % per_part)),
               pl.BlockSpec((None, tm, tn), lambda i, j: (j // per_chip, i, j % per_chip)),
               jax.ShapeDtypeStruct((N_CHIPS, D, 3 * D // N_CHIPS), BF16))


def _dw_ffn_up(name, hf, dup):
    t, tm, ns = hf.shape[0], D, 2 * F_FF // N_CHIPS
    return _mm(name, hf, dup, TN_DIMS, (N_CHIPS, D // tm), pl.BlockSpec((t, tm), lambda s, i: (0, i)),
               pl.BlockSpec((None, t, ns), lambda s, i: (s // 2, 0, s % 2)),
               pl.BlockSpec((None, tm, ns), lambda s, i: (s, i, 0)), jax.ShapeDtypeStruct((N_CHIPS, D, ns), BF16))


def _rms_fwd(x, g, name):
    t, d = x.shape
    tr = 512

    def body(x_ref, g_ref, o_ref):
        xv = x_ref[...]
        r = lax.rsqrt(jnp.mean(xv * xv, axis=1, keepdims=True) + EPS)
        o_ref[...] = (xv * r * g_ref[...]).astype(o_ref.dtype)

    row = pl.BlockSpec((tr, d), lambda i: (i, 0))
    return _tc_call(
        body, name=name, grid=(t // tr,), in_specs=[row, pl.BlockSpec((1, d), lambda i: (0, 0))],
        out_specs=row, out_shape=jax.ShapeDtypeStruct((t, d), BF16), compiler_params=_cp("parallel"),
    )(x, g)


def _rms_bwd_math(xv, g, dy):
    r = lax.rsqrt(jnp.mean(xv * xv, axis=1, keepdims=True) + EPS)
    xh = xv * r
    gy = dy * g
    dx = r * (gy - xh * jnp.mean(gy * xh, axis=1, keepdims=True))
    dg = jnp.sum(dy * xh, axis=0, keepdims=True)
    return dx, dg


def _rot_half(x):
    lane = lax.broadcasted_iota(jnp.int32, x.shape, 1)
    return jnp.where((lane % QK_ROPE) < QK_ROPE // 2, -pltpu.roll(x, LANES - 32, axis=1),
                     pltpu.roll(x, 32, axis=1))


def _rope_fwd_math(x, cos, sin):
    return x * cos + _rot_half(x) * sin


def _rope_bwd_math(dy, cos, sin):
    return dy * cos - _rot_half(dy * sin)


def _rms_rows(x, g):
    return x * lax.rsqrt(jnp.mean(x * x, axis=1, keepdims=True) + EPS) * g


def _attn_prep(h, g_attn, g_kvin, w_dq, g_ql, w_uq, w_kv, g_kvl, w_ukv, cos, sin):
    t, d = h.shape
    tr = 256
    wq = N_HEADS * HEAD_PAD

    def body(h_ref, ga_ref, gk_ref, wdq_ref, gq_ref, wuq_ref, wkv_ref, gl_ref, wukv_ref, c_ref, s_ref,
             hn_ref, hk_ref, cqp_ref, cq_ref, q_ref, kvp_ref, ckv_ref, kr_ref, knv_ref):
        xv, cv, sv = h_ref[...], c_ref[...], s_ref[...]
        xh = xv * lax.rsqrt(jnp.mean(xv * xv, axis=1, keepdims=True) + EPS)
        hn = (xh * ga_ref[...]).astype(BF16)
        hk = (xh * gk_ref[...]).astype(BF16)
        hn_ref[...], hk_ref[...] = hn, hk
        cq_pre = jnp.dot(hn, wdq_ref[...], preferred_element_type=F32)
        cqp_ref[...] = cq_pre
        cq = _rms_rows(cq_pre, gq_ref[...]).astype(BF16)
        cq_ref[...] = cq
        for hd in range(N_HEADS):
            lo = hd * HEAD_PAD
            qh = jnp.dot(cq, wuq_ref[:, lo:lo + HEAD_PAD], preferred_element_type=F32)
            q_ref[:, lo:lo + QK_NOPE] = qh[:, :QK_NOPE].astype(BF16)
            q_ref[:, lo + QK_NOPE:lo + HEAD_PAD] = _rope_fwd_math(qh[:, QK_NOPE:], cv, sv).astype(BF16)
        kvpre = jnp.dot(hk, wkv_ref[...], preferred_element_type=F32)
        kvp_ref[...] = kvpre
        ckv = _rms_rows(kvpre[:, :KV_LORA], gl_ref[...]).astype(BF16)
        ckv_ref[...] = ckv
        kr_ref[...] = _rope_fwd_math(kvpre[:, KV_LORA:], cv, sv).astype(BF16)
        for p in range(2):
            knv_ref[p] = jnp.dot(ckv, wukv_ref[p], preferred_element_type=F32).astype(BF16)

    rows = lambda w: pl.BlockSpec((tr, w), lambda i: (i, 0))
    whole = lambda a: pl.BlockSpec(a.shape, lambda i: (0,) * a.ndim)
    sds = lambda w, dt: jax.ShapeDtypeStruct((t, w), dt)
    args = (h, g_attn, g_kvin, w_dq, g_ql, w_uq, w_kv, g_kvl, w_ukv, cos, sin)
    return _tc_call(
        body, name="attn_prep", grid=(t // tr,),
        in_specs=[rows(d)] + [whole(a) for a in args[1:9]] + [rows(LANES), rows(LANES)],
        out_specs=[rows(d), rows(d), rows(Q_LORA), rows(Q_LORA), rows(wq), rows(KVP), rows(KV_LORA), rows(LANES),
                   pl.BlockSpec((2, tr, N_HEADS * QK_NOPE), lambda i: (0, i, 0))],
        out_shape=[sds(d, BF16), sds(d, BF16), sds(Q_LORA, F32), sds(Q_LORA, BF16), sds(wq, BF16), sds(KVP, F32),
                   sds(KV_LORA, BF16), sds(LANES, BF16), jax.ShapeDtypeStruct((2, t, N_HEADS * QK_NOPE), BF16)],
        compiler_params=_cp("parallel"),
    )(*args)


def _attn_prep_bwd(dq, dknv, dkr, dh, h, hn, hk, cq_pre, cq, kvpre, ckv, g_attn, g_kvin, w_dq, g_ql, w_uq, w_kv, g_kvl,
                   w_ukv, cos, sin):
    t, d = h.shape
    tr = 256
    n_steps = t // tr
    wq = N_HEADS * HEAD_PAD
    wk = N_HEADS * QK_NOPE

    def body(dq_ref, dknv_ref, dkr_ref, dh_ref, h_ref, hn_ref, hk_ref, cqp_ref, cq_ref, kvp_ref, ckv_ref,
             ga_ref, gk_ref, wdq_ref, gq_ref, wuq_ref, wkv_ref, gl_ref, wukv_ref, c_ref, s_ref,
             dho_ref, dhb_ref, dwuq_ref, dwdq_ref, dwukv_ref, dwkv_ref, dga_ref, dgk_ref, dgq_ref, dgl_ref,
             a_uq, a_dq, a_ukv, a_kv):
        i = pl.program_id(0)

        @pl.when(i == 0)
        def _():
            for ref in (a_uq, a_dq, a_ukv, a_kv, dga_ref, dgk_ref, dgq_ref, dgl_ref):
                ref[...] = jnp.zeros_like(ref)

        dqv = dq_ref[...]
        dcq = lax.dot_general(dqv, wuq_ref[...], NT_DIMS, preferred_element_type=F32)
        a_uq[...] += lax.dot_general(cq_ref[...], dqv, TN_DIMS, preferred_element_type=F32)
        dcq_pre, dg = _rms_bwd_math(cqp_ref[...], gq_ref[...], dcq)
        dgq_ref[...] += dg
        dcq_pre = dcq_pre.astype(BF16)
        dhn = lax.dot_general(dcq_pre, wdq_ref[...], NT_DIMS, preferred_element_type=F32)
        a_dq[...] += lax.dot_general(hn_ref[...], dcq_pre, TN_DIMS, preferred_element_type=F32)
        dckv = None
        for p in range(2):
            dk = dknv_ref[p].astype(BF16)
            part = lax.dot_general(dk, wukv_ref[p], NT_DIMS, preferred_element_type=F32)
            dckv = part if dckv is None else dckv + part
            a_ukv[p] += lax.dot_general(ckv_ref[...], dk, TN_DIMS, preferred_element_type=F32)
        dlat, dg = _rms_bwd_math(kvp_ref[:, :KV_LORA], gl_ref[...], dckv)
        dgl_ref[...] += dg
        dkr_pre = _rope_bwd_math(dkr_ref[...], c_ref[...], s_ref[...])
        dkvpre = jnp.concatenate([dlat, dkr_pre], axis=1).astype(BF16)
        dhk = lax.dot_general(dkvpre, wkv_ref[...], NT_DIMS, preferred_element_type=F32)
        a_kv[...] += lax.dot_general(hk_ref[...], dkvpre, TN_DIMS, preferred_element_type=F32)
        xv = h_ref[...]
        dx1, dg = _rms_bwd_math(xv, ga_ref[...], dhn)
        dga_ref[...] += dg
        dx2, dg = _rms_bwd_math(xv, gk_ref[...], dhk)
        dgk_ref[...] += dg
        dh_new = dh_ref[...] + dx1 + dx2
        dho_ref[...] = dh_new
        dhb_ref[...] = dh_new.astype(BF16)

        @pl.when(i == n_steps - 1)
        def _():
            dwuq_ref[...] = a_uq[...].astype(BF16)
            dwdq_ref[...] = a_dq[...].astype(BF16)
            dwukv_ref[...] = a_ukv[...].astype(BF16)
            dwkv_ref[...] = a_kv[...].astype(BF16)

    rows = lambda w: pl.BlockSpec((tr, w), lambda i: (i, 0))
    whole = lambda shape: pl.BlockSpec(shape, lambda i: (0,) * len(shape))
    weights = (g_attn, g_kvin, w_dq, g_ql, w_uq, w_kv, g_kvl, w_ukv)
    dw_shapes = [(Q_LORA, wq), (d, Q_LORA), (2, KV_LORA, wk), (d, KVP)]
    dg_shapes = [(1, d), (1, d), (1, Q_LORA), (1, KV_LORA)]
    return _tc_call(
        body, name="attn_prep_bwd", grid=(n_steps,),
        in_specs=[rows(wq), pl.BlockSpec((2, tr, wk), lambda i: (0, i, 0)), rows(LANES), rows(d), rows(d), rows(d),
                  rows(d), rows(Q_LORA), rows(Q_LORA), rows(KVP), rows(KV_LORA)]
        + [whole(a.shape) for a in weights] + [rows(LANES), rows(LANES)],
        out_specs=[rows(d), rows(d)] + [whole(s) for s in dw_shapes + dg_shapes],
        out_shape=[jax.ShapeDtypeStruct((t, d), F32), jax.ShapeDtypeStruct((t, d), BF16)]
        + [jax.ShapeDtypeStruct(s, BF16) for s in dw_shapes] + [jax.ShapeDtypeStruct(s, F32) for s in dg_shapes],
        scratch_shapes=[pltpu.VMEM(s, F32) for s in dw_shapes], compiler_params=_cp("arbitrary"),
    )(dq, dknv, dkr, dh, h, hn, hk, cq_pre, cq, kvpre, ckv, *weights, cos, sin)


ROW_CHUNK = 64
HALO = 16
WIN = ROW_CHUNK + 16
LANE_HALVES = (slice(0, LANES), slice(LANES, TC))


def _stage(s_ref, p, src):
    t = src.shape[0]
    s_ref[p, :HALO] = jnp.zeros((HALO, TC), BF16)
    s_ref[p, HALO:HALO + t] = src
    s_ref[p, HALO + t:] = jnp.zeros((HALO, TC), BF16)


def _window(s_ref, p, i, lanes):
    base = pl.multiple_of(i * ROW_CHUNK, ROW_CHUNK)
    return s_ref[p, pl.ds(base, ROW_CHUNK + 2 * HALO), lanes].astype(F32)[8:8 + WIN]


def _valid(x):
    return x[8:8 + ROW_CHUNK]


def _prev(x, k):
    return pltpu.roll(x, k, axis=0)


def _next(x, k):
    return pltpu.roll(x, WIN - k, axis=0)


def _taps(w_ref, lanes):
    return w_ref[0:1, lanes], w_ref[1:2, lanes], w_ref[2:3, lanes]


def _fold8(x):
    return jnp.sum(x.reshape(ROW_CHUNK // 8, 8, x.shape[-1]), axis=0)


def _store_rows(ref, idx, i, lanes, x):
    rows = pl.ds(pl.multiple_of(i * ROW_CHUNK, ROW_CHUNK), ROW_CHUNK)
    ref[(*idx, rows, lanes)] = x.astype(ref.dtype)


def _for_chunks(t, chunk):
    def step(i, carry):
        for lanes in LANE_HALVES:
            chunk(i, lanes)
        return carry

    lax.fori_loop(0, t // ROW_CHUNK, step, 0)


def _write_col_sums(acc_ref, outs):
    for k, (ref, row) in enumerate(outs):
        ref[row:row + 1, :] = jnp.sum(acc_ref[k], axis=0, keepdims=True)


def _shift_down(x, k):
    row = lax.broadcasted_iota(jnp.int32, x.shape, 0)
    return jnp.where(row >= k, pltpu.roll(x, k, axis=0), 0.0)


def _shift_up(x, k):
    n = x.shape[0]
    row = lax.broadcasted_iota(jnp.int32, x.shape, 0)
    return jnp.where(row < n - k, pltpu.roll(x, n - k, axis=0), 0.0)


def _conv3(x, w_ref):
    return _shift_down(x, 2) * w_ref[0:1, :] + _shift_down(x, 1) * w_ref[1:2, :] + x * w_ref[2:3, :]


def _col(parts, t):
    if parts is None:
        return pl.BlockSpec((t, TC), lambda j: (0, j))
    return pl.BlockSpec((parts, t, TC), lambda j: (0, 0, j))


def _staging(parts, t):
    return pltpu.VMEM((parts, t + 2 * HALO, TC), BF16)


def _scmix_fwd(z, w):
    t = z.shape[1]

    def body(z_ref, w_ref, m_ref):
        b, c, u = (z_ref[p].astype(F32) for p in range(3))
        m_ref[...] = (b * _conv3(c * u, w_ref)).astype(BF16)

    return _tc_call(
        body, name="scmix_fwd", grid=(D // TC,), in_specs=[_col(3, t), pl.BlockSpec((3, TC), lambda j: (0, j))],
        out_specs=_col(None, t), out_shape=jax.ShapeDtypeStruct((t, D), BF16), compiler_params=_cp("parallel"),
    )(z, w)


def _scmix_bwd(z, w, dm):
    t = z.shape[1]

    def body(z_ref, w_ref, dm_ref, dz_ref, dw_ref, s_ref, acc_ref):
        for p in range(3):
            _stage(s_ref, p, z_ref[p])
        _stage(s_ref, 3, dm_ref[...])
        acc_ref[...] = jnp.zeros_like(acc_ref)

        def chunk(i, lanes):
            w0, w1, w2 = _taps(w_ref, lanes)
            b, c, u, dm = (_window(s_ref, p, i, lanes) for p in range(4))
            cu = c * u
            cu1, cu2 = _prev(cu, 1), _prev(cu, 2)
            _store_rows(dz_ref, (0,), i, lanes, _valid(dm * (cu2 * w0 + cu1 * w1 + cu * w2)))
            dcv = dm * b
            dcu = dcv * w2 + _next(dcv, 1) * w1 + _next(dcv, 2) * w0
            _store_rows(dz_ref, (1,), i, lanes, _valid(dcu * u))
            _store_rows(dz_ref, (2,), i, lanes, _valid(dcu * c))
            for k, shifted in enumerate((cu2, cu1, cu)):
                acc_ref[k, :, lanes] += _fold8(_valid(dcv * shifted))

        _for_chunks(t, chunk)
        _write_col_sums(acc_ref, [(dw_ref, 0), (dw_ref, 1), (dw_ref, 2)])

    wspec = pl.BlockSpec((3, TC), lambda j: (0, j))
    return _tc_call(
        body, name="scmix_bwd", grid=(D // TC,), in_specs=[_col(3, t), wspec, _col(None, t)],
        out_specs=[_col(3, t), wspec],
        out_shape=[jax.ShapeDtypeStruct((3, t, D), BF16), jax.ShapeDtypeStruct((3, D), F32)],
        scratch_shapes=[_staging(4, t), pltpu.VMEM((3, 8, TC), F32)], compiler_params=_cp("parallel"),
    )(z, w, dm)


def _ffn_up_gate(hf, w_up, w, bias, name):
    t, d = hf.shape
    nb = F_FF // TC

    def body(hf_ref, wg_ref, wv_ref, w_ref, b_ref, up_ref, a_ref, prev_ref):
        @pl.when(pl.program_id(0) == 0)
        def _():
            prev_ref[...] = jnp.zeros_like(prev_ref)

        gc = _conv3(prev_ref[0].astype(F32), w_ref) + b_ref[...]
        a_ref[...] = (gc * jax.nn.sigmoid(gc) * prev_ref[1].astype(F32)).astype(BF16)
        hv = hf_ref[...]
        up_ref[0] = jnp.dot(hv, wg_ref[...], preferred_element_type=F32).astype(BF16)
        up_ref[1] = jnp.dot(hv, wv_ref[...], preferred_element_type=F32).astype(BF16)
        prev_ref[...] = up_ref[...]

    tile = lambda j: jnp.minimum(j, nb - 1)
    gated = lambda j: jnp.maximum(j - 1, 0)
    return _tc_call(
        body, name=name, grid=(nb + 1,),
        in_specs=[pl.BlockSpec((t, d), lambda j: (0, 0)), pl.BlockSpec((d, TC), lambda j: (0, tile(j))),
                  pl.BlockSpec((d, TC), lambda j: (0, nb + tile(j))), pl.BlockSpec((3, TC), lambda j: (0, gated(j))),
                  pl.BlockSpec((1, TC), lambda j: (0, gated(j)))],
        out_specs=[pl.BlockSpec((2, t, TC), lambda j: (0, 0, tile(j))), pl.BlockSpec((t, TC), lambda j: (0, gated(j)))],
        out_shape=[jax.ShapeDtypeStruct((2, t, F_FF), BF16), jax.ShapeDtypeStruct((t, F_FF), BF16)],
        scratch_shapes=[pltpu.VMEM((2, t, TC), BF16)], compiler_params=_cp("arbitrary"),
    )(hf, w_up, w_up, w, bias)


def _gate_bwd(up, w, bias, dh, w_down, name):
    t, d = dh.shape

    def body(u_ref, w_ref, b_ref, dh_ref, wd_ref, du_ref, dw_ref, db_ref, s_ref, acc_ref):
        for p in range(2):
            _stage(s_ref, p, u_ref[p])
        _stage(s_ref, 2, lax.dot_general(dh_ref[...], wd_ref[...], NT_DIMS, preferred_element_type=F32).astype(BF16))
        acc_ref[...] = jnp.zeros_like(acc_ref)

        def chunk(i, lanes):
            w0, w1, w2 = _taps(w_ref, lanes)
            g, v, da = (_window(s_ref, p, i, lanes) for p in range(3))
            g1, g2 = _prev(g, 1), _prev(g, 2)
            gc = g2 * w0 + g1 * w1 + g * w2 + b_ref[:, lanes]
            sg = jax.nn.sigmoid(gc)
            _store_rows(du_ref, (1,), i, lanes, _valid(da * (gc * sg)))
            dgc = da * v * (sg * (1.0 + gc * (1.0 - sg)))
            _store_rows(du_ref, (0,), i, lanes, _valid(dgc * w2 + _next(dgc, 1) * w1 + _next(dgc, 2) * w0))
            for k, shifted in enumerate((g2, g1, g)):
                acc_ref[k, :, lanes] += _fold8(_valid(dgc * shifted))
            acc_ref[3, :, lanes] += _fold8(_valid(dgc))

        _for_chunks(t, chunk)
        _write_col_sums(acc_ref, [(dw_ref, 0), (dw_ref, 1), (dw_ref, 2), (db_ref, 0)])

    wspec = pl.BlockSpec((3, TC), lambda j: (0, j))
    bspec = pl.BlockSpec((1, TC), lambda j: (0, j))
    return _tc_call(
        body, name=name, grid=(F_FF // TC,),
        in_specs=[_col(2, t), wspec, bspec, pl.BlockSpec((t, d), lambda j: (0, 0)), pl.BlockSpec((TC, d), lambda j: (j, 0))],
        out_specs=[_col(2, t), wspec, bspec],
        out_shape=[jax.ShapeDtypeStruct((2, t, F_FF), BF16), jax.ShapeDtypeStruct((3, F_FF), F32),
                   jax.ShapeDtypeStruct((1, F_FF), F32)],
        scratch_shapes=[_staging(3, t), pltpu.VMEM((4, 8, TC), F32)], compiler_params=_cp("parallel"),
    )(up, w, bias, dh, w_down)


ATT_TQ = 256
ATT_SCALE = (QK_NOPE + QK_ROPE) ** -0.5


def _key_ranges(lvl):
    lo = lvl * ATT_TQ
    return ([(0, lo, False)] if lvl else []) + [(lo, lo + ATT_TQ, True)]


FWD_HEADS = 4
BWD_HEADS = 2


def _fill_keys(k_ref, kn_ref, kr_ref):
    @pl.when(pl.program_id(1) == 0)
    def _():
        for hh in range(k_ref.shape[0]):
            k_ref[hh, :, :QK_NOPE] = kn_ref[:, hh * QK_NOPE:(hh + 1) * QK_NOPE]
            k_ref[hh, :, QK_NOPE:] = kr_ref[...]


def _attn_probs(q, k_ref, lvl):
    scores = []
    for lo, hi, diagonal in _key_ranges(lvl):
        s = lax.dot_general(q, k_ref[lo:hi, :], NT_DIMS, preferred_element_type=F32) * ATT_SCALE
        if diagonal:
            row = lax.broadcasted_iota(jnp.int32, s.shape, 0)
            col = lax.broadcasted_iota(jnp.int32, s.shape, 1)
            seen = lax.shift_right_logical(col, CHUNK_SHIFT) <= lax.shift_right_logical(row, CHUNK_SHIFT)
            s = jnp.where(seen, s, NEG_INF)
        scores.append(s)
    m = jnp.max(scores[0], axis=1, keepdims=True)
    for s in scores[1:]:
        m = jnp.maximum(m, jnp.max(s, axis=1, keepdims=True))
    ps = [jnp.exp(s - m) for s in scores]
    total = jnp.sum(ps[0], axis=1, keepdims=True)
    for p in ps[1:]:
        total = total + jnp.sum(p, axis=1, keepdims=True)
    inv = 1.0 / total
    return [p * inv for p in ps]


def _attn_probs_t(q, k_ref, lvl):
    scores = []
    for lo, hi, diagonal in _key_ranges(lvl):
        s = lax.dot_general(k_ref[lo:hi, :], q, NT_DIMS, preferred_element_type=F32) * ATT_SCALE
        if diagonal:
            key = lax.broadcasted_iota(jnp.int32, s.shape, 0)
            qry = lax.broadcasted_iota(jnp.int32, s.shape, 1)
            seen = lax.shift_right_logical(key, CHUNK_SHIFT) <= lax.shift_right_logical(qry, CHUNK_SHIFT)
            s = jnp.where(seen, s, NEG_INF)
        scores.append(s)
    m = jnp.max(scores[0], axis=0, keepdims=True)
    for s in scores[1:]:
        m = jnp.maximum(m, jnp.max(s, axis=0, keepdims=True))
    ps = [jnp.exp(s - m) for s in scores]
    total = jnp.sum(ps[0], axis=0, keepdims=True)
    for p in ps[1:]:
        total = total + jnp.sum(p, axis=0, keepdims=True)
    inv = 1.0 / total
    return [p * inv for p in ps]


def _per_query_block(qi, n_blocks, branch):
    for lvl in range(n_blocks):
        pl.when(qi == lvl)(lambda lvl=lvl: branch(lvl))


def _attn_specs(t, g):
    q = pl.BlockSpec((ATT_TQ, g * HEAD_PAD), lambda h, i: (i, h))
    kn = pl.BlockSpec((None, t, g * QK_NOPE), lambda h, i: (0, 0, h))
    kr = pl.BlockSpec((t, LANES), lambda h, i: (0, 0))
    v = pl.BlockSpec((None, t, g * V_HEAD), lambda h, i: (1, 0, h))
    o = pl.BlockSpec((ATT_TQ, g * V_HEAD), lambda h, i: (i, h))
    return q, kn, kr, v, o


def _attn_fwd(q, knv, kr):
    t = q.shape[0]

    def body(q_ref, kn_ref, kr_ref, v_ref, o_ref, k_ref):
        _fill_keys(k_ref, kn_ref, kr_ref)

        def branch(lvl):
            for hh in range(FWD_HEADS):
                vcols = slice(hh * V_HEAD, (hh + 1) * V_HEAD)
                ps = _attn_probs(q_ref[:, hh * HEAD_PAD:(hh + 1) * HEAD_PAD], k_ref.at[hh], lvl)
                o = None
                for p, (lo, hi, _) in zip(ps, _key_ranges(lvl)):
                    part = jnp.dot(p.astype(BF16), v_ref[lo:hi, vcols], preferred_element_type=F32)
                    o = part if o is None else o + part
                o_ref[:, vcols] = o.astype(BF16)

        _per_query_block(pl.program_id(1), t // ATT_TQ, branch)

    qs, kns, krs, vs, os_ = _attn_specs(t, FWD_HEADS)
    return _tc_call(
        body, name="attn_fwd", grid=(N_HEADS // FWD_HEADS, t // ATT_TQ), in_specs=[qs, kns, krs, vs],
        out_specs=os_, out_shape=jax.ShapeDtypeStruct((t, N_HEADS * V_HEAD), BF16),
        scratch_shapes=[pltpu.VMEM((FWD_HEADS, t, HEAD_PAD), BF16)], compiler_params=_cp("parallel", "arbitrary"),
    )(q, knv, kr, knv)


def _attn_bwd(q, knv, kr, do, cos, sin):
    t = q.shape[0]

    def body(q_ref, kn_ref, kr_ref, v_ref, do_ref, c_ref, s_ref, dq_ref, dknv_ref, dkr_ref, k_ref, dk_ref):
        h, qi = pl.program_id(0), pl.program_id(1)
        _fill_keys(k_ref, kn_ref, kr_ref)

        @pl.when(qi == 0)
        def _():
            dknv_ref[1] = jnp.zeros(dknv_ref.shape[1:], F32)
            dk_ref[...] = jnp.zeros_like(dk_ref)

        @pl.when((qi == 0) & (h == 0))
        def _():
            dkr_ref[...] = jnp.zeros_like(dkr_ref)

        def branch(lvl):
            ranges = _key_ranges(lvl)
            for hh in range(BWD_HEADS):
                qcols = slice(hh * HEAD_PAD, (hh + 1) * HEAD_PAD)
                vcols = slice(hh * V_HEAD, (hh + 1) * V_HEAD)
                qv, dov = q_ref[:, qcols], do_ref[:, vcols]
                ps = _attn_probs_t(qv, k_ref.at[hh], lvl)
                dps = [lax.dot_general(v_ref[lo:hi, vcols], dov, NT_DIMS, preferred_element_type=F32)
                       for lo, hi, _ in ranges]
                di = None
                for p, dp in zip(ps, dps):
                    part = jnp.sum(p * dp, axis=0, keepdims=True)
                    di = part if di is None else di + part
                dq = None
                for p, dp, (lo, hi, _) in zip(ps, dps, ranges):
                    ds = (p * (dp - di) * ATT_SCALE).astype(BF16)
                    part = lax.dot_general(ds, k_ref[hh, lo:hi, :], TN_DIMS, preferred_element_type=F32)
                    dq = part if dq is None else dq + part
                    dk_ref[hh, lo:hi, :] += jnp.dot(ds, qv, preferred_element_type=F32)
                    dknv_ref[1, lo:hi, vcols] += jnp.dot(p.astype(BF16), dov, preferred_element_type=F32)
                dq_ref[:, hh * HEAD_PAD:hh * HEAD_PAD + QK_NOPE] = dq[:, :QK_NOPE].astype(BF16)
                dq_ref[:, hh * HEAD_PAD + QK_NOPE:(hh + 1) * HEAD_PAD] = _rope_bwd_math(
                    dq[:, QK_NOPE:], c_ref[...], s_ref[...]).astype(BF16)

        _per_query_block(qi, t // ATT_TQ, branch)

        @pl.when(qi == t // ATT_TQ - 1)
        def _():
            for hh in range(BWD_HEADS):
                dknv_ref[0, :, hh * QK_NOPE:(hh + 1) * QK_NOPE] = dk_ref[hh, :, :QK_NOPE]
                dkr_ref[...] += dk_ref[hh, :, QK_NOPE:]

    qs, kns, krs, vs, os_ = _attn_specs(t, BWD_HEADS)
    tab = pl.BlockSpec((ATT_TQ, LANES), lambda h, i: (i, 0))
    return _tc_call(
        body, name="attn_bwd", grid=(N_HEADS // BWD_HEADS, t // ATT_TQ), in_specs=[qs, kns, krs, vs, os_, tab, tab],
        out_specs=[qs, pl.BlockSpec((2, t, BWD_HEADS * QK_NOPE), lambda h, i: (0, 0, h)), krs],
        out_shape=[jax.ShapeDtypeStruct((t, N_HEADS * HEAD_PAD), BF16),
                   jax.ShapeDtypeStruct((2, t, N_HEADS * QK_NOPE), F32), jax.ShapeDtypeStruct((t, LANES), F32)],
        scratch_shapes=[pltpu.VMEM((BWD_HEADS, t, HEAD_PAD), BF16), pltpu.VMEM((BWD_HEADS, t, HEAD_PAD), F32)],
        compiler_params=_cp("arbitrary", "arbitrary"),
    )(q, knv, kr, knv, do, cos, sin)


def _adam_math(w, g, m, v):
    nm = ADAM_B1 * m + (1.0 - ADAM_B1) * g
    nv = ADAM_B2 * v + (1.0 - ADAM_B2) * (g * g)
    m_hat = nm / (1.0 - ADAM_B1 ** ADAM_STEP)
    v_hat = nv / (1.0 - ADAM_B2 ** ADAM_STEP)
    return -ADAM_LR * (m_hat / (jnp.sqrt(v_hat) + ADAM_EPS) + ADAM_WD * w), nm, nv


def _adamw_small(w, g, m, v):
    def body(w_ref, g_ref, m_ref, v_ref, d_ref, nm_ref, nv_ref):
        d_ref[...], nm_ref[...], nv_ref[...] = _adam_math(w_ref[...], g_ref[...], m_ref[...], v_ref[...])

    shp = jax.ShapeDtypeStruct(w.shape, F32)
    return _tc_call(body, name="adamw_small", out_shape=[shp] * 3)(w, g, m, v)


ADAM_SPLIT = 4


def _adamw_shards(ids, items, name):
    n = len(items)

    def body(ids_ref, *refs):
        outs = refs[len(refs) - 4 * n:]
        mine = pl.program_id(0) == ids_ref[0]
        for i in range(n):
            w_ref, m_ref, v_ref, gm_ref, gs_ref = refs[5 * i:5 * i + 5]
            g_ref, d_ref, nm_ref, nv_ref = outs[4 * i:4 * i + 4]

            @pl.when(mine)
            def _(g_ref=g_ref, gm_ref=gm_ref):
                g_ref[...] = gm_ref[...]

            @pl.when(jnp.logical_not(mine))
            def _(g_ref=g_ref, gs_ref=gs_ref):
                g_ref[...] = gs_ref[...]

            d_ref[...], nm_ref[...], nv_ref[...] = _adam_math(w_ref[...], g_ref[...], m_ref[...], v_ref[...])

    in_specs, out_specs, out_shape, args, carried, aliases = [], [], [], [ids], [], {}
    for i, it in enumerate(items):
        w = it["w"]
        r, c = w.shape[-2:]
        tr = r // 2 // ADAM_SPLIT
        assert tr % 8 == 0, (name, w.shape)
        layer = it.get("layer")
        if layer is None:
            wspec = pl.BlockSpec((tr, c), lambda h, k, ids: (h * ADAM_SPLIT + k, 0))
        else:
            wspec = pl.BlockSpec((None, tr, c), lambda h, k, ids, layer=layer: (layer, h * ADAM_SPLIT + k, 0))
        gspec = pl.BlockSpec((tr, c), lambda h, k, ids: (k, 0))
        in_specs += [wspec] * 3 + [gspec] * 2
        args += [w, it["m"], it["v"], it["g_mine"], it["g_sib"]]
        out_specs += [wspec] * 4
        out_shape += [jax.ShapeDtypeStruct(w.shape, F32)] * 4
        if it.get("prev") is not None:
            for k, p in enumerate(it["prev"]):
                aliases[1 + 5 * n + len(carried)] = 4 * i + k
                carried.append(p)
    res = _tc_call(
        body, name=name, prefetch=1, grid=(2, ADAM_SPLIT), in_specs=in_specs + [ANY] * len(carried),
        out_specs=out_specs, out_shape=out_shape, input_output_aliases=aliases,
        compiler_params=_cp("parallel", "parallel"),
    )(*args, *carried)
    return [res[4 * i:4 * i + 4] for i in range(n)]


def _peer_chip(k_me, j):
    return k_me ^ jnp.where(j == 0, 2, jnp.where(j == 1, 1, 3))


def _pair_sums(ids, gs, ras, name):
    n = len(gs)

    def body(ids_ref, *refs):
        for i in range(n):
            g_ref, ra_ref, o_ref = refs[2 * i], refs[2 * i + 1], refs[2 * n + i]
            o_ref[...] = (g_ref[...].astype(F32) + ra_ref[...].astype(F32)).astype(BF16)

    in_specs, out_specs, out_shape = [], [], []
    for g in gs:
        half, c = g.shape[1] // 2, g.shape[2]
        in_specs += [pl.BlockSpec((None, half, c), lambda j, ids: (_peer_chip(ids[1], j), ids[0], 0)),
                     pl.BlockSpec((None, half, c), lambda j, ids: (_peer_chip(ids[1], j), 0, 0))]
        out_specs.append(pl.BlockSpec((None, half, c), lambda j, ids: (j, 0, 0)))
        out_shape.append(jax.ShapeDtypeStruct((3, half, c), BF16))
    return _tc_call(
        body, name=name, prefetch=1, grid=(3,), in_specs=in_specs, out_specs=out_specs, out_shape=out_shape,
        compiler_params=_cp("parallel"),
    )(ids, *[a for pair in zip(gs, ras) for a in pair])


def _chip_sums(ids, gs, ras, rbs, name):
    n = len(gs)

    def body(ids_ref, *refs):
        for i in range(n):
            g_ref, ra_ref, rb_ref, o_ref = refs[3 * i], refs[3 * i + 1], refs[3 * i + 2], refs[3 * n + i]
            acc = g_ref[...].astype(F32) + ra_ref[...].astype(F32)
            for j in range(3):
                acc = acc + rb_ref[j].astype(F32)
            o_ref[...] = acc

    in_specs, out_specs, out_shape = [], [], []
    for g in gs:
        half, c = g.shape[1] // 2, g.shape[2]
        in_specs += [pl.BlockSpec((None, half, c), lambda i, ids: (ids[1], ids[0], 0)),
                     pl.BlockSpec((None, half, c), lambda i, ids: (ids[1], 0, 0)),
                     pl.BlockSpec((3, half, c), lambda i, ids: (0, 0, 0))]
        out_specs.append(pl.BlockSpec((half, c), lambda i, ids: (0, 0)))
        out_shape.append(jax.ShapeDtypeStruct((half, c), F32))
    return _tc_call(
        body, name=name, prefetch=1, grid=(1,), in_specs=in_specs, out_specs=out_specs, out_shape=out_shape,
        compiler_params=_cp("arbitrary"),
    )(ids, *[a for trio in zip(gs, ras, rbs) for a in trio])


def _position():
    x, y, c = lax.axis_index("x"), lax.axis_index("y"), lax.axis_index("c")
    chips = [(1 - x, y), (x, 1 - y), (1 - x, 1 - y)]
    return x, y, c, chips


def _shard_half(ref, wm, h):
    if wm.kind == "tiny":
        return ref
    if wm.nl == 2:
        return ref.at[h]
    return ref.at[pl.ds(pl.multiple_of(h * (wm.k // 2), 16), wm.k // 2), :]


def _region(full, wm, s, h):
    if wm.kind == "tiny":
        return full.at[s]
    cols = pl.ds(pl.multiple_of(s * wm.n, LANES), wm.n) if wm.kind == "col" else slice(None)
    if wm.nl == 2:
        rows = pl.ds(pl.multiple_of(s * wm.k, 16), wm.k) if wm.kind == "row" else slice(None)
        return full.at[slice(None) if h is None else h, rows, cols]
    if wm.kind == "col":
        rows = slice(None) if h is None else pl.ds(pl.multiple_of(h * (wm.k // 2), 16), wm.k // 2)
    elif h is None:
        rows = pl.ds(pl.multiple_of(s * wm.k, 16), wm.k)
    else:
        rows = pl.ds(pl.multiple_of(s * wm.k + h * (wm.k // 2), 16), wm.k // 2)
    return full.at[rows, cols]


def _full_shape(wm):
    if wm.kind == "tiny":
        return (N_CHIPS, wm.k, wm.n)
    shape = (wm.k, N_CHIPS * wm.n) if wm.kind == "col" else (N_CHIPS * wm.k, wm.n)
    return shape if wm.nl == 1 else (wm.nl,) + shape


def _handshake(peers):
    barrier = pltpu.get_barrier_semaphore()
    for peer in peers:
        pl.semaphore_signal(barrier, inc=1, device_id=peer, device_id_type=MESH)
    pl.semaphore_wait(barrier, len(peers))


def _all_gather_group(gi, shards):
    wms = AG_GROUPS[gi]
    nw = len(wms)

    def body(*refs):
        sh, full = refs[:nw], refs[nw:2 * nw]
        ici_s, ici_r, pass_s, pass_r, own_s, own_r = refs[2 * nw:]
        x, y, c, _ = _position()
        me, sibling = 2 * x + y, (x, y, 1 - c)
        first, second, diagonal = (x ^ (1 - c), y ^ c), (x ^ c, y ^ (1 - c)), (1 - x, 1 - y)
        chip_id = lambda chip: 2 * chip[0] + chip[1]
        _handshake([(*first, c), (*second, c), sibling])

        def rcopy(src, dst, s_sem, r_sem, to):
            return pltpu.make_async_remote_copy(src_ref=src, dst_ref=dst, send_sem=s_sem, recv_sem=r_sem,
                                                device_id=to, device_id_type=MESH)

        started = []

        def go(cp):
            cp.start()
            started.append(cp)

        for i, wm in enumerate(wms):
            half, dst = _shard_half(sh[i], wm, c), _region(full[i], wm, me, c)
            go(rcopy(half, dst, ici_s.at[i, 0], ici_r.at[i, 0], (*first, c)))
            go(rcopy(half, dst, ici_s.at[i, 1], ici_r.at[i, 1], (*second, c)))
            go(rcopy(sh[i], _region(full[i], wm, me, None), own_s.at[i], own_r.at[i], sibling))
        for i, wm in enumerate(wms):
            got = _region(full[i], wm, chip_id(first), c)
            rcopy(got, got, ici_s.at[i, 0], ici_r.at[i, 0], sibling).wait_recv()
            go(rcopy(got, got, ici_s.at[i, 2], ici_r.at[i, 2], (*second, c)))
            if wm.kind != "tiny":
                go(rcopy(got, got, pass_s.at[i, 0], pass_r.at[i, 0], sibling))
        for i, wm in enumerate(wms):
            for j, chip in ((1, second), (2, diagonal)):
                got = _region(full[i], wm, chip_id(chip), c)
                rcopy(got, got, ici_s.at[i, j], ici_r.at[i, j], sibling).wait_recv()
                if wm.kind != "tiny":
                    go(rcopy(got, got, pass_s.at[i, j], pass_r.at[i, j], sibling))
        for i, wm in enumerate(wms):
            mine = _region(full[i], wm, me, None)
            rcopy(mine, mine, own_s.at[i], own_r.at[i], sibling).wait_recv()
            if wm.kind != "tiny":
                for j, chip in ((0, second), (1, first), (2, diagonal)):
                    got = _region(full[i], wm, chip_id(chip), 1 - c)
                    rcopy(got, got, pass_s.at[i, j], pass_r.at[i, j], sibling).wait_recv()
        for cp in started:
            cp.wait_send()

    return pl.kernel(
        body, out_type=[jax.ShapeDtypeStruct(_full_shape(wm), s.dtype) for wm, s in zip(wms, shards)],
        mesh=plsc.ScalarSubcoreMesh(axis_name="sequencer", num_cores=1), name=f"ag_group{gi}",
        scratch_types=[pltpu.SemaphoreType.DMA((nw, 3))] * 4 + [pltpu.SemaphoreType.DMA((nw,))] * 2,
        compiler_params=pltpu.CompilerParams(collective_id=gi),
    )(*shards)


def _sequencer_call(body, name, cid, out_types, scratch, args):
    return pl.kernel(
        body, out_type=out_types, mesh=plsc.ScalarSubcoreMesh(axis_name="sequencer", num_cores=1), name=name,
        scratch_types=scratch, compiler_params=pltpu.CompilerParams(collective_id=cid),
    )(*args)


def _pair_exchange(gs, tag, cid):
    n = len(gs)

    def body(*refs):
        g, out, send_sems, recv_sems = refs[:n], refs[n:2 * n], refs[2 * n], refs[2 * n + 1]
        x, y, c, _ = _position()
        _handshake([(x, y, 1 - c)])
        cps = []
        for i in range(n):
            half = g[i].shape[1] // 2
            cps.append(pltpu.make_async_remote_copy(
                src_ref=g[i].at[:, pl.ds(pl.multiple_of((1 - c) * half, 16), half), :], dst_ref=out[i],
                send_sem=send_sems.at[i], recv_sem=recv_sems.at[i], device_id=(x, y, 1 - c), device_id_type=MESH))
            cps[-1].start()
        for cp in cps:
            cp.wait()

    return _sequencer_call(
        body, f"rs_pair_exchange{tag}", cid,
        [jax.ShapeDtypeStruct((a.shape[0], a.shape[1] // 2, a.shape[2]), a.dtype) for a in gs],
        [pltpu.SemaphoreType.DMA((n,)), pltpu.SemaphoreType.DMA((n,))], gs)


def _chip_exchange(ss, tag, cid):
    n = len(ss)

    def body(*refs):
        s, out, send_sems, recv_sems = refs[:n], refs[n:2 * n], refs[2 * n], refs[2 * n + 1]
        x, y, c, chips = _position()
        _handshake([(*chip, c) for chip in chips])
        cps = []
        for i in range(n):
            for j, chip in enumerate(chips):
                cps.append(pltpu.make_async_remote_copy(
                    src_ref=s[i].at[j], dst_ref=out[i].at[j], send_sem=send_sems.at[i, j], recv_sem=recv_sems.at[i, j],
                    device_id=(*chip, c), device_id_type=MESH))
                cps[-1].start()
        for cp in cps:
            cp.wait()

    return _sequencer_call(
        body, f"rs_chip_exchange{tag}", cid, [jax.ShapeDtypeStruct(a.shape, a.dtype) for a in ss],
        [pltpu.SemaphoreType.DMA((n, 3)), pltpu.SemaphoreType.DMA((n, 3))], ss)


def _pair_swap(g8s, tag, cid):
    n = len(g8s)

    def body(*refs):
        g, out, send_sems, recv_sems = refs[:n], refs[n:2 * n], refs[2 * n], refs[2 * n + 1]
        x, y, c, _ = _position()
        _handshake([(x, y, 1 - c)])
        cps = []
        for i in range(n):
            cps.append(pltpu.make_async_remote_copy(
                src_ref=g[i], dst_ref=out[i], send_sem=send_sems.at[i], recv_sem=recv_sems.at[i],
                device_id=(x, y, 1 - c), device_id_type=MESH))
            cps[-1].start()
        for cp in cps:
            cp.wait()

    return _sequencer_call(
        body, f"rs_pair_swap{tag}", cid, [jax.ShapeDtypeStruct(a.shape, a.dtype) for a in g8s],
        [pltpu.SemaphoreType.DMA((n,)), pltpu.SemaphoreType.DMA((n,))], g8s)


def _all_reduce_small(vec, name):
    r, cols = vec.shape

    def body(v_ref, o_ref, gath, send_sems, recv_sems):
        x, y, c, _ = _position()
        me = 4 * x + 2 * y + c
        gath[me] = v_ref[...]
        cps = []
        for rel in range(1, N_DEV):
            peer = (x ^ (rel >> 2), y ^ ((rel >> 1) & 1), c ^ (rel & 1))
            cps.append(pltpu.make_async_remote_copy(
                src_ref=v_ref, dst_ref=gath.at[me], send_sem=send_sems.at[rel - 1], recv_sem=recv_sems.at[rel - 1],
                device_id=peer, device_id_type=MESH))
        for cp in cps:
            cp.start()
        for rel in range(1, N_DEV):
            pltpu.make_async_remote_copy(
                src_ref=v_ref, dst_ref=gath.at[me ^ rel], send_sem=send_sems.at[rel - 1],
                recv_sem=recv_sems.at[rel - 1], device_id=(x, y, c), device_id_type=MESH).wait_recv()
        for cp in cps:
            cp.wait_send()
        acc = gath[0]
        for d in range(1, N_DEV):
            acc = acc + gath[d]
        o_ref[...] = acc

    vm = pl.BlockSpec(memory_space=pltpu.VMEM)
    return _tc_call(
        body, name=name, in_specs=[vm], out_specs=vm, out_shape=jax.ShapeDtypeStruct((r, cols), F32),
        scratch_shapes=[pltpu.VMEM((N_DEV, r, cols), F32), pltpu.SemaphoreType.DMA((N_DEV - 1,)),
                        pltpu.SemaphoreType.DMA((N_DEV - 1,))],
    )(vec)


def _rope_tables(positions):
    half = QK_ROPE // 2
    inv_freq = 1.0 / (ROPE_THETA ** (jnp.arange(half, dtype=F32) / half))
    ang = positions.astype(F32)[:, None] * inv_freq
    zeros = jnp.zeros((positions.shape[0], LANES - QK_ROPE), F32)
    cos, sin = jnp.cos(ang), jnp.sin(ang)
    return jnp.concatenate([cos, cos, zeros], axis=1), jnp.concatenate([sin, sin, zeros], axis=1)


def _local_step(x, positions, tgt, wf, small, rs):
    cos, sin = _rope_tables(positions)
    w_in, w_out = wf["sc_w_in"], wf["sc_w_out"]
    w_ups, w_downs = (wf["ffn_w_up0"], wf["ffn_w_up1"]), (wf["ffn_w_down0"], wf["ffn_w_down1"])
    w_kv, w_ukv, w_dq, w_uq, w_o = wf["w_kv"], wf["w_ukv"], wf["w_dq"], wf["w_uq"], wf["w_o"]
    attn_norm, ffn_norm = small["attn_norm"], small["ffn_norm"]
    conv_b = small["ffn_conv_b"]

    def ffn_fwd(h, hf, l, then):
        up, a = _ffn_up_gate(hf, w_ups[l], small["ffn_conv_w"][l], conv_b[l:l + 1], f"ffn{l}_up_gate")
        return then(a, w_downs[l], h), (hf, up, a)

    def ffn_bwd(h, dh_out, dh_out_b, l, saved, gi, hooks):
        run = lambda stage: hooks.get(stage, lambda: None)()
        hf, up, a = saved
        d_down = _tn(f"ffn{l}_down_dw", a, dh_out_b, BF16)
        run("down_dw")
        dup, d_cw, d_cb = _gate_bwd(up, small["ffn_conv_w"][l], conv_b[l:l + 1], dh_out_b, w_downs[l],
                                    f"ffn{l}_gate_bwd")
        run("gate_bwd")
        d_up = _dw_ffn_up(f"ffn{l}_up_dw", hf, dup)
        rs.start(gi, {f"ffn_w_down{l}": d_down.reshape(N_CHIPS, F_FF // N_CHIPS, D), f"ffn_w_up{l}": d_up})
        dh, dh_b, d_norm = _dx_norm_bwd(f"ffn{l}_up_dx", dup, w_ups[l], h, ffn_norm[l:l + 1], dh_out)
        run("up_dx")
        return dh, dh_b, d_cw, d_cb, d_norm

    hn0 = _rms_fwd(x, attn_norm[0:1], "attn0_norm")
    z = _nn_parts("sc_in", hn0, w_in, 3, BF16)
    mix = _scmix_fwd(z, small["sc_conv_w"])
    h1, hf0 = _nn_add_norm("sc_out", mix, w_out, x, ffn_norm[0:1])
    h2, ffn0_saved = ffn_fwd(h1, hf0, 0, lambda a, w, h: _nn("ffn0_down", a, w, F32, add=h))

    hn1, hk, cq_pre, cq, q, kvpre, ckv, kr, knv = _attn_prep(
        h2, attn_norm[1:2], small["kv_in_norm"], w_dq, small["q_latent_norm"], w_uq, w_kv, small["kv_latent_norm"],
        w_ukv, cos, sin)
    o = _attn_fwd(q, knv, kr)
    h3, hf1 = _nn_add_norm("attn_out", o, w_o, h2, ffn_norm[1:2])
    (loss, dh4, dh4_b, d_final), ffn1_saved = ffn_fwd(
        h3, hf1, 1, lambda a, w, h: _nn_add_loss("ffn1_down_loss", a, w, h, small["final_norm"], tgt))

    rows = D // N_CHIPS
    dh3, dh3_b, d_cw1, d_cb1, d_fn1 = ffn_bwd(h3, dh4, dh4_b, 1, ffn1_saved, 0, {})

    do = _nt("attn_out_dx", dh3_b, w_o, BF16)
    d_wo = _tn("attn_out_dw", o, dh3_b, BF16)
    rs.pair_sums(0)
    dq, dknv, dkr = _attn_bwd(q, knv, kr, do, cos, sin)
    rs.chip_sums(0)
    dh2, dh2_b, d_wuq, d_wdq, d_wukv, d_wkv, d_an1, d_kvin, d_qln, d_kvln = _attn_prep_bwd(
        dq, dknv, dkr, dh3, h2, hn1, hk, cq_pre, cq, kvpre, ckv, attn_norm[1:2], small["kv_in_norm"], w_dq,
        small["q_latent_norm"], w_uq, w_kv, small["kv_latent_norm"], w_ukv, cos, sin)
    rs.finish(0)
    by_owner = lambda dw: dw.reshape(dw.shape[0], N_CHIPS, -1).transpose(1, 0, 2)
    rs.start(1, {
        "w_o": d_wo.reshape(N_CHIPS, rows, D), "w_uq": by_owner(d_wuq), "w_dq": d_wdq.reshape(N_CHIPS, rows, Q_LORA),
        "w_ukv": by_owner(d_wukv.reshape(2 * KV_LORA, -1)).reshape(N_CHIPS, 2 * KV_LORA, -1),
        "w_kv": d_wkv.reshape(N_CHIPS, rows, KVP),
    })

    dh1, dh1_b, d_cw0, d_cb0, d_fn0 = ffn_bwd(h1, dh2, dh2_b, 0, ffn0_saved, 2, {
        "down_dw": lambda: rs.pair_sums(1), "gate_bwd": lambda: rs.chip_sums(1), "up_dx": lambda: rs.finish(1)})
    rs.pair_sums(2)

    d_wout = _tn("sc_out_dw", mix, dh1_b, BF16)
    dmix = _nt("sc_out_dx", dh1_b, w_out, BF16)
    dz, d_scw = _scmix_bwd(z, small["sc_conv_w"], dmix)
    d_win = _dw_sc_in(hn0, dz)
    rs.start(3, {"sc_w_out": d_wout.reshape(N_CHIPS, rows, D), "sc_w_in": d_win})
    dx, _, d_an0 = _dx_norm_bwd("sc_in_dx", dz, w_in, x, attn_norm[0:1], dh1)

    small_g = {
        "attn_norm": jnp.concatenate([d_an0, d_an1]), "ffn_norm": jnp.concatenate([d_fn0, d_fn1]),
        "final_norm": d_final, "kv_in_norm": d_kvin, "kv_latent_norm": d_kvln, "q_latent_norm": d_qln,
        "ffn_conv_b": jnp.concatenate([d_cb0, d_cb1]), "sc_conv_w": d_scw, "ffn_conv_w": jnp.stack([d_cw0, d_cw1]),
    }
    return loss, dx, small_g


RS_GROUPS = (("ffn_w_down1", "ffn_w_up1"), ("w_o", "w_uq", "w_dq", "w_ukv", "w_kv"),
             ("ffn_w_down0", "ffn_w_up0"), ("sc_w_out", "sc_w_in"))


class _ReduceScatter:
    def __init__(self, ids, finish):
        self.ids, self.grads, self.step, self.mine, self.sib, self.finish = ids, {}, {}, {}, {}, finish

    def _cid(self, gi):
        return len(AG_GROUPS) + 3 * gi

    def start(self, gi, grads):
        self.grads.update(grads)
        own = [grads[n] for n in RS_GROUPS[gi]]
        self.step[gi] = (own, _pair_exchange(own, gi, self._cid(gi)))

    def pair_sums(self, gi):
        own, ra = self.step[gi]
        sums = _pair_sums(self.ids, own, ra, f"rs_pair_sums{gi}")
        self.step[gi] = (own, ra, _chip_exchange(sums, gi, self._cid(gi) + 1))

    def chip_sums(self, gi):
        own, ra, rb = self.step[gi]
        mine = _chip_sums(self.ids, own, ra, rb, f"rs_chip_sums{gi}")
        self.mine.update(zip(RS_GROUPS[gi], mine))
        self.sib.update(zip(RS_GROUPS[gi], _pair_swap(mine, gi, self._cid(gi) + 2)))

SMALL_REPL = ("attn_norm", "ffn_norm", "final_norm", "kv_in_norm", "kv_latent_norm", "q_latent_norm", "ffn_conv_b")
SMALL_SHARDED = ("sc_conv_w", "ffn_conv_w")
SMALL_ROWS = 256


def _pad_heads(w_uq):
    per_head = w_uq.reshape(Q_LORA, -1, QK_NOPE + QK_ROPE)
    return jnp.pad(per_head, ((0, 0), (0, 0), (0, HEAD_PAD - QK_NOPE - QK_ROPE))).reshape(Q_LORA, -1)


def _pack_kv(w_dkv, w_kr):
    return jnp.concatenate([w_dkv, w_kr, jnp.zeros((w_kr.shape[0], LANES - QK_ROPE), w_kr.dtype)], axis=1)


def kernel(x, positions, attn_norm, ffn_norm, final_norm, sc_w_in, sc_conv_w, sc_w_out, kv_in_norm, w_dkv, kv_latent_norm, w_kr, w_uk, w_uv, w_dq, q_latent_norm, w_uq, w_o, ffn_w_up, ffn_conv_w, ffn_conv_b, ffn_w_down, loss_target, m_attn_norm, m_ffn_norm, m_final_norm, m_sc_w_in, m_sc_conv_w, m_sc_w_out, m_kv_in_norm, m_w_dkv, m_kv_latent_norm, m_w_kr, m_w_uk, m_w_uv, m_w_dq, m_q_latent_norm, m_w_uq, m_w_o, m_ffn_w_up, m_ffn_conv_w, m_ffn_conv_b, m_ffn_w_down, v_attn_norm, v_ffn_norm, v_final_norm, v_sc_w_in, v_sc_conv_w, v_sc_w_out, v_kv_in_norm, v_w_dkv, v_kv_latent_norm, v_w_kr, v_w_uk, v_w_uv, v_w_dq, v_q_latent_norm, v_w_uq, v_w_o, v_ffn_w_up, v_ffn_conv_w, v_ffn_conv_b, v_ffn_w_down):
    names = ("attn_norm", "ffn_norm", "final_norm", "sc_w_in", "sc_conv_w", "sc_w_out", "kv_in_norm", "w_dkv",
             "kv_latent_norm", "w_kr", "w_uk", "w_uv", "w_dq", "q_latent_norm", "w_uq", "w_o", "ffn_w_up",
             "ffn_conv_w", "ffn_conv_b", "ffn_w_down")
    w = dict(zip(names, (attn_norm, ffn_norm, final_norm, sc_w_in, sc_conv_w, sc_w_out, kv_in_norm, w_dkv,
                         kv_latent_norm, w_kr, w_uk, w_uv, w_dq, q_latent_norm, w_uq, w_o, ffn_w_up,
                         ffn_conv_w, ffn_conv_b, ffn_w_down)))
    m = dict(zip(names, (m_attn_norm, m_ffn_norm, m_final_norm, m_sc_w_in, m_sc_conv_w, m_sc_w_out, m_kv_in_norm,
                         m_w_dkv, m_kv_latent_norm, m_w_kr, m_w_uk, m_w_uv, m_w_dq, m_q_latent_norm, m_w_uq, m_w_o,
                         m_ffn_w_up, m_ffn_conv_w, m_ffn_conv_b, m_ffn_w_down)))
    v = dict(zip(names, (v_attn_norm, v_ffn_norm, v_final_norm, v_sc_w_in, v_sc_conv_w, v_sc_w_out, v_kv_in_norm,
                         v_w_dkv, v_kv_latent_norm, v_w_kr, v_w_uk, v_w_uv, v_w_dq, v_q_latent_norm, v_w_uq, v_w_o,
                         v_ffn_w_up, v_ffn_conv_w, v_ffn_conv_b, v_ffn_w_down)))

    _ORDER[0] = None
    ix, iy, ic = lax.axis_index("x"), lax.axis_index("y"), lax.axis_index("c")
    chip = 2 * ix + iy
    ids = jnp.stack([ic, chip]).astype(jnp.int32)

    def shards_of(t):
        return {
            "sc_w_in": t["sc_w_in"][0], "sc_w_out": t["sc_w_out"][0], "ffn_w_up": t["ffn_w_up"],
            "ffn_w_down": t["ffn_w_down"], "w_kv": _pack_kv(t["w_dkv"], t["w_kr"]),
            "w_ukv": jnp.stack([t["w_uk"], t["w_uv"]]), "w_dq": t["w_dq"][0], "w_uq": _pad_heads(t["w_uq"][0]),
            "w_o": t["w_o"][0],
        }

    ws, ms, vs = shards_of(w), shards_of(m), shards_of(v)

    def ag_shard(name):
        if name == "sc_conv_w":
            return sc_conv_w[0]
        if name == "ffn_conv_w":
            return ffn_conv_w.reshape(6, -1)
        if name[:-1] in ("ffn_w_up", "ffn_w_down"):
            return ws[name[:-1]][int(name[-1])].astype(BF16)
        return ws[name].astype(BF16)

    wf = {}
    for gi, wms in enumerate(AG_GROUPS):
        fulls = _all_gather_group(gi, [ag_shard(wm.name) for wm in wms])
        wf.update({wm.name: f for wm, f in zip(wms, fulls)})
    small = {
        "attn_norm": attn_norm, "ffn_norm": ffn_norm, "final_norm": final_norm[None], "kv_in_norm": kv_in_norm[None],
        "kv_latent_norm": kv_latent_norm[None], "q_latent_norm": q_latent_norm, "ffn_conv_b": ffn_conv_b,
        "sc_conv_w": wf["sc_conv_w"].transpose(1, 0, 2).reshape(3, D),
        "ffn_conv_w": wf["ffn_conv_w"].reshape(N_CHIPS, 2, 3, -1).transpose(1, 2, 0, 3).reshape(2, 3, F_FF),
    }

    res = {}

    merged = lambda a: a.reshape(2 * KV_LORA, -1)

    def adamw_group(gi):
        items = []
        for key in RS_GROUPS[gi]:
            n, layer = (key[:-1], int(key[-1])) if key[:-1] in ("ffn_w_up", "ffn_w_down") else (key, None)
            w_, m_, v_ = (merged(t[n]) for t in (ws, ms, vs)) if n == "w_ukv" else (ws[n], ms[n], vs[n])
            items.append(dict(name=n, w=w_, m=m_, v=v_, g_mine=rs.mine[key], g_sib=rs.sib[key], layer=layer,
                              prev=res.get(n)))
        for it, out in zip(items, _adamw_shards(ids, items, f"adamw_group{gi}")):
            res[it["name"]] = out

    rs = _ReduceScatter(ids, adamw_group)
    loss, dx, small_g = _local_step(x[0], positions[0], loss_target[0], wf, small, rs)

    rs.chip_sums(2)
    rs.pair_sums(3)

    s_order = SMALL_REPL + SMALL_SHARDED
    flat = jnp.concatenate([small_g[n].reshape(-1) for n in s_order] + [loss.reshape(-1)])
    flat = jnp.pad(flat, (0, SMALL_ROWS * LANES - flat.shape[0])).reshape(SMALL_ROWS, LANES)
    red = _all_reduce_small(flat, "ar_small").reshape(-1)
    sg, off = {}, 0
    for n in s_order:
        sz = small_g[n].size
        sg[n] = red[off:off + sz].reshape(small_g[n].shape)
        off += sz
    loss_out = red[off]
    grads = {n: sg[n].reshape(w[n].shape) for n in SMALL_REPL}
    grads["sc_conv_w"] = lax.dynamic_slice_in_dim(sg["sc_conv_w"], chip * (D // N_CHIPS), D // N_CHIPS, axis=1)[None]
    grads["ffn_conv_w"] = lax.dynamic_slice_in_dim(sg["ffn_conv_w"], chip * (F_FF // N_CHIPS), F_FF // N_CHIPS, axis=2)

    small_names = SMALL_REPL + SMALL_SHARDED

    def pack_small(tree):
        return jnp.concatenate([tree[n].reshape(-1) for n in small_names]).reshape(-1, LANES)

    small_res = _adamw_small(pack_small(w), pack_small(grads), pack_small(m), pack_small(v))
    rs.finish(2)
    rs.chip_sums(3)
    rs.finish(3)
    outs = [grads, {}, {}, {}]
    for k, dst in enumerate(outs):
        for n in ("sc_w_in", "sc_w_out", "w_dq", "w_o"):
            dst[n] = res[n][k][None]
        unpadded = res["w_uq"][k].reshape(Q_LORA, -1, HEAD_PAD)[:, :, :QK_NOPE + QK_ROPE]
        dst["w_uq"] = unpadded.reshape(w_uq.shape)
        dst["ffn_w_up"], dst["ffn_w_down"] = res["ffn_w_up"][k], res["ffn_w_down"][k]
        dst["w_dkv"], dst["w_kr"] = res["w_kv"][k][:, :KV_LORA], res["w_kv"][k][:, KV_LORA:KV_LORA + QK_ROPE]
        dst["w_uk"], dst["w_uv"] = res["w_ukv"][k][:KV_LORA], res["w_ukv"][k][KV_LORA:]
    grads, delta, new_m, new_v = outs
    for slab, dst in zip(small_res, (delta, new_m, new_v)):
        f, off = slab.reshape(-1), 0
        for n in small_names:
            dst[n] = f[off:off + w[n].size].reshape(w[n].shape)
            off += w[n].size

    _ORDER[0] = None
    return (loss_out, dx[None], *[grads[n] for n in names], *[delta[n] for n in names],
            *[new_m[n] for n in names], *[new_v[n] for n in names])
```

```python
from typing import NamedTuple

import jax
import jax.numpy as jnp
from jax import lax
from jax.experimental import pallas as pl
from jax.experimental.pallas import tpu as pltpu
from jax.experimental.pallas import tpu_sc as plsc

F32 = jnp.float32
BF16 = jnp.bfloat16

T = 2048
D = 1024
F_FF = 2816
N_HEADS = 8
QK_NOPE = 128
QK_ROPE = 64
V_HEAD = 128
Q_LORA = 384
KV_LORA = 256
CHUNK_SHIFT = 6
ROPE_THETA = 10000.0
EPS = 1e-6
NEG_INF = -1e30
HEAD_PAD = 256
KVP = KV_LORA + 128

ADAM_LR = 0.001
ADAM_B1 = 0.9
ADAM_B2 = 0.999
ADAM_EPS = 1e-08
ADAM_WD = 0.01
ADAM_STEP = 10

N_CHIPS = 4
N_DEV = 8
LANES = 128
TC = 256
V7X_VMEM_LIMIT = 56 * 1024 * 1024

MESH = pl.DeviceIdType.MESH
ANY = pl.BlockSpec(memory_space=pl.ANY)


class _W(NamedTuple):
    name: str
    kind: str
    nl: int
    k: int
    n: int


AG_GROUPS = (
    (_W("sc_w_in", "col", 1, D, 3 * D // N_CHIPS), _W("sc_conv_w", "tiny", 1, 3, D // N_CHIPS),
     _W("ffn_conv_w", "tiny", 1, 6, F_FF // N_CHIPS), _W("sc_w_out", "row", 1, D // N_CHIPS, D)),
    (_W("ffn_w_up0", "col", 1, D, 2 * F_FF // N_CHIPS),),
    (_W("ffn_w_down0", "row", 1, F_FF // N_CHIPS, D),),
    (_W("w_kv", "row", 1, D // N_CHIPS, KVP), _W("w_ukv", "col", 2, KV_LORA, N_HEADS * QK_NOPE // N_CHIPS),
     _W("w_dq", "row", 1, D // N_CHIPS, Q_LORA),
     _W("w_uq", "col", 1, Q_LORA, N_HEADS * HEAD_PAD // N_CHIPS),
     _W("w_o", "row", 1, N_HEADS * V_HEAD // N_CHIPS, D)),
    (_W("ffn_w_up1", "col", 1, D, 2 * F_FF // N_CHIPS), _W("ffn_w_down1", "row", 1, F_FF // N_CHIPS, D)),
)


def _cp(*sem):
    return pltpu.CompilerParams(dimension_semantics=sem, vmem_limit_bytes=V7X_VMEM_LIMIT)


_ORDER = [None]


def _tc_call(body, *, name, out_shape, in_specs=None, out_specs=None, grid=(), scratch_shapes=(), prefetch=0,
             input_output_aliases=None, compiler_params=None):
    def run(*args):
        specs = [pl.BlockSpec(memory_space=pltpu.VMEM)] * (len(args) - prefetch) if in_specs is None else list(in_specs)
        inner, dep = body, _ORDER[0]
        if dep is not None:
            unread = prefetch + len(specs)
            specs, args = specs + [ANY], (*args, dep)

            def inner(*refs):
                return body(*refs[:unread], *refs[unread + 1:])

        kwargs = dict(name=name, out_shape=out_shape, input_output_aliases=input_output_aliases or {},
                      compiler_params=compiler_params)
        if prefetch:
            kwargs["grid_spec"] = pltpu.PrefetchScalarGridSpec(
                num_scalar_prefetch=prefetch, grid=grid, in_specs=specs, out_specs=out_specs,
                scratch_shapes=scratch_shapes)
        else:
            kwargs.update(grid=grid, in_specs=specs, scratch_shapes=scratch_shapes)
            if out_specs is not None:
                kwargs["out_specs"] = out_specs
        out = pl.pallas_call(inner, **kwargs)(*args)
        _ORDER[0] = out[0] if isinstance(out, (list, tuple)) else out
        return out

    return run


def _tile(n, cands):
    for c in cands:
        if n % c == 0:
            return c
    raise ValueError(f"no tile for {n}")


NN_DIMS = (((1,), (0,)), ((), ()))
NT_DIMS = (((1,), (1,)), ((), ()))
TN_DIMS = (((0,), (0,)), ((), ()))
M_TILES = (1024, 512, 384, 256, 128)
N_TILES = (1408, 1024, 768, 512, 384, 256, 128)
MM_BLOCK_BYTES = 36 * 1024 * 1024


def _fit(m, n, block_bytes, m_tiles=M_TILES, n_tiles=N_TILES):
    for tm in [c for c in m_tiles if m % c == 0]:
        for tn in [c for c in n_tiles if n % c == 0]:
            if 2 * block_bytes(tm, tn) + 4 * tm * tn <= MM_BLOCK_BYTES:
                return tm, tn
    raise ValueError(f"no tiles for {m} x {n}")


def _size(x):
    return x.dtype.itemsize


def _mm(name, a, b, dims, grid, a_spec, b_spec, o_spec, o_sds, add=None, red=None, acc_shape=None):
    n_red = None if red is None else grid[red]

    def body(*refs):
        a_ref, b_ref = refs[0], refs[1]
        add_ref = refs[2] if add is not None else None
        o_ref = refs[3] if add is not None else refs[2]
        part = lax.dot_general(a_ref[...].astype(BF16), b_ref[...].astype(BF16), dims, preferred_element_type=F32)
        if red is None:
            if add is not None:
                part = part + add_ref[...]
            o_ref[...] = part.astype(o_ref.dtype)
            return
        acc_ref = refs[-1]
        r = pl.program_id(red)

        @pl.when(r == 0)
        def _():
            acc_ref[...] = part

        @pl.when(r > 0)
        def _():
            acc_ref[...] += part

        @pl.when(r == n_red - 1)
        def _():
            o_ref[...] = acc_ref[...].astype(o_ref.dtype)

    sem = tuple("arbitrary" if ax == red else "parallel" for ax in range(len(grid)))
    in_specs = [a_spec, b_spec] + ([o_spec] if add is not None else [])
    args = (a, b) + ((add,) if add is not None else ())
    return _tc_call(
        body, name=name, grid=grid, in_specs=in_specs, out_specs=o_spec, out_shape=o_sds,
        scratch_shapes=[] if red is None else [pltpu.VMEM(acc_shape, F32)], compiler_params=_cp(*sem),
    )(*args)


def _nn(name, a, b, out_dtype, add=None, lead=None):
    (m, k), n = a.shape, b.shape[-1]
    osz = jnp.dtype(out_dtype).itemsize + (4 if add is not None else 0)
    tm, tn = _fit(m, n, lambda tm, tn: tm * k * _size(a) + k * tn * _size(b) + tm * tn * osz)
    if lead is None:
        b_spec = pl.BlockSpec((k, tn), lambda i, j: (0, j))
    else:
        b_spec = pl.BlockSpec((None, k, tn), lambda i, j: (lead, 0, j))
    return _mm(name, a, b, NN_DIMS, (m // tm, n // tn), pl.BlockSpec((tm, k), lambda i, j: (i, 0)), b_spec,
               pl.BlockSpec((tm, tn), lambda i, j: (i, j)), jax.ShapeDtypeStruct((m, n), out_dtype), add=add)


def _nn_parts(name, a, b, parts, out_dtype, lead=None, stacked=False):
    m, k = a.shape
    c = b.shape[-1] if stacked else b.shape[-1] // parts
    osz = jnp.dtype(out_dtype).itemsize
    tm, tn = _fit(m, c, lambda tm, tn: tm * k * _size(a) + k * tn * _size(b) + tm * tn * osz)
    nb = c // tn
    if stacked:
        b_spec = pl.BlockSpec((None, k, tn), lambda i, p, j: (p, 0, j))
    elif lead is None:
        b_spec = pl.BlockSpec((k, tn), lambda i, p, j: (0, p * nb + j))
    else:
        b_spec = pl.BlockSpec((None, k, tn), lambda i, p, j: (lead, 0, p * nb + j))
    return _mm(name, a, b, NN_DIMS, (m // tm, parts, nb), pl.BlockSpec((tm, k), lambda i, p, j: (i, 0)), b_spec,
               pl.BlockSpec((None, tm, tn), lambda i, p, j: (p, i, j)), jax.ShapeDtypeStruct((parts, m, c), out_dtype))


def _nt(name, a, b, out_dtype, lead=None):
    (m, k), n = a.shape, b.shape[-2]
    osz = jnp.dtype(out_dtype).itemsize
    tm, tn = _fit(m, n, lambda tm, tn: tm * k * _size(a) + tn * k * _size(b) + tm * tn * osz)
    if lead is None:
        b_spec = pl.BlockSpec((tn, k), lambda i, j: (j, 0))
    else:
        b_spec = pl.BlockSpec((None, tn, k), lambda i, j: (lead, j, 0))
    return _mm(name, a, b, NT_DIMS, (m // tm, n // tn), pl.BlockSpec((tm, k), lambda i, j: (i, 0)), b_spec,
               pl.BlockSpec((tm, tn), lambda i, j: (i, j)), jax.ShapeDtypeStruct((m, n), out_dtype))


def _tn(name, a, b, out_dtype):
    (k, m), n = a.shape, b.shape[1]
    osz = jnp.dtype(out_dtype).itemsize
    tm, tn = _fit(m, n, lambda tm, tn: k * tm * _size(a) + k * tn * _size(b) + tm * tn * osz,
                  m_tiles=(512, 384, 256, 128), n_tiles=(n,) + N_TILES)
    return _mm(name, a, b, TN_DIMS, (m // tm, n // tn), pl.BlockSpec((k, tm), lambda i, j: (0, i)),
               pl.BlockSpec((k, tn), lambda i, j: (0, j)), pl.BlockSpec((tm, tn), lambda i, j: (i, j)),
               jax.ShapeDtypeStruct((m, n), out_dtype))


def _nn_add_norm(name, a, b, add, g):
    (m, k), n = a.shape, b.shape[1]
    tm = 512

    def body(a_ref, b_ref, add_ref, g_ref, h_ref, hn_ref):
        h = jnp.dot(a_ref[...], b_ref[...], preferred_element_type=F32) + add_ref[...]
        h_ref[...] = h
        hn_ref[...] = _rms_rows(h, g_ref[...]).astype(BF16)

    rows = lambda w: pl.BlockSpec((tm, w), lambda i: (i, 0))
    return _tc_call(
        body, name=name, grid=(m // tm,),
        in_specs=[rows(k), pl.BlockSpec((k, n), lambda i: (0, 0)), rows(n), pl.BlockSpec((1, n), lambda i: (0, 0))],
        out_specs=[rows(n), rows(n)],
        out_shape=[jax.ShapeDtypeStruct((m, n), F32), jax.ShapeDtypeStruct((m, n), BF16)], compiler_params=_cp("parallel"),
    )(a, b, add, g)


def _nn_add_loss(name, a, b, add, g, tgt):
    (m, k), n = a.shape, b.shape[1]
    tm = 512

    def body(a_ref, b_ref, add_ref, g_ref, t_ref, loss_ref, dh_ref, dhb_ref, dg_ref):
        xv = jnp.dot(a_ref[...], b_ref[...], preferred_element_type=F32) + add_ref[...]
        gv = g_ref[...]
        r = lax.rsqrt(jnp.mean(xv * xv, axis=1, keepdims=True) + EPS)
        err = xv * r * gv - t_ref[...]
        part = 0.5 * jnp.sum(jnp.mean(err * err, axis=1, keepdims=True), axis=0, keepdims=True)
        dx, dg = _rms_bwd_math(xv, gv, err * (1.0 / n))
        dh_ref[...] = dx
        dhb_ref[...] = dx.astype(BF16)

        @pl.when(pl.program_id(0) == 0)
        def _():
            dg_ref[...] = jnp.zeros_like(dg_ref)
            loss_ref[...] = jnp.zeros_like(loss_ref)

        dg_ref[...] += dg
        loss_ref[...] += jnp.broadcast_to(part, loss_ref.shape)

    rows = lambda w: pl.BlockSpec((tm, w), lambda i: (i, 0))
    vec = pl.BlockSpec((1, n), lambda i: (0, 0))
    return _tc_call(
        body, name=name, grid=(m // tm,),
        in_specs=[rows(k), pl.BlockSpec((k, n), lambda i: (0, 0)), rows(n), vec, rows(n)],
        out_specs=[pl.BlockSpec((1, LANES), lambda i: (0, 0)), rows(n), rows(n), vec],
        out_shape=[jax.ShapeDtypeStruct((1, LANES), F32), jax.ShapeDtypeStruct((m, n), F32),
                   jax.ShapeDtypeStruct((m, n), BF16), jax.ShapeDtypeStruct((1, n), F32)],
        compiler_params=_cp("arbitrary"),
    )(a, b, add, g, tgt)


def _dx_norm_bwd(name, a, b, x, g, add):
    parts, t, c = a.shape
    d = b.shape[0]
    tm = 256

    def body(a_ref, b_ref, x_ref, g_ref, add_ref, dx_ref, dxb_ref, dg_ref):
        dy = None
        for p in range(parts):
            part = lax.dot_general(a_ref[p], b_ref[:, p * c:(p + 1) * c], NT_DIMS, preferred_element_type=F32)
            dy = part if dy is None else dy + part
        dx, dg = _rms_bwd_math(x_ref[...], g_ref[...], dy)
        dx = dx + add_ref[...]
        dx_ref[...] = dx
        dxb_ref[...] = dx.astype(BF16)

        @pl.when(pl.program_id(0) == 0)
        def _():
            dg_ref[...] = jnp.zeros_like(dg_ref)

        dg_ref[...] += dg

    rows = pl.BlockSpec((tm, d), lambda i: (i, 0))
    vec = pl.BlockSpec((1, d), lambda i: (0, 0))
    return _tc_call(
        body, name=name, grid=(t // tm,),
        in_specs=[pl.BlockSpec((parts, tm, c), lambda i: (0, i, 0)), pl.BlockSpec(b.shape, lambda i: (0, 0)), rows, vec,
                  rows],
        out_specs=[rows, rows, vec],
        out_shape=[jax.ShapeDtypeStruct((t, d), F32), jax.ShapeDtypeStruct((t, d), BF16),
                   jax.ShapeDtypeStruct((1, d), F32)],
        compiler_params=_cp("arbitrary"),
    )(a, b, x, g, add)


def _dw_sc_in(hn, dz):
    t, tn, tm = hn.shape[0], TC, D
    per_part, per_chip = D // tn, 3 * D // N_CHIPS // tn
    return _mm("sc_in_dw", hn, dz, TN_DIMS, (D // tm, 3 * D // tn), pl.BlockSpec((t, tm), lambda i, j: (0, i)),
               pl.BlockSpec((None, t, tn), lambda i, j: (j // per_part, 0, j % per_part)),
               pl.BlockSpec((None, tm, tn), lambda i, j: (j // per_chip, i, j % per_chip)),
               jax.ShapeDtypeStruct((N_CHIPS, D, 3 * D // N_CHIPS), BF16))


def _dw_ffn_up(name, hf, dup):
    t, tm, ns = hf.shape[0], D, 2 * F_FF // N_CHIPS
    return _mm(name, hf, dup, TN_DIMS, (N_CHIPS, D // tm), pl.BlockSpec((t, tm), lambda s, i: (0, i)),
               pl.BlockSpec((None, t, ns), lambda s, i: (s // 2, 0, s % 2)),
               pl.BlockSpec((None, tm, ns), lambda s, i: (s, i, 0)), jax.ShapeDtypeStruct((N_CHIPS, D, ns), BF16))


def _rms_fwd(x, g, name):
    t, d = x.shape
    tr = 512

    def body(x_ref, g_ref, o_ref):
        xv = x_ref[...]
        r = lax.rsqrt(jnp.mean(xv * xv, axis=1, keepdims=True) + EPS)
        o_ref[...] = (xv * r * g_ref[...]).astype(o_ref.dtype)

    row = pl.BlockSpec((tr, d), lambda i: (i, 0))
    return _tc_call(
        body, name=name, grid=(t // tr,), in_specs=[row, pl.BlockSpec((1, d), lambda i: (0, 0))],
        out_specs=row, out_shape=jax.ShapeDtypeStruct((t, d), BF16), compiler_params=_cp("parallel"),
    )(x, g)


def _rms_bwd_math(xv, g, dy):
    r = lax.rsqrt(jnp.mean(xv * xv, axis=1, keepdims=True) + EPS)
    xh = xv * r
    gy = dy * g
    dx = r * (gy - xh * jnp.mean(gy * xh, axis=1, keepdims=True))
    dg = jnp.sum(dy * xh, axis=0, keepdims=True)
    return dx, dg


def _rot_half(x):
    lane = lax.broadcasted_iota(jnp.int32, x.shape, 1)
    return jnp.where((lane % QK_ROPE) < QK_ROPE // 2, -pltpu.roll(x, LANES - 32, axis=1),
                     pltpu.roll(x, 32, axis=1))


def _rope_fwd_math(x, cos, sin):
    return x * cos + _rot_half(x) * sin


def _rope_bwd_math(dy, cos, sin):
    return dy * cos - _rot_half(dy * sin)


def _rms_rows(x, g):
    return x * lax.rsqrt(jnp.mean(x * x, axis=1, keepdims=True) + EPS) * g


def _attn_prep(h, g_attn, g_kvin, w_dq, g_ql, w_uq, w_kv, g_kvl, w_ukv, cos, sin):
    t, d = h.shape
    tr = 256
    wq = N_HEADS * HEAD_PAD

    def body(h_ref, ga_ref, gk_ref, wdq_ref, gq_ref, wuq_ref, wkv_ref, gl_ref, wukv_ref, c_ref, s_ref,
             hn_ref, hk_ref, cqp_ref, cq_ref, q_ref, kvp_ref, ckv_ref, kr_ref, knv_ref):
        xv, cv, sv = h_ref[...], c_ref[...], s_ref[...]
        xh = xv * lax.rsqrt(jnp.mean(xv * xv, axis=1, keepdims=True) + EPS)
        hn = (xh * ga_ref[...]).astype(BF16)
        hk = (xh * gk_ref[...]).astype(BF16)
        hn_ref[...], hk_ref[...] = hn, hk
        cq_pre = jnp.dot(hn, wdq_ref[...], preferred_element_type=F32)
        cqp_ref[...] = cq_pre
        cq = _rms_rows(cq_pre, gq_ref[...]).astype(BF16)
        cq_ref[...] = cq
        for hd in range(N_HEADS):
            lo = hd * HEAD_PAD
            qh = jnp.dot(cq, wuq_ref[:, lo:lo + HEAD_PAD], preferred_element_type=F32)
            q_ref[:, lo:lo + QK_NOPE] = qh[:, :QK_NOPE].astype(BF16)
            q_ref[:, lo + QK_NOPE:lo + HEAD_PAD] = _rope_fwd_math(qh[:, QK_NOPE:], cv, sv).astype(BF16)
        kvpre = jnp.dot(hk, wkv_ref[...], preferred_element_type=F32)
        kvp_ref[...] = kvpre
        ckv = _rms_rows(kvpre[:, :KV_LORA], gl_ref[...]).astype(BF16)
        ckv_ref[...] = ckv
        kr_ref[...] = _rope_fwd_math(kvpre[:, KV_LORA:], cv, sv).astype(BF16)
        for p in range(2):
            knv_ref[p] = jnp.dot(ckv, wukv_ref[p], preferred_element_type=F32).astype(BF16)

    rows = lambda w: pl.BlockSpec((tr, w), lambda i: (i, 0))
    whole = lambda a: pl.BlockSpec(a.shape, lambda i: (0,) * a.ndim)
    sds = lambda w, dt: jax.ShapeDtypeStruct((t, w), dt)
    args = (h, g_attn, g_kvin, w_dq, g_ql, w_uq, w_kv, g_kvl, w_ukv, cos, sin)
    return _tc_call(
        body, name="attn_prep", grid=(t // tr,),
        in_specs=[rows(d)] + [whole(a) for a in args[1:9]] + [rows(LANES), rows(LANES)],
        out_specs=[rows(d), rows(d), rows(Q_LORA), rows(Q_LORA), rows(wq), rows(KVP), rows(KV_LORA), rows(LANES),
                   pl.BlockSpec((2, tr, N_HEADS * QK_NOPE), lambda i: (0, i, 0))],
        out_shape=[sds(d, BF16), sds(d, BF16), sds(Q_LORA, F32), sds(Q_LORA, BF16), sds(wq, BF16), sds(KVP, F32),
                   sds(KV_LORA, BF16), sds(LANES, BF16), jax.ShapeDtypeStruct((2, t, N_HEADS * QK_NOPE), BF16)],
        compiler_params=_cp("parallel"),
    )(*args)


def _attn_prep_bwd(dq, dknv, dkr, dh, h, hn, hk, cq_pre, cq, kvpre, ckv, g_attn, g_kvin, w_dq, g_ql, w_uq, w_kv, g_kvl,
                   w_ukv, cos, sin):
    t, d = h.shape
    tr = 256
    n_steps = t // tr
    wq = N_HEADS * HEAD_PAD
    wk = N_HEADS * QK_NOPE

    def body(dq_ref, dknv_ref, dkr_ref, dh_ref, h_ref, hn_ref, hk_ref, cqp_ref, cq_ref, kvp_ref, ckv_ref,
             ga_ref, gk_ref, wdq_ref, gq_ref, wuq_ref, wkv_ref, gl_ref, wukv_ref, c_ref, s_ref,
             dho_ref, dhb_ref, dwuq_ref, dwdq_ref, dwukv_ref, dwkv_ref, dga_ref, dgk_ref, dgq_ref, dgl_ref,
             a_uq, a_dq, a_ukv, a_kv):
        i = pl.program_id(0)

        @pl.when(i == 0)
        def _():
            for ref in (a_uq, a_dq, a_ukv, a_kv, dga_ref, dgk_ref, dgq_ref, dgl_ref):
                ref[...] = jnp.zeros_like(ref)

        dqv = dq_ref[...]
        dcq = lax.dot_general(dqv, wuq_ref[...], NT_DIMS, preferred_element_type=F32)
        a_uq[...] += lax.dot_general(cq_ref[...], dqv, TN_DIMS, preferred_element_type=F32)
        dcq_pre, dg = _rms_bwd_math(cqp_ref[...], gq_ref[...], dcq)
        dgq_ref[...] += dg
        dcq_pre = dcq_pre.astype(BF16)
        dhn = lax.dot_general(dcq_pre, wdq_ref[...], NT_DIMS, preferred_element_type=F32)
        a_dq[...] += lax.dot_general(hn_ref[...], dcq_pre, TN_DIMS, preferred_element_type=F32)
        dckv = None
        for p in range(2):
            dk = dknv_ref[p].astype(BF16)
            part = lax.dot_general(dk, wukv_ref[p], NT_DIMS, preferred_element_type=F32)
            dckv = part if dckv is None else dckv + part
            a_ukv[p] += lax.dot_general(ckv_ref[...], dk, TN_DIMS, preferred_element_type=F32)
        dlat, dg = _rms_bwd_math(kvp_ref[:, :KV_LORA], gl_ref[...], dckv)
        dgl_ref[...] += dg
        dkr_pre = _rope_bwd_math(dkr_ref[...], c_ref[...], s_ref[...])
        dkvpre = jnp.concatenate([dlat, dkr_pre], axis=1).astype(BF16)
        dhk = lax.dot_general(dkvpre, wkv_ref[...], NT_DIMS, preferred_element_type=F32)
        a_kv[...] += lax.dot_general(hk_ref[...], dkvpre, TN_DIMS, preferred_element_type=F32)
        xv = h_ref[...]
        dx1, dg = _rms_bwd_math(xv, ga_ref[...], dhn)
        dga_ref[...] += dg
        dx2, dg = _rms_bwd_math(xv, gk_ref[...], dhk)
        dgk_ref[...] += dg
        dh_new = dh_ref[...] + dx1 + dx2
        dho_ref[...] = dh_new
        dhb_ref[...] = dh_new.astype(BF16)

        @pl.when(i == n_steps - 1)
        def _():
            dwuq_ref[...] = a_uq[...].astype(BF16)
            dwdq_ref[...] = a_dq[...].astype(BF16)
            dwukv_ref[...] = a_ukv[...].astype(BF16)
            dwkv_ref[...] = a_kv[...].astype(BF16)

    rows = lambda w: pl.BlockSpec((tr, w), lambda i: (i, 0))
    whole = lambda shape: pl.BlockSpec(shape, lambda i: (0,) * len(shape))
    weights = (g_attn, g_kvin, w_dq, g_ql, w_uq, w_kv, g_kvl, w_ukv)
    dw_shapes = [(Q_LORA, wq), (d, Q_LORA), (2, KV_LORA, wk), (d, KVP)]
    dg_shapes = [(1, d), (1, d), (1, Q_LORA), (1, KV_LORA)]
    return _tc_call(
        body, name="attn_prep_bwd", grid=(n_steps,),
        in_specs=[rows(wq), pl.BlockSpec((2, tr, wk), lambda i: (0, i, 0)), rows(LANES), rows(d), rows(d), rows(d),
                  rows(d), rows(Q_LORA), rows(Q_LORA), rows(KVP), rows(KV_LORA)]
        + [whole(a.shape) for a in weights] + [rows(LANES), rows(LANES)],
        out_specs=[rows(d), rows(d)] + [whole(s) for s in dw_shapes + dg_shapes],
        out_shape=[jax.ShapeDtypeStruct((t, d), F32), jax.ShapeDtypeStruct((t, d), BF16)]
        + [jax.ShapeDtypeStruct(s, BF16) for s in dw_shapes] + [jax.ShapeDtypeStruct(s, F32) for s in dg_shapes],
        scratch_shapes=[pltpu.VMEM(s, F32) for s in dw_shapes], compiler_params=_cp("arbitrary"),
    )(dq, dknv, dkr, dh, h, hn, hk, cq_pre, cq, kvpre, ckv, *weights, cos, sin)


ROW_CHUNK = 64
HALO = 16
WIN = ROW_CHUNK + 16
LANE_HALVES = (slice(0, LANES), slice(LANES, TC))


def _stage(s_ref, p, src):
    t = src.shape[0]
    s_ref[p, :HALO] = jnp.zeros((HALO, TC), BF16)
    s_ref[p, HALO:HALO + t] = src
    s_ref[p, HALO + t:] = jnp.zeros((HALO, TC), BF16)


def _window(s_ref, p, i, lanes):
    base = pl.multiple_of(i * ROW_CHUNK, ROW_CHUNK)
    return s_ref[p, pl.ds(base, ROW_CHUNK + 2 * HALO), lanes].astype(F32)[8:8 + WIN]


def _valid(x):
    return x[8:8 + ROW_CHUNK]


def _prev(x, k):
    return pltpu.roll(x, k, axis=0)


def _next(x, k):
    return pltpu.roll(x, WIN - k, axis=0)


def _taps(w_ref, lanes):
    return w_ref[0:1, lanes], w_ref[1:2, lanes], w_ref[2:3, lanes]


def _fold8(x):
    return jnp.sum(x.reshape(ROW_CHUNK // 8, 8, x.shape[-1]), axis=0)


def _store_rows(ref, idx, i, lanes, x):
    rows = pl.ds(pl.multiple_of(i * ROW_CHUNK, ROW_CHUNK), ROW_CHUNK)
    ref[(*idx, rows, lanes)] = x.astype(ref.dtype)


def _for_chunks(t, chunk):
    def step(i, carry):
        for lanes in LANE_HALVES:
            chunk(i, lanes)
        return carry

    lax.fori_loop(0, t // ROW_CHUNK, step, 0)


def _write_col_sums(acc_ref, outs):
    for k, (ref, row) in enumerate(outs):
        ref[row:row + 1, :] = jnp.sum(acc_ref[k], axis=0, keepdims=True)


def _shift_down(x, k):
    row = lax.broadcasted_iota(jnp.int32, x.shape, 0)
    return jnp.where(row >= k, pltpu.roll(x, k, axis=0), 0.0)


def _shift_up(x, k):
    n = x.shape[0]
    row = lax.broadcasted_iota(jnp.int32, x.shape, 0)
    return jnp.where(row < n - k, pltpu.roll(x, n - k, axis=0), 0.0)


def _conv3(x, w_ref):
    return _shift_down(x, 2) * w_ref[0:1, :] + _shift_down(x, 1) * w_ref[1:2, :] + x * w_ref[2:3, :]


def _col(parts, t):
    if parts is None:
        return pl.BlockSpec((t, TC), lambda j: (0, j))
    return pl.BlockSpec((parts, t, TC), lambda j: (0, 0, j))


def _staging(parts, t):
    return pltpu.VMEM((parts, t + 2 * HALO, TC), BF16)


def _scmix_fwd(z, w):
    t = z.shape[1]

    def body(z_ref, w_ref, m_ref):
        b, c, u = (z_ref[p].astype(F32) for p in range(3))
        m_ref[...] = (b * _conv3(c * u, w_ref)).astype(BF16)

    return _tc_call(
        body, name="scmix_fwd", grid=(D // TC,), in_specs=[_col(3, t), pl.BlockSpec((3, TC), lambda j: (0, j))],
        out_specs=_col(None, t), out_shape=jax.ShapeDtypeStruct((t, D), BF16), compiler_params=_cp("parallel"),
    )(z, w)


def _scmix_bwd(z, w, dm):
    t = z.shape[1]

    def body(z_ref, w_ref, dm_ref, dz_ref, dw_ref, s_ref, acc_ref):
        for p in range(3):
            _stage(s_ref, p, z_ref[p])
        _stage(s_ref, 3, dm_ref[...])
        acc_ref[...] = jnp.zeros_like(acc_ref)

        def chunk(i, lanes):
            w0, w1, w2 = _taps(w_ref, lanes)
            b, c, u, dm = (_window(s_ref, p, i, lanes) for p in range(4))
            cu = c * u
            cu1, cu2 = _prev(cu, 1), _prev(cu, 2)
            _store_rows(dz_ref, (0,), i, lanes, _valid(dm * (cu2 * w0 + cu1 * w1 + cu * w2)))
            dcv = dm * b
            dcu = dcv * w2 + _next(dcv, 1) * w1 + _next(dcv, 2) * w0
            _store_rows(dz_ref, (1,), i, lanes, _valid(dcu * u))
            _store_rows(dz_ref, (2,), i, lanes, _valid(dcu * c))
            for k, shifted in enumerate((cu2, cu1, cu)):
                acc_ref[k, :, lanes] += _fold8(_valid(dcv * shifted))

        _for_chunks(t, chunk)
        _write_col_sums(acc_ref, [(dw_ref, 0), (dw_ref, 1), (dw_ref, 2)])

    wspec = pl.BlockSpec((3, TC), lambda j: (0, j))
    return _tc_call(
        body, name="scmix_bwd", grid=(D // TC,), in_specs=[_col(3, t), wspec, _col(None, t)],
        out_specs=[_col(3, t), wspec],
        out_shape=[jax.ShapeDtypeStruct((3, t, D), BF16), jax.ShapeDtypeStruct((3, D), F32)],
        scratch_shapes=[_staging(4, t), pltpu.VMEM((3, 8, TC), F32)], compiler_params=_cp("parallel"),
    )(z, w, dm)


def _ffn_up_gate(hf, w_up, w, bias, name):
    t, d = hf.shape
    nb = F_FF // TC

    def body(hf_ref, wg_ref, wv_ref, w_ref, b_ref, up_ref, a_ref, prev_ref):
        @pl.when(pl.program_id(0) == 0)
        def _():
            prev_ref[...] = jnp.zeros_like(prev_ref)

        gc = _conv3(prev_ref[0].astype(F32), w_ref) + b_ref[...]
        a_ref[...] = (gc * jax.nn.sigmoid(gc) * prev_ref[1].astype(F32)).astype(BF16)
        hv = hf_ref[...]
        up_ref[0] = jnp.dot(hv, wg_ref[...], preferred_element_type=F32).astype(BF16)
        up_ref[1] = jnp.dot(hv, wv_ref[...], preferred_element_type=F32).astype(BF16)
        prev_ref[...] = up_ref[...]

    tile = lambda j: jnp.minimum(j, nb - 1)
    gated = lambda j: jnp.maximum(j - 1, 0)
    return _tc_call(
        body, name=name, grid=(nb + 1,),
        in_specs=[pl.BlockSpec((t, d), lambda j: (0, 0)), pl.BlockSpec((d, TC), lambda j: (0, tile(j))),
                  pl.BlockSpec((d, TC), lambda j: (0, nb + tile(j))), pl.BlockSpec((3, TC), lambda j: (0, gated(j))),
                  pl.BlockSpec((1, TC), lambda j: (0, gated(j)))],
        out_specs=[pl.BlockSpec((2, t, TC), lambda j: (0, 0, tile(j))), pl.BlockSpec((t, TC), lambda j: (0, gated(j)))],
        out_shape=[jax.ShapeDtypeStruct((2, t, F_FF), BF16), jax.ShapeDtypeStruct((t, F_FF), BF16)],
        scratch_shapes=[pltpu.VMEM((2, t, TC), BF16)], compiler_params=_cp("arbitrary"),
    )(hf, w_up, w_up, w, bias)


def _gate_bwd(up, w, bias, dh, w_down, name):
    t, d = dh.shape

    def body(u_ref, w_ref, b_ref, dh_ref, wd_ref, du_ref, dw_ref, db_ref, s_ref, acc_ref):
        for p in range(2):
            _stage(s_ref, p, u_ref[p])
        _stage(s_ref, 2, lax.dot_general(dh_ref[...], wd_ref[...], NT_DIMS, preferred_element_type=F32).astype(BF16))
        acc_ref[...] = jnp.zeros_like(acc_ref)

        def chunk(i, lanes):
            w0, w1, w2 = _taps(w_ref, lanes)
            g, v, da = (_window(s_ref, p, i, lanes) for p in range(3))
            g1, g2 = _prev(g, 1), _prev(g, 2)
            gc = g2 * w0 + g1 * w1 + g * w2 + b_ref[:, lanes]
            sg = jax.nn.sigmoid(gc)
            _store_rows(du_ref, (1,), i, lanes, _valid(da * (gc * sg)))
            dgc = da * v * (sg * (1.0 + gc * (1.0 - sg)))
            _store_rows(du_ref, (0,), i, lanes, _valid(dgc * w2 + _next(dgc, 1) * w1 + _next(dgc, 2) * w0))
            for k, shifted in enumerate((g2, g1, g)):
                acc_ref[k, :, lanes] += _fold8(_valid(dgc * shifted))
            acc_ref[3, :, lanes] += _fold8(_valid(dgc))

        _for_chunks(t, chunk)
        _write_col_sums(acc_ref, [(dw_ref, 0), (dw_ref, 1), (dw_ref, 2), (db_ref, 0)])

    wspec = pl.BlockSpec((3, TC), lambda j: (0, j))
    bspec = pl.BlockSpec((1, TC), lambda j: (0, j))
    return _tc_call(
        body, name=name, grid=(F_FF // TC,),
        in_specs=[_col(2, t), wspec, bspec, pl.BlockSpec((t, d), lambda j: (0, 0)), pl.BlockSpec((TC, d), lambda j: (j, 0))],
        out_specs=[_col(2, t), wspec, bspec],
        out_shape=[jax.ShapeDtypeStruct((2, t, F_FF), BF16), jax.ShapeDtypeStruct((3, F_FF), F32),
                   jax.ShapeDtypeStruct((1, F_FF), F32)],
        scratch_shapes=[_staging(3, t), pltpu.VMEM((4, 8, TC), F32)], compiler_params=_cp("parallel"),
    )(up, w, bias, dh, w_down)


ATT_TQ = 256
ATT_SCALE = (QK_NOPE + QK_ROPE) ** -0.5


def _key_ranges(lvl):
    lo = lvl * ATT_TQ
    return ([(0, lo, False)] if lvl else []) + [(lo, lo + ATT_TQ, True)]


FWD_HEADS = 4
BWD_HEADS = 2


def _fill_keys(k_ref, kn_ref, kr_ref):
    @pl.when(pl.program_id(1) == 0)
    def _():
        for hh in range(k_ref.shape[0]):
            k_ref[hh, :, :QK_NOPE] = kn_ref[:, hh * QK_NOPE:(hh + 1) * QK_NOPE]
            k_ref[hh, :, QK_NOPE:] = kr_ref[...]


def _attn_probs(q, k_ref, lvl):
    scores = []
    for lo, hi, diagonal in _key_ranges(lvl):
        s = lax.dot_general(q, k_ref[lo:hi, :], NT_DIMS, preferred_element_type=F32) * ATT_SCALE
        if diagonal:
            row = lax.broadcasted_iota(jnp.int32, s.shape, 0)
            col = lax.broadcasted_iota(jnp.int32, s.shape, 1)
            seen = lax.shift_right_logical(col, CHUNK_SHIFT) <= lax.shift_right_logical(row, CHUNK_SHIFT)
            s = jnp.where(seen, s, NEG_INF)
        scores.append(s)
    m = jnp.max(scores[0], axis=1, keepdims=True)
    for s in scores[1:]:
        m = jnp.maximum(m, jnp.max(s, axis=1, keepdims=True))
    ps = [jnp.exp(s - m) for s in scores]
    total = jnp.sum(ps[0], axis=1, keepdims=True)
    for p in ps[1:]:
        total = total + jnp.sum(p, axis=1, keepdims=True)
    inv = 1.0 / total
    return [p * inv for p in ps]


def _attn_probs_t(q, k_ref, lvl):
    scores = []
    for lo, hi, diagonal in _key_ranges(lvl):
        s = lax.dot_general(k_ref[lo:hi, :], q, NT_DIMS, preferred_element_type=F32) * ATT_SCALE
        if diagonal:
            key = lax.broadcasted_iota(jnp.int32, s.shape, 0)
            qry = lax.broadcasted_iota(jnp.int32, s.shape, 1)
            seen = lax.shift_right_logical(key, CHUNK_SHIFT) <= lax.shift_right_logical(qry, CHUNK_SHIFT)
            s = jnp.where(seen, s, NEG_INF)
        scores.append(s)
    m = jnp.max(scores[0], axis=0, keepdims=True)
    for s in scores[1:]:
        m = jnp.maximum(m, jnp.max(s, axis=0, keepdims=True))
    ps = [jnp.exp(s - m) for s in scores]
    total = jnp.sum(ps[0], axis=0, keepdims=True)
    for p in ps[1:]:
        total = total + jnp.sum(p, axis=0, keepdims=True)
    inv = 1.0 / total
    return [p * inv for p in ps]


def _per_query_block(qi, n_blocks, branch):
    for lvl in range(n_blocks):
        pl.when(qi == lvl)(lambda lvl=lvl: branch(lvl))


def _attn_specs(t, g):
    q = pl.BlockSpec((ATT_TQ, g * HEAD_PAD), lambda h, i: (i, h))
    kn = pl.BlockSpec((None, t, g * QK_NOPE), lambda h, i: (0, 0, h))
    kr = pl.BlockSpec((t, LANES), lambda h, i: (0, 0))
    v = pl.BlockSpec((None, t, g * V_HEAD), lambda h, i: (1, 0, h))
    o = pl.BlockSpec((ATT_TQ, g * V_HEAD), lambda h, i: (i, h))
    return q, kn, kr, v, o


def _attn_fwd(q, knv, kr):
    t = q.shape[0]

    def body(q_ref, kn_ref, kr_ref, v_ref, o_ref, k_ref):
        _fill_keys(k_ref, kn_ref, kr_ref)

        def branch(lvl):
            for hh in range(FWD_HEADS):
                vcols = slice(hh * V_HEAD, (hh + 1) * V_HEAD)
                ps = _attn_probs(q_ref[:, hh * HEAD_PAD:(hh + 1) * HEAD_PAD], k_ref.at[hh], lvl)
                o = None
                for p, (lo, hi, _) in zip(ps, _key_ranges(lvl)):
                    part = jnp.dot(p.astype(BF16), v_ref[lo:hi, vcols], preferred_element_type=F32)
                    o = part if o is None else o + part
                o_ref[:, vcols] = o.astype(BF16)

        _per_query_block(pl.program_id(1), t // ATT_TQ, branch)

    qs, kns, krs, vs, os_ = _attn_specs(t, FWD_HEADS)
    return _tc_call(
        body, name="attn_fwd", grid=(N_HEADS // FWD_HEADS, t // ATT_TQ), in_specs=[qs, kns, krs, vs],
        out_specs=os_, out_shape=jax.ShapeDtypeStruct((t, N_HEADS * V_HEAD), BF16),
        scratch_shapes=[pltpu.VMEM((FWD_HEADS, t, HEAD_PAD), BF16)], compiler_params=_cp("parallel", "arbitrary"),
    )(q, knv, kr, knv)


def _attn_bwd(q, knv, kr, do, cos, sin):
    t = q.shape[0]

    def body(q_ref, kn_ref, kr_ref, v_ref, do_ref, c_ref, s_ref, dq_ref, dknv_ref, dkr_ref, k_ref, dk_ref):
        h, qi = pl.program_id(0), pl.program_id(1)
        _fill_keys(k_ref, kn_ref, kr_ref)

        @pl.when(qi == 0)
        def _():
            dknv_ref[1] = jnp.zeros(dknv_ref.shape[1:], F32)
            dk_ref[...] = jnp.zeros_like(dk_ref)

        @pl.when((qi == 0) & (h == 0))
        def _():
            dkr_ref[...] = jnp.zeros_like(dkr_ref)

        def branch(lvl):
            ranges = _key_ranges(lvl)
            for hh in range(BWD_HEADS):
                qcols = slice(hh * HEAD_PAD, (hh + 1) * HEAD_PAD)
                vcols = slice(hh * V_HEAD, (hh + 1) * V_HEAD)
                qv, dov = q_ref[:, qcols], do_ref[:, vcols]
                ps = _attn_probs_t(qv, k_ref.at[hh], lvl)
                dps = [lax.dot_general(v_ref[lo:hi, vcols], dov, NT_DIMS, preferred_element_type=F32)
                       for lo, hi, _ in ranges]
                di = None
                for p, dp in zip(ps, dps):
                    part = jnp.sum(p * dp, axis=0, keepdims=True)
                    di = part if di is None else di + part
                dq = None
                for p, dp, (lo, hi, _) in zip(ps, dps, ranges):
                    ds = (p * (dp - di) * ATT_SCALE).astype(BF16)
                    part = lax.dot_general(ds, k_ref[hh, lo:hi, :], TN_DIMS, preferred_element_type=F32)
                    dq = part if dq is None else dq + part
                    dk_ref[hh, lo:hi, :] += jnp.dot(ds, qv, preferred_element_type=F32)
                    dknv_ref[1, lo:hi, vcols] += jnp.dot(p.astype(BF16), dov, preferred_element_type=F32)
                dq_ref[:, hh * HEAD_PAD:hh * HEAD_PAD + QK_NOPE] = dq[:, :QK_NOPE].astype(BF16)
                dq_ref[:, hh * HEAD_PAD + QK_NOPE:(hh + 1) * HEAD_PAD] = _rope_bwd_math(
                    dq[:, QK_NOPE:], c_ref[...], s_ref[...]).astype(BF16)

        _per_query_block(qi, t // ATT_TQ, branch)

        @pl.when(qi == t // ATT_TQ - 1)
        def _():
            for hh in range(BWD_HEADS):
                dknv_ref[0, :, hh * QK_NOPE:(hh + 1) * QK_NOPE] = dk_ref[hh, :, :QK_NOPE]
                dkr_ref[...] += dk_ref[hh, :, QK_NOPE:]

    qs, kns, krs, vs, os_ = _attn_specs(t, BWD_HEADS)
    tab = pl.BlockSpec((ATT_TQ, LANES), lambda h, i: (i, 0))
    return _tc_call(
        body, name="attn_bwd", grid=(N_HEADS // BWD_HEADS, t // ATT_TQ), in_specs=[qs, kns, krs, vs, os_, tab, tab],
        out_specs=[qs, pl.BlockSpec((2, t, BWD_HEADS * QK_NOPE), lambda h, i: (0, 0, h)), krs],
        out_shape=[jax.ShapeDtypeStruct((t, N_HEADS * HEAD_PAD), BF16),
                   jax.ShapeDtypeStruct((2, t, N_HEADS * QK_NOPE), F32), jax.ShapeDtypeStruct((t, LANES), F32)],
        scratch_shapes=[pltpu.VMEM((BWD_HEADS, t, HEAD_PAD), BF16), pltpu.VMEM((BWD_HEADS, t, HEAD_PAD), F32)],
        compiler_params=_cp("arbitrary", "arbitrary"),
    )(q, knv, kr, knv, do, cos, sin)


def _adam_math(w, g, m, v):
    nm = ADAM_B1 * m + (1.0 - ADAM_B1) * g
    nv = ADAM_B2 * v + (1.0 - ADAM_B2) * (g * g)
    m_hat = nm / (1.0 - ADAM_B1 ** ADAM_STEP)
    v_hat = nv / (1.0 - ADAM_B2 ** ADAM_STEP)
    return -ADAM_LR * (m_hat / (jnp.sqrt(v_hat) + ADAM_EPS) + ADAM_WD * w), nm, nv


def _adamw_small(w, g, m, v):
    def body(w_ref, g_ref, m_ref, v_ref, d_ref, nm_ref, nv_ref):
        d_ref[...], nm_ref[...], nv_ref[...] = _adam_math(w_ref[...], g_ref[...], m_ref[...], v_ref[...])

    shp = jax.ShapeDtypeStruct(w.shape, F32)
    return _tc_call(body, name="adamw_small", out_shape=[shp] * 3)(w, g, m, v)


ADAM_SPLIT = 4


def _adamw_shards(ids, items, name):
    n = len(items)

    def body(ids_ref, *refs):
        outs = refs[len(refs) - 4 * n:]
        mine = pl.program_id(0) == ids_ref[0]
        for i in range(n):
            w_ref, m_ref, v_ref, gm_ref, gs_ref = refs[5 * i:5 * i + 5]
            g_ref, d_ref, nm_ref, nv_ref = outs[4 * i:4 * i + 4]

            @pl.when(mine)
            def _(g_ref=g_ref, gm_ref=gm_ref):
                g_ref[...] = gm_ref[...]

            @pl.when(jnp.logical_not(mine))
            def _(g_ref=g_ref, gs_ref=gs_ref):
                g_ref[...] = gs_ref[...]

            d_ref[...], nm_ref[...], nv_ref[...] = _adam_math(w_ref[...], g_ref[...], m_ref[...], v_ref[...])

    in_specs, out_specs, out_shape, args, carried, aliases = [], [], [], [ids], [], {}
    for i, it in enumerate(items):
        w = it["w"]
        r, c = w.shape[-2:]
        tr = r // 2 // ADAM_SPLIT
        assert tr % 8 == 0, (name, w.shape)
        layer = it.get("layer")
        if layer is None:
            wspec = pl.BlockSpec((tr, c), lambda h, k, ids: (h * ADAM_SPLIT + k, 0))
        else:
            wspec = pl.BlockSpec((None, tr, c), lambda h, k, ids, layer=layer: (layer, h * ADAM_SPLIT + k, 0))
        gspec = pl.BlockSpec((tr, c), lambda h, k, ids: (k, 0))
        in_specs += [wspec] * 3 + [gspec] * 2
        args += [w, it["m"], it["v"], it["g_mine"], it["g_sib"]]
        out_specs += [wspec] * 4
        out_shape += [jax.ShapeDtypeStruct(w.shape, F32)] * 4
        if it.get("prev") is not None:
            for k, p in enumerate(it["prev"]):
                aliases[1 + 5 * n + len(carried)] = 4 * i + k
                carried.append(p)
    res = _tc_call(
        body, name=name, prefetch=1, grid=(2, ADAM_SPLIT), in_specs=in_specs + [ANY] * len(carried),
        out_specs=out_specs, out_shape=out_shape, input_output_aliases=aliases,
        compiler_params=_cp("parallel", "parallel"),
    )(*args, *carried)
    return [res[4 * i:4 * i + 4] for i in range(n)]


def _peer_chip(k_me, j):
    return k_me ^ jnp.where(j == 0, 2, jnp.where(j == 1, 1, 3))


def _pair_sums(ids, gs, ras, name):
    n = len(gs)

    def body(ids_ref, *refs):
        for i in range(n):
            g_ref, ra_ref, o_ref = refs[2 * i], refs[2 * i + 1], refs[2 * n + i]
            o_ref[...] = (g_ref[...].astype(F32) + ra_ref[...].astype(F32)).astype(BF16)

    in_specs, out_specs, out_shape = [], [], []
    for g in gs:
        half, c = g.shape[1] // 2, g.shape[2]
        in_specs += [pl.BlockSpec((None, half, c), lambda j, ids: (_peer_chip(ids[1], j), ids[0], 0)),
                     pl.BlockSpec((None, half, c), lambda j, ids: (_peer_chip(ids[1], j), 0, 0))]
        out_specs.append(pl.BlockSpec((None, half, c), lambda j, ids: (j, 0, 0)))
        out_shape.append(jax.ShapeDtypeStruct((3, half, c), BF16))
    return _tc_call(
        body, name=name, prefetch=1, grid=(3,), in_specs=in_specs, out_specs=out_specs, out_shape=out_shape,
        compiler_params=_cp("parallel"),
    )(ids, *[a for pair in zip(gs, ras) for a in pair])


def _chip_sums(ids, gs, ras, rbs, name):
    n = len(gs)

    def body(ids_ref, *refs):
        for i in range(n):
            g_ref, ra_ref, rb_ref, o_ref = refs[3 * i], refs[3 * i + 1], refs[3 * i + 2], refs[3 * n + i]
            acc = g_ref[...].astype(F32) + ra_ref[...].astype(F32)
            for j in range(3):
                acc = acc + rb_ref[j].astype(F32)
            o_ref[...] = acc

    in_specs, out_specs, out_shape = [], [], []
    for g in gs:
        half, c = g.shape[1] // 2, g.shape[2]
        in_specs += [pl.BlockSpec((None, half, c), lambda i, ids: (ids[1], ids[0], 0)),
                     pl.BlockSpec((None, half, c), lambda i, ids: (ids[1], 0, 0)),
                     pl.BlockSpec((3, half, c), lambda i, ids: (0, 0, 0))]
        out_specs.append(pl.BlockSpec((half, c), lambda i, ids: (0, 0)))
        out_shape.append(jax.ShapeDtypeStruct((half, c), F32))
    return _tc_call(
        body, name=name, prefetch=1, grid=(1,), in_specs=in_specs, out_specs=out_specs, out_shape=out_shape,
        compiler_params=_cp("arbitrary"),
    )(ids, *[a for trio in zip(gs, ras, rbs) for a in trio])


def _position():
    x, y, c = lax.axis_index("x"), lax.axis_index("y"), lax.axis_index("c")
    chips = [(1 - x, y), (x, 1 - y), (1 - x, 1 - y)]
    return x, y, c, chips


def _shard_half(ref, wm, h):
    if wm.kind == "tiny":
        return ref
    if wm.nl == 2:
        return ref.at[h]
    return ref.at[pl.ds(pl.multiple_of(h * (wm.k // 2), 16), wm.k // 2), :]


def _region(full, wm, s, h):
    if wm.kind == "tiny":
        return full.at[s]
    cols = pl.ds(pl.multiple_of(s * wm.n, LANES), wm.n) if wm.kind == "col" else slice(None)
    if wm.nl == 2:
        rows = pl.ds(pl.multiple_of(s * wm.k, 16), wm.k) if wm.kind == "row" else slice(None)
        return full.at[slice(None) if h is None else h, rows, cols]
    if wm.kind == "col":
        rows = slice(None) if h is None else pl.ds(pl.multiple_of(h * (wm.k // 2), 16), wm.k // 2)
    elif h is None:
        rows = pl.ds(pl.multiple_of(s * wm.k, 16), wm.k)
    else:
        rows = pl.ds(pl.multiple_of(s * wm.k + h * (wm.k // 2), 16), wm.k // 2)
    return full.at[rows, cols]


def _full_shape(wm):
    if wm.kind == "tiny":
        return (N_CHIPS, wm.k, wm.n)
    shape = (wm.k, N_CHIPS * wm.n) if wm.kind == "col" else (N_CHIPS * wm.k, wm.n)
    return shape if wm.nl == 1 else (wm.nl,) + shape


def _handshake(peers):
    barrier = pltpu.get_barrier_semaphore()
    for peer in peers:
        pl.semaphore_signal(barrier, inc=1, device_id=peer, device_id_type=MESH)
    pl.semaphore_wait(barrier, len(peers))


def _all_gather_group(gi, shards):
    wms = AG_GROUPS[gi]
    nw = len(wms)

    def body(*refs):
        sh, full = refs[:nw], refs[nw:2 * nw]
        ici_s, ici_r, pass_s, pass_r, own_s, own_r = refs[2 * nw:]
        x, y, c, _ = _position()
        me, sibling = 2 * x + y, (x, y, 1 - c)
        first, second, diagonal = (x ^ (1 - c), y ^ c), (x ^ c, y ^ (1 - c)), (1 - x, 1 - y)
        chip_id = lambda chip: 2 * chip[0] + chip[1]
        _handshake([(*first, c), (*second, c), sibling])

        def rcopy(src, dst, s_sem, r_sem, to):
            return pltpu.make_async_remote_copy(src_ref=src, dst_ref=dst, send_sem=s_sem, recv_sem=r_sem,
                                                device_id=to, device_id_type=MESH)

        started = []

        def go(cp):
            cp.start()
            started.append(cp)

        for i, wm in enumerate(wms):
            half, dst = _shard_half(sh[i], wm, c), _region(full[i], wm, me, c)
            go(rcopy(half, dst, ici_s.at[i, 0], ici_r.at[i, 0], (*first, c)))
            go(rcopy(half, dst, ici_s.at[i, 1], ici_r.at[i, 1], (*second, c)))
            go(rcopy(sh[i], _region(full[i], wm, me, None), own_s.at[i], own_r.at[i], sibling))
        for i, wm in enumerate(wms):
            got = _region(full[i], wm, chip_id(first), c)
            rcopy(got, got, ici_s.at[i, 0], ici_r.at[i, 0], sibling).wait_recv()
            go(rcopy(got, got, ici_s.at[i, 2], ici_r.at[i, 2], (*second, c)))
            if wm.kind != "tiny":
                go(rcopy(got, got, pass_s.at[i, 0], pass_r.at[i, 0], sibling))
        for i, wm in enumerate(wms):
            for j, chip in ((1, second), (2, diagonal)):
                got = _region(full[i], wm, chip_id(chip), c)
                rcopy(got, got, ici_s.at[i, j], ici_r.at[i, j], sibling).wait_recv()
                if wm.kind != "tiny":
                    go(rcopy(got, got, pass_s.at[i, j], pass_r.at[i, j], sibling))
        for i, wm in enumerate(wms):
            mine = _region(full[i], wm, me, None)
            rcopy(mine, mine, own_s.at[i], own_r.at[i], sibling).wait_recv()
            if wm.kind != "tiny":
                for j, chip in ((0, second), (1, first), (2, diagonal)):
                    got = _region(full[i], wm, chip_id(chip), 1 - c)
                    rcopy(got, got, pass_s.at[i, j], pass_r.at[i, j], sibling).wait_recv()
        for cp in started:
            cp.wait_send()

    return pl.kernel(
        body, out_type=[jax.ShapeDtypeStruct(_full_shape(wm), s.dtype) for wm, s in zip(wms, shards)],
        mesh=plsc.ScalarSubcoreMesh(axis_name="sequencer", num_cores=1), name=f"ag_group{gi}",
        scratch_types=[pltpu.SemaphoreType.DMA((nw, 3))] * 4 + [pltpu.SemaphoreType.DMA((nw,))] * 2,
        compiler_params=pltpu.CompilerParams(collective_id=gi),
    )(*shards)


def _sequencer_call(body, name, cid, out_types, scratch, args):
    return pl.kernel(
        body, out_type=out_types, mesh=plsc.ScalarSubcoreMesh(axis_name="sequencer", num_cores=1), name=name,
        scratch_types=scratch, compiler_params=pltpu.CompilerParams(collective_id=cid),
    )(*args)


def _pair_exchange(gs, tag, cid):
    n = len(gs)

    def body(*refs):
        g, out, send_sems, recv_sems = refs[:n], refs[n:2 * n], refs[2 * n], refs[2 * n + 1]
        x, y, c, _ = _position()
        _handshake([(x, y, 1 - c)])
        cps = []
        for i in range(n):
            half = g[i].shape[1] // 2
            cps.append(pltpu.make_async_remote_copy(
                src_ref=g[i].at[:, pl.ds(pl.multiple_of((1 - c) * half, 16), half), :], dst_ref=out[i],
                send_sem=send_sems.at[i], recv_sem=recv_sems.at[i], device_id=(x, y, 1 - c), device_id_type=MESH))
            cps[-1].start()
        for cp in cps:
            cp.wait()

    return _sequencer_call(
        body, f"rs_pair_exchange{tag}", cid,
        [jax.ShapeDtypeStruct((a.shape[0], a.shape[1] // 2, a.shape[2]), a.dtype) for a in gs],
        [pltpu.SemaphoreType.DMA((n,)), pltpu.SemaphoreType.DMA((n,))], gs)


def _chip_exchange(ss, tag, cid):
    n = len(ss)

    def body(*refs):
        s, out, send_sems, recv_sems = refs[:n], refs[n:2 * n], refs[2 * n], refs[2 * n + 1]
        x, y, c, chips = _position()
        _handshake([(*chip, c) for chip in chips])
        cps = []
        for i in range(n):
            for j, chip in enumerate(chips):
                cps.append(pltpu.make_async_remote_copy(
                    src_ref=s[i].at[j], dst_ref=out[i].at[j], send_sem=send_sems.at[i, j], recv_sem=recv_sems.at[i, j],
                    device_id=(*chip, c), device_id_type=MESH))
                cps[-1].start()
        for cp in cps:
            cp.wait()

    return _sequencer_call(
        body, f"rs_chip_exchange{tag}", cid, [jax.ShapeDtypeStruct(a.shape, a.dtype) for a in ss],
        [pltpu.SemaphoreType.DMA((n, 3)), pltpu.SemaphoreType.DMA((n, 3))], ss)


def _pair_swap(g8s, tag, cid):
    n = len(g8s)

    def body(*refs):
        g, out, send_sems, recv_sems = refs[:n], refs[n:2 * n], refs[2 * n], refs[2 * n + 1]
        x, y, c, _ = _position()
        _handshake([(x, y, 1 - c)])
        cps = []
        for i in range(n):
            cps.append(pltpu.make_async_remote_copy(
                src_ref=g[i], dst_ref=out[i], send_sem=send_sems.at[i], recv_sem=recv_sems.at[i],
                device_id=(x, y, 1 - c), device_id_type=MESH))
            cps[-1].start()
        for cp in cps:
            cp.wait()

    return _sequencer_call(
        body, f"rs_pair_swap{tag}", cid, [jax.ShapeDtypeStruct(a.shape, a.dtype) for a in g8s],
        [pltpu.SemaphoreType.DMA((n,)), pltpu.SemaphoreType.DMA((n,))], g8s)


def _pair_swap_now(g8s):
    n = len(g8s)

    def body(*refs):
        g, out, send_sems, recv_sems = refs[:n], refs[n:2 * n], refs[2 * n], refs[2 * n + 1]
        x, y, c, _ = _position()
        cps = []
        for i in range(n):
            cps.append(pltpu.make_async_remote_copy(
                src_ref=g[i], dst_ref=out[i], send_sem=send_sems.at[i], recv_sem=recv_sems.at[i],
                device_id=(x, y, 1 - c), device_id_type=MESH))
            cps[-1].start()
        for cp in cps:
            cp.wait()

    return _tc_call(
        body, name="rs_pair_swap_last", in_specs=[ANY] * n, out_specs=[ANY] * n,
        out_shape=[jax.ShapeDtypeStruct(a.shape, a.dtype) for a in g8s],
        scratch_shapes=[pltpu.SemaphoreType.DMA((n,)), pltpu.SemaphoreType.DMA((n,))],
    )(*g8s)


def _all_reduce_small(vec, name):
    r, cols = vec.shape

    def body(v_ref, o_ref, gath, send_sems, recv_sems):
        x, y, c, _ = _position()
        me = 4 * x + 2 * y + c
        gath[me] = v_ref[...]
        cps = []
        for rel in range(1, N_DEV):
            peer = (x ^ (rel >> 2), y ^ ((rel >> 1) & 1), c ^ (rel & 1))
            cps.append(pltpu.make_async_remote_copy(
                src_ref=v_ref, dst_ref=gath.at[me], send_sem=send_sems.at[rel - 1], recv_sem=recv_sems.at[rel - 1],
                device_id=peer, device_id_type=MESH))
        for cp in cps:
            cp.start()
        for rel in range(1, N_DEV):
            pltpu.make_async_remote_copy(
                src_ref=v_ref, dst_ref=gath.at[me ^ rel], send_sem=send_sems.at[rel - 1],
                recv_sem=recv_sems.at[rel - 1], device_id=(x, y, c), device_id_type=MESH).wait_recv()
        for cp in cps:
            cp.wait_send()
        acc = gath[0]
        for d in range(1, N_DEV):
            acc = acc + gath[d]
        o_ref[...] = acc

    vm = pl.BlockSpec(memory_space=pltpu.VMEM)
    return _tc_call(
        body, name=name, in_specs=[vm], out_specs=vm, out_shape=jax.ShapeDtypeStruct((r, cols), F32),
        scratch_shapes=[pltpu.VMEM((N_DEV, r, cols), F32), pltpu.SemaphoreType.DMA((N_DEV - 1,)),
                        pltpu.SemaphoreType.DMA((N_DEV - 1,))],
    )(vec)


def _rope_tables(positions):
    half = QK_ROPE // 2
    inv_freq = 1.0 / (ROPE_THETA ** (jnp.arange(half, dtype=F32) / half))
    ang = positions.astype(F32)[:, None] * inv_freq
    zeros = jnp.zeros((positions.shape[0], LANES - QK_ROPE), F32)
    cos, sin = jnp.cos(ang), jnp.sin(ang)
    return jnp.concatenate([cos, cos, zeros], axis=1), jnp.concatenate([sin, sin, zeros], axis=1)


def _local_step(x, positions, tgt, wf, small, rs):
    cos, sin = _rope_tables(positions)
    w_in, w_out = wf["sc_w_in"], wf["sc_w_out"]
    w_ups, w_downs = (wf["ffn_w_up0"], wf["ffn_w_up1"]), (wf["ffn_w_down0"], wf["ffn_w_down1"])
    w_kv, w_ukv, w_dq, w_uq, w_o = wf["w_kv"], wf["w_ukv"], wf["w_dq"], wf["w_uq"], wf["w_o"]
    attn_norm, ffn_norm = small["attn_norm"], small["ffn_norm"]
    conv_b = small["ffn_conv_b"]

    def ffn_fwd(h, hf, l, then):
        up, a = _ffn_up_gate(hf, w_ups[l], small["ffn_conv_w"][l], conv_b[l:l + 1], f"ffn{l}_up_gate")
        return then(a, w_downs[l], h), (hf, up, a)

    def ffn_bwd(h, dh_out, dh_out_b, l, saved, gi, hooks):
        run = lambda stage: hooks.get(stage, lambda: None)()
        hf, up, a = saved
        d_down = _tn(f"ffn{l}_down_dw", a, dh_out_b, BF16)
        run("down_dw")
        dup, d_cw, d_cb = _gate_bwd(up, small["ffn_conv_w"][l], conv_b[l:l + 1], dh_out_b, w_downs[l],
                                    f"ffn{l}_gate_bwd")
        run("gate_bwd")
        d_up = _dw_ffn_up(f"ffn{l}_up_dw", hf, dup)
        rs.start(gi, {f"ffn_w_down{l}": d_down.reshape(N_CHIPS, F_FF // N_CHIPS, D), f"ffn_w_up{l}": d_up})
        run("up_dw")
        dh, dh_b, d_norm = _dx_norm_bwd(f"ffn{l}_up_dx", dup, w_ups[l], h, ffn_norm[l:l + 1], dh_out)
        run("up_dx")
        return dh, dh_b, d_cw, d_cb, d_norm

    hn0 = _rms_fwd(x, attn_norm[0:1], "attn0_norm")
    z = _nn_parts("sc_in", hn0, w_in, 3, BF16)
    mix = _scmix_fwd(z, small["sc_conv_w"])
    h1, hf0 = _nn_add_norm("sc_out", mix, w_out, x, ffn_norm[0:1])
    h2, ffn0_saved = ffn_fwd(h1, hf0, 0, lambda a, w, h: _nn("ffn0_down", a, w, F32, add=h))

    hn1, hk, cq_pre, cq, q, kvpre, ckv, kr, knv = _attn_prep(
        h2, attn_norm[1:2], small["kv_in_norm"], w_dq, small["q_latent_norm"], w_uq, w_kv, small["kv_latent_norm"],
        w_ukv, cos, sin)
    o = _attn_fwd(q, knv, kr)
    h3, hf1 = _nn_add_norm("attn_out", o, w_o, h2, ffn_norm[1:2])
    (loss, dh4, dh4_b, d_final), ffn1_saved = ffn_fwd(
        h3, hf1, 1, lambda a, w, h: _nn_add_loss("ffn1_down_loss", a, w, h, small["final_norm"], tgt))

    rows = D // N_CHIPS
    dh3, dh3_b, d_cw1, d_cb1, d_fn1 = ffn_bwd(h3, dh4, dh4_b, 1, ffn1_saved, 0, {})

    do = _nt("attn_out_dx", dh3_b, w_o, BF16)
    d_wo = _tn("attn_out_dw", o, dh3_b, BF16)
    rs.pair_sums(0)
    dq, dknv, dkr = _attn_bwd(q, knv, kr, do, cos, sin)
    rs.chip_sums(0)
    dh2, dh2_b, d_wuq, d_wdq, d_wukv, d_wkv, d_an1, d_kvin, d_qln, d_kvln = _attn_prep_bwd(
        dq, dknv, dkr, dh3, h2, hn1, hk, cq_pre, cq, kvpre, ckv, attn_norm[1:2], small["kv_in_norm"], w_dq,
        small["q_latent_norm"], w_uq, w_kv, small["kv_latent_norm"], w_ukv, cos, sin)
    rs.finish(0)
    by_owner = lambda dw: dw.reshape(dw.shape[0], N_CHIPS, -1).transpose(1, 0, 2)
    rs.start(1, {
        "w_o": d_wo.reshape(N_CHIPS, rows, D), "w_uq": by_owner(d_wuq), "w_dq": d_wdq.reshape(N_CHIPS, rows, Q_LORA),
        "w_ukv": by_owner(d_wukv.reshape(2 * KV_LORA, -1)).reshape(N_CHIPS, 2 * KV_LORA, -1),
        "w_kv": d_wkv.reshape(N_CHIPS, rows, KVP),
    })

    dh1, dh1_b, d_cw0, d_cb0, d_fn0 = ffn_bwd(h1, dh2, dh2_b, 0, ffn0_saved, 2, {
        "down_dw": lambda: rs.pair_sums(1), "gate_bwd": lambda: rs.chip_sums(1), "up_dw": lambda: rs.finish(1),
        "up_dx": lambda: rs.pair_sums(2)})

    d_wout = _tn("sc_out_dw", mix, dh1_b, BF16)
    dmix = _nt("sc_out_dx", dh1_b, w_out, BF16)
    dz, d_scw = _scmix_bwd(z, small["sc_conv_w"], dmix)
    d_win = _dw_sc_in(hn0, dz)
    rs.start(3, {"sc_w_out": d_wout.reshape(N_CHIPS, rows, D), "sc_w_in": d_win})
    dx, _, d_an0 = _dx_norm_bwd("sc_in_dx", dz, w_in, x, attn_norm[0:1], dh1)

    small_g = {
        "attn_norm": jnp.concatenate([d_an0, d_an1]), "ffn_norm": jnp.concatenate([d_fn0, d_fn1]),
        "final_norm": d_final, "kv_in_norm": d_kvin, "kv_latent_norm": d_kvln, "q_latent_norm": d_qln,
        "ffn_conv_b": jnp.concatenate([d_cb0, d_cb1]), "sc_conv_w": d_scw, "ffn_conv_w": jnp.stack([d_cw0, d_cw1]),
    }
    return loss, dx, small_g


RS_GROUPS = (("ffn_w_down1", "ffn_w_up1"), ("w_o", "w_uq", "w_dq", "w_ukv", "w_kv"),
             ("ffn_w_down0", "ffn_w_up0"), ("sc_w_out", "sc_w_in"))


class _ReduceScatter:
    def __init__(self, ids, finish):
        self.ids, self.grads, self.step, self.mine, self.sib, self.finish = ids, {}, {}, {}, {}, finish

    def _cid(self, gi):
        return len(AG_GROUPS) + 3 * gi

    def start(self, gi, grads):
        self.grads.update(grads)
        own = [grads[n] for n in RS_GROUPS[gi]]
        self.step[gi] = (own, _pair_exchange(own, gi, self._cid(gi)))

    def pair_sums(self, gi):
        own, ra = self.step[gi]
        sums = _pair_sums(self.ids, own, ra, f"rs_pair_sums{gi}")
        self.step[gi] = (own, ra, _chip_exchange(sums, gi, self._cid(gi) + 1))

    def chip_sums(self, gi):
        own, ra, rb = self.step[gi]
        mine = _chip_sums(self.ids, own, ra, rb, f"rs_chip_sums{gi}")
        self.mine.update(zip(RS_GROUPS[gi], mine))
        last = gi == len(RS_GROUPS) - 1
        swapped = _pair_swap_now(mine) if last else _pair_swap(mine, gi, self._cid(gi) + 2)
        self.sib.update(zip(RS_GROUPS[gi], swapped))

SMALL_REPL = ("attn_norm", "ffn_norm", "final_norm", "kv_in_norm", "kv_latent_norm", "q_latent_norm", "ffn_conv_b")
SMALL_SHARDED = ("sc_conv_w", "ffn_conv_w")
SMALL_ROWS = 256


def _pad_heads(w_uq):
    per_head = w_uq.reshape(Q_LORA, -1, QK_NOPE + QK_ROPE)
    return jnp.pad(per_head, ((0, 0), (0, 0), (0, HEAD_PAD - QK_NOPE - QK_ROPE))).reshape(Q_LORA, -1)


def _pack_kv(w_dkv, w_kr):
    return jnp.concatenate([w_dkv, w_kr, jnp.zeros((w_kr.shape[0], LANES - QK_ROPE), w_kr.dtype)], axis=1)


def kernel(x, positions, attn_norm, ffn_norm, final_norm, sc_w_in, sc_conv_w, sc_w_out, kv_in_norm, w_dkv, kv_latent_norm, w_kr, w_uk, w_uv, w_dq, q_latent_norm, w_uq, w_o, ffn_w_up, ffn_conv_w, ffn_conv_b, ffn_w_down, loss_target, m_attn_norm, m_ffn_norm, m_final_norm, m_sc_w_in, m_sc_conv_w, m_sc_w_out, m_kv_in_norm, m_w_dkv, m_kv_latent_norm, m_w_kr, m_w_uk, m_w_uv, m_w_dq, m_q_latent_norm, m_w_uq, m_w_o, m_ffn_w_up, m_ffn_conv_w, m_ffn_conv_b, m_ffn_w_down, v_attn_norm, v_ffn_norm, v_final_norm, v_sc_w_in, v_sc_conv_w, v_sc_w_out, v_kv_in_norm, v_w_dkv, v_kv_latent_norm, v_w_kr, v_w_uk, v_w_uv, v_w_dq, v_q_latent_norm, v_w_uq, v_w_o, v_ffn_w_up, v_ffn_conv_w, v_ffn_conv_b, v_ffn_w_down):
    names = ("attn_norm", "ffn_norm", "final_norm", "sc_w_in", "sc_conv_w", "sc_w_out", "kv_in_norm", "w_dkv",
             "kv_latent_norm", "w_kr", "w_uk", "w_uv", "w_dq", "q_latent_norm", "w_uq", "w_o", "ffn_w_up",
             "ffn_conv_w", "ffn_conv_b", "ffn_w_down")
    w = dict(zip(names, (attn_norm, ffn_norm, final_norm, sc_w_in, sc_conv_w, sc_w_out, kv_in_norm, w_dkv,
                         kv_latent_norm, w_kr, w_uk, w_uv, w_dq, q_latent_norm, w_uq, w_o, ffn_w_up,
                         ffn_conv_w, ffn_conv_b, ffn_w_down)))
    m = dict(zip(names, (m_attn_norm, m_ffn_norm, m_final_norm, m_sc_w_in, m_sc_conv_w, m_sc_w_out, m_kv_in_norm,
                         m_w_dkv, m_kv_latent_norm, m_w_kr, m_w_uk, m_w_uv, m_w_dq, m_q_latent_norm, m_w_uq, m_w_o,
                         m_ffn_w_up, m_ffn_conv_w, m_ffn_conv_b, m_ffn_w_down)))
    v = dict(zip(names, (v_attn_norm, v_ffn_norm, v_final_norm, v_sc_w_in, v_sc_conv_w, v_sc_w_out, v_kv_in_norm,
                         v_w_dkv, v_kv_latent_norm, v_w_kr, v_w_uk, v_w_uv, v_w_dq, v_q_latent_norm, v_w_uq, v_w_o,
                         v_ffn_w_up, v_ffn_conv_w, v_ffn_conv_b, v_ffn_w_down)))

    _ORDER[0] = None
    ix, iy, ic = lax.axis_index("x"), lax.axis_index("y"), lax.axis_index("c")
    chip = 2 * ix + iy
    ids = jnp.stack([ic, chip]).astype(jnp.int32)

    def shards_of(t):
        return {
            "sc_w_in": t["sc_w_in"][0], "sc_w_out": t["sc_w_out"][0], "ffn_w_up": t["ffn_w_up"],
            "ffn_w_down": t["ffn_w_down"], "w_kv": _pack_kv(t["w_dkv"], t["w_kr"]),
            "w_ukv": jnp.stack([t["w_uk"], t["w_uv"]]), "w_dq": t["w_dq"][0], "w_uq": _pad_heads(t["w_uq"][0]),
            "w_o": t["w_o"][0],
        }

    ws, ms, vs = shards_of(w), shards_of(m), shards_of(v)

    def ag_shard(name):
        if name == "sc_conv_w":
            return sc_conv_w[0]
        if name == "ffn_conv_w":
            return ffn_conv_w.reshape(6, -1)
        if name[:-1] in ("ffn_w_up", "ffn_w_down"):
            return ws[name[:-1]][int(name[-1])].astype(BF16)
        return ws[name].astype(BF16)

    wf = {}
    for gi, wms in enumerate(AG_GROUPS):
        fulls = _all_gather_group(gi, [ag_shard(wm.name) for wm in wms])
        wf.update({wm.name: f for wm, f in zip(wms, fulls)})
    small = {
        "attn_norm": attn_norm, "ffn_norm": ffn_norm, "final_norm": final_norm[None], "kv_in_norm": kv_in_norm[None],
        "kv_latent_norm": kv_latent_norm[None], "q_latent_norm": q_latent_norm, "ffn_conv_b": ffn_conv_b,
        "sc_conv_w": wf["sc_conv_w"].transpose(1, 0, 2).reshape(3, D),
        "ffn_conv_w": wf["ffn_conv_w"].reshape(N_CHIPS, 2, 3, -1).transpose(1, 2, 0, 3).reshape(2, 3, F_FF),
    }

    res = {}

    merged = lambda a: a.reshape(2 * KV_LORA, -1)

    def adamw_group(gi):
        items = []
        for key in RS_GROUPS[gi]:
            n, layer = (key[:-1], int(key[-1])) if key[:-1] in ("ffn_w_up", "ffn_w_down") else (key, None)
            w_, m_, v_ = (merged(t[n]) for t in (ws, ms, vs)) if n == "w_ukv" else (ws[n], ms[n], vs[n])
            items.append(dict(name=n, w=w_, m=m_, v=v_, g_mine=rs.mine[key], g_sib=rs.sib[key], layer=layer,
                              prev=res.get(n)))
        for it, out in zip(items, _adamw_shards(ids, items, f"adamw_group{gi}")):
            res[it["name"]] = out

    rs = _ReduceScatter(ids, adamw_group)
    loss, dx, small_g = _local_step(x[0], positions[0], loss_target[0], wf, small, rs)

    rs.chip_sums(2)
    rs.pair_sums(3)

    s_order = SMALL_REPL + SMALL_SHARDED
    flat = jnp.concatenate([small_g[n].reshape(-1) for n in s_order] + [loss.reshape(-1)])
    flat = jnp.pad(flat, (0, SMALL_ROWS * LANES - flat.shape[0])).reshape(SMALL_ROWS, LANES)
    red = _all_reduce_small(flat, "ar_small").reshape(-1)
    sg, off = {}, 0
    for n in s_order:
        sz = small_g[n].size
        sg[n] = red[off:off + sz].reshape(small_g[n].shape)
        off += sz
    loss_out = red[off]
    grads = {n: sg[n].reshape(w[n].shape) for n in SMALL_REPL}
    grads["sc_conv_w"] = lax.dynamic_slice_in_dim(sg["sc_conv_w"], chip * (D // N_CHIPS), D // N_CHIPS, axis=1)[None]
    grads["ffn_conv_w"] = lax.dynamic_slice_in_dim(sg["ffn_conv_w"], chip * (F_FF // N_CHIPS), F_FF // N_CHIPS, axis=2)

    small_names = SMALL_REPL + SMALL_SHARDED

    def pack_small(tree):
        return jnp.concatenate([tree[n].reshape(-1) for n in small_names]).reshape(-1, LANES)

    small_res = _adamw_small(pack_small(w), pack_small(grads), pack_small(m), pack_small(v))
    rs.finish(2)
    rs.chip_sums(3)
    rs.finish(3)
    outs = [grads, {}, {}, {}]
    for k, dst in enumerate(outs):
        for n in ("sc_w_in", "sc_w_out", "w_dq", "w_o"):
            dst[n] = res[n][k][None]
        unpadded = res["w_uq"][k].reshape(Q_LORA, -1, HEAD_PAD)[:, :, :QK_NOPE + QK_ROPE]
        dst["w_uq"] = unpadded.reshape(w_uq.shape)
        dst["ffn_w_up"], dst["ffn_w_down"] = res["ffn_w_up"][k], res["ffn_w_down"][k]
        dst["w_dkv"], dst["w_kr"] = res["w_kv"][k][:, :KV_LORA], res["w_kv"][k][:, KV_LORA:KV_LORA + QK_ROPE]
        dst["w_uk"], dst["w_uv"] = res["w_ukv"][k][:KV_LORA], res["w_ukv"][k][KV_LORA:]
    grads, delta, new_m, new_v = outs
    for slab, dst in zip(small_res, (delta, new_m, new_v)):
        f, off = slab.reshape(-1), 0
        for n in small_names:
            dst[n] = f[off:off + w[n].size].reshape(w[n].shape)
            off += w[n].size

    _ORDER[0] = None
    return (loss_out, dx[None], *[grads[n] for n in names], *[delta[n] for n in names],
            *[new_m[n] for n in names], *[new_v[n] for n in names])
```

```python
from typing import NamedTuple

import jax
import jax.numpy as jnp
from jax import lax
from jax.experimental import pallas as pl
from jax.experimental.pallas import tpu as pltpu
from jax.experimental.pallas import tpu_sc as plsc

F32 = jnp.float32
BF16 = jnp.bfloat16

T = 2048
D = 1024
F_FF = 2816
N_HEADS = 8
QK_NOPE = 128
QK_ROPE = 64
V_HEAD = 128
Q_LORA = 384
KV_LORA = 256
CHUNK_SHIFT = 6
ROPE_THETA = 10000.0
EPS = 1e-6
NEG_INF = -1e30
HEAD_PAD = 256
KVP = KV_LORA + 128

ADAM_LR = 0.001
ADAM_B1 = 0.9
ADAM_B2 = 0.999
ADAM_EPS = 1e-08
ADAM_WD = 0.01
ADAM_STEP = 10

N_CHIPS = 4
N_DEV = 8
LANES = 128
TC = 256
V7X_VMEM_LIMIT = 56 * 1024 * 1024

MESH = pl.DeviceIdType.MESH
ANY = pl.BlockSpec(memory_space=pl.ANY)


class _W(NamedTuple):
    name: str
    kind: str
    nl: int
    k: int
    n: int


AG_GROUPS = (
    (_W("sc_w_in", "col", 1, D, 3 * D // N_CHIPS), _W("sc_conv_w", "tiny", 1, 3, D // N_CHIPS),
     _W("ffn_conv_w", "tiny", 1, 6, F_FF // N_CHIPS), _W("sc_w_out", "row", 1, D // N_CHIPS, D)),
    (_W("ffn_w_up0", "col", 1, D, 2 * F_FF // N_CHIPS),),
    (_W("ffn_w_down0", "row", 1, F_FF // N_CHIPS, D),),
    (_W("w_kv", "row", 1, D // N_CHIPS, KVP), _W("w_ukv", "col", 2, KV_LORA, N_HEADS * QK_NOPE // N_CHIPS),
     _W("w_dq", "row", 1, D // N_CHIPS, Q_LORA),
     _W("w_uq", "col", 1, Q_LORA, N_HEADS * HEAD_PAD // N_CHIPS),
     _W("w_o", "row", 1, N_HEADS * V_HEAD // N_CHIPS, D)),
    (_W("ffn_w_up1", "col", 1, D, 2 * F_FF // N_CHIPS), _W("ffn_w_down1", "row", 1, F_FF // N_CHIPS, D)),
)


def _cp(*sem):
    return pltpu.CompilerParams(dimension_semantics=sem, vmem_limit_bytes=V7X_VMEM_LIMIT)


_ORDER = [None]


def _tc_call(body, *, name, out_shape, in_specs=None, out_specs=None, grid=(), scratch_shapes=(), prefetch=0,
             input_output_aliases=None, compiler_params=None):
    def run(*args):
        specs = [pl.BlockSpec(memory_space=pltpu.VMEM)] * (len(args) - prefetch) if in_specs is None else list(in_specs)
        inner, dep = body, _ORDER[0]
        if dep is not None:
            unread = prefetch + len(specs)
            specs, args = specs + [ANY], (*args, dep)

            def inner(*refs):
                return body(*refs[:unread], *refs[unread + 1:])

        kwargs = dict(name=name, out_shape=out_shape, input_output_aliases=input_output_aliases or {},
                      compiler_params=compiler_params)
        if prefetch:
            kwargs["grid_spec"] = pltpu.PrefetchScalarGridSpec(
                num_scalar_prefetch=prefetch, grid=grid, in_specs=specs, out_specs=out_specs,
                scratch_shapes=scratch_shapes)
        else:
            kwargs.update(grid=grid, in_specs=specs, scratch_shapes=scratch_shapes)
            if out_specs is not None:
                kwargs["out_specs"] = out_specs
        out = pl.pallas_call(inner, **kwargs)(*args)
        _ORDER[0] = out[0] if isinstance(out, (list, tuple)) else out
        return out

    return run


def _tile(n, cands):
    for c in cands:
        if n % c == 0:
            return c
    raise ValueError(f"no tile for {n}")


NN_DIMS = (((1,), (0,)), ((), ()))
NT_DIMS = (((1,), (1,)), ((), ()))
TN_DIMS = (((0,), (0,)), ((), ()))
M_TILES = (1024, 512, 384, 256, 128)
N_TILES = (1408, 1024, 768, 512, 384, 256, 128)
MM_BLOCK_BYTES = 36 * 1024 * 1024


def _fit(m, n, block_bytes, m_tiles=M_TILES, n_tiles=N_TILES):
    for tm in [c for c in m_tiles if m % c == 0]:
        for tn in [c for c in n_tiles if n % c == 0]:
            if 2 * block_bytes(tm, tn) + 4 * tm * tn <= MM_BLOCK_BYTES:
                return tm, tn
    raise ValueError(f"no tiles for {m} x {n}")


def _size(x):
    return x.dtype.itemsize


def _mm(name, a, b, dims, grid, a_spec, b_spec, o_spec, o_sds, add=None, red=None, acc_shape=None):
    n_red = None if red is None else grid[red]

    def body(*refs):
        a_ref, b_ref = refs[0], refs[1]
        add_ref = refs[2] if add is not None else None
        o_ref = refs[3] if add is not None else refs[2]
        part = lax.dot_general(a_ref[...].astype(BF16), b_ref[...].astype(BF16), dims, preferred_element_type=F32)
        if red is None:
            if add is not None:
                part = part + add_ref[...]
            o_ref[...] = part.astype(o_ref.dtype)
            return
        acc_ref = refs[-1]
        r = pl.program_id(red)

        @pl.when(r == 0)
        def _():
            acc_ref[...] = part

        @pl.when(r > 0)
        def _():
            acc_ref[...] += part

        @pl.when(r == n_red - 1)
        def _():
            o_ref[...] = acc_ref[...].astype(o_ref.dtype)

    sem = tuple("arbitrary" if ax == red else "parallel" for ax in range(len(grid)))
    in_specs = [a_spec, b_spec] + ([o_spec] if add is not None else [])
    args = (a, b) + ((add,) if add is not None else ())
    return _tc_call(
        body, name=name, grid=grid, in_specs=in_specs, out_specs=o_spec, out_shape=o_sds,
        scratch_shapes=[] if red is None else [pltpu.VMEM(acc_shape, F32)], compiler_params=_cp(*sem),
    )(*args)


def _nn(name, a, b, out_dtype, add=None, lead=None):
    (m, k), n = a.shape, b.shape[-1]
    osz = jnp.dtype(out_dtype).itemsize + (4 if add is not None else 0)
    tm, tn = _fit(m, n, lambda tm, tn: tm * k * _size(a) + k * tn * _size(b) + tm * tn * osz)
    if lead is None:
        b_spec = pl.BlockSpec((k, tn), lambda i, j: (0, j))
    else:
        b_spec = pl.BlockSpec((None, k, tn), lambda i, j: (lead, 0, j))
    return _mm(name, a, b, NN_DIMS, (m // tm, n // tn), pl.BlockSpec((tm, k), lambda i, j: (i, 0)), b_spec,
               pl.BlockSpec((tm, tn), lambda i, j: (i, j)), jax.ShapeDtypeStruct((m, n), out_dtype), add=add)


def _nn_parts(name, a, b, parts, out_dtype, lead=None, stacked=False):
    m, k = a.shape
    c = b.shape[-1] if stacked else b.shape[-1] // parts
    osz = jnp.dtype(out_dtype).itemsize
    tm, tn = _fit(m, c, lambda tm, tn: tm * k * _size(a) + k * tn * _size(b) + tm * tn * osz)
    nb = c // tn
    if stacked:
        b_spec = pl.BlockSpec((None, k, tn), lambda i, p, j: (p, 0, j))
    elif lead is None:
        b_spec = pl.BlockSpec((k, tn), lambda i, p, j: (0, p * nb + j))
    else:
        b_spec = pl.BlockSpec((None, k, tn), lambda i, p, j: (lead, 0, p * nb + j))
    return _mm(name, a, b, NN_DIMS, (m // tm, parts, nb), pl.BlockSpec((tm, k), lambda i, p, j: (i, 0)), b_spec,
               pl.BlockSpec((None, tm, tn), lambda i, p, j: (p, i, j)), jax.ShapeDtypeStruct((parts, m, c), out_dtype))


def _nt(name, a, b, out_dtype, lead=None):
    (m, k), n = a.shape, b.shape[-2]
    osz = jnp.dtype(out_dtype).itemsize
    tm, tn = _fit(m, n, lambda tm, tn: tm * k * _size(a) + tn * k * _size(b) + tm * tn * osz)
    if lead is None:
        b_spec = pl.BlockSpec((tn, k), lambda i, j: (j, 0))
    else:
        b_spec = pl.BlockSpec((None, tn, k), lambda i, j: (lead, j, 0))
    return _mm(name, a, b, NT_DIMS, (m // tm, n // tn), pl.BlockSpec((tm, k), lambda i, j: (i, 0)), b_spec,
               pl.BlockSpec((tm, tn), lambda i, j: (i, j)), jax.ShapeDtypeStruct((m, n), out_dtype))


def _tn(name, a, b, out_dtype):
    (k, m), n = a.shape, b.shape[1]
    osz = jnp.dtype(out_dtype).itemsize
    tm, tn = _fit(m, n, lambda tm, tn: k * tm * _size(a) + k * tn * _size(b) + tm * tn * osz,
                  m_tiles=(512, 384, 256, 128), n_tiles=(n,) + N_TILES)
    return _mm(name, a, b, TN_DIMS, (m // tm, n // tn), pl.BlockSpec((k, tm), lambda i, j: (0, i)),
               pl.BlockSpec((k, tn), lambda i, j: (0, j)), pl.BlockSpec((tm, tn), lambda i, j: (i, j)),
               jax.ShapeDtypeStruct((m, n), out_dtype))


def _nn_add_norm(name, a, b, add, g):
    (m, k), n = a.shape, b.shape[1]
    tm = 512

    def body(a_ref, b_ref, add_ref, g_ref, h_ref, hn_ref):
        h = jnp.dot(a_ref[...], b_ref[...], preferred_element_type=F32) + add_ref[...]
        h_ref[...] = h
        hn_ref[...] = _rms_rows(h, g_ref[...]).astype(BF16)

    rows = lambda w: pl.BlockSpec((tm, w), lambda i: (i, 0))
    return _tc_call(
        body, name=name, grid=(m // tm,),
        in_specs=[rows(k), pl.BlockSpec((k, n), lambda i: (0, 0)), rows(n), pl.BlockSpec((1, n), lambda i: (0, 0))],
        out_specs=[rows(n), rows(n)],
        out_shape=[jax.ShapeDtypeStruct((m, n), F32), jax.ShapeDtypeStruct((m, n), BF16)], compiler_params=_cp("parallel"),
    )(a, b, add, g)


def _nn_add_loss(name, a, b, add, g, tgt):
    (m, k), n = a.shape, b.shape[1]
    tm = 512

    def body(a_ref, b_ref, add_ref, g_ref, t_ref, loss_ref, dh_ref, dhb_ref, dg_ref):
        xv = jnp.dot(a_ref[...], b_ref[...], preferred_element_type=F32) + add_ref[...]
        gv = g_ref[...]
        r = lax.rsqrt(jnp.mean(xv * xv, axis=1, keepdims=True) + EPS)
        err = xv * r * gv - t_ref[...]
        part = 0.5 * jnp.sum(jnp.mean(err * err, axis=1, keepdims=True), axis=0, keepdims=True)
        dx, dg = _rms_bwd_math(xv, gv, err * (1.0 / n))
        dh_ref[...] = dx
        dhb_ref[...] = dx.astype(BF16)

        @pl.when(pl.program_id(0) == 0)
        def _():
            dg_ref[...] = jnp.zeros_like(dg_ref)
            loss_ref[...] = jnp.zeros_like(loss_ref)

        dg_ref[...] += dg
        loss_ref[...] += jnp.broadcast_to(part, loss_ref.shape)

    rows = lambda w: pl.BlockSpec((tm, w), lambda i: (i, 0))
    vec = pl.BlockSpec((1, n), lambda i: (0, 0))
    return _tc_call(
        body, name=name, grid=(m // tm,),
        in_specs=[rows(k), pl.BlockSpec((k, n), lambda i: (0, 0)), rows(n), vec, rows(n)],
        out_specs=[pl.BlockSpec((1, LANES), lambda i: (0, 0)), rows(n), rows(n), vec],
        out_shape=[jax.ShapeDtypeStruct((1, LANES), F32), jax.ShapeDtypeStruct((m, n), F32),
                   jax.ShapeDtypeStruct((m, n), BF16), jax.ShapeDtypeStruct((1, n), F32)],
        compiler_params=_cp("arbitrary"),
    )(a, b, add, g, tgt)


def _dx_norm_bwd(name, a, b, x, g, add):
    parts, t, c = a.shape
    d = b.shape[0]
    tm = 256

    def body(a_ref, b_ref, x_ref, g_ref, add_ref, dx_ref, dxb_ref, dg_ref):
        dy = None
        for p in range(parts):
            part = lax.dot_general(a_ref[p], b_ref[:, p * c:(p + 1) * c], NT_DIMS, preferred_element_type=F32)
            dy = part if dy is None else dy + part
        dx, dg = _rms_bwd_math(x_ref[...], g_ref[...], dy)
        dx = dx + add_ref[...]
        dx_ref[...] = dx
        dxb_ref[...] = dx.astype(BF16)

        @pl.when(pl.program_id(0) == 0)
        def _():
            dg_ref[...] = jnp.zeros_like(dg_ref)

        dg_ref[...] += dg

    rows = pl.BlockSpec((tm, d), lambda i: (i, 0))
    vec = pl.BlockSpec((1, d), lambda i: (0, 0))
    return _tc_call(
        body, name=name, grid=(t // tm,),
        in_specs=[pl.BlockSpec((parts, tm, c), lambda i: (0, i, 0)), pl.BlockSpec(b.shape, lambda i: (0, 0)), rows, vec,
                  rows],
        out_specs=[rows, rows, vec],
        out_shape=[jax.ShapeDtypeStruct((t, d), F32), jax.ShapeDtypeStruct((t, d), BF16),
                   jax.ShapeDtypeStruct((1, d), F32)],
        compiler_params=_cp("arbitrary"),
    )(a, b, x, g, add)


def _dw_sc_in(hn, dz):
    t, tn, tm = hn.shape[0], TC, D
    per_part, per_chip = D // tn, 3 * D // N_CHIPS // tn
    return _mm("sc_in_dw", hn, dz, TN_DIMS, (D // tm, 3 * D // tn), pl.BlockSpec((t, tm), lambda i, j: (0, i)),
               pl.BlockSpec((None, t, tn), lambda i, j: (j // per_part, 0, j % per_part)),
               pl.BlockSpec((None, tm, tn), lambda i, j: (j // per_chip, i, j % per_chip)),
               jax.ShapeDtypeStruct((N_CHIPS, D, 3 * D // N_CHIPS), BF16))


def _dw_ffn_up(name, hf, dup):
    t, tm, ns = hf.shape[0], D, 2 * F_FF // N_CHIPS
    return _mm(name, hf, dup, TN_DIMS, (N_CHIPS, D // tm), pl.BlockSpec((t, tm), lambda s, i: (0, i)),
               pl.BlockSpec((None, t, ns), lambda s, i: (s // 2, 0, s % 2)),
               pl.BlockSpec((None, tm, ns), lambda s, i: (s, i, 0)), jax.ShapeDtypeStruct((N_CHIPS, D, ns), BF16))


def _rms_fwd(x, g, name):
    t, d = x.shape
    tr = 512

    def body(x_ref, g_ref, o_ref):
        xv = x_ref[...]
        r = lax.rsqrt(jnp.mean(xv * xv, axis=1, keepdims=True) + EPS)
        o_ref[...] = (xv * r * g_ref[...]).astype(o_ref.dtype)

    row = pl.BlockSpec((tr, d), lambda i: (i, 0))
    return _tc_call(
        body, name=name, grid=(t // tr,), in_specs=[row, pl.BlockSpec((1, d), lambda i: (0, 0))],
        out_specs=row, out_shape=jax.ShapeDtypeStruct((t, d), BF16), compiler_params=_cp("parallel"),
    )(x, g)


def _rms_bwd_math(xv, g, dy):
    r = lax.rsqrt(jnp.mean(xv * xv, axis=1, keepdims=True) + EPS)
    xh = xv * r
    gy = dy * g
    dx = r * (gy - xh * jnp.mean(gy * xh, axis=1, keepdims=True))
    dg = jnp.sum(dy * xh, axis=0, keepdims=True)
    return dx, dg


def _rot_half(x):
    lane = lax.broadcasted_iota(jnp.int32, x.shape, 1)
    return jnp.where((lane % QK_ROPE) < QK_ROPE // 2, -pltpu.roll(x, LANES - 32, axis=1),
                     pltpu.roll(x, 32, axis=1))


def _rope_fwd_math(x, cos, sin):
    return x * cos + _rot_half(x) * sin


def _rope_bwd_math(dy, cos, sin):
    return dy * cos - _rot_half(dy * sin)


def _rms_rows(x, g):
    return x * lax.rsqrt(jnp.mean(x * x, axis=1, keepdims=True) + EPS) * g


def _attn_prep(h, g_attn, g_kvin, w_dq, g_ql, w_uq, w_kv, g_kvl, w_ukv, cos, sin):
    t, d = h.shape
    tr = 256
    wq = N_HEADS * HEAD_PAD

    def body(h_ref, ga_ref, gk_ref, wdq_ref, gq_ref, wuq_ref, wkv_ref, gl_ref, wukv_ref, c_ref, s_ref,
             hn_ref, hk_ref, cqp_ref, cq_ref, q_ref, kvp_ref, ckv_ref, kr_ref, knv_ref):
        xv, cv, sv = h_ref[...], c_ref[...], s_ref[...]
        xh = xv * lax.rsqrt(jnp.mean(xv * xv, axis=1, keepdims=True) + EPS)
        hn = (xh * ga_ref[...]).astype(BF16)
        hk = (xh * gk_ref[...]).astype(BF16)
        hn_ref[...], hk_ref[...] = hn, hk
        cq_pre = jnp.dot(hn, wdq_ref[...], preferred_element_type=F32)
        cqp_ref[...] = cq_pre
        cq = _rms_rows(cq_pre, gq_ref[...]).astype(BF16)
        cq_ref[...] = cq
        for hd in range(N_HEADS):
            lo = hd * HEAD_PAD
            qh = jnp.dot(cq, wuq_ref[:, lo:lo + HEAD_PAD], preferred_element_type=F32)
            q_ref[:, lo:lo + QK_NOPE] = qh[:, :QK_NOPE].astype(BF16)
            q_ref[:, lo + QK_NOPE:lo + HEAD_PAD] = _rope_fwd_math(qh[:, QK_NOPE:], cv, sv).astype(BF16)
        kvpre = jnp.dot(hk, wkv_ref[...], preferred_element_type=F32)
        kvp_ref[...] = kvpre
        ckv = _rms_rows(kvpre[:, :KV_LORA], gl_ref[...]).astype(BF16)
        ckv_ref[...] = ckv
        kr_ref[...] = _rope_fwd_math(kvpre[:, KV_LORA:], cv, sv).astype(BF16)
        for p in range(2):
            knv_ref[p] = jnp.dot(ckv, wukv_ref[p], preferred_element_type=F32).astype(BF16)

    rows = lambda w: pl.BlockSpec((tr, w), lambda i: (i, 0))
    whole = lambda a: pl.BlockSpec(a.shape, lambda i: (0,) * a.ndim)
    sds = lambda w, dt: jax.ShapeDtypeStruct((t, w), dt)
    args = (h, g_attn, g_kvin, w_dq, g_ql, w_uq, w_kv, g_kvl, w_ukv, cos, sin)
    return _tc_call(
        body, name="attn_prep", grid=(t // tr,),
        in_specs=[rows(d)] + [whole(a) for a in args[1:9]] + [rows(LANES), rows(LANES)],
        out_specs=[rows(d), rows(d), rows(Q_LORA), rows(Q_LORA), rows(wq), rows(KVP), rows(KV_LORA), rows(LANES),
                   pl.BlockSpec((2, tr, N_HEADS * QK_NOPE), lambda i: (0, i, 0))],
        out_shape=[sds(d, BF16), sds(d, BF16), sds(Q_LORA, F32), sds(Q_LORA, BF16), sds(wq, BF16), sds(KVP, F32),
                   sds(KV_LORA, BF16), sds(LANES, BF16), jax.ShapeDtypeStruct((2, t, N_HEADS * QK_NOPE), BF16)],
        compiler_params=_cp("parallel"),
    )(*args)


def _attn_prep_bwd(dq, dknv, dkr, dh, h, hn, hk, cq_pre, cq, kvpre, ckv, g_attn, g_kvin, w_dq, g_ql, w_uq, w_kv, g_kvl,
                   w_ukv, cos, sin):
    t, d = h.shape
    tr = 256
    n_steps = t // tr
    wq = N_HEADS * HEAD_PAD
    wk = N_HEADS * QK_NOPE

    def body(dq_ref, dknv_ref, dkr_ref, dh_ref, h_ref, hn_ref, hk_ref, cqp_ref, cq_ref, kvp_ref, ckv_ref,
             ga_ref, gk_ref, wdq_ref, gq_ref, wuq_ref, wkv_ref, gl_ref, wukv_ref, c_ref, s_ref,
             dho_ref, dhb_ref, dwuq_ref, dwdq_ref, dwukv_ref, dwkv_ref, dga_ref, dgk_ref, dgq_ref, dgl_ref,
             a_uq, a_dq, a_ukv, a_kv):
        i = pl.program_id(0)

        @pl.when(i == 0)
        def _():
            for ref in (a_uq, a_dq, a_ukv, a_kv, dga_ref, dgk_ref, dgq_ref, dgl_ref):
                ref[...] = jnp.zeros_like(ref)

        dqv = dq_ref[...]
        dcq = lax.dot_general(dqv, wuq_ref[...], NT_DIMS, preferred_element_type=F32)
        a_uq[...] += lax.dot_general(cq_ref[...], dqv, TN_DIMS, preferred_element_type=F32)
        dcq_pre, dg = _rms_bwd_math(cqp_ref[...], gq_ref[...], dcq)
        dgq_ref[...] += dg
        dcq_pre = dcq_pre.astype(BF16)
        dhn = lax.dot_general(dcq_pre, wdq_ref[...], NT_DIMS, preferred_element_type=F32)
        a_dq[...] += lax.dot_general(hn_ref[...], dcq_pre, TN_DIMS, preferred_element_type=F32)
        dckv = None
        for p in range(2):
            dk = dknv_ref[p].astype(BF16)
            part = lax.dot_general(dk, wukv_ref[p], NT_DIMS, preferred_element_type=F32)
            dckv = part if dckv is None else dckv + part
            a_ukv[p] += lax.dot_general(ckv_ref[...], dk, TN_DIMS, preferred_element_type=F32)
        dlat, dg = _rms_bwd_math(kvp_ref[:, :KV_LORA], gl_ref[...], dckv)
        dgl_ref[...] += dg
        dkr_pre = _rope_bwd_math(dkr_ref[...], c_ref[...], s_ref[...])
        dkvpre = jnp.concatenate([dlat, dkr_pre], axis=1).astype(BF16)
        dhk = lax.dot_general(dkvpre, wkv_ref[...], NT_DIMS, preferred_element_type=F32)
        a_kv[...] += lax.dot_general(hk_ref[...], dkvpre, TN_DIMS, preferred_element_type=F32)
        xv = h_ref[...]
        dx1, dg = _rms_bwd_math(xv, ga_ref[...], dhn)
        dga_ref[...] += dg
        dx2, dg = _rms_bwd_math(xv, gk_ref[...], dhk)
        dgk_ref[...] += dg
        dh_new = dh_ref[...] + dx1 + dx2
        dho_ref[...] = dh_new
        dhb_ref[...] = dh_new.astype(BF16)

        @pl.when(i == n_steps - 1)
        def _():
            dwuq_ref[...] = a_uq[...].astype(BF16)
            dwdq_ref[...] = a_dq[...].astype(BF16)
            dwukv_ref[...] = a_ukv[...].astype(BF16)
            dwkv_ref[...] = a_kv[...].astype(BF16)

    rows = lambda w: pl.BlockSpec((tr, w), lambda i: (i, 0))
    whole = lambda shape: pl.BlockSpec(shape, lambda i: (0,) * len(shape))
    weights = (g_attn, g_kvin, w_dq, g_ql, w_uq, w_kv, g_kvl, w_ukv)
    dw_shapes = [(Q_LORA, wq), (d, Q_LORA), (2, KV_LORA, wk), (d, KVP)]
    dg_shapes = [(1, d), (1, d), (1, Q_LORA), (1, KV_LORA)]
    return _tc_call(
        body, name="attn_prep_bwd", grid=(n_steps,),
        in_specs=[rows(wq), pl.BlockSpec((2, tr, wk), lambda i: (0, i, 0)), rows(LANES), rows(d), rows(d), rows(d),
                  rows(d), rows(Q_LORA), rows(Q_LORA), rows(KVP), rows(KV_LORA)]
        + [whole(a.shape) for a in weights] + [rows(LANES), rows(LANES)],
        out_specs=[rows(d), rows(d)] + [whole(s) for s in dw_shapes + dg_shapes],
        out_shape=[jax.ShapeDtypeStruct((t, d), F32), jax.ShapeDtypeStruct((t, d), BF16)]
        + [jax.ShapeDtypeStruct(s, BF16) for s in dw_shapes] + [jax.ShapeDtypeStruct(s, F32) for s in dg_shapes],
        scratch_shapes=[pltpu.VMEM(s, F32) for s in dw_shapes], compiler_params=_cp("arbitrary"),
    )(dq, dknv, dkr, dh, h, hn, hk, cq_pre, cq, kvpre, ckv, *weights, cos, sin)


ROW_CHUNK = 64
HALO = 16
WIN = ROW_CHUNK + 16
LANE_HALVES = (slice(0, LANES), slice(LANES, TC))


def _stage(s_ref, p, src):
    t = src.shape[0]
    s_ref[p, :HALO] = jnp.zeros((HALO, TC), BF16)
    s_ref[p, HALO:HALO + t] = src
    s_ref[p, HALO + t:] = jnp.zeros((HALO, TC), BF16)


def _window(s_ref, p, i, lanes):
    base = pl.multiple_of(i * ROW_CHUNK, ROW_CHUNK)
    return s_ref[p, pl.ds(base, ROW_CHUNK + 2 * HALO), lanes].astype(F32)[8:8 + WIN]


def _valid(x):
    return x[8:8 + ROW_CHUNK]


def _prev(x, k):
    return pltpu.roll(x, k, axis=0)


def _next(x, k):
    return pltpu.roll(x, WIN - k, axis=0)


def _taps(w_ref, lanes):
    return w_ref[0:1, lanes], w_ref[1:2, lanes], w_ref[2:3, lanes]


def _fold8(x):
    return jnp.sum(x.reshape(ROW_CHUNK // 8, 8, x.shape[-1]), axis=0)


def _store_rows(ref, idx, i, lanes, x):
    rows = pl.ds(pl.multiple_of(i * ROW_CHUNK, ROW_CHUNK), ROW_CHUNK)
    ref[(*idx, rows, lanes)] = x.astype(ref.dtype)


def _for_chunks(t, chunk):
    def step(i, carry):
        for lanes in LANE_HALVES:
            chunk(i, lanes)
        return carry

    lax.fori_loop(0, t // ROW_CHUNK, step, 0)


def _write_col_sums(acc_ref, outs):
    for k, (ref, row) in enumerate(outs):
        ref[row:row + 1, :] = jnp.sum(acc_ref[k], axis=0, keepdims=True)


def _shift_down(x, k):
    row = lax.broadcasted_iota(jnp.int32, x.shape, 0)
    return jnp.where(row >= k, pltpu.roll(x, k, axis=0), 0.0)


def _shift_up(x, k):
    n = x.shape[0]
    row = lax.broadcasted_iota(jnp.int32, x.shape, 0)
    return jnp.where(row < n - k, pltpu.roll(x, n - k, axis=0), 0.0)


def _conv3(x, w_ref):
    return _shift_down(x, 2) * w_ref[0:1, :] + _shift_down(x, 1) * w_ref[1:2, :] + x * w_ref[2:3, :]


def _col(parts, t):
    if parts is None:
        return pl.BlockSpec((t, TC), lambda j: (0, j))
    return pl.BlockSpec((parts, t, TC), lambda j: (0, 0, j))


def _staging(parts, t):
    return pltpu.VMEM((parts, t + 2 * HALO, TC), BF16)


def _scmix_fwd(z, w):
    t = z.shape[1]

    def body(z_ref, w_ref, m_ref):
        b, c, u = (z_ref[p].astype(F32) for p in range(3))
        m_ref[...] = (b * _conv3(c * u, w_ref)).astype(BF16)

    return _tc_call(
        body, name="scmix_fwd", grid=(D // TC,), in_specs=[_col(3, t), pl.BlockSpec((3, TC), lambda j: (0, j))],
        out_specs=_col(None, t), out_shape=jax.ShapeDtypeStruct((t, D), BF16), compiler_params=_cp("parallel"),
    )(z, w)


def _scmix_bwd(z, w, dm):
    t = z.shape[1]

    def body(z_ref, w_ref, dm_ref, dz_ref, dw_ref, s_ref, acc_ref):
        for p in range(3):
            _stage(s_ref, p, z_ref[p])
        _stage(s_ref, 3, dm_ref[...])
        acc_ref[...] = jnp.zeros_like(acc_ref)

        def chunk(i, lanes):
            w0, w1, w2 = _taps(w_ref, lanes)
            b, c, u, dm = (_window(s_ref, p, i, lanes) for p in range(4))
            cu = c * u
            cu1, cu2 = _prev(cu, 1), _prev(cu, 2)
            _store_rows(dz_ref, (0,), i, lanes, _valid(dm * (cu2 * w0 + cu1 * w1 + cu * w2)))
            dcv = dm * b
            dcu = dcv * w2 + _next(dcv, 1) * w1 + _next(dcv, 2) * w0
            _store_rows(dz_ref, (1,), i, lanes, _valid(dcu * u))
            _store_rows(dz_ref, (2,), i, lanes, _valid(dcu * c))
            for k, shifted in enumerate((cu2, cu1, cu)):
                acc_ref[k, :, lanes] += _fold8(_valid(dcv * shifted))

        _for_chunks(t, chunk)
        _write_col_sums(acc_ref, [(dw_ref, 0), (dw_ref, 1), (dw_ref, 2)])

    wspec = pl.BlockSpec((3, TC), lambda j: (0, j))
    return _tc_call(
        body, name="scmix_bwd", grid=(D // TC,), in_specs=[_col(3, t), wspec, _col(None, t)],
        out_specs=[_col(3, t), wspec],
        out_shape=[jax.ShapeDtypeStruct((3, t, D), BF16), jax.ShapeDtypeStruct((3, D), F32)],
        scratch_shapes=[_staging(4, t), pltpu.VMEM((3, 8, TC), F32)], compiler_params=_cp("parallel"),
    )(z, w, dm)


def _ffn_up_gate(hf, w_up, w, bias, name):
    t, d = hf.shape
    nb = F_FF // TC

    def body(hf_ref, wg_ref, wv_ref, w_ref, b_ref, up_ref, a_ref, prev_ref):
        @pl.when(pl.program_id(0) == 0)
        def _():
            prev_ref[...] = jnp.zeros_like(prev_ref)

        gc = _conv3(prev_ref[0].astype(F32), w_ref) + b_ref[...]
        a_ref[...] = (gc * jax.nn.sigmoid(gc) * prev_ref[1].astype(F32)).astype(BF16)
        hv = hf_ref[...]
        up_ref[0] = jnp.dot(hv, wg_ref[...], preferred_element_type=F32).astype(BF16)
        up_ref[1] = jnp.dot(hv, wv_ref[...], preferred_element_type=F32).astype(BF16)
        prev_ref[...] = up_ref[...]

    tile = lambda j: jnp.minimum(j, nb - 1)
    gated = lambda j: jnp.maximum(j - 1, 0)
    return _tc_call(
        body, name=name, grid=(nb + 1,),
        in_specs=[pl.BlockSpec((t, d), lambda j: (0, 0)), pl.BlockSpec((d, TC), lambda j: (0, tile(j))),
                  pl.BlockSpec((d, TC), lambda j: (0, nb + tile(j))), pl.BlockSpec((3, TC), lambda j: (0, gated(j))),
                  pl.BlockSpec((1, TC), lambda j: (0, gated(j)))],
        out_specs=[pl.BlockSpec((2, t, TC), lambda j: (0, 0, tile(j))), pl.BlockSpec((t, TC), lambda j: (0, gated(j)))],
        out_shape=[jax.ShapeDtypeStruct((2, t, F_FF), BF16), jax.ShapeDtypeStruct((t, F_FF), BF16)],
        scratch_shapes=[pltpu.VMEM((2, t, TC), BF16)], compiler_params=_cp("arbitrary"),
    )(hf, w_up, w_up, w, bias)


def _gate_bwd(up, w, bias, dh, w_down, name):
    t, d = dh.shape

    def body(u_ref, w_ref, b_ref, dh_ref, wd_ref, du_ref, dw_ref, db_ref, s_ref, acc_ref):
        for p in range(2):
            _stage(s_ref, p, u_ref[p])
        _stage(s_ref, 2, lax.dot_general(dh_ref[...], wd_ref[...], NT_DIMS, preferred_element_type=F32).astype(BF16))
        acc_ref[...] = jnp.zeros_like(acc_ref)

        def chunk(i, lanes):
            w0, w1, w2 = _taps(w_ref, lanes)
            g, v, da = (_window(s_ref, p, i, lanes) for p in range(3))
            g1, g2 = _prev(g, 1), _prev(g, 2)
            gc = g2 * w0 + g1 * w1 + g * w2 + b_ref[:, lanes]
            sg = jax.nn.sigmoid(gc)
            _store_rows(du_ref, (1,), i, lanes, _valid(da * (gc * sg)))
            dgc = da * v * (sg * (1.0 + gc * (1.0 - sg)))
            _store_rows(du_ref, (0,), i, lanes, _valid(dgc * w2 + _next(dgc, 1) * w1 + _next(dgc, 2) * w0))
            for k, shifted in enumerate((g2, g1, g)):
                acc_ref[k, :, lanes] += _fold8(_valid(dgc * shifted))
            acc_ref[3, :, lanes] += _fold8(_valid(dgc))

        _for_chunks(t, chunk)
        _write_col_sums(acc_ref, [(dw_ref, 0), (dw_ref, 1), (dw_ref, 2), (db_ref, 0)])

    wspec = pl.BlockSpec((3, TC), lambda j: (0, j))
    bspec = pl.BlockSpec((1, TC), lambda j: (0, j))
    return _tc_call(
        body, name=name, grid=(F_FF // TC,),
        in_specs=[_col(2, t), wspec, bspec, pl.BlockSpec((t, d), lambda j: (0, 0)), pl.BlockSpec((TC, d), lambda j: (j, 0))],
        out_specs=[_col(2, t), wspec, bspec],
        out_shape=[jax.ShapeDtypeStruct((2, t, F_FF), BF16), jax.ShapeDtypeStruct((3, F_FF), F32),
                   jax.ShapeDtypeStruct((1, F_FF), F32)],
        scratch_shapes=[_staging(3, t), pltpu.VMEM((4, 8, TC), F32)], compiler_params=_cp("parallel"),
    )(up, w, bias, dh, w_down)


ATT_TQ = 256
ATT_SCALE = (QK_NOPE + QK_ROPE) ** -0.5


def _key_ranges(lvl):
    lo = lvl * ATT_TQ
    return ([(0, lo, False)] if lvl else []) + [(lo, lo + ATT_TQ, True)]


FWD_HEADS = 4
BWD_HEADS = 2


def _fill_keys(k_ref, kn_ref, kr_ref):
    @pl.when(pl.program_id(1) == 0)
    def _():
        for hh in range(k_ref.shape[0]):
            k_ref[hh, :, :QK_NOPE] = kn_ref[:, hh * QK_NOPE:(hh + 1) * QK_NOPE]
            k_ref[hh, :, QK_NOPE:] = kr_ref[...]


def _attn_probs(q, k_ref, lvl):
    scores = []
    for lo, hi, diagonal in _key_ranges(lvl):
        s = lax.dot_general(q, k_ref[lo:hi, :], NT_DIMS, preferred_element_type=F32) * ATT_SCALE
        if diagonal:
            row = lax.broadcasted_iota(jnp.int32, s.shape, 0)
            col = lax.broadcasted_iota(jnp.int32, s.shape, 1)
            seen = lax.shift_right_logical(col, CHUNK_SHIFT) <= lax.shift_right_logical(row, CHUNK_SHIFT)
            s = jnp.where(seen, s, NEG_INF)
        scores.append(s)
    m = jnp.max(scores[0], axis=1, keepdims=True)
    for s in scores[1:]:
        m = jnp.maximum(m, jnp.max(s, axis=1, keepdims=True))
    ps = [jnp.exp(s - m) for s in scores]
    total = jnp.sum(ps[0], axis=1, keepdims=True)
    for p in ps[1:]:
        total = total + jnp.sum(p, axis=1, keepdims=True)
    inv = 1.0 / total
    return [p * inv for p in ps]


def _attn_probs_t(q, k_ref, lvl):
    scores = []
    for lo, hi, diagonal in _key_ranges(lvl):
        s = lax.dot_general(k_ref[lo:hi, :], q, NT_DIMS, preferred_element_type=F32) * ATT_SCALE
        if diagonal:
            key = lax.broadcasted_iota(jnp.int32, s.shape, 0)
            qry = lax.broadcasted_iota(jnp.int32, s.shape, 1)
            seen = lax.shift_right_logical(key, CHUNK_SHIFT) <= lax.shift_right_logical(qry, CHUNK_SHIFT)
            s = jnp.where(seen, s, NEG_INF)
        scores.append(s)
    m = jnp.max(scores[0], axis=0, keepdims=True)
    for s in scores[1:]:
        m = jnp.maximum(m, jnp.max(s, axis=0, keepdims=True))
    ps = [jnp.exp(s - m) for s in scores]
    total = jnp.sum(ps[0], axis=0, keepdims=True)
    for p in ps[1:]:
        total = total + jnp.sum(p, axis=0, keepdims=True)
    inv = 1.0 / total
    return [p * inv for p in ps]


def _per_query_block(qi, n_blocks, branch):
    for lvl in range(n_blocks):
        pl.when(qi == lvl)(lambda lvl=lvl: branch(lvl))


def _attn_specs(t, g):
    q = pl.BlockSpec((ATT_TQ, g * HEAD_PAD), lambda h, i: (i, h))
    kn = pl.BlockSpec((None, t, g * QK_NOPE), lambda h, i: (0, 0, h))
    kr = pl.BlockSpec((t, LANES), lambda h, i: (0, 0))
    v = pl.BlockSpec((None, t, g * V_HEAD), lambda h, i: (1, 0, h))
    o = pl.BlockSpec((ATT_TQ, g * V_HEAD), lambda h, i: (i, h))
    return q, kn, kr, v, o


def _attn_fwd(q, knv, kr):
    t = q.shape[0]

    def body(q_ref, kn_ref, kr_ref, v_ref, o_ref, k_ref):
        _fill_keys(k_ref, kn_ref, kr_ref)

        def branch(lvl):
            for hh in range(FWD_HEADS):
                vcols = slice(hh * V_HEAD, (hh + 1) * V_HEAD)
                ps = _attn_probs(q_ref[:, hh * HEAD_PAD:(hh + 1) * HEAD_PAD], k_ref.at[hh], lvl)
                o = None
                for p, (lo, hi, _) in zip(ps, _key_ranges(lvl)):
                    part = jnp.dot(p.astype(BF16), v_ref[lo:hi, vcols], preferred_element_type=F32)
                    o = part if o is None else o + part
                o_ref[:, vcols] = o.astype(BF16)

        _per_query_block(pl.program_id(1), t // ATT_TQ, branch)

    qs, kns, krs, vs, os_ = _attn_specs(t, FWD_HEADS)
    return _tc_call(
        body, name="attn_fwd", grid=(N_HEADS // FWD_HEADS, t // ATT_TQ), in_specs=[qs, kns, krs, vs],
        out_specs=os_, out_shape=jax.ShapeDtypeStruct((t, N_HEADS * V_HEAD), BF16),
        scratch_shapes=[pltpu.VMEM((FWD_HEADS, t, HEAD_PAD), BF16)], compiler_params=_cp("parallel", "arbitrary"),
    )(q, knv, kr, knv)


def _attn_bwd(q, knv, kr, do, cos, sin):
    t = q.shape[0]

    def body(q_ref, kn_ref, kr_ref, v_ref, do_ref, c_ref, s_ref, dq_ref, dknv_ref, dkr_ref, k_ref, dk_ref):
        h, qi = pl.program_id(0), pl.program_id(1)
        _fill_keys(k_ref, kn_ref, kr_ref)

        @pl.when(qi == 0)
        def _():
            dknv_ref[1] = jnp.zeros(dknv_ref.shape[1:], F32)
            dk_ref[...] = jnp.zeros_like(dk_ref)

        @pl.when((qi == 0) & (h == 0))
        def _():
            dkr_ref[...] = jnp.zeros_like(dkr_ref)

        def branch(lvl):
            ranges = _key_ranges(lvl)
            for hh in range(BWD_HEADS):
                qcols = slice(hh * HEAD_PAD, (hh + 1) * HEAD_PAD)
                vcols = slice(hh * V_HEAD, (hh + 1) * V_HEAD)
                qv, dov = q_ref[:, qcols], do_ref[:, vcols]
                ps = _attn_probs_t(qv, k_ref.at[hh], lvl)
                dps = [lax.dot_general(v_ref[lo:hi, vcols], dov, NT_DIMS, preferred_element_type=F32)
                       for lo, hi, _ in ranges]
                di = None
                for p, dp in zip(ps, dps):
                    part = jnp.sum(p * dp, axis=0, keepdims=True)
                    di = part if di is None else di + part
                dq = None
                for p, dp, (lo, hi, _) in zip(ps, dps, ranges):
                    ds = (p * (dp - di) * ATT_SCALE).astype(BF16)
                    part = lax.dot_general(ds, k_ref[hh, lo:hi, :], TN_DIMS, preferred_element_type=F32)
                    dq = part if dq is None else dq + part
                    dk_ref[hh, lo:hi, :] += jnp.dot(ds, qv, preferred_element_type=F32)
                    dknv_ref[1, lo:hi, vcols] += jnp.dot(p.astype(BF16), dov, preferred_element_type=F32)
                dq_ref[:, hh * HEAD_PAD:hh * HEAD_PAD + QK_NOPE] = dq[:, :QK_NOPE].astype(BF16)
                dq_ref[:, hh * HEAD_PAD + QK_NOPE:(hh + 1) * HEAD_PAD] = _rope_bwd_math(
                    dq[:, QK_NOPE:], c_ref[...], s_ref[...]).astype(BF16)

        _per_query_block(qi, t // ATT_TQ, branch)

        @pl.when(qi == t // ATT_TQ - 1)
        def _():
            for hh in range(BWD_HEADS):
                dknv_ref[0, :, hh * QK_NOPE:(hh + 1) * QK_NOPE] = dk_ref[hh, :, :QK_NOPE]
                dkr_ref[...] += dk_ref[hh, :, QK_NOPE:]

    qs, kns, krs, vs, os_ = _attn_specs(t, BWD_HEADS)
    tab = pl.BlockSpec((ATT_TQ, LANES), lambda h, i: (i, 0))
    return _tc_call(
        body, name="attn_bwd", grid=(N_HEADS // BWD_HEADS, t // ATT_TQ), in_specs=[qs, kns, krs, vs, os_, tab, tab],
        out_specs=[qs, pl.BlockSpec((2, t, BWD_HEADS * QK_NOPE), lambda h, i: (0, 0, h)), krs],
        out_shape=[jax.ShapeDtypeStruct((t, N_HEADS * HEAD_PAD), BF16),
                   jax.ShapeDtypeStruct((2, t, N_HEADS * QK_NOPE), F32), jax.ShapeDtypeStruct((t, LANES), F32)],
        scratch_shapes=[pltpu.VMEM((BWD_HEADS, t, HEAD_PAD), BF16), pltpu.VMEM((BWD_HEADS, t, HEAD_PAD), F32)],
        compiler_params=_cp("arbitrary", "arbitrary"),
    )(q, knv, kr, knv, do, cos, sin)


def _adam_math(w, g, m, v):
    nm = ADAM_B1 * m + (1.0 - ADAM_B1) * g
    nv = ADAM_B2 * v + (1.0 - ADAM_B2) * (g * g)
    m_hat = nm / (1.0 - ADAM_B1 ** ADAM_STEP)
    v_hat = nv / (1.0 - ADAM_B2 ** ADAM_STEP)
    return -ADAM_LR * (m_hat / (jnp.sqrt(v_hat) + ADAM_EPS) + ADAM_WD * w), nm, nv


def _adamw_small(w, g, m, v):
    def body(w_ref, g_ref, m_ref, v_ref, d_ref, nm_ref, nv_ref):
        d_ref[...], nm_ref[...], nv_ref[...] = _adam_math(w_ref[...], g_ref[...], m_ref[...], v_ref[...])

    shp = jax.ShapeDtypeStruct(w.shape, F32)
    return _tc_call(body, name="adamw_small", out_shape=[shp] * 3)(w, g, m, v)


ADAM_SPLIT = 4


def _adamw_shards(ids, items, name):
    n = len(items)

    def body(ids_ref, *refs):
        outs = refs[len(refs) - 4 * n:]
        for i, it in enumerate(items):
            w_ref, m_ref, v_ref, gm_ref, gs_ref = refs[5 * i:5 * i + 5]
            g_ref, d_ref, nm_ref, nv_ref = outs[4 * i:4 * i + 4]
            cols = slice(*it["gcols"]) if it.get("gcols") else slice(None)
            whose = pl.program_id(0) if it.get("owner") is None else it["owner"]
            mine = whose == ids_ref[0]

            @pl.when(mine)
            def _(g_ref=g_ref, gm_ref=gm_ref, cols=cols):
                g_ref[...] = gm_ref[:, cols]

            @pl.when(jnp.logical_not(mine))
            def _(g_ref=g_ref, gs_ref=gs_ref, cols=cols):
                g_ref[...] = gs_ref[:, cols]

            d_ref[...], nm_ref[...], nv_ref[...] = _adam_math(w_ref[...], g_ref[...], m_ref[...], v_ref[...])

    in_specs, out_specs, out_shape, args, carried, aliases = [], [], [], [ids], [], {}
    for i, it in enumerate(items):
        w = it["w"]
        r, c = w.shape[-2:]
        tr = r // 2 // ADAM_SPLIT
        assert tr % 8 == 0, (name, w.shape)
        layer = it.get("layer")
        if layer is None:
            wspec = pl.BlockSpec((tr, c), lambda h, k, ids: (h * ADAM_SPLIT + k, 0))
        else:
            wspec = pl.BlockSpec((None, tr, c), lambda h, k, ids, layer=layer: (layer, h * ADAM_SPLIT + k, 0))
        gc = it["g_mine"].shape[1]
        if it.get("owner") is None:
            gspec = pl.BlockSpec((tr, gc), lambda h, k, ids: (k, 0))
        else:
            gspec = pl.BlockSpec((tr, gc), lambda h, k, ids: (h * ADAM_SPLIT + k, 0))
        in_specs += [wspec] * 3 + [gspec] * 2
        args += [w, it["m"], it["v"], it["g_mine"], it["g_sib"]]
        out_specs += [wspec] * 4
        out_shape += [jax.ShapeDtypeStruct(w.shape, F32)] * 4
        if it.get("prev") is not None:
            for k, p in enumerate(it["prev"]):
                aliases[1 + 5 * n + len(carried)] = 4 * i + k
                carried.append(p)
    res = _tc_call(
        body, name=name, prefetch=1, grid=(2, ADAM_SPLIT), in_specs=in_specs + [ANY] * len(carried),
        out_specs=out_specs, out_shape=out_shape, input_output_aliases=aliases,
        compiler_params=_cp("parallel", "parallel"),
    )(*args, *carried)
    return [res[4 * i:4 * i + 4] for i in range(n)]


def _peer_chip(k_me, j):
    return k_me ^ jnp.where(j == 0, 2, jnp.where(j == 1, 1, 3))


def _pair_sums(ids, gs, ras, name):
    n = len(gs)

    def body(ids_ref, *refs):
        for i in range(n):
            g_ref, ra_ref, o_ref = refs[2 * i], refs[2 * i + 1], refs[2 * n + i]
            o_ref[...] = (g_ref[...].astype(F32) + ra_ref[...].astype(F32)).astype(BF16)

    in_specs, out_specs, out_shape = [], [], []
    for g in gs:
        half, c = g.shape[1] // 2, g.shape[2]
        in_specs += [pl.BlockSpec((None, half, c), lambda j, ids: (_peer_chip(ids[1], j), ids[0], 0)),
                     pl.BlockSpec((None, half, c), lambda j, ids: (_peer_chip(ids[1], j), 0, 0))]
        out_specs.append(pl.BlockSpec((None, half, c), lambda j, ids: (j, 0, 0)))
        out_shape.append(jax.ShapeDtypeStruct((3, half, c), BF16))
    return _tc_call(
        body, name=name, prefetch=1, grid=(3,), in_specs=in_specs, out_specs=out_specs, out_shape=out_shape,
        compiler_params=_cp("parallel"),
    )(ids, *[a for pair in zip(gs, ras) for a in pair])


def _chip_sums(ids, gs, ras, rbs, name):
    n = len(gs)

    def body(ids_ref, *refs):
        for i in range(n):
            g_ref, ra_ref, rb_ref, o_ref = refs[3 * i], refs[3 * i + 1], refs[3 * i + 2], refs[3 * n + i]
            acc = g_ref[...].astype(F32) + ra_ref[...].astype(F32)
            for j in range(3):
                acc = acc + rb_ref[j].astype(F32)
            o_ref[...] = acc

    in_specs, out_specs, out_shape = [], [], []
    for g in gs:
        half, c = g.shape[1] // 2, g.shape[2]
        in_specs += [pl.BlockSpec((None, half, c), lambda i, ids: (ids[1], ids[0], 0)),
                     pl.BlockSpec((None, half, c), lambda i, ids: (ids[1], 0, 0)),
                     pl.BlockSpec((3, half, c), lambda i, ids: (0, 0, 0))]
        out_specs.append(pl.BlockSpec((half, c), lambda i, ids: (0, 0)))
        out_shape.append(jax.ShapeDtypeStruct((half, c), F32))
    return _tc_call(
        body, name=name, prefetch=1, grid=(1,), in_specs=in_specs, out_specs=out_specs, out_shape=out_shape,
        compiler_params=_cp("arbitrary"),
    )(ids, *[a for trio in zip(gs, ras, rbs) for a in trio])


def _position():
    x, y, c = lax.axis_index("x"), lax.axis_index("y"), lax.axis_index("c")
    chips = [(1 - x, y), (x, 1 - y), (1 - x, 1 - y)]
    return x, y, c, chips


def _shard_half(ref, wm, h):
    if wm.kind == "tiny":
        return ref
    if wm.nl == 2:
        return ref.at[h]
    return ref.at[pl.ds(pl.multiple_of(h * (wm.k // 2), 16), wm.k // 2), :]


def _region(full, wm, s, h):
    if wm.kind == "tiny":
        return full.at[s]
    cols = pl.ds(pl.multiple_of(s * wm.n, LANES), wm.n) if wm.kind == "col" else slice(None)
    if wm.nl == 2:
        rows = pl.ds(pl.multiple_of(s * wm.k, 16), wm.k) if wm.kind == "row" else slice(None)
        return full.at[slice(None) if h is None else h, rows, cols]
    if wm.kind == "col":
        rows = slice(None) if h is None else pl.ds(pl.multiple_of(h * (wm.k // 2), 16), wm.k // 2)
    elif h is None:
        rows = pl.ds(pl.multiple_of(s * wm.k, 16), wm.k)
    else:
        rows = pl.ds(pl.multiple_of(s * wm.k + h * (wm.k // 2), 16), wm.k // 2)
    return full.at[rows, cols]


def _full_shape(wm):
    if wm.kind == "tiny":
        return (N_CHIPS, wm.k, wm.n)
    shape = (wm.k, N_CHIPS * wm.n) if wm.kind == "col" else (N_CHIPS * wm.k, wm.n)
    return shape if wm.nl == 1 else (wm.nl,) + shape


def _handshake(peers):
    barrier = pltpu.get_barrier_semaphore()
    for peer in peers:
        pl.semaphore_signal(barrier, inc=1, device_id=peer, device_id_type=MESH)
    pl.semaphore_wait(barrier, len(peers))


def _all_gather_group(gi, shards):
    wms = AG_GROUPS[gi]
    nw = len(wms)

    def body(*refs):
        sh, full = refs[:nw], refs[nw:2 * nw]
        ici_s, ici_r, pass_s, pass_r, own_s, own_r = refs[2 * nw:]
        x, y, c, _ = _position()
        me, sibling = 2 * x + y, (x, y, 1 - c)
        first, second, diagonal = (x ^ (1 - c), y ^ c), (x ^ c, y ^ (1 - c)), (1 - x, 1 - y)
        chip_id = lambda chip: 2 * chip[0] + chip[1]
        _handshake([(*first, c), (*second, c), sibling])

        def rcopy(src, dst, s_sem, r_sem, to):
            return pltpu.make_async_remote_copy(src_ref=src, dst_ref=dst, send_sem=s_sem, recv_sem=r_sem,
                                                device_id=to, device_id_type=MESH)

        started = []

        def go(cp):
            cp.start()
            started.append(cp)

        for i, wm in enumerate(wms):
            half, dst = _shard_half(sh[i], wm, c), _region(full[i], wm, me, c)
            go(rcopy(half, dst, ici_s.at[i, 0], ici_r.at[i, 0], (*first, c)))
            go(rcopy(half, dst, ici_s.at[i, 1], ici_r.at[i, 1], (*second, c)))
            go(rcopy(sh[i], _region(full[i], wm, me, None), own_s.at[i], own_r.at[i], sibling))
        for i, wm in enumerate(wms):
            got = _region(full[i], wm, chip_id(first), c)
            rcopy(got, got, ici_s.at[i, 0], ici_r.at[i, 0], sibling).wait_recv()
            go(rcopy(got, got, ici_s.at[i, 2], ici_r.at[i, 2], (*second, c)))
            if wm.kind != "tiny":
                go(rcopy(got, got, pass_s.at[i, 0], pass_r.at[i, 0], sibling))
        for i, wm in enumerate(wms):
            for j, chip in ((1, second), (2, diagonal)):
                got = _region(full[i], wm, chip_id(chip), c)
                rcopy(got, got, ici_s.at[i, j], ici_r.at[i, j], sibling).wait_recv()
                if wm.kind != "tiny":
                    go(rcopy(got, got, pass_s.at[i, j], pass_r.at[i, j], sibling))
        for i, wm in enumerate(wms):
            mine = _region(full[i], wm, me, None)
            rcopy(mine, mine, own_s.at[i], own_r.at[i], sibling).wait_recv()
            if wm.kind != "tiny":
                for j, chip in ((0, second), (1, first), (2, diagonal)):
                    got = _region(full[i], wm, chip_id(chip), 1 - c)
                    rcopy(got, got, pass_s.at[i, j], pass_r.at[i, j], sibling).wait_recv()
        for cp in started:
            cp.wait_send()

    return pl.kernel(
        body, out_type=[jax.ShapeDtypeStruct(_full_shape(wm), s.dtype) for wm, s in zip(wms, shards)],
        mesh=plsc.ScalarSubcoreMesh(axis_name="sequencer", num_cores=1), name=f"ag_group{gi}",
        scratch_types=[pltpu.SemaphoreType.DMA((nw, 3))] * 4 + [pltpu.SemaphoreType.DMA((nw,))] * 2,
        compiler_params=pltpu.CompilerParams(collective_id=gi),
    )(*shards)


def _sequencer_call(body, name, cid, out_types, scratch, args):
    return pl.kernel(
        body, out_type=out_types, mesh=plsc.ScalarSubcoreMesh(axis_name="sequencer", num_cores=1), name=name,
        scratch_types=scratch, compiler_params=pltpu.CompilerParams(collective_id=cid),
    )(*args)


def _pair_exchange(gs, tag, cid):
    n = len(gs)

    def body(*refs):
        g, out, send_sems, recv_sems = refs[:n], refs[n:2 * n], refs[2 * n], refs[2 * n + 1]
        x, y, c, _ = _position()
        _handshake([(x, y, 1 - c)])
        cps = []
        for i in range(n):
            half = g[i].shape[1] // 2
            cps.append(pltpu.make_async_remote_copy(
                src_ref=g[i].at[:, pl.ds(pl.multiple_of((1 - c) * half, 16), half), :], dst_ref=out[i],
                send_sem=send_sems.at[i], recv_sem=recv_sems.at[i], device_id=(x, y, 1 - c), device_id_type=MESH))
            cps[-1].start()
        for cp in cps:
            cp.wait()

    return _sequencer_call(
        body, f"rs_pair_exchange{tag}", cid,
        [jax.ShapeDtypeStruct((a.shape[0], a.shape[1] // 2, a.shape[2]), a.dtype) for a in gs],
        [pltpu.SemaphoreType.DMA((n,)), pltpu.SemaphoreType.DMA((n,))], gs)


def _chip_exchange(ss, tag, cid):
    n = len(ss)

    def body(*refs):
        s, out, send_sems, recv_sems = refs[:n], refs[n:2 * n], refs[2 * n], refs[2 * n + 1]
        x, y, c, chips = _position()
        _handshake([(*chip, c) for chip in chips])
        cps = []
        for i in range(n):
            for j, chip in enumerate(chips):
                cps.append(pltpu.make_async_remote_copy(
                    src_ref=s[i].at[j], dst_ref=out[i].at[j], send_sem=send_sems.at[i, j], recv_sem=recv_sems.at[i, j],
                    device_id=(*chip, c), device_id_type=MESH))
                cps[-1].start()
        for cp in cps:
            cp.wait()

    return _sequencer_call(
        body, f"rs_chip_exchange{tag}", cid, [jax.ShapeDtypeStruct(a.shape, a.dtype) for a in ss],
        [pltpu.SemaphoreType.DMA((n, 3)), pltpu.SemaphoreType.DMA((n, 3))], ss)


def _pair_swap(g8s, tag, cid):
    n = len(g8s)

    def body(*refs):
        g, out, send_sems, recv_sems = refs[:n], refs[n:2 * n], refs[2 * n], refs[2 * n + 1]
        x, y, c, _ = _position()
        _handshake([(x, y, 1 - c)])
        cps = []
        for i in range(n):
            cps.append(pltpu.make_async_remote_copy(
                src_ref=g[i], dst_ref=out[i], send_sem=send_sems.at[i], recv_sem=recv_sems.at[i],
                device_id=(x, y, 1 - c), device_id_type=MESH))
            cps[-1].start()
        for cp in cps:
            cp.wait()

    return _sequencer_call(
        body, f"rs_pair_swap{tag}", cid, [jax.ShapeDtypeStruct(a.shape, a.dtype) for a in g8s],
        [pltpu.SemaphoreType.DMA((n,)), pltpu.SemaphoreType.DMA((n,))], g8s)


def _pair_swap_now(g8s):
    n = len(g8s)

    def body(*refs):
        g, out, send_sems, recv_sems = refs[:n], refs[n:2 * n], refs[2 * n], refs[2 * n + 1]
        x, y, c, _ = _position()
        cps = []
        for i in range(n):
            cps.append(pltpu.make_async_remote_copy(
                src_ref=g[i], dst_ref=out[i], send_sem=send_sems.at[i], recv_sem=recv_sems.at[i],
                device_id=(x, y, 1 - c), device_id_type=MESH))
            cps[-1].start()
        for cp in cps:
            cp.wait()

    return _tc_call(
        body, name="rs_pair_swap_last", in_specs=[ANY] * n, out_specs=[ANY] * n,
        out_shape=[jax.ShapeDtypeStruct(a.shape, a.dtype) for a in g8s],
        scratch_shapes=[pltpu.SemaphoreType.DMA((n,)), pltpu.SemaphoreType.DMA((n,))],
    )(*g8s)


def _all_reduce_small(vec, name):
    r, cols = vec.shape

    def body(v_ref, o_ref, gath, send_sems, recv_sems):
        x, y, c, _ = _position()
        me = 4 * x + 2 * y + c
        gath[me] = v_ref[...]
        cps = []
        for rel in range(1, N_DEV):
            peer = (x ^ (rel >> 2), y ^ ((rel >> 1) & 1), c ^ (rel & 1))
            cps.append(pltpu.make_async_remote_copy(
                src_ref=v_ref, dst_ref=gath.at[me], send_sem=send_sems.at[rel - 1], recv_sem=recv_sems.at[rel - 1],
                device_id=peer, device_id_type=MESH))
        for cp in cps:
            cp.start()
        for rel in range(1, N_DEV):
            pltpu.make_async_remote_copy(
                src_ref=v_ref, dst_ref=gath.at[me ^ rel], send_sem=send_sems.at[rel - 1],
                recv_sem=recv_sems.at[rel - 1], device_id=(x, y, c), device_id_type=MESH).wait_recv()
        for cp in cps:
            cp.wait_send()
        acc = gath[0]
        for d in range(1, N_DEV):
            acc = acc + gath[d]
        o_ref[...] = acc

    vm = pl.BlockSpec(memory_space=pltpu.VMEM)
    return _tc_call(
        body, name=name, in_specs=[vm], out_specs=vm, out_shape=jax.ShapeDtypeStruct((r, cols), F32),
        scratch_shapes=[pltpu.VMEM((N_DEV, r, cols), F32), pltpu.SemaphoreType.DMA((N_DEV - 1,)),
                        pltpu.SemaphoreType.DMA((N_DEV - 1,))],
    )(vec)


def _rope_tables(positions):
    half = QK_ROPE // 2
    inv_freq = 1.0 / (ROPE_THETA ** (jnp.arange(half, dtype=F32) / half))
    ang = positions.astype(F32)[:, None] * inv_freq
    zeros = jnp.zeros((positions.shape[0], LANES - QK_ROPE), F32)
    cos, sin = jnp.cos(ang), jnp.sin(ang)
    return jnp.concatenate([cos, cos, zeros], axis=1), jnp.concatenate([sin, sin, zeros], axis=1)


def _local_step(x, positions, tgt, wf, small, rs):
    cos, sin = _rope_tables(positions)
    w_in, w_out = wf["sc_w_in"], wf["sc_w_out"]
    w_ups, w_downs = (wf["ffn_w_up0"], wf["ffn_w_up1"]), (wf["ffn_w_down0"], wf["ffn_w_down1"])
    w_kv, w_ukv, w_dq, w_uq, w_o = wf["w_kv"], wf["w_ukv"], wf["w_dq"], wf["w_uq"], wf["w_o"]
    attn_norm, ffn_norm = small["attn_norm"], small["ffn_norm"]
    conv_b = small["ffn_conv_b"]

    def ffn_fwd(h, hf, l, then):
        up, a = _ffn_up_gate(hf, w_ups[l], small["ffn_conv_w"][l], conv_b[l:l + 1], f"ffn{l}_up_gate")
        return then(a, w_downs[l], h), (hf, up, a)

    def ffn_bwd(h, dh_out, dh_out_b, l, saved, gi, hooks):
        run = lambda stage: hooks.get(stage, lambda: None)()
        hf, up, a = saved
        d_down = _tn(f"ffn{l}_down_dw", a, dh_out_b, BF16)
        run("down_dw")
        dup, d_cw, d_cb = _gate_bwd(up, small["ffn_conv_w"][l], conv_b[l:l + 1], dh_out_b, w_downs[l],
                                    f"ffn{l}_gate_bwd")
        run("gate_bwd")
        d_up = _dw_ffn_up(f"ffn{l}_up_dw", hf, dup)
        rs.start(gi, {f"ffn_w_down{l}": d_down.reshape(N_CHIPS, F_FF // N_CHIPS, D), f"ffn_w_up{l}": d_up})
        run("up_dw")
        dh, dh_b, d_norm = _dx_norm_bwd(f"ffn{l}_up_dx", dup, w_ups[l], h, ffn_norm[l:l + 1], dh_out)
        run("up_dx")
        return dh, dh_b, d_cw, d_cb, d_norm

    hn0 = _rms_fwd(x, attn_norm[0:1], "attn0_norm")
    z = _nn_parts("sc_in", hn0, w_in, 3, BF16)
    mix = _scmix_fwd(z, small["sc_conv_w"])
    h1, hf0 = _nn_add_norm("sc_out", mix, w_out, x, ffn_norm[0:1])
    h2, ffn0_saved = ffn_fwd(h1, hf0, 0, lambda a, w, h: _nn("ffn0_down", a, w, F32, add=h))

    hn1, hk, cq_pre, cq, q, kvpre, ckv, kr, knv = _attn_prep(
        h2, attn_norm[1:2], small["kv_in_norm"], w_dq, small["q_latent_norm"], w_uq, w_kv, small["kv_latent_norm"],
        w_ukv, cos, sin)
    o = _attn_fwd(q, knv, kr)
    h3, hf1 = _nn_add_norm("attn_out", o, w_o, h2, ffn_norm[1:2])
    (loss, dh4, dh4_b, d_final), ffn1_saved = ffn_fwd(
        h3, hf1, 1, lambda a, w, h: _nn_add_loss("ffn1_down_loss", a, w, h, small["final_norm"], tgt))

    rows = D // N_CHIPS
    dh3, dh3_b, d_cw1, d_cb1, d_fn1 = ffn_bwd(h3, dh4, dh4_b, 1, ffn1_saved, 0, {})

    do = _nt("attn_out_dx", dh3_b, w_o, BF16)
    d_wo = _tn("attn_out_dw", o, dh3_b, BF16)
    rs.pair_sums(0)
    dq, dknv, dkr = _attn_bwd(q, knv, kr, do, cos, sin)
    rs.chip_sums(0)
    dh2, dh2_b, d_wuq, d_wdq, d_wukv, d_wkv, d_an1, d_kvin, d_qln, d_kvln = _attn_prep_bwd(
        dq, dknv, dkr, dh3, h2, hn1, hk, cq_pre, cq, kvpre, ckv, attn_norm[1:2], small["kv_in_norm"], w_dq,
        small["q_latent_norm"], w_uq, w_kv, small["kv_latent_norm"], w_ukv, cos, sin)
    rs.finish(0)
    by_owner = lambda dw: dw.reshape(dw.shape[0], N_CHIPS, -1).transpose(1, 0, 2)
    rs.start(1, {
        "w_o": d_wo.reshape(N_CHIPS, rows, D), "w_uq": by_owner(d_wuq), "w_dq": d_wdq.reshape(N_CHIPS, rows, Q_LORA),
        "w_ukv": by_owner(d_wukv.reshape(2 * KV_LORA, -1)).reshape(N_CHIPS, 2 * KV_LORA, -1),
        "w_kv": d_wkv.reshape(N_CHIPS, rows, KVP),
    })

    dh1, dh1_b, d_cw0, d_cb0, d_fn0 = ffn_bwd(h1, dh2, dh2_b, 0, ffn0_saved, 2, {
        "down_dw": lambda: rs.pair_sums(1), "gate_bwd": lambda: rs.chip_sums(1), "up_dw": lambda: rs.finish(1),
        "up_dx": lambda: rs.pair_sums(2)})

    d_wout = _tn("sc_out_dw", mix, dh1_b, BF16)
    dmix = _nt("sc_out_dx", dh1_b, w_out, BF16)
    dz, d_scw = _scmix_bwd(z, small["sc_conv_w"], dmix)
    d_win = _dw_sc_in(hn0, dz)
    rs.start(3, {"sc_w_out": d_wout.reshape(N_CHIPS, rows, D), "sc_w_in": d_win})
    dx, _, d_an0 = _dx_norm_bwd("sc_in_dx", dz, w_in, x, attn_norm[0:1], dh1)

    small_g = {
        "attn_norm": jnp.concatenate([d_an0, d_an1]), "ffn_norm": jnp.concatenate([d_fn0, d_fn1]),
        "final_norm": d_final, "kv_in_norm": d_kvin, "kv_latent_norm": d_kvln, "q_latent_norm": d_qln,
        "ffn_conv_b": jnp.concatenate([d_cb0, d_cb1]), "sc_conv_w": d_scw, "ffn_conv_w": jnp.stack([d_cw0, d_cw1]),
    }
    return loss, dx, small_g


RS_GROUPS = (("ffn_w_down1", "ffn_w_up1"), ("w_o", "w_uq", "w_dq", "w_ukv", "w_kv"),
             ("ffn_w_down0", "ffn_w_up0"), ("sc_w_out", "sc_w_in"))


class _ReduceScatter:
    def __init__(self, ids, finish):
        self.ids, self.grads, self.step, self.mine, self.sib, self.finish = ids, {}, {}, {}, {}, finish

    def _cid(self, gi):
        return len(AG_GROUPS) + 3 * gi

    def start(self, gi, grads):
        self.grads.update(grads)
        own = [grads[n] for n in RS_GROUPS[gi]]
        self.step[gi] = (own, _pair_exchange(own, gi, self._cid(gi)))

    def pair_sums(self, gi):
        own, ra = self.step[gi]
        sums = _pair_sums(self.ids, own, ra, f"rs_pair_sums{gi}")
        self.step[gi] = (own, ra, _chip_exchange(sums, gi, self._cid(gi) + 1))

    def chip_sums(self, gi):
        own, ra, rb = self.step[gi]
        mine = _chip_sums(self.ids, own, ra, rb, f"rs_chip_sums{gi}")
        self.mine.update(zip(RS_GROUPS[gi], mine))
        last = gi == len(RS_GROUPS) - 1
        swapped = _pair_swap_now(mine) if last else _pair_swap(mine, gi, self._cid(gi) + 2)
        self.sib.update(zip(RS_GROUPS[gi], swapped))

SMALL_REPL = ("attn_norm", "ffn_norm", "final_norm", "kv_in_norm", "kv_latent_norm", "q_latent_norm", "ffn_conv_b")
SMALL_SHARDED = ("sc_conv_w", "ffn_conv_w")
SMALL_ROWS = 256


def _pad_heads(w_uq):
    per_head = w_uq.reshape(Q_LORA, -1, QK_NOPE + QK_ROPE)
    return jnp.pad(per_head, ((0, 0), (0, 0), (0, HEAD_PAD - QK_NOPE - QK_ROPE))).reshape(Q_LORA, -1)


def _pack_kv(w_dkv, w_kr):
    return jnp.concatenate([w_dkv, w_kr, jnp.zeros((w_kr.shape[0], LANES - QK_ROPE), w_kr.dtype)], axis=1)


def kernel(x, positions, attn_norm, ffn_norm, final_norm, sc_w_in, sc_conv_w, sc_w_out, kv_in_norm, w_dkv, kv_latent_norm, w_kr, w_uk, w_uv, w_dq, q_latent_norm, w_uq, w_o, ffn_w_up, ffn_conv_w, ffn_conv_b, ffn_w_down, loss_target, m_attn_norm, m_ffn_norm, m_final_norm, m_sc_w_in, m_sc_conv_w, m_sc_w_out, m_kv_in_norm, m_w_dkv, m_kv_latent_norm, m_w_kr, m_w_uk, m_w_uv, m_w_dq, m_q_latent_norm, m_w_uq, m_w_o, m_ffn_w_up, m_ffn_conv_w, m_ffn_conv_b, m_ffn_w_down, v_attn_norm, v_ffn_norm, v_final_norm, v_sc_w_in, v_sc_conv_w, v_sc_w_out, v_kv_in_norm, v_w_dkv, v_kv_latent_norm, v_w_kr, v_w_uk, v_w_uv, v_w_dq, v_q_latent_norm, v_w_uq, v_w_o, v_ffn_w_up, v_ffn_conv_w, v_ffn_conv_b, v_ffn_w_down):
    names = ("attn_norm", "ffn_norm", "final_norm", "sc_w_in", "sc_conv_w", "sc_w_out", "kv_in_norm", "w_dkv",
             "kv_latent_norm", "w_kr", "w_uk", "w_uv", "w_dq", "q_latent_norm", "w_uq", "w_o", "ffn_w_up",
             "ffn_conv_w", "ffn_conv_b", "ffn_w_down")
    w = dict(zip(names, (attn_norm, ffn_norm, final_norm, sc_w_in, sc_conv_w, sc_w_out, kv_in_norm, w_dkv,
                         kv_latent_norm, w_kr, w_uk, w_uv, w_dq, q_latent_norm, w_uq, w_o, ffn_w_up,
                         ffn_conv_w, ffn_conv_b, ffn_w_down)))
    m = dict(zip(names, (m_attn_norm, m_ffn_norm, m_final_norm, m_sc_w_in, m_sc_conv_w, m_sc_w_out, m_kv_in_norm,
                         m_w_dkv, m_kv_latent_norm, m_w_kr, m_w_uk, m_w_uv, m_w_dq, m_q_latent_norm, m_w_uq, m_w_o,
                         m_ffn_w_up, m_ffn_conv_w, m_ffn_conv_b, m_ffn_w_down)))
    v = dict(zip(names, (v_attn_norm, v_ffn_norm, v_final_norm, v_sc_w_in, v_sc_conv_w, v_sc_w_out, v_kv_in_norm,
                         v_w_dkv, v_kv_latent_norm, v_w_kr, v_w_uk, v_w_uv, v_w_dq, v_q_latent_norm, v_w_uq, v_w_o,
                         v_ffn_w_up, v_ffn_conv_w, v_ffn_conv_b, v_ffn_w_down)))

    _ORDER[0] = None
    ix, iy, ic = lax.axis_index("x"), lax.axis_index("y"), lax.axis_index("c")
    chip = 2 * ix + iy
    ids = jnp.stack([ic, chip]).astype(jnp.int32)

    def shards_of(t):
        return {
            "sc_w_in": t["sc_w_in"][0], "sc_w_out": t["sc_w_out"][0], "ffn_w_up": t["ffn_w_up"],
            "ffn_w_down": t["ffn_w_down"], "w_kv": _pack_kv(t["w_dkv"], t["w_kr"]),
            "w_ukv": jnp.stack([t["w_uk"], t["w_uv"]]), "w_dq": t["w_dq"][0], "w_uq": _pad_heads(t["w_uq"][0]),
            "w_o": t["w_o"][0],
        }

    ws, ms, vs = shards_of(w), shards_of(m), shards_of(v)

    def ag_shard(name):
        if name == "sc_conv_w":
            return sc_conv_w[0]
        if name == "ffn_conv_w":
            return ffn_conv_w.reshape(6, -1)
        if name[:-1] in ("ffn_w_up", "ffn_w_down"):
            return ws[name[:-1]][int(name[-1])].astype(BF16)
        return ws[name].astype(BF16)

    wf = {}
    for gi, wms in enumerate(AG_GROUPS):
        fulls = _all_gather_group(gi, [ag_shard(wm.name) for wm in wms])
        wf.update({wm.name: f for wm, f in zip(wms, fulls)})
    small = {
        "attn_norm": attn_norm, "ffn_norm": ffn_norm, "final_norm": final_norm[None], "kv_in_norm": kv_in_norm[None],
        "kv_latent_norm": kv_latent_norm[None], "q_latent_norm": q_latent_norm, "ffn_conv_b": ffn_conv_b,
        "sc_conv_w": wf["sc_conv_w"].transpose(1, 0, 2).reshape(3, D),
        "ffn_conv_w": wf["ffn_conv_w"].reshape(N_CHIPS, 2, 3, -1).transpose(1, 2, 0, 3).reshape(2, 3, F_FF),
    }

    res = {}

    held = {
        "ffn_w_up0": [("ffn_w_up", dict(layer=0))], "ffn_w_up1": [("ffn_w_up", dict(layer=1))],
        "ffn_w_down0": [("ffn_w_down", dict(layer=0))], "ffn_w_down1": [("ffn_w_down", dict(layer=1))],
        "sc_w_in": [("sc_w_in", dict(layer=0))], "sc_w_out": [("sc_w_out", dict(layer=0))],
        "w_dq": [("w_dq", dict(layer=0))], "w_o": [("w_o", dict(layer=0))], "w_uq": [("w_uq", {})],
        "w_kv": [("w_dkv", dict(gcols=(0, KV_LORA))), ("w_kr", dict(gcols=(KV_LORA, KV_LORA + QK_ROPE)))],
        "w_ukv": [("w_uk", dict(owner=0)), ("w_uv", dict(owner=1))],
    }

    def adamw_group(gi):
        items = []
        for key in RS_GROUPS[gi]:
            for n, opts in held[key]:
                w_, m_, v_ = (ws[n], ms[n], vs[n]) if n == "w_uq" else (w[n], m[n], v[n])
                items.append(dict(name=n, w=w_, m=m_, v=v_, g_mine=rs.mine[key], g_sib=rs.sib[key],
                                  prev=res.get(n) if "layer" in opts and w_.shape[0] > 1 else None, **opts))
        for it, out in zip(items, _adamw_shards(ids, items, f"adamw_group{gi}")):
            res[it["name"]] = out

    rs = _ReduceScatter(ids, adamw_group)
    loss, dx, small_g = _local_step(x[0], positions[0], loss_target[0], wf, small, rs)

    rs.chip_sums(2)
    rs.pair_sums(3)

    s_order = SMALL_REPL + SMALL_SHARDED
    flat = jnp.concatenate([small_g[n].reshape(-1) for n in s_order] + [loss.reshape(-1)])
    flat = jnp.pad(flat, (0, SMALL_ROWS * LANES - flat.shape[0])).reshape(SMALL_ROWS, LANES)
    red = _all_reduce_small(flat, "ar_small").reshape(-1)
    sg, off = {}, 0
    for n in s_order:
        sz = small_g[n].size
        sg[n] = red[off:off + sz].reshape(small_g[n].shape)
        off += sz
    loss_out = red[off]
    grads = {n: sg[n].reshape(w[n].shape) for n in SMALL_REPL}
    grads["sc_conv_w"] = lax.dynamic_slice_in_dim(sg["sc_conv_w"], chip * (D // N_CHIPS), D // N_CHIPS, axis=1)[None]
    grads["ffn_conv_w"] = lax.dynamic_slice_in_dim(sg["ffn_conv_w"], chip * (F_FF // N_CHIPS), F_FF // N_CHIPS, axis=2)

    small_names = SMALL_REPL + SMALL_SHARDED

    def pack_small(tree):
        return jnp.concatenate([tree[n].reshape(-1) for n in small_names]).reshape(-1, LANES)

    small_res = _adamw_small(pack_small(w), pack_small(grads), pack_small(m), pack_small(v))
    rs.finish(2)
    rs.chip_sums(3)
    rs.finish(3)
    outs = [grads, {}, {}, {}]
    for k, dst in enumerate(outs):
        for n in res:
            dst[n] = res[n][k]
        unpadded = res["w_uq"][k].reshape(Q_LORA, -1, HEAD_PAD)[:, :, :QK_NOPE + QK_ROPE]
        dst["w_uq"] = unpadded.reshape(w_uq.shape)
    grads, delta, new_m, new_v = outs
    for slab, dst in zip(small_res, (delta, new_m, new_v)):
        f, off = slab.reshape(-1), 0
        for n in small_names:
            dst[n] = f[off:off + w[n].size].reshape(w[n].shape)
            off += w[n].size

    _ORDER[0] = None
    return (loss_out, dx[None], *[grads[n] for n in names], *[delta[n] for n in names],
            *[new_m[n] for n in names], *[new_v[n] for n in names])
```

```python
from typing import NamedTuple

import jax
import jax.numpy as jnp
from jax import lax
from jax.experimental import pallas as pl
from jax.experimental.pallas import tpu as pltpu
from jax.experimental.pallas import tpu_sc as plsc

F32 = jnp.float32
BF16 = jnp.bfloat16

T = 2048
D = 1024
F_FF = 2816
N_HEADS = 8
QK_NOPE = 128
QK_ROPE = 64
V_HEAD = 128
Q_LORA = 384
KV_LORA = 256
CHUNK_SHIFT = 6
ROPE_THETA = 10000.0
EPS = 1e-6
NEG_INF = -1e30
HEAD_PAD = 256
KVP = KV_LORA + 128

ADAM_LR = 0.001
ADAM_B1 = 0.9
ADAM_B2 = 0.999
ADAM_EPS = 1e-08
ADAM_WD = 0.01
ADAM_STEP = 10

N_CHIPS = 4
N_DEV = 8
LANES = 128
TC = 256
V7X_VMEM_LIMIT = 56 * 1024 * 1024

MESH = pl.DeviceIdType.MESH
ANY = pl.BlockSpec(memory_space=pl.ANY)


class _W(NamedTuple):
    name: str
    kind: str
    nl: int
    k: int
    n: int


AG_GROUPS = (
    (_W("sc_w_in", "col", 1, D, 3 * D // N_CHIPS), _W("sc_conv_w", "tiny", 1, 3, D // N_CHIPS),
     _W("ffn_conv_w", "tiny", 1, 6, F_FF // N_CHIPS), _W("sc_w_out", "row", 1, D // N_CHIPS, D)),
    (_W("ffn_w_up0", "col", 1, D, 2 * F_FF // N_CHIPS),),
    (_W("ffn_w_down0", "row", 1, F_FF // N_CHIPS, D),),
    (_W("w_kv", "row", 1, D // N_CHIPS, KVP), _W("w_ukv", "col", 2, KV_LORA, N_HEADS * QK_NOPE // N_CHIPS),
     _W("w_dq", "row", 1, D // N_CHIPS, Q_LORA),
     _W("w_uq", "col", 1, Q_LORA, N_HEADS * HEAD_PAD // N_CHIPS),
     _W("w_o", "row", 1, N_HEADS * V_HEAD // N_CHIPS, D)),
    (_W("ffn_w_up1", "col", 1, D, 2 * F_FF // N_CHIPS), _W("ffn_w_down1", "row", 1, F_FF // N_CHIPS, D)),
)


def _cp(*sem):
    return pltpu.CompilerParams(dimension_semantics=sem, vmem_limit_bytes=V7X_VMEM_LIMIT)


_ORDER = [None]


def _tc_call(body, *, name, out_shape, in_specs=None, out_specs=None, grid=(), scratch_shapes=(), prefetch=0,
             input_output_aliases=None, compiler_params=None):
    def run(*args):
        specs = [pl.BlockSpec(memory_space=pltpu.VMEM)] * (len(args) - prefetch) if in_specs is None else list(in_specs)
        inner, dep = body, _ORDER[0]
        if dep is not None:
            unread = prefetch + len(specs)
            specs, args = specs + [ANY], (*args, dep)

            def inner(*refs):
                return body(*refs[:unread], *refs[unread + 1:])

        kwargs = dict(name=name, out_shape=out_shape, input_output_aliases=input_output_aliases or {},
                      compiler_params=compiler_params)
        if prefetch:
            kwargs["grid_spec"] = pltpu.PrefetchScalarGridSpec(
                num_scalar_prefetch=prefetch, grid=grid, in_specs=specs, out_specs=out_specs,
                scratch_shapes=scratch_shapes)
        else:
            kwargs.update(grid=grid, in_specs=specs, scratch_shapes=scratch_shapes)
            if out_specs is not None:
                kwargs["out_specs"] = out_specs
        out = pl.pallas_call(inner, **kwargs)(*args)
        _ORDER[0] = out[0] if isinstance(out, (list, tuple)) else out
        return out

    return run


def _tile(n, cands):
    for c in cands:
        if n % c == 0:
            return c
    raise ValueError(f"no tile for {n}")


NN_DIMS = (((1,), (0,)), ((), ()))
NT_DIMS = (((1,), (1,)), ((), ()))
TN_DIMS = (((0,), (0,)), ((), ()))
M_TILES = (1024, 512, 384, 256, 128)
N_TILES = (1408, 1024, 768, 512, 384, 256, 128)
MM_BLOCK_BYTES = 36 * 1024 * 1024


def _fit(m, n, block_bytes, m_tiles=M_TILES, n_tiles=N_TILES):
    for tm in [c for c in m_tiles if m % c == 0]:
        for tn in [c for c in n_tiles if n % c == 0]:
            if 2 * block_bytes(tm, tn) + 4 * tm * tn <= MM_BLOCK_BYTES:
                return tm, tn
    raise ValueError(f"no tiles for {m} x {n}")


def _size(x):
    return x.dtype.itemsize


def _mm(name, a, b, dims, grid, a_spec, b_spec, o_spec, o_sds, add=None, red=None, acc_shape=None):
    n_red = None if red is None else grid[red]

    def body(*refs):
        a_ref, b_ref = refs[0], refs[1]
        add_ref = refs[2] if add is not None else None
        o_ref = refs[3] if add is not None else refs[2]
        part = lax.dot_general(a_ref[...].astype(BF16), b_ref[...].astype(BF16), dims, preferred_element_type=F32)
        if red is None:
            if add is not None:
                part = part + add_ref[...]
            o_ref[...] = part.astype(o_ref.dtype)
            return
        acc_ref = refs[-1]
        r = pl.program_id(red)

        @pl.when(r == 0)
        def _():
            acc_ref[...] = part

        @pl.when(r > 0)
        def _():
            acc_ref[...] += part

        @pl.when(r == n_red - 1)
        def _():
            o_ref[...] = acc_ref[...].astype(o_ref.dtype)

    sem = tuple("arbitrary" if ax == red else "parallel" for ax in range(len(grid)))
    in_specs = [a_spec, b_spec] + ([o_spec] if add is not None else [])
    args = (a, b) + ((add,) if add is not None else ())
    return _tc_call(
        body, name=name, grid=grid, in_specs=in_specs, out_specs=o_spec, out_shape=o_sds,
        scratch_shapes=[] if red is None else [pltpu.VMEM(acc_shape, F32)], compiler_params=_cp(*sem),
    )(*args)


def _nn(name, a, b, out_dtype, add=None, lead=None):
    (m, k), n = a.shape, b.shape[-1]
    osz = jnp.dtype(out_dtype).itemsize + (4 if add is not None else 0)
    tm, tn = _fit(m, n, lambda tm, tn: tm * k * _size(a) + k * tn * _size(b) + tm * tn * osz)
    if lead is None:
        b_spec = pl.BlockSpec((k, tn), lambda i, j: (0, j))
    else:
        b_spec = pl.BlockSpec((None, k, tn), lambda i, j: (lead, 0, j))
    return _mm(name, a, b, NN_DIMS, (m // tm, n // tn), pl.BlockSpec((tm, k), lambda i, j: (i, 0)), b_spec,
               pl.BlockSpec((tm, tn), lambda i, j: (i, j)), jax.ShapeDtypeStruct((m, n), out_dtype), add=add)


def _nn_parts(name, a, b, parts, out_dtype, lead=None, stacked=False):
    m, k = a.shape
    c = b.shape[-1] if stacked else b.shape[-1] // parts
    osz = jnp.dtype(out_dtype).itemsize
    tm, tn = _fit(m, c, lambda tm, tn: tm * k * _size(a) + k * tn * _size(b) + tm * tn * osz)
    nb = c // tn
    if stacked:
        b_spec = pl.BlockSpec((None, k, tn), lambda i, p, j: (p, 0, j))
    elif lead is None:
        b_spec = pl.BlockSpec((k, tn), lambda i, p, j: (0, p * nb + j))
    else:
        b_spec = pl.BlockSpec((None, k, tn), lambda i, p, j: (lead, 0, p * nb + j))
    return _mm(name, a, b, NN_DIMS, (m // tm, parts, nb), pl.BlockSpec((tm, k), lambda i, p, j: (i, 0)), b_spec,
               pl.BlockSpec((None, tm, tn), lambda i, p, j: (p, i, j)), jax.ShapeDtypeStruct((parts, m, c), out_dtype))


def _nt(name, a, b, out_dtype, lead=None):
    (m, k), n = a.shape, b.shape[-2]
    osz = jnp.dtype(out_dtype).itemsize
    tm, tn = _fit(m, n, lambda tm, tn: tm * k * _size(a) + tn * k * _size(b) + tm * tn * osz)
    if lead is None:
        b_spec = pl.BlockSpec((tn, k), lambda i, j: (j, 0))
    else:
        b_spec = pl.BlockSpec((None, tn, k), lambda i, j: (lead, j, 0))
    return _mm(name, a, b, NT_DIMS, (m // tm, n // tn), pl.BlockSpec((tm, k), lambda i, j: (i, 0)), b_spec,
               pl.BlockSpec((tm, tn), lambda i, j: (i, j)), jax.ShapeDtypeStruct((m, n), out_dtype))


def _tn(name, a, b, out_dtype):
    (k, m), n = a.shape, b.shape[1]
    osz = jnp.dtype(out_dtype).itemsize
    tm, tn = _fit(m, n, lambda tm, tn: k * tm * _size(a) + k * tn * _size(b) + tm * tn * osz,
                  m_tiles=(512, 384, 256, 128), n_tiles=(n,) + N_TILES)
    return _mm(name, a, b, TN_DIMS, (m // tm, n // tn), pl.BlockSpec((k, tm), lambda i, j: (0, i)),
               pl.BlockSpec((k, tn), lambda i, j: (0, j)), pl.BlockSpec((tm, tn), lambda i, j: (i, j)),
               jax.ShapeDtypeStruct((m, n), out_dtype))


def _nn_add_norm(name, a, b, add, g):
    (m, k), n = a.shape, b.shape[1]
    tm = 512

    def body(a_ref, b_ref, add_ref, g_ref, h_ref, hn_ref):
        h = jnp.dot(a_ref[...], b_ref[...], preferred_element_type=F32) + add_ref[...]
        h_ref[...] = h
        hn_ref[...] = _rms_rows(h, g_ref[...]).astype(BF16)

    rows = lambda w: pl.BlockSpec((tm, w), lambda i: (i, 0))
    return _tc_call(
        body, name=name, grid=(m // tm,),
        in_specs=[rows(k), pl.BlockSpec((k, n), lambda i: (0, 0)), rows(n), pl.BlockSpec((1, n), lambda i: (0, 0))],
        out_specs=[rows(n), rows(n)],
        out_shape=[jax.ShapeDtypeStruct((m, n), F32), jax.ShapeDtypeStruct((m, n), BF16)], compiler_params=_cp("parallel"),
    )(a, b, add, g)


def _nn_add_loss(name, a, b, add, g, tgt):
    (m, k), n = a.shape, b.shape[1]
    tm = 512

    def body(a_ref, b_ref, add_ref, g_ref, t_ref, loss_ref, dh_ref, dhb_ref, dg_ref):
        xv = jnp.dot(a_ref[...], b_ref[...], preferred_element_type=F32) + add_ref[...]
        gv = g_ref[...]
        r = lax.rsqrt(jnp.mean(xv * xv, axis=1, keepdims=True) + EPS)
        err = xv * r * gv - t_ref[...]
        part = 0.5 * jnp.sum(jnp.mean(err * err, axis=1, keepdims=True), axis=0, keepdims=True)
        dx, dg = _rms_bwd_math(xv, gv, err * (1.0 / n))
        dh_ref[...] = dx
        dhb_ref[...] = dx.astype(BF16)

        @pl.when(pl.program_id(0) == 0)
        def _():
            dg_ref[...] = jnp.zeros_like(dg_ref)
            loss_ref[...] = jnp.zeros_like(loss_ref)

        dg_ref[...] += dg
        loss_ref[...] += jnp.broadcast_to(part, loss_ref.shape)

    rows = lambda w: pl.BlockSpec((tm, w), lambda i: (i, 0))
    vec = pl.BlockSpec((1, n), lambda i: (0, 0))
    return _tc_call(
        body, name=name, grid=(m // tm,),
        in_specs=[rows(k), pl.BlockSpec((k, n), lambda i: (0, 0)), rows(n), vec, rows(n)],
        out_specs=[pl.BlockSpec((1, LANES), lambda i: (0, 0)), rows(n), rows(n), vec],
        out_shape=[jax.ShapeDtypeStruct((1, LANES), F32), jax.ShapeDtypeStruct((m, n), F32),
                   jax.ShapeDtypeStruct((m, n), BF16), jax.ShapeDtypeStruct((1, n), F32)],
        compiler_params=_cp("arbitrary"),
    )(a, b, add, g, tgt)


def _dx_norm_bwd(name, a, b, x, g, add):
    parts, t, c = a.shape
    d = b.shape[0]
    tm = 256

    def body(a_ref, b_ref, x_ref, g_ref, add_ref, dx_ref, dxb_ref, dg_ref):
        dy = None
        for p in range(parts):
            part = lax.dot_general(a_ref[p], b_ref[:, p * c:(p + 1) * c], NT_DIMS, preferred_element_type=F32)
            dy = part if dy is None else dy + part
        dx, dg = _rms_bwd_math(x_ref[...], g_ref[...], dy)
        dx = dx + add_ref[...]
        dx_ref[...] = dx
        dxb_ref[...] = dx.astype(BF16)

        @pl.when(pl.program_id(0) == 0)
        def _():
            dg_ref[...] = jnp.zeros_like(dg_ref)

        dg_ref[...] += dg

    rows = pl.BlockSpec((tm, d), lambda i: (i, 0))
    vec = pl.BlockSpec((1, d), lambda i: (0, 0))
    return _tc_call(
        body, name=name, grid=(t // tm,),
        in_specs=[pl.BlockSpec((parts, tm, c), lambda i: (0, i, 0)), pl.BlockSpec(b.shape, lambda i: (0, 0)), rows, vec,
                  rows],
        out_specs=[rows, rows, vec],
        out_shape=[jax.ShapeDtypeStruct((t, d), F32), jax.ShapeDtypeStruct((t, d), BF16),
                   jax.ShapeDtypeStruct((1, d), F32)],
        compiler_params=_cp("arbitrary"),
    )(a, b, x, g, add)


def _dw_sc_in(hn, dz):
    t, tn, tm = hn.shape[0], TC, D
    per_part, per_chip = D // tn, 3 * D // N_CHIPS // tn
    return _mm("sc_in_dw", hn, dz, TN_DIMS, (D // tm, 3 * D // tn), pl.BlockSpec((t, tm), lambda i, j: (0, i)),
               pl.BlockSpec((None, t, tn), lambda i, j: (j // per_part, 0, j % per_part)),
               pl.BlockSpec((None, tm, tn), lambda i, j: (j // per_chip, i, j % per_chip)),
               jax.ShapeDtypeStruct((N_CHIPS, D, 3 * D // N_CHIPS), BF16))


def _dw_ffn_up(name, hf, dup):
    t, tm, ns = hf.shape[0], D, 2 * F_FF // N_CHIPS
    return _mm(name, hf, dup, TN_DIMS, (N_CHIPS, D // tm), pl.BlockSpec((t, tm), lambda s, i: (0, i)),
               pl.BlockSpec((None, t, ns), lambda s, i: (s // 2, 0, s % 2)),
               pl.BlockSpec((None, tm, ns), lambda s, i: (s, i, 0)), jax.ShapeDtypeStruct((N_CHIPS, D, ns), BF16))


def _rms_fwd(x, g, name):
    t, d = x.shape
    tr = 512

    def body(x_ref, g_ref, o_ref):
        xv = x_ref[...]
        r = lax.rsqrt(jnp.mean(xv * xv, axis=1, keepdims=True) + EPS)
        o_ref[...] = (xv * r * g_ref[...]).astype(o_ref.dtype)

    row = pl.BlockSpec((tr, d), lambda i: (i, 0))
    return _tc_call(
        body, name=name, grid=(t // tr,), in_specs=[row, pl.BlockSpec((1, d), lambda i: (0, 0))],
        out_specs=row, out_shape=jax.ShapeDtypeStruct((t, d), BF16), compiler_params=_cp("parallel"),
    )(x, g)


def _rms_bwd_math(xv, g, dy):
    r = lax.rsqrt(jnp.mean(xv * xv, axis=1, keepdims=True) + EPS)
    xh = xv * r
    gy = dy * g
    dx = r * (gy - xh * jnp.mean(gy * xh, axis=1, keepdims=True))
    dg = jnp.sum(dy * xh, axis=0, keepdims=True)
    return dx, dg


def _rot_half(x):
    lane = lax.broadcasted_iota(jnp.int32, x.shape, 1)
    return jnp.where((lane % QK_ROPE) < QK_ROPE // 2, -pltpu.roll(x, LANES - 32, axis=1),
                     pltpu.roll(x, 32, axis=1))


def _rope_fwd_math(x, cos, sin):
    return x * cos + _rot_half(x) * sin


def _rope_bwd_math(dy, cos, sin):
    return dy * cos - _rot_half(dy * sin)


def _rms_rows(x, g):
    return x * lax.rsqrt(jnp.mean(x * x, axis=1, keepdims=True) + EPS) * g


def _attn_prep(h, g_attn, g_kvin, w_dq, g_ql, w_uq, w_kv, g_kvl, w_ukv, cos, sin):
    t, d = h.shape
    tr = 256
    wq = N_HEADS * HEAD_PAD

    def body(h_ref, ga_ref, gk_ref, wdq_ref, gq_ref, wuq_ref, wkv_ref, gl_ref, wukv_ref, c_ref, s_ref,
             hn_ref, hk_ref, cqp_ref, cq_ref, q_ref, kvp_ref, ckv_ref, kr_ref, knv_ref):
        xv, cv, sv = h_ref[...], c_ref[...], s_ref[...]
        xh = xv * lax.rsqrt(jnp.mean(xv * xv, axis=1, keepdims=True) + EPS)
        hn = (xh * ga_ref[...]).astype(BF16)
        hk = (xh * gk_ref[...]).astype(BF16)
        hn_ref[...], hk_ref[...] = hn, hk
        cq_pre = jnp.dot(hn, wdq_ref[...], preferred_element_type=F32)
        cqp_ref[...] = cq_pre
        cq = _rms_rows(cq_pre, gq_ref[...]).astype(BF16)
        cq_ref[...] = cq
        for hd in range(N_HEADS):
            lo = hd * HEAD_PAD
            qh = jnp.dot(cq, wuq_ref[:, lo:lo + HEAD_PAD], preferred_element_type=F32)
            q_ref[:, lo:lo + QK_NOPE] = qh[:, :QK_NOPE].astype(BF16)
            q_ref[:, lo + QK_NOPE:lo + HEAD_PAD] = _rope_fwd_math(qh[:, QK_NOPE:], cv, sv).astype(BF16)
        kvpre = jnp.dot(hk, wkv_ref[...], preferred_element_type=F32)
        kvp_ref[...] = kvpre
        ckv = _rms_rows(kvpre[:, :KV_LORA], gl_ref[...]).astype(BF16)
        ckv_ref[...] = ckv
        kr_ref[...] = _rope_fwd_math(kvpre[:, KV_LORA:], cv, sv).astype(BF16)
        for p in range(2):
            knv_ref[p] = jnp.dot(ckv, wukv_ref[p], preferred_element_type=F32).astype(BF16)

    rows = lambda w: pl.BlockSpec((tr, w), lambda i: (i, 0))
    whole = lambda a: pl.BlockSpec(a.shape, lambda i: (0,) * a.ndim)
    sds = lambda w, dt: jax.ShapeDtypeStruct((t, w), dt)
    args = (h, g_attn, g_kvin, w_dq, g_ql, w_uq, w_kv, g_kvl, w_ukv, cos, sin)
    return _tc_call(
        body, name="attn_prep", grid=(t // tr,),
        in_specs=[rows(d)] + [whole(a) for a in args[1:9]] + [rows(LANES), rows(LANES)],
        out_specs=[rows(d), rows(d), rows(Q_LORA), rows(Q_LORA), rows(wq), rows(KVP), rows(KV_LORA), rows(LANES),
                   pl.BlockSpec((2, tr, N_HEADS * QK_NOPE), lambda i: (0, i, 0))],
        out_shape=[sds(d, BF16), sds(d, BF16), sds(Q_LORA, F32), sds(Q_LORA, BF16), sds(wq, BF16), sds(KVP, F32),
                   sds(KV_LORA, BF16), sds(LANES, BF16), jax.ShapeDtypeStruct((2, t, N_HEADS * QK_NOPE), BF16)],
        compiler_params=_cp("parallel"),
    )(*args)


def _attn_prep_bwd(dq, dknv, dkr, dh, h, hn, hk, cq_pre, cq, kvpre, ckv, g_attn, g_kvin, w_dq, g_ql, w_uq, w_kv, g_kvl,
                   w_ukv, cos, sin):
    t, d = h.shape
    tr = 256
    n_steps = t // tr
    wq = N_HEADS * HEAD_PAD
    wk = N_HEADS * QK_NOPE

    def body(dq_ref, dknv_ref, dkr_ref, dh_ref, h_ref, hn_ref, hk_ref, cqp_ref, cq_ref, kvp_ref, ckv_ref,
             ga_ref, gk_ref, wdq_ref, gq_ref, wuq_ref, wkv_ref, gl_ref, wukv_ref, c_ref, s_ref,
             dho_ref, dhb_ref, dwuq_ref, dwdq_ref, dwukv_ref, dwkv_ref, dga_ref, dgk_ref, dgq_ref, dgl_ref,
             a_uq, a_dq, a_ukv, a_kv):
        i = pl.program_id(0)

        @pl.when(i == 0)
        def _():
            for ref in (a_uq, a_dq, a_ukv, a_kv, dga_ref, dgk_ref, dgq_ref, dgl_ref):
                ref[...] = jnp.zeros_like(ref)

        dqv = dq_ref[...]
        dcq = lax.dot_general(dqv, wuq_ref[...], NT_DIMS, preferred_element_type=F32)
        a_uq[...] += lax.dot_general(cq_ref[...], dqv, TN_DIMS, preferred_element_type=F32)
        dcq_pre, dg = _rms_bwd_math(cqp_ref[...], gq_ref[...], dcq)
        dgq_ref[...] += dg
        dcq_pre = dcq_pre.astype(BF16)
        dhn = lax.dot_general(dcq_pre, wdq_ref[...], NT_DIMS, preferred_element_type=F32)
        a_dq[...] += lax.dot_general(hn_ref[...], dcq_pre, TN_DIMS, preferred_element_type=F32)
        dckv = None
        for p in range(2):
            dk = dknv_ref[p].astype(BF16)
            part = lax.dot_general(dk, wukv_ref[p], NT_DIMS, preferred_element_type=F32)
            dckv = part if dckv is None else dckv + part
            a_ukv[p] += lax.dot_general(ckv_ref[...], dk, TN_DIMS, preferred_element_type=F32)
        dlat, dg = _rms_bwd_math(kvp_ref[:, :KV_LORA], gl_ref[...], dckv)
        dgl_ref[...] += dg
        dkr_pre = _rope_bwd_math(dkr_ref[...], c_ref[...], s_ref[...])
        dkvpre = jnp.concatenate([dlat, dkr_pre], axis=1).astype(BF16)
        dhk = lax.dot_general(dkvpre, wkv_ref[...], NT_DIMS, preferred_element_type=F32)
        a_kv[...] += lax.dot_general(hk_ref[...], dkvpre, TN_DIMS, preferred_element_type=F32)
        xv = h_ref[...]
        dx1, dg = _rms_bwd_math(xv, ga_ref[...], dhn)
        dga_ref[...] += dg
        dx2, dg = _rms_bwd_math(xv, gk_ref[...], dhk)
        dgk_ref[...] += dg
        dh_new = dh_ref[...] + dx1 + dx2
        dho_ref[...] = dh_new
        dhb_ref[...] = dh_new.astype(BF16)

        @pl.when(i == n_steps - 1)
        def _():
            dwuq_ref[...] = a_uq[...].astype(BF16)
            dwdq_ref[...] = a_dq[...].astype(BF16)
            dwukv_ref[...] = a_ukv[...].astype(BF16)
            dwkv_ref[...] = a_kv[...].astype(BF16)

    rows = lambda w: pl.BlockSpec((tr, w), lambda i: (i, 0))
    whole = lambda shape: pl.BlockSpec(shape, lambda i: (0,) * len(shape))
    weights = (g_attn, g_kvin, w_dq, g_ql, w_uq, w_kv, g_kvl, w_ukv)
    dw_shapes = [(Q_LORA, wq), (d, Q_LORA), (2, KV_LORA, wk), (d, KVP)]
    dg_shapes = [(1, d), (1, d), (1, Q_LORA), (1, KV_LORA)]
    return _tc_call(
        body, name="attn_prep_bwd", grid=(n_steps,),
        in_specs=[rows(wq), pl.BlockSpec((2, tr, wk), lambda i: (0, i, 0)), rows(LANES), rows(d), rows(d), rows(d),
                  rows(d), rows(Q_LORA), rows(Q_LORA), rows(KVP), rows(KV_LORA)]
        + [whole(a.shape) for a in weights] + [rows(LANES), rows(LANES)],
        out_specs=[rows(d), rows(d)] + [whole(s) for s in dw_shapes + dg_shapes],
        out_shape=[jax.ShapeDtypeStruct((t, d), F32), jax.ShapeDtypeStruct((t, d), BF16)]
        + [jax.ShapeDtypeStruct(s, BF16) for s in dw_shapes] + [jax.ShapeDtypeStruct(s, F32) for s in dg_shapes],
        scratch_shapes=[pltpu.VMEM(s, F32) for s in dw_shapes], compiler_params=_cp("arbitrary"),
    )(dq, dknv, dkr, dh, h, hn, hk, cq_pre, cq, kvpre, ckv, *weights, cos, sin)


ROW_CHUNK = 64
HALO = 16
WIN = ROW_CHUNK + 16
LANE_HALVES = (slice(0, LANES), slice(LANES, TC))


def _stage(s_ref, p, src):
    t = src.shape[0]
    s_ref[p, :HALO] = jnp.zeros((HALO, TC), BF16)
    s_ref[p, HALO:HALO + t] = src
    s_ref[p, HALO + t:] = jnp.zeros((HALO, TC), BF16)


def _window(s_ref, p, i, lanes):
    base = pl.multiple_of(i * ROW_CHUNK, ROW_CHUNK)
    return s_ref[p, pl.ds(base, ROW_CHUNK + 2 * HALO), lanes].astype(F32)[8:8 + WIN]


def _valid(x):
    return x[8:8 + ROW_CHUNK]


def _prev(x, k):
    return pltpu.roll(x, k, axis=0)


def _next(x, k):
    return pltpu.roll(x, WIN - k, axis=0)


def _taps(w_ref, lanes):
    return w_ref[0:1, lanes], w_ref[1:2, lanes], w_ref[2:3, lanes]


def _fold8(x):
    return jnp.sum(x.reshape(ROW_CHUNK // 8, 8, x.shape[-1]), axis=0)


def _store_rows(ref, idx, i, lanes, x):
    rows = pl.ds(pl.multiple_of(i * ROW_CHUNK, ROW_CHUNK), ROW_CHUNK)
    ref[(*idx, rows, lanes)] = x.astype(ref.dtype)


def _for_chunks(t, chunk):
    def step(i, carry):
        for lanes in LANE_HALVES:
            chunk(i, lanes)
        return carry

    lax.fori_loop(0, t // ROW_CHUNK, step, 0)


def _write_col_sums(acc_ref, outs):
    for k, (ref, row) in enumerate(outs):
        ref[row:row + 1, :] = jnp.sum(acc_ref[k], axis=0, keepdims=True)


def _shift_down(x, k):
    row = lax.broadcasted_iota(jnp.int32, x.shape, 0)
    return jnp.where(row >= k, pltpu.roll(x, k, axis=0), 0.0)


def _shift_up(x, k):
    n = x.shape[0]
    row = lax.broadcasted_iota(jnp.int32, x.shape, 0)
    return jnp.where(row < n - k, pltpu.roll(x, n - k, axis=0), 0.0)


def _conv3(x, w_ref):
    return _shift_down(x, 2) * w_ref[0:1, :] + _shift_down(x, 1) * w_ref[1:2, :] + x * w_ref[2:3, :]


def _col(parts, t):
    if parts is None:
        return pl.BlockSpec((t, TC), lambda j: (0, j))
    return pl.BlockSpec((parts, t, TC), lambda j: (0, 0, j))


def _staging(parts, t):
    return pltpu.VMEM((parts, t + 2 * HALO, TC), BF16)


def _scmix_fwd(z, w):
    t = z.shape[1]

    def body(z_ref, w_ref, m_ref):
        b, c, u = (z_ref[p].astype(F32) for p in range(3))
        m_ref[...] = (b * _conv3(c * u, w_ref)).astype(BF16)

    return _tc_call(
        body, name="scmix_fwd", grid=(D // TC,), in_specs=[_col(3, t), pl.BlockSpec((3, TC), lambda j: (0, j))],
        out_specs=_col(None, t), out_shape=jax.ShapeDtypeStruct((t, D), BF16), compiler_params=_cp("parallel"),
    )(z, w)


def _scmix_bwd(z, w, dm):
    t = z.shape[1]

    def body(z_ref, w_ref, dm_ref, dz_ref, dw_ref, s_ref, acc_ref):
        for p in range(3):
            _stage(s_ref, p, z_ref[p])
        _stage(s_ref, 3, dm_ref[...])
        acc_ref[...] = jnp.zeros_like(acc_ref)

        def chunk(i, lanes):
            w0, w1, w2 = _taps(w_ref, lanes)
            b, c, u, dm = (_window(s_ref, p, i, lanes) for p in range(4))
            cu = c * u
            cu1, cu2 = _prev(cu, 1), _prev(cu, 2)
            _store_rows(dz_ref, (0,), i, lanes, _valid(dm * (cu2 * w0 + cu1 * w1 + cu * w2)))
            dcv = dm * b
            dcu = dcv * w2 + _next(dcv, 1) * w1 + _next(dcv, 2) * w0
            _store_rows(dz_ref, (1,), i, lanes, _valid(dcu * u))
            _store_rows(dz_ref, (2,), i, lanes, _valid(dcu * c))
            for k, shifted in enumerate((cu2, cu1, cu)):
                acc_ref[k, :, lanes] += _fold8(_valid(dcv * shifted))

        _for_chunks(t, chunk)
        _write_col_sums(acc_ref, [(dw_ref, 0), (dw_ref, 1), (dw_ref, 2)])

    wspec = pl.BlockSpec((3, TC), lambda j: (0, j))
    return _tc_call(
        body, name="scmix_bwd", grid=(D // TC,), in_specs=[_col(3, t), wspec, _col(None, t)],
        out_specs=[_col(3, t), wspec],
        out_shape=[jax.ShapeDtypeStruct((3, t, D), BF16), jax.ShapeDtypeStruct((3, D), F32)],
        scratch_shapes=[_staging(4, t), pltpu.VMEM((3, 8, TC), F32)], compiler_params=_cp("parallel"),
    )(z, w, dm)


def _ffn_up_gate(hf, w_up, w, bias, name):
    t, d = hf.shape
    nb = F_FF // TC

    def body(hf_ref, wg_ref, wv_ref, w_ref, b_ref, up_ref, a_ref, prev_ref):
        @pl.when(pl.program_id(0) == 0)
        def _():
            prev_ref[...] = jnp.zeros_like(prev_ref)

        gc = _conv3(prev_ref[0].astype(F32), w_ref) + b_ref[...]
        a_ref[...] = (gc * jax.nn.sigmoid(gc) * prev_ref[1].astype(F32)).astype(BF16)
        hv = hf_ref[...]
        up_ref[0] = jnp.dot(hv, wg_ref[...], preferred_element_type=F32).astype(BF16)
        up_ref[1] = jnp.dot(hv, wv_ref[...], preferred_element_type=F32).astype(BF16)
        prev_ref[...] = up_ref[...]

    tile = lambda j: jnp.minimum(j, nb - 1)
    gated = lambda j: jnp.maximum(j - 1, 0)
    return _tc_call(
        body, name=name, grid=(nb + 1,),
        in_specs=[pl.BlockSpec((t, d), lambda j: (0, 0)), pl.BlockSpec((d, TC), lambda j: (0, tile(j))),
                  pl.BlockSpec((d, TC), lambda j: (0, nb + tile(j))), pl.BlockSpec((3, TC), lambda j: (0, gated(j))),
                  pl.BlockSpec((1, TC), lambda j: (0, gated(j)))],
        out_specs=[pl.BlockSpec((2, t, TC), lambda j: (0, 0, tile(j))), pl.BlockSpec((t, TC), lambda j: (0, gated(j)))],
        out_shape=[jax.ShapeDtypeStruct((2, t, F_FF), BF16), jax.ShapeDtypeStruct((t, F_FF), BF16)],
        scratch_shapes=[pltpu.VMEM((2, t, TC), BF16)], compiler_params=_cp("arbitrary"),
    )(hf, w_up, w_up, w, bias)


def _gate_bwd(up, w, bias, dh, w_down, name):
    t, d = dh.shape

    def body(u_ref, w_ref, b_ref, dh_ref, wd_ref, du_ref, dw_ref, db_ref, s_ref, acc_ref):
        for p in range(2):
            _stage(s_ref, p, u_ref[p])
        _stage(s_ref, 2, lax.dot_general(dh_ref[...], wd_ref[...], NT_DIMS, preferred_element_type=F32).astype(BF16))
        acc_ref[...] = jnp.zeros_like(acc_ref)

        def chunk(i, lanes):
            w0, w1, w2 = _taps(w_ref, lanes)
            g, v, da = (_window(s_ref, p, i, lanes) for p in range(3))
            g1, g2 = _prev(g, 1), _prev(g, 2)
            gc = g2 * w0 + g1 * w1 + g * w2 + b_ref[:, lanes]
            sg = jax.nn.sigmoid(gc)
            _store_rows(du_ref, (1,), i, lanes, _valid(da * (gc * sg)))
            dgc = da * v * (sg * (1.0 + gc * (1.0 - sg)))
            _store_rows(du_ref, (0,), i, lanes, _valid(dgc * w2 + _next(dgc, 1) * w1 + _next(dgc, 2) * w0))
            for k, shifted in enumerate((g2, g1, g)):
                acc_ref[k, :, lanes] += _fold8(_valid(dgc * shifted))
            acc_ref[3, :, lanes] += _fold8(_valid(dgc))

        _for_chunks(t, chunk)
        _write_col_sums(acc_ref, [(dw_ref, 0), (dw_ref, 1), (dw_ref, 2), (db_ref, 0)])

    wspec = pl.BlockSpec((3, TC), lambda j: (0, j))
    bspec = pl.BlockSpec((1, TC), lambda j: (0, j))
    return _tc_call(
        body, name=name, grid=(F_FF // TC,),
        in_specs=[_col(2, t), wspec, bspec, pl.BlockSpec((t, d), lambda j: (0, 0)), pl.BlockSpec((TC, d), lambda j: (j, 0))],
        out_specs=[_col(2, t), wspec, bspec],
        out_shape=[jax.ShapeDtypeStruct((2, t, F_FF), BF16), jax.ShapeDtypeStruct((3, F_FF), F32),
                   jax.ShapeDtypeStruct((1, F_FF), F32)],
        scratch_shapes=[_staging(3, t), pltpu.VMEM((4, 8, TC), F32)], compiler_params=_cp("parallel"),
    )(up, w, bias, dh, w_down)


ATT_TQ = 256
ATT_SCALE = (QK_NOPE + QK_ROPE) ** -0.5


def _key_ranges(lvl):
    lo = lvl * ATT_TQ
    return ([(0, lo, False)] if lvl else []) + [(lo, lo + ATT_TQ, True)]


FWD_HEADS = 4
BWD_HEADS = 2


def _fill_keys(k_ref, kn_ref, kr_ref):
    @pl.when(pl.program_id(1) == 0)
    def _():
        for hh in range(k_ref.shape[0]):
            k_ref[hh, :, :QK_NOPE] = kn_ref[:, hh * QK_NOPE:(hh + 1) * QK_NOPE]
            k_ref[hh, :, QK_NOPE:] = kr_ref[...]


def _attn_probs(q, k_ref, lvl):
    scores = []
    for lo, hi, diagonal in _key_ranges(lvl):
        s = lax.dot_general(q, k_ref[lo:hi, :], NT_DIMS, preferred_element_type=F32) * ATT_SCALE
        if diagonal:
            row = lax.broadcasted_iota(jnp.int32, s.shape, 0)
            col = lax.broadcasted_iota(jnp.int32, s.shape, 1)
            seen = lax.shift_right_logical(col, CHUNK_SHIFT) <= lax.shift_right_logical(row, CHUNK_SHIFT)
            s = jnp.where(seen, s, NEG_INF)
        scores.append(s)
    m = jnp.max(scores[0], axis=1, keepdims=True)
    for s in scores[1:]:
        m = jnp.maximum(m, jnp.max(s, axis=1, keepdims=True))
    ps = [jnp.exp(s - m) for s in scores]
    total = jnp.sum(ps[0], axis=1, keepdims=True)
    for p in ps[1:]:
        total = total + jnp.sum(p, axis=1, keepdims=True)
    inv = 1.0 / total
    return [p * inv for p in ps]


def _attn_probs_t(q, k_ref, lvl):
    scores = []
    for lo, hi, diagonal in _key_ranges(lvl):
        s = lax.dot_general(k_ref[lo:hi, :], q, NT_DIMS, preferred_element_type=F32) * ATT_SCALE
        if diagonal:
            key = lax.broadcasted_iota(jnp.int32, s.shape, 0)
            qry = lax.broadcasted_iota(jnp.int32, s.shape, 1)
            seen = lax.shift_right_logical(key, CHUNK_SHIFT) <= lax.shift_right_logical(qry, CHUNK_SHIFT)
            s = jnp.where(seen, s, NEG_INF)
        scores.append(s)
    m = jnp.max(scores[0], axis=0, keepdims=True)
    for s in scores[1:]:
        m = jnp.maximum(m, jnp.max(s, axis=0, keepdims=True))
    ps = [jnp.exp(s - m) for s in scores]
    total = jnp.sum(ps[0], axis=0, keepdims=True)
    for p in ps[1:]:
        total = total + jnp.sum(p, axis=0, keepdims=True)
    inv = 1.0 / total
    return [p * inv for p in ps]


def _per_query_block(qi, n_blocks, branch):
    for lvl in range(n_blocks):
        pl.when(qi == lvl)(lambda lvl=lvl: branch(lvl))


def _attn_specs(t, g):
    q = pl.BlockSpec((ATT_TQ, g * HEAD_PAD), lambda h, i: (i, h))
    kn = pl.BlockSpec((None, t, g * QK_NOPE), lambda h, i: (0, 0, h))
    kr = pl.BlockSpec((t, LANES), lambda h, i: (0, 0))
    v = pl.BlockSpec((None, t, g * V_HEAD), lambda h, i: (1, 0, h))
    o = pl.BlockSpec((ATT_TQ, g * V_HEAD), lambda h, i: (i, h))
    return q, kn, kr, v, o


def _attn_fwd(q, knv, kr):
    t = q.shape[0]

    def body(q_ref, kn_ref, kr_ref, v_ref, o_ref, k_ref):
        _fill_keys(k_ref, kn_ref, kr_ref)

        def branch(lvl):
            for hh in range(FWD_HEADS):
                vcols = slice(hh * V_HEAD, (hh + 1) * V_HEAD)
                ps = _attn_probs(q_ref[:, hh * HEAD_PAD:(hh + 1) * HEAD_PAD], k_ref.at[hh], lvl)
                o = None
                for p, (lo, hi, _) in zip(ps, _key_ranges(lvl)):
                    part = jnp.dot(p.astype(BF16), v_ref[lo:hi, vcols], preferred_element_type=F32)
                    o = part if o is None else o + part
                o_ref[:, vcols] = o.astype(BF16)

        _per_query_block(pl.program_id(1), t // ATT_TQ, branch)

    qs, kns, krs, vs, os_ = _attn_specs(t, FWD_HEADS)
    return _tc_call(
        body, name="attn_fwd", grid=(N_HEADS // FWD_HEADS, t // ATT_TQ), in_specs=[qs, kns, krs, vs],
        out_specs=os_, out_shape=jax.ShapeDtypeStruct((t, N_HEADS * V_HEAD), BF16),
        scratch_shapes=[pltpu.VMEM((FWD_HEADS, t, HEAD_PAD), BF16)], compiler_params=_cp("parallel", "arbitrary"),
    )(q, knv, kr, knv)


def _attn_bwd(q, knv, kr, do, cos, sin):
    t = q.shape[0]

    def body(q_ref, kn_ref, kr_ref, v_ref, do_ref, c_ref, s_ref, dq_ref, dknv_ref, dkr_ref, k_ref, dk_ref):
        h, qi = pl.program_id(0), pl.program_id(1)
        _fill_keys(k_ref, kn_ref, kr_ref)

        @pl.when(qi == 0)
        def _():
            dknv_ref[1] = jnp.zeros(dknv_ref.shape[1:], F32)
            dk_ref[...] = jnp.zeros_like(dk_ref)

        @pl.when((qi == 0) & (h == 0))
        def _():
            dkr_ref[...] = jnp.zeros_like(dkr_ref)

        def branch(lvl):
            ranges = _key_ranges(lvl)
            for hh in range(BWD_HEADS):
                qcols = slice(hh * HEAD_PAD, (hh + 1) * HEAD_PAD)
                vcols = slice(hh * V_HEAD, (hh + 1) * V_HEAD)
                qv, dov = q_ref[:, qcols], do_ref[:, vcols]
                ps = _attn_probs_t(qv, k_ref.at[hh], lvl)
                dps = [lax.dot_general(v_ref[lo:hi, vcols], dov, NT_DIMS, preferred_element_type=F32)
                       for lo, hi, _ in ranges]
                di = None
                for p, dp in zip(ps, dps):
                    part = jnp.sum(p * dp, axis=0, keepdims=True)
                    di = part if di is None else di + part
                dq = None
                for p, dp, (lo, hi, _) in zip(ps, dps, ranges):
                    ds = (p * (dp - di) * ATT_SCALE).astype(BF16)
                    part = lax.dot_general(ds, k_ref[hh, lo:hi, :], TN_DIMS, preferred_element_type=F32)
                    dq = part if dq is None else dq + part
                    dk_ref[hh, lo:hi, :] += jnp.dot(ds, qv, preferred_element_type=F32)
                    dknv_ref[1, lo:hi, vcols] += jnp.dot(p.astype(BF16), dov, preferred_element_type=F32)
                dq_ref[:, hh * HEAD_PAD:hh * HEAD_PAD + QK_NOPE] = dq[:, :QK_NOPE].astype(BF16)
                dq_ref[:, hh * HEAD_PAD + QK_NOPE:(hh + 1) * HEAD_PAD] = _rope_bwd_math(
                    dq[:, QK_NOPE:], c_ref[...], s_ref[...]).astype(BF16)

        _per_query_block(qi, t // ATT_TQ, branch)

        @pl.when(qi == t // ATT_TQ - 1)
        def _():
            for hh in range(BWD_HEADS):
                dknv_ref[0, :, hh * QK_NOPE:(hh + 1) * QK_NOPE] = dk_ref[hh, :, :QK_NOPE]
                dkr_ref[...] += dk_ref[hh, :, QK_NOPE:]

    qs, kns, krs, vs, os_ = _attn_specs(t, BWD_HEADS)
    tab = pl.BlockSpec((ATT_TQ, LANES), lambda h, i: (i, 0))
    return _tc_call(
        body, name="attn_bwd", grid=(N_HEADS // BWD_HEADS, t // ATT_TQ), in_specs=[qs, kns, krs, vs, os_, tab, tab],
        out_specs=[qs, pl.BlockSpec((2, t, BWD_HEADS * QK_NOPE), lambda h, i: (0, 0, h)), krs],
        out_shape=[jax.ShapeDtypeStruct((t, N_HEADS * HEAD_PAD), BF16),
                   jax.ShapeDtypeStruct((2, t, N_HEADS * QK_NOPE), F32), jax.ShapeDtypeStruct((t, LANES), F32)],
        scratch_shapes=[pltpu.VMEM((BWD_HEADS, t, HEAD_PAD), BF16), pltpu.VMEM((BWD_HEADS, t, HEAD_PAD), F32)],
        compiler_params=_cp("arbitrary", "arbitrary"),
    )(q, knv, kr, knv, do, cos, sin)


def _adam_math(w, g, m, v):
    nm = ADAM_B1 * m + (1.0 - ADAM_B1) * g
    nv = ADAM_B2 * v + (1.0 - ADAM_B2) * (g * g)
    m_hat = nm / (1.0 - ADAM_B1 ** ADAM_STEP)
    v_hat = nv / (1.0 - ADAM_B2 ** ADAM_STEP)
    return -ADAM_LR * (m_hat / (jnp.sqrt(v_hat) + ADAM_EPS) + ADAM_WD * w), nm, nv


def _adamw_small(ws, gs, ms, vs):
    n = len(ws)

    def body(*refs):
        for i in range(n):
            w_ref, g_ref, m_ref, v_ref = (refs[k * n + i] for k in range(4))
            d_ref, nm_ref, nv_ref = (refs[(4 + k) * n + i] for k in range(3))
            d_ref[...], nm_ref[...], nv_ref[...] = _adam_math(w_ref[...], g_ref[...], m_ref[...], v_ref[...])

    shapes = [jax.ShapeDtypeStruct(a.shape, F32) for a in ws]
    res = _tc_call(body, name="adamw_small", out_shape=shapes * 3)(*ws, *gs, *ms, *vs)
    return res[:n], res[n:2 * n], res[2 * n:]


ADAM_SPLIT = 4


def _adamw_shards(ids, items, name):
    n = len(items)

    def body(ids_ref, *refs):
        outs = refs[len(refs) - 4 * n:]
        for i, it in enumerate(items):
            w_ref, m_ref, v_ref, gm_ref, gs_ref = refs[5 * i:5 * i + 5]
            g_ref, d_ref, nm_ref, nv_ref = outs[4 * i:4 * i + 4]
            cols = slice(*it["gcols"]) if it.get("gcols") else slice(None)
            whose = pl.program_id(0) if it.get("owner") is None else it["owner"]
            mine = whose == ids_ref[0]

            @pl.when(mine)
            def _(g_ref=g_ref, gm_ref=gm_ref, cols=cols):
                g_ref[...] = gm_ref[:, cols]

            @pl.when(jnp.logical_not(mine))
            def _(g_ref=g_ref, gs_ref=gs_ref, cols=cols):
                g_ref[...] = gs_ref[:, cols]

            d_ref[...], nm_ref[...], nv_ref[...] = _adam_math(w_ref[...], g_ref[...], m_ref[...], v_ref[...])

    in_specs, out_specs, out_shape, args, carried, aliases = [], [], [], [ids], [], {}
    for i, it in enumerate(items):
        w = it["w"]
        r, c = w.shape[-2:]
        tr = r // 2 // ADAM_SPLIT
        assert tr % 8 == 0, (name, w.shape)
        layer = it.get("layer")
        if layer is None:
            wspec = pl.BlockSpec((tr, c), lambda h, k, ids: (h * ADAM_SPLIT + k, 0))
        else:
            wspec = pl.BlockSpec((None, tr, c), lambda h, k, ids, layer=layer: (layer, h * ADAM_SPLIT + k, 0))
        gc = it["g_mine"].shape[1]
        if it.get("owner") is None:
            gspec = pl.BlockSpec((tr, gc), lambda h, k, ids: (k, 0))
        else:
            gspec = pl.BlockSpec((tr, gc), lambda h, k, ids: (h * ADAM_SPLIT + k, 0))
        in_specs += [wspec] * 3 + [gspec] * 2
        args += [w, it["m"], it["v"], it["g_mine"], it["g_sib"]]
        out_specs += [wspec] * 4
        out_shape += [jax.ShapeDtypeStruct(w.shape, F32)] * 4
        if it.get("prev") is not None:
            for k, p in enumerate(it["prev"]):
                aliases[1 + 5 * n + len(carried)] = 4 * i + k
                carried.append(p)
    res = _tc_call(
        body, name=name, prefetch=1, grid=(2, ADAM_SPLIT), in_specs=in_specs + [ANY] * len(carried),
        out_specs=out_specs, out_shape=out_shape, input_output_aliases=aliases,
        compiler_params=_cp("parallel", "parallel"),
    )(*args, *carried)
    return [res[4 * i:4 * i + 4] for i in range(n)]


def _peer_chip(k_me, j):
    return k_me ^ jnp.where(j == 0, 2, jnp.where(j == 1, 1, 3))


def _pair_sums(ids, gs, ras, name):
    n = len(gs)

    def body(ids_ref, *refs):
        for i in range(n):
            g_ref, ra_ref, o_ref = refs[2 * i], refs[2 * i + 1], refs[2 * n + i]
            o_ref[...] = (g_ref[...].astype(F32) + ra_ref[...].astype(F32)).astype(BF16)

    in_specs, out_specs, out_shape = [], [], []
    for g in gs:
        half, c = g.shape[1] // 2, g.shape[2]
        in_specs += [pl.BlockSpec((None, half, c), lambda j, ids: (_peer_chip(ids[1], j), ids[0], 0)),
                     pl.BlockSpec((None, half, c), lambda j, ids: (_peer_chip(ids[1], j), 0, 0))]
        out_specs.append(pl.BlockSpec((None, half, c), lambda j, ids: (j, 0, 0)))
        out_shape.append(jax.ShapeDtypeStruct((3, half, c), BF16))
    return _tc_call(
        body, name=name, prefetch=1, grid=(3,), in_specs=in_specs, out_specs=out_specs, out_shape=out_shape,
        compiler_params=_cp("parallel"),
    )(ids, *[a for pair in zip(gs, ras) for a in pair])


def _chip_sums(ids, gs, ras, rbs, name):
    n = len(gs)

    def body(ids_ref, *refs):
        for i in range(n):
            g_ref, ra_ref, rb_ref, o_ref = refs[3 * i], refs[3 * i + 1], refs[3 * i + 2], refs[3 * n + i]
            acc = g_ref[...].astype(F32) + ra_ref[...].astype(F32)
            for j in range(3):
                acc = acc + rb_ref[j].astype(F32)
            o_ref[...] = acc

    in_specs, out_specs, out_shape = [], [], []
    for g in gs:
        half, c = g.shape[1] // 2, g.shape[2]
        in_specs += [pl.BlockSpec((None, half, c), lambda i, ids: (ids[1], ids[0], 0)),
                     pl.BlockSpec((None, half, c), lambda i, ids: (ids[1], 0, 0)),
                     pl.BlockSpec((3, half, c), lambda i, ids: (0, 0, 0))]
        out_specs.append(pl.BlockSpec((half, c), lambda i, ids: (0, 0)))
        out_shape.append(jax.ShapeDtypeStruct((half, c), F32))
    return _tc_call(
        body, name=name, prefetch=1, grid=(1,), in_specs=in_specs, out_specs=out_specs, out_shape=out_shape,
        compiler_params=_cp("arbitrary"),
    )(ids, *[a for trio in zip(gs, ras, rbs) for a in trio])


def _position():
    x, y, c = lax.axis_index("x"), lax.axis_index("y"), lax.axis_index("c")
    chips = [(1 - x, y), (x, 1 - y), (1 - x, 1 - y)]
    return x, y, c, chips


def _shard_half(ref, wm, h):
    if wm.kind == "tiny":
        return ref
    if wm.nl == 2:
        return ref.at[h]
    return ref.at[pl.ds(pl.multiple_of(h * (wm.k // 2), 16), wm.k // 2), :]


def _region(full, wm, s, h):
    if wm.kind == "tiny":
        return full.at[s]
    cols = pl.ds(pl.multiple_of(s * wm.n, LANES), wm.n) if wm.kind == "col" else slice(None)
    if wm.nl == 2:
        rows = pl.ds(pl.multiple_of(s * wm.k, 16), wm.k) if wm.kind == "row" else slice(None)
        return full.at[slice(None) if h is None else h, rows, cols]
    if wm.kind == "col":
        rows = slice(None) if h is None else pl.ds(pl.multiple_of(h * (wm.k // 2), 16), wm.k // 2)
    elif h is None:
        rows = pl.ds(pl.multiple_of(s * wm.k, 16), wm.k)
    else:
        rows = pl.ds(pl.multiple_of(s * wm.k + h * (wm.k // 2), 16), wm.k // 2)
    return full.at[rows, cols]


def _full_shape(wm):
    if wm.kind == "tiny":
        return (N_CHIPS, wm.k, wm.n)
    shape = (wm.k, N_CHIPS * wm.n) if wm.kind == "col" else (N_CHIPS * wm.k, wm.n)
    return shape if wm.nl == 1 else (wm.nl,) + shape


def _handshake(peers):
    barrier = pltpu.get_barrier_semaphore()
    for peer in peers:
        pl.semaphore_signal(barrier, inc=1, device_id=peer, device_id_type=MESH)
    pl.semaphore_wait(barrier, len(peers))


def _all_gather_group(gi, shards):
    wms = AG_GROUPS[gi]
    nw = len(wms)

    def body(*refs):
        sh, full = refs[:nw], refs[nw:2 * nw]
        ici_s, ici_r, pass_s, pass_r, own_s, own_r = refs[2 * nw:]
        x, y, c, _ = _position()
        me, sibling = 2 * x + y, (x, y, 1 - c)
        first, second, diagonal = (x ^ (1 - c), y ^ c), (x ^ c, y ^ (1 - c)), (1 - x, 1 - y)
        chip_id = lambda chip: 2 * chip[0] + chip[1]
        _handshake([(*first, c), (*second, c), sibling])

        def rcopy(src, dst, s_sem, r_sem, to):
            return pltpu.make_async_remote_copy(src_ref=src, dst_ref=dst, send_sem=s_sem, recv_sem=r_sem,
                                                device_id=to, device_id_type=MESH)

        started = []

        def go(cp):
            cp.start()
            started.append(cp)

        for i, wm in enumerate(wms):
            half, dst = _shard_half(sh[i], wm, c), _region(full[i], wm, me, c)
            go(rcopy(half, dst, ici_s.at[i, 0], ici_r.at[i, 0], (*first, c)))
            go(rcopy(half, dst, ici_s.at[i, 1], ici_r.at[i, 1], (*second, c)))
            go(rcopy(sh[i], _region(full[i], wm, me, None), own_s.at[i], own_r.at[i], sibling))
        for i, wm in enumerate(wms):
            got = _region(full[i], wm, chip_id(first), c)
            rcopy(got, got, ici_s.at[i, 0], ici_r.at[i, 0], sibling).wait_recv()
            go(rcopy(got, got, ici_s.at[i, 2], ici_r.at[i, 2], (*second, c)))
            if wm.kind != "tiny":
                go(rcopy(got, got, pass_s.at[i, 0], pass_r.at[i, 0], sibling))
        for i, wm in enumerate(wms):
            for j, chip in ((1, second), (2, diagonal)):
                got = _region(full[i], wm, chip_id(chip), c)
                rcopy(got, got, ici_s.at[i, j], ici_r.at[i, j], sibling).wait_recv()
                if wm.kind != "tiny":
                    go(rcopy(got, got, pass_s.at[i, j], pass_r.at[i, j], sibling))
        for i, wm in enumerate(wms):
            mine = _region(full[i], wm, me, None)
            rcopy(mine, mine, own_s.at[i], own_r.at[i], sibling).wait_recv()
            if wm.kind != "tiny":
                for j, chip in ((0, second), (1, first), (2, diagonal)):
                    got = _region(full[i], wm, chip_id(chip), 1 - c)
                    rcopy(got, got, pass_s.at[i, j], pass_r.at[i, j], sibling).wait_recv()
        for cp in started:
            cp.wait_send()

    return pl.kernel(
        body, out_type=[jax.ShapeDtypeStruct(_full_shape(wm), s.dtype) for wm, s in zip(wms, shards)],
        mesh=plsc.ScalarSubcoreMesh(axis_name="sequencer", num_cores=1), name=f"ag_group{gi}",
        scratch_types=[pltpu.SemaphoreType.DMA((nw, 3))] * 4 + [pltpu.SemaphoreType.DMA((nw,))] * 2,
        compiler_params=pltpu.CompilerParams(collective_id=gi),
    )(*shards)


def _sequencer_call(body, name, cid, out_types, scratch, args):
    return pl.kernel(
        body, out_type=out_types, mesh=plsc.ScalarSubcoreMesh(axis_name="sequencer", num_cores=1), name=name,
        scratch_types=scratch, compiler_params=pltpu.CompilerParams(collective_id=cid),
    )(*args)


def _pair_exchange(gs, tag, cid):
    n = len(gs)

    def body(*refs):
        g, out, send_sems, recv_sems = refs[:n], refs[n:2 * n], refs[2 * n], refs[2 * n + 1]
        x, y, c, _ = _position()
        _handshake([(x, y, 1 - c)])
        cps = []
        for i in range(n):
            half = g[i].shape[1] // 2
            cps.append(pltpu.make_async_remote_copy(
                src_ref=g[i].at[:, pl.ds(pl.multiple_of((1 - c) * half, 16), half), :], dst_ref=out[i],
                send_sem=send_sems.at[i], recv_sem=recv_sems.at[i], device_id=(x, y, 1 - c), device_id_type=MESH))
            cps[-1].start()
        for cp in cps:
            cp.wait()

    return _sequencer_call(
        body, f"rs_pair_exchange{tag}", cid,
        [jax.ShapeDtypeStruct((a.shape[0], a.shape[1] // 2, a.shape[2]), a.dtype) for a in gs],
        [pltpu.SemaphoreType.DMA((n,)), pltpu.SemaphoreType.DMA((n,))], gs)


def _chip_exchange(ss, tag, cid):
    n = len(ss)

    def body(*refs):
        s, out, send_sems, recv_sems = refs[:n], refs[n:2 * n], refs[2 * n], refs[2 * n + 1]
        x, y, c, chips = _position()
        _handshake([(*chip, c) for chip in chips])
        cps = []
        for i in range(n):
            for j, chip in enumerate(chips):
                cps.append(pltpu.make_async_remote_copy(
                    src_ref=s[i].at[j], dst_ref=out[i].at[j], send_sem=send_sems.at[i, j], recv_sem=recv_sems.at[i, j],
                    device_id=(*chip, c), device_id_type=MESH))
                cps[-1].start()
        for cp in cps:
            cp.wait()

    return _sequencer_call(
        body, f"rs_chip_exchange{tag}", cid, [jax.ShapeDtypeStruct(a.shape, a.dtype) for a in ss],
        [pltpu.SemaphoreType.DMA((n, 3)), pltpu.SemaphoreType.DMA((n, 3))], ss)


def _pair_swap(g8s, tag, cid):
    n = len(g8s)

    def body(*refs):
        g, out, send_sems, recv_sems = refs[:n], refs[n:2 * n], refs[2 * n], refs[2 * n + 1]
        x, y, c, _ = _position()
        _handshake([(x, y, 1 - c)])
        cps = []
        for i in range(n):
            cps.append(pltpu.make_async_remote_copy(
                src_ref=g[i], dst_ref=out[i], send_sem=send_sems.at[i], recv_sem=recv_sems.at[i],
                device_id=(x, y, 1 - c), device_id_type=MESH))
            cps[-1].start()
        for cp in cps:
            cp.wait()

    return _sequencer_call(
        body, f"rs_pair_swap{tag}", cid, [jax.ShapeDtypeStruct(a.shape, a.dtype) for a in g8s],
        [pltpu.SemaphoreType.DMA((n,)), pltpu.SemaphoreType.DMA((n,))], g8s)


def _pair_swap_now(g8s):
    n = len(g8s)

    def body(*refs):
        g, out, send_sems, recv_sems = refs[:n], refs[n:2 * n], refs[2 * n], refs[2 * n + 1]
        x, y, c, _ = _position()
        cps = []
        for i in range(n):
            cps.append(pltpu.make_async_remote_copy(
                src_ref=g[i], dst_ref=out[i], send_sem=send_sems.at[i], recv_sem=recv_sems.at[i],
                device_id=(x, y, 1 - c), device_id_type=MESH))
            cps[-1].start()
        for cp in cps:
            cp.wait()

    return _tc_call(
        body, name="rs_pair_swap_last", in_specs=[ANY] * n, out_specs=[ANY] * n,
        out_shape=[jax.ShapeDtypeStruct(a.shape, a.dtype) for a in g8s],
        scratch_shapes=[pltpu.SemaphoreType.DMA((n,)), pltpu.SemaphoreType.DMA((n,))],
    )(*g8s)


def _all_reduce_small(vecs, name):
    n = len(vecs)

    def body(*refs):
        v, o, gath = refs[:n], refs[n:2 * n], refs[2 * n:3 * n]
        send_sems, recv_sems = refs[3 * n], refs[3 * n + 1]
        x, y, c, _ = _position()
        me = 4 * x + 2 * y + c
        cps = []
        for i in range(n):
            gath[i][me] = v[i][...]
            for rel in range(1, N_DEV):
                peer = (x ^ (rel >> 2), y ^ ((rel >> 1) & 1), c ^ (rel & 1))
                cps.append(pltpu.make_async_remote_copy(
                    src_ref=v[i], dst_ref=gath[i].at[me], send_sem=send_sems.at[i, rel - 1],
                    recv_sem=recv_sems.at[i, rel - 1], device_id=peer, device_id_type=MESH))
        for cp in cps:
            cp.start()
        for i in range(n):
            for rel in range(1, N_DEV):
                pltpu.make_async_remote_copy(
                    src_ref=v[i], dst_ref=gath[i].at[me ^ rel], send_sem=send_sems.at[i, rel - 1],
                    recv_sem=recv_sems.at[i, rel - 1], device_id=(x, y, c), device_id_type=MESH).wait_recv()
        for cp in cps:
            cp.wait_send()
        for i in range(n):
            acc = gath[i][0]
            for d in range(1, N_DEV):
                acc = acc + gath[i][d]
            o[i][...] = acc

    vm = pl.BlockSpec(memory_space=pltpu.VMEM)
    return _tc_call(
        body, name=name, in_specs=[vm] * n, out_specs=[vm] * n,
        out_shape=[jax.ShapeDtypeStruct(a.shape, F32) for a in vecs],
        scratch_shapes=[pltpu.VMEM((N_DEV,) + a.shape, F32) for a in vecs]
        + [pltpu.SemaphoreType.DMA((n, N_DEV - 1)), pltpu.SemaphoreType.DMA((n, N_DEV - 1))],
    )(*vecs)


def _rope_tables(positions):
    half = QK_ROPE // 2
    inv_freq = 1.0 / (ROPE_THETA ** (jnp.arange(half, dtype=F32) / half))
    ang = positions.astype(F32)[:, None] * inv_freq
    zeros = jnp.zeros((positions.shape[0], LANES - QK_ROPE), F32)
    cos, sin = jnp.cos(ang), jnp.sin(ang)
    return jnp.concatenate([cos, cos, zeros], axis=1), jnp.concatenate([sin, sin, zeros], axis=1)


def _local_step(x, positions, tgt, wf, small, rs):
    cos, sin = _rope_tables(positions)
    w_in, w_out = wf["sc_w_in"], wf["sc_w_out"]
    w_ups, w_downs = (wf["ffn_w_up0"], wf["ffn_w_up1"]), (wf["ffn_w_down0"], wf["ffn_w_down1"])
    w_kv, w_ukv, w_dq, w_uq, w_o = wf["w_kv"], wf["w_ukv"], wf["w_dq"], wf["w_uq"], wf["w_o"]
    attn_norm, ffn_norm = small["attn_norm"], small["ffn_norm"]
    conv_b = small["ffn_conv_b"]

    def ffn_fwd(h, hf, l, then):
        up, a = _ffn_up_gate(hf, w_ups[l], small["ffn_conv_w"][l], conv_b[l:l + 1], f"ffn{l}_up_gate")
        return then(a, w_downs[l], h), (hf, up, a)

    def ffn_bwd(h, dh_out, dh_out_b, l, saved, gi, hooks):
        run = lambda stage: hooks.get(stage, lambda: None)()
        hf, up, a = saved
        d_down = _tn(f"ffn{l}_down_dw", a, dh_out_b, BF16)
        run("down_dw")
        dup, d_cw, d_cb = _gate_bwd(up, small["ffn_conv_w"][l], conv_b[l:l + 1], dh_out_b, w_downs[l],
                                    f"ffn{l}_gate_bwd")
        run("gate_bwd")
        d_up = _dw_ffn_up(f"ffn{l}_up_dw", hf, dup)
        rs.start(gi, {f"ffn_w_down{l}": d_down.reshape(N_CHIPS, F_FF // N_CHIPS, D), f"ffn_w_up{l}": d_up})
        run("up_dw")
        dh, dh_b, d_norm = _dx_norm_bwd(f"ffn{l}_up_dx", dup, w_ups[l], h, ffn_norm[l:l + 1], dh_out)
        run("up_dx")
        return dh, dh_b, d_cw, d_cb, d_norm

    hn0 = _rms_fwd(x, attn_norm[0:1], "attn0_norm")
    z = _nn_parts("sc_in", hn0, w_in, 3, BF16)
    mix = _scmix_fwd(z, small["sc_conv_w"])
    h1, hf0 = _nn_add_norm("sc_out", mix, w_out, x, ffn_norm[0:1])
    h2, ffn0_saved = ffn_fwd(h1, hf0, 0, lambda a, w, h: _nn("ffn0_down", a, w, F32, add=h))

    hn1, hk, cq_pre, cq, q, kvpre, ckv, kr, knv = _attn_prep(
        h2, attn_norm[1:2], small["kv_in_norm"], w_dq, small["q_latent_norm"], w_uq, w_kv, small["kv_latent_norm"],
        w_ukv, cos, sin)
    o = _attn_fwd(q, knv, kr)
    h3, hf1 = _nn_add_norm("attn_out", o, w_o, h2, ffn_norm[1:2])
    (loss, dh4, dh4_b, d_final), ffn1_saved = ffn_fwd(
        h3, hf1, 1, lambda a, w, h: _nn_add_loss("ffn1_down_loss", a, w, h, small["final_norm"], tgt))

    rows = D // N_CHIPS
    dh3, dh3_b, d_cw1, d_cb1, d_fn1 = ffn_bwd(h3, dh4, dh4_b, 1, ffn1_saved, 0, {})

    do = _nt("attn_out_dx", dh3_b, w_o, BF16)
    d_wo = _tn("attn_out_dw", o, dh3_b, BF16)
    rs.pair_sums(0)
    dq, dknv, dkr = _attn_bwd(q, knv, kr, do, cos, sin)
    rs.chip_sums(0)
    dh2, dh2_b, d_wuq, d_wdq, d_wukv, d_wkv, d_an1, d_kvin, d_qln, d_kvln = _attn_prep_bwd(
        dq, dknv, dkr, dh3, h2, hn1, hk, cq_pre, cq, kvpre, ckv, attn_norm[1:2], small["kv_in_norm"], w_dq,
        small["q_latent_norm"], w_uq, w_kv, small["kv_latent_norm"], w_ukv, cos, sin)
    rs.finish(0)
    by_owner = lambda dw: dw.reshape(dw.shape[0], N_CHIPS, -1).transpose(1, 0, 2)
    rs.start(1, {
        "w_o": d_wo.reshape(N_CHIPS, rows, D), "w_uq": by_owner(d_wuq), "w_dq": d_wdq.reshape(N_CHIPS, rows, Q_LORA),
        "w_ukv": by_owner(d_wukv.reshape(2 * KV_LORA, -1)).reshape(N_CHIPS, 2 * KV_LORA, -1),
        "w_kv": d_wkv.reshape(N_CHIPS, rows, KVP),
    })

    dh1, dh1_b, d_cw0, d_cb0, d_fn0 = ffn_bwd(h1, dh2, dh2_b, 0, ffn0_saved, 2, {
        "down_dw": lambda: rs.pair_sums(1), "gate_bwd": lambda: rs.chip_sums(1), "up_dw": lambda: rs.finish(1),
        "up_dx": lambda: rs.pair_sums(2)})

    d_wout = _tn("sc_out_dw", mix, dh1_b, BF16)
    dmix = _nt("sc_out_dx", dh1_b, w_out, BF16)
    dz, d_scw = _scmix_bwd(z, small["sc_conv_w"], dmix)
    d_win = _dw_sc_in(hn0, dz)
    rs.start(3, {"sc_w_out": d_wout.reshape(N_CHIPS, rows, D), "sc_w_in": d_win})
    dx, _, d_an0 = _dx_norm_bwd("sc_in_dx", dz, w_in, x, attn_norm[0:1], dh1)

    small_g = {
        "attn_norm": jnp.concatenate([d_an0, d_an1]), "ffn_norm": jnp.concatenate([d_fn0, d_fn1]),
        "final_norm": d_final, "kv_in_norm": d_kvin, "kv_latent_norm": d_kvln, "q_latent_norm": d_qln,
        "ffn_conv_b": jnp.concatenate([d_cb0, d_cb1]), "sc_conv_w": d_scw, "ffn_conv_w0": d_cw0, "ffn_conv_w1": d_cw1,
    }
    return loss, dx, small_g


RS_GROUPS = (("ffn_w_down1", "ffn_w_up1"), ("w_o", "w_uq", "w_dq", "w_ukv", "w_kv"),
             ("ffn_w_down0", "ffn_w_up0"), ("sc_w_out", "sc_w_in"))


class _ReduceScatter:
    def __init__(self, ids, finish):
        self.ids, self.grads, self.step, self.mine, self.sib, self.finish = ids, {}, {}, {}, {}, finish

    def _cid(self, gi):
        return len(AG_GROUPS) + 3 * gi

    def start(self, gi, grads):
        self.grads.update(grads)
        own = [grads[n] for n in RS_GROUPS[gi]]
        self.step[gi] = (own, _pair_exchange(own, gi, self._cid(gi)))

    def pair_sums(self, gi):
        own, ra = self.step[gi]
        sums = _pair_sums(self.ids, own, ra, f"rs_pair_sums{gi}")
        self.step[gi] = (own, ra, _chip_exchange(sums, gi, self._cid(gi) + 1))

    def chip_sums(self, gi):
        own, ra, rb = self.step[gi]
        mine = _chip_sums(self.ids, own, ra, rb, f"rs_chip_sums{gi}")
        self.mine.update(zip(RS_GROUPS[gi], mine))
        last = gi == len(RS_GROUPS) - 1
        swapped = _pair_swap_now(mine) if last else _pair_swap(mine, gi, self._cid(gi) + 2)
        self.sib.update(zip(RS_GROUPS[gi], swapped))


SMALL_REPL = ("attn_norm", "ffn_norm", "final_norm", "kv_in_norm", "kv_latent_norm", "q_latent_norm", "ffn_conv_b")


def _pad_heads(w_uq):
    per_head = w_uq.reshape(Q_LORA, -1, QK_NOPE + QK_ROPE)
    return jnp.pad(per_head, ((0, 0), (0, 0), (0, HEAD_PAD - QK_NOPE - QK_ROPE))).reshape(Q_LORA, -1)


def _pack_kv(w_dkv, w_kr):
    return jnp.concatenate([w_dkv, w_kr, jnp.zeros((w_kr.shape[0], LANES - QK_ROPE), w_kr.dtype)], axis=1)


def kernel(x, positions, attn_norm, ffn_norm, final_norm, sc_w_in, sc_conv_w, sc_w_out, kv_in_norm, w_dkv, kv_latent_norm, w_kr, w_uk, w_uv, w_dq, q_latent_norm, w_uq, w_o, ffn_w_up, ffn_conv_w, ffn_conv_b, ffn_w_down, loss_target, m_attn_norm, m_ffn_norm, m_final_norm, m_sc_w_in, m_sc_conv_w, m_sc_w_out, m_kv_in_norm, m_w_dkv, m_kv_latent_norm, m_w_kr, m_w_uk, m_w_uv, m_w_dq, m_q_latent_norm, m_w_uq, m_w_o, m_ffn_w_up, m_ffn_conv_w, m_ffn_conv_b, m_ffn_w_down, v_attn_norm, v_ffn_norm, v_final_norm, v_sc_w_in, v_sc_conv_w, v_sc_w_out, v_kv_in_norm, v_w_dkv, v_kv_latent_norm, v_w_kr, v_w_uk, v_w_uv, v_w_dq, v_q_latent_norm, v_w_uq, v_w_o, v_ffn_w_up, v_ffn_conv_w, v_ffn_conv_b, v_ffn_w_down):
    names = ("attn_norm", "ffn_norm", "final_norm", "sc_w_in", "sc_conv_w", "sc_w_out", "kv_in_norm", "w_dkv",
             "kv_latent_norm", "w_kr", "w_uk", "w_uv", "w_dq", "q_latent_norm", "w_uq", "w_o", "ffn_w_up",
             "ffn_conv_w", "ffn_conv_b", "ffn_w_down")
    w = dict(zip(names, (attn_norm, ffn_norm, final_norm, sc_w_in, sc_conv_w, sc_w_out, kv_in_norm, w_dkv,
                         kv_latent_norm, w_kr, w_uk, w_uv, w_dq, q_latent_norm, w_uq, w_o, ffn_w_up,
                         ffn_conv_w, ffn_conv_b, ffn_w_down)))
    m = dict(zip(names, (m_attn_norm, m_ffn_norm, m_final_norm, m_sc_w_in, m_sc_conv_w, m_sc_w_out, m_kv_in_norm,
                         m_w_dkv, m_kv_latent_norm, m_w_kr, m_w_uk, m_w_uv, m_w_dq, m_q_latent_norm, m_w_uq, m_w_o,
                         m_ffn_w_up, m_ffn_conv_w, m_ffn_conv_b, m_ffn_w_down)))
    v = dict(zip(names, (v_attn_norm, v_ffn_norm, v_final_norm, v_sc_w_in, v_sc_conv_w, v_sc_w_out, v_kv_in_norm,
                         v_w_dkv, v_kv_latent_norm, v_w_kr, v_w_uk, v_w_uv, v_w_dq, v_q_latent_norm, v_w_uq, v_w_o,
                         v_ffn_w_up, v_ffn_conv_w, v_ffn_conv_b, v_ffn_w_down)))

    _ORDER[0] = None
    ix, iy, ic = lax.axis_index("x"), lax.axis_index("y"), lax.axis_index("c")
    chip = 2 * ix + iy
    ids = jnp.stack([ic, chip]).astype(jnp.int32)

    def shards_of(t):
        return {
            "sc_w_in": t["sc_w_in"][0], "sc_w_out": t["sc_w_out"][0], "ffn_w_up": t["ffn_w_up"],
            "ffn_w_down": t["ffn_w_down"], "w_kv": _pack_kv(t["w_dkv"], t["w_kr"]),
            "w_ukv": jnp.stack([t["w_uk"], t["w_uv"]]), "w_dq": t["w_dq"][0], "w_uq": _pad_heads(t["w_uq"][0]),
            "w_o": t["w_o"][0],
        }

    ws, ms, vs = shards_of(w), shards_of(m), shards_of(v)

    def ag_shard(name):
        if name == "sc_conv_w":
            return sc_conv_w[0]
        if name == "ffn_conv_w":
            return ffn_conv_w.reshape(6, -1)
        if name[:-1] in ("ffn_w_up", "ffn_w_down"):
            return ws[name[:-1]][int(name[-1])].astype(BF16)
        return ws[name].astype(BF16)

    wf = {}
    for gi, wms in enumerate(AG_GROUPS):
        fulls = _all_gather_group(gi, [ag_shard(wm.name) for wm in wms])
        wf.update({wm.name: f for wm, f in zip(wms, fulls)})
    small = {
        "attn_norm": attn_norm, "ffn_norm": ffn_norm, "final_norm": final_norm[None], "kv_in_norm": kv_in_norm[None],
        "kv_latent_norm": kv_latent_norm[None], "q_latent_norm": q_latent_norm, "ffn_conv_b": ffn_conv_b,
        "sc_conv_w": wf["sc_conv_w"].transpose(1, 0, 2).reshape(3, D),
        "ffn_conv_w": wf["ffn_conv_w"].reshape(N_CHIPS, 2, 3, -1).transpose(1, 2, 0, 3).reshape(2, 3, F_FF),
    }

    res = {}

    held = {
        "ffn_w_up0": [("ffn_w_up", dict(layer=0))], "ffn_w_up1": [("ffn_w_up", dict(layer=1))],
        "ffn_w_down0": [("ffn_w_down", dict(layer=0))], "ffn_w_down1": [("ffn_w_down", dict(layer=1))],
        "sc_w_in": [("sc_w_in", dict(layer=0))], "sc_w_out": [("sc_w_out", dict(layer=0))],
        "w_dq": [("w_dq", dict(layer=0))], "w_o": [("w_o", dict(layer=0))], "w_uq": [("w_uq", {})],
        "w_kv": [("w_dkv", dict(gcols=(0, KV_LORA))), ("w_kr", dict(gcols=(KV_LORA, KV_LORA + QK_ROPE)))],
        "w_ukv": [("w_uk", dict(owner=0)), ("w_uv", dict(owner=1))],
    }

    def adamw_group(gi):
        items = []
        for key in RS_GROUPS[gi]:
            for n, opts in held[key]:
                w_, m_, v_ = (ws[n], ms[n], vs[n]) if n == "w_uq" else (w[n], m[n], v[n])
                items.append(dict(name=n, w=w_, m=m_, v=v_, g_mine=rs.mine[key], g_sib=rs.sib[key],
                                  prev=res.get(n) if "layer" in opts and w_.shape[0] > 1 else None, **opts))
        for it, out in zip(items, _adamw_shards(ids, items, f"adamw_group{gi}")):
            res[it["name"]] = out

    rs = _ReduceScatter(ids, adamw_group)
    loss, dx, small_g = _local_step(x[0], positions[0], loss_target[0], wf, small, rs)

    rs.chip_sums(2)
    rs.pair_sums(3)

    s_names = list(small_g)
    reduced = _all_reduce_small([small_g[n] for n in s_names] + [loss], "ar_small")
    sg, loss_out = dict(zip(s_names, reduced[:-1])), reduced[-1][0, 0]

    mine = lambda full: lax.dynamic_slice_in_dim(full, chip * (full.shape[1] // N_CHIPS), full.shape[1] // N_CHIPS, 1)
    row = lambda n: (lambda t: t[n][None])
    small_2d = {
        "attn_norm": (sg["attn_norm"], lambda t: t["attn_norm"]), "ffn_norm": (sg["ffn_norm"], lambda t: t["ffn_norm"]),
        "final_norm": (sg["final_norm"], row("final_norm")), "kv_in_norm": (sg["kv_in_norm"], row("kv_in_norm")),
        "kv_latent_norm": (sg["kv_latent_norm"], row("kv_latent_norm")),
        "q_latent_norm": (sg["q_latent_norm"], lambda t: t["q_latent_norm"]),
        "ffn_conv_b": (sg["ffn_conv_b"], lambda t: t["ffn_conv_b"]),
        "sc_conv_w": (mine(sg["sc_conv_w"]), lambda t: t["sc_conv_w"][0]),
        "ffn_conv_w0": (mine(sg["ffn_conv_w0"]), lambda t: t["ffn_conv_w"][0]),
        "ffn_conv_w1": (mine(sg["ffn_conv_w1"]), lambda t: t["ffn_conv_w"][1]),
    }
    s_keys = list(small_2d)
    small_grads = [small_2d[k][0] for k in s_keys]
    views = lambda tree: [small_2d[k][1](tree) for k in s_keys]
    small_res = _adamw_small(views(w), small_grads, views(m), views(v))

    def restore(vals):
        by = dict(zip(s_keys, vals))
        out = {n: by[n].reshape(w[n].shape) for n in SMALL_REPL}
        out["sc_conv_w"] = by["sc_conv_w"][None]
        out["ffn_conv_w"] = jnp.stack([by["ffn_conv_w0"], by["ffn_conv_w1"]])
        return out

    grads = restore(small_grads)
    rs.finish(2)
    rs.chip_sums(3)
    rs.finish(3)
    outs = [grads, {}, {}, {}]
    for k, dst in enumerate(outs):
        for n in res:
            dst[n] = res[n][k]
        unpadded = res["w_uq"][k].reshape(Q_LORA, -1, HEAD_PAD)[:, :, :QK_NOPE + QK_ROPE]
        dst["w_uq"] = unpadded.reshape(w_uq.shape)
    grads, delta, new_m, new_v = outs
    for vals, dst in zip(small_res, (delta, new_m, new_v)):
        dst.update(restore(vals))

    _ORDER[0] = None
    return (loss_out, dx[None], *[grads[n] for n in names], *[delta[n] for n in names],
            *[new_m[n] for n in names], *[new_v[n] for n in names])
```

```python
from typing import NamedTuple

import jax
import jax.numpy as jnp
from jax import lax
from jax.experimental import pallas as pl
from jax.experimental.pallas import tpu as pltpu
from jax.experimental.pallas import tpu_sc as plsc

F32 = jnp.float32
BF16 = jnp.bfloat16

T = 2048
D = 1024
F_FF = 2816
N_HEADS = 8
QK_NOPE = 128
QK_ROPE = 64
V_HEAD = 128
Q_LORA = 384
KV_LORA = 256
CHUNK_SHIFT = 6
ROPE_THETA = 10000.0
EPS = 1e-6
NEG_INF = -1e30
HEAD_PAD = 256
KVP = KV_LORA + 128

ADAM_LR = 0.001
ADAM_B1 = 0.9
ADAM_B2 = 0.999
ADAM_EPS = 1e-08
ADAM_WD = 0.01
ADAM_STEP = 10

N_CHIPS = 4
N_DEV = 8
LANES = 128
TC = 256
V7X_VMEM_LIMIT = 56 * 1024 * 1024

MESH = pl.DeviceIdType.MESH
ANY = pl.BlockSpec(memory_space=pl.ANY)


class _W(NamedTuple):
    name: str
    kind: str
    nl: int
    k: int
    n: int


AG_GROUPS = (
    (_W("sc_w_in", "col", 1, D, 3 * D // N_CHIPS), _W("sc_conv_w", "tiny", 1, 3, D // N_CHIPS),
     _W("ffn_conv_w", "tiny", 1, 6, F_FF // N_CHIPS), _W("sc_w_out", "row", 1, D // N_CHIPS, D)),
    (_W("ffn_w_up0", "col", 1, D, 2 * F_FF // N_CHIPS),),
    (_W("ffn_w_down0", "row", 1, F_FF // N_CHIPS, D),),
    (_W("w_kv", "row", 1, D // N_CHIPS, KVP), _W("w_ukv", "col", 2, KV_LORA, N_HEADS * QK_NOPE // N_CHIPS),
     _W("w_dq", "row", 1, D // N_CHIPS, Q_LORA),
     _W("w_uq", "col", 1, Q_LORA, N_HEADS * HEAD_PAD // N_CHIPS),
     _W("w_o", "row", 1, N_HEADS * V_HEAD // N_CHIPS, D)),
    (_W("ffn_w_up1", "col", 1, D, 2 * F_FF // N_CHIPS), _W("ffn_w_down1", "row", 1, F_FF // N_CHIPS, D)),
)


def _cp(*sem):
    return pltpu.CompilerParams(dimension_semantics=sem, vmem_limit_bytes=V7X_VMEM_LIMIT)


_ORDER = [None]


def _tc_call(body, *, name, out_shape, in_specs=None, out_specs=None, grid=(), scratch_shapes=(), prefetch=0,
             input_output_aliases=None, compiler_params=None):
    def run(*args):
        specs = [pl.BlockSpec(memory_space=pltpu.VMEM)] * (len(args) - prefetch) if in_specs is None else list(in_specs)
        inner, dep = body, _ORDER[0]
        if dep is not None:
            unread = prefetch + len(specs)
            specs, args = specs + [ANY], (*args, dep)

            def inner(*refs):
                return body(*refs[:unread], *refs[unread + 1:])

        kwargs = dict(name=name, out_shape=out_shape, input_output_aliases=input_output_aliases or {},
                      compiler_params=compiler_params)
        if prefetch:
            kwargs["grid_spec"] = pltpu.PrefetchScalarGridSpec(
                num_scalar_prefetch=prefetch, grid=grid, in_specs=specs, out_specs=out_specs,
                scratch_shapes=scratch_shapes)
        else:
            kwargs.update(grid=grid, in_specs=specs, scratch_shapes=scratch_shapes)
            if out_specs is not None:
                kwargs["out_specs"] = out_specs
        out = pl.pallas_call(inner, **kwargs)(*args)
        _ORDER[0] = out[0] if isinstance(out, (list, tuple)) else out
        return out

    return run


def _tile(n, cands):
    for c in cands:
        if n % c == 0:
            return c
    raise ValueError(f"no tile for {n}")


NN_DIMS = (((1,), (0,)), ((), ()))
NT_DIMS = (((1,), (1,)), ((), ()))
TN_DIMS = (((0,), (0,)), ((), ()))
M_TILES = (1024, 512, 384, 256, 128)
N_TILES = (1408, 1024, 768, 512, 384, 256, 128)
MM_BLOCK_BYTES = 36 * 1024 * 1024


def _fit(m, n, block_bytes, m_tiles=M_TILES, n_tiles=N_TILES):
    for tm in [c for c in m_tiles if m % c == 0]:
        for tn in [c for c in n_tiles if n % c == 0]:
            if 2 * block_bytes(tm, tn) + 4 * tm * tn <= MM_BLOCK_BYTES:
                return tm, tn
    raise ValueError(f"no tiles for {m} x {n}")


def _size(x):
    return x.dtype.itemsize


def _mm(name, a, b, dims, grid, a_spec, b_spec, o_spec, o_sds, add=None, red=None, acc_shape=None):
    n_red = None if red is None else grid[red]

    def body(*refs):
        a_ref, b_ref = refs[0], refs[1]
        add_ref = refs[2] if add is not None else None
        o_ref = refs[3] if add is not None else refs[2]
        part = lax.dot_general(a_ref[...].astype(BF16), b_ref[...].astype(BF16), dims, preferred_element_type=F32)
        if red is None:
            if add is not None:
                part = part + add_ref[...]
            o_ref[...] = part.astype(o_ref.dtype)
            return
        acc_ref = refs[-1]
        r = pl.program_id(red)

        @pl.when(r == 0)
        def _():
            acc_ref[...] = part

        @pl.when(r > 0)
        def _():
            acc_ref[...] += part

        @pl.when(r == n_red - 1)
        def _():
            o_ref[...] = acc_ref[...].astype(o_ref.dtype)

    sem = tuple("arbitrary" if ax == red else "parallel" for ax in range(len(grid)))
    in_specs = [a_spec, b_spec] + ([o_spec] if add is not None else [])
    args = (a, b) + ((add,) if add is not None else ())
    return _tc_call(
        body, name=name, grid=grid, in_specs=in_specs, out_specs=o_spec, out_shape=o_sds,
        scratch_shapes=[] if red is None else [pltpu.VMEM(acc_shape, F32)], compiler_params=_cp(*sem),
    )(*args)


def _nn(name, a, b, out_dtype, add=None, lead=None):
    (m, k), n = a.shape, b.shape[-1]
    osz = jnp.dtype(out_dtype).itemsize + (4 if add is not None else 0)
    tm, tn = _fit(m, n, lambda tm, tn: tm * k * _size(a) + k * tn * _size(b) + tm * tn * osz)
    if lead is None:
        b_spec = pl.BlockSpec((k, tn), lambda i, j: (0, j))
    else:
        b_spec = pl.BlockSpec((None, k, tn), lambda i, j: (lead, 0, j))
    return _mm(name, a, b, NN_DIMS, (m // tm, n // tn), pl.BlockSpec((tm, k), lambda i, j: (i, 0)), b_spec,
               pl.BlockSpec((tm, tn), lambda i, j: (i, j)), jax.ShapeDtypeStruct((m, n), out_dtype), add=add)


def _nn_parts(name, a, b, parts, out_dtype, lead=None, stacked=False):
    m, k = a.shape
    c = b.shape[-1] if stacked else b.shape[-1] // parts
    osz = jnp.dtype(out_dtype).itemsize
    tm, tn = _fit(m, c, lambda tm, tn: tm * k * _size(a) + k * tn * _size(b) + tm * tn * osz)
    nb = c // tn
    if stacked:
        b_spec = pl.BlockSpec((None, k, tn), lambda i, p, j: (p, 0, j))
    elif lead is None:
        b_spec = pl.BlockSpec((k, tn), lambda i, p, j: (0, p * nb + j))
    else:
        b_spec = pl.BlockSpec((None, k, tn), lambda i, p, j: (lead, 0, p * nb + j))
    return _mm(name, a, b, NN_DIMS, (m // tm, parts, nb), pl.BlockSpec((tm, k), lambda i, p, j: (i, 0)), b_spec,
               pl.BlockSpec((None, tm, tn), lambda i, p, j: (p, i, j)), jax.ShapeDtypeStruct((parts, m, c), out_dtype))


def _nt(name, a, b, out_dtype, lead=None):
    (m, k), n = a.shape, b.shape[-2]
    osz = jnp.dtype(out_dtype).itemsize
    tm, tn = _fit(m, n, lambda tm, tn: tm * k * _size(a) + tn * k * _size(b) + tm * tn * osz)
    if lead is None:
        b_spec = pl.BlockSpec((tn, k), lambda i, j: (j, 0))
    else:
        b_spec = pl.BlockSpec((None, tn, k), lambda i, j: (lead, j, 0))
    return _mm(name, a, b, NT_DIMS, (m // tm, n // tn), pl.BlockSpec((tm, k), lambda i, j: (i, 0)), b_spec,
               pl.BlockSpec((tm, tn), lambda i, j: (i, j)), jax.ShapeDtypeStruct((m, n), out_dtype))


def _tn(name, a, b, out_dtype):
    (k, m), n = a.shape, b.shape[1]
    osz = jnp.dtype(out_dtype).itemsize
    tm, tn = _fit(m, n, lambda tm, tn: k * tm * _size(a) + k * tn * _size(b) + tm * tn * osz,
                  m_tiles=(512, 384, 256, 128), n_tiles=(n,) + N_TILES)
    return _mm(name, a, b, TN_DIMS, (m // tm, n // tn), pl.BlockSpec((k, tm), lambda i, j: (0, i)),
               pl.BlockSpec((k, tn), lambda i, j: (0, j)), pl.BlockSpec((tm, tn), lambda i, j: (i, j)),
               jax.ShapeDtypeStruct((m, n), out_dtype))


def _nn_add_norm(name, a, b, add, g):
    (m, k), n = a.shape, b.shape[1]
    tm = 512

    def body(a_ref, b_ref, add_ref, g_ref, h_ref, hn_ref):
        h = jnp.dot(a_ref[...], b_ref[...], preferred_element_type=F32) + add_ref[...]
        h_ref[...] = h
        hn_ref[...] = _rms_rows(h, g_ref[...]).astype(BF16)

    rows = lambda w: pl.BlockSpec((tm, w), lambda i: (i, 0))
    return _tc_call(
        body, name=name, grid=(m // tm,),
        in_specs=[rows(k), pl.BlockSpec((k, n), lambda i: (0, 0)), rows(n), pl.BlockSpec((1, n), lambda i: (0, 0))],
        out_specs=[rows(n), rows(n)],
        out_shape=[jax.ShapeDtypeStruct((m, n), F32), jax.ShapeDtypeStruct((m, n), BF16)], compiler_params=_cp("parallel"),
    )(a, b, add, g)


def _nn_add_loss(name, a, b, add, g, tgt):
    (m, k), n = a.shape, b.shape[1]
    tm = 512

    def body(a_ref, b_ref, add_ref, g_ref, t_ref, loss_ref, dh_ref, dhb_ref, dg_ref):
        xv = jnp.dot(a_ref[...], b_ref[...], preferred_element_type=F32) + add_ref[...]
        gv = g_ref[...]
        r = lax.rsqrt(jnp.mean(xv * xv, axis=1, keepdims=True) + EPS)
        err = xv * r * gv - t_ref[...]
        part = 0.5 * jnp.sum(jnp.mean(err * err, axis=1, keepdims=True), axis=0, keepdims=True)
        dx, dg = _rms_bwd_math(xv, gv, err * (1.0 / n))
        dh_ref[...] = dx
        dhb_ref[...] = dx.astype(BF16)

        @pl.when(pl.program_id(0) == 0)
        def _():
            dg_ref[...] = jnp.zeros_like(dg_ref)
            loss_ref[...] = jnp.zeros_like(loss_ref)

        dg_ref[...] += dg
        loss_ref[...] += jnp.broadcast_to(part, loss_ref.shape)

    rows = lambda w: pl.BlockSpec((tm, w), lambda i: (i, 0))
    vec = pl.BlockSpec((1, n), lambda i: (0, 0))
    return _tc_call(
        body, name=name, grid=(m // tm,),
        in_specs=[rows(k), pl.BlockSpec((k, n), lambda i: (0, 0)), rows(n), vec, rows(n)],
        out_specs=[pl.BlockSpec((1, LANES), lambda i: (0, 0)), rows(n), rows(n), vec],
        out_shape=[jax.ShapeDtypeStruct((1, LANES), F32), jax.ShapeDtypeStruct((m, n), F32),
                   jax.ShapeDtypeStruct((m, n), BF16), jax.ShapeDtypeStruct((1, n), F32)],
        compiler_params=_cp("arbitrary"),
    )(a, b, add, g, tgt)


def _dx_norm_bwd(name, a, b, x, g, add):
    parts, t, c = a.shape
    d = b.shape[0]
    tm = 256

    def body(a_ref, b_ref, x_ref, g_ref, add_ref, dx_ref, dxb_ref, dg_ref):
        dy = None
        for p in range(parts):
            part = lax.dot_general(a_ref[p], b_ref[:, p * c:(p + 1) * c], NT_DIMS, preferred_element_type=F32)
            dy = part if dy is None else dy + part
        dx, dg = _rms_bwd_math(x_ref[...], g_ref[...], dy)
        dx = dx + add_ref[...]
        dx_ref[...] = dx
        dxb_ref[...] = dx.astype(BF16)

        @pl.when(pl.program_id(0) == 0)
        def _():
            dg_ref[...] = jnp.zeros_like(dg_ref)

        dg_ref[...] += dg

    rows = pl.BlockSpec((tm, d), lambda i: (i, 0))
    vec = pl.BlockSpec((1, d), lambda i: (0, 0))
    return _tc_call(
        body, name=name, grid=(t // tm,),
        in_specs=[pl.BlockSpec((parts, tm, c), lambda i: (0, i, 0)), pl.BlockSpec(b.shape, lambda i: (0, 0)), rows, vec,
                  rows],
        out_specs=[rows, rows, vec],
        out_shape=[jax.ShapeDtypeStruct((t, d), F32), jax.ShapeDtypeStruct((t, d), BF16),
                   jax.ShapeDtypeStruct((1, d), F32)],
        compiler_params=_cp("arbitrary"),
    )(a, b, x, g, add)


def _dw_sc_in(hn, dz):
    t, tn, tm = hn.shape[0], TC, D
    per_part, per_chip = D // tn, 3 * D // N_CHIPS // tn
    return _mm("sc_in_dw", hn, dz, TN_DIMS, (D // tm, 3 * D // tn), pl.BlockSpec((t, tm), lambda i, j: (0, i)),
               pl.BlockSpec((None, t, tn), lambda i, j: (j // per_part, 0, j % per_part)),
               pl.BlockSpec((None, tm, tn), lambda i, j: (j // per_chip, i, j % per_chip)),
               jax.ShapeDtypeStruct((N_CHIPS, D, 3 * D // N_CHIPS), BF16))


def _dw_ffn_up(name, hf, dup):
    t, tm, ns = hf.shape[0], D, 2 * F_FF // N_CHIPS
    return _mm(name, hf, dup, TN_DIMS, (N_CHIPS, D // tm), pl.BlockSpec((t, tm), lambda s, i: (0, i)),
               pl.BlockSpec((None, t, ns), lambda s, i: (s // 2, 0, s % 2)),
               pl.BlockSpec((None, tm, ns), lambda s, i: (s, i, 0)), jax.ShapeDtypeStruct((N_CHIPS, D, ns), BF16))


def _rms_fwd(x, g, name):
    t, d = x.shape
    tr = 512

    def body(x_ref, g_ref, o_ref):
        xv = x_ref[...]
        r = lax.rsqrt(jnp.mean(xv * xv, axis=1, keepdims=True) + EPS)
        o_ref[...] = (xv * r * g_ref[...]).astype(o_ref.dtype)

    row = pl.BlockSpec((tr, d), lambda i: (i, 0))
    return _tc_call(
        body, name=name, grid=(t // tr,), in_specs=[row, pl.BlockSpec((1, d), lambda i: (0, 0))],
        out_specs=row, out_shape=jax.ShapeDtypeStruct((t, d), BF16), compiler_params=_cp("parallel"),
    )(x, g)


def _rms_bwd_math(xv, g, dy):
    r = lax.rsqrt(jnp.mean(xv * xv, axis=1, keepdims=True) + EPS)
    xh = xv * r
    gy = dy * g
    dx = r * (gy - xh * jnp.mean(gy * xh, axis=1, keepdims=True))
    dg = jnp.sum(dy * xh, axis=0, keepdims=True)
    return dx, dg


def _rot_half(x):
    lane = lax.broadcasted_iota(jnp.int32, x.shape, 1)
    return jnp.where((lane % QK_ROPE) < QK_ROPE // 2, -pltpu.roll(x, LANES - 32, axis=1),
                     pltpu.roll(x, 32, axis=1))


def _rope_fwd_math(x, cos, sin):
    return x * cos + _rot_half(x) * sin


def _rope_bwd_math(dy, cos, sin):
    return dy * cos - _rot_half(dy * sin)


def _rms_rows(x, g):
    return x * lax.rsqrt(jnp.mean(x * x, axis=1, keepdims=True) + EPS) * g


def _attn_prep(h, g_attn, g_kvin, w_dq, g_ql, w_uq, w_kv, g_kvl, w_ukv, cos, sin):
    t, d = h.shape
    tr = 256
    wq = N_HEADS * HEAD_PAD

    def body(h_ref, ga_ref, gk_ref, wdq_ref, gq_ref, wuq_ref, wkv_ref, gl_ref, wukv_ref, c_ref, s_ref,
             hn_ref, hk_ref, cqp_ref, cq_ref, q_ref, kvp_ref, ckv_ref, kr_ref, knv_ref):
        xv, cv, sv = h_ref[...], c_ref[...], s_ref[...]
        xh = xv * lax.rsqrt(jnp.mean(xv * xv, axis=1, keepdims=True) + EPS)
        hn = (xh * ga_ref[...]).astype(BF16)
        hk = (xh * gk_ref[...]).astype(BF16)
        hn_ref[...], hk_ref[...] = hn, hk
        cq_pre = jnp.dot(hn, wdq_ref[...], preferred_element_type=F32)
        cqp_ref[...] = cq_pre
        cq = _rms_rows(cq_pre, gq_ref[...]).astype(BF16)
        cq_ref[...] = cq
        for hd in range(N_HEADS):
            lo = hd * HEAD_PAD
            qh = jnp.dot(cq, wuq_ref[:, lo:lo + HEAD_PAD], preferred_element_type=F32)
            q_ref[:, lo:lo + QK_NOPE] = qh[:, :QK_NOPE].astype(BF16)
            q_ref[:, lo + QK_NOPE:lo + HEAD_PAD] = _rope_fwd_math(qh[:, QK_NOPE:], cv, sv).astype(BF16)
        kvpre = jnp.dot(hk, wkv_ref[...], preferred_element_type=F32)
        kvp_ref[...] = kvpre
        ckv = _rms_rows(kvpre[:, :KV_LORA], gl_ref[...]).astype(BF16)
        ckv_ref[...] = ckv
        kr_ref[...] = _rope_fwd_math(kvpre[:, KV_LORA:], cv, sv).astype(BF16)
        for p in range(2):
            knv_ref[p] = jnp.dot(ckv, wukv_ref[p], preferred_element_type=F32).astype(BF16)

    rows = lambda w: pl.BlockSpec((tr, w), lambda i: (i, 0))
    whole = lambda a: pl.BlockSpec(a.shape, lambda i: (0,) * a.ndim)
    sds = lambda w, dt: jax.ShapeDtypeStruct((t, w), dt)
    args = (h, g_attn, g_kvin, w_dq, g_ql, w_uq, w_kv, g_kvl, w_ukv, cos, sin)
    return _tc_call(
        body, name="attn_prep", grid=(t // tr,),
        in_specs=[rows(d)] + [whole(a) for a in args[1:9]] + [rows(LANES), rows(LANES)],
        out_specs=[rows(d), rows(d), rows(Q_LORA), rows(Q_LORA), rows(wq), rows(KVP), rows(KV_LORA), rows(LANES),
                   pl.BlockSpec((2, tr, N_HEADS * QK_NOPE), lambda i: (0, i, 0))],
        out_shape=[sds(d, BF16), sds(d, BF16), sds(Q_LORA, F32), sds(Q_LORA, BF16), sds(wq, BF16), sds(KVP, F32),
                   sds(KV_LORA, BF16), sds(LANES, BF16), jax.ShapeDtypeStruct((2, t, N_HEADS * QK_NOPE), BF16)],
        compiler_params=_cp("parallel"),
    )(*args)


def _attn_prep_bwd(dq, dknv, dkr, dh, h, hn, hk, cq_pre, cq, kvpre, ckv, g_attn, g_kvin, w_dq, g_ql, w_uq, w_kv, g_kvl,
                   w_ukv, cos, sin):
    t, d = h.shape
    tr = 256
    n_steps = t // tr
    wq = N_HEADS * HEAD_PAD
    wk = N_HEADS * QK_NOPE

    def body(dq_ref, dknv_ref, dkr_ref, dh_ref, h_ref, hn_ref, hk_ref, cqp_ref, cq_ref, kvp_ref, ckv_ref,
             ga_ref, gk_ref, wdq_ref, gq_ref, wuq_ref, wkv_ref, gl_ref, wukv_ref, c_ref, s_ref,
             dho_ref, dhb_ref, dwuq_ref, dwdq_ref, dwukv_ref, dwkv_ref, dga_ref, dgk_ref, dgq_ref, dgl_ref,
             a_uq, a_dq, a_ukv, a_kv):
        i = pl.program_id(0)

        @pl.when(i == 0)
        def _():
            for ref in (a_uq, a_dq, a_ukv, a_kv, dga_ref, dgk_ref, dgq_ref, dgl_ref):
                ref[...] = jnp.zeros_like(ref)

        dqv = dq_ref[...]
        dcq = lax.dot_general(dqv, wuq_ref[...], NT_DIMS, preferred_element_type=F32)
        a_uq[...] += lax.dot_general(cq_ref[...], dqv, TN_DIMS, preferred_element_type=F32)
        dcq_pre, dg = _rms_bwd_math(cqp_ref[...], gq_ref[...], dcq)
        dgq_ref[...] += dg
        dcq_pre = dcq_pre.astype(BF16)
        dhn = lax.dot_general(dcq_pre, wdq_ref[...], NT_DIMS, preferred_element_type=F32)
        a_dq[...] += lax.dot_general(hn_ref[...], dcq_pre, TN_DIMS, preferred_element_type=F32)
        dckv = None
        for p in range(2):
            dk = dknv_ref[p].astype(BF16)
            part = lax.dot_general(dk, wukv_ref[p], NT_DIMS, preferred_element_type=F32)
            dckv = part if dckv is None else dckv + part
            a_ukv[p] += lax.dot_general(ckv_ref[...], dk, TN_DIMS, preferred_element_type=F32)
        dlat, dg = _rms_bwd_math(kvp_ref[:, :KV_LORA], gl_ref[...], dckv)
        dgl_ref[...] += dg
        dkr_pre = _rope_bwd_math(dkr_ref[...], c_ref[...], s_ref[...])
        dkvpre = jnp.concatenate([dlat, dkr_pre], axis=1).astype(BF16)
        dhk = lax.dot_general(dkvpre, wkv_ref[...], NT_DIMS, preferred_element_type=F32)
        a_kv[...] += lax.dot_general(hk_ref[...], dkvpre, TN_DIMS, preferred_element_type=F32)
        xv = h_ref[...]
        dx1, dg = _rms_bwd_math(xv, ga_ref[...], dhn)
        dga_ref[...] += dg
        dx2, dg = _rms_bwd_math(xv, gk_ref[...], dhk)
        dgk_ref[...] += dg
        dh_new = dh_ref[...] + dx1 + dx2
        dho_ref[...] = dh_new
        dhb_ref[...] = dh_new.astype(BF16)

        @pl.when(i == n_steps - 1)
        def _():
            dwuq_ref[...] = a_uq[...].astype(BF16)
            dwdq_ref[...] = a_dq[...].astype(BF16)
            dwukv_ref[...] = a_ukv[...].astype(BF16)
            dwkv_ref[...] = a_kv[...].astype(BF16)

    rows = lambda w: pl.BlockSpec((tr, w), lambda i: (i, 0))
    whole = lambda shape: pl.BlockSpec(shape, lambda i: (0,) * len(shape))
    weights = (g_attn, g_kvin, w_dq, g_ql, w_uq, w_kv, g_kvl, w_ukv)
    dw_shapes = [(Q_LORA, wq), (d, Q_LORA), (2, KV_LORA, wk), (d, KVP)]
    dg_shapes = [(1, d), (1, d), (1, Q_LORA), (1, KV_LORA)]
    return _tc_call(
        body, name="attn_prep_bwd", grid=(n_steps,),
        in_specs=[rows(wq), pl.BlockSpec((2, tr, wk), lambda i: (0, i, 0)), rows(LANES), rows(d), rows(d), rows(d),
                  rows(d), rows(Q_LORA), rows(Q_LORA), rows(KVP), rows(KV_LORA)]
        + [whole(a.shape) for a in weights] + [rows(LANES), rows(LANES)],
        out_specs=[rows(d), rows(d)] + [whole(s) for s in dw_shapes + dg_shapes],
        out_shape=[jax.ShapeDtypeStruct((t, d), F32), jax.ShapeDtypeStruct((t, d), BF16)]
        + [jax.ShapeDtypeStruct(s, BF16) for s in dw_shapes] + [jax.ShapeDtypeStruct(s, F32) for s in dg_shapes],
        scratch_shapes=[pltpu.VMEM(s, F32) for s in dw_shapes], compiler_params=_cp("arbitrary"),
    )(dq, dknv, dkr, dh, h, hn, hk, cq_pre, cq, kvpre, ckv, *weights, cos, sin)


ROW_CHUNK = 64
HALO = 16
WIN = ROW_CHUNK + 16
LANE_HALVES = (slice(0, LANES), slice(LANES, TC))


def _stage(s_ref, p, src):
    t = src.shape[0]
    s_ref[p, :HALO] = jnp.zeros((HALO, TC), BF16)
    s_ref[p, HALO:HALO + t] = src
    s_ref[p, HALO + t:] = jnp.zeros((HALO, TC), BF16)


def _window(s_ref, p, i, lanes):
    base = pl.multiple_of(i * ROW_CHUNK, ROW_CHUNK)
    return s_ref[p, pl.ds(base, ROW_CHUNK + 2 * HALO), lanes].astype(F32)[8:8 + WIN]


def _valid(x):
    return x[8:8 + ROW_CHUNK]


def _prev(x, k):
    return pltpu.roll(x, k, axis=0)


def _next(x, k):
    return pltpu.roll(x, WIN - k, axis=0)


def _taps(w_ref, lanes):
    return w_ref[0:1, lanes], w_ref[1:2, lanes], w_ref[2:3, lanes]


def _fold8(x):
    return jnp.sum(x.reshape(ROW_CHUNK // 8, 8, x.shape[-1]), axis=0)


def _store_rows(ref, idx, i, lanes, x):
    rows = pl.ds(pl.multiple_of(i * ROW_CHUNK, ROW_CHUNK), ROW_CHUNK)
    ref[(*idx, rows, lanes)] = x.astype(ref.dtype)


def _for_chunks(t, chunk):
    def step(i, carry):
        for lanes in LANE_HALVES:
            chunk(i, lanes)
        return carry

    lax.fori_loop(0, t // ROW_CHUNK, step, 0)


def _write_col_sums(acc_ref, outs):
    for k, (ref, row) in enumerate(outs):
        ref[row:row + 1, :] = jnp.sum(acc_ref[k], axis=0, keepdims=True)


def _shift_down(x, k):
    row = lax.broadcasted_iota(jnp.int32, x.shape, 0)
    return jnp.where(row >= k, pltpu.roll(x, k, axis=0), 0.0)


def _shift_up(x, k):
    n = x.shape[0]
    row = lax.broadcasted_iota(jnp.int32, x.shape, 0)
    return jnp.where(row < n - k, pltpu.roll(x, n - k, axis=0), 0.0)


def _conv3(x, w_ref):
    return _shift_down(x, 2) * w_ref[0:1, :] + _shift_down(x, 1) * w_ref[1:2, :] + x * w_ref[2:3, :]


def _col(parts, t):
    if parts is None:
        return pl.BlockSpec((t, TC), lambda j: (0, j))
    return pl.BlockSpec((parts, t, TC), lambda j: (0, 0, j))


def _staging(parts, t):
    return pltpu.VMEM((parts, t + 2 * HALO, TC), BF16)


def _scmix_fwd(z, w):
    t = z.shape[1]

    def body(z_ref, w_ref, m_ref):
        b, c, u = (z_ref[p].astype(F32) for p in range(3))
        m_ref[...] = (b * _conv3(c * u, w_ref)).astype(BF16)

    return _tc_call(
        body, name="scmix_fwd", grid=(D // TC,), in_specs=[_col(3, t), pl.BlockSpec((3, TC), lambda j: (0, j))],
        out_specs=_col(None, t), out_shape=jax.ShapeDtypeStruct((t, D), BF16), compiler_params=_cp("parallel"),
    )(z, w)


def _scmix_bwd(z, w, dm):
    t = z.shape[1]

    def body(z_ref, w_ref, dm_ref, dz_ref, dw_ref, s_ref, acc_ref):
        for p in range(3):
            _stage(s_ref, p, z_ref[p])
        _stage(s_ref, 3, dm_ref[...])
        acc_ref[...] = jnp.zeros_like(acc_ref)

        def chunk(i, lanes):
            w0, w1, w2 = _taps(w_ref, lanes)
            b, c, u, dm = (_window(s_ref, p, i, lanes) for p in range(4))
            cu = c * u
            cu1, cu2 = _prev(cu, 1), _prev(cu, 2)
            _store_rows(dz_ref, (0,), i, lanes, _valid(dm * (cu2 * w0 + cu1 * w1 + cu * w2)))
            dcv = dm * b
            dcu = dcv * w2 + _next(dcv, 1) * w1 + _next(dcv, 2) * w0
            _store_rows(dz_ref, (1,), i, lanes, _valid(dcu * u))
            _store_rows(dz_ref, (2,), i, lanes, _valid(dcu * c))
            for k, shifted in enumerate((cu2, cu1, cu)):
                acc_ref[k, :, lanes] += _fold8(_valid(dcv * shifted))

        _for_chunks(t, chunk)
        _write_col_sums(acc_ref, [(dw_ref, 0), (dw_ref, 1), (dw_ref, 2)])

    wspec = pl.BlockSpec((3, TC), lambda j: (0, j))
    return _tc_call(
        body, name="scmix_bwd", grid=(D // TC,), in_specs=[_col(3, t), wspec, _col(None, t)],
        out_specs=[_col(3, t), wspec],
        out_shape=[jax.ShapeDtypeStruct((3, t, D), BF16), jax.ShapeDtypeStruct((3, D), F32)],
        scratch_shapes=[_staging(4, t), pltpu.VMEM((3, 8, TC), F32)], compiler_params=_cp("parallel"),
    )(z, w, dm)


def _ffn_up_gate(hf, w_up, w, bias, name):
    t, d = hf.shape
    nb = F_FF // TC

    def body(hf_ref, wg_ref, wv_ref, w_ref, b_ref, up_ref, a_ref, prev_ref):
        @pl.when(pl.program_id(0) == 0)
        def _():
            prev_ref[...] = jnp.zeros_like(prev_ref)

        gc = _conv3(prev_ref[0].astype(F32), w_ref) + b_ref[...]
        a_ref[...] = (gc * jax.nn.sigmoid(gc) * prev_ref[1].astype(F32)).astype(BF16)
        hv = hf_ref[...]
        up_ref[0] = jnp.dot(hv, wg_ref[...], preferred_element_type=F32).astype(BF16)
        up_ref[1] = jnp.dot(hv, wv_ref[...], preferred_element_type=F32).astype(BF16)
        prev_ref[...] = up_ref[...]

    tile = lambda j: jnp.minimum(j, nb - 1)
    gated = lambda j: jnp.maximum(j - 1, 0)
    return _tc_call(
        body, name=name, grid=(nb + 1,),
        in_specs=[pl.BlockSpec((t, d), lambda j: (0, 0)), pl.BlockSpec((d, TC), lambda j: (0, tile(j))),
                  pl.BlockSpec((d, TC), lambda j: (0, nb + tile(j))), pl.BlockSpec((3, TC), lambda j: (0, gated(j))),
                  pl.BlockSpec((1, TC), lambda j: (0, gated(j)))],
        out_specs=[pl.BlockSpec((2, t, TC), lambda j: (0, 0, tile(j))), pl.BlockSpec((t, TC), lambda j: (0, gated(j)))],
        out_shape=[jax.ShapeDtypeStruct((2, t, F_FF), BF16), jax.ShapeDtypeStruct((t, F_FF), BF16)],
        scratch_shapes=[pltpu.VMEM((2, t, TC), BF16)], compiler_params=_cp("arbitrary"),
    )(hf, w_up, w_up, w, bias)


def _gate_bwd(up, w, bias, dh, w_down, name):
    t, d = dh.shape

    def body(u_ref, w_ref, b_ref, dh_ref, wd_ref, du_ref, dw_ref, db_ref, s_ref, acc_ref):
        for p in range(2):
            _stage(s_ref, p, u_ref[p])
        _stage(s_ref, 2, lax.dot_general(dh_ref[...], wd_ref[...], NT_DIMS, preferred_element_type=F32).astype(BF16))
        acc_ref[...] = jnp.zeros_like(acc_ref)

        def chunk(i, lanes):
            w0, w1, w2 = _taps(w_ref, lanes)
            g, v, da = (_window(s_ref, p, i, lanes) for p in range(3))
            g1, g2 = _prev(g, 1), _prev(g, 2)
            gc = g2 * w0 + g1 * w1 + g * w2 + b_ref[:, lanes]
            sg = jax.nn.sigmoid(gc)
            _store_rows(du_ref, (1,), i, lanes, _valid(da * (gc * sg)))
            dgc = da * v * (sg * (1.0 + gc * (1.0 - sg)))
            _store_rows(du_ref, (0,), i, lanes, _valid(dgc * w2 + _next(dgc, 1) * w1 + _next(dgc, 2) * w0))
            for k, shifted in enumerate((g2, g1, g)):
                acc_ref[k, :, lanes] += _fold8(_valid(dgc * shifted))
            acc_ref[3, :, lanes] += _fold8(_valid(dgc))

        _for_chunks(t, chunk)
        _write_col_sums(acc_ref, [(dw_ref, 0), (dw_ref, 1), (dw_ref, 2), (db_ref, 0)])

    wspec = pl.BlockSpec((3, TC), lambda j: (0, j))
    bspec = pl.BlockSpec((1, TC), lambda j: (0, j))
    return _tc_call(
        body, name=name, grid=(F_FF // TC,),
        in_specs=[_col(2, t), wspec, bspec, pl.BlockSpec((t, d), lambda j: (0, 0)), pl.BlockSpec((TC, d), lambda j: (j, 0))],
        out_specs=[_col(2, t), wspec, bspec],
        out_shape=[jax.ShapeDtypeStruct((2, t, F_FF), BF16), jax.ShapeDtypeStruct((3, F_FF), F32),
                   jax.ShapeDtypeStruct((1, F_FF), F32)],
        scratch_shapes=[_staging(3, t), pltpu.VMEM((4, 8, TC), F32)], compiler_params=_cp("parallel"),
    )(up, w, bias, dh, w_down)


ATT_TQ = 256
ATT_SCALE = (QK_NOPE + QK_ROPE) ** -0.5


def _key_ranges(lvl):
    lo = lvl * ATT_TQ
    return ([(0, lo, False)] if lvl else []) + [(lo, lo + ATT_TQ, True)]


FWD_HEADS = 4
BWD_HEADS = 2


def _fill_keys(k_ref, kn_ref, kr_ref):
    @pl.when(pl.program_id(1) == 0)
    def _():
        for hh in range(k_ref.shape[0]):
            k_ref[hh, :, :QK_NOPE] = kn_ref[:, hh * QK_NOPE:(hh + 1) * QK_NOPE]
            k_ref[hh, :, QK_NOPE:] = kr_ref[...]


def _attn_probs(q, k_ref, lvl):
    scores = []
    for lo, hi, diagonal in _key_ranges(lvl):
        s = lax.dot_general(q, k_ref[lo:hi, :], NT_DIMS, preferred_element_type=F32) * ATT_SCALE
        if diagonal:
            row = lax.broadcasted_iota(jnp.int32, s.shape, 0)
            col = lax.broadcasted_iota(jnp.int32, s.shape, 1)
            seen = lax.shift_right_logical(col, CHUNK_SHIFT) <= lax.shift_right_logical(row, CHUNK_SHIFT)
            s = jnp.where(seen, s, NEG_INF)
        scores.append(s)
    m = jnp.max(scores[0], axis=1, keepdims=True)
    for s in scores[1:]:
        m = jnp.maximum(m, jnp.max(s, axis=1, keepdims=True))
    ps = [jnp.exp(s - m) for s in scores]
    total = jnp.sum(ps[0], axis=1, keepdims=True)
    for p in ps[1:]:
        total = total + jnp.sum(p, axis=1, keepdims=True)
    inv = 1.0 / total
    return [p * inv for p in ps]


def _attn_probs_t(q, k_ref, lvl):
    scores = []
    for lo, hi, diagonal in _key_ranges(lvl):
        s = lax.dot_general(k_ref[lo:hi, :], q, NT_DIMS, preferred_element_type=F32) * ATT_SCALE
        if diagonal:
            key = lax.broadcasted_iota(jnp.int32, s.shape, 0)
            qry = lax.broadcasted_iota(jnp.int32, s.shape, 1)
            seen = lax.shift_right_logical(key, CHUNK_SHIFT) <= lax.shift_right_logical(qry, CHUNK_SHIFT)
            s = jnp.where(seen, s, NEG_INF)
        scores.append(s)
    m = jnp.max(scores[0], axis=0, keepdims=True)
    for s in scores[1:]:
        m = jnp.maximum(m, jnp.max(s, axis=0, keepdims=True))
    ps = [jnp.exp(s - m) for s in scores]
    total = jnp.sum(ps[0], axis=0, keepdims=True)
    for p in ps[1:]:
        total = total + jnp.sum(p, axis=0, keepdims=True)
    inv = 1.0 / total
    return [p * inv for p in ps]


def _per_query_block(qi, n_blocks, branch):
    for lvl in range(n_blocks):
        pl.when(qi == lvl)(lambda lvl=lvl: branch(lvl))


def _attn_specs(t, g):
    q = pl.BlockSpec((ATT_TQ, g * HEAD_PAD), lambda h, i: (i, h))
    kn = pl.BlockSpec((None, t, g * QK_NOPE), lambda h, i: (0, 0, h))
    kr = pl.BlockSpec((t, LANES), lambda h, i: (0, 0))
    v = pl.BlockSpec((None, t, g * V_HEAD), lambda h, i: (1, 0, h))
    o = pl.BlockSpec((ATT_TQ, g * V_HEAD), lambda h, i: (i, h))
    return q, kn, kr, v, o


def _attn_fwd(q, knv, kr):
    t = q.shape[0]

    def body(q_ref, kn_ref, kr_ref, v_ref, o_ref, k_ref):
        _fill_keys(k_ref, kn_ref, kr_ref)

        def branch(lvl):
            for hh in range(FWD_HEADS):
                vcols = slice(hh * V_HEAD, (hh + 1) * V_HEAD)
                ps = _attn_probs(q_ref[:, hh * HEAD_PAD:(hh + 1) * HEAD_PAD], k_ref.at[hh], lvl)
                o = None
                for p, (lo, hi, _) in zip(ps, _key_ranges(lvl)):
                    part = jnp.dot(p.astype(BF16), v_ref[lo:hi, vcols], preferred_element_type=F32)
                    o = part if o is None else o + part
                o_ref[:, vcols] = o.astype(BF16)

        _per_query_block(pl.program_id(1), t // ATT_TQ, branch)

    qs, kns, krs, vs, os_ = _attn_specs(t, FWD_HEADS)
    return _tc_call(
        body, name="attn_fwd", grid=(N_HEADS // FWD_HEADS, t // ATT_TQ), in_specs=[qs, kns, krs, vs],
        out_specs=os_, out_shape=jax.ShapeDtypeStruct((t, N_HEADS * V_HEAD), BF16),
        scratch_shapes=[pltpu.VMEM((FWD_HEADS, t, HEAD_PAD), BF16)], compiler_params=_cp("parallel", "arbitrary"),
    )(q, knv, kr, knv)


def _attn_bwd(q, knv, kr, do, cos, sin):
    t = q.shape[0]

    def body(q_ref, kn_ref, kr_ref, v_ref, do_ref, c_ref, s_ref, dq_ref, dknv_ref, dkr_ref, k_ref, dk_ref):
        h, qi = pl.program_id(0), pl.program_id(1)
        _fill_keys(k_ref, kn_ref, kr_ref)

        @pl.when(qi == 0)
        def _():
            dknv_ref[1] = jnp.zeros(dknv_ref.shape[1:], F32)
            dk_ref[...] = jnp.zeros_like(dk_ref)

        @pl.when((qi == 0) & (h == 0))
        def _():
            dkr_ref[...] = jnp.zeros_like(dkr_ref)

        def branch(lvl):
            ranges = _key_ranges(lvl)
            for hh in range(BWD_HEADS):
                qcols = slice(hh * HEAD_PAD, (hh + 1) * HEAD_PAD)
                vcols = slice(hh * V_HEAD, (hh + 1) * V_HEAD)
                qv, dov = q_ref[:, qcols], do_ref[:, vcols]
                ps = _attn_probs_t(qv, k_ref.at[hh], lvl)
                dps = [lax.dot_general(v_ref[lo:hi, vcols], dov, NT_DIMS, preferred_element_type=F32)
                       for lo, hi, _ in ranges]
                di = None
                for p, dp in zip(ps, dps):
                    part = jnp.sum(p * dp, axis=0, keepdims=True)
                    di = part if di is None else di + part
                dq = None
                for p, dp, (lo, hi, _) in zip(ps, dps, ranges):
                    ds = (p * (dp - di) * ATT_SCALE).astype(BF16)
                    part = lax.dot_general(ds, k_ref[hh, lo:hi, :], TN_DIMS, preferred_element_type=F32)
                    dq = part if dq is None else dq + part
                    dk_ref[hh, lo:hi, :] += jnp.dot(ds, qv, preferred_element_type=F32)
                    dknv_ref[1, lo:hi, vcols] += jnp.dot(p.astype(BF16), dov, preferred_element_type=F32)
                dq_ref[:, hh * HEAD_PAD:hh * HEAD_PAD + QK_NOPE] = dq[:, :QK_NOPE].astype(BF16)
                dq_ref[:, hh * HEAD_PAD + QK_NOPE:(hh + 1) * HEAD_PAD] = _rope_bwd_math(
                    dq[:, QK_NOPE:], c_ref[...], s_ref[...]).astype(BF16)

        _per_query_block(qi, t // ATT_TQ, branch)

        @pl.when(qi == t // ATT_TQ - 1)
        def _():
            for hh in range(BWD_HEADS):
                dknv_ref[0, :, hh * QK_NOPE:(hh + 1) * QK_NOPE] = dk_ref[hh, :, :QK_NOPE]
                dkr_ref[...] += dk_ref[hh, :, QK_NOPE:]

    qs, kns, krs, vs, os_ = _attn_specs(t, BWD_HEADS)
    tab = pl.BlockSpec((ATT_TQ, LANES), lambda h, i: (i, 0))
    return _tc_call(
        body, name="attn_bwd", grid=(N_HEADS // BWD_HEADS, t // ATT_TQ), in_specs=[qs, kns, krs, vs, os_, tab, tab],
        out_specs=[qs, pl.BlockSpec((2, t, BWD_HEADS * QK_NOPE), lambda h, i: (0, 0, h)), krs],
        out_shape=[jax.ShapeDtypeStruct((t, N_HEADS * HEAD_PAD), BF16),
                   jax.ShapeDtypeStruct((2, t, N_HEADS * QK_NOPE), F32), jax.ShapeDtypeStruct((t, LANES), F32)],
        scratch_shapes=[pltpu.VMEM((BWD_HEADS, t, HEAD_PAD), BF16), pltpu.VMEM((BWD_HEADS, t, HEAD_PAD), F32)],
        compiler_params=_cp("arbitrary", "arbitrary"),
    )(q, knv, kr, knv, do, cos, sin)


def _adam_math(w, g, m, v):
    nm = ADAM_B1 * m + (1.0 - ADAM_B1) * g
    nv = ADAM_B2 * v + (1.0 - ADAM_B2) * (g * g)
    m_hat = nm / (1.0 - ADAM_B1 ** ADAM_STEP)
    v_hat = nv / (1.0 - ADAM_B2 ** ADAM_STEP)
    return -ADAM_LR * (m_hat / (jnp.sqrt(v_hat) + ADAM_EPS) + ADAM_WD * w), nm, nv


def _adamw_small(ws, gs, ms, vs):
    n = len(ws)

    def body(*refs):
        for i in range(n):
            w_ref, g_ref, m_ref, v_ref = (refs[k * n + i] for k in range(4))
            d_ref, nm_ref, nv_ref = (refs[(4 + k) * n + i] for k in range(3))
            d_ref[...], nm_ref[...], nv_ref[...] = _adam_math(w_ref[...], g_ref[...], m_ref[...], v_ref[...])

    shapes = [jax.ShapeDtypeStruct(a.shape, F32) for a in ws]
    res = _tc_call(body, name="adamw_small", out_shape=shapes * 3)(*ws, *gs, *ms, *vs)
    return res[:n], res[n:2 * n], res[2 * n:]


ADAM_SPLIT = 4


def _adamw_shards(ids, items, name):
    n = len(items)

    def body(ids_ref, *refs):
        outs = refs[len(refs) - 4 * n:]
        for i, it in enumerate(items):
            w_ref, m_ref, v_ref, gm_ref, gs_ref = refs[5 * i:5 * i + 5]
            g_ref, d_ref, nm_ref, nv_ref = outs[4 * i:4 * i + 4]
            cols = slice(*it["gcols"]) if it.get("gcols") else slice(None)
            whose = pl.program_id(0) if it.get("owner") is None else it["owner"]
            mine = whose == ids_ref[0]

            @pl.when(mine)
            def _(g_ref=g_ref, gm_ref=gm_ref, cols=cols):
                g_ref[...] = gm_ref[:, cols]

            @pl.when(jnp.logical_not(mine))
            def _(g_ref=g_ref, gs_ref=gs_ref, cols=cols):
                g_ref[...] = gs_ref[:, cols]

            d_ref[...], nm_ref[...], nv_ref[...] = _adam_math(w_ref[...], g_ref[...], m_ref[...], v_ref[...])

    in_specs, out_specs, out_shape, args, carried, aliases = [], [], [], [ids], [], {}
    for i, it in enumerate(items):
        w = it["w"]
        r, c = w.shape[-2:]
        tr = r // 2 // ADAM_SPLIT
        assert tr % 8 == 0, (name, w.shape)
        layer = it.get("layer")
        if layer is None:
            wspec = pl.BlockSpec((tr, c), lambda h, k, ids: (h * ADAM_SPLIT + k, 0))
        else:
            wspec = pl.BlockSpec((None, tr, c), lambda h, k, ids, layer=layer: (layer, h * ADAM_SPLIT + k, 0))
        gc = it["g_mine"].shape[1]
        if it.get("owner") is None:
            gspec = pl.BlockSpec((tr, gc), lambda h, k, ids: (k, 0))
        else:
            gspec = pl.BlockSpec((tr, gc), lambda h, k, ids: (h * ADAM_SPLIT + k, 0))
        in_specs += [wspec] * 3 + [gspec] * 2
        args += [w, it["m"], it["v"], it["g_mine"], it["g_sib"]]
        out_specs += [wspec] * 4
        out_shape += [jax.ShapeDtypeStruct(w.shape, F32)] * 4
        if it.get("prev") is not None:
            for k, p in enumerate(it["prev"]):
                aliases[1 + 5 * n + len(carried)] = 4 * i + k
                carried.append(p)
    res = _tc_call(
        body, name=name, prefetch=1, grid=(2, ADAM_SPLIT), in_specs=in_specs + [ANY] * len(carried),
        out_specs=out_specs, out_shape=out_shape, input_output_aliases=aliases,
        compiler_params=_cp("parallel", "parallel"),
    )(*args, *carried)
    return [res[4 * i:4 * i + 4] for i in range(n)]


def _peer_chip(k_me, j):
    return k_me ^ jnp.where(j == 0, 2, jnp.where(j == 1, 1, 3))


def _pair_sums(ids, gs, ras, name):
    n = len(gs)

    def body(ids_ref, *refs):
        for i in range(n):
            g_ref, ra_ref, o_ref = refs[2 * i], refs[2 * i + 1], refs[2 * n + i]
            o_ref[...] = (g_ref[...].astype(F32) + ra_ref[...].astype(F32)).astype(BF16)

    in_specs, out_specs, out_shape = [], [], []
    for g in gs:
        half, c = g.shape[1] // 2, g.shape[2]
        in_specs += [pl.BlockSpec((None, half, c), lambda j, ids: (_peer_chip(ids[1], j), ids[0], 0)),
                     pl.BlockSpec((None, half, c), lambda j, ids: (_peer_chip(ids[1], j), 0, 0))]
        out_specs.append(pl.BlockSpec((None, half, c), lambda j, ids: (j, 0, 0)))
        out_shape.append(jax.ShapeDtypeStruct((3, half, c), BF16))
    return _tc_call(
        body, name=name, prefetch=1, grid=(3,), in_specs=in_specs, out_specs=out_specs, out_shape=out_shape,
        compiler_params=_cp("parallel"),
    )(ids, *[a for pair in zip(gs, ras) for a in pair])


def _chip_sums(ids, gs, ras, rbs, name):
    n = len(gs)

    def body(ids_ref, *refs):
        for i in range(n):
            g_ref, ra_ref, rb_ref, o_ref = refs[3 * i], refs[3 * i + 1], refs[3 * i + 2], refs[3 * n + i]
            acc = g_ref[...].astype(F32) + ra_ref[...].astype(F32)
            for j in range(3):
                acc = acc + rb_ref[j].astype(F32)
            o_ref[...] = acc

    in_specs, out_specs, out_shape = [], [], []
    for g in gs:
        half, c = g.shape[1] // 2, g.shape[2]
        in_specs += [pl.BlockSpec((None, half, c), lambda i, ids: (ids[1], ids[0], 0)),
                     pl.BlockSpec((None, half, c), lambda i, ids: (ids[1], 0, 0)),
                     pl.BlockSpec((3, half, c), lambda i, ids: (0, 0, 0))]
        out_specs.append(pl.BlockSpec((half, c), lambda i, ids: (0, 0)))
        out_shape.append(jax.ShapeDtypeStruct((half, c), F32))
    return _tc_call(
        body, name=name, prefetch=1, grid=(1,), in_specs=in_specs, out_specs=out_specs, out_shape=out_shape,
        compiler_params=_cp("arbitrary"),
    )(ids, *[a for trio in zip(gs, ras, rbs) for a in trio])


def _position():
    x, y, c = lax.axis_index("x"), lax.axis_index("y"), lax.axis_index("c")
    chips = [(1 - x, y), (x, 1 - y), (1 - x, 1 - y)]
    return x, y, c, chips


def _shard_half(ref, wm, h):
    if wm.kind == "tiny":
        return ref
    if wm.nl == 2:
        return ref.at[h]
    return ref.at[pl.ds(pl.multiple_of(h * (wm.k // 2), 16), wm.k // 2), :]


def _region(full, wm, s, h):
    if wm.kind == "tiny":
        return full.at[s]
    cols = pl.ds(pl.multiple_of(s * wm.n, LANES), wm.n) if wm.kind == "col" else slice(None)
    if wm.nl == 2:
        rows = pl.ds(pl.multiple_of(s * wm.k, 16), wm.k) if wm.kind == "row" else slice(None)
        return full.at[slice(None) if h is None else h, rows, cols]
    if wm.kind == "col":
        rows = slice(None) if h is None else pl.ds(pl.multiple_of(h * (wm.k // 2), 16), wm.k // 2)
    elif h is None:
        rows = pl.ds(pl.multiple_of(s * wm.k, 16), wm.k)
    else:
        rows = pl.ds(pl.multiple_of(s * wm.k + h * (wm.k // 2), 16), wm.k // 2)
    return full.at[rows, cols]


def _full_shape(wm):
    if wm.kind == "tiny":
        return (N_CHIPS, wm.k, wm.n)
    shape = (wm.k, N_CHIPS * wm.n) if wm.kind == "col" else (N_CHIPS * wm.k, wm.n)
    return shape if wm.nl == 1 else (wm.nl,) + shape


def _handshake(peers):
    barrier = pltpu.get_barrier_semaphore()
    for peer in peers:
        pl.semaphore_signal(barrier, inc=1, device_id=peer, device_id_type=MESH)
    pl.semaphore_wait(barrier, len(peers))


def _all_gather_group(gi, shards):
    wms = AG_GROUPS[gi]
    nw = len(wms)

    def body(*refs):
        sh, full = refs[:nw], refs[nw:2 * nw]
        ici_s, ici_r, pass_s, pass_r, own_s, own_r = refs[2 * nw:]
        x, y, c, _ = _position()
        me, sibling = 2 * x + y, (x, y, 1 - c)
        first, second, diagonal = (x ^ (1 - c), y ^ c), (x ^ c, y ^ (1 - c)), (1 - x, 1 - y)
        chip_id = lambda chip: 2 * chip[0] + chip[1]
        _handshake([(*first, c), (*second, c), sibling])

        def rcopy(src, dst, s_sem, r_sem, to):
            return pltpu.make_async_remote_copy(src_ref=src, dst_ref=dst, send_sem=s_sem, recv_sem=r_sem,
                                                device_id=to, device_id_type=MESH)

        started = []

        def go(cp):
            cp.start()
            started.append(cp)

        for i, wm in enumerate(wms):
            half, dst = _shard_half(sh[i], wm, c), _region(full[i], wm, me, c)
            go(rcopy(half, dst, ici_s.at[i, 0], ici_r.at[i, 0], (*first, c)))
            go(rcopy(half, dst, ici_s.at[i, 1], ici_r.at[i, 1], (*second, c)))
            go(rcopy(sh[i], _region(full[i], wm, me, None), own_s.at[i], own_r.at[i], sibling))
        for i, wm in enumerate(wms):
            got = _region(full[i], wm, chip_id(first), c)
            rcopy(got, got, ici_s.at[i, 0], ici_r.at[i, 0], sibling).wait_recv()
            go(rcopy(got, got, ici_s.at[i, 2], ici_r.at[i, 2], (*second, c)))
            if wm.kind != "tiny":
                go(rcopy(got, got, pass_s.at[i, 0], pass_r.at[i, 0], sibling))
        for i, wm in enumerate(wms):
            for j, chip in ((1, second), (2, diagonal)):
                got = _region(full[i], wm, chip_id(chip), c)
                rcopy(got, got, ici_s.at[i, j], ici_r.at[i, j], sibling).wait_recv()
                if wm.kind != "tiny":
                    go(rcopy(got, got, pass_s.at[i, j], pass_r.at[i, j], sibling))
        for i, wm in enumerate(wms):
            mine = _region(full[i], wm, me, None)
            rcopy(mine, mine, own_s.at[i], own_r.at[i], sibling).wait_recv()
            if wm.kind != "tiny":
                for j, chip in ((0, second), (1, first), (2, diagonal)):
                    got = _region(full[i], wm, chip_id(chip), 1 - c)
                    rcopy(got, got, pass_s.at[i, j], pass_r.at[i, j], sibling).wait_recv()
        for cp in started:
            cp.wait_send()

    return pl.kernel(
        body, out_type=[jax.ShapeDtypeStruct(_full_shape(wm), s.dtype) for wm, s in zip(wms, shards)],
        mesh=plsc.ScalarSubcoreMesh(axis_name="sequencer", num_cores=1), name=f"ag_group{gi}",
        scratch_types=[pltpu.SemaphoreType.DMA((nw, 3))] * 4 + [pltpu.SemaphoreType.DMA((nw,))] * 2,
        compiler_params=pltpu.CompilerParams(collective_id=gi),
    )(*shards)


def _sequencer_call(body, name, cid, out_types, scratch, args):
    return pl.kernel(
        body, out_type=out_types, mesh=plsc.ScalarSubcoreMesh(axis_name="sequencer", num_cores=1), name=name,
        scratch_types=scratch, compiler_params=pltpu.CompilerParams(collective_id=cid),
    )(*args)


def _pair_exchange(gs, tag, cid):
    n = len(gs)

    def body(*refs):
        g, out, send_sems, recv_sems = refs[:n], refs[n:2 * n], refs[2 * n], refs[2 * n + 1]
        x, y, c, _ = _position()
        _handshake([(x, y, 1 - c)])
        cps = []
        for i in range(n):
            half = g[i].shape[1] // 2
            cps.append(pltpu.make_async_remote_copy(
                src_ref=g[i].at[:, pl.ds(pl.multiple_of((1 - c) * half, 16), half), :], dst_ref=out[i],
                send_sem=send_sems.at[i], recv_sem=recv_sems.at[i], device_id=(x, y, 1 - c), device_id_type=MESH))
            cps[-1].start()
        for cp in cps:
            cp.wait()

    return _sequencer_call(
        body, f"rs_pair_exchange{tag}", cid,
        [jax.ShapeDtypeStruct((a.shape[0], a.shape[1] // 2, a.shape[2]), a.dtype) for a in gs],
        [pltpu.SemaphoreType.DMA((n,)), pltpu.SemaphoreType.DMA((n,))], gs)


def _chip_exchange(ss, tag, cid):
    n = len(ss)

    def body(*refs):
        s, out, send_sems, recv_sems = refs[:n], refs[n:2 * n], refs[2 * n], refs[2 * n + 1]
        x, y, c, chips = _position()
        _handshake([(*chip, c) for chip in chips])
        cps = []
        for i in range(n):
            for j, chip in enumerate(chips):
                cps.append(pltpu.make_async_remote_copy(
                    src_ref=s[i].at[j], dst_ref=out[i].at[j], send_sem=send_sems.at[i, j], recv_sem=recv_sems.at[i, j],
                    device_id=(*chip, c), device_id_type=MESH))
                cps[-1].start()
        for cp in cps:
            cp.wait()

    return _sequencer_call(
        body, f"rs_chip_exchange{tag}", cid, [jax.ShapeDtypeStruct(a.shape, a.dtype) for a in ss],
        [pltpu.SemaphoreType.DMA((n, 3)), pltpu.SemaphoreType.DMA((n, 3))], ss)


def _pair_swap(g8s, tag, cid):
    n = len(g8s)

    def body(*refs):
        g, out, send_sems, recv_sems = refs[:n], refs[n:2 * n], refs[2 * n], refs[2 * n + 1]
        x, y, c, _ = _position()
        _handshake([(x, y, 1 - c)])
        cps = []
        for i in range(n):
            cps.append(pltpu.make_async_remote_copy(
                src_ref=g[i], dst_ref=out[i], send_sem=send_sems.at[i], recv_sem=recv_sems.at[i],
                device_id=(x, y, 1 - c), device_id_type=MESH))
            cps[-1].start()
        for cp in cps:
            cp.wait()

    return _sequencer_call(
        body, f"rs_pair_swap{tag}", cid, [jax.ShapeDtypeStruct(a.shape, a.dtype) for a in g8s],
        [pltpu.SemaphoreType.DMA((n,)), pltpu.SemaphoreType.DMA((n,))], g8s)


def _pair_swap_now(g8s):
    n = len(g8s)

    def body(*refs):
        g, out, send_sems, recv_sems = refs[:n], refs[n:2 * n], refs[2 * n], refs[2 * n + 1]
        x, y, c, _ = _position()
        cps = []
        for i in range(n):
            cps.append(pltpu.make_async_remote_copy(
                src_ref=g[i], dst_ref=out[i], send_sem=send_sems.at[i], recv_sem=recv_sems.at[i],
                device_id=(x, y, 1 - c), device_id_type=MESH))
            cps[-1].start()
        for cp in cps:
            cp.wait()

    return _tc_call(
        body, name="rs_pair_swap_last", in_specs=[ANY] * n, out_specs=[ANY] * n,
        out_shape=[jax.ShapeDtypeStruct(a.shape, a.dtype) for a in g8s],
        scratch_shapes=[pltpu.SemaphoreType.DMA((n,)), pltpu.SemaphoreType.DMA((n,))],
    )(*g8s)


def _all_reduce_small(vecs, owner_major, name):
    n = len(vecs)
    block = lambda i, ref, chip: ref.at[chip] if owner_major[i] else ref
    out_shapes = [a.shape[1:] if owner_major[i] else a.shape for i, a in enumerate(vecs)]

    def body(*refs):
        v, o, gath = refs[:n], refs[n:2 * n], refs[2 * n:3 * n]
        send_sems, recv_sems = refs[3 * n], refs[3 * n + 1]
        x, y, c, _ = _position()
        me = 4 * x + 2 * y + c
        cps = []
        for i in range(n):
            gath[i][me] = block(i, v[i], 2 * x + y)[...]
            for rel in range(1, N_DEV):
                px, py, pc = x ^ (rel >> 2), y ^ ((rel >> 1) & 1), c ^ (rel & 1)
                cps.append(pltpu.make_async_remote_copy(
                    src_ref=block(i, v[i], 2 * px + py), dst_ref=gath[i].at[me], send_sem=send_sems.at[i, rel - 1],
                    recv_sem=recv_sems.at[i, rel - 1], device_id=(px, py, pc), device_id_type=MESH))
        for cp in cps:
            cp.start()
        for i in range(n):
            for rel in range(1, N_DEV):
                pltpu.make_async_remote_copy(
                    src_ref=block(i, v[i], 2 * x + y), dst_ref=gath[i].at[me ^ rel],
                    send_sem=send_sems.at[i, rel - 1], recv_sem=recv_sems.at[i, rel - 1], device_id=(x, y, c),
                    device_id_type=MESH).wait_recv()
        for cp in cps:
            cp.wait_send()
        for i in range(n):
            acc = gath[i][0]
            for d in range(1, N_DEV):
                acc = acc + gath[i][d]
            o[i][...] = acc

    vm = pl.BlockSpec(memory_space=pltpu.VMEM)
    return _tc_call(
        body, name=name, in_specs=[vm] * n, out_specs=[vm] * n,
        out_shape=[jax.ShapeDtypeStruct(s, F32) for s in out_shapes],
        scratch_shapes=[pltpu.VMEM((N_DEV,) + s, F32) for s in out_shapes]
        + [pltpu.SemaphoreType.DMA((n, N_DEV - 1)), pltpu.SemaphoreType.DMA((n, N_DEV - 1))],
    )(*vecs)


def _rope_tables(positions):
    half = QK_ROPE // 2
    inv_freq = 1.0 / (ROPE_THETA ** (jnp.arange(half, dtype=F32) / half))
    ang = positions.astype(F32)[:, None] * inv_freq
    zeros = jnp.zeros((positions.shape[0], LANES - QK_ROPE), F32)
    cos, sin = jnp.cos(ang), jnp.sin(ang)
    return jnp.concatenate([cos, cos, zeros], axis=1), jnp.concatenate([sin, sin, zeros], axis=1)


def _local_step(x, positions, tgt, wf, small, rs):
    cos, sin = _rope_tables(positions)
    w_in, w_out = wf["sc_w_in"], wf["sc_w_out"]
    w_ups, w_downs = (wf["ffn_w_up0"], wf["ffn_w_up1"]), (wf["ffn_w_down0"], wf["ffn_w_down1"])
    w_kv, w_ukv, w_dq, w_uq, w_o = wf["w_kv"], wf["w_ukv"], wf["w_dq"], wf["w_uq"], wf["w_o"]
    attn_norm, ffn_norm = small["attn_norm"], small["ffn_norm"]
    conv_b = small["ffn_conv_b"]

    def ffn_fwd(h, hf, l, then):
        up, a = _ffn_up_gate(hf, w_ups[l], small["ffn_conv_w"][l], conv_b[l:l + 1], f"ffn{l}_up_gate")
        return then(a, w_downs[l], h), (hf, up, a)

    def ffn_bwd(h, dh_out, dh_out_b, l, saved, gi, hooks):
        run = lambda stage: hooks.get(stage, lambda: None)()
        hf, up, a = saved
        d_down = _tn(f"ffn{l}_down_dw", a, dh_out_b, BF16)
        run("down_dw")
        dup, d_cw, d_cb = _gate_bwd(up, small["ffn_conv_w"][l], conv_b[l:l + 1], dh_out_b, w_downs[l],
                                    f"ffn{l}_gate_bwd")
        run("gate_bwd")
        d_up = _dw_ffn_up(f"ffn{l}_up_dw", hf, dup)
        rs.start(gi, {f"ffn_w_down{l}": d_down.reshape(N_CHIPS, F_FF // N_CHIPS, D), f"ffn_w_up{l}": d_up})
        run("up_dw")
        dh, dh_b, d_norm = _dx_norm_bwd(f"ffn{l}_up_dx", dup, w_ups[l], h, ffn_norm[l:l + 1], dh_out)
        run("up_dx")
        return dh, dh_b, d_cw, d_cb, d_norm

    hn0 = _rms_fwd(x, attn_norm[0:1], "attn0_norm")
    z = _nn_parts("sc_in", hn0, w_in, 3, BF16)
    mix = _scmix_fwd(z, small["sc_conv_w"])
    h1, hf0 = _nn_add_norm("sc_out", mix, w_out, x, ffn_norm[0:1])
    h2, ffn0_saved = ffn_fwd(h1, hf0, 0, lambda a, w, h: _nn("ffn0_down", a, w, F32, add=h))

    hn1, hk, cq_pre, cq, q, kvpre, ckv, kr, knv = _attn_prep(
        h2, attn_norm[1:2], small["kv_in_norm"], w_dq, small["q_latent_norm"], w_uq, w_kv, small["kv_latent_norm"],
        w_ukv, cos, sin)
    o = _attn_fwd(q, knv, kr)
    h3, hf1 = _nn_add_norm("attn_out", o, w_o, h2, ffn_norm[1:2])
    (loss, dh4, dh4_b, d_final), ffn1_saved = ffn_fwd(
        h3, hf1, 1, lambda a, w, h: _nn_add_loss("ffn1_down_loss", a, w, h, small["final_norm"], tgt))

    rows = D // N_CHIPS
    dh3, dh3_b, d_cw1, d_cb1, d_fn1 = ffn_bwd(h3, dh4, dh4_b, 1, ffn1_saved, 0, {})

    do = _nt("attn_out_dx", dh3_b, w_o, BF16)
    d_wo = _tn("attn_out_dw", o, dh3_b, BF16)
    rs.pair_sums(0)
    dq, dknv, dkr = _attn_bwd(q, knv, kr, do, cos, sin)
    rs.chip_sums(0)
    dh2, dh2_b, d_wuq, d_wdq, d_wukv, d_wkv, d_an1, d_kvin, d_qln, d_kvln = _attn_prep_bwd(
        dq, dknv, dkr, dh3, h2, hn1, hk, cq_pre, cq, kvpre, ckv, attn_norm[1:2], small["kv_in_norm"], w_dq,
        small["q_latent_norm"], w_uq, w_kv, small["kv_latent_norm"], w_ukv, cos, sin)
    rs.finish(0)
    by_owner = lambda dw: dw.reshape(dw.shape[0], N_CHIPS, -1).transpose(1, 0, 2)
    rs.start(1, {
        "w_o": d_wo.reshape(N_CHIPS, rows, D), "w_uq": by_owner(d_wuq), "w_dq": d_wdq.reshape(N_CHIPS, rows, Q_LORA),
        "w_ukv": by_owner(d_wukv.reshape(2 * KV_LORA, -1)).reshape(N_CHIPS, 2 * KV_LORA, -1),
        "w_kv": d_wkv.reshape(N_CHIPS, rows, KVP),
    })

    dh1, dh1_b, d_cw0, d_cb0, d_fn0 = ffn_bwd(h1, dh2, dh2_b, 0, ffn0_saved, 2, {
        "down_dw": lambda: rs.pair_sums(1), "gate_bwd": lambda: rs.chip_sums(1),
        "up_dw": lambda: (rs.finish(1), rs.pair_sums(2))})

    d_wout = _tn("sc_out_dw", mix, dh1_b, BF16)
    dmix = _nt("sc_out_dx", dh1_b, w_out, BF16)
    dz, d_scw = _scmix_bwd(z, small["sc_conv_w"], dmix)
    d_win = _dw_sc_in(hn0, dz)
    rs.start(3, {"sc_w_out": d_wout.reshape(N_CHIPS, rows, D), "sc_w_in": d_win})
    dx, _, d_an0 = _dx_norm_bwd("sc_in_dx", dz, w_in, x, attn_norm[0:1], dh1)

    small_g = {
        "attn_norm": jnp.concatenate([d_an0, d_an1]), "ffn_norm": jnp.concatenate([d_fn0, d_fn1]),
        "final_norm": d_final, "kv_in_norm": d_kvin, "kv_latent_norm": d_kvln, "q_latent_norm": d_qln,
        "ffn_conv_b": jnp.concatenate([d_cb0, d_cb1]),
        "sc_conv_w": by_owner(d_scw), "ffn_conv_w0": by_owner(d_cw0), "ffn_conv_w1": by_owner(d_cw1),
    }
    return loss, dx, small_g


RS_GROUPS = (("ffn_w_down1", "ffn_w_up1"), ("w_o", "w_uq", "w_dq", "w_ukv", "w_kv"),
             ("ffn_w_down0", "ffn_w_up0"), ("sc_w_out", "sc_w_in"))


class _ReduceScatter:
    def __init__(self, ids, finish):
        self.ids, self.grads, self.step, self.mine, self.sib, self.finish = ids, {}, {}, {}, {}, finish

    def _cid(self, gi):
        return len(AG_GROUPS) + 3 * gi

    def start(self, gi, grads):
        self.grads.update(grads)
        own = [grads[n] for n in RS_GROUPS[gi]]
        self.step[gi] = (own, _pair_exchange(own, gi, self._cid(gi)))

    def pair_sums(self, gi):
        own, ra = self.step[gi]
        sums = _pair_sums(self.ids, own, ra, f"rs_pair_sums{gi}")
        self.step[gi] = (own, ra, _chip_exchange(sums, gi, self._cid(gi) + 1))

    def chip_sums(self, gi):
        own, ra, rb = self.step[gi]
        mine = _chip_sums(self.ids, own, ra, rb, f"rs_chip_sums{gi}")
        self.mine.update(zip(RS_GROUPS[gi], mine))
        last = gi == len(RS_GROUPS) - 1
        swapped = _pair_swap_now(mine) if last else _pair_swap(mine, gi, self._cid(gi) + 2)
        self.sib.update(zip(RS_GROUPS[gi], swapped))


SMALL_REPL = ("attn_norm", "ffn_norm", "final_norm", "kv_in_norm", "kv_latent_norm", "q_latent_norm", "ffn_conv_b")


def _pad_heads(w_uq):
    per_head = w_uq.reshape(Q_LORA, -1, QK_NOPE + QK_ROPE)
    return jnp.pad(per_head, ((0, 0), (0, 0), (0, HEAD_PAD - QK_NOPE - QK_ROPE))).reshape(Q_LORA, -1)


def _pack_kv(w_dkv, w_kr):
    return jnp.concatenate([w_dkv, w_kr, jnp.zeros((w_kr.shape[0], LANES - QK_ROPE), w_kr.dtype)], axis=1)


def kernel(x, positions, attn_norm, ffn_norm, final_norm, sc_w_in, sc_conv_w, sc_w_out, kv_in_norm, w_dkv, kv_latent_norm, w_kr, w_uk, w_uv, w_dq, q_latent_norm, w_uq, w_o, ffn_w_up, ffn_conv_w, ffn_conv_b, ffn_w_down, loss_target, m_attn_norm, m_ffn_norm, m_final_norm, m_sc_w_in, m_sc_conv_w, m_sc_w_out, m_kv_in_norm, m_w_dkv, m_kv_latent_norm, m_w_kr, m_w_uk, m_w_uv, m_w_dq, m_q_latent_norm, m_w_uq, m_w_o, m_ffn_w_up, m_ffn_conv_w, m_ffn_conv_b, m_ffn_w_down, v_attn_norm, v_ffn_norm, v_final_norm, v_sc_w_in, v_sc_conv_w, v_sc_w_out, v_kv_in_norm, v_w_dkv, v_kv_latent_norm, v_w_kr, v_w_uk, v_w_uv, v_w_dq, v_q_latent_norm, v_w_uq, v_w_o, v_ffn_w_up, v_ffn_conv_w, v_ffn_conv_b, v_ffn_w_down):
    names = ("attn_norm", "ffn_norm", "final_norm", "sc_w_in", "sc_conv_w", "sc_w_out", "kv_in_norm", "w_dkv",
             "kv_latent_norm", "w_kr", "w_uk", "w_uv", "w_dq", "q_latent_norm", "w_uq", "w_o", "ffn_w_up",
             "ffn_conv_w", "ffn_conv_b", "ffn_w_down")
    w = dict(zip(names, (attn_norm, ffn_norm, final_norm, sc_w_in, sc_conv_w, sc_w_out, kv_in_norm, w_dkv,
                         kv_latent_norm, w_kr, w_uk, w_uv, w_dq, q_latent_norm, w_uq, w_o, ffn_w_up,
                         ffn_conv_w, ffn_conv_b, ffn_w_down)))
    m = dict(zip(names, (m_attn_norm, m_ffn_norm, m_final_norm, m_sc_w_in, m_sc_conv_w, m_sc_w_out, m_kv_in_norm,
                         m_w_dkv, m_kv_latent_norm, m_w_kr, m_w_uk, m_w_uv, m_w_dq, m_q_latent_norm, m_w_uq, m_w_o,
                         m_ffn_w_up, m_ffn_conv_w, m_ffn_conv_b, m_ffn_w_down)))
    v = dict(zip(names, (v_attn_norm, v_ffn_norm, v_final_norm, v_sc_w_in, v_sc_conv_w, v_sc_w_out, v_kv_in_norm,
                         v_w_dkv, v_kv_latent_norm, v_w_kr, v_w_uk, v_w_uv, v_w_dq, v_q_latent_norm, v_w_uq, v_w_o,
                         v_ffn_w_up, v_ffn_conv_w, v_ffn_conv_b, v_ffn_w_down)))

    _ORDER[0] = None
    ix, iy, ic = lax.axis_index("x"), lax.axis_index("y"), lax.axis_index("c")
    chip = 2 * ix + iy
    ids = jnp.stack([ic, chip]).astype(jnp.int32)

    def shards_of(t):
        return {
            "sc_w_in": t["sc_w_in"][0], "sc_w_out": t["sc_w_out"][0], "ffn_w_up": t["ffn_w_up"],
            "ffn_w_down": t["ffn_w_down"], "w_kv": _pack_kv(t["w_dkv"], t["w_kr"]),
            "w_ukv": jnp.stack([t["w_uk"], t["w_uv"]]), "w_dq": t["w_dq"][0], "w_uq": _pad_heads(t["w_uq"][0]),
            "w_o": t["w_o"][0],
        }

    ws, ms, vs = shards_of(w), shards_of(m), shards_of(v)

    def ag_shard(name):
        if name == "sc_conv_w":
            return sc_conv_w[0]
        if name == "ffn_conv_w":
            return ffn_conv_w.reshape(6, -1)
        if name[:-1] in ("ffn_w_up", "ffn_w_down"):
            return ws[name[:-1]][int(name[-1])].astype(BF16)
        return ws[name].astype(BF16)

    wf = {}
    for gi, wms in enumerate(AG_GROUPS):
        fulls = _all_gather_group(gi, [ag_shard(wm.name) for wm in wms])
        wf.update({wm.name: f for wm, f in zip(wms, fulls)})
    small = {
        "attn_norm": attn_norm, "ffn_norm": ffn_norm, "final_norm": final_norm[None], "kv_in_norm": kv_in_norm[None],
        "kv_latent_norm": kv_latent_norm[None], "q_latent_norm": q_latent_norm, "ffn_conv_b": ffn_conv_b,
        "sc_conv_w": wf["sc_conv_w"].transpose(1, 0, 2).reshape(3, D),
        "ffn_conv_w": wf["ffn_conv_w"].reshape(N_CHIPS, 2, 3, -1).transpose(1, 2, 0, 3).reshape(2, 3, F_FF),
    }

    res = {}

    held = {
        "ffn_w_up0": [("ffn_w_up", dict(layer=0))], "ffn_w_up1": [("ffn_w_up", dict(layer=1))],
        "ffn_w_down0": [("ffn_w_down", dict(layer=0))], "ffn_w_down1": [("ffn_w_down", dict(layer=1))],
        "sc_w_in": [("sc_w_in", dict(layer=0))], "sc_w_out": [("sc_w_out", dict(layer=0))],
        "w_dq": [("w_dq", dict(layer=0))], "w_o": [("w_o", dict(layer=0))], "w_uq": [("w_uq", {})],
        "w_kv": [("w_dkv", dict(gcols=(0, KV_LORA))), ("w_kr", dict(gcols=(KV_LORA, KV_LORA + QK_ROPE)))],
        "w_ukv": [("w_uk", dict(owner=0)), ("w_uv", dict(owner=1))],
    }

    def adamw_group(gi):
        items = []
        for key in RS_GROUPS[gi]:
            for n, opts in held[key]:
                w_, m_, v_ = (ws[n], ms[n], vs[n]) if n == "w_uq" else (w[n], m[n], v[n])
                items.append(dict(name=n, w=w_, m=m_, v=v_, g_mine=rs.mine[key], g_sib=rs.sib[key],
                                  prev=res.get(n) if "layer" in opts and w_.shape[0] > 1 else None, **opts))
        for it, out in zip(items, _adamw_shards(ids, items, f"adamw_group{gi}")):
            res[it["name"]] = out

    rs = _ReduceScatter(ids, adamw_group)
    loss, dx, small_g = _local_step(x[0], positions[0], loss_target[0], wf, small, rs)

    rs.chip_sums(2)
    rs.pair_sums(3)

    s_names = list(small_g)
    reduced = _all_reduce_small([small_g[n] for n in s_names] + [loss], [small_g[n].ndim == 3 for n in s_names] + [False],
                                "ar_small")
    sg, loss_out = dict(zip(s_names, reduced[:-1])), reduced[-1][0, 0]

    row = lambda n: (lambda t: t[n][None])
    small_2d = {
        "attn_norm": (sg["attn_norm"], lambda t: t["attn_norm"]), "ffn_norm": (sg["ffn_norm"], lambda t: t["ffn_norm"]),
        "final_norm": (sg["final_norm"], row("final_norm")), "kv_in_norm": (sg["kv_in_norm"], row("kv_in_norm")),
        "kv_latent_norm": (sg["kv_latent_norm"], row("kv_latent_norm")),
        "q_latent_norm": (sg["q_latent_norm"], lambda t: t["q_latent_norm"]),
        "ffn_conv_b": (sg["ffn_conv_b"], lambda t: t["ffn_conv_b"]),
        "sc_conv_w": (sg["sc_conv_w"], lambda t: t["sc_conv_w"][0]),
        "ffn_conv_w0": (sg["ffn_conv_w0"], lambda t: t["ffn_conv_w"][0]),
        "ffn_conv_w1": (sg["ffn_conv_w1"], lambda t: t["ffn_conv_w"][1]),
    }
    s_keys = list(small_2d)
    small_grads = [small_2d[k][0] for k in s_keys]
    views = lambda tree: [small_2d[k][1](tree) for k in s_keys]
    small_res = _adamw_small(views(w), small_grads, views(m), views(v))

    def restore(vals):
        by = dict(zip(s_keys, vals))
        out = {n: by[n].reshape(w[n].shape) for n in SMALL_REPL}
        out["sc_conv_w"] = by["sc_conv_w"][None]
        out["ffn_conv_w"] = jnp.stack([by["ffn_conv_w0"], by["ffn_conv_w1"]])
        return out

    grads = restore(small_grads)
    rs.finish(2)
    rs.chip_sums(3)
    rs.finish(3)
    outs = [grads, {}, {}, {}]
    for k, dst in enumerate(outs):
        for n in res:
            dst[n] = res[n][k]
        unpadded = res["w_uq"][k].reshape(Q_LORA, -1, HEAD_PAD)[:, :, :QK_NOPE + QK_ROPE]
        dst["w_uq"] = unpadded.reshape(w_uq.shape)
    grads, delta, new_m, new_v = outs
    for vals, dst in zip(small_res, (delta, new_m, new_v)):
        dst.update(restore(vals))

    _ORDER[0] = None
    return (loss_out, dx[None], *[grads[n] for n in names], *[delta[n] for n in names],
            *[new_m[n] for n in names], *[new_v[n] for n in names])
```

```python
from typing import NamedTuple

import jax
import jax.numpy as jnp
from jax import lax
from jax.experimental import pallas as pl
from jax.experimental.pallas import tpu as pltpu
from jax.experimental.pallas import tpu_sc as plsc

F32 = jnp.float32
BF16 = jnp.bfloat16

T = 2048
D = 1024
F_FF = 2816
N_HEADS = 8
QK_NOPE = 128
QK_ROPE = 64
V_HEAD = 128
Q_LORA = 384
KV_LORA = 256
CHUNK_SHIFT = 6
ROPE_THETA = 10000.0
EPS = 1e-6
NEG_INF = -1e30
HEAD_PAD = 256
KVP = KV_LORA + 128

ADAM_LR = 0.001
ADAM_B1 = 0.9
ADAM_B2 = 0.999
ADAM_EPS = 1e-08
ADAM_WD = 0.01
ADAM_STEP = 10

N_CHIPS = 4
N_DEV = 8
LANES = 128
TC = 256
V7X_VMEM_LIMIT = 56 * 1024 * 1024

MESH = pl.DeviceIdType.MESH
ANY = pl.BlockSpec(memory_space=pl.ANY)


class _W(NamedTuple):
    name: str
    kind: str
    nl: int
    k: int
    n: int


AG_GROUPS = (
    (_W("sc_w_in", "col", 1, D, 3 * D // N_CHIPS), _W("sc_conv_w", "tiny", 1, 3, D // N_CHIPS),
     _W("ffn_conv_w", "tiny", 1, 6, F_FF // N_CHIPS), _W("sc_w_out", "row", 1, D // N_CHIPS, D)),
    (_W("ffn_w_up0", "col", 1, D, 2 * F_FF // N_CHIPS),),
    (_W("ffn_w_down0", "row", 1, F_FF // N_CHIPS, D),),
    (_W("w_kv", "row", 1, D // N_CHIPS, KVP), _W("w_ukv", "col", 2, KV_LORA, N_HEADS * QK_NOPE // N_CHIPS),
     _W("w_dq", "row", 1, D // N_CHIPS, Q_LORA),
     _W("w_uq", "col", 1, Q_LORA, N_HEADS * HEAD_PAD // N_CHIPS),
     _W("w_o", "row", 1, N_HEADS * V_HEAD // N_CHIPS, D)),
    (_W("ffn_w_up1", "col", 1, D, 2 * F_FF // N_CHIPS), _W("ffn_w_down1", "row", 1, F_FF // N_CHIPS, D)),
)


def _cp(*sem):
    return pltpu.CompilerParams(dimension_semantics=sem, vmem_limit_bytes=V7X_VMEM_LIMIT)


_ORDER = [None]


def _tc_call(body, *, name, out_shape, in_specs=None, out_specs=None, grid=(), scratch_shapes=(), prefetch=0,
             input_output_aliases=None, compiler_params=None):
    def run(*args):
        specs = [pl.BlockSpec(memory_space=pltpu.VMEM)] * (len(args) - prefetch) if in_specs is None else list(in_specs)
        inner, dep = body, _ORDER[0]
        if dep is not None:
            unread = prefetch + len(specs)
            specs, args = specs + [ANY], (*args, dep)

            def inner(*refs):
                return body(*refs[:unread], *refs[unread + 1:])

        kwargs = dict(name=name, out_shape=out_shape, input_output_aliases=input_output_aliases or {},
                      compiler_params=compiler_params)
        if prefetch:
            kwargs["grid_spec"] = pltpu.PrefetchScalarGridSpec(
                num_scalar_prefetch=prefetch, grid=grid, in_specs=specs, out_specs=out_specs,
                scratch_shapes=scratch_shapes)
        else:
            kwargs.update(grid=grid, in_specs=specs, scratch_shapes=scratch_shapes)
            if out_specs is not None:
                kwargs["out_specs"] = out_specs
        out = pl.pallas_call(inner, **kwargs)(*args)
        _ORDER[0] = out[0] if isinstance(out, (list, tuple)) else out
        return out

    return run


def _tile(n, cands):
    for c in cands:
        if n % c == 0:
            return c
    raise ValueError(f"no tile for {n}")


NN_DIMS = (((1,), (0,)), ((), ()))
NT_DIMS = (((1,), (1,)), ((), ()))
TN_DIMS = (((0,), (0,)), ((), ()))
M_TILES = (1024, 512, 384, 256, 128)
N_TILES = (1408, 1024, 768, 512, 384, 256, 128)
MM_BLOCK_BYTES = 36 * 1024 * 1024


def _fit(m, n, block_bytes, m_tiles=M_TILES, n_tiles=N_TILES):
    for tm in [c for c in m_tiles if m % c == 0]:
        for tn in [c for c in n_tiles if n % c == 0]:
            if 2 * block_bytes(tm, tn) + 4 * tm * tn <= MM_BLOCK_BYTES:
                return tm, tn
    raise ValueError(f"no tiles for {m} x {n}")


def _size(x):
    return x.dtype.itemsize


def _mm(name, a, b, dims, grid, a_spec, b_spec, o_spec, o_sds, add=None, red=None, acc_shape=None):
    n_red = None if red is None else grid[red]

    def body(*refs):
        a_ref, b_ref = refs[0], refs[1]
        add_ref = refs[2] if add is not None else None
        o_ref = refs[3] if add is not None else refs[2]
        part = lax.dot_general(a_ref[...].astype(BF16), b_ref[...].astype(BF16), dims, preferred_element_type=F32)
        if red is None:
            if add is not None:
                part = part + add_ref[...]
            o_ref[...] = part.astype(o_ref.dtype)
            return
        acc_ref = refs[-1]
        r = pl.program_id(red)

        @pl.when(r == 0)
        def _():
            acc_ref[...] = part

        @pl.when(r > 0)
        def _():
            acc_ref[...] += part

        @pl.when(r == n_red - 1)
        def _():
            o_ref[...] = acc_ref[...].astype(o_ref.dtype)

    sem = tuple("arbitrary" if ax == red else "parallel" for ax in range(len(grid)))
    in_specs = [a_spec, b_spec] + ([o_spec] if add is not None else [])
    args = (a, b) + ((add,) if add is not None else ())
    return _tc_call(
        body, name=name, grid=grid, in_specs=in_specs, out_specs=o_spec, out_shape=o_sds,
        scratch_shapes=[] if red is None else [pltpu.VMEM(acc_shape, F32)], compiler_params=_cp(*sem),
    )(*args)


def _nn(name, a, b, out_dtype, add=None, lead=None):
    (m, k), n = a.shape, b.shape[-1]
    osz = jnp.dtype(out_dtype).itemsize + (4 if add is not None else 0)
    tm, tn = _fit(m, n, lambda tm, tn: tm * k * _size(a) + k * tn * _size(b) + tm * tn * osz)
    if lead is None:
        b_spec = pl.BlockSpec((k, tn), lambda i, j: (0, j))
    else:
        b_spec = pl.BlockSpec((None, k, tn), lambda i, j: (lead, 0, j))
    return _mm(name, a, b, NN_DIMS, (m // tm, n // tn), pl.BlockSpec((tm, k), lambda i, j: (i, 0)), b_spec,
               pl.BlockSpec((tm, tn), lambda i, j: (i, j)), jax.ShapeDtypeStruct((m, n), out_dtype), add=add)


def _nn_parts(name, a, b, parts, out_dtype, lead=None, stacked=False):
    m, k = a.shape
    c = b.shape[-1] if stacked else b.shape[-1] // parts
    osz = jnp.dtype(out_dtype).itemsize
    tm, tn = _fit(m, c, lambda tm, tn: tm * k * _size(a) + k * tn * _size(b) + tm * tn * osz)
    nb = c // tn
    if stacked:
        b_spec = pl.BlockSpec((None, k, tn), lambda i, p, j: (p, 0, j))
    elif lead is None:
        b_spec = pl.BlockSpec((k, tn), lambda i, p, j: (0, p * nb + j))
    else:
        b_spec = pl.BlockSpec((None, k, tn), lambda i, p, j: (lead, 0, p * nb + j))
    return _mm(name, a, b, NN_DIMS, (m // tm, parts, nb), pl.BlockSpec((tm, k), lambda i, p, j: (i, 0)), b_spec,
               pl.BlockSpec((None, tm, tn), lambda i, p, j: (p, i, j)), jax.ShapeDtypeStruct((parts, m, c), out_dtype))


def _nt(name, a, b, out_dtype, lead=None):
    (m, k), n = a.shape, b.shape[-2]
    osz = jnp.dtype(out_dtype).itemsize
    tm, tn = _fit(m, n, lambda tm, tn: tm * k * _size(a) + tn * k * _size(b) + tm * tn * osz)
    if lead is None:
        b_spec = pl.BlockSpec((tn, k), lambda i, j: (j, 0))
    else:
        b_spec = pl.BlockSpec((None, tn, k), lambda i, j: (lead, j, 0))
    return _mm(name, a, b, NT_DIMS, (m // tm, n // tn), pl.BlockSpec((tm, k), lambda i, j: (i, 0)), b_spec,
               pl.BlockSpec((tm, tn), lambda i, j: (i, j)), jax.ShapeDtypeStruct((m, n), out_dtype))


def _tn(name, a, b, out_dtype):
    (k, m), n = a.shape, b.shape[1]
    osz = jnp.dtype(out_dtype).itemsize
    tm, tn = _fit(m, n, lambda tm, tn: k * tm * _size(a) + k * tn * _size(b) + tm * tn * osz,
                  m_tiles=(512, 384, 256, 128), n_tiles=(n,) + N_TILES)
    return _mm(name, a, b, TN_DIMS, (m // tm, n // tn), pl.BlockSpec((k, tm), lambda i, j: (0, i)),
               pl.BlockSpec((k, tn), lambda i, j: (0, j)), pl.BlockSpec((tm, tn), lambda i, j: (i, j)),
               jax.ShapeDtypeStruct((m, n), out_dtype))


def _nn_add_norm(name, a, b, add, g):
    (m, k), n = a.shape, b.shape[1]
    tm = 512

    def body(a_ref, b_ref, add_ref, g_ref, h_ref, hn_ref):
        h = jnp.dot(a_ref[...], b_ref[...], preferred_element_type=F32) + add_ref[...]
        h_ref[...] = h
        hn_ref[...] = _rms_rows(h, g_ref[...]).astype(BF16)

    rows = lambda w: pl.BlockSpec((tm, w), lambda i: (i, 0))
    return _tc_call(
        body, name=name, grid=(m // tm,),
        in_specs=[rows(k), pl.BlockSpec((k, n), lambda i: (0, 0)), rows(n), pl.BlockSpec((1, n), lambda i: (0, 0))],
        out_specs=[rows(n), rows(n)],
        out_shape=[jax.ShapeDtypeStruct((m, n), F32), jax.ShapeDtypeStruct((m, n), BF16)], compiler_params=_cp("parallel"),
    )(a, b, add, g)


def _nn_add_loss(name, a, b, add, g, tgt):
    (m, k), n = a.shape, b.shape[1]
    tm = 512

    def body(a_ref, b_ref, add_ref, g_ref, t_ref, loss_ref, dh_ref, dhb_ref, dg_ref):
        xv = jnp.dot(a_ref[...], b_ref[...], preferred_element_type=F32) + add_ref[...]
        gv = g_ref[...]
        r = lax.rsqrt(jnp.mean(xv * xv, axis=1, keepdims=True) + EPS)
        err = xv * r * gv - t_ref[...]
        part = 0.5 * jnp.sum(jnp.mean(err * err, axis=1, keepdims=True), axis=0, keepdims=True)
        dx, dg = _rms_bwd_math(xv, gv, err * (1.0 / n))
        dh_ref[...] = dx
        dhb_ref[...] = dx.astype(BF16)

        @pl.when(pl.program_id(0) == 0)
        def _():
            dg_ref[...] = jnp.zeros_like(dg_ref)
            loss_ref[...] = jnp.zeros_like(loss_ref)

        dg_ref[...] += dg
        loss_ref[...] += jnp.broadcast_to(part, loss_ref.shape)

    rows = lambda w: pl.BlockSpec((tm, w), lambda i: (i, 0))
    vec = pl.BlockSpec((1, n), lambda i: (0, 0))
    return _tc_call(
        body, name=name, grid=(m // tm,),
        in_specs=[rows(k), pl.BlockSpec((k, n), lambda i: (0, 0)), rows(n), vec, rows(n)],
        out_specs=[pl.BlockSpec((1, LANES), lambda i: (0, 0)), rows(n), rows(n), vec],
        out_shape=[jax.ShapeDtypeStruct((1, LANES), F32), jax.ShapeDtypeStruct((m, n), F32),
                   jax.ShapeDtypeStruct((m, n), BF16), jax.ShapeDtypeStruct((1, n), F32)],
        compiler_params=_cp("arbitrary"),
    )(a, b, add, g, tgt)


def _dx_norm_bwd(name, a, b, x, g, add):
    parts, t, c = a.shape
    d = b.shape[0]
    tm = 256

    def body(a_ref, b_ref, x_ref, g_ref, add_ref, dx_ref, dxb_ref, dg_ref):
        dy = None
        for p in range(parts):
            part = lax.dot_general(a_ref[p], b_ref[:, p * c:(p + 1) * c], NT_DIMS, preferred_element_type=F32)
            dy = part if dy is None else dy + part
        dx, dg = _rms_bwd_math(x_ref[...], g_ref[...], dy)
        dx = dx + add_ref[...]
        dx_ref[...] = dx
        dxb_ref[...] = dx.astype(BF16)

        @pl.when(pl.program_id(0) == 0)
        def _():
            dg_ref[...] = jnp.zeros_like(dg_ref)

        dg_ref[...] += dg

    rows = pl.BlockSpec((tm, d), lambda i: (i, 0))
    vec = pl.BlockSpec((1, d), lambda i: (0, 0))
    return _tc_call(
        body, name=name, grid=(t // tm,),
        in_specs=[pl.BlockSpec((parts, tm, c), lambda i: (0, i, 0)), pl.BlockSpec(b.shape, lambda i: (0, 0)), rows, vec,
                  rows],
        out_specs=[rows, rows, vec],
        out_shape=[jax.ShapeDtypeStruct((t, d), F32), jax.ShapeDtypeStruct((t, d), BF16),
                   jax.ShapeDtypeStruct((1, d), F32)],
        compiler_params=_cp("arbitrary"),
    )(a, b, x, g, add)


def _dw_sc_in(hn, dz):
    t, tn, tm = hn.shape[0], TC, D
    per_part, per_chip = D // tn, 3 * D // N_CHIPS // tn
    return _mm("sc_in_dw", hn, dz, TN_DIMS, (D // tm, 3 * D // tn), pl.BlockSpec((t, tm), lambda i, j: (0, i)),
               pl.BlockSpec((None, t, tn), lambda i, j: (j // per_part, 0, j % per_part)),
               pl.BlockSpec((None, tm, tn), lambda i, j: (j // per_chip, i, j % per_chip)),
               jax.ShapeDtypeStruct((N_CHIPS, D, 3 * D // N_CHIPS), BF16))


def _dw_ffn_up(name, hf, dup):
    t, tm, ns = hf.shape[0], D, 2 * F_FF // N_CHIPS
    return _mm(name, hf, dup, TN_DIMS, (N_CHIPS, D // tm), pl.BlockSpec((t, tm), lambda s, i: (0, i)),
               pl.BlockSpec((None, t, ns), lambda s, i: (s // 2, 0, s % 2)),
               pl.BlockSpec((None, tm, ns), lambda s, i: (s, i, 0)), jax.ShapeDtypeStruct((N_CHIPS, D, ns), BF16))


def _rms_fwd(x, g, name):
    t, d = x.shape
    tr = 512

    def body(x_ref, g_ref, o_ref):
        xv = x_ref[...]
        r = lax.rsqrt(jnp.mean(xv * xv, axis=1, keepdims=True) + EPS)
        o_ref[...] = (xv * r * g_ref[...]).astype(o_ref.dtype)

    row = pl.BlockSpec((tr, d), lambda i: (i, 0))
    return _tc_call(
        body, name=name, grid=(t // tr,), in_specs=[row, pl.BlockSpec((1, d), lambda i: (0, 0))],
        out_specs=row, out_shape=jax.ShapeDtypeStruct((t, d), BF16), compiler_params=_cp("parallel"),
    )(x, g)


def _rms_bwd_math(xv, g, dy):
    r = lax.rsqrt(jnp.mean(xv * xv, axis=1, keepdims=True) + EPS)
    xh = xv * r
    gy = dy * g
    dx = r * (gy - xh * jnp.mean(gy * xh, axis=1, keepdims=True))
    dg = jnp.sum(dy * xh, axis=0, keepdims=True)
    return dx, dg


def _rot_half(x):
    lane = lax.broadcasted_iota(jnp.int32, x.shape, 1)
    return jnp.where((lane % QK_ROPE) < QK_ROPE // 2, -pltpu.roll(x, LANES - 32, axis=1),
                     pltpu.roll(x, 32, axis=1))


def _rope_fwd_math(x, cos, sin):
    return x * cos + _rot_half(x) * sin


def _rope_bwd_math(dy, cos, sin):
    return dy * cos - _rot_half(dy * sin)


def _rms_rows(x, g):
    return x * lax.rsqrt(jnp.mean(x * x, axis=1, keepdims=True) + EPS) * g


def _attn_prep(h, g_attn, g_kvin, w_dq, g_ql, w_uq, w_kv, g_kvl, w_ukv, cos, sin):
    t, d = h.shape
    tr = 256
    wq = N_HEADS * HEAD_PAD

    def body(h_ref, ga_ref, gk_ref, wdq_ref, gq_ref, wuq_ref, wkv_ref, gl_ref, wukv_ref, c_ref, s_ref,
             hn_ref, hk_ref, cqp_ref, cq_ref, q_ref, kvp_ref, ckv_ref, kr_ref, knv_ref):
        xv, cv, sv = h_ref[...], c_ref[...], s_ref[...]
        xh = xv * lax.rsqrt(jnp.mean(xv * xv, axis=1, keepdims=True) + EPS)
        hn = (xh * ga_ref[...]).astype(BF16)
        hk = (xh * gk_ref[...]).astype(BF16)
        hn_ref[...], hk_ref[...] = hn, hk
        cq_pre = jnp.dot(hn, wdq_ref[...], preferred_element_type=F32)
        cqp_ref[...] = cq_pre
        cq = _rms_rows(cq_pre, gq_ref[...]).astype(BF16)
        cq_ref[...] = cq
        for hd in range(N_HEADS):
            lo = hd * HEAD_PAD
            qh = jnp.dot(cq, wuq_ref[:, lo:lo + HEAD_PAD], preferred_element_type=F32)
            q_ref[:, lo:lo + QK_NOPE] = qh[:, :QK_NOPE].astype(BF16)
            q_ref[:, lo + QK_NOPE:lo + HEAD_PAD] = _rope_fwd_math(qh[:, QK_NOPE:], cv, sv).astype(BF16)
        kvpre = jnp.dot(hk, wkv_ref[...], preferred_element_type=F32)
        kvp_ref[...] = kvpre
        ckv = _rms_rows(kvpre[:, :KV_LORA], gl_ref[...]).astype(BF16)
        ckv_ref[...] = ckv
        kr_ref[...] = _rope_fwd_math(kvpre[:, KV_LORA:], cv, sv).astype(BF16)
        for p in range(2):
            knv_ref[p] = jnp.dot(ckv, wukv_ref[p], preferred_element_type=F32).astype(BF16)

    rows = lambda w: pl.BlockSpec((tr, w), lambda i: (i, 0))
    whole = lambda a: pl.BlockSpec(a.shape, lambda i: (0,) * a.ndim)
    sds = lambda w, dt: jax.ShapeDtypeStruct((t, w), dt)
    args = (h, g_attn, g_kvin, w_dq, g_ql, w_uq, w_kv, g_kvl, w_ukv, cos, sin)
    return _tc_call(
        body, name="attn_prep", grid=(t // tr,),
        in_specs=[rows(d)] + [whole(a) for a in args[1:9]] + [rows(LANES), rows(LANES)],
        out_specs=[rows(d), rows(d), rows(Q_LORA), rows(Q_LORA), rows(wq), rows(KVP), rows(KV_LORA), rows(LANES),
                   pl.BlockSpec((2, tr, N_HEADS * QK_NOPE), lambda i: (0, i, 0))],
        out_shape=[sds(d, BF16), sds(d, BF16), sds(Q_LORA, F32), sds(Q_LORA, BF16), sds(wq, BF16), sds(KVP, F32),
                   sds(KV_LORA, BF16), sds(LANES, BF16), jax.ShapeDtypeStruct((2, t, N_HEADS * QK_NOPE), BF16)],
        compiler_params=_cp("parallel"),
    )(*args)


def _attn_prep_bwd(dq, dknv, dkr, dh, h, hn, hk, cq_pre, cq, kvpre, ckv, g_attn, g_kvin, w_dq, g_ql, w_uq, w_kv, g_kvl,
                   w_ukv, cos, sin):
    t, d = h.shape
    tr = 256
    n_steps = t // tr
    wq = N_HEADS * HEAD_PAD
    wk = N_HEADS * QK_NOPE

    def body(dq_ref, dknv_ref, dkr_ref, dh_ref, h_ref, hn_ref, hk_ref, cqp_ref, cq_ref, kvp_ref, ckv_ref,
             ga_ref, gk_ref, wdq_ref, gq_ref, wuq_ref, wkv_ref, gl_ref, wukv_ref, c_ref, s_ref,
             dho_ref, dhb_ref, dwuq_ref, dwdq_ref, dwukv_ref, dwkv_ref, dga_ref, dgk_ref, dgq_ref, dgl_ref,
             a_uq, a_dq, a_ukv, a_kv):
        i = pl.program_id(0)

        @pl.when(i == 0)
        def _():
            for ref in (a_uq, a_dq, a_ukv, a_kv, dga_ref, dgk_ref, dgq_ref, dgl_ref):
                ref[...] = jnp.zeros_like(ref)

        dqv = dq_ref[...]
        dcq = lax.dot_general(dqv, wuq_ref[...], NT_DIMS, preferred_element_type=F32)
        a_uq[...] += lax.dot_general(cq_ref[...], dqv, TN_DIMS, preferred_element_type=F32)
        dcq_pre, dg = _rms_bwd_math(cqp_ref[...], gq_ref[...], dcq)
        dgq_ref[...] += dg
        dcq_pre = dcq_pre.astype(BF16)
        dhn = lax.dot_general(dcq_pre, wdq_ref[...], NT_DIMS, preferred_element_type=F32)
        a_dq[...] += lax.dot_general(hn_ref[...], dcq_pre, TN_DIMS, preferred_element_type=F32)
        dckv = None
        for p in range(2):
            dk = dknv_ref[p].astype(BF16)
            part = lax.dot_general(dk, wukv_ref[p], NT_DIMS, preferred_element_type=F32)
            dckv = part if dckv is None else dckv + part
            a_ukv[p] += lax.dot_general(ckv_ref[...], dk, TN_DIMS, preferred_element_type=F32)
        dlat, dg = _rms_bwd_math(kvp_ref[:, :KV_LORA], gl_ref[...], dckv)
        dgl_ref[...] += dg
        dkr_pre = _rope_bwd_math(dkr_ref[...], c_ref[...], s_ref[...])
        dkvpre = jnp.concatenate([dlat, dkr_pre], axis=1).astype(BF16)
        dhk = lax.dot_general(dkvpre, wkv_ref[...], NT_DIMS, preferred_element_type=F32)
        a_kv[...] += lax.dot_general(hk_ref[...], dkvpre, TN_DIMS, preferred_element_type=F32)
        xv = h_ref[...]
        dx1, dg = _rms_bwd_math(xv, ga_ref[...], dhn)
        dga_ref[...] += dg
        dx2, dg = _rms_bwd_math(xv, gk_ref[...], dhk)
        dgk_ref[...] += dg
        dh_new = dh_ref[...] + dx1 + dx2
        dho_ref[...] = dh_new
        dhb_ref[...] = dh_new.astype(BF16)

        @pl.when(i == n_steps - 1)
        def _():
            dwuq_ref[...] = a_uq[...].astype(BF16)
            dwdq_ref[...] = a_dq[...].astype(BF16)
            dwukv_ref[...] = a_ukv[...].astype(BF16)
            dwkv_ref[...] = a_kv[...].astype(BF16)

    rows = lambda w: pl.BlockSpec((tr, w), lambda i: (i, 0))
    whole = lambda shape: pl.BlockSpec(shape, lambda i: (0,) * len(shape))
    weights = (g_attn, g_kvin, w_dq, g_ql, w_uq, w_kv, g_kvl, w_ukv)
    dw_shapes = [(Q_LORA, wq), (d, Q_LORA), (2, KV_LORA, wk), (d, KVP)]
    dg_shapes = [(1, d), (1, d), (1, Q_LORA), (1, KV_LORA)]
    return _tc_call(
        body, name="attn_prep_bwd", grid=(n_steps,),
        in_specs=[rows(wq), pl.BlockSpec((2, tr, wk), lambda i: (0, i, 0)), rows(LANES), rows(d), rows(d), rows(d),
                  rows(d), rows(Q_LORA), rows(Q_LORA), rows(KVP), rows(KV_LORA)]
        + [whole(a.shape) for a in weights] + [rows(LANES), rows(LANES)],
        out_specs=[rows(d), rows(d)] + [whole(s) for s in dw_shapes + dg_shapes],
        out_shape=[jax.ShapeDtypeStruct((t, d), F32), jax.ShapeDtypeStruct((t, d), BF16)]
        + [jax.ShapeDtypeStruct(s, BF16) for s in dw_shapes] + [jax.ShapeDtypeStruct(s, F32) for s in dg_shapes],
        scratch_shapes=[pltpu.VMEM(s, F32) for s in dw_shapes], compiler_params=_cp("arbitrary"),
    )(dq, dknv, dkr, dh, h, hn, hk, cq_pre, cq, kvpre, ckv, *weights, cos, sin)


ROW_CHUNK = 64
HALO = 16
WIN = ROW_CHUNK + 16
LANE_HALVES = (slice(0, LANES), slice(LANES, TC))


def _stage(s_ref, p, src):
    t = src.shape[0]
    s_ref[p, :HALO] = jnp.zeros((HALO, TC), BF16)
    s_ref[p, HALO:HALO + t] = src
    s_ref[p, HALO + t:] = jnp.zeros((HALO, TC), BF16)


def _window(s_ref, p, i, lanes):
    base = pl.multiple_of(i * ROW_CHUNK, ROW_CHUNK)
    return s_ref[p, pl.ds(base, ROW_CHUNK + 2 * HALO), lanes].astype(F32)[8:8 + WIN]


def _valid(x):
    return x[8:8 + ROW_CHUNK]


def _prev(x, k):
    return pltpu.roll(x, k, axis=0)


def _next(x, k):
    return pltpu.roll(x, WIN - k, axis=0)


def _taps(w_ref, lanes):
    return w_ref[0:1, lanes], w_ref[1:2, lanes], w_ref[2:3, lanes]


def _fold8(x):
    return jnp.sum(x.reshape(ROW_CHUNK // 8, 8, x.shape[-1]), axis=0)


def _store_rows(ref, idx, i, lanes, x):
    rows = pl.ds(pl.multiple_of(i * ROW_CHUNK, ROW_CHUNK), ROW_CHUNK)
    ref[(*idx, rows, lanes)] = x.astype(ref.dtype)


def _for_chunks(t, chunk):
    def step(i, carry):
        for lanes in LANE_HALVES:
            chunk(i, lanes)
        return carry

    lax.fori_loop(0, t // ROW_CHUNK, step, 0)


def _write_col_sums(acc_ref, outs):
    for k, (ref, row) in enumerate(outs):
        ref[row:row + 1, :] = jnp.sum(acc_ref[k], axis=0, keepdims=True)


def _shift_down(x, k):
    row = lax.broadcasted_iota(jnp.int32, x.shape, 0)
    return jnp.where(row >= k, pltpu.roll(x, k, axis=0), 0.0)


def _shift_up(x, k):
    n = x.shape[0]
    row = lax.broadcasted_iota(jnp.int32, x.shape, 0)
    return jnp.where(row < n - k, pltpu.roll(x, n - k, axis=0), 0.0)


def _conv3(x, w_ref):
    return _shift_down(x, 2) * w_ref[0:1, :] + _shift_down(x, 1) * w_ref[1:2, :] + x * w_ref[2:3, :]


def _col(parts, t):
    if parts is None:
        return pl.BlockSpec((t, TC), lambda j: (0, j))
    return pl.BlockSpec((parts, t, TC), lambda j: (0, 0, j))


def _staging(parts, t):
    return pltpu.VMEM((parts, t + 2 * HALO, TC), BF16)


def _scmix_fwd(z, w):
    t = z.shape[1]

    def body(z_ref, w_ref, m_ref):
        b, c, u = (z_ref[p].astype(F32) for p in range(3))
        m_ref[...] = (b * _conv3(c * u, w_ref)).astype(BF16)

    return _tc_call(
        body, name="scmix_fwd", grid=(D // TC,), in_specs=[_col(3, t), pl.BlockSpec((3, TC), lambda j: (0, j))],
        out_specs=_col(None, t), out_shape=jax.ShapeDtypeStruct((t, D), BF16), compiler_params=_cp("parallel"),
    )(z, w)


def _scmix_bwd(z, w, dm):
    t = z.shape[1]

    def body(z_ref, w_ref, dm_ref, dz_ref, dw_ref, s_ref, acc_ref):
        for p in range(3):
            _stage(s_ref, p, z_ref[p])
        _stage(s_ref, 3, dm_ref[...])
        acc_ref[...] = jnp.zeros_like(acc_ref)

        def chunk(i, lanes):
            w0, w1, w2 = _taps(w_ref, lanes)
            b, c, u, dm = (_window(s_ref, p, i, lanes) for p in range(4))
            cu = c * u
            cu1, cu2 = _prev(cu, 1), _prev(cu, 2)
            _store_rows(dz_ref, (0,), i, lanes, _valid(dm * (cu2 * w0 + cu1 * w1 + cu * w2)))
            dcv = dm * b
            dcu = dcv * w2 + _next(dcv, 1) * w1 + _next(dcv, 2) * w0
            _store_rows(dz_ref, (1,), i, lanes, _valid(dcu * u))
            _store_rows(dz_ref, (2,), i, lanes, _valid(dcu * c))
            for k, shifted in enumerate((cu2, cu1, cu)):
                acc_ref[k, :, lanes] += _fold8(_valid(dcv * shifted))

        _for_chunks(t, chunk)
        _write_col_sums(acc_ref, [(dw_ref, 0), (dw_ref, 1), (dw_ref, 2)])

    wspec = pl.BlockSpec((3, TC), lambda j: (0, j))
    return _tc_call(
        body, name="scmix_bwd", grid=(D // TC,), in_specs=[_col(3, t), wspec, _col(None, t)],
        out_specs=[_col(3, t), wspec],
        out_shape=[jax.ShapeDtypeStruct((3, t, D), BF16), jax.ShapeDtypeStruct((3, D), F32)],
        scratch_shapes=[_staging(4, t), pltpu.VMEM((3, 8, TC), F32)], compiler_params=_cp("parallel"),
    )(z, w, dm)


def _ffn_up_gate(hf, w_up, w, bias, name):
    t, d = hf.shape
    nb = F_FF // TC

    def body(hf_ref, wg_ref, wv_ref, w_ref, b_ref, up_ref, a_ref, prev_ref):
        @pl.when(pl.program_id(0) == 0)
        def _():
            prev_ref[...] = jnp.zeros_like(prev_ref)

        gc = _conv3(prev_ref[0].astype(F32), w_ref) + b_ref[...]
        a_ref[...] = (gc * jax.nn.sigmoid(gc) * prev_ref[1].astype(F32)).astype(BF16)
        hv = hf_ref[...]
        up_ref[0] = jnp.dot(hv, wg_ref[...], preferred_element_type=F32).astype(BF16)
        up_ref[1] = jnp.dot(hv, wv_ref[...], preferred_element_type=F32).astype(BF16)
        prev_ref[...] = up_ref[...]

    tile = lambda j: jnp.minimum(j, nb - 1)
    gated = lambda j: jnp.maximum(j - 1, 0)
    return _tc_call(
        body, name=name, grid=(nb + 1,),
        in_specs=[pl.BlockSpec((t, d), lambda j: (0, 0)), pl.BlockSpec((d, TC), lambda j: (0, tile(j))),
                  pl.BlockSpec((d, TC), lambda j: (0, nb + tile(j))), pl.BlockSpec((3, TC), lambda j: (0, gated(j))),
                  pl.BlockSpec((1, TC), lambda j: (0, gated(j)))],
        out_specs=[pl.BlockSpec((2, t, TC), lambda j: (0, 0, tile(j))), pl.BlockSpec((t, TC), lambda j: (0, gated(j)))],
        out_shape=[jax.ShapeDtypeStruct((2, t, F_FF), BF16), jax.ShapeDtypeStruct((t, F_FF), BF16)],
        scratch_shapes=[pltpu.VMEM((2, t, TC), BF16)], compiler_params=_cp("arbitrary"),
    )(hf, w_up, w_up, w, bias)


def _gate_bwd(up, w, bias, dh, w_down, name):
    t, d = dh.shape

    def body(u_ref, w_ref, b_ref, dh_ref, wd_ref, du_ref, dw_ref, db_ref, s_ref, acc_ref):
        for p in range(2):
            _stage(s_ref, p, u_ref[p])
        _stage(s_ref, 2, lax.dot_general(dh_ref[...], wd_ref[...], NT_DIMS, preferred_element_type=F32).astype(BF16))
        acc_ref[...] = jnp.zeros_like(acc_ref)

        def chunk(i, lanes):
            w0, w1, w2 = _taps(w_ref, lanes)
            g, v, da = (_window(s_ref, p, i, lanes) for p in range(3))
            g1, g2 = _prev(g, 1), _prev(g, 2)
            gc = g2 * w0 + g1 * w1 + g * w2 + b_ref[:, lanes]
            sg = jax.nn.sigmoid(gc)
            _store_rows(du_ref, (1,), i, lanes, _valid(da * (gc * sg)))
            dgc = da * v * (sg * (1.0 + gc * (1.0 - sg)))
            _store_rows(du_ref, (0,), i, lanes, _valid(dgc * w2 + _next(dgc, 1) * w1 + _next(dgc, 2) * w0))
            for k, shifted in enumerate((g2, g1, g)):
                acc_ref[k, :, lanes] += _fold8(_valid(dgc * shifted))
            acc_ref[3, :, lanes] += _fold8(_valid(dgc))

        _for_chunks(t, chunk)
        _write_col_sums(acc_ref, [(dw_ref, 0), (dw_ref, 1), (dw_ref, 2), (db_ref, 0)])

    wspec = pl.BlockSpec((3, TC), lambda j: (0, j))
    bspec = pl.BlockSpec((1, TC), lambda j: (0, j))
    return _tc_call(
        body, name=name, grid=(F_FF // TC,),
        in_specs=[_col(2, t), wspec, bspec, pl.BlockSpec((t, d), lambda j: (0, 0)), pl.BlockSpec((TC, d), lambda j: (j, 0))],
        out_specs=[_col(2, t), wspec, bspec],
        out_shape=[jax.ShapeDtypeStruct((2, t, F_FF), BF16), jax.ShapeDtypeStruct((3, F_FF), F32),
                   jax.ShapeDtypeStruct((1, F_FF), F32)],
        scratch_shapes=[_staging(3, t), pltpu.VMEM((4, 8, TC), F32)], compiler_params=_cp("parallel"),
    )(up, w, bias, dh, w_down)


ATT_TQ = 256
ATT_SCALE = (QK_NOPE + QK_ROPE) ** -0.5


def _key_ranges(lvl):
    lo = lvl * ATT_TQ
    return ([(0, lo, False)] if lvl else []) + [(lo, lo + ATT_TQ, True)]


FWD_HEADS = 4
BWD_HEADS = 2


def _fill_keys(k_ref, kn_ref, kr_ref):
    @pl.when(pl.program_id(1) == 0)
    def _():
        for hh in range(k_ref.shape[0]):
            k_ref[hh, :, :QK_NOPE] = kn_ref[:, hh * QK_NOPE:(hh + 1) * QK_NOPE]
            k_ref[hh, :, QK_NOPE:] = kr_ref[...]


def _attn_probs(q, k_ref, lvl):
    scores = []
    for lo, hi, diagonal in _key_ranges(lvl):
        s = lax.dot_general(q, k_ref[lo:hi, :], NT_DIMS, preferred_element_type=F32) * ATT_SCALE
        if diagonal:
            row = lax.broadcasted_iota(jnp.int32, s.shape, 0)
            col = lax.broadcasted_iota(jnp.int32, s.shape, 1)
            seen = lax.shift_right_logical(col, CHUNK_SHIFT) <= lax.shift_right_logical(row, CHUNK_SHIFT)
            s = jnp.where(seen, s, NEG_INF)
        scores.append(s)
    m = jnp.max(scores[0], axis=1, keepdims=True)
    for s in scores[1:]:
        m = jnp.maximum(m, jnp.max(s, axis=1, keepdims=True))
    ps = [jnp.exp(s - m) for s in scores]
    total = jnp.sum(ps[0], axis=1, keepdims=True)
    for p in ps[1:]:
        total = total + jnp.sum(p, axis=1, keepdims=True)
    inv = 1.0 / total
    return [p * inv for p in ps]


def _attn_probs_t(q, k_ref, lvl):
    scores = []
    for lo, hi, diagonal in _key_ranges(lvl):
        s = lax.dot_general(k_ref[lo:hi, :], q, NT_DIMS, preferred_element_type=F32) * ATT_SCALE
        if diagonal:
            key = lax.broadcasted_iota(jnp.int32, s.shape, 0)
            qry = lax.broadcasted_iota(jnp.int32, s.shape, 1)
            seen = lax.shift_right_logical(key, CHUNK_SHIFT) <= lax.shift_right_logical(qry, CHUNK_SHIFT)
            s = jnp.where(seen, s, NEG_INF)
        scores.append(s)
    m = jnp.max(scores[0], axis=0, keepdims=True)
    for s in scores[1:]:
        m = jnp.maximum(m, jnp.max(s, axis=0, keepdims=True))
    ps = [jnp.exp(s - m) for s in scores]
    total = jnp.sum(ps[0], axis=0, keepdims=True)
    for p in ps[1:]:
        total = total + jnp.sum(p, axis=0, keepdims=True)
    inv = 1.0 / total
    return [p * inv for p in ps]


def _per_query_block(qi, n_blocks, branch):
    for lvl in range(n_blocks):
        pl.when(qi == lvl)(lambda lvl=lvl: branch(lvl))


def _attn_specs(t, g):
    q = pl.BlockSpec((ATT_TQ, g * HEAD_PAD), lambda h, i: (i, h))
    kn = pl.BlockSpec((None, t, g * QK_NOPE), lambda h, i: (0, 0, h))
    kr = pl.BlockSpec((t, LANES), lambda h, i: (0, 0))
    v = pl.BlockSpec((None, t, g * V_HEAD), lambda h, i: (1, 0, h))
    o = pl.BlockSpec((ATT_TQ, g * V_HEAD), lambda h, i: (i, h))
    return q, kn, kr, v, o


def _attn_fwd(q, knv, kr):
    t = q.shape[0]

    def body(q_ref, kn_ref, kr_ref, v_ref, o_ref, k_ref):
        _fill_keys(k_ref, kn_ref, kr_ref)

        def branch(lvl):
            for hh in range(FWD_HEADS):
                vcols = slice(hh * V_HEAD, (hh + 1) * V_HEAD)
                ps = _attn_probs(q_ref[:, hh * HEAD_PAD:(hh + 1) * HEAD_PAD], k_ref.at[hh], lvl)
                o = None
                for p, (lo, hi, _) in zip(ps, _key_ranges(lvl)):
                    part = jnp.dot(p.astype(BF16), v_ref[lo:hi, vcols], preferred_element_type=F32)
                    o = part if o is None else o + part
                o_ref[:, vcols] = o.astype(BF16)

        _per_query_block(pl.program_id(1), t // ATT_TQ, branch)

    qs, kns, krs, vs, os_ = _attn_specs(t, FWD_HEADS)
    return _tc_call(
        body, name="attn_fwd", grid=(N_HEADS // FWD_HEADS, t // ATT_TQ), in_specs=[qs, kns, krs, vs],
        out_specs=os_, out_shape=jax.ShapeDtypeStruct((t, N_HEADS * V_HEAD), BF16),
        scratch_shapes=[pltpu.VMEM((FWD_HEADS, t, HEAD_PAD), BF16)], compiler_params=_cp("parallel", "arbitrary"),
    )(q, knv, kr, knv)


def _attn_bwd(q, knv, kr, do, cos, sin):
    t = q.shape[0]

    def body(q_ref, kn_ref, kr_ref, v_ref, do_ref, c_ref, s_ref, dq_ref, dknv_ref, dkr_ref, k_ref, dk_ref):
        h, qi = pl.program_id(0), pl.program_id(1)
        _fill_keys(k_ref, kn_ref, kr_ref)

        @pl.when(qi == 0)
        def _():
            dknv_ref[1] = jnp.zeros(dknv_ref.shape[1:], F32)
            dk_ref[...] = jnp.zeros_like(dk_ref)

        @pl.when((qi == 0) & (h == 0))
        def _():
            dkr_ref[...] = jnp.zeros_like(dkr_ref)

        def branch(lvl):
            ranges = _key_ranges(lvl)
            for hh in range(BWD_HEADS):
                qcols = slice(hh * HEAD_PAD, (hh + 1) * HEAD_PAD)
                vcols = slice(hh * V_HEAD, (hh + 1) * V_HEAD)
                qv, dov = q_ref[:, qcols], do_ref[:, vcols]
                ps = _attn_probs_t(qv, k_ref.at[hh], lvl)
                dps = [lax.dot_general(v_ref[lo:hi, vcols], dov, NT_DIMS, preferred_element_type=F32)
                       for lo, hi, _ in ranges]
                di = None
                for p, dp in zip(ps, dps):
                    part = jnp.sum(p * dp, axis=0, keepdims=True)
                    di = part if di is None else di + part
                dq = None
                for p, dp, (lo, hi, _) in zip(ps, dps, ranges):
                    ds = (p * (dp - di) * ATT_SCALE).astype(BF16)
                    part = lax.dot_general(ds, k_ref[hh, lo:hi, :], TN_DIMS, preferred_element_type=F32)
                    dq = part if dq is None else dq + part
                    dk_ref[hh, lo:hi, :] += jnp.dot(ds, qv, preferred_element_type=F32)
                    dknv_ref[1, lo:hi, vcols] += jnp.dot(p.astype(BF16), dov, preferred_element_type=F32)
                dq_ref[:, hh * HEAD_PAD:hh * HEAD_PAD + QK_NOPE] = dq[:, :QK_NOPE].astype(BF16)
                dq_ref[:, hh * HEAD_PAD + QK_NOPE:(hh + 1) * HEAD_PAD] = _rope_bwd_math(
                    dq[:, QK_NOPE:], c_ref[...], s_ref[...]).astype(BF16)

        _per_query_block(qi, t // ATT_TQ, branch)

        @pl.when(qi == t // ATT_TQ - 1)
        def _():
            for hh in range(BWD_HEADS):
                dknv_ref[0, :, hh * QK_NOPE:(hh + 1) * QK_NOPE] = dk_ref[hh, :, :QK_NOPE]
                dkr_ref[...] += dk_ref[hh, :, QK_NOPE:]

    qs, kns, krs, vs, os_ = _attn_specs(t, BWD_HEADS)
    tab = pl.BlockSpec((ATT_TQ, LANES), lambda h, i: (i, 0))
    return _tc_call(
        body, name="attn_bwd", grid=(N_HEADS // BWD_HEADS, t // ATT_TQ), in_specs=[qs, kns, krs, vs, os_, tab, tab],
        out_specs=[qs, pl.BlockSpec((2, t, BWD_HEADS * QK_NOPE), lambda h, i: (0, 0, h)), krs],
        out_shape=[jax.ShapeDtypeStruct((t, N_HEADS * HEAD_PAD), BF16),
                   jax.ShapeDtypeStruct((2, t, N_HEADS * QK_NOPE), F32), jax.ShapeDtypeStruct((t, LANES), F32)],
        scratch_shapes=[pltpu.VMEM((BWD_HEADS, t, HEAD_PAD), BF16), pltpu.VMEM((BWD_HEADS, t, HEAD_PAD), F32)],
        compiler_params=_cp("arbitrary", "arbitrary"),
    )(q, knv, kr, knv, do, cos, sin)


def _adam_math(w, g, m, v):
    nm = ADAM_B1 * m + (1.0 - ADAM_B1) * g
    nv = ADAM_B2 * v + (1.0 - ADAM_B2) * (g * g)
    m_hat = nm / (1.0 - ADAM_B1 ** ADAM_STEP)
    v_hat = nv / (1.0 - ADAM_B2 ** ADAM_STEP)
    return -ADAM_LR * (m_hat / (jnp.sqrt(v_hat) + ADAM_EPS) + ADAM_WD * w), nm, nv


def _adamw_small(ws, gs, ms, vs):
    n = len(ws)

    def body(*refs):
        for i in range(n):
            w_ref, g_ref, m_ref, v_ref = (refs[k * n + i] for k in range(4))
            d_ref, nm_ref, nv_ref = (refs[(4 + k) * n + i] for k in range(3))
            d_ref[...], nm_ref[...], nv_ref[...] = _adam_math(w_ref[...], g_ref[...], m_ref[...], v_ref[...])

    shapes = [jax.ShapeDtypeStruct(a.shape, F32) for a in ws]
    res = _tc_call(body, name="adamw_small", out_shape=shapes * 3)(*ws, *gs, *ms, *vs)
    return res[:n], res[n:2 * n], res[2 * n:]


ADAM_SPLIT = 4


def _adamw_shards(ids, items, name):
    n = len(items)

    def body(ids_ref, *refs):
        outs = refs[len(refs) - 4 * n:]
        for i, it in enumerate(items):
            w_ref, m_ref, v_ref, gm_ref, gs_ref = refs[5 * i:5 * i + 5]
            g_ref, d_ref, nm_ref, nv_ref = outs[4 * i:4 * i + 4]
            cols = slice(*it["gcols"]) if it.get("gcols") else slice(None)
            whose = pl.program_id(0) if it.get("owner") is None else it["owner"]
            mine = whose == ids_ref[0]

            @pl.when(mine)
            def _(g_ref=g_ref, gm_ref=gm_ref, cols=cols):
                g_ref[...] = gm_ref[:, cols]

            @pl.when(jnp.logical_not(mine))
            def _(g_ref=g_ref, gs_ref=gs_ref, cols=cols):
                g_ref[...] = gs_ref[:, cols]

            d_ref[...], nm_ref[...], nv_ref[...] = _adam_math(w_ref[...], g_ref[...], m_ref[...], v_ref[...])

    in_specs, out_specs, out_shape, args, carried, aliases = [], [], [], [ids], [], {}
    for i, it in enumerate(items):
        w = it["w"]
        r, c = w.shape[-2:]
        tr = r // 2 // ADAM_SPLIT
        assert tr % 8 == 0, (name, w.shape)
        layer = it.get("layer")
        if layer is None:
            wspec = pl.BlockSpec((tr, c), lambda h, k, ids: (h * ADAM_SPLIT + k, 0))
        else:
            wspec = pl.BlockSpec((None, tr, c), lambda h, k, ids, layer=layer: (layer, h * ADAM_SPLIT + k, 0))
        gc = it["g_mine"].shape[1]
        if it.get("owner") is None:
            gspec = pl.BlockSpec((tr, gc), lambda h, k, ids: (k, 0))
        else:
            gspec = pl.BlockSpec((tr, gc), lambda h, k, ids: (h * ADAM_SPLIT + k, 0))
        in_specs += [wspec] * 3 + [gspec] * 2
        args += [w, it["m"], it["v"], it["g_mine"], it["g_sib"]]
        out_specs += [wspec] * 4
        out_shape += [jax.ShapeDtypeStruct(w.shape, F32)] * 4
        if it.get("prev") is not None:
            for k, p in enumerate(it["prev"]):
                aliases[1 + 5 * n + len(carried)] = 4 * i + k
                carried.append(p)
    res = _tc_call(
        body, name=name, prefetch=1, grid=(2, ADAM_SPLIT), in_specs=in_specs + [ANY] * len(carried),
        out_specs=out_specs, out_shape=out_shape, input_output_aliases=aliases,
        compiler_params=_cp("parallel", "parallel"),
    )(*args, *carried)
    return [res[4 * i:4 * i + 4] for i in range(n)]


def _peer_chip(k_me, j):
    return k_me ^ jnp.where(j == 0, 2, jnp.where(j == 1, 1, 3))


def _pair_sums(ids, gs, ras, name):
    n = len(gs)

    def body(ids_ref, *refs):
        for i in range(n):
            g_ref, ra_ref, o_ref = refs[2 * i], refs[2 * i + 1], refs[2 * n + i]
            o_ref[...] = (g_ref[...].astype(F32) + ra_ref[...].astype(F32)).astype(BF16)

    in_specs, out_specs, out_shape = [], [], []
    for g in gs:
        half, c = g.shape[1] // 2, g.shape[2]
        in_specs += [pl.BlockSpec((None, half, c), lambda j, ids: (_peer_chip(ids[1], j), ids[0], 0)),
                     pl.BlockSpec((None, half, c), lambda j, ids: (_peer_chip(ids[1], j), 0, 0))]
        out_specs.append(pl.BlockSpec((None, half, c), lambda j, ids: (j, 0, 0)))
        out_shape.append(jax.ShapeDtypeStruct((3, half, c), BF16))
    return _tc_call(
        body, name=name, prefetch=1, grid=(3,), in_specs=in_specs, out_specs=out_specs, out_shape=out_shape,
        compiler_params=_cp("parallel"),
    )(ids, *[a for pair in zip(gs, ras) for a in pair])


def _chip_sums(ids, gs, ras, rbs, name):
    n = len(gs)

    def body(ids_ref, *refs):
        for i in range(n):
            g_ref, ra_ref, rb_ref, o_ref = refs[3 * i], refs[3 * i + 1], refs[3 * i + 2], refs[3 * n + i]
            acc = g_ref[...].astype(F32) + ra_ref[...].astype(F32)
            for j in range(3):
                acc = acc + rb_ref[j].astype(F32)
            o_ref[...] = acc

    in_specs, out_specs, out_shape = [], [], []
    for g in gs:
        half, c = g.shape[1] // 2, g.shape[2]
        in_specs += [pl.BlockSpec((None, half, c), lambda i, ids: (ids[1], ids[0], 0)),
                     pl.BlockSpec((None, half, c), lambda i, ids: (ids[1], 0, 0)),
                     pl.BlockSpec((3, half, c), lambda i, ids: (0, 0, 0))]
        out_specs.append(pl.BlockSpec((half, c), lambda i, ids: (0, 0)))
        out_shape.append(jax.ShapeDtypeStruct((half, c), F32))
    return _tc_call(
        body, name=name, prefetch=1, grid=(1,), in_specs=in_specs, out_specs=out_specs, out_shape=out_shape,
        compiler_params=_cp("arbitrary"),
    )(ids, *[a for trio in zip(gs, ras, rbs) for a in trio])


def _position():
    x, y, c = lax.axis_index("x"), lax.axis_index("y"), lax.axis_index("c")
    chips = [(1 - x, y), (x, 1 - y), (1 - x, 1 - y)]
    return x, y, c, chips


def _shard_half(ref, wm, h):
    if wm.kind == "tiny":
        return ref
    if wm.nl == 2:
        return ref.at[h]
    return ref.at[pl.ds(pl.multiple_of(h * (wm.k // 2), 16), wm.k // 2), :]


def _region(full, wm, s, h):
    if wm.kind == "tiny":
        return full.at[s]
    cols = pl.ds(pl.multiple_of(s * wm.n, LANES), wm.n) if wm.kind == "col" else slice(None)
    if wm.nl == 2:
        rows = pl.ds(pl.multiple_of(s * wm.k, 16), wm.k) if wm.kind == "row" else slice(None)
        return full.at[slice(None) if h is None else h, rows, cols]
    if wm.kind == "col":
        rows = slice(None) if h is None else pl.ds(pl.multiple_of(h * (wm.k // 2), 16), wm.k // 2)
    elif h is None:
        rows = pl.ds(pl.multiple_of(s * wm.k, 16), wm.k)
    else:
        rows = pl.ds(pl.multiple_of(s * wm.k + h * (wm.k // 2), 16), wm.k // 2)
    return full.at[rows, cols]


def _full_shape(wm):
    if wm.kind == "tiny":
        return (N_CHIPS, wm.k, wm.n)
    shape = (wm.k, N_CHIPS * wm.n) if wm.kind == "col" else (N_CHIPS * wm.k, wm.n)
    return shape if wm.nl == 1 else (wm.nl,) + shape


def _handshake(peers):
    barrier = pltpu.get_barrier_semaphore()
    for peer in peers:
        pl.semaphore_signal(barrier, inc=1, device_id=peer, device_id_type=MESH)
    pl.semaphore_wait(barrier, len(peers))


def _all_gather_group(gi, shards):
    wms = AG_GROUPS[gi]
    nw = len(wms)

    def body(*refs):
        sh, full = refs[:nw], refs[nw:2 * nw]
        ici_s, ici_r, pass_s, pass_r, own_s, own_r = refs[2 * nw:]
        x, y, c, _ = _position()
        me, sibling = 2 * x + y, (x, y, 1 - c)
        first, second, diagonal = (x ^ (1 - c), y ^ c), (x ^ c, y ^ (1 - c)), (1 - x, 1 - y)
        chip_id = lambda chip: 2 * chip[0] + chip[1]
        _handshake([(*first, c), (*second, c), sibling])

        def rcopy(src, dst, s_sem, r_sem, to):
            return pltpu.make_async_remote_copy(src_ref=src, dst_ref=dst, send_sem=s_sem, recv_sem=r_sem,
                                                device_id=to, device_id_type=MESH)

        started = []

        def go(cp):
            cp.start()
            started.append(cp)

        for i, wm in enumerate(wms):
            half, dst = _shard_half(sh[i], wm, c), _region(full[i], wm, me, c)
            go(rcopy(half, dst, ici_s.at[i, 0], ici_r.at[i, 0], (*first, c)))
            go(rcopy(half, dst, ici_s.at[i, 1], ici_r.at[i, 1], (*second, c)))
            go(rcopy(sh[i], _region(full[i], wm, me, None), own_s.at[i], own_r.at[i], sibling))
        for i, wm in enumerate(wms):
            got = _region(full[i], wm, chip_id(first), c)
            rcopy(got, got, ici_s.at[i, 0], ici_r.at[i, 0], sibling).wait_recv()
            go(rcopy(got, got, ici_s.at[i, 2], ici_r.at[i, 2], (*second, c)))
            if wm.kind != "tiny":
                go(rcopy(got, got, pass_s.at[i, 0], pass_r.at[i, 0], sibling))
        for i, wm in enumerate(wms):
            for j, chip in ((1, second), (2, diagonal)):
                got = _region(full[i], wm, chip_id(chip), c)
                rcopy(got, got, ici_s.at[i, j], ici_r.at[i, j], sibling).wait_recv()
                if wm.kind != "tiny":
                    go(rcopy(got, got, pass_s.at[i, j], pass_r.at[i, j], sibling))
        for i, wm in enumerate(wms):
            mine = _region(full[i], wm, me, None)
            rcopy(mine, mine, own_s.at[i], own_r.at[i], sibling).wait_recv()
            if wm.kind != "tiny":
                for j, chip in ((0, second), (1, first), (2, diagonal)):
                    got = _region(full[i], wm, chip_id(chip), 1 - c)
                    rcopy(got, got, pass_s.at[i, j], pass_r.at[i, j], sibling).wait_recv()
        for cp in started:
            cp.wait_send()

    return pl.kernel(
        body, out_type=[jax.ShapeDtypeStruct(_full_shape(wm), s.dtype) for wm, s in zip(wms, shards)],
        mesh=plsc.ScalarSubcoreMesh(axis_name="sequencer", num_cores=1), name=f"ag_group{gi}",
        scratch_types=[pltpu.SemaphoreType.DMA((nw, 3))] * 4 + [pltpu.SemaphoreType.DMA((nw,))] * 2,
        compiler_params=pltpu.CompilerParams(collective_id=gi),
    )(*shards)


def _sequencer_call(body, name, cid, out_types, scratch, args):
    return pl.kernel(
        body, out_type=out_types, mesh=plsc.ScalarSubcoreMesh(axis_name="sequencer", num_cores=1), name=name,
        scratch_types=scratch, compiler_params=pltpu.CompilerParams(collective_id=cid),
    )(*args)


def _pair_exchange(gs, tag, cid):
    n = len(gs)

    def body(*refs):
        g, out, send_sems, recv_sems = refs[:n], refs[n:2 * n], refs[2 * n], refs[2 * n + 1]
        x, y, c, _ = _position()
        _handshake([(x, y, 1 - c)])
        cps = []
        for i in range(n):
            half = g[i].shape[1] // 2
            cps.append(pltpu.make_async_remote_copy(
                src_ref=g[i].at[:, pl.ds(pl.multiple_of((1 - c) * half, 16), half), :], dst_ref=out[i],
                send_sem=send_sems.at[i], recv_sem=recv_sems.at[i], device_id=(x, y, 1 - c), device_id_type=MESH))
            cps[-1].start()
        for cp in cps:
            cp.wait()

    return _sequencer_call(
        body, f"rs_pair_exchange{tag}", cid,
        [jax.ShapeDtypeStruct((a.shape[0], a.shape[1] // 2, a.shape[2]), a.dtype) for a in gs],
        [pltpu.SemaphoreType.DMA((n,)), pltpu.SemaphoreType.DMA((n,))], gs)


def _chip_exchange(ss, tag, cid):
    n = len(ss)

    def body(*refs):
        s, out, send_sems, recv_sems = refs[:n], refs[n:2 * n], refs[2 * n], refs[2 * n + 1]
        x, y, c, chips = _position()
        _handshake([(*chip, c) for chip in chips])
        cps = []
        for i in range(n):
            for j, chip in enumerate(chips):
                cps.append(pltpu.make_async_remote_copy(
                    src_ref=s[i].at[j], dst_ref=out[i].at[j], send_sem=send_sems.at[i, j], recv_sem=recv_sems.at[i, j],
                    device_id=(*chip, c), device_id_type=MESH))
                cps[-1].start()
        for cp in cps:
            cp.wait()

    return _sequencer_call(
        body, f"rs_chip_exchange{tag}", cid, [jax.ShapeDtypeStruct(a.shape, a.dtype) for a in ss],
        [pltpu.SemaphoreType.DMA((n, 3)), pltpu.SemaphoreType.DMA((n, 3))], ss)


def _pair_swap(g8s, tag, cid):
    n = len(g8s)

    def body(*refs):
        g, out, send_sems, recv_sems = refs[:n], refs[n:2 * n], refs[2 * n], refs[2 * n + 1]
        x, y, c, _ = _position()
        _handshake([(x, y, 1 - c)])
        cps = []
        for i in range(n):
            cps.append(pltpu.make_async_remote_copy(
                src_ref=g[i], dst_ref=out[i], send_sem=send_sems.at[i], recv_sem=recv_sems.at[i],
                device_id=(x, y, 1 - c), device_id_type=MESH))
            cps[-1].start()
        for cp in cps:
            cp.wait()

    return _sequencer_call(
        body, f"rs_pair_swap{tag}", cid, [jax.ShapeDtypeStruct(a.shape, a.dtype) for a in g8s],
        [pltpu.SemaphoreType.DMA((n,)), pltpu.SemaphoreType.DMA((n,))], g8s)


def _pair_swap_now(g8s):
    n = len(g8s)

    def body(*refs):
        g, out, send_sems, recv_sems = refs[:n], refs[n:2 * n], refs[2 * n], refs[2 * n + 1]
        x, y, c, _ = _position()
        cps = []
        for i in range(n):
            cps.append(pltpu.make_async_remote_copy(
                src_ref=g[i], dst_ref=out[i], send_sem=send_sems.at[i], recv_sem=recv_sems.at[i],
                device_id=(x, y, 1 - c), device_id_type=MESH))
            cps[-1].start()
        for cp in cps:
            cp.wait()

    return _tc_call(
        body, name="rs_pair_swap_last", in_specs=[ANY] * n, out_specs=[ANY] * n,
        out_shape=[jax.ShapeDtypeStruct(a.shape, a.dtype) for a in g8s],
        scratch_shapes=[pltpu.SemaphoreType.DMA((n,)), pltpu.SemaphoreType.DMA((n,))],
    )(*g8s)


def _all_reduce_small(vecs, owner_major, name):
    n = len(vecs)
    block = lambda i, ref, chip: ref.at[chip] if owner_major[i] else ref
    out_shapes = [a.shape[1:] if owner_major[i] else a.shape for i, a in enumerate(vecs)]

    def body(*refs):
        v, o, gath = refs[:n], refs[n:2 * n], refs[2 * n:3 * n]
        send_sems, recv_sems = refs[3 * n], refs[3 * n + 1]
        x, y, c, _ = _position()
        me = 4 * x + 2 * y + c
        cps = []
        for i in range(n):
            gath[i][me] = block(i, v[i], 2 * x + y)[...]
            for rel in range(1, N_DEV):
                px, py, pc = x ^ (rel >> 2), y ^ ((rel >> 1) & 1), c ^ (rel & 1)
                cps.append(pltpu.make_async_remote_copy(
                    src_ref=block(i, v[i], 2 * px + py), dst_ref=gath[i].at[me], send_sem=send_sems.at[i, rel - 1],
                    recv_sem=recv_sems.at[i, rel - 1], device_id=(px, py, pc), device_id_type=MESH))
        for cp in cps:
            cp.start()
        for i in range(n):
            for rel in range(1, N_DEV):
                pltpu.make_async_remote_copy(
                    src_ref=block(i, v[i], 2 * x + y), dst_ref=gath[i].at[me ^ rel],
                    send_sem=send_sems.at[i, rel - 1], recv_sem=recv_sems.at[i, rel - 1], device_id=(x, y, c),
                    device_id_type=MESH).wait_recv()
        for cp in cps:
            cp.wait_send()
        for i in range(n):
            acc = gath[i][0]
            for d in range(1, N_DEV):
                acc = acc + gath[i][d]
            o[i][...] = acc

    vm = pl.BlockSpec(memory_space=pltpu.VMEM)
    return _tc_call(
        body, name=name, in_specs=[vm] * n, out_specs=[vm] * n,
        out_shape=[jax.ShapeDtypeStruct(s, F32) for s in out_shapes],
        scratch_shapes=[pltpu.VMEM((N_DEV,) + s, F32) for s in out_shapes]
        + [pltpu.SemaphoreType.DMA((n, N_DEV - 1)), pltpu.SemaphoreType.DMA((n, N_DEV - 1))],
    )(*vecs)


def _rope_tables(positions):
    half = QK_ROPE // 2
    inv_freq = 1.0 / (ROPE_THETA ** (jnp.arange(half, dtype=F32) / half))
    ang = positions.astype(F32)[:, None] * inv_freq
    zeros = jnp.zeros((positions.shape[0], LANES - QK_ROPE), F32)
    cos, sin = jnp.cos(ang), jnp.sin(ang)
    return jnp.concatenate([cos, cos, zeros], axis=1), jnp.concatenate([sin, sin, zeros], axis=1)


def _local_step(x, positions, tgt, wf, small, rs):
    cos, sin = _rope_tables(positions)
    w_in, w_out = wf["sc_w_in"], wf["sc_w_out"]
    w_ups, w_downs = (wf["ffn_w_up0"], wf["ffn_w_up1"]), (wf["ffn_w_down0"], wf["ffn_w_down1"])
    w_kv, w_ukv, w_dq, w_uq, w_o = wf["w_kv"], wf["w_ukv"], wf["w_dq"], wf["w_uq"], wf["w_o"]
    attn_norm, ffn_norm = small["attn_norm"], small["ffn_norm"]
    conv_b = small["ffn_conv_b"]

    def ffn_fwd(h, hf, l, then):
        up, a = _ffn_up_gate(hf, w_ups[l], small["ffn_conv_w"][l], conv_b[l:l + 1], f"ffn{l}_up_gate")
        return then(a, w_downs[l], h), (hf, up, a)

    def ffn_bwd(h, dh_out, dh_out_b, l, saved, gi, hooks):
        run = lambda stage: hooks.get(stage, lambda: None)()
        hf, up, a = saved
        d_down = _tn(f"ffn{l}_down_dw", a, dh_out_b, BF16)
        run("down_dw")
        dup, d_cw, d_cb = _gate_bwd(up, small["ffn_conv_w"][l], conv_b[l:l + 1], dh_out_b, w_downs[l],
                                    f"ffn{l}_gate_bwd")
        run("gate_bwd")
        d_up = _dw_ffn_up(f"ffn{l}_up_dw", hf, dup)
        rs.start(gi, {f"ffn_w_down{l}": d_down.reshape(N_CHIPS, F_FF // N_CHIPS, D), f"ffn_w_up{l}": d_up})
        run("up_dw")
        dh, dh_b, d_norm = _dx_norm_bwd(f"ffn{l}_up_dx", dup, w_ups[l], h, ffn_norm[l:l + 1], dh_out)
        run("up_dx")
        return dh, dh_b, d_cw, d_cb, d_norm

    hn0 = _rms_fwd(x, attn_norm[0:1], "attn0_norm")
    z = _nn_parts("sc_in", hn0, w_in, 3, BF16)
    mix = _scmix_fwd(z, small["sc_conv_w"])
    h1, hf0 = _nn_add_norm("sc_out", mix, w_out, x, ffn_norm[0:1])
    h2, ffn0_saved = ffn_fwd(h1, hf0, 0, lambda a, w, h: _nn("ffn0_down", a, w, F32, add=h))

    hn1, hk, cq_pre, cq, q, kvpre, ckv, kr, knv = _attn_prep(
        h2, attn_norm[1:2], small["kv_in_norm"], w_dq, small["q_latent_norm"], w_uq, w_kv, small["kv_latent_norm"],
        w_ukv, cos, sin)
    o = _attn_fwd(q, knv, kr)
    h3, hf1 = _nn_add_norm("attn_out", o, w_o, h2, ffn_norm[1:2])
    (loss, dh4, dh4_b, d_final), ffn1_saved = ffn_fwd(
        h3, hf1, 1, lambda a, w, h: _nn_add_loss("ffn1_down_loss", a, w, h, small["final_norm"], tgt))

    rows = D // N_CHIPS
    dh3, dh3_b, d_cw1, d_cb1, d_fn1 = ffn_bwd(h3, dh4, dh4_b, 1, ffn1_saved, 0, {})

    do = _nt("attn_out_dx", dh3_b, w_o, BF16)
    d_wo = _tn("attn_out_dw", o, dh3_b, BF16)
    rs.pair_sums(0)
    dq, dknv, dkr = _attn_bwd(q, knv, kr, do, cos, sin)
    rs.chip_sums(0)
    dh2, dh2_b, d_wuq, d_wdq, d_wukv, d_wkv, d_an1, d_kvin, d_qln, d_kvln = _attn_prep_bwd(
        dq, dknv, dkr, dh3, h2, hn1, hk, cq_pre, cq, kvpre, ckv, attn_norm[1:2], small["kv_in_norm"], w_dq,
        small["q_latent_norm"], w_uq, w_kv, small["kv_latent_norm"], w_ukv, cos, sin)
    rs.finish(0)
    by_owner = lambda dw: dw.reshape(dw.shape[0], N_CHIPS, -1).transpose(1, 0, 2)
    rs.start(1, {
        "w_o": d_wo.reshape(N_CHIPS, rows, D), "w_uq": by_owner(d_wuq), "w_dq": d_wdq.reshape(N_CHIPS, rows, Q_LORA),
        "w_ukv": by_owner(d_wukv.reshape(2 * KV_LORA, -1)).reshape(N_CHIPS, 2 * KV_LORA, -1),
        "w_kv": d_wkv.reshape(N_CHIPS, rows, KVP),
    })

    dh1, dh1_b, d_cw0, d_cb0, d_fn0 = ffn_bwd(h1, dh2, dh2_b, 0, ffn0_saved, 2, {
        "down_dw": lambda: rs.pair_sums(1), "gate_bwd": lambda: rs.chip_sums(1),
        "up_dw": lambda: (rs.finish(1), rs.pair_sums(2))})

    d_wout = _tn("sc_out_dw", mix, dh1_b, BF16)
    dmix = _nt("sc_out_dx", dh1_b, w_out, BF16)
    dz, d_scw = _scmix_bwd(z, small["sc_conv_w"], dmix)
    d_win = _dw_sc_in(hn0, dz)
    rs.start(3, {"sc_w_out": d_wout.reshape(N_CHIPS, rows, D), "sc_w_in": d_win})
    dx, _, d_an0 = _dx_norm_bwd("sc_in_dx", dz, w_in, x, attn_norm[0:1], dh1)

    taps_by_owner = lambda per_layer: jnp.stack(per_layer, axis=1).reshape(3, len(per_layer), N_CHIPS, -1).transpose(2, 0, 1, 3)
    small_g = {
        "attn_norm": jnp.concatenate([d_an0, d_an1]), "ffn_norm": jnp.concatenate([d_fn0, d_fn1]),
        "final_norm": d_final, "kv_in_norm": d_kvin, "kv_latent_norm": d_kvln, "q_latent_norm": d_qln,
        "ffn_conv_b": jnp.concatenate([d_cb0, d_cb1]),
        "sc_conv_w": taps_by_owner([d_scw]), "ffn_conv_w": taps_by_owner([d_cw0, d_cw1]),
    }
    return loss, dx, small_g


RS_GROUPS = (("ffn_w_down1", "ffn_w_up1"), ("w_o", "w_uq", "w_dq", "w_ukv", "w_kv"),
             ("ffn_w_down0", "ffn_w_up0"), ("sc_w_out", "sc_w_in"))


class _ReduceScatter:
    def __init__(self, ids, finish):
        self.ids, self.grads, self.step, self.mine, self.sib, self.finish = ids, {}, {}, {}, {}, finish

    def _cid(self, gi):
        return len(AG_GROUPS) + 3 * gi

    def start(self, gi, grads):
        self.grads.update(grads)
        own = [grads[n] for n in RS_GROUPS[gi]]
        self.step[gi] = (own, _pair_exchange(own, gi, self._cid(gi)))

    def pair_sums(self, gi):
        own, ra = self.step[gi]
        sums = _pair_sums(self.ids, own, ra, f"rs_pair_sums{gi}")
        self.step[gi] = (own, ra, _chip_exchange(sums, gi, self._cid(gi) + 1))

    def chip_sums(self, gi):
        own, ra, rb = self.step[gi]
        mine = _chip_sums(self.ids, own, ra, rb, f"rs_chip_sums{gi}")
        self.mine.update(zip(RS_GROUPS[gi], mine))
        last = gi == len(RS_GROUPS) - 1
        swapped = _pair_swap_now(mine) if last else _pair_swap(mine, gi, self._cid(gi) + 2)
        self.sib.update(zip(RS_GROUPS[gi], swapped))


SMALL_REPL = ("attn_norm", "ffn_norm", "final_norm", "kv_in_norm", "kv_latent_norm", "q_latent_norm", "ffn_conv_b")


def _pad_heads(w_uq):
    per_head = w_uq.reshape(Q_LORA, -1, QK_NOPE + QK_ROPE)
    return jnp.pad(per_head, ((0, 0), (0, 0), (0, HEAD_PAD - QK_NOPE - QK_ROPE))).reshape(Q_LORA, -1)


def _pack_kv(w_dkv, w_kr):
    return jnp.concatenate([w_dkv, w_kr, jnp.zeros((w_kr.shape[0], LANES - QK_ROPE), w_kr.dtype)], axis=1)


def kernel(x, positions, attn_norm, ffn_norm, final_norm, sc_w_in, sc_conv_w, sc_w_out, kv_in_norm, w_dkv, kv_latent_norm, w_kr, w_uk, w_uv, w_dq, q_latent_norm, w_uq, w_o, ffn_w_up, ffn_conv_w, ffn_conv_b, ffn_w_down, loss_target, m_attn_norm, m_ffn_norm, m_final_norm, m_sc_w_in, m_sc_conv_w, m_sc_w_out, m_kv_in_norm, m_w_dkv, m_kv_latent_norm, m_w_kr, m_w_uk, m_w_uv, m_w_dq, m_q_latent_norm, m_w_uq, m_w_o, m_ffn_w_up, m_ffn_conv_w, m_ffn_conv_b, m_ffn_w_down, v_attn_norm, v_ffn_norm, v_final_norm, v_sc_w_in, v_sc_conv_w, v_sc_w_out, v_kv_in_norm, v_w_dkv, v_kv_latent_norm, v_w_kr, v_w_uk, v_w_uv, v_w_dq, v_q_latent_norm, v_w_uq, v_w_o, v_ffn_w_up, v_ffn_conv_w, v_ffn_conv_b, v_ffn_w_down):
    names = ("attn_norm", "ffn_norm", "final_norm", "sc_w_in", "sc_conv_w", "sc_w_out", "kv_in_norm", "w_dkv",
             "kv_latent_norm", "w_kr", "w_uk", "w_uv", "w_dq", "q_latent_norm", "w_uq", "w_o", "ffn_w_up",
             "ffn_conv_w", "ffn_conv_b", "ffn_w_down")
    w = dict(zip(names, (attn_norm, ffn_norm, final_norm, sc_w_in, sc_conv_w, sc_w_out, kv_in_norm, w_dkv,
                         kv_latent_norm, w_kr, w_uk, w_uv, w_dq, q_latent_norm, w_uq, w_o, ffn_w_up,
                         ffn_conv_w, ffn_conv_b, ffn_w_down)))
    m = dict(zip(names, (m_attn_norm, m_ffn_norm, m_final_norm, m_sc_w_in, m_sc_conv_w, m_sc_w_out, m_kv_in_norm,
                         m_w_dkv, m_kv_latent_norm, m_w_kr, m_w_uk, m_w_uv, m_w_dq, m_q_latent_norm, m_w_uq, m_w_o,
                         m_ffn_w_up, m_ffn_conv_w, m_ffn_conv_b, m_ffn_w_down)))
    v = dict(zip(names, (v_attn_norm, v_ffn_norm, v_final_norm, v_sc_w_in, v_sc_conv_w, v_sc_w_out, v_kv_in_norm,
                         v_w_dkv, v_kv_latent_norm, v_w_kr, v_w_uk, v_w_uv, v_w_dq, v_q_latent_norm, v_w_uq, v_w_o,
                         v_ffn_w_up, v_ffn_conv_w, v_ffn_conv_b, v_ffn_w_down)))

    _ORDER[0] = None
    ix, iy, ic = lax.axis_index("x"), lax.axis_index("y"), lax.axis_index("c")
    chip = 2 * ix + iy
    ids = jnp.stack([ic, chip]).astype(jnp.int32)

    def shards_of(t):
        return {
            "sc_w_in": t["sc_w_in"][0], "sc_w_out": t["sc_w_out"][0], "ffn_w_up": t["ffn_w_up"],
            "ffn_w_down": t["ffn_w_down"], "w_kv": _pack_kv(t["w_dkv"], t["w_kr"]),
            "w_ukv": jnp.stack([t["w_uk"], t["w_uv"]]), "w_dq": t["w_dq"][0], "w_uq": _pad_heads(t["w_uq"][0]),
            "w_o": t["w_o"][0],
        }

    ws, ms, vs = shards_of(w), shards_of(m), shards_of(v)

    def ag_shard(name):
        if name == "sc_conv_w":
            return sc_conv_w[0]
        if name == "ffn_conv_w":
            return ffn_conv_w.reshape(6, -1)
        if name[:-1] in ("ffn_w_up", "ffn_w_down"):
            return ws[name[:-1]][int(name[-1])].astype(BF16)
        return ws[name].astype(BF16)

    wf = {}
    for gi, wms in enumerate(AG_GROUPS):
        fulls = _all_gather_group(gi, [ag_shard(wm.name) for wm in wms])
        wf.update({wm.name: f for wm, f in zip(wms, fulls)})
    small = {
        "attn_norm": attn_norm, "ffn_norm": ffn_norm, "final_norm": final_norm[None], "kv_in_norm": kv_in_norm[None],
        "kv_latent_norm": kv_latent_norm[None], "q_latent_norm": q_latent_norm, "ffn_conv_b": ffn_conv_b,
        "sc_conv_w": wf["sc_conv_w"].transpose(1, 0, 2).reshape(3, D),
        "ffn_conv_w": wf["ffn_conv_w"].reshape(N_CHIPS, 2, 3, -1).transpose(1, 2, 0, 3).reshape(2, 3, F_FF),
    }

    res = {}

    held = {
        "ffn_w_up0": [("ffn_w_up", dict(layer=0))], "ffn_w_up1": [("ffn_w_up", dict(layer=1))],
        "ffn_w_down0": [("ffn_w_down", dict(layer=0))], "ffn_w_down1": [("ffn_w_down", dict(layer=1))],
        "sc_w_in": [("sc_w_in", dict(layer=0))], "sc_w_out": [("sc_w_out", dict(layer=0))],
        "w_dq": [("w_dq", dict(layer=0))], "w_o": [("w_o", dict(layer=0))], "w_uq": [("w_uq", {})],
        "w_kv": [("w_dkv", dict(gcols=(0, KV_LORA))), ("w_kr", dict(gcols=(KV_LORA, KV_LORA + QK_ROPE)))],
        "w_ukv": [("w_uk", dict(owner=0)), ("w_uv", dict(owner=1))],
    }

    def adamw_group(gi):
        items = []
        for key in RS_GROUPS[gi]:
            for n, opts in held[key]:
                w_, m_, v_ = (ws[n], ms[n], vs[n]) if n == "w_uq" else (w[n], m[n], v[n])
                items.append(dict(name=n, w=w_, m=m_, v=v_, g_mine=rs.mine[key], g_sib=rs.sib[key],
                                  prev=res.get(n) if "layer" in opts and w_.shape[0] > 1 else None, **opts))
        for it, out in zip(items, _adamw_shards(ids, items, f"adamw_group{gi}")):
            res[it["name"]] = out

    rs = _ReduceScatter(ids, adamw_group)
    loss, dx, small_g = _local_step(x[0], positions[0], loss_target[0], wf, small, rs)

    rs.chip_sums(2)
    rs.pair_sums(3)

    s_names = list(small_g)
    reduced = _all_reduce_small([small_g[n] for n in s_names] + [loss], [small_g[n].ndim == 4 for n in s_names] + [False],
                                "ar_small")
    sg, loss_out = dict(zip(s_names, reduced[:-1])), reduced[-1][0, 0]

    row = lambda n: (lambda t: t[n][None])
    taps = lambda n: (lambda t: t[n].transpose(1, 0, 2))
    small_2d = {
        "attn_norm": (sg["attn_norm"], lambda t: t["attn_norm"]), "ffn_norm": (sg["ffn_norm"], lambda t: t["ffn_norm"]),
        "final_norm": (sg["final_norm"], row("final_norm")), "kv_in_norm": (sg["kv_in_norm"], row("kv_in_norm")),
        "kv_latent_norm": (sg["kv_latent_norm"], row("kv_latent_norm")),
        "q_latent_norm": (sg["q_latent_norm"], lambda t: t["q_latent_norm"]),
        "ffn_conv_b": (sg["ffn_conv_b"], lambda t: t["ffn_conv_b"]),
        "sc_conv_w": (sg["sc_conv_w"], taps("sc_conv_w")), "ffn_conv_w": (sg["ffn_conv_w"], taps("ffn_conv_w")),
    }
    s_keys = list(small_2d)
    small_grads = [small_2d[k][0] for k in s_keys]
    views = lambda tree: [small_2d[k][1](tree) for k in s_keys]
    small_res = _adamw_small(views(w), small_grads, views(m), views(v))

    def restore(vals):
        by = dict(zip(s_keys, vals))
        out = {n: by[n].reshape(w[n].shape) for n in SMALL_REPL}
        out.update({n: by[n].transpose(1, 0, 2) for n in ("sc_conv_w", "ffn_conv_w")})
        return out

    grads = restore(small_grads)
    rs.finish(2)
    rs.chip_sums(3)
    rs.finish(3)
    outs = [grads, {}, {}, {}]
    for k, dst in enumerate(outs):
        for n in res:
            dst[n] = res[n][k]
        unpadded = res["w_uq"][k].reshape(Q_LORA, -1, HEAD_PAD)[:, :, :QK_NOPE + QK_ROPE]
        dst["w_uq"] = unpadded.reshape(w_uq.shape)
    grads, delta, new_m, new_v = outs
    for vals, dst in zip(small_res, (delta, new_m, new_v)):
        dst.update(restore(vals))

    _ORDER[0] = None
    return (loss_out, dx[None], *[grads[n] for n in names], *[delta[n] for n in names],
            *[new_m[n] for n in names], *[new_v[n] for n in names])
```

```python
from typing import NamedTuple

import jax
import jax.numpy as jnp
from jax import lax
from jax.experimental import pallas as pl
from jax.experimental.pallas import tpu as pltpu
from jax.experimental.pallas import tpu_sc as plsc

F32 = jnp.float32
BF16 = jnp.bfloat16

T = 2048
D = 1024
F_FF = 2816
N_HEADS = 8
QK_NOPE = 128
QK_ROPE = 64
V_HEAD = 128
Q_LORA = 384
KV_LORA = 256
CHUNK_SHIFT = 6
ROPE_THETA = 10000.0
EPS = 1e-6
NEG_INF = -1e30
HEAD_PAD = 256
KVP = KV_LORA + 128

ADAM_LR = 0.001
ADAM_B1 = 0.9
ADAM_B2 = 0.999
ADAM_EPS = 1e-08
ADAM_WD = 0.01
ADAM_STEP = 10

N_CHIPS = 4
N_DEV = 8
LANES = 128
TC = 256
V7X_VMEM_LIMIT = 56 * 1024 * 1024

MESH = pl.DeviceIdType.MESH
ANY = pl.BlockSpec(memory_space=pl.ANY)


class _W(NamedTuple):
    name: str
    kind: str
    nl: int
    k: int
    n: int


AG_GROUPS = (
    (_W("sc_w_in", "col", 1, D, 3 * D // N_CHIPS), _W("sc_conv_w", "tiny", 1, 3, D // N_CHIPS),
     _W("ffn_conv_w", "tiny", 1, 6, F_FF // N_CHIPS), _W("sc_w_out", "row", 1, D // N_CHIPS, D)),
    (_W("ffn_w_up0", "col", 1, D, 2 * F_FF // N_CHIPS),),
    (_W("ffn_w_down0", "row", 1, F_FF // N_CHIPS, D),),
    (_W("w_kv", "row", 1, D // N_CHIPS, KVP), _W("w_ukv", "col", 2, KV_LORA, N_HEADS * QK_NOPE // N_CHIPS),
     _W("w_dq", "row", 1, D // N_CHIPS, Q_LORA),
     _W("w_uq", "col", 1, Q_LORA, N_HEADS * HEAD_PAD // N_CHIPS),
     _W("w_o", "row", 1, N_HEADS * V_HEAD // N_CHIPS, D)),
    (_W("ffn_w_up1", "col", 1, D, 2 * F_FF // N_CHIPS), _W("ffn_w_down1", "row", 1, F_FF // N_CHIPS, D)),
)


def _cp(*sem):
    return pltpu.CompilerParams(dimension_semantics=sem, vmem_limit_bytes=V7X_VMEM_LIMIT)


_ORDER = [None]


def _tc_call(body, *, name, out_shape, in_specs=None, out_specs=None, grid=(), scratch_shapes=(), prefetch=0,
             input_output_aliases=None, compiler_params=None):
    def run(*args):
        specs = [pl.BlockSpec(memory_space=pltpu.VMEM)] * (len(args) - prefetch) if in_specs is None else list(in_specs)
        inner, dep = body, _ORDER[0]
        if dep is not None:
            unread = prefetch + len(specs)
            specs, args = specs + [ANY], (*args, dep)

            def inner(*refs):
                return body(*refs[:unread], *refs[unread + 1:])

        kwargs = dict(name=name, out_shape=out_shape, input_output_aliases=input_output_aliases or {},
                      compiler_params=compiler_params)
        if prefetch:
            kwargs["grid_spec"] = pltpu.PrefetchScalarGridSpec(
                num_scalar_prefetch=prefetch, grid=grid, in_specs=specs, out_specs=out_specs,
                scratch_shapes=scratch_shapes)
        else:
            kwargs.update(grid=grid, in_specs=specs, scratch_shapes=scratch_shapes)
            if out_specs is not None:
                kwargs["out_specs"] = out_specs
        out = pl.pallas_call(inner, **kwargs)(*args)
        _ORDER[0] = out[0] if isinstance(out, (list, tuple)) else out
        return out

    return run


def _tile(n, cands):
    for c in cands:
        if n % c == 0:
            return c
    raise ValueError(f"no tile for {n}")


NN_DIMS = (((1,), (0,)), ((), ()))
NT_DIMS = (((1,), (1,)), ((), ()))
TN_DIMS = (((0,), (0,)), ((), ()))
M_TILES = (1024, 512, 384, 256, 128)
N_TILES = (1408, 1024, 768, 512, 384, 256, 128)
MM_BLOCK_BYTES = 36 * 1024 * 1024


def _fit(m, n, block_bytes, m_tiles=M_TILES, n_tiles=N_TILES):
    for tm in [c for c in m_tiles if m % c == 0]:
        for tn in [c for c in n_tiles if n % c == 0]:
            if 2 * block_bytes(tm, tn) + 4 * tm * tn <= MM_BLOCK_BYTES:
                return tm, tn
    raise ValueError(f"no tiles for {m} x {n}")


def _size(x):
    return x.dtype.itemsize


def _mm(name, a, b, dims, grid, a_spec, b_spec, o_spec, o_sds, add=None, red=None, acc_shape=None):
    n_red = None if red is None else grid[red]

    def body(*refs):
        a_ref, b_ref = refs[0], refs[1]
        add_ref = refs[2] if add is not None else None
        o_ref = refs[3] if add is not None else refs[2]
        part = lax.dot_general(a_ref[...].astype(BF16), b_ref[...].astype(BF16), dims, preferred_element_type=F32)
        if red is None:
            if add is not None:
                part = part + add_ref[...]
            o_ref[...] = part.astype(o_ref.dtype)
            return
        acc_ref = refs[-1]
        r = pl.program_id(red)

        @pl.when(r == 0)
        def _():
            acc_ref[...] = part

        @pl.when(r > 0)
        def _():
            acc_ref[...] += part

        @pl.when(r == n_red - 1)
        def _():
            o_ref[...] = acc_ref[...].astype(o_ref.dtype)

    sem = tuple("arbitrary" if ax == red else "parallel" for ax in range(len(grid)))
    in_specs = [a_spec, b_spec] + ([o_spec] if add is not None else [])
    args = (a, b) + ((add,) if add is not None else ())
    return _tc_call(
        body, name=name, grid=grid, in_specs=in_specs, out_specs=o_spec, out_shape=o_sds,
        scratch_shapes=[] if red is None else [pltpu.VMEM(acc_shape, F32)], compiler_params=_cp(*sem),
    )(*args)


def _nn(name, a, b, out_dtype, add=None, lead=None):
    (m, k), n = a.shape, b.shape[-1]
    osz = jnp.dtype(out_dtype).itemsize + (4 if add is not None else 0)
    tm, tn = _fit(m, n, lambda tm, tn: tm * k * _size(a) + k * tn * _size(b) + tm * tn * osz)
    if lead is None:
        b_spec = pl.BlockSpec((k, tn), lambda i, j: (0, j))
    else:
        b_spec = pl.BlockSpec((None, k, tn), lambda i, j: (lead, 0, j))
    return _mm(name, a, b, NN_DIMS, (m // tm, n // tn), pl.BlockSpec((tm, k), lambda i, j: (i, 0)), b_spec,
               pl.BlockSpec((tm, tn), lambda i, j: (i, j)), jax.ShapeDtypeStruct((m, n), out_dtype), add=add)


def _nn_parts(name, a, b, parts, out_dtype, lead=None, stacked=False):
    m, k = a.shape
    c = b.shape[-1] if stacked else b.shape[-1] // parts
    osz = jnp.dtype(out_dtype).itemsize
    tm, tn = _fit(m, c, lambda tm, tn: tm * k * _size(a) + k * tn * _size(b) + tm * tn * osz)
    nb = c // tn
    if stacked:
        b_spec = pl.BlockSpec((None, k, tn), lambda i, p, j: (p, 0, j))
    elif lead is None:
        b_spec = pl.BlockSpec((k, tn), lambda i, p, j: (0, p * nb + j))
    else:
        b_spec = pl.BlockSpec((None, k, tn), lambda i, p, j: (lead, 0, p * nb + j))
    return _mm(name, a, b, NN_DIMS, (m // tm, parts, nb), pl.BlockSpec((tm, k), lambda i, p, j: (i, 0)), b_spec,
               pl.BlockSpec((None, tm, tn), lambda i, p, j: (p, i, j)), jax.ShapeDtypeStruct((parts, m, c), out_dtype))


def _nt(name, a, b, out_dtype, lead=None):
    (m, k), n = a.shape, b.shape[-2]
    osz = jnp.dtype(out_dtype).itemsize
    tm, tn = _fit(m, n, lambda tm, tn: tm * k * _size(a) + tn * k * _size(b) + tm * tn * osz)
    if lead is None:
        b_spec = pl.BlockSpec((tn, k), lambda i, j: (j, 0))
    else:
        b_spec = pl.BlockSpec((None, tn, k), lambda i, j: (lead, j, 0))
    return _mm(name, a, b, NT_DIMS, (m // tm, n // tn), pl.BlockSpec((tm, k), lambda i, j: (i, 0)), b_spec,
               pl.BlockSpec((tm, tn), lambda i, j: (i, j)), jax.ShapeDtypeStruct((m, n), out_dtype))


def _tn(name, a, b, out_dtype):
    (k, m), n = a.shape, b.shape[1]
    osz = jnp.dtype(out_dtype).itemsize
    tm, tn = _fit(m, n, lambda tm, tn: k * tm * _size(a) + k * tn * _size(b) + tm * tn * osz,
                  m_tiles=(512, 384, 256, 128), n_tiles=(n,) + N_TILES)
    return _mm(name, a, b, TN_DIMS, (m // tm, n // tn), pl.BlockSpec((k, tm), lambda i, j: (0, i)),
               pl.BlockSpec((k, tn), lambda i, j: (0, j)), pl.BlockSpec((tm, tn), lambda i, j: (i, j)),
               jax.ShapeDtypeStruct((m, n), out_dtype))


def _nn_add_norm(name, a, b, add, g):
    (m, k), n = a.shape, b.shape[1]
    tm = 512

    def body(a_ref, b_ref, add_ref, g_ref, h_ref, hn_ref):
        h = jnp.dot(a_ref[...], b_ref[...], preferred_element_type=F32) + add_ref[...]
        h_ref[...] = h
        hn_ref[...] = _rms_rows(h, g_ref[...]).astype(BF16)

    rows = lambda w: pl.BlockSpec((tm, w), lambda i: (i, 0))
    return _tc_call(
        body, name=name, grid=(m // tm,),
        in_specs=[rows(k), pl.BlockSpec((k, n), lambda i: (0, 0)), rows(n), pl.BlockSpec((1, n), lambda i: (0, 0))],
        out_specs=[rows(n), rows(n)],
        out_shape=[jax.ShapeDtypeStruct((m, n), F32), jax.ShapeDtypeStruct((m, n), BF16)], compiler_params=_cp("parallel"),
    )(a, b, add, g)


def _nn_add_loss(name, a, b, add, g, tgt):
    (m, k), n = a.shape, b.shape[1]
    tm = 512

    def body(a_ref, b_ref, add_ref, g_ref, t_ref, loss_ref, dh_ref, dhb_ref, dg_ref):
        xv = jnp.dot(a_ref[...], b_ref[...], preferred_element_type=F32) + add_ref[...]
        gv = g_ref[...]
        r = lax.rsqrt(jnp.mean(xv * xv, axis=1, keepdims=True) + EPS)
        err = xv * r * gv - t_ref[...]
        part = 0.5 * jnp.sum(jnp.mean(err * err, axis=1, keepdims=True), axis=0, keepdims=True)
        dx, dg = _rms_bwd_math(xv, gv, err * (1.0 / n))
        dh_ref[...] = dx
        dhb_ref[...] = dx.astype(BF16)

        @pl.when(pl.program_id(0) == 0)
        def _():
            dg_ref[...] = jnp.zeros_like(dg_ref)
            loss_ref[...] = jnp.zeros_like(loss_ref)

        dg_ref[...] += dg
        loss_ref[...] += jnp.broadcast_to(part, loss_ref.shape)

    rows = lambda w: pl.BlockSpec((tm, w), lambda i: (i, 0))
    vec = pl.BlockSpec((1, n), lambda i: (0, 0))
    return _tc_call(
        body, name=name, grid=(m // tm,),
        in_specs=[rows(k), pl.BlockSpec((k, n), lambda i: (0, 0)), rows(n), vec, rows(n)],
        out_specs=[pl.BlockSpec((1, LANES), lambda i: (0, 0)), rows(n), rows(n), vec],
        out_shape=[jax.ShapeDtypeStruct((1, LANES), F32), jax.ShapeDtypeStruct((m, n), F32),
                   jax.ShapeDtypeStruct((m, n), BF16), jax.ShapeDtypeStruct((1, n), F32)],
        compiler_params=_cp("arbitrary"),
    )(a, b, add, g, tgt)


def _dx_norm_bwd(name, a, b, x, g, add):
    parts, t, c = a.shape
    d = b.shape[0]
    tm = 256

    def body(a_ref, b_ref, x_ref, g_ref, add_ref, dx_ref, dxb_ref, dg_ref):
        dy = None
        for p in range(parts):
            part = lax.dot_general(a_ref[p], b_ref[:, p * c:(p + 1) * c], NT_DIMS, preferred_element_type=F32)
            dy = part if dy is None else dy + part
        dx, dg = _rms_bwd_math(x_ref[...], g_ref[...], dy)
        dx = dx + add_ref[...]
        dx_ref[...] = dx
        dxb_ref[...] = dx.astype(BF16)

        @pl.when(pl.program_id(0) == 0)
        def _():
            dg_ref[...] = jnp.zeros_like(dg_ref)

        dg_ref[...] += dg

    rows = pl.BlockSpec((tm, d), lambda i: (i, 0))
    vec = pl.BlockSpec((1, d), lambda i: (0, 0))
    return _tc_call(
        body, name=name, grid=(t // tm,),
        in_specs=[pl.BlockSpec((parts, tm, c), lambda i: (0, i, 0)), pl.BlockSpec(b.shape, lambda i: (0, 0)), rows, vec,
                  rows],
        out_specs=[rows, rows, vec],
        out_shape=[jax.ShapeDtypeStruct((t, d), F32), jax.ShapeDtypeStruct((t, d), BF16),
                   jax.ShapeDtypeStruct((1, d), F32)],
        compiler_params=_cp("arbitrary"),
    )(a, b, x, g, add)


def _dw_sc_in(hn, dz):
    t, tn, tm = hn.shape[0], TC, D
    per_part, per_chip = D // tn, 3 * D // N_CHIPS // tn
    return _mm("sc_in_dw", hn, dz, TN_DIMS, (D // tm, 3 * D // tn), pl.BlockSpec((t, tm), lambda i, j: (0, i)),
               pl.BlockSpec((None, t, tn), lambda i, j: (j // per_part, 0, j % per_part)),
               pl.BlockSpec((None, tm, tn), lambda i, j: (j // per_chip, i, j % per_chip)),
               jax.ShapeDtypeStruct((N_CHIPS, D, 3 * D // N_CHIPS), BF16))


def _dw_ffn_up(name, hf, dup):
    t, tm, ns = hf.shape[0], D, 2 * F_FF // N_CHIPS
    return _mm(name, hf, dup, TN_DIMS, (N_CHIPS, D // tm), pl.BlockSpec((t, tm), lambda s, i: (0, i)),
               pl.BlockSpec((None, t, ns), lambda s, i: (s // 2, 0, s % 2)),
               pl.BlockSpec((None, tm, ns), lambda s, i: (s, i, 0)), jax.ShapeDtypeStruct((N_CHIPS, D, ns), BF16))


def _rms_fwd(x, g, name):
    t, d = x.shape
    tr = 512

    def body(x_ref, g_ref, o_ref):
        xv = x_ref[...]
        r = lax.rsqrt(jnp.mean(xv * xv, axis=1, keepdims=True) + EPS)
        o_ref[...] = (xv * r * g_ref[...]).astype(o_ref.dtype)

    row = pl.BlockSpec((tr, d), lambda i: (i, 0))
    return _tc_call(
        body, name=name, grid=(t // tr,), in_specs=[row, pl.BlockSpec((1, d), lambda i: (0, 0))],
        out_specs=row, out_shape=jax.ShapeDtypeStruct((t, d), BF16), compiler_params=_cp("parallel"),
    )(x, g)


def _rms_bwd_math(xv, g, dy):
    r = lax.rsqrt(jnp.mean(xv * xv, axis=1, keepdims=True) + EPS)
    xh = xv * r
    gy = dy * g
    dx = r * (gy - xh * jnp.mean(gy * xh, axis=1, keepdims=True))
    dg = jnp.sum(dy * xh, axis=0, keepdims=True)
    return dx, dg


def _rot_half(x):
    lane = lax.broadcasted_iota(jnp.int32, x.shape, 1)
    return jnp.where((lane % QK_ROPE) < QK_ROPE // 2, -pltpu.roll(x, LANES - 32, axis=1),
                     pltpu.roll(x, 32, axis=1))


def _rope_fwd_math(x, cos, sin):
    return x * cos + _rot_half(x) * sin


def _rope_bwd_math(dy, cos, sin):
    return dy * cos - _rot_half(dy * sin)


def _rms_rows(x, g):
    return x * lax.rsqrt(jnp.mean(x * x, axis=1, keepdims=True) + EPS) * g


def _attn_prep(h, g_attn, g_kvin, w_dq, g_ql, w_uq, w_kv, g_kvl, w_ukv, cos, sin):
    t, d = h.shape
    tr = 256
    wq = N_HEADS * HEAD_PAD

    def body(h_ref, ga_ref, gk_ref, wdq_ref, gq_ref, wuq_ref, wkv_ref, gl_ref, wukv_ref, c_ref, s_ref,
             hn_ref, hk_ref, cqp_ref, cq_ref, q_ref, kvp_ref, ckv_ref, kr_ref, knv_ref):
        xv, cv, sv = h_ref[...], c_ref[...], s_ref[...]
        xh = xv * lax.rsqrt(jnp.mean(xv * xv, axis=1, keepdims=True) + EPS)
        hn = (xh * ga_ref[...]).astype(BF16)
        hk = (xh * gk_ref[...]).astype(BF16)
        hn_ref[...], hk_ref[...] = hn, hk
        cq_pre = jnp.dot(hn, wdq_ref[...], preferred_element_type=F32)
        cqp_ref[...] = cq_pre
        cq = _rms_rows(cq_pre, gq_ref[...]).astype(BF16)
        cq_ref[...] = cq
        for hd in range(N_HEADS):
            lo = hd * HEAD_PAD
            qh = jnp.dot(cq, wuq_ref[:, lo:lo + HEAD_PAD], preferred_element_type=F32)
            q_ref[:, lo:lo + QK_NOPE] = qh[:, :QK_NOPE].astype(BF16)
            q_ref[:, lo + QK_NOPE:lo + HEAD_PAD] = _rope_fwd_math(qh[:, QK_NOPE:], cv, sv).astype(BF16)
        kvpre = jnp.dot(hk, wkv_ref[...], preferred_element_type=F32)
        kvp_ref[...] = kvpre
        ckv = _rms_rows(kvpre[:, :KV_LORA], gl_ref[...]).astype(BF16)
        ckv_ref[...] = ckv
        kr_ref[...] = _rope_fwd_math(kvpre[:, KV_LORA:], cv, sv).astype(BF16)
        for p in range(2):
            knv_ref[p] = jnp.dot(ckv, wukv_ref[p], preferred_element_type=F32).astype(BF16)

    rows = lambda w: pl.BlockSpec((tr, w), lambda i: (i, 0))
    whole = lambda a: pl.BlockSpec(a.shape, lambda i: (0,) * a.ndim)
    sds = lambda w, dt: jax.ShapeDtypeStruct((t, w), dt)
    args = (h, g_attn, g_kvin, w_dq, g_ql, w_uq, w_kv, g_kvl, w_ukv, cos, sin)
    return _tc_call(
        body, name="attn_prep", grid=(t // tr,),
        in_specs=[rows(d)] + [whole(a) for a in args[1:9]] + [rows(LANES), rows(LANES)],
        out_specs=[rows(d), rows(d), rows(Q_LORA), rows(Q_LORA), rows(wq), rows(KVP), rows(KV_LORA), rows(LANES),
                   pl.BlockSpec((2, tr, N_HEADS * QK_NOPE), lambda i: (0, i, 0))],
        out_shape=[sds(d, BF16), sds(d, BF16), sds(Q_LORA, F32), sds(Q_LORA, BF16), sds(wq, BF16), sds(KVP, F32),
                   sds(KV_LORA, BF16), sds(LANES, BF16), jax.ShapeDtypeStruct((2, t, N_HEADS * QK_NOPE), BF16)],
        compiler_params=_cp("parallel"),
    )(*args)


def _attn_prep_bwd(dq, dknv, dkr, dh, h, hn, hk, cq_pre, cq, kvpre, ckv, g_attn, g_kvin, w_dq, g_ql, w_uq, w_kv, g_kvl,
                   w_ukv, cos, sin):
    t, d = h.shape
    tr = 256
    n_steps = t // tr
    wq = N_HEADS * HEAD_PAD
    wk = N_HEADS * QK_NOPE

    def body(dq_ref, dknv_ref, dkr_ref, dh_ref, h_ref, hn_ref, hk_ref, cqp_ref, cq_ref, kvp_ref, ckv_ref,
             ga_ref, gk_ref, wdq_ref, gq_ref, wuq_ref, wkv_ref, gl_ref, wukv_ref, c_ref, s_ref,
             dho_ref, dhb_ref, dwuq_ref, dwdq_ref, dwukv_ref, dwkv_ref, dga_ref, dgk_ref, dgq_ref, dgl_ref,
             a_uq, a_dq, a_ukv, a_kv):
        i = pl.program_id(0)

        @pl.when(i == 0)
        def _():
            for ref in (a_uq, a_dq, a_ukv, a_kv, dga_ref, dgk_ref, dgq_ref, dgl_ref):
                ref[...] = jnp.zeros_like(ref)

        dqv = dq_ref[...]
        dcq = lax.dot_general(dqv, wuq_ref[...], NT_DIMS, preferred_element_type=F32)
        a_uq[...] += lax.dot_general(cq_ref[...], dqv, TN_DIMS, preferred_element_type=F32)
        dcq_pre, dg = _rms_bwd_math(cqp_ref[...], gq_ref[...], dcq)
        dgq_ref[...] += dg
        dcq_pre = dcq_pre.astype(BF16)
        dhn = lax.dot_general(dcq_pre, wdq_ref[...], NT_DIMS, preferred_element_type=F32)
        a_dq[...] += lax.dot_general(hn_ref[...], dcq_pre, TN_DIMS, preferred_element_type=F32)
        dckv = None
        for p in range(2):
            dk = dknv_ref[p].astype(BF16)
            part = lax.dot_general(dk, wukv_ref[p], NT_DIMS, preferred_element_type=F32)
            dckv = part if dckv is None else dckv + part
            a_ukv[p] += lax.dot_general(ckv_ref[...], dk, TN_DIMS, preferred_element_type=F32)
        dlat, dg = _rms_bwd_math(kvp_ref[:, :KV_LORA], gl_ref[...], dckv)
        dgl_ref[...] += dg
        dkr_pre = _rope_bwd_math(dkr_ref[...], c_ref[...], s_ref[...])
        dkvpre = jnp.concatenate([dlat, dkr_pre], axis=1).astype(BF16)
        dhk = lax.dot_general(dkvpre, wkv_ref[...], NT_DIMS, preferred_element_type=F32)
        a_kv[...] += lax.dot_general(hk_ref[...], dkvpre, TN_DIMS, preferred_element_type=F32)
        xv = h_ref[...]
        dx1, dg = _rms_bwd_math(xv, ga_ref[...], dhn)
        dga_ref[...] += dg
        dx2, dg = _rms_bwd_math(xv, gk_ref[...], dhk)
        dgk_ref[...] += dg
        dh_new = dh_ref[...] + dx1 + dx2
        dho_ref[...] = dh_new
        dhb_ref[...] = dh_new.astype(BF16)

        @pl.when(i == n_steps - 1)
        def _():
            dwuq_ref[...] = a_uq[...].astype(BF16)
            dwdq_ref[...] = a_dq[...].astype(BF16)
            dwukv_ref[...] = a_ukv[...].astype(BF16)
            dwkv_ref[...] = a_kv[...].astype(BF16)

    rows = lambda w: pl.BlockSpec((tr, w), lambda i: (i, 0))
    whole = lambda shape: pl.BlockSpec(shape, lambda i: (0,) * len(shape))
    weights = (g_attn, g_kvin, w_dq, g_ql, w_uq, w_kv, g_kvl, w_ukv)
    dw_shapes = [(Q_LORA, wq), (d, Q_LORA), (2, KV_LORA, wk), (d, KVP)]
    dg_shapes = [(1, d), (1, d), (1, Q_LORA), (1, KV_LORA)]
    return _tc_call(
        body, name="attn_prep_bwd", grid=(n_steps,),
        in_specs=[rows(wq), pl.BlockSpec((2, tr, wk), lambda i: (0, i, 0)), rows(LANES), rows(d), rows(d), rows(d),
                  rows(d), rows(Q_LORA), rows(Q_LORA), rows(KVP), rows(KV_LORA)]
        + [whole(a.shape) for a in weights] + [rows(LANES), rows(LANES)],
        out_specs=[rows(d), rows(d)] + [whole(s) for s in dw_shapes + dg_shapes],
        out_shape=[jax.ShapeDtypeStruct((t, d), F32), jax.ShapeDtypeStruct((t, d), BF16)]
        + [jax.ShapeDtypeStruct(s, BF16) for s in dw_shapes] + [jax.ShapeDtypeStruct(s, F32) for s in dg_shapes],
        scratch_shapes=[pltpu.VMEM(s, F32) for s in dw_shapes], compiler_params=_cp("arbitrary"),
    )(dq, dknv, dkr, dh, h, hn, hk, cq_pre, cq, kvpre, ckv, *weights, cos, sin)


ROW_CHUNK = 64
HALO = 16
WIN = ROW_CHUNK + 16
LANE_HALVES = (slice(0, LANES), slice(LANES, TC))


def _stage(s_ref, p, src):
    t = src.shape[0]
    s_ref[p, :HALO] = jnp.zeros((HALO, TC), BF16)
    s_ref[p, HALO:HALO + t] = src
    s_ref[p, HALO + t:] = jnp.zeros((HALO, TC), BF16)


def _window(s_ref, p, i, lanes):
    base = pl.multiple_of(i * ROW_CHUNK, ROW_CHUNK)
    return s_ref[p, pl.ds(base, ROW_CHUNK + 2 * HALO), lanes].astype(F32)[8:8 + WIN]


def _valid(x):
    return x[8:8 + ROW_CHUNK]


def _prev(x, k):
    return pltpu.roll(x, k, axis=0)


def _next(x, k):
    return pltpu.roll(x, WIN - k, axis=0)


def _taps(w_ref, lanes):
    return w_ref[0:1, lanes], w_ref[1:2, lanes], w_ref[2:3, lanes]


def _fold8(x):
    return jnp.sum(x.reshape(ROW_CHUNK // 8, 8, x.shape[-1]), axis=0)


def _store_rows(ref, idx, i, lanes, x):
    rows = pl.ds(pl.multiple_of(i * ROW_CHUNK, ROW_CHUNK), ROW_CHUNK)
    ref[(*idx, rows, lanes)] = x.astype(ref.dtype)


def _for_chunks(t, chunk):
    def step(i, carry):
        for lanes in LANE_HALVES:
            chunk(i, lanes)
        return carry

    lax.fori_loop(0, t // ROW_CHUNK, step, 0)


def _write_col_sums(acc_ref, outs):
    for k, (ref, row) in enumerate(outs):
        ref[row:row + 1, :] = jnp.sum(acc_ref[k], axis=0, keepdims=True)


def _shift_down(x, k):
    row = lax.broadcasted_iota(jnp.int32, x.shape, 0)
    return jnp.where(row >= k, pltpu.roll(x, k, axis=0), 0.0)


def _shift_up(x, k):
    n = x.shape[0]
    row = lax.broadcasted_iota(jnp.int32, x.shape, 0)
    return jnp.where(row < n - k, pltpu.roll(x, n - k, axis=0), 0.0)


def _conv3(x, w_ref):
    return _shift_down(x, 2) * w_ref[0:1, :] + _shift_down(x, 1) * w_ref[1:2, :] + x * w_ref[2:3, :]


def _col(parts, t):
    if parts is None:
        return pl.BlockSpec((t, TC), lambda j: (0, j))
    return pl.BlockSpec((parts, t, TC), lambda j: (0, 0, j))


def _staging(parts, t):
    return pltpu.VMEM((parts, t + 2 * HALO, TC), BF16)


def _scmix_fwd(z, w):
    t = z.shape[1]

    def body(z_ref, w_ref, m_ref):
        b, c, u = (z_ref[p].astype(F32) for p in range(3))
        m_ref[...] = (b * _conv3(c * u, w_ref)).astype(BF16)

    return _tc_call(
        body, name="scmix_fwd", grid=(D // TC,), in_specs=[_col(3, t), pl.BlockSpec((3, TC), lambda j: (0, j))],
        out_specs=_col(None, t), out_shape=jax.ShapeDtypeStruct((t, D), BF16), compiler_params=_cp("parallel"),
    )(z, w)


def _scmix_bwd(z, w, dm):
    t = z.shape[1]

    def body(z_ref, w_ref, dm_ref, dz_ref, dw_ref, s_ref, acc_ref):
        for p in range(3):
            _stage(s_ref, p, z_ref[p])
        _stage(s_ref, 3, dm_ref[...])
        acc_ref[...] = jnp.zeros_like(acc_ref)

        def chunk(i, lanes):
            w0, w1, w2 = _taps(w_ref, lanes)
            b, c, u, dm = (_window(s_ref, p, i, lanes) for p in range(4))
            cu = c * u
            cu1, cu2 = _prev(cu, 1), _prev(cu, 2)
            _store_rows(dz_ref, (0,), i, lanes, _valid(dm * (cu2 * w0 + cu1 * w1 + cu * w2)))
            dcv = dm * b
            dcu = dcv * w2 + _next(dcv, 1) * w1 + _next(dcv, 2) * w0
            _store_rows(dz_ref, (1,), i, lanes, _valid(dcu * u))
            _store_rows(dz_ref, (2,), i, lanes, _valid(dcu * c))
            for k, shifted in enumerate((cu2, cu1, cu)):
                acc_ref[k, :, lanes] += _fold8(_valid(dcv * shifted))

        _for_chunks(t, chunk)
        _write_col_sums(acc_ref, [(dw_ref, 0), (dw_ref, 1), (dw_ref, 2)])

    wspec = pl.BlockSpec((3, TC), lambda j: (0, j))
    return _tc_call(
        body, name="scmix_bwd", grid=(D // TC,), in_specs=[_col(3, t), wspec, _col(None, t)],
        out_specs=[_col(3, t), wspec],
        out_shape=[jax.ShapeDtypeStruct((3, t, D), BF16), jax.ShapeDtypeStruct((3, D), F32)],
        scratch_shapes=[_staging(4, t), pltpu.VMEM((3, 8, TC), F32)], compiler_params=_cp("parallel"),
    )(z, w, dm)


def _ffn_up_gate(hf, w_up, w, bias, name):
    t, d = hf.shape
    nb = F_FF // TC

    def body(hf_ref, wg_ref, wv_ref, w_ref, b_ref, up_ref, a_ref, prev_ref):
        @pl.when(pl.program_id(0) == 0)
        def _():
            prev_ref[...] = jnp.zeros_like(prev_ref)

        gc = _conv3(prev_ref[0].astype(F32), w_ref) + b_ref[...]
        a_ref[...] = (gc * jax.nn.sigmoid(gc) * prev_ref[1].astype(F32)).astype(BF16)
        hv = hf_ref[...]
        up_ref[0] = jnp.dot(hv, wg_ref[...], preferred_element_type=F32).astype(BF16)
        up_ref[1] = jnp.dot(hv, wv_ref[...], preferred_element_type=F32).astype(BF16)
        prev_ref[...] = up_ref[...]

    tile = lambda j: jnp.minimum(j, nb - 1)
    gated = lambda j: jnp.maximum(j - 1, 0)
    return _tc_call(
        body, name=name, grid=(nb + 1,),
        in_specs=[pl.BlockSpec((t, d), lambda j: (0, 0)), pl.BlockSpec((d, TC), lambda j: (0, tile(j))),
                  pl.BlockSpec((d, TC), lambda j: (0, nb + tile(j))), pl.BlockSpec((3, TC), lambda j: (0, gated(j))),
                  pl.BlockSpec((1, TC), lambda j: (0, gated(j)))],
        out_specs=[pl.BlockSpec((2, t, TC), lambda j: (0, 0, tile(j))), pl.BlockSpec((t, TC), lambda j: (0, gated(j)))],
        out_shape=[jax.ShapeDtypeStruct((2, t, F_FF), BF16), jax.ShapeDtypeStruct((t, F_FF), BF16)],
        scratch_shapes=[pltpu.VMEM((2, t, TC), BF16)], compiler_params=_cp("arbitrary"),
    )(hf, w_up, w_up, w, bias)


def _gate_bwd(up, w, bias, dh, w_down, name):
    t, d = dh.shape

    def body(u_ref, w_ref, b_ref, dh_ref, wd_ref, du_ref, dw_ref, db_ref, s_ref, acc_ref):
        for p in range(2):
            _stage(s_ref, p, u_ref[p])
        _stage(s_ref, 2, lax.dot_general(dh_ref[...], wd_ref[...], NT_DIMS, preferred_element_type=F32).astype(BF16))
        acc_ref[...] = jnp.zeros_like(acc_ref)

        def chunk(i, lanes):
            w0, w1, w2 = _taps(w_ref, lanes)
            g, v, da = (_window(s_ref, p, i, lanes) for p in range(3))
            g1, g2 = _prev(g, 1), _prev(g, 2)
            gc = g2 * w0 + g1 * w1 + g * w2 + b_ref[:, lanes]
            sg = jax.nn.sigmoid(gc)
            _store_rows(du_ref, (1,), i, lanes, _valid(da * (gc * sg)))
            dgc = da * v * (sg * (1.0 + gc * (1.0 - sg)))
            _store_rows(du_ref, (0,), i, lanes, _valid(dgc * w2 + _next(dgc, 1) * w1 + _next(dgc, 2) * w0))
            for k, shifted in enumerate((g2, g1, g)):
                acc_ref[k, :, lanes] += _fold8(_valid(dgc * shifted))
            acc_ref[3, :, lanes] += _fold8(_valid(dgc))

        _for_chunks(t, chunk)
        _write_col_sums(acc_ref, [(dw_ref, 0), (dw_ref, 1), (dw_ref, 2), (db_ref, 0)])

    wspec = pl.BlockSpec((3, TC), lambda j: (0, j))
    bspec = pl.BlockSpec((1, TC), lambda j: (0, j))
    return _tc_call(
        body, name=name, grid=(F_FF // TC,),
        in_specs=[_col(2, t), wspec, bspec, pl.BlockSpec((t, d), lambda j: (0, 0)), pl.BlockSpec((TC, d), lambda j: (j, 0))],
        out_specs=[_col(2, t), wspec, bspec],
        out_shape=[jax.ShapeDtypeStruct((2, t, F_FF), BF16), jax.ShapeDtypeStruct((3, F_FF), F32),
                   jax.ShapeDtypeStruct((1, F_FF), F32)],
        scratch_shapes=[_staging(3, t), pltpu.VMEM((4, 8, TC), F32)], compiler_params=_cp("parallel"),
    )(up, w, bias, dh, w_down)


ATT_TQ = 256
ATT_SCALE = (QK_NOPE + QK_ROPE) ** -0.5


def _key_ranges(lvl):
    lo = lvl * ATT_TQ
    return ([(0, lo, False)] if lvl else []) + [(lo, lo + ATT_TQ, True)]


FWD_HEADS = 4
BWD_HEADS = 2


def _fill_keys(k_ref, kn_ref, kr_ref):
    @pl.when(pl.program_id(1) == 0)
    def _():
        for hh in range(k_ref.shape[0]):
            k_ref[hh, :, :QK_NOPE] = kn_ref[:, hh * QK_NOPE:(hh + 1) * QK_NOPE]
            k_ref[hh, :, QK_NOPE:] = kr_ref[...]


def _attn_probs(q, k_ref, lvl):
    scores = []
    for lo, hi, diagonal in _key_ranges(lvl):
        s = lax.dot_general(q, k_ref[lo:hi, :], NT_DIMS, preferred_element_type=F32) * ATT_SCALE
        if diagonal:
            row = lax.broadcasted_iota(jnp.int32, s.shape, 0)
            col = lax.broadcasted_iota(jnp.int32, s.shape, 1)
            seen = lax.shift_right_logical(col, CHUNK_SHIFT) <= lax.shift_right_logical(row, CHUNK_SHIFT)
            s = jnp.where(seen, s, NEG_INF)
        scores.append(s)
    m = jnp.max(scores[0], axis=1, keepdims=True)
    for s in scores[1:]:
        m = jnp.maximum(m, jnp.max(s, axis=1, keepdims=True))
    ps = [jnp.exp(s - m) for s in scores]
    total = jnp.sum(ps[0], axis=1, keepdims=True)
    for p in ps[1:]:
        total = total + jnp.sum(p, axis=1, keepdims=True)
    inv = 1.0 / total
    return [p * inv for p in ps]


def _attn_probs_t(q, k_ref, lvl):
    scores = []
    for lo, hi, diagonal in _key_ranges(lvl):
        s = lax.dot_general(k_ref[lo:hi, :], q, NT_DIMS, preferred_element_type=F32) * ATT_SCALE
        if diagonal:
            key = lax.broadcasted_iota(jnp.int32, s.shape, 0)
            qry = lax.broadcasted_iota(jnp.int32, s.shape, 1)
            seen = lax.shift_right_logical(key, CHUNK_SHIFT) <= lax.shift_right_logical(qry, CHUNK_SHIFT)
            s = jnp.where(seen, s, NEG_INF)
        scores.append(s)
    m = jnp.max(scores[0], axis=0, keepdims=True)
    for s in scores[1:]:
        m = jnp.maximum(m, jnp.max(s, axis=0, keepdims=True))
    ps = [jnp.exp(s - m) for s in scores]
    total = jnp.sum(ps[0], axis=0, keepdims=True)
    for p in ps[1:]:
        total = total + jnp.sum(p, axis=0, keepdims=True)
    inv = 1.0 / total
    return [p * inv for p in ps]


def _per_query_block(qi, n_blocks, branch):
    for lvl in range(n_blocks):
        pl.when(qi == lvl)(lambda lvl=lvl: branch(lvl))


def _attn_specs(t, g):
    q = pl.BlockSpec((ATT_TQ, g * HEAD_PAD), lambda h, i: (i, h))
    kn = pl.BlockSpec((None, t, g * QK_NOPE), lambda h, i: (0, 0, h))
    kr = pl.BlockSpec((t, LANES), lambda h, i: (0, 0))
    v = pl.BlockSpec((None, t, g * V_HEAD), lambda h, i: (1, 0, h))
    o = pl.BlockSpec((ATT_TQ, g * V_HEAD), lambda h, i: (i, h))
    return q, kn, kr, v, o


def _attn_fwd(q, knv, kr):
    t = q.shape[0]

    def body(q_ref, kn_ref, kr_ref, v_ref, o_ref, k_ref):
        _fill_keys(k_ref, kn_ref, kr_ref)

        def branch(lvl):
            for hh in range(FWD_HEADS):
                vcols = slice(hh * V_HEAD, (hh + 1) * V_HEAD)
                ps = _attn_probs(q_ref[:, hh * HEAD_PAD:(hh + 1) * HEAD_PAD], k_ref.at[hh], lvl)
                o = None
                for p, (lo, hi, _) in zip(ps, _key_ranges(lvl)):
                    part = jnp.dot(p.astype(BF16), v_ref[lo:hi, vcols], preferred_element_type=F32)
                    o = part if o is None else o + part
                o_ref[:, vcols] = o.astype(BF16)

        _per_query_block(pl.program_id(1), t // ATT_TQ, branch)

    qs, kns, krs, vs, os_ = _attn_specs(t, FWD_HEADS)
    return _tc_call(
        body, name="attn_fwd", grid=(N_HEADS // FWD_HEADS, t // ATT_TQ), in_specs=[qs, kns, krs, vs],
        out_specs=os_, out_shape=jax.ShapeDtypeStruct((t, N_HEADS * V_HEAD), BF16),
        scratch_shapes=[pltpu.VMEM((FWD_HEADS, t, HEAD_PAD), BF16)], compiler_params=_cp("parallel", "arbitrary"),
    )(q, knv, kr, knv)


def _attn_bwd(q, knv, kr, do, cos, sin):
    t = q.shape[0]

    def body(q_ref, kn_ref, kr_ref, v_ref, do_ref, c_ref, s_ref, dq_ref, dknv_ref, dkr_ref, k_ref, dk_ref):
        h, qi = pl.program_id(0), pl.program_id(1)
        _fill_keys(k_ref, kn_ref, kr_ref)

        @pl.when(qi == 0)
        def _():
            dknv_ref[1] = jnp.zeros(dknv_ref.shape[1:], F32)
            dk_ref[...] = jnp.zeros_like(dk_ref)

        @pl.when((qi == 0) & (h == 0))
        def _():
            dkr_ref[...] = jnp.zeros_like(dkr_ref)

        def branch(lvl):
            ranges = _key_ranges(lvl)
            for hh in range(BWD_HEADS):
                qcols = slice(hh * HEAD_PAD, (hh + 1) * HEAD_PAD)
                vcols = slice(hh * V_HEAD, (hh + 1) * V_HEAD)
                qv, dov = q_ref[:, qcols], do_ref[:, vcols]
                ps = _attn_probs_t(qv, k_ref.at[hh], lvl)
                dps = [lax.dot_general(v_ref[lo:hi, vcols], dov, NT_DIMS, preferred_element_type=F32)
                       for lo, hi, _ in ranges]
                di = None
                for p, dp in zip(ps, dps):
                    part = jnp.sum(p * dp, axis=0, keepdims=True)
                    di = part if di is None else di + part
                dq = None
                for p, dp, (lo, hi, _) in zip(ps, dps, ranges):
                    ds = (p * (dp - di) * ATT_SCALE).astype(BF16)
                    part = lax.dot_general(ds, k_ref[hh, lo:hi, :], TN_DIMS, preferred_element_type=F32)
                    dq = part if dq is None else dq + part
                    dk_ref[hh, lo:hi, :] += jnp.dot(ds, qv, preferred_element_type=F32)
                    dknv_ref[1, lo:hi, vcols] += jnp.dot(p.astype(BF16), dov, preferred_element_type=F32)
                dq_ref[:, hh * HEAD_PAD:hh * HEAD_PAD + QK_NOPE] = dq[:, :QK_NOPE].astype(BF16)
                dq_ref[:, hh * HEAD_PAD + QK_NOPE:(hh + 1) * HEAD_PAD] = _rope_bwd_math(
                    dq[:, QK_NOPE:], c_ref[...], s_ref[...]).astype(BF16)

        _per_query_block(qi, t // ATT_TQ, branch)

        @pl.when(qi == t // ATT_TQ - 1)
        def _():
            for hh in range(BWD_HEADS):
                dknv_ref[0, :, hh * QK_NOPE:(hh + 1) * QK_NOPE] = dk_ref[hh, :, :QK_NOPE]
                dkr_ref[...] += dk_ref[hh, :, QK_NOPE:]

    qs, kns, krs, vs, os_ = _attn_specs(t, BWD_HEADS)
    tab = pl.BlockSpec((ATT_TQ, LANES), lambda h, i: (i, 0))
    return _tc_call(
        body, name="attn_bwd", grid=(N_HEADS // BWD_HEADS, t // ATT_TQ), in_specs=[qs, kns, krs, vs, os_, tab, tab],
        out_specs=[qs, pl.BlockSpec((2, t, BWD_HEADS * QK_NOPE), lambda h, i: (0, 0, h)), krs],
        out_shape=[jax.ShapeDtypeStruct((t, N_HEADS * HEAD_PAD), BF16),
                   jax.ShapeDtypeStruct((2, t, N_HEADS * QK_NOPE), F32), jax.ShapeDtypeStruct((t, LANES), F32)],
        scratch_shapes=[pltpu.VMEM((BWD_HEADS, t, HEAD_PAD), BF16), pltpu.VMEM((BWD_HEADS, t, HEAD_PAD), F32)],
        compiler_params=_cp("arbitrary", "arbitrary"),
    )(q, knv, kr, knv, do, cos, sin)


def _adam_math(w, g, m, v):
    nm = ADAM_B1 * m + (1.0 - ADAM_B1) * g
    nv = ADAM_B2 * v + (1.0 - ADAM_B2) * (g * g)
    m_hat = nm / (1.0 - ADAM_B1 ** ADAM_STEP)
    v_hat = nv / (1.0 - ADAM_B2 ** ADAM_STEP)
    return -ADAM_LR * (m_hat / (jnp.sqrt(v_hat) + ADAM_EPS) + ADAM_WD * w), nm, nv


def _adamw_small(ws, gs, ms, vs):
    n = len(ws)

    def body(*refs):
        for i in range(n):
            w_ref, g_ref, m_ref, v_ref = (refs[k * n + i] for k in range(4))
            go_ref, d_ref, nm_ref, nv_ref = (refs[(4 + k) * n + i] for k in range(4))
            go_ref[...] = g_ref[...]
            d_ref[...], nm_ref[...], nv_ref[...] = _adam_math(w_ref[...], g_ref[...], m_ref[...], v_ref[...])

    shapes = [jax.ShapeDtypeStruct(a.shape, F32) for a in ws]
    res = _tc_call(body, name="adamw_small", out_shape=shapes * 4)(*ws, *gs, *ms, *vs)
    return [res[k * n:(k + 1) * n] for k in range(4)]


ADAM_SPLIT = 4


def _store_without_head_padding(dst_ref, g):
    assert HEAD_PAD == 2 * LANES and 2 * (QK_NOPE + QK_ROPE) == 3 * LANES, (HEAD_PAD, QK_NOPE, QK_ROPE)
    assert g.shape[1] % (2 * HEAD_PAD) == 0, g.shape
    low = lax.broadcasted_iota(jnp.int32, (g.shape[0], LANES), 1) < LANES // 2
    for pair in range(g.shape[1] // (2 * HEAD_PAD)):
        t = [g[:, (4 * pair + k) * LANES:(4 * pair + k + 1) * LANES] for k in range(4)]
        moved = [pltpu.roll(t[k], LANES // 2, axis=1) for k in (2, 3)]
        outs = (t[0], jnp.where(low, t[1], moved[0]), jnp.where(low, moved[0], moved[1]))
        for k, o in enumerate(outs):
            dst_ref[:, (3 * pair + k) * LANES:(3 * pair + k + 1) * LANES] = o


def _adamw_shards(ids, items, name):
    n = len(items)

    def body(ids_ref, *refs):
        outs = refs[len(refs) - 4 * n:]
        for i, it in enumerate(items):
            w_ref, m_ref, v_ref, gm_ref, gs_ref = refs[5 * i:5 * i + 5]
            g_ref, d_ref, nm_ref, nv_ref = outs[4 * i:4 * i + 4]
            cols = slice(*it["gcols"]) if it.get("gcols") else slice(None)
            whose = pl.program_id(0) if it.get("owner") is None else it["owner"]
            mine = whose == ids_ref[0]

            def take(src_ref, g_ref=g_ref, cols=cols, head_padded=it.get("head_padded")):
                if head_padded:
                    _store_without_head_padding(g_ref, src_ref[...])
                else:
                    g_ref[...] = src_ref[:, cols]

            @pl.when(mine)
            def _(take=take, gm_ref=gm_ref):
                take(gm_ref)

            @pl.when(jnp.logical_not(mine))
            def _(take=take, gs_ref=gs_ref):
                take(gs_ref)

            d_ref[...], nm_ref[...], nv_ref[...] = _adam_math(w_ref[...], g_ref[...], m_ref[...], v_ref[...])

    in_specs, out_specs, out_shape, args, carried, aliases = [], [], [], [ids], [], {}
    for i, it in enumerate(items):
        w = it["w"]
        r, c = w.shape[-2:]
        tr = r // 2 // ADAM_SPLIT
        assert tr % 8 == 0, (name, w.shape)
        layer = it.get("layer")
        if layer is None:
            wspec = pl.BlockSpec((tr, c), lambda h, k, ids: (h * ADAM_SPLIT + k, 0))
        else:
            wspec = pl.BlockSpec((None, tr, c), lambda h, k, ids, layer=layer: (layer, h * ADAM_SPLIT + k, 0))
        gc = it["g_mine"].shape[1]
        if it.get("owner") is None:
            gspec = pl.BlockSpec((tr, gc), lambda h, k, ids: (k, 0))
        else:
            gspec = pl.BlockSpec((tr, gc), lambda h, k, ids: (h * ADAM_SPLIT + k, 0))
        in_specs += [wspec] * 3 + [gspec] * 2
        args += [w, it["m"], it["v"], it["g_mine"], it["g_sib"]]
        out_specs += [wspec] * 4
        out_shape += [jax.ShapeDtypeStruct(w.shape, F32)] * 4
        if it.get("prev") is not None:
            for k, p in enumerate(it["prev"]):
                aliases[1 + 5 * n + len(carried)] = 4 * i + k
                carried.append(p)
    res = _tc_call(
        body, name=name, prefetch=1, grid=(2, ADAM_SPLIT), in_specs=in_specs + [ANY] * len(carried),
        out_specs=out_specs, out_shape=out_shape, input_output_aliases=aliases,
        compiler_params=_cp("parallel", "parallel"),
    )(*args, *carried)
    return [res[4 * i:4 * i + 4] for i in range(n)]


def _peer_chip(k_me, j):
    return k_me ^ jnp.where(j == 0, 2, jnp.where(j == 1, 1, 3))


def _pair_sums(ids, gs, ras, name):
    n = len(gs)

    def body(ids_ref, *refs):
        for i in range(n):
            g_ref, ra_ref, o_ref = refs[2 * i], refs[2 * i + 1], refs[2 * n + i]
            o_ref[...] = (g_ref[...].astype(F32) + ra_ref[...].astype(F32)).astype(BF16)

    in_specs, out_specs, out_shape = [], [], []
    for g in gs:
        half, c = g.shape[1] // 2, g.shape[2]
        in_specs += [pl.BlockSpec((None, half, c), lambda j, ids: (_peer_chip(ids[1], j), ids[0], 0)),
                     pl.BlockSpec((None, half, c), lambda j, ids: (_peer_chip(ids[1], j), 0, 0))]
        out_specs.append(pl.BlockSpec((None, half, c), lambda j, ids: (j, 0, 0)))
        out_shape.append(jax.ShapeDtypeStruct((3, half, c), BF16))
    return _tc_call(
        body, name=name, prefetch=1, grid=(3,), in_specs=in_specs, out_specs=out_specs, out_shape=out_shape,
        compiler_params=_cp("parallel"),
    )(ids, *[a for pair in zip(gs, ras) for a in pair])


def _chip_sums(ids, gs, ras, rbs, name):
    n = len(gs)

    def body(ids_ref, *refs):
        for i in range(n):
            g_ref, ra_ref, rb_ref, o_ref = refs[3 * i], refs[3 * i + 1], refs[3 * i + 2], refs[3 * n + i]
            acc = g_ref[...].astype(F32) + ra_ref[...].astype(F32)
            for j in range(3):
                acc = acc + rb_ref[j].astype(F32)
            o_ref[...] = acc

    in_specs, out_specs, out_shape = [], [], []
    for g in gs:
        half, c = g.shape[1] // 2, g.shape[2]
        in_specs += [pl.BlockSpec((None, half, c), lambda i, ids: (ids[1], ids[0], 0)),
                     pl.BlockSpec((None, half, c), lambda i, ids: (ids[1], 0, 0)),
                     pl.BlockSpec((3, half, c), lambda i, ids: (0, 0, 0))]
        out_specs.append(pl.BlockSpec((half, c), lambda i, ids: (0, 0)))
        out_shape.append(jax.ShapeDtypeStruct((half, c), F32))
    return _tc_call(
        body, name=name, prefetch=1, grid=(1,), in_specs=in_specs, out_specs=out_specs, out_shape=out_shape,
        compiler_params=_cp("arbitrary"),
    )(ids, *[a for trio in zip(gs, ras, rbs) for a in trio])


def _position():
    x, y, c = lax.axis_index("x"), lax.axis_index("y"), lax.axis_index("c")
    chips = [(1 - x, y), (x, 1 - y), (1 - x, 1 - y)]
    return x, y, c, chips


def _shard_half(ref, wm, h):
    if wm.kind == "tiny":
        return ref
    if wm.nl == 2:
        return ref.at[h]
    return ref.at[pl.ds(pl.multiple_of(h * (wm.k // 2), 16), wm.k // 2), :]


def _region(full, wm, s, h):
    if wm.kind == "tiny":
        return full.at[s]
    cols = pl.ds(pl.multiple_of(s * wm.n, LANES), wm.n) if wm.kind == "col" else slice(None)
    if wm.nl == 2:
        rows = pl.ds(pl.multiple_of(s * wm.k, 16), wm.k) if wm.kind == "row" else slice(None)
        return full.at[slice(None) if h is None else h, rows, cols]
    if wm.kind == "col":
        rows = slice(None) if h is None else pl.ds(pl.multiple_of(h * (wm.k // 2), 16), wm.k // 2)
    elif h is None:
        rows = pl.ds(pl.multiple_of(s * wm.k, 16), wm.k)
    else:
        rows = pl.ds(pl.multiple_of(s * wm.k + h * (wm.k // 2), 16), wm.k // 2)
    return full.at[rows, cols]


def _full_shape(wm):
    if wm.kind == "tiny":
        return (N_CHIPS, wm.k, wm.n)
    shape = (wm.k, N_CHIPS * wm.n) if wm.kind == "col" else (N_CHIPS * wm.k, wm.n)
    return shape if wm.nl == 1 else (wm.nl,) + shape


def _handshake(peers):
    barrier = pltpu.get_barrier_semaphore()
    for peer in peers:
        pl.semaphore_signal(barrier, inc=1, device_id=peer, device_id_type=MESH)
    pl.semaphore_wait(barrier, len(peers))


def _all_gather_group(gi, shards):
    wms = AG_GROUPS[gi]
    nw = len(wms)

    def body(*refs):
        sh, full = refs[:nw], refs[nw:2 * nw]
        ici_s, ici_r, pass_s, pass_r, own_s, own_r = refs[2 * nw:]
        x, y, c, _ = _position()
        me, sibling = 2 * x + y, (x, y, 1 - c)
        first, second, diagonal = (x ^ (1 - c), y ^ c), (x ^ c, y ^ (1 - c)), (1 - x, 1 - y)
        chip_id = lambda chip: 2 * chip[0] + chip[1]
        _handshake([(*first, c), (*second, c), sibling])

        def rcopy(src, dst, s_sem, r_sem, to):
            return pltpu.make_async_remote_copy(src_ref=src, dst_ref=dst, send_sem=s_sem, recv_sem=r_sem,
                                                device_id=to, device_id_type=MESH)

        started = []

        def go(cp):
            cp.start()
            started.append(cp)

        for i, wm in enumerate(wms):
            half, dst = _shard_half(sh[i], wm, c), _region(full[i], wm, me, c)
            go(rcopy(half, dst, ici_s.at[i, 0], ici_r.at[i, 0], (*first, c)))
            go(rcopy(half, dst, ici_s.at[i, 1], ici_r.at[i, 1], (*second, c)))
            go(rcopy(sh[i], _region(full[i], wm, me, None), own_s.at[i], own_r.at[i], sibling))
        for i, wm in enumerate(wms):
            got = _region(full[i], wm, chip_id(first), c)
            rcopy(got, got, ici_s.at[i, 0], ici_r.at[i, 0], sibling).wait_recv()
            go(rcopy(got, got, ici_s.at[i, 2], ici_r.at[i, 2], (*second, c)))
            if wm.kind != "tiny":
                go(rcopy(got, got, pass_s.at[i, 0], pass_r.at[i, 0], sibling))
        for i, wm in enumerate(wms):
            for j, chip in ((1, second), (2, diagonal)):
                got = _region(full[i], wm, chip_id(chip), c)
                rcopy(got, got, ici_s.at[i, j], ici_r.at[i, j], sibling).wait_recv()
                if wm.kind != "tiny":
                    go(rcopy(got, got, pass_s.at[i, j], pass_r.at[i, j], sibling))
        for i, wm in enumerate(wms):
            mine = _region(full[i], wm, me, None)
            rcopy(mine, mine, own_s.at[i], own_r.at[i], sibling).wait_recv()
            if wm.kind != "tiny":
                for j, chip in ((0, second), (1, first), (2, diagonal)):
                    got = _region(full[i], wm, chip_id(chip), 1 - c)
                    rcopy(got, got, pass_s.at[i, j], pass_r.at[i, j], sibling).wait_recv()
        for cp in started:
            cp.wait_send()

    return pl.kernel(
        body, out_type=[jax.ShapeDtypeStruct(_full_shape(wm), s.dtype) for wm, s in zip(wms, shards)],
        mesh=plsc.ScalarSubcoreMesh(axis_name="sequencer", num_cores=1), name=f"ag_group{gi}",
        scratch_types=[pltpu.SemaphoreType.DMA((nw, 3))] * 4 + [pltpu.SemaphoreType.DMA((nw,))] * 2,
        compiler_params=pltpu.CompilerParams(collective_id=gi),
    )(*shards)


def _sequencer_call(body, name, cid, out_types, scratch, args):
    return pl.kernel(
        body, out_type=out_types, mesh=plsc.ScalarSubcoreMesh(axis_name="sequencer", num_cores=1), name=name,
        scratch_types=scratch, compiler_params=pltpu.CompilerParams(collective_id=cid),
    )(*args)


def _pair_exchange(gs, tag, cid):
    n = len(gs)

    def body(*refs):
        g, out, send_sems, recv_sems = refs[:n], refs[n:2 * n], refs[2 * n], refs[2 * n + 1]
        x, y, c, _ = _position()
        _handshake([(x, y, 1 - c)])
        cps = []
        for i in range(n):
            half = g[i].shape[1] // 2
            cps.append(pltpu.make_async_remote_copy(
                src_ref=g[i].at[:, pl.ds(pl.multiple_of((1 - c) * half, 16), half), :], dst_ref=out[i],
                send_sem=send_sems.at[i], recv_sem=recv_sems.at[i], device_id=(x, y, 1 - c), device_id_type=MESH))
            cps[-1].start()
        for cp in cps:
            cp.wait()

    return _sequencer_call(
        body, f"rs_pair_exchange{tag}", cid,
        [jax.ShapeDtypeStruct((a.shape[0], a.shape[1] // 2, a.shape[2]), a.dtype) for a in gs],
        [pltpu.SemaphoreType.DMA((n,)), pltpu.SemaphoreType.DMA((n,))], gs)


def _chip_exchange(ss, tag, cid):
    n = len(ss)

    def body(*refs):
        s, out, send_sems, recv_sems = refs[:n], refs[n:2 * n], refs[2 * n], refs[2 * n + 1]
        x, y, c, chips = _position()
        _handshake([(*chip, c) for chip in chips])
        cps = []
        for i in range(n):
            for j, chip in enumerate(chips):
                cps.append(pltpu.make_async_remote_copy(
                    src_ref=s[i].at[j], dst_ref=out[i].at[j], send_sem=send_sems.at[i, j], recv_sem=recv_sems.at[i, j],
                    device_id=(*chip, c), device_id_type=MESH))
                cps[-1].start()
        for cp in cps:
            cp.wait()

    return _sequencer_call(
        body, f"rs_chip_exchange{tag}", cid, [jax.ShapeDtypeStruct(a.shape, a.dtype) for a in ss],
        [pltpu.SemaphoreType.DMA((n, 3)), pltpu.SemaphoreType.DMA((n, 3))], ss)


def _pair_swap(g8s, tag, cid):
    n = len(g8s)

    def body(*refs):
        g, out, send_sems, recv_sems = refs[:n], refs[n:2 * n], refs[2 * n], refs[2 * n + 1]
        x, y, c, _ = _position()
        _handshake([(x, y, 1 - c)])
        cps = []
        for i in range(n):
            cps.append(pltpu.make_async_remote_copy(
                src_ref=g[i], dst_ref=out[i], send_sem=send_sems.at[i], recv_sem=recv_sems.at[i],
                device_id=(x, y, 1 - c), device_id_type=MESH))
            cps[-1].start()
        for cp in cps:
            cp.wait()

    return _sequencer_call(
        body, f"rs_pair_swap{tag}", cid, [jax.ShapeDtypeStruct(a.shape, a.dtype) for a in g8s],
        [pltpu.SemaphoreType.DMA((n,)), pltpu.SemaphoreType.DMA((n,))], g8s)


def _pair_swap_now(g8s):
    n = len(g8s)

    def body(*refs):
        g, out, send_sems, recv_sems = refs[:n], refs[n:2 * n], refs[2 * n], refs[2 * n + 1]
        x, y, c, _ = _position()
        cps = []
        for i in range(n):
            cps.append(pltpu.make_async_remote_copy(
                src_ref=g[i], dst_ref=out[i], send_sem=send_sems.at[i], recv_sem=recv_sems.at[i],
                device_id=(x, y, 1 - c), device_id_type=MESH))
            cps[-1].start()
        for cp in cps:
            cp.wait()

    return _tc_call(
        body, name="rs_pair_swap_last", in_specs=[ANY] * n, out_specs=[ANY] * n,
        out_shape=[jax.ShapeDtypeStruct(a.shape, a.dtype) for a in g8s],
        scratch_shapes=[pltpu.SemaphoreType.DMA((n,)), pltpu.SemaphoreType.DMA((n,))],
    )(*g8s)


def _all_reduce_small(vecs, owner_major, name):
    n = len(vecs)
    block = lambda i, ref, chip: ref.at[chip] if owner_major[i] else ref
    out_shapes = [a.shape[1:] if owner_major[i] else a.shape for i, a in enumerate(vecs)]

    def body(*refs):
        v, o, gath = refs[:n], refs[n:2 * n], refs[2 * n:3 * n]
        send_sems, recv_sems = refs[3 * n], refs[3 * n + 1]
        x, y, c, _ = _position()
        me = 4 * x + 2 * y + c
        cps = []
        for i in range(n):
            gath[i][me] = block(i, v[i], 2 * x + y)[...]
            for rel in range(1, N_DEV):
                px, py, pc = x ^ (rel >> 2), y ^ ((rel >> 1) & 1), c ^ (rel & 1)
                cps.append(pltpu.make_async_remote_copy(
                    src_ref=block(i, v[i], 2 * px + py), dst_ref=gath[i].at[me], send_sem=send_sems.at[i, rel - 1],
                    recv_sem=recv_sems.at[i, rel - 1], device_id=(px, py, pc), device_id_type=MESH))
        for cp in cps:
            cp.start()
        for i in range(n):
            for rel in range(1, N_DEV):
                pltpu.make_async_remote_copy(
                    src_ref=block(i, v[i], 2 * x + y), dst_ref=gath[i].at[me ^ rel],
                    send_sem=send_sems.at[i, rel - 1], recv_sem=recv_sems.at[i, rel - 1], device_id=(x, y, c),
                    device_id_type=MESH).wait_recv()
        for cp in cps:
            cp.wait_send()
        for i in range(n):
            acc = gath[i][0]
            for d in range(1, N_DEV):
                acc = acc + gath[i][d]
            o[i][...] = acc

    vm = pl.BlockSpec(memory_space=pltpu.VMEM)
    return _tc_call(
        body, name=name, in_specs=[vm] * n, out_specs=[vm] * n,
        out_shape=[jax.ShapeDtypeStruct(s, F32) for s in out_shapes],
        scratch_shapes=[pltpu.VMEM((N_DEV,) + s, F32) for s in out_shapes]
        + [pltpu.SemaphoreType.DMA((n, N_DEV - 1)), pltpu.SemaphoreType.DMA((n, N_DEV - 1))],
    )(*vecs)


def _rope_tables(positions):
    half = QK_ROPE // 2
    inv_freq = 1.0 / (ROPE_THETA ** (jnp.arange(half, dtype=F32) / half))
    ang = positions.astype(F32)[:, None] * inv_freq
    zeros = jnp.zeros((positions.shape[0], LANES - QK_ROPE), F32)
    cos, sin = jnp.cos(ang), jnp.sin(ang)
    return jnp.concatenate([cos, cos, zeros], axis=1), jnp.concatenate([sin, sin, zeros], axis=1)


def _local_step(x, positions, tgt, wf, small, rs):
    cos, sin = _rope_tables(positions)
    w_in, w_out = wf["sc_w_in"], wf["sc_w_out"]
    w_ups, w_downs = (wf["ffn_w_up0"], wf["ffn_w_up1"]), (wf["ffn_w_down0"], wf["ffn_w_down1"])
    w_kv, w_ukv, w_dq, w_uq, w_o = wf["w_kv"], wf["w_ukv"], wf["w_dq"], wf["w_uq"], wf["w_o"]
    attn_norm, ffn_norm = small["attn_norm"], small["ffn_norm"]
    conv_b = small["ffn_conv_b"]

    def ffn_fwd(h, hf, l, then):
        up, a = _ffn_up_gate(hf, w_ups[l], small["ffn_conv_w"][l], conv_b[l:l + 1], f"ffn{l}_up_gate")
        return then(a, w_downs[l], h), (hf, up, a)

    def ffn_bwd(h, dh_out, dh_out_b, l, saved, gi, hooks):
        run = lambda stage: hooks.get(stage, lambda: None)()
        hf, up, a = saved
        d_down = _tn(f"ffn{l}_down_dw", a, dh_out_b, BF16)
        run("down_dw")
        dup, d_cw, d_cb = _gate_bwd(up, small["ffn_conv_w"][l], conv_b[l:l + 1], dh_out_b, w_downs[l],
                                    f"ffn{l}_gate_bwd")
        run("gate_bwd")
        d_up = _dw_ffn_up(f"ffn{l}_up_dw", hf, dup)
        rs.start(gi, {f"ffn_w_down{l}": d_down.reshape(N_CHIPS, F_FF // N_CHIPS, D), f"ffn_w_up{l}": d_up})
        run("up_dw")
        dh, dh_b, d_norm = _dx_norm_bwd(f"ffn{l}_up_dx", dup, w_ups[l], h, ffn_norm[l:l + 1], dh_out)
        run("up_dx")
        return dh, dh_b, d_cw, d_cb, d_norm

    hn0 = _rms_fwd(x, attn_norm[0:1], "attn0_norm")
    z = _nn_parts("sc_in", hn0, w_in, 3, BF16)
    mix = _scmix_fwd(z, small["sc_conv_w"])
    h1, hf0 = _nn_add_norm("sc_out", mix, w_out, x, ffn_norm[0:1])
    h2, ffn0_saved = ffn_fwd(h1, hf0, 0, lambda a, w, h: _nn("ffn0_down", a, w, F32, add=h))

    hn1, hk, cq_pre, cq, q, kvpre, ckv, kr, knv = _attn_prep(
        h2, attn_norm[1:2], small["kv_in_norm"], w_dq, small["q_latent_norm"], w_uq, w_kv, small["kv_latent_norm"],
        w_ukv, cos, sin)
    o = _attn_fwd(q, knv, kr)
    h3, hf1 = _nn_add_norm("attn_out", o, w_o, h2, ffn_norm[1:2])
    (loss, dh4, dh4_b, d_final), ffn1_saved = ffn_fwd(
        h3, hf1, 1, lambda a, w, h: _nn_add_loss("ffn1_down_loss", a, w, h, small["final_norm"], tgt))

    rows = D // N_CHIPS
    dh3, dh3_b, d_cw1, d_cb1, d_fn1 = ffn_bwd(h3, dh4, dh4_b, 1, ffn1_saved, 0, {})

    do = _nt("attn_out_dx", dh3_b, w_o, BF16)
    d_wo = _tn("attn_out_dw", o, dh3_b, BF16)
    rs.pair_sums(0)
    dq, dknv, dkr = _attn_bwd(q, knv, kr, do, cos, sin)
    rs.chip_sums(0)
    dh2, dh2_b, d_wuq, d_wdq, d_wukv, d_wkv, d_an1, d_kvin, d_qln, d_kvln = _attn_prep_bwd(
        dq, dknv, dkr, dh3, h2, hn1, hk, cq_pre, cq, kvpre, ckv, attn_norm[1:2], small["kv_in_norm"], w_dq,
        small["q_latent_norm"], w_uq, w_kv, small["kv_latent_norm"], w_ukv, cos, sin)
    rs.finish(0)
    by_owner = lambda dw: dw.reshape(dw.shape[0], N_CHIPS, -1).transpose(1, 0, 2)
    rs.start(1, {
        "w_o": d_wo.reshape(N_CHIPS, rows, D), "w_uq": by_owner(d_wuq), "w_dq": d_wdq.reshape(N_CHIPS, rows, Q_LORA),
        "w_ukv": by_owner(d_wukv.reshape(2 * KV_LORA, -1)).reshape(N_CHIPS, 2 * KV_LORA, -1),
        "w_kv": d_wkv.reshape(N_CHIPS, rows, KVP),
    })

    dh1, dh1_b, d_cw0, d_cb0, d_fn0 = ffn_bwd(h1, dh2, dh2_b, 0, ffn0_saved, 2, {
        "down_dw": lambda: rs.pair_sums(1), "gate_bwd": lambda: rs.chip_sums(1),
        "up_dw": lambda: (rs.finish(1), rs.pair_sums(2))})

    d_wout = _tn("sc_out_dw", mix, dh1_b, BF16)
    dmix = _nt("sc_out_dx", dh1_b, w_out, BF16)
    dz, d_scw = _scmix_bwd(z, small["sc_conv_w"], dmix)
    d_win = _dw_sc_in(hn0, dz)
    rs.start(3, {"sc_w_out": d_wout.reshape(N_CHIPS, rows, D), "sc_w_in": d_win})
    dx, _, d_an0 = _dx_norm_bwd("sc_in_dx", dz, w_in, x, attn_norm[0:1], dh1)

    taps_by_owner = lambda per_layer: jnp.stack(per_layer, axis=1).reshape(3, len(per_layer), N_CHIPS, -1).transpose(2, 0, 1, 3)
    small_g = {
        "attn_norm": jnp.concatenate([d_an0, d_an1]), "ffn_norm": jnp.concatenate([d_fn0, d_fn1]),
        "final_norm": d_final, "kv_in_norm": d_kvin, "kv_latent_norm": d_kvln, "q_latent_norm": d_qln,
        "ffn_conv_b": jnp.concatenate([d_cb0, d_cb1]),
        "sc_conv_w": taps_by_owner([d_scw]), "ffn_conv_w": taps_by_owner([d_cw0, d_cw1]),
    }
    return loss, dx, small_g


RS_GROUPS = (("ffn_w_down1", "ffn_w_up1"), ("w_o", "w_uq", "w_dq", "w_ukv", "w_kv"),
             ("ffn_w_down0", "ffn_w_up0"), ("sc_w_out", "sc_w_in"))


class _ReduceScatter:
    def __init__(self, ids, finish):
        self.ids, self.grads, self.step, self.mine, self.sib, self.finish = ids, {}, {}, {}, {}, finish

    def _cid(self, gi):
        return len(AG_GROUPS) + 3 * gi

    def start(self, gi, grads):
        self.grads.update(grads)
        own = [grads[n] for n in RS_GROUPS[gi]]
        self.step[gi] = (own, _pair_exchange(own, gi, self._cid(gi)))

    def pair_sums(self, gi):
        own, ra = self.step[gi]
        sums = _pair_sums(self.ids, own, ra, f"rs_pair_sums{gi}")
        self.step[gi] = (own, ra, _chip_exchange(sums, gi, self._cid(gi) + 1))

    def chip_sums(self, gi):
        own, ra, rb = self.step[gi]
        mine = _chip_sums(self.ids, own, ra, rb, f"rs_chip_sums{gi}")
        self.mine.update(zip(RS_GROUPS[gi], mine))
        last = gi == len(RS_GROUPS) - 1
        swapped = _pair_swap_now(mine) if last else _pair_swap(mine, gi, self._cid(gi) + 2)
        self.sib.update(zip(RS_GROUPS[gi], swapped))


SMALL_REPL = ("attn_norm", "ffn_norm", "final_norm", "kv_in_norm", "kv_latent_norm", "q_latent_norm", "ffn_conv_b")


def _pad_heads(w_uq):
    per_head = w_uq.reshape(Q_LORA, -1, QK_NOPE + QK_ROPE)
    return jnp.pad(per_head, ((0, 0), (0, 0), (0, HEAD_PAD - QK_NOPE - QK_ROPE))).reshape(Q_LORA, -1)


def _pack_kv(w_dkv, w_kr):
    return jnp.concatenate([w_dkv, w_kr, jnp.zeros((w_kr.shape[0], LANES - QK_ROPE), w_kr.dtype)], axis=1)


def kernel(x, positions, attn_norm, ffn_norm, final_norm, sc_w_in, sc_conv_w, sc_w_out, kv_in_norm, w_dkv, kv_latent_norm, w_kr, w_uk, w_uv, w_dq, q_latent_norm, w_uq, w_o, ffn_w_up, ffn_conv_w, ffn_conv_b, ffn_w_down, loss_target, m_attn_norm, m_ffn_norm, m_final_norm, m_sc_w_in, m_sc_conv_w, m_sc_w_out, m_kv_in_norm, m_w_dkv, m_kv_latent_norm, m_w_kr, m_w_uk, m_w_uv, m_w_dq, m_q_latent_norm, m_w_uq, m_w_o, m_ffn_w_up, m_ffn_conv_w, m_ffn_conv_b, m_ffn_w_down, v_attn_norm, v_ffn_norm, v_final_norm, v_sc_w_in, v_sc_conv_w, v_sc_w_out, v_kv_in_norm, v_w_dkv, v_kv_latent_norm, v_w_kr, v_w_uk, v_w_uv, v_w_dq, v_q_latent_norm, v_w_uq, v_w_o, v_ffn_w_up, v_ffn_conv_w, v_ffn_conv_b, v_ffn_w_down):
    names = ("attn_norm", "ffn_norm", "final_norm", "sc_w_in", "sc_conv_w", "sc_w_out", "kv_in_norm", "w_dkv",
             "kv_latent_norm", "w_kr", "w_uk", "w_uv", "w_dq", "q_latent_norm", "w_uq", "w_o", "ffn_w_up",
             "ffn_conv_w", "ffn_conv_b", "ffn_w_down")
    w = dict(zip(names, (attn_norm, ffn_norm, final_norm, sc_w_in, sc_conv_w, sc_w_out, kv_in_norm, w_dkv,
                         kv_latent_norm, w_kr, w_uk, w_uv, w_dq, q_latent_norm, w_uq, w_o, ffn_w_up,
                         ffn_conv_w, ffn_conv_b, ffn_w_down)))
    m = dict(zip(names, (m_attn_norm, m_ffn_norm, m_final_norm, m_sc_w_in, m_sc_conv_w, m_sc_w_out, m_kv_in_norm,
                         m_w_dkv, m_kv_latent_norm, m_w_kr, m_w_uk, m_w_uv, m_w_dq, m_q_latent_norm, m_w_uq, m_w_o,
                         m_ffn_w_up, m_ffn_conv_w, m_ffn_conv_b, m_ffn_w_down)))
    v = dict(zip(names, (v_attn_norm, v_ffn_norm, v_final_norm, v_sc_w_in, v_sc_conv_w, v_sc_w_out, v_kv_in_norm,
                         v_w_dkv, v_kv_latent_norm, v_w_kr, v_w_uk, v_w_uv, v_w_dq, v_q_latent_norm, v_w_uq, v_w_o,
                         v_ffn_w_up, v_ffn_conv_w, v_ffn_conv_b, v_ffn_w_down)))

    _ORDER[0] = None
    ix, iy, ic = lax.axis_index("x"), lax.axis_index("y"), lax.axis_index("c")
    chip = 2 * ix + iy
    ids = jnp.stack([ic, chip]).astype(jnp.int32)

    ws = {
        "sc_w_in": sc_w_in[0], "sc_w_out": sc_w_out[0], "ffn_w_up": ffn_w_up, "ffn_w_down": ffn_w_down,
        "w_kv": _pack_kv(w_dkv, w_kr), "w_ukv": jnp.stack([w_uk, w_uv]), "w_dq": w_dq[0],
        "w_uq": _pad_heads(w_uq[0]), "w_o": w_o[0],
    }

    def ag_shard(name):
        if name == "sc_conv_w":
            return sc_conv_w[0]
        if name == "ffn_conv_w":
            return ffn_conv_w.reshape(6, -1)
        if name[:-1] in ("ffn_w_up", "ffn_w_down"):
            return ws[name[:-1]][int(name[-1])].astype(BF16)
        return ws[name].astype(BF16)

    wf = {}
    for gi, wms in enumerate(AG_GROUPS):
        fulls = _all_gather_group(gi, [ag_shard(wm.name) for wm in wms])
        wf.update({wm.name: f for wm, f in zip(wms, fulls)})
    small = {
        "attn_norm": attn_norm, "ffn_norm": ffn_norm, "final_norm": final_norm[None], "kv_in_norm": kv_in_norm[None],
        "kv_latent_norm": kv_latent_norm[None], "q_latent_norm": q_latent_norm, "ffn_conv_b": ffn_conv_b,
        "sc_conv_w": wf["sc_conv_w"].transpose(1, 0, 2).reshape(3, D),
        "ffn_conv_w": wf["ffn_conv_w"].reshape(N_CHIPS, 2, 3, -1).transpose(1, 2, 0, 3).reshape(2, 3, F_FF),
    }

    res = {}

    held = {
        "ffn_w_up0": [("ffn_w_up", dict(layer=0))], "ffn_w_up1": [("ffn_w_up", dict(layer=1))],
        "ffn_w_down0": [("ffn_w_down", dict(layer=0))], "ffn_w_down1": [("ffn_w_down", dict(layer=1))],
        "sc_w_in": [("sc_w_in", dict(layer=0))], "sc_w_out": [("sc_w_out", dict(layer=0))],
        "w_dq": [("w_dq", dict(layer=0))], "w_o": [("w_o", dict(layer=0))], "w_uq": [("w_uq", dict(layer=0, head_padded=True))],
        "w_kv": [("w_dkv", dict(gcols=(0, KV_LORA))), ("w_kr", dict(gcols=(KV_LORA, KV_LORA + QK_ROPE)))],
        "w_ukv": [("w_uk", dict(owner=0)), ("w_uv", dict(owner=1))],
    }

    def adamw_group(gi):
        items = []
        for key in RS_GROUPS[gi]:
            for n, opts in held[key]:
                items.append(dict(name=n, w=w[n], m=m[n], v=v[n], g_mine=rs.mine[key], g_sib=rs.sib[key],
                                  prev=res.get(n) if "layer" in opts and w[n].shape[0] > 1 else None, **opts))
        for it, out in zip(items, _adamw_shards(ids, items, f"adamw_group{gi}")):
            res[it["name"]] = out

    rs = _ReduceScatter(ids, adamw_group)
    loss, dx, small_g = _local_step(x[0], positions[0], loss_target[0], wf, small, rs)

    rs.chip_sums(2)
    rs.pair_sums(3)

    s_names = list(small_g)
    reduced = _all_reduce_small([small_g[n] for n in s_names] + [loss], [small_g[n].ndim == 4 for n in s_names] + [False],
                                "ar_small")
    sg, loss_out = dict(zip(s_names, reduced[:-1])), reduced[-1][0, 0]

    row = lambda n: (lambda t: t[n][None])
    taps = lambda n: (lambda t: t[n].transpose(1, 0, 2))
    small_2d = {
        "attn_norm": (sg["attn_norm"], lambda t: t["attn_norm"]), "ffn_norm": (sg["ffn_norm"], lambda t: t["ffn_norm"]),
        "final_norm": (sg["final_norm"], row("final_norm")), "kv_in_norm": (sg["kv_in_norm"], row("kv_in_norm")),
        "kv_latent_norm": (sg["kv_latent_norm"], row("kv_latent_norm")),
        "q_latent_norm": (sg["q_latent_norm"], lambda t: t["q_latent_norm"]),
        "ffn_conv_b": (sg["ffn_conv_b"], lambda t: t["ffn_conv_b"]),
        "sc_conv_w": (sg["sc_conv_w"], taps("sc_conv_w")), "ffn_conv_w": (sg["ffn_conv_w"], taps("ffn_conv_w")),
    }
    s_keys = list(small_2d)
    small_grads = [small_2d[k][0] for k in s_keys]
    views = lambda tree: [small_2d[k][1](tree) for k in s_keys]
    small_res = _adamw_small(views(w), small_grads, views(m), views(v))

    def restore(vals):
        by = dict(zip(s_keys, vals))
        out = {n: by[n].reshape(w[n].shape) for n in SMALL_REPL}
        out.update({n: by[n].transpose(1, 0, 2) for n in ("sc_conv_w", "ffn_conv_w")})
        return out

    rs.finish(2)
    rs.chip_sums(3)
    rs.finish(3)
    outs = [restore(vals) for vals in small_res]
    for k, dst in enumerate(outs):
        for n in res:
            dst[n] = res[n][k]
    grads, delta, new_m, new_v = outs

    _ORDER[0] = None
    return (loss_out, dx[None], *[grads[n] for n in names], *[delta[n] for n in names],
            *[new_m[n] for n in names], *[new_v[n] for n in names])
```

```python
from typing import NamedTuple

import jax
import jax.numpy as jnp
from jax import lax
from jax.experimental import pallas as pl
from jax.experimental.pallas import tpu as pltpu
from jax.experimental.pallas import tpu_sc as plsc

F32 = jnp.float32
BF16 = jnp.bfloat16

T = 2048
D = 1024
F_FF = 2816
N_HEADS = 8
QK_NOPE = 128
QK_ROPE = 64
V_HEAD = 128
Q_LORA = 384
KV_LORA = 256
CHUNK_SHIFT = 6
ROPE_THETA = 10000.0
EPS = 1e-6
NEG_INF = -1e30
HEAD_PAD = 256
KVP = KV_LORA + 128

ADAM_LR = 0.001
ADAM_B1 = 0.9
ADAM_B2 = 0.999
ADAM_EPS = 1e-08
ADAM_WD = 0.01
ADAM_STEP = 10

N_CHIPS = 4
N_DEV = 8
LANES = 128
TC = 256
V7X_VMEM_LIMIT = 56 * 1024 * 1024

MESH = pl.DeviceIdType.MESH
ANY = pl.BlockSpec(memory_space=pl.ANY)


class _W(NamedTuple):
    name: str
    kind: str
    nl: int
    k: int
    n: int


AG_GROUPS = (
    (_W("sc_w_in", "col", 1, D, 3 * D // N_CHIPS), _W("sc_conv_w", "tiny", 1, 3, D // N_CHIPS),
     _W("ffn_conv_w", "tiny", 1, 6, F_FF // N_CHIPS), _W("sc_w_out", "row", 1, D // N_CHIPS, D)),
    (_W("ffn_w_up0", "col", 1, D, 2 * F_FF // N_CHIPS),),
    (_W("ffn_w_down0", "row", 1, F_FF // N_CHIPS, D),),
    (_W("w_kv", "row", 1, D // N_CHIPS, KVP), _W("w_ukv", "col", 2, KV_LORA, N_HEADS * QK_NOPE // N_CHIPS),
     _W("w_dq", "row", 1, D // N_CHIPS, Q_LORA),
     _W("w_uq", "col", 1, Q_LORA, N_HEADS * HEAD_PAD // N_CHIPS),
     _W("w_o", "row", 1, N_HEADS * V_HEAD // N_CHIPS, D)),
    (_W("ffn_w_up1", "col", 1, D, 2 * F_FF // N_CHIPS), _W("ffn_w_down1", "row", 1, F_FF // N_CHIPS, D)),
)


def _cp(*sem):
    return pltpu.CompilerParams(dimension_semantics=sem, vmem_limit_bytes=V7X_VMEM_LIMIT)


_ORDER = [None]


def _tc_call(body, *, name, out_shape, in_specs=None, out_specs=None, grid=(), scratch_shapes=(), prefetch=0,
             input_output_aliases=None, compiler_params=None):
    def run(*args):
        specs = [pl.BlockSpec(memory_space=pltpu.VMEM)] * (len(args) - prefetch) if in_specs is None else list(in_specs)
        inner, dep = body, _ORDER[0]
        if dep is not None:
            unread = prefetch + len(specs)
            specs, args = specs + [ANY], (*args, dep)

            def inner(*refs):
                return body(*refs[:unread], *refs[unread + 1:])

        kwargs = dict(name=name, out_shape=out_shape, input_output_aliases=input_output_aliases or {},
                      compiler_params=compiler_params)
        if prefetch:
            kwargs["grid_spec"] = pltpu.PrefetchScalarGridSpec(
                num_scalar_prefetch=prefetch, grid=grid, in_specs=specs, out_specs=out_specs,
                scratch_shapes=scratch_shapes)
        else:
            kwargs.update(grid=grid, in_specs=specs, scratch_shapes=scratch_shapes)
            if out_specs is not None:
                kwargs["out_specs"] = out_specs
        out = pl.pallas_call(inner, **kwargs)(*args)
        _ORDER[0] = out[0] if isinstance(out, (list, tuple)) else out
        return out

    return run


def _tile(n, cands):
    for c in cands:
        if n % c == 0:
            return c
    raise ValueError(f"no tile for {n}")


NN_DIMS = (((1,), (0,)), ((), ()))
NT_DIMS = (((1,), (1,)), ((), ()))
TN_DIMS = (((0,), (0,)), ((), ()))
M_TILES = (1024, 512, 384, 256, 128)
N_TILES = (1408, 1024, 768, 512, 384, 256, 128)
MM_BLOCK_BYTES = 36 * 1024 * 1024


def _fit(m, n, block_bytes, m_tiles=M_TILES, n_tiles=N_TILES):
    for tm in [c for c in m_tiles if m % c == 0]:
        for tn in [c for c in n_tiles if n % c == 0]:
            if 2 * block_bytes(tm, tn) + 4 * tm * tn <= MM_BLOCK_BYTES:
                return tm, tn
    raise ValueError(f"no tiles for {m} x {n}")


def _size(x):
    return x.dtype.itemsize


def _mm(name, a, b, dims, grid, a_spec, b_spec, o_spec, o_sds, add=None, red=None, acc_shape=None):
    n_red = None if red is None else grid[red]

    def body(*refs):
        a_ref, b_ref = refs[0], refs[1]
        add_ref = refs[2] if add is not None else None
        o_ref = refs[3] if add is not None else refs[2]
        part = lax.dot_general(a_ref[...].astype(BF16), b_ref[...].astype(BF16), dims, preferred_element_type=F32)
        if red is None:
            if add is not None:
                part = part + add_ref[...]
            o_ref[...] = part.astype(o_ref.dtype)
            return
        acc_ref = refs[-1]
        r = pl.program_id(red)

        @pl.when(r == 0)
        def _():
            acc_ref[...] = part

        @pl.when(r > 0)
        def _():
            acc_ref[...] += part

        @pl.when(r == n_red - 1)
        def _():
            o_ref[...] = acc_ref[...].astype(o_ref.dtype)

    sem = tuple("arbitrary" if ax == red else "parallel" for ax in range(len(grid)))
    in_specs = [a_spec, b_spec] + ([o_spec] if add is not None else [])
    args = (a, b) + ((add,) if add is not None else ())
    return _tc_call(
        body, name=name, grid=grid, in_specs=in_specs, out_specs=o_spec, out_shape=o_sds,
        scratch_shapes=[] if red is None else [pltpu.VMEM(acc_shape, F32)], compiler_params=_cp(*sem),
    )(*args)


def _nn(name, a, b, out_dtype, add=None, lead=None):
    (m, k), n = a.shape, b.shape[-1]
    osz = jnp.dtype(out_dtype).itemsize + (4 if add is not None else 0)
    tm, tn = _fit(m, n, lambda tm, tn: tm * k * _size(a) + k * tn * _size(b) + tm * tn * osz)
    if lead is None:
        b_spec = pl.BlockSpec((k, tn), lambda i, j: (0, j))
    else:
        b_spec = pl.BlockSpec((None, k, tn), lambda i, j: (lead, 0, j))
    return _mm(name, a, b, NN_DIMS, (m // tm, n // tn), pl.BlockSpec((tm, k), lambda i, j: (i, 0)), b_spec,
               pl.BlockSpec((tm, tn), lambda i, j: (i, j)), jax.ShapeDtypeStruct((m, n), out_dtype), add=add)


def _nn_parts(name, a, b, parts, out_dtype, lead=None, stacked=False):
    m, k = a.shape
    c = b.shape[-1] if stacked else b.shape[-1] // parts
    osz = jnp.dtype(out_dtype).itemsize
    tm, tn = _fit(m, c, lambda tm, tn: tm * k * _size(a) + k * tn * _size(b) + tm * tn * osz)
    nb = c // tn
    if stacked:
        b_spec = pl.BlockSpec((None, k, tn), lambda i, p, j: (p, 0, j))
    elif lead is None:
        b_spec = pl.BlockSpec((k, tn), lambda i, p, j: (0, p * nb + j))
    else:
        b_spec = pl.BlockSpec((None, k, tn), lambda i, p, j: (lead, 0, p * nb + j))
    return _mm(name, a, b, NN_DIMS, (m // tm, parts, nb), pl.BlockSpec((tm, k), lambda i, p, j: (i, 0)), b_spec,
               pl.BlockSpec((None, tm, tn), lambda i, p, j: (p, i, j)), jax.ShapeDtypeStruct((parts, m, c), out_dtype))


def _nt(name, a, b, out_dtype, lead=None):
    (m, k), n = a.shape, b.shape[-2]
    osz = jnp.dtype(out_dtype).itemsize
    tm, tn = _fit(m, n, lambda tm, tn: tm * k * _size(a) + tn * k * _size(b) + tm * tn * osz)
    if lead is None:
        b_spec = pl.BlockSpec((tn, k), lambda i, j: (j, 0))
    else:
        b_spec = pl.BlockSpec((None, tn, k), lambda i, j: (lead, j, 0))
    return _mm(name, a, b, NT_DIMS, (m // tm, n // tn), pl.BlockSpec((tm, k), lambda i, j: (i, 0)), b_spec,
               pl.BlockSpec((tm, tn), lambda i, j: (i, j)), jax.ShapeDtypeStruct((m, n), out_dtype))


def _tn(name, a, b, out_dtype):
    (k, m), n = a.shape, b.shape[1]
    osz = jnp.dtype(out_dtype).itemsize
    tm, tn = _fit(m, n, lambda tm, tn: k * tm * _size(a) + k * tn * _size(b) + tm * tn * osz,
                  m_tiles=(512, 384, 256, 128), n_tiles=(n,) + N_TILES)
    return _mm(name, a, b, TN_DIMS, (m // tm, n // tn), pl.BlockSpec((k, tm), lambda i, j: (0, i)),
               pl.BlockSpec((k, tn), lambda i, j: (0, j)), pl.BlockSpec((tm, tn), lambda i, j: (i, j)),
               jax.ShapeDtypeStruct((m, n), out_dtype))


def _nn_add_norm(name, a, b, add, g):
    (m, k), n = a.shape, b.shape[1]
    tm = 512

    def body(a_ref, b_ref, add_ref, g_ref, h_ref, hn_ref):
        h = jnp.dot(a_ref[...], b_ref[...], preferred_element_type=F32) + add_ref[...]
        h_ref[...] = h
        hn_ref[...] = _rms_rows(h, g_ref[...]).astype(BF16)

    rows = lambda w: pl.BlockSpec((tm, w), lambda i: (i, 0))
    return _tc_call(
        body, name=name, grid=(m // tm,),
        in_specs=[rows(k), pl.BlockSpec((k, n), lambda i: (0, 0)), rows(n), pl.BlockSpec((1, n), lambda i: (0, 0))],
        out_specs=[rows(n), rows(n)],
        out_shape=[jax.ShapeDtypeStruct((m, n), F32), jax.ShapeDtypeStruct((m, n), BF16)], compiler_params=_cp("parallel"),
    )(a, b, add, g)


def _nn_add_loss(name, a, b, add, g, tgt):
    (m, k), n = a.shape, b.shape[1]
    tm = 512

    def body(a_ref, b_ref, add_ref, g_ref, t_ref, loss_ref, dh_ref, dhb_ref, dg_ref):
        xv = jnp.dot(a_ref[...], b_ref[...], preferred_element_type=F32) + add_ref[...]
        gv = g_ref[...]
        r = lax.rsqrt(jnp.mean(xv * xv, axis=1, keepdims=True) + EPS)
        err = xv * r * gv - t_ref[...]
        part = 0.5 * jnp.sum(jnp.mean(err * err, axis=1, keepdims=True), axis=0, keepdims=True)
        dx, dg = _rms_bwd_math(xv, gv, err * (1.0 / n))
        dh_ref[...] = dx
        dhb_ref[...] = dx.astype(BF16)

        @pl.when(pl.program_id(0) == 0)
        def _():
            dg_ref[...] = jnp.zeros_like(dg_ref)
            loss_ref[...] = jnp.zeros_like(loss_ref)

        dg_ref[...] += dg
        loss_ref[...] += jnp.broadcast_to(part, loss_ref.shape)

    rows = lambda w: pl.BlockSpec((tm, w), lambda i: (i, 0))
    vec = pl.BlockSpec((1, n), lambda i: (0, 0))
    return _tc_call(
        body, name=name, grid=(m // tm,),
        in_specs=[rows(k), pl.BlockSpec((k, n), lambda i: (0, 0)), rows(n), vec, rows(n)],
        out_specs=[pl.BlockSpec((1, LANES), lambda i: (0, 0)), rows(n), rows(n), vec],
        out_shape=[jax.ShapeDtypeStruct((1, LANES), F32), jax.ShapeDtypeStruct((m, n), F32),
                   jax.ShapeDtypeStruct((m, n), BF16), jax.ShapeDtypeStruct((1, n), F32)],
        compiler_params=_cp("arbitrary"),
    )(a, b, add, g, tgt)


def _dx_norm_bwd(name, a, b, x, g, add):
    parts, t, c = a.shape
    d = b.shape[0]
    tm = 256

    def body(a_ref, b_ref, x_ref, g_ref, add_ref, dx_ref, dxb_ref, dg_ref):
        dy = None
        for p in range(parts):
            part = lax.dot_general(a_ref[p], b_ref[:, p * c:(p + 1) * c], NT_DIMS, preferred_element_type=F32)
            dy = part if dy is None else dy + part
        dx, dg = _rms_bwd_math(x_ref[...], g_ref[...], dy)
        dx = dx + add_ref[...]
        dx_ref[...] = dx
        dxb_ref[...] = dx.astype(BF16)

        @pl.when(pl.program_id(0) == 0)
        def _():
            dg_ref[...] = jnp.zeros_like(dg_ref)

        dg_ref[...] += dg

    rows = pl.BlockSpec((tm, d), lambda i: (i, 0))
    vec = pl.BlockSpec((1, d), lambda i: (0, 0))
    return _tc_call(
        body, name=name, grid=(t // tm,),
        in_specs=[pl.BlockSpec((parts, tm, c), lambda i: (0, i, 0)), pl.BlockSpec(b.shape, lambda i: (0, 0)), rows, vec,
                  rows],
        out_specs=[rows, rows, vec],
        out_shape=[jax.ShapeDtypeStruct((t, d), F32), jax.ShapeDtypeStruct((t, d), BF16),
                   jax.ShapeDtypeStruct((1, d), F32)],
        compiler_params=_cp("arbitrary"),
    )(a, b, x, g, add)


def _dw_sc_in(hn, dz):
    t, tn, tm = hn.shape[0], TC, D
    per_part, per_chip = D // tn, 3 * D // N_CHIPS // tn
    return _mm("sc_in_dw", hn, dz, TN_DIMS, (D // tm, 3 * D // tn), pl.BlockSpec((t, tm), lambda i, j: (0, i)),
               pl.BlockSpec((None, t, tn), lambda i, j: (j // per_part, 0, j % per_part)),
               pl.BlockSpec((None, tm, tn), lambda i, j: (j // per_chip, i, j % per_chip)),
               jax.ShapeDtypeStruct((N_CHIPS, D, 3 * D // N_CHIPS), BF16))


def _dw_ffn_up(name, hf, dup):
    t, tm, ns = hf.shape[0], D, 2 * F_FF // N_CHIPS
    return _mm(name, hf, dup, TN_DIMS, (N_CHIPS, D // tm), pl.BlockSpec((t, tm), lambda s, i: (0, i)),
               pl.BlockSpec((None, t, ns), lambda s, i: (s // 2, 0, s % 2)),
               pl.BlockSpec((None, tm, ns), lambda s, i: (s, i, 0)), jax.ShapeDtypeStruct((N_CHIPS, D, ns), BF16))


def _rms_fwd(x, g, name):
    t, d = x.shape
    tr = 512

    def body(x_ref, g_ref, o_ref):
        xv = x_ref[...]
        r = lax.rsqrt(jnp.mean(xv * xv, axis=1, keepdims=True) + EPS)
        o_ref[...] = (xv * r * g_ref[...]).astype(o_ref.dtype)

    row = pl.BlockSpec((tr, d), lambda i: (i, 0))
    return _tc_call(
        body, name=name, grid=(t // tr,), in_specs=[row, pl.BlockSpec((1, d), lambda i: (0, 0))],
        out_specs=row, out_shape=jax.ShapeDtypeStruct((t, d), BF16), compiler_params=_cp("parallel"),
    )(x, g)


def _rms_bwd_math(xv, g, dy):
    r = lax.rsqrt(jnp.mean(xv * xv, axis=1, keepdims=True) + EPS)
    xh = xv * r
    gy = dy * g
    dx = r * (gy - xh * jnp.mean(gy * xh, axis=1, keepdims=True))
    dg = jnp.sum(dy * xh, axis=0, keepdims=True)
    return dx, dg


def _rot_half(x):
    lane = lax.broadcasted_iota(jnp.int32, x.shape, 1)
    return jnp.where((lane % QK_ROPE) < QK_ROPE // 2, -pltpu.roll(x, LANES - 32, axis=1),
                     pltpu.roll(x, 32, axis=1))


def _rope_fwd_math(x, cos, sin):
    return x * cos + _rot_half(x) * sin


def _rope_bwd_math(dy, cos, sin):
    return dy * cos - _rot_half(dy * sin)


def _rms_rows(x, g):
    return x * lax.rsqrt(jnp.mean(x * x, axis=1, keepdims=True) + EPS) * g


def _attn_prep(h, g_attn, g_kvin, w_dq, g_ql, w_uq, w_kv, g_kvl, w_ukv, cos, sin):
    t, d = h.shape
    tr = 256
    wq = N_HEADS * HEAD_PAD

    def body(h_ref, ga_ref, gk_ref, wdq_ref, gq_ref, wuq_ref, wkv_ref, gl_ref, wukv_ref, c_ref, s_ref,
             hn_ref, hk_ref, cqp_ref, cq_ref, q_ref, kvp_ref, ckv_ref, kr_ref, knv_ref):
        xv, cv, sv = h_ref[...], c_ref[...], s_ref[...]
        xh = xv * lax.rsqrt(jnp.mean(xv * xv, axis=1, keepdims=True) + EPS)
        hn = (xh * ga_ref[...]).astype(BF16)
        hk = (xh * gk_ref[...]).astype(BF16)
        hn_ref[...], hk_ref[...] = hn, hk
        cq_pre = jnp.dot(hn, wdq_ref[...], preferred_element_type=F32)
        cqp_ref[...] = cq_pre
        cq = _rms_rows(cq_pre, gq_ref[...]).astype(BF16)
        cq_ref[...] = cq
        for hd in range(N_HEADS):
            lo = hd * HEAD_PAD
            qh = jnp.dot(cq, wuq_ref[:, lo:lo + HEAD_PAD], preferred_element_type=F32)
            q_ref[:, lo:lo + QK_NOPE] = qh[:, :QK_NOPE].astype(BF16)
            q_ref[:, lo + QK_NOPE:lo + HEAD_PAD] = _rope_fwd_math(qh[:, QK_NOPE:], cv, sv).astype(BF16)
        kvpre = jnp.dot(hk, wkv_ref[...], preferred_element_type=F32)
        kvp_ref[...] = kvpre
        ckv = _rms_rows(kvpre[:, :KV_LORA], gl_ref[...]).astype(BF16)
        ckv_ref[...] = ckv
        kr_ref[...] = _rope_fwd_math(kvpre[:, KV_LORA:], cv, sv).astype(BF16)
        for p in range(2):
            knv_ref[p] = jnp.dot(ckv, wukv_ref[p], preferred_element_type=F32).astype(BF16)

    rows = lambda w: pl.BlockSpec((tr, w), lambda i: (i, 0))
    whole = lambda a: pl.BlockSpec(a.shape, lambda i: (0,) * a.ndim)
    sds = lambda w, dt: jax.ShapeDtypeStruct((t, w), dt)
    args = (h, g_attn, g_kvin, w_dq, g_ql, w_uq, w_kv, g_kvl, w_ukv, cos, sin)
    return _tc_call(
        body, name="attn_prep", grid=(t // tr,),
        in_specs=[rows(d)] + [whole(a) for a in args[1:9]] + [rows(LANES), rows(LANES)],
        out_specs=[rows(d), rows(d), rows(Q_LORA), rows(Q_LORA), rows(wq), rows(KVP), rows(KV_LORA), rows(LANES),
                   pl.BlockSpec((2, tr, N_HEADS * QK_NOPE), lambda i: (0, i, 0))],
        out_shape=[sds(d, BF16), sds(d, BF16), sds(Q_LORA, F32), sds(Q_LORA, BF16), sds(wq, BF16), sds(KVP, F32),
                   sds(KV_LORA, BF16), sds(LANES, BF16), jax.ShapeDtypeStruct((2, t, N_HEADS * QK_NOPE), BF16)],
        compiler_params=_cp("parallel"),
    )(*args)


def _attn_prep_bwd(dq, dknv, dkr, dh, h, hn, hk, cq_pre, cq, kvpre, ckv, g_attn, g_kvin, w_dq, g_ql, w_uq, w_kv, g_kvl,
                   w_ukv, cos, sin):
    t, d = h.shape
    tr = 256
    n_steps = t // tr
    wq = N_HEADS * HEAD_PAD
    wk = N_HEADS * QK_NOPE

    def body(dq_ref, dknv_ref, dkr_ref, dh_ref, h_ref, hn_ref, hk_ref, cqp_ref, cq_ref, kvp_ref, ckv_ref,
             ga_ref, gk_ref, wdq_ref, gq_ref, wuq_ref, wkv_ref, gl_ref, wukv_ref, c_ref, s_ref,
             dho_ref, dhb_ref, dwuq_ref, dwdq_ref, dwukv_ref, dwkv_ref, dga_ref, dgk_ref, dgq_ref, dgl_ref,
             a_uq, a_dq, a_ukv, a_kv):
        i = pl.program_id(0)

        @pl.when(i == 0)
        def _():
            for ref in (a_uq, a_dq, a_ukv, a_kv, dga_ref, dgk_ref, dgq_ref, dgl_ref):
                ref[...] = jnp.zeros_like(ref)

        dqv = dq_ref[...]
        dcq = lax.dot_general(dqv, wuq_ref[...], NT_DIMS, preferred_element_type=F32)
        a_uq[...] += lax.dot_general(cq_ref[...], dqv, TN_DIMS, preferred_element_type=F32)
        dcq_pre, dg = _rms_bwd_math(cqp_ref[...], gq_ref[...], dcq)
        dgq_ref[...] += dg
        dcq_pre = dcq_pre.astype(BF16)
        dhn = lax.dot_general(dcq_pre, wdq_ref[...], NT_DIMS, preferred_element_type=F32)
        a_dq[...] += lax.dot_general(hn_ref[...], dcq_pre, TN_DIMS, preferred_element_type=F32)
        dckv = None
        for p in range(2):
            dk = dknv_ref[p].astype(BF16)
            part = lax.dot_general(dk, wukv_ref[p], NT_DIMS, preferred_element_type=F32)
            dckv = part if dckv is None else dckv + part
            a_ukv[p] += lax.dot_general(ckv_ref[...], dk, TN_DIMS, preferred_element_type=F32)
        dlat, dg = _rms_bwd_math(kvp_ref[:, :KV_LORA], gl_ref[...], dckv)
        dgl_ref[...] += dg
        dkr_pre = _rope_bwd_math(dkr_ref[...], c_ref[...], s_ref[...])
        dkvpre = jnp.concatenate([dlat, dkr_pre], axis=1).astype(BF16)
        dhk = lax.dot_general(dkvpre, wkv_ref[...], NT_DIMS, preferred_element_type=F32)
        a_kv[...] += lax.dot_general(hk_ref[...], dkvpre, TN_DIMS, preferred_element_type=F32)
        xv = h_ref[...]
        dx1, dg = _rms_bwd_math(xv, ga_ref[...], dhn)
        dga_ref[...] += dg
        dx2, dg = _rms_bwd_math(xv, gk_ref[...], dhk)
        dgk_ref[...] += dg
        dh_new = dh_ref[...] + dx1 + dx2
        dho_ref[...] = dh_new
        dhb_ref[...] = dh_new.astype(BF16)

        @pl.when(i == n_steps - 1)
        def _():
            dwuq_ref[...] = a_uq[...].astype(BF16)
            dwdq_ref[...] = a_dq[...].astype(BF16)
            dwukv_ref[...] = a_ukv[...].astype(BF16)
            dwkv_ref[...] = a_kv[...].astype(BF16)

    rows = lambda w: pl.BlockSpec((tr, w), lambda i: (i, 0))
    whole = lambda shape: pl.BlockSpec(shape, lambda i: (0,) * len(shape))
    weights = (g_attn, g_kvin, w_dq, g_ql, w_uq, w_kv, g_kvl, w_ukv)
    dw_shapes = [(Q_LORA, wq), (d, Q_LORA), (2, KV_LORA, wk), (d, KVP)]
    dg_shapes = [(1, d), (1, d), (1, Q_LORA), (1, KV_LORA)]
    return _tc_call(
        body, name="attn_prep_bwd", grid=(n_steps,),
        in_specs=[rows(wq), pl.BlockSpec((2, tr, wk), lambda i: (0, i, 0)), rows(LANES), rows(d), rows(d), rows(d),
                  rows(d), rows(Q_LORA), rows(Q_LORA), rows(KVP), rows(KV_LORA)]
        + [whole(a.shape) for a in weights] + [rows(LANES), rows(LANES)],
        out_specs=[rows(d), rows(d)] + [whole(s) for s in dw_shapes + dg_shapes],
        out_shape=[jax.ShapeDtypeStruct((t, d), F32), jax.ShapeDtypeStruct((t, d), BF16)]
        + [jax.ShapeDtypeStruct(s, BF16) for s in dw_shapes] + [jax.ShapeDtypeStruct(s, F32) for s in dg_shapes],
        scratch_shapes=[pltpu.VMEM(s, F32) for s in dw_shapes], compiler_params=_cp("arbitrary"),
    )(dq, dknv, dkr, dh, h, hn, hk, cq_pre, cq, kvpre, ckv, *weights, cos, sin)


ROW_CHUNK = 64
HALO = 16
WIN = ROW_CHUNK + 16
LANE_HALVES = (slice(0, LANES), slice(LANES, TC))


def _stage(s_ref, p, src):
    t = src.shape[0]
    s_ref[p, :HALO] = jnp.zeros((HALO, TC), BF16)
    s_ref[p, HALO:HALO + t] = src
    s_ref[p, HALO + t:] = jnp.zeros((HALO, TC), BF16)


def _window(s_ref, p, i, lanes):
    base = pl.multiple_of(i * ROW_CHUNK, ROW_CHUNK)
    return s_ref[p, pl.ds(base, ROW_CHUNK + 2 * HALO), lanes].astype(F32)[8:8 + WIN]


def _valid(x):
    return x[8:8 + ROW_CHUNK]


def _prev(x, k):
    return pltpu.roll(x, k, axis=0)


def _next(x, k):
    return pltpu.roll(x, WIN - k, axis=0)


def _taps(w_ref, lanes):
    return w_ref[0:1, lanes], w_ref[1:2, lanes], w_ref[2:3, lanes]


def _fold8(x):
    return jnp.sum(x.reshape(ROW_CHUNK // 8, 8, x.shape[-1]), axis=0)


def _store_rows(ref, idx, i, lanes, x):
    rows = pl.ds(pl.multiple_of(i * ROW_CHUNK, ROW_CHUNK), ROW_CHUNK)
    ref[(*idx, rows, lanes)] = x.astype(ref.dtype)


def _for_chunks(t, chunk):
    def step(i, carry):
        for lanes in LANE_HALVES:
            chunk(i, lanes)
        return carry

    lax.fori_loop(0, t // ROW_CHUNK, step, 0)


def _write_col_sums(acc_ref, outs):
    for k, (ref, row) in enumerate(outs):
        ref[row:row + 1, :] = jnp.sum(acc_ref[k], axis=0, keepdims=True)


def _shift_down(x, k):
    row = lax.broadcasted_iota(jnp.int32, x.shape, 0)
    return jnp.where(row >= k, pltpu.roll(x, k, axis=0), 0.0)


def _shift_up(x, k):
    n = x.shape[0]
    row = lax.broadcasted_iota(jnp.int32, x.shape, 0)
    return jnp.where(row < n - k, pltpu.roll(x, n - k, axis=0), 0.0)


def _conv3(x, w_ref):
    return _shift_down(x, 2) * w_ref[0:1, :] + _shift_down(x, 1) * w_ref[1:2, :] + x * w_ref[2:3, :]


def _col(parts, t):
    if parts is None:
        return pl.BlockSpec((t, TC), lambda j: (0, j))
    return pl.BlockSpec((parts, t, TC), lambda j: (0, 0, j))


def _staging(parts, t):
    return pltpu.VMEM((parts, t + 2 * HALO, TC), BF16)


def _scmix_fwd(z, w):
    t = z.shape[1]

    def body(z_ref, w_ref, m_ref):
        b, c, u = (z_ref[p].astype(F32) for p in range(3))
        m_ref[...] = (b * _conv3(c * u, w_ref)).astype(BF16)

    return _tc_call(
        body, name="scmix_fwd", grid=(D // TC,), in_specs=[_col(3, t), pl.BlockSpec((3, TC), lambda j: (0, j))],
        out_specs=_col(None, t), out_shape=jax.ShapeDtypeStruct((t, D), BF16), compiler_params=_cp("parallel"),
    )(z, w)


def _scmix_bwd(z, w, dm):
    t = z.shape[1]

    def body(z_ref, w_ref, dm_ref, dz_ref, dw_ref, s_ref, acc_ref):
        for p in range(3):
            _stage(s_ref, p, z_ref[p])
        _stage(s_ref, 3, dm_ref[...])
        acc_ref[...] = jnp.zeros_like(acc_ref)

        def chunk(i, lanes):
            w0, w1, w2 = _taps(w_ref, lanes)
            b, c, u, dm = (_window(s_ref, p, i, lanes) for p in range(4))
            cu = c * u
            cu1, cu2 = _prev(cu, 1), _prev(cu, 2)
            _store_rows(dz_ref, (0,), i, lanes, _valid(dm * (cu2 * w0 + cu1 * w1 + cu * w2)))
            dcv = dm * b
            dcu = dcv * w2 + _next(dcv, 1) * w1 + _next(dcv, 2) * w0
            _store_rows(dz_ref, (1,), i, lanes, _valid(dcu * u))
            _store_rows(dz_ref, (2,), i, lanes, _valid(dcu * c))
            for k, shifted in enumerate((cu2, cu1, cu)):
                acc_ref[k, :, lanes] += _fold8(_valid(dcv * shifted))

        _for_chunks(t, chunk)
        _write_col_sums(acc_ref, [(dw_ref, 0), (dw_ref, 1), (dw_ref, 2)])

    wspec = pl.BlockSpec((3, TC), lambda j: (0, j))
    return _tc_call(
        body, name="scmix_bwd", grid=(D // TC,), in_specs=[_col(3, t), wspec, _col(None, t)],
        out_specs=[_col(3, t), wspec],
        out_shape=[jax.ShapeDtypeStruct((3, t, D), BF16), jax.ShapeDtypeStruct((3, D), F32)],
        scratch_shapes=[_staging(4, t), pltpu.VMEM((3, 8, TC), F32)], compiler_params=_cp("parallel"),
    )(z, w, dm)


def _ffn_up_gate(hf, w_up, w, bias, name):
    t, d = hf.shape
    nb = F_FF // TC

    def body(hf_ref, wg_ref, wv_ref, w_ref, b_ref, up_ref, a_ref, prev_ref):
        @pl.when(pl.program_id(0) == 0)
        def _():
            prev_ref[...] = jnp.zeros_like(prev_ref)

        gc = _conv3(prev_ref[0].astype(F32), w_ref) + b_ref[...]
        a_ref[...] = (gc * jax.nn.sigmoid(gc) * prev_ref[1].astype(F32)).astype(BF16)
        hv = hf_ref[...]
        up_ref[0] = jnp.dot(hv, wg_ref[...], preferred_element_type=F32).astype(BF16)
        up_ref[1] = jnp.dot(hv, wv_ref[...], preferred_element_type=F32).astype(BF16)
        prev_ref[...] = up_ref[...]

    tile = lambda j: jnp.minimum(j, nb - 1)
    gated = lambda j: jnp.maximum(j - 1, 0)
    return _tc_call(
        body, name=name, grid=(nb + 1,),
        in_specs=[pl.BlockSpec((t, d), lambda j: (0, 0)), pl.BlockSpec((d, TC), lambda j: (0, tile(j))),
                  pl.BlockSpec((d, TC), lambda j: (0, nb + tile(j))), pl.BlockSpec((3, TC), lambda j: (0, gated(j))),
                  pl.BlockSpec((1, TC), lambda j: (0, gated(j)))],
        out_specs=[pl.BlockSpec((2, t, TC), lambda j: (0, 0, tile(j))), pl.BlockSpec((t, TC), lambda j: (0, gated(j)))],
        out_shape=[jax.ShapeDtypeStruct((2, t, F_FF), BF16), jax.ShapeDtypeStruct((t, F_FF), BF16)],
        scratch_shapes=[pltpu.VMEM((2, t, TC), BF16)], compiler_params=_cp("arbitrary"),
    )(hf, w_up, w_up, w, bias)


def _gate_bwd(up, w, bias, dh, w_down, name):
    t, d = dh.shape

    def body(u_ref, w_ref, b_ref, dh_ref, wd_ref, du_ref, dw_ref, db_ref, s_ref, acc_ref):
        for p in range(2):
            _stage(s_ref, p, u_ref[p])
        _stage(s_ref, 2, lax.dot_general(dh_ref[...], wd_ref[...], NT_DIMS, preferred_element_type=F32).astype(BF16))
        acc_ref[...] = jnp.zeros_like(acc_ref)

        def chunk(i, lanes):
            w0, w1, w2 = _taps(w_ref, lanes)
            g, v, da = (_window(s_ref, p, i, lanes) for p in range(3))
            g1, g2 = _prev(g, 1), _prev(g, 2)
            gc = g2 * w0 + g1 * w1 + g * w2 + b_ref[:, lanes]
            sg = jax.nn.sigmoid(gc)
            _store_rows(du_ref, (1,), i, lanes, _valid(da * (gc * sg)))
            dgc = da * v * (sg * (1.0 + gc * (1.0 - sg)))
            _store_rows(du_ref, (0,), i, lanes, _valid(dgc * w2 + _next(dgc, 1) * w1 + _next(dgc, 2) * w0))
            for k, shifted in enumerate((g2, g1, g)):
                acc_ref[k, :, lanes] += _fold8(_valid(dgc * shifted))
            acc_ref[3, :, lanes] += _fold8(_valid(dgc))

        _for_chunks(t, chunk)
        _write_col_sums(acc_ref, [(dw_ref, 0), (dw_ref, 1), (dw_ref, 2), (db_ref, 0)])

    wspec = pl.BlockSpec((3, TC), lambda j: (0, j))
    bspec = pl.BlockSpec((1, TC), lambda j: (0, j))
    return _tc_call(
        body, name=name, grid=(F_FF // TC,),
        in_specs=[_col(2, t), wspec, bspec, pl.BlockSpec((t, d), lambda j: (0, 0)), pl.BlockSpec((TC, d), lambda j: (j, 0))],
        out_specs=[_col(2, t), wspec, bspec],
        out_shape=[jax.ShapeDtypeStruct((2, t, F_FF), BF16), jax.ShapeDtypeStruct((3, F_FF), F32),
                   jax.ShapeDtypeStruct((1, F_FF), F32)],
        scratch_shapes=[_staging(3, t), pltpu.VMEM((4, 8, TC), F32)], compiler_params=_cp("parallel"),
    )(up, w, bias, dh, w_down)


ATT_TQ = 256
ATT_SCALE = (QK_NOPE + QK_ROPE) ** -0.5


def _key_ranges(lvl):
    lo = lvl * ATT_TQ
    return ([(0, lo, False)] if lvl else []) + [(lo, lo + ATT_TQ, True)]


FWD_HEADS = 4
BWD_HEADS = 2


def _fill_keys(k_ref, kn_ref, kr_ref):
    @pl.when(pl.program_id(1) == 0)
    def _():
        for hh in range(k_ref.shape[0]):
            k_ref[hh, :, :QK_NOPE] = kn_ref[:, hh * QK_NOPE:(hh + 1) * QK_NOPE]
            k_ref[hh, :, QK_NOPE:] = kr_ref[...]


def _attn_probs(q, k_ref, lvl):
    scores = []
    for lo, hi, diagonal in _key_ranges(lvl):
        s = lax.dot_general(q, k_ref[lo:hi, :], NT_DIMS, preferred_element_type=F32) * ATT_SCALE
        if diagonal:
            row = lax.broadcasted_iota(jnp.int32, s.shape, 0)
            col = lax.broadcasted_iota(jnp.int32, s.shape, 1)
            seen = lax.shift_right_logical(col, CHUNK_SHIFT) <= lax.shift_right_logical(row, CHUNK_SHIFT)
            s = jnp.where(seen, s, NEG_INF)
        scores.append(s)
    m = jnp.max(scores[0], axis=1, keepdims=True)
    for s in scores[1:]:
        m = jnp.maximum(m, jnp.max(s, axis=1, keepdims=True))
    ps = [jnp.exp(s - m) for s in scores]
    total = jnp.sum(ps[0], axis=1, keepdims=True)
    for p in ps[1:]:
        total = total + jnp.sum(p, axis=1, keepdims=True)
    inv = 1.0 / total
    return [p * inv for p in ps]


def _attn_probs_t(q, k_ref, lvl):
    scores = []
    for lo, hi, diagonal in _key_ranges(lvl):
        s = lax.dot_general(k_ref[lo:hi, :], q, NT_DIMS, preferred_element_type=F32) * ATT_SCALE
        if diagonal:
            key = lax.broadcasted_iota(jnp.int32, s.shape, 0)
            qry = lax.broadcasted_iota(jnp.int32, s.shape, 1)
            seen = lax.shift_right_logical(key, CHUNK_SHIFT) <= lax.shift_right_logical(qry, CHUNK_SHIFT)
            s = jnp.where(seen, s, NEG_INF)
        scores.append(s)
    m = jnp.max(scores[0], axis=0, keepdims=True)
    for s in scores[1:]:
        m = jnp.maximum(m, jnp.max(s, axis=0, keepdims=True))
    ps = [jnp.exp(s - m) for s in scores]
    total = jnp.sum(ps[0], axis=0, keepdims=True)
    for p in ps[1:]:
        total = total + jnp.sum(p, axis=0, keepdims=True)
    inv = 1.0 / total
    return [p * inv for p in ps]


def _per_query_block(qi, n_blocks, branch):
    for lvl in range(n_blocks):
        pl.when(qi == lvl)(lambda lvl=lvl: branch(lvl))


def _attn_specs(t, g):
    q = pl.BlockSpec((ATT_TQ, g * HEAD_PAD), lambda h, i: (i, h))
    kn = pl.BlockSpec((None, t, g * QK_NOPE), lambda h, i: (0, 0, h))
    kr = pl.BlockSpec((t, LANES), lambda h, i: (0, 0))
    v = pl.BlockSpec((None, t, g * V_HEAD), lambda h, i: (1, 0, h))
    o = pl.BlockSpec((ATT_TQ, g * V_HEAD), lambda h, i: (i, h))
    return q, kn, kr, v, o


def _attn_fwd(q, knv, kr):
    t = q.shape[0]

    def body(q_ref, kn_ref, kr_ref, v_ref, o_ref, k_ref):
        _fill_keys(k_ref, kn_ref, kr_ref)

        def branch(lvl):
            for hh in range(FWD_HEADS):
                vcols = slice(hh * V_HEAD, (hh + 1) * V_HEAD)
                ps = _attn_probs(q_ref[:, hh * HEAD_PAD:(hh + 1) * HEAD_PAD], k_ref.at[hh], lvl)
                o = None
                for p, (lo, hi, _) in zip(ps, _key_ranges(lvl)):
                    part = jnp.dot(p.astype(BF16), v_ref[lo:hi, vcols], preferred_element_type=F32)
                    o = part if o is None else o + part
                o_ref[:, vcols] = o.astype(BF16)

        _per_query_block(pl.program_id(1), t // ATT_TQ, branch)

    qs, kns, krs, vs, os_ = _attn_specs(t, FWD_HEADS)
    return _tc_call(
        body, name="attn_fwd", grid=(N_HEADS // FWD_HEADS, t // ATT_TQ), in_specs=[qs, kns, krs, vs],
        out_specs=os_, out_shape=jax.ShapeDtypeStruct((t, N_HEADS * V_HEAD), BF16),
        scratch_shapes=[pltpu.VMEM((FWD_HEADS, t, HEAD_PAD), BF16)], compiler_params=_cp("parallel", "arbitrary"),
    )(q, knv, kr, knv)


def _attn_bwd(q, knv, kr, do, cos, sin):
    t = q.shape[0]

    def body(q_ref, kn_ref, kr_ref, v_ref, do_ref, c_ref, s_ref, dq_ref, dknv_ref, dkr_ref, k_ref, dk_ref):
        h, qi = pl.program_id(0), pl.program_id(1)
        _fill_keys(k_ref, kn_ref, kr_ref)

        @pl.when(qi == 0)
        def _():
            dknv_ref[1] = jnp.zeros(dknv_ref.shape[1:], F32)
            dk_ref[...] = jnp.zeros_like(dk_ref)

        @pl.when((qi == 0) & (h == 0))
        def _():
            dkr_ref[...] = jnp.zeros_like(dkr_ref)

        def branch(lvl):
            ranges = _key_ranges(lvl)
            for hh in range(BWD_HEADS):
                qcols = slice(hh * HEAD_PAD, (hh + 1) * HEAD_PAD)
                vcols = slice(hh * V_HEAD, (hh + 1) * V_HEAD)
                qv, dov = q_ref[:, qcols], do_ref[:, vcols]
                ps = _attn_probs_t(qv, k_ref.at[hh], lvl)
                dps = [lax.dot_general(v_ref[lo:hi, vcols], dov, NT_DIMS, preferred_element_type=F32)
                       for lo, hi, _ in ranges]
                di = None
                for p, dp in zip(ps, dps):
                    part = jnp.sum(p * dp, axis=0, keepdims=True)
                    di = part if di is None else di + part
                dq = None
                for p, dp, (lo, hi, _) in zip(ps, dps, ranges):
                    ds = (p * (dp - di) * ATT_SCALE).astype(BF16)
                    part = lax.dot_general(ds, k_ref[hh, lo:hi, :], TN_DIMS, preferred_element_type=F32)
                    dq = part if dq is None else dq + part
                    dk_ref[hh, lo:hi, :] += jnp.dot(ds, qv, preferred_element_type=F32)
                    dknv_ref[1, lo:hi, vcols] += jnp.dot(p.astype(BF16), dov, preferred_element_type=F32)
                dq_ref[:, hh * HEAD_PAD:hh * HEAD_PAD + QK_NOPE] = dq[:, :QK_NOPE].astype(BF16)
                dq_ref[:, hh * HEAD_PAD + QK_NOPE:(hh + 1) * HEAD_PAD] = _rope_bwd_math(
                    dq[:, QK_NOPE:], c_ref[...], s_ref[...]).astype(BF16)

        _per_query_block(qi, t // ATT_TQ, branch)

        @pl.when(qi == t // ATT_TQ - 1)
        def _():
            for hh in range(BWD_HEADS):
                dknv_ref[0, :, hh * QK_NOPE:(hh + 1) * QK_NOPE] = dk_ref[hh, :, :QK_NOPE]
                dkr_ref[...] += dk_ref[hh, :, QK_NOPE:]

    qs, kns, krs, vs, os_ = _attn_specs(t, BWD_HEADS)
    tab = pl.BlockSpec((ATT_TQ, LANES), lambda h, i: (i, 0))
    return _tc_call(
        body, name="attn_bwd", grid=(N_HEADS // BWD_HEADS, t // ATT_TQ), in_specs=[qs, kns, krs, vs, os_, tab, tab],
        out_specs=[qs, pl.BlockSpec((2, t, BWD_HEADS * QK_NOPE), lambda h, i: (0, 0, h)), krs],
        out_shape=[jax.ShapeDtypeStruct((t, N_HEADS * HEAD_PAD), BF16),
                   jax.ShapeDtypeStruct((2, t, N_HEADS * QK_NOPE), F32), jax.ShapeDtypeStruct((t, LANES), F32)],
        scratch_shapes=[pltpu.VMEM((BWD_HEADS, t, HEAD_PAD), BF16), pltpu.VMEM((BWD_HEADS, t, HEAD_PAD), F32)],
        compiler_params=_cp("arbitrary", "arbitrary"),
    )(q, knv, kr, knv, do, cos, sin)


def _adam_math(w, g, m, v):
    nm = ADAM_B1 * m + (1.0 - ADAM_B1) * g
    nv = ADAM_B2 * v + (1.0 - ADAM_B2) * (g * g)
    m_hat = nm / (1.0 - ADAM_B1 ** ADAM_STEP)
    v_hat = nv / (1.0 - ADAM_B2 ** ADAM_STEP)
    return -ADAM_LR * (m_hat / (jnp.sqrt(v_hat) + ADAM_EPS) + ADAM_WD * w), nm, nv


def _adamw_small(ws, gs, ms, vs):
    n = len(ws)

    def body(*refs):
        for i in range(n):
            w_ref, g_ref, m_ref, v_ref = (refs[k * n + i] for k in range(4))
            go_ref, d_ref, nm_ref, nv_ref = (refs[(4 + k) * n + i] for k in range(4))
            go_ref[...] = g_ref[...]
            d_ref[...], nm_ref[...], nv_ref[...] = _adam_math(w_ref[...], g_ref[...], m_ref[...], v_ref[...])

    shapes = [jax.ShapeDtypeStruct(a.shape, F32) for a in ws]
    res = _tc_call(body, name="adamw_small", out_shape=shapes * 4)(*ws, *gs, *ms, *vs)
    return [res[k * n:(k + 1) * n] for k in range(4)]


ADAM_SPLIT = 4


def _store_without_head_padding(dst_ref, g):
    assert HEAD_PAD == 2 * LANES and 2 * (QK_NOPE + QK_ROPE) == 3 * LANES, (HEAD_PAD, QK_NOPE, QK_ROPE)
    assert g.shape[1] % (2 * HEAD_PAD) == 0, g.shape
    low = lax.broadcasted_iota(jnp.int32, (g.shape[0], LANES), 1) < LANES // 2
    for pair in range(g.shape[1] // (2 * HEAD_PAD)):
        t = [g[:, (4 * pair + k) * LANES:(4 * pair + k + 1) * LANES] for k in range(4)]
        moved = [pltpu.roll(t[k], LANES // 2, axis=1) for k in (2, 3)]
        outs = (t[0], jnp.where(low, t[1], moved[0]), jnp.where(low, moved[0], moved[1]))
        for k, o in enumerate(outs):
            dst_ref[:, (3 * pair + k) * LANES:(3 * pair + k + 1) * LANES] = o


def _adamw_shards(ids, items, name):
    n = len(items)

    def body(ids_ref, *refs):
        outs = refs[len(refs) - 4 * n:]
        for i, it in enumerate(items):
            w_ref, m_ref, v_ref, gm_ref, gs_ref = refs[5 * i:5 * i + 5]
            g_ref, d_ref, nm_ref, nv_ref = outs[4 * i:4 * i + 4]
            cols = slice(*it["gcols"]) if it.get("gcols") else slice(None)
            whose = pl.program_id(0) if it.get("owner") is None else it["owner"]
            mine = whose == ids_ref[0]

            def take(src_ref, g_ref=g_ref, cols=cols, head_padded=it.get("head_padded")):
                if head_padded:
                    _store_without_head_padding(g_ref, src_ref[...])
                else:
                    g_ref[...] = src_ref[:, cols]

            @pl.when(mine)
            def _(take=take, gm_ref=gm_ref):
                take(gm_ref)

            @pl.when(jnp.logical_not(mine))
            def _(take=take, gs_ref=gs_ref):
                take(gs_ref)

            d_ref[...], nm_ref[...], nv_ref[...] = _adam_math(w_ref[...], g_ref[...], m_ref[...], v_ref[...])

    in_specs, out_specs, out_shape, args, carried, aliases = [], [], [], [ids], [], {}
    for i, it in enumerate(items):
        w = it["w"]
        r, c = w.shape[-2:]
        tr = r // 2 // ADAM_SPLIT
        assert tr % 8 == 0, (name, w.shape)
        layer = it.get("layer")
        if layer is None:
            wspec = pl.BlockSpec((tr, c), lambda h, k, ids: (h * ADAM_SPLIT + k, 0))
        else:
            wspec = pl.BlockSpec((None, tr, c), lambda h, k, ids, layer=layer: (layer, h * ADAM_SPLIT + k, 0))
        gc = it["g_mine"].shape[1]

        def g_index(of_mine, owner=it.get("owner")):
            def index(h, k, ids):
                if owner is None:
                    half = ids[0] if of_mine else 1 - ids[0]
                    return jnp.where(h == half, k, jnp.where(h < half, 0, ADAM_SPLIT - 1)), 0
                read = (owner == ids[0]) if of_mine else (owner != ids[0])
                return jnp.where(read, h * ADAM_SPLIT + k, 0), 0
            return index

        in_specs += [wspec] * 3 + [pl.BlockSpec((tr, gc), g_index(True)), pl.BlockSpec((tr, gc), g_index(False))]
        args += [w, it["m"], it["v"], it["g_mine"], it["g_sib"]]
        out_specs += [wspec] * 4
        out_shape += [jax.ShapeDtypeStruct(w.shape, F32)] * 4
        if it.get("prev") is not None:
            for k, p in enumerate(it["prev"]):
                aliases[1 + 5 * n + len(carried)] = 4 * i + k
                carried.append(p)
    res = _tc_call(
        body, name=name, prefetch=1, grid=(2, ADAM_SPLIT), in_specs=in_specs + [ANY] * len(carried),
        out_specs=out_specs, out_shape=out_shape, input_output_aliases=aliases,
        compiler_params=_cp("parallel", "parallel"),
    )(*args, *carried)
    return [res[4 * i:4 * i + 4] for i in range(n)]


def _peer_chip(k_me, j):
    return k_me ^ jnp.where(j == 0, 2, jnp.where(j == 1, 1, 3))


def _pair_sums(ids, gs, ras, name):
    n = len(gs)

    def body(ids_ref, *refs):
        for i in range(n):
            g_ref, ra_ref, o_ref = refs[2 * i], refs[2 * i + 1], refs[2 * n + i]
            o_ref[...] = (g_ref[...].astype(F32) + ra_ref[...].astype(F32)).astype(BF16)

    in_specs, out_specs, out_shape = [], [], []
    for g in gs:
        half, c = g.shape[1] // 2, g.shape[2]
        in_specs += [pl.BlockSpec((None, half, c), lambda j, ids: (_peer_chip(ids[1], j), ids[0], 0)),
                     pl.BlockSpec((None, half, c), lambda j, ids: (_peer_chip(ids[1], j), 0, 0))]
        out_specs.append(pl.BlockSpec((None, half, c), lambda j, ids: (j, 0, 0)))
        out_shape.append(jax.ShapeDtypeStruct((3, half, c), BF16))
    return _tc_call(
        body, name=name, prefetch=1, grid=(3,), in_specs=in_specs, out_specs=out_specs, out_shape=out_shape,
        compiler_params=_cp("parallel"),
    )(ids, *[a for pair in zip(gs, ras) for a in pair])


def _chip_sums(ids, gs, ras, rbs, name):
    n = len(gs)

    def body(ids_ref, *refs):
        for i in range(n):
            g_ref, ra_ref, rb_ref, o_ref = refs[3 * i], refs[3 * i + 1], refs[3 * i + 2], refs[3 * n + i]
            acc = g_ref[...].astype(F32) + ra_ref[...].astype(F32)
            for j in range(3):
                acc = acc + rb_ref[j].astype(F32)
            o_ref[...] = acc

    in_specs, out_specs, out_shape = [], [], []
    for g in gs:
        half, c = g.shape[1] // 2, g.shape[2]
        in_specs += [pl.BlockSpec((None, half, c), lambda i, ids: (ids[1], ids[0], 0)),
                     pl.BlockSpec((None, half, c), lambda i, ids: (ids[1], 0, 0)),
                     pl.BlockSpec((3, half, c), lambda i, ids: (0, 0, 0))]
        out_specs.append(pl.BlockSpec((half, c), lambda i, ids: (0, 0)))
        out_shape.append(jax.ShapeDtypeStruct((half, c), F32))
    return _tc_call(
        body, name=name, prefetch=1, grid=(1,), in_specs=in_specs, out_specs=out_specs, out_shape=out_shape,
        compiler_params=_cp("arbitrary"),
    )(ids, *[a for trio in zip(gs, ras, rbs) for a in trio])


def _position():
    x, y, c = lax.axis_index("x"), lax.axis_index("y"), lax.axis_index("c")
    chips = [(1 - x, y), (x, 1 - y), (1 - x, 1 - y)]
    return x, y, c, chips


def _shard_half(ref, wm, h):
    if wm.kind == "tiny":
        return ref
    if wm.nl == 2:
        return ref.at[h]
    return ref.at[pl.ds(pl.multiple_of(h * (wm.k // 2), 16), wm.k // 2), :]


def _region(full, wm, s, h):
    if wm.kind == "tiny":
        return full.at[s]
    cols = pl.ds(pl.multiple_of(s * wm.n, LANES), wm.n) if wm.kind == "col" else slice(None)
    if wm.nl == 2:
        rows = pl.ds(pl.multiple_of(s * wm.k, 16), wm.k) if wm.kind == "row" else slice(None)
        return full.at[slice(None) if h is None else h, rows, cols]
    if wm.kind == "col":
        rows = slice(None) if h is None else pl.ds(pl.multiple_of(h * (wm.k // 2), 16), wm.k // 2)
    elif h is None:
        rows = pl.ds(pl.multiple_of(s * wm.k, 16), wm.k)
    else:
        rows = pl.ds(pl.multiple_of(s * wm.k + h * (wm.k // 2), 16), wm.k // 2)
    return full.at[rows, cols]


def _full_shape(wm):
    if wm.kind == "tiny":
        return (N_CHIPS, wm.k, wm.n)
    shape = (wm.k, N_CHIPS * wm.n) if wm.kind == "col" else (N_CHIPS * wm.k, wm.n)
    return shape if wm.nl == 1 else (wm.nl,) + shape


def _handshake(peers):
    barrier = pltpu.get_barrier_semaphore()
    for peer in peers:
        pl.semaphore_signal(barrier, inc=1, device_id=peer, device_id_type=MESH)
    pl.semaphore_wait(barrier, len(peers))


def _all_gather_group(gi, shards):
    wms = AG_GROUPS[gi]
    nw = len(wms)

    def body(*refs):
        sh, full = refs[:nw], refs[nw:2 * nw]
        ici_s, ici_r, pass_s, pass_r, own_s, own_r = refs[2 * nw:]
        x, y, c, _ = _position()
        me, sibling = 2 * x + y, (x, y, 1 - c)
        first, second, diagonal = (x ^ (1 - c), y ^ c), (x ^ c, y ^ (1 - c)), (1 - x, 1 - y)
        chip_id = lambda chip: 2 * chip[0] + chip[1]
        _handshake([(*first, c), (*second, c), sibling])

        def rcopy(src, dst, s_sem, r_sem, to):
            return pltpu.make_async_remote_copy(src_ref=src, dst_ref=dst, send_sem=s_sem, recv_sem=r_sem,
                                                device_id=to, device_id_type=MESH)

        started = []

        def go(cp):
            cp.start()
            started.append(cp)

        for i, wm in enumerate(wms):
            half, dst = _shard_half(sh[i], wm, c), _region(full[i], wm, me, c)
            go(rcopy(half, dst, ici_s.at[i, 0], ici_r.at[i, 0], (*first, c)))
            go(rcopy(half, dst, ici_s.at[i, 1], ici_r.at[i, 1], (*second, c)))
            go(rcopy(sh[i], _region(full[i], wm, me, None), own_s.at[i], own_r.at[i], sibling))
        for i, wm in enumerate(wms):
            got = _region(full[i], wm, chip_id(first), c)
            rcopy(got, got, ici_s.at[i, 0], ici_r.at[i, 0], sibling).wait_recv()
            go(rcopy(got, got, ici_s.at[i, 2], ici_r.at[i, 2], (*second, c)))
            if wm.kind != "tiny":
                go(rcopy(got, got, pass_s.at[i, 0], pass_r.at[i, 0], sibling))
        for i, wm in enumerate(wms):
            for j, chip in ((1, second), (2, diagonal)):
                got = _region(full[i], wm, chip_id(chip), c)
                rcopy(got, got, ici_s.at[i, j], ici_r.at[i, j], sibling).wait_recv()
                if wm.kind != "tiny":
                    go(rcopy(got, got, pass_s.at[i, j], pass_r.at[i, j], sibling))
        for i, wm in enumerate(wms):
            mine = _region(full[i], wm, me, None)
            rcopy(mine, mine, own_s.at[i], own_r.at[i], sibling).wait_recv()
            if wm.kind != "tiny":
                for j, chip in ((0, second), (1, first), (2, diagonal)):
                    got = _region(full[i], wm, chip_id(chip), 1 - c)
                    rcopy(got, got, pass_s.at[i, j], pass_r.at[i, j], sibling).wait_recv()
        for cp in started:
            cp.wait_send()

    return pl.kernel(
        body, out_type=[jax.ShapeDtypeStruct(_full_shape(wm), s.dtype) for wm, s in zip(wms, shards)],
        mesh=plsc.ScalarSubcoreMesh(axis_name="sequencer", num_cores=1), name=f"ag_group{gi}",
        scratch_types=[pltpu.SemaphoreType.DMA((nw, 3))] * 4 + [pltpu.SemaphoreType.DMA((nw,))] * 2,
        compiler_params=pltpu.CompilerParams(collective_id=gi),
    )(*shards)


def _sequencer_call(body, name, cid, out_types, scratch, args):
    return pl.kernel(
        body, out_type=out_types, mesh=plsc.ScalarSubcoreMesh(axis_name="sequencer", num_cores=1), name=name,
        scratch_types=scratch, compiler_params=pltpu.CompilerParams(collective_id=cid),
    )(*args)


def _pair_exchange(gs, tag, cid):
    n = len(gs)

    def body(*refs):
        g, out, send_sems, recv_sems = refs[:n], refs[n:2 * n], refs[2 * n], refs[2 * n + 1]
        x, y, c, _ = _position()
        _handshake([(x, y, 1 - c)])
        cps = []
        for i in range(n):
            half = g[i].shape[1] // 2
            cps.append(pltpu.make_async_remote_copy(
                src_ref=g[i].at[:, pl.ds(pl.multiple_of((1 - c) * half, 16), half), :], dst_ref=out[i],
                send_sem=send_sems.at[i], recv_sem=recv_sems.at[i], device_id=(x, y, 1 - c), device_id_type=MESH))
            cps[-1].start()
        for cp in cps:
            cp.wait()

    return _sequencer_call(
        body, f"rs_pair_exchange{tag}", cid,
        [jax.ShapeDtypeStruct((a.shape[0], a.shape[1] // 2, a.shape[2]), a.dtype) for a in gs],
        [pltpu.SemaphoreType.DMA((n,)), pltpu.SemaphoreType.DMA((n,))], gs)


def _chip_exchange(ss, tag, cid):
    n = len(ss)

    def body(*refs):
        s, out, send_sems, recv_sems = refs[:n], refs[n:2 * n], refs[2 * n], refs[2 * n + 1]
        x, y, c, chips = _position()
        _handshake([(*chip, c) for chip in chips])
        cps = []
        for i in range(n):
            for j, chip in enumerate(chips):
                cps.append(pltpu.make_async_remote_copy(
                    src_ref=s[i].at[j], dst_ref=out[i].at[j], send_sem=send_sems.at[i, j], recv_sem=recv_sems.at[i, j],
                    device_id=(*chip, c), device_id_type=MESH))
                cps[-1].start()
        for cp in cps:
            cp.wait()

    return _sequencer_call(
        body, f"rs_chip_exchange{tag}", cid, [jax.ShapeDtypeStruct(a.shape, a.dtype) for a in ss],
        [pltpu.SemaphoreType.DMA((n, 3)), pltpu.SemaphoreType.DMA((n, 3))], ss)


def _pair_swap(g8s, tag, cid):
    n = len(g8s)

    def body(*refs):
        g, out, send_sems, recv_sems = refs[:n], refs[n:2 * n], refs[2 * n], refs[2 * n + 1]
        x, y, c, _ = _position()
        _handshake([(x, y, 1 - c)])
        cps = []
        for i in range(n):
            cps.append(pltpu.make_async_remote_copy(
                src_ref=g[i], dst_ref=out[i], send_sem=send_sems.at[i], recv_sem=recv_sems.at[i],
                device_id=(x, y, 1 - c), device_id_type=MESH))
            cps[-1].start()
        for cp in cps:
            cp.wait()

    return _sequencer_call(
        body, f"rs_pair_swap{tag}", cid, [jax.ShapeDtypeStruct(a.shape, a.dtype) for a in g8s],
        [pltpu.SemaphoreType.DMA((n,)), pltpu.SemaphoreType.DMA((n,))], g8s)


def _pair_swap_now(g8s):
    n = len(g8s)

    def body(*refs):
        g, out, send_sems, recv_sems = refs[:n], refs[n:2 * n], refs[2 * n], refs[2 * n + 1]
        x, y, c, _ = _position()
        cps = []
        for i in range(n):
            cps.append(pltpu.make_async_remote_copy(
                src_ref=g[i], dst_ref=out[i], send_sem=send_sems.at[i], recv_sem=recv_sems.at[i],
                device_id=(x, y, 1 - c), device_id_type=MESH))
            cps[-1].start()
        for cp in cps:
            cp.wait()

    return _tc_call(
        body, name="rs_pair_swap_last", in_specs=[ANY] * n, out_specs=[ANY] * n,
        out_shape=[jax.ShapeDtypeStruct(a.shape, a.dtype) for a in g8s],
        scratch_shapes=[pltpu.SemaphoreType.DMA((n,)), pltpu.SemaphoreType.DMA((n,))],
    )(*g8s)


def _all_reduce_small(vecs, owner_major, name):
    n = len(vecs)
    block = lambda i, ref, chip: ref.at[chip] if owner_major[i] else ref
    out_shapes = [a.shape[1:] if owner_major[i] else a.shape for i, a in enumerate(vecs)]

    def body(*refs):
        v, o, gath = refs[:n], refs[n:2 * n], refs[2 * n:3 * n]
        send_sems, recv_sems = refs[3 * n], refs[3 * n + 1]
        x, y, c, _ = _position()
        me = 4 * x + 2 * y + c
        cps = []
        for i in range(n):
            gath[i][me] = block(i, v[i], 2 * x + y)[...]
            for rel in range(1, N_DEV):
                px, py, pc = x ^ (rel >> 2), y ^ ((rel >> 1) & 1), c ^ (rel & 1)
                cps.append(pltpu.make_async_remote_copy(
                    src_ref=block(i, v[i], 2 * px + py), dst_ref=gath[i].at[me], send_sem=send_sems.at[i, rel - 1],
                    recv_sem=recv_sems.at[i, rel - 1], device_id=(px, py, pc), device_id_type=MESH))
        for cp in cps:
            cp.start()
        for i in range(n):
            for rel in range(1, N_DEV):
                pltpu.make_async_remote_copy(
                    src_ref=block(i, v[i], 2 * x + y), dst_ref=gath[i].at[me ^ rel],
                    send_sem=send_sems.at[i, rel - 1], recv_sem=recv_sems.at[i, rel - 1], device_id=(x, y, c),
                    device_id_type=MESH).wait_recv()
        for cp in cps:
            cp.wait_send()
        for i in range(n):
            acc = gath[i][0]
            for d in range(1, N_DEV):
                acc = acc + gath[i][d]
            o[i][...] = acc

    vm = pl.BlockSpec(memory_space=pltpu.VMEM)
    return _tc_call(
        body, name=name, in_specs=[vm] * n, out_specs=[vm] * n,
        out_shape=[jax.ShapeDtypeStruct(s, F32) for s in out_shapes],
        scratch_shapes=[pltpu.VMEM((N_DEV,) + s, F32) for s in out_shapes]
        + [pltpu.SemaphoreType.DMA((n, N_DEV - 1)), pltpu.SemaphoreType.DMA((n, N_DEV - 1))],
    )(*vecs)


def _rope_tables(positions):
    half = QK_ROPE // 2
    inv_freq = 1.0 / (ROPE_THETA ** (jnp.arange(half, dtype=F32) / half))
    ang = positions.astype(F32)[:, None] * inv_freq
    zeros = jnp.zeros((positions.shape[0], LANES - QK_ROPE), F32)
    cos, sin = jnp.cos(ang), jnp.sin(ang)
    return jnp.concatenate([cos, cos, zeros], axis=1), jnp.concatenate([sin, sin, zeros], axis=1)


def _local_step(x, positions, tgt, wf, small, rs):
    cos, sin = _rope_tables(positions)
    w_in, w_out = wf["sc_w_in"], wf["sc_w_out"]
    w_ups, w_downs = (wf["ffn_w_up0"], wf["ffn_w_up1"]), (wf["ffn_w_down0"], wf["ffn_w_down1"])
    w_kv, w_ukv, w_dq, w_uq, w_o = wf["w_kv"], wf["w_ukv"], wf["w_dq"], wf["w_uq"], wf["w_o"]
    attn_norm, ffn_norm = small["attn_norm"], small["ffn_norm"]
    conv_b = small["ffn_conv_b"]

    def ffn_fwd(h, hf, l, then):
        up, a = _ffn_up_gate(hf, w_ups[l], small["ffn_conv_w"][l], conv_b[l:l + 1], f"ffn{l}_up_gate")
        return then(a, w_downs[l], h), (hf, up, a)

    def ffn_bwd(h, dh_out, dh_out_b, l, saved, gi, hooks):
        run = lambda stage: hooks.get(stage, lambda: None)()
        hf, up, a = saved
        d_down = _tn(f"ffn{l}_down_dw", a, dh_out_b, BF16)
        run("down_dw")
        dup, d_cw, d_cb = _gate_bwd(up, small["ffn_conv_w"][l], conv_b[l:l + 1], dh_out_b, w_downs[l],
                                    f"ffn{l}_gate_bwd")
        run("gate_bwd")
        d_up = _dw_ffn_up(f"ffn{l}_up_dw", hf, dup)
        rs.start(gi, {f"ffn_w_down{l}": d_down.reshape(N_CHIPS, F_FF // N_CHIPS, D), f"ffn_w_up{l}": d_up})
        run("up_dw")
        dh, dh_b, d_norm = _dx_norm_bwd(f"ffn{l}_up_dx", dup, w_ups[l], h, ffn_norm[l:l + 1], dh_out)
        run("up_dx")
        return dh, dh_b, d_cw, d_cb, d_norm

    hn0 = _rms_fwd(x, attn_norm[0:1], "attn0_norm")
    z = _nn_parts("sc_in", hn0, w_in, 3, BF16)
    mix = _scmix_fwd(z, small["sc_conv_w"])
    h1, hf0 = _nn_add_norm("sc_out", mix, w_out, x, ffn_norm[0:1])
    h2, ffn0_saved = ffn_fwd(h1, hf0, 0, lambda a, w, h: _nn("ffn0_down", a, w, F32, add=h))

    hn1, hk, cq_pre, cq, q, kvpre, ckv, kr, knv = _attn_prep(
        h2, attn_norm[1:2], small["kv_in_norm"], w_dq, small["q_latent_norm"], w_uq, w_kv, small["kv_latent_norm"],
        w_ukv, cos, sin)
    o = _attn_fwd(q, knv, kr)
    h3, hf1 = _nn_add_norm("attn_out", o, w_o, h2, ffn_norm[1:2])
    (loss, dh4, dh4_b, d_final), ffn1_saved = ffn_fwd(
        h3, hf1, 1, lambda a, w, h: _nn_add_loss("ffn1_down_loss", a, w, h, small["final_norm"], tgt))

    rows = D // N_CHIPS
    dh3, dh3_b, d_cw1, d_cb1, d_fn1 = ffn_bwd(h3, dh4, dh4_b, 1, ffn1_saved, 0, {})

    do = _nt("attn_out_dx", dh3_b, w_o, BF16)
    d_wo = _tn("attn_out_dw", o, dh3_b, BF16)
    rs.pair_sums(0)
    dq, dknv, dkr = _attn_bwd(q, knv, kr, do, cos, sin)
    rs.chip_sums(0)
    dh2, dh2_b, d_wuq, d_wdq, d_wukv, d_wkv, d_an1, d_kvin, d_qln, d_kvln = _attn_prep_bwd(
        dq, dknv, dkr, dh3, h2, hn1, hk, cq_pre, cq, kvpre, ckv, attn_norm[1:2], small["kv_in_norm"], w_dq,
        small["q_latent_norm"], w_uq, w_kv, small["kv_latent_norm"], w_ukv, cos, sin)
    rs.finish(0)
    by_owner = lambda dw: dw.reshape(dw.shape[0], N_CHIPS, -1).transpose(1, 0, 2)
    rs.start(1, {
        "w_o": d_wo.reshape(N_CHIPS, rows, D), "w_uq": by_owner(d_wuq), "w_dq": d_wdq.reshape(N_CHIPS, rows, Q_LORA),
        "w_ukv": by_owner(d_wukv.reshape(2 * KV_LORA, -1)).reshape(N_CHIPS, 2 * KV_LORA, -1),
        "w_kv": d_wkv.reshape(N_CHIPS, rows, KVP),
    })

    dh1, dh1_b, d_cw0, d_cb0, d_fn0 = ffn_bwd(h1, dh2, dh2_b, 0, ffn0_saved, 2, {
        "down_dw": lambda: rs.pair_sums(1), "gate_bwd": lambda: rs.chip_sums(1),
        "up_dw": lambda: (rs.finish(1), rs.pair_sums(2))})

    d_wout = _tn("sc_out_dw", mix, dh1_b, BF16)
    dmix = _nt("sc_out_dx", dh1_b, w_out, BF16)
    dz, d_scw = _scmix_bwd(z, small["sc_conv_w"], dmix)
    d_win = _dw_sc_in(hn0, dz)
    rs.start(3, {"sc_w_out": d_wout.reshape(N_CHIPS, rows, D), "sc_w_in": d_win})
    dx, _, d_an0 = _dx_norm_bwd("sc_in_dx", dz, w_in, x, attn_norm[0:1], dh1)

    taps_by_owner = lambda per_layer: jnp.stack(per_layer, axis=1).reshape(3, len(per_layer), N_CHIPS, -1).transpose(2, 0, 1, 3)
    small_g = {
        "attn_norm": jnp.concatenate([d_an0, d_an1]), "ffn_norm": jnp.concatenate([d_fn0, d_fn1]),
        "final_norm": d_final, "kv_in_norm": d_kvin, "kv_latent_norm": d_kvln, "q_latent_norm": d_qln,
        "ffn_conv_b": jnp.concatenate([d_cb0, d_cb1]),
        "sc_conv_w": taps_by_owner([d_scw]), "ffn_conv_w": taps_by_owner([d_cw0, d_cw1]),
    }
    return loss, dx, small_g


RS_GROUPS = (("ffn_w_down1", "ffn_w_up1"), ("w_o", "w_uq", "w_dq", "w_ukv", "w_kv"),
             ("ffn_w_down0", "ffn_w_up0"), ("sc_w_out", "sc_w_in"))


class _ReduceScatter:
    def __init__(self, ids, finish):
        self.ids, self.grads, self.step, self.mine, self.sib, self.finish = ids, {}, {}, {}, {}, finish

    def _cid(self, gi):
        return len(AG_GROUPS) + 3 * gi

    def start(self, gi, grads):
        self.grads.update(grads)
        own = [grads[n] for n in RS_GROUPS[gi]]
        self.step[gi] = (own, _pair_exchange(own, gi, self._cid(gi)))

    def pair_sums(self, gi):
        own, ra = self.step[gi]
        sums = _pair_sums(self.ids, own, ra, f"rs_pair_sums{gi}")
        self.step[gi] = (own, ra, _chip_exchange(sums, gi, self._cid(gi) + 1))

    def chip_sums(self, gi):
        own, ra, rb = self.step[gi]
        mine = _chip_sums(self.ids, own, ra, rb, f"rs_chip_sums{gi}")
        self.mine.update(zip(RS_GROUPS[gi], mine))
        last = gi == len(RS_GROUPS) - 1
        swapped = _pair_swap_now(mine) if last else _pair_swap(mine, gi, self._cid(gi) + 2)
        self.sib.update(zip(RS_GROUPS[gi], swapped))


SMALL_REPL = ("attn_norm", "ffn_norm", "final_norm", "kv_in_norm", "kv_latent_norm", "q_latent_norm", "ffn_conv_b")


def _pad_heads(w_uq):
    per_head = w_uq.reshape(Q_LORA, -1, QK_NOPE + QK_ROPE)
    return jnp.pad(per_head, ((0, 0), (0, 0), (0, HEAD_PAD - QK_NOPE - QK_ROPE))).reshape(Q_LORA, -1)


def _pack_kv(w_dkv, w_kr):
    return jnp.concatenate([w_dkv, w_kr, jnp.zeros((w_kr.shape[0], LANES - QK_ROPE), w_kr.dtype)], axis=1)


def kernel(x, positions, attn_norm, ffn_norm, final_norm, sc_w_in, sc_conv_w, sc_w_out, kv_in_norm, w_dkv, kv_latent_norm, w_kr, w_uk, w_uv, w_dq, q_latent_norm, w_uq, w_o, ffn_w_up, ffn_conv_w, ffn_conv_b, ffn_w_down, loss_target, m_attn_norm, m_ffn_norm, m_final_norm, m_sc_w_in, m_sc_conv_w, m_sc_w_out, m_kv_in_norm, m_w_dkv, m_kv_latent_norm, m_w_kr, m_w_uk, m_w_uv, m_w_dq, m_q_latent_norm, m_w_uq, m_w_o, m_ffn_w_up, m_ffn_conv_w, m_ffn_conv_b, m_ffn_w_down, v_attn_norm, v_ffn_norm, v_final_norm, v_sc_w_in, v_sc_conv_w, v_sc_w_out, v_kv_in_norm, v_w_dkv, v_kv_latent_norm, v_w_kr, v_w_uk, v_w_uv, v_w_dq, v_q_latent_norm, v_w_uq, v_w_o, v_ffn_w_up, v_ffn_conv_w, v_ffn_conv_b, v_ffn_w_down):
    names = ("attn_norm", "ffn_norm", "final_norm", "sc_w_in", "sc_conv_w", "sc_w_out", "kv_in_norm", "w_dkv",
             "kv_latent_norm", "w_kr", "w_uk", "w_uv", "w_dq", "q_latent_norm", "w_uq", "w_o", "ffn_w_up",
             "ffn_conv_w", "ffn_conv_b", "ffn_w_down")
    w = dict(zip(names, (attn_norm, ffn_norm, final_norm, sc_w_in, sc_conv_w, sc_w_out, kv_in_norm, w_dkv,
                         kv_latent_norm, w_kr, w_uk, w_uv, w_dq, q_latent_norm, w_uq, w_o, ffn_w_up,
                         ffn_conv_w, ffn_conv_b, ffn_w_down)))
    m = dict(zip(names, (m_attn_norm, m_ffn_norm, m_final_norm, m_sc_w_in, m_sc_conv_w, m_sc_w_out, m_kv_in_norm,
                         m_w_dkv, m_kv_latent_norm, m_w_kr, m_w_uk, m_w_uv, m_w_dq, m_q_latent_norm, m_w_uq, m_w_o,
                         m_ffn_w_up, m_ffn_conv_w, m_ffn_conv_b, m_ffn_w_down)))
    v = dict(zip(names, (v_attn_norm, v_ffn_norm, v_final_norm, v_sc_w_in, v_sc_conv_w, v_sc_w_out, v_kv_in_norm,
                         v_w_dkv, v_kv_latent_norm, v_w_kr, v_w_uk, v_w_uv, v_w_dq, v_q_latent_norm, v_w_uq, v_w_o,
                         v_ffn_w_up, v_ffn_conv_w, v_ffn_conv_b, v_ffn_w_down)))

    _ORDER[0] = None
    ix, iy, ic = lax.axis_index("x"), lax.axis_index("y"), lax.axis_index("c")
    chip = 2 * ix + iy
    ids = jnp.stack([ic, chip]).astype(jnp.int32)

    ws = {
        "sc_w_in": sc_w_in[0], "sc_w_out": sc_w_out[0], "ffn_w_up": ffn_w_up, "ffn_w_down": ffn_w_down,
        "w_kv": _pack_kv(w_dkv, w_kr), "w_ukv": jnp.stack([w_uk, w_uv]), "w_dq": w_dq[0],
        "w_uq": _pad_heads(w_uq[0]), "w_o": w_o[0],
    }

    def ag_shard(name):
        if name == "sc_conv_w":
            return sc_conv_w[0]
        if name == "ffn_conv_w":
            return ffn_conv_w.reshape(6, -1)
        if name[:-1] in ("ffn_w_up", "ffn_w_down"):
            return ws[name[:-1]][int(name[-1])].astype(BF16)
        return ws[name].astype(BF16)

    wf = {}
    for gi, wms in enumerate(AG_GROUPS):
        fulls = _all_gather_group(gi, [ag_shard(wm.name) for wm in wms])
        wf.update({wm.name: f for wm, f in zip(wms, fulls)})
    small = {
        "attn_norm": attn_norm, "ffn_norm": ffn_norm, "final_norm": final_norm[None], "kv_in_norm": kv_in_norm[None],
        "kv_latent_norm": kv_latent_norm[None], "q_latent_norm": q_latent_norm, "ffn_conv_b": ffn_conv_b,
        "sc_conv_w": wf["sc_conv_w"].transpose(1, 0, 2).reshape(3, D),
        "ffn_conv_w": wf["ffn_conv_w"].reshape(N_CHIPS, 2, 3, -1).transpose(1, 2, 0, 3).reshape(2, 3, F_FF),
    }

    res = {}

    held = {
        "ffn_w_up0": [("ffn_w_up", dict(layer=0))], "ffn_w_up1": [("ffn_w_up", dict(layer=1))],
        "ffn_w_down0": [("ffn_w_down", dict(layer=0))], "ffn_w_down1": [("ffn_w_down", dict(layer=1))],
        "sc_w_in": [("sc_w_in", dict(layer=0))], "sc_w_out": [("sc_w_out", dict(layer=0))],
        "w_dq": [("w_dq", dict(layer=0))], "w_o": [("w_o", dict(layer=0))], "w_uq": [("w_uq", dict(layer=0, head_padded=True))],
        "w_kv": [("w_dkv", dict(gcols=(0, KV_LORA))), ("w_kr", dict(gcols=(KV_LORA, KV_LORA + QK_ROPE)))],
        "w_ukv": [("w_uk", dict(owner=0)), ("w_uv", dict(owner=1))],
    }

    def adamw_group(gi):
        items = []
        for key in RS_GROUPS[gi]:
            for n, opts in held[key]:
                items.append(dict(name=n, w=w[n], m=m[n], v=v[n], g_mine=rs.mine[key], g_sib=rs.sib[key],
                                  prev=res.get(n) if "layer" in opts and w[n].shape[0] > 1 else None, **opts))
        for it, out in zip(items, _adamw_shards(ids, items, f"adamw_group{gi}")):
            res[it["name"]] = out

    rs = _ReduceScatter(ids, adamw_group)
    loss, dx, small_g = _local_step(x[0], positions[0], loss_target[0], wf, small, rs)

    rs.chip_sums(2)
    rs.pair_sums(3)

    s_names = list(small_g)
    reduced = _all_reduce_small([small_g[n] for n in s_names] + [loss], [small_g[n].ndim == 4 for n in s_names] + [False],
                                "ar_small")
    sg, loss_out = dict(zip(s_names, reduced[:-1])), reduced[-1][0, 0]

    row = lambda n: (lambda t: t[n][None])
    taps = lambda n: (lambda t: t[n].transpose(1, 0, 2))
    small_2d = {
        "attn_norm": (sg["attn_norm"], lambda t: t["attn_norm"]), "ffn_norm": (sg["ffn_norm"], lambda t: t["ffn_norm"]),
        "final_norm": (sg["final_norm"], row("final_norm")), "kv_in_norm": (sg["kv_in_norm"], row("kv_in_norm")),
        "kv_latent_norm": (sg["kv_latent_norm"], row("kv_latent_norm")),
        "q_latent_norm": (sg["q_latent_norm"], lambda t: t["q_latent_norm"]),
        "ffn_conv_b": (sg["ffn_conv_b"], lambda t: t["ffn_conv_b"]),
        "sc_conv_w": (sg["sc_conv_w"], taps("sc_conv_w")), "ffn_conv_w": (sg["ffn_conv_w"], taps("ffn_conv_w")),
    }
    s_keys = list(small_2d)
    small_grads = [small_2d[k][0] for k in s_keys]
    views = lambda tree: [small_2d[k][1](tree) for k in s_keys]
    small_res = _adamw_small(views(w), small_grads, views(m), views(v))

    def restore(vals):
        by = dict(zip(s_keys, vals))
        out = {n: by[n].reshape(w[n].shape) for n in SMALL_REPL}
        out.update({n: by[n].transpose(1, 0, 2) for n in ("sc_conv_w", "ffn_conv_w")})
        return out

    rs.finish(2)
    rs.chip_sums(3)
    rs.finish(3)
    outs = [restore(vals) for vals in small_res]
    for k, dst in enumerate(outs):
        for n in res:
            dst[n] = res[n][k]
    grads, delta, new_m, new_v = outs

    _ORDER[0] = None
    return (loss_out, dx[None], *[grads[n] for n in names], *[delta[n] for n in names],
            *[new_m[n] for n in names], *[new_v[n] for n in names])
```

```python
from typing import NamedTuple

import jax
import jax.numpy as jnp
from jax import lax
from jax.experimental import pallas as pl
from jax.experimental.pallas import tpu as pltpu
from jax.experimental.pallas import tpu_sc as plsc

F32 = jnp.float32
BF16 = jnp.bfloat16

T = 2048
D = 1024
F_FF = 2816
N_HEADS = 8
QK_NOPE = 128
QK_ROPE = 64
V_HEAD = 128
Q_LORA = 384
KV_LORA = 256
CHUNK_SHIFT = 6
ROPE_THETA = 10000.0
EPS = 1e-6
NEG_INF = -1e30
HEAD_PAD = 256
KVP = KV_LORA + 128

ADAM_LR = 0.001
ADAM_B1 = 0.9
ADAM_B2 = 0.999
ADAM_EPS = 1e-08
ADAM_WD = 0.01
ADAM_STEP = 10

N_CHIPS = 4
N_DEV = 8
LANES = 128
TC = 256
V7X_VMEM_LIMIT = 56 * 1024 * 1024

MESH = pl.DeviceIdType.MESH
ANY = pl.BlockSpec(memory_space=pl.ANY)


class _W(NamedTuple):
    name: str
    kind: str
    nl: int
    k: int
    n: int


AG_GROUPS = (
    (_W("sc_w_in", "col", 1, D, 3 * D // N_CHIPS), _W("sc_conv_w", "tiny", 1, 3, D // N_CHIPS),
     _W("ffn_conv_w", "tiny", 1, 6, F_FF // N_CHIPS), _W("sc_w_out", "row", 1, D // N_CHIPS, D)),
    (_W("ffn_w_up0", "col", 1, D, 2 * F_FF // N_CHIPS),),
    (_W("ffn_w_down0", "row", 1, F_FF // N_CHIPS, D),),
    (_W("w_kv", "row", 1, D // N_CHIPS, KVP), _W("w_ukv", "col", 2, KV_LORA, N_HEADS * QK_NOPE // N_CHIPS),
     _W("w_dq", "row", 1, D // N_CHIPS, Q_LORA),
     _W("w_uq", "col", 1, Q_LORA, N_HEADS * HEAD_PAD // N_CHIPS),
     _W("w_o", "row", 1, N_HEADS * V_HEAD // N_CHIPS, D)),
    (_W("ffn_w_up1", "col", 1, D, 2 * F_FF // N_CHIPS), _W("ffn_w_down1", "row", 1, F_FF // N_CHIPS, D)),
)


def _cp(*sem):
    return pltpu.CompilerParams(dimension_semantics=sem, vmem_limit_bytes=V7X_VMEM_LIMIT)


_ORDER = [None]


def _tc_call(body, *, name, out_shape, in_specs=None, out_specs=None, grid=(), scratch_shapes=(), prefetch=0,
             input_output_aliases=None, compiler_params=None):
    def run(*args):
        specs = [pl.BlockSpec(memory_space=pltpu.VMEM)] * (len(args) - prefetch) if in_specs is None else list(in_specs)
        inner, dep = body, _ORDER[0]
        if dep is not None:
            unread = prefetch + len(specs)
            specs, args = specs + [ANY], (*args, dep)

            def inner(*refs):
                return body(*refs[:unread], *refs[unread + 1:])

        kwargs = dict(name=name, out_shape=out_shape, input_output_aliases=input_output_aliases or {},
                      compiler_params=compiler_params)
        if prefetch:
            kwargs["grid_spec"] = pltpu.PrefetchScalarGridSpec(
                num_scalar_prefetch=prefetch, grid=grid, in_specs=specs, out_specs=out_specs,
                scratch_shapes=scratch_shapes)
        else:
            kwargs.update(grid=grid, in_specs=specs, scratch_shapes=scratch_shapes)
            if out_specs is not None:
                kwargs["out_specs"] = out_specs
        out = pl.pallas_call(inner, **kwargs)(*args)
        _ORDER[0] = out[0] if isinstance(out, (list, tuple)) else out
        return out

    return run


def _tile(n, cands):
    for c in cands:
        if n % c == 0:
            return c
    raise ValueError(f"no tile for {n}")


NN_DIMS = (((1,), (0,)), ((), ()))
NT_DIMS = (((1,), (1,)), ((), ()))
TN_DIMS = (((0,), (0,)), ((), ()))
M_TILES = (1024, 512, 384, 256, 128)
N_TILES = (1408, 1024, 768, 512, 384, 256, 128)
MM_BLOCK_BYTES = 36 * 1024 * 1024


def _fit(m, n, block_bytes, m_tiles=M_TILES, n_tiles=N_TILES, n_first=False):
    tms, tns = [c for c in m_tiles if m % c == 0], [c for c in n_tiles if n % c == 0]
    pairs = [(tm, tn) for tn in tns for tm in tms] if n_first else [(tm, tn) for tm in tms for tn in tns]
    for tm, tn in pairs:
        if 2 * block_bytes(tm, tn) + 4 * tm * tn <= MM_BLOCK_BYTES:
            return tm, tn
    raise ValueError(f"no tiles for {m} x {n}")


def _size(x):
    return x.dtype.itemsize


def _mm(name, a, b, dims, grid, a_spec, b_spec, o_spec, o_sds, add=None, red=None, acc_shape=None):
    n_red = None if red is None else grid[red]

    def body(*refs):
        a_ref, b_ref = refs[0], refs[1]
        add_ref = refs[2] if add is not None else None
        o_ref = refs[3] if add is not None else refs[2]
        part = lax.dot_general(a_ref[...].astype(BF16), b_ref[...].astype(BF16), dims, preferred_element_type=F32)
        if red is None:
            if add is not None:
                part = part + add_ref[...]
            o_ref[...] = part.astype(o_ref.dtype)
            return
        acc_ref = refs[-1]
        r = pl.program_id(red)

        @pl.when(r == 0)
        def _():
            acc_ref[...] = part

        @pl.when(r > 0)
        def _():
            acc_ref[...] += part

        @pl.when(r == n_red - 1)
        def _():
            o_ref[...] = acc_ref[...].astype(o_ref.dtype)

    sem = tuple("arbitrary" if ax == red else "parallel" for ax in range(len(grid)))
    in_specs = [a_spec, b_spec] + ([o_spec] if add is not None else [])
    args = (a, b) + ((add,) if add is not None else ())
    return _tc_call(
        body, name=name, grid=grid, in_specs=in_specs, out_specs=o_spec, out_shape=o_sds,
        scratch_shapes=[] if red is None else [pltpu.VMEM(acc_shape, F32)], compiler_params=_cp(*sem),
    )(*args)


def _nn(name, a, b, out_dtype, add=None, lead=None):
    (m, k), n = a.shape, b.shape[-1]
    osz = jnp.dtype(out_dtype).itemsize + (4 if add is not None else 0)
    tm, tn = _fit(m, n, lambda tm, tn: tm * k * _size(a) + k * tn * _size(b) + tm * tn * osz, n_first=True)
    if lead is None:
        b_spec = pl.BlockSpec((k, tn), lambda i, j: (0, j))
    else:
        b_spec = pl.BlockSpec((None, k, tn), lambda i, j: (lead, 0, j))
    return _mm(name, a, b, NN_DIMS, (m // tm, n // tn), pl.BlockSpec((tm, k), lambda i, j: (i, 0)), b_spec,
               pl.BlockSpec((tm, tn), lambda i, j: (i, j)), jax.ShapeDtypeStruct((m, n), out_dtype), add=add)


def _nn_parts(name, a, b, parts, out_dtype, lead=None, stacked=False):
    m, k = a.shape
    c = b.shape[-1] if stacked else b.shape[-1] // parts
    osz = jnp.dtype(out_dtype).itemsize
    tm, tn = _fit(m, c, lambda tm, tn: tm * k * _size(a) + k * tn * _size(b) + tm * tn * osz)
    nb = c // tn
    if stacked:
        b_spec = pl.BlockSpec((None, k, tn), lambda i, p, j: (p, 0, j))
    elif lead is None:
        b_spec = pl.BlockSpec((k, tn), lambda i, p, j: (0, p * nb + j))
    else:
        b_spec = pl.BlockSpec((None, k, tn), lambda i, p, j: (lead, 0, p * nb + j))
    return _mm(name, a, b, NN_DIMS, (m // tm, parts, nb), pl.BlockSpec((tm, k), lambda i, p, j: (i, 0)), b_spec,
               pl.BlockSpec((None, tm, tn), lambda i, p, j: (p, i, j)), jax.ShapeDtypeStruct((parts, m, c), out_dtype))


def _nt(name, a, b, out_dtype, lead=None):
    (m, k), n = a.shape, b.shape[-2]
    osz = jnp.dtype(out_dtype).itemsize
    tm, tn = _fit(m, n, lambda tm, tn: tm * k * _size(a) + tn * k * _size(b) + tm * tn * osz)
    if lead is None:
        b_spec = pl.BlockSpec((tn, k), lambda i, j: (j, 0))
    else:
        b_spec = pl.BlockSpec((None, tn, k), lambda i, j: (lead, j, 0))
    return _mm(name, a, b, NT_DIMS, (m // tm, n // tn), pl.BlockSpec((tm, k), lambda i, j: (i, 0)), b_spec,
               pl.BlockSpec((tm, tn), lambda i, j: (i, j)), jax.ShapeDtypeStruct((m, n), out_dtype))


def _tn(name, a, b, out_dtype):
    (k, m), n = a.shape, b.shape[1]
    osz = jnp.dtype(out_dtype).itemsize
    tm, tn = _fit(m, n, lambda tm, tn: k * tm * _size(a) + k * tn * _size(b) + tm * tn * osz,
                  m_tiles=(512, 384, 256, 128), n_tiles=(n,) + N_TILES)
    return _mm(name, a, b, TN_DIMS, (m // tm, n // tn), pl.BlockSpec((k, tm), lambda i, j: (0, i)),
               pl.BlockSpec((k, tn), lambda i, j: (0, j)), pl.BlockSpec((tm, tn), lambda i, j: (i, j)),
               jax.ShapeDtypeStruct((m, n), out_dtype))


def _nn_add_norm(name, a, b, add, g):
    (m, k), n = a.shape, b.shape[1]
    tm = 512

    def body(a_ref, b_ref, add_ref, g_ref, h_ref, hn_ref):
        h = jnp.dot(a_ref[...], b_ref[...], preferred_element_type=F32) + add_ref[...]
        h_ref[...] = h
        hn_ref[...] = _rms_rows(h, g_ref[...]).astype(BF16)

    rows = lambda w: pl.BlockSpec((tm, w), lambda i: (i, 0))
    return _tc_call(
        body, name=name, grid=(m // tm,),
        in_specs=[rows(k), pl.BlockSpec((k, n), lambda i: (0, 0)), rows(n), pl.BlockSpec((1, n), lambda i: (0, 0))],
        out_specs=[rows(n), rows(n)],
        out_shape=[jax.ShapeDtypeStruct((m, n), F32), jax.ShapeDtypeStruct((m, n), BF16)], compiler_params=_cp("parallel"),
    )(a, b, add, g)


def _nn_add_loss(name, a, b, add, g, tgt):
    (m, k), n = a.shape, b.shape[1]
    tm = 512

    def body(a_ref, b_ref, add_ref, g_ref, t_ref, loss_ref, dh_ref, dhb_ref, dg_ref):
        xv = jnp.dot(a_ref[...], b_ref[...], preferred_element_type=F32) + add_ref[...]
        gv = g_ref[...]
        r = lax.rsqrt(jnp.mean(xv * xv, axis=1, keepdims=True) + EPS)
        err = xv * r * gv - t_ref[...]
        part = 0.5 * jnp.sum(jnp.mean(err * err, axis=1, keepdims=True), axis=0, keepdims=True)
        dx, dg = _rms_bwd_math(xv, gv, err * (1.0 / n))
        dh_ref[...] = dx
        dhb_ref[...] = dx.astype(BF16)

        @pl.when(pl.program_id(0) == 0)
        def _():
            dg_ref[...] = jnp.zeros_like(dg_ref)
            loss_ref[...] = jnp.zeros_like(loss_ref)

        dg_ref[...] += dg
        loss_ref[...] += jnp.broadcast_to(part, loss_ref.shape)

    rows = lambda w: pl.BlockSpec((tm, w), lambda i: (i, 0))
    vec = pl.BlockSpec((1, n), lambda i: (0, 0))
    return _tc_call(
        body, name=name, grid=(m // tm,),
        in_specs=[rows(k), pl.BlockSpec((k, n), lambda i: (0, 0)), rows(n), vec, rows(n)],
        out_specs=[pl.BlockSpec((1, LANES), lambda i: (0, 0)), rows(n), rows(n), vec],
        out_shape=[jax.ShapeDtypeStruct((1, LANES), F32), jax.ShapeDtypeStruct((m, n), F32),
                   jax.ShapeDtypeStruct((m, n), BF16), jax.ShapeDtypeStruct((1, n), F32)],
        compiler_params=_cp("arbitrary"),
    )(a, b, add, g, tgt)


def _dx_norm_bwd(name, a, b, x, g, add):
    parts, t, c = a.shape
    d = b.shape[0]
    tm = 256

    def body(a_ref, b_ref, x_ref, g_ref, add_ref, dx_ref, dxb_ref, dg_ref):
        dy = None
        for p in range(parts):
            part = lax.dot_general(a_ref[p], b_ref[:, p * c:(p + 1) * c], NT_DIMS, preferred_element_type=F32)
            dy = part if dy is None else dy + part
        dx, dg = _rms_bwd_math(x_ref[...], g_ref[...], dy)
        dx = dx + add_ref[...]
        dx_ref[...] = dx
        dxb_ref[...] = dx.astype(BF16)

        @pl.when(pl.program_id(0) == 0)
        def _():
            dg_ref[...] = jnp.zeros_like(dg_ref)

        dg_ref[...] += dg

    rows = pl.BlockSpec((tm, d), lambda i: (i, 0))
    vec = pl.BlockSpec((1, d), lambda i: (0, 0))
    return _tc_call(
        body, name=name, grid=(t // tm,),
        in_specs=[pl.BlockSpec((parts, tm, c), lambda i: (0, i, 0)), pl.BlockSpec(b.shape, lambda i: (0, 0)), rows, vec,
                  rows],
        out_specs=[rows, rows, vec],
        out_shape=[jax.ShapeDtypeStruct((t, d), F32), jax.ShapeDtypeStruct((t, d), BF16),
                   jax.ShapeDtypeStruct((1, d), F32)],
        compiler_params=_cp("arbitrary"),
    )(a, b, x, g, add)


def _dw_sc_in(hn, dz):
    t, tn, tm = hn.shape[0], TC, D
    per_part, per_chip = D // tn, 3 * D // N_CHIPS // tn
    return _mm("sc_in_dw", hn, dz, TN_DIMS, (D // tm, 3 * D // tn), pl.BlockSpec((t, tm), lambda i, j: (0, i)),
               pl.BlockSpec((None, t, tn), lambda i, j: (j // per_part, 0, j % per_part)),
               pl.BlockSpec((None, tm, tn), lambda i, j: (j // per_chip, i, j % per_chip)),
               jax.ShapeDtypeStruct((N_CHIPS, D, 3 * D // N_CHIPS), BF16))


def _dw_ffn_up(name, hf, dup):
    t, tm, ns = hf.shape[0], D, 2 * F_FF // N_CHIPS
    return _mm(name, hf, dup, TN_DIMS, (N_CHIPS, D // tm), pl.BlockSpec((t, tm), lambda s, i: (0, i)),
               pl.BlockSpec((None, t, ns), lambda s, i: (s // 2, 0, s % 2)),
               pl.BlockSpec((None, tm, ns), lambda s, i: (s, i, 0)), jax.ShapeDtypeStruct((N_CHIPS, D, ns), BF16))


def _rms_fwd(x, g, name):
    t, d = x.shape
    tr = 512

    def body(x_ref, g_ref, o_ref):
        xv = x_ref[...]
        r = lax.rsqrt(jnp.mean(xv * xv, axis=1, keepdims=True) + EPS)
        o_ref[...] = (xv * r * g_ref[...]).astype(o_ref.dtype)

    row = pl.BlockSpec((tr, d), lambda i: (i, 0))
    return _tc_call(
        body, name=name, grid=(t // tr,), in_specs=[row, pl.BlockSpec((1, d), lambda i: (0, 0))],
        out_specs=row, out_shape=jax.ShapeDtypeStruct((t, d), BF16), compiler_params=_cp("parallel"),
    )(x, g)


def _rms_bwd_math(xv, g, dy):
    r = lax.rsqrt(jnp.mean(xv * xv, axis=1, keepdims=True) + EPS)
    xh = xv * r
    gy = dy * g
    dx = r * (gy - xh * jnp.mean(gy * xh, axis=1, keepdims=True))
    dg = jnp.sum(dy * xh, axis=0, keepdims=True)
    return dx, dg


def _rot_half(x):
    lane = lax.broadcasted_iota(jnp.int32, x.shape, 1)
    return jnp.where((lane % QK_ROPE) < QK_ROPE // 2, -pltpu.roll(x, LANES - 32, axis=1),
                     pltpu.roll(x, 32, axis=1))


def _rope_fwd_math(x, cos, sin):
    return x * cos + _rot_half(x) * sin


def _rope_bwd_math(dy, cos, sin):
    return dy * cos - _rot_half(dy * sin)


def _rms_rows(x, g):
    return x * lax.rsqrt(jnp.mean(x * x, axis=1, keepdims=True) + EPS) * g


def _attn_prep(h, g_attn, g_kvin, w_dq, g_ql, w_uq, w_kv, g_kvl, w_ukv, cos, sin):
    t, d = h.shape
    tr = 256
    wq = N_HEADS * HEAD_PAD

    def body(h_ref, ga_ref, gk_ref, wdq_ref, gq_ref, wuq_ref, wkv_ref, gl_ref, wukv_ref, c_ref, s_ref,
             hn_ref, hk_ref, cqp_ref, cq_ref, q_ref, kvp_ref, ckv_ref, kr_ref, knv_ref):
        xv, cv, sv = h_ref[...], c_ref[...], s_ref[...]
        xh = xv * lax.rsqrt(jnp.mean(xv * xv, axis=1, keepdims=True) + EPS)
        hn = (xh * ga_ref[...]).astype(BF16)
        hk = (xh * gk_ref[...]).astype(BF16)
        hn_ref[...], hk_ref[...] = hn, hk
        cq_pre = jnp.dot(hn, wdq_ref[...], preferred_element_type=F32)
        cqp_ref[...] = cq_pre
        cq = _rms_rows(cq_pre, gq_ref[...]).astype(BF16)
        cq_ref[...] = cq
        for hd in range(N_HEADS):
            lo = hd * HEAD_PAD
            qh = jnp.dot(cq, wuq_ref[:, lo:lo + HEAD_PAD], preferred_element_type=F32)
            q_ref[:, lo:lo + QK_NOPE] = qh[:, :QK_NOPE].astype(BF16)
            q_ref[:, lo + QK_NOPE:lo + HEAD_PAD] = _rope_fwd_math(qh[:, QK_NOPE:], cv, sv).astype(BF16)
        kvpre = jnp.dot(hk, wkv_ref[...], preferred_element_type=F32)
        kvp_ref[...] = kvpre
        ckv = _rms_rows(kvpre[:, :KV_LORA], gl_ref[...]).astype(BF16)
        ckv_ref[...] = ckv
        kr_ref[...] = _rope_fwd_math(kvpre[:, KV_LORA:], cv, sv).astype(BF16)
        for p in range(2):
            knv_ref[p] = jnp.dot(ckv, wukv_ref[p], preferred_element_type=F32).astype(BF16)

    rows = lambda w: pl.BlockSpec((tr, w), lambda i: (i, 0))
    whole = lambda a: pl.BlockSpec(a.shape, lambda i: (0,) * a.ndim)
    sds = lambda w, dt: jax.ShapeDtypeStruct((t, w), dt)
    args = (h, g_attn, g_kvin, w_dq, g_ql, w_uq, w_kv, g_kvl, w_ukv, cos, sin)
    return _tc_call(
        body, name="attn_prep", grid=(t // tr,),
        in_specs=[rows(d)] + [whole(a) for a in args[1:9]] + [rows(LANES), rows(LANES)],
        out_specs=[rows(d), rows(d), rows(Q_LORA), rows(Q_LORA), rows(wq), rows(KVP), rows(KV_LORA), rows(LANES),
                   pl.BlockSpec((2, tr, N_HEADS * QK_NOPE), lambda i: (0, i, 0))],
        out_shape=[sds(d, BF16), sds(d, BF16), sds(Q_LORA, F32), sds(Q_LORA, BF16), sds(wq, BF16), sds(KVP, F32),
                   sds(KV_LORA, BF16), sds(LANES, BF16), jax.ShapeDtypeStruct((2, t, N_HEADS * QK_NOPE), BF16)],
        compiler_params=_cp("parallel"),
    )(*args)


def _attn_prep_bwd(dq, dknv, dkr, dh, h, hn, hk, cq_pre, cq, kvpre, ckv, g_attn, g_kvin, w_dq, g_ql, w_uq, w_kv, g_kvl,
                   w_ukv, cos, sin):
    t, d = h.shape
    tr = 256
    n_steps = t // tr
    wq = N_HEADS * HEAD_PAD
    wk = N_HEADS * QK_NOPE

    def body(dq_ref, dknv_ref, dkr_ref, dh_ref, h_ref, hn_ref, hk_ref, cqp_ref, cq_ref, kvp_ref, ckv_ref,
             ga_ref, gk_ref, wdq_ref, gq_ref, wuq_ref, wkv_ref, gl_ref, wukv_ref, c_ref, s_ref,
             dho_ref, dhb_ref, dwuq_ref, dwdq_ref, dwukv_ref, dwkv_ref, dga_ref, dgk_ref, dgq_ref, dgl_ref,
             a_uq, a_dq, a_ukv, a_kv):
        i = pl.program_id(0)

        @pl.when(i == 0)
        def _():
            for ref in (a_uq, a_dq, a_ukv, a_kv, dga_ref, dgk_ref, dgq_ref, dgl_ref):
                ref[...] = jnp.zeros_like(ref)

        dqv = dq_ref[...]
        dcq = lax.dot_general(dqv, wuq_ref[...], NT_DIMS, preferred_element_type=F32)
        a_uq[...] += lax.dot_general(cq_ref[...], dqv, TN_DIMS, preferred_element_type=F32)
        dcq_pre, dg = _rms_bwd_math(cqp_ref[...], gq_ref[...], dcq)
        dgq_ref[...] += dg
        dcq_pre = dcq_pre.astype(BF16)
        dhn = lax.dot_general(dcq_pre, wdq_ref[...], NT_DIMS, preferred_element_type=F32)
        a_dq[...] += lax.dot_general(hn_ref[...], dcq_pre, TN_DIMS, preferred_element_type=F32)
        dckv = None
        for p in range(2):
            dk = dknv_ref[p].astype(BF16)
            part = lax.dot_general(dk, wukv_ref[p], NT_DIMS, preferred_element_type=F32)
            dckv = part if dckv is None else dckv + part
            a_ukv[p] += lax.dot_general(ckv_ref[...], dk, TN_DIMS, preferred_element_type=F32)
        dlat, dg = _rms_bwd_math(kvp_ref[:, :KV_LORA], gl_ref[...], dckv)
        dgl_ref[...] += dg
        dkr_pre = _rope_bwd_math(dkr_ref[...], c_ref[...], s_ref[...])
        dkvpre = jnp.concatenate([dlat, dkr_pre], axis=1).astype(BF16)
        dhk = lax.dot_general(dkvpre, wkv_ref[...], NT_DIMS, preferred_element_type=F32)
        a_kv[...] += lax.dot_general(hk_ref[...], dkvpre, TN_DIMS, preferred_element_type=F32)
        xv = h_ref[...]
        dx1, dg = _rms_bwd_math(xv, ga_ref[...], dhn)
        dga_ref[...] += dg
        dx2, dg = _rms_bwd_math(xv, gk_ref[...], dhk)
        dgk_ref[...] += dg
        dh_new = dh_ref[...] + dx1 + dx2
        dho_ref[...] = dh_new
        dhb_ref[...] = dh_new.astype(BF16)

        @pl.when(i == n_steps - 1)
        def _():
            dwuq_ref[...] = a_uq[...].astype(BF16)
            dwdq_ref[...] = a_dq[...].astype(BF16)
            dwukv_ref[...] = a_ukv[...].astype(BF16)
            dwkv_ref[...] = a_kv[...].astype(BF16)

    rows = lambda w: pl.BlockSpec((tr, w), lambda i: (i, 0))
    whole = lambda shape: pl.BlockSpec(shape, lambda i: (0,) * len(shape))
    weights = (g_attn, g_kvin, w_dq, g_ql, w_uq, w_kv, g_kvl, w_ukv)
    dw_shapes = [(Q_LORA, wq), (d, Q_LORA), (2, KV_LORA, wk), (d, KVP)]
    dg_shapes = [(1, d), (1, d), (1, Q_LORA), (1, KV_LORA)]
    return _tc_call(
        body, name="attn_prep_bwd", grid=(n_steps,),
        in_specs=[rows(wq), pl.BlockSpec((2, tr, wk), lambda i: (0, i, 0)), rows(LANES), rows(d), rows(d), rows(d),
                  rows(d), rows(Q_LORA), rows(Q_LORA), rows(KVP), rows(KV_LORA)]
        + [whole(a.shape) for a in weights] + [rows(LANES), rows(LANES)],
        out_specs=[rows(d), rows(d)] + [whole(s) for s in dw_shapes + dg_shapes],
        out_shape=[jax.ShapeDtypeStruct((t, d), F32), jax.ShapeDtypeStruct((t, d), BF16)]
        + [jax.ShapeDtypeStruct(s, BF16) for s in dw_shapes] + [jax.ShapeDtypeStruct(s, F32) for s in dg_shapes],
        scratch_shapes=[pltpu.VMEM(s, F32) for s in dw_shapes], compiler_params=_cp("arbitrary"),
    )(dq, dknv, dkr, dh, h, hn, hk, cq_pre, cq, kvpre, ckv, *weights, cos, sin)


ROW_CHUNK = 64
HALO = 16
WIN = ROW_CHUNK + 16
LANE_HALVES = (slice(0, LANES), slice(LANES, TC))


def _stage(s_ref, p, src):
    t = src.shape[0]
    s_ref[p, :HALO] = jnp.zeros((HALO, TC), BF16)
    s_ref[p, HALO:HALO + t] = src
    s_ref[p, HALO + t:] = jnp.zeros((HALO, TC), BF16)


def _window(s_ref, p, i, lanes):
    base = pl.multiple_of(i * ROW_CHUNK, ROW_CHUNK)
    return s_ref[p, pl.ds(base, ROW_CHUNK + 2 * HALO), lanes].astype(F32)[8:8 + WIN]


def _valid(x):
    return x[8:8 + ROW_CHUNK]


def _prev(x, k):
    return pltpu.roll(x, k, axis=0)


def _next(x, k):
    return pltpu.roll(x, WIN - k, axis=0)


def _taps(w_ref, lanes):
    return w_ref[0:1, lanes], w_ref[1:2, lanes], w_ref[2:3, lanes]


def _fold8(x):
    return jnp.sum(x.reshape(ROW_CHUNK // 8, 8, x.shape[-1]), axis=0)


def _store_rows(ref, idx, i, lanes, x):
    rows = pl.ds(pl.multiple_of(i * ROW_CHUNK, ROW_CHUNK), ROW_CHUNK)
    ref[(*idx, rows, lanes)] = x.astype(ref.dtype)


def _for_chunks(t, chunk):
    def step(i, carry):
        for lanes in LANE_HALVES:
            chunk(i, lanes)
        return carry

    lax.fori_loop(0, t // ROW_CHUNK, step, 0)


def _write_col_sums(acc_ref, outs):
    for k, (ref, row) in enumerate(outs):
        ref[row:row + 1, :] = jnp.sum(acc_ref[k], axis=0, keepdims=True)


def _shift_down(x, k):
    row = lax.broadcasted_iota(jnp.int32, x.shape, 0)
    return jnp.where(row >= k, pltpu.roll(x, k, axis=0), 0.0)


def _shift_up(x, k):
    n = x.shape[0]
    row = lax.broadcasted_iota(jnp.int32, x.shape, 0)
    return jnp.where(row < n - k, pltpu.roll(x, n - k, axis=0), 0.0)


def _conv3(x, w_ref):
    return _shift_down(x, 2) * w_ref[0:1, :] + _shift_down(x, 1) * w_ref[1:2, :] + x * w_ref[2:3, :]


def _col(parts, t):
    if parts is None:
        return pl.BlockSpec((t, TC), lambda j: (0, j))
    return pl.BlockSpec((parts, t, TC), lambda j: (0, 0, j))


def _staging(parts, t):
    return pltpu.VMEM((parts, t + 2 * HALO, TC), BF16)


def _scmix_fwd(z, w):
    t = z.shape[1]

    def body(z_ref, w_ref, m_ref):
        b, c, u = (z_ref[p].astype(F32) for p in range(3))
        m_ref[...] = (b * _conv3(c * u, w_ref)).astype(BF16)

    return _tc_call(
        body, name="scmix_fwd", grid=(D // TC,), in_specs=[_col(3, t), pl.BlockSpec((3, TC), lambda j: (0, j))],
        out_specs=_col(None, t), out_shape=jax.ShapeDtypeStruct((t, D), BF16), compiler_params=_cp("parallel"),
    )(z, w)


def _scmix_bwd(z, w, dm):
    t = z.shape[1]

    def body(z_ref, w_ref, dm_ref, dz_ref, dw_ref, s_ref, acc_ref):
        for p in range(3):
            _stage(s_ref, p, z_ref[p])
        _stage(s_ref, 3, dm_ref[...])
        acc_ref[...] = jnp.zeros_like(acc_ref)

        def chunk(i, lanes):
            w0, w1, w2 = _taps(w_ref, lanes)
            b, c, u, dm = (_window(s_ref, p, i, lanes) for p in range(4))
            cu = c * u
            cu1, cu2 = _prev(cu, 1), _prev(cu, 2)
            _store_rows(dz_ref, (0,), i, lanes, _valid(dm * (cu2 * w0 + cu1 * w1 + cu * w2)))
            dcv = dm * b
            dcu = dcv * w2 + _next(dcv, 1) * w1 + _next(dcv, 2) * w0
            _store_rows(dz_ref, (1,), i, lanes, _valid(dcu * u))
            _store_rows(dz_ref, (2,), i, lanes, _valid(dcu * c))
            for k, shifted in enumerate((cu2, cu1, cu)):
                acc_ref[k, :, lanes] += _fold8(_valid(dcv * shifted))

        _for_chunks(t, chunk)
        _write_col_sums(acc_ref, [(dw_ref, 0), (dw_ref, 1), (dw_ref, 2)])

    wspec = pl.BlockSpec((3, TC), lambda j: (0, j))
    return _tc_call(
        body, name="scmix_bwd", grid=(D // TC,), in_specs=[_col(3, t), wspec, _col(None, t)],
        out_specs=[_col(3, t), wspec],
        out_shape=[jax.ShapeDtypeStruct((3, t, D), BF16), jax.ShapeDtypeStruct((3, D), F32)],
        scratch_shapes=[_staging(4, t), pltpu.VMEM((3, 8, TC), F32)], compiler_params=_cp("parallel"),
    )(z, w, dm)


def _ffn_up_gate(hf, w_up, w, bias, name):
    t, d = hf.shape
    nb = F_FF // TC

    def body(hf_ref, wg_ref, wv_ref, w_ref, b_ref, up_ref, a_ref, prev_ref):
        @pl.when(pl.program_id(0) == 0)
        def _():
            prev_ref[...] = jnp.zeros_like(prev_ref)

        gc = _conv3(prev_ref[0].astype(F32), w_ref) + b_ref[...]
        a_ref[...] = (gc * jax.nn.sigmoid(gc) * prev_ref[1].astype(F32)).astype(BF16)
        hv = hf_ref[...]
        up_ref[0] = jnp.dot(hv, wg_ref[...], preferred_element_type=F32).astype(BF16)
        up_ref[1] = jnp.dot(hv, wv_ref[...], preferred_element_type=F32).astype(BF16)
        prev_ref[...] = up_ref[...]

    tile = lambda j: jnp.minimum(j, nb - 1)
    gated = lambda j: jnp.maximum(j - 1, 0)
    return _tc_call(
        body, name=name, grid=(nb + 1,),
        in_specs=[pl.BlockSpec((t, d), lambda j: (0, 0)), pl.BlockSpec((d, TC), lambda j: (0, tile(j))),
                  pl.BlockSpec((d, TC), lambda j: (0, nb + tile(j))), pl.BlockSpec((3, TC), lambda j: (0, gated(j))),
                  pl.BlockSpec((1, TC), lambda j: (0, gated(j)))],
        out_specs=[pl.BlockSpec((2, t, TC), lambda j: (0, 0, tile(j))), pl.BlockSpec((t, TC), lambda j: (0, gated(j)))],
        out_shape=[jax.ShapeDtypeStruct((2, t, F_FF), BF16), jax.ShapeDtypeStruct((t, F_FF), BF16)],
        scratch_shapes=[pltpu.VMEM((2, t, TC), BF16)], compiler_params=_cp("arbitrary"),
    )(hf, w_up, w_up, w, bias)


def _gate_bwd(up, w, bias, dh, w_down, name):
    t, d = dh.shape

    def body(u_ref, w_ref, b_ref, dh_ref, wd_ref, du_ref, dw_ref, db_ref, s_ref, acc_ref):
        for p in range(2):
            _stage(s_ref, p, u_ref[p])
        _stage(s_ref, 2, lax.dot_general(dh_ref[...], wd_ref[...], NT_DIMS, preferred_element_type=F32).astype(BF16))
        acc_ref[...] = jnp.zeros_like(acc_ref)

        def chunk(i, lanes):
            w0, w1, w2 = _taps(w_ref, lanes)
            g, v, da = (_window(s_ref, p, i, lanes) for p in range(3))
            g1, g2 = _prev(g, 1), _prev(g, 2)
            gc = g2 * w0 + g1 * w1 + g * w2 + b_ref[:, lanes]
            sg = jax.nn.sigmoid(gc)
            _store_rows(du_ref, (1,), i, lanes, _valid(da * (gc * sg)))
            dgc = da * v * (sg * (1.0 + gc * (1.0 - sg)))
            _store_rows(du_ref, (0,), i, lanes, _valid(dgc * w2 + _next(dgc, 1) * w1 + _next(dgc, 2) * w0))
            for k, shifted in enumerate((g2, g1, g)):
                acc_ref[k, :, lanes] += _fold8(_valid(dgc * shifted))
            acc_ref[3, :, lanes] += _fold8(_valid(dgc))

        _for_chunks(t, chunk)
        _write_col_sums(acc_ref, [(dw_ref, 0), (dw_ref, 1), (dw_ref, 2), (db_ref, 0)])

    wspec = pl.BlockSpec((3, TC), lambda j: (0, j))
    bspec = pl.BlockSpec((1, TC), lambda j: (0, j))
    return _tc_call(
        body, name=name, grid=(F_FF // TC,),
        in_specs=[_col(2, t), wspec, bspec, pl.BlockSpec((t, d), lambda j: (0, 0)), pl.BlockSpec((TC, d), lambda j: (j, 0))],
        out_specs=[_col(2, t), wspec, bspec],
        out_shape=[jax.ShapeDtypeStruct((2, t, F_FF), BF16), jax.ShapeDtypeStruct((3, F_FF), F32),
                   jax.ShapeDtypeStruct((1, F_FF), F32)],
        scratch_shapes=[_staging(3, t), pltpu.VMEM((4, 8, TC), F32)], compiler_params=_cp("parallel"),
    )(up, w, bias, dh, w_down)


ATT_TQ = 256
ATT_SCALE = (QK_NOPE + QK_ROPE) ** -0.5


def _key_ranges(lvl):
    lo = lvl * ATT_TQ
    return ([(0, lo, False)] if lvl else []) + [(lo, lo + ATT_TQ, True)]


FWD_HEADS = 4
BWD_HEADS = 2


def _fill_keys(k_ref, kn_ref, kr_ref):
    @pl.when(pl.program_id(1) == 0)
    def _():
        for hh in range(k_ref.shape[0]):
            k_ref[hh, :, :QK_NOPE] = kn_ref[:, hh * QK_NOPE:(hh + 1) * QK_NOPE]
            k_ref[hh, :, QK_NOPE:] = kr_ref[...]


def _attn_probs(q, k_ref, lvl):
    scores = []
    for lo, hi, diagonal in _key_ranges(lvl):
        s = lax.dot_general(q, k_ref[lo:hi, :], NT_DIMS, preferred_element_type=F32) * ATT_SCALE
        if diagonal:
            row = lax.broadcasted_iota(jnp.int32, s.shape, 0)
            col = lax.broadcasted_iota(jnp.int32, s.shape, 1)
            seen = lax.shift_right_logical(col, CHUNK_SHIFT) <= lax.shift_right_logical(row, CHUNK_SHIFT)
            s = jnp.where(seen, s, NEG_INF)
        scores.append(s)
    m = jnp.max(scores[0], axis=1, keepdims=True)
    for s in scores[1:]:
        m = jnp.maximum(m, jnp.max(s, axis=1, keepdims=True))
    ps = [jnp.exp(s - m) for s in scores]
    total = jnp.sum(ps[0], axis=1, keepdims=True)
    for p in ps[1:]:
        total = total + jnp.sum(p, axis=1, keepdims=True)
    inv = 1.0 / total
    return [p * inv for p in ps]


def _attn_probs_t(q, k_ref, lvl):
    scores = []
    for lo, hi, diagonal in _key_ranges(lvl):
        s = lax.dot_general(k_ref[lo:hi, :], q, NT_DIMS, preferred_element_type=F32) * ATT_SCALE
        if diagonal:
            key = lax.broadcasted_iota(jnp.int32, s.shape, 0)
            qry = lax.broadcasted_iota(jnp.int32, s.shape, 1)
            seen = lax.shift_right_logical(key, CHUNK_SHIFT) <= lax.shift_right_logical(qry, CHUNK_SHIFT)
            s = jnp.where(seen, s, NEG_INF)
        scores.append(s)
    m = jnp.max(scores[0], axis=0, keepdims=True)
    for s in scores[1:]:
        m = jnp.maximum(m, jnp.max(s, axis=0, keepdims=True))
    ps = [jnp.exp(s - m) for s in scores]
    total = jnp.sum(ps[0], axis=0, keepdims=True)
    for p in ps[1:]:
        total = total + jnp.sum(p, axis=0, keepdims=True)
    inv = 1.0 / total
    return [p * inv for p in ps]


def _per_query_block(qi, n_blocks, branch):
    for lvl in range(n_blocks):
        pl.when(qi == lvl)(lambda lvl=lvl: branch(lvl))


def _attn_specs(t, g):
    q = pl.BlockSpec((ATT_TQ, g * HEAD_PAD), lambda h, i: (i, h))
    kn = pl.BlockSpec((None, t, g * QK_NOPE), lambda h, i: (0, 0, h))
    kr = pl.BlockSpec((t, LANES), lambda h, i: (0, 0))
    v = pl.BlockSpec((None, t, g * V_HEAD), lambda h, i: (1, 0, h))
    o = pl.BlockSpec((ATT_TQ, g * V_HEAD), lambda h, i: (i, h))
    return q, kn, kr, v, o


def _attn_fwd(q, knv, kr):
    t = q.shape[0]

    def body(q_ref, kn_ref, kr_ref, v_ref, o_ref, k_ref):
        _fill_keys(k_ref, kn_ref, kr_ref)

        def branch(lvl):
            for hh in range(FWD_HEADS):
                vcols = slice(hh * V_HEAD, (hh + 1) * V_HEAD)
                ps = _attn_probs(q_ref[:, hh * HEAD_PAD:(hh + 1) * HEAD_PAD], k_ref.at[hh], lvl)
                o = None
                for p, (lo, hi, _) in zip(ps, _key_ranges(lvl)):
                    part = jnp.dot(p.astype(BF16), v_ref[lo:hi, vcols], preferred_element_type=F32)
                    o = part if o is None else o + part
                o_ref[:, vcols] = o.astype(BF16)

        _per_query_block(pl.program_id(1), t // ATT_TQ, branch)

    qs, kns, krs, vs, os_ = _attn_specs(t, FWD_HEADS)
    return _tc_call(
        body, name="attn_fwd", grid=(N_HEADS // FWD_HEADS, t // ATT_TQ), in_specs=[qs, kns, krs, vs],
        out_specs=os_, out_shape=jax.ShapeDtypeStruct((t, N_HEADS * V_HEAD), BF16),
        scratch_shapes=[pltpu.VMEM((FWD_HEADS, t, HEAD_PAD), BF16)], compiler_params=_cp("parallel", "arbitrary"),
    )(q, knv, kr, knv)


def _attn_bwd(q, knv, kr, do, cos, sin):
    t = q.shape[0]

    def body(q_ref, kn_ref, kr_ref, v_ref, do_ref, c_ref, s_ref, dq_ref, dknv_ref, dkr_ref, k_ref, dk_ref):
        h, qi = pl.program_id(0), pl.program_id(1)
        _fill_keys(k_ref, kn_ref, kr_ref)

        @pl.when(qi == 0)
        def _():
            dknv_ref[1] = jnp.zeros(dknv_ref.shape[1:], F32)
            dk_ref[...] = jnp.zeros_like(dk_ref)

        @pl.when((qi == 0) & (h == 0))
        def _():
            dkr_ref[...] = jnp.zeros_like(dkr_ref)

        def branch(lvl):
            ranges = _key_ranges(lvl)
            for hh in range(BWD_HEADS):
                qcols = slice(hh * HEAD_PAD, (hh + 1) * HEAD_PAD)
                vcols = slice(hh * V_HEAD, (hh + 1) * V_HEAD)
                qv, dov = q_ref[:, qcols], do_ref[:, vcols]
                ps = _attn_probs_t(qv, k_ref.at[hh], lvl)
                dps = [lax.dot_general(v_ref[lo:hi, vcols], dov, NT_DIMS, preferred_element_type=F32)
                       for lo, hi, _ in ranges]
                di = None
                for p, dp in zip(ps, dps):
                    part = jnp.sum(p * dp, axis=0, keepdims=True)
                    di = part if di is None else di + part
                dq = None
                for p, dp, (lo, hi, _) in zip(ps, dps, ranges):
                    ds = (p * (dp - di) * ATT_SCALE).astype(BF16)
                    part = lax.dot_general(ds, k_ref[hh, lo:hi, :], TN_DIMS, preferred_element_type=F32)
                    dq = part if dq is None else dq + part
                    dk_ref[hh, lo:hi, :] += jnp.dot(ds, qv, preferred_element_type=F32)
                    dknv_ref[1, lo:hi, vcols] += jnp.dot(p.astype(BF16), dov, preferred_element_type=F32)
                dq_ref[:, hh * HEAD_PAD:hh * HEAD_PAD + QK_NOPE] = dq[:, :QK_NOPE].astype(BF16)
                dq_ref[:, hh * HEAD_PAD + QK_NOPE:(hh + 1) * HEAD_PAD] = _rope_bwd_math(
                    dq[:, QK_NOPE:], c_ref[...], s_ref[...]).astype(BF16)

        _per_query_block(qi, t // ATT_TQ, branch)

        @pl.when(qi == t // ATT_TQ - 1)
        def _():
            for hh in range(BWD_HEADS):
                dknv_ref[0, :, hh * QK_NOPE:(hh + 1) * QK_NOPE] = dk_ref[hh, :, :QK_NOPE]
                dkr_ref[...] += dk_ref[hh, :, QK_NOPE:]

    qs, kns, krs, vs, os_ = _attn_specs(t, BWD_HEADS)
    tab = pl.BlockSpec((ATT_TQ, LANES), lambda h, i: (i, 0))
    return _tc_call(
        body, name="attn_bwd", grid=(N_HEADS // BWD_HEADS, t // ATT_TQ), in_specs=[qs, kns, krs, vs, os_, tab, tab],
        out_specs=[qs, pl.BlockSpec((2, t, BWD_HEADS * QK_NOPE), lambda h, i: (0, 0, h)), krs],
        out_shape=[jax.ShapeDtypeStruct((t, N_HEADS * HEAD_PAD), BF16),
                   jax.ShapeDtypeStruct((2, t, N_HEADS * QK_NOPE), F32), jax.ShapeDtypeStruct((t, LANES), F32)],
        scratch_shapes=[pltpu.VMEM((BWD_HEADS, t, HEAD_PAD), BF16), pltpu.VMEM((BWD_HEADS, t, HEAD_PAD), F32)],
        compiler_params=_cp("arbitrary", "arbitrary"),
    )(q, knv, kr, knv, do, cos, sin)


def _adam_math(w, g, m, v):
    nm = ADAM_B1 * m + (1.0 - ADAM_B1) * g
    nv = ADAM_B2 * v + (1.0 - ADAM_B2) * (g * g)
    m_hat = nm / (1.0 - ADAM_B1 ** ADAM_STEP)
    v_hat = nv / (1.0 - ADAM_B2 ** ADAM_STEP)
    return -ADAM_LR * (m_hat / (jnp.sqrt(v_hat) + ADAM_EPS) + ADAM_WD * w), nm, nv


def _adamw_small(ws, gs, ms, vs):
    n = len(ws)

    def body(*refs):
        for i in range(n):
            w_ref, g_ref, m_ref, v_ref = (refs[k * n + i] for k in range(4))
            go_ref, d_ref, nm_ref, nv_ref = (refs[(4 + k) * n + i] for k in range(4))
            go_ref[...] = g_ref[...]
            d_ref[...], nm_ref[...], nv_ref[...] = _adam_math(w_ref[...], g_ref[...], m_ref[...], v_ref[...])

    shapes = [jax.ShapeDtypeStruct(a.shape, F32) for a in ws]
    res = _tc_call(body, name="adamw_small", out_shape=shapes * 4)(*ws, *gs, *ms, *vs)
    return [res[k * n:(k + 1) * n] for k in range(4)]


ADAM_SPLIT = 4


def _store_without_head_padding(dst_ref, g):
    assert HEAD_PAD == 2 * LANES and 2 * (QK_NOPE + QK_ROPE) == 3 * LANES, (HEAD_PAD, QK_NOPE, QK_ROPE)
    assert g.shape[1] % (2 * HEAD_PAD) == 0, g.shape
    low = lax.broadcasted_iota(jnp.int32, (g.shape[0], LANES), 1) < LANES // 2
    for pair in range(g.shape[1] // (2 * HEAD_PAD)):
        t = [g[:, (4 * pair + k) * LANES:(4 * pair + k + 1) * LANES] for k in range(4)]
        moved = [pltpu.roll(t[k], LANES // 2, axis=1) for k in (2, 3)]
        outs = (t[0], jnp.where(low, t[1], moved[0]), jnp.where(low, moved[0], moved[1]))
        for k, o in enumerate(outs):
            dst_ref[:, (3 * pair + k) * LANES:(3 * pair + k + 1) * LANES] = o


def _adamw_shards(ids, items, name):
    n = len(items)

    def body(ids_ref, *refs):
        outs = refs[len(refs) - 4 * n:]
        for i, it in enumerate(items):
            w_ref, m_ref, v_ref, gm_ref, gs_ref = refs[5 * i:5 * i + 5]
            g_ref, d_ref, nm_ref, nv_ref = outs[4 * i:4 * i + 4]
            cols = slice(*it["gcols"]) if it.get("gcols") else slice(None)
            whose = pl.program_id(0) if it.get("owner") is None else it["owner"]
            mine = whose == ids_ref[0]

            def take(src_ref, g_ref=g_ref, cols=cols, head_padded=it.get("head_padded")):
                if head_padded:
                    _store_without_head_padding(g_ref, src_ref[...])
                else:
                    g_ref[...] = src_ref[:, cols]

            @pl.when(mine)
            def _(take=take, gm_ref=gm_ref):
                take(gm_ref)

            @pl.when(jnp.logical_not(mine))
            def _(take=take, gs_ref=gs_ref):
                take(gs_ref)

            d_ref[...], nm_ref[...], nv_ref[...] = _adam_math(w_ref[...], g_ref[...], m_ref[...], v_ref[...])

    in_specs, out_specs, out_shape, args, carried, aliases = [], [], [], [ids], [], {}
    for i, it in enumerate(items):
        w = it["w"]
        r, c = w.shape[-2:]
        tr = r // 2 // ADAM_SPLIT
        assert tr % 8 == 0, (name, w.shape)
        layer = it.get("layer")
        if layer is None:
            wspec = pl.BlockSpec((tr, c), lambda h, k, ids: (h * ADAM_SPLIT + k, 0))
        else:
            wspec = pl.BlockSpec((None, tr, c), lambda h, k, ids, layer=layer: (layer, h * ADAM_SPLIT + k, 0))
        gc = it["g_mine"].shape[1]

        def g_index(of_mine, owner=it.get("owner")):
            def index(h, k, ids):
                if owner is None:
                    half = ids[0] if of_mine else 1 - ids[0]
                    return jnp.where(h == half, k, jnp.where(h < half, 0, ADAM_SPLIT - 1)), 0
                read = (owner == ids[0]) if of_mine else (owner != ids[0])
                return jnp.where(read, h * ADAM_SPLIT + k, 0), 0
            return index

        in_specs += [wspec] * 3 + [pl.BlockSpec((tr, gc), g_index(True)), pl.BlockSpec((tr, gc), g_index(False))]
        args += [w, it["m"], it["v"], it["g_mine"], it["g_sib"]]
        out_specs += [wspec] * 4
        out_shape += [jax.ShapeDtypeStruct(w.shape, F32)] * 4
        if it.get("prev") is not None:
            for k, p in enumerate(it["prev"]):
                aliases[1 + 5 * n + len(carried)] = 4 * i + k
                carried.append(p)
    res = _tc_call(
        body, name=name, prefetch=1, grid=(2, ADAM_SPLIT), in_specs=in_specs + [ANY] * len(carried),
        out_specs=out_specs, out_shape=out_shape, input_output_aliases=aliases,
        compiler_params=_cp("parallel", "parallel"),
    )(*args, *carried)
    return [res[4 * i:4 * i + 4] for i in range(n)]


def _peer_chip(k_me, j):
    return k_me ^ jnp.where(j == 0, 2, jnp.where(j == 1, 1, 3))


def _pair_sums(ids, gs, ras, name):
    n = len(gs)

    def body(ids_ref, *refs):
        for i in range(n):
            g_ref, ra_ref, o_ref = refs[2 * i], refs[2 * i + 1], refs[2 * n + i]
            o_ref[...] = (g_ref[...].astype(F32) + ra_ref[...].astype(F32)).astype(BF16)

    in_specs, out_specs, out_shape = [], [], []
    for g in gs:
        half, c = g.shape[1] // 2, g.shape[2]
        in_specs += [pl.BlockSpec((None, half, c), lambda j, ids: (_peer_chip(ids[1], j), ids[0], 0)),
                     pl.BlockSpec((None, half, c), lambda j, ids: (_peer_chip(ids[1], j), 0, 0))]
        out_specs.append(pl.BlockSpec((None, half, c), lambda j, ids: (j, 0, 0)))
        out_shape.append(jax.ShapeDtypeStruct((3, half, c), BF16))
    return _tc_call(
        body, name=name, prefetch=1, grid=(3,), in_specs=in_specs, out_specs=out_specs, out_shape=out_shape,
        compiler_params=_cp("parallel"),
    )(ids, *[a for pair in zip(gs, ras) for a in pair])


def _chip_sums(ids, gs, ras, rbs, name):
    n = len(gs)

    def body(ids_ref, *refs):
        for i in range(n):
            g_ref, ra_ref, rb_ref, o_ref = refs[3 * i], refs[3 * i + 1], refs[3 * i + 2], refs[3 * n + i]
            acc = g_ref[...].astype(F32) + ra_ref[...].astype(F32)
            for j in range(3):
                acc = acc + rb_ref[j].astype(F32)
            o_ref[...] = acc

    in_specs, out_specs, out_shape = [], [], []
    for g in gs:
        half, c = g.shape[1] // 2, g.shape[2]
        in_specs += [pl.BlockSpec((None, half, c), lambda i, ids: (ids[1], ids[0], 0)),
                     pl.BlockSpec((None, half, c), lambda i, ids: (ids[1], 0, 0)),
                     pl.BlockSpec((3, half, c), lambda i, ids: (0, 0, 0))]
        out_specs.append(pl.BlockSpec((half, c), lambda i, ids: (0, 0)))
        out_shape.append(jax.ShapeDtypeStruct((half, c), F32))
    return _tc_call(
        body, name=name, prefetch=1, grid=(1,), in_specs=in_specs, out_specs=out_specs, out_shape=out_shape,
        compiler_params=_cp("arbitrary"),
    )(ids, *[a for trio in zip(gs, ras, rbs) for a in trio])


def _position():
    x, y, c = lax.axis_index("x"), lax.axis_index("y"), lax.axis_index("c")
    chips = [(1 - x, y), (x, 1 - y), (1 - x, 1 - y)]
    return x, y, c, chips


def _shard_half(ref, wm, h):
    if wm.kind == "tiny":
        return ref
    if wm.nl == 2:
        return ref.at[h]
    return ref.at[pl.ds(pl.multiple_of(h * (wm.k // 2), 16), wm.k // 2), :]


def _region(full, wm, s, h):
    if wm.kind == "tiny":
        return full.at[s]
    cols = pl.ds(pl.multiple_of(s * wm.n, LANES), wm.n) if wm.kind == "col" else slice(None)
    if wm.nl == 2:
        rows = pl.ds(pl.multiple_of(s * wm.k, 16), wm.k) if wm.kind == "row" else slice(None)
        return full.at[slice(None) if h is None else h, rows, cols]
    if wm.kind == "col":
        rows = slice(None) if h is None else pl.ds(pl.multiple_of(h * (wm.k // 2), 16), wm.k // 2)
    elif h is None:
        rows = pl.ds(pl.multiple_of(s * wm.k, 16), wm.k)
    else:
        rows = pl.ds(pl.multiple_of(s * wm.k + h * (wm.k // 2), 16), wm.k // 2)
    return full.at[rows, cols]


def _full_shape(wm):
    if wm.kind == "tiny":
        return (N_CHIPS, wm.k, wm.n)
    shape = (wm.k, N_CHIPS * wm.n) if wm.kind == "col" else (N_CHIPS * wm.k, wm.n)
    return shape if wm.nl == 1 else (wm.nl,) + shape


def _handshake(peers):
    barrier = pltpu.get_barrier_semaphore()
    for peer in peers:
        pl.semaphore_signal(barrier, inc=1, device_id=peer, device_id_type=MESH)
    pl.semaphore_wait(barrier, len(peers))


def _all_gather_group(gi, shards):
    wms = AG_GROUPS[gi]
    nw = len(wms)

    def body(*refs):
        sh, full = refs[:nw], refs[nw:2 * nw]
        ici_s, ici_r, pass_s, pass_r, own_s, own_r = refs[2 * nw:]
        x, y, c, _ = _position()
        me, sibling = 2 * x + y, (x, y, 1 - c)
        first, second, diagonal = (x ^ (1 - c), y ^ c), (x ^ c, y ^ (1 - c)), (1 - x, 1 - y)
        chip_id = lambda chip: 2 * chip[0] + chip[1]
        _handshake([(*first, c), (*second, c), sibling])

        def rcopy(src, dst, s_sem, r_sem, to):
            return pltpu.make_async_remote_copy(src_ref=src, dst_ref=dst, send_sem=s_sem, recv_sem=r_sem,
                                                device_id=to, device_id_type=MESH)

        started = []

        def go(cp):
            cp.start()
            started.append(cp)

        for i, wm in enumerate(wms):
            half, dst = _shard_half(sh[i], wm, c), _region(full[i], wm, me, c)
            go(rcopy(half, dst, ici_s.at[i, 0], ici_r.at[i, 0], (*first, c)))
            go(rcopy(half, dst, ici_s.at[i, 1], ici_r.at[i, 1], (*second, c)))
            go(rcopy(sh[i], _region(full[i], wm, me, None), own_s.at[i], own_r.at[i], sibling))
        for i, wm in enumerate(wms):
            got = _region(full[i], wm, chip_id(first), c)
            rcopy(got, got, ici_s.at[i, 0], ici_r.at[i, 0], sibling).wait_recv()
            go(rcopy(got, got, ici_s.at[i, 2], ici_r.at[i, 2], (*second, c)))
            if wm.kind != "tiny":
                go(rcopy(got, got, pass_s.at[i, 0], pass_r.at[i, 0], sibling))
        for i, wm in enumerate(wms):
            for j, chip in ((1, second), (2, diagonal)):
                got = _region(full[i], wm, chip_id(chip), c)
                rcopy(got, got, ici_s.at[i, j], ici_r.at[i, j], sibling).wait_recv()
                if wm.kind != "tiny":
                    go(rcopy(got, got, pass_s.at[i, j], pass_r.at[i, j], sibling))
        for i, wm in enumerate(wms):
            mine = _region(full[i], wm, me, None)
            rcopy(mine, mine, own_s.at[i], own_r.at[i], sibling).wait_recv()
            if wm.kind != "tiny":
                for j, chip in ((0, second), (1, first), (2, diagonal)):
                    got = _region(full[i], wm, chip_id(chip), 1 - c)
                    rcopy(got, got, pass_s.at[i, j], pass_r.at[i, j], sibling).wait_recv()
        for cp in started:
            cp.wait_send()

    return pl.kernel(
        body, out_type=[jax.ShapeDtypeStruct(_full_shape(wm), s.dtype) for wm, s in zip(wms, shards)],
        mesh=plsc.ScalarSubcoreMesh(axis_name="sequencer", num_cores=1), name=f"ag_group{gi}",
        scratch_types=[pltpu.SemaphoreType.DMA((nw, 3))] * 4 + [pltpu.SemaphoreType.DMA((nw,))] * 2,
        compiler_params=pltpu.CompilerParams(collective_id=gi),
    )(*shards)


def _sequencer_call(body, name, cid, out_types, scratch, args):
    return pl.kernel(
        body, out_type=out_types, mesh=plsc.ScalarSubcoreMesh(axis_name="sequencer", num_cores=1), name=name,
        scratch_types=scratch, compiler_params=pltpu.CompilerParams(collective_id=cid),
    )(*args)


def _pair_exchange(gs, tag, cid):
    n = len(gs)

    def body(*refs):
        g, out, send_sems, recv_sems = refs[:n], refs[n:2 * n], refs[2 * n], refs[2 * n + 1]
        x, y, c, _ = _position()
        _handshake([(x, y, 1 - c)])
        cps = []
        for i in range(n):
            half = g[i].shape[1] // 2
            cps.append(pltpu.make_async_remote_copy(
                src_ref=g[i].at[:, pl.ds(pl.multiple_of((1 - c) * half, 16), half), :], dst_ref=out[i],
                send_sem=send_sems.at[i], recv_sem=recv_sems.at[i], device_id=(x, y, 1 - c), device_id_type=MESH))
            cps[-1].start()
        for cp in cps:
            cp.wait()

    return _sequencer_call(
        body, f"rs_pair_exchange{tag}", cid,
        [jax.ShapeDtypeStruct((a.shape[0], a.shape[1] // 2, a.shape[2]), a.dtype) for a in gs],
        [pltpu.SemaphoreType.DMA((n,)), pltpu.SemaphoreType.DMA((n,))], gs)


def _chip_exchange(ss, tag, cid):
    n = len(ss)

    def body(*refs):
        s, out, send_sems, recv_sems = refs[:n], refs[n:2 * n], refs[2 * n], refs[2 * n + 1]
        x, y, c, chips = _position()
        _handshake([(*chip, c) for chip in chips])
        cps = []
        for i in range(n):
            for j, chip in enumerate(chips):
                cps.append(pltpu.make_async_remote_copy(
                    src_ref=s[i].at[j], dst_ref=out[i].at[j], send_sem=send_sems.at[i, j], recv_sem=recv_sems.at[i, j],
                    device_id=(*chip, c), device_id_type=MESH))
                cps[-1].start()
        for cp in cps:
            cp.wait()

    return _sequencer_call(
        body, f"rs_chip_exchange{tag}", cid, [jax.ShapeDtypeStruct(a.shape, a.dtype) for a in ss],
        [pltpu.SemaphoreType.DMA((n, 3)), pltpu.SemaphoreType.DMA((n, 3))], ss)


def _pair_swap(g8s, tag, cid):
    n = len(g8s)

    def body(*refs):
        g, out, send_sems, recv_sems = refs[:n], refs[n:2 * n], refs[2 * n], refs[2 * n + 1]
        x, y, c, _ = _position()
        _handshake([(x, y, 1 - c)])
        cps = []
        for i in range(n):
            cps.append(pltpu.make_async_remote_copy(
                src_ref=g[i], dst_ref=out[i], send_sem=send_sems.at[i], recv_sem=recv_sems.at[i],
                device_id=(x, y, 1 - c), device_id_type=MESH))
            cps[-1].start()
        for cp in cps:
            cp.wait()

    return _sequencer_call(
        body, f"rs_pair_swap{tag}", cid, [jax.ShapeDtypeStruct(a.shape, a.dtype) for a in g8s],
        [pltpu.SemaphoreType.DMA((n,)), pltpu.SemaphoreType.DMA((n,))], g8s)


def _pair_swap_now(g8s):
    n = len(g8s)

    def body(*refs):
        g, out, send_sems, recv_sems = refs[:n], refs[n:2 * n], refs[2 * n], refs[2 * n + 1]
        x, y, c, _ = _position()
        cps = []
        for i in range(n):
            cps.append(pltpu.make_async_remote_copy(
                src_ref=g[i], dst_ref=out[i], send_sem=send_sems.at[i], recv_sem=recv_sems.at[i],
                device_id=(x, y, 1 - c), device_id_type=MESH))
            cps[-1].start()
        for cp in cps:
            cp.wait()

    return _tc_call(
        body, name="rs_pair_swap_last", in_specs=[ANY] * n, out_specs=[ANY] * n,
        out_shape=[jax.ShapeDtypeStruct(a.shape, a.dtype) for a in g8s],
        scratch_shapes=[pltpu.SemaphoreType.DMA((n,)), pltpu.SemaphoreType.DMA((n,))],
    )(*g8s)


def _all_reduce_small(vecs, owner_major, name):
    n = len(vecs)
    block = lambda i, ref, chip: ref.at[chip] if owner_major[i] else ref
    out_shapes = [a.shape[1:] if owner_major[i] else a.shape for i, a in enumerate(vecs)]

    def body(*refs):
        v, o, gath = refs[:n], refs[n:2 * n], refs[2 * n:3 * n]
        send_sems, recv_sems = refs[3 * n], refs[3 * n + 1]
        x, y, c, _ = _position()
        me = 4 * x + 2 * y + c
        cps = []
        for i in range(n):
            gath[i][me] = block(i, v[i], 2 * x + y)[...]
            for rel in range(1, N_DEV):
                px, py, pc = x ^ (rel >> 2), y ^ ((rel >> 1) & 1), c ^ (rel & 1)
                cps.append(pltpu.make_async_remote_copy(
                    src_ref=block(i, v[i], 2 * px + py), dst_ref=gath[i].at[me], send_sem=send_sems.at[i, rel - 1],
                    recv_sem=recv_sems.at[i, rel - 1], device_id=(px, py, pc), device_id_type=MESH))
        for cp in cps:
            cp.start()
        for i in range(n):
            for rel in range(1, N_DEV):
                pltpu.make_async_remote_copy(
                    src_ref=block(i, v[i], 2 * x + y), dst_ref=gath[i].at[me ^ rel],
                    send_sem=send_sems.at[i, rel - 1], recv_sem=recv_sems.at[i, rel - 1], device_id=(x, y, c),
                    device_id_type=MESH).wait_recv()
        for cp in cps:
            cp.wait_send()
        for i in range(n):
            acc = gath[i][0]
            for d in range(1, N_DEV):
                acc = acc + gath[i][d]
            o[i][...] = acc

    vm = pl.BlockSpec(memory_space=pltpu.VMEM)
    return _tc_call(
        body, name=name, in_specs=[vm] * n, out_specs=[vm] * n,
        out_shape=[jax.ShapeDtypeStruct(s, F32) for s in out_shapes],
        scratch_shapes=[pltpu.VMEM((N_DEV,) + s, F32) for s in out_shapes]
        + [pltpu.SemaphoreType.DMA((n, N_DEV - 1)), pltpu.SemaphoreType.DMA((n, N_DEV - 1))],
    )(*vecs)


def _rope_tables(positions):
    half = QK_ROPE // 2
    inv_freq = 1.0 / (ROPE_THETA ** (jnp.arange(half, dtype=F32) / half))
    ang = positions.astype(F32)[:, None] * inv_freq
    zeros = jnp.zeros((positions.shape[0], LANES - QK_ROPE), F32)
    cos, sin = jnp.cos(ang), jnp.sin(ang)
    return jnp.concatenate([cos, cos, zeros], axis=1), jnp.concatenate([sin, sin, zeros], axis=1)


def _local_step(x, positions, tgt, wf, small, rs):
    cos, sin = _rope_tables(positions)
    w_in, w_out = wf["sc_w_in"], wf["sc_w_out"]
    w_ups, w_downs = (wf["ffn_w_up0"], wf["ffn_w_up1"]), (wf["ffn_w_down0"], wf["ffn_w_down1"])
    w_kv, w_ukv, w_dq, w_uq, w_o = wf["w_kv"], wf["w_ukv"], wf["w_dq"], wf["w_uq"], wf["w_o"]
    attn_norm, ffn_norm = small["attn_norm"], small["ffn_norm"]
    conv_b = small["ffn_conv_b"]

    def ffn_fwd(h, hf, l, then):
        up, a = _ffn_up_gate(hf, w_ups[l], small["ffn_conv_w"][l], conv_b[l:l + 1], f"ffn{l}_up_gate")
        return then(a, w_downs[l], h), (hf, up, a)

    def ffn_bwd(h, dh_out, dh_out_b, l, saved, gi, hooks):
        run = lambda stage: hooks.get(stage, lambda: None)()
        hf, up, a = saved
        d_down = _tn(f"ffn{l}_down_dw", a, dh_out_b, BF16)
        run("down_dw")
        dup, d_cw, d_cb = _gate_bwd(up, small["ffn_conv_w"][l], conv_b[l:l + 1], dh_out_b, w_downs[l],
                                    f"ffn{l}_gate_bwd")
        run("gate_bwd")
        d_up = _dw_ffn_up(f"ffn{l}_up_dw", hf, dup)
        rs.start(gi, {f"ffn_w_down{l}": d_down.reshape(N_CHIPS, F_FF // N_CHIPS, D), f"ffn_w_up{l}": d_up})
        run("up_dw")
        dh, dh_b, d_norm = _dx_norm_bwd(f"ffn{l}_up_dx", dup, w_ups[l], h, ffn_norm[l:l + 1], dh_out)
        run("up_dx")
        return dh, dh_b, d_cw, d_cb, d_norm

    hn0 = _rms_fwd(x, attn_norm[0:1], "attn0_norm")
    z = _nn_parts("sc_in", hn0, w_in, 3, BF16)
    mix = _scmix_fwd(z, small["sc_conv_w"])
    h1, hf0 = _nn_add_norm("sc_out", mix, w_out, x, ffn_norm[0:1])
    h2, ffn0_saved = ffn_fwd(h1, hf0, 0, lambda a, w, h: _nn("ffn0_down", a, w, F32, add=h))

    hn1, hk, cq_pre, cq, q, kvpre, ckv, kr, knv = _attn_prep(
        h2, attn_norm[1:2], small["kv_in_norm"], w_dq, small["q_latent_norm"], w_uq, w_kv, small["kv_latent_norm"],
        w_ukv, cos, sin)
    o = _attn_fwd(q, knv, kr)
    h3, hf1 = _nn_add_norm("attn_out", o, w_o, h2, ffn_norm[1:2])
    (loss, dh4, dh4_b, d_final), ffn1_saved = ffn_fwd(
        h3, hf1, 1, lambda a, w, h: _nn_add_loss("ffn1_down_loss", a, w, h, small["final_norm"], tgt))

    rows = D // N_CHIPS
    dh3, dh3_b, d_cw1, d_cb1, d_fn1 = ffn_bwd(h3, dh4, dh4_b, 1, ffn1_saved, 0, {})

    do = _nt("attn_out_dx", dh3_b, w_o, BF16)
    d_wo = _tn("attn_out_dw", o, dh3_b, BF16)
    rs.pair_sums(0)
    dq, dknv, dkr = _attn_bwd(q, knv, kr, do, cos, sin)
    rs.chip_sums(0)
    dh2, dh2_b, d_wuq, d_wdq, d_wukv, d_wkv, d_an1, d_kvin, d_qln, d_kvln = _attn_prep_bwd(
        dq, dknv, dkr, dh3, h2, hn1, hk, cq_pre, cq, kvpre, ckv, attn_norm[1:2], small["kv_in_norm"], w_dq,
        small["q_latent_norm"], w_uq, w_kv, small["kv_latent_norm"], w_ukv, cos, sin)
    rs.finish(0)
    by_owner = lambda dw: dw.reshape(dw.shape[0], N_CHIPS, -1).transpose(1, 0, 2)
    rs.start(1, {
        "w_o": d_wo.reshape(N_CHIPS, rows, D), "w_uq": by_owner(d_wuq), "w_dq": d_wdq.reshape(N_CHIPS, rows, Q_LORA),
        "w_ukv": by_owner(d_wukv.reshape(2 * KV_LORA, -1)).reshape(N_CHIPS, 2 * KV_LORA, -1),
        "w_kv": d_wkv.reshape(N_CHIPS, rows, KVP),
    })

    dh1, dh1_b, d_cw0, d_cb0, d_fn0 = ffn_bwd(h1, dh2, dh2_b, 0, ffn0_saved, 2, {
        "down_dw": lambda: rs.pair_sums(1), "gate_bwd": lambda: rs.chip_sums(1),
        "up_dw": lambda: (rs.finish(1), rs.pair_sums(2))})

    d_wout = _tn("sc_out_dw", mix, dh1_b, BF16)
    dmix = _nt("sc_out_dx", dh1_b, w_out, BF16)
    dz, d_scw = _scmix_bwd(z, small["sc_conv_w"], dmix)
    d_win = _dw_sc_in(hn0, dz)
    rs.start(3, {"sc_w_out": d_wout.reshape(N_CHIPS, rows, D), "sc_w_in": d_win})
    dx, _, d_an0 = _dx_norm_bwd("sc_in_dx", dz, w_in, x, attn_norm[0:1], dh1)

    taps_by_owner = lambda per_layer: jnp.stack(per_layer, axis=1).reshape(3, len(per_layer), N_CHIPS, -1).transpose(2, 0, 1, 3)
    small_g = {
        "attn_norm": jnp.concatenate([d_an0, d_an1]), "ffn_norm": jnp.concatenate([d_fn0, d_fn1]),
        "final_norm": d_final, "kv_in_norm": d_kvin, "kv_latent_norm": d_kvln, "q_latent_norm": d_qln,
        "ffn_conv_b": jnp.concatenate([d_cb0, d_cb1]),
        "sc_conv_w": taps_by_owner([d_scw]), "ffn_conv_w": taps_by_owner([d_cw0, d_cw1]),
    }
    return loss, dx, small_g


RS_GROUPS = (("ffn_w_down1", "ffn_w_up1"), ("w_o", "w_uq", "w_dq", "w_ukv", "w_kv"),
             ("ffn_w_down0", "ffn_w_up0"), ("sc_w_out", "sc_w_in"))


class _ReduceScatter:
    def __init__(self, ids, finish):
        self.ids, self.grads, self.step, self.mine, self.sib, self.finish = ids, {}, {}, {}, {}, finish

    def _cid(self, gi):
        return len(AG_GROUPS) + 3 * gi

    def start(self, gi, grads):
        self.grads.update(grads)
        own = [grads[n] for n in RS_GROUPS[gi]]
        self.step[gi] = (own, _pair_exchange(own, gi, self._cid(gi)))

    def pair_sums(self, gi):
        own, ra = self.step[gi]
        sums = _pair_sums(self.ids, own, ra, f"rs_pair_sums{gi}")
        self.step[gi] = (own, ra, _chip_exchange(sums, gi, self._cid(gi) + 1))

    def chip_sums(self, gi):
        own, ra, rb = self.step[gi]
        mine = _chip_sums(self.ids, own, ra, rb, f"rs_chip_sums{gi}")
        self.mine.update(zip(RS_GROUPS[gi], mine))
        last = gi == len(RS_GROUPS) - 1
        swapped = _pair_swap_now(mine) if last else _pair_swap(mine, gi, self._cid(gi) + 2)
        self.sib.update(zip(RS_GROUPS[gi], swapped))


SMALL_REPL = ("attn_norm", "ffn_norm", "final_norm", "kv_in_norm", "kv_latent_norm", "q_latent_norm", "ffn_conv_b")


def _pad_heads(w_uq):
    per_head = w_uq.reshape(Q_LORA, -1, QK_NOPE + QK_ROPE)
    return jnp.pad(per_head, ((0, 0), (0, 0), (0, HEAD_PAD - QK_NOPE - QK_ROPE))).reshape(Q_LORA, -1)


def _pack_kv(w_dkv, w_kr):
    return jnp.concatenate([w_dkv, w_kr, jnp.zeros((w_kr.shape[0], LANES - QK_ROPE), w_kr.dtype)], axis=1)


def kernel(x, positions, attn_norm, ffn_norm, final_norm, sc_w_in, sc_conv_w, sc_w_out, kv_in_norm, w_dkv, kv_latent_norm, w_kr, w_uk, w_uv, w_dq, q_latent_norm, w_uq, w_o, ffn_w_up, ffn_conv_w, ffn_conv_b, ffn_w_down, loss_target, m_attn_norm, m_ffn_norm, m_final_norm, m_sc_w_in, m_sc_conv_w, m_sc_w_out, m_kv_in_norm, m_w_dkv, m_kv_latent_norm, m_w_kr, m_w_uk, m_w_uv, m_w_dq, m_q_latent_norm, m_w_uq, m_w_o, m_ffn_w_up, m_ffn_conv_w, m_ffn_conv_b, m_ffn_w_down, v_attn_norm, v_ffn_norm, v_final_norm, v_sc_w_in, v_sc_conv_w, v_sc_w_out, v_kv_in_norm, v_w_dkv, v_kv_latent_norm, v_w_kr, v_w_uk, v_w_uv, v_w_dq, v_q_latent_norm, v_w_uq, v_w_o, v_ffn_w_up, v_ffn_conv_w, v_ffn_conv_b, v_ffn_w_down):
    names = ("attn_norm", "ffn_norm", "final_norm", "sc_w_in", "sc_conv_w", "sc_w_out", "kv_in_norm", "w_dkv",
             "kv_latent_norm", "w_kr", "w_uk", "w_uv", "w_dq", "q_latent_norm", "w_uq", "w_o", "ffn_w_up",
             "ffn_conv_w", "ffn_conv_b", "ffn_w_down")
    w = dict(zip(names, (attn_norm, ffn_norm, final_norm, sc_w_in, sc_conv_w, sc_w_out, kv_in_norm, w_dkv,
                         kv_latent_norm, w_kr, w_uk, w_uv, w_dq, q_latent_norm, w_uq, w_o, ffn_w_up,
                         ffn_conv_w, ffn_conv_b, ffn_w_down)))
    m = dict(zip(names, (m_attn_norm, m_ffn_norm, m_final_norm, m_sc_w_in, m_sc_conv_w, m_sc_w_out, m_kv_in_norm,
                         m_w_dkv, m_kv_latent_norm, m_w_kr, m_w_uk, m_w_uv, m_w_dq, m_q_latent_norm, m_w_uq, m_w_o,
                         m_ffn_w_up, m_ffn_conv_w, m_ffn_conv_b, m_ffn_w_down)))
    v = dict(zip(names, (v_attn_norm, v_ffn_norm, v_final_norm, v_sc_w_in, v_sc_conv_w, v_sc_w_out, v_kv_in_norm,
                         v_w_dkv, v_kv_latent_norm, v_w_kr, v_w_uk, v_w_uv, v_w_dq, v_q_latent_norm, v_w_uq, v_w_o,
                         v_ffn_w_up, v_ffn_conv_w, v_ffn_conv_b, v_ffn_w_down)))

    _ORDER[0] = None
    ix, iy, ic = lax.axis_index("x"), lax.axis_index("y"), lax.axis_index("c")
    chip = 2 * ix + iy
    ids = jnp.stack([ic, chip]).astype(jnp.int32)

    ws = {
        "sc_w_in": sc_w_in[0], "sc_w_out": sc_w_out[0], "ffn_w_up": ffn_w_up, "ffn_w_down": ffn_w_down,
        "w_kv": _pack_kv(w_dkv, w_kr), "w_ukv": jnp.stack([w_uk, w_uv]), "w_dq": w_dq[0],
        "w_uq": _pad_heads(w_uq[0]), "w_o": w_o[0],
    }

    def ag_shard(name):
        if name == "sc_conv_w":
            return sc_conv_w[0]
        if name == "ffn_conv_w":
            return ffn_conv_w.reshape(6, -1)
        if name[:-1] in ("ffn_w_up", "ffn_w_down"):
            return ws[name[:-1]][int(name[-1])].astype(BF16)
        return ws[name].astype(BF16)

    wf = {}
    for gi, wms in enumerate(AG_GROUPS):
        fulls = _all_gather_group(gi, [ag_shard(wm.name) for wm in wms])
        wf.update({wm.name: f for wm, f in zip(wms, fulls)})
    small = {
        "attn_norm": attn_norm, "ffn_norm": ffn_norm, "final_norm": final_norm[None], "kv_in_norm": kv_in_norm[None],
        "kv_latent_norm": kv_latent_norm[None], "q_latent_norm": q_latent_norm, "ffn_conv_b": ffn_conv_b,
        "sc_conv_w": wf["sc_conv_w"].transpose(1, 0, 2).reshape(3, D),
        "ffn_conv_w": wf["ffn_conv_w"].reshape(N_CHIPS, 2, 3, -1).transpose(1, 2, 0, 3).reshape(2, 3, F_FF),
    }

    res = {}

    held = {
        "ffn_w_up0": [("ffn_w_up", dict(layer=0))], "ffn_w_up1": [("ffn_w_up", dict(layer=1))],
        "ffn_w_down0": [("ffn_w_down", dict(layer=0))], "ffn_w_down1": [("ffn_w_down", dict(layer=1))],
        "sc_w_in": [("sc_w_in", dict(layer=0))], "sc_w_out": [("sc_w_out", dict(layer=0))],
        "w_dq": [("w_dq", dict(layer=0))], "w_o": [("w_o", dict(layer=0))], "w_uq": [("w_uq", dict(layer=0, head_padded=True))],
        "w_kv": [("w_dkv", dict(gcols=(0, KV_LORA))), ("w_kr", dict(gcols=(KV_LORA, KV_LORA + QK_ROPE)))],
        "w_ukv": [("w_uk", dict(owner=0)), ("w_uv", dict(owner=1))],
    }

    def adamw_group(gi):
        items = []
        for key in RS_GROUPS[gi]:
            for n, opts in held[key]:
                items.append(dict(name=n, w=w[n], m=m[n], v=v[n], g_mine=rs.mine[key], g_sib=rs.sib[key],
                                  prev=res.get(n) if "layer" in opts and w[n].shape[0] > 1 else None, **opts))
        for it, out in zip(items, _adamw_shards(ids, items, f"adamw_group{gi}")):
            res[it["name"]] = out

    rs = _ReduceScatter(ids, adamw_group)
    loss, dx, small_g = _local_step(x[0], positions[0], loss_target[0], wf, small, rs)

    rs.chip_sums(2)
    rs.pair_sums(3)

    s_names = list(small_g)
    reduced = _all_reduce_small([small_g[n] for n in s_names] + [loss], [small_g[n].ndim == 4 for n in s_names] + [False],
                                "ar_small")
    sg, loss_out = dict(zip(s_names, reduced[:-1])), reduced[-1][0, 0]

    row = lambda n: (lambda t: t[n][None])
    taps = lambda n: (lambda t: t[n].transpose(1, 0, 2))
    small_2d = {
        "attn_norm": (sg["attn_norm"], lambda t: t["attn_norm"]), "ffn_norm": (sg["ffn_norm"], lambda t: t["ffn_norm"]),
        "final_norm": (sg["final_norm"], row("final_norm")), "kv_in_norm": (sg["kv_in_norm"], row("kv_in_norm")),
        "kv_latent_norm": (sg["kv_latent_norm"], row("kv_latent_norm")),
        "q_latent_norm": (sg["q_latent_norm"], lambda t: t["q_latent_norm"]),
        "ffn_conv_b": (sg["ffn_conv_b"], lambda t: t["ffn_conv_b"]),
        "sc_conv_w": (sg["sc_conv_w"], taps("sc_conv_w")), "ffn_conv_w": (sg["ffn_conv_w"], taps("ffn_conv_w")),
    }
    s_keys = list(small_2d)
    small_grads = [small_2d[k][0] for k in s_keys]
    views = lambda tree: [small_2d[k][1](tree) for k in s_keys]
    small_res = _adamw_small(views(w), small_grads, views(m), views(v))

    def restore(vals):
        by = dict(zip(s_keys, vals))
        out = {n: by[n].reshape(w[n].shape) for n in SMALL_REPL}
        out.update({n: by[n].transpose(1, 0, 2) for n in ("sc_conv_w", "ffn_conv_w")})
        return out

    rs.finish(2)
    rs.chip_sums(3)
    rs.finish(3)
    outs = [restore(vals) for vals in small_res]
    for k, dst in enumerate(outs):
        for n in res:
            dst[n] = res[n][k]
    grads, delta, new_m, new_v = outs

    _ORDER[0] = None
    return (loss_out, dx[None], *[grads[n] for n in names], *[delta[n] for n in names],
            *[new_m[n] for n in names], *[new_v[n] for n in names])
```

```python
from typing import NamedTuple

import jax
import jax.numpy as jnp
from jax import lax
from jax.experimental import pallas as pl
from jax.experimental.pallas import tpu as pltpu
from jax.experimental.pallas import tpu_sc as plsc

F32 = jnp.float32
BF16 = jnp.bfloat16

T = 2048
D = 1024
F_FF = 2816
N_HEADS = 8
QK_NOPE = 128
QK_ROPE = 64
V_HEAD = 128
Q_LORA = 384
KV_LORA = 256
CHUNK_SHIFT = 6
ROPE_THETA = 10000.0
EPS = 1e-6
NEG_INF = -1e30
HEAD_PAD = 256
KVP = KV_LORA + 128

ADAM_LR = 0.001
ADAM_B1 = 0.9
ADAM_B2 = 0.999
ADAM_EPS = 1e-08
ADAM_WD = 0.01
ADAM_STEP = 10

N_CHIPS = 4
N_DEV = 8
LANES = 128
TC = 256
V7X_VMEM_LIMIT = 56 * 1024 * 1024

MESH = pl.DeviceIdType.MESH
ANY = pl.BlockSpec(memory_space=pl.ANY)


class _W(NamedTuple):
    name: str
    kind: str
    nl: int
    k: int
    n: int


AG_GROUPS = (
    (_W("sc_w_in", "col", 1, D, 3 * D // N_CHIPS), _W("sc_conv_w", "tiny", 1, 3, D // N_CHIPS),
     _W("ffn_conv_w", "tiny", 1, 6, F_FF // N_CHIPS), _W("sc_w_out", "row", 1, D // N_CHIPS, D)),
    (_W("ffn_w_up0", "col", 1, D, 2 * F_FF // N_CHIPS),),
    (_W("ffn_w_down0", "row", 1, F_FF // N_CHIPS, D),),
    (_W("w_kv", "row", 1, D // N_CHIPS, KVP), _W("w_ukv", "col", 2, KV_LORA, N_HEADS * QK_NOPE // N_CHIPS),
     _W("w_dq", "row", 1, D // N_CHIPS, Q_LORA),
     _W("w_uq", "col", 1, Q_LORA, N_HEADS * HEAD_PAD // N_CHIPS),
     _W("w_o", "row", 1, N_HEADS * V_HEAD // N_CHIPS, D)),
    (_W("ffn_w_up1", "col", 1, D, 2 * F_FF // N_CHIPS), _W("ffn_w_down1", "row", 1, F_FF // N_CHIPS, D)),
)


def _cp(*sem):
    return pltpu.CompilerParams(dimension_semantics=sem, vmem_limit_bytes=V7X_VMEM_LIMIT)


_ORDER = [None]


def _tc_call(body, *, name, out_shape, in_specs=None, out_specs=None, grid=(), scratch_shapes=(), prefetch=0,
             input_output_aliases=None, compiler_params=None):
    def run(*args):
        specs = [pl.BlockSpec(memory_space=pltpu.VMEM)] * (len(args) - prefetch) if in_specs is None else list(in_specs)
        inner, dep = body, _ORDER[0]
        if dep is not None:
            unread = prefetch + len(specs)
            specs, args = specs + [ANY], (*args, dep)

            def inner(*refs):
                return body(*refs[:unread], *refs[unread + 1:])

        kwargs = dict(name=name, out_shape=out_shape, input_output_aliases=input_output_aliases or {},
                      compiler_params=compiler_params)
        if prefetch:
            kwargs["grid_spec"] = pltpu.PrefetchScalarGridSpec(
                num_scalar_prefetch=prefetch, grid=grid, in_specs=specs, out_specs=out_specs,
                scratch_shapes=scratch_shapes)
        else:
            kwargs.update(grid=grid, in_specs=specs, scratch_shapes=scratch_shapes)
            if out_specs is not None:
                kwargs["out_specs"] = out_specs
        out = pl.pallas_call(inner, **kwargs)(*args)
        _ORDER[0] = out[0] if isinstance(out, (list, tuple)) else out
        return out

    return run


def _tile(n, cands):
    for c in cands:
        if n % c == 0:
            return c
    raise ValueError(f"no tile for {n}")


NN_DIMS = (((1,), (0,)), ((), ()))
NT_DIMS = (((1,), (1,)), ((), ()))
TN_DIMS = (((0,), (0,)), ((), ()))
M_TILES = (1024, 512, 384, 256, 128)
N_TILES = (1408, 1024, 768, 512, 384, 256, 128)
MM_BLOCK_BYTES = 36 * 1024 * 1024


def _fit(m, n, block_bytes, m_tiles=M_TILES, n_tiles=N_TILES, n_first=False):
    tms, tns = [c for c in m_tiles if m % c == 0], [c for c in n_tiles if n % c == 0]
    pairs = [(tm, tn) for tn in tns for tm in tms] if n_first else [(tm, tn) for tm in tms for tn in tns]
    for tm, tn in pairs:
        if 2 * block_bytes(tm, tn) + 4 * tm * tn <= MM_BLOCK_BYTES:
            return tm, tn
    raise ValueError(f"no tiles for {m} x {n}")


def _size(x):
    return x.dtype.itemsize


def _mm(name, a, b, dims, grid, a_spec, b_spec, o_spec, o_sds, add=None, red=None, acc_shape=None):
    n_red = None if red is None else grid[red]

    def body(*refs):
        a_ref, b_ref = refs[0], refs[1]
        add_ref = refs[2] if add is not None else None
        o_ref = refs[3] if add is not None else refs[2]
        part = lax.dot_general(a_ref[...].astype(BF16), b_ref[...].astype(BF16), dims, preferred_element_type=F32)
        if red is None:
            if add is not None:
                part = part + add_ref[...]
            o_ref[...] = part.astype(o_ref.dtype)
            return
        acc_ref = refs[-1]
        r = pl.program_id(red)

        @pl.when(r == 0)
        def _():
            acc_ref[...] = part

        @pl.when(r > 0)
        def _():
            acc_ref[...] += part

        @pl.when(r == n_red - 1)
        def _():
            o_ref[...] = acc_ref[...].astype(o_ref.dtype)

    sem = tuple("arbitrary" if ax == red else "parallel" for ax in range(len(grid)))
    in_specs = [a_spec, b_spec] + ([o_spec] if add is not None else [])
    args = (a, b) + ((add,) if add is not None else ())
    return _tc_call(
        body, name=name, grid=grid, in_specs=in_specs, out_specs=o_spec, out_shape=o_sds,
        scratch_shapes=[] if red is None else [pltpu.VMEM(acc_shape, F32)], compiler_params=_cp(*sem),
    )(*args)


def _nn(name, a, b, out_dtype, add=None, lead=None):
    (m, k), n = a.shape, b.shape[-1]
    osz = jnp.dtype(out_dtype).itemsize + (4 if add is not None else 0)
    tm, tn = _fit(m, n, lambda tm, tn: tm * k * _size(a) + k * tn * _size(b) + tm * tn * osz, n_first=True)
    if lead is None:
        b_spec = pl.BlockSpec((k, tn), lambda i, j: (0, j))
    else:
        b_spec = pl.BlockSpec((None, k, tn), lambda i, j: (lead, 0, j))
    return _mm(name, a, b, NN_DIMS, (m // tm, n // tn), pl.BlockSpec((tm, k), lambda i, j: (i, 0)), b_spec,
               pl.BlockSpec((tm, tn), lambda i, j: (i, j)), jax.ShapeDtypeStruct((m, n), out_dtype), add=add)


def _nn_parts(name, a, b, parts, out_dtype, lead=None, stacked=False):
    m, k = a.shape
    c = b.shape[-1] if stacked else b.shape[-1] // parts
    osz = jnp.dtype(out_dtype).itemsize
    tm, tn = _fit(m, c, lambda tm, tn: tm * k * _size(a) + k * tn * _size(b) + tm * tn * osz)
    nb = c // tn
    if stacked:
        b_spec = pl.BlockSpec((None, k, tn), lambda i, p, j: (p, 0, j))
    elif lead is None:
        b_spec = pl.BlockSpec((k, tn), lambda i, p, j: (0, p * nb + j))
    else:
        b_spec = pl.BlockSpec((None, k, tn), lambda i, p, j: (lead, 0, p * nb + j))
    return _mm(name, a, b, NN_DIMS, (m // tm, parts, nb), pl.BlockSpec((tm, k), lambda i, p, j: (i, 0)), b_spec,
               pl.BlockSpec((None, tm, tn), lambda i, p, j: (p, i, j)), jax.ShapeDtypeStruct((parts, m, c), out_dtype))


def _nt(name, a, b, out_dtype, lead=None):
    (m, k), n = a.shape, b.shape[-2]
    osz = jnp.dtype(out_dtype).itemsize
    tm, tn = _fit(m, n, lambda tm, tn: tm * k * _size(a) + tn * k * _size(b) + tm * tn * osz)
    if lead is None:
        b_spec = pl.BlockSpec((tn, k), lambda i, j: (j, 0))
    else:
        b_spec = pl.BlockSpec((None, tn, k), lambda i, j: (lead, j, 0))
    return _mm(name, a, b, NT_DIMS, (m // tm, n // tn), pl.BlockSpec((tm, k), lambda i, j: (i, 0)), b_spec,
               pl.BlockSpec((tm, tn), lambda i, j: (i, j)), jax.ShapeDtypeStruct((m, n), out_dtype))


def _tn(name, a, b, out_dtype):
    (k, m), n = a.shape, b.shape[1]
    osz = jnp.dtype(out_dtype).itemsize
    tm, tn = _fit(m, n, lambda tm, tn: k * tm * _size(a) + k * tn * _size(b) + tm * tn * osz,
                  m_tiles=(1408, 512, 384, 256, 128), n_tiles=(n,) + N_TILES)
    return _mm(name, a, b, TN_DIMS, (m // tm, n // tn), pl.BlockSpec((k, tm), lambda i, j: (0, i)),
               pl.BlockSpec((k, tn), lambda i, j: (0, j)), pl.BlockSpec((tm, tn), lambda i, j: (i, j)),
               jax.ShapeDtypeStruct((m, n), out_dtype))


def _nn_add_norm(name, a, b, add, g):
    (m, k), n = a.shape, b.shape[1]
    tm = 512

    def body(a_ref, b_ref, add_ref, g_ref, h_ref, hn_ref):
        h = jnp.dot(a_ref[...], b_ref[...], preferred_element_type=F32) + add_ref[...]
        h_ref[...] = h
        hn_ref[...] = _rms_rows(h, g_ref[...]).astype(BF16)

    rows = lambda w: pl.BlockSpec((tm, w), lambda i: (i, 0))
    return _tc_call(
        body, name=name, grid=(m // tm,),
        in_specs=[rows(k), pl.BlockSpec((k, n), lambda i: (0, 0)), rows(n), pl.BlockSpec((1, n), lambda i: (0, 0))],
        out_specs=[rows(n), rows(n)],
        out_shape=[jax.ShapeDtypeStruct((m, n), F32), jax.ShapeDtypeStruct((m, n), BF16)], compiler_params=_cp("parallel"),
    )(a, b, add, g)


def _nn_add_loss(name, a, b, add, g, tgt):
    (m, k), n = a.shape, b.shape[1]
    tm = 512

    def body(a_ref, b_ref, add_ref, g_ref, t_ref, loss_ref, dh_ref, dhb_ref, dg_ref):
        xv = jnp.dot(a_ref[...], b_ref[...], preferred_element_type=F32) + add_ref[...]
        gv = g_ref[...]
        r = lax.rsqrt(jnp.mean(xv * xv, axis=1, keepdims=True) + EPS)
        err = xv * r * gv - t_ref[...]
        part = 0.5 * jnp.sum(jnp.mean(err * err, axis=1, keepdims=True), axis=0, keepdims=True)
        dx, dg = _rms_bwd_math(xv, gv, err * (1.0 / n))
        dh_ref[...] = dx
        dhb_ref[...] = dx.astype(BF16)

        @pl.when(pl.program_id(0) == 0)
        def _():
            dg_ref[...] = jnp.zeros_like(dg_ref)
            loss_ref[...] = jnp.zeros_like(loss_ref)

        dg_ref[...] += dg
        loss_ref[...] += jnp.broadcast_to(part, loss_ref.shape)

    rows = lambda w: pl.BlockSpec((tm, w), lambda i: (i, 0))
    vec = pl.BlockSpec((1, n), lambda i: (0, 0))
    return _tc_call(
        body, name=name, grid=(m // tm,),
        in_specs=[rows(k), pl.BlockSpec((k, n), lambda i: (0, 0)), rows(n), vec, rows(n)],
        out_specs=[pl.BlockSpec((1, LANES), lambda i: (0, 0)), rows(n), rows(n), vec],
        out_shape=[jax.ShapeDtypeStruct((1, LANES), F32), jax.ShapeDtypeStruct((m, n), F32),
                   jax.ShapeDtypeStruct((m, n), BF16), jax.ShapeDtypeStruct((1, n), F32)],
        compiler_params=_cp("arbitrary"),
    )(a, b, add, g, tgt)


def _dx_norm_bwd(name, a, b, x, g, add):
    parts, t, c = a.shape
    d = b.shape[0]
    tm = 256

    def body(a_ref, b_ref, x_ref, g_ref, add_ref, dx_ref, dxb_ref, dg_ref):
        dy = None
        for p in range(parts):
            part = lax.dot_general(a_ref[p], b_ref[:, p * c:(p + 1) * c], NT_DIMS, preferred_element_type=F32)
            dy = part if dy is None else dy + part
        dx, dg = _rms_bwd_math(x_ref[...], g_ref[...], dy)
        dx = dx + add_ref[...]
        dx_ref[...] = dx
        dxb_ref[...] = dx.astype(BF16)

        @pl.when(pl.program_id(0) == 0)
        def _():
            dg_ref[...] = jnp.zeros_like(dg_ref)

        dg_ref[...] += dg

    rows = pl.BlockSpec((tm, d), lambda i: (i, 0))
    vec = pl.BlockSpec((1, d), lambda i: (0, 0))
    return _tc_call(
        body, name=name, grid=(t // tm,),
        in_specs=[pl.BlockSpec((parts, tm, c), lambda i: (0, i, 0)), pl.BlockSpec(b.shape, lambda i: (0, 0)), rows, vec,
                  rows],
        out_specs=[rows, rows, vec],
        out_shape=[jax.ShapeDtypeStruct((t, d), F32), jax.ShapeDtypeStruct((t, d), BF16),
                   jax.ShapeDtypeStruct((1, d), F32)],
        compiler_params=_cp("arbitrary"),
    )(a, b, x, g, add)


def _dw_sc_in(hn, dz):
    t, tn, tm = hn.shape[0], TC, D
    per_part, per_chip = D // tn, 3 * D // N_CHIPS // tn
    return _mm("sc_in_dw", hn, dz, TN_DIMS, (D // tm, 3 * D // tn), pl.BlockSpec((t, tm), lambda i, j: (0, i)),
               pl.BlockSpec((None, t, tn), lambda i, j: (j // per_part, 0, j % per_part)),
               pl.BlockSpec((None, tm, tn), lambda i, j: (j // per_chip, i, j % per_chip)),
               jax.ShapeDtypeStruct((N_CHIPS, D, 3 * D // N_CHIPS), BF16))


def _dw_ffn_up(name, hf, dup):
    t, tm, ns = hf.shape[0], D, 2 * F_FF // N_CHIPS
    return _mm(name, hf, dup, TN_DIMS, (N_CHIPS, D // tm), pl.BlockSpec((t, tm), lambda s, i: (0, i)),
               pl.BlockSpec((None, t, ns), lambda s, i: (s // 2, 0, s % 2)),
               pl.BlockSpec((None, tm, ns), lambda s, i: (s, i, 0)), jax.ShapeDtypeStruct((N_CHIPS, D, ns), BF16))


def _rms_fwd(x, g, name):
    t, d = x.shape
    tr = 512

    def body(x_ref, g_ref, o_ref):
        xv = x_ref[...]
        r = lax.rsqrt(jnp.mean(xv * xv, axis=1, keepdims=True) + EPS)
        o_ref[...] = (xv * r * g_ref[...]).astype(o_ref.dtype)

    row = pl.BlockSpec((tr, d), lambda i: (i, 0))
    return _tc_call(
        body, name=name, grid=(t // tr,), in_specs=[row, pl.BlockSpec((1, d), lambda i: (0, 0))],
        out_specs=row, out_shape=jax.ShapeDtypeStruct((t, d), BF16), compiler_params=_cp("parallel"),
    )(x, g)


def _rms_bwd_math(xv, g, dy):
    r = lax.rsqrt(jnp.mean(xv * xv, axis=1, keepdims=True) + EPS)
    xh = xv * r
    gy = dy * g
    dx = r * (gy - xh * jnp.mean(gy * xh, axis=1, keepdims=True))
    dg = jnp.sum(dy * xh, axis=0, keepdims=True)
    return dx, dg


def _rot_half(x):
    lane = lax.broadcasted_iota(jnp.int32, x.shape, 1)
    return jnp.where((lane % QK_ROPE) < QK_ROPE // 2, -pltpu.roll(x, LANES - 32, axis=1),
                     pltpu.roll(x, 32, axis=1))


def _rope_fwd_math(x, cos, sin):
    return x * cos + _rot_half(x) * sin


def _rope_bwd_math(dy, cos, sin):
    return dy * cos - _rot_half(dy * sin)


def _rms_rows(x, g):
    return x * lax.rsqrt(jnp.mean(x * x, axis=1, keepdims=True) + EPS) * g


def _attn_prep(h, g_attn, g_kvin, w_dq, g_ql, w_uq, w_kv, g_kvl, w_ukv, cos, sin):
    t, d = h.shape
    tr = 256
    wq = N_HEADS * HEAD_PAD

    def body(h_ref, ga_ref, gk_ref, wdq_ref, gq_ref, wuq_ref, wkv_ref, gl_ref, wukv_ref, c_ref, s_ref,
             hn_ref, hk_ref, cqp_ref, cq_ref, q_ref, kvp_ref, ckv_ref, kr_ref, knv_ref):
        xv, cv, sv = h_ref[...], c_ref[...], s_ref[...]
        xh = xv * lax.rsqrt(jnp.mean(xv * xv, axis=1, keepdims=True) + EPS)
        hn = (xh * ga_ref[...]).astype(BF16)
        hk = (xh * gk_ref[...]).astype(BF16)
        hn_ref[...], hk_ref[...] = hn, hk
        cq_pre = jnp.dot(hn, wdq_ref[...], preferred_element_type=F32)
        cqp_ref[...] = cq_pre
        cq = _rms_rows(cq_pre, gq_ref[...]).astype(BF16)
        cq_ref[...] = cq
        for hd in range(N_HEADS):
            lo = hd * HEAD_PAD
            qh = jnp.dot(cq, wuq_ref[:, lo:lo + HEAD_PAD], preferred_element_type=F32)
            q_ref[:, lo:lo + QK_NOPE] = qh[:, :QK_NOPE].astype(BF16)
            q_ref[:, lo + QK_NOPE:lo + HEAD_PAD] = _rope_fwd_math(qh[:, QK_NOPE:], cv, sv).astype(BF16)
        kvpre = jnp.dot(hk, wkv_ref[...], preferred_element_type=F32)
        kvp_ref[...] = kvpre
        ckv = _rms_rows(kvpre[:, :KV_LORA], gl_ref[...]).astype(BF16)
        ckv_ref[...] = ckv
        kr_ref[...] = _rope_fwd_math(kvpre[:, KV_LORA:], cv, sv).astype(BF16)
        for p in range(2):
            knv_ref[p] = jnp.dot(ckv, wukv_ref[p], preferred_element_type=F32).astype(BF16)

    rows = lambda w: pl.BlockSpec((tr, w), lambda i: (i, 0))
    whole = lambda a: pl.BlockSpec(a.shape, lambda i: (0,) * a.ndim)
    sds = lambda w, dt: jax.ShapeDtypeStruct((t, w), dt)
    args = (h, g_attn, g_kvin, w_dq, g_ql, w_uq, w_kv, g_kvl, w_ukv, cos, sin)
    return _tc_call(
        body, name="attn_prep", grid=(t // tr,),
        in_specs=[rows(d)] + [whole(a) for a in args[1:9]] + [rows(LANES), rows(LANES)],
        out_specs=[rows(d), rows(d), rows(Q_LORA), rows(Q_LORA), rows(wq), rows(KVP), rows(KV_LORA), rows(LANES),
                   pl.BlockSpec((2, tr, N_HEADS * QK_NOPE), lambda i: (0, i, 0))],
        out_shape=[sds(d, BF16), sds(d, BF16), sds(Q_LORA, F32), sds(Q_LORA, BF16), sds(wq, BF16), sds(KVP, F32),
                   sds(KV_LORA, BF16), sds(LANES, BF16), jax.ShapeDtypeStruct((2, t, N_HEADS * QK_NOPE), BF16)],
        compiler_params=_cp("parallel"),
    )(*args)


def _attn_prep_bwd(dq, dknv, dkr, dh, h, hn, hk, cq_pre, cq, kvpre, ckv, g_attn, g_kvin, w_dq, g_ql, w_uq, w_kv, g_kvl,
                   w_ukv, cos, sin):
    t, d = h.shape
    tr = 256
    n_steps = t // tr
    wq = N_HEADS * HEAD_PAD
    wk = N_HEADS * QK_NOPE

    def body(dq_ref, dknv_ref, dkr_ref, dh_ref, h_ref, hn_ref, hk_ref, cqp_ref, cq_ref, kvp_ref, ckv_ref,
             ga_ref, gk_ref, wdq_ref, gq_ref, wuq_ref, wkv_ref, gl_ref, wukv_ref, c_ref, s_ref,
             dho_ref, dhb_ref, dwuq_ref, dwdq_ref, dwukv_ref, dwkv_ref, dga_ref, dgk_ref, dgq_ref, dgl_ref,
             a_uq, a_dq, a_ukv, a_kv):
        i = pl.program_id(0)

        @pl.when(i == 0)
        def _():
            for ref in (a_uq, a_dq, a_ukv, a_kv, dga_ref, dgk_ref, dgq_ref, dgl_ref):
                ref[...] = jnp.zeros_like(ref)

        dqv = dq_ref[...]
        dcq = lax.dot_general(dqv, wuq_ref[...], NT_DIMS, preferred_element_type=F32)
        a_uq[...] += lax.dot_general(cq_ref[...], dqv, TN_DIMS, preferred_element_type=F32)
        dcq_pre, dg = _rms_bwd_math(cqp_ref[...], gq_ref[...], dcq)
        dgq_ref[...] += dg
        dcq_pre = dcq_pre.astype(BF16)
        dhn = lax.dot_general(dcq_pre, wdq_ref[...], NT_DIMS, preferred_element_type=F32)
        a_dq[...] += lax.dot_general(hn_ref[...], dcq_pre, TN_DIMS, preferred_element_type=F32)
        dckv = None
        for p in range(2):
            dk = dknv_ref[p].astype(BF16)
            part = lax.dot_general(dk, wukv_ref[p], NT_DIMS, preferred_element_type=F32)
            dckv = part if dckv is None else dckv + part
            a_ukv[p] += lax.dot_general(ckv_ref[...], dk, TN_DIMS, preferred_element_type=F32)
        dlat, dg = _rms_bwd_math(kvp_ref[:, :KV_LORA], gl_ref[...], dckv)
        dgl_ref[...] += dg
        dkr_pre = _rope_bwd_math(dkr_ref[...], c_ref[...], s_ref[...])
        dkvpre = jnp.concatenate([dlat, dkr_pre], axis=1).astype(BF16)
        dhk = lax.dot_general(dkvpre, wkv_ref[...], NT_DIMS, preferred_element_type=F32)
        a_kv[...] += lax.dot_general(hk_ref[...], dkvpre, TN_DIMS, preferred_element_type=F32)
        xv = h_ref[...]
        dx1, dg = _rms_bwd_math(xv, ga_ref[...], dhn)
        dga_ref[...] += dg
        dx2, dg = _rms_bwd_math(xv, gk_ref[...], dhk)
        dgk_ref[...] += dg
        dh_new = dh_ref[...] + dx1 + dx2
        dho_ref[...] = dh_new
        dhb_ref[...] = dh_new.astype(BF16)

        @pl.when(i == n_steps - 1)
        def _():
            dwuq_ref[...] = a_uq[...].astype(BF16)
            dwdq_ref[...] = a_dq[...].astype(BF16)
            dwukv_ref[...] = a_ukv[...].astype(BF16)
            dwkv_ref[...] = a_kv[...].astype(BF16)

    rows = lambda w: pl.BlockSpec((tr, w), lambda i: (i, 0))
    whole = lambda shape: pl.BlockSpec(shape, lambda i: (0,) * len(shape))
    weights = (g_attn, g_kvin, w_dq, g_ql, w_uq, w_kv, g_kvl, w_ukv)
    dw_shapes = [(Q_LORA, wq), (d, Q_LORA), (2, KV_LORA, wk), (d, KVP)]
    dg_shapes = [(1, d), (1, d), (1, Q_LORA), (1, KV_LORA)]
    return _tc_call(
        body, name="attn_prep_bwd", grid=(n_steps,),
        in_specs=[rows(wq), pl.BlockSpec((2, tr, wk), lambda i: (0, i, 0)), rows(LANES), rows(d), rows(d), rows(d),
                  rows(d), rows(Q_LORA), rows(Q_LORA), rows(KVP), rows(KV_LORA)]
        + [whole(a.shape) for a in weights] + [rows(LANES), rows(LANES)],
        out_specs=[rows(d), rows(d)] + [whole(s) for s in dw_shapes + dg_shapes],
        out_shape=[jax.ShapeDtypeStruct((t, d), F32), jax.ShapeDtypeStruct((t, d), BF16)]
        + [jax.ShapeDtypeStruct(s, BF16) for s in dw_shapes] + [jax.ShapeDtypeStruct(s, F32) for s in dg_shapes],
        scratch_shapes=[pltpu.VMEM(s, F32) for s in dw_shapes], compiler_params=_cp("arbitrary"),
    )(dq, dknv, dkr, dh, h, hn, hk, cq_pre, cq, kvpre, ckv, *weights, cos, sin)


ROW_CHUNK = 64
HALO = 16
WIN = ROW_CHUNK + 16
LANE_HALVES = (slice(0, LANES), slice(LANES, TC))


def _stage(s_ref, p, src):
    t = src.shape[0]
    s_ref[p, :HALO] = jnp.zeros((HALO, TC), BF16)
    s_ref[p, HALO:HALO + t] = src
    s_ref[p, HALO + t:] = jnp.zeros((HALO, TC), BF16)


def _window(s_ref, p, i, lanes):
    base = pl.multiple_of(i * ROW_CHUNK, ROW_CHUNK)
    return s_ref[p, pl.ds(base, ROW_CHUNK + 2 * HALO), lanes].astype(F32)[8:8 + WIN]


def _valid(x):
    return x[8:8 + ROW_CHUNK]


def _prev(x, k):
    return pltpu.roll(x, k, axis=0)


def _next(x, k):
    return pltpu.roll(x, WIN - k, axis=0)


def _taps(w_ref, lanes):
    return w_ref[0:1, lanes], w_ref[1:2, lanes], w_ref[2:3, lanes]


def _fold8(x):
    return jnp.sum(x.reshape(ROW_CHUNK // 8, 8, x.shape[-1]), axis=0)


def _store_rows(ref, idx, i, lanes, x):
    rows = pl.ds(pl.multiple_of(i * ROW_CHUNK, ROW_CHUNK), ROW_CHUNK)
    ref[(*idx, rows, lanes)] = x.astype(ref.dtype)


def _for_chunks(t, chunk):
    def step(i, carry):
        for lanes in LANE_HALVES:
            chunk(i, lanes)
        return carry

    lax.fori_loop(0, t // ROW_CHUNK, step, 0)


def _write_col_sums(acc_ref, outs):
    for k, (ref, row) in enumerate(outs):
        ref[row:row + 1, :] = jnp.sum(acc_ref[k], axis=0, keepdims=True)


def _shift_down(x, k):
    row = lax.broadcasted_iota(jnp.int32, x.shape, 0)
    return jnp.where(row >= k, pltpu.roll(x, k, axis=0), 0.0)


def _shift_up(x, k):
    n = x.shape[0]
    row = lax.broadcasted_iota(jnp.int32, x.shape, 0)
    return jnp.where(row < n - k, pltpu.roll(x, n - k, axis=0), 0.0)


def _conv3(x, w_ref):
    return _shift_down(x, 2) * w_ref[0:1, :] + _shift_down(x, 1) * w_ref[1:2, :] + x * w_ref[2:3, :]


def _col(parts, t):
    if parts is None:
        return pl.BlockSpec((t, TC), lambda j: (0, j))
    return pl.BlockSpec((parts, t, TC), lambda j: (0, 0, j))


def _staging(parts, t):
    return pltpu.VMEM((parts, t + 2 * HALO, TC), BF16)


def _scmix_fwd(z, w):
    t = z.shape[1]

    def body(z_ref, w_ref, m_ref):
        b, c, u = (z_ref[p].astype(F32) for p in range(3))
        m_ref[...] = (b * _conv3(c * u, w_ref)).astype(BF16)

    return _tc_call(
        body, name="scmix_fwd", grid=(D // TC,), in_specs=[_col(3, t), pl.BlockSpec((3, TC), lambda j: (0, j))],
        out_specs=_col(None, t), out_shape=jax.ShapeDtypeStruct((t, D), BF16), compiler_params=_cp("parallel"),
    )(z, w)


def _scmix_bwd(z, w, dm):
    t = z.shape[1]

    def body(z_ref, w_ref, dm_ref, dz_ref, dw_ref, s_ref, acc_ref):
        for p in range(3):
            _stage(s_ref, p, z_ref[p])
        _stage(s_ref, 3, dm_ref[...])
        acc_ref[...] = jnp.zeros_like(acc_ref)

        def chunk(i, lanes):
            w0, w1, w2 = _taps(w_ref, lanes)
            b, c, u, dm = (_window(s_ref, p, i, lanes) for p in range(4))
            cu = c * u
            cu1, cu2 = _prev(cu, 1), _prev(cu, 2)
            _store_rows(dz_ref, (0,), i, lanes, _valid(dm * (cu2 * w0 + cu1 * w1 + cu * w2)))
            dcv = dm * b
            dcu = dcv * w2 + _next(dcv, 1) * w1 + _next(dcv, 2) * w0
            _store_rows(dz_ref, (1,), i, lanes, _valid(dcu * u))
            _store_rows(dz_ref, (2,), i, lanes, _valid(dcu * c))
            for k, shifted in enumerate((cu2, cu1, cu)):
                acc_ref[k, :, lanes] += _fold8(_valid(dcv * shifted))

        _for_chunks(t, chunk)
        _write_col_sums(acc_ref, [(dw_ref, 0), (dw_ref, 1), (dw_ref, 2)])

    wspec = pl.BlockSpec((3, TC), lambda j: (0, j))
    return _tc_call(
        body, name="scmix_bwd", grid=(D // TC,), in_specs=[_col(3, t), wspec, _col(None, t)],
        out_specs=[_col(3, t), wspec],
        out_shape=[jax.ShapeDtypeStruct((3, t, D), BF16), jax.ShapeDtypeStruct((3, D), F32)],
        scratch_shapes=[_staging(4, t), pltpu.VMEM((3, 8, TC), F32)], compiler_params=_cp("parallel"),
    )(z, w, dm)


def _ffn_up_gate(hf, w_up, w, bias, name):
    t, d = hf.shape
    nb = F_FF // TC

    def body(hf_ref, wg_ref, wv_ref, w_ref, b_ref, up_ref, a_ref, prev_ref):
        @pl.when(pl.program_id(0) == 0)
        def _():
            prev_ref[...] = jnp.zeros_like(prev_ref)

        gc = _conv3(prev_ref[0].astype(F32), w_ref) + b_ref[...]
        a_ref[...] = (gc * jax.nn.sigmoid(gc) * prev_ref[1].astype(F32)).astype(BF16)
        hv = hf_ref[...]
        up_ref[0] = jnp.dot(hv, wg_ref[...], preferred_element_type=F32).astype(BF16)
        up_ref[1] = jnp.dot(hv, wv_ref[...], preferred_element_type=F32).astype(BF16)
        prev_ref[...] = up_ref[...]

    tile = lambda j: jnp.minimum(j, nb - 1)
    gated = lambda j: jnp.maximum(j - 1, 0)
    return _tc_call(
        body, name=name, grid=(nb + 1,),
        in_specs=[pl.BlockSpec((t, d), lambda j: (0, 0)), pl.BlockSpec((d, TC), lambda j: (0, tile(j))),
                  pl.BlockSpec((d, TC), lambda j: (0, nb + tile(j))), pl.BlockSpec((3, TC), lambda j: (0, gated(j))),
                  pl.BlockSpec((1, TC), lambda j: (0, gated(j)))],
        out_specs=[pl.BlockSpec((2, t, TC), lambda j: (0, 0, tile(j))), pl.BlockSpec((t, TC), lambda j: (0, gated(j)))],
        out_shape=[jax.ShapeDtypeStruct((2, t, F_FF), BF16), jax.ShapeDtypeStruct((t, F_FF), BF16)],
        scratch_shapes=[pltpu.VMEM((2, t, TC), BF16)], compiler_params=_cp("arbitrary"),
    )(hf, w_up, w_up, w, bias)


def _gate_bwd(up, w, bias, dh, w_down, name):
    t, d = dh.shape

    def body(u_ref, w_ref, b_ref, dh_ref, wd_ref, du_ref, dw_ref, db_ref, s_ref, acc_ref):
        for p in range(2):
            _stage(s_ref, p, u_ref[p])
        _stage(s_ref, 2, lax.dot_general(dh_ref[...], wd_ref[...], NT_DIMS, preferred_element_type=F32).astype(BF16))
        acc_ref[...] = jnp.zeros_like(acc_ref)

        def chunk(i, lanes):
            w0, w1, w2 = _taps(w_ref, lanes)
            g, v, da = (_window(s_ref, p, i, lanes) for p in range(3))
            g1, g2 = _prev(g, 1), _prev(g, 2)
            gc = g2 * w0 + g1 * w1 + g * w2 + b_ref[:, lanes]
            sg = jax.nn.sigmoid(gc)
            _store_rows(du_ref, (1,), i, lanes, _valid(da * (gc * sg)))
            dgc = da * v * (sg * (1.0 + gc * (1.0 - sg)))
            _store_rows(du_ref, (0,), i, lanes, _valid(dgc * w2 + _next(dgc, 1) * w1 + _next(dgc, 2) * w0))
            for k, shifted in enumerate((g2, g1, g)):
                acc_ref[k, :, lanes] += _fold8(_valid(dgc * shifted))
            acc_ref[3, :, lanes] += _fold8(_valid(dgc))

        _for_chunks(t, chunk)
        _write_col_sums(acc_ref, [(dw_ref, 0), (dw_ref, 1), (dw_ref, 2), (db_ref, 0)])

    wspec = pl.BlockSpec((3, TC), lambda j: (0, j))
    bspec = pl.BlockSpec((1, TC), lambda j: (0, j))
    return _tc_call(
        body, name=name, grid=(F_FF // TC,),
        in_specs=[_col(2, t), wspec, bspec, pl.BlockSpec((t, d), lambda j: (0, 0)), pl.BlockSpec((TC, d), lambda j: (j, 0))],
        out_specs=[_col(2, t), wspec, bspec],
        out_shape=[jax.ShapeDtypeStruct((2, t, F_FF), BF16), jax.ShapeDtypeStruct((3, F_FF), F32),
                   jax.ShapeDtypeStruct((1, F_FF), F32)],
        scratch_shapes=[_staging(3, t), pltpu.VMEM((4, 8, TC), F32)], compiler_params=_cp("parallel"),
    )(up, w, bias, dh, w_down)


ATT_TQ = 256
ATT_SCALE = (QK_NOPE + QK_ROPE) ** -0.5


def _key_ranges(lvl):
    lo = lvl * ATT_TQ
    return ([(0, lo, False)] if lvl else []) + [(lo, lo + ATT_TQ, True)]


FWD_HEADS = 4
BWD_HEADS = 2


def _fill_keys(k_ref, kn_ref, kr_ref):
    @pl.when(pl.program_id(1) == 0)
    def _():
        for hh in range(k_ref.shape[0]):
            k_ref[hh, :, :QK_NOPE] = kn_ref[:, hh * QK_NOPE:(hh + 1) * QK_NOPE]
            k_ref[hh, :, QK_NOPE:] = kr_ref[...]


def _attn_probs(q, k_ref, lvl):
    scores = []
    for lo, hi, diagonal in _key_ranges(lvl):
        s = lax.dot_general(q, k_ref[lo:hi, :], NT_DIMS, preferred_element_type=F32) * ATT_SCALE
        if diagonal:
            row = lax.broadcasted_iota(jnp.int32, s.shape, 0)
            col = lax.broadcasted_iota(jnp.int32, s.shape, 1)
            seen = lax.shift_right_logical(col, CHUNK_SHIFT) <= lax.shift_right_logical(row, CHUNK_SHIFT)
            s = jnp.where(seen, s, NEG_INF)
        scores.append(s)
    m = jnp.max(scores[0], axis=1, keepdims=True)
    for s in scores[1:]:
        m = jnp.maximum(m, jnp.max(s, axis=1, keepdims=True))
    ps = [jnp.exp(s - m) for s in scores]
    total = jnp.sum(ps[0], axis=1, keepdims=True)
    for p in ps[1:]:
        total = total + jnp.sum(p, axis=1, keepdims=True)
    inv = 1.0 / total
    return [p * inv for p in ps]


def _attn_probs_t(q, k_ref, lvl):
    scores = []
    for lo, hi, diagonal in _key_ranges(lvl):
        s = lax.dot_general(k_ref[lo:hi, :], q, NT_DIMS, preferred_element_type=F32) * ATT_SCALE
        if diagonal:
            key = lax.broadcasted_iota(jnp.int32, s.shape, 0)
            qry = lax.broadcasted_iota(jnp.int32, s.shape, 1)
            seen = lax.shift_right_logical(key, CHUNK_SHIFT) <= lax.shift_right_logical(qry, CHUNK_SHIFT)
            s = jnp.where(seen, s, NEG_INF)
        scores.append(s)
    m = jnp.max(scores[0], axis=0, keepdims=True)
    for s in scores[1:]:
        m = jnp.maximum(m, jnp.max(s, axis=0, keepdims=True))
    ps = [jnp.exp(s - m) for s in scores]
    total = jnp.sum(ps[0], axis=0, keepdims=True)
    for p in ps[1:]:
        total = total + jnp.sum(p, axis=0, keepdims=True)
    inv = 1.0 / total
    return [p * inv for p in ps]


def _per_query_block(qi, n_blocks, branch):
    for lvl in range(n_blocks):
        pl.when(qi == lvl)(lambda lvl=lvl: branch(lvl))


def _attn_specs(t, g):
    q = pl.BlockSpec((ATT_TQ, g * HEAD_PAD), lambda h, i: (i, h))
    kn = pl.BlockSpec((None, t, g * QK_NOPE), lambda h, i: (0, 0, h))
    kr = pl.BlockSpec((t, LANES), lambda h, i: (0, 0))
    v = pl.BlockSpec((None, t, g * V_HEAD), lambda h, i: (1, 0, h))
    o = pl.BlockSpec((ATT_TQ, g * V_HEAD), lambda h, i: (i, h))
    return q, kn, kr, v, o


def _attn_fwd(q, knv, kr):
    t = q.shape[0]

    def body(q_ref, kn_ref, kr_ref, v_ref, o_ref, k_ref):
        _fill_keys(k_ref, kn_ref, kr_ref)

        def branch(lvl):
            for hh in range(FWD_HEADS):
                vcols = slice(hh * V_HEAD, (hh + 1) * V_HEAD)
                ps = _attn_probs(q_ref[:, hh * HEAD_PAD:(hh + 1) * HEAD_PAD], k_ref.at[hh], lvl)
                o = None
                for p, (lo, hi, _) in zip(ps, _key_ranges(lvl)):
                    part = jnp.dot(p.astype(BF16), v_ref[lo:hi, vcols], preferred_element_type=F32)
                    o = part if o is None else o + part
                o_ref[:, vcols] = o.astype(BF16)

        _per_query_block(pl.program_id(1), t // ATT_TQ, branch)

    qs, kns, krs, vs, os_ = _attn_specs(t, FWD_HEADS)
    return _tc_call(
        body, name="attn_fwd", grid=(N_HEADS // FWD_HEADS, t // ATT_TQ), in_specs=[qs, kns, krs, vs],
        out_specs=os_, out_shape=jax.ShapeDtypeStruct((t, N_HEADS * V_HEAD), BF16),
        scratch_shapes=[pltpu.VMEM((FWD_HEADS, t, HEAD_PAD), BF16)], compiler_params=_cp("parallel", "arbitrary"),
    )(q, knv, kr, knv)


def _attn_bwd(q, knv, kr, do, cos, sin):
    t = q.shape[0]

    def body(q_ref, kn_ref, kr_ref, v_ref, do_ref, c_ref, s_ref, dq_ref, dknv_ref, dkr_ref, k_ref, dk_ref):
        h, qi = pl.program_id(0), pl.program_id(1)
        _fill_keys(k_ref, kn_ref, kr_ref)

        @pl.when(qi == 0)
        def _():
            dknv_ref[1] = jnp.zeros(dknv_ref.shape[1:], F32)
            dk_ref[...] = jnp.zeros_like(dk_ref)

        @pl.when((qi == 0) & (h == 0))
        def _():
            dkr_ref[...] = jnp.zeros_like(dkr_ref)

        def branch(lvl):
            ranges = _key_ranges(lvl)
            for hh in range(BWD_HEADS):
                qcols = slice(hh * HEAD_PAD, (hh + 1) * HEAD_PAD)
                vcols = slice(hh * V_HEAD, (hh + 1) * V_HEAD)
                qv, dov = q_ref[:, qcols], do_ref[:, vcols]
                ps = _attn_probs_t(qv, k_ref.at[hh], lvl)
                dps = [lax.dot_general(v_ref[lo:hi, vcols], dov, NT_DIMS, preferred_element_type=F32)
                       for lo, hi, _ in ranges]
                di = None
                for p, dp in zip(ps, dps):
                    part = jnp.sum(p * dp, axis=0, keepdims=True)
                    di = part if di is None else di + part
                dq = None
                for p, dp, (lo, hi, _) in zip(ps, dps, ranges):
                    ds = (p * (dp - di) * ATT_SCALE).astype(BF16)
                    part = lax.dot_general(ds, k_ref[hh, lo:hi, :], TN_DIMS, preferred_element_type=F32)
                    dq = part if dq is None else dq + part
                    dk_ref[hh, lo:hi, :] += jnp.dot(ds, qv, preferred_element_type=F32)
                    dknv_ref[1, lo:hi, vcols] += jnp.dot(p.astype(BF16), dov, preferred_element_type=F32)
                dq_ref[:, hh * HEAD_PAD:hh * HEAD_PAD + QK_NOPE] = dq[:, :QK_NOPE].astype(BF16)
                dq_ref[:, hh * HEAD_PAD + QK_NOPE:(hh + 1) * HEAD_PAD] = _rope_bwd_math(
                    dq[:, QK_NOPE:], c_ref[...], s_ref[...]).astype(BF16)

        _per_query_block(qi, t // ATT_TQ, branch)

        @pl.when(qi == t // ATT_TQ - 1)
        def _():
            for hh in range(BWD_HEADS):
                dknv_ref[0, :, hh * QK_NOPE:(hh + 1) * QK_NOPE] = dk_ref[hh, :, :QK_NOPE]
                dkr_ref[...] += dk_ref[hh, :, QK_NOPE:]

    qs, kns, krs, vs, os_ = _attn_specs(t, BWD_HEADS)
    tab = pl.BlockSpec((ATT_TQ, LANES), lambda h, i: (i, 0))
    return _tc_call(
        body, name="attn_bwd", grid=(N_HEADS // BWD_HEADS, t // ATT_TQ), in_specs=[qs, kns, krs, vs, os_, tab, tab],
        out_specs=[qs, pl.BlockSpec((2, t, BWD_HEADS * QK_NOPE), lambda h, i: (0, 0, h)), krs],
        out_shape=[jax.ShapeDtypeStruct((t, N_HEADS * HEAD_PAD), BF16),
                   jax.ShapeDtypeStruct((2, t, N_HEADS * QK_NOPE), F32), jax.ShapeDtypeStruct((t, LANES), F32)],
        scratch_shapes=[pltpu.VMEM((BWD_HEADS, t, HEAD_PAD), BF16), pltpu.VMEM((BWD_HEADS, t, HEAD_PAD), F32)],
        compiler_params=_cp("arbitrary", "arbitrary"),
    )(q, knv, kr, knv, do, cos, sin)


def _adam_math(w, g, m, v):
    nm = ADAM_B1 * m + (1.0 - ADAM_B1) * g
    nv = ADAM_B2 * v + (1.0 - ADAM_B2) * (g * g)
    m_hat = nm / (1.0 - ADAM_B1 ** ADAM_STEP)
    v_hat = nv / (1.0 - ADAM_B2 ** ADAM_STEP)
    return -ADAM_LR * (m_hat / (jnp.sqrt(v_hat) + ADAM_EPS) + ADAM_WD * w), nm, nv


def _adamw_small(ws, gs, ms, vs):
    n = len(ws)

    def body(*refs):
        for i in range(n):
            w_ref, g_ref, m_ref, v_ref = (refs[k * n + i] for k in range(4))
            go_ref, d_ref, nm_ref, nv_ref = (refs[(4 + k) * n + i] for k in range(4))
            go_ref[...] = g_ref[...]
            d_ref[...], nm_ref[...], nv_ref[...] = _adam_math(w_ref[...], g_ref[...], m_ref[...], v_ref[...])

    shapes = [jax.ShapeDtypeStruct(a.shape, F32) for a in ws]
    res = _tc_call(body, name="adamw_small", out_shape=shapes * 4)(*ws, *gs, *ms, *vs)
    return [res[k * n:(k + 1) * n] for k in range(4)]


ADAM_SPLIT = 4


def _store_without_head_padding(dst_ref, g):
    assert HEAD_PAD == 2 * LANES and 2 * (QK_NOPE + QK_ROPE) == 3 * LANES, (HEAD_PAD, QK_NOPE, QK_ROPE)
    assert g.shape[1] % (2 * HEAD_PAD) == 0, g.shape
    low = lax.broadcasted_iota(jnp.int32, (g.shape[0], LANES), 1) < LANES // 2
    for pair in range(g.shape[1] // (2 * HEAD_PAD)):
        t = [g[:, (4 * pair + k) * LANES:(4 * pair + k + 1) * LANES] for k in range(4)]
        moved = [pltpu.roll(t[k], LANES // 2, axis=1) for k in (2, 3)]
        outs = (t[0], jnp.where(low, t[1], moved[0]), jnp.where(low, moved[0], moved[1]))
        for k, o in enumerate(outs):
            dst_ref[:, (3 * pair + k) * LANES:(3 * pair + k + 1) * LANES] = o


def _adamw_shards(ids, items, name):
    n = len(items)

    def body(ids_ref, *refs):
        outs = refs[len(refs) - 4 * n:]
        for i, it in enumerate(items):
            w_ref, m_ref, v_ref, gm_ref, gs_ref = refs[5 * i:5 * i + 5]
            g_ref, d_ref, nm_ref, nv_ref = outs[4 * i:4 * i + 4]
            cols = slice(*it["gcols"]) if it.get("gcols") else slice(None)
            whose = pl.program_id(0) if it.get("owner") is None else it["owner"]
            mine = whose == ids_ref[0]

            def take(src_ref, g_ref=g_ref, cols=cols, head_padded=it.get("head_padded")):
                if head_padded:
                    _store_without_head_padding(g_ref, src_ref[...])
                else:
                    g_ref[...] = src_ref[:, cols]

            @pl.when(mine)
            def _(take=take, gm_ref=gm_ref):
                take(gm_ref)

            @pl.when(jnp.logical_not(mine))
            def _(take=take, gs_ref=gs_ref):
                take(gs_ref)

            d_ref[...], nm_ref[...], nv_ref[...] = _adam_math(w_ref[...], g_ref[...], m_ref[...], v_ref[...])

    in_specs, out_specs, out_shape, args, carried, aliases = [], [], [], [ids], [], {}
    for i, it in enumerate(items):
        w = it["w"]
        r, c = w.shape[-2:]
        tr = r // 2 // ADAM_SPLIT
        assert tr % 8 == 0, (name, w.shape)
        layer = it.get("layer")
        if layer is None:
            wspec = pl.BlockSpec((tr, c), lambda h, k, ids: (h * ADAM_SPLIT + k, 0))
        else:
            wspec = pl.BlockSpec((None, tr, c), lambda h, k, ids, layer=layer: (layer, h * ADAM_SPLIT + k, 0))
        gc = it["g_mine"].shape[1]

        def g_index(of_mine, owner=it.get("owner")):
            def index(h, k, ids):
                if owner is None:
                    half = ids[0] if of_mine else 1 - ids[0]
                    return jnp.where(h == half, k, jnp.where(h < half, 0, ADAM_SPLIT - 1)), 0
                read = (owner == ids[0]) if of_mine else (owner != ids[0])
                return jnp.where(read, h * ADAM_SPLIT + k, 0), 0
            return index

        in_specs += [wspec] * 3 + [pl.BlockSpec((tr, gc), g_index(True)), pl.BlockSpec((tr, gc), g_index(False))]
        args += [w, it["m"], it["v"], it["g_mine"], it["g_sib"]]
        out_specs += [wspec] * 4
        out_shape += [jax.ShapeDtypeStruct(w.shape, F32)] * 4
        if it.get("prev") is not None:
            for k, p in enumerate(it["prev"]):
                aliases[1 + 5 * n + len(carried)] = 4 * i + k
                carried.append(p)
    res = _tc_call(
        body, name=name, prefetch=1, grid=(2, ADAM_SPLIT), in_specs=in_specs + [ANY] * len(carried),
        out_specs=out_specs, out_shape=out_shape, input_output_aliases=aliases,
        compiler_params=_cp("parallel", "parallel"),
    )(*args, *carried)
    return [res[4 * i:4 * i + 4] for i in range(n)]


def _peer_chip(k_me, j):
    return k_me ^ jnp.where(j == 0, 2, jnp.where(j == 1, 1, 3))


def _pair_sums(ids, gs, ras, name):
    n = len(gs)

    def body(ids_ref, *refs):
        for i in range(n):
            g_ref, ra_ref, o_ref = refs[2 * i], refs[2 * i + 1], refs[2 * n + i]
            o_ref[...] = (g_ref[...].astype(F32) + ra_ref[...].astype(F32)).astype(BF16)

    in_specs, out_specs, out_shape = [], [], []
    for g in gs:
        half, c = g.shape[1] // 2, g.shape[2]
        in_specs += [pl.BlockSpec((None, half, c), lambda j, ids: (_peer_chip(ids[1], j), ids[0], 0)),
                     pl.BlockSpec((None, half, c), lambda j, ids: (_peer_chip(ids[1], j), 0, 0))]
        out_specs.append(pl.BlockSpec((None, half, c), lambda j, ids: (j, 0, 0)))
        out_shape.append(jax.ShapeDtypeStruct((3, half, c), BF16))
    return _tc_call(
        body, name=name, prefetch=1, grid=(3,), in_specs=in_specs, out_specs=out_specs, out_shape=out_shape,
        compiler_params=_cp("parallel"),
    )(ids, *[a for pair in zip(gs, ras) for a in pair])


def _chip_sums(ids, gs, ras, rbs, name):
    n = len(gs)

    def body(ids_ref, *refs):
        for i in range(n):
            g_ref, ra_ref, rb_ref, o_ref = refs[3 * i], refs[3 * i + 1], refs[3 * i + 2], refs[3 * n + i]
            acc = g_ref[...].astype(F32) + ra_ref[...].astype(F32)
            for j in range(3):
                acc = acc + rb_ref[j].astype(F32)
            o_ref[...] = acc

    in_specs, out_specs, out_shape = [], [], []
    for g in gs:
        half, c = g.shape[1] // 2, g.shape[2]
        in_specs += [pl.BlockSpec((None, half, c), lambda i, ids: (ids[1], ids[0], 0)),
                     pl.BlockSpec((None, half, c), lambda i, ids: (ids[1], 0, 0)),
                     pl.BlockSpec((3, half, c), lambda i, ids: (0, 0, 0))]
        out_specs.append(pl.BlockSpec((half, c), lambda i, ids: (0, 0)))
        out_shape.append(jax.ShapeDtypeStruct((half, c), F32))
    return _tc_call(
        body, name=name, prefetch=1, grid=(1,), in_specs=in_specs, out_specs=out_specs, out_shape=out_shape,
        compiler_params=_cp("arbitrary"),
    )(ids, *[a for trio in zip(gs, ras, rbs) for a in trio])


def _position():
    x, y, c = lax.axis_index("x"), lax.axis_index("y"), lax.axis_index("c")
    chips = [(1 - x, y), (x, 1 - y), (1 - x, 1 - y)]
    return x, y, c, chips


def _shard_half(ref, wm, h):
    if wm.kind == "tiny":
        return ref
    if wm.nl == 2:
        return ref.at[h]
    return ref.at[pl.ds(pl.multiple_of(h * (wm.k // 2), 16), wm.k // 2), :]


def _region(full, wm, s, h):
    if wm.kind == "tiny":
        return full.at[s]
    cols = pl.ds(pl.multiple_of(s * wm.n, LANES), wm.n) if wm.kind == "col" else slice(None)
    if wm.nl == 2:
        rows = pl.ds(pl.multiple_of(s * wm.k, 16), wm.k) if wm.kind == "row" else slice(None)
        return full.at[slice(None) if h is None else h, rows, cols]
    if wm.kind == "col":
        rows = slice(None) if h is None else pl.ds(pl.multiple_of(h * (wm.k // 2), 16), wm.k // 2)
    elif h is None:
        rows = pl.ds(pl.multiple_of(s * wm.k, 16), wm.k)
    else:
        rows = pl.ds(pl.multiple_of(s * wm.k + h * (wm.k // 2), 16), wm.k // 2)
    return full.at[rows, cols]


def _full_shape(wm):
    if wm.kind == "tiny":
        return (N_CHIPS, wm.k, wm.n)
    shape = (wm.k, N_CHIPS * wm.n) if wm.kind == "col" else (N_CHIPS * wm.k, wm.n)
    return shape if wm.nl == 1 else (wm.nl,) + shape


def _handshake(peers):
    barrier = pltpu.get_barrier_semaphore()
    for peer in peers:
        pl.semaphore_signal(barrier, inc=1, device_id=peer, device_id_type=MESH)
    pl.semaphore_wait(barrier, len(peers))


def _all_gather_group(gi, shards):
    wms = AG_GROUPS[gi]
    nw = len(wms)

    def body(*refs):
        sh, full = refs[:nw], refs[nw:2 * nw]
        ici_s, ici_r, pass_s, pass_r, own_s, own_r = refs[2 * nw:]
        x, y, c, _ = _position()
        me, sibling = 2 * x + y, (x, y, 1 - c)
        first, second, diagonal = (x ^ (1 - c), y ^ c), (x ^ c, y ^ (1 - c)), (1 - x, 1 - y)
        chip_id = lambda chip: 2 * chip[0] + chip[1]
        _handshake([(*first, c), (*second, c), sibling])

        def rcopy(src, dst, s_sem, r_sem, to):
            return pltpu.make_async_remote_copy(src_ref=src, dst_ref=dst, send_sem=s_sem, recv_sem=r_sem,
                                                device_id=to, device_id_type=MESH)

        started = []

        def go(cp):
            cp.start()
            started.append(cp)

        for i, wm in enumerate(wms):
            half, dst = _shard_half(sh[i], wm, c), _region(full[i], wm, me, c)
            go(rcopy(half, dst, ici_s.at[i, 0], ici_r.at[i, 0], (*first, c)))
            go(rcopy(half, dst, ici_s.at[i, 1], ici_r.at[i, 1], (*second, c)))
            go(rcopy(sh[i], _region(full[i], wm, me, None), own_s.at[i], own_r.at[i], sibling))
        for i, wm in enumerate(wms):
            got = _region(full[i], wm, chip_id(first), c)
            rcopy(got, got, ici_s.at[i, 0], ici_r.at[i, 0], sibling).wait_recv()
            go(rcopy(got, got, ici_s.at[i, 2], ici_r.at[i, 2], (*second, c)))
            if wm.kind != "tiny":
                go(rcopy(got, got, pass_s.at[i, 0], pass_r.at[i, 0], sibling))
        for i, wm in enumerate(wms):
            for j, chip in ((1, second), (2, diagonal)):
                got = _region(full[i], wm, chip_id(chip), c)
                rcopy(got, got, ici_s.at[i, j], ici_r.at[i, j], sibling).wait_recv()
                if wm.kind != "tiny":
                    go(rcopy(got, got, pass_s.at[i, j], pass_r.at[i, j], sibling))
        for i, wm in enumerate(wms):
            mine = _region(full[i], wm, me, None)
            rcopy(mine, mine, own_s.at[i], own_r.at[i], sibling).wait_recv()
            if wm.kind != "tiny":
                for j, chip in ((0, second), (1, first), (2, diagonal)):
                    got = _region(full[i], wm, chip_id(chip), 1 - c)
                    rcopy(got, got, pass_s.at[i, j], pass_r.at[i, j], sibling).wait_recv()
        for cp in started:
            cp.wait_send()

    return pl.kernel(
        body, out_type=[jax.ShapeDtypeStruct(_full_shape(wm), s.dtype) for wm, s in zip(wms, shards)],
        mesh=plsc.ScalarSubcoreMesh(axis_name="sequencer", num_cores=1), name=f"ag_group{gi}",
        scratch_types=[pltpu.SemaphoreType.DMA((nw, 3))] * 4 + [pltpu.SemaphoreType.DMA((nw,))] * 2,
        compiler_params=pltpu.CompilerParams(collective_id=gi),
    )(*shards)


def _sequencer_call(body, name, cid, out_types, scratch, args):
    return pl.kernel(
        body, out_type=out_types, mesh=plsc.ScalarSubcoreMesh(axis_name="sequencer", num_cores=1), name=name,
        scratch_types=scratch, compiler_params=pltpu.CompilerParams(collective_id=cid),
    )(*args)


def _pair_exchange(gs, tag, cid):
    n = len(gs)

    def body(*refs):
        g, out, send_sems, recv_sems = refs[:n], refs[n:2 * n], refs[2 * n], refs[2 * n + 1]
        x, y, c, _ = _position()
        _handshake([(x, y, 1 - c)])
        cps = []
        for i in range(n):
            half = g[i].shape[1] // 2
            cps.append(pltpu.make_async_remote_copy(
                src_ref=g[i].at[:, pl.ds(pl.multiple_of((1 - c) * half, 16), half), :], dst_ref=out[i],
                send_sem=send_sems.at[i], recv_sem=recv_sems.at[i], device_id=(x, y, 1 - c), device_id_type=MESH))
            cps[-1].start()
        for cp in cps:
            cp.wait()

    return _sequencer_call(
        body, f"rs_pair_exchange{tag}", cid,
        [jax.ShapeDtypeStruct((a.shape[0], a.shape[1] // 2, a.shape[2]), a.dtype) for a in gs],
        [pltpu.SemaphoreType.DMA((n,)), pltpu.SemaphoreType.DMA((n,))], gs)


def _chip_exchange(ss, tag, cid):
    n = len(ss)

    def body(*refs):
        s, out, send_sems, recv_sems = refs[:n], refs[n:2 * n], refs[2 * n], refs[2 * n + 1]
        x, y, c, chips = _position()
        _handshake([(*chip, c) for chip in chips])
        cps = []
        for i in range(n):
            for j, chip in enumerate(chips):
                cps.append(pltpu.make_async_remote_copy(
                    src_ref=s[i].at[j], dst_ref=out[i].at[j], send_sem=send_sems.at[i, j], recv_sem=recv_sems.at[i, j],
                    device_id=(*chip, c), device_id_type=MESH))
                cps[-1].start()
        for cp in cps:
            cp.wait()

    return _sequencer_call(
        body, f"rs_chip_exchange{tag}", cid, [jax.ShapeDtypeStruct(a.shape, a.dtype) for a in ss],
        [pltpu.SemaphoreType.DMA((n, 3)), pltpu.SemaphoreType.DMA((n, 3))], ss)


def _pair_swap(g8s, tag, cid):
    n = len(g8s)

    def body(*refs):
        g, out, send_sems, recv_sems = refs[:n], refs[n:2 * n], refs[2 * n], refs[2 * n + 1]
        x, y, c, _ = _position()
        _handshake([(x, y, 1 - c)])
        cps = []
        for i in range(n):
            cps.append(pltpu.make_async_remote_copy(
                src_ref=g[i], dst_ref=out[i], send_sem=send_sems.at[i], recv_sem=recv_sems.at[i],
                device_id=(x, y, 1 - c), device_id_type=MESH))
            cps[-1].start()
        for cp in cps:
            cp.wait()

    return _sequencer_call(
        body, f"rs_pair_swap{tag}", cid, [jax.ShapeDtypeStruct(a.shape, a.dtype) for a in g8s],
        [pltpu.SemaphoreType.DMA((n,)), pltpu.SemaphoreType.DMA((n,))], g8s)


def _pair_swap_now(g8s):
    n = len(g8s)

    def body(*refs):
        g, out, send_sems, recv_sems = refs[:n], refs[n:2 * n], refs[2 * n], refs[2 * n + 1]
        x, y, c, _ = _position()
        cps = []
        for i in range(n):
            cps.append(pltpu.make_async_remote_copy(
                src_ref=g[i], dst_ref=out[i], send_sem=send_sems.at[i], recv_sem=recv_sems.at[i],
                device_id=(x, y, 1 - c), device_id_type=MESH))
            cps[-1].start()
        for cp in cps:
            cp.wait()

    return _tc_call(
        body, name="rs_pair_swap_last", in_specs=[ANY] * n, out_specs=[ANY] * n,
        out_shape=[jax.ShapeDtypeStruct(a.shape, a.dtype) for a in g8s],
        scratch_shapes=[pltpu.SemaphoreType.DMA((n,)), pltpu.SemaphoreType.DMA((n,))],
    )(*g8s)


def _all_reduce_small(vecs, owner_major, name):
    n = len(vecs)
    block = lambda i, ref, chip: ref.at[chip] if owner_major[i] else ref
    out_shapes = [a.shape[1:] if owner_major[i] else a.shape for i, a in enumerate(vecs)]

    def body(*refs):
        v, o, gath = refs[:n], refs[n:2 * n], refs[2 * n:3 * n]
        send_sems, recv_sems = refs[3 * n], refs[3 * n + 1]
        x, y, c, _ = _position()
        me = 4 * x + 2 * y + c
        cps = []
        for i in range(n):
            gath[i][me] = block(i, v[i], 2 * x + y)[...]
            for rel in range(1, N_DEV):
                px, py, pc = x ^ (rel >> 2), y ^ ((rel >> 1) & 1), c ^ (rel & 1)
                cps.append(pltpu.make_async_remote_copy(
                    src_ref=block(i, v[i], 2 * px + py), dst_ref=gath[i].at[me], send_sem=send_sems.at[i, rel - 1],
                    recv_sem=recv_sems.at[i, rel - 1], device_id=(px, py, pc), device_id_type=MESH))
        for cp in cps:
            cp.start()
        for i in range(n):
            for rel in range(1, N_DEV):
                pltpu.make_async_remote_copy(
                    src_ref=block(i, v[i], 2 * x + y), dst_ref=gath[i].at[me ^ rel],
                    send_sem=send_sems.at[i, rel - 1], recv_sem=recv_sems.at[i, rel - 1], device_id=(x, y, c),
                    device_id_type=MESH).wait_recv()
        for cp in cps:
            cp.wait_send()
        for i in range(n):
            acc = gath[i][0]
            for d in range(1, N_DEV):
                acc = acc + gath[i][d]
            o[i][...] = acc

    vm = pl.BlockSpec(memory_space=pltpu.VMEM)
    return _tc_call(
        body, name=name, in_specs=[vm] * n, out_specs=[vm] * n,
        out_shape=[jax.ShapeDtypeStruct(s, F32) for s in out_shapes],
        scratch_shapes=[pltpu.VMEM((N_DEV,) + s, F32) for s in out_shapes]
        + [pltpu.SemaphoreType.DMA((n, N_DEV - 1)), pltpu.SemaphoreType.DMA((n, N_DEV - 1))],
    )(*vecs)


def _rope_tables(positions):
    half = QK_ROPE // 2
    inv_freq = 1.0 / (ROPE_THETA ** (jnp.arange(half, dtype=F32) / half))
    ang = positions.astype(F32)[:, None] * inv_freq
    zeros = jnp.zeros((positions.shape[0], LANES - QK_ROPE), F32)
    cos, sin = jnp.cos(ang), jnp.sin(ang)
    return jnp.concatenate([cos, cos, zeros], axis=1), jnp.concatenate([sin, sin, zeros], axis=1)


def _local_step(x, positions, tgt, wf, small, rs):
    cos, sin = _rope_tables(positions)
    w_in, w_out = wf["sc_w_in"], wf["sc_w_out"]
    w_ups, w_downs = (wf["ffn_w_up0"], wf["ffn_w_up1"]), (wf["ffn_w_down0"], wf["ffn_w_down1"])
    w_kv, w_ukv, w_dq, w_uq, w_o = wf["w_kv"], wf["w_ukv"], wf["w_dq"], wf["w_uq"], wf["w_o"]
    attn_norm, ffn_norm = small["attn_norm"], small["ffn_norm"]
    conv_b = small["ffn_conv_b"]

    def ffn_fwd(h, hf, l, then):
        up, a = _ffn_up_gate(hf, w_ups[l], small["ffn_conv_w"][l], conv_b[l:l + 1], f"ffn{l}_up_gate")
        return then(a, w_downs[l], h), (hf, up, a)

    def ffn_bwd(h, dh_out, dh_out_b, l, saved, gi, hooks):
        run = lambda stage: hooks.get(stage, lambda: None)()
        hf, up, a = saved
        d_down = _tn(f"ffn{l}_down_dw", a, dh_out_b, BF16)
        run("down_dw")
        dup, d_cw, d_cb = _gate_bwd(up, small["ffn_conv_w"][l], conv_b[l:l + 1], dh_out_b, w_downs[l],
                                    f"ffn{l}_gate_bwd")
        run("gate_bwd")
        d_up = _dw_ffn_up(f"ffn{l}_up_dw", hf, dup)
        rs.start(gi, {f"ffn_w_down{l}": d_down.reshape(N_CHIPS, F_FF // N_CHIPS, D), f"ffn_w_up{l}": d_up})
        run("up_dw")
        dh, dh_b, d_norm = _dx_norm_bwd(f"ffn{l}_up_dx", dup, w_ups[l], h, ffn_norm[l:l + 1], dh_out)
        run("up_dx")
        return dh, dh_b, d_cw, d_cb, d_norm

    hn0 = _rms_fwd(x, attn_norm[0:1], "attn0_norm")
    z = _nn_parts("sc_in", hn0, w_in, 3, BF16)
    mix = _scmix_fwd(z, small["sc_conv_w"])
    h1, hf0 = _nn_add_norm("sc_out", mix, w_out, x, ffn_norm[0:1])
    h2, ffn0_saved = ffn_fwd(h1, hf0, 0, lambda a, w, h: _nn("ffn0_down", a, w, F32, add=h))

    hn1, hk, cq_pre, cq, q, kvpre, ckv, kr, knv = _attn_prep(
        h2, attn_norm[1:2], small["kv_in_norm"], w_dq, small["q_latent_norm"], w_uq, w_kv, small["kv_latent_norm"],
        w_ukv, cos, sin)
    o = _attn_fwd(q, knv, kr)
    h3, hf1 = _nn_add_norm("attn_out", o, w_o, h2, ffn_norm[1:2])
    (loss, dh4, dh4_b, d_final), ffn1_saved = ffn_fwd(
        h3, hf1, 1, lambda a, w, h: _nn_add_loss("ffn1_down_loss", a, w, h, small["final_norm"], tgt))

    rows = D // N_CHIPS
    dh3, dh3_b, d_cw1, d_cb1, d_fn1 = ffn_bwd(h3, dh4, dh4_b, 1, ffn1_saved, 0, {})

    do = _nt("attn_out_dx", dh3_b, w_o, BF16)
    d_wo = _tn("attn_out_dw", o, dh3_b, BF16)
    rs.pair_sums(0)
    dq, dknv, dkr = _attn_bwd(q, knv, kr, do, cos, sin)
    rs.chip_sums(0)
    dh2, dh2_b, d_wuq, d_wdq, d_wukv, d_wkv, d_an1, d_kvin, d_qln, d_kvln = _attn_prep_bwd(
        dq, dknv, dkr, dh3, h2, hn1, hk, cq_pre, cq, kvpre, ckv, attn_norm[1:2], small["kv_in_norm"], w_dq,
        small["q_latent_norm"], w_uq, w_kv, small["kv_latent_norm"], w_ukv, cos, sin)
    rs.finish(0)
    by_owner = lambda dw: dw.reshape(dw.shape[0], N_CHIPS, -1).transpose(1, 0, 2)
    rs.start(1, {
        "w_o": d_wo.reshape(N_CHIPS, rows, D), "w_uq": by_owner(d_wuq), "w_dq": d_wdq.reshape(N_CHIPS, rows, Q_LORA),
        "w_ukv": by_owner(d_wukv.reshape(2 * KV_LORA, -1)).reshape(N_CHIPS, 2 * KV_LORA, -1),
        "w_kv": d_wkv.reshape(N_CHIPS, rows, KVP),
    })

    dh1, dh1_b, d_cw0, d_cb0, d_fn0 = ffn_bwd(h1, dh2, dh2_b, 0, ffn0_saved, 2, {
        "down_dw": lambda: rs.pair_sums(1), "gate_bwd": lambda: rs.chip_sums(1),
        "up_dw": lambda: (rs.finish(1), rs.pair_sums(2))})

    d_wout = _tn("sc_out_dw", mix, dh1_b, BF16)
    dmix = _nt("sc_out_dx", dh1_b, w_out, BF16)
    dz, d_scw = _scmix_bwd(z, small["sc_conv_w"], dmix)
    d_win = _dw_sc_in(hn0, dz)
    rs.start(3, {"sc_w_out": d_wout.reshape(N_CHIPS, rows, D), "sc_w_in": d_win})
    dx, _, d_an0 = _dx_norm_bwd("sc_in_dx", dz, w_in, x, attn_norm[0:1], dh1)

    taps_by_owner = lambda per_layer: jnp.stack(per_layer, axis=1).reshape(3, len(per_layer), N_CHIPS, -1).transpose(2, 0, 1, 3)
    small_g = {
        "attn_norm": jnp.concatenate([d_an0, d_an1]), "ffn_norm": jnp.concatenate([d_fn0, d_fn1]),
        "final_norm": d_final, "kv_in_norm": d_kvin, "kv_latent_norm": d_kvln, "q_latent_norm": d_qln,
        "ffn_conv_b": jnp.concatenate([d_cb0, d_cb1]),
        "sc_conv_w": taps_by_owner([d_scw]), "ffn_conv_w": taps_by_owner([d_cw0, d_cw1]),
    }
    return loss, dx, small_g


RS_GROUPS = (("ffn_w_down1", "ffn_w_up1"), ("w_o", "w_uq", "w_dq", "w_ukv", "w_kv"),
             ("ffn_w_down0", "ffn_w_up0"), ("sc_w_out", "sc_w_in"))


class _ReduceScatter:
    def __init__(self, ids, finish):
        self.ids, self.grads, self.step, self.mine, self.sib, self.finish = ids, {}, {}, {}, {}, finish

    def _cid(self, gi):
        return len(AG_GROUPS) + 3 * gi

    def start(self, gi, grads):
        self.grads.update(grads)
        own = [grads[n] for n in RS_GROUPS[gi]]
        self.step[gi] = (own, _pair_exchange(own, gi, self._cid(gi)))

    def pair_sums(self, gi):
        own, ra = self.step[gi]
        sums = _pair_sums(self.ids, own, ra, f"rs_pair_sums{gi}")
        self.step[gi] = (own, ra, _chip_exchange(sums, gi, self._cid(gi) + 1))

    def chip_sums(self, gi):
        own, ra, rb = self.step[gi]
        mine = _chip_sums(self.ids, own, ra, rb, f"rs_chip_sums{gi}")
        self.mine.update(zip(RS_GROUPS[gi], mine))
        last = gi == len(RS_GROUPS) - 1
        swapped = _pair_swap_now(mine) if last else _pair_swap(mine, gi, self._cid(gi) + 2)
        self.sib.update(zip(RS_GROUPS[gi], swapped))


SMALL_REPL = ("attn_norm", "ffn_norm", "final_norm", "kv_in_norm", "kv_latent_norm", "q_latent_norm", "ffn_conv_b")


def _pad_heads(w_uq):
    per_head = w_uq.reshape(Q_LORA, -1, QK_NOPE + QK_ROPE)
    return jnp.pad(per_head, ((0, 0), (0, 0), (0, HEAD_PAD - QK_NOPE - QK_ROPE))).reshape(Q_LORA, -1)


def _pack_kv(w_dkv, w_kr):
    return jnp.concatenate([w_dkv, w_kr, jnp.zeros((w_kr.shape[0], LANES - QK_ROPE), w_kr.dtype)], axis=1)


def kernel(x, positions, attn_norm, ffn_norm, final_norm, sc_w_in, sc_conv_w, sc_w_out, kv_in_norm, w_dkv, kv_latent_norm, w_kr, w_uk, w_uv, w_dq, q_latent_norm, w_uq, w_o, ffn_w_up, ffn_conv_w, ffn_conv_b, ffn_w_down, loss_target, m_attn_norm, m_ffn_norm, m_final_norm, m_sc_w_in, m_sc_conv_w, m_sc_w_out, m_kv_in_norm, m_w_dkv, m_kv_latent_norm, m_w_kr, m_w_uk, m_w_uv, m_w_dq, m_q_latent_norm, m_w_uq, m_w_o, m_ffn_w_up, m_ffn_conv_w, m_ffn_conv_b, m_ffn_w_down, v_attn_norm, v_ffn_norm, v_final_norm, v_sc_w_in, v_sc_conv_w, v_sc_w_out, v_kv_in_norm, v_w_dkv, v_kv_latent_norm, v_w_kr, v_w_uk, v_w_uv, v_w_dq, v_q_latent_norm, v_w_uq, v_w_o, v_ffn_w_up, v_ffn_conv_w, v_ffn_conv_b, v_ffn_w_down):
    names = ("attn_norm", "ffn_norm", "final_norm", "sc_w_in", "sc_conv_w", "sc_w_out", "kv_in_norm", "w_dkv",
             "kv_latent_norm", "w_kr", "w_uk", "w_uv", "w_dq", "q_latent_norm", "w_uq", "w_o", "ffn_w_up",
             "ffn_conv_w", "ffn_conv_b", "ffn_w_down")
    w = dict(zip(names, (attn_norm, ffn_norm, final_norm, sc_w_in, sc_conv_w, sc_w_out, kv_in_norm, w_dkv,
                         kv_latent_norm, w_kr, w_uk, w_uv, w_dq, q_latent_norm, w_uq, w_o, ffn_w_up,
                         ffn_conv_w, ffn_conv_b, ffn_w_down)))
    m = dict(zip(names, (m_attn_norm, m_ffn_norm, m_final_norm, m_sc_w_in, m_sc_conv_w, m_sc_w_out, m_kv_in_norm,
                         m_w_dkv, m_kv_latent_norm, m_w_kr, m_w_uk, m_w_uv, m_w_dq, m_q_latent_norm, m_w_uq, m_w_o,
                         m_ffn_w_up, m_ffn_conv_w, m_ffn_conv_b, m_ffn_w_down)))
    v = dict(zip(names, (v_attn_norm, v_ffn_norm, v_final_norm, v_sc_w_in, v_sc_conv_w, v_sc_w_out, v_kv_in_norm,
                         v_w_dkv, v_kv_latent_norm, v_w_kr, v_w_uk, v_w_uv, v_w_dq, v_q_latent_norm, v_w_uq, v_w_o,
                         v_ffn_w_up, v_ffn_conv_w, v_ffn_conv_b, v_ffn_w_down)))

    _ORDER[0] = None
    ix, iy, ic = lax.axis_index("x"), lax.axis_index("y"), lax.axis_index("c")
    chip = 2 * ix + iy
    ids = jnp.stack([ic, chip]).astype(jnp.int32)

    ws = {
        "sc_w_in": sc_w_in[0], "sc_w_out": sc_w_out[0], "ffn_w_up": ffn_w_up, "ffn_w_down": ffn_w_down,
        "w_kv": _pack_kv(w_dkv, w_kr), "w_ukv": jnp.stack([w_uk, w_uv]), "w_dq": w_dq[0],
        "w_uq": _pad_heads(w_uq[0]), "w_o": w_o[0],
    }

    def ag_shard(name):
        if name == "sc_conv_w":
            return sc_conv_w[0]
        if name == "ffn_conv_w":
            return ffn_conv_w.reshape(6, -1)
        if name[:-1] in ("ffn_w_up", "ffn_w_down"):
            return ws[name[:-1]][int(name[-1])].astype(BF16)
        return ws[name].astype(BF16)

    wf = {}
    for gi, wms in enumerate(AG_GROUPS):
        fulls = _all_gather_group(gi, [ag_shard(wm.name) for wm in wms])
        wf.update({wm.name: f for wm, f in zip(wms, fulls)})
    small = {
        "attn_norm": attn_norm, "ffn_norm": ffn_norm, "final_norm": final_norm[None], "kv_in_norm": kv_in_norm[None],
        "kv_latent_norm": kv_latent_norm[None], "q_latent_norm": q_latent_norm, "ffn_conv_b": ffn_conv_b,
        "sc_conv_w": wf["sc_conv_w"].transpose(1, 0, 2).reshape(3, D),
        "ffn_conv_w": wf["ffn_conv_w"].reshape(N_CHIPS, 2, 3, -1).transpose(1, 2, 0, 3).reshape(2, 3, F_FF),
    }

    res = {}

    held = {
        "ffn_w_up0": [("ffn_w_up", dict(layer=0))], "ffn_w_up1": [("ffn_w_up", dict(layer=1))],
        "ffn_w_down0": [("ffn_w_down", dict(layer=0))], "ffn_w_down1": [("ffn_w_down", dict(layer=1))],
        "sc_w_in": [("sc_w_in", dict(layer=0))], "sc_w_out": [("sc_w_out", dict(layer=0))],
        "w_dq": [("w_dq", dict(layer=0))], "w_o": [("w_o", dict(layer=0))], "w_uq": [("w_uq", dict(layer=0, head_padded=True))],
        "w_kv": [("w_dkv", dict(gcols=(0, KV_LORA))), ("w_kr", dict(gcols=(KV_LORA, KV_LORA + QK_ROPE)))],
        "w_ukv": [("w_uk", dict(owner=0)), ("w_uv", dict(owner=1))],
    }

    def adamw_group(gi):
        items = []
        for key in RS_GROUPS[gi]:
            for n, opts in held[key]:
                items.append(dict(name=n, w=w[n], m=m[n], v=v[n], g_mine=rs.mine[key], g_sib=rs.sib[key],
                                  prev=res.get(n) if "layer" in opts and w[n].shape[0] > 1 else None, **opts))
        for it, out in zip(items, _adamw_shards(ids, items, f"adamw_group{gi}")):
            res[it["name"]] = out

    rs = _ReduceScatter(ids, adamw_group)
    loss, dx, small_g = _local_step(x[0], positions[0], loss_target[0], wf, small, rs)

    rs.chip_sums(2)
    rs.pair_sums(3)

    s_names = list(small_g)
    reduced = _all_reduce_small([small_g[n] for n in s_names] + [loss], [small_g[n].ndim == 4 for n in s_names] + [False],
                                "ar_small")
    sg, loss_out = dict(zip(s_names, reduced[:-1])), reduced[-1][0, 0]

    row = lambda n: (lambda t: t[n][None])
    taps = lambda n: (lambda t: t[n].transpose(1, 0, 2))
    small_2d = {
        "attn_norm": (sg["attn_norm"], lambda t: t["attn_norm"]), "ffn_norm": (sg["ffn_norm"], lambda t: t["ffn_norm"]),
        "final_norm": (sg["final_norm"], row("final_norm")), "kv_in_norm": (sg["kv_in_norm"], row("kv_in_norm")),
        "kv_latent_norm": (sg["kv_latent_norm"], row("kv_latent_norm")),
        "q_latent_norm": (sg["q_latent_norm"], lambda t: t["q_latent_norm"]),
        "ffn_conv_b": (sg["ffn_conv_b"], lambda t: t["ffn_conv_b"]),
        "sc_conv_w": (sg["sc_conv_w"], taps("sc_conv_w")), "ffn_conv_w": (sg["ffn_conv_w"], taps("ffn_conv_w")),
    }
    s_keys = list(small_2d)
    small_grads = [small_2d[k][0] for k in s_keys]
    views = lambda tree: [small_2d[k][1](tree) for k in s_keys]
    small_res = _adamw_small(views(w), small_grads, views(m), views(v))

    def restore(vals):
        by = dict(zip(s_keys, vals))
        out = {n: by[n].reshape(w[n].shape) for n in SMALL_REPL}
        out.update({n: by[n].transpose(1, 0, 2) for n in ("sc_conv_w", "ffn_conv_w")})
        return out

    rs.finish(2)
    rs.chip_sums(3)
    rs.finish(3)
    outs = [restore(vals) for vals in small_res]
    for k, dst in enumerate(outs):
        for n in res:
            dst[n] = res[n][k]
    grads, delta, new_m, new_v = outs

    _ORDER[0] = None
    return (loss_out, dx[None], *[grads[n] for n in names], *[delta[n] for n in names],
            *[new_m[n] for n in names], *[new_v[n] for n in names])
```

```python
from typing import NamedTuple

import jax
import jax.numpy as jnp
from jax import lax
from jax.experimental import pallas as pl
from jax.experimental.pallas import tpu as pltpu
from jax.experimental.pallas import tpu_sc as plsc

F32 = jnp.float32
BF16 = jnp.bfloat16

T = 2048
D = 1024
F_FF = 2816
N_HEADS = 8
QK_NOPE = 128
QK_ROPE = 64
V_HEAD = 128
Q_LORA = 384
KV_LORA = 256
CHUNK_SHIFT = 6
ROPE_THETA = 10000.0
EPS = 1e-6
NEG_INF = -1e30
HEAD_PAD = 256
KVP = KV_LORA + 128

ADAM_LR = 0.001
ADAM_B1 = 0.9
ADAM_B2 = 0.999
ADAM_EPS = 1e-08
ADAM_WD = 0.01
ADAM_STEP = 10

N_CHIPS = 4
N_DEV = 8
LANES = 128
TC = 256
V7X_VMEM_LIMIT = 56 * 1024 * 1024

MESH = pl.DeviceIdType.MESH
ANY = pl.BlockSpec(memory_space=pl.ANY)


class _W(NamedTuple):
    name: str
    kind: str
    nl: int
    k: int
    n: int


AG_GROUPS = (
    (_W("sc_w_in", "col", 1, D, 3 * D // N_CHIPS), _W("sc_conv_w", "tiny", 1, 3, D // N_CHIPS),
     _W("ffn_conv_w", "tiny", 1, 6, F_FF // N_CHIPS), _W("sc_w_out", "row", 1, D // N_CHIPS, D)),
    (_W("ffn_w_up0", "col", 1, D, 2 * F_FF // N_CHIPS),),
    (_W("ffn_w_down0", "row", 1, F_FF // N_CHIPS, D),),
    (_W("w_kv", "row", 1, D // N_CHIPS, KVP), _W("w_ukv", "col", 2, KV_LORA, N_HEADS * QK_NOPE // N_CHIPS),
     _W("w_dq", "row", 1, D // N_CHIPS, Q_LORA),
     _W("w_uq", "col", 1, Q_LORA, N_HEADS * HEAD_PAD // N_CHIPS),
     _W("w_o", "row", 1, N_HEADS * V_HEAD // N_CHIPS, D)),
    (_W("ffn_w_up1", "col", 1, D, 2 * F_FF // N_CHIPS), _W("ffn_w_down1", "row", 1, F_FF // N_CHIPS, D)),
)


def _cp(*sem):
    return pltpu.CompilerParams(dimension_semantics=sem, vmem_limit_bytes=V7X_VMEM_LIMIT)


_ORDER = [None]


def _tc_call(body, *, name, out_shape, in_specs=None, out_specs=None, grid=(), scratch_shapes=(), prefetch=0,
             input_output_aliases=None, compiler_params=None):
    def run(*args):
        specs = [pl.BlockSpec(memory_space=pltpu.VMEM)] * (len(args) - prefetch) if in_specs is None else list(in_specs)
        inner, dep = body, _ORDER[0]
        if dep is not None:
            unread = prefetch + len(specs)
            specs, args = specs + [ANY], (*args, dep)

            def inner(*refs):
                return body(*refs[:unread], *refs[unread + 1:])

        kwargs = dict(name=name, out_shape=out_shape, input_output_aliases=input_output_aliases or {},
                      compiler_params=compiler_params)
        if prefetch:
            kwargs["grid_spec"] = pltpu.PrefetchScalarGridSpec(
                num_scalar_prefetch=prefetch, grid=grid, in_specs=specs, out_specs=out_specs,
                scratch_shapes=scratch_shapes)
        else:
            kwargs.update(grid=grid, in_specs=specs, scratch_shapes=scratch_shapes)
            if out_specs is not None:
                kwargs["out_specs"] = out_specs
        out = pl.pallas_call(inner, **kwargs)(*args)
        _ORDER[0] = out[0] if isinstance(out, (list, tuple)) else out
        return out

    return run


def _tile(n, cands):
    for c in cands:
        if n % c == 0:
            return c
    raise ValueError(f"no tile for {n}")


NN_DIMS = (((1,), (0,)), ((), ()))
NT_DIMS = (((1,), (1,)), ((), ()))
TN_DIMS = (((0,), (0,)), ((), ()))
M_TILES = (1024, 512, 384, 256, 128)
N_TILES = (1408, 1024, 768, 512, 384, 256, 128)
MM_BLOCK_BYTES = 36 * 1024 * 1024


def _fit(m, n, block_bytes, m_tiles=M_TILES, n_tiles=N_TILES, n_first=False):
    tms, tns = [c for c in m_tiles if m % c == 0], [c for c in n_tiles if n % c == 0]
    pairs = [(tm, tn) for tn in tns for tm in tms] if n_first else [(tm, tn) for tm in tms for tn in tns]
    for tm, tn in pairs:
        if 2 * block_bytes(tm, tn) + 4 * tm * tn <= MM_BLOCK_BYTES:
            return tm, tn
    raise ValueError(f"no tiles for {m} x {n}")


def _size(x):
    return x.dtype.itemsize


def _mm(name, a, b, dims, grid, a_spec, b_spec, o_spec, o_sds, add=None, red=None, acc_shape=None):
    n_red = None if red is None else grid[red]

    def body(*refs):
        a_ref, b_ref = refs[0], refs[1]
        add_ref = refs[2] if add is not None else None
        o_ref = refs[3] if add is not None else refs[2]
        part = lax.dot_general(a_ref[...].astype(BF16), b_ref[...].astype(BF16), dims, preferred_element_type=F32)
        if red is None:
            if add is not None:
                part = part + add_ref[...]
            o_ref[...] = part.astype(o_ref.dtype)
            return
        acc_ref = refs[-1]
        r = pl.program_id(red)

        @pl.when(r == 0)
        def _():
            acc_ref[...] = part

        @pl.when(r > 0)
        def _():
            acc_ref[...] += part

        @pl.when(r == n_red - 1)
        def _():
            o_ref[...] = acc_ref[...].astype(o_ref.dtype)

    sem = tuple("arbitrary" if ax == red else "parallel" for ax in range(len(grid)))
    in_specs = [a_spec, b_spec] + ([o_spec] if add is not None else [])
    args = (a, b) + ((add,) if add is not None else ())
    return _tc_call(
        body, name=name, grid=grid, in_specs=in_specs, out_specs=o_spec, out_shape=o_sds,
        scratch_shapes=[] if red is None else [pltpu.VMEM(acc_shape, F32)], compiler_params=_cp(*sem),
    )(*args)


def _nn(name, a, b, out_dtype, add=None, lead=None):
    (m, k), n = a.shape, b.shape[-1]
    osz = jnp.dtype(out_dtype).itemsize + (4 if add is not None else 0)
    tm, tn = _fit(m, n, lambda tm, tn: tm * k * _size(a) + k * tn * _size(b) + tm * tn * osz, n_first=True)
    if lead is None:
        b_spec = pl.BlockSpec((k, tn), lambda i, j: (0, j))
    else:
        b_spec = pl.BlockSpec((None, k, tn), lambda i, j: (lead, 0, j))
    return _mm(name, a, b, NN_DIMS, (m // tm, n // tn), pl.BlockSpec((tm, k), lambda i, j: (i, 0)), b_spec,
               pl.BlockSpec((tm, tn), lambda i, j: (i, j)), jax.ShapeDtypeStruct((m, n), out_dtype), add=add)


def _nn_parts(name, a, b, parts, out_dtype, lead=None, stacked=False):
    m, k = a.shape
    c = b.shape[-1] if stacked else b.shape[-1] // parts
    osz = jnp.dtype(out_dtype).itemsize
    tm, tn = _fit(m, c, lambda tm, tn: tm * k * _size(a) + k * tn * _size(b) + tm * tn * osz)
    nb = c // tn
    if stacked:
        b_spec = pl.BlockSpec((None, k, tn), lambda i, p, j: (p, 0, j))
    elif lead is None:
        b_spec = pl.BlockSpec((k, tn), lambda i, p, j: (0, p * nb + j))
    else:
        b_spec = pl.BlockSpec((None, k, tn), lambda i, p, j: (lead, 0, p * nb + j))
    return _mm(name, a, b, NN_DIMS, (m // tm, parts, nb), pl.BlockSpec((tm, k), lambda i, p, j: (i, 0)), b_spec,
               pl.BlockSpec((None, tm, tn), lambda i, p, j: (p, i, j)), jax.ShapeDtypeStruct((parts, m, c), out_dtype))


def _nt(name, a, b, out_dtype, lead=None):
    (m, k), n = a.shape, b.shape[-2]
    osz = jnp.dtype(out_dtype).itemsize
    tm, tn = _fit(m, n, lambda tm, tn: tm * k * _size(a) + tn * k * _size(b) + tm * tn * osz)
    if lead is None:
        b_spec = pl.BlockSpec((tn, k), lambda i, j: (j, 0))
    else:
        b_spec = pl.BlockSpec((None, tn, k), lambda i, j: (lead, j, 0))
    return _mm(name, a, b, NT_DIMS, (m // tm, n // tn), pl.BlockSpec((tm, k), lambda i, j: (i, 0)), b_spec,
               pl.BlockSpec((tm, tn), lambda i, j: (i, j)), jax.ShapeDtypeStruct((m, n), out_dtype))


def _tn(name, a, b, out_dtype):
    (k, m), n = a.shape, b.shape[1]
    osz = jnp.dtype(out_dtype).itemsize
    tm, tn = _fit(m, n, lambda tm, tn: k * tm * _size(a) + k * tn * _size(b) + tm * tn * osz,
                  m_tiles=(1408, 512, 384, 256, 128), n_tiles=(n,) + N_TILES)
    return _mm(name, a, b, TN_DIMS, (m // tm, n // tn), pl.BlockSpec((k, tm), lambda i, j: (0, i)),
               pl.BlockSpec((k, tn), lambda i, j: (0, j)), pl.BlockSpec((tm, tn), lambda i, j: (i, j)),
               jax.ShapeDtypeStruct((m, n), out_dtype))


def _nn_add_norm(name, a, b, add, g):
    (m, k), n = a.shape, b.shape[1]
    tm = 512

    def body(a_ref, b_ref, add_ref, g_ref, h_ref, hn_ref):
        h = jnp.dot(a_ref[...], b_ref[...], preferred_element_type=F32) + add_ref[...]
        h_ref[...] = h
        hn_ref[...] = _rms_rows(h, g_ref[...]).astype(BF16)

    rows = lambda w: pl.BlockSpec((tm, w), lambda i: (i, 0))
    return _tc_call(
        body, name=name, grid=(m // tm,),
        in_specs=[rows(k), pl.BlockSpec((k, n), lambda i: (0, 0)), rows(n), pl.BlockSpec((1, n), lambda i: (0, 0))],
        out_specs=[rows(n), rows(n)],
        out_shape=[jax.ShapeDtypeStruct((m, n), F32), jax.ShapeDtypeStruct((m, n), BF16)], compiler_params=_cp("parallel"),
    )(a, b, add, g)


def _nn_add_loss(name, a, b, add, g, tgt):
    (m, k), n = a.shape, b.shape[1]
    tm = 512

    def body(a_ref, b_ref, add_ref, g_ref, t_ref, loss_ref, dh_ref, dhb_ref, dg_ref):
        xv = jnp.dot(a_ref[...], b_ref[...], preferred_element_type=F32) + add_ref[...]
        gv = g_ref[...]
        r = lax.rsqrt(jnp.mean(xv * xv, axis=1, keepdims=True) + EPS)
        err = xv * r * gv - t_ref[...]
        part = 0.5 * jnp.sum(jnp.mean(err * err, axis=1, keepdims=True), axis=0, keepdims=True)
        dx, dg = _rms_bwd_math(xv, gv, err * (1.0 / n))
        dh_ref[...] = dx
        dhb_ref[...] = dx.astype(BF16)

        @pl.when(pl.program_id(0) == 0)
        def _():
            dg_ref[...] = jnp.zeros_like(dg_ref)
            loss_ref[...] = jnp.zeros_like(loss_ref)

        dg_ref[...] += dg
        loss_ref[...] += jnp.broadcast_to(part, loss_ref.shape)

    rows = lambda w: pl.BlockSpec((tm, w), lambda i: (i, 0))
    vec = pl.BlockSpec((1, n), lambda i: (0, 0))
    return _tc_call(
        body, name=name, grid=(m // tm,),
        in_specs=[rows(k), pl.BlockSpec((k, n), lambda i: (0, 0)), rows(n), vec, rows(n)],
        out_specs=[pl.BlockSpec((1, LANES), lambda i: (0, 0)), rows(n), rows(n), vec],
        out_shape=[jax.ShapeDtypeStruct((1, LANES), F32), jax.ShapeDtypeStruct((m, n), F32),
                   jax.ShapeDtypeStruct((m, n), BF16), jax.ShapeDtypeStruct((1, n), F32)],
        compiler_params=_cp("arbitrary"),
    )(a, b, add, g, tgt)


def _dx_norm_bwd(name, a, b, x, g, add):
    parts, t, c = a.shape
    d = b.shape[0]
    tm = 256
    wc = c // 2
    n_pieces = 2 * parts
    assert wc % LANES == 0, a.shape

    def body(a_ref, b_hbm, x_ref, g_ref, add_ref, dx_ref, dxb_ref, dg_ref, b_ref, sems):
        first = pl.program_id(0) == 0
        piece = lambda j: pltpu.make_async_copy(b_hbm.at[:, j * wc:(j + 1) * wc], b_ref.at[:, j * wc:(j + 1) * wc],
                                                sems.at[j])

        @pl.when(first)
        def _():
            for j in range(n_pieces):
                piece(j).start()

        dy = None
        for j in range(n_pieces):
            @pl.when(first)
            def _(j=j):
                piece(j).wait()

            p, off = divmod(j * wc, c)
            part = lax.dot_general(a_ref[p, :, off:off + wc], b_ref[:, j * wc:(j + 1) * wc], NT_DIMS,
                                   preferred_element_type=F32)
            dy = part if dy is None else dy + part
        dx, dg = _rms_bwd_math(x_ref[...], g_ref[...], dy)
        dx = dx + add_ref[...]
        dx_ref[...] = dx
        dxb_ref[...] = dx.astype(BF16)

        @pl.when(pl.program_id(0) == 0)
        def _():
            dg_ref[...] = jnp.zeros_like(dg_ref)

        dg_ref[...] += dg

    rows = pl.BlockSpec((tm, d), lambda i: (i, 0))
    vec = pl.BlockSpec((1, d), lambda i: (0, 0))
    return _tc_call(
        body, name=name, grid=(t // tm,),
        in_specs=[pl.BlockSpec((parts, tm, c), lambda i: (0, i, 0)), ANY, rows, vec, rows],
        out_specs=[rows, rows, vec],
        out_shape=[jax.ShapeDtypeStruct((t, d), F32), jax.ShapeDtypeStruct((t, d), BF16),
                   jax.ShapeDtypeStruct((1, d), F32)],
        scratch_shapes=[pltpu.VMEM(b.shape, b.dtype), pltpu.SemaphoreType.DMA((n_pieces,))],
        compiler_params=_cp("arbitrary"),
    )(a, b, x, g, add)


def _dw_sc_in(hn, dz):
    t, tn, tm = hn.shape[0], TC, D
    per_part, per_chip = D // tn, 3 * D // N_CHIPS // tn
    return _mm("sc_in_dw", hn, dz, TN_DIMS, (D // tm, 3 * D // tn), pl.BlockSpec((t, tm), lambda i, j: (0, i)),
               pl.BlockSpec((None, t, tn), lambda i, j: (j // per_part, 0, j % per_part)),
               pl.BlockSpec((None, tm, tn), lambda i, j: (j // per_chip, i, j % per_chip)),
               jax.ShapeDtypeStruct((N_CHIPS, D, 3 * D // N_CHIPS), BF16))


def _dw_ffn_up(name, hf, dup):
    t, tm, ns = hf.shape[0], D, 2 * F_FF // N_CHIPS
    return _mm(name, hf, dup, TN_DIMS, (N_CHIPS, D // tm), pl.BlockSpec((t, tm), lambda s, i: (0, i)),
               pl.BlockSpec((None, t, ns), lambda s, i: (s // 2, 0, s % 2)),
               pl.BlockSpec((None, tm, ns), lambda s, i: (s, i, 0)), jax.ShapeDtypeStruct((N_CHIPS, D, ns), BF16))


def _rms_fwd(x, g, name):
    t, d = x.shape
    tr = 512

    def body(x_ref, g_ref, o_ref):
        xv = x_ref[...]
        r = lax.rsqrt(jnp.mean(xv * xv, axis=1, keepdims=True) + EPS)
        o_ref[...] = (xv * r * g_ref[...]).astype(o_ref.dtype)

    row = pl.BlockSpec((tr, d), lambda i: (i, 0))
    return _tc_call(
        body, name=name, grid=(t // tr,), in_specs=[row, pl.BlockSpec((1, d), lambda i: (0, 0))],
        out_specs=row, out_shape=jax.ShapeDtypeStruct((t, d), BF16), compiler_params=_cp("parallel"),
    )(x, g)


def _rms_bwd_math(xv, g, dy):
    r = lax.rsqrt(jnp.mean(xv * xv, axis=1, keepdims=True) + EPS)
    xh = xv * r
    gy = dy * g
    dx = r * (gy - xh * jnp.mean(gy * xh, axis=1, keepdims=True))
    dg = jnp.sum(dy * xh, axis=0, keepdims=True)
    return dx, dg


def _rot_half(x):
    lane = lax.broadcasted_iota(jnp.int32, x.shape, 1)
    return jnp.where((lane % QK_ROPE) < QK_ROPE // 2, -pltpu.roll(x, LANES - 32, axis=1),
                     pltpu.roll(x, 32, axis=1))


def _rope_fwd_math(x, cos, sin):
    return x * cos + _rot_half(x) * sin


def _rope_bwd_math(dy, cos, sin):
    return dy * cos - _rot_half(dy * sin)


def _rms_rows(x, g):
    return x * lax.rsqrt(jnp.mean(x * x, axis=1, keepdims=True) + EPS) * g


def _attn_prep(h, g_attn, g_kvin, w_dq, g_ql, w_uq, w_kv, g_kvl, w_ukv, cos, sin):
    t, d = h.shape
    tr = 256
    wq = N_HEADS * HEAD_PAD

    def body(h_ref, ga_ref, gk_ref, wdq_ref, gq_ref, wuq_ref, wkv_ref, gl_ref, wukv_ref, c_ref, s_ref,
             hn_ref, hk_ref, cqp_ref, cq_ref, q_ref, kvp_ref, ckv_ref, kr_ref, knv_ref):
        xv, cv, sv = h_ref[...], c_ref[...], s_ref[...]
        xh = xv * lax.rsqrt(jnp.mean(xv * xv, axis=1, keepdims=True) + EPS)
        hn = (xh * ga_ref[...]).astype(BF16)
        hk = (xh * gk_ref[...]).astype(BF16)
        hn_ref[...], hk_ref[...] = hn, hk
        cq_pre = jnp.dot(hn, wdq_ref[...], preferred_element_type=F32)
        cqp_ref[...] = cq_pre
        cq = _rms_rows(cq_pre, gq_ref[...]).astype(BF16)
        cq_ref[...] = cq
        for hd in range(N_HEADS):
            lo = hd * HEAD_PAD
            qh = jnp.dot(cq, wuq_ref[:, lo:lo + HEAD_PAD], preferred_element_type=F32)
            q_ref[:, lo:lo + QK_NOPE] = qh[:, :QK_NOPE].astype(BF16)
            q_ref[:, lo + QK_NOPE:lo + HEAD_PAD] = _rope_fwd_math(qh[:, QK_NOPE:], cv, sv).astype(BF16)
        kvpre = jnp.dot(hk, wkv_ref[...], preferred_element_type=F32)
        kvp_ref[...] = kvpre
        ckv = _rms_rows(kvpre[:, :KV_LORA], gl_ref[...]).astype(BF16)
        ckv_ref[...] = ckv
        kr_ref[...] = _rope_fwd_math(kvpre[:, KV_LORA:], cv, sv).astype(BF16)
        for p in range(2):
            knv_ref[p] = jnp.dot(ckv, wukv_ref[p], preferred_element_type=F32).astype(BF16)

    rows = lambda w: pl.BlockSpec((tr, w), lambda i: (i, 0))
    whole = lambda a: pl.BlockSpec(a.shape, lambda i: (0,) * a.ndim)
    sds = lambda w, dt: jax.ShapeDtypeStruct((t, w), dt)
    args = (h, g_attn, g_kvin, w_dq, g_ql, w_uq, w_kv, g_kvl, w_ukv, cos, sin)
    return _tc_call(
        body, name="attn_prep", grid=(t // tr,),
        in_specs=[rows(d)] + [whole(a) for a in args[1:9]] + [rows(LANES), rows(LANES)],
        out_specs=[rows(d), rows(d), rows(Q_LORA), rows(Q_LORA), rows(wq), rows(KVP), rows(KV_LORA), rows(LANES),
                   pl.BlockSpec((2, tr, N_HEADS * QK_NOPE), lambda i: (0, i, 0))],
        out_shape=[sds(d, BF16), sds(d, BF16), sds(Q_LORA, F32), sds(Q_LORA, BF16), sds(wq, BF16), sds(KVP, F32),
                   sds(KV_LORA, BF16), sds(LANES, BF16), jax.ShapeDtypeStruct((2, t, N_HEADS * QK_NOPE), BF16)],
        compiler_params=_cp("parallel"),
    )(*args)


def _attn_prep_bwd(dq, dknv, dkr, dh, h, hn, hk, cq_pre, cq, kvpre, ckv, g_attn, g_kvin, w_dq, g_ql, w_uq, w_kv, g_kvl,
                   w_ukv, cos, sin):
    t, d = h.shape
    tr = 256
    n_steps = t // tr
    wq = N_HEADS * HEAD_PAD
    wk = N_HEADS * QK_NOPE

    def body(dq_ref, dknv_ref, dkr_ref, dh_ref, h_ref, hn_ref, hk_ref, cqp_ref, cq_ref, kvp_ref, ckv_ref,
             ga_ref, gk_ref, wdq_ref, gq_ref, wuq_ref, wkv_ref, gl_ref, wukv_ref, c_ref, s_ref,
             dho_ref, dhb_ref, dwuq_ref, dwdq_ref, dwukv_ref, dwkv_ref, dga_ref, dgk_ref, dgq_ref, dgl_ref,
             a_uq, a_dq, a_ukv, a_kv):
        i = pl.program_id(0)

        @pl.when(i == 0)
        def _():
            for ref in (a_uq, a_dq, a_ukv, a_kv, dga_ref, dgk_ref, dgq_ref, dgl_ref):
                ref[...] = jnp.zeros_like(ref)

        dqv = dq_ref[...]
        dcq = lax.dot_general(dqv, wuq_ref[...], NT_DIMS, preferred_element_type=F32)
        a_uq[...] += lax.dot_general(cq_ref[...], dqv, TN_DIMS, preferred_element_type=F32)
        dcq_pre, dg = _rms_bwd_math(cqp_ref[...], gq_ref[...], dcq)
        dgq_ref[...] += dg
        dcq_pre = dcq_pre.astype(BF16)
        dhn = lax.dot_general(dcq_pre, wdq_ref[...], NT_DIMS, preferred_element_type=F32)
        a_dq[...] += lax.dot_general(hn_ref[...], dcq_pre, TN_DIMS, preferred_element_type=F32)
        dckv = None
        for p in range(2):
            dk = dknv_ref[p].astype(BF16)
            part = lax.dot_general(dk, wukv_ref[p], NT_DIMS, preferred_element_type=F32)
            dckv = part if dckv is None else dckv + part
            a_ukv[p] += lax.dot_general(ckv_ref[...], dk, TN_DIMS, preferred_element_type=F32)
        dlat, dg = _rms_bwd_math(kvp_ref[:, :KV_LORA], gl_ref[...], dckv)
        dgl_ref[...] += dg
        dkr_pre = _rope_bwd_math(dkr_ref[...], c_ref[...], s_ref[...])
        dkvpre = jnp.concatenate([dlat, dkr_pre], axis=1).astype(BF16)
        dhk = lax.dot_general(dkvpre, wkv_ref[...], NT_DIMS, preferred_element_type=F32)
        a_kv[...] += lax.dot_general(hk_ref[...], dkvpre, TN_DIMS, preferred_element_type=F32)
        xv = h_ref[...]
        dx1, dg = _rms_bwd_math(xv, ga_ref[...], dhn)
        dga_ref[...] += dg
        dx2, dg = _rms_bwd_math(xv, gk_ref[...], dhk)
        dgk_ref[...] += dg
        dh_new = dh_ref[...] + dx1 + dx2
        dho_ref[...] = dh_new
        dhb_ref[...] = dh_new.astype(BF16)

        @pl.when(i == n_steps - 1)
        def _():
            dwuq_ref[...] = a_uq[...].astype(BF16)
            dwdq_ref[...] = a_dq[...].astype(BF16)
            dwukv_ref[...] = a_ukv[...].astype(BF16)
            dwkv_ref[...] = a_kv[...].astype(BF16)

    rows = lambda w: pl.BlockSpec((tr, w), lambda i: (i, 0))
    whole = lambda shape: pl.BlockSpec(shape, lambda i: (0,) * len(shape))
    weights = (g_attn, g_kvin, w_dq, g_ql, w_uq, w_kv, g_kvl, w_ukv)
    dw_shapes = [(Q_LORA, wq), (d, Q_LORA), (2, KV_LORA, wk), (d, KVP)]
    dg_shapes = [(1, d), (1, d), (1, Q_LORA), (1, KV_LORA)]
    return _tc_call(
        body, name="attn_prep_bwd", grid=(n_steps,),
        in_specs=[rows(wq), pl.BlockSpec((2, tr, wk), lambda i: (0, i, 0)), rows(LANES), rows(d), rows(d), rows(d),
                  rows(d), rows(Q_LORA), rows(Q_LORA), rows(KVP), rows(KV_LORA)]
        + [whole(a.shape) for a in weights] + [rows(LANES), rows(LANES)],
        out_specs=[rows(d), rows(d)] + [whole(s) for s in dw_shapes + dg_shapes],
        out_shape=[jax.ShapeDtypeStruct((t, d), F32), jax.ShapeDtypeStruct((t, d), BF16)]
        + [jax.ShapeDtypeStruct(s, BF16) for s in dw_shapes] + [jax.ShapeDtypeStruct(s, F32) for s in dg_shapes],
        scratch_shapes=[pltpu.VMEM(s, F32) for s in dw_shapes], compiler_params=_cp("arbitrary"),
    )(dq, dknv, dkr, dh, h, hn, hk, cq_pre, cq, kvpre, ckv, *weights, cos, sin)


ROW_CHUNK = 64
HALO = 16
WIN = ROW_CHUNK + 16
LANE_HALVES = (slice(0, LANES), slice(LANES, TC))


def _stage(s_ref, p, src):
    t = src.shape[0]
    s_ref[p, :HALO] = jnp.zeros((HALO, TC), BF16)
    s_ref[p, HALO:HALO + t] = src
    s_ref[p, HALO + t:] = jnp.zeros((HALO, TC), BF16)


def _window(s_ref, p, i, lanes):
    base = pl.multiple_of(i * ROW_CHUNK, ROW_CHUNK)
    return s_ref[p, pl.ds(base, ROW_CHUNK + 2 * HALO), lanes].astype(F32)[8:8 + WIN]


def _valid(x):
    return x[8:8 + ROW_CHUNK]


def _prev(x, k):
    return pltpu.roll(x, k, axis=0)


def _next(x, k):
    return pltpu.roll(x, WIN - k, axis=0)


def _taps(w_ref, lanes):
    return w_ref[0:1, lanes], w_ref[1:2, lanes], w_ref[2:3, lanes]


def _fold8(x):
    return jnp.sum(x.reshape(ROW_CHUNK // 8, 8, x.shape[-1]), axis=0)


def _store_rows(ref, idx, i, lanes, x):
    rows = pl.ds(pl.multiple_of(i * ROW_CHUNK, ROW_CHUNK), ROW_CHUNK)
    ref[(*idx, rows, lanes)] = x.astype(ref.dtype)


def _for_chunks(t, chunk):
    def step(i, carry):
        for lanes in LANE_HALVES:
            chunk(i, lanes)
        return carry

    lax.fori_loop(0, t // ROW_CHUNK, step, 0)


def _write_col_sums(acc_ref, outs):
    for k, (ref, row) in enumerate(outs):
        ref[row:row + 1, :] = jnp.sum(acc_ref[k], axis=0, keepdims=True)


def _shift_down(x, k):
    row = lax.broadcasted_iota(jnp.int32, x.shape, 0)
    return jnp.where(row >= k, pltpu.roll(x, k, axis=0), 0.0)


def _shift_up(x, k):
    n = x.shape[0]
    row = lax.broadcasted_iota(jnp.int32, x.shape, 0)
    return jnp.where(row < n - k, pltpu.roll(x, n - k, axis=0), 0.0)


def _conv3(x, w_ref):
    return _shift_down(x, 2) * w_ref[0:1, :] + _shift_down(x, 1) * w_ref[1:2, :] + x * w_ref[2:3, :]


def _col(parts, t):
    if parts is None:
        return pl.BlockSpec((t, TC), lambda j: (0, j))
    return pl.BlockSpec((parts, t, TC), lambda j: (0, 0, j))


def _staging(parts, t):
    return pltpu.VMEM((parts, t + 2 * HALO, TC), BF16)


def _scmix_fwd(z, w):
    t = z.shape[1]

    def body(z_ref, w_ref, m_ref):
        b, c, u = (z_ref[p].astype(F32) for p in range(3))
        m_ref[...] = (b * _conv3(c * u, w_ref)).astype(BF16)

    return _tc_call(
        body, name="scmix_fwd", grid=(D // TC,), in_specs=[_col(3, t), pl.BlockSpec((3, TC), lambda j: (0, j))],
        out_specs=_col(None, t), out_shape=jax.ShapeDtypeStruct((t, D), BF16), compiler_params=_cp("parallel"),
    )(z, w)


def _scmix_bwd(z, w, dm):
    t = z.shape[1]

    def body(z_ref, w_ref, dm_ref, dz_ref, dw_ref, s_ref, acc_ref):
        for p in range(3):
            _stage(s_ref, p, z_ref[p])
        _stage(s_ref, 3, dm_ref[...])
        acc_ref[...] = jnp.zeros_like(acc_ref)

        def chunk(i, lanes):
            w0, w1, w2 = _taps(w_ref, lanes)
            b, c, u, dm = (_window(s_ref, p, i, lanes) for p in range(4))
            cu = c * u
            cu1, cu2 = _prev(cu, 1), _prev(cu, 2)
            _store_rows(dz_ref, (0,), i, lanes, _valid(dm * (cu2 * w0 + cu1 * w1 + cu * w2)))
            dcv = dm * b
            dcu = dcv * w2 + _next(dcv, 1) * w1 + _next(dcv, 2) * w0
            _store_rows(dz_ref, (1,), i, lanes, _valid(dcu * u))
            _store_rows(dz_ref, (2,), i, lanes, _valid(dcu * c))
            for k, shifted in enumerate((cu2, cu1, cu)):
                acc_ref[k, :, lanes] += _fold8(_valid(dcv * shifted))

        _for_chunks(t, chunk)
        _write_col_sums(acc_ref, [(dw_ref, 0), (dw_ref, 1), (dw_ref, 2)])

    wspec = pl.BlockSpec((3, TC), lambda j: (0, j))
    return _tc_call(
        body, name="scmix_bwd", grid=(D // TC,), in_specs=[_col(3, t), wspec, _col(None, t)],
        out_specs=[_col(3, t), wspec],
        out_shape=[jax.ShapeDtypeStruct((3, t, D), BF16), jax.ShapeDtypeStruct((3, D), F32)],
        scratch_shapes=[_staging(4, t), pltpu.VMEM((3, 8, TC), F32)], compiler_params=_cp("parallel"),
    )(z, w, dm)


def _ffn_up_gate(hf, w_up, w, bias, name):
    t, d = hf.shape
    nb = F_FF // TC

    def body(hf_ref, wg_ref, wv_ref, w_ref, b_ref, up_ref, a_ref, prev_ref):
        @pl.when(pl.program_id(0) == 0)
        def _():
            prev_ref[...] = jnp.zeros_like(prev_ref)

        gc = _conv3(prev_ref[0].astype(F32), w_ref) + b_ref[...]
        a_ref[...] = (gc * jax.nn.sigmoid(gc) * prev_ref[1].astype(F32)).astype(BF16)
        hv = hf_ref[...]
        up_ref[0] = jnp.dot(hv, wg_ref[...], preferred_element_type=F32).astype(BF16)
        up_ref[1] = jnp.dot(hv, wv_ref[...], preferred_element_type=F32).astype(BF16)
        prev_ref[...] = up_ref[...]

    tile = lambda j: jnp.minimum(j, nb - 1)
    gated = lambda j: jnp.maximum(j - 1, 0)
    return _tc_call(
        body, name=name, grid=(nb + 1,),
        in_specs=[pl.BlockSpec((t, d), lambda j: (0, 0)), pl.BlockSpec((d, TC), lambda j: (0, tile(j))),
                  pl.BlockSpec((d, TC), lambda j: (0, nb + tile(j))), pl.BlockSpec((3, TC), lambda j: (0, gated(j))),
                  pl.BlockSpec((1, TC), lambda j: (0, gated(j)))],
        out_specs=[pl.BlockSpec((2, t, TC), lambda j: (0, 0, tile(j))), pl.BlockSpec((t, TC), lambda j: (0, gated(j)))],
        out_shape=[jax.ShapeDtypeStruct((2, t, F_FF), BF16), jax.ShapeDtypeStruct((t, F_FF), BF16)],
        scratch_shapes=[pltpu.VMEM((2, t, TC), BF16)], compiler_params=_cp("arbitrary"),
    )(hf, w_up, w_up, w, bias)


def _gate_bwd(up, w, bias, dh, w_down, name):
    t, d = dh.shape

    def body(u_ref, w_ref, b_ref, dh_ref, wd_ref, du_ref, dw_ref, db_ref, s_ref, acc_ref):
        for p in range(2):
            _stage(s_ref, p, u_ref[p])
        _stage(s_ref, 2, lax.dot_general(dh_ref[...], wd_ref[...], NT_DIMS, preferred_element_type=F32).astype(BF16))
        acc_ref[...] = jnp.zeros_like(acc_ref)

        def chunk(i, lanes):
            w0, w1, w2 = _taps(w_ref, lanes)
            g, v, da = (_window(s_ref, p, i, lanes) for p in range(3))
            g1, g2 = _prev(g, 1), _prev(g, 2)
            gc = g2 * w0 + g1 * w1 + g * w2 + b_ref[:, lanes]
            sg = jax.nn.sigmoid(gc)
            _store_rows(du_ref, (1,), i, lanes, _valid(da * (gc * sg)))
            dgc = da * v * (sg * (1.0 + gc * (1.0 - sg)))
            _store_rows(du_ref, (0,), i, lanes, _valid(dgc * w2 + _next(dgc, 1) * w1 + _next(dgc, 2) * w0))
            for k, shifted in enumerate((g2, g1, g)):
                acc_ref[k, :, lanes] += _fold8(_valid(dgc * shifted))
            acc_ref[3, :, lanes] += _fold8(_valid(dgc))

        _for_chunks(t, chunk)
        _write_col_sums(acc_ref, [(dw_ref, 0), (dw_ref, 1), (dw_ref, 2), (db_ref, 0)])

    wspec = pl.BlockSpec((3, TC), lambda j: (0, j))
    bspec = pl.BlockSpec((1, TC), lambda j: (0, j))
    return _tc_call(
        body, name=name, grid=(F_FF // TC,),
        in_specs=[_col(2, t), wspec, bspec, pl.BlockSpec((t, d), lambda j: (0, 0)), pl.BlockSpec((TC, d), lambda j: (j, 0))],
        out_specs=[_col(2, t), wspec, bspec],
        out_shape=[jax.ShapeDtypeStruct((2, t, F_FF), BF16), jax.ShapeDtypeStruct((3, F_FF), F32),
                   jax.ShapeDtypeStruct((1, F_FF), F32)],
        scratch_shapes=[_staging(3, t), pltpu.VMEM((4, 8, TC), F32)], compiler_params=_cp("parallel"),
    )(up, w, bias, dh, w_down)


ATT_TQ = 256
ATT_SCALE = (QK_NOPE + QK_ROPE) ** -0.5


def _key_ranges(lvl):
    lo = lvl * ATT_TQ
    return ([(0, lo, False)] if lvl else []) + [(lo, lo + ATT_TQ, True)]


FWD_HEADS = 4
BWD_HEADS = 2


def _fill_keys(k_ref, kn_ref, kr_ref):
    @pl.when(pl.program_id(1) == 0)
    def _():
        for hh in range(k_ref.shape[0]):
            k_ref[hh, :, :QK_NOPE] = kn_ref[:, hh * QK_NOPE:(hh + 1) * QK_NOPE]
            k_ref[hh, :, QK_NOPE:] = kr_ref[...]


def _attn_probs(q, k_ref, lvl):
    scores = []
    for lo, hi, diagonal in _key_ranges(lvl):
        s = lax.dot_general(q, k_ref[lo:hi, :], NT_DIMS, preferred_element_type=F32) * ATT_SCALE
        if diagonal:
            row = lax.broadcasted_iota(jnp.int32, s.shape, 0)
            col = lax.broadcasted_iota(jnp.int32, s.shape, 1)
            seen = lax.shift_right_logical(col, CHUNK_SHIFT) <= lax.shift_right_logical(row, CHUNK_SHIFT)
            s = jnp.where(seen, s, NEG_INF)
        scores.append(s)
    m = jnp.max(scores[0], axis=1, keepdims=True)
    for s in scores[1:]:
        m = jnp.maximum(m, jnp.max(s, axis=1, keepdims=True))
    ps = [jnp.exp(s - m) for s in scores]
    total = jnp.sum(ps[0], axis=1, keepdims=True)
    for p in ps[1:]:
        total = total + jnp.sum(p, axis=1, keepdims=True)
    inv = 1.0 / total
    return [p * inv for p in ps]


def _attn_probs_t(q, k_ref, lvl):
    scores = []
    for lo, hi, diagonal in _key_ranges(lvl):
        s = lax.dot_general(k_ref[lo:hi, :], q, NT_DIMS, preferred_element_type=F32) * ATT_SCALE
        if diagonal:
            key = lax.broadcasted_iota(jnp.int32, s.shape, 0)
            qry = lax.broadcasted_iota(jnp.int32, s.shape, 1)
            seen = lax.shift_right_logical(key, CHUNK_SHIFT) <= lax.shift_right_logical(qry, CHUNK_SHIFT)
            s = jnp.where(seen, s, NEG_INF)
        scores.append(s)
    m = jnp.max(scores[0], axis=0, keepdims=True)
    for s in scores[1:]:
        m = jnp.maximum(m, jnp.max(s, axis=0, keepdims=True))
    ps = [jnp.exp(s - m) for s in scores]
    total = jnp.sum(ps[0], axis=0, keepdims=True)
    for p in ps[1:]:
        total = total + jnp.sum(p, axis=0, keepdims=True)
    inv = 1.0 / total
    return [p * inv for p in ps]


def _per_query_block(qi, n_blocks, branch):
    for lvl in range(n_blocks):
        pl.when(qi == lvl)(lambda lvl=lvl: branch(lvl))


def _attn_specs(t, g):
    q = pl.BlockSpec((ATT_TQ, g * HEAD_PAD), lambda h, i: (i, h))
    kn = pl.BlockSpec((None, t, g * QK_NOPE), lambda h, i: (0, 0, h))
    kr = pl.BlockSpec((t, LANES), lambda h, i: (0, 0))
    v = pl.BlockSpec((None, t, g * V_HEAD), lambda h, i: (1, 0, h))
    o = pl.BlockSpec((ATT_TQ, g * V_HEAD), lambda h, i: (i, h))
    return q, kn, kr, v, o


def _attn_fwd(q, knv, kr):
    t = q.shape[0]

    def body(q_ref, kn_ref, kr_ref, v_ref, o_ref, k_ref):
        _fill_keys(k_ref, kn_ref, kr_ref)

        def branch(lvl):
            for hh in range(FWD_HEADS):
                vcols = slice(hh * V_HEAD, (hh + 1) * V_HEAD)
                ps = _attn_probs(q_ref[:, hh * HEAD_PAD:(hh + 1) * HEAD_PAD], k_ref.at[hh], lvl)
                o = None
                for p, (lo, hi, _) in zip(ps, _key_ranges(lvl)):
                    part = jnp.dot(p.astype(BF16), v_ref[lo:hi, vcols], preferred_element_type=F32)
                    o = part if o is None else o + part
                o_ref[:, vcols] = o.astype(BF16)

        _per_query_block(pl.program_id(1), t // ATT_TQ, branch)

    qs, kns, krs, vs, os_ = _attn_specs(t, FWD_HEADS)
    return _tc_call(
        body, name="attn_fwd", grid=(N_HEADS // FWD_HEADS, t // ATT_TQ), in_specs=[qs, kns, krs, vs],
        out_specs=os_, out_shape=jax.ShapeDtypeStruct((t, N_HEADS * V_HEAD), BF16),
        scratch_shapes=[pltpu.VMEM((FWD_HEADS, t, HEAD_PAD), BF16)], compiler_params=_cp("parallel", "arbitrary"),
    )(q, knv, kr, knv)


def _attn_bwd(q, knv, kr, do, cos, sin):
    t = q.shape[0]

    def body(q_ref, kn_ref, kr_ref, v_ref, do_ref, c_ref, s_ref, dq_ref, dknv_ref, dkr_ref, k_ref, dk_ref):
        h, qi = pl.program_id(0), pl.program_id(1)
        _fill_keys(k_ref, kn_ref, kr_ref)

        @pl.when(qi == 0)
        def _():
            dknv_ref[1] = jnp.zeros(dknv_ref.shape[1:], F32)
            dk_ref[...] = jnp.zeros_like(dk_ref)

        @pl.when((qi == 0) & (h == 0))
        def _():
            dkr_ref[...] = jnp.zeros_like(dkr_ref)

        def branch(lvl):
            ranges = _key_ranges(lvl)
            for hh in range(BWD_HEADS):
                qcols = slice(hh * HEAD_PAD, (hh + 1) * HEAD_PAD)
                vcols = slice(hh * V_HEAD, (hh + 1) * V_HEAD)
                qv, dov = q_ref[:, qcols], do_ref[:, vcols]
                ps = _attn_probs_t(qv, k_ref.at[hh], lvl)
                dps = [lax.dot_general(v_ref[lo:hi, vcols], dov, NT_DIMS, preferred_element_type=F32)
                       for lo, hi, _ in ranges]
                di = None
                for p, dp in zip(ps, dps):
                    part = jnp.sum(p * dp, axis=0, keepdims=True)
                    di = part if di is None else di + part
                dq = None
                for p, dp, (lo, hi, _) in zip(ps, dps, ranges):
                    ds = (p * (dp - di) * ATT_SCALE).astype(BF16)
                    part = lax.dot_general(ds, k_ref[hh, lo:hi, :], TN_DIMS, preferred_element_type=F32)
                    dq = part if dq is None else dq + part
                    dk_ref[hh, lo:hi, :] += jnp.dot(ds, qv, preferred_element_type=F32)
                    dknv_ref[1, lo:hi, vcols] += jnp.dot(p.astype(BF16), dov, preferred_element_type=F32)
                dq_ref[:, hh * HEAD_PAD:hh * HEAD_PAD + QK_NOPE] = dq[:, :QK_NOPE].astype(BF16)
                dq_ref[:, hh * HEAD_PAD + QK_NOPE:(hh + 1) * HEAD_PAD] = _rope_bwd_math(
                    dq[:, QK_NOPE:], c_ref[...], s_ref[...]).astype(BF16)

        _per_query_block(qi, t // ATT_TQ, branch)

        @pl.when(qi == t // ATT_TQ - 1)
        def _():
            for hh in range(BWD_HEADS):
                dknv_ref[0, :, hh * QK_NOPE:(hh + 1) * QK_NOPE] = dk_ref[hh, :, :QK_NOPE]
                dkr_ref[...] += dk_ref[hh, :, QK_NOPE:]

    qs, kns, krs, vs, os_ = _attn_specs(t, BWD_HEADS)
    tab = pl.BlockSpec((ATT_TQ, LANES), lambda h, i: (i, 0))
    return _tc_call(
        body, name="attn_bwd", grid=(N_HEADS // BWD_HEADS, t // ATT_TQ), in_specs=[qs, kns, krs, vs, os_, tab, tab],
        out_specs=[qs, pl.BlockSpec((2, t, BWD_HEADS * QK_NOPE), lambda h, i: (0, 0, h)), krs],
        out_shape=[jax.ShapeDtypeStruct((t, N_HEADS * HEAD_PAD), BF16),
                   jax.ShapeDtypeStruct((2, t, N_HEADS * QK_NOPE), F32), jax.ShapeDtypeStruct((t, LANES), F32)],
        scratch_shapes=[pltpu.VMEM((BWD_HEADS, t, HEAD_PAD), BF16), pltpu.VMEM((BWD_HEADS, t, HEAD_PAD), F32)],
        compiler_params=_cp("arbitrary", "arbitrary"),
    )(q, knv, kr, knv, do, cos, sin)


def _adam_math(w, g, m, v):
    nm = ADAM_B1 * m + (1.0 - ADAM_B1) * g
    nv = ADAM_B2 * v + (1.0 - ADAM_B2) * (g * g)
    m_hat = nm / (1.0 - ADAM_B1 ** ADAM_STEP)
    v_hat = nv / (1.0 - ADAM_B2 ** ADAM_STEP)
    return -ADAM_LR * (m_hat / (jnp.sqrt(v_hat) + ADAM_EPS) + ADAM_WD * w), nm, nv


def _adamw_small(ws, gs, ms, vs):
    n = len(ws)

    def body(*refs):
        for i in range(n):
            w_ref, g_ref, m_ref, v_ref = (refs[k * n + i] for k in range(4))
            go_ref, d_ref, nm_ref, nv_ref = (refs[(4 + k) * n + i] for k in range(4))
            go_ref[...] = g_ref[...]
            d_ref[...], nm_ref[...], nv_ref[...] = _adam_math(w_ref[...], g_ref[...], m_ref[...], v_ref[...])

    shapes = [jax.ShapeDtypeStruct(a.shape, F32) for a in ws]
    res = _tc_call(body, name="adamw_small", out_shape=shapes * 4)(*ws, *gs, *ms, *vs)
    return [res[k * n:(k + 1) * n] for k in range(4)]


ADAM_SPLIT = 4


def _store_without_head_padding(dst_ref, g):
    assert HEAD_PAD == 2 * LANES and 2 * (QK_NOPE + QK_ROPE) == 3 * LANES, (HEAD_PAD, QK_NOPE, QK_ROPE)
    assert g.shape[1] % (2 * HEAD_PAD) == 0, g.shape
    low = lax.broadcasted_iota(jnp.int32, (g.shape[0], LANES), 1) < LANES // 2
    for pair in range(g.shape[1] // (2 * HEAD_PAD)):
        t = [g[:, (4 * pair + k) * LANES:(4 * pair + k + 1) * LANES] for k in range(4)]
        moved = [pltpu.roll(t[k], LANES // 2, axis=1) for k in (2, 3)]
        outs = (t[0], jnp.where(low, t[1], moved[0]), jnp.where(low, moved[0], moved[1]))
        for k, o in enumerate(outs):
            dst_ref[:, (3 * pair + k) * LANES:(3 * pair + k + 1) * LANES] = o


def _adamw_shards(ids, items, name):
    n = len(items)

    def body(ids_ref, *refs):
        outs = refs[len(refs) - 4 * n:]
        for i, it in enumerate(items):
            w_ref, m_ref, v_ref, gm_ref, gs_ref = refs[5 * i:5 * i + 5]
            g_ref, d_ref, nm_ref, nv_ref = outs[4 * i:4 * i + 4]
            cols = slice(*it["gcols"]) if it.get("gcols") else slice(None)
            whose = pl.program_id(0) if it.get("owner") is None else it["owner"]
            mine = whose == ids_ref[0]

            def take(src_ref, g_ref=g_ref, cols=cols, head_padded=it.get("head_padded")):
                if head_padded:
                    _store_without_head_padding(g_ref, src_ref[...])
                else:
                    g_ref[...] = src_ref[:, cols]

            @pl.when(mine)
            def _(take=take, gm_ref=gm_ref):
                take(gm_ref)

            @pl.when(jnp.logical_not(mine))
            def _(take=take, gs_ref=gs_ref):
                take(gs_ref)

            d_ref[...], nm_ref[...], nv_ref[...] = _adam_math(w_ref[...], g_ref[...], m_ref[...], v_ref[...])

    in_specs, out_specs, out_shape, args, carried, aliases = [], [], [], [ids], [], {}
    for i, it in enumerate(items):
        w = it["w"]
        r, c = w.shape[-2:]
        tr = r // 2 // ADAM_SPLIT
        assert tr % 8 == 0, (name, w.shape)
        layer = it.get("layer")
        if layer is None:
            wspec = pl.BlockSpec((tr, c), lambda h, k, ids: (h * ADAM_SPLIT + k, 0))
        else:
            wspec = pl.BlockSpec((None, tr, c), lambda h, k, ids, layer=layer: (layer, h * ADAM_SPLIT + k, 0))
        gc = it["g_mine"].shape[1]

        def g_index(of_mine, owner=it.get("owner")):
            def index(h, k, ids):
                if owner is None:
                    half = ids[0] if of_mine else 1 - ids[0]
                    return jnp.where(h == half, k, jnp.where(h < half, 0, ADAM_SPLIT - 1)), 0
                read = (owner == ids[0]) if of_mine else (owner != ids[0])
                return jnp.where(read, h * ADAM_SPLIT + k, 0), 0
            return index

        in_specs += [wspec] * 3 + [pl.BlockSpec((tr, gc), g_index(True)), pl.BlockSpec((tr, gc), g_index(False))]
        args += [w, it["m"], it["v"], it["g_mine"], it["g_sib"]]
        out_specs += [wspec] * 4
        out_shape += [jax.ShapeDtypeStruct(w.shape, F32)] * 4
        if it.get("prev") is not None:
            for k, p in enumerate(it["prev"]):
                aliases[1 + 5 * n + len(carried)] = 4 * i + k
                carried.append(p)
    res = _tc_call(
        body, name=name, prefetch=1, grid=(2, ADAM_SPLIT), in_specs=in_specs + [ANY] * len(carried),
        out_specs=out_specs, out_shape=out_shape, input_output_aliases=aliases,
        compiler_params=_cp("parallel", "parallel"),
    )(*args, *carried)
    return [res[4 * i:4 * i + 4] for i in range(n)]


def _peer_chip(k_me, j):
    return k_me ^ jnp.where(j == 0, 2, jnp.where(j == 1, 1, 3))


def _pair_sums(ids, gs, ras, name):
    n = len(gs)

    def body(ids_ref, *refs):
        for i in range(n):
            g_ref, ra_ref, o_ref = refs[2 * i], refs[2 * i + 1], refs[2 * n + i]
            o_ref[...] = (g_ref[...].astype(F32) + ra_ref[...].astype(F32)).astype(BF16)

    in_specs, out_specs, out_shape = [], [], []
    for g in gs:
        half, c = g.shape[1] // 2, g.shape[2]
        in_specs += [pl.BlockSpec((None, half, c), lambda j, ids: (_peer_chip(ids[1], j), ids[0], 0)),
                     pl.BlockSpec((None, half, c), lambda j, ids: (_peer_chip(ids[1], j), 0, 0))]
        out_specs.append(pl.BlockSpec((None, half, c), lambda j, ids: (j, 0, 0)))
        out_shape.append(jax.ShapeDtypeStruct((3, half, c), BF16))
    return _tc_call(
        body, name=name, prefetch=1, grid=(3,), in_specs=in_specs, out_specs=out_specs, out_shape=out_shape,
        compiler_params=_cp("parallel"),
    )(ids, *[a for pair in zip(gs, ras) for a in pair])


def _chip_sums(ids, gs, ras, rbs, name):
    n = len(gs)

    def body(ids_ref, *refs):
        for i in range(n):
            g_ref, ra_ref, rb_ref, o_ref = refs[3 * i], refs[3 * i + 1], refs[3 * i + 2], refs[3 * n + i]
            acc = g_ref[...].astype(F32) + ra_ref[...].astype(F32)
            for j in range(3):
                acc = acc + rb_ref[j].astype(F32)
            o_ref[...] = acc

    in_specs, out_specs, out_shape = [], [], []
    for g in gs:
        half, c = g.shape[1] // 2, g.shape[2]
        in_specs += [pl.BlockSpec((None, half, c), lambda i, ids: (ids[1], ids[0], 0)),
                     pl.BlockSpec((None, half, c), lambda i, ids: (ids[1], 0, 0)),
                     pl.BlockSpec((3, half, c), lambda i, ids: (0, 0, 0))]
        out_specs.append(pl.BlockSpec((half, c), lambda i, ids: (0, 0)))
        out_shape.append(jax.ShapeDtypeStruct((half, c), F32))
    return _tc_call(
        body, name=name, prefetch=1, grid=(1,), in_specs=in_specs, out_specs=out_specs, out_shape=out_shape,
        compiler_params=_cp("arbitrary"),
    )(ids, *[a for trio in zip(gs, ras, rbs) for a in trio])


def _position():
    x, y, c = lax.axis_index("x"), lax.axis_index("y"), lax.axis_index("c")
    chips = [(1 - x, y), (x, 1 - y), (1 - x, 1 - y)]
    return x, y, c, chips


def _shard_half(ref, wm, h):
    if wm.kind == "tiny":
        return ref
    if wm.nl == 2:
        return ref.at[h]
    return ref.at[pl.ds(pl.multiple_of(h * (wm.k // 2), 16), wm.k // 2), :]


def _region(full, wm, s, h):
    if wm.kind == "tiny":
        return full.at[s]
    cols = pl.ds(pl.multiple_of(s * wm.n, LANES), wm.n) if wm.kind == "col" else slice(None)
    if wm.nl == 2:
        rows = pl.ds(pl.multiple_of(s * wm.k, 16), wm.k) if wm.kind == "row" else slice(None)
        return full.at[slice(None) if h is None else h, rows, cols]
    if wm.kind == "col":
        rows = slice(None) if h is None else pl.ds(pl.multiple_of(h * (wm.k // 2), 16), wm.k // 2)
    elif h is None:
        rows = pl.ds(pl.multiple_of(s * wm.k, 16), wm.k)
    else:
        rows = pl.ds(pl.multiple_of(s * wm.k + h * (wm.k // 2), 16), wm.k // 2)
    return full.at[rows, cols]


def _full_shape(wm):
    if wm.kind == "tiny":
        return (N_CHIPS, wm.k, wm.n)
    shape = (wm.k, N_CHIPS * wm.n) if wm.kind == "col" else (N_CHIPS * wm.k, wm.n)
    return shape if wm.nl == 1 else (wm.nl,) + shape


def _handshake(peers):
    barrier = pltpu.get_barrier_semaphore()
    for peer in peers:
        pl.semaphore_signal(barrier, inc=1, device_id=peer, device_id_type=MESH)
    pl.semaphore_wait(barrier, len(peers))


def _all_gather_group(gi, shards):
    wms = AG_GROUPS[gi]
    nw = len(wms)

    def body(*refs):
        sh, full = refs[:nw], refs[nw:2 * nw]
        ici_s, ici_r, pass_s, pass_r, own_s, own_r = refs[2 * nw:]
        x, y, c, _ = _position()
        me, sibling = 2 * x + y, (x, y, 1 - c)
        first, second, diagonal = (x ^ (1 - c), y ^ c), (x ^ c, y ^ (1 - c)), (1 - x, 1 - y)
        chip_id = lambda chip: 2 * chip[0] + chip[1]
        _handshake([(*first, c), (*second, c), sibling])

        def rcopy(src, dst, s_sem, r_sem, to):
            return pltpu.make_async_remote_copy(src_ref=src, dst_ref=dst, send_sem=s_sem, recv_sem=r_sem,
                                                device_id=to, device_id_type=MESH)

        started = []

        def go(cp):
            cp.start()
            started.append(cp)

        for i, wm in enumerate(wms):
            half, dst = _shard_half(sh[i], wm, c), _region(full[i], wm, me, c)
            go(rcopy(half, dst, ici_s.at[i, 0], ici_r.at[i, 0], (*first, c)))
            go(rcopy(half, dst, ici_s.at[i, 1], ici_r.at[i, 1], (*second, c)))
            go(rcopy(sh[i], _region(full[i], wm, me, None), own_s.at[i], own_r.at[i], sibling))
        for i, wm in enumerate(wms):
            got = _region(full[i], wm, chip_id(first), c)
            rcopy(got, got, ici_s.at[i, 0], ici_r.at[i, 0], sibling).wait_recv()
            go(rcopy(got, got, ici_s.at[i, 2], ici_r.at[i, 2], (*second, c)))
            if wm.kind != "tiny":
                go(rcopy(got, got, pass_s.at[i, 0], pass_r.at[i, 0], sibling))
        for i, wm in enumerate(wms):
            for j, chip in ((1, second), (2, diagonal)):
                got = _region(full[i], wm, chip_id(chip), c)
                rcopy(got, got, ici_s.at[i, j], ici_r.at[i, j], sibling).wait_recv()
                if wm.kind != "tiny":
                    go(rcopy(got, got, pass_s.at[i, j], pass_r.at[i, j], sibling))
        for i, wm in enumerate(wms):
            mine = _region(full[i], wm, me, None)
            rcopy(mine, mine, own_s.at[i], own_r.at[i], sibling).wait_recv()
            if wm.kind != "tiny":
                for j, chip in ((0, second), (1, first), (2, diagonal)):
                    got = _region(full[i], wm, chip_id(chip), 1 - c)
                    rcopy(got, got, pass_s.at[i, j], pass_r.at[i, j], sibling).wait_recv()
        for cp in started:
            cp.wait_send()

    return pl.kernel(
        body, out_type=[jax.ShapeDtypeStruct(_full_shape(wm), s.dtype) for wm, s in zip(wms, shards)],
        mesh=plsc.ScalarSubcoreMesh(axis_name="sequencer", num_cores=1), name=f"ag_group{gi}",
        scratch_types=[pltpu.SemaphoreType.DMA((nw, 3))] * 4 + [pltpu.SemaphoreType.DMA((nw,))] * 2,
        compiler_params=pltpu.CompilerParams(collective_id=gi),
    )(*shards)


def _sequencer_call(body, name, cid, out_types, scratch, args):
    return pl.kernel(
        body, out_type=out_types, mesh=plsc.ScalarSubcoreMesh(axis_name="sequencer", num_cores=1), name=name,
        scratch_types=scratch, compiler_params=pltpu.CompilerParams(collective_id=cid),
    )(*args)


def _pair_exchange(gs, tag, cid):
    n = len(gs)

    def body(*refs):
        g, out, send_sems, recv_sems = refs[:n], refs[n:2 * n], refs[2 * n], refs[2 * n + 1]
        x, y, c, _ = _position()
        _handshake([(x, y, 1 - c)])
        cps = []
        for i in range(n):
            half = g[i].shape[1] // 2
            cps.append(pltpu.make_async_remote_copy(
                src_ref=g[i].at[:, pl.ds(pl.multiple_of((1 - c) * half, 16), half), :], dst_ref=out[i],
                send_sem=send_sems.at[i], recv_sem=recv_sems.at[i], device_id=(x, y, 1 - c), device_id_type=MESH))
            cps[-1].start()
        for cp in cps:
            cp.wait()

    return _sequencer_call(
        body, f"rs_pair_exchange{tag}", cid,
        [jax.ShapeDtypeStruct((a.shape[0], a.shape[1] // 2, a.shape[2]), a.dtype) for a in gs],
        [pltpu.SemaphoreType.DMA((n,)), pltpu.SemaphoreType.DMA((n,))], gs)


def _chip_exchange(ss, tag, cid):
    n = len(ss)

    def body(*refs):
        s, out, send_sems, recv_sems = refs[:n], refs[n:2 * n], refs[2 * n], refs[2 * n + 1]
        x, y, c, chips = _position()
        _handshake([(*chip, c) for chip in chips])
        cps = []
        for i in range(n):
            for j, chip in enumerate(chips):
                cps.append(pltpu.make_async_remote_copy(
                    src_ref=s[i].at[j], dst_ref=out[i].at[j], send_sem=send_sems.at[i, j], recv_sem=recv_sems.at[i, j],
                    device_id=(*chip, c), device_id_type=MESH))
                cps[-1].start()
        for cp in cps:
            cp.wait()

    return _sequencer_call(
        body, f"rs_chip_exchange{tag}", cid, [jax.ShapeDtypeStruct(a.shape, a.dtype) for a in ss],
        [pltpu.SemaphoreType.DMA((n, 3)), pltpu.SemaphoreType.DMA((n, 3))], ss)


def _pair_swap(g8s, tag, cid):
    n = len(g8s)

    def body(*refs):
        g, out, send_sems, recv_sems = refs[:n], refs[n:2 * n], refs[2 * n], refs[2 * n + 1]
        x, y, c, _ = _position()
        _handshake([(x, y, 1 - c)])
        cps = []
        for i in range(n):
            cps.append(pltpu.make_async_remote_copy(
                src_ref=g[i], dst_ref=out[i], send_sem=send_sems.at[i], recv_sem=recv_sems.at[i],
                device_id=(x, y, 1 - c), device_id_type=MESH))
            cps[-1].start()
        for cp in cps:
            cp.wait()

    return _sequencer_call(
        body, f"rs_pair_swap{tag}", cid, [jax.ShapeDtypeStruct(a.shape, a.dtype) for a in g8s],
        [pltpu.SemaphoreType.DMA((n,)), pltpu.SemaphoreType.DMA((n,))], g8s)


def _pair_swap_now(g8s):
    n = len(g8s)

    def body(*refs):
        g, out, send_sems, recv_sems = refs[:n], refs[n:2 * n], refs[2 * n], refs[2 * n + 1]
        x, y, c, _ = _position()
        cps = []
        for i in range(n):
            cps.append(pltpu.make_async_remote_copy(
                src_ref=g[i], dst_ref=out[i], send_sem=send_sems.at[i], recv_sem=recv_sems.at[i],
                device_id=(x, y, 1 - c), device_id_type=MESH))
            cps[-1].start()
        for cp in cps:
            cp.wait()

    return _tc_call(
        body, name="rs_pair_swap_last", in_specs=[ANY] * n, out_specs=[ANY] * n,
        out_shape=[jax.ShapeDtypeStruct(a.shape, a.dtype) for a in g8s],
        scratch_shapes=[pltpu.SemaphoreType.DMA((n,)), pltpu.SemaphoreType.DMA((n,))],
    )(*g8s)


def _all_reduce_small(vecs, owner_major, name):
    n = len(vecs)
    block = lambda i, ref, chip: ref.at[chip] if owner_major[i] else ref
    out_shapes = [a.shape[1:] if owner_major[i] else a.shape for i, a in enumerate(vecs)]

    def body(*refs):
        v, o, gath = refs[:n], refs[n:2 * n], refs[2 * n:3 * n]
        send_sems, recv_sems = refs[3 * n], refs[3 * n + 1]
        x, y, c, _ = _position()
        me = 4 * x + 2 * y + c
        cps = []
        for i in range(n):
            gath[i][me] = block(i, v[i], 2 * x + y)[...]
            for rel in range(1, N_DEV):
                px, py, pc = x ^ (rel >> 2), y ^ ((rel >> 1) & 1), c ^ (rel & 1)
                cps.append(pltpu.make_async_remote_copy(
                    src_ref=block(i, v[i], 2 * px + py), dst_ref=gath[i].at[me], send_sem=send_sems.at[i, rel - 1],
                    recv_sem=recv_sems.at[i, rel - 1], device_id=(px, py, pc), device_id_type=MESH))
        for cp in cps:
            cp.start()
        for i in range(n):
            for rel in range(1, N_DEV):
                pltpu.make_async_remote_copy(
                    src_ref=block(i, v[i], 2 * x + y), dst_ref=gath[i].at[me ^ rel],
                    send_sem=send_sems.at[i, rel - 1], recv_sem=recv_sems.at[i, rel - 1], device_id=(x, y, c),
                    device_id_type=MESH).wait_recv()
        for cp in cps:
            cp.wait_send()
        for i in range(n):
            acc = gath[i][0]
            for d in range(1, N_DEV):
                acc = acc + gath[i][d]
            o[i][...] = acc

    vm = pl.BlockSpec(memory_space=pltpu.VMEM)
    return _tc_call(
        body, name=name, in_specs=[vm] * n, out_specs=[vm] * n,
        out_shape=[jax.ShapeDtypeStruct(s, F32) for s in out_shapes],
        scratch_shapes=[pltpu.VMEM((N_DEV,) + s, F32) for s in out_shapes]
        + [pltpu.SemaphoreType.DMA((n, N_DEV - 1)), pltpu.SemaphoreType.DMA((n, N_DEV - 1))],
    )(*vecs)


def _rope_tables(positions):
    half = QK_ROPE // 2
    inv_freq = 1.0 / (ROPE_THETA ** (jnp.arange(half, dtype=F32) / half))
    ang = positions.astype(F32)[:, None] * inv_freq
    zeros = jnp.zeros((positions.shape[0], LANES - QK_ROPE), F32)
    cos, sin = jnp.cos(ang), jnp.sin(ang)
    return jnp.concatenate([cos, cos, zeros], axis=1), jnp.concatenate([sin, sin, zeros], axis=1)


def _local_step(x, positions, tgt, wf, small, rs):
    cos, sin = _rope_tables(positions)
    w_in, w_out = wf["sc_w_in"], wf["sc_w_out"]
    w_ups, w_downs = (wf["ffn_w_up0"], wf["ffn_w_up1"]), (wf["ffn_w_down0"], wf["ffn_w_down1"])
    w_kv, w_ukv, w_dq, w_uq, w_o = wf["w_kv"], wf["w_ukv"], wf["w_dq"], wf["w_uq"], wf["w_o"]
    attn_norm, ffn_norm = small["attn_norm"], small["ffn_norm"]
    conv_b = small["ffn_conv_b"]

    def ffn_fwd(h, hf, l, then):
        up, a = _ffn_up_gate(hf, w_ups[l], small["ffn_conv_w"][l], conv_b[l:l + 1], f"ffn{l}_up_gate")
        return then(a, w_downs[l], h), (hf, up, a)

    def ffn_bwd(h, dh_out, dh_out_b, l, saved, gi, hooks):
        run = lambda stage: hooks.get(stage, lambda: None)()
        hf, up, a = saved
        d_down = _tn(f"ffn{l}_down_dw", a, dh_out_b, BF16)
        run("down_dw")
        dup, d_cw, d_cb = _gate_bwd(up, small["ffn_conv_w"][l], conv_b[l:l + 1], dh_out_b, w_downs[l],
                                    f"ffn{l}_gate_bwd")
        run("gate_bwd")
        d_up = _dw_ffn_up(f"ffn{l}_up_dw", hf, dup)
        rs.start(gi, {f"ffn_w_down{l}": d_down.reshape(N_CHIPS, F_FF // N_CHIPS, D), f"ffn_w_up{l}": d_up})
        run("up_dw")
        dh, dh_b, d_norm = _dx_norm_bwd(f"ffn{l}_up_dx", dup, w_ups[l], h, ffn_norm[l:l + 1], dh_out)
        run("up_dx")
        return dh, dh_b, d_cw, d_cb, d_norm

    hn0 = _rms_fwd(x, attn_norm[0:1], "attn0_norm")
    z = _nn_parts("sc_in", hn0, w_in, 3, BF16)
    mix = _scmix_fwd(z, small["sc_conv_w"])
    h1, hf0 = _nn_add_norm("sc_out", mix, w_out, x, ffn_norm[0:1])
    h2, ffn0_saved = ffn_fwd(h1, hf0, 0, lambda a, w, h: _nn("ffn0_down", a, w, F32, add=h))

    hn1, hk, cq_pre, cq, q, kvpre, ckv, kr, knv = _attn_prep(
        h2, attn_norm[1:2], small["kv_in_norm"], w_dq, small["q_latent_norm"], w_uq, w_kv, small["kv_latent_norm"],
        w_ukv, cos, sin)
    o = _attn_fwd(q, knv, kr)
    h3, hf1 = _nn_add_norm("attn_out", o, w_o, h2, ffn_norm[1:2])
    (loss, dh4, dh4_b, d_final), ffn1_saved = ffn_fwd(
        h3, hf1, 1, lambda a, w, h: _nn_add_loss("ffn1_down_loss", a, w, h, small["final_norm"], tgt))

    rows = D // N_CHIPS
    dh3, dh3_b, d_cw1, d_cb1, d_fn1 = ffn_bwd(h3, dh4, dh4_b, 1, ffn1_saved, 0, {})

    do = _nt("attn_out_dx", dh3_b, w_o, BF16)
    d_wo = _tn("attn_out_dw", o, dh3_b, BF16)
    rs.pair_sums(0)
    dq, dknv, dkr = _attn_bwd(q, knv, kr, do, cos, sin)
    rs.chip_sums(0)
    dh2, dh2_b, d_wuq, d_wdq, d_wukv, d_wkv, d_an1, d_kvin, d_qln, d_kvln = _attn_prep_bwd(
        dq, dknv, dkr, dh3, h2, hn1, hk, cq_pre, cq, kvpre, ckv, attn_norm[1:2], small["kv_in_norm"], w_dq,
        small["q_latent_norm"], w_uq, w_kv, small["kv_latent_norm"], w_ukv, cos, sin)
    rs.finish(0)
    by_owner = lambda dw: dw.reshape(dw.shape[0], N_CHIPS, -1).transpose(1, 0, 2)
    rs.start(1, {
        "w_o": d_wo.reshape(N_CHIPS, rows, D), "w_uq": by_owner(d_wuq), "w_dq": d_wdq.reshape(N_CHIPS, rows, Q_LORA),
        "w_ukv": by_owner(d_wukv.reshape(2 * KV_LORA, -1)).reshape(N_CHIPS, 2 * KV_LORA, -1),
        "w_kv": d_wkv.reshape(N_CHIPS, rows, KVP),
    })

    dh1, dh1_b, d_cw0, d_cb0, d_fn0 = ffn_bwd(h1, dh2, dh2_b, 0, ffn0_saved, 2, {
        "down_dw": lambda: rs.pair_sums(1), "gate_bwd": lambda: rs.chip_sums(1),
        "up_dw": lambda: (rs.finish(1), rs.pair_sums(2))})

    d_wout = _tn("sc_out_dw", mix, dh1_b, BF16)
    dmix = _nt("sc_out_dx", dh1_b, w_out, BF16)
    dz, d_scw = _scmix_bwd(z, small["sc_conv_w"], dmix)
    d_win = _dw_sc_in(hn0, dz)
    rs.start(3, {"sc_w_out": d_wout.reshape(N_CHIPS, rows, D), "sc_w_in": d_win})
    dx, _, d_an0 = _dx_norm_bwd("sc_in_dx", dz, w_in, x, attn_norm[0:1], dh1)

    taps_by_owner = lambda per_layer: jnp.stack(per_layer, axis=1).reshape(3, len(per_layer), N_CHIPS, -1).transpose(2, 0, 1, 3)
    small_g = {
        "attn_norm": jnp.concatenate([d_an0, d_an1]), "ffn_norm": jnp.concatenate([d_fn0, d_fn1]),
        "final_norm": d_final, "kv_in_norm": d_kvin, "kv_latent_norm": d_kvln, "q_latent_norm": d_qln,
        "ffn_conv_b": jnp.concatenate([d_cb0, d_cb1]),
        "sc_conv_w": taps_by_owner([d_scw]), "ffn_conv_w": taps_by_owner([d_cw0, d_cw1]),
    }
    return loss, dx, small_g


RS_GROUPS = (("ffn_w_down1", "ffn_w_up1"), ("w_o", "w_uq", "w_dq", "w_ukv", "w_kv"),
             ("ffn_w_down0", "ffn_w_up0"), ("sc_w_out", "sc_w_in"))


class _ReduceScatter:
    def __init__(self, ids, finish):
        self.ids, self.grads, self.step, self.mine, self.sib, self.finish = ids, {}, {}, {}, {}, finish

    def _cid(self, gi):
        return len(AG_GROUPS) + 3 * gi

    def start(self, gi, grads):
        self.grads.update(grads)
        own = [grads[n] for n in RS_GROUPS[gi]]
        self.step[gi] = (own, _pair_exchange(own, gi, self._cid(gi)))

    def pair_sums(self, gi):
        own, ra = self.step[gi]
        sums = _pair_sums(self.ids, own, ra, f"rs_pair_sums{gi}")
        self.step[gi] = (own, ra, _chip_exchange(sums, gi, self._cid(gi) + 1))

    def chip_sums(self, gi):
        own, ra, rb = self.step[gi]
        mine = _chip_sums(self.ids, own, ra, rb, f"rs_chip_sums{gi}")
        self.mine.update(zip(RS_GROUPS[gi], mine))
        last = gi == len(RS_GROUPS) - 1
        swapped = _pair_swap_now(mine) if last else _pair_swap(mine, gi, self._cid(gi) + 2)
        self.sib.update(zip(RS_GROUPS[gi], swapped))


SMALL_REPL = ("attn_norm", "ffn_norm", "final_norm", "kv_in_norm", "kv_latent_norm", "q_latent_norm", "ffn_conv_b")


def _pad_heads(w_uq):
    per_head = w_uq.reshape(Q_LORA, -1, QK_NOPE + QK_ROPE)
    return jnp.pad(per_head, ((0, 0), (0, 0), (0, HEAD_PAD - QK_NOPE - QK_ROPE))).reshape(Q_LORA, -1)


def _pack_kv(w_dkv, w_kr):
    return jnp.concatenate([w_dkv, w_kr, jnp.zeros((w_kr.shape[0], LANES - QK_ROPE), w_kr.dtype)], axis=1)


def kernel(x, positions, attn_norm, ffn_norm, final_norm, sc_w_in, sc_conv_w, sc_w_out, kv_in_norm, w_dkv, kv_latent_norm, w_kr, w_uk, w_uv, w_dq, q_latent_norm, w_uq, w_o, ffn_w_up, ffn_conv_w, ffn_conv_b, ffn_w_down, loss_target, m_attn_norm, m_ffn_norm, m_final_norm, m_sc_w_in, m_sc_conv_w, m_sc_w_out, m_kv_in_norm, m_w_dkv, m_kv_latent_norm, m_w_kr, m_w_uk, m_w_uv, m_w_dq, m_q_latent_norm, m_w_uq, m_w_o, m_ffn_w_up, m_ffn_conv_w, m_ffn_conv_b, m_ffn_w_down, v_attn_norm, v_ffn_norm, v_final_norm, v_sc_w_in, v_sc_conv_w, v_sc_w_out, v_kv_in_norm, v_w_dkv, v_kv_latent_norm, v_w_kr, v_w_uk, v_w_uv, v_w_dq, v_q_latent_norm, v_w_uq, v_w_o, v_ffn_w_up, v_ffn_conv_w, v_ffn_conv_b, v_ffn_w_down):
    names = ("attn_norm", "ffn_norm", "final_norm", "sc_w_in", "sc_conv_w", "sc_w_out", "kv_in_norm", "w_dkv",
             "kv_latent_norm", "w_kr", "w_uk", "w_uv", "w_dq", "q_latent_norm", "w_uq", "w_o", "ffn_w_up",
             "ffn_conv_w", "ffn_conv_b", "ffn_w_down")
    w = dict(zip(names, (attn_norm, ffn_norm, final_norm, sc_w_in, sc_conv_w, sc_w_out, kv_in_norm, w_dkv,
                         kv_latent_norm, w_kr, w_uk, w_uv, w_dq, q_latent_norm, w_uq, w_o, ffn_w_up,
                         ffn_conv_w, ffn_conv_b, ffn_w_down)))
    m = dict(zip(names, (m_attn_norm, m_ffn_norm, m_final_norm, m_sc_w_in, m_sc_conv_w, m_sc_w_out, m_kv_in_norm,
                         m_w_dkv, m_kv_latent_norm, m_w_kr, m_w_uk, m_w_uv, m_w_dq, m_q_latent_norm, m_w_uq, m_w_o,
                         m_ffn_w_up, m_ffn_conv_w, m_ffn_conv_b, m_ffn_w_down)))
    v = dict(zip(names, (v_attn_norm, v_ffn_norm, v_final_norm, v_sc_w_in, v_sc_conv_w, v_sc_w_out, v_kv_in_norm,
                         v_w_dkv, v_kv_latent_norm, v_w_kr, v_w_uk, v_w_uv, v_w_dq, v_q_latent_norm, v_w_uq, v_w_o,
                         v_ffn_w_up, v_ffn_conv_w, v_ffn_conv_b, v_ffn_w_down)))

    _ORDER[0] = None
    ix, iy, ic = lax.axis_index("x"), lax.axis_index("y"), lax.axis_index("c")
    chip = 2 * ix + iy
    ids = jnp.stack([ic, chip]).astype(jnp.int32)

    ws = {
        "sc_w_in": sc_w_in[0], "sc_w_out": sc_w_out[0], "ffn_w_up": ffn_w_up, "ffn_w_down": ffn_w_down,
        "w_kv": _pack_kv(w_dkv, w_kr), "w_ukv": jnp.stack([w_uk, w_uv]), "w_dq": w_dq[0],
        "w_uq": _pad_heads(w_uq[0]), "w_o": w_o[0],
    }

    def ag_shard(name):
        if name == "sc_conv_w":
            return sc_conv_w[0]
        if name == "ffn_conv_w":
            return ffn_conv_w.reshape(6, -1)
        if name[:-1] in ("ffn_w_up", "ffn_w_down"):
            return ws[name[:-1]][int(name[-1])].astype(BF16)
        return ws[name].astype(BF16)

    wf = {}
    for gi, wms in enumerate(AG_GROUPS):
        fulls = _all_gather_group(gi, [ag_shard(wm.name) for wm in wms])
        wf.update({wm.name: f for wm, f in zip(wms, fulls)})
    small = {
        "attn_norm": attn_norm, "ffn_norm": ffn_norm, "final_norm": final_norm[None], "kv_in_norm": kv_in_norm[None],
        "kv_latent_norm": kv_latent_norm[None], "q_latent_norm": q_latent_norm, "ffn_conv_b": ffn_conv_b,
        "sc_conv_w": wf["sc_conv_w"].transpose(1, 0, 2).reshape(3, D),
        "ffn_conv_w": wf["ffn_conv_w"].reshape(N_CHIPS, 2, 3, -1).transpose(1, 2, 0, 3).reshape(2, 3, F_FF),
    }

    res = {}

    held = {
        "ffn_w_up0": [("ffn_w_up", dict(layer=0))], "ffn_w_up1": [("ffn_w_up", dict(layer=1))],
        "ffn_w_down0": [("ffn_w_down", dict(layer=0))], "ffn_w_down1": [("ffn_w_down", dict(layer=1))],
        "sc_w_in": [("sc_w_in", dict(layer=0))], "sc_w_out": [("sc_w_out", dict(layer=0))],
        "w_dq": [("w_dq", dict(layer=0))], "w_o": [("w_o", dict(layer=0))], "w_uq": [("w_uq", dict(layer=0, head_padded=True))],
        "w_kv": [("w_dkv", dict(gcols=(0, KV_LORA))), ("w_kr", dict(gcols=(KV_LORA, KV_LORA + QK_ROPE)))],
        "w_ukv": [("w_uk", dict(owner=0)), ("w_uv", dict(owner=1))],
    }

    def adamw_group(gi):
        items = []
        for key in RS_GROUPS[gi]:
            for n, opts in held[key]:
                items.append(dict(name=n, w=w[n], m=m[n], v=v[n], g_mine=rs.mine[key], g_sib=rs.sib[key],
                                  prev=res.get(n) if "layer" in opts and w[n].shape[0] > 1 else None, **opts))
        for it, out in zip(items, _adamw_shards(ids, items, f"adamw_group{gi}")):
            res[it["name"]] = out

    rs = _ReduceScatter(ids, adamw_group)
    loss, dx, small_g = _local_step(x[0], positions[0], loss_target[0], wf, small, rs)

    rs.chip_sums(2)
    rs.pair_sums(3)

    s_names = list(small_g)
    reduced = _all_reduce_small([small_g[n] for n in s_names] + [loss], [small_g[n].ndim == 4 for n in s_names] + [False],
                                "ar_small")
    sg, loss_out = dict(zip(s_names, reduced[:-1])), reduced[-1][0, 0]

    row = lambda n: (lambda t: t[n][None])
    taps = lambda n: (lambda t: t[n].transpose(1, 0, 2))
    small_2d = {
        "attn_norm": (sg["attn_norm"], lambda t: t["attn_norm"]), "ffn_norm": (sg["ffn_norm"], lambda t: t["ffn_norm"]),
        "final_norm": (sg["final_norm"], row("final_norm")), "kv_in_norm": (sg["kv_in_norm"], row("kv_in_norm")),
        "kv_latent_norm": (sg["kv_latent_norm"], row("kv_latent_norm")),
        "q_latent_norm": (sg["q_latent_norm"], lambda t: t["q_latent_norm"]),
        "ffn_conv_b": (sg["ffn_conv_b"], lambda t: t["ffn_conv_b"]),
        "sc_conv_w": (sg["sc_conv_w"], taps("sc_conv_w")), "ffn_conv_w": (sg["ffn_conv_w"], taps("ffn_conv_w")),
    }
    s_keys = list(small_2d)
    small_grads = [small_2d[k][0] for k in s_keys]
    views = lambda tree: [small_2d[k][1](tree) for k in s_keys]
    small_res = _adamw_small(views(w), small_grads, views(m), views(v))

    def restore(vals):
        by = dict(zip(s_keys, vals))
        out = {n: by[n].reshape(w[n].shape) for n in SMALL_REPL}
        out.update({n: by[n].transpose(1, 0, 2) for n in ("sc_conv_w", "ffn_conv_w")})
        return out

    rs.finish(2)
    rs.chip_sums(3)
    rs.finish(3)
    outs = [restore(vals) for vals in small_res]
    for k, dst in enumerate(outs):
        for n in res:
            dst[n] = res[n][k]
    grads, delta, new_m, new_v = outs

    _ORDER[0] = None
    return (loss_out, dx[None], *[grads[n] for n in names], *[delta[n] for n in names],
            *[new_m[n] for n in names], *[new_v[n] for n in names])
```

```python
from typing import NamedTuple

import jax
import jax.numpy as jnp
from jax import lax
from jax.experimental import pallas as pl
from jax.experimental.pallas import tpu as pltpu
from jax.experimental.pallas import tpu_sc as plsc

F32 = jnp.float32
BF16 = jnp.bfloat16

T = 2048
D = 1024
F_FF = 2816
N_HEADS = 8
QK_NOPE = 128
QK_ROPE = 64
V_HEAD = 128
Q_LORA = 384
KV_LORA = 256
CHUNK_SHIFT = 6
ROPE_THETA = 10000.0
EPS = 1e-6
NEG_INF = -1e30
HEAD_PAD = 256
KVP = KV_LORA + 128

ADAM_LR = 0.001
ADAM_B1 = 0.9
ADAM_B2 = 0.999
ADAM_EPS = 1e-08
ADAM_WD = 0.01
ADAM_STEP = 10

N_CHIPS = 4
N_DEV = 8
LANES = 128
TC = 256
V7X_VMEM_LIMIT = 56 * 1024 * 1024

MESH = pl.DeviceIdType.MESH
ANY = pl.BlockSpec(memory_space=pl.ANY)


class _W(NamedTuple):
    name: str
    kind: str
    nl: int
    k: int
    n: int


AG_GROUPS = (
    (_W("sc_w_in", "col", 1, D, 3 * D // N_CHIPS), _W("sc_conv_w", "tiny", 1, 3, D // N_CHIPS),
     _W("ffn_conv_w", "tiny", 1, 6, F_FF // N_CHIPS), _W("sc_w_out", "row", 1, D // N_CHIPS, D)),
    (_W("ffn_w_up0", "col", 1, D, 2 * F_FF // N_CHIPS),),
    (_W("ffn_w_down0", "row", 1, F_FF // N_CHIPS, D),),
    (_W("w_kv", "row", 1, D // N_CHIPS, KVP), _W("w_ukv", "col", 2, KV_LORA, N_HEADS * QK_NOPE // N_CHIPS),
     _W("w_dq", "row", 1, D // N_CHIPS, Q_LORA),
     _W("w_uq", "col", 1, Q_LORA, N_HEADS * HEAD_PAD // N_CHIPS),
     _W("w_o", "row", 1, N_HEADS * V_HEAD // N_CHIPS, D)),
    (_W("ffn_w_up1", "col", 1, D, 2 * F_FF // N_CHIPS), _W("ffn_w_down1", "row", 1, F_FF // N_CHIPS, D)),
)


def _cp(*sem):
    return pltpu.CompilerParams(dimension_semantics=sem, vmem_limit_bytes=V7X_VMEM_LIMIT)


_ORDER = [None]


def _tc_call(body, *, name, out_shape, in_specs=None, out_specs=None, grid=(), scratch_shapes=(), prefetch=0,
             input_output_aliases=None, compiler_params=None):
    def run(*args):
        specs = [pl.BlockSpec(memory_space=pltpu.VMEM)] * (len(args) - prefetch) if in_specs is None else list(in_specs)
        inner, dep = body, _ORDER[0]
        if dep is not None:
            unread = prefetch + len(specs)
            specs, args = specs + [ANY], (*args, dep)

            def inner(*refs):
                return body(*refs[:unread], *refs[unread + 1:])

        kwargs = dict(name=name, out_shape=out_shape, input_output_aliases=input_output_aliases or {},
                      compiler_params=compiler_params)
        if prefetch:
            kwargs["grid_spec"] = pltpu.PrefetchScalarGridSpec(
                num_scalar_prefetch=prefetch, grid=grid, in_specs=specs, out_specs=out_specs,
                scratch_shapes=scratch_shapes)
        else:
            kwargs.update(grid=grid, in_specs=specs, scratch_shapes=scratch_shapes)
            if out_specs is not None:
                kwargs["out_specs"] = out_specs
        out = pl.pallas_call(inner, **kwargs)(*args)
        _ORDER[0] = out[0] if isinstance(out, (list, tuple)) else out
        return out

    return run


def _tile(n, cands):
    for c in cands:
        if n % c == 0:
            return c
    raise ValueError(f"no tile for {n}")


NN_DIMS = (((1,), (0,)), ((), ()))
NT_DIMS = (((1,), (1,)), ((), ()))
TN_DIMS = (((0,), (0,)), ((), ()))
M_TILES = (1024, 512, 384, 256, 128)
N_TILES = (1408, 1024, 768, 512, 384, 256, 128)
MM_BLOCK_BYTES = 36 * 1024 * 1024


def _fit(m, n, block_bytes, m_tiles=M_TILES, n_tiles=N_TILES, n_first=False):
    tms, tns = [c for c in m_tiles if m % c == 0], [c for c in n_tiles if n % c == 0]
    pairs = [(tm, tn) for tn in tns for tm in tms] if n_first else [(tm, tn) for tm in tms for tn in tns]
    for tm, tn in pairs:
        if 2 * block_bytes(tm, tn) + 4 * tm * tn <= MM_BLOCK_BYTES:
            return tm, tn
    raise ValueError(f"no tiles for {m} x {n}")


def _size(x):
    return x.dtype.itemsize


def _mm(name, a, b, dims, grid, a_spec, b_spec, o_spec, o_sds, add=None, red=None, acc_shape=None):
    n_red = None if red is None else grid[red]

    def body(*refs):
        a_ref, b_ref = refs[0], refs[1]
        add_ref = refs[2] if add is not None else None
        o_ref = refs[3] if add is not None else refs[2]
        part = lax.dot_general(a_ref[...].astype(BF16), b_ref[...].astype(BF16), dims, preferred_element_type=F32)
        if red is None:
            if add is not None:
                part = part + add_ref[...]
            o_ref[...] = part.astype(o_ref.dtype)
            return
        acc_ref = refs[-1]
        r = pl.program_id(red)

        @pl.when(r == 0)
        def _():
            acc_ref[...] = part

        @pl.when(r > 0)
        def _():
            acc_ref[...] += part

        @pl.when(r == n_red - 1)
        def _():
            o_ref[...] = acc_ref[...].astype(o_ref.dtype)

    sem = tuple("arbitrary" if ax == red else "parallel" for ax in range(len(grid)))
    in_specs = [a_spec, b_spec] + ([o_spec] if add is not None else [])
    args = (a, b) + ((add,) if add is not None else ())
    return _tc_call(
        body, name=name, grid=grid, in_specs=in_specs, out_specs=o_spec, out_shape=o_sds,
        scratch_shapes=[] if red is None else [pltpu.VMEM(acc_shape, F32)], compiler_params=_cp(*sem),
    )(*args)


def _nn(name, a, b, out_dtype, add=None, lead=None):
    (m, k), n = a.shape, b.shape[-1]
    osz = jnp.dtype(out_dtype).itemsize + (4 if add is not None else 0)
    tm, tn = _fit(m, n, lambda tm, tn: tm * k * _size(a) + k * tn * _size(b) + tm * tn * osz, n_first=True)
    if lead is None:
        b_spec = pl.BlockSpec((k, tn), lambda i, j: (0, j))
    else:
        b_spec = pl.BlockSpec((None, k, tn), lambda i, j: (lead, 0, j))
    return _mm(name, a, b, NN_DIMS, (m // tm, n // tn), pl.BlockSpec((tm, k), lambda i, j: (i, 0)), b_spec,
               pl.BlockSpec((tm, tn), lambda i, j: (i, j)), jax.ShapeDtypeStruct((m, n), out_dtype), add=add)


def _nn_parts(name, a, b, parts, out_dtype, lead=None, stacked=False):
    m, k = a.shape
    c = b.shape[-1] if stacked else b.shape[-1] // parts
    osz = jnp.dtype(out_dtype).itemsize
    tm, tn = _fit(m, c, lambda tm, tn: tm * k * _size(a) + k * tn * _size(b) + tm * tn * osz)
    nb = c // tn
    if stacked:
        b_spec = pl.BlockSpec((None, k, tn), lambda i, p, j: (p, 0, j))
    elif lead is None:
        b_spec = pl.BlockSpec((k, tn), lambda i, p, j: (0, p * nb + j))
    else:
        b_spec = pl.BlockSpec((None, k, tn), lambda i, p, j: (lead, 0, p * nb + j))
    return _mm(name, a, b, NN_DIMS, (m // tm, parts, nb), pl.BlockSpec((tm, k), lambda i, p, j: (i, 0)), b_spec,
               pl.BlockSpec((None, tm, tn), lambda i, p, j: (p, i, j)), jax.ShapeDtypeStruct((parts, m, c), out_dtype))


def _nt(name, a, b, out_dtype, lead=None):
    (m, k), n = a.shape, b.shape[-2]
    osz = jnp.dtype(out_dtype).itemsize
    tm, tn = _fit(m, n, lambda tm, tn: tm * k * _size(a) + tn * k * _size(b) + tm * tn * osz)
    if lead is None:
        b_spec = pl.BlockSpec((tn, k), lambda i, j: (j, 0))
    else:
        b_spec = pl.BlockSpec((None, tn, k), lambda i, j: (lead, j, 0))
    return _mm(name, a, b, NT_DIMS, (m // tm, n // tn), pl.BlockSpec((tm, k), lambda i, j: (i, 0)), b_spec,
               pl.BlockSpec((tm, tn), lambda i, j: (i, j)), jax.ShapeDtypeStruct((m, n), out_dtype))


def _tn(name, a, b, out_dtype):
    (k, m), n = a.shape, b.shape[1]
    osz = jnp.dtype(out_dtype).itemsize
    tm, tn = _fit(m, n, lambda tm, tn: k * tm * _size(a) + k * tn * _size(b) + tm * tn * osz,
                  m_tiles=(1408, 512, 384, 256, 128), n_tiles=(n,) + N_TILES)
    return _mm(name, a, b, TN_DIMS, (m // tm, n // tn), pl.BlockSpec((k, tm), lambda i, j: (0, i)),
               pl.BlockSpec((k, tn), lambda i, j: (0, j)), pl.BlockSpec((tm, tn), lambda i, j: (i, j)),
               jax.ShapeDtypeStruct((m, n), out_dtype))


def _nn_add_norm(name, a, b, add, g):
    (m, k), n = a.shape, b.shape[1]
    tm = 512

    def body(a_ref, b_ref, add_ref, g_ref, h_ref, hn_ref):
        h = jnp.dot(a_ref[...], b_ref[...], preferred_element_type=F32) + add_ref[...]
        h_ref[...] = h
        hn_ref[...] = _rms_rows(h, g_ref[...]).astype(BF16)

    rows = lambda w: pl.BlockSpec((tm, w), lambda i: (i, 0))
    return _tc_call(
        body, name=name, grid=(m // tm,),
        in_specs=[rows(k), pl.BlockSpec((k, n), lambda i: (0, 0)), rows(n), pl.BlockSpec((1, n), lambda i: (0, 0))],
        out_specs=[rows(n), rows(n)],
        out_shape=[jax.ShapeDtypeStruct((m, n), F32), jax.ShapeDtypeStruct((m, n), BF16)], compiler_params=_cp("parallel"),
    )(a, b, add, g)


def _nn_add_loss(name, a, b, add, g, tgt):
    (m, k), n = a.shape, b.shape[1]
    tm = 512

    def body(a_ref, b_ref, add_ref, g_ref, t_ref, loss_ref, dh_ref, dhb_ref, dg_ref):
        xv = jnp.dot(a_ref[...], b_ref[...], preferred_element_type=F32) + add_ref[...]
        gv = g_ref[...]
        r = lax.rsqrt(jnp.mean(xv * xv, axis=1, keepdims=True) + EPS)
        err = xv * r * gv - t_ref[...]
        part = 0.5 * jnp.sum(jnp.mean(err * err, axis=1, keepdims=True), axis=0, keepdims=True)
        dx, dg = _rms_bwd_math(xv, gv, err * (1.0 / n))
        dh_ref[...] = dx
        dhb_ref[...] = dx.astype(BF16)

        @pl.when(pl.program_id(0) == 0)
        def _():
            dg_ref[...] = jnp.zeros_like(dg_ref)
            loss_ref[...] = jnp.zeros_like(loss_ref)

        dg_ref[...] += dg
        loss_ref[...] += jnp.broadcast_to(part, loss_ref.shape)

    rows = lambda w: pl.BlockSpec((tm, w), lambda i: (i, 0))
    vec = pl.BlockSpec((1, n), lambda i: (0, 0))
    return _tc_call(
        body, name=name, grid=(m // tm,),
        in_specs=[rows(k), pl.BlockSpec((k, n), lambda i: (0, 0)), rows(n), vec, rows(n)],
        out_specs=[pl.BlockSpec((1, LANES), lambda i: (0, 0)), rows(n), rows(n), vec],
        out_shape=[jax.ShapeDtypeStruct((1, LANES), F32), jax.ShapeDtypeStruct((m, n), F32),
                   jax.ShapeDtypeStruct((m, n), BF16), jax.ShapeDtypeStruct((1, n), F32)],
        compiler_params=_cp("arbitrary"),
    )(a, b, add, g, tgt)


def _dx_norm_bwd(name, a, b, x, g, add):
    parts, t, c = a.shape
    d = b.shape[0]
    tm = 256
    n_steps, ring = t // tm, 3
    assert n_steps >= ring, a.shape

    def body(a_hbm, b_ref, x_ref, g_ref, add_ref, dx_ref, dxb_ref, dg_ref, a_buf, sems):
        s = pl.program_id(0)
        fetch = lambda step: pltpu.make_async_copy(a_hbm.at[:, pl.ds(step * tm, tm), :], a_buf.at[step % ring],
                                                   sems.at[step % ring])

        @pl.when(s == 0)
        def _():
            for first in range(ring - 1):
                fetch(first).start()

        @pl.when(s + ring - 1 < n_steps)
        def _():
            fetch(s + ring - 1).start()

        fetch(s).wait()
        a_ref = a_buf.at[s % ring]
        dy = None
        for p in range(parts):
            part = lax.dot_general(a_ref[p], b_ref[:, p * c:(p + 1) * c], NT_DIMS, preferred_element_type=F32)
            dy = part if dy is None else dy + part
        dx, dg = _rms_bwd_math(x_ref[...], g_ref[...], dy)
        dx = dx + add_ref[...]
        dx_ref[...] = dx
        dxb_ref[...] = dx.astype(BF16)

        @pl.when(pl.program_id(0) == 0)
        def _():
            dg_ref[...] = jnp.zeros_like(dg_ref)

        dg_ref[...] += dg

    rows = pl.BlockSpec((tm, d), lambda i: (i, 0))
    vec = pl.BlockSpec((1, d), lambda i: (0, 0))
    return _tc_call(
        body, name=name, grid=(t // tm,),
        in_specs=[ANY, pl.BlockSpec(b.shape, lambda i: (0, 0)), rows, vec, rows],
        out_specs=[rows, rows, vec],
        out_shape=[jax.ShapeDtypeStruct((t, d), F32), jax.ShapeDtypeStruct((t, d), BF16),
                   jax.ShapeDtypeStruct((1, d), F32)],
        scratch_shapes=[pltpu.VMEM((ring, parts, tm, c), a.dtype), pltpu.SemaphoreType.DMA((ring,))],
        compiler_params=_cp("arbitrary"),
    )(a, b, x, g, add)


def _dw_sc_in(hn, dz):
    t, tn, tm = hn.shape[0], TC, D
    per_part, per_chip = D // tn, 3 * D // N_CHIPS // tn
    return _mm("sc_in_dw", hn, dz, TN_DIMS, (D // tm, 3 * D // tn), pl.BlockSpec((t, tm), lambda i, j: (0, i)),
               pl.BlockSpec((None, t, tn), lambda i, j: (j // per_part, 0, j % per_part)),
               pl.BlockSpec((None, tm, tn), lambda i, j: (j // per_chip, i, j % per_chip)),
               jax.ShapeDtypeStruct((N_CHIPS, D, 3 * D // N_CHIPS), BF16))


def _dw_ffn_up(name, hf, dup):
    t, tm, ns = hf.shape[0], D, 2 * F_FF // N_CHIPS
    return _mm(name, hf, dup, TN_DIMS, (N_CHIPS, D // tm), pl.BlockSpec((t, tm), lambda s, i: (0, i)),
               pl.BlockSpec((None, t, ns), lambda s, i: (s // 2, 0, s % 2)),
               pl.BlockSpec((None, tm, ns), lambda s, i: (s, i, 0)), jax.ShapeDtypeStruct((N_CHIPS, D, ns), BF16))


def _rms_fwd(x, g, name):
    t, d = x.shape
    tr = 512

    def body(x_ref, g_ref, o_ref):
        xv = x_ref[...]
        r = lax.rsqrt(jnp.mean(xv * xv, axis=1, keepdims=True) + EPS)
        o_ref[...] = (xv * r * g_ref[...]).astype(o_ref.dtype)

    row = pl.BlockSpec((tr, d), lambda i: (i, 0))
    return _tc_call(
        body, name=name, grid=(t // tr,), in_specs=[row, pl.BlockSpec((1, d), lambda i: (0, 0))],
        out_specs=row, out_shape=jax.ShapeDtypeStruct((t, d), BF16), compiler_params=_cp("parallel"),
    )(x, g)


def _rms_bwd_math(xv, g, dy):
    r = lax.rsqrt(jnp.mean(xv * xv, axis=1, keepdims=True) + EPS)
    xh = xv * r
    gy = dy * g
    dx = r * (gy - xh * jnp.mean(gy * xh, axis=1, keepdims=True))
    dg = jnp.sum(dy * xh, axis=0, keepdims=True)
    return dx, dg


def _rot_half(x):
    lane = lax.broadcasted_iota(jnp.int32, x.shape, 1)
    return jnp.where((lane % QK_ROPE) < QK_ROPE // 2, -pltpu.roll(x, LANES - 32, axis=1),
                     pltpu.roll(x, 32, axis=1))


def _rope_fwd_math(x, cos, sin):
    return x * cos + _rot_half(x) * sin


def _rope_bwd_math(dy, cos, sin):
    return dy * cos - _rot_half(dy * sin)


def _rms_rows(x, g):
    return x * lax.rsqrt(jnp.mean(x * x, axis=1, keepdims=True) + EPS) * g


def _attn_prep(h, g_attn, g_kvin, w_dq, g_ql, w_uq, w_kv, g_kvl, w_ukv, cos, sin):
    t, d = h.shape
    tr = 256
    wq = N_HEADS * HEAD_PAD

    def body(h_ref, ga_ref, gk_ref, wdq_ref, gq_ref, wuq_ref, wkv_ref, gl_ref, wukv_ref, c_ref, s_ref,
             hn_ref, hk_ref, cqp_ref, cq_ref, q_ref, kvp_ref, ckv_ref, kr_ref, knv_ref):
        xv, cv, sv = h_ref[...], c_ref[...], s_ref[...]
        xh = xv * lax.rsqrt(jnp.mean(xv * xv, axis=1, keepdims=True) + EPS)
        hn = (xh * ga_ref[...]).astype(BF16)
        hk = (xh * gk_ref[...]).astype(BF16)
        hn_ref[...], hk_ref[...] = hn, hk
        cq_pre = jnp.dot(hn, wdq_ref[...], preferred_element_type=F32)
        cqp_ref[...] = cq_pre
        cq = _rms_rows(cq_pre, gq_ref[...]).astype(BF16)
        cq_ref[...] = cq
        for hd in range(N_HEADS):
            lo = hd * HEAD_PAD
            qh = jnp.dot(cq, wuq_ref[:, lo:lo + HEAD_PAD], preferred_element_type=F32)
            q_ref[:, lo:lo + QK_NOPE] = qh[:, :QK_NOPE].astype(BF16)
            q_ref[:, lo + QK_NOPE:lo + HEAD_PAD] = _rope_fwd_math(qh[:, QK_NOPE:], cv, sv).astype(BF16)
        kvpre = jnp.dot(hk, wkv_ref[...], preferred_element_type=F32)
        kvp_ref[...] = kvpre
        ckv = _rms_rows(kvpre[:, :KV_LORA], gl_ref[...]).astype(BF16)
        ckv_ref[...] = ckv
        kr_ref[...] = _rope_fwd_math(kvpre[:, KV_LORA:], cv, sv).astype(BF16)
        for p in range(2):
            knv_ref[p] = jnp.dot(ckv, wukv_ref[p], preferred_element_type=F32).astype(BF16)

    rows = lambda w: pl.BlockSpec((tr, w), lambda i: (i, 0))
    whole = lambda a: pl.BlockSpec(a.shape, lambda i: (0,) * a.ndim)
    sds = lambda w, dt: jax.ShapeDtypeStruct((t, w), dt)
    args = (h, g_attn, g_kvin, w_dq, g_ql, w_uq, w_kv, g_kvl, w_ukv, cos, sin)
    return _tc_call(
        body, name="attn_prep", grid=(t // tr,),
        in_specs=[rows(d)] + [whole(a) for a in args[1:9]] + [rows(LANES), rows(LANES)],
        out_specs=[rows(d), rows(d), rows(Q_LORA), rows(Q_LORA), rows(wq), rows(KVP), rows(KV_LORA), rows(LANES),
                   pl.BlockSpec((2, tr, N_HEADS * QK_NOPE), lambda i: (0, i, 0))],
        out_shape=[sds(d, BF16), sds(d, BF16), sds(Q_LORA, F32), sds(Q_LORA, BF16), sds(wq, BF16), sds(KVP, F32),
                   sds(KV_LORA, BF16), sds(LANES, BF16), jax.ShapeDtypeStruct((2, t, N_HEADS * QK_NOPE), BF16)],
        compiler_params=_cp("parallel"),
    )(*args)


def _attn_prep_bwd(dq, dknv, dkr, dh, h, hn, hk, cq_pre, cq, kvpre, ckv, g_attn, g_kvin, w_dq, g_ql, w_uq, w_kv, g_kvl,
                   w_ukv, cos, sin):
    t, d = h.shape
    tr = 256
    n_steps = t // tr
    wq = N_HEADS * HEAD_PAD
    wk = N_HEADS * QK_NOPE

    def body(dq_ref, dknv_ref, dkr_ref, dh_ref, h_ref, hn_ref, hk_ref, cqp_ref, cq_ref, kvp_ref, ckv_ref,
             ga_ref, gk_ref, wdq_ref, gq_ref, wuq_ref, wkv_ref, gl_ref, wukv_ref, c_ref, s_ref,
             dho_ref, dhb_ref, dwuq_ref, dwdq_ref, dwukv_ref, dwkv_ref, dga_ref, dgk_ref, dgq_ref, dgl_ref,
             a_uq, a_dq, a_ukv, a_kv):
        i = pl.program_id(0)

        @pl.when(i == 0)
        def _():
            for ref in (a_uq, a_dq, a_ukv, a_kv, dga_ref, dgk_ref, dgq_ref, dgl_ref):
                ref[...] = jnp.zeros_like(ref)

        dqv = dq_ref[...]
        dcq = lax.dot_general(dqv, wuq_ref[...], NT_DIMS, preferred_element_type=F32)
        a_uq[...] += lax.dot_general(cq_ref[...], dqv, TN_DIMS, preferred_element_type=F32)
        dcq_pre, dg = _rms_bwd_math(cqp_ref[...], gq_ref[...], dcq)
        dgq_ref[...] += dg
        dcq_pre = dcq_pre.astype(BF16)
        dhn = lax.dot_general(dcq_pre, wdq_ref[...], NT_DIMS, preferred_element_type=F32)
        a_dq[...] += lax.dot_general(hn_ref[...], dcq_pre, TN_DIMS, preferred_element_type=F32)
        dckv = None
        for p in range(2):
            dk = dknv_ref[p].astype(BF16)
            part = lax.dot_general(dk, wukv_ref[p], NT_DIMS, preferred_element_type=F32)
            dckv = part if dckv is None else dckv + part
            a_ukv[p] += lax.dot_general(ckv_ref[...], dk, TN_DIMS, preferred_element_type=F32)
        dlat, dg = _rms_bwd_math(kvp_ref[:, :KV_LORA], gl_ref[...], dckv)
        dgl_ref[...] += dg
        dkr_pre = _rope_bwd_math(dkr_ref[...], c_ref[...], s_ref[...])
        dkvpre = jnp.concatenate([dlat, dkr_pre], axis=1).astype(BF16)
        dhk = lax.dot_general(dkvpre, wkv_ref[...], NT_DIMS, preferred_element_type=F32)
        a_kv[...] += lax.dot_general(hk_ref[...], dkvpre, TN_DIMS, preferred_element_type=F32)
        xv = h_ref[...]
        dx1, dg = _rms_bwd_math(xv, ga_ref[...], dhn)
        dga_ref[...] += dg
        dx2, dg = _rms_bwd_math(xv, gk_ref[...], dhk)
        dgk_ref[...] += dg
        dh_new = dh_ref[...] + dx1 + dx2
        dho_ref[...] = dh_new
        dhb_ref[...] = dh_new.astype(BF16)

        @pl.when(i == n_steps - 1)
        def _():
            dwuq_ref[...] = a_uq[...].astype(BF16)
            dwdq_ref[...] = a_dq[...].astype(BF16)
            dwukv_ref[...] = a_ukv[...].astype(BF16)
            dwkv_ref[...] = a_kv[...].astype(BF16)

    rows = lambda w: pl.BlockSpec((tr, w), lambda i: (i, 0))
    whole = lambda shape: pl.BlockSpec(shape, lambda i: (0,) * len(shape))
    weights = (g_attn, g_kvin, w_dq, g_ql, w_uq, w_kv, g_kvl, w_ukv)
    dw_shapes = [(Q_LORA, wq), (d, Q_LORA), (2, KV_LORA, wk), (d, KVP)]
    dg_shapes = [(1, d), (1, d), (1, Q_LORA), (1, KV_LORA)]
    return _tc_call(
        body, name="attn_prep_bwd", grid=(n_steps,),
        in_specs=[rows(wq), pl.BlockSpec((2, tr, wk), lambda i: (0, i, 0)), rows(LANES), rows(d), rows(d), rows(d),
                  rows(d), rows(Q_LORA), rows(Q_LORA), rows(KVP), rows(KV_LORA)]
        + [whole(a.shape) for a in weights] + [rows(LANES), rows(LANES)],
        out_specs=[rows(d), rows(d)] + [whole(s) for s in dw_shapes + dg_shapes],
        out_shape=[jax.ShapeDtypeStruct((t, d), F32), jax.ShapeDtypeStruct((t, d), BF16)]
        + [jax.ShapeDtypeStruct(s, BF16) for s in dw_shapes] + [jax.ShapeDtypeStruct(s, F32) for s in dg_shapes],
        scratch_shapes=[pltpu.VMEM(s, F32) for s in dw_shapes], compiler_params=_cp("arbitrary"),
    )(dq, dknv, dkr, dh, h, hn, hk, cq_pre, cq, kvpre, ckv, *weights, cos, sin)


ROW_CHUNK = 64
HALO = 16
WIN = ROW_CHUNK + 16
LANE_HALVES = (slice(0, LANES), slice(LANES, TC))


def _stage(s_ref, p, src):
    t = src.shape[0]
    s_ref[p, :HALO] = jnp.zeros((HALO, TC), BF16)
    s_ref[p, HALO:HALO + t] = src
    s_ref[p, HALO + t:] = jnp.zeros((HALO, TC), BF16)


def _window(s_ref, p, i, lanes):
    base = pl.multiple_of(i * ROW_CHUNK, ROW_CHUNK)
    return s_ref[p, pl.ds(base, ROW_CHUNK + 2 * HALO), lanes].astype(F32)[8:8 + WIN]


def _valid(x):
    return x[8:8 + ROW_CHUNK]


def _prev(x, k):
    return pltpu.roll(x, k, axis=0)


def _next(x, k):
    return pltpu.roll(x, WIN - k, axis=0)


def _taps(w_ref, lanes):
    return w_ref[0:1, lanes], w_ref[1:2, lanes], w_ref[2:3, lanes]


def _fold8(x):
    return jnp.sum(x.reshape(ROW_CHUNK // 8, 8, x.shape[-1]), axis=0)


def _store_rows(ref, idx, i, lanes, x):
    rows = pl.ds(pl.multiple_of(i * ROW_CHUNK, ROW_CHUNK), ROW_CHUNK)
    ref[(*idx, rows, lanes)] = x.astype(ref.dtype)


def _for_chunks(t, chunk):
    def step(i, carry):
        for lanes in LANE_HALVES:
            chunk(i, lanes)
        return carry

    lax.fori_loop(0, t // ROW_CHUNK, step, 0)


def _write_col_sums(acc_ref, outs):
    for k, (ref, row) in enumerate(outs):
        ref[row:row + 1, :] = jnp.sum(acc_ref[k], axis=0, keepdims=True)


def _shift_down(x, k):
    row = lax.broadcasted_iota(jnp.int32, x.shape, 0)
    return jnp.where(row >= k, pltpu.roll(x, k, axis=0), 0.0)


def _shift_up(x, k):
    n = x.shape[0]
    row = lax.broadcasted_iota(jnp.int32, x.shape, 0)
    return jnp.where(row < n - k, pltpu.roll(x, n - k, axis=0), 0.0)


def _conv3(x, w_ref):
    return _shift_down(x, 2) * w_ref[0:1, :] + _shift_down(x, 1) * w_ref[1:2, :] + x * w_ref[2:3, :]


def _col(parts, t):
    if parts is None:
        return pl.BlockSpec((t, TC), lambda j: (0, j))
    return pl.BlockSpec((parts, t, TC), lambda j: (0, 0, j))


def _staging(parts, t):
    return pltpu.VMEM((parts, t + 2 * HALO, TC), BF16)


def _scmix_fwd(z, w):
    t = z.shape[1]

    def body(z_ref, w_ref, m_ref):
        b, c, u = (z_ref[p].astype(F32) for p in range(3))
        m_ref[...] = (b * _conv3(c * u, w_ref)).astype(BF16)

    return _tc_call(
        body, name="scmix_fwd", grid=(D // TC,), in_specs=[_col(3, t), pl.BlockSpec((3, TC), lambda j: (0, j))],
        out_specs=_col(None, t), out_shape=jax.ShapeDtypeStruct((t, D), BF16), compiler_params=_cp("parallel"),
    )(z, w)


def _scmix_bwd(z, w, dm):
    t = z.shape[1]

    def body(z_ref, w_ref, dm_ref, dz_ref, dw_ref, s_ref, acc_ref):
        for p in range(3):
            _stage(s_ref, p, z_ref[p])
        _stage(s_ref, 3, dm_ref[...])
        acc_ref[...] = jnp.zeros_like(acc_ref)

        def chunk(i, lanes):
            w0, w1, w2 = _taps(w_ref, lanes)
            b, c, u, dm = (_window(s_ref, p, i, lanes) for p in range(4))
            cu = c * u
            cu1, cu2 = _prev(cu, 1), _prev(cu, 2)
            _store_rows(dz_ref, (0,), i, lanes, _valid(dm * (cu2 * w0 + cu1 * w1 + cu * w2)))
            dcv = dm * b
            dcu = dcv * w2 + _next(dcv, 1) * w1 + _next(dcv, 2) * w0
            _store_rows(dz_ref, (1,), i, lanes, _valid(dcu * u))
            _store_rows(dz_ref, (2,), i, lanes, _valid(dcu * c))
            for k, shifted in enumerate((cu2, cu1, cu)):
                acc_ref[k, :, lanes] += _fold8(_valid(dcv * shifted))

        _for_chunks(t, chunk)
        _write_col_sums(acc_ref, [(dw_ref, 0), (dw_ref, 1), (dw_ref, 2)])

    wspec = pl.BlockSpec((3, TC), lambda j: (0, j))
    return _tc_call(
        body, name="scmix_bwd", grid=(D // TC,), in_specs=[_col(3, t), wspec, _col(None, t)],
        out_specs=[_col(3, t), wspec],
        out_shape=[jax.ShapeDtypeStruct((3, t, D), BF16), jax.ShapeDtypeStruct((3, D), F32)],
        scratch_shapes=[_staging(4, t), pltpu.VMEM((3, 8, TC), F32)], compiler_params=_cp("parallel"),
    )(z, w, dm)


def _ffn_up_gate(hf, w_up, w, bias, name):
    t, d = hf.shape
    nb = F_FF // TC

    def body(hf_ref, wg_ref, wv_ref, w_ref, b_ref, up_ref, a_ref, prev_ref):
        @pl.when(pl.program_id(0) == 0)
        def _():
            prev_ref[...] = jnp.zeros_like(prev_ref)

        gc = _conv3(prev_ref[0].astype(F32), w_ref) + b_ref[...]
        a_ref[...] = (gc * jax.nn.sigmoid(gc) * prev_ref[1].astype(F32)).astype(BF16)
        hv = hf_ref[...]
        up_ref[0] = jnp.dot(hv, wg_ref[...], preferred_element_type=F32).astype(BF16)
        up_ref[1] = jnp.dot(hv, wv_ref[...], preferred_element_type=F32).astype(BF16)
        prev_ref[...] = up_ref[...]

    tile = lambda j: jnp.minimum(j, nb - 1)
    gated = lambda j: jnp.maximum(j - 1, 0)
    return _tc_call(
        body, name=name, grid=(nb + 1,),
        in_specs=[pl.BlockSpec((t, d), lambda j: (0, 0)), pl.BlockSpec((d, TC), lambda j: (0, tile(j))),
                  pl.BlockSpec((d, TC), lambda j: (0, nb + tile(j))), pl.BlockSpec((3, TC), lambda j: (0, gated(j))),
                  pl.BlockSpec((1, TC), lambda j: (0, gated(j)))],
        out_specs=[pl.BlockSpec((2, t, TC), lambda j: (0, 0, tile(j))), pl.BlockSpec((t, TC), lambda j: (0, gated(j)))],
        out_shape=[jax.ShapeDtypeStruct((2, t, F_FF), BF16), jax.ShapeDtypeStruct((t, F_FF), BF16)],
        scratch_shapes=[pltpu.VMEM((2, t, TC), BF16)], compiler_params=_cp("arbitrary"),
    )(hf, w_up, w_up, w, bias)


def _gate_bwd(up, w, bias, dh, w_down, name):
    t, d = dh.shape

    def body(u_ref, w_ref, b_ref, dh_ref, wd_ref, du_ref, dw_ref, db_ref, s_ref, acc_ref):
        for p in range(2):
            _stage(s_ref, p, u_ref[p])
        _stage(s_ref, 2, lax.dot_general(dh_ref[...], wd_ref[...], NT_DIMS, preferred_element_type=F32).astype(BF16))
        acc_ref[...] = jnp.zeros_like(acc_ref)

        def chunk(i, lanes):
            w0, w1, w2 = _taps(w_ref, lanes)
            g, v, da = (_window(s_ref, p, i, lanes) for p in range(3))
            g1, g2 = _prev(g, 1), _prev(g, 2)
            gc = g2 * w0 + g1 * w1 + g * w2 + b_ref[:, lanes]
            sg = jax.nn.sigmoid(gc)
            _store_rows(du_ref, (1,), i, lanes, _valid(da * (gc * sg)))
            dgc = da * v * (sg * (1.0 + gc * (1.0 - sg)))
            _store_rows(du_ref, (0,), i, lanes, _valid(dgc * w2 + _next(dgc, 1) * w1 + _next(dgc, 2) * w0))
            for k, shifted in enumerate((g2, g1, g)):
                acc_ref[k, :, lanes] += _fold8(_valid(dgc * shifted))
            acc_ref[3, :, lanes] += _fold8(_valid(dgc))

        _for_chunks(t, chunk)
        _write_col_sums(acc_ref, [(dw_ref, 0), (dw_ref, 1), (dw_ref, 2), (db_ref, 0)])

    wspec = pl.BlockSpec((3, TC), lambda j: (0, j))
    bspec = pl.BlockSpec((1, TC), lambda j: (0, j))
    return _tc_call(
        body, name=name, grid=(F_FF // TC,),
        in_specs=[_col(2, t), wspec, bspec, pl.BlockSpec((t, d), lambda j: (0, 0)), pl.BlockSpec((TC, d), lambda j: (j, 0))],
        out_specs=[_col(2, t), wspec, bspec],
        out_shape=[jax.ShapeDtypeStruct((2, t, F_FF), BF16), jax.ShapeDtypeStruct((3, F_FF), F32),
                   jax.ShapeDtypeStruct((1, F_FF), F32)],
        scratch_shapes=[_staging(3, t), pltpu.VMEM((4, 8, TC), F32)], compiler_params=_cp("parallel"),
    )(up, w, bias, dh, w_down)


ATT_TQ = 256
ATT_SCALE = (QK_NOPE + QK_ROPE) ** -0.5


def _key_ranges(lvl):
    lo = lvl * ATT_TQ
    return ([(0, lo, False)] if lvl else []) + [(lo, lo + ATT_TQ, True)]


FWD_HEADS = 4
BWD_HEADS = 2


def _fill_keys(k_ref, kn_ref, kr_ref):
    @pl.when(pl.program_id(1) == 0)
    def _():
        for hh in range(k_ref.shape[0]):
            k_ref[hh, :, :QK_NOPE] = kn_ref[:, hh * QK_NOPE:(hh + 1) * QK_NOPE]
            k_ref[hh, :, QK_NOPE:] = kr_ref[...]


def _attn_probs(q, k_ref, lvl):
    scores = []
    for lo, hi, diagonal in _key_ranges(lvl):
        s = lax.dot_general(q, k_ref[lo:hi, :], NT_DIMS, preferred_element_type=F32) * ATT_SCALE
        if diagonal:
            row = lax.broadcasted_iota(jnp.int32, s.shape, 0)
            col = lax.broadcasted_iota(jnp.int32, s.shape, 1)
            seen = lax.shift_right_logical(col, CHUNK_SHIFT) <= lax.shift_right_logical(row, CHUNK_SHIFT)
            s = jnp.where(seen, s, NEG_INF)
        scores.append(s)
    m = jnp.max(scores[0], axis=1, keepdims=True)
    for s in scores[1:]:
        m = jnp.maximum(m, jnp.max(s, axis=1, keepdims=True))
    ps = [jnp.exp(s - m) for s in scores]
    total = jnp.sum(ps[0], axis=1, keepdims=True)
    for p in ps[1:]:
        total = total + jnp.sum(p, axis=1, keepdims=True)
    inv = 1.0 / total
    return [p * inv for p in ps]


def _attn_probs_t(q, k_ref, lvl):
    scores = []
    for lo, hi, diagonal in _key_ranges(lvl):
        s = lax.dot_general(k_ref[lo:hi, :], q, NT_DIMS, preferred_element_type=F32) * ATT_SCALE
        if diagonal:
            key = lax.broadcasted_iota(jnp.int32, s.shape, 0)
            qry = lax.broadcasted_iota(jnp.int32, s.shape, 1)
            seen = lax.shift_right_logical(key, CHUNK_SHIFT) <= lax.shift_right_logical(qry, CHUNK_SHIFT)
            s = jnp.where(seen, s, NEG_INF)
        scores.append(s)
    m = jnp.max(scores[0], axis=0, keepdims=True)
    for s in scores[1:]:
        m = jnp.maximum(m, jnp.max(s, axis=0, keepdims=True))
    ps = [jnp.exp(s - m) for s in scores]
    total = jnp.sum(ps[0], axis=0, keepdims=True)
    for p in ps[1:]:
        total = total + jnp.sum(p, axis=0, keepdims=True)
    inv = 1.0 / total
    return [p * inv for p in ps]


def _per_query_block(qi, n_blocks, branch):
    for lvl in range(n_blocks):
        pl.when(qi == lvl)(lambda lvl=lvl: branch(lvl))


def _attn_specs(t, g):
    q = pl.BlockSpec((ATT_TQ, g * HEAD_PAD), lambda h, i: (i, h))
    kn = pl.BlockSpec((None, t, g * QK_NOPE), lambda h, i: (0, 0, h))
    kr = pl.BlockSpec((t, LANES), lambda h, i: (0, 0))
    v = pl.BlockSpec((None, t, g * V_HEAD), lambda h, i: (1, 0, h))
    o = pl.BlockSpec((ATT_TQ, g * V_HEAD), lambda h, i: (i, h))
    return q, kn, kr, v, o


def _attn_fwd(q, knv, kr):
    t = q.shape[0]

    def body(q_ref, kn_ref, kr_ref, v_ref, o_ref, k_ref):
        _fill_keys(k_ref, kn_ref, kr_ref)

        def branch(lvl):
            for hh in range(FWD_HEADS):
                vcols = slice(hh * V_HEAD, (hh + 1) * V_HEAD)
                ps = _attn_probs(q_ref[:, hh * HEAD_PAD:(hh + 1) * HEAD_PAD], k_ref.at[hh], lvl)
                o = None
                for p, (lo, hi, _) in zip(ps, _key_ranges(lvl)):
                    part = jnp.dot(p.astype(BF16), v_ref[lo:hi, vcols], preferred_element_type=F32)
                    o = part if o is None else o + part
                o_ref[:, vcols] = o.astype(BF16)

        _per_query_block(pl.program_id(1), t // ATT_TQ, branch)

    qs, kns, krs, vs, os_ = _attn_specs(t, FWD_HEADS)
    return _tc_call(
        body, name="attn_fwd", grid=(N_HEADS // FWD_HEADS, t // ATT_TQ), in_specs=[qs, kns, krs, vs],
        out_specs=os_, out_shape=jax.ShapeDtypeStruct((t, N_HEADS * V_HEAD), BF16),
        scratch_shapes=[pltpu.VMEM((FWD_HEADS, t, HEAD_PAD), BF16)], compiler_params=_cp("parallel", "arbitrary"),
    )(q, knv, kr, knv)


def _attn_bwd(q, knv, kr, do, cos, sin):
    t = q.shape[0]

    def body(q_ref, kn_ref, kr_ref, v_ref, do_ref, c_ref, s_ref, dq_ref, dknv_ref, dkr_ref, k_ref, dk_ref):
        h, qi = pl.program_id(0), pl.program_id(1)
        _fill_keys(k_ref, kn_ref, kr_ref)

        @pl.when(qi == 0)
        def _():
            dknv_ref[1] = jnp.zeros(dknv_ref.shape[1:], F32)
            dk_ref[...] = jnp.zeros_like(dk_ref)

        @pl.when((qi == 0) & (h == 0))
        def _():
            dkr_ref[...] = jnp.zeros_like(dkr_ref)

        def branch(lvl):
            ranges = _key_ranges(lvl)
            for hh in range(BWD_HEADS):
                qcols = slice(hh * HEAD_PAD, (hh + 1) * HEAD_PAD)
                vcols = slice(hh * V_HEAD, (hh + 1) * V_HEAD)
                qv, dov = q_ref[:, qcols], do_ref[:, vcols]
                ps = _attn_probs_t(qv, k_ref.at[hh], lvl)
                dps = [lax.dot_general(v_ref[lo:hi, vcols], dov, NT_DIMS, preferred_element_type=F32)
                       for lo, hi, _ in ranges]
                di = None
                for p, dp in zip(ps, dps):
                    part = jnp.sum(p * dp, axis=0, keepdims=True)
                    di = part if di is None else di + part
                dq = None
                for p, dp, (lo, hi, _) in zip(ps, dps, ranges):
                    ds = (p * (dp - di) * ATT_SCALE).astype(BF16)
                    part = lax.dot_general(ds, k_ref[hh, lo:hi, :], TN_DIMS, preferred_element_type=F32)
                    dq = part if dq is None else dq + part
                    dk_ref[hh, lo:hi, :] += jnp.dot(ds, qv, preferred_element_type=F32)
                    dknv_ref[1, lo:hi, vcols] += jnp.dot(p.astype(BF16), dov, preferred_element_type=F32)
                dq_ref[:, hh * HEAD_PAD:hh * HEAD_PAD + QK_NOPE] = dq[:, :QK_NOPE].astype(BF16)
                dq_ref[:, hh * HEAD_PAD + QK_NOPE:(hh + 1) * HEAD_PAD] = _rope_bwd_math(
                    dq[:, QK_NOPE:], c_ref[...], s_ref[...]).astype(BF16)

        _per_query_block(qi, t // ATT_TQ, branch)

        @pl.when(qi == t // ATT_TQ - 1)
        def _():
            for hh in range(BWD_HEADS):
                dknv_ref[0, :, hh * QK_NOPE:(hh + 1) * QK_NOPE] = dk_ref[hh, :, :QK_NOPE]
                dkr_ref[...] += dk_ref[hh, :, QK_NOPE:]

    qs, kns, krs, vs, os_ = _attn_specs(t, BWD_HEADS)
    tab = pl.BlockSpec((ATT_TQ, LANES), lambda h, i: (i, 0))
    return _tc_call(
        body, name="attn_bwd", grid=(N_HEADS // BWD_HEADS, t // ATT_TQ), in_specs=[qs, kns, krs, vs, os_, tab, tab],
        out_specs=[qs, pl.BlockSpec((2, t, BWD_HEADS * QK_NOPE), lambda h, i: (0, 0, h)), krs],
        out_shape=[jax.ShapeDtypeStruct((t, N_HEADS * HEAD_PAD), BF16),
                   jax.ShapeDtypeStruct((2, t, N_HEADS * QK_NOPE), F32), jax.ShapeDtypeStruct((t, LANES), F32)],
        scratch_shapes=[pltpu.VMEM((BWD_HEADS, t, HEAD_PAD), BF16), pltpu.VMEM((BWD_HEADS, t, HEAD_PAD), F32)],
        compiler_params=_cp("arbitrary", "arbitrary"),
    )(q, knv, kr, knv, do, cos, sin)


def _adam_math(w, g, m, v):
    nm = ADAM_B1 * m + (1.0 - ADAM_B1) * g
    nv = ADAM_B2 * v + (1.0 - ADAM_B2) * (g * g)
    m_hat = nm / (1.0 - ADAM_B1 ** ADAM_STEP)
    v_hat = nv / (1.0 - ADAM_B2 ** ADAM_STEP)
    return -ADAM_LR * (m_hat / (jnp.sqrt(v_hat) + ADAM_EPS) + ADAM_WD * w), nm, nv


def _adamw_small(ws, gs, ms, vs):
    n = len(ws)

    def body(*refs):
        for i in range(n):
            w_ref, g_ref, m_ref, v_ref = (refs[k * n + i] for k in range(4))
            go_ref, d_ref, nm_ref, nv_ref = (refs[(4 + k) * n + i] for k in range(4))
            go_ref[...] = g_ref[...]
            d_ref[...], nm_ref[...], nv_ref[...] = _adam_math(w_ref[...], g_ref[...], m_ref[...], v_ref[...])

    shapes = [jax.ShapeDtypeStruct(a.shape, F32) for a in ws]
    res = _tc_call(body, name="adamw_small", out_shape=shapes * 4)(*ws, *gs, *ms, *vs)
    return [res[k * n:(k + 1) * n] for k in range(4)]


ADAM_SPLIT = 4


def _store_without_head_padding(dst_ref, g):
    assert HEAD_PAD == 2 * LANES and 2 * (QK_NOPE + QK_ROPE) == 3 * LANES, (HEAD_PAD, QK_NOPE, QK_ROPE)
    assert g.shape[1] % (2 * HEAD_PAD) == 0, g.shape
    low = lax.broadcasted_iota(jnp.int32, (g.shape[0], LANES), 1) < LANES // 2
    for pair in range(g.shape[1] // (2 * HEAD_PAD)):
        t = [g[:, (4 * pair + k) * LANES:(4 * pair + k + 1) * LANES] for k in range(4)]
        moved = [pltpu.roll(t[k], LANES // 2, axis=1) for k in (2, 3)]
        outs = (t[0], jnp.where(low, t[1], moved[0]), jnp.where(low, moved[0], moved[1]))
        for k, o in enumerate(outs):
            dst_ref[:, (3 * pair + k) * LANES:(3 * pair + k + 1) * LANES] = o


def _adamw_shards(ids, items, name):
    n = len(items)

    def body(ids_ref, *refs):
        outs = refs[len(refs) - 4 * n:]
        for i, it in enumerate(items):
            w_ref, m_ref, v_ref, gm_ref, gs_ref = refs[5 * i:5 * i + 5]
            g_ref, d_ref, nm_ref, nv_ref = outs[4 * i:4 * i + 4]
            cols = slice(*it["gcols"]) if it.get("gcols") else slice(None)
            whose = pl.program_id(0) if it.get("owner") is None else it["owner"]
            mine = whose == ids_ref[0]

            def take(src_ref, g_ref=g_ref, cols=cols, head_padded=it.get("head_padded")):
                if head_padded:
                    _store_without_head_padding(g_ref, src_ref[...])
                else:
                    g_ref[...] = src_ref[:, cols]

            @pl.when(mine)
            def _(take=take, gm_ref=gm_ref):
                take(gm_ref)

            @pl.when(jnp.logical_not(mine))
            def _(take=take, gs_ref=gs_ref):
                take(gs_ref)

            d_ref[...], nm_ref[...], nv_ref[...] = _adam_math(w_ref[...], g_ref[...], m_ref[...], v_ref[...])

    in_specs, out_specs, out_shape, args, carried, aliases = [], [], [], [ids], [], {}
    for i, it in enumerate(items):
        w = it["w"]
        r, c = w.shape[-2:]
        tr = r // 2 // ADAM_SPLIT
        assert tr % 8 == 0, (name, w.shape)
        layer = it.get("layer")
        if layer is None:
            wspec = pl.BlockSpec((tr, c), lambda h, k, ids: (h * ADAM_SPLIT + k, 0))
        else:
            wspec = pl.BlockSpec((None, tr, c), lambda h, k, ids, layer=layer: (layer, h * ADAM_SPLIT + k, 0))
        gc = it["g_mine"].shape[1]

        def g_index(of_mine, owner=it.get("owner")):
            def index(h, k, ids):
                if owner is None:
                    half = ids[0] if of_mine else 1 - ids[0]
                    return jnp.where(h == half, k, jnp.where(h < half, 0, ADAM_SPLIT - 1)), 0
                read = (owner == ids[0]) if of_mine else (owner != ids[0])
                return jnp.where(read, h * ADAM_SPLIT + k, 0), 0
            return index

        in_specs += [wspec] * 3 + [pl.BlockSpec((tr, gc), g_index(True)), pl.BlockSpec((tr, gc), g_index(False))]
        args += [w, it["m"], it["v"], it["g_mine"], it["g_sib"]]
        out_specs += [wspec] * 4
        out_shape += [jax.ShapeDtypeStruct(w.shape, F32)] * 4
        if it.get("prev") is not None:
            for k, p in enumerate(it["prev"]):
                aliases[1 + 5 * n + len(carried)] = 4 * i + k
                carried.append(p)
    res = _tc_call(
        body, name=name, prefetch=1, grid=(2, ADAM_SPLIT), in_specs=in_specs + [ANY] * len(carried),
        out_specs=out_specs, out_shape=out_shape, input_output_aliases=aliases,
        compiler_params=_cp("parallel", "parallel"),
    )(*args, *carried)
    return [res[4 * i:4 * i + 4] for i in range(n)]


def _peer_chip(k_me, j):
    return k_me ^ jnp.where(j == 0, 2, jnp.where(j == 1, 1, 3))


def _pair_sums(ids, gs, ras, name):
    n = len(gs)

    def body(ids_ref, *refs):
        for i in range(n):
            g_ref, ra_ref, o_ref = refs[2 * i], refs[2 * i + 1], refs[2 * n + i]
            o_ref[...] = (g_ref[...].astype(F32) + ra_ref[...].astype(F32)).astype(BF16)

    in_specs, out_specs, out_shape = [], [], []
    for g in gs:
        half, c = g.shape[1] // 2, g.shape[2]
        in_specs += [pl.BlockSpec((None, half, c), lambda j, ids: (_peer_chip(ids[1], j), ids[0], 0)),
                     pl.BlockSpec((None, half, c), lambda j, ids: (_peer_chip(ids[1], j), 0, 0))]
        out_specs.append(pl.BlockSpec((None, half, c), lambda j, ids: (j, 0, 0)))
        out_shape.append(jax.ShapeDtypeStruct((3, half, c), BF16))
    return _tc_call(
        body, name=name, prefetch=1, grid=(3,), in_specs=in_specs, out_specs=out_specs, out_shape=out_shape,
        compiler_params=_cp("parallel"),
    )(ids, *[a for pair in zip(gs, ras) for a in pair])


def _chip_sums(ids, gs, ras, rbs, name):
    n = len(gs)

    def body(ids_ref, *refs):
        for i in range(n):
            g_ref, ra_ref, rb_ref, o_ref = refs[3 * i], refs[3 * i + 1], refs[3 * i + 2], refs[3 * n + i]
            acc = g_ref[...].astype(F32) + ra_ref[...].astype(F32)
            for j in range(3):
                acc = acc + rb_ref[j].astype(F32)
            o_ref[...] = acc

    in_specs, out_specs, out_shape = [], [], []
    for g in gs:
        half, c = g.shape[1] // 2, g.shape[2]
        in_specs += [pl.BlockSpec((None, half, c), lambda i, ids: (ids[1], ids[0], 0)),
                     pl.BlockSpec((None, half, c), lambda i, ids: (ids[1], 0, 0)),
                     pl.BlockSpec((3, half, c), lambda i, ids: (0, 0, 0))]
        out_specs.append(pl.BlockSpec((half, c), lambda i, ids: (0, 0)))
        out_shape.append(jax.ShapeDtypeStruct((half, c), F32))
    return _tc_call(
        body, name=name, prefetch=1, grid=(1,), in_specs=in_specs, out_specs=out_specs, out_shape=out_shape,
        compiler_params=_cp("arbitrary"),
    )(ids, *[a for trio in zip(gs, ras, rbs) for a in trio])


def _position():
    x, y, c = lax.axis_index("x"), lax.axis_index("y"), lax.axis_index("c")
    chips = [(1 - x, y), (x, 1 - y), (1 - x, 1 - y)]
    return x, y, c, chips


def _shard_half(ref, wm, h):
    if wm.kind == "tiny":
        return ref
    if wm.nl == 2:
        return ref.at[h]
    return ref.at[pl.ds(pl.multiple_of(h * (wm.k // 2), 16), wm.k // 2), :]


def _region(full, wm, s, h):
    if wm.kind == "tiny":
        return full.at[s]
    cols = pl.ds(pl.multiple_of(s * wm.n, LANES), wm.n) if wm.kind == "col" else slice(None)
    if wm.nl == 2:
        rows = pl.ds(pl.multiple_of(s * wm.k, 16), wm.k) if wm.kind == "row" else slice(None)
        return full.at[slice(None) if h is None else h, rows, cols]
    if wm.kind == "col":
        rows = slice(None) if h is None else pl.ds(pl.multiple_of(h * (wm.k // 2), 16), wm.k // 2)
    elif h is None:
        rows = pl.ds(pl.multiple_of(s * wm.k, 16), wm.k)
    else:
        rows = pl.ds(pl.multiple_of(s * wm.k + h * (wm.k // 2), 16), wm.k // 2)
    return full.at[rows, cols]


def _full_shape(wm):
    if wm.kind == "tiny":
        return (N_CHIPS, wm.k, wm.n)
    shape = (wm.k, N_CHIPS * wm.n) if wm.kind == "col" else (N_CHIPS * wm.k, wm.n)
    return shape if wm.nl == 1 else (wm.nl,) + shape


def _handshake(peers):
    barrier = pltpu.get_barrier_semaphore()
    for peer in peers:
        pl.semaphore_signal(barrier, inc=1, device_id=peer, device_id_type=MESH)
    pl.semaphore_wait(barrier, len(peers))


def _all_gather_group(gi, shards):
    wms = AG_GROUPS[gi]
    nw = len(wms)

    def body(*refs):
        sh, full = refs[:nw], refs[nw:2 * nw]
        ici_s, ici_r, pass_s, pass_r, own_s, own_r = refs[2 * nw:]
        x, y, c, _ = _position()
        me, sibling = 2 * x + y, (x, y, 1 - c)
        first, second, diagonal = (x ^ (1 - c), y ^ c), (x ^ c, y ^ (1 - c)), (1 - x, 1 - y)
        chip_id = lambda chip: 2 * chip[0] + chip[1]
        _handshake([(*first, c), (*second, c), sibling])

        def rcopy(src, dst, s_sem, r_sem, to):
            return pltpu.make_async_remote_copy(src_ref=src, dst_ref=dst, send_sem=s_sem, recv_sem=r_sem,
                                                device_id=to, device_id_type=MESH)

        started = []

        def go(cp):
            cp.start()
            started.append(cp)

        for i, wm in enumerate(wms):
            half, dst = _shard_half(sh[i], wm, c), _region(full[i], wm, me, c)
            go(rcopy(half, dst, ici_s.at[i, 0], ici_r.at[i, 0], (*first, c)))
            go(rcopy(half, dst, ici_s.at[i, 1], ici_r.at[i, 1], (*second, c)))
            go(rcopy(sh[i], _region(full[i], wm, me, None), own_s.at[i], own_r.at[i], sibling))
        for i, wm in enumerate(wms):
            got = _region(full[i], wm, chip_id(first), c)
            rcopy(got, got, ici_s.at[i, 0], ici_r.at[i, 0], sibling).wait_recv()
            go(rcopy(got, got, ici_s.at[i, 2], ici_r.at[i, 2], (*second, c)))
            if wm.kind != "tiny":
                go(rcopy(got, got, pass_s.at[i, 0], pass_r.at[i, 0], sibling))
        for i, wm in enumerate(wms):
            for j, chip in ((1, second), (2, diagonal)):
                got = _region(full[i], wm, chip_id(chip), c)
                rcopy(got, got, ici_s.at[i, j], ici_r.at[i, j], sibling).wait_recv()
                if wm.kind != "tiny":
                    go(rcopy(got, got, pass_s.at[i, j], pass_r.at[i, j], sibling))
        for i, wm in enumerate(wms):
            mine = _region(full[i], wm, me, None)
            rcopy(mine, mine, own_s.at[i], own_r.at[i], sibling).wait_recv()
            if wm.kind != "tiny":
                for j, chip in ((0, second), (1, first), (2, diagonal)):
                    got = _region(full[i], wm, chip_id(chip), 1 - c)
                    rcopy(got, got, pass_s.at[i, j], pass_r.at[i, j], sibling).wait_recv()
        for cp in started:
            cp.wait_send()

    return pl.kernel(
        body, out_type=[jax.ShapeDtypeStruct(_full_shape(wm), s.dtype) for wm, s in zip(wms, shards)],
        mesh=plsc.ScalarSubcoreMesh(axis_name="sequencer", num_cores=1), name=f"ag_group{gi}",
        scratch_types=[pltpu.SemaphoreType.DMA((nw, 3))] * 4 + [pltpu.SemaphoreType.DMA((nw,))] * 2,
        compiler_params=pltpu.CompilerParams(collective_id=gi),
    )(*shards)


def _sequencer_call(body, name, cid, out_types, scratch, args):
    return pl.kernel(
        body, out_type=out_types, mesh=plsc.ScalarSubcoreMesh(axis_name="sequencer", num_cores=1), name=name,
        scratch_types=scratch, compiler_params=pltpu.CompilerParams(collective_id=cid),
    )(*args)


def _pair_exchange(gs, tag, cid):
    n = len(gs)

    def body(*refs):
        g, out, send_sems, recv_sems = refs[:n], refs[n:2 * n], refs[2 * n], refs[2 * n + 1]
        x, y, c, _ = _position()
        _handshake([(x, y, 1 - c)])
        cps = []
        for i in range(n):
            half = g[i].shape[1] // 2
            cps.append(pltpu.make_async_remote_copy(
                src_ref=g[i].at[:, pl.ds(pl.multiple_of((1 - c) * half, 16), half), :], dst_ref=out[i],
                send_sem=send_sems.at[i], recv_sem=recv_sems.at[i], device_id=(x, y, 1 - c), device_id_type=MESH))
            cps[-1].start()
        for cp in cps:
            cp.wait()

    return _sequencer_call(
        body, f"rs_pair_exchange{tag}", cid,
        [jax.ShapeDtypeStruct((a.shape[0], a.shape[1] // 2, a.shape[2]), a.dtype) for a in gs],
        [pltpu.SemaphoreType.DMA((n,)), pltpu.SemaphoreType.DMA((n,))], gs)


def _chip_exchange(ss, tag, cid):
    n = len(ss)

    def body(*refs):
        s, out, send_sems, recv_sems = refs[:n], refs[n:2 * n], refs[2 * n], refs[2 * n + 1]
        x, y, c, chips = _position()
        _handshake([(*chip, c) for chip in chips])
        cps = []
        for i in range(n):
            for j, chip in enumerate(chips):
                cps.append(pltpu.make_async_remote_copy(
                    src_ref=s[i].at[j], dst_ref=out[i].at[j], send_sem=send_sems.at[i, j], recv_sem=recv_sems.at[i, j],
                    device_id=(*chip, c), device_id_type=MESH))
                cps[-1].start()
        for cp in cps:
            cp.wait()

    return _sequencer_call(
        body, f"rs_chip_exchange{tag}", cid, [jax.ShapeDtypeStruct(a.shape, a.dtype) for a in ss],
        [pltpu.SemaphoreType.DMA((n, 3)), pltpu.SemaphoreType.DMA((n, 3))], ss)


def _pair_swap(g8s, tag, cid):
    n = len(g8s)

    def body(*refs):
        g, out, send_sems, recv_sems = refs[:n], refs[n:2 * n], refs[2 * n], refs[2 * n + 1]
        x, y, c, _ = _position()
        _handshake([(x, y, 1 - c)])
        cps = []
        for i in range(n):
            cps.append(pltpu.make_async_remote_copy(
                src_ref=g[i], dst_ref=out[i], send_sem=send_sems.at[i], recv_sem=recv_sems.at[i],
                device_id=(x, y, 1 - c), device_id_type=MESH))
            cps[-1].start()
        for cp in cps:
            cp.wait()

    return _sequencer_call(
        body, f"rs_pair_swap{tag}", cid, [jax.ShapeDtypeStruct(a.shape, a.dtype) for a in g8s],
        [pltpu.SemaphoreType.DMA((n,)), pltpu.SemaphoreType.DMA((n,))], g8s)


def _pair_swap_now(g8s):
    n = len(g8s)

    def body(*refs):
        g, out, send_sems, recv_sems = refs[:n], refs[n:2 * n], refs[2 * n], refs[2 * n + 1]
        x, y, c, _ = _position()
        cps = []
        for i in range(n):
            cps.append(pltpu.make_async_remote_copy(
                src_ref=g[i], dst_ref=out[i], send_sem=send_sems.at[i], recv_sem=recv_sems.at[i],
                device_id=(x, y, 1 - c), device_id_type=MESH))
            cps[-1].start()
        for cp in cps:
            cp.wait()

    return _tc_call(
        body, name="rs_pair_swap_last", in_specs=[ANY] * n, out_specs=[ANY] * n,
        out_shape=[jax.ShapeDtypeStruct(a.shape, a.dtype) for a in g8s],
        scratch_shapes=[pltpu.SemaphoreType.DMA((n,)), pltpu.SemaphoreType.DMA((n,))],
    )(*g8s)


def _all_reduce_small(vecs, owner_major, name):
    n = len(vecs)
    block = lambda i, ref, chip: ref.at[chip] if owner_major[i] else ref
    out_shapes = [a.shape[1:] if owner_major[i] else a.shape for i, a in enumerate(vecs)]

    def body(*refs):
        v, o, gath = refs[:n], refs[n:2 * n], refs[2 * n:3 * n]
        send_sems, recv_sems = refs[3 * n], refs[3 * n + 1]
        x, y, c, _ = _position()
        me = 4 * x + 2 * y + c
        cps = []
        for i in range(n):
            gath[i][me] = block(i, v[i], 2 * x + y)[...]
            for rel in range(1, N_DEV):
                px, py, pc = x ^ (rel >> 2), y ^ ((rel >> 1) & 1), c ^ (rel & 1)
                cps.append(pltpu.make_async_remote_copy(
                    src_ref=block(i, v[i], 2 * px + py), dst_ref=gath[i].at[me], send_sem=send_sems.at[i, rel - 1],
                    recv_sem=recv_sems.at[i, rel - 1], device_id=(px, py, pc), device_id_type=MESH))
        for cp in cps:
            cp.start()
        for i in range(n):
            for rel in range(1, N_DEV):
                pltpu.make_async_remote_copy(
                    src_ref=block(i, v[i], 2 * x + y), dst_ref=gath[i].at[me ^ rel],
                    send_sem=send_sems.at[i, rel - 1], recv_sem=recv_sems.at[i, rel - 1], device_id=(x, y, c),
                    device_id_type=MESH).wait_recv()
        for cp in cps:
            cp.wait_send()
        for i in range(n):
            acc = gath[i][0]
            for d in range(1, N_DEV):
                acc = acc + gath[i][d]
            o[i][...] = acc

    vm = pl.BlockSpec(memory_space=pltpu.VMEM)
    return _tc_call(
        body, name=name, in_specs=[vm] * n, out_specs=[vm] * n,
        out_shape=[jax.ShapeDtypeStruct(s, F32) for s in out_shapes],
        scratch_shapes=[pltpu.VMEM((N_DEV,) + s, F32) for s in out_shapes]
        + [pltpu.SemaphoreType.DMA((n, N_DEV - 1)), pltpu.SemaphoreType.DMA((n, N_DEV - 1))],
    )(*vecs)


def _rope_tables(positions):
    half = QK_ROPE // 2
    inv_freq = 1.0 / (ROPE_THETA ** (jnp.arange(half, dtype=F32) / half))
    ang = positions.astype(F32)[:, None] * inv_freq
    zeros = jnp.zeros((positions.shape[0], LANES - QK_ROPE), F32)
    cos, sin = jnp.cos(ang), jnp.sin(ang)
    return jnp.concatenate([cos, cos, zeros], axis=1), jnp.concatenate([sin, sin, zeros], axis=1)


def _local_step(x, positions, tgt, wf, small, rs):
    cos, sin = _rope_tables(positions)
    w_in, w_out = wf["sc_w_in"], wf["sc_w_out"]
    w_ups, w_downs = (wf["ffn_w_up0"], wf["ffn_w_up1"]), (wf["ffn_w_down0"], wf["ffn_w_down1"])
    w_kv, w_ukv, w_dq, w_uq, w_o = wf["w_kv"], wf["w_ukv"], wf["w_dq"], wf["w_uq"], wf["w_o"]
    attn_norm, ffn_norm = small["attn_norm"], small["ffn_norm"]
    conv_b = small["ffn_conv_b"]

    def ffn_fwd(h, hf, l, then):
        up, a = _ffn_up_gate(hf, w_ups[l], small["ffn_conv_w"][l], conv_b[l:l + 1], f"ffn{l}_up_gate")
        return then(a, w_downs[l], h), (hf, up, a)

    def ffn_bwd(h, dh_out, dh_out_b, l, saved, gi, hooks):
        run = lambda stage: hooks.get(stage, lambda: None)()
        hf, up, a = saved
        d_down = _tn(f"ffn{l}_down_dw", a, dh_out_b, BF16)
        run("down_dw")
        dup, d_cw, d_cb = _gate_bwd(up, small["ffn_conv_w"][l], conv_b[l:l + 1], dh_out_b, w_downs[l],
                                    f"ffn{l}_gate_bwd")
        run("gate_bwd")
        d_up = _dw_ffn_up(f"ffn{l}_up_dw", hf, dup)
        rs.start(gi, {f"ffn_w_down{l}": d_down.reshape(N_CHIPS, F_FF // N_CHIPS, D), f"ffn_w_up{l}": d_up})
        run("up_dw")
        dh, dh_b, d_norm = _dx_norm_bwd(f"ffn{l}_up_dx", dup, w_ups[l], h, ffn_norm[l:l + 1], dh_out)
        run("up_dx")
        return dh, dh_b, d_cw, d_cb, d_norm

    hn0 = _rms_fwd(x, attn_norm[0:1], "attn0_norm")
    z = _nn_parts("sc_in", hn0, w_in, 3, BF16)
    mix = _scmix_fwd(z, small["sc_conv_w"])
    h1, hf0 = _nn_add_norm("sc_out", mix, w_out, x, ffn_norm[0:1])
    h2, ffn0_saved = ffn_fwd(h1, hf0, 0, lambda a, w, h: _nn("ffn0_down", a, w, F32, add=h))

    hn1, hk, cq_pre, cq, q, kvpre, ckv, kr, knv = _attn_prep(
        h2, attn_norm[1:2], small["kv_in_norm"], w_dq, small["q_latent_norm"], w_uq, w_kv, small["kv_latent_norm"],
        w_ukv, cos, sin)
    o = _attn_fwd(q, knv, kr)
    h3, hf1 = _nn_add_norm("attn_out", o, w_o, h2, ffn_norm[1:2])
    (loss, dh4, dh4_b, d_final), ffn1_saved = ffn_fwd(
        h3, hf1, 1, lambda a, w, h: _nn_add_loss("ffn1_down_loss", a, w, h, small["final_norm"], tgt))

    rows = D // N_CHIPS
    dh3, dh3_b, d_cw1, d_cb1, d_fn1 = ffn_bwd(h3, dh4, dh4_b, 1, ffn1_saved, 0, {})

    do = _nt("attn_out_dx", dh3_b, w_o, BF16)
    d_wo = _tn("attn_out_dw", o, dh3_b, BF16)
    rs.pair_sums(0)
    dq, dknv, dkr = _attn_bwd(q, knv, kr, do, cos, sin)
    rs.chip_sums(0)
    dh2, dh2_b, d_wuq, d_wdq, d_wukv, d_wkv, d_an1, d_kvin, d_qln, d_kvln = _attn_prep_bwd(
        dq, dknv, dkr, dh3, h2, hn1, hk, cq_pre, cq, kvpre, ckv, attn_norm[1:2], small["kv_in_norm"], w_dq,
        small["q_latent_norm"], w_uq, w_kv, small["kv_latent_norm"], w_ukv, cos, sin)
    rs.finish(0)
    by_owner = lambda dw: dw.reshape(dw.shape[0], N_CHIPS, -1).transpose(1, 0, 2)
    rs.start(1, {
        "w_o": d_wo.reshape(N_CHIPS, rows, D), "w_uq": by_owner(d_wuq), "w_dq": d_wdq.reshape(N_CHIPS, rows, Q_LORA),
        "w_ukv": by_owner(d_wukv.reshape(2 * KV_LORA, -1)).reshape(N_CHIPS, 2 * KV_LORA, -1),
        "w_kv": d_wkv.reshape(N_CHIPS, rows, KVP),
    })

    dh1, dh1_b, d_cw0, d_cb0, d_fn0 = ffn_bwd(h1, dh2, dh2_b, 0, ffn0_saved, 2, {
        "down_dw": lambda: rs.pair_sums(1), "gate_bwd": lambda: rs.chip_sums(1),
        "up_dw": lambda: (rs.finish(1), rs.pair_sums(2))})

    d_wout = _tn("sc_out_dw", mix, dh1_b, BF16)
    dmix = _nt("sc_out_dx", dh1_b, w_out, BF16)
    dz, d_scw = _scmix_bwd(z, small["sc_conv_w"], dmix)
    d_win = _dw_sc_in(hn0, dz)
    rs.start(3, {"sc_w_out": d_wout.reshape(N_CHIPS, rows, D), "sc_w_in": d_win})
    dx, _, d_an0 = _dx_norm_bwd("sc_in_dx", dz, w_in, x, attn_norm[0:1], dh1)

    taps_by_owner = lambda per_layer: jnp.stack(per_layer, axis=1).reshape(3, len(per_layer), N_CHIPS, -1).transpose(2, 0, 1, 3)
    small_g = {
        "attn_norm": jnp.concatenate([d_an0, d_an1]), "ffn_norm": jnp.concatenate([d_fn0, d_fn1]),
        "final_norm": d_final, "kv_in_norm": d_kvin, "kv_latent_norm": d_kvln, "q_latent_norm": d_qln,
        "ffn_conv_b": jnp.concatenate([d_cb0, d_cb1]),
        "sc_conv_w": taps_by_owner([d_scw]), "ffn_conv_w": taps_by_owner([d_cw0, d_cw1]),
    }
    return loss, dx, small_g


RS_GROUPS = (("ffn_w_down1", "ffn_w_up1"), ("w_o", "w_uq", "w_dq", "w_ukv", "w_kv"),
             ("ffn_w_down0", "ffn_w_up0"), ("sc_w_out", "sc_w_in"))


class _ReduceScatter:
    def __init__(self, ids, finish):
        self.ids, self.grads, self.step, self.mine, self.sib, self.finish = ids, {}, {}, {}, {}, finish

    def _cid(self, gi):
        return len(AG_GROUPS) + 3 * gi

    def start(self, gi, grads):
        self.grads.update(grads)
        own = [grads[n] for n in RS_GROUPS[gi]]
        self.step[gi] = (own, _pair_exchange(own, gi, self._cid(gi)))

    def pair_sums(self, gi):
        own, ra = self.step[gi]
        sums = _pair_sums(self.ids, own, ra, f"rs_pair_sums{gi}")
        self.step[gi] = (own, ra, _chip_exchange(sums, gi, self._cid(gi) + 1))

    def chip_sums(self, gi):
        own, ra, rb = self.step[gi]
        mine = _chip_sums(self.ids, own, ra, rb, f"rs_chip_sums{gi}")
        self.mine.update(zip(RS_GROUPS[gi], mine))
        last = gi == len(RS_GROUPS) - 1
        swapped = _pair_swap_now(mine) if last else _pair_swap(mine, gi, self._cid(gi) + 2)
        self.sib.update(zip(RS_GROUPS[gi], swapped))


SMALL_REPL = ("attn_norm", "ffn_norm", "final_norm", "kv_in_norm", "kv_latent_norm", "q_latent_norm", "ffn_conv_b")


def _pad_heads(w_uq):
    per_head = w_uq.reshape(Q_LORA, -1, QK_NOPE + QK_ROPE)
    return jnp.pad(per_head, ((0, 0), (0, 0), (0, HEAD_PAD - QK_NOPE - QK_ROPE))).reshape(Q_LORA, -1)


def _pack_kv(w_dkv, w_kr):
    return jnp.concatenate([w_dkv, w_kr, jnp.zeros((w_kr.shape[0], LANES - QK_ROPE), w_kr.dtype)], axis=1)


def kernel(x, positions, attn_norm, ffn_norm, final_norm, sc_w_in, sc_conv_w, sc_w_out, kv_in_norm, w_dkv, kv_latent_norm, w_kr, w_uk, w_uv, w_dq, q_latent_norm, w_uq, w_o, ffn_w_up, ffn_conv_w, ffn_conv_b, ffn_w_down, loss_target, m_attn_norm, m_ffn_norm, m_final_norm, m_sc_w_in, m_sc_conv_w, m_sc_w_out, m_kv_in_norm, m_w_dkv, m_kv_latent_norm, m_w_kr, m_w_uk, m_w_uv, m_w_dq, m_q_latent_norm, m_w_uq, m_w_o, m_ffn_w_up, m_ffn_conv_w, m_ffn_conv_b, m_ffn_w_down, v_attn_norm, v_ffn_norm, v_final_norm, v_sc_w_in, v_sc_conv_w, v_sc_w_out, v_kv_in_norm, v_w_dkv, v_kv_latent_norm, v_w_kr, v_w_uk, v_w_uv, v_w_dq, v_q_latent_norm, v_w_uq, v_w_o, v_ffn_w_up, v_ffn_conv_w, v_ffn_conv_b, v_ffn_w_down):
    names = ("attn_norm", "ffn_norm", "final_norm", "sc_w_in", "sc_conv_w", "sc_w_out", "kv_in_norm", "w_dkv",
             "kv_latent_norm", "w_kr", "w_uk", "w_uv", "w_dq", "q_latent_norm", "w_uq", "w_o", "ffn_w_up",
             "ffn_conv_w", "ffn_conv_b", "ffn_w_down")
    w = dict(zip(names, (attn_norm, ffn_norm, final_norm, sc_w_in, sc_conv_w, sc_w_out, kv_in_norm, w_dkv,
                         kv_latent_norm, w_kr, w_uk, w_uv, w_dq, q_latent_norm, w_uq, w_o, ffn_w_up,
                         ffn_conv_w, ffn_conv_b, ffn_w_down)))
    m = dict(zip(names, (m_attn_norm, m_ffn_norm, m_final_norm, m_sc_w_in, m_sc_conv_w, m_sc_w_out, m_kv_in_norm,
                         m_w_dkv, m_kv_latent_norm, m_w_kr, m_w_uk, m_w_uv, m_w_dq, m_q_latent_norm, m_w_uq, m_w_o,
                         m_ffn_w_up, m_ffn_conv_w, m_ffn_conv_b, m_ffn_w_down)))
    v = dict(zip(names, (v_attn_norm, v_ffn_norm, v_final_norm, v_sc_w_in, v_sc_conv_w, v_sc_w_out, v_kv_in_norm,
                         v_w_dkv, v_kv_latent_norm, v_w_kr, v_w_uk, v_w_uv, v_w_dq, v_q_latent_norm, v_w_uq, v_w_o,
                         v_ffn_w_up, v_ffn_conv_w, v_ffn_conv_b, v_ffn_w_down)))

    _ORDER[0] = None
    ix, iy, ic = lax.axis_index("x"), lax.axis_index("y"), lax.axis_index("c")
    chip = 2 * ix + iy
    ids = jnp.stack([ic, chip]).astype(jnp.int32)

    ws = {
        "sc_w_in": sc_w_in[0], "sc_w_out": sc_w_out[0], "ffn_w_up": ffn_w_up, "ffn_w_down": ffn_w_down,
        "w_kv": _pack_kv(w_dkv, w_kr), "w_ukv": jnp.stack([w_uk, w_uv]), "w_dq": w_dq[0],
        "w_uq": _pad_heads(w_uq[0]), "w_o": w_o[0],
    }

    def ag_shard(name):
        if name == "sc_conv_w":
            return sc_conv_w[0]
        if name == "ffn_conv_w":
            return ffn_conv_w.reshape(6, -1)
        if name[:-1] in ("ffn_w_up", "ffn_w_down"):
            return ws[name[:-1]][int(name[-1])].astype(BF16)
        return ws[name].astype(BF16)

    wf = {}
    for gi, wms in enumerate(AG_GROUPS):
        fulls = _all_gather_group(gi, [ag_shard(wm.name) for wm in wms])
        wf.update({wm.name: f for wm, f in zip(wms, fulls)})
    small = {
        "attn_norm": attn_norm, "ffn_norm": ffn_norm, "final_norm": final_norm[None], "kv_in_norm": kv_in_norm[None],
        "kv_latent_norm": kv_latent_norm[None], "q_latent_norm": q_latent_norm, "ffn_conv_b": ffn_conv_b,
        "sc_conv_w": wf["sc_conv_w"].transpose(1, 0, 2).reshape(3, D),
        "ffn_conv_w": wf["ffn_conv_w"].reshape(N_CHIPS, 2, 3, -1).transpose(1, 2, 0, 3).reshape(2, 3, F_FF),
    }

    res = {}

    held = {
        "ffn_w_up0": [("ffn_w_up", dict(layer=0))], "ffn_w_up1": [("ffn_w_up", dict(layer=1))],
        "ffn_w_down0": [("ffn_w_down", dict(layer=0))], "ffn_w_down1": [("ffn_w_down", dict(layer=1))],
        "sc_w_in": [("sc_w_in", dict(layer=0))], "sc_w_out": [("sc_w_out", dict(layer=0))],
        "w_dq": [("w_dq", dict(layer=0))], "w_o": [("w_o", dict(layer=0))], "w_uq": [("w_uq", dict(layer=0, head_padded=True))],
        "w_kv": [("w_dkv", dict(gcols=(0, KV_LORA))), ("w_kr", dict(gcols=(KV_LORA, KV_LORA + QK_ROPE)))],
        "w_ukv": [("w_uk", dict(owner=0)), ("w_uv", dict(owner=1))],
    }

    def adamw_group(gi):
        items = []
        for key in RS_GROUPS[gi]:
            for n, opts in held[key]:
                items.append(dict(name=n, w=w[n], m=m[n], v=v[n], g_mine=rs.mine[key], g_sib=rs.sib[key],
                                  prev=res.get(n) if "layer" in opts and w[n].shape[0] > 1 else None, **opts))
        for it, out in zip(items, _adamw_shards(ids, items, f"adamw_group{gi}")):
            res[it["name"]] = out

    rs = _ReduceScatter(ids, adamw_group)
    loss, dx, small_g = _local_step(x[0], positions[0], loss_target[0], wf, small, rs)

    rs.chip_sums(2)
    rs.pair_sums(3)

    s_names = list(small_g)
    reduced = _all_reduce_small([small_g[n] for n in s_names] + [loss], [small_g[n].ndim == 4 for n in s_names] + [False],
                                "ar_small")
    sg, loss_out = dict(zip(s_names, reduced[:-1])), reduced[-1][0, 0]

    row = lambda n: (lambda t: t[n][None])
    taps = lambda n: (lambda t: t[n].transpose(1, 0, 2))
    small_2d = {
        "attn_norm": (sg["attn_norm"], lambda t: t["attn_norm"]), "ffn_norm": (sg["ffn_norm"], lambda t: t["ffn_norm"]),
        "final_norm": (sg["final_norm"], row("final_norm")), "kv_in_norm": (sg["kv_in_norm"], row("kv_in_norm")),
        "kv_latent_norm": (sg["kv_latent_norm"], row("kv_latent_norm")),
        "q_latent_norm": (sg["q_latent_norm"], lambda t: t["q_latent_norm"]),
        "ffn_conv_b": (sg["ffn_conv_b"], lambda t: t["ffn_conv_b"]),
        "sc_conv_w": (sg["sc_conv_w"], taps("sc_conv_w")), "ffn_conv_w": (sg["ffn_conv_w"], taps("ffn_conv_w")),
    }
    s_keys = list(small_2d)
    small_grads = [small_2d[k][0] for k in s_keys]
    views = lambda tree: [small_2d[k][1](tree) for k in s_keys]
    small_res = _adamw_small(views(w), small_grads, views(m), views(v))

    def restore(vals):
        by = dict(zip(s_keys, vals))
        out = {n: by[n].reshape(w[n].shape) for n in SMALL_REPL}
        out.update({n: by[n].transpose(1, 0, 2) for n in ("sc_conv_w", "ffn_conv_w")})
        return out

    rs.finish(2)
    rs.chip_sums(3)
    rs.finish(3)
    outs = [restore(vals) for vals in small_res]
    for k, dst in enumerate(outs):
        for n in res:
            dst[n] = res[n][k]
    grads, delta, new_m, new_v = outs

    _ORDER[0] = None
    return (loss_out, dx[None], *[grads[n] for n in names], *[delta[n] for n in names],
            *[new_m[n] for n in names], *[new_v[n] for n in names])
```

```python
from typing import NamedTuple

import jax
import jax.numpy as jnp
from jax import lax
from jax.experimental import pallas as pl
from jax.experimental.pallas import tpu as pltpu
from jax.experimental.pallas import tpu_sc as plsc

F32 = jnp.float32
BF16 = jnp.bfloat16

T = 2048
D = 1024
F_FF = 2816
N_HEADS = 8
QK_NOPE = 128
QK_ROPE = 64
V_HEAD = 128
Q_LORA = 384
KV_LORA = 256
CHUNK_SHIFT = 6
ROPE_THETA = 10000.0
EPS = 1e-6
NEG_INF = -1e30
HEAD_PAD = 256
KVP = KV_LORA + 128

ADAM_LR = 0.001
ADAM_B1 = 0.9
ADAM_B2 = 0.999
ADAM_EPS = 1e-08
ADAM_WD = 0.01
ADAM_STEP = 10

N_CHIPS = 4
N_DEV = 8
LANES = 128
TC = 256
V7X_VMEM_LIMIT = 56 * 1024 * 1024

MESH = pl.DeviceIdType.MESH
ANY = pl.BlockSpec(memory_space=pl.ANY)


class _W(NamedTuple):
    name: str
    kind: str
    nl: int
    k: int
    n: int


AG_GROUPS = (
    (_W("sc_w_in", "col", 1, D, 3 * D // N_CHIPS), _W("sc_conv_w", "tiny", 1, 3, D // N_CHIPS),
     _W("ffn_conv_w", "tiny", 1, 6, F_FF // N_CHIPS), _W("sc_w_out", "row", 1, D // N_CHIPS, D)),
    (_W("ffn_w_up0", "col", 1, D, 2 * F_FF // N_CHIPS),),
    (_W("ffn_w_down0", "row", 1, F_FF // N_CHIPS, D),),
    (_W("w_kv", "row", 1, D // N_CHIPS, KVP), _W("w_ukv", "col", 2, KV_LORA, N_HEADS * QK_NOPE // N_CHIPS),
     _W("w_dq", "row", 1, D // N_CHIPS, Q_LORA),
     _W("w_uq", "col", 1, Q_LORA, N_HEADS * HEAD_PAD // N_CHIPS),
     _W("w_o", "row", 1, N_HEADS * V_HEAD // N_CHIPS, D)),
    (_W("ffn_w_up1", "col", 1, D, 2 * F_FF // N_CHIPS), _W("ffn_w_down1", "row", 1, F_FF // N_CHIPS, D)),
)


def _cp(*sem):
    return pltpu.CompilerParams(dimension_semantics=sem, vmem_limit_bytes=V7X_VMEM_LIMIT)


_ORDER = [None]


def _tc_call(body, *, name, out_shape, in_specs=None, out_specs=None, grid=(), scratch_shapes=(), prefetch=0,
             input_output_aliases=None, compiler_params=None):
    def run(*args):
        specs = [pl.BlockSpec(memory_space=pltpu.VMEM)] * (len(args) - prefetch) if in_specs is None else list(in_specs)
        inner, dep = body, _ORDER[0]
        if dep is not None:
            unread = prefetch + len(specs)
            specs, args = specs + [ANY], (*args, dep)

            def inner(*refs):
                return body(*refs[:unread], *refs[unread + 1:])

        kwargs = dict(name=name, out_shape=out_shape, input_output_aliases=input_output_aliases or {},
                      compiler_params=compiler_params)
        if prefetch:
            kwargs["grid_spec"] = pltpu.PrefetchScalarGridSpec(
                num_scalar_prefetch=prefetch, grid=grid, in_specs=specs, out_specs=out_specs,
                scratch_shapes=scratch_shapes)
        else:
            kwargs.update(grid=grid, in_specs=specs, scratch_shapes=scratch_shapes)
            if out_specs is not None:
                kwargs["out_specs"] = out_specs
        out = pl.pallas_call(inner, **kwargs)(*args)
        _ORDER[0] = out[0] if isinstance(out, (list, tuple)) else out
        return out

    return run


def _tile(n, cands):
    for c in cands:
        if n % c == 0:
            return c
    raise ValueError(f"no tile for {n}")


NN_DIMS = (((1,), (0,)), ((), ()))
NT_DIMS = (((1,), (1,)), ((), ()))
TN_DIMS = (((0,), (0,)), ((), ()))
M_TILES = (1024, 512, 384, 256, 128)
N_TILES = (1408, 1024, 768, 512, 384, 256, 128)
MM_BLOCK_BYTES = 36 * 1024 * 1024


def _fit(m, n, block_bytes, m_tiles=M_TILES, n_tiles=N_TILES, n_first=False):
    tms, tns = [c for c in m_tiles if m % c == 0], [c for c in n_tiles if n % c == 0]
    pairs = [(tm, tn) for tn in tns for tm in tms] if n_first else [(tm, tn) for tm in tms for tn in tns]
    for tm, tn in pairs:
        if 2 * block_bytes(tm, tn) + 4 * tm * tn <= MM_BLOCK_BYTES:
            return tm, tn
    raise ValueError(f"no tiles for {m} x {n}")


def _size(x):
    return x.dtype.itemsize


def _mm(name, a, b, dims, grid, a_spec, b_spec, o_spec, o_sds, add=None, red=None, acc_shape=None):
    n_red = None if red is None else grid[red]

    def body(*refs):
        a_ref, b_ref = refs[0], refs[1]
        add_ref = refs[2] if add is not None else None
        o_ref = refs[3] if add is not None else refs[2]
        part = lax.dot_general(a_ref[...].astype(BF16), b_ref[...].astype(BF16), dims, preferred_element_type=F32)
        if red is None:
            if add is not None:
                part = part + add_ref[...]
            o_ref[...] = part.astype(o_ref.dtype)
            return
        acc_ref = refs[-1]
        r = pl.program_id(red)

        @pl.when(r == 0)
        def _():
            acc_ref[...] = part

        @pl.when(r > 0)
        def _():
            acc_ref[...] += part

        @pl.when(r == n_red - 1)
        def _():
            o_ref[...] = acc_ref[...].astype(o_ref.dtype)

    sem = tuple("arbitrary" if ax == red else "parallel" for ax in range(len(grid)))
    in_specs = [a_spec, b_spec] + ([o_spec] if add is not None else [])
    args = (a, b) + ((add,) if add is not None else ())
    return _tc_call(
        body, name=name, grid=grid, in_specs=in_specs, out_specs=o_spec, out_shape=o_sds,
        scratch_shapes=[] if red is None else [pltpu.VMEM(acc_shape, F32)], compiler_params=_cp(*sem),
    )(*args)


def _nn(name, a, b, out_dtype, add=None, lead=None):
    (m, k), n = a.shape, b.shape[-1]
    osz = jnp.dtype(out_dtype).itemsize + (4 if add is not None else 0)
    tm, tn = _fit(m, n, lambda tm, tn: tm * k * _size(a) + k * tn * _size(b) + tm * tn * osz, n_first=True)
    if lead is None:
        b_spec = pl.BlockSpec((k, tn), lambda i, j: (0, j))
    else:
        b_spec = pl.BlockSpec((None, k, tn), lambda i, j: (lead, 0, j))
    return _mm(name, a, b, NN_DIMS, (m // tm, n // tn), pl.BlockSpec((tm, k), lambda i, j: (i, 0)), b_spec,
               pl.BlockSpec((tm, tn), lambda i, j: (i, j)), jax.ShapeDtypeStruct((m, n), out_dtype), add=add)


def _nn_parts(name, a, b, parts, out_dtype, lead=None, stacked=False):
    m, k = a.shape
    c = b.shape[-1] if stacked else b.shape[-1] // parts
    osz = jnp.dtype(out_dtype).itemsize
    tm, tn = _fit(m, c, lambda tm, tn: tm * k * _size(a) + k * tn * _size(b) + tm * tn * osz)
    nb = c // tn
    if stacked:
        b_spec = pl.BlockSpec((None, k, tn), lambda i, p, j: (p, 0, j))
    elif lead is None:
        b_spec = pl.BlockSpec((k, tn), lambda i, p, j: (0, p * nb + j))
    else:
        b_spec = pl.BlockSpec((None, k, tn), lambda i, p, j: (lead, 0, p * nb + j))
    return _mm(name, a, b, NN_DIMS, (m // tm, parts, nb), pl.BlockSpec((tm, k), lambda i, p, j: (i, 0)), b_spec,
               pl.BlockSpec((None, tm, tn), lambda i, p, j: (p, i, j)), jax.ShapeDtypeStruct((parts, m, c), out_dtype))


def _nt(name, a, b, out_dtype, lead=None):
    (m, k), n = a.shape, b.shape[-2]
    osz = jnp.dtype(out_dtype).itemsize
    tm, tn = _fit(m, n, lambda tm, tn: tm * k * _size(a) + tn * k * _size(b) + tm * tn * osz)
    if lead is None:
        b_spec = pl.BlockSpec((tn, k), lambda i, j: (j, 0))
    else:
        b_spec = pl.BlockSpec((None, tn, k), lambda i, j: (lead, j, 0))
    return _mm(name, a, b, NT_DIMS, (m // tm, n // tn), pl.BlockSpec((tm, k), lambda i, j: (i, 0)), b_spec,
               pl.BlockSpec((tm, tn), lambda i, j: (i, j)), jax.ShapeDtypeStruct((m, n), out_dtype))


def _tn(name, a, b, out_dtype):
    (k, m), n = a.shape, b.shape[1]
    osz = jnp.dtype(out_dtype).itemsize
    tm, tn = _fit(m, n, lambda tm, tn: k * tm * _size(a) + k * tn * _size(b) + tm * tn * osz,
                  m_tiles=(1408, 512, 384, 256, 128), n_tiles=(n,) + N_TILES)
    return _mm(name, a, b, TN_DIMS, (m // tm, n // tn), pl.BlockSpec((k, tm), lambda i, j: (0, i)),
               pl.BlockSpec((k, tn), lambda i, j: (0, j)), pl.BlockSpec((tm, tn), lambda i, j: (i, j)),
               jax.ShapeDtypeStruct((m, n), out_dtype))


def _nn_add_norm(name, a, b, add, g):
    (m, k), n = a.shape, b.shape[1]
    tm = 512

    def body(a_ref, b_ref, add_ref, g_ref, h_ref, hn_ref):
        h = jnp.dot(a_ref[...], b_ref[...], preferred_element_type=F32) + add_ref[...]
        h_ref[...] = h
        hn_ref[...] = _rms_rows(h, g_ref[...]).astype(BF16)

    rows = lambda w: pl.BlockSpec((tm, w), lambda i: (i, 0))
    return _tc_call(
        body, name=name, grid=(m // tm,),
        in_specs=[rows(k), pl.BlockSpec((k, n), lambda i: (0, 0)), rows(n), pl.BlockSpec((1, n), lambda i: (0, 0))],
        out_specs=[rows(n), rows(n)],
        out_shape=[jax.ShapeDtypeStruct((m, n), F32), jax.ShapeDtypeStruct((m, n), BF16)], compiler_params=_cp("parallel"),
    )(a, b, add, g)


def _nn_add_loss(name, a, b, add, g, tgt):
    (m, k), n = a.shape, b.shape[1]
    tm = 512

    def body(a_ref, b_ref, add_ref, g_ref, t_ref, loss_ref, dh_ref, dhb_ref, dg_ref):
        xv = jnp.dot(a_ref[...], b_ref[...], preferred_element_type=F32) + add_ref[...]
        gv = g_ref[...]
        r = lax.rsqrt(jnp.mean(xv * xv, axis=1, keepdims=True) + EPS)
        err = xv * r * gv - t_ref[...]
        part = 0.5 * jnp.sum(jnp.mean(err * err, axis=1, keepdims=True), axis=0, keepdims=True)
        dx, dg = _rms_bwd_math(xv, gv, err * (1.0 / n))
        dh_ref[...] = dx
        dhb_ref[...] = dx.astype(BF16)

        @pl.when(pl.program_id(0) == 0)
        def _():
            dg_ref[...] = jnp.zeros_like(dg_ref)
            loss_ref[...] = jnp.zeros_like(loss_ref)

        dg_ref[...] += dg
        loss_ref[...] += jnp.broadcast_to(part, loss_ref.shape)

    rows = lambda w: pl.BlockSpec((tm, w), lambda i: (i, 0))
    vec = pl.BlockSpec((1, n), lambda i: (0, 0))
    return _tc_call(
        body, name=name, grid=(m // tm,),
        in_specs=[rows(k), pl.BlockSpec((k, n), lambda i: (0, 0)), rows(n), vec, rows(n)],
        out_specs=[pl.BlockSpec((1, LANES), lambda i: (0, 0)), rows(n), rows(n), vec],
        out_shape=[jax.ShapeDtypeStruct((1, LANES), F32), jax.ShapeDtypeStruct((m, n), F32),
                   jax.ShapeDtypeStruct((m, n), BF16), jax.ShapeDtypeStruct((1, n), F32)],
        compiler_params=_cp("arbitrary"),
    )(a, b, add, g, tgt)


def _dx_norm_bwd(name, a, b, x, g, add):
    parts, t, c = a.shape
    d = b.shape[0]
    tm = 512

    def body(a_ref, b_ref, x_ref, g_ref, add_ref, dx_ref, dxb_ref, dg_ref):
        dy = None
        for p in range(parts):
            part = lax.dot_general(a_ref[p], b_ref[:, p * c:(p + 1) * c], NT_DIMS, preferred_element_type=F32)
            dy = part if dy is None else dy + part
        dx, dg = _rms_bwd_math(x_ref[...], g_ref[...], dy)
        dx = dx + add_ref[...]
        dx_ref[...] = dx
        dxb_ref[...] = dx.astype(BF16)

        @pl.when(pl.program_id(0) == 0)
        def _():
            dg_ref[...] = jnp.zeros_like(dg_ref)

        dg_ref[...] += dg

    rows = pl.BlockSpec((tm, d), lambda i: (i, 0))
    vec = pl.BlockSpec((1, d), lambda i: (0, 0))
    return _tc_call(
        body, name=name, grid=(t // tm,),
        in_specs=[pl.BlockSpec((parts, tm, c), lambda i: (0, i, 0)), pl.BlockSpec(b.shape, lambda i: (0, 0)), rows, vec,
                  rows],
        out_specs=[rows, rows, vec],
        out_shape=[jax.ShapeDtypeStruct((t, d), F32), jax.ShapeDtypeStruct((t, d), BF16),
                   jax.ShapeDtypeStruct((1, d), F32)],
        compiler_params=_cp("arbitrary"),
    )(a, b, x, g, add)


def _dw_sc_in(hn, dz):
    t, tn, tm = hn.shape[0], TC, D
    per_part, per_chip = D // tn, 3 * D // N_CHIPS // tn
    return _mm("sc_in_dw", hn, dz, TN_DIMS, (D // tm, 3 * D // tn), pl.BlockSpec((t, tm), lambda i, j: (0, i)),
               pl.BlockSpec((None, t, tn), lambda i, j: (j // per_part, 0, j % per_part)),
               pl.BlockSpec((None, tm, tn), lambda i, j: (j // per_chip, i, j % per_chip)),
               jax.ShapeDtypeStruct((N_CHIPS, D, 3 * D // N_CHIPS), BF16))


def _dw_ffn_up(name, hf, dup):
    t, tm, ns = hf.shape[0], D, 2 * F_FF // N_CHIPS
    return _mm(name, hf, dup, TN_DIMS, (N_CHIPS, D // tm), pl.BlockSpec((t, tm), lambda s, i: (0, i)),
               pl.BlockSpec((None, t, ns), lambda s, i: (s // 2, 0, s % 2)),
               pl.BlockSpec((None, tm, ns), lambda s, i: (s, i, 0)), jax.ShapeDtypeStruct((N_CHIPS, D, ns), BF16))


def _rms_fwd(x, g, name):
    t, d = x.shape
    tr = 512

    def body(x_ref, g_ref, o_ref):
        xv = x_ref[...]
        r = lax.rsqrt(jnp.mean(xv * xv, axis=1, keepdims=True) + EPS)
        o_ref[...] = (xv * r * g_ref[...]).astype(o_ref.dtype)

    row = pl.BlockSpec((tr, d), lambda i: (i, 0))
    return _tc_call(
        body, name=name, grid=(t // tr,), in_specs=[row, pl.BlockSpec((1, d), lambda i: (0, 0))],
        out_specs=row, out_shape=jax.ShapeDtypeStruct((t, d), BF16), compiler_params=_cp("parallel"),
    )(x, g)


def _rms_bwd_math(xv, g, dy):
    r = lax.rsqrt(jnp.mean(xv * xv, axis=1, keepdims=True) + EPS)
    xh = xv * r
    gy = dy * g
    dx = r * (gy - xh * jnp.mean(gy * xh, axis=1, keepdims=True))
    dg = jnp.sum(dy * xh, axis=0, keepdims=True)
    return dx, dg


def _rot_half(x):
    lane = lax.broadcasted_iota(jnp.int32, x.shape, 1)
    return jnp.where((lane % QK_ROPE) < QK_ROPE // 2, -pltpu.roll(x, LANES - 32, axis=1),
                     pltpu.roll(x, 32, axis=1))


def _rope_fwd_math(x, cos, sin):
    return x * cos + _rot_half(x) * sin


def _rope_bwd_math(dy, cos, sin):
    return dy * cos - _rot_half(dy * sin)


def _rms_rows(x, g):
    return x * lax.rsqrt(jnp.mean(x * x, axis=1, keepdims=True) + EPS) * g


def _attn_prep(h, g_attn, g_kvin, w_dq, g_ql, w_uq, w_kv, g_kvl, w_ukv, cos, sin):
    t, d = h.shape
    tr = 256
    wq = N_HEADS * HEAD_PAD

    def body(h_ref, ga_ref, gk_ref, wdq_ref, gq_ref, wuq_ref, wkv_ref, gl_ref, wukv_ref, c_ref, s_ref,
             hn_ref, hk_ref, cqp_ref, cq_ref, q_ref, kvp_ref, ckv_ref, kr_ref, knv_ref):
        xv, cv, sv = h_ref[...], c_ref[...], s_ref[...]
        xh = xv * lax.rsqrt(jnp.mean(xv * xv, axis=1, keepdims=True) + EPS)
        hn = (xh * ga_ref[...]).astype(BF16)
        hk = (xh * gk_ref[...]).astype(BF16)
        hn_ref[...], hk_ref[...] = hn, hk
        cq_pre = jnp.dot(hn, wdq_ref[...], preferred_element_type=F32)
        cqp_ref[...] = cq_pre
        cq = _rms_rows(cq_pre, gq_ref[...]).astype(BF16)
        cq_ref[...] = cq
        for hd in range(N_HEADS):
            lo = hd * HEAD_PAD
            qh = jnp.dot(cq, wuq_ref[:, lo:lo + HEAD_PAD], preferred_element_type=F32)
            q_ref[:, lo:lo + QK_NOPE] = qh[:, :QK_NOPE].astype(BF16)
            q_ref[:, lo + QK_NOPE:lo + HEAD_PAD] = _rope_fwd_math(qh[:, QK_NOPE:], cv, sv).astype(BF16)
        kvpre = jnp.dot(hk, wkv_ref[...], preferred_element_type=F32)
        kvp_ref[...] = kvpre
        ckv = _rms_rows(kvpre[:, :KV_LORA], gl_ref[...]).astype(BF16)
        ckv_ref[...] = ckv
        kr_ref[...] = _rope_fwd_math(kvpre[:, KV_LORA:], cv, sv).astype(BF16)
        for p in range(2):
            knv_ref[p] = jnp.dot(ckv, wukv_ref[p], preferred_element_type=F32).astype(BF16)

    rows = lambda w: pl.BlockSpec((tr, w), lambda i: (i, 0))
    whole = lambda a: pl.BlockSpec(a.shape, lambda i: (0,) * a.ndim)
    sds = lambda w, dt: jax.ShapeDtypeStruct((t, w), dt)
    args = (h, g_attn, g_kvin, w_dq, g_ql, w_uq, w_kv, g_kvl, w_ukv, cos, sin)
    return _tc_call(
        body, name="attn_prep", grid=(t // tr,),
        in_specs=[rows(d)] + [whole(a) for a in args[1:9]] + [rows(LANES), rows(LANES)],
        out_specs=[rows(d), rows(d), rows(Q_LORA), rows(Q_LORA), rows(wq), rows(KVP), rows(KV_LORA), rows(LANES),
                   pl.BlockSpec((2, tr, N_HEADS * QK_NOPE), lambda i: (0, i, 0))],
        out_shape=[sds(d, BF16), sds(d, BF16), sds(Q_LORA, F32), sds(Q_LORA, BF16), sds(wq, BF16), sds(KVP, F32),
                   sds(KV_LORA, BF16), sds(LANES, BF16), jax.ShapeDtypeStruct((2, t, N_HEADS * QK_NOPE), BF16)],
        compiler_params=_cp("parallel"),
    )(*args)


def _attn_prep_bwd(dq, dknv, dkr, dh, h, hn, hk, cq_pre, cq, kvpre, ckv, g_attn, g_kvin, w_dq, g_ql, w_uq, w_kv, g_kvl,
                   w_ukv, cos, sin):
    t, d = h.shape
    tr = 256
    n_steps = t // tr
    wq = N_HEADS * HEAD_PAD
    wk = N_HEADS * QK_NOPE

    def body(dq_ref, dknv_ref, dkr_ref, dh_ref, h_ref, hn_ref, hk_ref, cqp_ref, cq_ref, kvp_ref, ckv_ref,
             ga_ref, gk_ref, wdq_ref, gq_ref, wuq_ref, wkv_ref, gl_ref, wukv_ref, c_ref, s_ref,
             dho_ref, dhb_ref, dwuq_ref, dwdq_ref, dwukv_ref, dwkv_ref, dga_ref, dgk_ref, dgq_ref, dgl_ref,
             a_uq, a_dq, a_ukv, a_kv):
        i = pl.program_id(0)

        @pl.when(i == 0)
        def _():
            for ref in (a_uq, a_dq, a_ukv, a_kv, dga_ref, dgk_ref, dgq_ref, dgl_ref):
                ref[...] = jnp.zeros_like(ref)

        dqv = dq_ref[...]
        dcq = lax.dot_general(dqv, wuq_ref[...], NT_DIMS, preferred_element_type=F32)
        a_uq[...] += lax.dot_general(cq_ref[...], dqv, TN_DIMS, preferred_element_type=F32)
        dcq_pre, dg = _rms_bwd_math(cqp_ref[...], gq_ref[...], dcq)
        dgq_ref[...] += dg
        dcq_pre = dcq_pre.astype(BF16)
        dhn = lax.dot_general(dcq_pre, wdq_ref[...], NT_DIMS, preferred_element_type=F32)
        a_dq[...] += lax.dot_general(hn_ref[...], dcq_pre, TN_DIMS, preferred_element_type=F32)
        dckv = None
        for p in range(2):
            dk = dknv_ref[p].astype(BF16)
            part = lax.dot_general(dk, wukv_ref[p], NT_DIMS, preferred_element_type=F32)
            dckv = part if dckv is None else dckv + part
            a_ukv[p] += lax.dot_general(ckv_ref[...], dk, TN_DIMS, preferred_element_type=F32)
        dlat, dg = _rms_bwd_math(kvp_ref[:, :KV_LORA], gl_ref[...], dckv)
        dgl_ref[...] += dg
        dkr_pre = _rope_bwd_math(dkr_ref[...], c_ref[...], s_ref[...])
        dkvpre = jnp.concatenate([dlat, dkr_pre], axis=1).astype(BF16)
        dhk = lax.dot_general(dkvpre, wkv_ref[...], NT_DIMS, preferred_element_type=F32)
        a_kv[...] += lax.dot_general(hk_ref[...], dkvpre, TN_DIMS, preferred_element_type=F32)
        xv = h_ref[...]
        dx1, dg = _rms_bwd_math(xv, ga_ref[...], dhn)
        dga_ref[...] += dg
        dx2, dg = _rms_bwd_math(xv, gk_ref[...], dhk)
        dgk_ref[...] += dg
        dh_new = dh_ref[...] + dx1 + dx2
        dho_ref[...] = dh_new
        dhb_ref[...] = dh_new.astype(BF16)

        @pl.when(i == n_steps - 1)
        def _():
            dwuq_ref[...] = a_uq[...].astype(BF16)
            dwdq_ref[...] = a_dq[...].astype(BF16)
            dwukv_ref[...] = a_ukv[...].astype(BF16)
            dwkv_ref[...] = a_kv[...].astype(BF16)

    rows = lambda w: pl.BlockSpec((tr, w), lambda i: (i, 0))
    whole = lambda shape: pl.BlockSpec(shape, lambda i: (0,) * len(shape))
    weights = (g_attn, g_kvin, w_dq, g_ql, w_uq, w_kv, g_kvl, w_ukv)
    dw_shapes = [(Q_LORA, wq), (d, Q_LORA), (2, KV_LORA, wk), (d, KVP)]
    dg_shapes = [(1, d), (1, d), (1, Q_LORA), (1, KV_LORA)]
    return _tc_call(
        body, name="attn_prep_bwd", grid=(n_steps,),
        in_specs=[rows(wq), pl.BlockSpec((2, tr, wk), lambda i: (0, i, 0)), rows(LANES), rows(d), rows(d), rows(d),
                  rows(d), rows(Q_LORA), rows(Q_LORA), rows(KVP), rows(KV_LORA)]
        + [whole(a.shape) for a in weights] + [rows(LANES), rows(LANES)],
        out_specs=[rows(d), rows(d)] + [whole(s) for s in dw_shapes + dg_shapes],
        out_shape=[jax.ShapeDtypeStruct((t, d), F32), jax.ShapeDtypeStruct((t, d), BF16)]
        + [jax.ShapeDtypeStruct(s, BF16) for s in dw_shapes] + [jax.ShapeDtypeStruct(s, F32) for s in dg_shapes],
        scratch_shapes=[pltpu.VMEM(s, F32) for s in dw_shapes], compiler_params=_cp("arbitrary"),
    )(dq, dknv, dkr, dh, h, hn, hk, cq_pre, cq, kvpre, ckv, *weights, cos, sin)


ROW_CHUNK = 64
HALO = 16
WIN = ROW_CHUNK + 16
LANE_HALVES = (slice(0, LANES), slice(LANES, TC))


def _stage(s_ref, p, src):
    t = src.shape[0]
    s_ref[p, :HALO] = jnp.zeros((HALO, TC), BF16)
    s_ref[p, HALO:HALO + t] = src
    s_ref[p, HALO + t:] = jnp.zeros((HALO, TC), BF16)


def _window(s_ref, p, i, lanes):
    base = pl.multiple_of(i * ROW_CHUNK, ROW_CHUNK)
    return s_ref[p, pl.ds(base, ROW_CHUNK + 2 * HALO), lanes].astype(F32)[8:8 + WIN]


def _valid(x):
    return x[8:8 + ROW_CHUNK]


def _prev(x, k):
    return pltpu.roll(x, k, axis=0)


def _next(x, k):
    return pltpu.roll(x, WIN - k, axis=0)


def _taps(w_ref, lanes):
    return w_ref[0:1, lanes], w_ref[1:2, lanes], w_ref[2:3, lanes]


def _fold8(x):
    return jnp.sum(x.reshape(ROW_CHUNK // 8, 8, x.shape[-1]), axis=0)


def _store_rows(ref, idx, i, lanes, x):
    rows = pl.ds(pl.multiple_of(i * ROW_CHUNK, ROW_CHUNK), ROW_CHUNK)
    ref[(*idx, rows, lanes)] = x.astype(ref.dtype)


def _for_chunks(t, chunk):
    def step(i, carry):
        for lanes in LANE_HALVES:
            chunk(i, lanes)
        return carry

    lax.fori_loop(0, t // ROW_CHUNK, step, 0)


def _write_col_sums(acc_ref, outs):
    for k, (ref, row) in enumerate(outs):
        ref[row:row + 1, :] = jnp.sum(acc_ref[k], axis=0, keepdims=True)


def _shift_down(x, k):
    row = lax.broadcasted_iota(jnp.int32, x.shape, 0)
    return jnp.where(row >= k, pltpu.roll(x, k, axis=0), 0.0)


def _shift_up(x, k):
    n = x.shape[0]
    row = lax.broadcasted_iota(jnp.int32, x.shape, 0)
    return jnp.where(row < n - k, pltpu.roll(x, n - k, axis=0), 0.0)


def _conv3(x, w_ref):
    return _shift_down(x, 2) * w_ref[0:1, :] + _shift_down(x, 1) * w_ref[1:2, :] + x * w_ref[2:3, :]


def _col(parts, t):
    if parts is None:
        return pl.BlockSpec((t, TC), lambda j: (0, j))
    return pl.BlockSpec((parts, t, TC), lambda j: (0, 0, j))


def _staging(parts, t):
    return pltpu.VMEM((parts, t + 2 * HALO, TC), BF16)


def _scmix_fwd(z, w):
    t = z.shape[1]

    def body(z_ref, w_ref, m_ref):
        b, c, u = (z_ref[p].astype(F32) for p in range(3))
        m_ref[...] = (b * _conv3(c * u, w_ref)).astype(BF16)

    return _tc_call(
        body, name="scmix_fwd", grid=(D // TC,), in_specs=[_col(3, t), pl.BlockSpec((3, TC), lambda j: (0, j))],
        out_specs=_col(None, t), out_shape=jax.ShapeDtypeStruct((t, D), BF16), compiler_params=_cp("parallel"),
    )(z, w)


def _scmix_bwd(z, w, dm):
    t = z.shape[1]

    def body(z_ref, w_ref, dm_ref, dz_ref, dw_ref, s_ref, acc_ref):
        for p in range(3):
            _stage(s_ref, p, z_ref[p])
        _stage(s_ref, 3, dm_ref[...])
        acc_ref[...] = jnp.zeros_like(acc_ref)

        def chunk(i, lanes):
            w0, w1, w2 = _taps(w_ref, lanes)
            b, c, u, dm = (_window(s_ref, p, i, lanes) for p in range(4))
            cu = c * u
            cu1, cu2 = _prev(cu, 1), _prev(cu, 2)
            _store_rows(dz_ref, (0,), i, lanes, _valid(dm * (cu2 * w0 + cu1 * w1 + cu * w2)))
            dcv = dm * b
            dcu = dcv * w2 + _next(dcv, 1) * w1 + _next(dcv, 2) * w0
            _store_rows(dz_ref, (1,), i, lanes, _valid(dcu * u))
            _store_rows(dz_ref, (2,), i, lanes, _valid(dcu * c))
            for k, shifted in enumerate((cu2, cu1, cu)):
                acc_ref[k, :, lanes] += _fold8(_valid(dcv * shifted))

        _for_chunks(t, chunk)
        _write_col_sums(acc_ref, [(dw_ref, 0), (dw_ref, 1), (dw_ref, 2)])

    wspec = pl.BlockSpec((3, TC), lambda j: (0, j))
    return _tc_call(
        body, name="scmix_bwd", grid=(D // TC,), in_specs=[_col(3, t), wspec, _col(None, t)],
        out_specs=[_col(3, t), wspec],
        out_shape=[jax.ShapeDtypeStruct((3, t, D), BF16), jax.ShapeDtypeStruct((3, D), F32)],
        scratch_shapes=[_staging(4, t), pltpu.VMEM((3, 8, TC), F32)], compiler_params=_cp("parallel"),
    )(z, w, dm)


def _ffn_up_gate(hf, w_up, w, bias, name):
    t, d = hf.shape
    nb = F_FF // TC

    def body(hf_ref, wg_ref, wv_ref, w_ref, b_ref, up_ref, a_ref, prev_ref):
        @pl.when(pl.program_id(0) == 0)
        def _():
            prev_ref[...] = jnp.zeros_like(prev_ref)

        gc = _conv3(prev_ref[0].astype(F32), w_ref) + b_ref[...]
        a_ref[...] = (gc * jax.nn.sigmoid(gc) * prev_ref[1].astype(F32)).astype(BF16)
        hv = hf_ref[...]
        up_ref[0] = jnp.dot(hv, wg_ref[...], preferred_element_type=F32).astype(BF16)
        up_ref[1] = jnp.dot(hv, wv_ref[...], preferred_element_type=F32).astype(BF16)
        prev_ref[...] = up_ref[...]

    tile = lambda j: jnp.minimum(j, nb - 1)
    gated = lambda j: jnp.maximum(j - 1, 0)
    return _tc_call(
        body, name=name, grid=(nb + 1,),
        in_specs=[pl.BlockSpec((t, d), lambda j: (0, 0)), pl.BlockSpec((d, TC), lambda j: (0, tile(j))),
                  pl.BlockSpec((d, TC), lambda j: (0, nb + tile(j))), pl.BlockSpec((3, TC), lambda j: (0, gated(j))),
                  pl.BlockSpec((1, TC), lambda j: (0, gated(j)))],
        out_specs=[pl.BlockSpec((2, t, TC), lambda j: (0, 0, tile(j))), pl.BlockSpec((t, TC), lambda j: (0, gated(j)))],
        out_shape=[jax.ShapeDtypeStruct((2, t, F_FF), BF16), jax.ShapeDtypeStruct((t, F_FF), BF16)],
        scratch_shapes=[pltpu.VMEM((2, t, TC), BF16)], compiler_params=_cp("arbitrary"),
    )(hf, w_up, w_up, w, bias)


def _gate_bwd(up, w, bias, dh, w_down, name):
    t, d = dh.shape

    def body(u_ref, w_ref, b_ref, dh_ref, wd_ref, du_ref, dw_ref, db_ref, s_ref, acc_ref):
        for p in range(2):
            _stage(s_ref, p, u_ref[p])
        _stage(s_ref, 2, lax.dot_general(dh_ref[...], wd_ref[...], NT_DIMS, preferred_element_type=F32).astype(BF16))
        acc_ref[...] = jnp.zeros_like(acc_ref)

        def chunk(i, lanes):
            w0, w1, w2 = _taps(w_ref, lanes)
            g, v, da = (_window(s_ref, p, i, lanes) for p in range(3))
            g1, g2 = _prev(g, 1), _prev(g, 2)
            gc = g2 * w0 + g1 * w1 + g * w2 + b_ref[:, lanes]
            sg = jax.nn.sigmoid(gc)
            _store_rows(du_ref, (1,), i, lanes, _valid(da * (gc * sg)))
            dgc = da * v * (sg * (1.0 + gc * (1.0 - sg)))
            _store_rows(du_ref, (0,), i, lanes, _valid(dgc * w2 + _next(dgc, 1) * w1 + _next(dgc, 2) * w0))
            for k, shifted in enumerate((g2, g1, g)):
                acc_ref[k, :, lanes] += _fold8(_valid(dgc * shifted))
            acc_ref[3, :, lanes] += _fold8(_valid(dgc))

        _for_chunks(t, chunk)
        _write_col_sums(acc_ref, [(dw_ref, 0), (dw_ref, 1), (dw_ref, 2), (db_ref, 0)])

    wspec = pl.BlockSpec((3, TC), lambda j: (0, j))
    bspec = pl.BlockSpec((1, TC), lambda j: (0, j))
    return _tc_call(
        body, name=name, grid=(F_FF // TC,),
        in_specs=[_col(2, t), wspec, bspec, pl.BlockSpec((t, d), lambda j: (0, 0)), pl.BlockSpec((TC, d), lambda j: (j, 0))],
        out_specs=[_col(2, t), wspec, bspec],
        out_shape=[jax.ShapeDtypeStruct((2, t, F_FF), BF16), jax.ShapeDtypeStruct((3, F_FF), F32),
                   jax.ShapeDtypeStruct((1, F_FF), F32)],
        scratch_shapes=[_staging(3, t), pltpu.VMEM((4, 8, TC), F32)], compiler_params=_cp("parallel"),
    )(up, w, bias, dh, w_down)


ATT_TQ = 256
ATT_SCALE = (QK_NOPE + QK_ROPE) ** -0.5


def _key_ranges(lvl):
    lo = lvl * ATT_TQ
    return ([(0, lo, False)] if lvl else []) + [(lo, lo + ATT_TQ, True)]


FWD_HEADS = 4
BWD_HEADS = 2


def _fill_keys(k_ref, kn_ref, kr_ref):
    @pl.when(pl.program_id(1) == 0)
    def _():
        for hh in range(k_ref.shape[0]):
            k_ref[hh, :, :QK_NOPE] = kn_ref[:, hh * QK_NOPE:(hh + 1) * QK_NOPE]
            k_ref[hh, :, QK_NOPE:] = kr_ref[...]


def _attn_probs(q, k_ref, lvl):
    scores = []
    for lo, hi, diagonal in _key_ranges(lvl):
        s = lax.dot_general(q, k_ref[lo:hi, :], NT_DIMS, preferred_element_type=F32) * ATT_SCALE
        if diagonal:
            row = lax.broadcasted_iota(jnp.int32, s.shape, 0)
            col = lax.broadcasted_iota(jnp.int32, s.shape, 1)
            seen = lax.shift_right_logical(col, CHUNK_SHIFT) <= lax.shift_right_logical(row, CHUNK_SHIFT)
            s = jnp.where(seen, s, NEG_INF)
        scores.append(s)
    m = jnp.max(scores[0], axis=1, keepdims=True)
    for s in scores[1:]:
        m = jnp.maximum(m, jnp.max(s, axis=1, keepdims=True))
    ps = [jnp.exp(s - m) for s in scores]
    total = jnp.sum(ps[0], axis=1, keepdims=True)
    for p in ps[1:]:
        total = total + jnp.sum(p, axis=1, keepdims=True)
    inv = 1.0 / total
    return [p * inv for p in ps]


def _attn_probs_t(q, k_ref, lvl):
    scores = []
    for lo, hi, diagonal in _key_ranges(lvl):
        s = lax.dot_general(k_ref[lo:hi, :], q, NT_DIMS, preferred_element_type=F32) * ATT_SCALE
        if diagonal:
            key = lax.broadcasted_iota(jnp.int32, s.shape, 0)
            qry = lax.broadcasted_iota(jnp.int32, s.shape, 1)
            seen = lax.shift_right_logical(key, CHUNK_SHIFT) <= lax.shift_right_logical(qry, CHUNK_SHIFT)
            s = jnp.where(seen, s, NEG_INF)
        scores.append(s)
    m = jnp.max(scores[0], axis=0, keepdims=True)
    for s in scores[1:]:
        m = jnp.maximum(m, jnp.max(s, axis=0, keepdims=True))
    ps = [jnp.exp(s - m) for s in scores]
    total = jnp.sum(ps[0], axis=0, keepdims=True)
    for p in ps[1:]:
        total = total + jnp.sum(p, axis=0, keepdims=True)
    inv = 1.0 / total
    return [p * inv for p in ps]


def _per_query_block(qi, n_blocks, branch):
    for lvl in range(n_blocks):
        pl.when(qi == lvl)(lambda lvl=lvl: branch(lvl))


def _attn_specs(t, g):
    q = pl.BlockSpec((ATT_TQ, g * HEAD_PAD), lambda h, i: (i, h))
    kn = pl.BlockSpec((None, t, g * QK_NOPE), lambda h, i: (0, 0, h))
    kr = pl.BlockSpec((t, LANES), lambda h, i: (0, 0))
    v = pl.BlockSpec((None, t, g * V_HEAD), lambda h, i: (1, 0, h))
    o = pl.BlockSpec((ATT_TQ, g * V_HEAD), lambda h, i: (i, h))
    return q, kn, kr, v, o


def _attn_fwd(q, knv, kr):
    t = q.shape[0]

    def body(q_ref, kn_ref, kr_ref, v_ref, o_ref, k_ref):
        _fill_keys(k_ref, kn_ref, kr_ref)

        def branch(lvl):
            for hh in range(FWD_HEADS):
                vcols = slice(hh * V_HEAD, (hh + 1) * V_HEAD)
                ps = _attn_probs(q_ref[:, hh * HEAD_PAD:(hh + 1) * HEAD_PAD], k_ref.at[hh], lvl)
                o = None
                for p, (lo, hi, _) in zip(ps, _key_ranges(lvl)):
                    part = jnp.dot(p.astype(BF16), v_ref[lo:hi, vcols], preferred_element_type=F32)
                    o = part if o is None else o + part
                o_ref[:, vcols] = o.astype(BF16)

        _per_query_block(pl.program_id(1), t // ATT_TQ, branch)

    qs, kns, krs, vs, os_ = _attn_specs(t, FWD_HEADS)
    return _tc_call(
        body, name="attn_fwd", grid=(N_HEADS // FWD_HEADS, t // ATT_TQ), in_specs=[qs, kns, krs, vs],
        out_specs=os_, out_shape=jax.ShapeDtypeStruct((t, N_HEADS * V_HEAD), BF16),
        scratch_shapes=[pltpu.VMEM((FWD_HEADS, t, HEAD_PAD), BF16)], compiler_params=_cp("parallel", "arbitrary"),
    )(q, knv, kr, knv)


def _attn_bwd(q, knv, kr, do, cos, sin):
    t = q.shape[0]

    def body(q_ref, kn_ref, kr_ref, v_ref, do_ref, c_ref, s_ref, dq_ref, dknv_ref, dkr_ref, k_ref, dk_ref):
        h, qi = pl.program_id(0), pl.program_id(1)
        _fill_keys(k_ref, kn_ref, kr_ref)

        @pl.when(qi == 0)
        def _():
            dknv_ref[1] = jnp.zeros(dknv_ref.shape[1:], F32)
            dk_ref[...] = jnp.zeros_like(dk_ref)

        @pl.when((qi == 0) & (h == 0))
        def _():
            dkr_ref[...] = jnp.zeros_like(dkr_ref)

        def branch(lvl):
            ranges = _key_ranges(lvl)
            for hh in range(BWD_HEADS):
                qcols = slice(hh * HEAD_PAD, (hh + 1) * HEAD_PAD)
                vcols = slice(hh * V_HEAD, (hh + 1) * V_HEAD)
                qv, dov = q_ref[:, qcols], do_ref[:, vcols]
                ps = _attn_probs_t(qv, k_ref.at[hh], lvl)
                dps = [lax.dot_general(v_ref[lo:hi, vcols], dov, NT_DIMS, preferred_element_type=F32)
                       for lo, hi, _ in ranges]
                di = None
                for p, dp in zip(ps, dps):
                    part = jnp.sum(p * dp, axis=0, keepdims=True)
                    di = part if di is None else di + part
                dq = None
                for p, dp, (lo, hi, _) in zip(ps, dps, ranges):
                    ds = (p * (dp - di) * ATT_SCALE).astype(BF16)
                    part = lax.dot_general(ds, k_ref[hh, lo:hi, :], TN_DIMS, preferred_element_type=F32)
                    dq = part if dq is None else dq + part
                    dk_ref[hh, lo:hi, :] += jnp.dot(ds, qv, preferred_element_type=F32)
                    dknv_ref[1, lo:hi, vcols] += jnp.dot(p.astype(BF16), dov, preferred_element_type=F32)
                dq_ref[:, hh * HEAD_PAD:hh * HEAD_PAD + QK_NOPE] = dq[:, :QK_NOPE].astype(BF16)
                dq_ref[:, hh * HEAD_PAD + QK_NOPE:(hh + 1) * HEAD_PAD] = _rope_bwd_math(
                    dq[:, QK_NOPE:], c_ref[...], s_ref[...]).astype(BF16)

        _per_query_block(qi, t // ATT_TQ, branch)

        @pl.when(qi == t // ATT_TQ - 1)
        def _():
            for hh in range(BWD_HEADS):
                dknv_ref[0, :, hh * QK_NOPE:(hh + 1) * QK_NOPE] = dk_ref[hh, :, :QK_NOPE]
                dkr_ref[...] += dk_ref[hh, :, QK_NOPE:]

    qs, kns, krs, vs, os_ = _attn_specs(t, BWD_HEADS)
    tab = pl.BlockSpec((ATT_TQ, LANES), lambda h, i: (i, 0))
    return _tc_call(
        body, name="attn_bwd", grid=(N_HEADS // BWD_HEADS, t // ATT_TQ), in_specs=[qs, kns, krs, vs, os_, tab, tab],
        out_specs=[qs, pl.BlockSpec((2, t, BWD_HEADS * QK_NOPE), lambda h, i: (0, 0, h)), krs],
        out_shape=[jax.ShapeDtypeStruct((t, N_HEADS * HEAD_PAD), BF16),
                   jax.ShapeDtypeStruct((2, t, N_HEADS * QK_NOPE), F32), jax.ShapeDtypeStruct((t, LANES), F32)],
        scratch_shapes=[pltpu.VMEM((BWD_HEADS, t, HEAD_PAD), BF16), pltpu.VMEM((BWD_HEADS, t, HEAD_PAD), F32)],
        compiler_params=_cp("arbitrary", "arbitrary"),
    )(q, knv, kr, knv, do, cos, sin)


def _adam_math(w, g, m, v):
    nm = ADAM_B1 * m + (1.0 - ADAM_B1) * g
    nv = ADAM_B2 * v + (1.0 - ADAM_B2) * (g * g)
    m_hat = nm / (1.0 - ADAM_B1 ** ADAM_STEP)
    v_hat = nv / (1.0 - ADAM_B2 ** ADAM_STEP)
    return -ADAM_LR * (m_hat / (jnp.sqrt(v_hat) + ADAM_EPS) + ADAM_WD * w), nm, nv


def _adamw_small(ws, gs, ms, vs):
    n = len(ws)

    def body(*refs):
        for i in range(n):
            w_ref, g_ref, m_ref, v_ref = (refs[k * n + i] for k in range(4))
            go_ref, d_ref, nm_ref, nv_ref = (refs[(4 + k) * n + i] for k in range(4))
            go_ref[...] = g_ref[...]
            d_ref[...], nm_ref[...], nv_ref[...] = _adam_math(w_ref[...], g_ref[...], m_ref[...], v_ref[...])

    shapes = [jax.ShapeDtypeStruct(a.shape, F32) for a in ws]
    res = _tc_call(body, name="adamw_small", out_shape=shapes * 4)(*ws, *gs, *ms, *vs)
    return [res[k * n:(k + 1) * n] for k in range(4)]


ADAM_SPLIT = 4


def _store_without_head_padding(dst_ref, g):
    assert HEAD_PAD == 2 * LANES and 2 * (QK_NOPE + QK_ROPE) == 3 * LANES, (HEAD_PAD, QK_NOPE, QK_ROPE)
    assert g.shape[1] % (2 * HEAD_PAD) == 0, g.shape
    low = lax.broadcasted_iota(jnp.int32, (g.shape[0], LANES), 1) < LANES // 2
    for pair in range(g.shape[1] // (2 * HEAD_PAD)):
        t = [g[:, (4 * pair + k) * LANES:(4 * pair + k + 1) * LANES] for k in range(4)]
        moved = [pltpu.roll(t[k], LANES // 2, axis=1) for k in (2, 3)]
        outs = (t[0], jnp.where(low, t[1], moved[0]), jnp.where(low, moved[0], moved[1]))
        for k, o in enumerate(outs):
            dst_ref[:, (3 * pair + k) * LANES:(3 * pair + k + 1) * LANES] = o


def _adamw_shards(ids, items, name):
    n = len(items)

    def body(ids_ref, *refs):
        outs = refs[len(refs) - 4 * n:]
        for i, it in enumerate(items):
            w_ref, m_ref, v_ref, gm_ref, gs_ref = refs[5 * i:5 * i + 5]
            g_ref, d_ref, nm_ref, nv_ref = outs[4 * i:4 * i + 4]
            cols = slice(*it["gcols"]) if it.get("gcols") else slice(None)
            whose = pl.program_id(0) if it.get("owner") is None else it["owner"]
            mine = whose == ids_ref[0]

            def take(src_ref, g_ref=g_ref, cols=cols, head_padded=it.get("head_padded")):
                if head_padded:
                    _store_without_head_padding(g_ref, src_ref[...])
                else:
                    g_ref[...] = src_ref[:, cols]

            @pl.when(mine)
            def _(take=take, gm_ref=gm_ref):
                take(gm_ref)

            @pl.when(jnp.logical_not(mine))
            def _(take=take, gs_ref=gs_ref):
                take(gs_ref)

            d_ref[...], nm_ref[...], nv_ref[...] = _adam_math(w_ref[...], g_ref[...], m_ref[...], v_ref[...])

    in_specs, out_specs, out_shape, args, carried, aliases = [], [], [], [ids], [], {}
    for i, it in enumerate(items):
        w = it["w"]
        r, c = w.shape[-2:]
        tr = r // 2 // ADAM_SPLIT
        assert tr % 8 == 0, (name, w.shape)
        layer = it.get("layer")
        if layer is None:
            wspec = pl.BlockSpec((tr, c), lambda h, k, ids: (h * ADAM_SPLIT + k, 0))
        else:
            wspec = pl.BlockSpec((None, tr, c), lambda h, k, ids, layer=layer: (layer, h * ADAM_SPLIT + k, 0))
        gc = it["g_mine"].shape[1]

        def g_index(of_mine, owner=it.get("owner")):
            def index(h, k, ids):
                if owner is None:
                    half = ids[0] if of_mine else 1 - ids[0]
                    return jnp.where(h == half, k, jnp.where(h < half, 0, ADAM_SPLIT - 1)), 0
                read = (owner == ids[0]) if of_mine else (owner != ids[0])
                return jnp.where(read, h * ADAM_SPLIT + k, 0), 0
            return index

        in_specs += [wspec] * 3 + [pl.BlockSpec((tr, gc), g_index(True)), pl.BlockSpec((tr, gc), g_index(False))]
        args += [w, it["m"], it["v"], it["g_mine"], it["g_sib"]]
        out_specs += [wspec] * 4
        out_shape += [jax.ShapeDtypeStruct(w.shape, F32)] * 4
        if it.get("prev") is not None:
            for k, p in enumerate(it["prev"]):
                aliases[1 + 5 * n + len(carried)] = 4 * i + k
                carried.append(p)
    res = _tc_call(
        body, name=name, prefetch=1, grid=(2, ADAM_SPLIT), in_specs=in_specs + [ANY] * len(carried),
        out_specs=out_specs, out_shape=out_shape, input_output_aliases=aliases,
        compiler_params=_cp("parallel", "parallel"),
    )(*args, *carried)
    return [res[4 * i:4 * i + 4] for i in range(n)]


def _peer_chip(k_me, j):
    return k_me ^ jnp.where(j == 0, 2, jnp.where(j == 1, 1, 3))


def _pair_sums(ids, gs, ras, name):
    n = len(gs)

    def body(ids_ref, *refs):
        for i in range(n):
            g_ref, ra_ref, o_ref = refs[2 * i], refs[2 * i + 1], refs[2 * n + i]
            o_ref[...] = (g_ref[...].astype(F32) + ra_ref[...].astype(F32)).astype(BF16)

    in_specs, out_specs, out_shape = [], [], []
    for g in gs:
        half, c = g.shape[1] // 2, g.shape[2]
        in_specs += [pl.BlockSpec((None, half, c), lambda j, ids: (_peer_chip(ids[1], j), ids[0], 0)),
                     pl.BlockSpec((None, half, c), lambda j, ids: (_peer_chip(ids[1], j), 0, 0))]
        out_specs.append(pl.BlockSpec((None, half, c), lambda j, ids: (j, 0, 0)))
        out_shape.append(jax.ShapeDtypeStruct((3, half, c), BF16))
    return _tc_call(
        body, name=name, prefetch=1, grid=(3,), in_specs=in_specs, out_specs=out_specs, out_shape=out_shape,
        compiler_params=_cp("parallel"),
    )(ids, *[a for pair in zip(gs, ras) for a in pair])


def _chip_sums(ids, gs, ras, rbs, name):
    n = len(gs)

    def body(ids_ref, *refs):
        for i in range(n):
            g_ref, ra_ref, rb_ref, o_ref = refs[3 * i], refs[3 * i + 1], refs[3 * i + 2], refs[3 * n + i]
            acc = g_ref[...].astype(F32) + ra_ref[...].astype(F32)
            for j in range(3):
                acc = acc + rb_ref[j].astype(F32)
            o_ref[...] = acc

    in_specs, out_specs, out_shape = [], [], []
    for g in gs:
        half, c = g.shape[1] // 2, g.shape[2]
        in_specs += [pl.BlockSpec((None, half, c), lambda i, ids: (ids[1], ids[0], 0)),
                     pl.BlockSpec((None, half, c), lambda i, ids: (ids[1], 0, 0)),
                     pl.BlockSpec((3, half, c), lambda i, ids: (0, 0, 0))]
        out_specs.append(pl.BlockSpec((half, c), lambda i, ids: (0, 0)))
        out_shape.append(jax.ShapeDtypeStruct((half, c), F32))
    return _tc_call(
        body, name=name, prefetch=1, grid=(1,), in_specs=in_specs, out_specs=out_specs, out_shape=out_shape,
        compiler_params=_cp("arbitrary"),
    )(ids, *[a for trio in zip(gs, ras, rbs) for a in trio])


def _position():
    x, y, c = lax.axis_index("x"), lax.axis_index("y"), lax.axis_index("c")
    chips = [(1 - x, y), (x, 1 - y), (1 - x, 1 - y)]
    return x, y, c, chips


def _shard_half(ref, wm, h):
    if wm.kind == "tiny":
        return ref
    if wm.nl == 2:
        return ref.at[h]
    return ref.at[pl.ds(pl.multiple_of(h * (wm.k // 2), 16), wm.k // 2), :]


def _region(full, wm, s, h):
    if wm.kind == "tiny":
        return full.at[s]
    cols = pl.ds(pl.multiple_of(s * wm.n, LANES), wm.n) if wm.kind == "col" else slice(None)
    if wm.nl == 2:
        rows = pl.ds(pl.multiple_of(s * wm.k, 16), wm.k) if wm.kind == "row" else slice(None)
        return full.at[slice(None) if h is None else h, rows, cols]
    if wm.kind == "col":
        rows = slice(None) if h is None else pl.ds(pl.multiple_of(h * (wm.k // 2), 16), wm.k // 2)
    elif h is None:
        rows = pl.ds(pl.multiple_of(s * wm.k, 16), wm.k)
    else:
        rows = pl.ds(pl.multiple_of(s * wm.k + h * (wm.k // 2), 16), wm.k // 2)
    return full.at[rows, cols]


def _full_shape(wm):
    if wm.kind == "tiny":
        return (N_CHIPS, wm.k, wm.n)
    shape = (wm.k, N_CHIPS * wm.n) if wm.kind == "col" else (N_CHIPS * wm.k, wm.n)
    return shape if wm.nl == 1 else (wm.nl,) + shape


def _handshake(peers):
    barrier = pltpu.get_barrier_semaphore()
    for peer in peers:
        pl.semaphore_signal(barrier, inc=1, device_id=peer, device_id_type=MESH)
    pl.semaphore_wait(barrier, len(peers))


def _all_gather_group(gi, shards):
    wms = AG_GROUPS[gi]
    nw = len(wms)

    def body(*refs):
        sh, full = refs[:nw], refs[nw:2 * nw]
        ici_s, ici_r, pass_s, pass_r, own_s, own_r = refs[2 * nw:]
        x, y, c, _ = _position()
        me, sibling = 2 * x + y, (x, y, 1 - c)
        first, second, diagonal = (x ^ (1 - c), y ^ c), (x ^ c, y ^ (1 - c)), (1 - x, 1 - y)
        chip_id = lambda chip: 2 * chip[0] + chip[1]
        _handshake([(*first, c), (*second, c), sibling])

        def rcopy(src, dst, s_sem, r_sem, to):
            return pltpu.make_async_remote_copy(src_ref=src, dst_ref=dst, send_sem=s_sem, recv_sem=r_sem,
                                                device_id=to, device_id_type=MESH)

        started = []

        def go(cp):
            cp.start()
            started.append(cp)

        for i, wm in enumerate(wms):
            half, dst = _shard_half(sh[i], wm, c), _region(full[i], wm, me, c)
            go(rcopy(half, dst, ici_s.at[i, 0], ici_r.at[i, 0], (*first, c)))
            go(rcopy(half, dst, ici_s.at[i, 1], ici_r.at[i, 1], (*second, c)))
            go(rcopy(sh[i], _region(full[i], wm, me, None), own_s.at[i], own_r.at[i], sibling))
        for i, wm in enumerate(wms):
            got = _region(full[i], wm, chip_id(first), c)
            rcopy(got, got, ici_s.at[i, 0], ici_r.at[i, 0], sibling).wait_recv()
            go(rcopy(got, got, ici_s.at[i, 2], ici_r.at[i, 2], (*second, c)))
            if wm.kind != "tiny":
                go(rcopy(got, got, pass_s.at[i, 0], pass_r.at[i, 0], sibling))
        for i, wm in enumerate(wms):
            for j, chip in ((1, second), (2, diagonal)):
                got = _region(full[i], wm, chip_id(chip), c)
                rcopy(got, got, ici_s.at[i, j], ici_r.at[i, j], sibling).wait_recv()
                if wm.kind != "tiny":
                    go(rcopy(got, got, pass_s.at[i, j], pass_r.at[i, j], sibling))
        for i, wm in enumerate(wms):
            mine = _region(full[i], wm, me, None)
            rcopy(mine, mine, own_s.at[i], own_r.at[i], sibling).wait_recv()
            if wm.kind != "tiny":
                for j, chip in ((0, second), (1, first), (2, diagonal)):
                    got = _region(full[i], wm, chip_id(chip), 1 - c)
                    rcopy(got, got, pass_s.at[i, j], pass_r.at[i, j], sibling).wait_recv()
        for cp in started:
            cp.wait_send()

    return pl.kernel(
        body, out_type=[jax.ShapeDtypeStruct(_full_shape(wm), s.dtype) for wm, s in zip(wms, shards)],
        mesh=plsc.ScalarSubcoreMesh(axis_name="sequencer", num_cores=1), name=f"ag_group{gi}",
        scratch_types=[pltpu.SemaphoreType.DMA((nw, 3))] * 4 + [pltpu.SemaphoreType.DMA((nw,))] * 2,
        compiler_params=pltpu.CompilerParams(collective_id=gi),
    )(*shards)


def _sequencer_call(body, name, cid, out_types, scratch, args):
    return pl.kernel(
        body, out_type=out_types, mesh=plsc.ScalarSubcoreMesh(axis_name="sequencer", num_cores=1), name=name,
        scratch_types=scratch, compiler_params=pltpu.CompilerParams(collective_id=cid),
    )(*args)


def _pair_exchange(gs, tag, cid):
    n = len(gs)

    def body(*refs):
        g, out, send_sems, recv_sems = refs[:n], refs[n:2 * n], refs[2 * n], refs[2 * n + 1]
        x, y, c, _ = _position()
        _handshake([(x, y, 1 - c)])
        cps = []
        for i in range(n):
            half = g[i].shape[1] // 2
            cps.append(pltpu.make_async_remote_copy(
                src_ref=g[i].at[:, pl.ds(pl.multiple_of((1 - c) * half, 16), half), :], dst_ref=out[i],
                send_sem=send_sems.at[i], recv_sem=recv_sems.at[i], device_id=(x, y, 1 - c), device_id_type=MESH))
            cps[-1].start()
        for cp in cps:
            cp.wait()

    return _sequencer_call(
        body, f"rs_pair_exchange{tag}", cid,
        [jax.ShapeDtypeStruct((a.shape[0], a.shape[1] // 2, a.shape[2]), a.dtype) for a in gs],
        [pltpu.SemaphoreType.DMA((n,)), pltpu.SemaphoreType.DMA((n,))], gs)


def _chip_exchange(ss, tag, cid):
    n = len(ss)

    def body(*refs):
        s, out, send_sems, recv_sems = refs[:n], refs[n:2 * n], refs[2 * n], refs[2 * n + 1]
        x, y, c, chips = _position()
        _handshake([(*chip, c) for chip in chips])
        cps = []
        for i in range(n):
            for j, chip in enumerate(chips):
                cps.append(pltpu.make_async_remote_copy(
                    src_ref=s[i].at[j], dst_ref=out[i].at[j], send_sem=send_sems.at[i, j], recv_sem=recv_sems.at[i, j],
                    device_id=(*chip, c), device_id_type=MESH))
                cps[-1].start()
        for cp in cps:
            cp.wait()

    return _sequencer_call(
        body, f"rs_chip_exchange{tag}", cid, [jax.ShapeDtypeStruct(a.shape, a.dtype) for a in ss],
        [pltpu.SemaphoreType.DMA((n, 3)), pltpu.SemaphoreType.DMA((n, 3))], ss)


def _pair_swap(g8s, tag, cid):
    n = len(g8s)

    def body(*refs):
        g, out, send_sems, recv_sems = refs[:n], refs[n:2 * n], refs[2 * n], refs[2 * n + 1]
        x, y, c, _ = _position()
        _handshake([(x, y, 1 - c)])
        cps = []
        for i in range(n):
            cps.append(pltpu.make_async_remote_copy(
                src_ref=g[i], dst_ref=out[i], send_sem=send_sems.at[i], recv_sem=recv_sems.at[i],
                device_id=(x, y, 1 - c), device_id_type=MESH))
            cps[-1].start()
        for cp in cps:
            cp.wait()

    return _sequencer_call(
        body, f"rs_pair_swap{tag}", cid, [jax.ShapeDtypeStruct(a.shape, a.dtype) for a in g8s],
        [pltpu.SemaphoreType.DMA((n,)), pltpu.SemaphoreType.DMA((n,))], g8s)


def _pair_swap_now(g8s):
    n = len(g8s)

    def body(*refs):
        g, out, send_sems, recv_sems = refs[:n], refs[n:2 * n], refs[2 * n], refs[2 * n + 1]
        x, y, c, _ = _position()
        cps = []
        for i in range(n):
            cps.append(pltpu.make_async_remote_copy(
                src_ref=g[i], dst_ref=out[i], send_sem=send_sems.at[i], recv_sem=recv_sems.at[i],
                device_id=(x, y, 1 - c), device_id_type=MESH))
            cps[-1].start()
        for cp in cps:
            cp.wait()

    return _tc_call(
        body, name="rs_pair_swap_last", in_specs=[ANY] * n, out_specs=[ANY] * n,
        out_shape=[jax.ShapeDtypeStruct(a.shape, a.dtype) for a in g8s],
        scratch_shapes=[pltpu.SemaphoreType.DMA((n,)), pltpu.SemaphoreType.DMA((n,))],
    )(*g8s)


def _all_reduce_small(vecs, owner_major, name):
    n = len(vecs)
    block = lambda i, ref, chip: ref.at[chip] if owner_major[i] else ref
    out_shapes = [a.shape[1:] if owner_major[i] else a.shape for i, a in enumerate(vecs)]

    def body(*refs):
        v, o, gath = refs[:n], refs[n:2 * n], refs[2 * n:3 * n]
        send_sems, recv_sems = refs[3 * n], refs[3 * n + 1]
        x, y, c, _ = _position()
        me = 4 * x + 2 * y + c
        cps = []
        for i in range(n):
            gath[i][me] = block(i, v[i], 2 * x + y)[...]
            for rel in range(1, N_DEV):
                px, py, pc = x ^ (rel >> 2), y ^ ((rel >> 1) & 1), c ^ (rel & 1)
                cps.append(pltpu.make_async_remote_copy(
                    src_ref=block(i, v[i], 2 * px + py), dst_ref=gath[i].at[me], send_sem=send_sems.at[i, rel - 1],
                    recv_sem=recv_sems.at[i, rel - 1], device_id=(px, py, pc), device_id_type=MESH))
        for cp in cps:
            cp.start()
        for i in range(n):
            for rel in range(1, N_DEV):
                pltpu.make_async_remote_copy(
                    src_ref=block(i, v[i], 2 * x + y), dst_ref=gath[i].at[me ^ rel],
                    send_sem=send_sems.at[i, rel - 1], recv_sem=recv_sems.at[i, rel - 1], device_id=(x, y, c),
                    device_id_type=MESH).wait_recv()
        for cp in cps:
            cp.wait_send()
        for i in range(n):
            acc = gath[i][0]
            for d in range(1, N_DEV):
                acc = acc + gath[i][d]
            o[i][...] = acc

    vm = pl.BlockSpec(memory_space=pltpu.VMEM)
    return _tc_call(
        body, name=name, in_specs=[vm] * n, out_specs=[vm] * n,
        out_shape=[jax.ShapeDtypeStruct(s, F32) for s in out_shapes],
        scratch_shapes=[pltpu.VMEM((N_DEV,) + s, F32) for s in out_shapes]
        + [pltpu.SemaphoreType.DMA((n, N_DEV - 1)), pltpu.SemaphoreType.DMA((n, N_DEV - 1))],
    )(*vecs)


def _rope_tables(positions):
    half = QK_ROPE // 2
    inv_freq = 1.0 / (ROPE_THETA ** (jnp.arange(half, dtype=F32) / half))
    ang = positions.astype(F32)[:, None] * inv_freq
    zeros = jnp.zeros((positions.shape[0], LANES - QK_ROPE), F32)
    cos, sin = jnp.cos(ang), jnp.sin(ang)
    return jnp.concatenate([cos, cos, zeros], axis=1), jnp.concatenate([sin, sin, zeros], axis=1)


def _local_step(x, positions, tgt, wf, small, rs):
    cos, sin = _rope_tables(positions)
    w_in, w_out = wf["sc_w_in"], wf["sc_w_out"]
    w_ups, w_downs = (wf["ffn_w_up0"], wf["ffn_w_up1"]), (wf["ffn_w_down0"], wf["ffn_w_down1"])
    w_kv, w_ukv, w_dq, w_uq, w_o = wf["w_kv"], wf["w_ukv"], wf["w_dq"], wf["w_uq"], wf["w_o"]
    attn_norm, ffn_norm = small["attn_norm"], small["ffn_norm"]
    conv_b = small["ffn_conv_b"]

    def ffn_fwd(h, hf, l, then):
        up, a = _ffn_up_gate(hf, w_ups[l], small["ffn_conv_w"][l], conv_b[l:l + 1], f"ffn{l}_up_gate")
        return then(a, w_downs[l], h), (hf, up, a)

    def ffn_bwd(h, dh_out, dh_out_b, l, saved, gi, hooks):
        run = lambda stage: hooks.get(stage, lambda: None)()
        hf, up, a = saved
        d_down = _tn(f"ffn{l}_down_dw", a, dh_out_b, BF16)
        run("down_dw")
        dup, d_cw, d_cb = _gate_bwd(up, small["ffn_conv_w"][l], conv_b[l:l + 1], dh_out_b, w_downs[l],
                                    f"ffn{l}_gate_bwd")
        run("gate_bwd")
        d_up = _dw_ffn_up(f"ffn{l}_up_dw", hf, dup)
        rs.start(gi, {f"ffn_w_down{l}": d_down.reshape(N_CHIPS, F_FF // N_CHIPS, D), f"ffn_w_up{l}": d_up})
        run("up_dw")
        dh, dh_b, d_norm = _dx_norm_bwd(f"ffn{l}_up_dx", dup, w_ups[l], h, ffn_norm[l:l + 1], dh_out)
        run("up_dx")
        return dh, dh_b, d_cw, d_cb, d_norm

    hn0 = _rms_fwd(x, attn_norm[0:1], "attn0_norm")
    z = _nn_parts("sc_in", hn0, w_in, 3, BF16)
    mix = _scmix_fwd(z, small["sc_conv_w"])
    h1, hf0 = _nn_add_norm("sc_out", mix, w_out, x, ffn_norm[0:1])
    h2, ffn0_saved = ffn_fwd(h1, hf0, 0, lambda a, w, h: _nn("ffn0_down", a, w, F32, add=h))

    hn1, hk, cq_pre, cq, q, kvpre, ckv, kr, knv = _attn_prep(
        h2, attn_norm[1:2], small["kv_in_norm"], w_dq, small["q_latent_norm"], w_uq, w_kv, small["kv_latent_norm"],
        w_ukv, cos, sin)
    o = _attn_fwd(q, knv, kr)
    h3, hf1 = _nn_add_norm("attn_out", o, w_o, h2, ffn_norm[1:2])
    (loss, dh4, dh4_b, d_final), ffn1_saved = ffn_fwd(
        h3, hf1, 1, lambda a, w, h: _nn_add_loss("ffn1_down_loss", a, w, h, small["final_norm"], tgt))

    rows = D // N_CHIPS
    dh3, dh3_b, d_cw1, d_cb1, d_fn1 = ffn_bwd(h3, dh4, dh4_b, 1, ffn1_saved, 0, {})

    do = _nt("attn_out_dx", dh3_b, w_o, BF16)
    d_wo = _tn("attn_out_dw", o, dh3_b, BF16)
    rs.pair_sums(0)
    dq, dknv, dkr = _attn_bwd(q, knv, kr, do, cos, sin)
    rs.chip_sums(0)
    dh2, dh2_b, d_wuq, d_wdq, d_wukv, d_wkv, d_an1, d_kvin, d_qln, d_kvln = _attn_prep_bwd(
        dq, dknv, dkr, dh3, h2, hn1, hk, cq_pre, cq, kvpre, ckv, attn_norm[1:2], small["kv_in_norm"], w_dq,
        small["q_latent_norm"], w_uq, w_kv, small["kv_latent_norm"], w_ukv, cos, sin)
    rs.finish(0)
    by_owner = lambda dw: dw.reshape(dw.shape[0], N_CHIPS, -1).transpose(1, 0, 2)
    rs.start(1, {
        "w_o": d_wo.reshape(N_CHIPS, rows, D), "w_uq": by_owner(d_wuq), "w_dq": d_wdq.reshape(N_CHIPS, rows, Q_LORA),
        "w_ukv": by_owner(d_wukv.reshape(2 * KV_LORA, -1)).reshape(N_CHIPS, 2 * KV_LORA, -1),
        "w_kv": d_wkv.reshape(N_CHIPS, rows, KVP),
    })

    dh1, dh1_b, d_cw0, d_cb0, d_fn0 = ffn_bwd(h1, dh2, dh2_b, 0, ffn0_saved, 2, {
        "down_dw": lambda: rs.pair_sums(1), "gate_bwd": lambda: rs.chip_sums(1),
        "up_dw": lambda: (rs.finish(1), rs.pair_sums(2))})

    d_wout = _tn("sc_out_dw", mix, dh1_b, BF16)
    dmix = _nt("sc_out_dx", dh1_b, w_out, BF16)
    dz, d_scw = _scmix_bwd(z, small["sc_conv_w"], dmix)
    d_win = _dw_sc_in(hn0, dz)
    rs.start(3, {"sc_w_out": d_wout.reshape(N_CHIPS, rows, D), "sc_w_in": d_win})
    dx, _, d_an0 = _dx_norm_bwd("sc_in_dx", dz, w_in, x, attn_norm[0:1], dh1)

    taps_by_owner = lambda per_layer: jnp.stack(per_layer, axis=1).reshape(3, len(per_layer), N_CHIPS, -1).transpose(2, 0, 1, 3)
    small_g = {
        "attn_norm": jnp.concatenate([d_an0, d_an1]), "ffn_norm": jnp.concatenate([d_fn0, d_fn1]),
        "final_norm": d_final, "kv_in_norm": d_kvin, "kv_latent_norm": d_kvln, "q_latent_norm": d_qln,
        "ffn_conv_b": jnp.concatenate([d_cb0, d_cb1]),
        "sc_conv_w": taps_by_owner([d_scw]), "ffn_conv_w": taps_by_owner([d_cw0, d_cw1]),
    }
    return loss, dx, small_g


RS_GROUPS = (("ffn_w_down1", "ffn_w_up1"), ("w_o", "w_uq", "w_dq", "w_ukv", "w_kv"),
             ("ffn_w_down0", "ffn_w_up0"), ("sc_w_out", "sc_w_in"))


class _ReduceScatter:
    def __init__(self, ids, finish):
        self.ids, self.grads, self.step, self.mine, self.sib, self.finish = ids, {}, {}, {}, {}, finish

    def _cid(self, gi):
        return len(AG_GROUPS) + 3 * gi

    def start(self, gi, grads):
        self.grads.update(grads)
        own = [grads[n] for n in RS_GROUPS[gi]]
        self.step[gi] = (own, _pair_exchange(own, gi, self._cid(gi)))

    def pair_sums(self, gi):
        own, ra = self.step[gi]
        sums = _pair_sums(self.ids, own, ra, f"rs_pair_sums{gi}")
        self.step[gi] = (own, ra, _chip_exchange(sums, gi, self._cid(gi) + 1))

    def chip_sums(self, gi):
        own, ra, rb = self.step[gi]
        mine = _chip_sums(self.ids, own, ra, rb, f"rs_chip_sums{gi}")
        self.mine.update(zip(RS_GROUPS[gi], mine))
        last = gi == len(RS_GROUPS) - 1
        swapped = _pair_swap_now(mine) if last else _pair_swap(mine, gi, self._cid(gi) + 2)
        self.sib.update(zip(RS_GROUPS[gi], swapped))


SMALL_REPL = ("attn_norm", "ffn_norm", "final_norm", "kv_in_norm", "kv_latent_norm", "q_latent_norm", "ffn_conv_b")


def _pad_heads(w_uq):
    per_head = w_uq.reshape(Q_LORA, -1, QK_NOPE + QK_ROPE)
    return jnp.pad(per_head, ((0, 0), (0, 0), (0, HEAD_PAD - QK_NOPE - QK_ROPE))).reshape(Q_LORA, -1)


def _pack_kv(w_dkv, w_kr):
    return jnp.concatenate([w_dkv, w_kr, jnp.zeros((w_kr.shape[0], LANES - QK_ROPE), w_kr.dtype)], axis=1)


def kernel(x, positions, attn_norm, ffn_norm, final_norm, sc_w_in, sc_conv_w, sc_w_out, kv_in_norm, w_dkv, kv_latent_norm, w_kr, w_uk, w_uv, w_dq, q_latent_norm, w_uq, w_o, ffn_w_up, ffn_conv_w, ffn_conv_b, ffn_w_down, loss_target, m_attn_norm, m_ffn_norm, m_final_norm, m_sc_w_in, m_sc_conv_w, m_sc_w_out, m_kv_in_norm, m_w_dkv, m_kv_latent_norm, m_w_kr, m_w_uk, m_w_uv, m_w_dq, m_q_latent_norm, m_w_uq, m_w_o, m_ffn_w_up, m_ffn_conv_w, m_ffn_conv_b, m_ffn_w_down, v_attn_norm, v_ffn_norm, v_final_norm, v_sc_w_in, v_sc_conv_w, v_sc_w_out, v_kv_in_norm, v_w_dkv, v_kv_latent_norm, v_w_kr, v_w_uk, v_w_uv, v_w_dq, v_q_latent_norm, v_w_uq, v_w_o, v_ffn_w_up, v_ffn_conv_w, v_ffn_conv_b, v_ffn_w_down):
    names = ("attn_norm", "ffn_norm", "final_norm", "sc_w_in", "sc_conv_w", "sc_w_out", "kv_in_norm", "w_dkv",
             "kv_latent_norm", "w_kr", "w_uk", "w_uv", "w_dq", "q_latent_norm", "w_uq", "w_o", "ffn_w_up",
             "ffn_conv_w", "ffn_conv_b", "ffn_w_down")
    w = dict(zip(names, (attn_norm, ffn_norm, final_norm, sc_w_in, sc_conv_w, sc_w_out, kv_in_norm, w_dkv,
                         kv_latent_norm, w_kr, w_uk, w_uv, w_dq, q_latent_norm, w_uq, w_o, ffn_w_up,
                         ffn_conv_w, ffn_conv_b, ffn_w_down)))
    m = dict(zip(names, (m_attn_norm, m_ffn_norm, m_final_norm, m_sc_w_in, m_sc_conv_w, m_sc_w_out, m_kv_in_norm,
                         m_w_dkv, m_kv_latent_norm, m_w_kr, m_w_uk, m_w_uv, m_w_dq, m_q_latent_norm, m_w_uq, m_w_o,
                         m_ffn_w_up, m_ffn_conv_w, m_ffn_conv_b, m_ffn_w_down)))
    v = dict(zip(names, (v_attn_norm, v_ffn_norm, v_final_norm, v_sc_w_in, v_sc_conv_w, v_sc_w_out, v_kv_in_norm,
                         v_w_dkv, v_kv_latent_norm, v_w_kr, v_w_uk, v_w_uv, v_w_dq, v_q_latent_norm, v_w_uq, v_w_o,
                         v_ffn_w_up, v_ffn_conv_w, v_ffn_conv_b, v_ffn_w_down)))

    _ORDER[0] = None
    ix, iy, ic = lax.axis_index("x"), lax.axis_index("y"), lax.axis_index("c")
    chip = 2 * ix + iy
    ids = jnp.stack([ic, chip]).astype(jnp.int32)

    ws = {
        "sc_w_in": sc_w_in[0], "sc_w_out": sc_w_out[0], "ffn_w_up": ffn_w_up, "ffn_w_down": ffn_w_down,
        "w_kv": _pack_kv(w_dkv, w_kr), "w_ukv": jnp.stack([w_uk, w_uv]), "w_dq": w_dq[0],
        "w_uq": _pad_heads(w_uq[0]), "w_o": w_o[0],
    }

    def ag_shard(name):
        if name == "sc_conv_w":
            return sc_conv_w[0]
        if name == "ffn_conv_w":
            return ffn_conv_w.reshape(6, -1)
        if name[:-1] in ("ffn_w_up", "ffn_w_down"):
            return ws[name[:-1]][int(name[-1])].astype(BF16)
        return ws[name].astype(BF16)

    wf = {}
    for gi, wms in enumerate(AG_GROUPS):
        fulls = _all_gather_group(gi, [ag_shard(wm.name) for wm in wms])
        wf.update({wm.name: f for wm, f in zip(wms, fulls)})
    small = {
        "attn_norm": attn_norm, "ffn_norm": ffn_norm, "final_norm": final_norm[None], "kv_in_norm": kv_in_norm[None],
        "kv_latent_norm": kv_latent_norm[None], "q_latent_norm": q_latent_norm, "ffn_conv_b": ffn_conv_b,
        "sc_conv_w": wf["sc_conv_w"].transpose(1, 0, 2).reshape(3, D),
        "ffn_conv_w": wf["ffn_conv_w"].reshape(N_CHIPS, 2, 3, -1).transpose(1, 2, 0, 3).reshape(2, 3, F_FF),
    }

    res = {}

    held = {
        "ffn_w_up0": [("ffn_w_up", dict(layer=0))], "ffn_w_up1": [("ffn_w_up", dict(layer=1))],
        "ffn_w_down0": [("ffn_w_down", dict(layer=0))], "ffn_w_down1": [("ffn_w_down", dict(layer=1))],
        "sc_w_in": [("sc_w_in", dict(layer=0))], "sc_w_out": [("sc_w_out", dict(layer=0))],
        "w_dq": [("w_dq", dict(layer=0))], "w_o": [("w_o", dict(layer=0))], "w_uq": [("w_uq", dict(layer=0, head_padded=True))],
        "w_kv": [("w_dkv", dict(gcols=(0, KV_LORA))), ("w_kr", dict(gcols=(KV_LORA, KV_LORA + QK_ROPE)))],
        "w_ukv": [("w_uk", dict(owner=0)), ("w_uv", dict(owner=1))],
    }

    def adamw_group(gi):
        items = []
        for key in RS_GROUPS[gi]:
            for n, opts in held[key]:
                items.append(dict(name=n, w=w[n], m=m[n], v=v[n], g_mine=rs.mine[key], g_sib=rs.sib[key],
                                  prev=res.get(n) if "layer" in opts and w[n].shape[0] > 1 else None, **opts))
        for it, out in zip(items, _adamw_shards(ids, items, f"adamw_group{gi}")):
            res[it["name"]] = out

    rs = _ReduceScatter(ids, adamw_group)
    loss, dx, small_g = _local_step(x[0], positions[0], loss_target[0], wf, small, rs)

    rs.chip_sums(2)
    rs.pair_sums(3)

    s_names = list(small_g)
    reduced = _all_reduce_small([small_g[n] for n in s_names] + [loss], [small_g[n].ndim == 4 for n in s_names] + [False],
                                "ar_small")
    sg, loss_out = dict(zip(s_names, reduced[:-1])), reduced[-1][0, 0]

    row = lambda n: (lambda t: t[n][None])
    taps = lambda n: (lambda t: t[n].transpose(1, 0, 2))
    small_2d = {
        "attn_norm": (sg["attn_norm"], lambda t: t["attn_norm"]), "ffn_norm": (sg["ffn_norm"], lambda t: t["ffn_norm"]),
        "final_norm": (sg["final_norm"], row("final_norm")), "kv_in_norm": (sg["kv_in_norm"], row("kv_in_norm")),
        "kv_latent_norm": (sg["kv_latent_norm"], row("kv_latent_norm")),
        "q_latent_norm": (sg["q_latent_norm"], lambda t: t["q_latent_norm"]),
        "ffn_conv_b": (sg["ffn_conv_b"], lambda t: t["ffn_conv_b"]),
        "sc_conv_w": (sg["sc_conv_w"], taps("sc_conv_w")), "ffn_conv_w": (sg["ffn_conv_w"], taps("ffn_conv_w")),
    }
    s_keys = list(small_2d)
    small_grads = [small_2d[k][0] for k in s_keys]
    views = lambda tree: [small_2d[k][1](tree) for k in s_keys]
    small_res = _adamw_small(views(w), small_grads, views(m), views(v))

    def restore(vals):
        by = dict(zip(s_keys, vals))
        out = {n: by[n].reshape(w[n].shape) for n in SMALL_REPL}
        out.update({n: by[n].transpose(1, 0, 2) for n in ("sc_conv_w", "ffn_conv_w")})
        return out

    rs.finish(2)
    rs.chip_sums(3)
    rs.finish(3)
    outs = [restore(vals) for vals in small_res]
    for k, dst in enumerate(outs):
        for n in res:
            dst[n] = res[n][k]
    grads, delta, new_m, new_v = outs

    _ORDER[0] = None
    return (loss_out, dx[None], *[grads[n] for n in names], *[delta[n] for n in names],
            *[new_m[n] for n in names], *[new_v[n] for n in names])
```
